```python
import math
import jax, jax.numpy as jnp
from jax import lax
import numpy as np

D_MODEL = 2048
BATCH = 8
SEQ = 4096
DEPTH = 1

HEAD_DIM = 128
D_MIX = D_MODEL
N_HEADS_A = 8
N_KV_A = 2
N_HEADS_B = 8
N_KV_B = 2
D_FF = 4 * D_MODEL
D_PLE = 256
GRID_W = 64
BLOCK_Q = 128
WINDOW = 128
N_BUCKETS = 32
MAX_DISTANCE = 128
ROPE_THETA = 10000.0
EPS = 1e-6
NEG_INF = -1e30

Q_A = N_HEADS_A * HEAD_DIM
KV_A = N_KV_A * HEAD_DIM
Q_B = N_HEADS_B * HEAD_DIM
KV_B = N_KV_B * HEAD_DIM
D_IN_PROJ = Q_A + 2 * KV_A + Q_B + 2 * KV_B

kernel_name = "hybrid_axial_window_sink_encoder_layer"


def rmsnorm(x, g):
    xf = x.astype(jnp.float32)
    y = xf * lax.rsqrt(jnp.mean(xf * xf, axis=-1, keepdims=True) + EPS)
    return (y * g.astype(jnp.float32)).astype(x.dtype)


def axial_rope_tables(seq):
    rows = seq // GRID_W
    row = jnp.repeat(jnp.arange(rows, dtype=jnp.int32), GRID_W)
    col = jnp.tile(jnp.arange(GRID_W, dtype=jnp.int32), rows)
    half = HEAD_DIM // 2
    inv_freq = ROPE_THETA ** (-jnp.arange(0, half, 2, dtype=jnp.float32) / half)
    ang_r = row.astype(jnp.float32)[:, None] * inv_freq
    ang_c = col.astype(jnp.float32)[:, None] * inv_freq
    return jnp.cos(ang_r), jnp.sin(ang_r), jnp.cos(ang_c), jnp.sin(ang_c)


def _rotate(x, cos, sin):
    x1, x2 = x[..., : x.shape[-1] // 2], x[..., x.shape[-1] // 2:]
    return jnp.concatenate([x1 * cos - x2 * sin, x2 * cos + x1 * sin], axis=-1)


def apply_axial_rope(x, tabs):
    cr, sr, cc, sc = tabs
    xf = x.astype(jnp.float32)
    half = HEAD_DIM // 2
    out = jnp.concatenate([_rotate(xf[..., :half], cr, sr), _rotate(xf[..., half:], cc, sc)], axis=-1)
    return out.astype(x.dtype)


def global_axial_attention(q, k, v, g_q, g_k, tabs):
    B, S, H, D = q.shape
    KV = k.shape[2]
    G = H // KV
    nb = S // BLOCK_Q
    q = apply_axial_rope(rmsnorm(q, g_q).transpose(0, 2, 1, 3), tabs)
    k = apply_axial_rope(rmsnorm(k, g_k).transpose(0, 2, 1, 3), tabs)
    v = v.transpose(0, 2, 1, 3)
    qb = q.reshape(B, KV, G, nb, BLOCK_Q, D).transpose(3, 0, 1, 2, 4, 5)
    scale = D ** -0.5

    def attend(q_blk):
        s = jnp.einsum('bkgqd,bksd->bkgqs', q_blk, k, preferred_element_type=jnp.float32) * scale
        pr = jax.nn.softmax(s, axis=-1)
        return jnp.einsum('bkgqs,bksd->bkgqd', pr.astype(v.dtype), v)

    o = lax.map(attend, qb)
    return o.transpose(1, 0, 4, 2, 3, 5).reshape(B, S, H * D)


def t5_bucket(rel):
    nb = N_BUCKETS // 2
    ret = jnp.where(rel > 0, nb, 0)
    n = jnp.abs(rel)
    max_exact = nb // 2
    nf = jnp.maximum(n, 1).astype(jnp.float32)
    large = max_exact + (jnp.log(nf / max_exact) / math.log(MAX_DISTANCE / max_exact)
                         * (nb - max_exact)).astype(jnp.int32)
    large = jnp.minimum(large, nb - 1)
    return ret + jnp.where(n < max_exact, n, large)


def window_sink_attention(q, k, v, rel_bias_table, sink):
    B, S, H, D = q.shape
    KV = k.shape[2]
    G = H // KV
    nb = S // BLOCK_Q
    Q = BLOCK_Q
    q = q.transpose(0, 2, 1, 3).reshape(B, KV, G, nb, Q, D)
    pad = ((0, 0), (0, 0), (Q, Q), (0, 0))
    kp = jnp.pad(k.transpose(0, 2, 1, 3), pad).reshape(B, KV, nb + 2, Q, D)
    vp = jnp.pad(v.transpose(0, 2, 1, 3), pad).reshape(B, KV, nb + 2, Q, D)
    kband = jnp.concatenate([kp[:, :, :-2], kp[:, :, 1:-1], kp[:, :, 2:]], axis=3)
    vband = jnp.concatenate([vp[:, :, :-2], vp[:, :, 1:-1], vp[:, :, 2:]], axis=3)
    s = jnp.einsum('bkgnqd,bknjd->bkgnqj', q, kband,
                   preferred_element_type=jnp.float32) * (D ** -0.5)
    r = jnp.arange(Q, dtype=jnp.int32)
    j = jnp.arange(3 * Q, dtype=jnp.int32)
    rel = (j[None, :] - Q) - r[:, None]
    bias = rel_bias_table[t5_bucket(rel)].astype(jnp.float32)
    bias = bias.transpose(2, 0, 1).reshape(KV, G, 1, Q, 3 * Q)
    kabs = jnp.arange(nb, dtype=jnp.int32)[:, None] * Q + j[None, :] - Q
    in_range = (kabs >= 0) & (kabs < S)
    mask = (jnp.abs(rel) <= WINDOW)[None, :, :] & in_range[:, None, :]
    s = jnp.where(mask, s + bias, NEG_INF)
    sink_col = jnp.broadcast_to(sink.astype(jnp.float32).reshape(1, KV, G, 1, 1, 1),
                                s.shape[:-1] + (1,))
    pr = jax.nn.softmax(jnp.concatenate([s, sink_col], axis=-1), axis=-1)[..., :-1]
    o = jnp.einsum('bkgnqj,bknjd->bkgnqd', pr.astype(vband.dtype), vband)
    return o.transpose(0, 3, 4, 1, 2, 5).reshape(B, S, H * D)


def _fwd_setup_inputs(seed: int = 0) -> dict:
    key = jax.random.key(seed)
    ks = jax.random.split(key, 20)
    f32 = jnp.float32

    def nrm(k, shape, scale):
        return jax.random.normal(k, shape, f32) * scale

    def gain(k, shape):
        return 1.0 + 0.02 * jax.random.normal(k, shape, f32)

    return {
        "x": nrm(ks[0], (BATCH, SEQ, D_MODEL), 1.0),
        "p": nrm(ks[1], (DEPTH, BATCH, SEQ, D_PLE), 1.0),
        "attn_norm_g": gain(ks[2], (DEPTH, D_MODEL)),
        "w_in": nrm(ks[3], (DEPTH, D_MODEL, D_IN_PROJ), D_MODEL ** -0.5),
        "q_norm_g": gain(ks[4], (DEPTH, HEAD_DIM)),
        "k_norm_g": gain(ks[5], (DEPTH, HEAD_DIM)),
        "sink_logits": nrm(ks[6], (DEPTH, N_HEADS_B), 1.0),
        "w_out": nrm(ks[7], (DEPTH, D_MIX, D_MODEL), D_MIX ** -0.5),
        "mlp_norm_g": gain(ks[8], (DEPTH, D_MODEL)),
        "w_up": nrm(ks[9], (DEPTH, D_MODEL, D_FF), D_MODEL ** -0.5),
        "w_down": nrm(ks[10], (DEPTH, D_FF, D_MODEL), D_FF ** -0.5),
        "ple_w": nrm(ks[11], (DEPTH, D_PLE, D_MODEL), D_PLE ** -0.5),
        "ple_norm_g": gain(ks[12], (DEPTH, D_MODEL)),
        "gate_norm_g": gain(ks[13], (DEPTH, D_MODEL)),
        "w_gate": nrm(ks[14], (DEPTH, D_MODEL, D_MODEL), D_MODEL ** -0.5),
        "rel_bias_table": nrm(ks[15], (N_BUCKETS, N_HEADS_B), 0.5),
        "final_norm_g": gain(ks[16], (D_MODEL,)),
    }


def _fwd_reference(x, p, attn_norm_g, w_in, q_norm_g, k_norm_g, sink_logits, w_out,
              mlp_norm_g, w_up, w_down, ple_w, ple_norm_g, gate_norm_g, w_gate,
              rel_bias_table, final_norm_g):
    B, S, _ = x.shape
    tabs = axial_rope_tables(S)
    splits = [Q_A, Q_A + KV_A, Q_A + 2 * KV_A, Q_A + 2 * KV_A + Q_B, Q_A + 2 * KV_A + Q_B + KV_B]
    h = x
    for i in range(DEPTH):
        u = rmsnorm(h, attn_norm_g[i])
        proj = u @ w_in[i]
        qa, ka, va, qb, kb, vb = jnp.split(proj, splits, axis=-1)
        oa = global_axial_attention(
            qa.reshape(B, S, N_HEADS_A, HEAD_DIM), ka.reshape(B, S, N_KV_A, HEAD_DIM),
            va.reshape(B, S, N_KV_A, HEAD_DIM), q_norm_g[i], k_norm_g[i], tabs)
        ob = window_sink_attention(
            qb.reshape(B, S, N_HEADS_B, HEAD_DIM), kb.reshape(B, S, N_KV_B, HEAD_DIM),
            vb.reshape(B, S, N_KV_B, HEAD_DIM), rel_bias_table, sink_logits[i])
        h = h + jnp.concatenate([oa, ob], axis=-1) @ w_out[i]
        m = rmsnorm(h, mlp_norm_g[i])
        h = h + jnp.square(jax.nn.relu(m @ w_up[i])) @ w_down[i]
        e = rmsnorm(p[i] @ ple_w[i], ple_norm_g[i])
        gate = jax.nn.sigmoid(rmsnorm(h, gate_norm_g[i]) @ w_gate[i])
        h = h + gate * e
    return rmsnorm(h, final_norm_g)


import jax as _jax
import jax.numpy as _jnp

TWIN_FORMAT = 'train_step'
FWD_PARAMS = ['x', 'p', 'attn_norm_g', 'w_in', 'q_norm_g', 'k_norm_g', 'sink_logits', 'w_out', 'mlp_norm_g', 'w_up', 'w_down', 'ple_w', 'ple_norm_g', 'gate_norm_g', 'w_gate', 'rel_bias_table', 'final_norm_g']
TWIN_WEIGHTS = ['attn_norm_g', 'w_in', 'q_norm_g', 'k_norm_g', 'sink_logits', 'w_out', 'mlp_norm_g', 'w_up', 'w_down', 'ple_w', 'ple_norm_g', 'gate_norm_g', 'w_gate', 'rel_bias_table', 'final_norm_g']
TWIN_DIFF_INPUT = 'x'
TWIN_INPUTS = ['x', 'p', 'attn_norm_g', 'w_in', 'q_norm_g', 'k_norm_g', 'sink_logits', 'w_out', 'mlp_norm_g', 'w_up', 'w_down', 'ple_w', 'ple_norm_g', 'gate_norm_g', 'w_gate', 'rel_bias_table', 'final_norm_g', 'loss_target', 'm_attn_norm_g', 'm_w_in', 'm_q_norm_g', 'm_k_norm_g', 'm_sink_logits', 'm_w_out', 'm_mlp_norm_g', 'm_w_up', 'm_w_down', 'm_ple_w', 'm_ple_norm_g', 'm_gate_norm_g', 'm_w_gate', 'm_rel_bias_table', 'm_final_norm_g', 'v_attn_norm_g', 'v_w_in', 'v_q_norm_g', 'v_k_norm_g', 'v_sink_logits', 'v_w_out', 'v_mlp_norm_g', 'v_w_up', 'v_w_down', 'v_ple_w', 'v_ple_norm_g', 'v_gate_norm_g', 'v_w_gate', 'v_rel_bias_table', 'v_final_norm_g']
TWIN_OUTPUTS = ['loss', 'grad_x', 'grad_attn_norm_g', 'grad_w_in', 'grad_q_norm_g', 'grad_k_norm_g', 'grad_sink_logits', 'grad_w_out', 'grad_mlp_norm_g', 'grad_w_up', 'grad_w_down', 'grad_ple_w', 'grad_ple_norm_g', 'grad_gate_norm_g', 'grad_w_gate', 'grad_rel_bias_table', 'grad_final_norm_g', 'delta_attn_norm_g', 'delta_w_in', 'delta_q_norm_g', 'delta_k_norm_g', 'delta_sink_logits', 'delta_w_out', 'delta_mlp_norm_g', 'delta_w_up', 'delta_w_down', 'delta_ple_w', 'delta_ple_norm_g', 'delta_gate_norm_g', 'delta_w_gate', 'delta_rel_bias_table', 'delta_final_norm_g', 'new_m_attn_norm_g', 'new_m_w_in', 'new_m_q_norm_g', 'new_m_k_norm_g', 'new_m_sink_logits', 'new_m_w_out', 'new_m_mlp_norm_g', 'new_m_w_up', 'new_m_w_down', 'new_m_ple_w', 'new_m_ple_norm_g', 'new_m_gate_norm_g', 'new_m_w_gate', 'new_m_rel_bias_table', 'new_m_final_norm_g', 'new_v_attn_norm_g', 'new_v_w_in', 'new_v_q_norm_g', 'new_v_k_norm_g', 'new_v_sink_logits', 'new_v_w_out', 'new_v_mlp_norm_g', 'new_v_w_up', 'new_v_w_down', 'new_v_ple_w', 'new_v_ple_norm_g', 'new_v_gate_norm_g', 'new_v_w_gate', 'new_v_rel_bias_table', 'new_v_final_norm_g']
TWIN_LEAF_KINDS = {'loss': 'loss', 'grad_x': 'grad_x', 'grad_attn_norm_g': 'grad_w', 'grad_w_in': 'grad_w', 'grad_q_norm_g': 'grad_w', 'grad_k_norm_g': 'grad_w', 'grad_sink_logits': 'grad_w', 'grad_w_out': 'grad_w', 'grad_mlp_norm_g': 'grad_w', 'grad_w_up': 'grad_w', 'grad_w_down': 'grad_w', 'grad_ple_w': 'grad_w', 'grad_ple_norm_g': 'grad_w', 'grad_gate_norm_g': 'grad_w', 'grad_w_gate': 'grad_w', 'grad_rel_bias_table': 'grad_w', 'grad_final_norm_g': 'grad_w', 'delta_attn_norm_g': 'delta_w', 'delta_w_in': 'delta_w', 'delta_q_norm_g': 'delta_w', 'delta_k_norm_g': 'delta_w', 'delta_sink_logits': 'delta_w', 'delta_w_out': 'delta_w', 'delta_mlp_norm_g': 'delta_w', 'delta_w_up': 'delta_w', 'delta_w_down': 'delta_w', 'delta_ple_w': 'delta_w', 'delta_ple_norm_g': 'delta_w', 'delta_gate_norm_g': 'delta_w', 'delta_w_gate': 'delta_w', 'delta_rel_bias_table': 'delta_w', 'delta_final_norm_g': 'delta_w', 'new_m_attn_norm_g': 'new_m', 'new_m_w_in': 'new_m', 'new_m_q_norm_g': 'new_m', 'new_m_k_norm_g': 'new_m', 'new_m_sink_logits': 'new_m', 'new_m_w_out': 'new_m', 'new_m_mlp_norm_g': 'new_m', 'new_m_w_up': 'new_m', 'new_m_w_down': 'new_m', 'new_m_ple_w': 'new_m', 'new_m_ple_norm_g': 'new_m', 'new_m_gate_norm_g': 'new_m', 'new_m_w_gate': 'new_m', 'new_m_rel_bias_table': 'new_m', 'new_m_final_norm_g': 'new_m', 'new_v_attn_norm_g': 'new_v', 'new_v_w_in': 'new_v', 'new_v_q_norm_g': 'new_v', 'new_v_k_norm_g': 'new_v', 'new_v_sink_logits': 'new_v', 'new_v_w_out': 'new_v', 'new_v_mlp_norm_g': 'new_v', 'new_v_w_up': 'new_v', 'new_v_w_down': 'new_v', 'new_v_ple_w': 'new_v', 'new_v_ple_norm_g': 'new_v', 'new_v_gate_norm_g': 'new_v', 'new_v_w_gate': 'new_v', 'new_v_rel_bias_table': 'new_v', 'new_v_final_norm_g': 'new_v'}


def _forward(args):
    return _fwd_reference(*[args[k] for k in FWD_PARAMS])


def _output_shape():
    def fwd():
        inp = _fwd_setup_inputs(0)
        return _fwd_reference(*[inp[k] for k in FWD_PARAMS])
    out = _jax.eval_shape(fwd)
    return out.shape, out.dtype

N_MICROBATCH = 1
ADAM_LR = 0.001
ADAM_B1 = 0.9
ADAM_B2 = 0.999
ADAM_EPS = 1e-08
ADAM_WD = 0.01
ADAM_STEP = 10
PER_EXAMPLE_BATCH_AXIS = {'x': 0, 'p': 1, 'loss_target': 0}
SHARED_INPUTS = []
_WEIGHT_DTYPES = {'attn_norm_g': _jnp.float32, 'w_in': _jnp.float32, 'q_norm_g': _jnp.float32, 'k_norm_g': _jnp.float32, 'sink_logits': _jnp.float32, 'w_out': _jnp.float32, 'mlp_norm_g': _jnp.float32, 'w_up': _jnp.float32, 'w_down': _jnp.float32, 'ple_w': _jnp.float32, 'ple_norm_g': _jnp.float32, 'gate_norm_g': _jnp.float32, 'w_gate': _jnp.float32, 'rel_bias_table': _jnp.float32, 'final_norm_g': _jnp.float32}
MOMENT_SCALE = {'attn_norm_g': 1.851460e-02, 'w_in': 1.508342e-02, 'q_norm_g': 2.573595e-02, 'k_norm_g': 2.736006e-02, 'sink_logits': 1.191725e-03, 'w_out': 1.084983e-02, 'mlp_norm_g': 7.797260e-02, 'w_up': 3.783665e-02, 'w_down': 7.647717e-02, 'ple_w': 2.899881e-02, 'ple_norm_g': 5.933407e-02, 'gate_norm_g': 1.237344e-02, 'w_gate': 1.227762e-02, 'rel_bias_table': 2.034996e-02, 'final_norm_g': 1.613345e+01}


def _to_microbatches(a, axis):
    t = _jnp.moveaxis(a, axis, 0)
    t = t.reshape((N_MICROBATCH, t.shape[0] // N_MICROBATCH) + t.shape[1:])
    return _jnp.moveaxis(t, 1, axis + 1)


def setup_inputs(seed: int = 0) -> dict:
    inp = _fwd_setup_inputs(seed)
    key = _jax.random.fold_in(_jax.random.key(seed), 7919)
    shape, _ = _output_shape()
    out = dict(inp)
    out["loss_target"] = _jax.random.normal(_jax.random.fold_in(key, 0), shape, _jnp.float32)
    for i, name in enumerate(TWIN_WEIGHTS):
        w = inp[name].astype(_jnp.float32)
        if MOMENT_SCALE is None:
            s = _jnp.sqrt(_jnp.mean(_jnp.square(w)) + 1e-30)
        else:
            s = MOMENT_SCALE[name]
        km, kv = _jax.random.split(_jax.random.fold_in(key, i + 1))
        out[name] = w
        out["m_" + name] = s * _jax.random.normal(km, w.shape, _jnp.float32)
        out["v_" + name] = (s * s) * _jax.random.uniform(kv, w.shape, _jnp.float32, 0.5, 1.5)
    if N_MICROBATCH > 1:
        for name, axis in PER_EXAMPLE_BATCH_AXIS.items():
            out[name] = _to_microbatches(out[name], axis)
    return {'x': out['x'], 'p': out['p'], 'attn_norm_g': out['attn_norm_g'], 'w_in': out['w_in'], 'q_norm_g': out['q_norm_g'], 'k_norm_g': out['k_norm_g'], 'sink_logits': out['sink_logits'], 'w_out': out['w_out'], 'mlp_norm_g': out['mlp_norm_g'], 'w_up': out['w_up'], 'w_down': out['w_down'], 'ple_w': out['ple_w'], 'ple_norm_g': out['ple_norm_g'], 'gate_norm_g': out['gate_norm_g'], 'w_gate': out['w_gate'], 'rel_bias_table': out['rel_bias_table'], 'final_norm_g': out['final_norm_g'], 'loss_target': out['loss_target'], 'm_attn_norm_g': out['m_attn_norm_g'], 'm_w_in': out['m_w_in'], 'm_q_norm_g': out['m_q_norm_g'], 'm_k_norm_g': out['m_k_norm_g'], 'm_sink_logits': out['m_sink_logits'], 'm_w_out': out['m_w_out'], 'm_mlp_norm_g': out['m_mlp_norm_g'], 'm_w_up': out['m_w_up'], 'm_w_down': out['m_w_down'], 'm_ple_w': out['m_ple_w'], 'm_ple_norm_g': out['m_ple_norm_g'], 'm_gate_norm_g': out['m_gate_norm_g'], 'm_w_gate': out['m_w_gate'], 'm_rel_bias_table': out['m_rel_bias_table'], 'm_final_norm_g': out['m_final_norm_g'], 'v_attn_norm_g': out['v_attn_norm_g'], 'v_w_in': out['v_w_in'], 'v_q_norm_g': out['v_q_norm_g'], 'v_k_norm_g': out['v_k_norm_g'], 'v_sink_logits': out['v_sink_logits'], 'v_w_out': out['v_w_out'], 'v_mlp_norm_g': out['v_mlp_norm_g'], 'v_w_up': out['v_w_up'], 'v_w_down': out['v_w_down'], 'v_ple_w': out['v_ple_w'], 'v_ple_norm_g': out['v_ple_norm_g'], 'v_gate_norm_g': out['v_gate_norm_g'], 'v_w_gate': out['v_w_gate'], 'v_rel_bias_table': out['v_rel_bias_table'], 'v_final_norm_g': out['v_final_norm_g']}


def _loss(weights, diff, rest, loss_target):
    with _jax.named_scope("forward"):
        args = {**rest, TWIN_DIFF_INPUT: diff, **{k: w.astype(_WEIGHT_DTYPES[k]) for k, w in weights.items()}}
        y = _forward(args)
    with _jax.named_scope("loss_head"):
        err = _jnp.square(y.astype(_jnp.float32) - loss_target)
        return 0.5 * _jnp.sum(_jnp.mean(err, axis=-1)) if err.ndim else 0.5 * err


def _adamw(w, g, m, v):
    m = ADAM_B1 * m + (1.0 - ADAM_B1) * g
    v = ADAM_B2 * v + (1.0 - ADAM_B2) * _jnp.square(g)
    m_hat = m / (1.0 - ADAM_B1 ** ADAM_STEP)
    v_hat = v / (1.0 - ADAM_B2 ** ADAM_STEP)
    delta = -ADAM_LR * (m_hat / (_jnp.sqrt(v_hat) + ADAM_EPS) + ADAM_WD * w)
    return delta, m, v


def reference(x, p, attn_norm_g, w_in, q_norm_g, k_norm_g, sink_logits, w_out, mlp_norm_g, w_up, w_down, ple_w, ple_norm_g, gate_norm_g, w_gate, rel_bias_table, final_norm_g, loss_target, m_attn_norm_g, m_w_in, m_q_norm_g, m_k_norm_g, m_sink_logits, m_w_out, m_mlp_norm_g, m_w_up, m_w_down, m_ple_w, m_ple_norm_g, m_gate_norm_g, m_w_gate, m_rel_bias_table, m_final_norm_g, v_attn_norm_g, v_w_in, v_q_norm_g, v_k_norm_g, v_sink_logits, v_w_out, v_mlp_norm_g, v_w_up, v_w_down, v_ple_w, v_ple_norm_g, v_gate_norm_g, v_w_gate, v_rel_bias_table, v_final_norm_g):
    given = dict(x=x, p=p, attn_norm_g=attn_norm_g, w_in=w_in, q_norm_g=q_norm_g, k_norm_g=k_norm_g, sink_logits=sink_logits, w_out=w_out, mlp_norm_g=mlp_norm_g, w_up=w_up, w_down=w_down, ple_w=ple_w, ple_norm_g=ple_norm_g, gate_norm_g=gate_norm_g, w_gate=w_gate, rel_bias_table=rel_bias_table, final_norm_g=final_norm_g, loss_target=loss_target, m_attn_norm_g=m_attn_norm_g, m_w_in=m_w_in, m_q_norm_g=m_q_norm_g, m_k_norm_g=m_k_norm_g, m_sink_logits=m_sink_logits, m_w_out=m_w_out, m_mlp_norm_g=m_mlp_norm_g, m_w_up=m_w_up, m_w_down=m_w_down, m_ple_w=m_ple_w, m_ple_norm_g=m_ple_norm_g, m_gate_norm_g=m_gate_norm_g, m_w_gate=m_w_gate, m_rel_bias_table=m_rel_bias_table, m_final_norm_g=m_final_norm_g, v_attn_norm_g=v_attn_norm_g, v_w_in=v_w_in, v_q_norm_g=v_q_norm_g, v_k_norm_g=v_k_norm_g, v_sink_logits=v_sink_logits, v_w_out=v_w_out, v_mlp_norm_g=v_mlp_norm_g, v_w_up=v_w_up, v_w_down=v_w_down, v_ple_w=v_ple_w, v_ple_norm_g=v_ple_norm_g, v_gate_norm_g=v_gate_norm_g, v_w_gate=v_w_gate, v_rel_bias_table=v_rel_bias_table, v_final_norm_g=v_final_norm_g)
    weights = {n: given[n] for n in TWIN_WEIGHTS}
    shared = {n: given[n] for n in SHARED_INPUTS}
    per_example = {n: given[n] for n in ['x', 'p']}
    grad_fn = _jax.value_and_grad(_loss, argnums=(0, 1))

    def one_microbatch(ex, loss_target):
        ex = dict(ex)
        diff = ex.pop(TWIN_DIFF_INPUT)
        return grad_fn(weights, diff, {**shared, **ex}, loss_target)

    if N_MICROBATCH == 1:
        loss, (grad_w, grad_x) = one_microbatch(per_example, given["loss_target"])
    else:
        def body(carry, xs):
            loss_sum, grad_sum = carry
            l_k, (gw_k, gx_k) = one_microbatch(xs[0], xs[1])
            with _jax.named_scope("update"):
                return (loss_sum + l_k, _jax.tree.map(_jnp.add, grad_sum, gw_k)), gx_k

        init = (_jnp.zeros((), _jnp.float32), _jax.tree.map(_jnp.zeros_like, weights))
        (loss, grad_w), grad_x = _jax.lax.scan(body, init, (per_example, given["loss_target"]))
    with _jax.named_scope("update"):
        delta_w, new_m, new_v = {}, {}, {}
        for n in TWIN_WEIGHTS:
            delta_w[n], new_m[n], new_v[n] = _adamw(weights[n], grad_w[n], given["m_" + n], given["v_" + n])
    return (loss, grad_x, *[grad_w[n] for n in TWIN_WEIGHTS], *[delta_w[n] for n in TWIN_WEIGHTS],
            *[new_m[n] for n in TWIN_WEIGHTS], *[new_v[n] for n in TWIN_WEIGHTS])
```

```python
import functools
import math

import numpy as np
import jax
import jax.numpy as jnp
from jax import lax
from jax.experimental import pallas as pl
from jax.experimental.pallas import tpu as pltpu

F32 = jnp.float32
BF16 = jnp.bfloat16

N_DEV = 8
N_CHIP = 4
HEAD_DIM = 128
GROUP = 4
GRID_W = 64
WINDOW = 128
BLOCK_Q = 128
N_BUCKETS = 32
MAX_DISTANCE = 128
ROPE_THETA = 10000.0
EPS = 1e-6
NEG_INF = -1e30
ADAM_LR = 0.001
ADAM_B1 = 0.9
ADAM_B2 = 0.999
ADAM_EPS = 1e-08
ADAM_WD = 0.01
ADAM_STEP = 10
LANES = 128
SUBLANES = 8
MESH = pl.DeviceIdType.MESH

_NT = (((1,), (1,)), ((), ()))
_NN = (((1,), (0,)), ((), ()))
_TN = (((0,), (0,)), ((), ()))


def _tile(dim, pref):
    return pref if dim % pref == 0 else dim


def _params(sem):
    return pltpu.CompilerParams(dimension_semantics=sem, vmem_limit_bytes=56 * 1024 * 1024)


def _mm(name, a, b, dims, grid, a_spec, b_spec, out_shape, out_specs, acc_shape, epilogue,
        extras=(), extra_specs=()):
    nk = grid[2]
    n_extra = len(extras)

    def body(*refs):
        a_ref, b_ref = refs[0], refs[1]
        extra = refs[2:2 + n_extra]
        outs = refs[2 + n_extra:-1]
        acc = refs[-1]
        part = lax.dot_general(a_ref[...], b_ref[...], dims, preferred_element_type=F32)
        if nk == 1:
            epilogue(part, extra, outs)
        else:
            k = pl.program_id(2)

            @pl.when(k == 0)
            def _():
                acc[...] = part

            @pl.when(k > 0)
            def _():
                acc[...] += part

            @pl.when(k == nk - 1)
            def _():
                epilogue(acc[...], extra, outs)

    return pl.pallas_call(
        body, name=name, grid=grid,
        in_specs=[a_spec, b_spec, *extra_specs],
        out_specs=out_specs, out_shape=out_shape,
        scratch_shapes=[pltpu.VMEM(acc_shape if nk > 1 else (SUBLANES, LANES), F32)],
        compiler_params=_params(("parallel", "parallel", "arbitrary")),
    )(a, b, *extras)


def _store(dtype):
    def ep(acc, extra, outs):
        outs[0][...] = acc.astype(dtype)
    return ep


def _store_add(acc, extra, outs):
    outs[0][...] = acc + extra[0][...]


def _mm_nn(name, a, b, out_dtype=F32, epilogue=None, extras=(), n_out=1, out_dtypes=None, tm=1024, tn=512, tk=2048):
    M, K = a.shape
    tm, tk = _tile(M, tm), _tile(K, tk)
    if b.ndim == 3:
        nb, _, n = b.shape
        N, tn = nb * n, n
        b_spec = pl.BlockSpec((None, tk, n), lambda i, j, k: (j, k, 0))
    else:
        N = b.shape[1]
        tn = _tile(N, tn)
        b_spec = pl.BlockSpec((tk, tn), lambda i, j, k: (k, j))
    grid = (M // tm, N // tn, K // tk)
    o_spec = pl.BlockSpec((tm, tn), lambda i, j, k: (i, j))
    out_dtypes = out_dtypes or [out_dtype] * n_out
    out_shape = [jax.ShapeDtypeStruct((M, N), d) for d in out_dtypes]
    res = _mm(name, a, b, _NN, grid, pl.BlockSpec((tm, tk), lambda i, j, k: (i, k)), b_spec,
              out_shape, [o_spec] * len(out_dtypes), (tm, tn), epilogue or _store(out_dtype),
              extras, [o_spec] * len(extras))
    return res if len(out_dtypes) > 1 else res[0]


def _mm_nt(name, a, b, out_dtype=F32, epilogue=None, extras=(), tm=1024, tn=512, tk=2048):
    M, C = a.shape
    tm = _tile(M, tm)
    if b.ndim == 3:
        nb, N, n = b.shape
        tk = n
        tn = _tile(N, tn)
        b_spec = pl.BlockSpec((None, tn, n), lambda i, j, k: (k, j, 0))
    else:
        N = b.shape[0]
        tn, tk = _tile(N, tn), _tile(C, tk)
        b_spec = pl.BlockSpec((tn, tk), lambda i, j, k: (j, k))
    grid = (M // tm, N // tn, C // tk)
    o_spec = pl.BlockSpec((tm, tn), lambda i, j, k: (i, j))
    return _mm(name, a, b, _NT, grid, pl.BlockSpec((tm, tk), lambda i, j, k: (i, k)), b_spec,
               [jax.ShapeDtypeStruct((M, N), out_dtype)], [o_spec], (tm, tn), epilogue or _store(out_dtype),
               extras, [o_spec] * len(extras))[0]


def _mm_tn(name, a, b, blocked_n=None, tm=1024, tn=1024, tk=1024):
    T, M = a.shape
    N = b.shape[1]
    tm, tk = _tile(M, tm), _tile(T, tk)
    if blocked_n is not None:
        tn = blocked_n
        out_shape = jax.ShapeDtypeStruct((N // tn, M, tn), F32)
        o_spec = pl.BlockSpec((None, tm, tn), lambda i, j, k: (j, i, 0))
    else:
        tn = _tile(N, tn)
        out_shape = jax.ShapeDtypeStruct((M, N), F32)
        o_spec = pl.BlockSpec((tm, tn), lambda i, j, k: (i, j))
    grid = (M // tm, N // tn, T // tk)
    return _mm(name, a, b, _TN, grid, pl.BlockSpec((tk, tm), lambda i, j, k: (k, i)),
               pl.BlockSpec((tk, tn), lambda i, j, k: (k, j)), [out_shape], [o_spec], (tm, tn), _store(F32))[0]


def _mean_last(v):
    return jnp.mean(v, axis=-1, keepdims=True)


def _rows_to_sublanes(v):
    r, c = v.shape
    return jnp.sum(v.reshape(r // SUBLANES, SUBLANES, c), axis=0)


def _accumulate(ref, val, first):
    @pl.when(first)
    def _():
        ref[...] = val

    @pl.when(jnp.logical_not(first))
    def _():
        ref[...] += val


def _rms_fwd(name, x, g, tr=256):
    T, D = x.shape
    tr = _tile(T, tr)

    def body(x_ref, g_ref, o_ref):
        xv = x_ref[...]
        r = lax.rsqrt(_mean_last(xv * xv) + EPS)
        o_ref[...] = (xv * r * g_ref[...]).astype(BF16)

    row = pl.BlockSpec((tr, D), lambda i: (i, 0))
    return pl.pallas_call(
        body, name=name, grid=(T // tr,),
        in_specs=[row, pl.BlockSpec((1, D), lambda i: (0, 0))],
        out_specs=row, out_shape=jax.ShapeDtypeStruct((T, D), BF16),
        compiler_params=_params(("parallel",)),
    )(x, g)


def _rms_bwd(name, dyn, x, g, dres, tr=256):
    T, D = x.shape
    tr = _tile(T, tr)

    def body(dy_ref, x_ref, g_ref, dr_ref, dx_ref, dxb_ref, dg_ref):
        xv = x_ref[...]
        r = lax.rsqrt(_mean_last(xv * xv) + EPS)
        xn = xv * r
        dy = dy_ref[...]
        dxn = dy * g_ref[...]
        dx = dr_ref[...] + r * (dxn - xn * _mean_last(dxn * xn))
        dx_ref[...] = dx
        dxb_ref[...] = dx.astype(BF16)
        _accumulate(dg_ref, _rows_to_sublanes(dy * xn), pl.program_id(0) == 0)

    row = pl.BlockSpec((tr, D), lambda i: (i, 0))
    return pl.pallas_call(
        body, name=name, grid=(T // tr,),
        in_specs=[row, row, pl.BlockSpec((1, D), lambda i: (0, 0)), row],
        out_specs=[row, row, pl.BlockSpec((SUBLANES, D), lambda i: (0, 0))],
        out_shape=[jax.ShapeDtypeStruct((T, D), F32), jax.ShapeDtypeStruct((T, D), BF16),
                   jax.ShapeDtypeStruct((SUBLANES, D), F32)],
        compiler_params=_params(("arbitrary",)),
    )(dyn, x, g, dres)


def _tail(h2, z, pe, target, g_ple, g_final, tr=256):
    T, D = h2.shape
    tr = _tile(T, tr)

    def body(h2_ref, z_ref, pe_ref, t_ref, gp_ref, gf_ref,
             dh3_ref, dz_ref, dpe_ref, dgf_ref, dgp_ref, loss_ref):
        first = pl.program_id(0) == 0
        pev = pe_ref[...]
        r3 = lax.rsqrt(_mean_last(pev * pev) + EPS)
        en = pev * r3
        e = en * gp_ref[...]
        gate = 1.0 / (1.0 + jnp.exp(-z_ref[...]))
        h3 = h2_ref[...] + gate * e
        r5 = lax.rsqrt(_mean_last(h3 * h3) + EPS)
        hn = h3 * r5
        diff = hn * gf_ref[...] - t_ref[...]
        loss_rows = 0.5 * _mean_last(diff * diff)
        row0 = lax.broadcasted_iota(jnp.int32, (SUBLANES, LANES), 0) == 0
        _accumulate(loss_ref, jnp.where(row0, jnp.sum(loss_rows), 0.0), first)
        dy = diff * (1.0 / D)
        _accumulate(dgf_ref, _rows_to_sublanes(dy * hn), first)
        dhn = dy * gf_ref[...]
        dh3 = r5 * (dhn - hn * _mean_last(dhn * hn))
        dh3_ref[...] = dh3
        dgate = dh3 * e
        de = dh3 * gate
        dz_ref[...] = (dgate * gate * (1.0 - gate)).astype(BF16)
        _accumulate(dgp_ref, _rows_to_sublanes(de * en), first)
        den = de * gp_ref[...]
        dpe_ref[...] = (r3 * (den - en * _mean_last(den * en))).astype(BF16)

    row = pl.BlockSpec((tr, D), lambda i: (i, 0))
    vec = pl.BlockSpec((1, D), lambda i: (0, 0))
    part = pl.BlockSpec((SUBLANES, D), lambda i: (0, 0))
    return pl.pallas_call(
        body, name="tail", grid=(T // tr,),
        in_specs=[row, row, row, row, vec, vec],
        out_specs=[row, row, row, part, part, pl.BlockSpec((SUBLANES, LANES), lambda i: (0, 0))],
        out_shape=[jax.ShapeDtypeStruct((T, D), F32), jax.ShapeDtypeStruct((T, D), BF16),
                   jax.ShapeDtypeStruct((T, D), BF16), jax.ShapeDtypeStruct((SUBLANES, D), F32),
                   jax.ShapeDtypeStruct((SUBLANES, D), F32), jax.ShapeDtypeStruct((SUBLANES, LANES), F32)],
        compiler_params=_params(("arbitrary",)),
    )(h2, z, pe, target, g_ple, g_final)


def _rope_tables(T):
    pos = np.arange(T)
    half = HEAD_DIM // 2
    inv = (ROPE_THETA ** (-np.arange(0, half, 2, dtype=np.float32) / half)).astype(np.float32)
    ang_r = (pos // GRID_W).astype(np.float32)[:, None] * inv
    ang_c = (pos % GRID_W).astype(np.float32)[:, None] * inv
    cos = np.concatenate([np.cos(ang_r), np.cos(ang_r), np.cos(ang_c), np.cos(ang_c)], axis=-1)
    sin = np.concatenate([-np.sin(ang_r), np.sin(ang_r), -np.sin(ang_c), np.sin(ang_c)], axis=-1)
    return jnp.asarray(cos, F32), jnp.asarray(sin, F32)


def _swap32(x):
    lane = lax.broadcasted_iota(jnp.int32, x.shape, 1)
    return jnp.where((lane % 64) < 32, pltpu.roll(x, 96, 1), pltpu.roll(x, 32, 1))


def _qk_prep(proj, cos, sin, g_q, g_k, n_norm, tr=256):
    T, W = proj.shape
    tr = _tile(T, tr)
    n_q = n_norm * GROUP // (GROUP + 1)

    def body(p_ref, c_ref, s_ref, gq_ref, gk_ref, o_ref):
        c, s = c_ref[...], s_ref[...]
        for h in range(n_norm):
            cols = slice(h * HEAD_DIM, (h + 1) * HEAD_DIM)
            xv = p_ref[:, cols]
            g = gq_ref[...] if h < n_q else gk_ref[...]
            xn = xv * lax.rsqrt(_mean_last(xv * xv) + EPS) * g
            o_ref[:, cols] = (xn * c + _swap32(xn) * s).astype(BF16)
        rest = slice(n_norm * HEAD_DIM, W)
        o_ref[:, rest] = p_ref[:, rest].astype(BF16)

    row = pl.BlockSpec((tr, W), lambda i: (i, 0))
    tab = pl.BlockSpec((tr, HEAD_DIM), lambda i: (i, 0))
    vec = pl.BlockSpec((1, HEAD_DIM), lambda i: (0, 0))
    return pl.pallas_call(
        body, name="qk_prep", grid=(T // tr,),
        in_specs=[row, tab, tab, vec, vec], out_specs=row,
        out_shape=jax.ShapeDtypeStruct((T, W), BF16),
        compiler_params=_params(("parallel",)),
    )(proj, cos, sin, g_q, g_k)


def _dproj(proj, dqa, dka, dva, dqb, dkb, dvb, cos, sin, g_q, g_k, tr=256):
    T, W = proj.shape
    tr = _tile(T, tr)
    n_q = dqa.shape[1] // HEAD_DIM
    n_kv = dka.shape[1] // HEAD_DIM
    wa = (n_q + n_kv) * HEAD_DIM

    def body(p_ref, dqa_ref, dka_ref, dva_ref, dqb_ref, dkb_ref, dvb_ref, c_ref, s_ref, gq_ref, gk_ref,
             o_ref, dgq_ref, dgk_ref):
        c, s = c_ref[...], s_ref[...]
        dgq = jnp.zeros((SUBLANES, HEAD_DIM), F32)
        dgk = jnp.zeros((SUBLANES, HEAD_DIM), F32)
        for h in range(n_q + n_kv):
            cols = slice(h * HEAD_DIM, (h + 1) * HEAD_DIM)
            xv = p_ref[:, cols]
            r = lax.rsqrt(_mean_last(xv * xv) + EPS)
            xn = xv * r
            if h < n_q:
                d = dqa_ref[:, cols]
                g = gq_ref[...]
            else:
                d = dka_ref[:, (h - n_q) * HEAD_DIM:(h - n_q + 1) * HEAD_DIM]
                g = gk_ref[...]
            dqn = d * c + _swap32(d * s)
            part = _rows_to_sublanes(dqn * xn)
            if h < n_q:
                dgq = dgq + part
            else:
                dgk = dgk + part
            dxn = dqn * g
            o_ref[:, cols] = (r * (dxn - xn * _mean_last(dxn * xn))).astype(BF16)
        off = wa
        for ref in (dva_ref, dqb_ref, dkb_ref, dvb_ref):
            w = ref.shape[1]
            o_ref[:, off:off + w] = ref[...].astype(BF16)
            off += w
        first = pl.program_id(0) == 0
        _accumulate(dgq_ref, dgq, first)
        _accumulate(dgk_ref, dgk, first)

    def row(w):
        return pl.BlockSpec((tr, w), lambda i: (i, 0))

    vec = pl.BlockSpec((1, HEAD_DIM), lambda i: (0, 0))
    part = pl.BlockSpec((SUBLANES, HEAD_DIM), lambda i: (0, 0))
    return pl.pallas_call(
        body, name="dproj", grid=(T // tr,),
        in_specs=[row(wa), row(dqa.shape[1]), row(dka.shape[1]), row(dva.shape[1]), row(dqb.shape[1]),
                  row(dkb.shape[1]), row(dvb.shape[1]), row(HEAD_DIM), row(HEAD_DIM), vec, vec],
        out_specs=[row(W), part, part],
        out_shape=[jax.ShapeDtypeStruct((T, W), BF16), jax.ShapeDtypeStruct((SUBLANES, HEAD_DIM), F32),
                   jax.ShapeDtypeStruct((SUBLANES, HEAD_DIM), F32)],
        compiler_params=_params(("arbitrary",)),
    )(proj, dqa, dka, dva, dqb, dkb, dvb, cos, sin, g_q, g_k)


def _attn_a_fwd(pb, n_q, n_kv, tq=256):
    T = pb.shape[0]
    tq = _tile(T, tq)
    scale = HEAD_DIM ** -0.5

    def body(q_ref, k_ref, v_ref, o_ref, lse_ref):
        s = lax.dot_general(q_ref[...], k_ref[...], _NT, preferred_element_type=F32) * scale
        m = jnp.max(s, axis=-1, keepdims=True)
        p = jnp.exp(s - m)
        l = jnp.sum(p, axis=-1, keepdims=True)
        o = lax.dot_general(p.astype(BF16), v_ref[...], _NN, preferred_element_type=F32)
        o_ref[...] = (o / l).astype(BF16)
        lse_ref[...] = m + jnp.log(l)

    return pl.pallas_call(
        body, name="attn_a_fwd", grid=(n_kv, GROUP, T // tq),
        in_specs=[pl.BlockSpec((tq, HEAD_DIM), lambda kv, g, i: (i, kv * GROUP + g)),
                  pl.BlockSpec((T, HEAD_DIM), lambda kv, g, i: (0, n_q + kv)),
                  pl.BlockSpec((T, HEAD_DIM), lambda kv, g, i: (0, n_q + n_kv + kv))],
        out_specs=[pl.BlockSpec((tq, HEAD_DIM), lambda kv, g, i: (i, kv * GROUP + g)),
                   pl.BlockSpec((None, tq, 1), lambda kv, g, i: (kv * GROUP + g, i, 0))],
        out_shape=[jax.ShapeDtypeStruct((T, n_q * HEAD_DIM), BF16), jax.ShapeDtypeStruct((n_q, T, 1), F32)],
        compiler_params=_params(("parallel", "parallel", "parallel")),
    )(pb, pb, pb)


def _attn_a_bwd(pb, o_cat, d_o, lse, n_q, n_kv, tq=256):
    T = pb.shape[0]
    tq = _tile(T, tq)
    scale = HEAD_DIM ** -0.5

    def body(q_ref, k_ref, v_ref, o_ref, do_ref, lse_ref, dq_ref, dk_ref, dv_ref):
        q, k, v, do = q_ref[...], k_ref[...], v_ref[...], do_ref[...]
        delta = jnp.sum(do.astype(F32) * o_ref[...].astype(F32), axis=-1, keepdims=True)
        s = lax.dot_general(q, k, _NT, preferred_element_type=F32) * scale
        p = jnp.exp(s - lse_ref[...])
        dp = lax.dot_general(do, v, _NT, preferred_element_type=F32)
        ds = (p * (dp - delta) * scale).astype(BF16)
        dq_ref[...] = lax.dot_general(ds, k, _NN, preferred_element_type=F32)
        first = jnp.logical_and(pl.program_id(1) == 0, pl.program_id(2) == 0)
        _accumulate(dv_ref, lax.dot_general(p.astype(BF16), do, _TN, preferred_element_type=F32), first)
        _accumulate(dk_ref, lax.dot_general(ds, q, _TN, preferred_element_type=F32), first)

    qmap = lambda kv, g, i: (i, kv * GROUP + g)
    return pl.pallas_call(
        body, name="attn_a_bwd", grid=(n_kv, GROUP, T // tq),
        in_specs=[pl.BlockSpec((tq, HEAD_DIM), qmap),
                  pl.BlockSpec((T, HEAD_DIM), lambda kv, g, i: (0, n_q + kv)),
                  pl.BlockSpec((T, HEAD_DIM), lambda kv, g, i: (0, n_q + n_kv + kv)),
                  pl.BlockSpec((tq, HEAD_DIM), qmap),
                  pl.BlockSpec((tq, HEAD_DIM), qmap),
                  pl.BlockSpec((None, tq, 1), lambda kv, g, i: (kv * GROUP + g, i, 0))],
        out_specs=[pl.BlockSpec((tq, HEAD_DIM), qmap),
                   pl.BlockSpec((T, HEAD_DIM), lambda kv, g, i: (0, kv)),
                   pl.BlockSpec((T, HEAD_DIM), lambda kv, g, i: (0, kv))],
        out_shape=[jax.ShapeDtypeStruct((T, n_q * HEAD_DIM), F32),
                   jax.ShapeDtypeStruct((T, n_kv * HEAD_DIM), F32),
                   jax.ShapeDtypeStruct((T, n_kv * HEAD_DIM), F32)],
        compiler_params=_params(("parallel", "arbitrary", "arbitrary")),
    )(pb, pb, pb, o_cat, d_o, lse)


def _bucket_index():
    r = np.arange(BLOCK_Q)[:, None]
    j = np.arange(3 * BLOCK_Q)[None, :]
    rel = (j - BLOCK_Q) - r
    nb = N_BUCKETS // 2
    ret = np.where(rel > 0, nb, 0)
    n = np.abs(rel)
    max_exact = nb // 2
    nf = np.maximum(n, 1).astype(np.float32)
    large = max_exact + (np.log(nf / max_exact) / math.log(MAX_DISTANCE / max_exact) * (nb - max_exact)).astype(np.int32)
    large = np.minimum(large, nb - 1)
    return jnp.asarray(ret + np.where(n < max_exact, n, large), jnp.int32)


def _bias_build(idx, table_flat, n_heads):
    def body(idx_ref, tab_ref, o_ref):
        h = pl.program_id(0)
        iv = idx_ref[...]
        acc = jnp.zeros(iv.shape, F32)
        for b in range(N_BUCKETS):
            acc = jnp.where(iv == b, tab_ref[b * n_heads + h], acc)
        o_ref[...] = acc

    return pl.pallas_call(
        body, name="bias_build", grid=(n_heads,),
        in_specs=[pl.BlockSpec(idx.shape, lambda h: (0, 0)), pl.BlockSpec(memory_space=pltpu.SMEM)],
        out_specs=pl.BlockSpec((None,) + idx.shape, lambda h: (h, 0, 0)),
        out_shape=jax.ShapeDtypeStruct((n_heads,) + idx.shape, F32),
        compiler_params=_params(("parallel",)),
    )(idx, table_flat)


def _band_mask(n, T):
    r = lax.broadcasted_iota(jnp.int32, (BLOCK_Q, 3 * BLOCK_Q), 0)
    j = lax.broadcasted_iota(jnp.int32, (BLOCK_Q, 3 * BLOCK_Q), 1)
    rel = (j - BLOCK_Q) - r
    kabs = n * BLOCK_Q + j - BLOCK_Q
    return (jnp.abs(rel) <= WINDOW) & (kabs >= 0) & (kabs < T)


def _band_specs(col, nblk):
    return [pl.BlockSpec((BLOCK_Q, HEAD_DIM), lambda kv, n: (jnp.maximum(n - 1, 0), col(kv))),
            pl.BlockSpec((BLOCK_Q, HEAD_DIM), lambda kv, n: (n, col(kv))),
            pl.BlockSpec((BLOCK_Q, HEAD_DIM), lambda kv, n: (jnp.minimum(n + 1, nblk - 1), col(kv)))]


def _attn_b_fwd(pb, bias, sink, q_off, n_q, n_kv):
    T = pb.shape[0]
    nblk = T // BLOCK_Q
    scale = HEAD_DIM ** -0.5

    def body(*refs):
        q_refs = refs[0:GROUP]
        k_refs, v_refs = refs[GROUP:GROUP + 3], refs[GROUP + 3:GROUP + 6]
        bias_ref, sink_ref, o_ref, lse_ref = refs[GROUP + 6:]
        kv, n = pl.program_id(0), pl.program_id(1)
        kb = jnp.concatenate([r[...] for r in k_refs], axis=0)
        vb = jnp.concatenate([r[...] for r in v_refs], axis=0)
        mask = _band_mask(n, T)
        for g in range(GROUP):
            sk = sink_ref[kv * GROUP + g]
            s = lax.dot_general(q_refs[g][...], kb, _NT, preferred_element_type=F32) * scale + bias_ref[g]
            s = jnp.where(mask, s, NEG_INF)
            m = jnp.maximum(jnp.max(s, axis=-1, keepdims=True), sk)
            p = jnp.exp(s - m)
            l = jnp.sum(p, axis=-1, keepdims=True) + jnp.exp(sk - m)
            o = lax.dot_general(p.astype(BF16), vb, _NN, preferred_element_type=F32)
            o_ref[:, g * HEAD_DIM:(g + 1) * HEAD_DIM] = (o / l).astype(BF16)
            lse_ref[g] = m + jnp.log(l)

    q_specs = [pl.BlockSpec((BLOCK_Q, HEAD_DIM), functools.partial(lambda kv, n, g: (n, q_off + kv * GROUP + g), g=g))
               for g in range(GROUP)]
    return pl.pallas_call(
        body, name="attn_b_fwd", grid=(n_kv, nblk),
        in_specs=[*q_specs,
                  *_band_specs(lambda kv: q_off + n_q + kv, nblk),
                  *_band_specs(lambda kv: q_off + n_q + n_kv + kv, nblk),
                  pl.BlockSpec((GROUP, BLOCK_Q, 3 * BLOCK_Q), lambda kv, n: (kv, 0, 0)),
                  pl.BlockSpec(memory_space=pltpu.SMEM)],
        out_specs=[pl.BlockSpec((BLOCK_Q, GROUP * HEAD_DIM), lambda kv, n: (n, kv)),
                   pl.BlockSpec((GROUP, BLOCK_Q, 1), lambda kv, n: (kv, n, 0))],
        out_shape=[jax.ShapeDtypeStruct((T, n_q * HEAD_DIM), BF16), jax.ShapeDtypeStruct((n_q, T, 1), F32)],
        compiler_params=_params(("parallel", "parallel")),
    )(*([pb] * (GROUP + 6)), bias, sink)


def _attn_b_bwd(pb, o_cat, d_o, lse, bias, sink, q_off, n_q, n_kv, o_off):
    T = pb.shape[0]
    nblk = T // BLOCK_Q
    scale = HEAD_DIM ** -0.5

    def body(*refs):
        q_refs = refs[0:GROUP]
        k_refs, v_refs = refs[GROUP:GROUP + 3], refs[GROUP + 3:GROUP + 6]
        o_refs, do_refs = refs[GROUP + 6:2 * GROUP + 6], refs[2 * GROUP + 6:3 * GROUP + 6]
        lse_ref, bias_ref, sink_ref, dq_ref, dk_ref, dv_ref, dbias_ref, dsink_ref = refs[3 * GROUP + 6:]
        kv, n = pl.program_id(0), pl.program_id(1)
        first = n == 0
        kb = jnp.concatenate([r[...] for r in k_refs], axis=0)
        vb = jnp.concatenate([r[...] for r in v_refs], axis=0)
        mask = _band_mask(n, T)
        dkb = jnp.zeros((3 * BLOCK_Q, HEAD_DIM), F32)
        dvb = jnp.zeros((3 * BLOCK_Q, HEAD_DIM), F32)
        row = lax.broadcasted_iota(jnp.int32, (SUBLANES, LANES), 0)
        dsink = jnp.zeros((SUBLANES, LANES), F32)
        for g in range(GROUP):
            sk = sink_ref[kv * GROUP + g]
            q, do = q_refs[g][...], do_refs[g][...]
            lse_g = lse_ref[g]
            delta = jnp.sum(do.astype(F32) * o_refs[g][...].astype(F32), axis=-1, keepdims=True)
            s = lax.dot_general(q, kb, _NT, preferred_element_type=F32) * scale + bias_ref[g]
            s = jnp.where(mask, s, NEG_INF)
            p = jnp.exp(s - lse_g)
            dp = lax.dot_general(do, vb, _NT, preferred_element_type=F32)
            ds = p * (dp - delta)
            _accumulate(dbias_ref.at[g], ds, first)
            dsink = dsink + jnp.where(row == g, -jnp.sum(jnp.exp(sk - lse_g) * delta), 0.0)
            dsb = (ds * scale).astype(BF16)
            dq_ref[:, g * HEAD_DIM:(g + 1) * HEAD_DIM] = lax.dot_general(dsb, kb, _NN, preferred_element_type=F32)
            dkb = dkb + lax.dot_general(dsb, q, _TN, preferred_element_type=F32)
            dvb = dvb + lax.dot_general(p.astype(BF16), do, _TN, preferred_element_type=F32)
        _accumulate(dsink_ref, dsink, first)

        @pl.when(first)
        def _():
            dk_ref[...] = jnp.zeros(dk_ref.shape, F32)
            dv_ref[...] = jnp.zeros(dv_ref.shape, F32)

        blocks = (jnp.maximum(n - 1, 0), n, jnp.minimum(n + 1, nblk - 1))
        for t, blk in enumerate(blocks):
            rows = pl.ds(pl.multiple_of(blk * BLOCK_Q, BLOCK_Q), BLOCK_Q)
            dk_ref[rows, :] += dkb[t * BLOCK_Q:(t + 1) * BLOCK_Q]
            dv_ref[rows, :] += dvb[t * BLOCK_Q:(t + 1) * BLOCK_Q]

    def head_specs(base):
        return [pl.BlockSpec((BLOCK_Q, HEAD_DIM), functools.partial(lambda kv, n, g: (n, base + kv * GROUP + g), g=g))
                for g in range(GROUP)]

    return pl.pallas_call(
        body, name="attn_b_bwd", grid=(n_kv, nblk),
        in_specs=[*head_specs(q_off),
                  *_band_specs(lambda kv: q_off + n_q + kv, nblk),
                  *_band_specs(lambda kv: q_off + n_q + n_kv + kv, nblk),
                  *head_specs(o_off), *head_specs(o_off),
                  pl.BlockSpec((GROUP, BLOCK_Q, 1), lambda kv, n: (kv, n, 0)),
                  pl.BlockSpec((GROUP, BLOCK_Q, 3 * BLOCK_Q), lambda kv, n: (kv, 0, 0)),
                  pl.BlockSpec(memory_space=pltpu.SMEM)],
        out_specs=[pl.BlockSpec((BLOCK_Q, GROUP * HEAD_DIM), lambda kv, n: (n, kv)),
                   pl.BlockSpec((T, HEAD_DIM), lambda kv, n: (0, kv)),
                   pl.BlockSpec((T, HEAD_DIM), lambda kv, n: (0, kv)),
                   pl.BlockSpec((GROUP, BLOCK_Q, 3 * BLOCK_Q), lambda kv, n: (kv, 0, 0)),
                   pl.BlockSpec((None, SUBLANES, LANES), lambda kv, n: (kv, 0, 0))],
        out_shape=[jax.ShapeDtypeStruct((T, n_q * HEAD_DIM), F32),
                   jax.ShapeDtypeStruct((T, n_kv * HEAD_DIM), F32),
                   jax.ShapeDtypeStruct((T, n_kv * HEAD_DIM), F32),
                   jax.ShapeDtypeStruct((n_q, BLOCK_Q, 3 * BLOCK_Q), F32),
                   jax.ShapeDtypeStruct((n_kv, SUBLANES, LANES), F32)],
        compiler_params=_params(("parallel", "arbitrary")),
    )(*([pb] * (GROUP + 6)), *([o_cat] * GROUP), *([d_o] * GROUP), lse, bias, sink)


def _table_grads(dbias, dsink_raw, idx):
    n_heads = dbias.shape[0]
    n_kv = dsink_raw.shape[0]

    def body(db_ref, ds_ref, idx_ref, dt_ref, dsk_ref):
        iv = idx_ref[...]
        row = lax.broadcasted_iota(jnp.int32, (SUBLANES, LANES), 0)
        lane = lax.broadcasted_iota(jnp.int32, (SUBLANES, LANES), 1)
        dsk = jnp.zeros((SUBLANES, LANES), F32)
        for h in range(n_heads):
            d = db_ref[h]
            acc = jnp.zeros((SUBLANES, LANES), F32)
            for b in range(N_BUCKETS):
                acc = jnp.where((row == 0) & (lane == b), jnp.sum(jnp.where(iv == b, d, 0.0)), acc)
            dt_ref[:, h * LANES:(h + 1) * LANES] = acc
            raw = ds_ref[h // GROUP]
            val = jnp.sum(jnp.where((row == h % GROUP) & (lane == 0), raw, 0.0))
            dsk = jnp.where((row == 0) & (lane == h), val, dsk)
        dsk_ref[...] = dsk

    return pl.pallas_call(
        body, name="table_grads",
        in_specs=[pl.BlockSpec(memory_space=pltpu.VMEM)] * 3,
        out_specs=[pl.BlockSpec(memory_space=pltpu.VMEM)] * 2,
        out_shape=[jax.ShapeDtypeStruct((SUBLANES, n_heads * LANES), F32),
                   jax.ShapeDtypeStruct((SUBLANES, LANES), F32)],
        compiler_params=pltpu.CompilerParams(vmem_limit_bytes=56 * 1024 * 1024),
    )(dbias, dsink_raw, idx)


def _position():
    x, y, c = lax.axis_index("x"), lax.axis_index("y"), lax.axis_index("c")
    return x, y, c


def _hbm_specs(n):
    return [pl.BlockSpec(memory_space=pltpu.HBM)] * n


def _all_gather(shards):
    nw = len(shards)

    def body(*refs):
        ins, outs = refs[:nw], refs[nw:2 * nw]
        send_sems, recv_sems, local_sems = refs[2 * nw:]
        x, y, c = _position()
        me, sibling = (x, y, c), (x, y, 1 - c)
        chips = [(1 - x, y), (x, 1 - y), (1 - x, 1 - y)]

        def blk(pos):
            return 4 * pos[0] + 2 * pos[1] + pos[2]

        def copy(w, k, block, to, src=None):
            dst = outs[w].at[blk(block)]
            return pltpu.make_async_remote_copy(
                src_ref=dst if src is None else src, dst_ref=dst,
                send_sem=send_sems.at[w, k], recv_sem=recv_sems.at[w, k], device_id=to, device_id_type=MESH)

        started = []
        mine = []
        for w in range(nw):
            cp = pltpu.make_async_copy(ins[w], outs[w].at[blk(me)], local_sems.at[w])
            cp.start()
            mine.append(cp)
            first = [copy(w, 0, me, sibling, src=ins[w])]
            first += [copy(w, 1 + j, me, (*chip, c), src=ins[w]) for j, chip in enumerate(chips)]
            for cp in first:
                cp.start()
            started += first
        for w in range(nw):
            for j, chip in enumerate(chips):
                copy(w, 1 + j, (*chip, c), me).wait_recv()
                fwd = copy(w, 4 + j, (*chip, c), sibling)
                fwd.start()
                started.append(fwd)
        for w in range(nw):
            copy(w, 0, sibling, me).wait_recv()
            for j, chip in enumerate(chips):
                copy(w, 4 + j, (*chip, 1 - c), me).wait_recv()
        for cp in started:
            cp.wait_send()
        for cp in mine:
            cp.wait()

    return pl.pallas_call(
        body, name="all_gather_weights",
        in_specs=_hbm_specs(nw), out_specs=_hbm_specs(nw),
        out_shape=[jax.ShapeDtypeStruct((N_DEV,) + s.shape, s.dtype) for s in shards],
        scratch_shapes=[pltpu.SemaphoreType.DMA((nw, 7)), pltpu.SemaphoreType.DMA((nw, 7)),
                        pltpu.SemaphoreType.DMA((nw,))],
    )(*shards)


def _pair_exchange(grads):
    nw = len(grads)

    def body(*refs):
        ins, outs = refs[:nw], refs[nw:2 * nw]
        send_sems, recv_sems = refs[2 * nw:]
        x, y, c = _position()
        sibling = (x, y, 1 - c)
        copies = []
        for w in range(nw):
            for q in range(N_CHIP):
                cp = pltpu.make_async_remote_copy(
                    src_ref=ins[w].at[2 * q + 1 - c], dst_ref=outs[w].at[q],
                    send_sem=send_sems.at[w, q], recv_sem=recv_sems.at[w, q], device_id=sibling, device_id_type=MESH)
                cp.start()
                copies.append(cp)
        for cp in copies:
            cp.wait()

    return pl.pallas_call(
        body, name="pair_exchange",
        in_specs=_hbm_specs(nw), out_specs=_hbm_specs(nw),
        out_shape=[jax.ShapeDtypeStruct((N_CHIP,) + g.shape[1:], g.dtype) for g in grads],
        scratch_shapes=[pltpu.SemaphoreType.DMA((nw, N_CHIP)), pltpu.SemaphoreType.DMA((nw, N_CHIP))],
    )(*grads)


def _pair_sum(grad, landed, tr=256):
    _, R, C = grad.shape
    tr = _tile(R, tr)
    core = lax.axis_index("c").astype(jnp.int32).reshape(1)

    def body(c_ref, g_ref, l_ref, o_ref):
        o_ref[...] = (g_ref[...] + l_ref[...]).astype(BF16)

    slot = pl.BlockSpec((None, tr, C), lambda q, i, c_ref: (q, i, 0))
    return pl.pallas_call(
        body, name="pair_sum",
        grid_spec=pltpu.PrefetchScalarGridSpec(
            num_scalar_prefetch=1, grid=(N_CHIP, R // tr),
            in_specs=[pl.BlockSpec((None, tr, C), lambda q, i, c_ref: (2 * q + c_ref[0], i, 0)), slot],
            out_specs=slot),
        out_shape=jax.ShapeDtypeStruct((N_CHIP, R, C), BF16),
        compiler_params=_params(("parallel", "parallel")),
    )(core, grad, landed)


def _chip_exchange(sums):
    nw = len(sums)

    def body(*refs):
        ins, outs = refs[:nw], refs[nw:2 * nw]
        send_sems, recv_sems, local_sems = refs[2 * nw:]
        x, y, c = _position()
        my_chip = 2 * x + y
        copies = []
        for w in range(nw):
            own = pltpu.make_async_copy(ins[w].at[my_chip], outs[w].at[my_chip], local_sems.at[w])
            own.start()
            copies.append(own)
            for k, (px, py) in enumerate([(1 - x, y), (x, 1 - y), (1 - x, 1 - y)]):
                cp = pltpu.make_async_remote_copy(
                    src_ref=ins[w].at[2 * px + py], dst_ref=outs[w].at[my_chip],
                    send_sem=send_sems.at[w, k], recv_sem=recv_sems.at[w, k], device_id=(px, py, c),
                    device_id_type=MESH)
                cp.start()
                copies.append(cp)
        for cp in copies:
            cp.wait()

    return pl.pallas_call(
        body, name="chip_exchange",
        in_specs=_hbm_specs(nw), out_specs=_hbm_specs(nw),
        out_shape=[jax.ShapeDtypeStruct(s.shape, s.dtype) for s in sums],
        scratch_shapes=[pltpu.SemaphoreType.DMA((nw, 3)), pltpu.SemaphoreType.DMA((nw, 3)),
                        pltpu.SemaphoreType.DMA((nw,))],
    )(*sums)


def _adam(w, g, m, v):
    m = ADAM_B1 * m + (1.0 - ADAM_B1) * g
    v = ADAM_B2 * v + (1.0 - ADAM_B2) * (g * g)
    m_hat = m / (1.0 - ADAM_B1 ** ADAM_STEP)
    v_hat = v / (1.0 - ADAM_B2 ** ADAM_STEP)
    delta = -ADAM_LR * (m_hat / (jnp.sqrt(v_hat) + ADAM_EPS) + ADAM_WD * w)
    return delta, m, v


def _sum_adam(name, landed, w, m, v, tr=256):
    R, C = w.shape
    tr = _tile(R, tr)

    def body(l_ref, w_ref, m_ref, v_ref, g_ref, d_ref, nm_ref, nv_ref):
        g = l_ref[0].astype(F32)
        for q in range(1, N_CHIP):
            g = g + l_ref[q].astype(F32)
        g_ref[...] = g
        d_ref[...], nm_ref[...], nv_ref[...] = _adam(w_ref[...], g, m_ref[...], v_ref[...])

    tile = pl.BlockSpec((tr, C), lambda i: (i, 0))
    return pl.pallas_call(
        body, name=name, grid=(R // tr,),
        in_specs=[pl.BlockSpec((N_CHIP, tr, C), lambda i: (0, i, 0)), tile, tile, tile],
        out_specs=[tile] * 4, out_shape=[jax.ShapeDtypeStruct((R, C), F32)] * 4,
        compiler_params=_params(("parallel",)),
    )(landed, w, m, v)


def _small_all_reduce(parts):
    W = parts.shape[1]

    def body(p_ref, o_ref, slots, send_sems, recv_sems):
        x, y, c = _position()
        me = 4 * x + 2 * y + c
        slots[me] = jnp.sum(p_ref[...], axis=0, keepdims=True)
        peers = [(x, y, 1 - c), (1 - x, y, c), (1 - x, y, 1 - c), (x, 1 - y, c), (x, 1 - y, 1 - c),
                 (1 - x, 1 - y, c), (1 - x, 1 - y, 1 - c)]
        copies = []
        for k, peer in enumerate(peers):
            cp = pltpu.make_async_remote_copy(
                src_ref=slots.at[me], dst_ref=slots.at[me], send_sem=send_sems.at[k], recv_sem=recv_sems.at[k],
                device_id=peer, device_id_type=MESH)
            cp.start()
            copies.append(cp)
        for cp in copies:
            cp.wait()
        total = slots[0]
        for d in range(1, N_DEV):
            total = total + slots[d]
        o_ref[...] = total

    return pl.pallas_call(
        body, name="small_all_reduce",
        in_specs=[pl.BlockSpec(memory_space=pltpu.VMEM)], out_specs=pl.BlockSpec(memory_space=pltpu.VMEM),
        out_shape=jax.ShapeDtypeStruct((1, W), F32),
        scratch_shapes=[pltpu.VMEM((N_DEV, 1, W), F32), pltpu.SemaphoreType.DMA((7,)), pltpu.SemaphoreType.DMA((7,))],
    )(parts)


def _adam_small(w, g, m, v):
    def body(w_ref, g_ref, m_ref, v_ref, d_ref, nm_ref, nv_ref):
        d_ref[...], nm_ref[...], nv_ref[...] = _adam(w_ref[...], g_ref[...], m_ref[...], v_ref[...])

    return pl.pallas_call(
        body, name="adam_small",
        in_specs=[pl.BlockSpec(memory_space=pltpu.VMEM)] * 4, out_specs=[pl.BlockSpec(memory_space=pltpu.VMEM)] * 3,
        out_shape=[jax.ShapeDtypeStruct(w.shape, F32)] * 3,
    )(w, g, m, v)


def _local_step(x, p, target, gains, full):
    T, D = x.shape
    n_q = D // (2 * HEAD_DIM)
    n_kv = n_q // GROUP
    cos, sin = _rope_tables(T)
    idx = _bucket_index()

    u = _rms_fwd("norm_attn", x, gains["attn_norm_g"])
    proj = _mm_nn("in_proj", u, full["w_in"])
    pb = _qk_prep(proj, cos, sin, gains["q_norm_g"], gains["k_norm_g"], n_q + n_kv)
    o_a, lse_a = _attn_a_fwd(pb, n_q, n_kv)
    bias = _bias_build(idx, gains["rel_bias_table"].reshape(-1), n_q)
    sink = gains["sink_logits"].reshape(-1)
    b_off = n_q + 2 * n_kv
    o_b, lse_b = _attn_b_fwd(pb, bias, sink, b_off, n_q, n_kv)
    o_cat = jnp.concatenate([o_a, o_b], axis=1)
    h1 = _mm_nn("out_proj", o_cat, full["w_out"], epilogue=_store_add, extras=(x,))
    m_in = _rms_fwd("norm_mlp", h1, gains["mlp_norm_g"])

    def up_epilogue(acc, extra, outs):
        outs[0][...] = acc.astype(BF16)
        r = jnp.maximum(acc, 0.0)
        outs[1][...] = (r * r).astype(BF16)

    a_act, f_act = _mm_nn("up_proj", m_in, full["w_up"], epilogue=up_epilogue, out_dtypes=[BF16, BF16])
    h2 = _mm_nn("down_proj", f_act, full["w_down"], epilogue=_store_add, extras=(h1,), tn=1024, tk=1024)
    p_b = p.astype(BF16)
    pe = _mm_nn("ple_proj", p_b, full["ple_w"])
    gn = _rms_fwd("norm_gate", h2, gains["gate_norm_g"])
    z = _mm_nn("gate_proj", gn, full["w_gate"])

    dh3, dz, dpe, dg_final, dg_ple, loss_part = _tail(h2, z, pe, target, gains["ple_norm_g"], gains["final_norm_g"])
    gw_gate = _mm_tn("grad_w_gate", gn, dz)
    dgn = _mm_nt("d_gate_in", dz, full["w_gate"])
    dh2, dh2_b, dg_gate = _rms_bwd("norm_gate_bwd", dgn, h2, gains["gate_norm_g"], dh3)
    gw_ple = _mm_tn("grad_ple_w", p_b, dpe, blocked_n=full["ple_w"].shape[2])
    gw_down = _mm_tn("grad_w_down", f_act, dh2_b)

    def act_bwd(acc, extra, outs):
        outs[0][...] = (acc * (2.0 * jnp.maximum(extra[0][...].astype(F32), 0.0))).astype(BF16)

    da = _mm_nt("d_act", dh2_b, full["w_down"], out_dtype=BF16, epilogue=act_bwd, extras=(a_act,), tn=1024)
    gw_up = _mm_tn("grad_w_up", m_in, da, blocked_n=full["w_up"].shape[2])
    dm = _mm_nt("d_mlp_in", da, full["w_up"])
    dh1, dh1_b, dg_mlp = _rms_bwd("norm_mlp_bwd", dm, h1, gains["mlp_norm_g"], dh2)
    gw_out = _mm_tn("grad_w_out", o_cat, dh1_b)
    d_o = _mm_nt("d_attn_out", dh1_b, full["w_out"], out_dtype=BF16)
    dqa, dka, dva = _attn_a_bwd(pb, o_cat, d_o, lse_a, n_q, n_kv)
    dqb, dkb, dvb, dbias, dsink_raw = _attn_b_bwd(pb, o_cat, d_o, lse_b, bias, sink, b_off, n_q, n_kv, n_q)
    dtable, dsink = _table_grads(dbias, dsink_raw, idx)
    dproj, dg_q, dg_k = _dproj(proj, dqa, dka, dva, dqb, dkb, dvb, cos, sin, gains["q_norm_g"], gains["k_norm_g"])
    gw_in = _mm_tn("grad_w_in", u, dproj, blocked_n=full["w_in"].shape[2])
    du = _mm_nt("d_attn_in", dproj, full["w_in"])
    dx, _, dg_attn = _rms_bwd("norm_attn_bwd", du, x, gains["attn_norm_g"], dh1)

    parts = jnp.concatenate([dg_attn, dg_mlp, dg_ple, dg_gate, dg_final, dg_q, dg_k, dtable, dsink, loss_part], axis=1)
    rows = lambda g: g.reshape((N_DEV, g.shape[0] // N_DEV, g.shape[1]))
    grads = dict(w_in=gw_in, w_out=rows(gw_out), w_up=gw_up, w_down=rows(gw_down), ple_w=gw_ple, w_gate=rows(gw_gate))
    return dx, grads, parts


_SHARDED = ("w_in", "w_out", "w_up", "w_down", "ple_w", "w_gate")
_VECTORS = ("attn_norm_g", "mlp_norm_g", "ple_norm_g", "gate_norm_g", "final_norm_g")
_ORDER = ("attn_norm_g", "w_in", "q_norm_g", "k_norm_g", "sink_logits", "w_out", "mlp_norm_g", "w_up", "w_down",
          "ple_w", "ple_norm_g", "gate_norm_g", "w_gate", "rel_bias_table", "final_norm_g")


def _pack_small(vals, n_heads):
    lane_pad = lambda v: jnp.pad(v, ((0, 0), (0, LANES - v.shape[1])))
    table = lane_pad(vals["rel_bias_table"].T).reshape(1, n_heads * LANES)
    return jnp.concatenate(
        [vals[n].reshape(1, -1) for n in _VECTORS] + [vals["q_norm_g"], vals["k_norm_g"], table,
                                                      lane_pad(vals["sink_logits"]), jnp.zeros((1, LANES), F32)], axis=1)


def _unpack_small(row, like, n_heads):
    out, off = {}, 0
    for n in _VECTORS:
        out[n] = row[:, off:off + like[n].size].reshape(like[n].shape)
        off += like[n].size
    for n in ("q_norm_g", "k_norm_g"):
        out[n] = row[:, off:off + LANES]
        off += LANES
    out["rel_bias_table"] = row[:, off:off + n_heads * LANES].reshape(n_heads, LANES)[:, :N_BUCKETS].T
    off += n_heads * LANES
    out["sink_logits"] = row[:, off:off + n_heads]
    off += LANES
    return out, row[0, off]


def kernel(x, p, attn_norm_g, w_in, q_norm_g, k_norm_g, sink_logits, w_out, mlp_norm_g, w_up, w_down, ple_w, ple_norm_g, gate_norm_g, w_gate, rel_bias_table, final_norm_g, loss_target, m_attn_norm_g, m_w_in, m_q_norm_g, m_k_norm_g, m_sink_logits, m_w_out, m_mlp_norm_g, m_w_up, m_w_down, m_ple_w, m_ple_norm_g, m_gate_norm_g, m_w_gate, m_rel_bias_table, m_final_norm_g, v_attn_norm_g, v_w_in, v_q_norm_g, v_k_norm_g, v_sink_logits, v_w_out, v_mlp_norm_g, v_w_up, v_w_down, v_ple_w, v_ple_norm_g, v_gate_norm_g, v_w_gate, v_rel_bias_table, v_final_norm_g):
    w = dict(attn_norm_g=attn_norm_g, w_in=w_in[0], q_norm_g=q_norm_g, k_norm_g=k_norm_g, sink_logits=sink_logits,
             w_out=w_out[0], mlp_norm_g=mlp_norm_g, w_up=w_up[0], w_down=w_down[0], ple_w=ple_w[0],
             ple_norm_g=ple_norm_g, gate_norm_g=gate_norm_g, w_gate=w_gate[0], rel_bias_table=rel_bias_table,
             final_norm_g=final_norm_g)
    mom = dict(attn_norm_g=m_attn_norm_g, w_in=m_w_in[0], q_norm_g=m_q_norm_g, k_norm_g=m_k_norm_g,
               sink_logits=m_sink_logits, w_out=m_w_out[0], mlp_norm_g=m_mlp_norm_g, w_up=m_w_up[0],
               w_down=m_w_down[0], ple_w=m_ple_w[0], ple_norm_g=m_ple_norm_g, gate_norm_g=m_gate_norm_g,
               w_gate=m_w_gate[0], rel_bias_table=m_rel_bias_table, final_norm_g=m_final_norm_g)
    var = dict(attn_norm_g=v_attn_norm_g, w_in=v_w_in[0], q_norm_g=v_q_norm_g, k_norm_g=v_k_norm_g,
               sink_logits=v_sink_logits, w_out=v_w_out[0], mlp_norm_g=v_mlp_norm_g, w_up=v_w_up[0],
               w_down=v_w_down[0], ple_w=v_ple_w[0], ple_norm_g=v_ple_norm_g, gate_norm_g=v_gate_norm_g,
               w_gate=v_w_gate[0], rel_bias_table=v_rel_bias_table, final_norm_g=v_final_norm_g)
    D = x.shape[-1]
    n_heads = D // (2 * HEAD_DIM)

    gathered = _all_gather([w[n].astype(BF16) for n in _SHARDED])
    full = dict(zip(_SHARDED, gathered))
    for n in ("w_out", "w_down", "w_gate"):
        full[n] = full[n].reshape(-1, full[n].shape[-1])
    gains = {n: w[n] for n in w if n not in _SHARDED}
    gains["final_norm_g"] = final_norm_g.reshape(1, -1)

    dx, grads, parts = _local_step(x[0], p[0, 0], loss_target[0], gains, full)

    landed = _pair_exchange([grads[n] for n in _SHARDED])
    sums = [_pair_sum(grads[n], l) for n, l in zip(_SHARDED, landed)]
    arrived = _chip_exchange(sums)
    g_out, d_out, m_out, v_out = {}, {}, {}, {}
    for n, l in zip(_SHARDED, arrived):
        g, d, nm, nv = _sum_adam("adam_" + n, l, w[n], mom[n], var[n])
        g_out[n], d_out[n], m_out[n], v_out[n] = g[None], d[None], nm[None], nv[None]

    small_g = _small_all_reduce(parts)
    small = {n: v for n, v in w.items() if n not in _SHARDED}
    pack = lambda vals: _pack_small({n: vals[n] for n in small}, n_heads)
    sd, sm, sv = _adam_small(pack(w), small_g, pack(mom), pack(var))
    sg, loss = _unpack_small(small_g, small, n_heads)
    g_out.update(sg)
    for dst, row in ((d_out, sd), (m_out, sm), (v_out, sv)):
        dst.update(_unpack_small(row, small, n_heads)[0])

    return (loss, dx[None], *[g_out[n] for n in _ORDER], *[d_out[n] for n in _ORDER],
            *[m_out[n] for n in _ORDER], *[v_out[n] for n in _ORDER])
```

```python
import functools
import math

import numpy as np
import jax
import jax.numpy as jnp
from jax import lax
from jax.experimental import pallas as pl
from jax.experimental.pallas import tpu as pltpu

F32 = jnp.float32
BF16 = jnp.bfloat16

N_DEV = 8
N_CHIP = 4
HEAD_DIM = 128
GROUP = 4
GRID_W = 64
WINDOW = 128
BLOCK_Q = 128
N_BUCKETS = 32
MAX_DISTANCE = 128
ROPE_THETA = 10000.0
EPS = 1e-6
NEG_INF = -1e30
ADAM_LR = 0.001
ADAM_B1 = 0.9
ADAM_B2 = 0.999
ADAM_EPS = 1e-08
ADAM_WD = 0.01
ADAM_STEP = 10
LANES = 128
SUBLANES = 8
MESH = pl.DeviceIdType.MESH

_NT = (((1,), (1,)), ((), ()))
_NN = (((1,), (0,)), ((), ()))
_TN = (((0,), (0,)), ((), ()))


def _tile(dim, pref):
    return pref if dim % pref == 0 else dim


def _params(sem):
    return pltpu.CompilerParams(dimension_semantics=sem, vmem_limit_bytes=56 * 1024 * 1024)


_HBM = pl.BlockSpec(memory_space=pltpu.HBM)
_SEM = pl.BlockSpec(memory_space=pltpu.SEMAPHORE)
_ANY = pl.BlockSpec(memory_space=pl.ANY)
_VMEM = pl.BlockSpec(memory_space=pltpu.VMEM)
_EFFECT = pltpu.SideEffectType.DATAFLOW_SIDE_EFFECTING


def _pcall(body, deps=(), *, in_specs, **kw):
    deps = [d for d in deps if d is not None]
    nd = len(deps)

    def wrapped(*refs):
        body(*refs[nd:])

    call = pl.pallas_call(wrapped, in_specs=[_ANY] * nd + list(in_specs), **kw)
    return lambda *args: call(*deps, *args)


def _mm(name, a, b, dims, grid, a_spec, b_spec, out_shape, out_specs, acc_shape, epilogue,
        extras=(), extra_specs=(), deps=()):
    nk = grid[2]
    n_extra = len(extras)

    def body(*refs):
        a_ref, b_ref = refs[0], refs[1]
        extra = refs[2:2 + n_extra]
        outs = refs[2 + n_extra:-1]
        acc = refs[-1]
        part = lax.dot_general(a_ref[...], b_ref[...], dims, preferred_element_type=F32)
        if nk == 1:
            epilogue(part, extra, outs)
        else:
            k = pl.program_id(2)

            @pl.when(k == 0)
            def _():
                acc[...] = part

            @pl.when(k > 0)
            def _():
                acc[...] += part

            @pl.when(k == nk - 1)
            def _():
                epilogue(acc[...], extra, outs)

    return _pcall(
        body, deps, name=name, grid=grid,
        in_specs=[a_spec, b_spec, *extra_specs],
        out_specs=out_specs, out_shape=out_shape,
        scratch_shapes=[pltpu.VMEM(acc_shape if nk > 1 else (SUBLANES, LANES), F32)],
        compiler_params=_params(("parallel", "parallel", "arbitrary")),
    )(a, b, *extras)


def _store(dtype):
    def ep(acc, extra, outs):
        outs[0][...] = acc.astype(dtype)
    return ep


def _store_add(acc, extra, outs):
    outs[0][...] = acc + extra[0][...]


def _mm_nn(name, a, b, out_dtype=F32, epilogue=None, extras=(), n_out=1, out_dtypes=None, tm=1024, tn=512, tk=2048,
           deps=()):
    M, K = a.shape
    tm, tk = _tile(M, tm), _tile(K, tk)
    if b.ndim == 3:
        nb, _, n = b.shape
        N, tn = nb * n, n
        b_spec = pl.BlockSpec((None, tk, n), lambda i, j, k: (j, k, 0))
    else:
        N = b.shape[1]
        tn = _tile(N, tn)
        b_spec = pl.BlockSpec((tk, tn), lambda i, j, k: (k, j))
    grid = (M // tm, N // tn, K // tk)
    o_spec = pl.BlockSpec((tm, tn), lambda i, j, k: (i, j))
    out_dtypes = out_dtypes or [out_dtype] * n_out
    out_shape = [jax.ShapeDtypeStruct((M, N), d) for d in out_dtypes]
    res = _mm(name, a, b, _NN, grid, pl.BlockSpec((tm, tk), lambda i, j, k: (i, k)), b_spec,
              out_shape, [o_spec] * len(out_dtypes), (tm, tn), epilogue or _store(out_dtype),
              extras, [o_spec] * len(extras), deps)
    return res if len(out_dtypes) > 1 else res[0]


def _mm_nt(name, a, b, out_dtype=F32, epilogue=None, extras=(), tm=1024, tn=512, tk=2048, deps=()):
    M, C = a.shape
    tm = _tile(M, tm)
    if b.ndim == 3:
        nb, N, n = b.shape
        tk = n
        tn = _tile(N, tn)
        b_spec = pl.BlockSpec((None, tn, n), lambda i, j, k: (k, j, 0))
    else:
        N = b.shape[0]
        tn, tk = _tile(N, tn), _tile(C, tk)
        b_spec = pl.BlockSpec((tn, tk), lambda i, j, k: (j, k))
    grid = (M // tm, N // tn, C // tk)
    o_spec = pl.BlockSpec((tm, tn), lambda i, j, k: (i, j))
    return _mm(name, a, b, _NT, grid, pl.BlockSpec((tm, tk), lambda i, j, k: (i, k)), b_spec,
               [jax.ShapeDtypeStruct((M, N), out_dtype)], [o_spec], (tm, tn), epilogue or _store(out_dtype),
               extras, [o_spec] * len(extras), deps)[0]


def _mm_tn(name, a, b, blocked_n=None, tm=1024, tn=1024, tk=1024, deps=()):
    T, M = a.shape
    N = b.shape[1]
    tm, tk = _tile(M, tm), _tile(T, tk)
    if blocked_n is not None:
        tn = blocked_n
        out_shape = jax.ShapeDtypeStruct((N // tn, M, tn), F32)
        o_spec = pl.BlockSpec((None, tm, tn), lambda i, j, k: (j, i, 0))
    else:
        tn = _tile(N, tn)
        out_shape = jax.ShapeDtypeStruct((M, N), F32)
        o_spec = pl.BlockSpec((tm, tn), lambda i, j, k: (i, j))
    grid = (M // tm, N // tn, T // tk)
    return _mm(name, a, b, _TN, grid, pl.BlockSpec((tk, tm), lambda i, j, k: (k, i)),
               pl.BlockSpec((tk, tn), lambda i, j, k: (k, j)), [out_shape], [o_spec], (tm, tn), _store(F32),
               deps=deps)[0]


def _mean_last(v):
    return jnp.mean(v, axis=-1, keepdims=True)


def _rows_to_sublanes(v):
    r, c = v.shape
    return jnp.sum(v.reshape(r // SUBLANES, SUBLANES, c), axis=0)


def _accumulate(ref, val, first):
    @pl.when(first)
    def _():
        ref[...] = val

    @pl.when(jnp.logical_not(first))
    def _():
        ref[...] += val


def _rms_fwd(name, x, g, tr=256, deps=()):
    T, D = x.shape
    tr = _tile(T, tr)

    def body(x_ref, g_ref, o_ref):
        xv = x_ref[...]
        r = lax.rsqrt(_mean_last(xv * xv) + EPS)
        o_ref[...] = (xv * r * g_ref[...]).astype(BF16)

    row = pl.BlockSpec((tr, D), lambda i: (i, 0))
    return _pcall(
        body, deps, name=name, grid=(T // tr,),
        in_specs=[row, pl.BlockSpec((1, D), lambda i: (0, 0))],
        out_specs=row, out_shape=jax.ShapeDtypeStruct((T, D), BF16),
        compiler_params=_params(("parallel",)),
    )(x, g)


def _rms_bwd(name, dyn, x, g, dres, tr=256, deps=()):
    T, D = x.shape
    tr = _tile(T, tr)

    def body(dy_ref, x_ref, g_ref, dr_ref, dx_ref, dxb_ref, dg_ref):
        xv = x_ref[...]
        r = lax.rsqrt(_mean_last(xv * xv) + EPS)
        xn = xv * r
        dy = dy_ref[...]
        dxn = dy * g_ref[...]
        dx = dr_ref[...] + r * (dxn - xn * _mean_last(dxn * xn))
        dx_ref[...] = dx
        dxb_ref[...] = dx.astype(BF16)
        _accumulate(dg_ref, _rows_to_sublanes(dy * xn), pl.program_id(0) == 0)

    row = pl.BlockSpec((tr, D), lambda i: (i, 0))
    return _pcall(
        body, deps, name=name, grid=(T // tr,),
        in_specs=[row, row, pl.BlockSpec((1, D), lambda i: (0, 0)), row],
        out_specs=[row, row, pl.BlockSpec((SUBLANES, D), lambda i: (0, 0))],
        out_shape=[jax.ShapeDtypeStruct((T, D), F32), jax.ShapeDtypeStruct((T, D), BF16),
                   jax.ShapeDtypeStruct((SUBLANES, D), F32)],
        compiler_params=_params(("arbitrary",)),
    )(dyn, x, g, dres)


def _tail(h2, z, pe, target, g_ple, g_final, tr=256):
    T, D = h2.shape
    tr = _tile(T, tr)

    def body(h2_ref, z_ref, pe_ref, t_ref, gp_ref, gf_ref,
             dh3_ref, dz_ref, dpe_ref, dgf_ref, dgp_ref, loss_ref):
        first = pl.program_id(0) == 0
        pev = pe_ref[...]
        r3 = lax.rsqrt(_mean_last(pev * pev) + EPS)
        en = pev * r3
        e = en * gp_ref[...]
        gate = 1.0 / (1.0 + jnp.exp(-z_ref[...]))
        h3 = h2_ref[...] + gate * e
        r5 = lax.rsqrt(_mean_last(h3 * h3) + EPS)
        hn = h3 * r5
        diff = hn * gf_ref[...] - t_ref[...]
        loss_rows = 0.5 * _mean_last(diff * diff)
        row0 = lax.broadcasted_iota(jnp.int32, (SUBLANES, LANES), 0) == 0
        _accumulate(loss_ref, jnp.where(row0, jnp.sum(loss_rows), 0.0), first)
        dy = diff * (1.0 / D)
        _accumulate(dgf_ref, _rows_to_sublanes(dy * hn), first)
        dhn = dy * gf_ref[...]
        dh3 = r5 * (dhn - hn * _mean_last(dhn * hn))
        dh3_ref[...] = dh3
        dgate = dh3 * e
        de = dh3 * gate
        dz_ref[...] = (dgate * gate * (1.0 - gate)).astype(BF16)
        _accumulate(dgp_ref, _rows_to_sublanes(de * en), first)
        den = de * gp_ref[...]
        dpe_ref[...] = (r3 * (den - en * _mean_last(den * en))).astype(BF16)

    row = pl.BlockSpec((tr, D), lambda i: (i, 0))
    vec = pl.BlockSpec((1, D), lambda i: (0, 0))
    part = pl.BlockSpec((SUBLANES, D), lambda i: (0, 0))
    return pl.pallas_call(
        body, name="tail", grid=(T // tr,),
        in_specs=[row, row, row, row, vec, vec],
        out_specs=[row, row, row, part, part, pl.BlockSpec((SUBLANES, LANES), lambda i: (0, 0))],
        out_shape=[jax.ShapeDtypeStruct((T, D), F32), jax.ShapeDtypeStruct((T, D), BF16),
                   jax.ShapeDtypeStruct((T, D), BF16), jax.ShapeDtypeStruct((SUBLANES, D), F32),
                   jax.ShapeDtypeStruct((SUBLANES, D), F32), jax.ShapeDtypeStruct((SUBLANES, LANES), F32)],
        compiler_params=_params(("arbitrary",)),
    )(h2, z, pe, target, g_ple, g_final)


def _rope_tables(T):
    pos = np.arange(T)
    half = HEAD_DIM // 2
    inv = (ROPE_THETA ** (-np.arange(0, half, 2, dtype=np.float32) / half)).astype(np.float32)
    ang_r = (pos // GRID_W).astype(np.float32)[:, None] * inv
    ang_c = (pos % GRID_W).astype(np.float32)[:, None] * inv
    cos = np.concatenate([np.cos(ang_r), np.cos(ang_r), np.cos(ang_c), np.cos(ang_c)], axis=-1)
    sin = np.concatenate([-np.sin(ang_r), np.sin(ang_r), -np.sin(ang_c), np.sin(ang_c)], axis=-1)
    return jnp.asarray(cos, F32), jnp.asarray(sin, F32)


def _swap32(x):
    lane = lax.broadcasted_iota(jnp.int32, x.shape, 1)
    return jnp.where((lane % 64) < 32, pltpu.roll(x, 96, 1), pltpu.roll(x, 32, 1))


def _qk_prep(proj, cos, sin, g_q, g_k, n_norm, tr=256):
    T, W = proj.shape
    tr = _tile(T, tr)
    n_q = n_norm * GROUP // (GROUP + 1)

    def body(p_ref, c_ref, s_ref, gq_ref, gk_ref, o_ref):
        c, s = c_ref[...], s_ref[...]
        for h in range(n_norm):
            cols = slice(h * HEAD_DIM, (h + 1) * HEAD_DIM)
            xv = p_ref[:, cols]
            g = gq_ref[...] if h < n_q else gk_ref[...]
            xn = xv * lax.rsqrt(_mean_last(xv * xv) + EPS) * g
            o_ref[:, cols] = (xn * c + _swap32(xn) * s).astype(BF16)
        rest = slice(n_norm * HEAD_DIM, W)
        o_ref[:, rest] = p_ref[:, rest].astype(BF16)

    row = pl.BlockSpec((tr, W), lambda i: (i, 0))
    tab = pl.BlockSpec((tr, HEAD_DIM), lambda i: (i, 0))
    vec = pl.BlockSpec((1, HEAD_DIM), lambda i: (0, 0))
    return pl.pallas_call(
        body, name="qk_prep", grid=(T // tr,),
        in_specs=[row, tab, tab, vec, vec], out_specs=row,
        out_shape=jax.ShapeDtypeStruct((T, W), BF16),
        compiler_params=_params(("parallel",)),
    )(proj, cos, sin, g_q, g_k)


def _dproj(proj, dqa, dka, dva, dqb, dkb, dvb, cos, sin, g_q, g_k, tr=256):
    T, W = proj.shape
    tr = _tile(T, tr)
    n_q = dqa.shape[1] // HEAD_DIM
    n_kv = dka.shape[1] // HEAD_DIM
    wa = (n_q + n_kv) * HEAD_DIM

    def body(p_ref, dqa_ref, dka_ref, dva_ref, dqb_ref, dkb_ref, dvb_ref, c_ref, s_ref, gq_ref, gk_ref,
             o_ref, dgq_ref, dgk_ref):
        c, s = c_ref[...], s_ref[...]
        dgq = jnp.zeros((SUBLANES, HEAD_DIM), F32)
        dgk = jnp.zeros((SUBLANES, HEAD_DIM), F32)
        for h in range(n_q + n_kv):
            cols = slice(h * HEAD_DIM, (h + 1) * HEAD_DIM)
            xv = p_ref[:, cols]
            r = lax.rsqrt(_mean_last(xv * xv) + EPS)
            xn = xv * r
            if h < n_q:
                d = dqa_ref[:, cols]
                g = gq_ref[...]
            else:
                d = dka_ref[:, (h - n_q) * HEAD_DIM:(h - n_q + 1) * HEAD_DIM]
                g = gk_ref[...]
            dqn = d * c + _swap32(d * s)
            part = _rows_to_sublanes(dqn * xn)
            if h < n_q:
                dgq = dgq + part
            else:
                dgk = dgk + part
            dxn = dqn * g
            o_ref[:, cols] = (r * (dxn - xn * _mean_last(dxn * xn))).astype(BF16)
        off = wa
        for ref in (dva_ref, dqb_ref, dkb_ref, dvb_ref):
            w = ref.shape[1]
            o_ref[:, off:off + w] = ref[...].astype(BF16)
            off += w
        first = pl.program_id(0) == 0
        _accumulate(dgq_ref, dgq, first)
        _accumulate(dgk_ref, dgk, first)

    def row(w):
        return pl.BlockSpec((tr, w), lambda i: (i, 0))

    vec = pl.BlockSpec((1, HEAD_DIM), lambda i: (0, 0))
    part = pl.BlockSpec((SUBLANES, HEAD_DIM), lambda i: (0, 0))
    return pl.pallas_call(
        body, name="dproj", grid=(T // tr,),
        in_specs=[row(wa), row(dqa.shape[1]), row(dka.shape[1]), row(dva.shape[1]), row(dqb.shape[1]),
                  row(dkb.shape[1]), row(dvb.shape[1]), row(HEAD_DIM), row(HEAD_DIM), vec, vec],
        out_specs=[row(W), part, part],
        out_shape=[jax.ShapeDtypeStruct((T, W), BF16), jax.ShapeDtypeStruct((SUBLANES, HEAD_DIM), F32),
                   jax.ShapeDtypeStruct((SUBLANES, HEAD_DIM), F32)],
        compiler_params=_params(("arbitrary",)),
    )(proj, dqa, dka, dva, dqb, dkb, dvb, cos, sin, g_q, g_k)


def _attn_a_fwd(pb, n_q, n_kv, tq=256):
    T = pb.shape[0]
    tq = _tile(T, tq)
    scale = HEAD_DIM ** -0.5

    def body(q_ref, k_ref, v_ref, o_ref, lse_ref):
        s = lax.dot_general(q_ref[...], k_ref[...], _NT, preferred_element_type=F32) * scale
        m = jnp.max(s, axis=-1, keepdims=True)
        p = jnp.exp(s - m)
        l = jnp.sum(p, axis=-1, keepdims=True)
        o = lax.dot_general(p.astype(BF16), v_ref[...], _NN, preferred_element_type=F32)
        o_ref[...] = (o / l).astype(BF16)
        lse_ref[...] = m + jnp.log(l)

    return pl.pallas_call(
        body, name="attn_a_fwd", grid=(n_kv, GROUP, T // tq),
        in_specs=[pl.BlockSpec((tq, HEAD_DIM), lambda kv, g, i: (i, kv * GROUP + g)),
                  pl.BlockSpec((T, HEAD_DIM), lambda kv, g, i: (0, n_q + kv)),
                  pl.BlockSpec((T, HEAD_DIM), lambda kv, g, i: (0, n_q + n_kv + kv))],
        out_specs=[pl.BlockSpec((tq, HEAD_DIM), lambda kv, g, i: (i, kv * GROUP + g)),
                   pl.BlockSpec((None, tq, 1), lambda kv, g, i: (kv * GROUP + g, i, 0))],
        out_shape=[jax.ShapeDtypeStruct((T, n_q * HEAD_DIM), BF16), jax.ShapeDtypeStruct((n_q, T, 1), F32)],
        compiler_params=_params(("parallel", "parallel", "parallel")),
    )(pb, pb, pb)


def _attn_a_bwd(pb, o_cat, d_o, lse, n_q, n_kv, tq=256):
    T = pb.shape[0]
    tq = _tile(T, tq)
    scale = HEAD_DIM ** -0.5

    def body(q_ref, k_ref, v_ref, o_ref, do_ref, lse_ref, dq_ref, dk_ref, dv_ref):
        q, k, v, do = q_ref[...], k_ref[...], v_ref[...], do_ref[...]
        delta = jnp.sum(do.astype(F32) * o_ref[...].astype(F32), axis=-1, keepdims=True)
        s = lax.dot_general(q, k, _NT, preferred_element_type=F32) * scale
        p = jnp.exp(s - lse_ref[...])
        dp = lax.dot_general(do, v, _NT, preferred_element_type=F32)
        ds = (p * (dp - delta) * scale).astype(BF16)
        dq_ref[...] = lax.dot_general(ds, k, _NN, preferred_element_type=F32)
        first = jnp.logical_and(pl.program_id(1) == 0, pl.program_id(2) == 0)
        _accumulate(dv_ref, lax.dot_general(p.astype(BF16), do, _TN, preferred_element_type=F32), first)
        _accumulate(dk_ref, lax.dot_general(ds, q, _TN, preferred_element_type=F32), first)

    qmap = lambda kv, g, i: (i, kv * GROUP + g)
    return pl.pallas_call(
        body, name="attn_a_bwd", grid=(n_kv, GROUP, T // tq),
        in_specs=[pl.BlockSpec((tq, HEAD_DIM), qmap),
                  pl.BlockSpec((T, HEAD_DIM), lambda kv, g, i: (0, n_q + kv)),
                  pl.BlockSpec((T, HEAD_DIM), lambda kv, g, i: (0, n_q + n_kv + kv)),
                  pl.BlockSpec((tq, HEAD_DIM), qmap),
                  pl.BlockSpec((tq, HEAD_DIM), qmap),
                  pl.BlockSpec((None, tq, 1), lambda kv, g, i: (kv * GROUP + g, i, 0))],
        out_specs=[pl.BlockSpec((tq, HEAD_DIM), qmap),
                   pl.BlockSpec((T, HEAD_DIM), lambda kv, g, i: (0, kv)),
                   pl.BlockSpec((T, HEAD_DIM), lambda kv, g, i: (0, kv))],
        out_shape=[jax.ShapeDtypeStruct((T, n_q * HEAD_DIM), F32),
                   jax.ShapeDtypeStruct((T, n_kv * HEAD_DIM), F32),
                   jax.ShapeDtypeStruct((T, n_kv * HEAD_DIM), F32)],
        compiler_params=_params(("parallel", "arbitrary", "arbitrary")),
    )(pb, pb, pb, o_cat, d_o, lse)


def _bucket_index():
    r = np.arange(BLOCK_Q)[:, None]
    j = np.arange(3 * BLOCK_Q)[None, :]
    rel = (j - BLOCK_Q) - r
    nb = N_BUCKETS // 2
    ret = np.where(rel > 0, nb, 0)
    n = np.abs(rel)
    max_exact = nb // 2
    nf = np.maximum(n, 1).astype(np.float32)
    large = max_exact + (np.log(nf / max_exact) / math.log(MAX_DISTANCE / max_exact) * (nb - max_exact)).astype(np.int32)
    large = np.minimum(large, nb - 1)
    return jnp.asarray(ret + np.where(n < max_exact, n, large), jnp.int32)


def _bias_build(idx, table_flat, n_heads, deps=()):
    def body(idx_ref, tab_ref, o_ref):
        h = pl.program_id(0)
        iv = idx_ref[...]
        acc = jnp.zeros(iv.shape, F32)
        for b in range(N_BUCKETS):
            acc = jnp.where(iv == b, tab_ref[b * n_heads + h], acc)
        o_ref[...] = acc

    return _pcall(
        body, deps, name="bias_build", grid=(n_heads,),
        in_specs=[pl.BlockSpec(idx.shape, lambda h: (0, 0)), pl.BlockSpec(memory_space=pltpu.SMEM)],
        out_specs=pl.BlockSpec((None,) + idx.shape, lambda h: (h, 0, 0)),
        out_shape=jax.ShapeDtypeStruct((n_heads,) + idx.shape, F32),
        compiler_params=_params(("parallel",)),
    )(idx, table_flat)


def _band_mask(n, T):
    r = lax.broadcasted_iota(jnp.int32, (BLOCK_Q, 3 * BLOCK_Q), 0)
    j = lax.broadcasted_iota(jnp.int32, (BLOCK_Q, 3 * BLOCK_Q), 1)
    rel = (j - BLOCK_Q) - r
    kabs = n * BLOCK_Q + j - BLOCK_Q
    return (jnp.abs(rel) <= WINDOW) & (kabs >= 0) & (kabs < T)


def _band_specs(col, nblk):
    return [pl.BlockSpec((BLOCK_Q, HEAD_DIM), lambda kv, n: (jnp.maximum(n - 1, 0), col(kv))),
            pl.BlockSpec((BLOCK_Q, HEAD_DIM), lambda kv, n: (n, col(kv))),
            pl.BlockSpec((BLOCK_Q, HEAD_DIM), lambda kv, n: (jnp.minimum(n + 1, nblk - 1), col(kv)))]


def _attn_b_fwd(pb, bias, sink, q_off, n_q, n_kv, deps=()):
    T = pb.shape[0]
    nblk = T // BLOCK_Q
    scale = HEAD_DIM ** -0.5

    def body(*refs):
        q_refs = refs[0:GROUP]
        k_refs, v_refs = refs[GROUP:GROUP + 3], refs[GROUP + 3:GROUP + 6]
        bias_ref, sink_ref, o_ref, lse_ref = refs[GROUP + 6:]
        kv, n = pl.program_id(0), pl.program_id(1)
        kb = jnp.concatenate([r[...] for r in k_refs], axis=0)
        vb = jnp.concatenate([r[...] for r in v_refs], axis=0)
        mask = _band_mask(n, T)
        for g in range(GROUP):
            sk = sink_ref[kv * GROUP + g]
            s = lax.dot_general(q_refs[g][...], kb, _NT, preferred_element_type=F32) * scale + bias_ref[g]
            s = jnp.where(mask, s, NEG_INF)
            m = jnp.maximum(jnp.max(s, axis=-1, keepdims=True), sk)
            p = jnp.exp(s - m)
            l = jnp.sum(p, axis=-1, keepdims=True) + jnp.exp(sk - m)
            o = lax.dot_general(p.astype(BF16), vb, _NN, preferred_element_type=F32)
            o_ref[:, g * HEAD_DIM:(g + 1) * HEAD_DIM] = (o / l).astype(BF16)
            lse_ref[g] = m + jnp.log(l)

    q_specs = [pl.BlockSpec((BLOCK_Q, HEAD_DIM), functools.partial(lambda kv, n, g: (n, q_off + kv * GROUP + g), g=g))
               for g in range(GROUP)]
    return _pcall(
        body, deps, name="attn_b_fwd", grid=(n_kv, nblk),
        in_specs=[*q_specs,
                  *_band_specs(lambda kv: q_off + n_q + kv, nblk),
                  *_band_specs(lambda kv: q_off + n_q + n_kv + kv, nblk),
                  pl.BlockSpec((GROUP, BLOCK_Q, 3 * BLOCK_Q), lambda kv, n: (kv, 0, 0)),
                  pl.BlockSpec(memory_space=pltpu.SMEM)],
        out_specs=[pl.BlockSpec((BLOCK_Q, GROUP * HEAD_DIM), lambda kv, n: (n, kv)),
                   pl.BlockSpec((GROUP, BLOCK_Q, 1), lambda kv, n: (kv, n, 0))],
        out_shape=[jax.ShapeDtypeStruct((T, n_q * HEAD_DIM), BF16), jax.ShapeDtypeStruct((n_q, T, 1), F32)],
        compiler_params=_params(("parallel", "parallel")),
    )(*([pb] * (GROUP + 6)), bias, sink)


def _attn_b_bwd(pb, o_cat, d_o, lse, bias, sink, q_off, n_q, n_kv, o_off, deps=()):
    T = pb.shape[0]
    nblk = T // BLOCK_Q
    scale = HEAD_DIM ** -0.5

    def body(*refs):
        q_refs = refs[0:GROUP]
        k_refs, v_refs = refs[GROUP:GROUP + 3], refs[GROUP + 3:GROUP + 6]
        o_refs, do_refs = refs[GROUP + 6:2 * GROUP + 6], refs[2 * GROUP + 6:3 * GROUP + 6]
        lse_ref, bias_ref, sink_ref, dq_ref, dk_ref, dv_ref, dbias_ref, dsink_ref = refs[3 * GROUP + 6:]
        kv, n = pl.program_id(0), pl.program_id(1)
        first = n == 0
        kb = jnp.concatenate([r[...] for r in k_refs], axis=0)
        vb = jnp.concatenate([r[...] for r in v_refs], axis=0)
        mask = _band_mask(n, T)
        dkb = jnp.zeros((3 * BLOCK_Q, HEAD_DIM), F32)
        dvb = jnp.zeros((3 * BLOCK_Q, HEAD_DIM), F32)
        row = lax.broadcasted_iota(jnp.int32, (SUBLANES, LANES), 0)
        dsink = jnp.zeros((SUBLANES, LANES), F32)
        for g in range(GROUP):
            sk = sink_ref[kv * GROUP + g]
            q, do = q_refs[g][...], do_refs[g][...]
            lse_g = lse_ref[g]
            delta = jnp.sum(do.astype(F32) * o_refs[g][...].astype(F32), axis=-1, keepdims=True)
            s = lax.dot_general(q, kb, _NT, preferred_element_type=F32) * scale + bias_ref[g]
            s = jnp.where(mask, s, NEG_INF)
            p = jnp.exp(s - lse_g)
            dp = lax.dot_general(do, vb, _NT, preferred_element_type=F32)
            ds = p * (dp - delta)
            _accumulate(dbias_ref.at[g], ds, first)
            dsink = dsink + jnp.where(row == g, -jnp.sum(jnp.exp(sk - lse_g) * delta), 0.0)
            dsb = (ds * scale).astype(BF16)
            dq_ref[:, g * HEAD_DIM:(g + 1) * HEAD_DIM] = lax.dot_general(dsb, kb, _NN, preferred_element_type=F32)
            dkb = dkb + lax.dot_general(dsb, q, _TN, preferred_element_type=F32)
            dvb = dvb + lax.dot_general(p.astype(BF16), do, _TN, preferred_element_type=F32)
        _accumulate(dsink_ref, dsink, first)

        @pl.when(first)
        def _():
            dk_ref[...] = jnp.zeros(dk_ref.shape, F32)
            dv_ref[...] = jnp.zeros(dv_ref.shape, F32)

        blocks = (jnp.maximum(n - 1, 0), n, jnp.minimum(n + 1, nblk - 1))
        for t, blk in enumerate(blocks):
            rows = pl.ds(pl.multiple_of(blk * BLOCK_Q, BLOCK_Q), BLOCK_Q)
            dk_ref[rows, :] += dkb[t * BLOCK_Q:(t + 1) * BLOCK_Q]
            dv_ref[rows, :] += dvb[t * BLOCK_Q:(t + 1) * BLOCK_Q]

    def head_specs(base):
        return [pl.BlockSpec((BLOCK_Q, HEAD_DIM), functools.partial(lambda kv, n, g: (n, base + kv * GROUP + g), g=g))
                for g in range(GROUP)]

    return _pcall(
        body, deps, name="attn_b_bwd", grid=(n_kv, nblk),
        in_specs=[*head_specs(q_off),
                  *_band_specs(lambda kv: q_off + n_q + kv, nblk),
                  *_band_specs(lambda kv: q_off + n_q + n_kv + kv, nblk),
                  *head_specs(o_off), *head_specs(o_off),
                  pl.BlockSpec((GROUP, BLOCK_Q, 1), lambda kv, n: (kv, n, 0)),
                  pl.BlockSpec((GROUP, BLOCK_Q, 3 * BLOCK_Q), lambda kv, n: (kv, 0, 0)),
                  pl.BlockSpec(memory_space=pltpu.SMEM)],
        out_specs=[pl.BlockSpec((BLOCK_Q, GROUP * HEAD_DIM), lambda kv, n: (n, kv)),
                   pl.BlockSpec((T, HEAD_DIM), lambda kv, n: (0, kv)),
                   pl.BlockSpec((T, HEAD_DIM), lambda kv, n: (0, kv)),
                   pl.BlockSpec((GROUP, BLOCK_Q, 3 * BLOCK_Q), lambda kv, n: (kv, 0, 0)),
                   pl.BlockSpec((None, SUBLANES, LANES), lambda kv, n: (kv, 0, 0))],
        out_shape=[jax.ShapeDtypeStruct((T, n_q * HEAD_DIM), F32),
                   jax.ShapeDtypeStruct((T, n_kv * HEAD_DIM), F32),
                   jax.ShapeDtypeStruct((T, n_kv * HEAD_DIM), F32),
                   jax.ShapeDtypeStruct((n_q, BLOCK_Q, 3 * BLOCK_Q), F32),
                   jax.ShapeDtypeStruct((n_kv, SUBLANES, LANES), F32)],
        compiler_params=_params(("parallel", "arbitrary")),
    )(*([pb] * (GROUP + 6)), *([o_cat] * GROUP), *([d_o] * GROUP), lse, bias, sink)


def _table_grads(dbias, dsink_raw, idx):
    n_heads = dbias.shape[0]
    n_kv = dsink_raw.shape[0]

    def body(db_ref, ds_ref, idx_ref, dt_ref, dsk_ref):
        iv = idx_ref[...]
        row = lax.broadcasted_iota(jnp.int32, (SUBLANES, LANES), 0)
        lane = lax.broadcasted_iota(jnp.int32, (SUBLANES, LANES), 1)
        dsk = jnp.zeros((SUBLANES, LANES), F32)
        for h in range(n_heads):
            d = db_ref[h]
            acc = jnp.zeros((SUBLANES, LANES), F32)
            for b in range(N_BUCKETS):
                acc = jnp.where((row == 0) & (lane == b), jnp.sum(jnp.where(iv == b, d, 0.0)), acc)
            dt_ref[:, h * LANES:(h + 1) * LANES] = acc
            raw = ds_ref[h // GROUP]
            val = jnp.sum(jnp.where((row == h % GROUP) & (lane == 0), raw, 0.0))
            dsk = jnp.where((row == 0) & (lane == h), val, dsk)
        dsk_ref[...] = dsk

    return pl.pallas_call(
        body, name="table_grads",
        in_specs=[pl.BlockSpec(memory_space=pltpu.VMEM)] * 3,
        out_specs=[pl.BlockSpec(memory_space=pltpu.VMEM)] * 2,
        out_shape=[jax.ShapeDtypeStruct((SUBLANES, n_heads * LANES), F32),
                   jax.ShapeDtypeStruct((SUBLANES, LANES), F32)],
        compiler_params=pltpu.CompilerParams(vmem_limit_bytes=56 * 1024 * 1024),
    )(dbias, dsink_raw, idx)


def _position():
    x, y, c = lax.axis_index("x"), lax.axis_index("y"), lax.axis_index("c")
    return x, y, c


def _hbm(a):
    return pltpu.with_memory_space_constraint(a, pltpu.HBM)


def _split_start(name, bufs, sem_shapes, issue, n_local=1):
    nb, ns = len(bufs), len(sem_shapes)

    def body(*refs):
        buf_refs = refs[:nb]
        sems = refs[nb:nb + ns]
        token = refs[nb + ns + nb]
        issue(buf_refs, sems, refs[-1])
        token[...] = jnp.zeros(token.shape, F32)

    outs = pl.pallas_call(
        body, name=name,
        in_specs=[_HBM] * nb,
        out_specs=[_SEM] * ns + [_HBM] * nb + [_VMEM],
        out_shape=[pltpu.SemaphoreType.DMA(s) for s in sem_shapes] + [pltpu.HBM(b.shape, b.dtype) for b in bufs]
        + [jax.ShapeDtypeStruct((SUBLANES, LANES), F32)],
        input_output_aliases={i: ns + i for i in range(nb)},
        scratch_shapes=[pltpu.SemaphoreType.DMA((n_local,))],
        compiler_params=pltpu.CompilerParams(has_side_effects=_EFFECT),
    )(*[_hbm(b) for b in bufs])
    return outs[:ns], outs[ns:ns + nb], outs[-1]


def _split_wait(name, bufs, send, recv, counts, size_of, after):
    nb = len(bufs)

    def body(*refs):
        buf_refs = refs[:nb]
        send_ref, recv_ref = refs[nb], refs[nb + 1]
        x, y, c = _position()
        for w, n in enumerate(counts):
            ref = size_of(buf_refs, w)
            for k in range(n):
                s = sum(counts[:w]) + k
                cp = pltpu.make_async_remote_copy(
                    src_ref=ref, dst_ref=ref, send_sem=send_ref.at[s], recv_sem=recv_ref.at[s],
                    device_id=(x, y, c), device_id_type=MESH)
                cp.wait_send()
                cp.wait_recv()

    return pl.pallas_call(
        body, name=name,
        in_specs=[_HBM] * nb + [_SEM, _SEM, _ANY],
        out_specs=[_HBM] * nb,
        out_shape=[pltpu.HBM(b.shape, b.dtype) for b in bufs],
        input_output_aliases={i: i for i in range(nb)},
        compiler_params=pltpu.CompilerParams(has_side_effects=_EFFECT),
    )(*bufs, send, recv, after)


def _block_of(pos):
    return 4 * pos[0] + 2 * pos[1] + pos[2]


def _gather_start(shards, groups):
    nw = len(shards)
    lands = [lax.empty((N_DEV,) + s.shape, s.dtype) for s in shards]

    def issue(bufs, sems, local_sems):
        ins, land = bufs[:nw], bufs[nw:]
        x, y, c = _position()
        me = (x, y, c)
        peers = [(x, y, 1 - c), (1 - x, y, c), (x, 1 - y, c), (1 - x, 1 - y, c)]
        local = []
        for gi, grp in enumerate(groups):
            for wi, w in enumerate(grp):
                dst = land[w].at[_block_of(me)]
                cp = pltpu.make_async_copy(ins[w], dst, local_sems.at[w])
                cp.start()
                local.append(cp)
                for k, peer in enumerate(peers):
                    pltpu.make_async_remote_copy(
                        src_ref=ins[w], dst_ref=dst, send_sem=sems[2 * gi].at[4 * wi + k],
                        recv_sem=sems[2 * gi + 1].at[4 * wi + k], device_id=peer, device_id_type=MESH).start()
        for cp in local:
            cp.wait()

    sem_shapes = [(4 * len(g),) for g in groups for _ in range(2)]
    sems, thru, token = _split_start("gather_start", list(shards) + lands, sem_shapes, issue, n_local=nw)
    return sems, thru[:nw], thru[nw:], token


def _gather_forward(name, lands):
    nw = len(lands)

    def issue(land, sems, local_sems):
        x, y, c = _position()
        for w in range(nw):
            for k, chip in enumerate([(1 - x, y), (x, 1 - y), (1 - x, 1 - y)]):
                blk = land[w].at[_block_of((*chip, c))]
                pltpu.make_async_remote_copy(
                    src_ref=blk, dst_ref=blk, send_sem=sems[0].at[3 * w + k], recv_sem=sems[1].at[3 * w + k],
                    device_id=(x, y, 1 - c), device_id_type=MESH).start()

    return _split_start(name, lands, [(3 * nw,), (3 * nw,)], issue)


def _first_block(bufs, w, offset=0):
    return bufs[offset + w].at[0]


def _pair_start(name, grads):
    nw = len(grads)
    lands = [lax.empty((N_CHIP,) + g.shape[1:], g.dtype) for g in grads]

    def issue(bufs, sems, local_sems):
        x, y, c = _position()
        for w in range(nw):
            for q in range(N_CHIP):
                pltpu.make_async_remote_copy(
                    src_ref=bufs[w].at[2 * q + 1 - c], dst_ref=bufs[nw + w].at[q], send_sem=sems[0].at[N_CHIP * w + q],
                    recv_sem=sems[1].at[N_CHIP * w + q], device_id=(x, y, 1 - c), device_id_type=MESH).start()

    return _split_start(name, list(grads) + lands, [(N_CHIP * nw,), (N_CHIP * nw,)], issue)


def _chip_start(name, sums):
    nw = len(sums)
    lands = [lax.empty(s.shape, s.dtype) for s in sums]

    def issue(bufs, sems, local_sems):
        x, y, c = _position()
        my_chip = 2 * x + y
        local = []
        for w in range(nw):
            cp = pltpu.make_async_copy(bufs[w].at[my_chip], bufs[nw + w].at[my_chip], local_sems.at[w])
            cp.start()
            local.append(cp)
            for k, (px, py) in enumerate([(1 - x, y), (x, 1 - y), (1 - x, 1 - y)]):
                pltpu.make_async_remote_copy(
                    src_ref=bufs[w].at[2 * px + py], dst_ref=bufs[nw + w].at[my_chip], send_sem=sems[0].at[3 * w + k],
                    recv_sem=sems[1].at[3 * w + k], device_id=(px, py, c), device_id_type=MESH).start()
        for cp in local:
            cp.wait()

    return _split_start(name, list(sums) + lands, [(3 * nw,), (3 * nw,)], issue, n_local=nw)


def _pair_sum(name, grad, landed, tr=256):
    _, R, C = grad.shape
    tr = _tile(R, tr)
    core = lax.axis_index("c").astype(jnp.int32).reshape(1)

    def body(c_ref, g_ref, l_ref, o_ref):
        o_ref[...] = (g_ref[...] + l_ref[...]).astype(BF16)

    slot = pl.BlockSpec((None, tr, C), lambda q, i, c_ref: (q, i, 0))
    return pl.pallas_call(
        body, name=name,
        grid_spec=pltpu.PrefetchScalarGridSpec(
            num_scalar_prefetch=1, grid=(N_CHIP, R // tr),
            in_specs=[pl.BlockSpec((None, tr, C), lambda q, i, c_ref: (2 * q + c_ref[0], i, 0)), slot],
            out_specs=slot),
        out_shape=jax.ShapeDtypeStruct((N_CHIP, R, C), BF16),
        compiler_params=_params(("parallel", "parallel")),
    )(core, grad, landed)


def _adam(w, g, m, v):
    m = ADAM_B1 * m + (1.0 - ADAM_B1) * g
    v = ADAM_B2 * v + (1.0 - ADAM_B2) * (g * g)
    m_hat = m / (1.0 - ADAM_B1 ** ADAM_STEP)
    v_hat = v / (1.0 - ADAM_B2 ** ADAM_STEP)
    delta = -ADAM_LR * (m_hat / (jnp.sqrt(v_hat) + ADAM_EPS) + ADAM_WD * w)
    return delta, m, v


def _sum_adam(name, landed, w, m, v, tr=256):
    R, C = w.shape
    tr = _tile(R, tr)

    def body(l_ref, w_ref, m_ref, v_ref, g_ref, d_ref, nm_ref, nv_ref):
        g = l_ref[0].astype(F32)
        for q in range(1, N_CHIP):
            g = g + l_ref[q].astype(F32)
        g_ref[...] = g
        d_ref[...], nm_ref[...], nv_ref[...] = _adam(w_ref[...], g, m_ref[...], v_ref[...])

    tile = pl.BlockSpec((tr, C), lambda i: (i, 0))
    return pl.pallas_call(
        body, name=name, grid=(R // tr,),
        in_specs=[pl.BlockSpec((N_CHIP, tr, C), lambda i: (0, i, 0)), tile, tile, tile],
        out_specs=[tile] * 4, out_shape=[jax.ShapeDtypeStruct((R, C), F32)] * 4,
        compiler_params=_params(("parallel",)),
    )(landed, w, m, v)


def _small_all_reduce(parts):
    W = parts.shape[1]

    def body(p_ref, o_ref, slots, send_sems, recv_sems):
        x, y, c = _position()
        me = 4 * x + 2 * y + c
        slots[me] = jnp.sum(p_ref[...], axis=0, keepdims=True)
        peers = [(x, y, 1 - c), (1 - x, y, c), (1 - x, y, 1 - c), (x, 1 - y, c), (x, 1 - y, 1 - c),
                 (1 - x, 1 - y, c), (1 - x, 1 - y, 1 - c)]
        copies = []
        for k, peer in enumerate(peers):
            cp = pltpu.make_async_remote_copy(
                src_ref=slots.at[me], dst_ref=slots.at[me], send_sem=send_sems.at[k], recv_sem=recv_sems.at[k],
                device_id=peer, device_id_type=MESH)
            cp.start()
            copies.append(cp)
        for cp in copies:
            cp.wait()
        total = slots[0]
        for d in range(1, N_DEV):
            total = total + slots[d]
        o_ref[...] = total

    return pl.pallas_call(
        body, name="small_all_reduce",
        in_specs=[pl.BlockSpec(memory_space=pltpu.VMEM)], out_specs=pl.BlockSpec(memory_space=pltpu.VMEM),
        out_shape=jax.ShapeDtypeStruct((1, W), F32),
        scratch_shapes=[pltpu.VMEM((N_DEV, 1, W), F32), pltpu.SemaphoreType.DMA((7,)), pltpu.SemaphoreType.DMA((7,))],
    )(parts)


def _adam_small(w, g, m, v):
    def body(w_ref, g_ref, m_ref, v_ref, d_ref, nm_ref, nv_ref):
        d_ref[...], nm_ref[...], nv_ref[...] = _adam(w_ref[...], g_ref[...], m_ref[...], v_ref[...])

    return pl.pallas_call(
        body, name="adam_small",
        in_specs=[pl.BlockSpec(memory_space=pltpu.VMEM)] * 4, out_specs=[pl.BlockSpec(memory_space=pltpu.VMEM)] * 3,
        out_shape=[jax.ShapeDtypeStruct(w.shape, F32)] * 3,
    )(w, g, m, v)


_GATHER_GROUPS = (("w_in",), ("w_out", "w_up", "ple_w"), ("w_down", "w_gate"))
_ROW_SHARDED = ("w_out", "w_down", "w_gate")


class _MeshComm:
    def __init__(self, w, mom, var):
        self.w, self.mom, self.var = w, mom, var
        self.out = {}
        self._pairs, self._chips = {}, {}

    def gather_begin(self):
        names = [n for g in _GATHER_GROUPS for n in g]
        self._idx = {n: i for i, n in enumerate(names)}
        groups = [[self._idx[n] for n in g] for g in _GATHER_GROUPS]
        self._sems, self._src, self._lands, token = _gather_start([self.w[n].astype(BF16) for n in names], groups)
        return token

    def gather_arrive(self, gi, after):
        ids = [self._idx[n] for n in _GATHER_GROUPS[gi]]
        bufs = [self._src[i] for i in ids] + [self._lands[i] for i in ids]
        out = _split_wait("gather_arrive%d" % gi, bufs, self._sems[2 * gi], self._sems[2 * gi + 1], [4] * len(ids),
                          functools.partial(_first_block, offset=len(ids)), after)
        self._arrived = out[len(ids):]

    def gather_forward(self, gi):
        self._fsems, self._fthru, token = _gather_forward("gather_forward%d" % gi, self._arrived)
        return token

    def gather_finish(self, gi, after):
        names = _GATHER_GROUPS[gi]
        out = _split_wait("gather_finish%d" % gi, self._fthru, self._fsems[0], self._fsems[1], [3] * len(names),
                          _first_block, after)
        return {n: a.reshape(-1, a.shape[-1]) if n in _ROW_SHARDED else a for n, a in zip(names, out)}

    def reduce_begin(self, key, grads):
        names = list(grads)
        sems, thru, token = _pair_start("pair_start_" + key, [grads[n] for n in names])
        self._pairs[key] = (names, sems, thru)
        return token

    def reduce_middle(self, key, after):
        names, sems, thru = self._pairs[key]
        nw = len(names)
        out = _split_wait("pair_wait_" + key, thru, sems[0], sems[1], [N_CHIP] * nw,
                          functools.partial(_first_block, offset=nw), after)
        sums = [_pair_sum("pair_sum_" + n, out[i], out[nw + i]) for i, n in enumerate(names)]
        sems2, thru2, token = _chip_start("chip_start_" + key, sums)
        self._chips[key] = (names, sems2, thru2)
        return token

    def reduce_finish(self, key, after):
        names, sems, thru = self._chips[key]
        nw = len(names)
        out = _split_wait("chip_wait_" + key, thru, sems[0], sems[1], [3] * nw,
                          functools.partial(_first_block, offset=nw), after)
        for i, n in enumerate(names):
            self.out[n] = _sum_adam("adam_" + n, out[nw + i], self.w[n], self.mom[n], self.var[n])


def _step(x, p, target, gains, comm):
    T, D = x.shape
    n_q = D // (2 * HEAD_DIM)
    n_kv = n_q // GROUP
    cos, sin = _rope_tables(T)
    idx = _bucket_index()
    rows = lambda g: g.reshape((N_DEV, g.shape[0] // N_DEV, g.shape[1]))

    t = comm.gather_begin()
    u = _rms_fwd("norm_attn", x, gains["attn_norm_g"], deps=(t,))
    comm.gather_arrive(0, u)
    t = comm.gather_forward(0)
    bias = _bias_build(idx, gains["rel_bias_table"].reshape(-1), n_q, deps=(t,))
    full = comm.gather_finish(0, bias)
    proj = _mm_nn("in_proj", u, full["w_in"])
    pb = _qk_prep(proj, cos, sin, gains["q_norm_g"], gains["k_norm_g"], n_q + n_kv)
    o_a, lse_a = _attn_a_fwd(pb, n_q, n_kv)
    comm.gather_arrive(1, o_a)
    t = comm.gather_forward(1)
    sink = gains["sink_logits"].reshape(-1)
    b_off = n_q + 2 * n_kv
    o_b, lse_b = _attn_b_fwd(pb, bias, sink, b_off, n_q, n_kv, deps=(t,))
    full.update(comm.gather_finish(1, o_b))
    o_cat = jnp.concatenate([o_a, o_b], axis=1)
    h1 = _mm_nn("out_proj", o_cat, full["w_out"], epilogue=_store_add, extras=(x,))
    m_in = _rms_fwd("norm_mlp", h1, gains["mlp_norm_g"])

    def up_epilogue(acc, extra, outs):
        outs[0][...] = acc.astype(BF16)
        r = jnp.maximum(acc, 0.0)
        outs[1][...] = (r * r).astype(BF16)

    a_act, f_act = _mm_nn("up_proj", m_in, full["w_up"], epilogue=up_epilogue, out_dtypes=[BF16, BF16])
    comm.gather_arrive(2, f_act)
    t = comm.gather_forward(2)
    p_b = p.astype(BF16)
    pe = _mm_nn("ple_proj", p_b, full["ple_w"], deps=(t,))
    full.update(comm.gather_finish(2, pe))
    h2 = _mm_nn("down_proj", f_act, full["w_down"], epilogue=_store_add, extras=(h1,), tn=1024, tk=1024)
    gn = _rms_fwd("norm_gate", h2, gains["gate_norm_g"])
    z = _mm_nn("gate_proj", gn, full["w_gate"])

    dh3, dz, dpe, dg_final, dg_ple, loss_part = _tail(h2, z, pe, target, gains["ple_norm_g"], gains["final_norm_g"])
    gw_gate = _mm_tn("grad_w_gate", gn, dz)
    gw_ple = _mm_tn("grad_ple_w", p_b, dpe, blocked_n=full["ple_w"].shape[2])
    t = comm.reduce_begin("a", dict(w_gate=rows(gw_gate), ple_w=gw_ple))
    dgn = _mm_nt("d_gate_in", dz, full["w_gate"], deps=(t,))
    dh2, dh2_b, dg_gate = _rms_bwd("norm_gate_bwd", dgn, h2, gains["gate_norm_g"], dh3)
    t = comm.reduce_middle("a", dh2_b)
    gw_down = _mm_tn("grad_w_down", f_act, dh2_b, deps=(t,))
    t = comm.reduce_begin("b", dict(w_down=rows(gw_down)))

    def act_bwd(acc, extra, outs):
        outs[0][...] = (acc * (2.0 * jnp.maximum(extra[0][...].astype(F32), 0.0))).astype(BF16)

    da = _mm_nt("d_act", dh2_b, full["w_down"], out_dtype=BF16, epilogue=act_bwd, extras=(a_act,), tn=1024, deps=(t,))
    comm.reduce_finish("a", da)
    t = comm.reduce_middle("b", da)
    gw_up = _mm_tn("grad_w_up", m_in, da, blocked_n=full["w_up"].shape[2], deps=(t,))
    t = comm.reduce_begin("c", dict(w_up=gw_up))
    dm = _mm_nt("d_mlp_in", da, full["w_up"], deps=(t,))
    t = comm.reduce_middle("c", dm)
    dh1, dh1_b, dg_mlp = _rms_bwd("norm_mlp_bwd", dm, h1, gains["mlp_norm_g"], dh2, deps=(t,))
    comm.reduce_finish("b", dh1_b)
    gw_out = _mm_tn("grad_w_out", o_cat, dh1_b)
    t = comm.reduce_begin("d", dict(w_out=rows(gw_out)))
    d_o = _mm_nt("d_attn_out", dh1_b, full["w_out"], out_dtype=BF16, deps=(t,))
    dqa, dka, dva = _attn_a_bwd(pb, o_cat, d_o, lse_a, n_q, n_kv)
    t = comm.reduce_middle("d", dqa)
    dqb, dkb, dvb, dbias, dsink_raw = _attn_b_bwd(pb, o_cat, d_o, lse_b, bias, sink, b_off, n_q, n_kv, n_q, deps=(t,))
    comm.reduce_finish("c", dqb)
    comm.reduce_finish("d", dqb)
    dtable, dsink = _table_grads(dbias, dsink_raw, idx)
    dproj, dg_q, dg_k = _dproj(proj, dqa, dka, dva, dqb, dkb, dvb, cos, sin, gains["q_norm_g"], gains["k_norm_g"])
    gw_in = _mm_tn("grad_w_in", u, dproj, blocked_n=full["w_in"].shape[2])
    t = comm.reduce_begin("e", dict(w_in=gw_in))
    du = _mm_nt("d_attn_in", dproj, full["w_in"], deps=(t,))
    t = comm.reduce_middle("e", du)
    dx, _, dg_attn = _rms_bwd("norm_attn_bwd", du, x, gains["attn_norm_g"], dh1, deps=(t,))
    comm.reduce_finish("e", dx)

    parts = jnp.concatenate([dg_attn, dg_mlp, dg_ple, dg_gate, dg_final, dg_q, dg_k, dtable, dsink, loss_part], axis=1)
    return dx, parts


_SHARDED = ("w_in", "w_out", "w_up", "w_down", "ple_w", "w_gate")
_VECTORS = ("attn_norm_g", "mlp_norm_g", "ple_norm_g", "gate_norm_g", "final_norm_g")
_ORDER = ("attn_norm_g", "w_in", "q_norm_g", "k_norm_g", "sink_logits", "w_out", "mlp_norm_g", "w_up", "w_down",
          "ple_w", "ple_norm_g", "gate_norm_g", "w_gate", "rel_bias_table", "final_norm_g")


def _pack_small(vals, n_heads):
    lane_pad = lambda v: jnp.pad(v, ((0, 0), (0, LANES - v.shape[1])))
    table = lane_pad(vals["rel_bias_table"].T).reshape(1, n_heads * LANES)
    return jnp.concatenate(
        [vals[n].reshape(1, -1) for n in _VECTORS] + [vals["q_norm_g"], vals["k_norm_g"], table,
                                                      lane_pad(vals["sink_logits"]), jnp.zeros((1, LANES), F32)], axis=1)


def _unpack_small(row, like, n_heads):
    out, off = {}, 0
    for n in _VECTORS:
        out[n] = row[:, off:off + like[n].size].reshape(like[n].shape)
        off += like[n].size
    for n in ("q_norm_g", "k_norm_g"):
        out[n] = row[:, off:off + LANES]
        off += LANES
    out["rel_bias_table"] = row[:, off:off + n_heads * LANES].reshape(n_heads, LANES)[:, :N_BUCKETS].T
    off += n_heads * LANES
    out["sink_logits"] = row[:, off:off + n_heads]
    off += LANES
    return out, row[0, off]


def kernel(x, p, attn_norm_g, w_in, q_norm_g, k_norm_g, sink_logits, w_out, mlp_norm_g, w_up, w_down, ple_w, ple_norm_g, gate_norm_g, w_gate, rel_bias_table, final_norm_g, loss_target, m_attn_norm_g, m_w_in, m_q_norm_g, m_k_norm_g, m_sink_logits, m_w_out, m_mlp_norm_g, m_w_up, m_w_down, m_ple_w, m_ple_norm_g, m_gate_norm_g, m_w_gate, m_rel_bias_table, m_final_norm_g, v_attn_norm_g, v_w_in, v_q_norm_g, v_k_norm_g, v_sink_logits, v_w_out, v_mlp_norm_g, v_w_up, v_w_down, v_ple_w, v_ple_norm_g, v_gate_norm_g, v_w_gate, v_rel_bias_table, v_final_norm_g):
    w = dict(attn_norm_g=attn_norm_g, w_in=w_in[0], q_norm_g=q_norm_g, k_norm_g=k_norm_g, sink_logits=sink_logits,
             w_out=w_out[0], mlp_norm_g=mlp_norm_g, w_up=w_up[0], w_down=w_down[0], ple_w=ple_w[0],
             ple_norm_g=ple_norm_g, gate_norm_g=gate_norm_g, w_gate=w_gate[0], rel_bias_table=rel_bias_table,
             final_norm_g=final_norm_g)
    mom = dict(attn_norm_g=m_attn_norm_g, w_in=m_w_in[0], q_norm_g=m_q_norm_g, k_norm_g=m_k_norm_g,
               sink_logits=m_sink_logits, w_out=m_w_out[0], mlp_norm_g=m_mlp_norm_g, w_up=m_w_up[0],
               w_down=m_w_down[0], ple_w=m_ple_w[0], ple_norm_g=m_ple_norm_g, gate_norm_g=m_gate_norm_g,
               w_gate=m_w_gate[0], rel_bias_table=m_rel_bias_table, final_norm_g=m_final_norm_g)
    var = dict(attn_norm_g=v_attn_norm_g, w_in=v_w_in[0], q_norm_g=v_q_norm_g, k_norm_g=v_k_norm_g,
               sink_logits=v_sink_logits, w_out=v_w_out[0], mlp_norm_g=v_mlp_norm_g, w_up=v_w_up[0],
               w_down=v_w_down[0], ple_w=v_ple_w[0], ple_norm_g=v_ple_norm_g, gate_norm_g=v_gate_norm_g,
               w_gate=v_w_gate[0], rel_bias_table=v_rel_bias_table, final_norm_g=v_final_norm_g)
    D = x.shape[-1]
    n_heads = D // (2 * HEAD_DIM)

    gains = {n: w[n] for n in w if n not in _SHARDED}
    gains["final_norm_g"] = final_norm_g.reshape(1, -1)

    comm = _MeshComm(w, mom, var)
    dx, parts = _step(x[0], p[0, 0], loss_target[0], gains, comm)

    g_out, d_out, m_out, v_out = {}, {}, {}, {}
    for n in _SHARDED:
        g, d, nm, nv = comm.out[n]
        g_out[n], d_out[n], m_out[n], v_out[n] = g[None], d[None], nm[None], nv[None]

    small_g = _small_all_reduce(parts)
    small = {n: v for n, v in w.items() if n not in _SHARDED}
    pack = lambda vals: _pack_small({n: vals[n] for n in small}, n_heads)
    sd, sm, sv = _adam_small(pack(w), small_g, pack(mom), pack(var))
    sg, loss = _unpack_small(small_g, small, n_heads)
    g_out.update(sg)
    for dst, row in ((d_out, sd), (m_out, sm), (v_out, sv)):
        dst.update(_unpack_small(row, small, n_heads)[0])

    return (loss, dx[None], *[g_out[n] for n in _ORDER], *[d_out[n] for n in _ORDER],
            *[m_out[n] for n in _ORDER], *[v_out[n] for n in _ORDER])
```

```python
import functools
import math

import numpy as np
import jax
import jax.numpy as jnp
from jax import lax
from jax.experimental import pallas as pl
from jax.experimental.pallas import tpu as pltpu

F32 = jnp.float32
BF16 = jnp.bfloat16

N_DEV = 8
N_CHIP = 4
HEAD_DIM = 128
GROUP = 4
GRID_W = 64
WINDOW = 128
BLOCK_Q = 128
N_BUCKETS = 32
MAX_DISTANCE = 128
ROPE_THETA = 10000.0
EPS = 1e-6
NEG_INF = -1e30
ADAM_LR = 0.001
ADAM_B1 = 0.9
ADAM_B2 = 0.999
ADAM_EPS = 1e-08
ADAM_WD = 0.01
ADAM_STEP = 10
LANES = 128
SUBLANES = 8
MESH = pl.DeviceIdType.MESH

_NT = (((1,), (1,)), ((), ()))
_NN = (((1,), (0,)), ((), ()))
_TN = (((0,), (0,)), ((), ()))


def _tile(dim, pref):
    return pref if dim % pref == 0 else dim


def _params(sem):
    return pltpu.CompilerParams(dimension_semantics=sem, vmem_limit_bytes=56 * 1024 * 1024)


_HBM = pl.BlockSpec(memory_space=pltpu.HBM)
_SEM = pl.BlockSpec(memory_space=pltpu.SEMAPHORE)
_ANY = pl.BlockSpec(memory_space=pl.ANY)
_VMEM = pl.BlockSpec(memory_space=pltpu.VMEM)
_EFFECT = pltpu.SideEffectType.DATAFLOW_SIDE_EFFECTING


def _pcall(body, deps=(), *, in_specs, **kw):
    deps = [d for d in deps if d is not None]
    nd = len(deps)

    def wrapped(*refs):
        body(*refs[nd:])

    call = pl.pallas_call(wrapped, in_specs=[_ANY] * nd + list(in_specs), **kw)
    return lambda *args: call(*deps, *args)


def _mm(name, a, b, dims, grid, a_spec, b_spec, out_shape, out_specs, acc_shape, epilogue,
        extras=(), extra_specs=(), deps=()):
    nk = grid[2]
    n_extra = len(extras)

    def body(*refs):
        a_ref, b_ref = refs[0], refs[1]
        extra = refs[2:2 + n_extra]
        outs = refs[2 + n_extra:-1]
        acc = refs[-1]
        part = lax.dot_general(a_ref[...], b_ref[...], dims, preferred_element_type=F32)
        if nk == 1:
            epilogue(part, extra, outs)
        else:
            k = pl.program_id(2)

            @pl.when(k == 0)
            def _():
                acc[...] = part

            @pl.when(k > 0)
            def _():
                acc[...] += part

            @pl.when(k == nk - 1)
            def _():
                epilogue(acc[...], extra, outs)

    return _pcall(
        body, deps, name=name, grid=grid,
        in_specs=[a_spec, b_spec, *extra_specs],
        out_specs=out_specs, out_shape=out_shape,
        scratch_shapes=[pltpu.VMEM(acc_shape if nk > 1 else (SUBLANES, LANES), F32)],
        compiler_params=_params(("parallel", "parallel", "arbitrary")),
    )(a, b, *extras)


def _store(dtype):
    def ep(acc, extra, outs):
        outs[0][...] = acc.astype(dtype)
    return ep


def _store_add(acc, extra, outs):
    outs[0][...] = acc + extra[0][...]


def _mm_nn(name, a, b, out_dtype=F32, epilogue=None, extras=(), n_out=1, out_dtypes=None, tm=1024, tn=512, tk=2048,
           deps=()):
    M, K = a.shape
    tm, tk = _tile(M, tm), _tile(K, tk)
    if b.ndim == 3:
        nb, _, n = b.shape
        N, tn = nb * n, n
        b_spec = pl.BlockSpec((None, tk, n), lambda i, j, k: (j, k, 0))
    else:
        N = b.shape[1]
        tn = _tile(N, tn)
        b_spec = pl.BlockSpec((tk, tn), lambda i, j, k: (k, j))
    grid = (M // tm, N // tn, K // tk)
    o_spec = pl.BlockSpec((tm, tn), lambda i, j, k: (i, j))
    out_dtypes = out_dtypes or [out_dtype] * n_out
    out_shape = [jax.ShapeDtypeStruct((M, N), d) for d in out_dtypes]
    res = _mm(name, a, b, _NN, grid, pl.BlockSpec((tm, tk), lambda i, j, k: (i, k)), b_spec,
              out_shape, [o_spec] * len(out_dtypes), (tm, tn), epilogue or _store(out_dtype),
              extras, [o_spec] * len(extras), deps)
    return res if len(out_dtypes) > 1 else res[0]


def _mm_nt(name, a, b, out_dtype=F32, epilogue=None, extras=(), tm=1024, tn=512, tk=2048, deps=()):
    M, C = a.shape
    tm = _tile(M, tm)
    if b.ndim == 3:
        nb, N, n = b.shape
        tk = n
        tn = _tile(N, tn)
        b_spec = pl.BlockSpec((None, tn, n), lambda i, j, k: (k, j, 0))
    else:
        N = b.shape[0]
        tn, tk = _tile(N, tn), _tile(C, tk)
        b_spec = pl.BlockSpec((tn, tk), lambda i, j, k: (j, k))
    grid = (M // tm, N // tn, C // tk)
    o_spec = pl.BlockSpec((tm, tn), lambda i, j, k: (i, j))
    return _mm(name, a, b, _NT, grid, pl.BlockSpec((tm, tk), lambda i, j, k: (i, k)), b_spec,
               [jax.ShapeDtypeStruct((M, N), out_dtype)], [o_spec], (tm, tn), epilogue or _store(out_dtype),
               extras, [o_spec] * len(extras), deps)[0]


def _mm_tn(name, a, b, blocked_n=None, tm=1024, tn=1024, tk=1024, deps=()):
    T, M = a.shape
    N = b.shape[1]
    tm, tk = _tile(M, tm), _tile(T, tk)
    if blocked_n is not None:
        tn = blocked_n
        out_shape = jax.ShapeDtypeStruct((N // tn, M, tn), F32)
        o_spec = pl.BlockSpec((None, tm, tn), lambda i, j, k: (j, i, 0))
    else:
        tn = _tile(N, tn)
        out_shape = jax.ShapeDtypeStruct((M, N), F32)
        o_spec = pl.BlockSpec((tm, tn), lambda i, j, k: (i, j))
    grid = (M // tm, N // tn, T // tk)
    return _mm(name, a, b, _TN, grid, pl.BlockSpec((tk, tm), lambda i, j, k: (k, i)),
               pl.BlockSpec((tk, tn), lambda i, j, k: (k, j)), [out_shape], [o_spec], (tm, tn), _store(F32),
               deps=deps)[0]


def _mean_last(v):
    return jnp.mean(v, axis=-1, keepdims=True)


def _rows_to_sublanes(v):
    r, c = v.shape
    return jnp.sum(v.reshape(r // SUBLANES, SUBLANES, c), axis=0)


def _accumulate(ref, val, first):
    @pl.when(first)
    def _():
        ref[...] = val

    @pl.when(jnp.logical_not(first))
    def _():
        ref[...] += val


def _rms_fwd(name, x, g, tr=256, deps=()):
    T, D = x.shape
    tr = _tile(T, tr)

    def body(x_ref, g_ref, o_ref):
        xv = x_ref[...]
        r = lax.rsqrt(_mean_last(xv * xv) + EPS)
        o_ref[...] = (xv * r * g_ref[...]).astype(BF16)

    row = pl.BlockSpec((tr, D), lambda i: (i, 0))
    return _pcall(
        body, deps, name=name, grid=(T // tr,),
        in_specs=[row, pl.BlockSpec((1, D), lambda i: (0, 0))],
        out_specs=row, out_shape=jax.ShapeDtypeStruct((T, D), BF16),
        compiler_params=_params(("parallel",)),
    )(x, g)


def _rms_bwd(name, dyn, x, g, dres, tr=256, deps=()):
    T, D = x.shape
    tr = _tile(T, tr)

    def body(dy_ref, x_ref, g_ref, dr_ref, dx_ref, dxb_ref, dg_ref):
        xv = x_ref[...]
        r = lax.rsqrt(_mean_last(xv * xv) + EPS)
        xn = xv * r
        dy = dy_ref[...]
        dxn = dy * g_ref[...]
        dx = dr_ref[...] + r * (dxn - xn * _mean_last(dxn * xn))
        dx_ref[...] = dx
        dxb_ref[...] = dx.astype(BF16)
        _accumulate(dg_ref, _rows_to_sublanes(dy * xn), pl.program_id(0) == 0)

    row = pl.BlockSpec((tr, D), lambda i: (i, 0))
    return _pcall(
        body, deps, name=name, grid=(T // tr,),
        in_specs=[row, row, pl.BlockSpec((1, D), lambda i: (0, 0)), row],
        out_specs=[row, row, pl.BlockSpec((SUBLANES, D), lambda i: (0, 0))],
        out_shape=[jax.ShapeDtypeStruct((T, D), F32), jax.ShapeDtypeStruct((T, D), BF16),
                   jax.ShapeDtypeStruct((SUBLANES, D), F32)],
        compiler_params=_params(("arbitrary",)),
    )(dyn, x, g, dres)


def _tail(h2, z, pe, target, g_ple, g_final, tr=256):
    T, D = h2.shape
    tr = _tile(T, tr)

    def body(h2_ref, z_ref, pe_ref, t_ref, gp_ref, gf_ref,
             dh3_ref, dz_ref, dpe_ref, dgf_ref, dgp_ref, loss_ref):
        first = pl.program_id(0) == 0
        pev = pe_ref[...]
        r3 = lax.rsqrt(_mean_last(pev * pev) + EPS)
        en = pev * r3
        e = en * gp_ref[...]
        gate = 1.0 / (1.0 + jnp.exp(-z_ref[...]))
        h3 = h2_ref[...] + gate * e
        r5 = lax.rsqrt(_mean_last(h3 * h3) + EPS)
        hn = h3 * r5
        diff = hn * gf_ref[...] - t_ref[...]
        loss_rows = 0.5 * _mean_last(diff * diff)
        row0 = lax.broadcasted_iota(jnp.int32, (SUBLANES, LANES), 0) == 0
        _accumulate(loss_ref, jnp.where(row0, jnp.sum(loss_rows), 0.0), first)
        dy = diff * (1.0 / D)
        _accumulate(dgf_ref, _rows_to_sublanes(dy * hn), first)
        dhn = dy * gf_ref[...]
        dh3 = r5 * (dhn - hn * _mean_last(dhn * hn))
        dh3_ref[...] = dh3
        dgate = dh3 * e
        de = dh3 * gate
        dz_ref[...] = (dgate * gate * (1.0 - gate)).astype(BF16)
        _accumulate(dgp_ref, _rows_to_sublanes(de * en), first)
        den = de * gp_ref[...]
        dpe_ref[...] = (r3 * (den - en * _mean_last(den * en))).astype(BF16)

    row = pl.BlockSpec((tr, D), lambda i: (i, 0))
    vec = pl.BlockSpec((1, D), lambda i: (0, 0))
    part = pl.BlockSpec((SUBLANES, D), lambda i: (0, 0))
    return pl.pallas_call(
        body, name="tail", grid=(T // tr,),
        in_specs=[row, row, row, row, vec, vec],
        out_specs=[row, row, row, part, part, pl.BlockSpec((SUBLANES, LANES), lambda i: (0, 0))],
        out_shape=[jax.ShapeDtypeStruct((T, D), F32), jax.ShapeDtypeStruct((T, D), BF16),
                   jax.ShapeDtypeStruct((T, D), BF16), jax.ShapeDtypeStruct((SUBLANES, D), F32),
                   jax.ShapeDtypeStruct((SUBLANES, D), F32), jax.ShapeDtypeStruct((SUBLANES, LANES), F32)],
        compiler_params=_params(("arbitrary",)),
    )(h2, z, pe, target, g_ple, g_final)


def _rope_tables(T):
    pos = np.arange(T)
    half = HEAD_DIM // 2
    inv = (ROPE_THETA ** (-np.arange(0, half, 2, dtype=np.float32) / half)).astype(np.float32)
    ang_r = (pos // GRID_W).astype(np.float32)[:, None] * inv
    ang_c = (pos % GRID_W).astype(np.float32)[:, None] * inv
    cos = np.concatenate([np.cos(ang_r), np.cos(ang_r), np.cos(ang_c), np.cos(ang_c)], axis=-1)
    sin = np.concatenate([-np.sin(ang_r), np.sin(ang_r), -np.sin(ang_c), np.sin(ang_c)], axis=-1)
    return jnp.asarray(cos, F32), jnp.asarray(sin, F32)


def _swap32(x):
    lane = lax.broadcasted_iota(jnp.int32, x.shape, 1)
    return jnp.where((lane % 64) < 32, pltpu.roll(x, 96, 1), pltpu.roll(x, 32, 1))


def _qk_prep(proj, cos, sin, g_q, g_k, n_norm, tr=256):
    T, W = proj.shape
    tr = _tile(T, tr)
    n_q = n_norm * GROUP // (GROUP + 1)

    def body(p_ref, c_ref, s_ref, gq_ref, gk_ref, o_ref):
        c, s = c_ref[...], s_ref[...]
        for h in range(n_norm):
            cols = slice(h * HEAD_DIM, (h + 1) * HEAD_DIM)
            xv = p_ref[:, cols]
            g = gq_ref[...] if h < n_q else gk_ref[...]
            xn = xv * lax.rsqrt(_mean_last(xv * xv) + EPS) * g
            o_ref[:, cols] = (xn * c + _swap32(xn) * s).astype(BF16)
        rest = slice(n_norm * HEAD_DIM, W)
        o_ref[:, rest] = p_ref[:, rest].astype(BF16)

    row = pl.BlockSpec((tr, W), lambda i: (i, 0))
    tab = pl.BlockSpec((tr, HEAD_DIM), lambda i: (i, 0))
    vec = pl.BlockSpec((1, HEAD_DIM), lambda i: (0, 0))
    return pl.pallas_call(
        body, name="qk_prep", grid=(T // tr,),
        in_specs=[row, tab, tab, vec, vec], out_specs=row,
        out_shape=jax.ShapeDtypeStruct((T, W), BF16),
        compiler_params=_params(("parallel",)),
    )(proj, cos, sin, g_q, g_k)


def _dproj(proj, dqa, dka, dva, dqb, dkb, dvb, cos, sin, g_q, g_k, tr=256):
    T, W = proj.shape
    tr = _tile(T, tr)
    n_q = dqa.shape[1] // HEAD_DIM
    n_kv = dka.shape[1] // HEAD_DIM
    wa = (n_q + n_kv) * HEAD_DIM

    def body(p_ref, dqa_ref, dka_ref, dva_ref, dqb_ref, dkb_ref, dvb_ref, c_ref, s_ref, gq_ref, gk_ref,
             o_ref, dgq_ref, dgk_ref):
        c, s = c_ref[...], s_ref[...]
        dgq = jnp.zeros((SUBLANES, HEAD_DIM), F32)
        dgk = jnp.zeros((SUBLANES, HEAD_DIM), F32)
        for h in range(n_q + n_kv):
            cols = slice(h * HEAD_DIM, (h + 1) * HEAD_DIM)
            xv = p_ref[:, cols]
            r = lax.rsqrt(_mean_last(xv * xv) + EPS)
            xn = xv * r
            if h < n_q:
                d = dqa_ref[:, cols]
                g = gq_ref[...]
            else:
                d = dka_ref[:, (h - n_q) * HEAD_DIM:(h - n_q + 1) * HEAD_DIM]
                g = gk_ref[...]
            dqn = d * c + _swap32(d * s)
            part = _rows_to_sublanes(dqn * xn)
            if h < n_q:
                dgq = dgq + part
            else:
                dgk = dgk + part
            dxn = dqn * g
            o_ref[:, cols] = (r * (dxn - xn * _mean_last(dxn * xn))).astype(BF16)
        off = wa
        for ref in (dva_ref, dqb_ref, dkb_ref, dvb_ref):
            w = ref.shape[1]
            o_ref[:, off:off + w] = ref[...].astype(BF16)
            off += w
        first = pl.program_id(0) == 0
        _accumulate(dgq_ref, dgq, first)
        _accumulate(dgk_ref, dgk, first)

    def row(w):
        return pl.BlockSpec((tr, w), lambda i: (i, 0))

    vec = pl.BlockSpec((1, HEAD_DIM), lambda i: (0, 0))
    part = pl.BlockSpec((SUBLANES, HEAD_DIM), lambda i: (0, 0))
    return pl.pallas_call(
        body, name="dproj", grid=(T // tr,),
        in_specs=[row(wa), row(dqa.shape[1]), row(dka.shape[1]), row(dva.shape[1]), row(dqb.shape[1]),
                  row(dkb.shape[1]), row(dvb.shape[1]), row(HEAD_DIM), row(HEAD_DIM), vec, vec],
        out_specs=[row(W), part, part],
        out_shape=[jax.ShapeDtypeStruct((T, W), BF16), jax.ShapeDtypeStruct((SUBLANES, HEAD_DIM), F32),
                   jax.ShapeDtypeStruct((SUBLANES, HEAD_DIM), F32)],
        compiler_params=_params(("arbitrary",)),
    )(proj, dqa, dka, dva, dqb, dkb, dvb, cos, sin, g_q, g_k)


def _attn_a_fwd(pb, n_q, n_kv, tq=256):
    T = pb.shape[0]
    tq = _tile(T, tq)
    scale = HEAD_DIM ** -0.5

    def body(q_ref, k_ref, v_ref, o_ref, lse_ref):
        s = lax.dot_general(q_ref[...], k_ref[...], _NT, preferred_element_type=F32) * scale
        m = jnp.max(s, axis=-1, keepdims=True)
        p = jnp.exp(s - m)
        l = jnp.sum(p, axis=-1, keepdims=True)
        o = lax.dot_general(p.astype(BF16), v_ref[...], _NN, preferred_element_type=F32)
        o_ref[...] = (o / l).astype(BF16)
        lse_ref[...] = m + jnp.log(l)

    return pl.pallas_call(
        body, name="attn_a_fwd", grid=(n_kv, GROUP, T // tq),
        in_specs=[pl.BlockSpec((tq, HEAD_DIM), lambda kv, g, i: (i, kv * GROUP + g)),
                  pl.BlockSpec((T, HEAD_DIM), lambda kv, g, i: (0, n_q + kv)),
                  pl.BlockSpec((T, HEAD_DIM), lambda kv, g, i: (0, n_q + n_kv + kv))],
        out_specs=[pl.BlockSpec((tq, HEAD_DIM), lambda kv, g, i: (i, kv * GROUP + g)),
                   pl.BlockSpec((None, tq, 1), lambda kv, g, i: (kv * GROUP + g, i, 0))],
        out_shape=[jax.ShapeDtypeStruct((T, n_q * HEAD_DIM), BF16), jax.ShapeDtypeStruct((n_q, T, 1), F32)],
        compiler_params=_params(("parallel", "parallel", "parallel")),
    )(pb, pb, pb)


def _attn_a_bwd(pb, o_cat, d_o, lse, n_q, n_kv, tq=256):
    T = pb.shape[0]
    tq = _tile(T, tq)
    scale = HEAD_DIM ** -0.5

    def body(q_ref, k_ref, v_ref, o_ref, do_ref, lse_ref, dq_ref, dk_ref, dv_ref):
        q, k, v, do = q_ref[...], k_ref[...], v_ref[...], do_ref[...]
        delta = jnp.sum(do.astype(F32) * o_ref[...].astype(F32), axis=-1, keepdims=True)
        s = lax.dot_general(q, k, _NT, preferred_element_type=F32) * scale
        p = jnp.exp(s - lse_ref[...])
        dp = lax.dot_general(do, v, _NT, preferred_element_type=F32)
        ds = (p * (dp - delta) * scale).astype(BF16)
        dq_ref[...] = lax.dot_general(ds, k, _NN, preferred_element_type=F32)
        first = jnp.logical_and(pl.program_id(1) == 0, pl.program_id(2) == 0)
        _accumulate(dv_ref, lax.dot_general(p.astype(BF16), do, _TN, preferred_element_type=F32), first)
        _accumulate(dk_ref, lax.dot_general(ds, q, _TN, preferred_element_type=F32), first)

    qmap = lambda kv, g, i: (i, kv * GROUP + g)
    return pl.pallas_call(
        body, name="attn_a_bwd", grid=(n_kv, GROUP, T // tq),
        in_specs=[pl.BlockSpec((tq, HEAD_DIM), qmap),
                  pl.BlockSpec((T, HEAD_DIM), lambda kv, g, i: (0, n_q + kv)),
                  pl.BlockSpec((T, HEAD_DIM), lambda kv, g, i: (0, n_q + n_kv + kv)),
                  pl.BlockSpec((tq, HEAD_DIM), qmap),
                  pl.BlockSpec((tq, HEAD_DIM), qmap),
                  pl.BlockSpec((None, tq, 1), lambda kv, g, i: (kv * GROUP + g, i, 0))],
        out_specs=[pl.BlockSpec((tq, HEAD_DIM), qmap),
                   pl.BlockSpec((T, HEAD_DIM), lambda kv, g, i: (0, kv)),
                   pl.BlockSpec((T, HEAD_DIM), lambda kv, g, i: (0, kv))],
        out_shape=[jax.ShapeDtypeStruct((T, n_q * HEAD_DIM), F32),
                   jax.ShapeDtypeStruct((T, n_kv * HEAD_DIM), F32),
                   jax.ShapeDtypeStruct((T, n_kv * HEAD_DIM), F32)],
        compiler_params=_params(("parallel", "arbitrary", "arbitrary")),
    )(pb, pb, pb, o_cat, d_o, lse)


def _bucket_index():
    r = np.arange(BLOCK_Q)[:, None]
    j = np.arange(3 * BLOCK_Q)[None, :]
    rel = (j - BLOCK_Q) - r
    nb = N_BUCKETS // 2
    ret = np.where(rel > 0, nb, 0)
    n = np.abs(rel)
    max_exact = nb // 2
    nf = np.maximum(n, 1).astype(np.float32)
    large = max_exact + (np.log(nf / max_exact) / math.log(MAX_DISTANCE / max_exact) * (nb - max_exact)).astype(np.int32)
    large = np.minimum(large, nb - 1)
    return jnp.asarray(ret + np.where(n < max_exact, n, large), jnp.int32)


def _bias_build(idx, table_flat, n_heads, deps=()):
    def body(idx_ref, tab_ref, o_ref):
        h = pl.program_id(0)
        iv = idx_ref[...]
        acc = jnp.zeros(iv.shape, F32)
        for b in range(N_BUCKETS):
            acc = jnp.where(iv == b, tab_ref[b * n_heads + h], acc)
        o_ref[...] = acc

    return _pcall(
        body, deps, name="bias_build", grid=(n_heads,),
        in_specs=[pl.BlockSpec(idx.shape, lambda h: (0, 0)), pl.BlockSpec(memory_space=pltpu.SMEM)],
        out_specs=pl.BlockSpec((None,) + idx.shape, lambda h: (h, 0, 0)),
        out_shape=jax.ShapeDtypeStruct((n_heads,) + idx.shape, F32),
        compiler_params=_params(("parallel",)),
    )(idx, table_flat)


def _band_mask(n, T):
    r = lax.broadcasted_iota(jnp.int32, (BLOCK_Q, 3 * BLOCK_Q), 0)
    j = lax.broadcasted_iota(jnp.int32, (BLOCK_Q, 3 * BLOCK_Q), 1)
    rel = (j - BLOCK_Q) - r
    kabs = n * BLOCK_Q + j - BLOCK_Q
    return (jnp.abs(rel) <= WINDOW) & (kabs >= 0) & (kabs < T)


def _band_specs(col, nblk):
    return [pl.BlockSpec((BLOCK_Q, HEAD_DIM), lambda kv, n: (jnp.maximum(n - 1, 0), col(kv))),
            pl.BlockSpec((BLOCK_Q, HEAD_DIM), lambda kv, n: (n, col(kv))),
            pl.BlockSpec((BLOCK_Q, HEAD_DIM), lambda kv, n: (jnp.minimum(n + 1, nblk - 1), col(kv)))]


def _attn_b_fwd(pb, bias, sink, q_off, n_q, n_kv, deps=()):
    T = pb.shape[0]
    nblk = T // BLOCK_Q
    scale = HEAD_DIM ** -0.5

    def body(*refs):
        q_refs = refs[0:GROUP]
        k_refs, v_refs = refs[GROUP:GROUP + 3], refs[GROUP + 3:GROUP + 6]
        bias_ref, sink_ref, o_ref, lse_ref = refs[GROUP + 6:]
        kv, n = pl.program_id(0), pl.program_id(1)
        kb = jnp.concatenate([r[...] for r in k_refs], axis=0)
        vb = jnp.concatenate([r[...] for r in v_refs], axis=0)
        mask = _band_mask(n, T)
        for g in range(GROUP):
            sk = sink_ref[kv * GROUP + g]
            s = lax.dot_general(q_refs[g][...], kb, _NT, preferred_element_type=F32) * scale + bias_ref[g]
            s = jnp.where(mask, s, NEG_INF)
            m = jnp.maximum(jnp.max(s, axis=-1, keepdims=True), sk)
            p = jnp.exp(s - m)
            l = jnp.sum(p, axis=-1, keepdims=True) + jnp.exp(sk - m)
            o = lax.dot_general(p.astype(BF16), vb, _NN, preferred_element_type=F32)
            o_ref[:, g * HEAD_DIM:(g + 1) * HEAD_DIM] = (o / l).astype(BF16)
            lse_ref[g] = m + jnp.log(l)

    q_specs = [pl.BlockSpec((BLOCK_Q, HEAD_DIM), functools.partial(lambda kv, n, g: (n, q_off + kv * GROUP + g), g=g))
               for g in range(GROUP)]
    return _pcall(
        body, deps, name="attn_b_fwd", grid=(n_kv, nblk),
        in_specs=[*q_specs,
                  *_band_specs(lambda kv: q_off + n_q + kv, nblk),
                  *_band_specs(lambda kv: q_off + n_q + n_kv + kv, nblk),
                  pl.BlockSpec((GROUP, BLOCK_Q, 3 * BLOCK_Q), lambda kv, n: (kv, 0, 0)),
                  pl.BlockSpec(memory_space=pltpu.SMEM)],
        out_specs=[pl.BlockSpec((BLOCK_Q, GROUP * HEAD_DIM), lambda kv, n: (n, kv)),
                   pl.BlockSpec((GROUP, BLOCK_Q, 1), lambda kv, n: (kv, n, 0))],
        out_shape=[jax.ShapeDtypeStruct((T, n_q * HEAD_DIM), BF16), jax.ShapeDtypeStruct((n_q, T, 1), F32)],
        compiler_params=_params(("parallel", "parallel")),
    )(*([pb] * (GROUP + 6)), bias, sink)


def _attn_b_bwd(pb, o_cat, d_o, lse, bias, sink, q_off, n_q, n_kv, o_off, deps=()):
    T = pb.shape[0]
    nblk = T // BLOCK_Q
    scale = HEAD_DIM ** -0.5

    def body(*refs):
        q_refs = refs[0:GROUP]
        k_refs, v_refs = refs[GROUP:GROUP + 3], refs[GROUP + 3:GROUP + 6]
        o_refs, do_refs = refs[GROUP + 6:2 * GROUP + 6], refs[2 * GROUP + 6:3 * GROUP + 6]
        lse_ref, bias_ref, sink_ref, dq_ref, dk_ref, dv_ref, dbias_ref, dsink_ref = refs[3 * GROUP + 6:]
        kv, n = pl.program_id(0), pl.program_id(1)
        first = n == 0
        kb = jnp.concatenate([r[...] for r in k_refs], axis=0)
        vb = jnp.concatenate([r[...] for r in v_refs], axis=0)
        mask = _band_mask(n, T)
        dkb = jnp.zeros((3 * BLOCK_Q, HEAD_DIM), F32)
        dvb = jnp.zeros((3 * BLOCK_Q, HEAD_DIM), F32)
        row = lax.broadcasted_iota(jnp.int32, (SUBLANES, LANES), 0)
        dsink = jnp.zeros((SUBLANES, LANES), F32)
        for g in range(GROUP):
            sk = sink_ref[kv * GROUP + g]
            q, do = q_refs[g][...], do_refs[g][...]
            lse_g = lse_ref[g]
            delta = jnp.sum(do.astype(F32) * o_refs[g][...].astype(F32), axis=-1, keepdims=True)
            s = lax.dot_general(q, kb, _NT, preferred_element_type=F32) * scale + bias_ref[g]
            s = jnp.where(mask, s, NEG_INF)
            p = jnp.exp(s - lse_g)
            dp = lax.dot_general(do, vb, _NT, preferred_element_type=F32)
            ds = p * (dp - delta)
            _accumulate(dbias_ref.at[g], ds, first)
            dsink = dsink + jnp.where(row == g, -jnp.sum(jnp.exp(sk - lse_g) * delta), 0.0)
            dsb = (ds * scale).astype(BF16)
            dq_ref[:, g * HEAD_DIM:(g + 1) * HEAD_DIM] = lax.dot_general(dsb, kb, _NN, preferred_element_type=F32)
            dkb = dkb + lax.dot_general(dsb, q, _TN, preferred_element_type=F32)
            dvb = dvb + lax.dot_general(p.astype(BF16), do, _TN, preferred_element_type=F32)
        _accumulate(dsink_ref, dsink, first)

        @pl.when(first)
        def _():
            dk_ref[...] = jnp.zeros(dk_ref.shape, F32)
            dv_ref[...] = jnp.zeros(dv_ref.shape, F32)

        blocks = (jnp.maximum(n - 1, 0), n, jnp.minimum(n + 1, nblk - 1))
        for t, blk in enumerate(blocks):
            rows = pl.ds(pl.multiple_of(blk * BLOCK_Q, BLOCK_Q), BLOCK_Q)
            dk_ref[rows, :] += dkb[t * BLOCK_Q:(t + 1) * BLOCK_Q]
            dv_ref[rows, :] += dvb[t * BLOCK_Q:(t + 1) * BLOCK_Q]

    def head_specs(base):
        return [pl.BlockSpec((BLOCK_Q, HEAD_DIM), functools.partial(lambda kv, n, g: (n, base + kv * GROUP + g), g=g))
                for g in range(GROUP)]

    return _pcall(
        body, deps, name="attn_b_bwd", grid=(n_kv, nblk),
        in_specs=[*head_specs(q_off),
                  *_band_specs(lambda kv: q_off + n_q + kv, nblk),
                  *_band_specs(lambda kv: q_off + n_q + n_kv + kv, nblk),
                  *head_specs(o_off), *head_specs(o_off),
                  pl.BlockSpec((GROUP, BLOCK_Q, 1), lambda kv, n: (kv, n, 0)),
                  pl.BlockSpec((GROUP, BLOCK_Q, 3 * BLOCK_Q), lambda kv, n: (kv, 0, 0)),
                  pl.BlockSpec(memory_space=pltpu.SMEM)],
        out_specs=[pl.BlockSpec((BLOCK_Q, GROUP * HEAD_DIM), lambda kv, n: (n, kv)),
                   pl.BlockSpec((T, HEAD_DIM), lambda kv, n: (0, kv)),
                   pl.BlockSpec((T, HEAD_DIM), lambda kv, n: (0, kv)),
                   pl.BlockSpec((GROUP, BLOCK_Q, 3 * BLOCK_Q), lambda kv, n: (kv, 0, 0)),
                   pl.BlockSpec((None, SUBLANES, LANES), lambda kv, n: (kv, 0, 0))],
        out_shape=[jax.ShapeDtypeStruct((T, n_q * HEAD_DIM), F32),
                   jax.ShapeDtypeStruct((T, n_kv * HEAD_DIM), F32),
                   jax.ShapeDtypeStruct((T, n_kv * HEAD_DIM), F32),
                   jax.ShapeDtypeStruct((n_q, BLOCK_Q, 3 * BLOCK_Q), F32),
                   jax.ShapeDtypeStruct((n_kv, SUBLANES, LANES), F32)],
        compiler_params=_params(("parallel", "arbitrary")),
    )(*([pb] * (GROUP + 6)), *([o_cat] * GROUP), *([d_o] * GROUP), lse, bias, sink)


def _table_grads(dbias, dsink_raw, idx):
    n_heads = dbias.shape[0]
    n_kv = dsink_raw.shape[0]

    def body(db_ref, ds_ref, idx_ref, dt_ref, dsk_ref):
        iv = idx_ref[...]
        row = lax.broadcasted_iota(jnp.int32, (SUBLANES, LANES), 0)
        lane = lax.broadcasted_iota(jnp.int32, (SUBLANES, LANES), 1)
        dsk = jnp.zeros((SUBLANES, LANES), F32)
        for h in range(n_heads):
            d = db_ref[h]
            acc = jnp.zeros((SUBLANES, LANES), F32)
            for b in range(N_BUCKETS):
                acc = jnp.where((row == 0) & (lane == b), jnp.sum(jnp.where(iv == b, d, 0.0)), acc)
            dt_ref[:, h * LANES:(h + 1) * LANES] = acc
            raw = ds_ref[h // GROUP]
            val = jnp.sum(jnp.where((row == h % GROUP) & (lane == 0), raw, 0.0))
            dsk = jnp.where((row == 0) & (lane == h), val, dsk)
        dsk_ref[...] = dsk

    return pl.pallas_call(
        body, name="table_grads",
        in_specs=[pl.BlockSpec(memory_space=pltpu.VMEM)] * 3,
        out_specs=[pl.BlockSpec(memory_space=pltpu.VMEM)] * 2,
        out_shape=[jax.ShapeDtypeStruct((SUBLANES, n_heads * LANES), F32),
                   jax.ShapeDtypeStruct((SUBLANES, LANES), F32)],
        compiler_params=pltpu.CompilerParams(vmem_limit_bytes=56 * 1024 * 1024),
    )(dbias, dsink_raw, idx)


def _position():
    x, y, c = lax.axis_index("x"), lax.axis_index("y"), lax.axis_index("c")
    return x, y, c


def _hbm(a):
    return pltpu.with_memory_space_constraint(a, pltpu.HBM)


def _split_start(name, bufs, sem_shapes, issue):
    nb, ns = len(bufs), len(sem_shapes)

    def body(*refs):
        buf_refs = refs[:nb]
        sems = refs[nb:nb + ns]
        token = refs[nb + ns + nb]
        issue(buf_refs, sems)
        token[...] = jnp.zeros(token.shape, F32)

    outs = pl.pallas_call(
        body, name=name,
        in_specs=[_HBM] * nb,
        out_specs=[_SEM] * ns + [_HBM] * nb + [_VMEM],
        out_shape=[pltpu.SemaphoreType.DMA(s) for s in sem_shapes] + [pltpu.HBM(b.shape, b.dtype) for b in bufs]
        + [jax.ShapeDtypeStruct((SUBLANES, LANES), F32)],
        input_output_aliases={i: ns + i for i in range(nb)},
        compiler_params=pltpu.CompilerParams(has_side_effects=_EFFECT),
    )(*[_hbm(b) for b in bufs])
    return outs[:ns], outs[ns:ns + nb], outs[-1]


def _split_wait(name, bufs, send, recv, counts, size_of, after):
    nb = len(bufs)

    def body(*refs):
        buf_refs = refs[:nb]
        send_ref, recv_ref = refs[nb], refs[nb + 1]
        x, y, c = _position()
        for w, n in enumerate(counts):
            ref = size_of(buf_refs, w)
            for k in range(n):
                s = sum(counts[:w]) + k
                cp = pltpu.make_async_remote_copy(
                    src_ref=ref, dst_ref=ref, send_sem=send_ref.at[s], recv_sem=recv_ref.at[s],
                    device_id=(x, y, c), device_id_type=MESH)
                cp.wait_send()
                cp.wait_recv()

    return pl.pallas_call(
        body, name=name,
        in_specs=[_HBM] * nb + [_SEM, _SEM, _ANY],
        out_specs=[_HBM] * nb,
        out_shape=[pltpu.HBM(b.shape, b.dtype) for b in bufs],
        input_output_aliases={i: i for i in range(nb)},
        compiler_params=pltpu.CompilerParams(has_side_effects=_EFFECT),
    )(*bufs, send, recv, after)


def _block_of(pos):
    return 4 * pos[0] + 2 * pos[1] + pos[2]


def _gather_start(shards, groups):
    nw = len(shards)
    mine = _block_of(_position())
    lands = [lax.dynamic_update_slice(lax.empty((N_DEV,) + s.shape, s.dtype), s[None], (mine, 0, 0)) for s in shards]

    def issue(bufs, sems):
        ins, land = bufs[:nw], bufs[nw:]
        x, y, c = _position()
        peers = [(x, y, 1 - c), (1 - x, y, c), (x, 1 - y, c), (1 - x, 1 - y, c)]
        for gi, grp in enumerate(groups):
            for wi, w in enumerate(grp):
                for k, peer in enumerate(peers):
                    pltpu.make_async_remote_copy(
                        src_ref=ins[w], dst_ref=land[w].at[_block_of((x, y, c))], send_sem=sems[2 * gi].at[4 * wi + k],
                        recv_sem=sems[2 * gi + 1].at[4 * wi + k], device_id=peer, device_id_type=MESH).start()

    sem_shapes = [(4 * len(g),) for g in groups for _ in range(2)]
    sems, thru, token = _split_start("gather_start", list(shards) + lands, sem_shapes, issue)
    return sems, thru[:nw], thru[nw:], token


def _gather_forward(name, lands):
    nw = len(lands)

    def issue(land, sems):
        x, y, c = _position()
        for w in range(nw):
            for k, chip in enumerate([(1 - x, y), (x, 1 - y), (1 - x, 1 - y)]):
                blk = land[w].at[_block_of((*chip, c))]
                pltpu.make_async_remote_copy(
                    src_ref=blk, dst_ref=blk, send_sem=sems[0].at[3 * w + k], recv_sem=sems[1].at[3 * w + k],
                    device_id=(x, y, 1 - c), device_id_type=MESH).start()

    return _split_start(name, lands, [(3 * nw,), (3 * nw,)], issue)


def _first_block(bufs, w, offset=0):
    return bufs[offset + w].at[0]


def _pair_start(name, grads):
    nw = len(grads)
    lands = [lax.empty((N_CHIP,) + g.shape[1:], g.dtype) for g in grads]

    def issue(bufs, sems):
        x, y, c = _position()
        for w in range(nw):
            for q in range(N_CHIP):
                pltpu.make_async_remote_copy(
                    src_ref=bufs[w].at[2 * q + 1 - c], dst_ref=bufs[nw + w].at[q], send_sem=sems[0].at[N_CHIP * w + q],
                    recv_sem=sems[1].at[N_CHIP * w + q], device_id=(x, y, 1 - c), device_id_type=MESH).start()

    return _split_start(name, list(grads) + lands, [(N_CHIP * nw,), (N_CHIP * nw,)], issue)


def _chip_start(name, sums):
    nw = len(sums)
    x, y, _ = _position()
    mine = 2 * x + y
    lands = [lax.dynamic_update_slice(lax.empty(s.shape, s.dtype), lax.dynamic_slice_in_dim(s, mine, 1, 0), (mine, 0, 0))
             for s in sums]

    def issue(bufs, sems):
        x, y, c = _position()
        for w in range(nw):
            for k, (px, py) in enumerate([(1 - x, y), (x, 1 - y), (1 - x, 1 - y)]):
                pltpu.make_async_remote_copy(
                    src_ref=bufs[w].at[2 * px + py], dst_ref=bufs[nw + w].at[2 * x + y], send_sem=sems[0].at[3 * w + k],
                    recv_sem=sems[1].at[3 * w + k], device_id=(px, py, c), device_id_type=MESH).start()

    return _split_start(name, list(sums) + lands, [(3 * nw,), (3 * nw,)], issue)


def _pair_sum(name, grad, landed, tr=256):
    _, R, C = grad.shape
    tr = _tile(R, tr)
    core = lax.axis_index("c").astype(jnp.int32).reshape(1)

    def body(c_ref, g_ref, l_ref, o_ref):
        o_ref[...] = (g_ref[...] + l_ref[...]).astype(BF16)

    slot = pl.BlockSpec((None, tr, C), lambda q, i, c_ref: (q, i, 0))
    return pl.pallas_call(
        body, name=name,
        grid_spec=pltpu.PrefetchScalarGridSpec(
            num_scalar_prefetch=1, grid=(N_CHIP, R // tr),
            in_specs=[pl.BlockSpec((None, tr, C), lambda q, i, c_ref: (2 * q + c_ref[0], i, 0)), slot],
            out_specs=slot),
        out_shape=jax.ShapeDtypeStruct((N_CHIP, R, C), BF16),
        compiler_params=_params(("parallel", "parallel")),
    )(core, grad, landed)


def _adam(w, g, m, v):
    m = ADAM_B1 * m + (1.0 - ADAM_B1) * g
    v = ADAM_B2 * v + (1.0 - ADAM_B2) * (g * g)
    m_hat = m / (1.0 - ADAM_B1 ** ADAM_STEP)
    v_hat = v / (1.0 - ADAM_B2 ** ADAM_STEP)
    delta = -ADAM_LR * (m_hat / (jnp.sqrt(v_hat) + ADAM_EPS) + ADAM_WD * w)
    return delta, m, v


def _sum_adam(name, landed, w, m, v, tr=256):
    R, C = w.shape
    tr = _tile(R, tr)

    def body(l_ref, w_ref, m_ref, v_ref, g_ref, d_ref, nm_ref, nv_ref):
        g = l_ref[0].astype(F32)
        for q in range(1, N_CHIP):
            g = g + l_ref[q].astype(F32)
        g_ref[...] = g
        d_ref[...], nm_ref[...], nv_ref[...] = _adam(w_ref[...], g, m_ref[...], v_ref[...])

    tile = pl.BlockSpec((tr, C), lambda i: (i, 0))
    return pl.pallas_call(
        body, name=name, grid=(R // tr,),
        in_specs=[pl.BlockSpec((N_CHIP, tr, C), lambda i: (0, i, 0)), tile, tile, tile],
        out_specs=[tile] * 4, out_shape=[jax.ShapeDtypeStruct((R, C), F32)] * 4,
        compiler_params=_params(("parallel",)),
    )(landed, w, m, v)


def _small_all_reduce(parts):
    W = parts.shape[1]

    def body(p_ref, o_ref, slots, send_sems, recv_sems):
        x, y, c = _position()
        me = 4 * x + 2 * y + c
        slots[me] = jnp.sum(p_ref[...], axis=0, keepdims=True)
        peers = [(x, y, 1 - c), (1 - x, y, c), (1 - x, y, 1 - c), (x, 1 - y, c), (x, 1 - y, 1 - c),
                 (1 - x, 1 - y, c), (1 - x, 1 - y, 1 - c)]
        copies = []
        for k, peer in enumerate(peers):
            cp = pltpu.make_async_remote_copy(
                src_ref=slots.at[me], dst_ref=slots.at[me], send_sem=send_sems.at[k], recv_sem=recv_sems.at[k],
                device_id=peer, device_id_type=MESH)
            cp.start()
            copies.append(cp)
        for cp in copies:
            cp.wait()
        total = slots[0]
        for d in range(1, N_DEV):
            total = total + slots[d]
        o_ref[...] = total

    return pl.pallas_call(
        body, name="small_all_reduce",
        in_specs=[pl.BlockSpec(memory_space=pltpu.VMEM)], out_specs=pl.BlockSpec(memory_space=pltpu.VMEM),
        out_shape=jax.ShapeDtypeStruct((1, W), F32),
        scratch_shapes=[pltpu.VMEM((N_DEV, 1, W), F32), pltpu.SemaphoreType.DMA((7,)), pltpu.SemaphoreType.DMA((7,))],
    )(parts)


def _adam_small(w, g, m, v):
    def body(w_ref, g_ref, m_ref, v_ref, d_ref, nm_ref, nv_ref):
        d_ref[...], nm_ref[...], nv_ref[...] = _adam(w_ref[...], g_ref[...], m_ref[...], v_ref[...])

    return pl.pallas_call(
        body, name="adam_small",
        in_specs=[pl.BlockSpec(memory_space=pltpu.VMEM)] * 4, out_specs=[pl.BlockSpec(memory_space=pltpu.VMEM)] * 3,
        out_shape=[jax.ShapeDtypeStruct(w.shape, F32)] * 3,
    )(w, g, m, v)


_GATHER_GROUPS = (("w_in",), ("w_out", "w_up", "ple_w"), ("w_down", "w_gate"))
_ROW_SHARDED = ("w_out", "w_down", "w_gate")


class _MeshComm:
    def __init__(self, w, mom, var):
        self.w, self.mom, self.var = w, mom, var
        self.out = {}
        self._pairs, self._chips = {}, {}

    def gather_begin(self):
        names = [n for g in _GATHER_GROUPS for n in g]
        self._idx = {n: i for i, n in enumerate(names)}
        groups = [[self._idx[n] for n in g] for g in _GATHER_GROUPS]
        self._sems, self._src, self._lands, token = _gather_start([self.w[n].astype(BF16) for n in names], groups)
        return token

    def gather_arrive(self, gi, after):
        ids = [self._idx[n] for n in _GATHER_GROUPS[gi]]
        bufs = [self._src[i] for i in ids] + [self._lands[i] for i in ids]
        out = _split_wait("gather_arrive%d" % gi, bufs, self._sems[2 * gi], self._sems[2 * gi + 1], [4] * len(ids),
                          functools.partial(_first_block, offset=len(ids)), after)
        self._arrived = out[len(ids):]

    def gather_forward(self, gi):
        self._fsems, self._fthru, token = _gather_forward("gather_forward%d" % gi, self._arrived)
        return token

    def gather_finish(self, gi, after):
        names = _GATHER_GROUPS[gi]
        out = _split_wait("gather_finish%d" % gi, self._fthru, self._fsems[0], self._fsems[1], [3] * len(names),
                          _first_block, after)
        return {n: a.reshape(-1, a.shape[-1]) if n in _ROW_SHARDED else a for n, a in zip(names, out)}

    def reduce_begin(self, key, grads):
        names = list(grads)
        sems, thru, token = _pair_start("pair_start_" + key, [grads[n] for n in names])
        self._pairs[key] = (names, sems, thru)
        return token

    def reduce_middle(self, key, after):
        names, sems, thru = self._pairs[key]
        nw = len(names)
        out = _split_wait("pair_wait_" + key, thru, sems[0], sems[1], [N_CHIP] * nw,
                          functools.partial(_first_block, offset=nw), after)
        sums = [_pair_sum("pair_sum_" + n, out[i], out[nw + i]) for i, n in enumerate(names)]
        sems2, thru2, token = _chip_start("chip_start_" + key, sums)
        self._chips[key] = (names, sems2, thru2)
        return token

    def reduce_finish(self, key, after):
        names, sems, thru = self._chips[key]
        nw = len(names)
        out = _split_wait("chip_wait_" + key, thru, sems[0], sems[1], [3] * nw,
                          functools.partial(_first_block, offset=nw), after)
        for i, n in enumerate(names):
            self.out[n] = _sum_adam("adam_" + n, out[nw + i], self.w[n], self.mom[n], self.var[n])


def _step(x, p, target, gains, comm):
    T, D = x.shape
    n_q = D // (2 * HEAD_DIM)
    n_kv = n_q // GROUP
    cos, sin = _rope_tables(T)
    idx = _bucket_index()
    rows = lambda g: g.reshape((N_DEV, g.shape[0] // N_DEV, g.shape[1]))

    t = comm.gather_begin()
    u = _rms_fwd("norm_attn", x, gains["attn_norm_g"], deps=(t,))
    comm.gather_arrive(0, u)
    t = comm.gather_forward(0)
    bias = _bias_build(idx, gains["rel_bias_table"].reshape(-1), n_q, deps=(t,))
    full = comm.gather_finish(0, bias)
    proj = _mm_nn("in_proj", u, full["w_in"])
    pb = _qk_prep(proj, cos, sin, gains["q_norm_g"], gains["k_norm_g"], n_q + n_kv)
    o_a, lse_a = _attn_a_fwd(pb, n_q, n_kv)
    comm.gather_arrive(1, o_a)
    t = comm.gather_forward(1)
    sink = gains["sink_logits"].reshape(-1)
    b_off = n_q + 2 * n_kv
    o_b, lse_b = _attn_b_fwd(pb, bias, sink, b_off, n_q, n_kv, deps=(t,))
    full.update(comm.gather_finish(1, o_b))
    o_cat = jnp.concatenate([o_a, o_b], axis=1)
    h1 = _mm_nn("out_proj", o_cat, full["w_out"], epilogue=_store_add, extras=(x,))
    m_in = _rms_fwd("norm_mlp", h1, gains["mlp_norm_g"])

    def up_epilogue(acc, extra, outs):
        outs[0][...] = acc.astype(BF16)
        r = jnp.maximum(acc, 0.0)
        outs[1][...] = (r * r).astype(BF16)

    a_act, f_act = _mm_nn("up_proj", m_in, full["w_up"], epilogue=up_epilogue, out_dtypes=[BF16, BF16])
    comm.gather_arrive(2, f_act)
    t = comm.gather_forward(2)
    p_b = p.astype(BF16)
    pe = _mm_nn("ple_proj", p_b, full["ple_w"], deps=(t,))
    full.update(comm.gather_finish(2, pe))
    h2 = _mm_nn("down_proj", f_act, full["w_down"], epilogue=_store_add, extras=(h1,), tn=1024, tk=1024)
    gn = _rms_fwd("norm_gate", h2, gains["gate_norm_g"])
    z = _mm_nn("gate_proj", gn, full["w_gate"])

    dh3, dz, dpe, dg_final, dg_ple, loss_part = _tail(h2, z, pe, target, gains["ple_norm_g"], gains["final_norm_g"])
    gw_gate = _mm_tn("grad_w_gate", gn, dz)
    gw_ple = _mm_tn("grad_ple_w", p_b, dpe, blocked_n=full["ple_w"].shape[2])
    t = comm.reduce_begin("a", dict(w_gate=rows(gw_gate), ple_w=gw_ple))
    dgn = _mm_nt("d_gate_in", dz, full["w_gate"], deps=(t,))
    dh2, dh2_b, dg_gate = _rms_bwd("norm_gate_bwd", dgn, h2, gains["gate_norm_g"], dh3)
    t = comm.reduce_middle("a", dh2_b)
    gw_down = _mm_tn("grad_w_down", f_act, dh2_b, deps=(t,))
    t = comm.reduce_begin("b", dict(w_down=rows(gw_down)))

    def act_bwd(acc, extra, outs):
        outs[0][...] = (acc * (2.0 * jnp.maximum(extra[0][...].astype(F32), 0.0))).astype(BF16)

    da = _mm_nt("d_act", dh2_b, full["w_down"], out_dtype=BF16, epilogue=act_bwd, extras=(a_act,), tn=1024, deps=(t,))
    comm.reduce_finish("a", da)
    t = comm.reduce_middle("b", da)
    gw_up = _mm_tn("grad_w_up", m_in, da, blocked_n=full["w_up"].shape[2], deps=(t,))
    t = comm.reduce_begin("c", dict(w_up=gw_up))
    dm = _mm_nt("d_mlp_in", da, full["w_up"], deps=(t,))
    t = comm.reduce_middle("c", dm)
    dh1, dh1_b, dg_mlp = _rms_bwd("norm_mlp_bwd", dm, h1, gains["mlp_norm_g"], dh2, deps=(t,))
    comm.reduce_finish("b", dh1_b)
    gw_out = _mm_tn("grad_w_out", o_cat, dh1_b)
    t = comm.reduce_begin("d", dict(w_out=rows(gw_out)))
    d_o = _mm_nt("d_attn_out", dh1_b, full["w_out"], out_dtype=BF16, deps=(t,))
    dqa, dka, dva = _attn_a_bwd(pb, o_cat, d_o, lse_a, n_q, n_kv)
    t = comm.reduce_middle("d", dqa)
    dqb, dkb, dvb, dbias, dsink_raw = _attn_b_bwd(pb, o_cat, d_o, lse_b, bias, sink, b_off, n_q, n_kv, n_q, deps=(t,))
    comm.reduce_finish("c", dqb)
    comm.reduce_finish("d", dqb)
    dtable, dsink = _table_grads(dbias, dsink_raw, idx)
    dproj, dg_q, dg_k = _dproj(proj, dqa, dka, dva, dqb, dkb, dvb, cos, sin, gains["q_norm_g"], gains["k_norm_g"])
    gw_in = _mm_tn("grad_w_in", u, dproj, blocked_n=full["w_in"].shape[2])
    t = comm.reduce_begin("e", dict(w_in=gw_in))
    du = _mm_nt("d_attn_in", dproj, full["w_in"], deps=(t,))
    t = comm.reduce_middle("e", du)
    dx, _, dg_attn = _rms_bwd("norm_attn_bwd", du, x, gains["attn_norm_g"], dh1, deps=(t,))
    comm.reduce_finish("e", dx)

    parts = jnp.concatenate([dg_attn, dg_mlp, dg_ple, dg_gate, dg_final, dg_q, dg_k, dtable, dsink, loss_part], axis=1)
    return dx, parts


_SHARDED = ("w_in", "w_out", "w_up", "w_down", "ple_w", "w_gate")
_VECTORS = ("attn_norm_g", "mlp_norm_g", "ple_norm_g", "gate_norm_g", "final_norm_g")
_ORDER = ("attn_norm_g", "w_in", "q_norm_g", "k_norm_g", "sink_logits", "w_out", "mlp_norm_g", "w_up", "w_down",
          "ple_w", "ple_norm_g", "gate_norm_g", "w_gate", "rel_bias_table", "final_norm_g")


def _pack_small(vals, n_heads):
    lane_pad = lambda v: jnp.pad(v, ((0, 0), (0, LANES - v.shape[1])))
    table = lane_pad(vals["rel_bias_table"].T).reshape(1, n_heads * LANES)
    return jnp.concatenate(
        [vals[n].reshape(1, -1) for n in _VECTORS] + [vals["q_norm_g"], vals["k_norm_g"], table,
                                                      lane_pad(vals["sink_logits"]), jnp.zeros((1, LANES), F32)], axis=1)


def _unpack_small(row, like, n_heads):
    out, off = {}, 0
    for n in _VECTORS:
        out[n] = row[:, off:off + like[n].size].reshape(like[n].shape)
        off += like[n].size
    for n in ("q_norm_g", "k_norm_g"):
        out[n] = row[:, off:off + LANES]
        off += LANES
    out["rel_bias_table"] = row[:, off:off + n_heads * LANES].reshape(n_heads, LANES)[:, :N_BUCKETS].T
    off += n_heads * LANES
    out["sink_logits"] = row[:, off:off + n_heads]
    off += LANES
    return out, row[0, off]


def kernel(x, p, attn_norm_g, w_in, q_norm_g, k_norm_g, sink_logits, w_out, mlp_norm_g, w_up, w_down, ple_w, ple_norm_g, gate_norm_g, w_gate, rel_bias_table, final_norm_g, loss_target, m_attn_norm_g, m_w_in, m_q_norm_g, m_k_norm_g, m_sink_logits, m_w_out, m_mlp_norm_g, m_w_up, m_w_down, m_ple_w, m_ple_norm_g, m_gate_norm_g, m_w_gate, m_rel_bias_table, m_final_norm_g, v_attn_norm_g, v_w_in, v_q_norm_g, v_k_norm_g, v_sink_logits, v_w_out, v_mlp_norm_g, v_w_up, v_w_down, v_ple_w, v_ple_norm_g, v_gate_norm_g, v_w_gate, v_rel_bias_table, v_final_norm_g):
    w = dict(attn_norm_g=attn_norm_g, w_in=w_in[0], q_norm_g=q_norm_g, k_norm_g=k_norm_g, sink_logits=sink_logits,
             w_out=w_out[0], mlp_norm_g=mlp_norm_g, w_up=w_up[0], w_down=w_down[0], ple_w=ple_w[0],
             ple_norm_g=ple_norm_g, gate_norm_g=gate_norm_g, w_gate=w_gate[0], rel_bias_table=rel_bias_table,
             final_norm_g=final_norm_g)
    mom = dict(attn_norm_g=m_attn_norm_g, w_in=m_w_in[0], q_norm_g=m_q_norm_g, k_norm_g=m_k_norm_g,
               sink_logits=m_sink_logits, w_out=m_w_out[0], mlp_norm_g=m_mlp_norm_g, w_up=m_w_up[0],
               w_down=m_w_down[0], ple_w=m_ple_w[0], ple_norm_g=m_ple_norm_g, gate_norm_g=m_gate_norm_g,
               w_gate=m_w_gate[0], rel_bias_table=m_rel_bias_table, final_norm_g=m_final_norm_g)
    var = dict(attn_norm_g=v_attn_norm_g, w_in=v_w_in[0], q_norm_g=v_q_norm_g, k_norm_g=v_k_norm_g,
               sink_logits=v_sink_logits, w_out=v_w_out[0], mlp_norm_g=v_mlp_norm_g, w_up=v_w_up[0],
               w_down=v_w_down[0], ple_w=v_ple_w[0], ple_norm_g=v_ple_norm_g, gate_norm_g=v_gate_norm_g,
               w_gate=v_w_gate[0], rel_bias_table=v_rel_bias_table, final_norm_g=v_final_norm_g)
    D = x.shape[-1]
    n_heads = D // (2 * HEAD_DIM)

    gains = {n: w[n] for n in w if n not in _SHARDED}
    gains["final_norm_g"] = final_norm_g.reshape(1, -1)

    comm = _MeshComm(w, mom, var)
    dx, parts = _step(x[0], p[0, 0], loss_target[0], gains, comm)

    g_out, d_out, m_out, v_out = {}, {}, {}, {}
    for n in _SHARDED:
        g, d, nm, nv = comm.out[n]
        g_out[n], d_out[n], m_out[n], v_out[n] = g[None], d[None], nm[None], nv[None]

    small_g = _small_all_reduce(parts)
    small = {n: v for n, v in w.items() if n not in _SHARDED}
    pack = lambda vals: _pack_small({n: vals[n] for n in small}, n_heads)
    sd, sm, sv = _adam_small(pack(w), small_g, pack(mom), pack(var))
    sg, loss = _unpack_small(small_g, small, n_heads)
    g_out.update(sg)
    for dst, row in ((d_out, sd), (m_out, sm), (v_out, sv)):
        dst.update(_unpack_small(row, small, n_heads)[0])

    return (loss, dx[None], *[g_out[n] for n in _ORDER], *[d_out[n] for n in _ORDER],
            *[m_out[n] for n in _ORDER], *[v_out[n] for n in _ORDER])
```

```python
import functools
import math

import numpy as np
import jax
import jax.numpy as jnp
from jax import lax
from jax.experimental import pallas as pl
from jax.experimental.pallas import tpu as pltpu

F32 = jnp.float32
BF16 = jnp.bfloat16

N_DEV = 8
N_CHIP = 4
HEAD_DIM = 128
GROUP = 4
GRID_W = 64
WINDOW = 128
BLOCK_Q = 128
N_BUCKETS = 32
MAX_DISTANCE = 128
ROPE_THETA = 10000.0
EPS = 1e-6
NEG_INF = -1e30
ADAM_LR = 0.001
ADAM_B1 = 0.9
ADAM_B2 = 0.999
ADAM_EPS = 1e-08
ADAM_WD = 0.01
ADAM_STEP = 10
LOG2E = math.log2(math.e)
LANES = 128
SUBLANES = 8
MESH = pl.DeviceIdType.MESH

_NT = (((1,), (1,)), ((), ()))
_NN = (((1,), (0,)), ((), ()))
_TN = (((0,), (0,)), ((), ()))


def _tile(dim, pref):
    return pref if dim % pref == 0 else dim


def _params(sem):
    return pltpu.CompilerParams(dimension_semantics=sem, vmem_limit_bytes=56 * 1024 * 1024)


_HBM = pl.BlockSpec(memory_space=pltpu.HBM)
_SEM = pl.BlockSpec(memory_space=pltpu.SEMAPHORE)
_ANY = pl.BlockSpec(memory_space=pl.ANY)
_VMEM = pl.BlockSpec(memory_space=pltpu.VMEM)
_EFFECT = pltpu.SideEffectType.DATAFLOW_SIDE_EFFECTING


def _pcall(body, deps=(), *, in_specs, **kw):
    deps = [d for d in deps if d is not None]
    nd = len(deps)

    def wrapped(*refs):
        body(*refs[nd:])

    call = pl.pallas_call(wrapped, in_specs=[_ANY] * nd + list(in_specs), **kw)
    return lambda *args: call(*deps, *args)


def _mm(name, a, b, dims, grid, a_spec, b_spec, out_shape, out_specs, acc_shape, epilogue,
        extras=(), extra_specs=(), deps=()):
    nk = grid[2]
    n_extra = len(extras)

    def body(*refs):
        a_ref, b_ref = refs[0], refs[1]
        extra = refs[2:2 + n_extra]
        outs = refs[2 + n_extra:-1]
        acc = refs[-1]
        part = lax.dot_general(a_ref[...], b_ref[...], dims, preferred_element_type=F32)
        if nk == 1:
            epilogue(part, extra, outs)
        else:
            k = pl.program_id(2)

            @pl.when(k == 0)
            def _():
                acc[...] = part

            @pl.when(k > 0)
            def _():
                acc[...] += part

            @pl.when(k == nk - 1)
            def _():
                epilogue(acc[...], extra, outs)

    return _pcall(
        body, deps, name=name, grid=grid,
        in_specs=[a_spec, b_spec, *extra_specs],
        out_specs=out_specs, out_shape=out_shape,
        scratch_shapes=[pltpu.VMEM(acc_shape if nk > 1 else (SUBLANES, LANES), F32)],
        compiler_params=_params(("parallel", "parallel", "arbitrary")),
    )(a, b, *extras)


def _store(dtype):
    def ep(acc, extra, outs):
        outs[0][...] = acc.astype(dtype)
    return ep


def _store_add(acc, extra, outs):
    outs[0][...] = acc + extra[0][...]


def _mm_nn(name, a, b, out_dtype=F32, epilogue=None, extras=(), n_out=1, out_dtypes=None, tm=1024, tn=1024, tk=None,
           deps=()):
    M, K = a.shape
    N = b.shape[1]
    tm, tn, tk = _tile(M, tm), _tile(N, tn), _tile(K, tk or K)
    b_spec = pl.BlockSpec((tk, tn), lambda i, j, k: (k, j))
    grid = (M // tm, N // tn, K // tk)
    o_spec = pl.BlockSpec((tm, tn), lambda i, j, k: (i, j))
    out_dtypes = out_dtypes or [out_dtype] * n_out
    out_shape = [jax.ShapeDtypeStruct((M, N), d) for d in out_dtypes]
    res = _mm(name, a, b, _NN, grid, pl.BlockSpec((tm, tk), lambda i, j, k: (i, k)), b_spec,
              out_shape, [o_spec] * len(out_dtypes), (tm, tn), epilogue or _store(out_dtype),
              extras, [o_spec] * len(extras), deps)
    return res if len(out_dtypes) > 1 else res[0]


def _mm_nt(name, a, b, out_dtype=F32, epilogue=None, extras=(), tm=1024, tn=1024, tk=None, deps=()):
    M, C = a.shape
    N = b.shape[0]
    tm, tn, tk = _tile(M, tm), _tile(N, tn), _tile(C, tk or C)
    b_spec = pl.BlockSpec((tn, tk), lambda i, j, k: (j, k))
    grid = (M // tm, N // tn, C // tk)
    o_spec = pl.BlockSpec((tm, tn), lambda i, j, k: (i, j))
    return _mm(name, a, b, _NT, grid, pl.BlockSpec((tm, tk), lambda i, j, k: (i, k)), b_spec,
               [jax.ShapeDtypeStruct((M, N), out_dtype)], [o_spec], (tm, tn), epilogue or _store(out_dtype),
               extras, [o_spec] * len(extras), deps)[0]


def _mm_tn(name, a, b, tm=1024, tn=512, tk=None, deps=()):
    T, M = a.shape
    N = b.shape[1]
    tm, tn, tk = _tile(M, tm), _tile(N, tn), _tile(T, tk or T)
    out_shape = jax.ShapeDtypeStruct((M, N), F32)
    o_spec = pl.BlockSpec((tm, tn), lambda i, j, k: (i, j))
    grid = (M // tm, N // tn, T // tk)
    return _mm(name, a, b, _TN, grid, pl.BlockSpec((tk, tm), lambda i, j, k: (k, i)),
               pl.BlockSpec((tk, tn), lambda i, j, k: (k, j)), [out_shape], [o_spec], (tm, tn), _store(F32),
               deps=deps)[0]


def _mean_last(v):
    return jnp.mean(v, axis=-1, keepdims=True)


def _rows_to_sublanes(v):
    r, c = v.shape
    return jnp.sum(v.reshape(r // SUBLANES, SUBLANES, c), axis=0)


def _accumulate(ref, val, first):
    @pl.when(first)
    def _():
        ref[...] = val

    @pl.when(jnp.logical_not(first))
    def _():
        ref[...] += val


def _rms_fwd(name, x, g, tr=256, deps=()):
    T, D = x.shape
    tr = _tile(T, tr)

    def body(x_ref, g_ref, o_ref):
        xv = x_ref[...]
        r = lax.rsqrt(_mean_last(xv * xv) + EPS)
        o_ref[...] = (xv * r * g_ref[...]).astype(BF16)

    row = pl.BlockSpec((tr, D), lambda i: (i, 0))
    return _pcall(
        body, deps, name=name, grid=(T // tr,),
        in_specs=[row, pl.BlockSpec((1, D), lambda i: (0, 0))],
        out_specs=row, out_shape=jax.ShapeDtypeStruct((T, D), BF16),
        compiler_params=_params(("parallel",)),
    )(x, g)


def _rms_bwd(name, dyn, x, g, dres, tr=256, deps=()):
    T, D = x.shape
    tr = _tile(T, tr)

    def body(dy_ref, x_ref, g_ref, dr_ref, dx_ref, dxb_ref, dg_ref):
        xv = x_ref[...]
        r = lax.rsqrt(_mean_last(xv * xv) + EPS)
        xn = xv * r
        dy = dy_ref[...]
        dxn = dy * g_ref[...]
        dx = dr_ref[...] + r * (dxn - xn * _mean_last(dxn * xn))
        dx_ref[...] = dx
        dxb_ref[...] = dx.astype(BF16)
        _accumulate(dg_ref, _rows_to_sublanes(dy * xn), pl.program_id(0) == 0)

    row = pl.BlockSpec((tr, D), lambda i: (i, 0))
    return _pcall(
        body, deps, name=name, grid=(T // tr,),
        in_specs=[row, row, pl.BlockSpec((1, D), lambda i: (0, 0)), row],
        out_specs=[row, row, pl.BlockSpec((SUBLANES, D), lambda i: (0, 0))],
        out_shape=[jax.ShapeDtypeStruct((T, D), F32), jax.ShapeDtypeStruct((T, D), BF16),
                   jax.ShapeDtypeStruct((SUBLANES, D), F32)],
        compiler_params=_params(("arbitrary",)),
    )(dyn, x, g, dres)


def _tail(h2, z, pe, target, g_ple, g_final, tr=256):
    T, D = h2.shape
    tr = _tile(T, tr)

    def body(h2_ref, z_ref, pe_ref, t_ref, gp_ref, gf_ref,
             dh3_ref, dz_ref, dpe_ref, dgf_ref, dgp_ref, loss_ref):
        first = pl.program_id(0) == 0
        pev = pe_ref[...]
        r3 = lax.rsqrt(_mean_last(pev * pev) + EPS)
        en = pev * r3
        e = en * gp_ref[...]
        gate = 1.0 / (1.0 + jnp.exp(-z_ref[...]))
        h3 = h2_ref[...] + gate * e
        r5 = lax.rsqrt(_mean_last(h3 * h3) + EPS)
        hn = h3 * r5
        diff = hn * gf_ref[...] - t_ref[...]
        loss_rows = 0.5 * _mean_last(diff * diff)
        row0 = lax.broadcasted_iota(jnp.int32, (SUBLANES, LANES), 0) == 0
        _accumulate(loss_ref, jnp.where(row0, jnp.sum(loss_rows), 0.0), first)
        dy = diff * (1.0 / D)
        _accumulate(dgf_ref, _rows_to_sublanes(dy * hn), first)
        dhn = dy * gf_ref[...]
        dh3 = r5 * (dhn - hn * _mean_last(dhn * hn))
        dh3_ref[...] = dh3
        dgate = dh3 * e
        de = dh3 * gate
        dz_ref[...] = (dgate * gate * (1.0 - gate)).astype(BF16)
        _accumulate(dgp_ref, _rows_to_sublanes(de * en), first)
        den = de * gp_ref[...]
        dpe_ref[...] = (r3 * (den - en * _mean_last(den * en))).astype(BF16)

    row = pl.BlockSpec((tr, D), lambda i: (i, 0))
    vec = pl.BlockSpec((1, D), lambda i: (0, 0))
    part = pl.BlockSpec((SUBLANES, D), lambda i: (0, 0))
    return pl.pallas_call(
        body, name="tail", grid=(T // tr,),
        in_specs=[row, row, row, row, vec, vec],
        out_specs=[row, row, row, part, part, pl.BlockSpec((SUBLANES, LANES), lambda i: (0, 0))],
        out_shape=[jax.ShapeDtypeStruct((T, D), F32), jax.ShapeDtypeStruct((T, D), BF16),
                   jax.ShapeDtypeStruct((T, D), BF16), jax.ShapeDtypeStruct((SUBLANES, D), F32),
                   jax.ShapeDtypeStruct((SUBLANES, D), F32), jax.ShapeDtypeStruct((SUBLANES, LANES), F32)],
        compiler_params=_params(("arbitrary",)),
    )(h2, z, pe, target, g_ple, g_final)


def _rope_tables(T):
    pos = np.arange(T)
    half = HEAD_DIM // 2
    inv = (ROPE_THETA ** (-np.arange(0, half, 2, dtype=np.float32) / half)).astype(np.float32)
    ang_r = (pos // GRID_W).astype(np.float32)[:, None] * inv
    ang_c = (pos % GRID_W).astype(np.float32)[:, None] * inv
    cos = np.concatenate([np.cos(ang_r), np.cos(ang_r), np.cos(ang_c), np.cos(ang_c)], axis=-1)
    sin = np.concatenate([-np.sin(ang_r), np.sin(ang_r), -np.sin(ang_c), np.sin(ang_c)], axis=-1)
    return jnp.asarray(cos, F32), jnp.asarray(sin, F32)


def _swap32(x):
    lane = lax.broadcasted_iota(jnp.int32, x.shape, 1)
    return jnp.where((lane % 64) < 32, pltpu.roll(x, 96, 1), pltpu.roll(x, 32, 1))


def _qk_prep(proj, cos, sin, g_q, g_k, n_norm, tr=256):
    T, W = proj.shape
    tr = _tile(T, tr)
    n_q = n_norm * GROUP // (GROUP + 1)

    def body(p_ref, c_ref, s_ref, gq_ref, gk_ref, o_ref):
        c, s = c_ref[...], s_ref[...]
        for h in range(n_norm):
            cols = slice(h * HEAD_DIM, (h + 1) * HEAD_DIM)
            xv = p_ref[:, cols]
            g = gq_ref[...] if h < n_q else gk_ref[...]
            xn = xv * lax.rsqrt(_mean_last(xv * xv) + EPS) * g
            o_ref[:, cols] = (xn * c + _swap32(xn) * s).astype(BF16)
        rest = slice(n_norm * HEAD_DIM, W)
        o_ref[:, rest] = p_ref[:, rest].astype(BF16)

    row = pl.BlockSpec((tr, W), lambda i: (i, 0))
    tab = pl.BlockSpec((tr, HEAD_DIM), lambda i: (i, 0))
    vec = pl.BlockSpec((1, HEAD_DIM), lambda i: (0, 0))
    return pl.pallas_call(
        body, name="qk_prep", grid=(T // tr,),
        in_specs=[row, tab, tab, vec, vec], out_specs=row,
        out_shape=jax.ShapeDtypeStruct((T, W), BF16),
        compiler_params=_params(("parallel",)),
    )(proj, cos, sin, g_q, g_k)


def _dproj(proj, dqa, dka, dva, dqb, dkb, dvb, cos, sin, g_q, g_k, tr=256):
    T, W = proj.shape
    tr = _tile(T, tr)
    n_q = dqa.shape[1] // HEAD_DIM
    n_kv = dka.shape[1] // HEAD_DIM
    wa = (n_q + n_kv) * HEAD_DIM

    def body(p_ref, dqa_ref, dka_ref, dva_ref, dqb_ref, dkb_ref, dvb_ref, c_ref, s_ref, gq_ref, gk_ref,
             o_ref, dgq_ref, dgk_ref):
        c, s = c_ref[...], s_ref[...]
        dgq = jnp.zeros((SUBLANES, HEAD_DIM), F32)
        dgk = jnp.zeros((SUBLANES, HEAD_DIM), F32)
        for h in range(n_q + n_kv):
            cols = slice(h * HEAD_DIM, (h + 1) * HEAD_DIM)
            xv = p_ref[:, cols]
            r = lax.rsqrt(_mean_last(xv * xv) + EPS)
            xn = xv * r
            if h < n_q:
                d = dqa_ref[:, cols]
                g = gq_ref[...]
            else:
                d = dka_ref[:, (h - n_q) * HEAD_DIM:(h - n_q + 1) * HEAD_DIM]
                g = gk_ref[...]
            dqn = d * c + _swap32(d * s)
            part = _rows_to_sublanes(dqn * xn)
            if h < n_q:
                dgq = dgq + part
            else:
                dgk = dgk + part
            dxn = dqn * g
            o_ref[:, cols] = (r * (dxn - xn * _mean_last(dxn * xn))).astype(BF16)
        off = wa
        for ref in (dva_ref, dqb_ref, dkb_ref, dvb_ref):
            w = ref.shape[1]
            o_ref[:, off:off + w] = ref[...].astype(BF16)
            off += w
        first = pl.program_id(0) == 0
        _accumulate(dgq_ref, dgq, first)
        _accumulate(dgk_ref, dgk, first)

    def row(w):
        return pl.BlockSpec((tr, w), lambda i: (i, 0))

    vec = pl.BlockSpec((1, HEAD_DIM), lambda i: (0, 0))
    part = pl.BlockSpec((SUBLANES, HEAD_DIM), lambda i: (0, 0))
    return pl.pallas_call(
        body, name="dproj", grid=(T // tr,),
        in_specs=[row(wa), row(dqa.shape[1]), row(dka.shape[1]), row(dva.shape[1]), row(dqb.shape[1]),
                  row(dkb.shape[1]), row(dvb.shape[1]), row(HEAD_DIM), row(HEAD_DIM), vec, vec],
        out_specs=[row(W), part, part],
        out_shape=[jax.ShapeDtypeStruct((T, W), BF16), jax.ShapeDtypeStruct((SUBLANES, HEAD_DIM), F32),
                   jax.ShapeDtypeStruct((SUBLANES, HEAD_DIM), F32)],
        compiler_params=_params(("arbitrary",)),
    )(proj, dqa, dka, dva, dqb, dkb, dvb, cos, sin, g_q, g_k)


def _attn_a_fwd(pb, n_q, n_kv, tq=1024, tc=1024):
    T = pb.shape[0]
    tq, tc = _tile(T, tq), _tile(T, tc)
    scale = HEAD_DIM ** -0.5
    c = scale * LOG2E

    def body(q_ref, k_ref, v_ref, o_ref, lse_ref):
        q = q_ref[...]
        m = l = acc = None
        for j in range(T // tc):
            keys = slice(j * tc, (j + 1) * tc)
            s = lax.dot_general(q, k_ref[keys, :], _NT, preferred_element_type=F32)
            mj = jnp.max(s, axis=-1, keepdims=True)
            m_new = mj if j == 0 else jnp.maximum(m, mj)
            p = jnp.exp2((s - m_new) * c)
            pv = lax.dot_general(p.astype(BF16), v_ref[keys, :], _NN, preferred_element_type=F32)
            if j == 0:
                l, acc = jnp.sum(p, axis=-1, keepdims=True), pv
            else:
                alpha = jnp.exp2((m - m_new) * c)
                l = alpha * l + jnp.sum(p, axis=-1, keepdims=True)
                acc = alpha * acc + pv
            m = m_new
        o_ref[...] = (acc / l).astype(BF16)
        lse_ref[...] = m * scale + jnp.log(l)

    return pl.pallas_call(
        body, name="attn_a_fwd", grid=(n_kv, GROUP, T // tq),
        in_specs=[pl.BlockSpec((tq, HEAD_DIM), lambda kv, g, i: (i, kv * GROUP + g)),
                  pl.BlockSpec((T, HEAD_DIM), lambda kv, g, i: (0, n_q + kv)),
                  pl.BlockSpec((T, HEAD_DIM), lambda kv, g, i: (0, n_q + n_kv + kv))],
        out_specs=[pl.BlockSpec((tq, HEAD_DIM), lambda kv, g, i: (i, kv * GROUP + g)),
                   pl.BlockSpec((None, tq, 1), lambda kv, g, i: (kv * GROUP + g, i, 0))],
        out_shape=[jax.ShapeDtypeStruct((T, n_q * HEAD_DIM), BF16), jax.ShapeDtypeStruct((n_q, T, 1), F32)],
        compiler_params=_params(("parallel", "parallel", "parallel")),
    )(pb, pb, pb)


def _attn_a_bwd(pb, o_cat, d_o, lse, n_q, n_kv, tq=2048, tc=512):
    T = pb.shape[0]
    tq, tc = _tile(T, tq), _tile(T, tc)
    scale = HEAD_DIM ** -0.5
    c = scale * LOG2E

    def body(q_ref, k_ref, v_ref, o_ref, do_ref, lse_ref, dq_ref, dk_ref, dv_ref):
        q, do = q_ref[...], do_ref[...]
        delta = jnp.sum(do.astype(F32) * o_ref[...].astype(F32), axis=-1, keepdims=True)
        lse2 = lse_ref[...] * LOG2E
        first = jnp.logical_and(pl.program_id(1) == 0, pl.program_id(2) == 0)
        dq = None
        for j in range(T // tc):
            keys = slice(j * tc, (j + 1) * tc)
            kc, vc = k_ref[keys, :], v_ref[keys, :]
            s = lax.dot_general(q, kc, _NT, preferred_element_type=F32)
            p = jnp.exp2(s * c - lse2)
            dp = lax.dot_general(do, vc, _NT, preferred_element_type=F32)
            ds = (p * (dp - delta) * scale).astype(BF16)
            dqj = lax.dot_general(ds, kc, _NN, preferred_element_type=F32)
            dq = dqj if dq is None else dq + dqj
            _accumulate(dv_ref.at[keys, :], lax.dot_general(p.astype(BF16), do, _TN, preferred_element_type=F32), first)
            _accumulate(dk_ref.at[keys, :], lax.dot_general(ds, q, _TN, preferred_element_type=F32), first)
        dq_ref[...] = dq

    qmap = lambda kv, g, i: (i, kv * GROUP + g)
    return pl.pallas_call(
        body, name="attn_a_bwd", grid=(n_kv, GROUP, T // tq),
        in_specs=[pl.BlockSpec((tq, HEAD_DIM), qmap),
                  pl.BlockSpec((T, HEAD_DIM), lambda kv, g, i: (0, n_q + kv)),
                  pl.BlockSpec((T, HEAD_DIM), lambda kv, g, i: (0, n_q + n_kv + kv)),
                  pl.BlockSpec((tq, HEAD_DIM), qmap),
                  pl.BlockSpec((tq, HEAD_DIM), qmap),
                  pl.BlockSpec((None, tq, 1), lambda kv, g, i: (kv * GROUP + g, i, 0))],
        out_specs=[pl.BlockSpec((tq, HEAD_DIM), qmap),
                   pl.BlockSpec((T, HEAD_DIM), lambda kv, g, i: (0, kv)),
                   pl.BlockSpec((T, HEAD_DIM), lambda kv, g, i: (0, kv))],
        out_shape=[jax.ShapeDtypeStruct((T, n_q * HEAD_DIM), F32),
                   jax.ShapeDtypeStruct((T, n_kv * HEAD_DIM), F32),
                   jax.ShapeDtypeStruct((T, n_kv * HEAD_DIM), F32)],
        compiler_params=_params(("parallel", "arbitrary", "arbitrary")),
    )(pb, pb, pb, o_cat, d_o, lse)


def _bucket_index():
    r = np.arange(BLOCK_Q)[:, None]
    j = np.arange(3 * BLOCK_Q)[None, :]
    rel = (j - BLOCK_Q) - r
    nb = N_BUCKETS // 2
    ret = np.where(rel > 0, nb, 0)
    n = np.abs(rel)
    max_exact = nb // 2
    nf = np.maximum(n, 1).astype(np.float32)
    large = max_exact + (np.log(nf / max_exact) / math.log(MAX_DISTANCE / max_exact) * (nb - max_exact)).astype(np.int32)
    large = np.minimum(large, nb - 1)
    return jnp.asarray(ret + np.where(n < max_exact, n, large), jnp.int32)


def _bias_build(idx, table_flat, n_heads, deps=()):
    def body(idx_ref, tab_ref, o_ref):
        h = pl.program_id(0)
        iv = idx_ref[...]
        acc = jnp.zeros(iv.shape, F32)
        for b in range(N_BUCKETS):
            acc = jnp.where(iv == b, tab_ref[b * n_heads + h], acc)
        o_ref[...] = acc

    return _pcall(
        body, deps, name="bias_build", grid=(n_heads,),
        in_specs=[pl.BlockSpec(idx.shape, lambda h: (0, 0)), pl.BlockSpec(memory_space=pltpu.SMEM)],
        out_specs=pl.BlockSpec((None,) + idx.shape, lambda h: (h, 0, 0)),
        out_shape=jax.ShapeDtypeStruct((n_heads,) + idx.shape, F32),
        compiler_params=_params(("parallel",)),
    )(idx, table_flat)


def _band_mask(n, T):
    r = lax.broadcasted_iota(jnp.int32, (BLOCK_Q, 3 * BLOCK_Q), 0)
    j = lax.broadcasted_iota(jnp.int32, (BLOCK_Q, 3 * BLOCK_Q), 1)
    rel = (j - BLOCK_Q) - r
    kabs = n * BLOCK_Q + j - BLOCK_Q
    return (jnp.abs(rel) <= WINDOW) & (kabs >= 0) & (kabs < T)


def _band_specs(col, nblk):
    return [pl.BlockSpec((BLOCK_Q, HEAD_DIM), lambda kv, n: (jnp.maximum(n - 1, 0), col(kv))),
            pl.BlockSpec((BLOCK_Q, HEAD_DIM), lambda kv, n: (n, col(kv))),
            pl.BlockSpec((BLOCK_Q, HEAD_DIM), lambda kv, n: (jnp.minimum(n + 1, nblk - 1), col(kv)))]


def _attn_b_fwd(pb, bias, sink, q_off, n_q, n_kv, deps=()):
    T = pb.shape[0]
    nblk = T // BLOCK_Q
    scale = HEAD_DIM ** -0.5

    def body(*refs):
        q_refs = refs[0:GROUP]
        k_refs, v_refs = refs[GROUP:GROUP + 3], refs[GROUP + 3:GROUP + 6]
        bias_ref, sink_ref, o_ref, lse_ref = refs[GROUP + 6:]
        kv, n = pl.program_id(0), pl.program_id(1)
        kb = jnp.concatenate([r[...] for r in k_refs], axis=0)
        vb = jnp.concatenate([r[...] for r in v_refs], axis=0)
        mask = _band_mask(n, T)
        for g in range(GROUP):
            sk = sink_ref[kv * GROUP + g]
            s = lax.dot_general(q_refs[g][...], kb, _NT, preferred_element_type=F32) * scale + bias_ref[g]
            s = jnp.where(mask, s, NEG_INF)
            m = jnp.maximum(jnp.max(s, axis=-1, keepdims=True), sk)
            p = jnp.exp(s - m)
            l = jnp.sum(p, axis=-1, keepdims=True) + jnp.exp(sk - m)
            o = lax.dot_general(p.astype(BF16), vb, _NN, preferred_element_type=F32)
            o_ref[:, g * HEAD_DIM:(g + 1) * HEAD_DIM] = (o / l).astype(BF16)
            lse_ref[g] = m + jnp.log(l)

    q_specs = [pl.BlockSpec((BLOCK_Q, HEAD_DIM), functools.partial(lambda kv, n, g: (n, q_off + kv * GROUP + g), g=g))
               for g in range(GROUP)]
    return _pcall(
        body, deps, name="attn_b_fwd", grid=(n_kv, nblk),
        in_specs=[*q_specs,
                  *_band_specs(lambda kv: q_off + n_q + kv, nblk),
                  *_band_specs(lambda kv: q_off + n_q + n_kv + kv, nblk),
                  pl.BlockSpec((GROUP, BLOCK_Q, 3 * BLOCK_Q), lambda kv, n: (kv, 0, 0)),
                  pl.BlockSpec(memory_space=pltpu.SMEM)],
        out_specs=[pl.BlockSpec((BLOCK_Q, GROUP * HEAD_DIM), lambda kv, n: (n, kv)),
                   pl.BlockSpec((GROUP, BLOCK_Q, 1), lambda kv, n: (kv, n, 0))],
        out_shape=[jax.ShapeDtypeStruct((T, n_q * HEAD_DIM), BF16), jax.ShapeDtypeStruct((n_q, T, 1), F32)],
        compiler_params=_params(("parallel", "parallel")),
    )(*([pb] * (GROUP + 6)), bias, sink)


def _attn_b_bwd(pb, o_cat, d_o, lse, bias, sink, q_off, n_q, n_kv, o_off, deps=()):
    T = pb.shape[0]
    nblk = T // BLOCK_Q
    scale = HEAD_DIM ** -0.5

    def body(*refs):
        q_refs = refs[0:GROUP]
        k_refs, v_refs = refs[GROUP:GROUP + 3], refs[GROUP + 3:GROUP + 6]
        o_refs, do_refs = refs[GROUP + 6:2 * GROUP + 6], refs[2 * GROUP + 6:3 * GROUP + 6]
        lse_ref, bias_ref, sink_ref, dq_ref, dk_ref, dv_ref, dbias_ref, dsink_ref = refs[3 * GROUP + 6:]
        kv, n = pl.program_id(0), pl.program_id(1)
        first = n == 0
        kb = jnp.concatenate([r[...] for r in k_refs], axis=0)
        vb = jnp.concatenate([r[...] for r in v_refs], axis=0)
        mask = _band_mask(n, T)
        dkb = jnp.zeros((3 * BLOCK_Q, HEAD_DIM), F32)
        dvb = jnp.zeros((3 * BLOCK_Q, HEAD_DIM), F32)
        row = lax.broadcasted_iota(jnp.int32, (SUBLANES, LANES), 0)
        dsink = jnp.zeros((SUBLANES, LANES), F32)
        for g in range(GROUP):
            sk = sink_ref[kv * GROUP + g]
            q, do = q_refs[g][...], do_refs[g][...]
            lse_g = lse_ref[g]
            delta = jnp.sum(do.astype(F32) * o_refs[g][...].astype(F32), axis=-1, keepdims=True)
            s = lax.dot_general(q, kb, _NT, preferred_element_type=F32) * scale + bias_ref[g]
            s = jnp.where(mask, s, NEG_INF)
            p = jnp.exp(s - lse_g)
            dp = lax.dot_general(do, vb, _NT, preferred_element_type=F32)
            ds = p * (dp - delta)
            _accumulate(dbias_ref.at[g], ds, first)
            dsink = dsink + jnp.where(row == g, -jnp.sum(jnp.exp(sk - lse_g) * delta), 0.0)
            dsb = (ds * scale).astype(BF16)
            dq_ref[:, g * HEAD_DIM:(g + 1) * HEAD_DIM] = lax.dot_general(dsb, kb, _NN, preferred_element_type=F32)
            dkb = dkb + lax.dot_general(dsb, q, _TN, preferred_element_type=F32)
            dvb = dvb + lax.dot_general(p.astype(BF16), do, _TN, preferred_element_type=F32)
        _accumulate(dsink_ref, dsink, first)

        @pl.when(first)
        def _():
            dk_ref[...] = jnp.zeros(dk_ref.shape, F32)
            dv_ref[...] = jnp.zeros(dv_ref.shape, F32)

        blocks = (jnp.maximum(n - 1, 0), n, jnp.minimum(n + 1, nblk - 1))
        for t, blk in enumerate(blocks):
            rows = pl.ds(pl.multiple_of(blk * BLOCK_Q, BLOCK_Q), BLOCK_Q)
            dk_ref[rows, :] += dkb[t * BLOCK_Q:(t + 1) * BLOCK_Q]
            dv_ref[rows, :] += dvb[t * BLOCK_Q:(t + 1) * BLOCK_Q]

    def head_specs(base):
        return [pl.BlockSpec((BLOCK_Q, HEAD_DIM), functools.partial(lambda kv, n, g: (n, base + kv * GROUP + g), g=g))
                for g in range(GROUP)]

    return _pcall(
        body, deps, name="attn_b_bwd", grid=(n_kv, nblk),
        in_specs=[*head_specs(q_off),
                  *_band_specs(lambda kv: q_off + n_q + kv, nblk),
                  *_band_specs(lambda kv: q_off + n_q + n_kv + kv, nblk),
                  *head_specs(o_off), *head_specs(o_off),
                  pl.BlockSpec((GROUP, BLOCK_Q, 1), lambda kv, n: (kv, n, 0)),
                  pl.BlockSpec((GROUP, BLOCK_Q, 3 * BLOCK_Q), lambda kv, n: (kv, 0, 0)),
                  pl.BlockSpec(memory_space=pltpu.SMEM)],
        out_specs=[pl.BlockSpec((BLOCK_Q, GROUP * HEAD_DIM), lambda kv, n: (n, kv)),
                   pl.BlockSpec((T, HEAD_DIM), lambda kv, n: (0, kv)),
                   pl.BlockSpec((T, HEAD_DIM), lambda kv, n: (0, kv)),
                   pl.BlockSpec((GROUP, BLOCK_Q, 3 * BLOCK_Q), lambda kv, n: (kv, 0, 0)),
                   pl.BlockSpec((None, SUBLANES, LANES), lambda kv, n: (kv, 0, 0))],
        out_shape=[jax.ShapeDtypeStruct((T, n_q * HEAD_DIM), F32),
                   jax.ShapeDtypeStruct((T, n_kv * HEAD_DIM), F32),
                   jax.ShapeDtypeStruct((T, n_kv * HEAD_DIM), F32),
                   jax.ShapeDtypeStruct((n_q, BLOCK_Q, 3 * BLOCK_Q), F32),
                   jax.ShapeDtypeStruct((n_kv, SUBLANES, LANES), F32)],
        compiler_params=_params(("parallel", "arbitrary")),
    )(*([pb] * (GROUP + 6)), *([o_cat] * GROUP), *([d_o] * GROUP), lse, bias, sink)


def _table_grads(dbias, dsink_raw, idx):
    n_heads = dbias.shape[0]
    n_kv = dsink_raw.shape[0]

    def body(db_ref, ds_ref, idx_ref, dt_ref, dsk_ref):
        iv = idx_ref[...]
        row = lax.broadcasted_iota(jnp.int32, (SUBLANES, LANES), 0)
        lane = lax.broadcasted_iota(jnp.int32, (SUBLANES, LANES), 1)
        dsk = jnp.zeros((SUBLANES, LANES), F32)
        for h in range(n_heads):
            d = db_ref[h]
            acc = jnp.zeros((SUBLANES, LANES), F32)
            for b in range(N_BUCKETS):
                acc = jnp.where((row == 0) & (lane == b), jnp.sum(jnp.where(iv == b, d, 0.0)), acc)
            dt_ref[:, h * LANES:(h + 1) * LANES] = acc
            raw = ds_ref[h // GROUP]
            val = jnp.sum(jnp.where((row == h % GROUP) & (lane == 0), raw, 0.0))
            dsk = jnp.where((row == 0) & (lane == h), val, dsk)
        dsk_ref[...] = dsk

    return pl.pallas_call(
        body, name="table_grads",
        in_specs=[pl.BlockSpec(memory_space=pltpu.VMEM)] * 3,
        out_specs=[pl.BlockSpec(memory_space=pltpu.VMEM)] * 2,
        out_shape=[jax.ShapeDtypeStruct((SUBLANES, n_heads * LANES), F32),
                   jax.ShapeDtypeStruct((SUBLANES, LANES), F32)],
        compiler_params=pltpu.CompilerParams(vmem_limit_bytes=56 * 1024 * 1024),
    )(dbias, dsink_raw, idx)


def _position():
    x, y, c = lax.axis_index("x"), lax.axis_index("y"), lax.axis_index("c")
    return x, y, c


def _hbm(a):
    return pltpu.with_memory_space_constraint(a, pltpu.HBM)


def _split_start(name, bufs, sem_shapes, issue):
    nb, ns = len(bufs), len(sem_shapes)

    def body(*refs):
        buf_refs = refs[:nb]
        sems = refs[nb:nb + ns]
        token = refs[nb + ns + nb]
        issue(buf_refs, sems)
        token[...] = jnp.zeros(token.shape, F32)

    outs = pl.pallas_call(
        body, name=name,
        in_specs=[_HBM] * nb,
        out_specs=[_SEM] * ns + [_HBM] * nb + [_VMEM],
        out_shape=[pltpu.SemaphoreType.DMA(s) for s in sem_shapes] + [pltpu.HBM(b.shape, b.dtype) for b in bufs]
        + [jax.ShapeDtypeStruct((SUBLANES, LANES), F32)],
        input_output_aliases={i: ns + i for i in range(nb)},
        compiler_params=pltpu.CompilerParams(has_side_effects=_EFFECT),
    )(*[_hbm(b) for b in bufs])
    return outs[:ns], outs[ns:ns + nb], outs[-1]


def _split_wait(name, bufs, send, recv, counts, size_of, after):
    nb = len(bufs)

    def body(*refs):
        buf_refs = refs[:nb]
        send_ref, recv_ref = refs[nb], refs[nb + 1]
        x, y, c = _position()
        for w, n in enumerate(counts):
            ref = size_of(buf_refs, w)
            for k in range(n):
                s = sum(counts[:w]) + k
                cp = pltpu.make_async_remote_copy(
                    src_ref=ref, dst_ref=ref, send_sem=send_ref.at[s], recv_sem=recv_ref.at[s],
                    device_id=(x, y, c), device_id_type=MESH)
                cp.wait_send()
                cp.wait_recv()

    return pl.pallas_call(
        body, name=name,
        in_specs=[_HBM] * nb + [_SEM, _SEM, _ANY],
        out_specs=[_HBM] * nb,
        out_shape=[pltpu.HBM(b.shape, b.dtype) for b in bufs],
        input_output_aliases={i: i for i in range(nb)},
        compiler_params=pltpu.CompilerParams(has_side_effects=_EFFECT),
    )(*bufs, send, recv, after)


def _block_of(pos):
    return 4 * pos[0] + 2 * pos[1] + pos[2]


def _shard_of(ref, blk, by_cols):
    aligned = (lambda v, a: v) if isinstance(blk, int) else pl.multiple_of
    if by_cols:
        n = ref.shape[1] // N_DEV
        return ref.at[:, pl.ds(aligned(blk * n, LANES), n)]
    r = ref.shape[0] // N_DEV
    return ref.at[pl.ds(aligned(blk * r, SUBLANES), r), :]


def _gather_start(shards, by_cols, groups):
    nw = len(shards)
    mine = _block_of(_position())
    lands = []
    for s, cols in zip(shards, by_cols):
        r, n = s.shape
        full = lax.empty((r, n * N_DEV) if cols else (r * N_DEV, n), s.dtype)
        lands.append(lax.dynamic_update_slice(full, s, (0, mine * n) if cols else (mine * r, 0)))

    def issue(bufs, sems):
        ins, land = bufs[:nw], bufs[nw:]
        x, y, c = _position()
        peers = [(x, y, 1 - c), (1 - x, y, c), (x, 1 - y, c), (1 - x, 1 - y, c)]
        for gi, grp in enumerate(groups):
            for wi, w in enumerate(grp):
                dst = _shard_of(land[w], _block_of((x, y, c)), by_cols[w])
                for k, peer in enumerate(peers):
                    pltpu.make_async_remote_copy(
                        src_ref=ins[w], dst_ref=dst, send_sem=sems[2 * gi].at[4 * wi + k],
                        recv_sem=sems[2 * gi + 1].at[4 * wi + k], device_id=peer, device_id_type=MESH).start()

    sem_shapes = [(4 * len(g),) for g in groups for _ in range(2)]
    sems, thru, token = _split_start("gather_start", list(shards) + lands, sem_shapes, issue)
    return sems, thru[:nw], thru[nw:], token


def _gather_forward(name, lands, by_cols):
    nw = len(lands)

    def issue(land, sems):
        x, y, c = _position()
        for w in range(nw):
            for k, chip in enumerate([(1 - x, y), (x, 1 - y), (1 - x, 1 - y)]):
                blk = _shard_of(land[w], _block_of((*chip, c)), by_cols[w])
                pltpu.make_async_remote_copy(
                    src_ref=blk, dst_ref=blk, send_sem=sems[0].at[3 * w + k], recv_sem=sems[1].at[3 * w + k],
                    device_id=(x, y, 1 - c), device_id_type=MESH).start()

    return _split_start(name, lands, [(3 * nw,), (3 * nw,)], issue)


def _first_block(bufs, w, offset=0):
    return bufs[offset + w].at[0]


def _pair_start(name, grads, by_cols):
    nw = len(grads)
    lands = []
    for g, cols in zip(grads, by_cols):
        shard = (g.shape[0], g.shape[1] // N_DEV) if cols else (g.shape[0] // N_DEV, g.shape[1])
        lands.append(lax.empty((N_CHIP,) + shard, g.dtype))

    def issue(bufs, sems):
        x, y, c = _position()
        for w in range(nw):
            for q in range(N_CHIP):
                pltpu.make_async_remote_copy(
                    src_ref=_shard_of(bufs[w], 2 * q + 1 - c, by_cols[w]), dst_ref=bufs[nw + w].at[q],
                    send_sem=sems[0].at[N_CHIP * w + q], recv_sem=sems[1].at[N_CHIP * w + q],
                    device_id=(x, y, 1 - c), device_id_type=MESH).start()

    return _split_start(name, list(grads) + lands, [(N_CHIP * nw,), (N_CHIP * nw,)], issue)


def _chip_start(name, sums):
    nw = len(sums)
    x, y, _ = _position()
    mine = 2 * x + y
    lands = [lax.dynamic_update_slice(lax.empty(s.shape, s.dtype), lax.dynamic_slice_in_dim(s, mine, 1, 0), (mine, 0, 0))
             for s in sums]

    def issue(bufs, sems):
        x, y, c = _position()
        for w in range(nw):
            for k, (px, py) in enumerate([(1 - x, y), (x, 1 - y), (1 - x, 1 - y)]):
                pltpu.make_async_remote_copy(
                    src_ref=bufs[w].at[2 * px + py], dst_ref=bufs[nw + w].at[2 * x + y], send_sem=sems[0].at[3 * w + k],
                    recv_sem=sems[1].at[3 * w + k], device_id=(px, py, c), device_id_type=MESH).start()

    return _split_start(name, list(sums) + lands, [(3 * nw,), (3 * nw,)], issue)


def _pair_sum(name, grad, landed, by_cols, tr=256):
    _, R, C = landed.shape
    tr = _tile(R, tr)
    core = lax.axis_index("c").astype(jnp.int32).reshape(1)

    def body(c_ref, g_ref, l_ref, o_ref):
        o_ref[...] = (g_ref[...] + l_ref[...]).astype(BF16)

    if by_cols:
        mine = pl.BlockSpec((tr, C), lambda q, i, c_ref: (i, 2 * q + c_ref[0]))
    else:
        mine = pl.BlockSpec((tr, C), lambda q, i, c_ref: ((2 * q + c_ref[0]) * (R // tr) + i, 0))
    slot = pl.BlockSpec((None, tr, C), lambda q, i, c_ref: (q, i, 0))
    return pl.pallas_call(
        body, name=name,
        grid_spec=pltpu.PrefetchScalarGridSpec(
            num_scalar_prefetch=1, grid=(N_CHIP, R // tr),
            in_specs=[mine, slot],
            out_specs=slot),
        out_shape=jax.ShapeDtypeStruct((N_CHIP, R, C), BF16),
        compiler_params=_params(("parallel", "parallel")),
    )(core, grad, landed)


def _adam(w, g, m, v):
    m = ADAM_B1 * m + (1.0 - ADAM_B1) * g
    v = ADAM_B2 * v + (1.0 - ADAM_B2) * (g * g)
    m_hat = m / (1.0 - ADAM_B1 ** ADAM_STEP)
    v_hat = v / (1.0 - ADAM_B2 ** ADAM_STEP)
    delta = -ADAM_LR * (m_hat / (jnp.sqrt(v_hat) + ADAM_EPS) + ADAM_WD * w)
    return delta, m, v


def _sum_adam(name, landed, w, m, v, tr=256):
    R, C = w.shape
    tr = _tile(R, tr)

    def body(l_ref, w_ref, m_ref, v_ref, g_ref, d_ref, nm_ref, nv_ref):
        g = l_ref[0].astype(F32)
        for q in range(1, N_CHIP):
            g = g + l_ref[q].astype(F32)
        g_ref[...] = g
        d_ref[...], nm_ref[...], nv_ref[...] = _adam(w_ref[...], g, m_ref[...], v_ref[...])

    tile = pl.BlockSpec((tr, C), lambda i: (i, 0))
    return pl.pallas_call(
        body, name=name, grid=(R // tr,),
        in_specs=[pl.BlockSpec((N_CHIP, tr, C), lambda i: (0, i, 0)), tile, tile, tile],
        out_specs=[tile] * 4, out_shape=[jax.ShapeDtypeStruct((R, C), F32)] * 4,
        compiler_params=_params(("parallel",)),
    )(landed, w, m, v)


def _small_all_reduce(parts):
    W = parts.shape[1]

    def body(p_ref, o_ref, slots, send_sems, recv_sems):
        x, y, c = _position()
        me = 4 * x + 2 * y + c
        slots[me] = jnp.sum(p_ref[...], axis=0, keepdims=True)
        peers = [(x, y, 1 - c), (1 - x, y, c), (1 - x, y, 1 - c), (x, 1 - y, c), (x, 1 - y, 1 - c),
                 (1 - x, 1 - y, c), (1 - x, 1 - y, 1 - c)]
        copies = []
        for k, peer in enumerate(peers):
            cp = pltpu.make_async_remote_copy(
                src_ref=slots.at[me], dst_ref=slots.at[me], send_sem=send_sems.at[k], recv_sem=recv_sems.at[k],
                device_id=peer, device_id_type=MESH)
            cp.start()
            copies.append(cp)
        for cp in copies:
            cp.wait()
        total = slots[0]
        for d in range(1, N_DEV):
            total = total + slots[d]
        o_ref[...] = total

    return pl.pallas_call(
        body, name="small_all_reduce",
        in_specs=[pl.BlockSpec(memory_space=pltpu.VMEM)], out_specs=pl.BlockSpec(memory_space=pltpu.VMEM),
        out_shape=jax.ShapeDtypeStruct((1, W), F32),
        scratch_shapes=[pltpu.VMEM((N_DEV, 1, W), F32), pltpu.SemaphoreType.DMA((7,)), pltpu.SemaphoreType.DMA((7,))],
    )(parts)


def _adam_small(w, g, m, v):
    def body(w_ref, g_ref, m_ref, v_ref, d_ref, nm_ref, nv_ref):
        d_ref[...], nm_ref[...], nv_ref[...] = _adam(w_ref[...], g_ref[...], m_ref[...], v_ref[...])

    return pl.pallas_call(
        body, name="adam_small",
        in_specs=[pl.BlockSpec(memory_space=pltpu.VMEM)] * 4, out_specs=[pl.BlockSpec(memory_space=pltpu.VMEM)] * 3,
        out_shape=[jax.ShapeDtypeStruct(w.shape, F32)] * 3,
    )(w, g, m, v)


_GATHER_GROUPS = (("w_in",), ("w_out", "w_up", "ple_w"), ("w_down", "w_gate"))
_COL_SHARDED = ("w_in", "w_up", "ple_w")


class _MeshComm:
    def __init__(self, w, mom, var):
        self.w, self.mom, self.var = w, mom, var
        self.out = {}
        self._pairs, self._chips = {}, {}

    def gather_begin(self):
        names = [n for g in _GATHER_GROUPS for n in g]
        self._idx = {n: i for i, n in enumerate(names)}
        groups = [[self._idx[n] for n in g] for g in _GATHER_GROUPS]
        self._sems, self._src, self._lands, token = _gather_start(
            [self.w[n].astype(BF16) for n in names], [n in _COL_SHARDED for n in names], groups)
        return token

    @staticmethod
    def _shard_size(names, offset):
        return lambda bufs, w: _shard_of(bufs[offset + w], 0, names[w] in _COL_SHARDED)

    def gather_arrive(self, gi, after):
        names = _GATHER_GROUPS[gi]
        ids = [self._idx[n] for n in names]
        bufs = [self._src[i] for i in ids] + [self._lands[i] for i in ids]
        out = _split_wait("gather_arrive%d" % gi, bufs, self._sems[2 * gi], self._sems[2 * gi + 1], [4] * len(ids),
                          self._shard_size(names, len(ids)), after)
        self._arrived = out[len(ids):]

    def gather_forward(self, gi):
        by_cols = [n in _COL_SHARDED for n in _GATHER_GROUPS[gi]]
        self._fsems, self._fthru, token = _gather_forward("gather_forward%d" % gi, self._arrived, by_cols)
        return token

    def gather_finish(self, gi, after):
        names = _GATHER_GROUPS[gi]
        out = _split_wait("gather_finish%d" % gi, self._fthru, self._fsems[0], self._fsems[1], [3] * len(names),
                          self._shard_size(names, 0), after)
        return dict(zip(names, out))

    def reduce_begin(self, key, grads):
        names = list(grads)
        sems, thru, token = _pair_start("pair_start_" + key, [grads[n] for n in names],
                                        [n in _COL_SHARDED for n in names])
        self._pairs[key] = (names, sems, thru)
        return token

    def reduce_middle(self, key, after):
        names, sems, thru = self._pairs[key]
        nw = len(names)
        out = _split_wait("pair_wait_" + key, thru, sems[0], sems[1], [N_CHIP] * nw,
                          functools.partial(_first_block, offset=nw), after)
        sums = [_pair_sum("pair_sum_" + n, out[i], out[nw + i], n in _COL_SHARDED) for i, n in enumerate(names)]
        sems2, thru2, token = _chip_start("chip_start_" + key, sums)
        self._chips[key] = (names, sems2, thru2)
        return token

    def reduce_finish(self, key, after):
        names, sems, thru = self._chips[key]
        nw = len(names)
        out = _split_wait("chip_wait_" + key, thru, sems[0], sems[1], [3] * nw,
                          functools.partial(_first_block, offset=nw), after)
        for i, n in enumerate(names):
            self.out[n] = _sum_adam("adam_" + n, out[nw + i], self.w[n], self.mom[n], self.var[n])


def _step(x, p, target, gains, comm):
    T, D = x.shape
    n_q = D // (2 * HEAD_DIM)
    n_kv = n_q // GROUP
    cos, sin = _rope_tables(T)
    idx = _bucket_index()

    t = comm.gather_begin()
    u = _rms_fwd("norm_attn", x, gains["attn_norm_g"], deps=(t,))
    comm.gather_arrive(0, u)
    t = comm.gather_forward(0)
    bias = _bias_build(idx, gains["rel_bias_table"].reshape(-1), n_q, deps=(t,))
    full = comm.gather_finish(0, bias)
    proj = _mm_nn("in_proj", u, full["w_in"])
    pb = _qk_prep(proj, cos, sin, gains["q_norm_g"], gains["k_norm_g"], n_q + n_kv)
    o_a, lse_a = _attn_a_fwd(pb, n_q, n_kv)
    comm.gather_arrive(1, o_a)
    t = comm.gather_forward(1)
    sink = gains["sink_logits"].reshape(-1)
    b_off = n_q + 2 * n_kv
    o_b, lse_b = _attn_b_fwd(pb, bias, sink, b_off, n_q, n_kv, deps=(t,))
    full.update(comm.gather_finish(1, o_b))
    o_cat = jnp.concatenate([o_a, o_b], axis=1)
    h1 = _mm_nn("out_proj", o_cat, full["w_out"], epilogue=_store_add, extras=(x,))
    m_in = _rms_fwd("norm_mlp", h1, gains["mlp_norm_g"])

    def up_epilogue(acc, extra, outs):
        outs[0][...] = acc.astype(BF16)
        r = jnp.maximum(acc, 0.0)
        outs[1][...] = (r * r).astype(BF16)

    a_act, f_act = _mm_nn("up_proj", m_in, full["w_up"], epilogue=up_epilogue, out_dtypes=[BF16, BF16])
    comm.gather_arrive(2, f_act)
    t = comm.gather_forward(2)
    p_b = p.astype(BF16)
    pe = _mm_nn("ple_proj", p_b, full["ple_w"], deps=(t,))
    full.update(comm.gather_finish(2, pe))
    h2 = _mm_nn("down_proj", f_act, full["w_down"], epilogue=_store_add, extras=(h1,), tm=512, tn=512)
    gn = _rms_fwd("norm_gate", h2, gains["gate_norm_g"])
    z = _mm_nn("gate_proj", gn, full["w_gate"])

    dh3, dz, dpe, dg_final, dg_ple, loss_part = _tail(h2, z, pe, target, gains["ple_norm_g"], gains["final_norm_g"])
    gw_gate = _mm_tn("grad_w_gate", gn, dz)
    gw_ple = _mm_tn("grad_ple_w", p_b, dpe)
    t = comm.reduce_begin("a", dict(w_gate=gw_gate, ple_w=gw_ple))
    dgn = _mm_nt("d_gate_in", dz, full["w_gate"], deps=(t,))
    dh2, dh2_b, dg_gate = _rms_bwd("norm_gate_bwd", dgn, h2, gains["gate_norm_g"], dh3)
    t = comm.reduce_middle("a", dh2_b)
    gw_down = _mm_tn("grad_w_down", f_act, dh2_b, deps=(t,))
    t = comm.reduce_begin("b", dict(w_down=gw_down))

    def act_bwd(acc, extra, outs):
        outs[0][...] = (acc * (2.0 * jnp.maximum(extra[0][...].astype(F32), 0.0))).astype(BF16)

    da = _mm_nt("d_act", dh2_b, full["w_down"], out_dtype=BF16, epilogue=act_bwd, extras=(a_act,), deps=(t,))
    comm.reduce_finish("a", da)
    t = comm.reduce_middle("b", da)
    gw_up = _mm_tn("grad_w_up", m_in, da, deps=(t,))
    t = comm.reduce_begin("c", dict(w_up=gw_up))
    dm = _mm_nt("d_mlp_in", da, full["w_up"], tm=512, tn=512, deps=(t,))
    t = comm.reduce_middle("c", dm)
    dh1, dh1_b, dg_mlp = _rms_bwd("norm_mlp_bwd", dm, h1, gains["mlp_norm_g"], dh2, deps=(t,))
    comm.reduce_finish("b", dh1_b)
    gw_out = _mm_tn("grad_w_out", o_cat, dh1_b)
    t = comm.reduce_begin("d", dict(w_out=gw_out))
    d_o = _mm_nt("d_attn_out", dh1_b, full["w_out"], out_dtype=BF16, deps=(t,))
    dqa, dka, dva = _attn_a_bwd(pb, o_cat, d_o, lse_a, n_q, n_kv)
    t = comm.reduce_middle("d", dqa)
    dqb, dkb, dvb, dbias, dsink_raw = _attn_b_bwd(pb, o_cat, d_o, lse_b, bias, sink, b_off, n_q, n_kv, n_q, deps=(t,))
    comm.reduce_finish("c", dqb)
    comm.reduce_finish("d", dqb)
    dtable, dsink = _table_grads(dbias, dsink_raw, idx)
    dproj, dg_q, dg_k = _dproj(proj, dqa, dka, dva, dqb, dkb, dvb, cos, sin, gains["q_norm_g"], gains["k_norm_g"])
    gw_in = _mm_tn("grad_w_in", u, dproj)
    t = comm.reduce_begin("e", dict(w_in=gw_in))
    du = _mm_nt("d_attn_in", dproj, full["w_in"], deps=(t,))
    t = comm.reduce_middle("e", du)
    dx, _, dg_attn = _rms_bwd("norm_attn_bwd", du, x, gains["attn_norm_g"], dh1, deps=(t,))
    comm.reduce_finish("e", dx)

    parts = jnp.concatenate([dg_attn, dg_mlp, dg_ple, dg_gate, dg_final, dg_q, dg_k, dtable, dsink, loss_part], axis=1)
    return dx, parts


_SHARDED = ("w_in", "w_out", "w_up", "w_down", "ple_w", "w_gate")
_VECTORS = ("attn_norm_g", "mlp_norm_g", "ple_norm_g", "gate_norm_g", "final_norm_g")
_ORDER = ("attn_norm_g", "w_in", "q_norm_g", "k_norm_g", "sink_logits", "w_out", "mlp_norm_g", "w_up", "w_down",
          "ple_w", "ple_norm_g", "gate_norm_g", "w_gate", "rel_bias_table", "final_norm_g")


def _pack_small(vals, n_heads):
    lane_pad = lambda v: jnp.pad(v, ((0, 0), (0, LANES - v.shape[1])))
    table = lane_pad(vals["rel_bias_table"].T).reshape(1, n_heads * LANES)
    return jnp.concatenate(
        [vals[n].reshape(1, -1) for n in _VECTORS] + [vals["q_norm_g"], vals["k_norm_g"], table,
                                                      lane_pad(vals["sink_logits"]), jnp.zeros((1, LANES), F32)], axis=1)


def _unpack_small(row, like, n_heads):
    out, off = {}, 0
    for n in _VECTORS:
        out[n] = row[:, off:off + like[n].size].reshape(like[n].shape)
        off += like[n].size
    for n in ("q_norm_g", "k_norm_g"):
        out[n] = row[:, off:off + LANES]
        off += LANES
    out["rel_bias_table"] = row[:, off:off + n_heads * LANES].reshape(n_heads, LANES)[:, :N_BUCKETS].T
    off += n_heads * LANES
    out["sink_logits"] = row[:, off:off + n_heads]
    off += LANES
    return out, row[0, off]


def kernel(x, p, attn_norm_g, w_in, q_norm_g, k_norm_g, sink_logits, w_out, mlp_norm_g, w_up, w_down, ple_w, ple_norm_g, gate_norm_g, w_gate, rel_bias_table, final_norm_g, loss_target, m_attn_norm_g, m_w_in, m_q_norm_g, m_k_norm_g, m_sink_logits, m_w_out, m_mlp_norm_g, m_w_up, m_w_down, m_ple_w, m_ple_norm_g, m_gate_norm_g, m_w_gate, m_rel_bias_table, m_final_norm_g, v_attn_norm_g, v_w_in, v_q_norm_g, v_k_norm_g, v_sink_logits, v_w_out, v_mlp_norm_g, v_w_up, v_w_down, v_ple_w, v_ple_norm_g, v_gate_norm_g, v_w_gate, v_rel_bias_table, v_final_norm_g):
    w = dict(attn_norm_g=attn_norm_g, w_in=w_in[0], q_norm_g=q_norm_g, k_norm_g=k_norm_g, sink_logits=sink_logits,
             w_out=w_out[0], mlp_norm_g=mlp_norm_g, w_up=w_up[0], w_down=w_down[0], ple_w=ple_w[0],
             ple_norm_g=ple_norm_g, gate_norm_g=gate_norm_g, w_gate=w_gate[0], rel_bias_table=rel_bias_table,
             final_norm_g=final_norm_g)
    mom = dict(attn_norm_g=m_attn_norm_g, w_in=m_w_in[0], q_norm_g=m_q_norm_g, k_norm_g=m_k_norm_g,
               sink_logits=m_sink_logits, w_out=m_w_out[0], mlp_norm_g=m_mlp_norm_g, w_up=m_w_up[0],
               w_down=m_w_down[0], ple_w=m_ple_w[0], ple_norm_g=m_ple_norm_g, gate_norm_g=m_gate_norm_g,
               w_gate=m_w_gate[0], rel_bias_table=m_rel_bias_table, final_norm_g=m_final_norm_g)
    var = dict(attn_norm_g=v_attn_norm_g, w_in=v_w_in[0], q_norm_g=v_q_norm_g, k_norm_g=v_k_norm_g,
               sink_logits=v_sink_logits, w_out=v_w_out[0], mlp_norm_g=v_mlp_norm_g, w_up=v_w_up[0],
               w_down=v_w_down[0], ple_w=v_ple_w[0], ple_norm_g=v_ple_norm_g, gate_norm_g=v_gate_norm_g,
               w_gate=v_w_gate[0], rel_bias_table=v_rel_bias_table, final_norm_g=v_final_norm_g)
    D = x.shape[-1]
    n_heads = D // (2 * HEAD_DIM)

    gains = {n: w[n] for n in w if n not in _SHARDED}
    gains["final_norm_g"] = final_norm_g.reshape(1, -1)

    comm = _MeshComm(w, mom, var)
    dx, parts = _step(x[0], p[0, 0], loss_target[0], gains, comm)

    g_out, d_out, m_out, v_out = {}, {}, {}, {}
    for n in _SHARDED:
        g, d, nm, nv = comm.out[n]
        g_out[n], d_out[n], m_out[n], v_out[n] = g[None], d[None], nm[None], nv[None]

    small_g = _small_all_reduce(parts)
    small = {n: v for n, v in w.items() if n not in _SHARDED}
    pack = lambda vals: _pack_small({n: vals[n] for n in small}, n_heads)
    sd, sm, sv = _adam_small(pack(w), small_g, pack(mom), pack(var))
    sg, loss = _unpack_small(small_g, small, n_heads)
    g_out.update(sg)
    for dst, row in ((d_out, sd), (m_out, sm), (v_out, sv)):
        dst.update(_unpack_small(row, small, n_heads)[0])

    return (loss, dx[None], *[g_out[n] for n in _ORDER], *[d_out[n] for n in _ORDER],
            *[m_out[n] for n in _ORDER], *[v_out[n] for n in _ORDER])
```

```python
import functools
import math

import numpy as np
import jax
import jax.numpy as jnp
from jax import lax
from jax.experimental import pallas as pl
from jax.experimental.pallas import tpu as pltpu

F32 = jnp.float32
BF16 = jnp.bfloat16

N_DEV = 8
N_CHIP = 4
HEAD_DIM = 128
GROUP = 4
GRID_W = 64
WINDOW = 128
BLOCK_Q = 128
N_BUCKETS = 32
MAX_DISTANCE = 128
ROPE_THETA = 10000.0
EPS = 1e-6
NEG_INF = -1e30
ADAM_LR = 0.001
ADAM_B1 = 0.9
ADAM_B2 = 0.999
ADAM_EPS = 1e-08
ADAM_WD = 0.01
ADAM_STEP = 10
LOG2E = math.log2(math.e)
LANES = 128
SUBLANES = 8
MESH = pl.DeviceIdType.MESH

_NT = (((1,), (1,)), ((), ()))
_NN = (((1,), (0,)), ((), ()))
_TN = (((0,), (0,)), ((), ()))


def _tile(dim, pref):
    return pref if dim % pref == 0 else dim


def _params(sem):
    return pltpu.CompilerParams(dimension_semantics=sem, vmem_limit_bytes=56 * 1024 * 1024)


_HBM = pl.BlockSpec(memory_space=pltpu.HBM)
_SEM = pl.BlockSpec(memory_space=pltpu.SEMAPHORE)
_ANY = pl.BlockSpec(memory_space=pl.ANY)
_VMEM = pl.BlockSpec(memory_space=pltpu.VMEM)
_EFFECT = pltpu.SideEffectType.DATAFLOW_SIDE_EFFECTING


def _pcall(body, deps=(), *, in_specs, **kw):
    deps = [d for d in deps if d is not None]
    nd = len(deps)

    def wrapped(*refs):
        body(*refs[nd:])

    call = pl.pallas_call(wrapped, in_specs=[_ANY] * nd + list(in_specs), **kw)
    return lambda *args: call(*deps, *args)


def _mm(name, a, b, dims, grid, a_spec, b_spec, out_shape, out_specs, acc_shape, epilogue,
        extras=(), extra_specs=(), deps=()):
    nk = grid[2]
    n_extra = len(extras)

    def body(*refs):
        a_ref, b_ref = refs[0], refs[1]
        extra = refs[2:2 + n_extra]
        outs = refs[2 + n_extra:-1]
        acc = refs[-1]
        part = lax.dot_general(a_ref[...], b_ref[...], dims, preferred_element_type=F32)
        if nk == 1:
            epilogue(part, extra, outs)
        else:
            k = pl.program_id(2)

            @pl.when(k == 0)
            def _():
                acc[...] = part

            @pl.when(k > 0)
            def _():
                acc[...] += part

            @pl.when(k == nk - 1)
            def _():
                epilogue(acc[...], extra, outs)

    return _pcall(
        body, deps, name=name, grid=grid,
        in_specs=[a_spec, b_spec, *extra_specs],
        out_specs=out_specs, out_shape=out_shape,
        scratch_shapes=[pltpu.VMEM(acc_shape if nk > 1 else (SUBLANES, LANES), F32)],
        compiler_params=_params(("parallel", "parallel", "arbitrary")),
    )(a, b, *extras)


def _store(dtype):
    def ep(acc, extra, outs):
        outs[0][...] = acc.astype(dtype)
    return ep


def _store_add(acc, extra, outs):
    outs[0][...] = acc + extra[0][...]


def _mm_nn(name, a, b, out_dtype=F32, epilogue=None, extras=(), n_out=1, out_dtypes=None, tm=1024, tn=1024, tk=None,
           deps=()):
    M, K = a.shape
    N = b.shape[1]
    tm, tn, tk = _tile(M, tm), _tile(N, tn), _tile(K, tk or K)
    b_spec = pl.BlockSpec((tk, tn), lambda i, j, k: (k, j))
    grid = (M // tm, N // tn, K // tk)
    o_spec = pl.BlockSpec((tm, tn), lambda i, j, k: (i, j))
    out_dtypes = out_dtypes or [out_dtype] * n_out
    out_shape = [jax.ShapeDtypeStruct((M, N), d) for d in out_dtypes]
    res = _mm(name, a, b, _NN, grid, pl.BlockSpec((tm, tk), lambda i, j, k: (i, k)), b_spec,
              out_shape, [o_spec] * len(out_dtypes), (tm, tn), epilogue or _store(out_dtype),
              extras, [o_spec] * len(extras), deps)
    return res if len(out_dtypes) > 1 else res[0]


def _mm_nt(name, a, b, out_dtype=F32, epilogue=None, extras=(), tm=1024, tn=1024, tk=None, deps=()):
    M, C = a.shape
    N = b.shape[0]
    tm, tn, tk = _tile(M, tm), _tile(N, tn), _tile(C, tk or C)
    b_spec = pl.BlockSpec((tn, tk), lambda i, j, k: (j, k))
    grid = (M // tm, N // tn, C // tk)
    o_spec = pl.BlockSpec((tm, tn), lambda i, j, k: (i, j))
    return _mm(name, a, b, _NT, grid, pl.BlockSpec((tm, tk), lambda i, j, k: (i, k)), b_spec,
               [jax.ShapeDtypeStruct((M, N), out_dtype)], [o_spec], (tm, tn), epilogue or _store(out_dtype),
               extras, [o_spec] * len(extras), deps)[0]


def _mm_tn(name, a, b, tm=1024, tn=512, tk=None, deps=()):
    T, M = a.shape
    N = b.shape[1]
    tm, tn, tk = _tile(M, tm), _tile(N, tn), _tile(T, tk or T)
    out_shape = jax.ShapeDtypeStruct((M, N), F32)
    o_spec = pl.BlockSpec((tm, tn), lambda i, j, k: (i, j))
    grid = (M // tm, N // tn, T // tk)
    return _mm(name, a, b, _TN, grid, pl.BlockSpec((tk, tm), lambda i, j, k: (k, i)),
               pl.BlockSpec((tk, tn), lambda i, j, k: (k, j)), [out_shape], [o_spec], (tm, tn), _store(F32),
               deps=deps)[0]


def _mean_last(v):
    return jnp.mean(v, axis=-1, keepdims=True)


def _rows_to_sublanes(v):
    r, c = v.shape
    return jnp.sum(v.reshape(r // SUBLANES, SUBLANES, c), axis=0)


def _accumulate(ref, val, first):
    @pl.when(first)
    def _():
        ref[...] = val

    @pl.when(jnp.logical_not(first))
    def _():
        ref[...] += val


def _rms_fwd(name, x, g, tr=256, deps=()):
    T, D = x.shape
    tr = _tile(T, tr)

    def body(x_ref, g_ref, o_ref):
        xv = x_ref[...]
        r = lax.rsqrt(_mean_last(xv * xv) + EPS)
        o_ref[...] = (xv * r * g_ref[...]).astype(BF16)

    row = pl.BlockSpec((tr, D), lambda i: (i, 0))
    return _pcall(
        body, deps, name=name, grid=(T // tr,),
        in_specs=[row, pl.BlockSpec((1, D), lambda i: (0, 0))],
        out_specs=row, out_shape=jax.ShapeDtypeStruct((T, D), BF16),
        compiler_params=_params(("parallel",)),
    )(x, g)


def _rms_bwd(name, dyn, x, g, dres, tr=256, deps=()):
    T, D = x.shape
    tr = _tile(T, tr)

    def body(dy_ref, x_ref, g_ref, dr_ref, dx_ref, dxb_ref, dg_ref):
        xv = x_ref[...]
        r = lax.rsqrt(_mean_last(xv * xv) + EPS)
        xn = xv * r
        dy = dy_ref[...]
        dxn = dy * g_ref[...]
        dx = dr_ref[...] + r * (dxn - xn * _mean_last(dxn * xn))
        dx_ref[...] = dx
        dxb_ref[...] = dx.astype(BF16)
        _accumulate(dg_ref, _rows_to_sublanes(dy * xn), pl.program_id(0) == 0)

    row = pl.BlockSpec((tr, D), lambda i: (i, 0))
    return _pcall(
        body, deps, name=name, grid=(T // tr,),
        in_specs=[row, row, pl.BlockSpec((1, D), lambda i: (0, 0)), row],
        out_specs=[row, row, pl.BlockSpec((SUBLANES, D), lambda i: (0, 0))],
        out_shape=[jax.ShapeDtypeStruct((T, D), F32), jax.ShapeDtypeStruct((T, D), BF16),
                   jax.ShapeDtypeStruct((SUBLANES, D), F32)],
        compiler_params=_params(("arbitrary",)),
    )(dyn, x, g, dres)


def _tail(h2, z, pe, target, g_ple, g_final, tr=256):
    T, D = h2.shape
    tr = _tile(T, tr)

    def body(h2_ref, z_ref, pe_ref, t_ref, gp_ref, gf_ref,
             dh3_ref, dz_ref, dpe_ref, dgf_ref, dgp_ref, loss_ref):
        first = pl.program_id(0) == 0
        pev = pe_ref[...]
        r3 = lax.rsqrt(_mean_last(pev * pev) + EPS)
        en = pev * r3
        e = en * gp_ref[...]
        gate = 1.0 / (1.0 + jnp.exp(-z_ref[...]))
        h3 = h2_ref[...] + gate * e
        r5 = lax.rsqrt(_mean_last(h3 * h3) + EPS)
        hn = h3 * r5
        diff = hn * gf_ref[...] - t_ref[...]
        loss_rows = 0.5 * _mean_last(diff * diff)
        row0 = lax.broadcasted_iota(jnp.int32, (SUBLANES, LANES), 0) == 0
        _accumulate(loss_ref, jnp.where(row0, jnp.sum(loss_rows), 0.0), first)
        dy = diff * (1.0 / D)
        _accumulate(dgf_ref, _rows_to_sublanes(dy * hn), first)
        dhn = dy * gf_ref[...]
        dh3 = r5 * (dhn - hn * _mean_last(dhn * hn))
        dh3_ref[...] = dh3
        dgate = dh3 * e
        de = dh3 * gate
        dz_ref[...] = (dgate * gate * (1.0 - gate)).astype(BF16)
        _accumulate(dgp_ref, _rows_to_sublanes(de * en), first)
        den = de * gp_ref[...]
        dpe_ref[...] = (r3 * (den - en * _mean_last(den * en))).astype(BF16)

    row = pl.BlockSpec((tr, D), lambda i: (i, 0))
    vec = pl.BlockSpec((1, D), lambda i: (0, 0))
    part = pl.BlockSpec((SUBLANES, D), lambda i: (0, 0))
    return pl.pallas_call(
        body, name="tail", grid=(T // tr,),
        in_specs=[row, row, row, row, vec, vec],
        out_specs=[row, row, row, part, part, pl.BlockSpec((SUBLANES, LANES), lambda i: (0, 0))],
        out_shape=[jax.ShapeDtypeStruct((T, D), F32), jax.ShapeDtypeStruct((T, D), BF16),
                   jax.ShapeDtypeStruct((T, D), BF16), jax.ShapeDtypeStruct((SUBLANES, D), F32),
                   jax.ShapeDtypeStruct((SUBLANES, D), F32), jax.ShapeDtypeStruct((SUBLANES, LANES), F32)],
        compiler_params=_params(("arbitrary",)),
    )(h2, z, pe, target, g_ple, g_final)


def _rope_tables(T):
    pos = np.arange(T)
    half = HEAD_DIM // 2
    inv = (ROPE_THETA ** (-np.arange(0, half, 2, dtype=np.float32) / half)).astype(np.float32)
    ang_r = (pos // GRID_W).astype(np.float32)[:, None] * inv
    ang_c = (pos % GRID_W).astype(np.float32)[:, None] * inv
    cos = np.concatenate([np.cos(ang_r), np.cos(ang_r), np.cos(ang_c), np.cos(ang_c)], axis=-1)
    sin = np.concatenate([-np.sin(ang_r), np.sin(ang_r), -np.sin(ang_c), np.sin(ang_c)], axis=-1)
    return jnp.asarray(cos, F32), jnp.asarray(sin, F32)


def _swap32(x):
    lane = lax.broadcasted_iota(jnp.int32, x.shape, 1)
    return jnp.where((lane % 64) < 32, pltpu.roll(x, 96, 1), pltpu.roll(x, 32, 1))


def _qk_prep(proj, cos, sin, g_q, g_k, n_norm, tr=256):
    T, W = proj.shape
    tr = _tile(T, tr)
    n_q = n_norm * GROUP // (GROUP + 1)

    def body(p_ref, c_ref, s_ref, gq_ref, gk_ref, o_ref):
        c, s = c_ref[...], s_ref[...]
        for h in range(n_norm):
            cols = slice(h * HEAD_DIM, (h + 1) * HEAD_DIM)
            xv = p_ref[:, cols]
            g = gq_ref[...] if h < n_q else gk_ref[...]
            xn = xv * lax.rsqrt(_mean_last(xv * xv) + EPS) * g
            o_ref[:, cols] = (xn * c + _swap32(xn) * s).astype(BF16)
        rest = slice(n_norm * HEAD_DIM, W)
        o_ref[:, rest] = p_ref[:, rest].astype(BF16)

    row = pl.BlockSpec((tr, W), lambda i: (i, 0))
    tab = pl.BlockSpec((tr, HEAD_DIM), lambda i: (i, 0))
    vec = pl.BlockSpec((1, HEAD_DIM), lambda i: (0, 0))
    return pl.pallas_call(
        body, name="qk_prep", grid=(T // tr,),
        in_specs=[row, tab, tab, vec, vec], out_specs=row,
        out_shape=jax.ShapeDtypeStruct((T, W), BF16),
        compiler_params=_params(("parallel",)),
    )(proj, cos, sin, g_q, g_k)


def _dproj(proj, dqa, dka, dva, dqb, dkb, dvb, cos, sin, g_q, g_k, tr=256):
    T, W = proj.shape
    tr = _tile(T, tr)
    n_q = dqa.shape[1] // HEAD_DIM
    n_kv = dka.shape[1] // HEAD_DIM
    wa = (n_q + n_kv) * HEAD_DIM

    def body(p_ref, dqa_ref, dka_ref, dva_ref, dqb_ref, dkb_ref, dvb_ref, c_ref, s_ref, gq_ref, gk_ref,
             o_ref, dgq_ref, dgk_ref):
        c, s = c_ref[...], s_ref[...]
        dgq = jnp.zeros((SUBLANES, HEAD_DIM), F32)
        dgk = jnp.zeros((SUBLANES, HEAD_DIM), F32)
        for h in range(n_q + n_kv):
            cols = slice(h * HEAD_DIM, (h + 1) * HEAD_DIM)
            xv = p_ref[:, cols]
            r = lax.rsqrt(_mean_last(xv * xv) + EPS)
            xn = xv * r
            if h < n_q:
                d = dqa_ref[:, cols]
                g = gq_ref[...]
            else:
                d = dka_ref[:, (h - n_q) * HEAD_DIM:(h - n_q + 1) * HEAD_DIM]
                g = gk_ref[...]
            dqn = d * c + _swap32(d * s)
            part = _rows_to_sublanes(dqn * xn)
            if h < n_q:
                dgq = dgq + part
            else:
                dgk = dgk + part
            dxn = dqn * g
            o_ref[:, cols] = (r * (dxn - xn * _mean_last(dxn * xn))).astype(BF16)
        off = wa
        for ref in (dva_ref, dqb_ref, dkb_ref, dvb_ref):
            w = ref.shape[1]
            o_ref[:, off:off + w] = ref[...].astype(BF16)
            off += w
        first = pl.program_id(0) == 0
        _accumulate(dgq_ref, dgq, first)
        _accumulate(dgk_ref, dgk, first)

    def row(w):
        return pl.BlockSpec((tr, w), lambda i: (i, 0))

    vec = pl.BlockSpec((1, HEAD_DIM), lambda i: (0, 0))
    part = pl.BlockSpec((SUBLANES, HEAD_DIM), lambda i: (0, 0))
    return pl.pallas_call(
        body, name="dproj", grid=(T // tr,),
        in_specs=[row(wa), row(dqa.shape[1]), row(dka.shape[1]), row(dva.shape[1]), row(dqb.shape[1]),
                  row(dkb.shape[1]), row(dvb.shape[1]), row(HEAD_DIM), row(HEAD_DIM), vec, vec],
        out_specs=[row(W), part, part],
        out_shape=[jax.ShapeDtypeStruct((T, W), BF16), jax.ShapeDtypeStruct((SUBLANES, HEAD_DIM), F32),
                   jax.ShapeDtypeStruct((SUBLANES, HEAD_DIM), F32)],
        compiler_params=_params(("arbitrary",)),
    )(proj, dqa, dka, dva, dqb, dkb, dvb, cos, sin, g_q, g_k)


def _attn_a_fwd(pb, n_q, n_kv, tq=1024, tc=1024):
    T = pb.shape[0]
    tq, tc = _tile(T, tq), _tile(T, tc)
    scale = HEAD_DIM ** -0.5
    c = scale * LOG2E

    def body(q_ref, k_ref, v_ref, o_ref, lse_ref):
        q = q_ref[...]
        m = l = acc = None
        for j in range(T // tc):
            keys = slice(j * tc, (j + 1) * tc)
            s = lax.dot_general(q, k_ref[keys, :], _NT, preferred_element_type=F32)
            mj = jnp.max(s, axis=-1, keepdims=True)
            m_new = mj if j == 0 else jnp.maximum(m, mj)
            p = jnp.exp2((s - m_new) * c)
            pv = lax.dot_general(p.astype(BF16), v_ref[keys, :], _NN, preferred_element_type=F32)
            if j == 0:
                l, acc = jnp.sum(p, axis=-1, keepdims=True), pv
            else:
                alpha = jnp.exp2((m - m_new) * c)
                l = alpha * l + jnp.sum(p, axis=-1, keepdims=True)
                acc = alpha * acc + pv
            m = m_new
        o_ref[...] = (acc / l).astype(BF16)
        lse_ref[...] = m * scale + jnp.log(l)

    return pl.pallas_call(
        body, name="attn_a_fwd", grid=(n_kv, GROUP, T // tq),
        in_specs=[pl.BlockSpec((tq, HEAD_DIM), lambda kv, g, i: (i, kv * GROUP + g)),
                  pl.BlockSpec((T, HEAD_DIM), lambda kv, g, i: (0, n_q + kv)),
                  pl.BlockSpec((T, HEAD_DIM), lambda kv, g, i: (0, n_q + n_kv + kv))],
        out_specs=[pl.BlockSpec((tq, HEAD_DIM), lambda kv, g, i: (i, kv * GROUP + g)),
                   pl.BlockSpec((None, tq, 1), lambda kv, g, i: (kv * GROUP + g, i, 0))],
        out_shape=[jax.ShapeDtypeStruct((T, n_q * HEAD_DIM), BF16), jax.ShapeDtypeStruct((n_q, T, 1), F32)],
        compiler_params=_params(("parallel", "parallel", "parallel")),
    )(pb, pb, pb)


def _attn_a_bwd(pb, o_cat, d_o, lse, n_q, n_kv, tq=1024, tc=512):
    T = pb.shape[0]
    tq, tc = _tile(T, tq), _tile(T, tc)
    scale = HEAD_DIM ** -0.5
    c = scale * LOG2E

    def body(q_ref, k_ref, v_ref, o_ref, do_ref, lse_ref, dq_ref, dk_ref, dv_ref):
        q, do = q_ref[...], do_ref[...]
        delta = jnp.sum(do.astype(F32) * o_ref[...].astype(F32), axis=-1, keepdims=True)
        lse2 = lse_ref[...] * LOG2E
        first = jnp.logical_and(pl.program_id(1) == 0, pl.program_id(2) == 0)
        dq = None
        for j in range(T // tc):
            keys = slice(j * tc, (j + 1) * tc)
            kc, vc = k_ref[keys, :], v_ref[keys, :]
            s = lax.dot_general(q, kc, _NT, preferred_element_type=F32)
            p = jnp.exp2(s * c - lse2)
            dp = lax.dot_general(do, vc, _NT, preferred_element_type=F32)
            ds = (p * (dp - delta) * scale).astype(BF16)
            dqj = lax.dot_general(ds, kc, _NN, preferred_element_type=F32)
            dq = dqj if dq is None else dq + dqj
            _accumulate(dv_ref.at[keys, :], lax.dot_general(p.astype(BF16), do, _TN, preferred_element_type=F32), first)
            _accumulate(dk_ref.at[keys, :], lax.dot_general(ds, q, _TN, preferred_element_type=F32), first)
        dq_ref[...] = dq

    qmap = lambda kv, g, i: (i, kv * GROUP + g)
    return pl.pallas_call(
        body, name="attn_a_bwd", grid=(n_kv, GROUP, T // tq),
        in_specs=[pl.BlockSpec((tq, HEAD_DIM), qmap),
                  pl.BlockSpec((T, HEAD_DIM), lambda kv, g, i: (0, n_q + kv)),
                  pl.BlockSpec((T, HEAD_DIM), lambda kv, g, i: (0, n_q + n_kv + kv)),
                  pl.BlockSpec((tq, HEAD_DIM), qmap),
                  pl.BlockSpec((tq, HEAD_DIM), qmap),
                  pl.BlockSpec((None, tq, 1), lambda kv, g, i: (kv * GROUP + g, i, 0))],
        out_specs=[pl.BlockSpec((tq, HEAD_DIM), qmap),
                   pl.BlockSpec((T, HEAD_DIM), lambda kv, g, i: (0, kv)),
                   pl.BlockSpec((T, HEAD_DIM), lambda kv, g, i: (0, kv))],
        out_shape=[jax.ShapeDtypeStruct((T, n_q * HEAD_DIM), F32),
                   jax.ShapeDtypeStruct((T, n_kv * HEAD_DIM), F32),
                   jax.ShapeDtypeStruct((T, n_kv * HEAD_DIM), F32)],
        compiler_params=_params(("parallel", "arbitrary", "arbitrary")),
    )(pb, pb, pb, o_cat, d_o, lse)


def _bucket_index():
    r = np.arange(BLOCK_Q)[:, None]
    j = np.arange(3 * BLOCK_Q)[None, :]
    rel = (j - BLOCK_Q) - r
    nb = N_BUCKETS // 2
    ret = np.where(rel > 0, nb, 0)
    n = np.abs(rel)
    max_exact = nb // 2
    nf = np.maximum(n, 1).astype(np.float32)
    large = max_exact + (np.log(nf / max_exact) / math.log(MAX_DISTANCE / max_exact) * (nb - max_exact)).astype(np.int32)
    large = np.minimum(large, nb - 1)
    return jnp.asarray(ret + np.where(n < max_exact, n, large), jnp.int32)


def _bias_build(idx, table_flat, n_heads, deps=()):
    def body(idx_ref, tab_ref, o_ref):
        h = pl.program_id(0)
        iv = idx_ref[...]
        acc = jnp.zeros(iv.shape, F32)
        for b in range(N_BUCKETS):
            acc = jnp.where(iv == b, tab_ref[b * n_heads + h], acc)
        r = lax.broadcasted_iota(jnp.int32, iv.shape, 0)
        j = lax.broadcasted_iota(jnp.int32, iv.shape, 1)
        o_ref[...] = jnp.where(jnp.abs(j - BLOCK_Q - r) <= WINDOW, acc, NEG_INF)

    return _pcall(
        body, deps, name="bias_build", grid=(n_heads,),
        in_specs=[pl.BlockSpec(idx.shape, lambda h: (0, 0)), pl.BlockSpec(memory_space=pltpu.SMEM)],
        out_specs=pl.BlockSpec((None,) + idx.shape, lambda h: (h, 0, 0)),
        out_shape=jax.ShapeDtypeStruct((n_heads,) + idx.shape, F32),
        compiler_params=_params(("parallel",)),
    )(idx, table_flat)


def _in_sequence(n, T):
    j = lax.broadcasted_iota(jnp.int32, (BLOCK_Q, 3 * BLOCK_Q), 1)
    kabs = n * BLOCK_Q + j - BLOCK_Q
    return (kabs >= 0) & (kabs < T)


def _band_specs(col, nblk, sb):
    return [pl.BlockSpec((BLOCK_Q, HEAD_DIM), lambda kv, i: (jnp.maximum(sb * i - 1, 0), col(kv))),
            pl.BlockSpec((sb * BLOCK_Q, HEAD_DIM), lambda kv, i: (i, col(kv))),
            pl.BlockSpec((BLOCK_Q, HEAD_DIM), lambda kv, i: (jnp.minimum(sb * i + sb, nblk - 1), col(kv)))]


def _head_specs(base, rows):
    return [pl.BlockSpec((rows, HEAD_DIM), functools.partial(lambda kv, i, g: (i, base + kv * GROUP + g), g=g))
            for g in range(GROUP)]


def _attn_b_fwd(pb, bias, sink, q_off, n_q, n_kv, deps=(), sb=8):
    T = pb.shape[0]
    nblk = T // BLOCK_Q
    sb = min(sb, nblk)
    tq = sb * BLOCK_Q
    scale = HEAD_DIM ** -0.5

    def body(*refs):
        q_refs = refs[0:GROUP]
        k_refs, v_refs = refs[GROUP:GROUP + 3], refs[GROUP + 3:GROUP + 6]
        bias_ref, sink_ref, o_ref, lse_ref = refs[GROUP + 6:]
        kv, i = pl.program_id(0), pl.program_id(1)
        kb = jnp.concatenate([r[...] for r in k_refs], axis=0)
        vb = jnp.concatenate([r[...] for r in v_refs], axis=0)
        for b in range(sb):
            at_end = b == 0 or b == sb - 1
            mask = _in_sequence(i * sb + b, T) if at_end else None
            rows = slice(b * BLOCK_Q, (b + 1) * BLOCK_Q)
            kw, vw = kb[b * BLOCK_Q:(b + 3) * BLOCK_Q], vb[b * BLOCK_Q:(b + 3) * BLOCK_Q]
            for g in range(GROUP):
                sk = sink_ref[kv * GROUP + g]
                s = lax.dot_general(q_refs[g][rows, :], kw, _NT, preferred_element_type=F32) * scale + bias_ref[g]
                if at_end:
                    s = jnp.where(mask, s, NEG_INF)
                m = jnp.maximum(jnp.max(s, axis=-1, keepdims=True), sk)
                p = jnp.exp(s - m)
                l = jnp.sum(p, axis=-1, keepdims=True) + jnp.exp(sk - m)
                o = lax.dot_general(p.astype(BF16), vw, _NN, preferred_element_type=F32)
                o_ref[rows, g * HEAD_DIM:(g + 1) * HEAD_DIM] = (o / l).astype(BF16)
                lse_ref[g, rows, :] = m + jnp.log(l)

    return _pcall(
        body, deps, name="attn_b_fwd", grid=(n_kv, nblk // sb),
        in_specs=[*_head_specs(q_off, tq),
                  *_band_specs(lambda kv: q_off + n_q + kv, nblk, sb),
                  *_band_specs(lambda kv: q_off + n_q + n_kv + kv, nblk, sb),
                  pl.BlockSpec((GROUP, BLOCK_Q, 3 * BLOCK_Q), lambda kv, i: (kv, 0, 0)),
                  pl.BlockSpec(memory_space=pltpu.SMEM)],
        out_specs=[pl.BlockSpec((tq, GROUP * HEAD_DIM), lambda kv, i: (i, kv)),
                   pl.BlockSpec((GROUP, tq, 1), lambda kv, i: (kv, i, 0))],
        out_shape=[jax.ShapeDtypeStruct((T, n_q * HEAD_DIM), BF16), jax.ShapeDtypeStruct((n_q, T, 1), F32)],
        compiler_params=_params(("parallel", "parallel")),
    )(*([pb] * (GROUP + 6)), bias, sink)


def _attn_b_bwd(pb, o_cat, d_o, lse, bias, sink, q_off, n_q, n_kv, o_off, deps=(), sb=8):
    T = pb.shape[0]
    nblk = T // BLOCK_Q
    sb = min(sb, nblk)
    tq = sb * BLOCK_Q
    scale = HEAD_DIM ** -0.5

    def body(*refs):
        q_refs = refs[0:GROUP]
        k_refs, v_refs = refs[GROUP:GROUP + 3], refs[GROUP + 3:GROUP + 6]
        o_refs, do_refs = refs[GROUP + 6:2 * GROUP + 6], refs[2 * GROUP + 6:3 * GROUP + 6]
        lse_ref, bias_ref, sink_ref, dq_ref, dk_ref, dv_ref, dbias_ref, dsink_ref, dkb_ref, dvb_ref = refs[3 * GROUP + 6:]
        kv, i = pl.program_id(0), pl.program_id(1)
        first = i == 0
        kb = jnp.concatenate([r[...] for r in k_refs], axis=0)
        vb = jnp.concatenate([r[...] for r in v_refs], axis=0)
        dkb_ref[...] = jnp.zeros(dkb_ref.shape, F32)
        dvb_ref[...] = jnp.zeros(dvb_ref.shape, F32)
        row = lax.broadcasted_iota(jnp.int32, (SUBLANES, LANES), 0)
        dsink = jnp.zeros((SUBLANES, LANES), F32)
        for b in range(sb):
            at_end = b == 0 or b == sb - 1
            mask = _in_sequence(i * sb + b, T) if at_end else None
            rows = slice(b * BLOCK_Q, (b + 1) * BLOCK_Q)
            win = slice(b * BLOCK_Q, (b + 3) * BLOCK_Q)
            kw, vw = kb[win], vb[win]
            dkw = jnp.zeros((3 * BLOCK_Q, HEAD_DIM), F32)
            dvw = jnp.zeros((3 * BLOCK_Q, HEAD_DIM), F32)
            for g in range(GROUP):
                sk = sink_ref[kv * GROUP + g]
                q, do = q_refs[g][rows, :], do_refs[g][rows, :]
                lse_g = lse_ref[g, rows, :]
                delta = jnp.sum(do.astype(F32) * o_refs[g][rows, :].astype(F32), axis=-1, keepdims=True)
                s = lax.dot_general(q, kw, _NT, preferred_element_type=F32) * scale + bias_ref[g]
                if at_end:
                    s = jnp.where(mask, s, NEG_INF)
                p = jnp.exp(s - lse_g)
                dp = lax.dot_general(do, vw, _NT, preferred_element_type=F32)
                ds = p * (dp - delta)
                _accumulate(dbias_ref.at[g], ds, jnp.logical_and(first, b == 0))
                dsink = dsink + jnp.where(row == g, -jnp.sum(jnp.exp(sk - lse_g) * delta), 0.0)
                dsb = (ds * scale).astype(BF16)
                dq_ref[rows, g * HEAD_DIM:(g + 1) * HEAD_DIM] = lax.dot_general(dsb, kw, _NN, preferred_element_type=F32)
                dkw = dkw + lax.dot_general(dsb, q, _TN, preferred_element_type=F32)
                dvw = dvw + lax.dot_general(p.astype(BF16), do, _TN, preferred_element_type=F32)
            dkb_ref[win, :] += dkw
            dvb_ref[win, :] += dvw
        _accumulate(dsink_ref, dsink, first)

        @pl.when(first)
        def _():
            dk_ref[...] = jnp.zeros(dk_ref.shape, F32)
            dv_ref[...] = jnp.zeros(dv_ref.shape, F32)

        before = pl.ds(pl.multiple_of(jnp.maximum(sb * i - 1, 0) * BLOCK_Q, BLOCK_Q), BLOCK_Q)
        own = pl.ds(pl.multiple_of(i * tq, BLOCK_Q), tq)
        after = pl.ds(pl.multiple_of(jnp.minimum(sb * i + sb, nblk - 1) * BLOCK_Q, BLOCK_Q), BLOCK_Q)
        for acc_ref, band_ref in ((dk_ref, dkb_ref), (dv_ref, dvb_ref)):
            acc_ref[before, :] += band_ref[0:BLOCK_Q, :]
            acc_ref[own, :] += band_ref[BLOCK_Q:BLOCK_Q + tq, :]
            acc_ref[after, :] += band_ref[BLOCK_Q + tq:, :]

    return _pcall(
        body, deps, name="attn_b_bwd", grid=(n_kv, nblk // sb),
        in_specs=[*_head_specs(q_off, tq),
                  *_band_specs(lambda kv: q_off + n_q + kv, nblk, sb),
                  *_band_specs(lambda kv: q_off + n_q + n_kv + kv, nblk, sb),
                  *_head_specs(o_off, tq), *_head_specs(o_off, tq),
                  pl.BlockSpec((GROUP, tq, 1), lambda kv, i: (kv, i, 0)),
                  pl.BlockSpec((GROUP, BLOCK_Q, 3 * BLOCK_Q), lambda kv, i: (kv, 0, 0)),
                  pl.BlockSpec(memory_space=pltpu.SMEM)],
        out_specs=[pl.BlockSpec((tq, GROUP * HEAD_DIM), lambda kv, i: (i, kv)),
                   pl.BlockSpec((T, HEAD_DIM), lambda kv, i: (0, kv)),
                   pl.BlockSpec((T, HEAD_DIM), lambda kv, i: (0, kv)),
                   pl.BlockSpec((GROUP, BLOCK_Q, 3 * BLOCK_Q), lambda kv, i: (kv, 0, 0)),
                   pl.BlockSpec((None, SUBLANES, LANES), lambda kv, i: (kv, 0, 0))],
        out_shape=[jax.ShapeDtypeStruct((T, n_q * HEAD_DIM), F32),
                   jax.ShapeDtypeStruct((T, n_kv * HEAD_DIM), F32),
                   jax.ShapeDtypeStruct((T, n_kv * HEAD_DIM), F32),
                   jax.ShapeDtypeStruct((n_q, BLOCK_Q, 3 * BLOCK_Q), F32),
                   jax.ShapeDtypeStruct((n_kv, SUBLANES, LANES), F32)],
        scratch_shapes=[pltpu.VMEM((tq + 2 * BLOCK_Q, HEAD_DIM), F32), pltpu.VMEM((tq + 2 * BLOCK_Q, HEAD_DIM), F32)],
        compiler_params=_params(("parallel", "arbitrary")),
    )(*([pb] * (GROUP + 6)), *([o_cat] * GROUP), *([d_o] * GROUP), lse, bias, sink)


def _table_grads(dbias, dsink_raw, idx):
    n_heads = dbias.shape[0]
    n_kv = dsink_raw.shape[0]

    def body(db_ref, ds_ref, idx_ref, dt_ref, dsk_ref):
        iv = idx_ref[...]
        row = lax.broadcasted_iota(jnp.int32, (SUBLANES, LANES), 0)
        lane = lax.broadcasted_iota(jnp.int32, (SUBLANES, LANES), 1)
        dsk = jnp.zeros((SUBLANES, LANES), F32)
        for h in range(n_heads):
            d = db_ref[h]
            acc = jnp.zeros((SUBLANES, LANES), F32)
            for b in range(N_BUCKETS):
                acc = jnp.where((row == 0) & (lane == b), jnp.sum(jnp.where(iv == b, d, 0.0)), acc)
            dt_ref[:, h * LANES:(h + 1) * LANES] = acc
            raw = ds_ref[h // GROUP]
            val = jnp.sum(jnp.where((row == h % GROUP) & (lane == 0), raw, 0.0))
            dsk = jnp.where((row == 0) & (lane == h), val, dsk)
        dsk_ref[...] = dsk

    return pl.pallas_call(
        body, name="table_grads",
        in_specs=[pl.BlockSpec(memory_space=pltpu.VMEM)] * 3,
        out_specs=[pl.BlockSpec(memory_space=pltpu.VMEM)] * 2,
        out_shape=[jax.ShapeDtypeStruct((SUBLANES, n_heads * LANES), F32),
                   jax.ShapeDtypeStruct((SUBLANES, LANES), F32)],
        compiler_params=pltpu.CompilerParams(vmem_limit_bytes=56 * 1024 * 1024),
    )(dbias, dsink_raw, idx)


def _position():
    x, y, c = lax.axis_index("x"), lax.axis_index("y"), lax.axis_index("c")
    return x, y, c


def _hbm(a):
    return pltpu.with_memory_space_constraint(a, pltpu.HBM)


def _split_start(name, bufs, sem_shapes, issue):
    nb, ns = len(bufs), len(sem_shapes)

    def body(*refs):
        buf_refs = refs[:nb]
        sems = refs[nb:nb + ns]
        token = refs[nb + ns + nb]
        issue(buf_refs, sems)
        token[...] = jnp.zeros(token.shape, F32)

    outs = pl.pallas_call(
        body, name=name,
        in_specs=[_HBM] * nb,
        out_specs=[_SEM] * ns + [_HBM] * nb + [_VMEM],
        out_shape=[pltpu.SemaphoreType.DMA(s) for s in sem_shapes] + [pltpu.HBM(b.shape, b.dtype) for b in bufs]
        + [jax.ShapeDtypeStruct((SUBLANES, LANES), F32)],
        input_output_aliases={i: ns + i for i in range(nb)},
        compiler_params=pltpu.CompilerParams(has_side_effects=_EFFECT),
    )(*[_hbm(b) for b in bufs])
    return outs[:ns], outs[ns:ns + nb], outs[-1]


def _split_wait(name, bufs, send, recv, counts, size_of, after):
    nb = len(bufs)

    def body(*refs):
        buf_refs = refs[:nb]
        send_ref, recv_ref = refs[nb], refs[nb + 1]
        x, y, c = _position()
        for w, n in enumerate(counts):
            ref = size_of(buf_refs, w)
            for k in range(n):
                s = sum(counts[:w]) + k
                cp = pltpu.make_async_remote_copy(
                    src_ref=ref, dst_ref=ref, send_sem=send_ref.at[s], recv_sem=recv_ref.at[s],
                    device_id=(x, y, c), device_id_type=MESH)
                cp.wait_send()
                cp.wait_recv()

    return pl.pallas_call(
        body, name=name,
        in_specs=[_HBM] * nb + [_SEM, _SEM, _ANY],
        out_specs=[_HBM] * nb,
        out_shape=[pltpu.HBM(b.shape, b.dtype) for b in bufs],
        input_output_aliases={i: i for i in range(nb)},
        compiler_params=pltpu.CompilerParams(has_side_effects=_EFFECT),
    )(*bufs, send, recv, after)


def _block_of(pos):
    return 4 * pos[0] + 2 * pos[1] + pos[2]


def _shard_of(ref, blk, by_cols):
    aligned = (lambda v, a: v) if isinstance(blk, int) else pl.multiple_of
    if by_cols:
        n = ref.shape[1] // N_DEV
        return ref.at[:, pl.ds(aligned(blk * n, LANES), n)]
    r = ref.shape[0] // N_DEV
    return ref.at[pl.ds(aligned(blk * r, SUBLANES), r), :]


def _place_shards(shards, by_cols):
    nw = len(shards)

    def body(*refs):
        ins, outs, sems = refs[:nw], refs[nw:2 * nw], refs[-1]
        mine = _block_of(_position())
        copies = [pltpu.make_async_copy(ins[w], _shard_of(outs[w], mine, by_cols[w]), sems.at[w]) for w in range(nw)]
        for cp in copies:
            cp.start()
        for cp in copies:
            cp.wait()

    full = lambda s, cols: (s.shape[0], s.shape[1] * N_DEV) if cols else (s.shape[0] * N_DEV, s.shape[1])
    return pl.pallas_call(
        body, name="place_shards", in_specs=[_HBM] * nw, out_specs=[_HBM] * nw,
        out_shape=[pltpu.HBM(full(s, cols), s.dtype) for s, cols in zip(shards, by_cols)],
        scratch_shapes=[pltpu.SemaphoreType.DMA((nw,))],
    )(*[_hbm(s) for s in shards])


def _gather_start(shards, by_cols, groups):
    nw = len(shards)
    lands = list(_place_shards(shards, by_cols))

    def issue(bufs, sems):
        ins, land = bufs[:nw], bufs[nw:]
        x, y, c = _position()
        peers = [(x, y, 1 - c), (1 - x, y, c), (x, 1 - y, c), (1 - x, 1 - y, c)]
        for gi, grp in enumerate(groups):
            for wi, w in enumerate(grp):
                dst = _shard_of(land[w], _block_of((x, y, c)), by_cols[w])
                for k, peer in enumerate(peers):
                    pltpu.make_async_remote_copy(
                        src_ref=ins[w], dst_ref=dst, send_sem=sems[2 * gi].at[4 * wi + k],
                        recv_sem=sems[2 * gi + 1].at[4 * wi + k], device_id=peer, device_id_type=MESH).start()

    sem_shapes = [(4 * len(g),) for g in groups for _ in range(2)]
    sems, thru, token = _split_start("gather_start", list(shards) + lands, sem_shapes, issue)
    return sems, thru[:nw], thru[nw:], token


def _gather_forward(name, lands, by_cols):
    nw = len(lands)

    def issue(land, sems):
        x, y, c = _position()
        for w in range(nw):
            for k, chip in enumerate([(1 - x, y), (x, 1 - y), (1 - x, 1 - y)]):
                blk = _shard_of(land[w], _block_of((*chip, c)), by_cols[w])
                pltpu.make_async_remote_copy(
                    src_ref=blk, dst_ref=blk, send_sem=sems[0].at[3 * w + k], recv_sem=sems[1].at[3 * w + k],
                    device_id=(x, y, 1 - c), device_id_type=MESH).start()

    return _split_start(name, lands, [(3 * nw,), (3 * nw,)], issue)


def _first_block(bufs, w, offset=0):
    return bufs[offset + w].at[0]


def _pair_start(name, grads, by_cols):
    nw = len(grads)
    lands = []
    for g, cols in zip(grads, by_cols):
        shard = (g.shape[0], g.shape[1] // N_DEV) if cols else (g.shape[0] // N_DEV, g.shape[1])
        lands.append(lax.empty((N_CHIP,) + shard, g.dtype))

    def issue(bufs, sems):
        x, y, c = _position()
        for w in range(nw):
            for q in range(N_CHIP):
                pltpu.make_async_remote_copy(
                    src_ref=_shard_of(bufs[w], 2 * q + 1 - c, by_cols[w]), dst_ref=bufs[nw + w].at[q],
                    send_sem=sems[0].at[N_CHIP * w + q], recv_sem=sems[1].at[N_CHIP * w + q],
                    device_id=(x, y, 1 - c), device_id_type=MESH).start()

    return _split_start(name, list(grads) + lands, [(N_CHIP * nw,), (N_CHIP * nw,)], issue)


def _chip_start(name, sums):
    nw = len(sums)
    x, y, _ = _position()
    mine = 2 * x + y
    lands = [lax.dynamic_update_slice(lax.empty(s.shape, s.dtype), lax.dynamic_slice_in_dim(s, mine, 1, 0), (mine, 0, 0))
             for s in sums]

    def issue(bufs, sems):
        x, y, c = _position()
        for w in range(nw):
            for k, (px, py) in enumerate([(1 - x, y), (x, 1 - y), (1 - x, 1 - y)]):
                pltpu.make_async_remote_copy(
                    src_ref=bufs[w].at[2 * px + py], dst_ref=bufs[nw + w].at[2 * x + y], send_sem=sems[0].at[3 * w + k],
                    recv_sem=sems[1].at[3 * w + k], device_id=(px, py, c), device_id_type=MESH).start()

    return _split_start(name, list(sums) + lands, [(3 * nw,), (3 * nw,)], issue)


def _pair_sum(name, grad, landed, by_cols, tr=256):
    _, R, C = landed.shape
    tr = _tile(R, tr)
    core = lax.axis_index("c").astype(jnp.int32).reshape(1)

    def body(c_ref, g_ref, l_ref, o_ref):
        o_ref[...] = (g_ref[...] + l_ref[...]).astype(BF16)

    if by_cols:
        mine = pl.BlockSpec((tr, C), lambda q, i, c_ref: (i, 2 * q + c_ref[0]))
    else:
        mine = pl.BlockSpec((tr, C), lambda q, i, c_ref: ((2 * q + c_ref[0]) * (R // tr) + i, 0))
    slot = pl.BlockSpec((None, tr, C), lambda q, i, c_ref: (q, i, 0))
    return pl.pallas_call(
        body, name=name,
        grid_spec=pltpu.PrefetchScalarGridSpec(
            num_scalar_prefetch=1, grid=(N_CHIP, R // tr),
            in_specs=[mine, slot],
            out_specs=slot),
        out_shape=jax.ShapeDtypeStruct((N_CHIP, R, C), BF16),
        compiler_params=_params(("parallel", "parallel")),
    )(core, grad, landed)


def _adam(w, g, m, v):
    m = ADAM_B1 * m + (1.0 - ADAM_B1) * g
    v = ADAM_B2 * v + (1.0 - ADAM_B2) * (g * g)
    m_hat = m / (1.0 - ADAM_B1 ** ADAM_STEP)
    v_hat = v / (1.0 - ADAM_B2 ** ADAM_STEP)
    delta = -ADAM_LR * (m_hat / (jnp.sqrt(v_hat) + ADAM_EPS) + ADAM_WD * w)
    return delta, m, v


def _sum_adam(name, landed, w, m, v, tr=256):
    R, C = w.shape
    tr = _tile(R, tr)

    def body(l_ref, w_ref, m_ref, v_ref, g_ref, d_ref, nm_ref, nv_ref):
        g = l_ref[0].astype(F32)
        for q in range(1, N_CHIP):
            g = g + l_ref[q].astype(F32)
        g_ref[...] = g
        d_ref[...], nm_ref[...], nv_ref[...] = _adam(w_ref[...], g, m_ref[...], v_ref[...])

    tile = pl.BlockSpec((tr, C), lambda i: (i, 0))
    return pl.pallas_call(
        body, name=name, grid=(R // tr,),
        in_specs=[pl.BlockSpec((N_CHIP, tr, C), lambda i: (0, i, 0)), tile, tile, tile],
        out_specs=[tile] * 4, out_shape=[jax.ShapeDtypeStruct((R, C), F32)] * 4,
        compiler_params=_params(("parallel",)),
    )(landed, w, m, v)


def _small_all_reduce(parts):
    W = parts.shape[1]

    def body(p_ref, o_ref, slots, send_sems, recv_sems):
        x, y, c = _position()
        me = 4 * x + 2 * y + c
        slots[me] = jnp.sum(p_ref[...], axis=0, keepdims=True)
        peers = [(x, y, 1 - c), (1 - x, y, c), (1 - x, y, 1 - c), (x, 1 - y, c), (x, 1 - y, 1 - c),
                 (1 - x, 1 - y, c), (1 - x, 1 - y, 1 - c)]
        copies = []
        for k, peer in enumerate(peers):
            cp = pltpu.make_async_remote_copy(
                src_ref=slots.at[me], dst_ref=slots.at[me], send_sem=send_sems.at[k], recv_sem=recv_sems.at[k],
                device_id=peer, device_id_type=MESH)
            cp.start()
            copies.append(cp)
        for cp in copies:
            cp.wait()
        total = slots[0]
        for d in range(1, N_DEV):
            total = total + slots[d]
        o_ref[...] = total

    return pl.pallas_call(
        body, name="small_all_reduce",
        in_specs=[pl.BlockSpec(memory_space=pltpu.VMEM)], out_specs=pl.BlockSpec(memory_space=pltpu.VMEM),
        out_shape=jax.ShapeDtypeStruct((1, W), F32),
        scratch_shapes=[pltpu.VMEM((N_DEV, 1, W), F32), pltpu.SemaphoreType.DMA((7,)), pltpu.SemaphoreType.DMA((7,))],
    )(parts)


def _adam_small(w, g, m, v):
    def body(w_ref, g_ref, m_ref, v_ref, d_ref, nm_ref, nv_ref):
        d_ref[...], nm_ref[...], nv_ref[...] = _adam(w_ref[...], g_ref[...], m_ref[...], v_ref[...])

    return pl.pallas_call(
        body, name="adam_small",
        in_specs=[pl.BlockSpec(memory_space=pltpu.VMEM)] * 4, out_specs=[pl.BlockSpec(memory_space=pltpu.VMEM)] * 3,
        out_shape=[jax.ShapeDtypeStruct(w.shape, F32)] * 3,
    )(w, g, m, v)


_GATHER_GROUPS = (("w_in",), ("w_out", "w_up", "ple_w"), ("w_down", "w_gate"))
_COL_SHARDED = ("w_in", "w_up", "ple_w")


class _MeshComm:
    def __init__(self, w, mom, var):
        self.w, self.mom, self.var = w, mom, var
        self.out = {}
        self._pairs, self._chips = {}, {}

    def gather_begin(self):
        names = [n for g in _GATHER_GROUPS for n in g]
        self._idx = {n: i for i, n in enumerate(names)}
        groups = [[self._idx[n] for n in g] for g in _GATHER_GROUPS]
        self._sems, self._src, self._lands, token = _gather_start(
            [self.w[n].astype(BF16) for n in names], [n in _COL_SHARDED for n in names], groups)
        return token

    @staticmethod
    def _shard_size(names, offset):
        return lambda bufs, w: _shard_of(bufs[offset + w], 0, names[w] in _COL_SHARDED)

    def gather_arrive(self, gi, after):
        names = _GATHER_GROUPS[gi]
        ids = [self._idx[n] for n in names]
        bufs = [self._src[i] for i in ids] + [self._lands[i] for i in ids]
        out = _split_wait("gather_arrive%d" % gi, bufs, self._sems[2 * gi], self._sems[2 * gi + 1], [4] * len(ids),
                          self._shard_size(names, len(ids)), after)
        self._arrived = out[len(ids):]

    def gather_forward(self, gi):
        by_cols = [n in _COL_SHARDED for n in _GATHER_GROUPS[gi]]
        self._fsems, self._fthru, token = _gather_forward("gather_forward%d" % gi, self._arrived, by_cols)
        return token

    def gather_finish(self, gi, after):
        names = _GATHER_GROUPS[gi]
        out = _split_wait("gather_finish%d" % gi, self._fthru, self._fsems[0], self._fsems[1], [3] * len(names),
                          self._shard_size(names, 0), after)
        return dict(zip(names, out))

    def reduce_begin(self, key, grads):
        names = list(grads)
        sems, thru, token = _pair_start("pair_start_" + key, [grads[n] for n in names],
                                        [n in _COL_SHARDED for n in names])
        self._pairs[key] = (names, sems, thru)
        return token

    def reduce_middle(self, key, after):
        names, sems, thru = self._pairs[key]
        nw = len(names)
        out = _split_wait("pair_wait_" + key, thru, sems[0], sems[1], [N_CHIP] * nw,
                          functools.partial(_first_block, offset=nw), after)
        sums = [_pair_sum("pair_sum_" + n, out[i], out[nw + i], n in _COL_SHARDED) for i, n in enumerate(names)]
        sems2, thru2, token = _chip_start("chip_start_" + key, sums)
        self._chips[key] = (names, sems2, thru2)
        return token

    def reduce_finish(self, key, after):
        names, sems, thru = self._chips[key]
        nw = len(names)
        out = _split_wait("chip_wait_" + key, thru, sems[0], sems[1], [3] * nw,
                          functools.partial(_first_block, offset=nw), after)
        for i, n in enumerate(names):
            self.out[n] = _sum_adam("adam_" + n, out[nw + i], self.w[n], self.mom[n], self.var[n])


def _step(x, p, target, gains, comm):
    T, D = x.shape
    n_q = D // (2 * HEAD_DIM)
    n_kv = n_q // GROUP
    cos, sin = _rope_tables(T)
    idx = _bucket_index()

    t = comm.gather_begin()
    u = _rms_fwd("norm_attn", x, gains["attn_norm_g"], deps=(t,))
    comm.gather_arrive(0, u)
    t = comm.gather_forward(0)
    bias = _bias_build(idx, gains["rel_bias_table"].reshape(-1), n_q, deps=(t,))
    full = comm.gather_finish(0, bias)
    proj = _mm_nn("in_proj", u, full["w_in"])
    pb = _qk_prep(proj, cos, sin, gains["q_norm_g"], gains["k_norm_g"], n_q + n_kv)
    o_a, lse_a = _attn_a_fwd(pb, n_q, n_kv)
    comm.gather_arrive(1, o_a)
    t = comm.gather_forward(1)
    sink = gains["sink_logits"].reshape(-1)
    b_off = n_q + 2 * n_kv
    o_b, lse_b = _attn_b_fwd(pb, bias, sink, b_off, n_q, n_kv, deps=(t,))
    full.update(comm.gather_finish(1, o_b))
    o_cat = jnp.concatenate([o_a, o_b], axis=1)
    h1 = _mm_nn("out_proj", o_cat, full["w_out"], epilogue=_store_add, extras=(x,))
    m_in = _rms_fwd("norm_mlp", h1, gains["mlp_norm_g"])

    def up_epilogue(acc, extra, outs):
        outs[0][...] = acc.astype(BF16)
        r = jnp.maximum(acc, 0.0)
        outs[1][...] = (r * r).astype(BF16)

    a_act, f_act = _mm_nn("up_proj", m_in, full["w_up"], epilogue=up_epilogue, out_dtypes=[BF16, BF16])
    comm.gather_arrive(2, f_act)
    t = comm.gather_forward(2)
    p_b = p.astype(BF16)
    pe = _mm_nn("ple_proj", p_b, full["ple_w"], deps=(t,))
    full.update(comm.gather_finish(2, pe))
    h2 = _mm_nn("down_proj", f_act, full["w_down"], epilogue=_store_add, extras=(h1,), tm=512, tn=512)
    gn = _rms_fwd("norm_gate", h2, gains["gate_norm_g"])
    z = _mm_nn("gate_proj", gn, full["w_gate"])

    dh3, dz, dpe, dg_final, dg_ple, loss_part = _tail(h2, z, pe, target, gains["ple_norm_g"], gains["final_norm_g"])
    gw_gate = _mm_tn("grad_w_gate", gn, dz)
    gw_ple = _mm_tn("grad_ple_w", p_b, dpe)
    t = comm.reduce_begin("a", dict(w_gate=gw_gate, ple_w=gw_ple))
    dgn = _mm_nt("d_gate_in", dz, full["w_gate"], deps=(t,))
    dh2, dh2_b, dg_gate = _rms_bwd("norm_gate_bwd", dgn, h2, gains["gate_norm_g"], dh3)
    t = comm.reduce_middle("a", dh2_b)
    gw_down = _mm_tn("grad_w_down", f_act, dh2_b, deps=(t,))
    t = comm.reduce_begin("b", dict(w_down=gw_down))

    def act_bwd(acc, extra, outs):
        outs[0][...] = (acc * (2.0 * jnp.maximum(extra[0][...].astype(F32), 0.0))).astype(BF16)

    da = _mm_nt("d_act", dh2_b, full["w_down"], out_dtype=BF16, epilogue=act_bwd, extras=(a_act,), deps=(t,))
    comm.reduce_finish("a", da)
    t = comm.reduce_middle("b", da)
    gw_up = _mm_tn("grad_w_up", m_in, da, deps=(t,))
    t = comm.reduce_begin("c", dict(w_up=gw_up))
    dm = _mm_nt("d_mlp_in", da, full["w_up"], tm=512, tn=512, deps=(t,))
    t = comm.reduce_middle("c", dm)
    dh1, dh1_b, dg_mlp = _rms_bwd("norm_mlp_bwd", dm, h1, gains["mlp_norm_g"], dh2, deps=(t,))
    comm.reduce_finish("b", dh1_b)
    gw_out = _mm_tn("grad_w_out", o_cat, dh1_b)
    t = comm.reduce_begin("d", dict(w_out=gw_out))
    d_o = _mm_nt("d_attn_out", dh1_b, full["w_out"], out_dtype=BF16, deps=(t,))
    dqa, dka, dva = _attn_a_bwd(pb, o_cat, d_o, lse_a, n_q, n_kv)
    t = comm.reduce_middle("d", dqa)
    dqb, dkb, dvb, dbias, dsink_raw = _attn_b_bwd(pb, o_cat, d_o, lse_b, bias, sink, b_off, n_q, n_kv, n_q, deps=(t,))
    comm.reduce_finish("c", dqb)
    comm.reduce_finish("d", dqb)
    dtable, dsink = _table_grads(dbias, dsink_raw, idx)
    dproj, dg_q, dg_k = _dproj(proj, dqa, dka, dva, dqb, dkb, dvb, cos, sin, gains["q_norm_g"], gains["k_norm_g"])
    gw_in = _mm_tn("grad_w_in", u, dproj)
    t = comm.reduce_begin("e", dict(w_in=gw_in))
    du = _mm_nt("d_attn_in", dproj, full["w_in"], deps=(t,))
    t = comm.reduce_middle("e", du)
    dx, _, dg_attn = _rms_bwd("norm_attn_bwd", du, x, gains["attn_norm_g"], dh1, deps=(t,))
    comm.reduce_finish("e", dx)

    parts = jnp.concatenate([dg_attn, dg_mlp, dg_ple, dg_gate, dg_final, dg_q, dg_k, dtable, dsink, loss_part], axis=1)
    return dx, parts


_SHARDED = ("w_in", "w_out", "w_up", "w_down", "ple_w", "w_gate")
_VECTORS = ("attn_norm_g", "mlp_norm_g", "ple_norm_g", "gate_norm_g", "final_norm_g")
_ORDER = ("attn_norm_g", "w_in", "q_norm_g", "k_norm_g", "sink_logits", "w_out", "mlp_norm_g", "w_up", "w_down",
          "ple_w", "ple_norm_g", "gate_norm_g", "w_gate", "rel_bias_table", "final_norm_g")


def _pack_small(vals, n_heads):
    lane_pad = lambda v: jnp.pad(v, ((0, 0), (0, LANES - v.shape[1])))
    table = lane_pad(vals["rel_bias_table"].T).reshape(1, n_heads * LANES)
    return jnp.concatenate(
        [vals[n].reshape(1, -1) for n in _VECTORS] + [vals["q_norm_g"], vals["k_norm_g"], table,
                                                      lane_pad(vals["sink_logits"]), jnp.zeros((1, LANES), F32)], axis=1)


def _unpack_small(row, like, n_heads):
    out, off = {}, 0
    for n in _VECTORS:
        out[n] = row[:, off:off + like[n].size].reshape(like[n].shape)
        off += like[n].size
    for n in ("q_norm_g", "k_norm_g"):
        out[n] = row[:, off:off + LANES]
        off += LANES
    out["rel_bias_table"] = row[:, off:off + n_heads * LANES].reshape(n_heads, LANES)[:, :N_BUCKETS].T
    off += n_heads * LANES
    out["sink_logits"] = row[:, off:off + n_heads]
    off += LANES
    return out, row[0, off]


def kernel(x, p, attn_norm_g, w_in, q_norm_g, k_norm_g, sink_logits, w_out, mlp_norm_g, w_up, w_down, ple_w, ple_norm_g, gate_norm_g, w_gate, rel_bias_table, final_norm_g, loss_target, m_attn_norm_g, m_w_in, m_q_norm_g, m_k_norm_g, m_sink_logits, m_w_out, m_mlp_norm_g, m_w_up, m_w_down, m_ple_w, m_ple_norm_g, m_gate_norm_g, m_w_gate, m_rel_bias_table, m_final_norm_g, v_attn_norm_g, v_w_in, v_q_norm_g, v_k_norm_g, v_sink_logits, v_w_out, v_mlp_norm_g, v_w_up, v_w_down, v_ple_w, v_ple_norm_g, v_gate_norm_g, v_w_gate, v_rel_bias_table, v_final_norm_g):
    w = dict(attn_norm_g=attn_norm_g, w_in=w_in[0], q_norm_g=q_norm_g, k_norm_g=k_norm_g, sink_logits=sink_logits,
             w_out=w_out[0], mlp_norm_g=mlp_norm_g, w_up=w_up[0], w_down=w_down[0], ple_w=ple_w[0],
             ple_norm_g=ple_norm_g, gate_norm_g=gate_norm_g, w_gate=w_gate[0], rel_bias_table=rel_bias_table,
             final_norm_g=final_norm_g)
    mom = dict(attn_norm_g=m_attn_norm_g, w_in=m_w_in[0], q_norm_g=m_q_norm_g, k_norm_g=m_k_norm_g,
               sink_logits=m_sink_logits, w_out=m_w_out[0], mlp_norm_g=m_mlp_norm_g, w_up=m_w_up[0],
               w_down=m_w_down[0], ple_w=m_ple_w[0], ple_norm_g=m_ple_norm_g, gate_norm_g=m_gate_norm_g,
               w_gate=m_w_gate[0], rel_bias_table=m_rel_bias_table, final_norm_g=m_final_norm_g)
    var = dict(attn_norm_g=v_attn_norm_g, w_in=v_w_in[0], q_norm_g=v_q_norm_g, k_norm_g=v_k_norm_g,
               sink_logits=v_sink_logits, w_out=v_w_out[0], mlp_norm_g=v_mlp_norm_g, w_up=v_w_up[0],
               w_down=v_w_down[0], ple_w=v_ple_w[0], ple_norm_g=v_ple_norm_g, gate_norm_g=v_gate_norm_g,
               w_gate=v_w_gate[0], rel_bias_table=v_rel_bias_table, final_norm_g=v_final_norm_g)
    D = x.shape[-1]
    n_heads = D // (2 * HEAD_DIM)

    gains = {n: w[n] for n in w if n not in _SHARDED}
    gains["final_norm_g"] = final_norm_g.reshape(1, -1)

    comm = _MeshComm(w, mom, var)
    dx, parts = _step(x[0], p[0, 0], loss_target[0], gains, comm)

    g_out, d_out, m_out, v_out = {}, {}, {}, {}
    for n in _SHARDED:
        g, d, nm, nv = comm.out[n]
        g_out[n], d_out[n], m_out[n], v_out[n] = g[None], d[None], nm[None], nv[None]

    small_g = _small_all_reduce(parts)
    small = {n: v for n, v in w.items() if n not in _SHARDED}
    pack = lambda vals: _pack_small({n: vals[n] for n in small}, n_heads)
    sd, sm, sv = _adam_small(pack(w), small_g, pack(mom), pack(var))
    sg, loss = _unpack_small(small_g, small, n_heads)
    g_out.update(sg)
    for dst, row in ((d_out, sd), (m_out, sm), (v_out, sv)):
        dst.update(_unpack_small(row, small, n_heads)[0])

    return (loss, dx[None], *[g_out[n] for n in _ORDER], *[d_out[n] for n in _ORDER],
            *[m_out[n] for n in _ORDER], *[v_out[n] for n in _ORDER])
```

```python
import functools
import math

import numpy as np
import jax
import jax.numpy as jnp
from jax import lax
from jax.experimental import pallas as pl
from jax.experimental.pallas import tpu as pltpu

F32 = jnp.float32
BF16 = jnp.bfloat16

N_DEV = 8
N_CHIP = 4
HEAD_DIM = 128
GROUP = 4
GRID_W = 64
WINDOW = 128
BLOCK_Q = 128
N_BUCKETS = 32
MAX_DISTANCE = 128
ROPE_THETA = 10000.0
EPS = 1e-6
NEG_INF = -1e30
ADAM_LR = 0.001
ADAM_B1 = 0.9
ADAM_B2 = 0.999
ADAM_EPS = 1e-08
ADAM_WD = 0.01
ADAM_STEP = 10
LOG2E = math.log2(math.e)
LANES = 128
SUBLANES = 8
MESH = pl.DeviceIdType.MESH

_NT = (((1,), (1,)), ((), ()))
_NN = (((1,), (0,)), ((), ()))
_TN = (((0,), (0,)), ((), ()))


def _tile(dim, pref):
    return pref if dim % pref == 0 else dim


def _params(sem):
    return pltpu.CompilerParams(dimension_semantics=sem, vmem_limit_bytes=56 * 1024 * 1024)


_HBM = pl.BlockSpec(memory_space=pltpu.HBM)
_SEM = pl.BlockSpec(memory_space=pltpu.SEMAPHORE)
_ANY = pl.BlockSpec(memory_space=pl.ANY)
_VMEM = pl.BlockSpec(memory_space=pltpu.VMEM)
_EFFECT = pltpu.SideEffectType.DATAFLOW_SIDE_EFFECTING


def _pcall(body, deps=(), *, in_specs, **kw):
    deps = [d for d in deps if d is not None]
    nd = len(deps)

    def wrapped(*refs):
        body(*refs[nd:])

    call = pl.pallas_call(wrapped, in_specs=[_ANY] * nd + list(in_specs), **kw)
    return lambda *args: call(*deps, *args)


def _mm(name, a, b, dims, grid, a_spec, b_spec, out_shape, out_specs, acc_shape, epilogue,
        extras=(), extra_specs=(), deps=()):
    nk = grid[2]
    n_extra = len(extras)

    def body(*refs):
        a_ref, b_ref = refs[0], refs[1]
        extra = refs[2:2 + n_extra]
        outs = refs[2 + n_extra:-1]
        acc = refs[-1]
        part = lax.dot_general(a_ref[...], b_ref[...], dims, preferred_element_type=F32)
        if nk == 1:
            epilogue(part, extra, outs)
        else:
            k = pl.program_id(2)

            @pl.when(k == 0)
            def _():
                acc[...] = part

            @pl.when(k > 0)
            def _():
                acc[...] += part

            @pl.when(k == nk - 1)
            def _():
                epilogue(acc[...], extra, outs)

    return _pcall(
        body, deps, name=name, grid=grid,
        in_specs=[a_spec, b_spec, *extra_specs],
        out_specs=out_specs, out_shape=out_shape,
        scratch_shapes=[pltpu.VMEM(acc_shape if nk > 1 else (SUBLANES, LANES), F32)],
        compiler_params=_params(("parallel", "parallel", "arbitrary")),
    )(a, b, *extras)


def _store(dtype):
    def ep(acc, extra, outs):
        outs[0][...] = acc.astype(dtype)
    return ep


def _store_add(acc, extra, outs):
    outs[0][...] = acc + extra[0][...]


def _mm_nn(name, a, b, out_dtype=F32, epilogue=None, extras=(), n_out=1, out_dtypes=None, tm=1024, tn=1024, tk=None,
           deps=()):
    M, K = a.shape
    N = b.shape[1]
    tm, tn, tk = _tile(M, tm), _tile(N, tn), _tile(K, tk or K)
    b_spec = pl.BlockSpec((tk, tn), lambda i, j, k: (k, j))
    grid = (M // tm, N // tn, K // tk)
    o_spec = pl.BlockSpec((tm, tn), lambda i, j, k: (i, j))
    out_dtypes = out_dtypes or [out_dtype] * n_out
    out_shape = [jax.ShapeDtypeStruct((M, N), d) for d in out_dtypes]
    res = _mm(name, a, b, _NN, grid, pl.BlockSpec((tm, tk), lambda i, j, k: (i, k)), b_spec,
              out_shape, [o_spec] * len(out_dtypes), (tm, tn), epilogue or _store(out_dtype),
              extras, [o_spec] * len(extras), deps)
    return res if len(out_dtypes) > 1 else res[0]


def _mm_nt(name, a, b, out_dtype=F32, epilogue=None, extras=(), tm=1024, tn=1024, tk=None, deps=()):
    M, C = a.shape
    N = b.shape[0]
    tm, tn, tk = _tile(M, tm), _tile(N, tn), _tile(C, tk or C)
    b_spec = pl.BlockSpec((tn, tk), lambda i, j, k: (j, k))
    grid = (M // tm, N // tn, C // tk)
    o_spec = pl.BlockSpec((tm, tn), lambda i, j, k: (i, j))
    return _mm(name, a, b, _NT, grid, pl.BlockSpec((tm, tk), lambda i, j, k: (i, k)), b_spec,
               [jax.ShapeDtypeStruct((M, N), out_dtype)], [o_spec], (tm, tn), epilogue or _store(out_dtype),
               extras, [o_spec] * len(extras), deps)[0]


def _mm_tn(name, a, b, tm=1024, tn=512, tk=None, deps=()):
    T, M = a.shape
    N = b.shape[1]
    tm, tn, tk = _tile(M, tm), _tile(N, tn), _tile(T, tk or T)
    out_shape = jax.ShapeDtypeStruct((M, N), F32)
    o_spec = pl.BlockSpec((tm, tn), lambda i, j, k: (i, j))
    grid = (M // tm, N // tn, T // tk)
    return _mm(name, a, b, _TN, grid, pl.BlockSpec((tk, tm), lambda i, j, k: (k, i)),
               pl.BlockSpec((tk, tn), lambda i, j, k: (k, j)), [out_shape], [o_spec], (tm, tn), _store(F32),
               deps=deps)[0]


def _mean_last(v):
    return jnp.mean(v, axis=-1, keepdims=True)


def _rows_to_sublanes(v):
    r, c = v.shape
    return jnp.sum(v.reshape(r // SUBLANES, SUBLANES, c), axis=0)


def _accumulate(ref, val, first):
    @pl.when(first)
    def _():
        ref[...] = val

    @pl.when(jnp.logical_not(first))
    def _():
        ref[...] += val


def _rms_fwd(name, x, g, tr=256, deps=()):
    T, D = x.shape
    tr = _tile(T, tr)

    def body(x_ref, g_ref, o_ref):
        xv = x_ref[...]
        r = lax.rsqrt(_mean_last(xv * xv) + EPS)
        o_ref[...] = (xv * r * g_ref[...]).astype(BF16)

    row = pl.BlockSpec((tr, D), lambda i: (i, 0))
    return _pcall(
        body, deps, name=name, grid=(T // tr,),
        in_specs=[row, pl.BlockSpec((1, D), lambda i: (0, 0))],
        out_specs=row, out_shape=jax.ShapeDtypeStruct((T, D), BF16),
        compiler_params=_params(("parallel",)),
    )(x, g)


def _rms_bwd(name, dyn, x, g, dres, tr=256, deps=()):
    T, D = x.shape
    tr = _tile(T, tr)

    def body(dy_ref, x_ref, g_ref, dr_ref, dx_ref, dxb_ref, dg_ref):
        xv = x_ref[...]
        r = lax.rsqrt(_mean_last(xv * xv) + EPS)
        xn = xv * r
        dy = dy_ref[...]
        dxn = dy * g_ref[...]
        dx = dr_ref[...] + r * (dxn - xn * _mean_last(dxn * xn))
        dx_ref[...] = dx
        dxb_ref[...] = dx.astype(BF16)
        _accumulate(dg_ref, _rows_to_sublanes(dy * xn), pl.program_id(0) == 0)

    row = pl.BlockSpec((tr, D), lambda i: (i, 0))
    return _pcall(
        body, deps, name=name, grid=(T // tr,),
        in_specs=[row, row, pl.BlockSpec((1, D), lambda i: (0, 0)), row],
        out_specs=[row, row, pl.BlockSpec((SUBLANES, D), lambda i: (0, 0))],
        out_shape=[jax.ShapeDtypeStruct((T, D), F32), jax.ShapeDtypeStruct((T, D), BF16),
                   jax.ShapeDtypeStruct((SUBLANES, D), F32)],
        compiler_params=_params(("arbitrary",)),
    )(dyn, x, g, dres)


def _tail(h2, z, pe, target, g_ple, g_final, tr=256):
    T, D = h2.shape
    tr = _tile(T, tr)

    def body(h2_ref, z_ref, pe_ref, t_ref, gp_ref, gf_ref,
             dh3_ref, dz_ref, dpe_ref, dgf_ref, dgp_ref, loss_ref):
        first = pl.program_id(0) == 0
        pev = pe_ref[...]
        r3 = lax.rsqrt(_mean_last(pev * pev) + EPS)
        en = pev * r3
        e = en * gp_ref[...]
        gate = 1.0 / (1.0 + jnp.exp(-z_ref[...]))
        h3 = h2_ref[...] + gate * e
        r5 = lax.rsqrt(_mean_last(h3 * h3) + EPS)
        hn = h3 * r5
        diff = hn * gf_ref[...] - t_ref[...]
        loss_rows = 0.5 * _mean_last(diff * diff)
        row0 = lax.broadcasted_iota(jnp.int32, (SUBLANES, LANES), 0) == 0
        _accumulate(loss_ref, jnp.where(row0, jnp.sum(loss_rows), 0.0), first)
        dy = diff * (1.0 / D)
        _accumulate(dgf_ref, _rows_to_sublanes(dy * hn), first)
        dhn = dy * gf_ref[...]
        dh3 = r5 * (dhn - hn * _mean_last(dhn * hn))
        dh3_ref[...] = dh3
        dgate = dh3 * e
        de = dh3 * gate
        dz_ref[...] = (dgate * gate * (1.0 - gate)).astype(BF16)
        _accumulate(dgp_ref, _rows_to_sublanes(de * en), first)
        den = de * gp_ref[...]
        dpe_ref[...] = (r3 * (den - en * _mean_last(den * en))).astype(BF16)

    row = pl.BlockSpec((tr, D), lambda i: (i, 0))
    vec = pl.BlockSpec((1, D), lambda i: (0, 0))
    part = pl.BlockSpec((SUBLANES, D), lambda i: (0, 0))
    return pl.pallas_call(
        body, name="tail", grid=(T // tr,),
        in_specs=[row, row, row, row, vec, vec],
        out_specs=[row, row, row, part, part, pl.BlockSpec((SUBLANES, LANES), lambda i: (0, 0))],
        out_shape=[jax.ShapeDtypeStruct((T, D), F32), jax.ShapeDtypeStruct((T, D), BF16),
                   jax.ShapeDtypeStruct((T, D), BF16), jax.ShapeDtypeStruct((SUBLANES, D), F32),
                   jax.ShapeDtypeStruct((SUBLANES, D), F32), jax.ShapeDtypeStruct((SUBLANES, LANES), F32)],
        compiler_params=_params(("arbitrary",)),
    )(h2, z, pe, target, g_ple, g_final)


def _rope_tables(T):
    pos = np.arange(T)
    half = HEAD_DIM // 2
    inv = (ROPE_THETA ** (-np.arange(0, half, 2, dtype=np.float32) / half)).astype(np.float32)
    ang_r = (pos // GRID_W).astype(np.float32)[:, None] * inv
    ang_c = (pos % GRID_W).astype(np.float32)[:, None] * inv
    cos = np.concatenate([np.cos(ang_r), np.cos(ang_r), np.cos(ang_c), np.cos(ang_c)], axis=-1)
    sin = np.concatenate([-np.sin(ang_r), np.sin(ang_r), -np.sin(ang_c), np.sin(ang_c)], axis=-1)
    return jnp.asarray(cos, F32), jnp.asarray(sin, F32)


def _swap32(x):
    lane = lax.broadcasted_iota(jnp.int32, x.shape, 1)
    return jnp.where((lane % 64) < 32, pltpu.roll(x, 96, 1), pltpu.roll(x, 32, 1))


def _qk_prep(proj, cos, sin, g_q, g_k, n_norm, tr=256):
    T, W = proj.shape
    tr = _tile(T, tr)
    n_q = n_norm * GROUP // (GROUP + 1)

    def body(p_ref, c_ref, s_ref, gq_ref, gk_ref, o_ref):
        c, s = c_ref[...], s_ref[...]
        for h in range(n_norm):
            cols = slice(h * HEAD_DIM, (h + 1) * HEAD_DIM)
            xv = p_ref[:, cols]
            g = gq_ref[...] if h < n_q else gk_ref[...]
            xn = xv * lax.rsqrt(_mean_last(xv * xv) + EPS) * g
            o_ref[:, cols] = (xn * c + _swap32(xn) * s).astype(BF16)
        rest = slice(n_norm * HEAD_DIM, W)
        o_ref[:, rest] = p_ref[:, rest].astype(BF16)

    row = pl.BlockSpec((tr, W), lambda i: (i, 0))
    tab = pl.BlockSpec((tr, HEAD_DIM), lambda i: (i, 0))
    vec = pl.BlockSpec((1, HEAD_DIM), lambda i: (0, 0))
    return pl.pallas_call(
        body, name="qk_prep", grid=(T // tr,),
        in_specs=[row, tab, tab, vec, vec], out_specs=row,
        out_shape=jax.ShapeDtypeStruct((T, W), BF16),
        compiler_params=_params(("parallel",)),
    )(proj, cos, sin, g_q, g_k)


def _dproj(proj, dqa, dka, dva, dqb, dkb, dvb, cos, sin, g_q, g_k, tr=256):
    T, W = proj.shape
    tr = _tile(T, tr)
    n_q = dqa.shape[1] // HEAD_DIM
    n_kv = dka.shape[1] // HEAD_DIM
    wa = (n_q + n_kv) * HEAD_DIM

    def body(p_ref, dqa_ref, dka_ref, dva_ref, dqb_ref, dkb_ref, dvb_ref, c_ref, s_ref, gq_ref, gk_ref,
             o_ref, dgq_ref, dgk_ref):
        c, s = c_ref[...], s_ref[...]
        dgq = jnp.zeros((SUBLANES, HEAD_DIM), F32)
        dgk = jnp.zeros((SUBLANES, HEAD_DIM), F32)
        for h in range(n_q + n_kv):
            cols = slice(h * HEAD_DIM, (h + 1) * HEAD_DIM)
            xv = p_ref[:, cols]
            r = lax.rsqrt(_mean_last(xv * xv) + EPS)
            xn = xv * r
            if h < n_q:
                d = dqa_ref[:, cols]
                g = gq_ref[...]
            else:
                d = dka_ref[:, (h - n_q) * HEAD_DIM:(h - n_q + 1) * HEAD_DIM]
                g = gk_ref[...]
            dqn = d * c + _swap32(d * s)
            part = _rows_to_sublanes(dqn * xn)
            if h < n_q:
                dgq = dgq + part
            else:
                dgk = dgk + part
            dxn = dqn * g
            o_ref[:, cols] = (r * (dxn - xn * _mean_last(dxn * xn))).astype(BF16)
        off = wa
        for ref in (dva_ref, dqb_ref, dkb_ref, dvb_ref):
            w = ref.shape[1]
            o_ref[:, off:off + w] = ref[...].astype(BF16)
            off += w
        first = pl.program_id(0) == 0
        _accumulate(dgq_ref, dgq, first)
        _accumulate(dgk_ref, dgk, first)

    def row(w):
        return pl.BlockSpec((tr, w), lambda i: (i, 0))

    vec = pl.BlockSpec((1, HEAD_DIM), lambda i: (0, 0))
    part = pl.BlockSpec((SUBLANES, HEAD_DIM), lambda i: (0, 0))
    return pl.pallas_call(
        body, name="dproj", grid=(T // tr,),
        in_specs=[row(wa), row(dqa.shape[1]), row(dka.shape[1]), row(dva.shape[1]), row(dqb.shape[1]),
                  row(dkb.shape[1]), row(dvb.shape[1]), row(HEAD_DIM), row(HEAD_DIM), vec, vec],
        out_specs=[row(W), part, part],
        out_shape=[jax.ShapeDtypeStruct((T, W), BF16), jax.ShapeDtypeStruct((SUBLANES, HEAD_DIM), F32),
                   jax.ShapeDtypeStruct((SUBLANES, HEAD_DIM), F32)],
        compiler_params=_params(("arbitrary",)),
    )(proj, dqa, dka, dva, dqb, dkb, dvb, cos, sin, g_q, g_k)


def _attn_a_fwd(pb, n_q, n_kv, tq=1024, tc=1024):
    T = pb.shape[0]
    tq, tc = _tile(T, tq), _tile(T, tc)
    scale = HEAD_DIM ** -0.5
    c = scale * LOG2E

    def body(q_ref, k_ref, v_ref, o_ref, lse_ref):
        q = q_ref[...]
        m = l = acc = None
        for j in range(T // tc):
            keys = slice(j * tc, (j + 1) * tc)
            s = lax.dot_general(q, k_ref[keys, :], _NT, preferred_element_type=F32)
            mj = jnp.max(s, axis=-1, keepdims=True)
            m_new = mj if j == 0 else jnp.maximum(m, mj)
            p = jnp.exp2((s - m_new) * c)
            pv = lax.dot_general(p.astype(BF16), v_ref[keys, :], _NN, preferred_element_type=F32)
            if j == 0:
                l, acc = jnp.sum(p, axis=-1, keepdims=True), pv
            else:
                alpha = jnp.exp2((m - m_new) * c)
                l = alpha * l + jnp.sum(p, axis=-1, keepdims=True)
                acc = alpha * acc + pv
            m = m_new
        o_ref[...] = (acc / l).astype(BF16)
        lse_ref[...] = m * scale + jnp.log(l)

    return pl.pallas_call(
        body, name="attn_a_fwd", grid=(n_kv, GROUP, T // tq),
        in_specs=[pl.BlockSpec((tq, HEAD_DIM), lambda kv, g, i: (i, kv * GROUP + g)),
                  pl.BlockSpec((T, HEAD_DIM), lambda kv, g, i: (0, n_q + kv)),
                  pl.BlockSpec((T, HEAD_DIM), lambda kv, g, i: (0, n_q + n_kv + kv))],
        out_specs=[pl.BlockSpec((tq, HEAD_DIM), lambda kv, g, i: (i, kv * GROUP + g)),
                   pl.BlockSpec((None, tq, 1), lambda kv, g, i: (kv * GROUP + g, i, 0))],
        out_shape=[jax.ShapeDtypeStruct((T, n_q * HEAD_DIM), BF16), jax.ShapeDtypeStruct((n_q, T, 1), F32)],
        compiler_params=_params(("parallel", "parallel", "parallel")),
    )(pb, pb, pb)


def _attn_a_bwd(pb, o_cat, d_o, lse, n_q, n_kv, tq=512, tc=1024):
    T = pb.shape[0]
    tq, tc = _tile(T, tq), _tile(T, tc)
    scale = HEAD_DIM ** -0.5
    c = scale * LOG2E

    def body(q_ref, k_ref, v_ref, o_ref, do_ref, lse_ref, dq_ref, dk_ref, dv_ref):
        q, do = q_ref[...], do_ref[...]
        delta = jnp.sum(do.astype(F32) * o_ref[...].astype(F32), axis=-1, keepdims=True)
        lse2 = lse_ref[...] * LOG2E
        first = jnp.logical_and(pl.program_id(1) == 0, pl.program_id(2) == 0)
        dq = None
        for j in range(T // tc):
            keys = slice(j * tc, (j + 1) * tc)
            kc, vc = k_ref[keys, :], v_ref[keys, :]
            s = lax.dot_general(q, kc, _NT, preferred_element_type=F32)
            p = jnp.exp2(s * c - lse2)
            dp = lax.dot_general(do, vc, _NT, preferred_element_type=F32)
            ds = (p * (dp - delta) * scale).astype(BF16)
            dqj = lax.dot_general(ds, kc, _NN, preferred_element_type=F32)
            dq = dqj if dq is None else dq + dqj
            _accumulate(dv_ref.at[keys, :], lax.dot_general(p.astype(BF16), do, _TN, preferred_element_type=F32), first)
            _accumulate(dk_ref.at[keys, :], lax.dot_general(ds, q, _TN, preferred_element_type=F32), first)
        dq_ref[...] = dq

    qmap = lambda kv, g, i: (i, kv * GROUP + g)
    return pl.pallas_call(
        body, name="attn_a_bwd", grid=(n_kv, GROUP, T // tq),
        in_specs=[pl.BlockSpec((tq, HEAD_DIM), qmap),
                  pl.BlockSpec((T, HEAD_DIM), lambda kv, g, i: (0, n_q + kv)),
                  pl.BlockSpec((T, HEAD_DIM), lambda kv, g, i: (0, n_q + n_kv + kv)),
                  pl.BlockSpec((tq, HEAD_DIM), qmap),
                  pl.BlockSpec((tq, HEAD_DIM), qmap),
                  pl.BlockSpec((None, tq, 1), lambda kv, g, i: (kv * GROUP + g, i, 0))],
        out_specs=[pl.BlockSpec((tq, HEAD_DIM), qmap),
                   pl.BlockSpec((T, HEAD_DIM), lambda kv, g, i: (0, kv)),
                   pl.BlockSpec((T, HEAD_DIM), lambda kv, g, i: (0, kv))],
        out_shape=[jax.ShapeDtypeStruct((T, n_q * HEAD_DIM), F32),
                   jax.ShapeDtypeStruct((T, n_kv * HEAD_DIM), F32),
                   jax.ShapeDtypeStruct((T, n_kv * HEAD_DIM), F32)],
        compiler_params=_params(("parallel", "arbitrary", "arbitrary")),
    )(pb, pb, pb, o_cat, d_o, lse)


def _bucket_index():
    r = np.arange(BLOCK_Q)[:, None]
    j = np.arange(3 * BLOCK_Q)[None, :]
    rel = (j - BLOCK_Q) - r
    nb = N_BUCKETS // 2
    ret = np.where(rel > 0, nb, 0)
    n = np.abs(rel)
    max_exact = nb // 2
    nf = np.maximum(n, 1).astype(np.float32)
    large = max_exact + (np.log(nf / max_exact) / math.log(MAX_DISTANCE / max_exact) * (nb - max_exact)).astype(np.int32)
    large = np.minimum(large, nb - 1)
    return jnp.asarray(ret + np.where(n < max_exact, n, large), jnp.int32)


def _bias_build(idx, table_flat, n_heads, deps=()):
    def body(idx_ref, tab_ref, o_ref):
        h = pl.program_id(0)
        iv = idx_ref[...]
        acc = jnp.zeros(iv.shape, F32)
        for b in range(N_BUCKETS):
            acc = jnp.where(iv == b, tab_ref[b * n_heads + h], acc)
        r = lax.broadcasted_iota(jnp.int32, iv.shape, 0)
        j = lax.broadcasted_iota(jnp.int32, iv.shape, 1)
        o_ref[...] = jnp.where(jnp.abs(j - BLOCK_Q - r) <= WINDOW, acc, NEG_INF)

    return _pcall(
        body, deps, name="bias_build", grid=(n_heads,),
        in_specs=[pl.BlockSpec(idx.shape, lambda h: (0, 0)), pl.BlockSpec(memory_space=pltpu.SMEM)],
        out_specs=pl.BlockSpec((None,) + idx.shape, lambda h: (h, 0, 0)),
        out_shape=jax.ShapeDtypeStruct((n_heads,) + idx.shape, F32),
        compiler_params=_params(("parallel",)),
    )(idx, table_flat)


def _in_sequence(n, T):
    j = lax.broadcasted_iota(jnp.int32, (BLOCK_Q, 3 * BLOCK_Q), 1)
    kabs = n * BLOCK_Q + j - BLOCK_Q
    return (kabs >= 0) & (kabs < T)


def _band_specs(col, nblk, sb):
    return [pl.BlockSpec((BLOCK_Q, HEAD_DIM), lambda kv, i: (jnp.maximum(sb * i - 1, 0), col(kv))),
            pl.BlockSpec((sb * BLOCK_Q, HEAD_DIM), lambda kv, i: (i, col(kv))),
            pl.BlockSpec((BLOCK_Q, HEAD_DIM), lambda kv, i: (jnp.minimum(sb * i + sb, nblk - 1), col(kv)))]


def _head_specs(base, rows):
    return [pl.BlockSpec((rows, HEAD_DIM), functools.partial(lambda kv, i, g: (i, base + kv * GROUP + g), g=g))
            for g in range(GROUP)]


def _attn_b_fwd(pb, bias, sink, q_off, n_q, n_kv, deps=(), sb=8):
    T = pb.shape[0]
    nblk = T // BLOCK_Q
    sb = min(sb, nblk)
    tq = sb * BLOCK_Q
    scale = HEAD_DIM ** -0.5

    def body(*refs):
        q_refs = refs[0:GROUP]
        k_refs, v_refs = refs[GROUP:GROUP + 3], refs[GROUP + 3:GROUP + 6]
        bias_ref, sink_ref, o_ref, lse_ref = refs[GROUP + 6:]
        kv, i = pl.program_id(0), pl.program_id(1)
        kb = jnp.concatenate([r[...] for r in k_refs], axis=0)
        vb = jnp.concatenate([r[...] for r in v_refs], axis=0)
        for b in range(sb):
            at_end = b == 0 or b == sb - 1
            mask = _in_sequence(i * sb + b, T) if at_end else None
            rows = slice(b * BLOCK_Q, (b + 1) * BLOCK_Q)
            kw, vw = kb[b * BLOCK_Q:(b + 3) * BLOCK_Q], vb[b * BLOCK_Q:(b + 3) * BLOCK_Q]
            for g in range(GROUP):
                sk = sink_ref[kv * GROUP + g]
                s = lax.dot_general(q_refs[g][rows, :], kw, _NT, preferred_element_type=F32) * scale + bias_ref[g]
                if at_end:
                    s = jnp.where(mask, s, NEG_INF)
                m = jnp.maximum(jnp.max(s, axis=-1, keepdims=True), sk)
                p = jnp.exp(s - m)
                l = jnp.sum(p, axis=-1, keepdims=True) + jnp.exp(sk - m)
                o = lax.dot_general(p.astype(BF16), vw, _NN, preferred_element_type=F32)
                o_ref[rows, g * HEAD_DIM:(g + 1) * HEAD_DIM] = (o / l).astype(BF16)
                lse_ref[g, rows, :] = m + jnp.log(l)

    return _pcall(
        body, deps, name="attn_b_fwd", grid=(n_kv, nblk // sb),
        in_specs=[*_head_specs(q_off, tq),
                  *_band_specs(lambda kv: q_off + n_q + kv, nblk, sb),
                  *_band_specs(lambda kv: q_off + n_q + n_kv + kv, nblk, sb),
                  pl.BlockSpec((GROUP, BLOCK_Q, 3 * BLOCK_Q), lambda kv, i: (kv, 0, 0)),
                  pl.BlockSpec(memory_space=pltpu.SMEM)],
        out_specs=[pl.BlockSpec((tq, GROUP * HEAD_DIM), lambda kv, i: (i, kv)),
                   pl.BlockSpec((GROUP, tq, 1), lambda kv, i: (kv, i, 0))],
        out_shape=[jax.ShapeDtypeStruct((T, n_q * HEAD_DIM), BF16), jax.ShapeDtypeStruct((n_q, T, 1), F32)],
        compiler_params=_params(("parallel", "parallel")),
    )(*([pb] * (GROUP + 6)), bias, sink)


def _attn_b_bwd(pb, o_cat, d_o, lse, bias, sink, q_off, n_q, n_kv, o_off, deps=(), sb=8):
    T = pb.shape[0]
    nblk = T // BLOCK_Q
    sb = min(sb, nblk)
    tq = sb * BLOCK_Q
    scale = HEAD_DIM ** -0.5

    def body(*refs):
        q_refs = refs[0:GROUP]
        k_refs, v_refs = refs[GROUP:GROUP + 3], refs[GROUP + 3:GROUP + 6]
        o_refs, do_refs = refs[GROUP + 6:2 * GROUP + 6], refs[2 * GROUP + 6:3 * GROUP + 6]
        lse_ref, bias_ref, sink_ref, dq_ref, dk_ref, dv_ref, dbias_ref, dsink_ref, dkb_ref, dvb_ref = refs[3 * GROUP + 6:]
        kv, i = pl.program_id(0), pl.program_id(1)
        first = i == 0
        kb = jnp.concatenate([r[...] for r in k_refs], axis=0)
        vb = jnp.concatenate([r[...] for r in v_refs], axis=0)
        dkb_ref[...] = jnp.zeros(dkb_ref.shape, F32)
        dvb_ref[...] = jnp.zeros(dvb_ref.shape, F32)
        row = lax.broadcasted_iota(jnp.int32, (SUBLANES, LANES), 0)
        dsink = jnp.zeros((SUBLANES, LANES), F32)
        for b in range(sb):
            at_end = b == 0 or b == sb - 1
            mask = _in_sequence(i * sb + b, T) if at_end else None
            rows = slice(b * BLOCK_Q, (b + 1) * BLOCK_Q)
            win = slice(b * BLOCK_Q, (b + 3) * BLOCK_Q)
            kw, vw = kb[win], vb[win]
            dkw = jnp.zeros((3 * BLOCK_Q, HEAD_DIM), F32)
            dvw = jnp.zeros((3 * BLOCK_Q, HEAD_DIM), F32)
            for g in range(GROUP):
                sk = sink_ref[kv * GROUP + g]
                q, do = q_refs[g][rows, :], do_refs[g][rows, :]
                lse_g = lse_ref[g, rows, :]
                delta = jnp.sum(do.astype(F32) * o_refs[g][rows, :].astype(F32), axis=-1, keepdims=True)
                s = lax.dot_general(q, kw, _NT, preferred_element_type=F32) * scale + bias_ref[g]
                if at_end:
                    s = jnp.where(mask, s, NEG_INF)
                p = jnp.exp(s - lse_g)
                dp = lax.dot_general(do, vw, _NT, preferred_element_type=F32)
                ds = p * (dp - delta)
                _accumulate(dbias_ref.at[g], ds, jnp.logical_and(first, b == 0))
                dsink = dsink + jnp.where(row == g, -jnp.sum(jnp.exp(sk - lse_g) * delta), 0.0)
                dsb = (ds * scale).astype(BF16)
                dq_ref[rows, g * HEAD_DIM:(g + 1) * HEAD_DIM] = lax.dot_general(dsb, kw, _NN, preferred_element_type=F32)
                dkw = dkw + lax.dot_general(dsb, q, _TN, preferred_element_type=F32)
                dvw = dvw + lax.dot_general(p.astype(BF16), do, _TN, preferred_element_type=F32)
            dkb_ref[win, :] += dkw
            dvb_ref[win, :] += dvw
        _accumulate(dsink_ref, dsink, first)

        @pl.when(first)
        def _():
            dk_ref[...] = jnp.zeros(dk_ref.shape, F32)
            dv_ref[...] = jnp.zeros(dv_ref.shape, F32)

        before = pl.ds(pl.multiple_of(jnp.maximum(sb * i - 1, 0) * BLOCK_Q, BLOCK_Q), BLOCK_Q)
        own = pl.ds(pl.multiple_of(i * tq, BLOCK_Q), tq)
        after = pl.ds(pl.multiple_of(jnp.minimum(sb * i + sb, nblk - 1) * BLOCK_Q, BLOCK_Q), BLOCK_Q)
        for acc_ref, band_ref in ((dk_ref, dkb_ref), (dv_ref, dvb_ref)):
            acc_ref[before, :] += band_ref[0:BLOCK_Q, :]
            acc_ref[own, :] += band_ref[BLOCK_Q:BLOCK_Q + tq, :]
            acc_ref[after, :] += band_ref[BLOCK_Q + tq:, :]

    return _pcall(
        body, deps, name="attn_b_bwd", grid=(n_kv, nblk // sb),
        in_specs=[*_head_specs(q_off, tq),
                  *_band_specs(lambda kv: q_off + n_q + kv, nblk, sb),
                  *_band_specs(lambda kv: q_off + n_q + n_kv + kv, nblk, sb),
                  *_head_specs(o_off, tq), *_head_specs(o_off, tq),
                  pl.BlockSpec((GROUP, tq, 1), lambda kv, i: (kv, i, 0)),
                  pl.BlockSpec((GROUP, BLOCK_Q, 3 * BLOCK_Q), lambda kv, i: (kv, 0, 0)),
                  pl.BlockSpec(memory_space=pltpu.SMEM)],
        out_specs=[pl.BlockSpec((tq, GROUP * HEAD_DIM), lambda kv, i: (i, kv)),
                   pl.BlockSpec((T, HEAD_DIM), lambda kv, i: (0, kv)),
                   pl.BlockSpec((T, HEAD_DIM), lambda kv, i: (0, kv)),
                   pl.BlockSpec((GROUP, BLOCK_Q, 3 * BLOCK_Q), lambda kv, i: (kv, 0, 0)),
                   pl.BlockSpec((None, SUBLANES, LANES), lambda kv, i: (kv, 0, 0))],
        out_shape=[jax.ShapeDtypeStruct((T, n_q * HEAD_DIM), F32),
                   jax.ShapeDtypeStruct((T, n_kv * HEAD_DIM), F32),
                   jax.ShapeDtypeStruct((T, n_kv * HEAD_DIM), F32),
                   jax.ShapeDtypeStruct((n_q, BLOCK_Q, 3 * BLOCK_Q), F32),
                   jax.ShapeDtypeStruct((n_kv, SUBLANES, LANES), F32)],
        scratch_shapes=[pltpu.VMEM((tq + 2 * BLOCK_Q, HEAD_DIM), F32), pltpu.VMEM((tq + 2 * BLOCK_Q, HEAD_DIM), F32)],
        compiler_params=_params(("parallel", "arbitrary")),
    )(*([pb] * (GROUP + 6)), *([o_cat] * GROUP), *([d_o] * GROUP), lse, bias, sink)


def _table_grads(dbias, dsink_raw, idx):
    n_heads = dbias.shape[0]
    n_kv = dsink_raw.shape[0]

    def body(db_ref, ds_ref, idx_ref, dt_ref, dsk_ref):
        iv = idx_ref[...]
        row = lax.broadcasted_iota(jnp.int32, (SUBLANES, LANES), 0)
        lane = lax.broadcasted_iota(jnp.int32, (SUBLANES, LANES), 1)
        dsk = jnp.zeros((SUBLANES, LANES), F32)
        for h in range(n_heads):
            d = db_ref[h]
            acc = jnp.zeros((SUBLANES, LANES), F32)
            for b in range(N_BUCKETS):
                acc = jnp.where((row == 0) & (lane == b), jnp.sum(jnp.where(iv == b, d, 0.0)), acc)
            dt_ref[:, h * LANES:(h + 1) * LANES] = acc
            raw = ds_ref[h // GROUP]
            val = jnp.sum(jnp.where((row == h % GROUP) & (lane == 0), raw, 0.0))
            dsk = jnp.where((row == 0) & (lane == h), val, dsk)
        dsk_ref[...] = dsk

    return pl.pallas_call(
        body, name="table_grads",
        in_specs=[pl.BlockSpec(memory_space=pltpu.VMEM)] * 3,
        out_specs=[pl.BlockSpec(memory_space=pltpu.VMEM)] * 2,
        out_shape=[jax.ShapeDtypeStruct((SUBLANES, n_heads * LANES), F32),
                   jax.ShapeDtypeStruct((SUBLANES, LANES), F32)],
        compiler_params=pltpu.CompilerParams(vmem_limit_bytes=56 * 1024 * 1024),
    )(dbias, dsink_raw, idx)


def _position():
    x, y, c = lax.axis_index("x"), lax.axis_index("y"), lax.axis_index("c")
    return x, y, c


def _hbm(a):
    return pltpu.with_memory_space_constraint(a, pltpu.HBM)


def _split_start(name, bufs, sem_shapes, issue):
    nb, ns = len(bufs), len(sem_shapes)

    def body(*refs):
        buf_refs = refs[:nb]
        sems = refs[nb:nb + ns]
        token = refs[nb + ns + nb]
        issue(buf_refs, sems)
        token[...] = jnp.zeros(token.shape, F32)

    outs = pl.pallas_call(
        body, name=name,
        in_specs=[_HBM] * nb,
        out_specs=[_SEM] * ns + [_HBM] * nb + [_VMEM],
        out_shape=[pltpu.SemaphoreType.DMA(s) for s in sem_shapes] + [pltpu.HBM(b.shape, b.dtype) for b in bufs]
        + [jax.ShapeDtypeStruct((SUBLANES, LANES), F32)],
        input_output_aliases={i: ns + i for i in range(nb)},
        compiler_params=pltpu.CompilerParams(has_side_effects=_EFFECT),
    )(*[_hbm(b) for b in bufs])
    return outs[:ns], outs[ns:ns + nb], outs[-1]


def _split_wait(name, bufs, send, recv, counts, size_of, after):
    nb = len(bufs)

    def body(*refs):
        buf_refs = refs[:nb]
        send_ref, recv_ref = refs[nb], refs[nb + 1]
        x, y, c = _position()
        for w, n in enumerate(counts):
            ref = size_of(buf_refs, w)
            for k in range(n):
                s = sum(counts[:w]) + k
                cp = pltpu.make_async_remote_copy(
                    src_ref=ref, dst_ref=ref, send_sem=send_ref.at[s], recv_sem=recv_ref.at[s],
                    device_id=(x, y, c), device_id_type=MESH)
                cp.wait_send()
                cp.wait_recv()

    return pl.pallas_call(
        body, name=name,
        in_specs=[_HBM] * nb + [_SEM, _SEM, _ANY],
        out_specs=[_HBM] * nb,
        out_shape=[pltpu.HBM(b.shape, b.dtype) for b in bufs],
        input_output_aliases={i: i for i in range(nb)},
        compiler_params=pltpu.CompilerParams(has_side_effects=_EFFECT),
    )(*bufs, send, recv, after)


def _block_of(pos):
    return 4 * pos[0] + 2 * pos[1] + pos[2]


def _shard_of(ref, blk, by_cols):
    aligned = (lambda v, a: v) if isinstance(blk, int) else pl.multiple_of
    if by_cols:
        n = ref.shape[1] // N_DEV
        return ref.at[:, pl.ds(aligned(blk * n, LANES), n)]
    r = ref.shape[0] // N_DEV
    return ref.at[pl.ds(aligned(blk * r, SUBLANES), r), :]


def _place_shards(shards, by_cols):
    mine = _block_of(_position()).astype(jnp.int32).reshape(1)

    def place(name, s, cols, tr=256):
        r, n = s.shape
        tr = _tile(r, tr)

        def body(m_ref, s_ref, o_ref):
            o_ref[...] = s_ref[...].astype(BF16)

        if cols:
            out = pl.BlockSpec((tr, n), lambda i, m_ref: (i, m_ref[0]))
        else:
            out = pl.BlockSpec((tr, n), lambda i, m_ref: (m_ref[0] * (r // tr) + i, 0))
        return pl.pallas_call(
            body, name=name,
            grid_spec=pltpu.PrefetchScalarGridSpec(
                num_scalar_prefetch=1, grid=(r // tr,),
                in_specs=[pl.BlockSpec((tr, n), lambda i, m_ref: (i, 0))], out_specs=out),
            out_shape=jax.ShapeDtypeStruct((r, n * N_DEV) if cols else (r * N_DEV, n), BF16),
            compiler_params=_params(("parallel",)),
        )(mine, s)

    return [place("place_shard_%d" % w, s, cols) for w, (s, cols) in enumerate(zip(shards, by_cols))]


def _gather_start(shards, by_cols, groups):
    lands = _place_shards(shards, by_cols)

    def issue(land, sems):
        x, y, c = _position()
        peers = [(x, y, 1 - c), (1 - x, y, c), (x, 1 - y, c), (1 - x, 1 - y, c)]
        for gi, grp in enumerate(groups):
            for wi, w in enumerate(grp):
                own = _shard_of(land[w], _block_of((x, y, c)), by_cols[w])
                for k, peer in enumerate(peers):
                    pltpu.make_async_remote_copy(
                        src_ref=own, dst_ref=own, send_sem=sems[2 * gi].at[4 * wi + k],
                        recv_sem=sems[2 * gi + 1].at[4 * wi + k], device_id=peer, device_id_type=MESH).start()

    sem_shapes = [(4 * len(g),) for g in groups for _ in range(2)]
    return _split_start("gather_start", lands, sem_shapes, issue)


def _gather_forward(name, lands, by_cols):
    nw = len(lands)

    def issue(land, sems):
        x, y, c = _position()
        for w in range(nw):
            for k, chip in enumerate([(1 - x, y), (x, 1 - y), (1 - x, 1 - y)]):
                blk = _shard_of(land[w], _block_of((*chip, c)), by_cols[w])
                pltpu.make_async_remote_copy(
                    src_ref=blk, dst_ref=blk, send_sem=sems[0].at[3 * w + k], recv_sem=sems[1].at[3 * w + k],
                    device_id=(x, y, 1 - c), device_id_type=MESH).start()

    return _split_start(name, lands, [(3 * nw,), (3 * nw,)], issue)


def _first_block(bufs, w, offset=0):
    return bufs[offset + w].at[0]


def _pair_start(name, grads, by_cols):
    nw = len(grads)
    lands = []
    for g, cols in zip(grads, by_cols):
        shard = (g.shape[0], g.shape[1] // N_DEV) if cols else (g.shape[0] // N_DEV, g.shape[1])
        lands.append(lax.empty((N_CHIP,) + shard, g.dtype))

    def issue(bufs, sems):
        x, y, c = _position()
        for w in range(nw):
            for q in range(N_CHIP):
                pltpu.make_async_remote_copy(
                    src_ref=_shard_of(bufs[w], 2 * q + 1 - c, by_cols[w]), dst_ref=bufs[nw + w].at[q],
                    send_sem=sems[0].at[N_CHIP * w + q], recv_sem=sems[1].at[N_CHIP * w + q],
                    device_id=(x, y, 1 - c), device_id_type=MESH).start()

    return _split_start(name, list(grads) + lands, [(N_CHIP * nw,), (N_CHIP * nw,)], issue)


def _chip_start(name, sums):
    nw = len(sums)
    x, y, _ = _position()
    mine = 2 * x + y
    lands = [lax.dynamic_update_slice(lax.empty(s.shape, s.dtype), lax.dynamic_slice_in_dim(s, mine, 1, 0), (mine, 0, 0))
             for s in sums]

    def issue(bufs, sems):
        x, y, c = _position()
        for w in range(nw):
            for k, (px, py) in enumerate([(1 - x, y), (x, 1 - y), (1 - x, 1 - y)]):
                pltpu.make_async_remote_copy(
                    src_ref=bufs[w].at[2 * px + py], dst_ref=bufs[nw + w].at[2 * x + y], send_sem=sems[0].at[3 * w + k],
                    recv_sem=sems[1].at[3 * w + k], device_id=(px, py, c), device_id_type=MESH).start()

    return _split_start(name, list(sums) + lands, [(3 * nw,), (3 * nw,)], issue)


def _pair_sum(name, grad, landed, by_cols, tr=256):
    _, R, C = landed.shape
    tr = _tile(R, tr)
    core = lax.axis_index("c").astype(jnp.int32).reshape(1)

    def body(c_ref, g_ref, l_ref, o_ref):
        o_ref[...] = (g_ref[...] + l_ref[...]).astype(BF16)

    if by_cols:
        mine = pl.BlockSpec((tr, C), lambda q, i, c_ref: (i, 2 * q + c_ref[0]))
    else:
        mine = pl.BlockSpec((tr, C), lambda q, i, c_ref: ((2 * q + c_ref[0]) * (R // tr) + i, 0))
    slot = pl.BlockSpec((None, tr, C), lambda q, i, c_ref: (q, i, 0))
    return pl.pallas_call(
        body, name=name,
        grid_spec=pltpu.PrefetchScalarGridSpec(
            num_scalar_prefetch=1, grid=(N_CHIP, R // tr),
            in_specs=[mine, slot],
            out_specs=slot),
        out_shape=jax.ShapeDtypeStruct((N_CHIP, R, C), BF16),
        compiler_params=_params(("parallel", "parallel")),
    )(core, grad, landed)


def _adam(w, g, m, v):
    m = ADAM_B1 * m + (1.0 - ADAM_B1) * g
    v = ADAM_B2 * v + (1.0 - ADAM_B2) * (g * g)
    m_hat = m / (1.0 - ADAM_B1 ** ADAM_STEP)
    v_hat = v / (1.0 - ADAM_B2 ** ADAM_STEP)
    delta = -ADAM_LR * (m_hat / (jnp.sqrt(v_hat) + ADAM_EPS) + ADAM_WD * w)
    return delta, m, v


def _sum_adam(name, landed, w, m, v, tr=256):
    R, C = w.shape
    tr = _tile(R, tr)

    def body(l_ref, w_ref, m_ref, v_ref, g_ref, d_ref, nm_ref, nv_ref):
        g = l_ref[0].astype(F32)
        for q in range(1, N_CHIP):
            g = g + l_ref[q].astype(F32)
        g_ref[...] = g
        d_ref[...], nm_ref[...], nv_ref[...] = _adam(w_ref[...], g, m_ref[...], v_ref[...])

    tile = pl.BlockSpec((tr, C), lambda i: (i, 0))
    return pl.pallas_call(
        body, name=name, grid=(R // tr,),
        in_specs=[pl.BlockSpec((N_CHIP, tr, C), lambda i: (0, i, 0)), tile, tile, tile],
        out_specs=[tile] * 4, out_shape=[jax.ShapeDtypeStruct((R, C), F32)] * 4,
        compiler_params=_params(("parallel",)),
    )(landed, w, m, v)


def _small_all_reduce(parts):
    W = parts.shape[1]

    def body(p_ref, o_ref, slots, send_sems, recv_sems):
        x, y, c = _position()
        me = 4 * x + 2 * y + c
        slots[me] = jnp.sum(p_ref[...], axis=0, keepdims=True)
        peers = [(x, y, 1 - c), (1 - x, y, c), (1 - x, y, 1 - c), (x, 1 - y, c), (x, 1 - y, 1 - c),
                 (1 - x, 1 - y, c), (1 - x, 1 - y, 1 - c)]
        copies = []
        for k, peer in enumerate(peers):
            cp = pltpu.make_async_remote_copy(
                src_ref=slots.at[me], dst_ref=slots.at[me], send_sem=send_sems.at[k], recv_sem=recv_sems.at[k],
                device_id=peer, device_id_type=MESH)
            cp.start()
            copies.append(cp)
        for cp in copies:
            cp.wait()
        total = slots[0]
        for d in range(1, N_DEV):
            total = total + slots[d]
        o_ref[...] = total

    return pl.pallas_call(
        body, name="small_all_reduce",
        in_specs=[pl.BlockSpec(memory_space=pltpu.VMEM)], out_specs=pl.BlockSpec(memory_space=pltpu.VMEM),
        out_shape=jax.ShapeDtypeStruct((1, W), F32),
        scratch_shapes=[pltpu.VMEM((N_DEV, 1, W), F32), pltpu.SemaphoreType.DMA((7,)), pltpu.SemaphoreType.DMA((7,))],
    )(parts)


def _adam_small(w, g, m, v):
    def body(w_ref, g_ref, m_ref, v_ref, d_ref, nm_ref, nv_ref):
        d_ref[...], nm_ref[...], nv_ref[...] = _adam(w_ref[...], g_ref[...], m_ref[...], v_ref[...])

    return pl.pallas_call(
        body, name="adam_small",
        in_specs=[pl.BlockSpec(memory_space=pltpu.VMEM)] * 4, out_specs=[pl.BlockSpec(memory_space=pltpu.VMEM)] * 3,
        out_shape=[jax.ShapeDtypeStruct(w.shape, F32)] * 3,
    )(w, g, m, v)


_GATHER_GROUPS = (("w_in",), ("w_out", "w_up", "ple_w"), ("w_down", "w_gate"))
_COL_SHARDED = ("w_in", "w_up", "ple_w")


class _MeshComm:
    def __init__(self, w, mom, var):
        self.w, self.mom, self.var = w, mom, var
        self.out = {}
        self._pairs, self._chips = {}, {}

    def gather_begin(self):
        names = [n for g in _GATHER_GROUPS for n in g]
        self._idx = {n: i for i, n in enumerate(names)}
        groups = [[self._idx[n] for n in g] for g in _GATHER_GROUPS]
        self._sems, self._lands, token = _gather_start(
            [self.w[n] for n in names], [n in _COL_SHARDED for n in names], groups)
        return token

    @staticmethod
    def _shard_size(names, offset):
        return lambda bufs, w: _shard_of(bufs[offset + w], 0, names[w] in _COL_SHARDED)

    def gather_arrive(self, gi, after):
        names = _GATHER_GROUPS[gi]
        ids = [self._idx[n] for n in names]
        self._arrived = _split_wait("gather_arrive%d" % gi, [self._lands[i] for i in ids], self._sems[2 * gi],
                                    self._sems[2 * gi + 1], [4] * len(ids), self._shard_size(names, 0), after)

    def gather_forward(self, gi):
        by_cols = [n in _COL_SHARDED for n in _GATHER_GROUPS[gi]]
        self._fsems, self._fthru, token = _gather_forward("gather_forward%d" % gi, self._arrived, by_cols)
        return token

    def gather_finish(self, gi, after):
        names = _GATHER_GROUPS[gi]
        out = _split_wait("gather_finish%d" % gi, self._fthru, self._fsems[0], self._fsems[1], [3] * len(names),
                          self._shard_size(names, 0), after)
        return dict(zip(names, out))

    def reduce_begin(self, key, grads):
        names = list(grads)
        sems, thru, token = _pair_start("pair_start_" + key, [grads[n] for n in names],
                                        [n in _COL_SHARDED for n in names])
        self._pairs[key] = (names, sems, thru)
        return token

    def reduce_middle(self, key, after):
        names, sems, thru = self._pairs[key]
        nw = len(names)
        out = _split_wait("pair_wait_" + key, thru, sems[0], sems[1], [N_CHIP] * nw,
                          functools.partial(_first_block, offset=nw), after)
        sums = [_pair_sum("pair_sum_" + n, out[i], out[nw + i], n in _COL_SHARDED) for i, n in enumerate(names)]
        sems2, thru2, token = _chip_start("chip_start_" + key, sums)
        self._chips[key] = (names, sems2, thru2)
        return token

    def reduce_finish(self, key, after):
        names, sems, thru = self._chips[key]
        nw = len(names)
        out = _split_wait("chip_wait_" + key, thru, sems[0], sems[1], [3] * nw,
                          functools.partial(_first_block, offset=nw), after)
        for i, n in enumerate(names):
            self.out[n] = _sum_adam("adam_" + n, out[nw + i], self.w[n], self.mom[n], self.var[n])


def _step(x, p, target, gains, comm):
    T, D = x.shape
    n_q = D // (2 * HEAD_DIM)
    n_kv = n_q // GROUP
    cos, sin = _rope_tables(T)
    idx = _bucket_index()

    t = comm.gather_begin()
    u = _rms_fwd("norm_attn", x, gains["attn_norm_g"], deps=(t,))
    comm.gather_arrive(0, u)
    t = comm.gather_forward(0)
    bias = _bias_build(idx, gains["rel_bias_table"].reshape(-1), n_q, deps=(t,))
    full = comm.gather_finish(0, bias)
    proj = _mm_nn("in_proj", u, full["w_in"])
    pb = _qk_prep(proj, cos, sin, gains["q_norm_g"], gains["k_norm_g"], n_q + n_kv)
    o_a, lse_a = _attn_a_fwd(pb, n_q, n_kv)
    comm.gather_arrive(1, o_a)
    t = comm.gather_forward(1)
    sink = gains["sink_logits"].reshape(-1)
    b_off = n_q + 2 * n_kv
    o_b, lse_b = _attn_b_fwd(pb, bias, sink, b_off, n_q, n_kv, deps=(t,))
    full.update(comm.gather_finish(1, o_b))
    o_cat = jnp.concatenate([o_a, o_b], axis=1)
    h1 = _mm_nn("out_proj", o_cat, full["w_out"], epilogue=_store_add, extras=(x,))
    m_in = _rms_fwd("norm_mlp", h1, gains["mlp_norm_g"])

    def up_epilogue(acc, extra, outs):
        outs[0][...] = acc.astype(BF16)
        r = jnp.maximum(acc, 0.0)
        outs[1][...] = (r * r).astype(BF16)

    a_act, f_act = _mm_nn("up_proj", m_in, full["w_up"], epilogue=up_epilogue, out_dtypes=[BF16, BF16])
    comm.gather_arrive(2, f_act)
    t = comm.gather_forward(2)
    p_b = p.astype(BF16)
    pe = _mm_nn("ple_proj", p_b, full["ple_w"], deps=(t,))
    full.update(comm.gather_finish(2, pe))
    h2 = _mm_nn("down_proj", f_act, full["w_down"], epilogue=_store_add, extras=(h1,), tm=512, tn=512)
    gn = _rms_fwd("norm_gate", h2, gains["gate_norm_g"])
    z = _mm_nn("gate_proj", gn, full["w_gate"])

    dh3, dz, dpe, dg_final, dg_ple, loss_part = _tail(h2, z, pe, target, gains["ple_norm_g"], gains["final_norm_g"])
    gw_gate = _mm_tn("grad_w_gate", gn, dz)
    gw_ple = _mm_tn("grad_ple_w", p_b, dpe)
    t = comm.reduce_begin("a", dict(w_gate=gw_gate, ple_w=gw_ple))
    dgn = _mm_nt("d_gate_in", dz, full["w_gate"], deps=(t,))
    dh2, dh2_b, dg_gate = _rms_bwd("norm_gate_bwd", dgn, h2, gains["gate_norm_g"], dh3)
    t = comm.reduce_middle("a", dh2_b)
    gw_down = _mm_tn("grad_w_down", f_act, dh2_b, deps=(t,))
    t = comm.reduce_begin("b", dict(w_down=gw_down))

    def act_bwd(acc, extra, outs):
        outs[0][...] = (acc * (2.0 * jnp.maximum(extra[0][...].astype(F32), 0.0))).astype(BF16)

    da = _mm_nt("d_act", dh2_b, full["w_down"], out_dtype=BF16, epilogue=act_bwd, extras=(a_act,), deps=(t,))
    comm.reduce_finish("a", da)
    t = comm.reduce_middle("b", da)
    gw_up = _mm_tn("grad_w_up", m_in, da, deps=(t,))
    t = comm.reduce_begin("c", dict(w_up=gw_up))
    dm = _mm_nt("d_mlp_in", da, full["w_up"], tm=512, tn=512, deps=(t,))
    t = comm.reduce_middle("c", dm)
    dh1, dh1_b, dg_mlp = _rms_bwd("norm_mlp_bwd", dm, h1, gains["mlp_norm_g"], dh2, deps=(t,))
    comm.reduce_finish("b", dh1_b)
    gw_out = _mm_tn("grad_w_out", o_cat, dh1_b)
    t = comm.reduce_begin("d", dict(w_out=gw_out))
    d_o = _mm_nt("d_attn_out", dh1_b, full["w_out"], out_dtype=BF16, deps=(t,))
    dqa, dka, dva = _attn_a_bwd(pb, o_cat, d_o, lse_a, n_q, n_kv)
    t = comm.reduce_middle("d", dqa)
    dqb, dkb, dvb, dbias, dsink_raw = _attn_b_bwd(pb, o_cat, d_o, lse_b, bias, sink, b_off, n_q, n_kv, n_q, deps=(t,))
    comm.reduce_finish("c", dqb)
    comm.reduce_finish("d", dqb)
    dtable, dsink = _table_grads(dbias, dsink_raw, idx)
    dproj, dg_q, dg_k = _dproj(proj, dqa, dka, dva, dqb, dkb, dvb, cos, sin, gains["q_norm_g"], gains["k_norm_g"])
    gw_in = _mm_tn("grad_w_in", u, dproj)
    t = comm.reduce_begin("e", dict(w_in=gw_in))
    du = _mm_nt("d_attn_in", dproj, full["w_in"], deps=(t,))
    t = comm.reduce_middle("e", du)
    dx, _, dg_attn = _rms_bwd("norm_attn_bwd", du, x, gains["attn_norm_g"], dh1, deps=(t,))
    comm.reduce_finish("e", dx)

    parts = jnp.concatenate([dg_attn, dg_mlp, dg_ple, dg_gate, dg_final, dg_q, dg_k, dtable, dsink, loss_part], axis=1)
    return dx, parts


_SHARDED = ("w_in", "w_out", "w_up", "w_down", "ple_w", "w_gate")
_VECTORS = ("attn_norm_g", "mlp_norm_g", "ple_norm_g", "gate_norm_g", "final_norm_g")
_ORDER = ("attn_norm_g", "w_in", "q_norm_g", "k_norm_g", "sink_logits", "w_out", "mlp_norm_g", "w_up", "w_down",
          "ple_w", "ple_norm_g", "gate_norm_g", "w_gate", "rel_bias_table", "final_norm_g")


def _pack_small(vals, n_heads):
    lane_pad = lambda v: jnp.pad(v, ((0, 0), (0, LANES - v.shape[1])))
    table = lane_pad(vals["rel_bias_table"].T).reshape(1, n_heads * LANES)
    return jnp.concatenate(
        [vals[n].reshape(1, -1) for n in _VECTORS] + [vals["q_norm_g"], vals["k_norm_g"], table,
                                                      lane_pad(vals["sink_logits"]), jnp.zeros((1, LANES), F32)], axis=1)


def _unpack_small(row, like, n_heads):
    out, off = {}, 0
    for n in _VECTORS:
        out[n] = row[:, off:off + like[n].size].reshape(like[n].shape)
        off += like[n].size
    for n in ("q_norm_g", "k_norm_g"):
        out[n] = row[:, off:off + LANES]
        off += LANES
    out["rel_bias_table"] = row[:, off:off + n_heads * LANES].reshape(n_heads, LANES)[:, :N_BUCKETS].T
    off += n_heads * LANES
    out["sink_logits"] = row[:, off:off + n_heads]
    off += LANES
    return out, row[0, off]


def kernel(x, p, attn_norm_g, w_in, q_norm_g, k_norm_g, sink_logits, w_out, mlp_norm_g, w_up, w_down, ple_w, ple_norm_g, gate_norm_g, w_gate, rel_bias_table, final_norm_g, loss_target, m_attn_norm_g, m_w_in, m_q_norm_g, m_k_norm_g, m_sink_logits, m_w_out, m_mlp_norm_g, m_w_up, m_w_down, m_ple_w, m_ple_norm_g, m_gate_norm_g, m_w_gate, m_rel_bias_table, m_final_norm_g, v_attn_norm_g, v_w_in, v_q_norm_g, v_k_norm_g, v_sink_logits, v_w_out, v_mlp_norm_g, v_w_up, v_w_down, v_ple_w, v_ple_norm_g, v_gate_norm_g, v_w_gate, v_rel_bias_table, v_final_norm_g):
    w = dict(attn_norm_g=attn_norm_g, w_in=w_in[0], q_norm_g=q_norm_g, k_norm_g=k_norm_g, sink_logits=sink_logits,
             w_out=w_out[0], mlp_norm_g=mlp_norm_g, w_up=w_up[0], w_down=w_down[0], ple_w=ple_w[0],
             ple_norm_g=ple_norm_g, gate_norm_g=gate_norm_g, w_gate=w_gate[0], rel_bias_table=rel_bias_table,
             final_norm_g=final_norm_g)
    mom = dict(attn_norm_g=m_attn_norm_g, w_in=m_w_in[0], q_norm_g=m_q_norm_g, k_norm_g=m_k_norm_g,
               sink_logits=m_sink_logits, w_out=m_w_out[0], mlp_norm_g=m_mlp_norm_g, w_up=m_w_up[0],
               w_down=m_w_down[0], ple_w=m_ple_w[0], ple_norm_g=m_ple_norm_g, gate_norm_g=m_gate_norm_g,
               w_gate=m_w_gate[0], rel_bias_table=m_rel_bias_table, final_norm_g=m_final_norm_g)
    var = dict(attn_norm_g=v_attn_norm_g, w_in=v_w_in[0], q_norm_g=v_q_norm_g, k_norm_g=v_k_norm_g,
               sink_logits=v_sink_logits, w_out=v_w_out[0], mlp_norm_g=v_mlp_norm_g, w_up=v_w_up[0],
               w_down=v_w_down[0], ple_w=v_ple_w[0], ple_norm_g=v_ple_norm_g, gate_norm_g=v_gate_norm_g,
               w_gate=v_w_gate[0], rel_bias_table=v_rel_bias_table, final_norm_g=v_final_norm_g)
    D = x.shape[-1]
    n_heads = D // (2 * HEAD_DIM)

    gains = {n: w[n] for n in w if n not in _SHARDED}
    gains["final_norm_g"] = final_norm_g.reshape(1, -1)

    comm = _MeshComm(w, mom, var)
    dx, parts = _step(x[0], p[0, 0], loss_target[0], gains, comm)

    g_out, d_out, m_out, v_out = {}, {}, {}, {}
    for n in _SHARDED:
        g, d, nm, nv = comm.out[n]
        g_out[n], d_out[n], m_out[n], v_out[n] = g[None], d[None], nm[None], nv[None]

    small_g = _small_all_reduce(parts)
    small = {n: v for n, v in w.items() if n not in _SHARDED}
    pack = lambda vals: _pack_small({n: vals[n] for n in small}, n_heads)
    sd, sm, sv = _adam_small(pack(w), small_g, pack(mom), pack(var))
    sg, loss = _unpack_small(small_g, small, n_heads)
    g_out.update(sg)
    for dst, row in ((d_out, sd), (m_out, sm), (v_out, sv)):
        dst.update(_unpack_small(row, small, n_heads)[0])

    return (loss, dx[None], *[g_out[n] for n in _ORDER], *[d_out[n] for n in _ORDER],
            *[m_out[n] for n in _ORDER], *[v_out[n] for n in _ORDER])
```

```python
import functools
import math

import numpy as np
import jax
import jax.numpy as jnp
from jax import lax
from jax.experimental import pallas as pl
from jax.experimental.pallas import tpu as pltpu

F32 = jnp.float32
BF16 = jnp.bfloat16

N_DEV = 8
N_CHIP = 4
HEAD_DIM = 128
GROUP = 4
GRID_W = 64
WINDOW = 128
BLOCK_Q = 128
N_BUCKETS = 32
MAX_DISTANCE = 128
ROPE_THETA = 10000.0
EPS = 1e-6
NEG_INF = -1e30
ADAM_LR = 0.001
ADAM_B1 = 0.9
ADAM_B2 = 0.999
ADAM_EPS = 1e-08
ADAM_WD = 0.01
ADAM_STEP = 10
LOG2E = math.log2(math.e)
LANES = 128
SUBLANES = 8
MESH = pl.DeviceIdType.MESH

_NT = (((1,), (1,)), ((), ()))
_NN = (((1,), (0,)), ((), ()))
_TN = (((0,), (0,)), ((), ()))


def _tile(dim, pref):
    return pref if dim % pref == 0 else dim


def _params(sem):
    return pltpu.CompilerParams(dimension_semantics=sem, vmem_limit_bytes=56 * 1024 * 1024)


_HBM = pl.BlockSpec(memory_space=pltpu.HBM)
_SEM = pl.BlockSpec(memory_space=pltpu.SEMAPHORE)
_ANY = pl.BlockSpec(memory_space=pl.ANY)
_VMEM = pl.BlockSpec(memory_space=pltpu.VMEM)
_EFFECT = pltpu.SideEffectType.DATAFLOW_SIDE_EFFECTING


def _pcall(body, deps=(), *, in_specs, **kw):
    deps = [d for d in deps if d is not None]
    nd = len(deps)

    def wrapped(*refs):
        body(*refs[nd:])

    call = pl.pallas_call(wrapped, in_specs=[_ANY] * nd + list(in_specs), **kw)
    return lambda *args: call(*deps, *args)


def _mm(name, a, b, dims, grid, a_spec, b_spec, out_shape, out_specs, acc_shape, epilogue,
        extras=(), extra_specs=(), deps=()):
    nk = grid[2]
    n_extra = len(extras)

    def body(*refs):
        a_ref, b_ref = refs[0], refs[1]
        extra = refs[2:2 + n_extra]
        outs = refs[2 + n_extra:-1]
        acc = refs[-1]
        part = lax.dot_general(a_ref[...], b_ref[...], dims, preferred_element_type=F32)
        if nk == 1:
            epilogue(part, extra, outs)
        else:
            k = pl.program_id(2)

            @pl.when(k == 0)
            def _():
                acc[...] = part

            @pl.when(k > 0)
            def _():
                acc[...] += part

            @pl.when(k == nk - 1)
            def _():
                epilogue(acc[...], extra, outs)

    return _pcall(
        body, deps, name=name, grid=grid,
        in_specs=[a_spec, b_spec, *extra_specs],
        out_specs=out_specs, out_shape=out_shape,
        scratch_shapes=[pltpu.VMEM(acc_shape if nk > 1 else (SUBLANES, LANES), F32)],
        compiler_params=_params(("parallel", "parallel", "arbitrary")),
    )(a, b, *extras)


def _store(dtype):
    def ep(acc, extra, outs):
        outs[0][...] = acc.astype(dtype)
    return ep


def _store_add(acc, extra, outs):
    outs[0][...] = acc + extra[0][...]


def _mm_nn(name, a, b, out_dtype=F32, epilogue=None, extras=(), n_out=1, out_dtypes=None, tm=1024, tn=1024, tk=None,
           deps=()):
    M, K = a.shape
    N = b.shape[1]
    tm, tn, tk = _tile(M, tm), _tile(N, tn), _tile(K, tk or K)
    b_spec = pl.BlockSpec((tk, tn), lambda i, j, k: (k, j))
    grid = (M // tm, N // tn, K // tk)
    o_spec = pl.BlockSpec((tm, tn), lambda i, j, k: (i, j))
    out_dtypes = out_dtypes or [out_dtype] * n_out
    out_shape = [jax.ShapeDtypeStruct((M, N), d) for d in out_dtypes]
    res = _mm(name, a, b, _NN, grid, pl.BlockSpec((tm, tk), lambda i, j, k: (i, k)), b_spec,
              out_shape, [o_spec] * len(out_dtypes), (tm, tn), epilogue or _store(out_dtype),
              extras, [o_spec] * len(extras), deps)
    return res if len(out_dtypes) > 1 else res[0]


def _mm_nt(name, a, b, out_dtype=F32, epilogue=None, extras=(), tm=1024, tn=1024, tk=None, deps=()):
    M, C = a.shape
    N = b.shape[0]
    tm, tn, tk = _tile(M, tm), _tile(N, tn), _tile(C, tk or C)
    b_spec = pl.BlockSpec((tn, tk), lambda i, j, k: (j, k))
    grid = (M // tm, N // tn, C // tk)
    o_spec = pl.BlockSpec((tm, tn), lambda i, j, k: (i, j))
    return _mm(name, a, b, _NT, grid, pl.BlockSpec((tm, tk), lambda i, j, k: (i, k)), b_spec,
               [jax.ShapeDtypeStruct((M, N), out_dtype)], [o_spec], (tm, tn), epilogue or _store(out_dtype),
               extras, [o_spec] * len(extras), deps)[0]


def _mm_tn(name, a, b, out_dtype=BF16, tm=1024, tn=512, tk=None, deps=()):
    T, M = a.shape
    N = b.shape[1]
    tm, tn, tk = _tile(M, tm), _tile(N, tn), _tile(T, tk or T)
    out_shape = jax.ShapeDtypeStruct((M, N), out_dtype)
    o_spec = pl.BlockSpec((tm, tn), lambda i, j, k: (i, j))
    grid = (M // tm, N // tn, T // tk)
    return _mm(name, a, b, _TN, grid, pl.BlockSpec((tk, tm), lambda i, j, k: (k, i)),
               pl.BlockSpec((tk, tn), lambda i, j, k: (k, j)), [out_shape], [o_spec], (tm, tn), _store(out_dtype),
               deps=deps)[0]


def _mean_last(v):
    return jnp.mean(v, axis=-1, keepdims=True)


def _rows_to_sublanes(v):
    r, c = v.shape
    return jnp.sum(v.reshape(r // SUBLANES, SUBLANES, c), axis=0)


def _accumulate(ref, val, first):
    @pl.when(first)
    def _():
        ref[...] = val

    @pl.when(jnp.logical_not(first))
    def _():
        ref[...] += val


def _rms_fwd(name, x, g, tr=256, deps=()):
    T, D = x.shape
    tr = _tile(T, tr)

    def body(x_ref, g_ref, o_ref):
        xv = x_ref[...]
        r = lax.rsqrt(_mean_last(xv * xv) + EPS)
        o_ref[...] = (xv * r * g_ref[...]).astype(BF16)

    row = pl.BlockSpec((tr, D), lambda i: (i, 0))
    return _pcall(
        body, deps, name=name, grid=(T // tr,),
        in_specs=[row, pl.BlockSpec((1, D), lambda i: (0, 0))],
        out_specs=row, out_shape=jax.ShapeDtypeStruct((T, D), BF16),
        compiler_params=_params(("parallel",)),
    )(x, g)


def _rms_bwd(name, dyn, x, g, dres, tr=256, deps=()):
    T, D = x.shape
    tr = _tile(T, tr)

    def body(dy_ref, x_ref, g_ref, dr_ref, dx_ref, dxb_ref, dg_ref):
        xv = x_ref[...]
        r = lax.rsqrt(_mean_last(xv * xv) + EPS)
        xn = xv * r
        dy = dy_ref[...]
        dxn = dy * g_ref[...]
        dx = dr_ref[...] + r * (dxn - xn * _mean_last(dxn * xn))
        dx_ref[...] = dx
        dxb_ref[...] = dx.astype(BF16)
        _accumulate(dg_ref, _rows_to_sublanes(dy * xn), pl.program_id(0) == 0)

    row = pl.BlockSpec((tr, D), lambda i: (i, 0))
    return _pcall(
        body, deps, name=name, grid=(T // tr,),
        in_specs=[row, row, pl.BlockSpec((1, D), lambda i: (0, 0)), row],
        out_specs=[row, row, pl.BlockSpec((SUBLANES, D), lambda i: (0, 0))],
        out_shape=[jax.ShapeDtypeStruct((T, D), F32), jax.ShapeDtypeStruct((T, D), BF16),
                   jax.ShapeDtypeStruct((SUBLANES, D), F32)],
        compiler_params=_params(("arbitrary",)),
    )(dyn, x, g, dres)


def _tail(h2, z, pe, target, g_ple, g_final, tr=256):
    T, D = h2.shape
    tr = _tile(T, tr)

    def body(h2_ref, z_ref, pe_ref, t_ref, gp_ref, gf_ref,
             dh3_ref, dz_ref, dpe_ref, dgf_ref, dgp_ref, loss_ref):
        first = pl.program_id(0) == 0
        pev = pe_ref[...]
        r3 = lax.rsqrt(_mean_last(pev * pev) + EPS)
        en = pev * r3
        e = en * gp_ref[...]
        gate = 1.0 / (1.0 + jnp.exp(-z_ref[...]))
        h3 = h2_ref[...] + gate * e
        r5 = lax.rsqrt(_mean_last(h3 * h3) + EPS)
        hn = h3 * r5
        diff = hn * gf_ref[...] - t_ref[...]
        loss_rows = 0.5 * _mean_last(diff * diff)
        row0 = lax.broadcasted_iota(jnp.int32, (SUBLANES, LANES), 0) == 0
        _accumulate(loss_ref, jnp.where(row0, jnp.sum(loss_rows), 0.0), first)
        dy = diff * (1.0 / D)
        _accumulate(dgf_ref, _rows_to_sublanes(dy * hn), first)
        dhn = dy * gf_ref[...]
        dh3 = r5 * (dhn - hn * _mean_last(dhn * hn))
        dh3_ref[...] = dh3
        dgate = dh3 * e
        de = dh3 * gate
        dz_ref[...] = (dgate * gate * (1.0 - gate)).astype(BF16)
        _accumulate(dgp_ref, _rows_to_sublanes(de * en), first)
        den = de * gp_ref[...]
        dpe_ref[...] = (r3 * (den - en * _mean_last(den * en))).astype(BF16)

    row = pl.BlockSpec((tr, D), lambda i: (i, 0))
    vec = pl.BlockSpec((1, D), lambda i: (0, 0))
    part = pl.BlockSpec((SUBLANES, D), lambda i: (0, 0))
    return pl.pallas_call(
        body, name="tail", grid=(T // tr,),
        in_specs=[row, row, row, row, vec, vec],
        out_specs=[row, row, row, part, part, pl.BlockSpec((SUBLANES, LANES), lambda i: (0, 0))],
        out_shape=[jax.ShapeDtypeStruct((T, D), F32), jax.ShapeDtypeStruct((T, D), BF16),
                   jax.ShapeDtypeStruct((T, D), BF16), jax.ShapeDtypeStruct((SUBLANES, D), F32),
                   jax.ShapeDtypeStruct((SUBLANES, D), F32), jax.ShapeDtypeStruct((SUBLANES, LANES), F32)],
        compiler_params=_params(("arbitrary",)),
    )(h2, z, pe, target, g_ple, g_final)


def _rope_tables(T):
    pos = np.arange(T)
    half = HEAD_DIM // 2
    inv = (ROPE_THETA ** (-np.arange(0, half, 2, dtype=np.float32) / half)).astype(np.float32)
    ang_r = (pos // GRID_W).astype(np.float32)[:, None] * inv
    ang_c = (pos % GRID_W).astype(np.float32)[:, None] * inv
    cos = np.concatenate([np.cos(ang_r), np.cos(ang_r), np.cos(ang_c), np.cos(ang_c)], axis=-1)
    sin = np.concatenate([-np.sin(ang_r), np.sin(ang_r), -np.sin(ang_c), np.sin(ang_c)], axis=-1)
    return jnp.asarray(cos, F32), jnp.asarray(sin, F32)


def _swap32(x):
    lane = lax.broadcasted_iota(jnp.int32, x.shape, 1)
    return jnp.where((lane % 64) < 32, pltpu.roll(x, 96, 1), pltpu.roll(x, 32, 1))


def _qk_prep(proj, cos, sin, g_q, g_k, n_norm, tr=256):
    T, W = proj.shape
    tr = _tile(T, tr)
    n_q = n_norm * GROUP // (GROUP + 1)

    def body(p_ref, c_ref, s_ref, gq_ref, gk_ref, o_ref):
        c, s = c_ref[...], s_ref[...]
        for h in range(n_norm):
            cols = slice(h * HEAD_DIM, (h + 1) * HEAD_DIM)
            xv = p_ref[:, cols]
            g = gq_ref[...] if h < n_q else gk_ref[...]
            xn = xv * lax.rsqrt(_mean_last(xv * xv) + EPS) * g
            o_ref[:, cols] = (xn * c + _swap32(xn) * s).astype(BF16)
        rest = slice(n_norm * HEAD_DIM, W)
        o_ref[:, rest] = p_ref[:, rest].astype(BF16)

    row = pl.BlockSpec((tr, W), lambda i: (i, 0))
    tab = pl.BlockSpec((tr, HEAD_DIM), lambda i: (i, 0))
    vec = pl.BlockSpec((1, HEAD_DIM), lambda i: (0, 0))
    return pl.pallas_call(
        body, name="qk_prep", grid=(T // tr,),
        in_specs=[row, tab, tab, vec, vec], out_specs=row,
        out_shape=jax.ShapeDtypeStruct((T, W), BF16),
        compiler_params=_params(("parallel",)),
    )(proj, cos, sin, g_q, g_k)


def _dproj(proj, dqa, dka, dva, dqb, dkb, dvb, cos, sin, g_q, g_k, tr=256):
    T, W = proj.shape
    tr = _tile(T, tr)
    n_q = dqa.shape[1] // HEAD_DIM
    n_kv = dka.shape[1] // HEAD_DIM
    wa = (n_q + n_kv) * HEAD_DIM

    def body(p_ref, dqa_ref, dka_ref, dva_ref, dqb_ref, dkb_ref, dvb_ref, c_ref, s_ref, gq_ref, gk_ref,
             o_ref, dgq_ref, dgk_ref):
        c, s = c_ref[...], s_ref[...]
        dgq = jnp.zeros((SUBLANES, HEAD_DIM), F32)
        dgk = jnp.zeros((SUBLANES, HEAD_DIM), F32)
        for h in range(n_q + n_kv):
            cols = slice(h * HEAD_DIM, (h + 1) * HEAD_DIM)
            xv = p_ref[:, cols]
            r = lax.rsqrt(_mean_last(xv * xv) + EPS)
            xn = xv * r
            if h < n_q:
                d = dqa_ref[:, cols]
                g = gq_ref[...]
            else:
                d = dka_ref[:, (h - n_q) * HEAD_DIM:(h - n_q + 1) * HEAD_DIM]
                g = gk_ref[...]
            dqn = d * c + _swap32(d * s)
            part = _rows_to_sublanes(dqn * xn)
            if h < n_q:
                dgq = dgq + part
            else:
                dgk = dgk + part
            dxn = dqn * g
            o_ref[:, cols] = (r * (dxn - xn * _mean_last(dxn * xn))).astype(BF16)
        off = wa
        for ref in (dva_ref, dqb_ref, dkb_ref, dvb_ref):
            w = ref.shape[1]
            o_ref[:, off:off + w] = ref[...].astype(BF16)
            off += w
        first = pl.program_id(0) == 0
        _accumulate(dgq_ref, dgq, first)
        _accumulate(dgk_ref, dgk, first)

    def row(w):
        return pl.BlockSpec((tr, w), lambda i: (i, 0))

    vec = pl.BlockSpec((1, HEAD_DIM), lambda i: (0, 0))
    part = pl.BlockSpec((SUBLANES, HEAD_DIM), lambda i: (0, 0))
    return pl.pallas_call(
        body, name="dproj", grid=(T // tr,),
        in_specs=[row(wa), row(dqa.shape[1]), row(dka.shape[1]), row(dva.shape[1]), row(dqb.shape[1]),
                  row(dkb.shape[1]), row(dvb.shape[1]), row(HEAD_DIM), row(HEAD_DIM), vec, vec],
        out_specs=[row(W), part, part],
        out_shape=[jax.ShapeDtypeStruct((T, W), BF16), jax.ShapeDtypeStruct((SUBLANES, HEAD_DIM), F32),
                   jax.ShapeDtypeStruct((SUBLANES, HEAD_DIM), F32)],
        compiler_params=_params(("arbitrary",)),
    )(proj, dqa, dka, dva, dqb, dkb, dvb, cos, sin, g_q, g_k)


def _attn_a_fwd(pb, n_q, n_kv, tq=1024, tc=1024):
    T = pb.shape[0]
    tq, tc = _tile(T, tq), _tile(T, tc)
    scale = HEAD_DIM ** -0.5
    c = scale * LOG2E

    def body(q_ref, k_ref, v_ref, o_ref, lse_ref):
        q = q_ref[...]
        m = l = acc = None
        for j in range(T // tc):
            keys = slice(j * tc, (j + 1) * tc)
            s = lax.dot_general(q, k_ref[keys, :], _NT, preferred_element_type=F32)
            mj = jnp.max(s, axis=-1, keepdims=True)
            m_new = mj if j == 0 else jnp.maximum(m, mj)
            p = jnp.exp2((s - m_new) * c)
            pv = lax.dot_general(p.astype(BF16), v_ref[keys, :], _NN, preferred_element_type=F32)
            if j == 0:
                l, acc = jnp.sum(p, axis=-1, keepdims=True), pv
            else:
                alpha = jnp.exp2((m - m_new) * c)
                l = alpha * l + jnp.sum(p, axis=-1, keepdims=True)
                acc = alpha * acc + pv
            m = m_new
        o_ref[...] = (acc / l).astype(BF16)
        lse_ref[...] = m * scale + jnp.log(l)

    return pl.pallas_call(
        body, name="attn_a_fwd", grid=(n_kv, GROUP, T // tq),
        in_specs=[pl.BlockSpec((tq, HEAD_DIM), lambda kv, g, i: (i, kv * GROUP + g)),
                  pl.BlockSpec((T, HEAD_DIM), lambda kv, g, i: (0, n_q + kv)),
                  pl.BlockSpec((T, HEAD_DIM), lambda kv, g, i: (0, n_q + n_kv + kv))],
        out_specs=[pl.BlockSpec((tq, HEAD_DIM), lambda kv, g, i: (i, kv * GROUP + g)),
                   pl.BlockSpec((None, tq, 1), lambda kv, g, i: (kv * GROUP + g, i, 0))],
        out_shape=[jax.ShapeDtypeStruct((T, n_q * HEAD_DIM), BF16), jax.ShapeDtypeStruct((n_q, T, 1), F32)],
        compiler_params=_params(("parallel", "parallel", "parallel")),
    )(pb, pb, pb)


def _attn_a_bwd(pb, o_cat, d_o, lse, n_q, n_kv, tq=512, tc=1024):
    T = pb.shape[0]
    tq, tc = _tile(T, tq), _tile(T, tc)
    scale = HEAD_DIM ** -0.5
    c = scale * LOG2E

    def body(q_ref, k_ref, v_ref, o_ref, do_ref, lse_ref, dq_ref, dk_ref, dv_ref):
        q, do = q_ref[...], do_ref[...]
        delta = jnp.sum(do.astype(F32) * o_ref[...].astype(F32), axis=-1, keepdims=True)
        lse2 = lse_ref[...] * LOG2E
        first = jnp.logical_and(pl.program_id(1) == 0, pl.program_id(2) == 0)
        dq = None
        for j in range(T // tc):
            keys = slice(j * tc, (j + 1) * tc)
            kc, vc = k_ref[keys, :], v_ref[keys, :]
            s = lax.dot_general(q, kc, _NT, preferred_element_type=F32)
            p = jnp.exp2(s * c - lse2)
            dp = lax.dot_general(do, vc, _NT, preferred_element_type=F32)
            ds = (p * (dp - delta) * scale).astype(BF16)
            dqj = lax.dot_general(ds, kc, _NN, preferred_element_type=F32)
            dq = dqj if dq is None else dq + dqj
            _accumulate(dv_ref.at[keys, :], lax.dot_general(p.astype(BF16), do, _TN, preferred_element_type=F32), first)
            _accumulate(dk_ref.at[keys, :], lax.dot_general(ds, q, _TN, preferred_element_type=F32), first)
        dq_ref[...] = dq

    qmap = lambda kv, g, i: (i, kv * GROUP + g)
    return pl.pallas_call(
        body, name="attn_a_bwd", grid=(n_kv, GROUP, T // tq),
        in_specs=[pl.BlockSpec((tq, HEAD_DIM), qmap),
                  pl.BlockSpec((T, HEAD_DIM), lambda kv, g, i: (0, n_q + kv)),
                  pl.BlockSpec((T, HEAD_DIM), lambda kv, g, i: (0, n_q + n_kv + kv)),
                  pl.BlockSpec((tq, HEAD_DIM), qmap),
                  pl.BlockSpec((tq, HEAD_DIM), qmap),
                  pl.BlockSpec((None, tq, 1), lambda kv, g, i: (kv * GROUP + g, i, 0))],
        out_specs=[pl.BlockSpec((tq, HEAD_DIM), qmap),
                   pl.BlockSpec((T, HEAD_DIM), lambda kv, g, i: (0, kv)),
                   pl.BlockSpec((T, HEAD_DIM), lambda kv, g, i: (0, kv))],
        out_shape=[jax.ShapeDtypeStruct((T, n_q * HEAD_DIM), F32),
                   jax.ShapeDtypeStruct((T, n_kv * HEAD_DIM), F32),
                   jax.ShapeDtypeStruct((T, n_kv * HEAD_DIM), F32)],
        compiler_params=_params(("parallel", "arbitrary", "arbitrary")),
    )(pb, pb, pb, o_cat, d_o, lse)


def _bucket_index():
    r = np.arange(BLOCK_Q)[:, None]
    j = np.arange(3 * BLOCK_Q)[None, :]
    rel = (j - BLOCK_Q) - r
    nb = N_BUCKETS // 2
    ret = np.where(rel > 0, nb, 0)
    n = np.abs(rel)
    max_exact = nb // 2
    nf = np.maximum(n, 1).astype(np.float32)
    large = max_exact + (np.log(nf / max_exact) / math.log(MAX_DISTANCE / max_exact) * (nb - max_exact)).astype(np.int32)
    large = np.minimum(large, nb - 1)
    return jnp.asarray(ret + np.where(n < max_exact, n, large), jnp.int32)


def _bias_build(idx, table_flat, n_heads, deps=()):
    def body(idx_ref, tab_ref, o_ref):
        h = pl.program_id(0)
        iv = idx_ref[...]
        acc = jnp.zeros(iv.shape, F32)
        for b in range(N_BUCKETS):
            acc = jnp.where(iv == b, tab_ref[b * n_heads + h], acc)
        r = lax.broadcasted_iota(jnp.int32, iv.shape, 0)
        j = lax.broadcasted_iota(jnp.int32, iv.shape, 1)
        o_ref[...] = jnp.where(jnp.abs(j - BLOCK_Q - r) <= WINDOW, acc, NEG_INF)

    return _pcall(
        body, deps, name="bias_build", grid=(n_heads,),
        in_specs=[pl.BlockSpec(idx.shape, lambda h: (0, 0)), pl.BlockSpec(memory_space=pltpu.SMEM)],
        out_specs=pl.BlockSpec((None,) + idx.shape, lambda h: (h, 0, 0)),
        out_shape=jax.ShapeDtypeStruct((n_heads,) + idx.shape, F32),
        compiler_params=_params(("parallel",)),
    )(idx, table_flat)


def _in_sequence(n, T):
    j = lax.broadcasted_iota(jnp.int32, (BLOCK_Q, 3 * BLOCK_Q), 1)
    kabs = n * BLOCK_Q + j - BLOCK_Q
    return (kabs >= 0) & (kabs < T)


def _band_specs(col, nblk, sb):
    return [pl.BlockSpec((BLOCK_Q, HEAD_DIM), lambda kv, i: (jnp.maximum(sb * i - 1, 0), col(kv))),
            pl.BlockSpec((sb * BLOCK_Q, HEAD_DIM), lambda kv, i: (i, col(kv))),
            pl.BlockSpec((BLOCK_Q, HEAD_DIM), lambda kv, i: (jnp.minimum(sb * i + sb, nblk - 1), col(kv)))]


def _head_specs(base, rows):
    return [pl.BlockSpec((rows, HEAD_DIM), functools.partial(lambda kv, i, g: (i, base + kv * GROUP + g), g=g))
            for g in range(GROUP)]


def _attn_b_fwd(pb, bias, sink, q_off, n_q, n_kv, deps=(), sb=8):
    T = pb.shape[0]
    nblk = T // BLOCK_Q
    sb = min(sb, nblk)
    tq = sb * BLOCK_Q
    scale = HEAD_DIM ** -0.5

    def body(*refs):
        q_refs = refs[0:GROUP]
        k_refs, v_refs = refs[GROUP:GROUP + 3], refs[GROUP + 3:GROUP + 6]
        bias_ref, sink_ref, o_ref, lse_ref = refs[GROUP + 6:]
        kv, i = pl.program_id(0), pl.program_id(1)
        kb = jnp.concatenate([r[...] for r in k_refs], axis=0)
        vb = jnp.concatenate([r[...] for r in v_refs], axis=0)
        for b in range(sb):
            at_end = b == 0 or b == sb - 1
            mask = _in_sequence(i * sb + b, T) if at_end else None
            rows = slice(b * BLOCK_Q, (b + 1) * BLOCK_Q)
            kw, vw = kb[b * BLOCK_Q:(b + 3) * BLOCK_Q], vb[b * BLOCK_Q:(b + 3) * BLOCK_Q]
            for g in range(GROUP):
                sk = sink_ref[kv * GROUP + g]
                s = lax.dot_general(q_refs[g][rows, :], kw, _NT, preferred_element_type=F32) * scale + bias_ref[g]
                if at_end:
                    s = jnp.where(mask, s, NEG_INF)
                m = jnp.maximum(jnp.max(s, axis=-1, keepdims=True), sk)
                p = jnp.exp(s - m)
                l = jnp.sum(p, axis=-1, keepdims=True) + jnp.exp(sk - m)
                o = lax.dot_general(p.astype(BF16), vw, _NN, preferred_element_type=F32)
                o_ref[rows, g * HEAD_DIM:(g + 1) * HEAD_DIM] = (o / l).astype(BF16)
                lse_ref[g, rows, :] = m + jnp.log(l)

    return _pcall(
        body, deps, name="attn_b_fwd", grid=(n_kv, nblk // sb),
        in_specs=[*_head_specs(q_off, tq),
                  *_band_specs(lambda kv: q_off + n_q + kv, nblk, sb),
                  *_band_specs(lambda kv: q_off + n_q + n_kv + kv, nblk, sb),
                  pl.BlockSpec((GROUP, BLOCK_Q, 3 * BLOCK_Q), lambda kv, i: (kv, 0, 0)),
                  pl.BlockSpec(memory_space=pltpu.SMEM)],
        out_specs=[pl.BlockSpec((tq, GROUP * HEAD_DIM), lambda kv, i: (i, kv)),
                   pl.BlockSpec((GROUP, tq, 1), lambda kv, i: (kv, i, 0))],
        out_shape=[jax.ShapeDtypeStruct((T, n_q * HEAD_DIM), BF16), jax.ShapeDtypeStruct((n_q, T, 1), F32)],
        compiler_params=_params(("parallel", "parallel")),
    )(*([pb] * (GROUP + 6)), bias, sink)


def _attn_b_bwd(pb, o_cat, d_o, lse, bias, sink, q_off, n_q, n_kv, o_off, deps=(), sb=8):
    T = pb.shape[0]
    nblk = T // BLOCK_Q
    sb = min(sb, nblk)
    tq = sb * BLOCK_Q
    scale = HEAD_DIM ** -0.5

    def body(*refs):
        q_refs = refs[0:GROUP]
        k_refs, v_refs = refs[GROUP:GROUP + 3], refs[GROUP + 3:GROUP + 6]
        o_refs, do_refs = refs[GROUP + 6:2 * GROUP + 6], refs[2 * GROUP + 6:3 * GROUP + 6]
        lse_ref, bias_ref, sink_ref, dq_ref, dk_ref, dv_ref, dbias_ref, dsink_ref, dkb_ref, dvb_ref = refs[3 * GROUP + 6:]
        kv, i = pl.program_id(0), pl.program_id(1)
        first = i == 0
        kb = jnp.concatenate([r[...] for r in k_refs], axis=0)
        vb = jnp.concatenate([r[...] for r in v_refs], axis=0)
        dkb_ref[...] = jnp.zeros(dkb_ref.shape, F32)
        dvb_ref[...] = jnp.zeros(dvb_ref.shape, F32)
        row = lax.broadcasted_iota(jnp.int32, (SUBLANES, LANES), 0)
        dsink = jnp.zeros((SUBLANES, LANES), F32)
        for b in range(sb):
            at_end = b == 0 or b == sb - 1
            mask = _in_sequence(i * sb + b, T) if at_end else None
            rows = slice(b * BLOCK_Q, (b + 1) * BLOCK_Q)
            win = slice(b * BLOCK_Q, (b + 3) * BLOCK_Q)
            kw, vw = kb[win], vb[win]
            dkw = jnp.zeros((3 * BLOCK_Q, HEAD_DIM), F32)
            dvw = jnp.zeros((3 * BLOCK_Q, HEAD_DIM), F32)
            for g in range(GROUP):
                sk = sink_ref[kv * GROUP + g]
                q, do = q_refs[g][rows, :], do_refs[g][rows, :]
                lse_g = lse_ref[g, rows, :]
                delta = jnp.sum(do.astype(F32) * o_refs[g][rows, :].astype(F32), axis=-1, keepdims=True)
                s = lax.dot_general(q, kw, _NT, preferred_element_type=F32) * scale + bias_ref[g]
                if at_end:
                    s = jnp.where(mask, s, NEG_INF)
                p = jnp.exp(s - lse_g)
                dp = lax.dot_general(do, vw, _NT, preferred_element_type=F32)
                ds = p * (dp - delta)
                _accumulate(dbias_ref.at[g], ds, jnp.logical_and(first, b == 0))
                dsink = dsink + jnp.where(row == g, -jnp.sum(jnp.exp(sk - lse_g) * delta), 0.0)
                dsb = (ds * scale).astype(BF16)
                dq_ref[rows, g * HEAD_DIM:(g + 1) * HEAD_DIM] = lax.dot_general(dsb, kw, _NN, preferred_element_type=F32)
                dkw = dkw + lax.dot_general(dsb, q, _TN, preferred_element_type=F32)
                dvw = dvw + lax.dot_general(p.astype(BF16), do, _TN, preferred_element_type=F32)
            dkb_ref[win, :] += dkw
            dvb_ref[win, :] += dvw
        _accumulate(dsink_ref, dsink, first)

        @pl.when(first)
        def _():
            dk_ref[...] = jnp.zeros(dk_ref.shape, F32)
            dv_ref[...] = jnp.zeros(dv_ref.shape, F32)

        before = pl.ds(pl.multiple_of(jnp.maximum(sb * i - 1, 0) * BLOCK_Q, BLOCK_Q), BLOCK_Q)
        own = pl.ds(pl.multiple_of(i * tq, BLOCK_Q), tq)
        after = pl.ds(pl.multiple_of(jnp.minimum(sb * i + sb, nblk - 1) * BLOCK_Q, BLOCK_Q), BLOCK_Q)
        for acc_ref, band_ref in ((dk_ref, dkb_ref), (dv_ref, dvb_ref)):
            acc_ref[before, :] += band_ref[0:BLOCK_Q, :]
            acc_ref[own, :] += band_ref[BLOCK_Q:BLOCK_Q + tq, :]
            acc_ref[after, :] += band_ref[BLOCK_Q + tq:, :]

    return _pcall(
        body, deps, name="attn_b_bwd", grid=(n_kv, nblk // sb),
        in_specs=[*_head_specs(q_off, tq),
                  *_band_specs(lambda kv: q_off + n_q + kv, nblk, sb),
                  *_band_specs(lambda kv: q_off + n_q + n_kv + kv, nblk, sb),
                  *_head_specs(o_off, tq), *_head_specs(o_off, tq),
                  pl.BlockSpec((GROUP, tq, 1), lambda kv, i: (kv, i, 0)),
                  pl.BlockSpec((GROUP, BLOCK_Q, 3 * BLOCK_Q), lambda kv, i: (kv, 0, 0)),
                  pl.BlockSpec(memory_space=pltpu.SMEM)],
        out_specs=[pl.BlockSpec((tq, GROUP * HEAD_DIM), lambda kv, i: (i, kv)),
                   pl.BlockSpec((T, HEAD_DIM), lambda kv, i: (0, kv)),
                   pl.BlockSpec((T, HEAD_DIM), lambda kv, i: (0, kv)),
                   pl.BlockSpec((GROUP, BLOCK_Q, 3 * BLOCK_Q), lambda kv, i: (kv, 0, 0)),
                   pl.BlockSpec((None, SUBLANES, LANES), lambda kv, i: (kv, 0, 0))],
        out_shape=[jax.ShapeDtypeStruct((T, n_q * HEAD_DIM), F32),
                   jax.ShapeDtypeStruct((T, n_kv * HEAD_DIM), F32),
                   jax.ShapeDtypeStruct((T, n_kv * HEAD_DIM), F32),
                   jax.ShapeDtypeStruct((n_q, BLOCK_Q, 3 * BLOCK_Q), F32),
                   jax.ShapeDtypeStruct((n_kv, SUBLANES, LANES), F32)],
        scratch_shapes=[pltpu.VMEM((tq + 2 * BLOCK_Q, HEAD_DIM), F32), pltpu.VMEM((tq + 2 * BLOCK_Q, HEAD_DIM), F32)],
        compiler_params=_params(("parallel", "arbitrary")),
    )(*([pb] * (GROUP + 6)), *([o_cat] * GROUP), *([d_o] * GROUP), lse, bias, sink)


def _table_grads(dbias, dsink_raw, idx):
    n_heads = dbias.shape[0]
    n_kv = dsink_raw.shape[0]

    def body(db_ref, ds_ref, idx_ref, dt_ref, dsk_ref):
        iv = idx_ref[...]
        row = lax.broadcasted_iota(jnp.int32, (SUBLANES, LANES), 0)
        lane = lax.broadcasted_iota(jnp.int32, (SUBLANES, LANES), 1)
        dsk = jnp.zeros((SUBLANES, LANES), F32)
        for h in range(n_heads):
            d = db_ref[h]
            acc = jnp.zeros((SUBLANES, LANES), F32)
            for b in range(N_BUCKETS):
                acc = jnp.where((row == 0) & (lane == b), jnp.sum(jnp.where(iv == b, d, 0.0)), acc)
            dt_ref[:, h * LANES:(h + 1) * LANES] = acc
            raw = ds_ref[h // GROUP]
            val = jnp.sum(jnp.where((row == h % GROUP) & (lane == 0), raw, 0.0))
            dsk = jnp.where((row == 0) & (lane == h), val, dsk)
        dsk_ref[...] = dsk

    return pl.pallas_call(
        body, name="table_grads",
        in_specs=[pl.BlockSpec(memory_space=pltpu.VMEM)] * 3,
        out_specs=[pl.BlockSpec(memory_space=pltpu.VMEM)] * 2,
        out_shape=[jax.ShapeDtypeStruct((SUBLANES, n_heads * LANES), F32),
                   jax.ShapeDtypeStruct((SUBLANES, LANES), F32)],
        compiler_params=pltpu.CompilerParams(vmem_limit_bytes=56 * 1024 * 1024),
    )(dbias, dsink_raw, idx)


def _position():
    x, y, c = lax.axis_index("x"), lax.axis_index("y"), lax.axis_index("c")
    return x, y, c


def _hbm(a):
    return pltpu.with_memory_space_constraint(a, pltpu.HBM)


def _split_start(name, bufs, sem_shapes, issue):
    nb, ns = len(bufs), len(sem_shapes)

    def body(*refs):
        buf_refs = refs[:nb]
        sems = refs[nb:nb + ns]
        token = refs[nb + ns + nb]
        issue(buf_refs, sems)
        token[...] = jnp.zeros(token.shape, F32)

    outs = pl.pallas_call(
        body, name=name,
        in_specs=[_HBM] * nb,
        out_specs=[_SEM] * ns + [_HBM] * nb + [_VMEM],
        out_shape=[pltpu.SemaphoreType.DMA(s) for s in sem_shapes] + [pltpu.HBM(b.shape, b.dtype) for b in bufs]
        + [jax.ShapeDtypeStruct((SUBLANES, LANES), F32)],
        input_output_aliases={i: ns + i for i in range(nb)},
        compiler_params=pltpu.CompilerParams(has_side_effects=_EFFECT),
    )(*[_hbm(b) for b in bufs])
    return outs[:ns], outs[ns:ns + nb], outs[-1]


def _split_wait(name, bufs, send, recv, counts, size_of, after):
    nb = len(bufs)

    def body(*refs):
        buf_refs = refs[:nb]
        send_ref, recv_ref = refs[nb], refs[nb + 1]
        x, y, c = _position()
        for w, n in enumerate(counts):
            ref = size_of(buf_refs, w)
            for k in range(n):
                s = sum(counts[:w]) + k
                cp = pltpu.make_async_remote_copy(
                    src_ref=ref, dst_ref=ref, send_sem=send_ref.at[s], recv_sem=recv_ref.at[s],
                    device_id=(x, y, c), device_id_type=MESH)
                cp.wait_send()
                cp.wait_recv()

    return pl.pallas_call(
        body, name=name,
        in_specs=[_HBM] * nb + [_SEM, _SEM, _ANY],
        out_specs=[_HBM] * nb,
        out_shape=[pltpu.HBM(b.shape, b.dtype) for b in bufs],
        input_output_aliases={i: i for i in range(nb)},
        compiler_params=pltpu.CompilerParams(has_side_effects=_EFFECT),
    )(*bufs, send, recv, after)


def _block_of(pos):
    return 4 * pos[0] + 2 * pos[1] + pos[2]


def _shard_of(ref, blk, by_cols):
    aligned = (lambda v, a: v) if isinstance(blk, int) else pl.multiple_of
    if by_cols:
        n = ref.shape[1] // N_DEV
        return ref.at[:, pl.ds(aligned(blk * n, LANES), n)]
    r = ref.shape[0] // N_DEV
    return ref.at[pl.ds(aligned(blk * r, SUBLANES), r), :]


def _place_shards(shards, by_cols):
    mine = _block_of(_position()).astype(jnp.int32).reshape(1)

    def place(name, s, cols, tr=256):
        r, n = s.shape
        tr = _tile(r, tr)

        def body(m_ref, s_ref, o_ref):
            o_ref[...] = s_ref[...].astype(BF16)

        if cols:
            out = pl.BlockSpec((tr, n), lambda i, m_ref: (i, m_ref[0]))
        else:
            out = pl.BlockSpec((tr, n), lambda i, m_ref: (m_ref[0] * (r // tr) + i, 0))
        return pl.pallas_call(
            body, name=name,
            grid_spec=pltpu.PrefetchScalarGridSpec(
                num_scalar_prefetch=1, grid=(r // tr,),
                in_specs=[pl.BlockSpec((tr, n), lambda i, m_ref: (i, 0))], out_specs=out),
            out_shape=jax.ShapeDtypeStruct((r, n * N_DEV) if cols else (r * N_DEV, n), BF16),
            compiler_params=_params(("parallel",)),
        )(mine, s)

    return [place("place_shard_%d" % w, s, cols) for w, (s, cols) in enumerate(zip(shards, by_cols))]


def _gather_start(shards, by_cols, groups):
    lands = _place_shards(shards, by_cols)

    def issue(land, sems):
        x, y, c = _position()
        peers = [(x, y, 1 - c), (1 - x, y, c), (x, 1 - y, c), (1 - x, 1 - y, c)]
        for gi, grp in enumerate(groups):
            for wi, w in enumerate(grp):
                own = _shard_of(land[w], _block_of((x, y, c)), by_cols[w])
                for k, peer in enumerate(peers):
                    pltpu.make_async_remote_copy(
                        src_ref=own, dst_ref=own, send_sem=sems[2 * gi].at[4 * wi + k],
                        recv_sem=sems[2 * gi + 1].at[4 * wi + k], device_id=peer, device_id_type=MESH).start()

    sem_shapes = [(4 * len(g),) for g in groups for _ in range(2)]
    return _split_start("gather_start", lands, sem_shapes, issue)


def _gather_forward(name, lands, by_cols):
    nw = len(lands)

    def issue(land, sems):
        x, y, c = _position()
        for w in range(nw):
            for k, chip in enumerate([(1 - x, y), (x, 1 - y), (1 - x, 1 - y)]):
                blk = _shard_of(land[w], _block_of((*chip, c)), by_cols[w])
                pltpu.make_async_remote_copy(
                    src_ref=blk, dst_ref=blk, send_sem=sems[0].at[3 * w + k], recv_sem=sems[1].at[3 * w + k],
                    device_id=(x, y, 1 - c), device_id_type=MESH).start()

    return _split_start(name, lands, [(3 * nw,), (3 * nw,)], issue)


def _first_block(bufs, w, offset=0):
    return bufs[offset + w].at[0]


_PEER_FLIPS = ((0, 0, 1), (1, 0, 0), (1, 0, 1), (0, 1, 0), (0, 1, 1), (1, 1, 0), (1, 1, 1))


def _scatter_start(name, grads, by_cols):
    nw = len(grads)
    lands = []
    for g, cols in zip(grads, by_cols):
        shard = (g.shape[0], g.shape[1] // N_DEV) if cols else (g.shape[0] // N_DEV, g.shape[1])
        lands.append(lax.empty((N_DEV,) + shard, g.dtype))

    def issue(bufs, sems):
        x, y, c = _position()
        flip = lambda v, f: 1 - v if f else v
        for w in range(nw):
            for k, (fx, fy, fc) in enumerate(_PEER_FLIPS):
                peer = (flip(x, fx), flip(y, fy), flip(c, fc))
                pltpu.make_async_remote_copy(
                    src_ref=_shard_of(bufs[w], _block_of(peer), by_cols[w]), dst_ref=bufs[nw + w].at[_block_of((x, y, c))],
                    send_sem=sems[0].at[7 * w + k], recv_sem=sems[1].at[7 * w + k],
                    device_id=peer, device_id_type=MESH).start()

    return _split_start(name, list(grads) + lands, [(7 * nw,), (7 * nw,)], issue)


def _adam(w, g, m, v):
    m = ADAM_B1 * m + (1.0 - ADAM_B1) * g
    v = ADAM_B2 * v + (1.0 - ADAM_B2) * (g * g)
    m_hat = m / (1.0 - ADAM_B1 ** ADAM_STEP)
    v_hat = v / (1.0 - ADAM_B2 ** ADAM_STEP)
    delta = -ADAM_LR * (m_hat / (jnp.sqrt(v_hat) + ADAM_EPS) + ADAM_WD * w)
    return delta, m, v


def _sum_adam(name, landed, grad, by_cols, w, m, v, tr=256):
    R, C = w.shape
    tr = _tile(R, tr)
    mine = _block_of(_position()).astype(jnp.int32).reshape(1)

    def body(me_ref, l_ref, own_ref, w_ref, m_ref, v_ref, g_ref, d_ref, nm_ref, nv_ref):
        own = own_ref[...].astype(F32)
        g = None
        for d in range(N_DEV):
            part = jnp.where(me_ref[0] == d, own, l_ref[d].astype(F32))
            g = part if g is None else g + part
        g_ref[...] = g
        d_ref[...], nm_ref[...], nv_ref[...] = _adam(w_ref[...], g, m_ref[...], v_ref[...])

    tile = pl.BlockSpec((tr, C), lambda i, me_ref: (i, 0))
    if by_cols:
        own = pl.BlockSpec((tr, C), lambda i, me_ref: (i, me_ref[0]))
    else:
        own = pl.BlockSpec((tr, C), lambda i, me_ref: (me_ref[0] * (R // tr) + i, 0))
    return pl.pallas_call(
        body, name=name,
        grid_spec=pltpu.PrefetchScalarGridSpec(
            num_scalar_prefetch=1, grid=(R // tr,),
            in_specs=[pl.BlockSpec((N_DEV, tr, C), lambda i, me_ref: (0, i, 0)), own, tile, tile, tile],
            out_specs=[tile] * 4),
        out_shape=[jax.ShapeDtypeStruct((R, C), F32)] * 4,
        compiler_params=_params(("parallel",)),
    )(mine, landed, grad, w, m, v)


def _small_all_reduce(parts):
    W = parts.shape[1]

    def body(p_ref, o_ref, slots, send_sems, recv_sems):
        x, y, c = _position()
        me = 4 * x + 2 * y + c
        slots[me] = jnp.sum(p_ref[...], axis=0, keepdims=True)
        peers = [(x, y, 1 - c), (1 - x, y, c), (1 - x, y, 1 - c), (x, 1 - y, c), (x, 1 - y, 1 - c),
                 (1 - x, 1 - y, c), (1 - x, 1 - y, 1 - c)]
        copies = []
        for k, peer in enumerate(peers):
            cp = pltpu.make_async_remote_copy(
                src_ref=slots.at[me], dst_ref=slots.at[me], send_sem=send_sems.at[k], recv_sem=recv_sems.at[k],
                device_id=peer, device_id_type=MESH)
            cp.start()
            copies.append(cp)
        for cp in copies:
            cp.wait()
        total = slots[0]
        for d in range(1, N_DEV):
            total = total + slots[d]
        o_ref[...] = total

    return pl.pallas_call(
        body, name="small_all_reduce",
        in_specs=[pl.BlockSpec(memory_space=pltpu.VMEM)], out_specs=pl.BlockSpec(memory_space=pltpu.VMEM),
        out_shape=jax.ShapeDtypeStruct((1, W), F32),
        scratch_shapes=[pltpu.VMEM((N_DEV, 1, W), F32), pltpu.SemaphoreType.DMA((7,)), pltpu.SemaphoreType.DMA((7,))],
    )(parts)


def _adam_small(w, g, m, v):
    def body(w_ref, g_ref, m_ref, v_ref, d_ref, nm_ref, nv_ref):
        d_ref[...], nm_ref[...], nv_ref[...] = _adam(w_ref[...], g_ref[...], m_ref[...], v_ref[...])

    return pl.pallas_call(
        body, name="adam_small",
        in_specs=[pl.BlockSpec(memory_space=pltpu.VMEM)] * 4, out_specs=[pl.BlockSpec(memory_space=pltpu.VMEM)] * 3,
        out_shape=[jax.ShapeDtypeStruct(w.shape, F32)] * 3,
    )(w, g, m, v)


_GATHER_GROUPS = (("w_in",), ("w_out", "w_up", "ple_w"), ("w_down", "w_gate"))
_COL_SHARDED = ("w_in", "w_up", "ple_w")


class _MeshComm:
    def __init__(self, w, mom, var):
        self.w, self.mom, self.var = w, mom, var
        self.out = {}
        self._scatters = {}

    def gather_begin(self):
        names = [n for g in _GATHER_GROUPS for n in g]
        self._idx = {n: i for i, n in enumerate(names)}
        groups = [[self._idx[n] for n in g] for g in _GATHER_GROUPS]
        self._sems, self._lands, token = _gather_start(
            [self.w[n] for n in names], [n in _COL_SHARDED for n in names], groups)
        return token

    @staticmethod
    def _shard_size(names, offset):
        return lambda bufs, w: _shard_of(bufs[offset + w], 0, names[w] in _COL_SHARDED)

    def gather_arrive(self, gi, after):
        names = _GATHER_GROUPS[gi]
        ids = [self._idx[n] for n in names]
        self._arrived = _split_wait("gather_arrive%d" % gi, [self._lands[i] for i in ids], self._sems[2 * gi],
                                    self._sems[2 * gi + 1], [4] * len(ids), self._shard_size(names, 0), after)

    def gather_forward(self, gi):
        by_cols = [n in _COL_SHARDED for n in _GATHER_GROUPS[gi]]
        self._fsems, self._fthru, token = _gather_forward("gather_forward%d" % gi, self._arrived, by_cols)
        return token

    def gather_finish(self, gi, after):
        names = _GATHER_GROUPS[gi]
        out = _split_wait("gather_finish%d" % gi, self._fthru, self._fsems[0], self._fsems[1], [3] * len(names),
                          self._shard_size(names, 0), after)
        return dict(zip(names, out))

    def reduce_begin(self, key, grads):
        names = list(grads)
        sems, thru, token = _scatter_start("scatter_start_" + key, [grads[n] for n in names],
                                           [n in _COL_SHARDED for n in names])
        self._scatters[key] = (names, sems, thru)
        return token

    def reduce_finish(self, key, after):
        names, sems, thru = self._scatters[key]
        nw = len(names)
        out = _split_wait("scatter_wait_" + key, thru, sems[0], sems[1], [N_DEV - 1] * nw,
                          functools.partial(_first_block, offset=nw), after)
        for i, n in enumerate(names):
            self.out[n] = _sum_adam("adam_" + n, out[nw + i], out[i], n in _COL_SHARDED, self.w[n], self.mom[n],
                                    self.var[n])


def _step(x, p, target, gains, comm):
    T, D = x.shape
    n_q = D // (2 * HEAD_DIM)
    n_kv = n_q // GROUP
    cos, sin = _rope_tables(T)
    idx = _bucket_index()

    t = comm.gather_begin()
    u = _rms_fwd("norm_attn", x, gains["attn_norm_g"], deps=(t,))
    comm.gather_arrive(0, u)
    t = comm.gather_forward(0)
    bias = _bias_build(idx, gains["rel_bias_table"].reshape(-1), n_q, deps=(t,))
    full = comm.gather_finish(0, bias)
    proj = _mm_nn("in_proj", u, full["w_in"])
    pb = _qk_prep(proj, cos, sin, gains["q_norm_g"], gains["k_norm_g"], n_q + n_kv)
    o_a, lse_a = _attn_a_fwd(pb, n_q, n_kv)
    comm.gather_arrive(1, o_a)
    t = comm.gather_forward(1)
    sink = gains["sink_logits"].reshape(-1)
    b_off = n_q + 2 * n_kv
    o_b, lse_b = _attn_b_fwd(pb, bias, sink, b_off, n_q, n_kv, deps=(t,))
    full.update(comm.gather_finish(1, o_b))
    o_cat = jnp.concatenate([o_a, o_b], axis=1)
    h1 = _mm_nn("out_proj", o_cat, full["w_out"], epilogue=_store_add, extras=(x,))
    m_in = _rms_fwd("norm_mlp", h1, gains["mlp_norm_g"])

    def up_epilogue(acc, extra, outs):
        outs[0][...] = acc.astype(BF16)
        r = jnp.maximum(acc, 0.0)
        outs[1][...] = (r * r).astype(BF16)

    a_act, f_act = _mm_nn("up_proj", m_in, full["w_up"], epilogue=up_epilogue, out_dtypes=[BF16, BF16])
    comm.gather_arrive(2, f_act)
    t = comm.gather_forward(2)
    p_b = p.astype(BF16)
    pe = _mm_nn("ple_proj", p_b, full["ple_w"], deps=(t,))
    full.update(comm.gather_finish(2, pe))
    h2 = _mm_nn("down_proj", f_act, full["w_down"], epilogue=_store_add, extras=(h1,), tm=512, tn=512)
    gn = _rms_fwd("norm_gate", h2, gains["gate_norm_g"])
    z = _mm_nn("gate_proj", gn, full["w_gate"])

    dh3, dz, dpe, dg_final, dg_ple, loss_part = _tail(h2, z, pe, target, gains["ple_norm_g"], gains["final_norm_g"])
    gw_gate = _mm_tn("grad_w_gate", gn, dz)
    gw_ple = _mm_tn("grad_ple_w", p_b, dpe)
    t = comm.reduce_begin("a", dict(w_gate=gw_gate, ple_w=gw_ple))
    dgn = _mm_nt("d_gate_in", dz, full["w_gate"], deps=(t,))
    dh2, dh2_b, dg_gate = _rms_bwd("norm_gate_bwd", dgn, h2, gains["gate_norm_g"], dh3)
    gw_down = _mm_tn("grad_w_down", f_act, dh2_b)
    t = comm.reduce_begin("b", dict(w_down=gw_down))

    def act_bwd(acc, extra, outs):
        outs[0][...] = (acc * (2.0 * jnp.maximum(extra[0][...].astype(F32), 0.0))).astype(BF16)

    da = _mm_nt("d_act", dh2_b, full["w_down"], out_dtype=BF16, epilogue=act_bwd, extras=(a_act,), deps=(t,))
    comm.reduce_finish("a", da)
    gw_up = _mm_tn("grad_w_up", m_in, da)
    t = comm.reduce_begin("c", dict(w_up=gw_up))
    dm = _mm_nt("d_mlp_in", da, full["w_up"], tm=512, tn=512, deps=(t,))
    dh1, dh1_b, dg_mlp = _rms_bwd("norm_mlp_bwd", dm, h1, gains["mlp_norm_g"], dh2)
    comm.reduce_finish("b", dh1_b)
    gw_out = _mm_tn("grad_w_out", o_cat, dh1_b)
    t = comm.reduce_begin("d", dict(w_out=gw_out))
    d_o = _mm_nt("d_attn_out", dh1_b, full["w_out"], out_dtype=BF16, deps=(t,))
    dqa, dka, dva = _attn_a_bwd(pb, o_cat, d_o, lse_a, n_q, n_kv)
    dqb, dkb, dvb, dbias, dsink_raw = _attn_b_bwd(pb, o_cat, d_o, lse_b, bias, sink, b_off, n_q, n_kv, n_q)
    comm.reduce_finish("c", dqb)
    comm.reduce_finish("d", dqb)
    dtable, dsink = _table_grads(dbias, dsink_raw, idx)
    dproj, dg_q, dg_k = _dproj(proj, dqa, dka, dva, dqb, dkb, dvb, cos, sin, gains["q_norm_g"], gains["k_norm_g"])
    gw_in = _mm_tn("grad_w_in", u, dproj)
    t = comm.reduce_begin("e", dict(w_in=gw_in))
    du = _mm_nt("d_attn_in", dproj, full["w_in"], deps=(t,))
    dx, _, dg_attn = _rms_bwd("norm_attn_bwd", du, x, gains["attn_norm_g"], dh1)
    comm.reduce_finish("e", dx)

    parts = jnp.concatenate([dg_attn, dg_mlp, dg_ple, dg_gate, dg_final, dg_q, dg_k, dtable, dsink, loss_part], axis=1)
    return dx, parts


_SHARDED = ("w_in", "w_out", "w_up", "w_down", "ple_w", "w_gate")
_VECTORS = ("attn_norm_g", "mlp_norm_g", "ple_norm_g", "gate_norm_g", "final_norm_g")
_ORDER = ("attn_norm_g", "w_in", "q_norm_g", "k_norm_g", "sink_logits", "w_out", "mlp_norm_g", "w_up", "w_down",
          "ple_w", "ple_norm_g", "gate_norm_g", "w_gate", "rel_bias_table", "final_norm_g")


def _pack_small(vals, n_heads):
    lane_pad = lambda v: jnp.pad(v, ((0, 0), (0, LANES - v.shape[1])))
    table = lane_pad(vals["rel_bias_table"].T).reshape(1, n_heads * LANES)
    return jnp.concatenate(
        [vals[n].reshape(1, -1) for n in _VECTORS] + [vals["q_norm_g"], vals["k_norm_g"], table,
                                                      lane_pad(vals["sink_logits"]), jnp.zeros((1, LANES), F32)], axis=1)


def _unpack_small(row, like, n_heads):
    out, off = {}, 0
    for n in _VECTORS:
        out[n] = row[:, off:off + like[n].size].reshape(like[n].shape)
        off += like[n].size
    for n in ("q_norm_g", "k_norm_g"):
        out[n] = row[:, off:off + LANES]
        off += LANES
    out["rel_bias_table"] = row[:, off:off + n_heads * LANES].reshape(n_heads, LANES)[:, :N_BUCKETS].T
    off += n_heads * LANES
    out["sink_logits"] = row[:, off:off + n_heads]
    off += LANES
    return out, row[0, off]


def kernel(x, p, attn_norm_g, w_in, q_norm_g, k_norm_g, sink_logits, w_out, mlp_norm_g, w_up, w_down, ple_w, ple_norm_g, gate_norm_g, w_gate, rel_bias_table, final_norm_g, loss_target, m_attn_norm_g, m_w_in, m_q_norm_g, m_k_norm_g, m_sink_logits, m_w_out, m_mlp_norm_g, m_w_up, m_w_down, m_ple_w, m_ple_norm_g, m_gate_norm_g, m_w_gate, m_rel_bias_table, m_final_norm_g, v_attn_norm_g, v_w_in, v_q_norm_g, v_k_norm_g, v_sink_logits, v_w_out, v_mlp_norm_g, v_w_up, v_w_down, v_ple_w, v_ple_norm_g, v_gate_norm_g, v_w_gate, v_rel_bias_table, v_final_norm_g):
    w = dict(attn_norm_g=attn_norm_g, w_in=w_in[0], q_norm_g=q_norm_g, k_norm_g=k_norm_g, sink_logits=sink_logits,
             w_out=w_out[0], mlp_norm_g=mlp_norm_g, w_up=w_up[0], w_down=w_down[0], ple_w=ple_w[0],
             ple_norm_g=ple_norm_g, gate_norm_g=gate_norm_g, w_gate=w_gate[0], rel_bias_table=rel_bias_table,
             final_norm_g=final_norm_g)
    mom = dict(attn_norm_g=m_attn_norm_g, w_in=m_w_in[0], q_norm_g=m_q_norm_g, k_norm_g=m_k_norm_g,
               sink_logits=m_sink_logits, w_out=m_w_out[0], mlp_norm_g=m_mlp_norm_g, w_up=m_w_up[0],
               w_down=m_w_down[0], ple_w=m_ple_w[0], ple_norm_g=m_ple_norm_g, gate_norm_g=m_gate_norm_g,
               w_gate=m_w_gate[0], rel_bias_table=m_rel_bias_table, final_norm_g=m_final_norm_g)
    var = dict(attn_norm_g=v_attn_norm_g, w_in=v_w_in[0], q_norm_g=v_q_norm_g, k_norm_g=v_k_norm_g,
               sink_logits=v_sink_logits, w_out=v_w_out[0], mlp_norm_g=v_mlp_norm_g, w_up=v_w_up[0],
               w_down=v_w_down[0], ple_w=v_ple_w[0], ple_norm_g=v_ple_norm_g, gate_norm_g=v_gate_norm_g,
               w_gate=v_w_gate[0], rel_bias_table=v_rel_bias_table, final_norm_g=v_final_norm_g)
    D = x.shape[-1]
    n_heads = D // (2 * HEAD_DIM)

    gains = {n: w[n] for n in w if n not in _SHARDED}
    gains["final_norm_g"] = final_norm_g.reshape(1, -1)

    comm = _MeshComm(w, mom, var)
    dx, parts = _step(x[0], p[0, 0], loss_target[0], gains, comm)

    g_out, d_out, m_out, v_out = {}, {}, {}, {}
    for n in _SHARDED:
        g, d, nm, nv = comm.out[n]
        g_out[n], d_out[n], m_out[n], v_out[n] = g[None], d[None], nm[None], nv[None]

    small_g = _small_all_reduce(parts)
    small = {n: v for n, v in w.items() if n not in _SHARDED}
    pack = lambda vals: _pack_small({n: vals[n] for n in small}, n_heads)
    sd, sm, sv = _adam_small(pack(w), small_g, pack(mom), pack(var))
    sg, loss = _unpack_small(small_g, small, n_heads)
    g_out.update(sg)
    for dst, row in ((d_out, sd), (m_out, sm), (v_out, sv)):
        dst.update(_unpack_small(row, small, n_heads)[0])

    return (loss, dx[None], *[g_out[n] for n in _ORDER], *[d_out[n] for n in _ORDER],
            *[m_out[n] for n in _ORDER], *[v_out[n] for n in _ORDER])
```

```python
import functools
import math

import numpy as np
import jax
import jax.numpy as jnp
from jax import lax
from jax.experimental import pallas as pl
from jax.experimental.pallas import tpu as pltpu

F32 = jnp.float32
BF16 = jnp.bfloat16

N_DEV = 8
N_CHIP = 4
HEAD_DIM = 128
GROUP = 4
GRID_W = 64
WINDOW = 128
BLOCK_Q = 128
N_BUCKETS = 32
MAX_DISTANCE = 128
ROPE_THETA = 10000.0
EPS = 1e-6
NEG_INF = -1e30
ADAM_LR = 0.001
ADAM_B1 = 0.9
ADAM_B2 = 0.999
ADAM_EPS = 1e-08
ADAM_WD = 0.01
ADAM_STEP = 10
LOG2E = math.log2(math.e)
LANES = 128
SUBLANES = 8
MESH = pl.DeviceIdType.MESH

_NT = (((1,), (1,)), ((), ()))
_NN = (((1,), (0,)), ((), ()))
_TN = (((0,), (0,)), ((), ()))


def _tile(dim, pref):
    return pref if dim % pref == 0 else dim


def _params(sem):
    return pltpu.CompilerParams(dimension_semantics=sem, vmem_limit_bytes=56 * 1024 * 1024)


_HBM = pl.BlockSpec(memory_space=pltpu.HBM)
_SEM = pl.BlockSpec(memory_space=pltpu.SEMAPHORE)
_ANY = pl.BlockSpec(memory_space=pl.ANY)
_VMEM = pl.BlockSpec(memory_space=pltpu.VMEM)
_EFFECT = pltpu.SideEffectType.DATAFLOW_SIDE_EFFECTING


def _pcall(body, deps=(), *, in_specs, **kw):
    deps = [d for d in deps if d is not None]
    nd = len(deps)

    def wrapped(*refs):
        body(*refs[nd:])

    call = pl.pallas_call(wrapped, in_specs=[_ANY] * nd + list(in_specs), **kw)
    return lambda *args: call(*deps, *args)


def _mm(name, a, b, dims, grid, a_spec, b_spec, out_shape, out_specs, acc_shape, epilogue,
        extras=(), extra_specs=(), deps=(), semantics=("parallel", "parallel", "arbitrary")):
    nk = grid[2]
    n_extra = len(extras)

    def body(*refs):
        a_ref, b_ref = refs[0], refs[1]
        extra = refs[2:2 + n_extra]
        outs = refs[2 + n_extra:-1]
        acc = refs[-1]
        part = lax.dot_general(a_ref[...], b_ref[...], dims, preferred_element_type=F32)
        if nk == 1:
            epilogue(part, extra, outs)
        else:
            k = pl.program_id(2)

            @pl.when(k == 0)
            def _():
                acc[...] = part

            @pl.when(k > 0)
            def _():
                acc[...] += part

            @pl.when(k == nk - 1)
            def _():
                epilogue(acc[...], extra, outs)

    return _pcall(
        body, deps, name=name, grid=grid,
        in_specs=[a_spec, b_spec, *extra_specs],
        out_specs=out_specs, out_shape=out_shape,
        scratch_shapes=[pltpu.VMEM(acc_shape if nk > 1 else (SUBLANES, LANES), F32)],
        compiler_params=_params(semantics),
    )(a, b, *extras)


def _store(dtype):
    def ep(acc, extra, outs):
        outs[0][...] = acc.astype(dtype)
    return ep


def _store_add(acc, extra, outs):
    outs[0][...] = acc + extra[0][...]


def _mm_nn(name, a, b, out_dtype=F32, epilogue=None, extras=(), n_out=1, out_dtypes=None, tm=1024, tn=1024, tk=None,
           deps=()):
    M, K = a.shape
    N = b.shape[1]
    tm, tn, tk = _tile(M, tm), _tile(N, tn), _tile(K, tk or K)
    b_spec = pl.BlockSpec((tk, tn), lambda i, j, k: (k, j))
    grid = (M // tm, N // tn, K // tk)
    o_spec = pl.BlockSpec((tm, tn), lambda i, j, k: (i, j))
    out_dtypes = out_dtypes or [out_dtype] * n_out
    out_shape = [jax.ShapeDtypeStruct((M, N), d) for d in out_dtypes]
    res = _mm(name, a, b, _NN, grid, pl.BlockSpec((tm, tk), lambda i, j, k: (i, k)), b_spec,
              out_shape, [o_spec] * len(out_dtypes), (tm, tn), epilogue or _store(out_dtype),
              extras, [o_spec] * len(extras), deps)
    return res if len(out_dtypes) > 1 else res[0]


def _mm_nt(name, a, b, out_dtype=F32, epilogue=None, extras=(), tm=1024, tn=1024, tk=None, deps=()):
    M, C = a.shape
    N = b.shape[0]
    tm, tn, tk = _tile(M, tm), _tile(N, tn), _tile(C, tk or C)
    b_spec = pl.BlockSpec((tn, tk), lambda i, j, k: (j, k))
    grid = (M // tm, N // tn, C // tk)
    o_spec = pl.BlockSpec((tm, tn), lambda i, j, k: (i, j))
    return _mm(name, a, b, _NT, grid, pl.BlockSpec((tm, tk), lambda i, j, k: (i, k)), b_spec,
               [jax.ShapeDtypeStruct((M, N), out_dtype)], [o_spec], (tm, tn), epilogue or _store(out_dtype),
               extras, [o_spec] * len(extras), deps)[0]


def _mm_tn(name, a, b, out_dtype=BF16, tm=1024, tn=512, tk=None, deps=()):
    T, M = a.shape
    N = b.shape[1]
    tm, tn, tk = _tile(M, tm), _tile(N, tn), _tile(T, tk or T)
    out_shape = jax.ShapeDtypeStruct((M, N), out_dtype)
    o_spec = pl.BlockSpec((tm, tn), lambda i, j, k: (i, j))
    grid = (M // tm, N // tn, T // tk)
    return _mm(name, a, b, _TN, grid, pl.BlockSpec((tk, tm), lambda i, j, k: (k, i)),
               pl.BlockSpec((tk, tn), lambda i, j, k: (k, j)), [out_shape], [o_spec], (tm, tn), _store(out_dtype),
               deps=deps)[0]


def _mean_last(v):
    return jnp.mean(v, axis=-1, keepdims=True)


def _rows_to_sublanes(v):
    r, c = v.shape
    return jnp.sum(v.reshape(r // SUBLANES, SUBLANES, c), axis=0)


def _accumulate(ref, val, first):
    @pl.when(first)
    def _():
        ref[...] = val

    @pl.when(jnp.logical_not(first))
    def _():
        ref[...] += val


def _rms_fwd(name, x, g, tr=256, deps=()):
    T, D = x.shape
    tr = _tile(T, tr)

    def body(x_ref, g_ref, o_ref):
        xv = x_ref[...]
        r = lax.rsqrt(_mean_last(xv * xv) + EPS)
        o_ref[...] = (xv * r * g_ref[...]).astype(BF16)

    row = pl.BlockSpec((tr, D), lambda i: (i, 0))
    return _pcall(
        body, deps, name=name, grid=(T // tr,),
        in_specs=[row, pl.BlockSpec((1, D), lambda i: (0, 0))],
        out_specs=row, out_shape=jax.ShapeDtypeStruct((T, D), BF16),
        compiler_params=_params(("parallel",)),
    )(x, g)


def _rms_bwd(name, dyn, x, g, dres, tr=256, deps=()):
    T, D = x.shape
    tr = _tile(T, tr)

    def body(dy_ref, x_ref, g_ref, dr_ref, dx_ref, dxb_ref, dg_ref):
        xv = x_ref[...]
        r = lax.rsqrt(_mean_last(xv * xv) + EPS)
        xn = xv * r
        dy = dy_ref[...]
        dxn = dy * g_ref[...]
        dx = dr_ref[...] + r * (dxn - xn * _mean_last(dxn * xn))
        dx_ref[...] = dx
        dxb_ref[...] = dx.astype(BF16)
        _accumulate(dg_ref, _rows_to_sublanes(dy * xn), pl.program_id(0) == 0)

    row = pl.BlockSpec((tr, D), lambda i: (i, 0))
    return _pcall(
        body, deps, name=name, grid=(T // tr,),
        in_specs=[row, row, pl.BlockSpec((1, D), lambda i: (0, 0)), row],
        out_specs=[row, row, pl.BlockSpec((SUBLANES, D), lambda i: (0, 0))],
        out_shape=[jax.ShapeDtypeStruct((T, D), F32), jax.ShapeDtypeStruct((T, D), BF16),
                   jax.ShapeDtypeStruct((SUBLANES, D), F32)],
        compiler_params=_params(("arbitrary",)),
    )(dyn, x, g, dres)


def _mm_nt_rms_bwd(name, a, b, x, g, dres, tm=256, deps=()):
    M, C = a.shape
    N = b.shape[0]
    tm = _tile(M, tm)

    def epilogue(dy, extra, outs):
        x_ref, dr_ref, g_ref = extra
        xv = x_ref[...]
        r = lax.rsqrt(_mean_last(xv * xv) + EPS)
        xn = xv * r
        dxn = dy * g_ref[...]
        dx = dr_ref[...] + r * (dxn - xn * _mean_last(dxn * xn))
        outs[0][...] = dx
        outs[1][...] = dx.astype(BF16)
        _accumulate(outs[2], _rows_to_sublanes(dy * xn), pl.program_id(0) == 0)

    row = pl.BlockSpec((tm, N), lambda i, j, k: (i, 0))
    return _mm(name, a, b, _NT, (M // tm, 1, 1), pl.BlockSpec((tm, C), lambda i, j, k: (i, 0)),
               pl.BlockSpec((N, C), lambda i, j, k: (0, 0)),
               [jax.ShapeDtypeStruct((M, N), F32), jax.ShapeDtypeStruct((M, N), BF16),
                jax.ShapeDtypeStruct((SUBLANES, N), F32)],
               [row, row, pl.BlockSpec((SUBLANES, N), lambda i, j, k: (0, 0))], (tm, N), epilogue,
               (x, dres, g), [row, row, pl.BlockSpec((1, N), lambda i, j, k: (0, 0))], deps,
               semantics=("arbitrary", "arbitrary", "arbitrary"))


def _tail(h2, z, pe, target, g_ple, g_final, tr=256):
    T, D = h2.shape
    tr = _tile(T, tr)

    def body(h2_ref, z_ref, pe_ref, t_ref, gp_ref, gf_ref,
             dh3_ref, dz_ref, dpe_ref, dgf_ref, dgp_ref, loss_ref):
        first = pl.program_id(0) == 0
        pev = pe_ref[...]
        r3 = lax.rsqrt(_mean_last(pev * pev) + EPS)
        en = pev * r3
        e = en * gp_ref[...]
        gate = 1.0 / (1.0 + jnp.exp(-z_ref[...]))
        h3 = h2_ref[...] + gate * e
        r5 = lax.rsqrt(_mean_last(h3 * h3) + EPS)
        hn = h3 * r5
        diff = hn * gf_ref[...] - t_ref[...]
        loss_rows = 0.5 * _mean_last(diff * diff)
        row0 = lax.broadcasted_iota(jnp.int32, (SUBLANES, LANES), 0) == 0
        _accumulate(loss_ref, jnp.where(row0, jnp.sum(loss_rows), 0.0), first)
        dy = diff * (1.0 / D)
        _accumulate(dgf_ref, _rows_to_sublanes(dy * hn), first)
        dhn = dy * gf_ref[...]
        dh3 = r5 * (dhn - hn * _mean_last(dhn * hn))
        dh3_ref[...] = dh3
        dgate = dh3 * e
        de = dh3 * gate
        dz_ref[...] = (dgate * gate * (1.0 - gate)).astype(BF16)
        _accumulate(dgp_ref, _rows_to_sublanes(de * en), first)
        den = de * gp_ref[...]
        dpe_ref[...] = (r3 * (den - en * _mean_last(den * en))).astype(BF16)

    row = pl.BlockSpec((tr, D), lambda i: (i, 0))
    vec = pl.BlockSpec((1, D), lambda i: (0, 0))
    part = pl.BlockSpec((SUBLANES, D), lambda i: (0, 0))
    return pl.pallas_call(
        body, name="tail", grid=(T // tr,),
        in_specs=[row, row, row, row, vec, vec],
        out_specs=[row, row, row, part, part, pl.BlockSpec((SUBLANES, LANES), lambda i: (0, 0))],
        out_shape=[jax.ShapeDtypeStruct((T, D), F32), jax.ShapeDtypeStruct((T, D), BF16),
                   jax.ShapeDtypeStruct((T, D), BF16), jax.ShapeDtypeStruct((SUBLANES, D), F32),
                   jax.ShapeDtypeStruct((SUBLANES, D), F32), jax.ShapeDtypeStruct((SUBLANES, LANES), F32)],
        compiler_params=_params(("arbitrary",)),
    )(h2, z, pe, target, g_ple, g_final)


def _rope_tables(T):
    pos = np.arange(T)
    half = HEAD_DIM // 2
    inv = (ROPE_THETA ** (-np.arange(0, half, 2, dtype=np.float32) / half)).astype(np.float32)
    ang_r = (pos // GRID_W).astype(np.float32)[:, None] * inv
    ang_c = (pos % GRID_W).astype(np.float32)[:, None] * inv
    cos = np.concatenate([np.cos(ang_r), np.cos(ang_r), np.cos(ang_c), np.cos(ang_c)], axis=-1)
    sin = np.concatenate([-np.sin(ang_r), np.sin(ang_r), -np.sin(ang_c), np.sin(ang_c)], axis=-1)
    return jnp.asarray(cos, F32), jnp.asarray(sin, F32)


def _swap32(x):
    lane = lax.broadcasted_iota(jnp.int32, x.shape, 1)
    return jnp.where((lane % 64) < 32, pltpu.roll(x, 96, 1), pltpu.roll(x, 32, 1))


def _qk_prep(proj, cos, sin, g_q, g_k, n_norm, tr=256):
    T, W = proj.shape
    tr = _tile(T, tr)
    n_q = n_norm * GROUP // (GROUP + 1)

    def body(p_ref, c_ref, s_ref, gq_ref, gk_ref, o_ref):
        c, s = c_ref[...], s_ref[...]
        for h in range(n_norm):
            cols = slice(h * HEAD_DIM, (h + 1) * HEAD_DIM)
            xv = p_ref[:, cols]
            g = gq_ref[...] if h < n_q else gk_ref[...]
            xn = xv * lax.rsqrt(_mean_last(xv * xv) + EPS) * g
            o_ref[:, cols] = (xn * c + _swap32(xn) * s).astype(BF16)
        rest = slice(n_norm * HEAD_DIM, W)
        o_ref[:, rest] = p_ref[:, rest].astype(BF16)

    row = pl.BlockSpec((tr, W), lambda i: (i, 0))
    tab = pl.BlockSpec((tr, HEAD_DIM), lambda i: (i, 0))
    vec = pl.BlockSpec((1, HEAD_DIM), lambda i: (0, 0))
    return pl.pallas_call(
        body, name="qk_prep", grid=(T // tr,),
        in_specs=[row, tab, tab, vec, vec], out_specs=row,
        out_shape=jax.ShapeDtypeStruct((T, W), BF16),
        compiler_params=_params(("parallel",)),
    )(proj, cos, sin, g_q, g_k)


def _dproj(proj, dqa, dka, dva, dqb, dkb, dvb, cos, sin, g_q, g_k, tr=256):
    T, W = proj.shape
    tr = _tile(T, tr)
    n_q = dqa.shape[1] // HEAD_DIM
    n_kv = dka.shape[1] // HEAD_DIM
    wa = (n_q + n_kv) * HEAD_DIM

    def body(p_ref, dqa_ref, dka_ref, dva_ref, dqb_ref, dkb_ref, dvb_ref, c_ref, s_ref, gq_ref, gk_ref,
             o_ref, dgq_ref, dgk_ref):
        c, s = c_ref[...], s_ref[...]
        dgq = jnp.zeros((SUBLANES, HEAD_DIM), F32)
        dgk = jnp.zeros((SUBLANES, HEAD_DIM), F32)
        for h in range(n_q + n_kv):
            cols = slice(h * HEAD_DIM, (h + 1) * HEAD_DIM)
            xv = p_ref[:, cols]
            r = lax.rsqrt(_mean_last(xv * xv) + EPS)
            xn = xv * r
            if h < n_q:
                d = dqa_ref[:, cols]
                g = gq_ref[...]
            else:
                d = dka_ref[:, (h - n_q) * HEAD_DIM:(h - n_q + 1) * HEAD_DIM]
                g = gk_ref[...]
            dqn = d * c + _swap32(d * s)
            part = _rows_to_sublanes(dqn * xn)
            if h < n_q:
                dgq = dgq + part
            else:
                dgk = dgk + part
            dxn = dqn * g
            o_ref[:, cols] = (r * (dxn - xn * _mean_last(dxn * xn))).astype(BF16)
        off = wa
        for ref in (dva_ref, dqb_ref, dkb_ref, dvb_ref):
            w = ref.shape[1]
            o_ref[:, off:off + w] = ref[...].astype(BF16)
            off += w
        first = pl.program_id(0) == 0
        _accumulate(dgq_ref, dgq, first)
        _accumulate(dgk_ref, dgk, first)

    def row(w):
        return pl.BlockSpec((tr, w), lambda i: (i, 0))

    vec = pl.BlockSpec((1, HEAD_DIM), lambda i: (0, 0))
    part = pl.BlockSpec((SUBLANES, HEAD_DIM), lambda i: (0, 0))
    return pl.pallas_call(
        body, name="dproj", grid=(T // tr,),
        in_specs=[row(wa), row(dqa.shape[1]), row(dka.shape[1]), row(dva.shape[1]), row(dqb.shape[1]),
                  row(dkb.shape[1]), row(dvb.shape[1]), row(HEAD_DIM), row(HEAD_DIM), vec, vec],
        out_specs=[row(W), part, part],
        out_shape=[jax.ShapeDtypeStruct((T, W), BF16), jax.ShapeDtypeStruct((SUBLANES, HEAD_DIM), F32),
                   jax.ShapeDtypeStruct((SUBLANES, HEAD_DIM), F32)],
        compiler_params=_params(("arbitrary",)),
    )(proj, dqa, dka, dva, dqb, dkb, dvb, cos, sin, g_q, g_k)


def _attn_a_fwd(pb, n_q, n_kv, tq=1024, tc=1024):
    T = pb.shape[0]
    tq, tc = _tile(T, tq), _tile(T, tc)
    scale = HEAD_DIM ** -0.5
    c = scale * LOG2E

    def body(q_ref, k_ref, v_ref, o_ref, lse_ref):
        q = q_ref[...]
        m = l = acc = None
        for j in range(T // tc):
            keys = slice(j * tc, (j + 1) * tc)
            s = lax.dot_general(q, k_ref[keys, :], _NT, preferred_element_type=F32)
            mj = jnp.max(s, axis=-1, keepdims=True)
            m_new = mj if j == 0 else jnp.maximum(m, mj)
            p = jnp.exp2((s - m_new) * c)
            pv = lax.dot_general(p.astype(BF16), v_ref[keys, :], _NN, preferred_element_type=F32)
            if j == 0:
                l, acc = jnp.sum(p, axis=-1, keepdims=True), pv
            else:
                alpha = jnp.exp2((m - m_new) * c)
                l = alpha * l + jnp.sum(p, axis=-1, keepdims=True)
                acc = alpha * acc + pv
            m = m_new
        o_ref[...] = (acc / l).astype(BF16)
        lse_ref[...] = m * scale + jnp.log(l)

    return pl.pallas_call(
        body, name="attn_a_fwd", grid=(n_kv, GROUP, T // tq),
        in_specs=[pl.BlockSpec((tq, HEAD_DIM), lambda kv, g, i: (i, kv * GROUP + g)),
                  pl.BlockSpec((T, HEAD_DIM), lambda kv, g, i: (0, n_q + kv)),
                  pl.BlockSpec((T, HEAD_DIM), lambda kv, g, i: (0, n_q + n_kv + kv))],
        out_specs=[pl.BlockSpec((tq, HEAD_DIM), lambda kv, g, i: (i, kv * GROUP + g)),
                   pl.BlockSpec((None, tq, 1), lambda kv, g, i: (kv * GROUP + g, i, 0))],
        out_shape=[jax.ShapeDtypeStruct((T, n_q * HEAD_DIM), BF16), jax.ShapeDtypeStruct((n_q, T, 1), F32)],
        compiler_params=_params(("parallel", "parallel", "parallel")),
    )(pb, pb, pb)


def _attn_a_bwd(pb, o_cat, d_o, lse, n_q, n_kv, tq=512, tc=512):
    T = pb.shape[0]
    tq, tc = _tile(T, tq), _tile(T, tc)
    scale = HEAD_DIM ** -0.5
    c = scale * LOG2E

    def body(q_ref, k_ref, v_ref, o_ref, do_ref, lse_ref, dq_ref, dk_ref, dv_ref):
        q, do = q_ref[...], do_ref[...]
        delta = jnp.sum(do.astype(F32) * o_ref[...].astype(F32), axis=-1, keepdims=True)
        lse2 = lse_ref[...] * LOG2E

        @pl.when(jnp.logical_and(pl.program_id(1) == 0, pl.program_id(2) == 0))
        def _():
            dk_ref[...] = jnp.zeros(dk_ref.shape, F32)
            dv_ref[...] = jnp.zeros(dv_ref.shape, F32)

        dq = None
        for j in range(T // tc):
            keys = slice(j * tc, (j + 1) * tc)
            kc, vc = k_ref[keys, :], v_ref[keys, :]
            s = lax.dot_general(q, kc, _NT, preferred_element_type=F32)
            p = jnp.exp2(s * c - lse2)
            dp = lax.dot_general(do, vc, _NT, preferred_element_type=F32)
            ds = (p * (dp - delta) * scale).astype(BF16)
            dqj = lax.dot_general(ds, kc, _NN, preferred_element_type=F32)
            dq = dqj if dq is None else dq + dqj
            dv_ref[keys, :] += lax.dot_general(p.astype(BF16), do, _TN, preferred_element_type=F32)
            dk_ref[keys, :] += lax.dot_general(ds, q, _TN, preferred_element_type=F32)
        dq_ref[...] = dq

    qmap = lambda kv, g, i: (i, kv * GROUP + g)
    return pl.pallas_call(
        body, name="attn_a_bwd", grid=(n_kv, GROUP, T // tq),
        in_specs=[pl.BlockSpec((tq, HEAD_DIM), qmap),
                  pl.BlockSpec((T, HEAD_DIM), lambda kv, g, i: (0, n_q + kv)),
                  pl.BlockSpec((T, HEAD_DIM), lambda kv, g, i: (0, n_q + n_kv + kv)),
                  pl.BlockSpec((tq, HEAD_DIM), qmap),
                  pl.BlockSpec((tq, HEAD_DIM), qmap),
                  pl.BlockSpec((None, tq, 1), lambda kv, g, i: (kv * GROUP + g, i, 0))],
        out_specs=[pl.BlockSpec((tq, HEAD_DIM), qmap),
                   pl.BlockSpec((T, HEAD_DIM), lambda kv, g, i: (0, kv)),
                   pl.BlockSpec((T, HEAD_DIM), lambda kv, g, i: (0, kv))],
        out_shape=[jax.ShapeDtypeStruct((T, n_q * HEAD_DIM), F32),
                   jax.ShapeDtypeStruct((T, n_kv * HEAD_DIM), F32),
                   jax.ShapeDtypeStruct((T, n_kv * HEAD_DIM), F32)],
        compiler_params=_params(("parallel", "arbitrary", "arbitrary")),
    )(pb, pb, pb, o_cat, d_o, lse)


def _bucket_index():
    r = np.arange(BLOCK_Q)[:, None]
    j = np.arange(3 * BLOCK_Q)[None, :]
    rel = (j - BLOCK_Q) - r
    nb = N_BUCKETS // 2
    ret = np.where(rel > 0, nb, 0)
    n = np.abs(rel)
    max_exact = nb // 2
    nf = np.maximum(n, 1).astype(np.float32)
    large = max_exact + (np.log(nf / max_exact) / math.log(MAX_DISTANCE / max_exact) * (nb - max_exact)).astype(np.int32)
    large = np.minimum(large, nb - 1)
    return jnp.asarray(ret + np.where(n < max_exact, n, large), jnp.int32)


def _bias_build(idx, table_flat, n_heads, deps=()):
    def body(idx_ref, tab_ref, o_ref):
        h = pl.program_id(0)
        iv = idx_ref[...]
        acc = jnp.zeros(iv.shape, F32)
        for b in range(N_BUCKETS):
            acc = jnp.where(iv == b, tab_ref[b * n_heads + h], acc)
        r = lax.broadcasted_iota(jnp.int32, iv.shape, 0)
        j = lax.broadcasted_iota(jnp.int32, iv.shape, 1)
        o_ref[...] = jnp.where(jnp.abs(j - BLOCK_Q - r) <= WINDOW, acc, NEG_INF)

    return _pcall(
        body, deps, name="bias_build", grid=(n_heads,),
        in_specs=[pl.BlockSpec(idx.shape, lambda h: (0, 0)), pl.BlockSpec(memory_space=pltpu.SMEM)],
        out_specs=pl.BlockSpec((None,) + idx.shape, lambda h: (h, 0, 0)),
        out_shape=jax.ShapeDtypeStruct((n_heads,) + idx.shape, F32),
        compiler_params=_params(("parallel",)),
    )(idx, table_flat)


def _in_sequence(n, T):
    j = lax.broadcasted_iota(jnp.int32, (BLOCK_Q, 3 * BLOCK_Q), 1)
    kabs = n * BLOCK_Q + j - BLOCK_Q
    return (kabs >= 0) & (kabs < T)


def _band_specs(col, nblk, sb):
    return [pl.BlockSpec((BLOCK_Q, HEAD_DIM), lambda kv, i: (jnp.maximum(sb * i - 1, 0), col(kv))),
            pl.BlockSpec((sb * BLOCK_Q, HEAD_DIM), lambda kv, i: (i, col(kv))),
            pl.BlockSpec((BLOCK_Q, HEAD_DIM), lambda kv, i: (jnp.minimum(sb * i + sb, nblk - 1), col(kv)))]


def _head_specs(base, rows):
    return [pl.BlockSpec((rows, HEAD_DIM), functools.partial(lambda kv, i, g: (i, base + kv * GROUP + g), g=g))
            for g in range(GROUP)]


def _attn_b_fwd(pb, bias, sink, q_off, n_q, n_kv, deps=(), sb=8):
    T = pb.shape[0]
    nblk = T // BLOCK_Q
    sb = min(sb, nblk)
    tq = sb * BLOCK_Q
    scale = HEAD_DIM ** -0.5

    def body(*refs):
        q_refs = refs[0:GROUP]
        k_refs, v_refs = refs[GROUP:GROUP + 3], refs[GROUP + 3:GROUP + 6]
        bias_ref, sink_ref, o_ref, lse_ref = refs[GROUP + 6:]
        kv, i = pl.program_id(0), pl.program_id(1)
        kb = jnp.concatenate([r[...] for r in k_refs], axis=0)
        vb = jnp.concatenate([r[...] for r in v_refs], axis=0)
        for b in range(sb):
            at_end = b == 0 or b == sb - 1
            mask = _in_sequence(i * sb + b, T) if at_end else None
            rows = slice(b * BLOCK_Q, (b + 1) * BLOCK_Q)
            kw, vw = kb[b * BLOCK_Q:(b + 3) * BLOCK_Q], vb[b * BLOCK_Q:(b + 3) * BLOCK_Q]
            for g in range(GROUP):
                sk = sink_ref[kv * GROUP + g]
                s = lax.dot_general(q_refs[g][rows, :], kw, _NT, preferred_element_type=F32) * scale + bias_ref[g]
                if at_end:
                    s = jnp.where(mask, s, NEG_INF)
                m = jnp.maximum(jnp.max(s, axis=-1, keepdims=True), sk)
                p = jnp.exp(s - m)
                l = jnp.sum(p, axis=-1, keepdims=True) + jnp.exp(sk - m)
                o = lax.dot_general(p.astype(BF16), vw, _NN, preferred_element_type=F32)
                o_ref[rows, g * HEAD_DIM:(g + 1) * HEAD_DIM] = (o / l).astype(BF16)
                lse_ref[g, rows, :] = m + jnp.log(l)

    return _pcall(
        body, deps, name="attn_b_fwd", grid=(n_kv, nblk // sb),
        in_specs=[*_head_specs(q_off, tq),
                  *_band_specs(lambda kv: q_off + n_q + kv, nblk, sb),
                  *_band_specs(lambda kv: q_off + n_q + n_kv + kv, nblk, sb),
                  pl.BlockSpec((GROUP, BLOCK_Q, 3 * BLOCK_Q), lambda kv, i: (kv, 0, 0)),
                  pl.BlockSpec(memory_space=pltpu.SMEM)],
        out_specs=[pl.BlockSpec((tq, GROUP * HEAD_DIM), lambda kv, i: (i, kv)),
                   pl.BlockSpec((GROUP, tq, 1), lambda kv, i: (kv, i, 0))],
        out_shape=[jax.ShapeDtypeStruct((T, n_q * HEAD_DIM), BF16), jax.ShapeDtypeStruct((n_q, T, 1), F32)],
        compiler_params=_params(("parallel", "parallel")),
    )(*([pb] * (GROUP + 6)), bias, sink)


def _attn_b_bwd(pb, o_cat, d_o, lse, bias, sink, q_off, n_q, n_kv, o_off, deps=(), sb=8):
    T = pb.shape[0]
    nblk = T // BLOCK_Q
    sb = min(sb, nblk)
    tq = sb * BLOCK_Q
    scale = HEAD_DIM ** -0.5

    def body(*refs):
        q_refs = refs[0:GROUP]
        k_refs, v_refs = refs[GROUP:GROUP + 3], refs[GROUP + 3:GROUP + 6]
        o_refs, do_refs = refs[GROUP + 6:2 * GROUP + 6], refs[2 * GROUP + 6:3 * GROUP + 6]
        lse_ref, bias_ref, sink_ref, dq_ref, dk_ref, dv_ref, dbias_ref, dsink_ref, dkb_ref, dvb_ref = refs[3 * GROUP + 6:]
        kv, i = pl.program_id(0), pl.program_id(1)
        first = i == 0

        @pl.when(first)
        def _():
            dk_ref[...] = jnp.zeros(dk_ref.shape, F32)
            dv_ref[...] = jnp.zeros(dv_ref.shape, F32)
            dbias_ref[...] = jnp.zeros(dbias_ref.shape, F32)

        kb = jnp.concatenate([r[...] for r in k_refs], axis=0)
        vb = jnp.concatenate([r[...] for r in v_refs], axis=0)
        dkb_ref[...] = jnp.zeros(dkb_ref.shape, F32)
        dvb_ref[...] = jnp.zeros(dvb_ref.shape, F32)
        row = lax.broadcasted_iota(jnp.int32, (SUBLANES, LANES), 0)
        dsink = jnp.zeros((SUBLANES, LANES), F32)
        for b in range(sb):
            at_end = b == 0 or b == sb - 1
            mask = _in_sequence(i * sb + b, T) if at_end else None
            rows = slice(b * BLOCK_Q, (b + 1) * BLOCK_Q)
            win = slice(b * BLOCK_Q, (b + 3) * BLOCK_Q)
            kw, vw = kb[win], vb[win]
            dkw = jnp.zeros((3 * BLOCK_Q, HEAD_DIM), F32)
            dvw = jnp.zeros((3 * BLOCK_Q, HEAD_DIM), F32)
            for g in range(GROUP):
                sk = sink_ref[kv * GROUP + g]
                q, do = q_refs[g][rows, :], do_refs[g][rows, :]
                lse_g = lse_ref[g, rows, :]
                delta = jnp.sum(do.astype(F32) * o_refs[g][rows, :].astype(F32), axis=-1, keepdims=True)
                s = lax.dot_general(q, kw, _NT, preferred_element_type=F32) * scale + bias_ref[g]
                if at_end:
                    s = jnp.where(mask, s, NEG_INF)
                p = jnp.exp(s - lse_g)
                dp = lax.dot_general(do, vw, _NT, preferred_element_type=F32)
                ds = p * (dp - delta)
                dbias_ref[g] += ds
                dsink = dsink + jnp.where(row == g, -jnp.sum(jnp.exp(sk - lse_g) * delta), 0.0)
                dsb = (ds * scale).astype(BF16)
                dq_ref[rows, g * HEAD_DIM:(g + 1) * HEAD_DIM] = lax.dot_general(dsb, kw, _NN, preferred_element_type=F32)
                dkw = dkw + lax.dot_general(dsb, q, _TN, preferred_element_type=F32)
                dvw = dvw + lax.dot_general(p.astype(BF16), do, _TN, preferred_element_type=F32)
            dkb_ref[win, :] += dkw
            dvb_ref[win, :] += dvw
        _accumulate(dsink_ref, dsink, first)

        before = pl.ds(pl.multiple_of(jnp.maximum(sb * i - 1, 0) * BLOCK_Q, BLOCK_Q), BLOCK_Q)
        own = pl.ds(pl.multiple_of(i * tq, BLOCK_Q), tq)
        after = pl.ds(pl.multiple_of(jnp.minimum(sb * i + sb, nblk - 1) * BLOCK_Q, BLOCK_Q), BLOCK_Q)
        for acc_ref, band_ref in ((dk_ref, dkb_ref), (dv_ref, dvb_ref)):
            acc_ref[before, :] += band_ref[0:BLOCK_Q, :]
            acc_ref[own, :] += band_ref[BLOCK_Q:BLOCK_Q + tq, :]
            acc_ref[after, :] += band_ref[BLOCK_Q + tq:, :]

    return _pcall(
        body, deps, name="attn_b_bwd", grid=(n_kv, nblk // sb),
        in_specs=[*_head_specs(q_off, tq),
                  *_band_specs(lambda kv: q_off + n_q + kv, nblk, sb),
                  *_band_specs(lambda kv: q_off + n_q + n_kv + kv, nblk, sb),
                  *_head_specs(o_off, tq), *_head_specs(o_off, tq),
                  pl.BlockSpec((GROUP, tq, 1), lambda kv, i: (kv, i, 0)),
                  pl.BlockSpec((GROUP, BLOCK_Q, 3 * BLOCK_Q), lambda kv, i: (kv, 0, 0)),
                  pl.BlockSpec(memory_space=pltpu.SMEM)],
        out_specs=[pl.BlockSpec((tq, GROUP * HEAD_DIM), lambda kv, i: (i, kv)),
                   pl.BlockSpec((T, HEAD_DIM), lambda kv, i: (0, kv)),
                   pl.BlockSpec((T, HEAD_DIM), lambda kv, i: (0, kv)),
                   pl.BlockSpec((GROUP, BLOCK_Q, 3 * BLOCK_Q), lambda kv, i: (kv, 0, 0)),
                   pl.BlockSpec((None, SUBLANES, LANES), lambda kv, i: (kv, 0, 0))],
        out_shape=[jax.ShapeDtypeStruct((T, n_q * HEAD_DIM), F32),
                   jax.ShapeDtypeStruct((T, n_kv * HEAD_DIM), F32),
                   jax.ShapeDtypeStruct((T, n_kv * HEAD_DIM), F32),
                   jax.ShapeDtypeStruct((n_q, BLOCK_Q, 3 * BLOCK_Q), F32),
                   jax.ShapeDtypeStruct((n_kv, SUBLANES, LANES), F32)],
        scratch_shapes=[pltpu.VMEM((tq + 2 * BLOCK_Q, HEAD_DIM), F32), pltpu.VMEM((tq + 2 * BLOCK_Q, HEAD_DIM), F32)],
        compiler_params=_params(("parallel", "arbitrary")),
    )(*([pb] * (GROUP + 6)), *([o_cat] * GROUP), *([d_o] * GROUP), lse, bias, sink)


def _table_grads(dbias, dsink_raw, idx):
    n_heads = dbias.shape[0]
    n_kv = dsink_raw.shape[0]

    def body(db_ref, ds_ref, idx_ref, dt_ref, dsk_ref):
        iv = idx_ref[...]
        row = lax.broadcasted_iota(jnp.int32, (SUBLANES, LANES), 0)
        lane = lax.broadcasted_iota(jnp.int32, (SUBLANES, LANES), 1)
        dsk = jnp.zeros((SUBLANES, LANES), F32)
        for h in range(n_heads):
            d = db_ref[h]
            acc = jnp.zeros((SUBLANES, LANES), F32)
            for b in range(N_BUCKETS):
                acc = jnp.where((row == 0) & (lane == b), jnp.sum(jnp.where(iv == b, d, 0.0)), acc)
            dt_ref[:, h * LANES:(h + 1) * LANES] = acc
            raw = ds_ref[h // GROUP]
            val = jnp.sum(jnp.where((row == h % GROUP) & (lane == 0), raw, 0.0))
            dsk = jnp.where((row == 0) & (lane == h), val, dsk)
        dsk_ref[...] = dsk

    return pl.pallas_call(
        body, name="table_grads",
        in_specs=[pl.BlockSpec(memory_space=pltpu.VMEM)] * 3,
        out_specs=[pl.BlockSpec(memory_space=pltpu.VMEM)] * 2,
        out_shape=[jax.ShapeDtypeStruct((SUBLANES, n_heads * LANES), F32),
                   jax.ShapeDtypeStruct((SUBLANES, LANES), F32)],
        compiler_params=pltpu.CompilerParams(vmem_limit_bytes=56 * 1024 * 1024),
    )(dbias, dsink_raw, idx)


def _position():
    x, y, c = lax.axis_index("x"), lax.axis_index("y"), lax.axis_index("c")
    return x, y, c


def _hbm(a):
    return pltpu.with_memory_space_constraint(a, pltpu.HBM)


def _split_start(name, bufs, sem_shapes, issue):
    nb, ns = len(bufs), len(sem_shapes)

    def body(*refs):
        buf_refs = refs[:nb]
        sems = refs[nb:nb + ns]
        token = refs[nb + ns + nb]
        issue(buf_refs, sems)
        token[...] = jnp.zeros(token.shape, F32)

    outs = pl.pallas_call(
        body, name=name,
        in_specs=[_HBM] * nb,
        out_specs=[_SEM] * ns + [_HBM] * nb + [_VMEM],
        out_shape=[pltpu.SemaphoreType.DMA(s) for s in sem_shapes] + [pltpu.HBM(b.shape, b.dtype) for b in bufs]
        + [jax.ShapeDtypeStruct((SUBLANES, LANES), F32)],
        input_output_aliases={i: ns + i for i in range(nb)},
        compiler_params=pltpu.CompilerParams(has_side_effects=_EFFECT),
    )(*[_hbm(b) for b in bufs])
    return outs[:ns], outs[ns:ns + nb], outs[-1]


def _split_wait(name, bufs, send, recv, counts, size_of, after):
    nb = len(bufs)

    def body(*refs):
        buf_refs = refs[:nb]
        send_ref, recv_ref = refs[nb], refs[nb + 1]
        x, y, c = _position()
        for w, n in enumerate(counts):
            ref = size_of(buf_refs, w)
            for k in range(n):
                s = sum(counts[:w]) + k
                cp = pltpu.make_async_remote_copy(
                    src_ref=ref, dst_ref=ref, send_sem=send_ref.at[s], recv_sem=recv_ref.at[s],
                    device_id=(x, y, c), device_id_type=MESH)
                cp.wait_send()
                cp.wait_recv()

    return pl.pallas_call(
        body, name=name,
        in_specs=[_HBM] * nb + [_SEM, _SEM, _ANY],
        out_specs=[_HBM] * nb,
        out_shape=[pltpu.HBM(b.shape, b.dtype) for b in bufs],
        input_output_aliases={i: i for i in range(nb)},
        compiler_params=pltpu.CompilerParams(has_side_effects=_EFFECT),
    )(*bufs, send, recv, after)


def _block_of(pos):
    return 4 * pos[0] + 2 * pos[1] + pos[2]


def _shard_of(ref, blk, by_cols):
    aligned = (lambda v, a: v) if isinstance(blk, int) else pl.multiple_of
    if by_cols:
        n = ref.shape[1] // N_DEV
        return ref.at[:, pl.ds(aligned(blk * n, LANES), n)]
    r = ref.shape[0] // N_DEV
    return ref.at[pl.ds(aligned(blk * r, SUBLANES), r), :]


def _place_shards(shards, by_cols):
    mine = _block_of(_position()).astype(jnp.int32).reshape(1)

    def place(name, s, cols, tr=256):
        r, n = s.shape
        tr = _tile(r, tr)

        def body(m_ref, s_ref, o_ref):
            o_ref[...] = s_ref[...].astype(BF16)

        if cols:
            out = pl.BlockSpec((tr, n), lambda i, m_ref: (i, m_ref[0]))
        else:
            out = pl.BlockSpec((tr, n), lambda i, m_ref: (m_ref[0] * (r // tr) + i, 0))
        return pl.pallas_call(
            body, name=name,
            grid_spec=pltpu.PrefetchScalarGridSpec(
                num_scalar_prefetch=1, grid=(r // tr,),
                in_specs=[pl.BlockSpec((tr, n), lambda i, m_ref: (i, 0))], out_specs=out),
            out_shape=jax.ShapeDtypeStruct((r, n * N_DEV) if cols else (r * N_DEV, n), BF16),
            compiler_params=_params(("parallel",)),
        )(mine, s)

    return [place("place_shard_%d" % w, s, cols) for w, (s, cols) in enumerate(zip(shards, by_cols))]


def _gather_start(shards, by_cols, groups):
    lands = _place_shards(shards, by_cols)

    def issue(land, sems):
        x, y, c = _position()
        peers = [(x, y, 1 - c), (1 - x, y, c), (x, 1 - y, c), (1 - x, 1 - y, c)]
        for gi, grp in enumerate(groups):
            for wi, w in enumerate(grp):
                own = _shard_of(land[w], _block_of((x, y, c)), by_cols[w])
                for k, peer in enumerate(peers):
                    pltpu.make_async_remote_copy(
                        src_ref=own, dst_ref=own, send_sem=sems[2 * gi].at[4 * wi + k],
                        recv_sem=sems[2 * gi + 1].at[4 * wi + k], device_id=peer, device_id_type=MESH).start()

    sem_shapes = [(4 * len(g),) for g in groups for _ in range(2)]
    return _split_start("gather_start", lands, sem_shapes, issue)


def _gather_forward(name, lands, by_cols):
    nw = len(lands)

    def issue(land, sems):
        x, y, c = _position()
        for w in range(nw):
            for k, chip in enumerate([(1 - x, y), (x, 1 - y), (1 - x, 1 - y)]):
                blk = _shard_of(land[w], _block_of((*chip, c)), by_cols[w])
                pltpu.make_async_remote_copy(
                    src_ref=blk, dst_ref=blk, send_sem=sems[0].at[3 * w + k], recv_sem=sems[1].at[3 * w + k],
                    device_id=(x, y, 1 - c), device_id_type=MESH).start()

    return _split_start(name, lands, [(3 * nw,), (3 * nw,)], issue)


def _first_block(bufs, w, offset=0):
    return bufs[offset + w].at[0]


_PEER_FLIPS = ((0, 0, 1), (1, 0, 0), (1, 0, 1), (0, 1, 0), (0, 1, 1), (1, 1, 0), (1, 1, 1))


def _scatter_start(name, grads, by_cols):
    nw = len(grads)
    lands = []
    for g, cols in zip(grads, by_cols):
        shard = (g.shape[0], g.shape[1] // N_DEV) if cols else (g.shape[0] // N_DEV, g.shape[1])
        lands.append(lax.empty((N_DEV,) + shard, g.dtype))

    def issue(bufs, sems):
        x, y, c = _position()
        flip = lambda v, f: 1 - v if f else v
        for w in range(nw):
            for k, (fx, fy, fc) in enumerate(_PEER_FLIPS):
                peer = (flip(x, fx), flip(y, fy), flip(c, fc))
                pltpu.make_async_remote_copy(
                    src_ref=_shard_of(bufs[w], _block_of(peer), by_cols[w]), dst_ref=bufs[nw + w].at[_block_of((x, y, c))],
                    send_sem=sems[0].at[7 * w + k], recv_sem=sems[1].at[7 * w + k],
                    device_id=peer, device_id_type=MESH).start()

    return _split_start(name, list(grads) + lands, [(7 * nw,), (7 * nw,)], issue)


def _adam(w, g, m, v):
    m = ADAM_B1 * m + (1.0 - ADAM_B1) * g
    v = ADAM_B2 * v + (1.0 - ADAM_B2) * (g * g)
    m_hat = m / (1.0 - ADAM_B1 ** ADAM_STEP)
    v_hat = v / (1.0 - ADAM_B2 ** ADAM_STEP)
    delta = -ADAM_LR * (m_hat / (jnp.sqrt(v_hat) + ADAM_EPS) + ADAM_WD * w)
    return delta, m, v


def _sum_adam(name, landed, grad, by_cols, w, m, v, tr=256):
    R, C = w.shape
    tr = _tile(R, tr)
    mine = _block_of(_position()).astype(jnp.int32).reshape(1)

    def body(me_ref, l_ref, own_ref, w_ref, m_ref, v_ref, g_ref, d_ref, nm_ref, nv_ref):
        own = own_ref[...].astype(F32)
        g = None
        for d in range(N_DEV):
            part = jnp.where(me_ref[0] == d, own, l_ref[d].astype(F32))
            g = part if g is None else g + part
        g_ref[...] = g
        d_ref[...], nm_ref[...], nv_ref[...] = _adam(w_ref[...], g, m_ref[...], v_ref[...])

    tile = pl.BlockSpec((tr, C), lambda i, me_ref: (i, 0))
    if by_cols:
        own = pl.BlockSpec((tr, C), lambda i, me_ref: (i, me_ref[0]))
    else:
        own = pl.BlockSpec((tr, C), lambda i, me_ref: (me_ref[0] * (R // tr) + i, 0))
    return pl.pallas_call(
        body, name=name,
        grid_spec=pltpu.PrefetchScalarGridSpec(
            num_scalar_prefetch=1, grid=(R // tr,),
            in_specs=[pl.BlockSpec((N_DEV, tr, C), lambda i, me_ref: (0, i, 0)), own, tile, tile, tile],
            out_specs=[tile] * 4),
        out_shape=[jax.ShapeDtypeStruct((R, C), F32)] * 4,
        compiler_params=_params(("parallel",)),
    )(mine, landed, grad, w, m, v)


def _small_all_reduce(parts):
    W = parts.shape[1]

    def body(p_ref, o_ref, slots, send_sems, recv_sems):
        x, y, c = _position()
        me = 4 * x + 2 * y + c
        slots[me] = jnp.sum(p_ref[...], axis=0, keepdims=True)
        peers = [(x, y, 1 - c), (1 - x, y, c), (1 - x, y, 1 - c), (x, 1 - y, c), (x, 1 - y, 1 - c),
                 (1 - x, 1 - y, c), (1 - x, 1 - y, 1 - c)]
        copies = []
        for k, peer in enumerate(peers):
            cp = pltpu.make_async_remote_copy(
                src_ref=slots.at[me], dst_ref=slots.at[me], send_sem=send_sems.at[k], recv_sem=recv_sems.at[k],
                device_id=peer, device_id_type=MESH)
            cp.start()
            copies.append(cp)
        for cp in copies:
            cp.wait()
        total = slots[0]
        for d in range(1, N_DEV):
            total = total + slots[d]
        o_ref[...] = total

    return pl.pallas_call(
        body, name="small_all_reduce",
        in_specs=[pl.BlockSpec(memory_space=pltpu.VMEM)], out_specs=pl.BlockSpec(memory_space=pltpu.VMEM),
        out_shape=jax.ShapeDtypeStruct((1, W), F32),
        scratch_shapes=[pltpu.VMEM((N_DEV, 1, W), F32), pltpu.SemaphoreType.DMA((7,)), pltpu.SemaphoreType.DMA((7,))],
    )(parts)


def _adam_small(w, g, m, v):
    def body(w_ref, g_ref, m_ref, v_ref, d_ref, nm_ref, nv_ref):
        d_ref[...], nm_ref[...], nv_ref[...] = _adam(w_ref[...], g_ref[...], m_ref[...], v_ref[...])

    return pl.pallas_call(
        body, name="adam_small",
        in_specs=[pl.BlockSpec(memory_space=pltpu.VMEM)] * 4, out_specs=[pl.BlockSpec(memory_space=pltpu.VMEM)] * 3,
        out_shape=[jax.ShapeDtypeStruct(w.shape, F32)] * 3,
    )(w, g, m, v)


_GATHER_GROUPS = (("w_in",), ("w_out", "w_up", "ple_w"), ("w_down", "w_gate"))
_COL_SHARDED = ("w_in", "w_up", "ple_w")


class _MeshComm:
    def __init__(self, w, mom, var):
        self.w, self.mom, self.var = w, mom, var
        self.out = {}
        self._scatters = {}

    def gather_begin(self):
        names = [n for g in _GATHER_GROUPS for n in g]
        self._idx = {n: i for i, n in enumerate(names)}
        groups = [[self._idx[n] for n in g] for g in _GATHER_GROUPS]
        self._sems, self._lands, token = _gather_start(
            [self.w[n] for n in names], [n in _COL_SHARDED for n in names], groups)
        return token

    @staticmethod
    def _shard_size(names, offset):
        return lambda bufs, w: _shard_of(bufs[offset + w], 0, names[w] in _COL_SHARDED)

    def gather_arrive(self, gi, after):
        names = _GATHER_GROUPS[gi]
        ids = [self._idx[n] for n in names]
        self._arrived = _split_wait("gather_arrive%d" % gi, [self._lands[i] for i in ids], self._sems[2 * gi],
                                    self._sems[2 * gi + 1], [4] * len(ids), self._shard_size(names, 0), after)

    def gather_forward(self, gi):
        by_cols = [n in _COL_SHARDED for n in _GATHER_GROUPS[gi]]
        self._fsems, self._fthru, token = _gather_forward("gather_forward%d" % gi, self._arrived, by_cols)
        return token

    def gather_finish(self, gi, after):
        names = _GATHER_GROUPS[gi]
        out = _split_wait("gather_finish%d" % gi, self._fthru, self._fsems[0], self._fsems[1], [3] * len(names),
                          self._shard_size(names, 0), after)
        return dict(zip(names, out))

    def reduce_begin(self, key, grads):
        names = list(grads)
        sems, thru, token = _scatter_start("scatter_start_" + key, [grads[n] for n in names],
                                           [n in _COL_SHARDED for n in names])
        self._scatters[key] = (names, sems, thru)
        return token

    def reduce_finish(self, key, after):
        names, sems, thru = self._scatters[key]
        nw = len(names)
        out = _split_wait("scatter_wait_" + key, thru, sems[0], sems[1], [N_DEV - 1] * nw,
                          functools.partial(_first_block, offset=nw), after)
        for i, n in enumerate(names):
            self.out[n] = _sum_adam("adam_" + n, out[nw + i], out[i], n in _COL_SHARDED, self.w[n], self.mom[n],
                                    self.var[n])


def _step(x, p, target, gains, comm):
    T, D = x.shape
    n_q = D // (2 * HEAD_DIM)
    n_kv = n_q // GROUP
    cos, sin = _rope_tables(T)
    idx = _bucket_index()

    t = comm.gather_begin()
    u = _rms_fwd("norm_attn", x, gains["attn_norm_g"], deps=(t,))
    comm.gather_arrive(0, u)
    t = comm.gather_forward(0)
    bias = _bias_build(idx, gains["rel_bias_table"].reshape(-1), n_q, deps=(t,))
    full = comm.gather_finish(0, bias)
    proj = _mm_nn("in_proj", u, full["w_in"])
    pb = _qk_prep(proj, cos, sin, gains["q_norm_g"], gains["k_norm_g"], n_q + n_kv)
    o_a, lse_a = _attn_a_fwd(pb, n_q, n_kv)
    comm.gather_arrive(1, o_a)
    t = comm.gather_forward(1)
    sink = gains["sink_logits"].reshape(-1)
    b_off = n_q + 2 * n_kv
    o_b, lse_b = _attn_b_fwd(pb, bias, sink, b_off, n_q, n_kv, deps=(t,))
    full.update(comm.gather_finish(1, o_b))
    o_cat = jnp.concatenate([o_a, o_b], axis=1)
    h1 = _mm_nn("out_proj", o_cat, full["w_out"], epilogue=_store_add, extras=(x,))
    m_in = _rms_fwd("norm_mlp", h1, gains["mlp_norm_g"])

    def up_epilogue(acc, extra, outs):
        outs[0][...] = acc.astype(BF16)
        r = jnp.maximum(acc, 0.0)
        outs[1][...] = (r * r).astype(BF16)

    a_act, f_act = _mm_nn("up_proj", m_in, full["w_up"], epilogue=up_epilogue, out_dtypes=[BF16, BF16])
    comm.gather_arrive(2, f_act)
    t = comm.gather_forward(2)
    p_b = p.astype(BF16)
    pe = _mm_nn("ple_proj", p_b, full["ple_w"], deps=(t,))
    full.update(comm.gather_finish(2, pe))
    h2 = _mm_nn("down_proj", f_act, full["w_down"], epilogue=_store_add, extras=(h1,), tm=512, tn=512)
    gn = _rms_fwd("norm_gate", h2, gains["gate_norm_g"])
    z = _mm_nn("gate_proj", gn, full["w_gate"])

    dh3, dz, dpe, dg_final, dg_ple, loss_part = _tail(h2, z, pe, target, gains["ple_norm_g"], gains["final_norm_g"])
    gw_gate = _mm_tn("grad_w_gate", gn, dz)
    gw_ple = _mm_tn("grad_ple_w", p_b, dpe)
    t = comm.reduce_begin("a", dict(w_gate=gw_gate, ple_w=gw_ple))
    dh2, dh2_b, dg_gate = _mm_nt_rms_bwd("d_gate_in", dz, full["w_gate"], h2, gains["gate_norm_g"], dh3, deps=(t,))
    gw_down = _mm_tn("grad_w_down", f_act, dh2_b)
    t = comm.reduce_begin("b", dict(w_down=gw_down))

    def act_bwd(acc, extra, outs):
        outs[0][...] = (acc * (2.0 * jnp.maximum(extra[0][...].astype(F32), 0.0))).astype(BF16)

    da = _mm_nt("d_act", dh2_b, full["w_down"], out_dtype=BF16, epilogue=act_bwd, extras=(a_act,), deps=(t,))
    comm.reduce_finish("a", da)
    gw_up = _mm_tn("grad_w_up", m_in, da)
    t = comm.reduce_begin("c", dict(w_up=gw_up))
    dm = _mm_nt("d_mlp_in", da, full["w_up"], tm=512, tn=512, deps=(t,))
    dh1, dh1_b, dg_mlp = _rms_bwd("norm_mlp_bwd", dm, h1, gains["mlp_norm_g"], dh2)
    comm.reduce_finish("b", dh1_b)
    gw_out = _mm_tn("grad_w_out", o_cat, dh1_b)
    t = comm.reduce_begin("d", dict(w_out=gw_out))
    d_o = _mm_nt("d_attn_out", dh1_b, full["w_out"], out_dtype=BF16, deps=(t,))
    dqa, dka, dva = _attn_a_bwd(pb, o_cat, d_o, lse_a, n_q, n_kv)
    dqb, dkb, dvb, dbias, dsink_raw = _attn_b_bwd(pb, o_cat, d_o, lse_b, bias, sink, b_off, n_q, n_kv, n_q)
    comm.reduce_finish("c", dqb)
    comm.reduce_finish("d", dqb)
    dtable, dsink = _table_grads(dbias, dsink_raw, idx)
    dproj, dg_q, dg_k = _dproj(proj, dqa, dka, dva, dqb, dkb, dvb, cos, sin, gains["q_norm_g"], gains["k_norm_g"])
    gw_in = _mm_tn("grad_w_in", u, dproj)
    t = comm.reduce_begin("e", dict(w_in=gw_in))
    dx, _, dg_attn = _mm_nt_rms_bwd("d_attn_in", dproj, full["w_in"], x, gains["attn_norm_g"], dh1, deps=(t,))
    comm.reduce_finish("e", dx)

    parts = jnp.concatenate([dg_attn, dg_mlp, dg_ple, dg_gate, dg_final, dg_q, dg_k, dtable, dsink, loss_part], axis=1)
    return dx, parts


_SHARDED = ("w_in", "w_out", "w_up", "w_down", "ple_w", "w_gate")
_VECTORS = ("attn_norm_g", "mlp_norm_g", "ple_norm_g", "gate_norm_g", "final_norm_g")
_ORDER = ("attn_norm_g", "w_in", "q_norm_g", "k_norm_g", "sink_logits", "w_out", "mlp_norm_g", "w_up", "w_down",
          "ple_w", "ple_norm_g", "gate_norm_g", "w_gate", "rel_bias_table", "final_norm_g")


def _pack_small(vals, n_heads):
    lane_pad = lambda v: jnp.pad(v, ((0, 0), (0, LANES - v.shape[1])))
    table = lane_pad(vals["rel_bias_table"].T).reshape(1, n_heads * LANES)
    return jnp.concatenate(
        [vals[n].reshape(1, -1) for n in _VECTORS] + [vals["q_norm_g"], vals["k_norm_g"], table,
                                                      lane_pad(vals["sink_logits"]), jnp.zeros((1, LANES), F32)], axis=1)


def _unpack_small(row, like, n_heads):
    out, off = {}, 0
    for n in _VECTORS:
        out[n] = row[:, off:off + like[n].size].reshape(like[n].shape)
        off += like[n].size
    for n in ("q_norm_g", "k_norm_g"):
        out[n] = row[:, off:off + LANES]
        off += LANES
    out["rel_bias_table"] = row[:, off:off + n_heads * LANES].reshape(n_heads, LANES)[:, :N_BUCKETS].T
    off += n_heads * LANES
    out["sink_logits"] = row[:, off:off + n_heads]
    off += LANES
    return out, row[0, off]


def kernel(x, p, attn_norm_g, w_in, q_norm_g, k_norm_g, sink_logits, w_out, mlp_norm_g, w_up, w_down, ple_w, ple_norm_g, gate_norm_g, w_gate, rel_bias_table, final_norm_g, loss_target, m_attn_norm_g, m_w_in, m_q_norm_g, m_k_norm_g, m_sink_logits, m_w_out, m_mlp_norm_g, m_w_up, m_w_down, m_ple_w, m_ple_norm_g, m_gate_norm_g, m_w_gate, m_rel_bias_table, m_final_norm_g, v_attn_norm_g, v_w_in, v_q_norm_g, v_k_norm_g, v_sink_logits, v_w_out, v_mlp_norm_g, v_w_up, v_w_down, v_ple_w, v_ple_norm_g, v_gate_norm_g, v_w_gate, v_rel_bias_table, v_final_norm_g):
    w = dict(attn_norm_g=attn_norm_g, w_in=w_in[0], q_norm_g=q_norm_g, k_norm_g=k_norm_g, sink_logits=sink_logits,
             w_out=w_out[0], mlp_norm_g=mlp_norm_g, w_up=w_up[0], w_down=w_down[0], ple_w=ple_w[0],
             ple_norm_g=ple_norm_g, gate_norm_g=gate_norm_g, w_gate=w_gate[0], rel_bias_table=rel_bias_table,
             final_norm_g=final_norm_g)
    mom = dict(attn_norm_g=m_attn_norm_g, w_in=m_w_in[0], q_norm_g=m_q_norm_g, k_norm_g=m_k_norm_g,
               sink_logits=m_sink_logits, w_out=m_w_out[0], mlp_norm_g=m_mlp_norm_g, w_up=m_w_up[0],
               w_down=m_w_down[0], ple_w=m_ple_w[0], ple_norm_g=m_ple_norm_g, gate_norm_g=m_gate_norm_g,
               w_gate=m_w_gate[0], rel_bias_table=m_rel_bias_table, final_norm_g=m_final_norm_g)
    var = dict(attn_norm_g=v_attn_norm_g, w_in=v_w_in[0], q_norm_g=v_q_norm_g, k_norm_g=v_k_norm_g,
               sink_logits=v_sink_logits, w_out=v_w_out[0], mlp_norm_g=v_mlp_norm_g, w_up=v_w_up[0],
               w_down=v_w_down[0], ple_w=v_ple_w[0], ple_norm_g=v_ple_norm_g, gate_norm_g=v_gate_norm_g,
               w_gate=v_w_gate[0], rel_bias_table=v_rel_bias_table, final_norm_g=v_final_norm_g)
    D = x.shape[-1]
    n_heads = D // (2 * HEAD_DIM)

    gains = {n: w[n] for n in w if n not in _SHARDED}
    gains["final_norm_g"] = final_norm_g.reshape(1, -1)

    comm = _MeshComm(w, mom, var)
    dx, parts = _step(x[0], p[0, 0], loss_target[0], gains, comm)

    g_out, d_out, m_out, v_out = {}, {}, {}, {}
    for n in _SHARDED:
        g, d, nm, nv = comm.out[n]
        g_out[n], d_out[n], m_out[n], v_out[n] = g[None], d[None], nm[None], nv[None]

    small_g = _small_all_reduce(parts)
    small = {n: v for n, v in w.items() if n not in _SHARDED}
    pack = lambda vals: _pack_small({n: vals[n] for n in small}, n_heads)
    sd, sm, sv = _adam_small(pack(w), small_g, pack(mom), pack(var))
    sg, loss = _unpack_small(small_g, small, n_heads)
    g_out.update(sg)
    for dst, row in ((d_out, sd), (m_out, sm), (v_out, sv)):
        dst.update(_unpack_small(row, small, n_heads)[0])

    return (loss, dx[None], *[g_out[n] for n in _ORDER], *[d_out[n] for n in _ORDER],
            *[m_out[n] for n in _ORDER], *[v_out[n] for n in _ORDER])
```

```python
import functools
import math

import numpy as np
import jax
import jax.numpy as jnp
from jax import lax
from jax.experimental import pallas as pl
from jax.experimental.pallas import tpu as pltpu

F32 = jnp.float32
BF16 = jnp.bfloat16

N_DEV = 8
N_CHIP = 4
HEAD_DIM = 128
GROUP = 4
GRID_W = 64
WINDOW = 128
BLOCK_Q = 128
N_BUCKETS = 32
MAX_DISTANCE = 128
ROPE_THETA = 10000.0
EPS = 1e-6
NEG_INF = -1e30
ADAM_LR = 0.001
ADAM_B1 = 0.9
ADAM_B2 = 0.999
ADAM_EPS = 1e-08
ADAM_WD = 0.01
ADAM_STEP = 10
LOG2E = math.log2(math.e)
LANES = 128
SUBLANES = 8
MESH = pl.DeviceIdType.MESH

_NT = (((1,), (1,)), ((), ()))
_NN = (((1,), (0,)), ((), ()))
_TN = (((0,), (0,)), ((), ()))


def _tile(dim, pref):
    return pref if dim % pref == 0 else dim


def _params(sem):
    return pltpu.CompilerParams(dimension_semantics=sem, vmem_limit_bytes=56 * 1024 * 1024)


_HBM = pl.BlockSpec(memory_space=pltpu.HBM)
_SEM = pl.BlockSpec(memory_space=pltpu.SEMAPHORE)
_ANY = pl.BlockSpec(memory_space=pl.ANY)
_VMEM = pl.BlockSpec(memory_space=pltpu.VMEM)
_EFFECT = pltpu.SideEffectType.DATAFLOW_SIDE_EFFECTING


def _pcall(body, deps=(), *, in_specs, **kw):
    deps = [d for d in deps if d is not None]
    nd = len(deps)

    def wrapped(*refs):
        body(*refs[nd:])

    call = pl.pallas_call(wrapped, in_specs=[_ANY] * nd + list(in_specs), **kw)
    return lambda *args: call(*deps, *args)


def _mm(name, a, b, dims, grid, a_spec, b_spec, out_shape, out_specs, acc_shape, epilogue,
        extras=(), extra_specs=(), deps=(), semantics=("parallel", "parallel", "arbitrary")):
    nk = grid[2]
    n_extra = len(extras)

    def body(*refs):
        a_ref, b_ref = refs[0], refs[1]
        extra = refs[2:2 + n_extra]
        outs = refs[2 + n_extra:-1]
        acc = refs[-1]
        part = lax.dot_general(a_ref[...], b_ref[...], dims, preferred_element_type=F32)
        if nk == 1:
            epilogue(part, extra, outs)
        else:
            k = pl.program_id(2)

            @pl.when(k == 0)
            def _():
                acc[...] = part

            @pl.when(k > 0)
            def _():
                acc[...] += part

            @pl.when(k == nk - 1)
            def _():
                epilogue(acc[...], extra, outs)

    return _pcall(
        body, deps, name=name, grid=grid,
        in_specs=[a_spec, b_spec, *extra_specs],
        out_specs=out_specs, out_shape=out_shape,
        scratch_shapes=[pltpu.VMEM(acc_shape if nk > 1 else (SUBLANES, LANES), F32)],
        compiler_params=_params(semantics),
    )(a, b, *extras)


def _store(dtype):
    def ep(acc, extra, outs):
        outs[0][...] = acc.astype(dtype)
    return ep


def _store_add(acc, extra, outs):
    outs[0][...] = acc + extra[0][...]


def _mm_nn(name, a, b, out_dtype=F32, epilogue=None, extras=(), n_out=1, out_dtypes=None, tm=1024, tn=1024, tk=None,
           deps=()):
    M, K = a.shape
    N = b.shape[1]
    tm, tn, tk = _tile(M, tm), _tile(N, tn), _tile(K, tk or K)
    b_spec = pl.BlockSpec((tk, tn), lambda i, j, k: (k, j))
    grid = (M // tm, N // tn, K // tk)
    o_spec = pl.BlockSpec((tm, tn), lambda i, j, k: (i, j))
    out_dtypes = out_dtypes or [out_dtype] * n_out
    out_shape = [jax.ShapeDtypeStruct((M, N), d) for d in out_dtypes]
    res = _mm(name, a, b, _NN, grid, pl.BlockSpec((tm, tk), lambda i, j, k: (i, k)), b_spec,
              out_shape, [o_spec] * len(out_dtypes), (tm, tn), epilogue or _store(out_dtype),
              extras, [o_spec] * len(extras), deps)
    return res if len(out_dtypes) > 1 else res[0]


def _mm_nt(name, a, b, out_dtype=F32, epilogue=None, extras=(), tm=1024, tn=1024, tk=None, deps=()):
    M, C = a.shape
    N = b.shape[0]
    tm, tn, tk = _tile(M, tm), _tile(N, tn), _tile(C, tk or C)
    b_spec = pl.BlockSpec((tn, tk), lambda i, j, k: (j, k))
    grid = (M // tm, N // tn, C // tk)
    o_spec = pl.BlockSpec((tm, tn), lambda i, j, k: (i, j))
    return _mm(name, a, b, _NT, grid, pl.BlockSpec((tm, tk), lambda i, j, k: (i, k)), b_spec,
               [jax.ShapeDtypeStruct((M, N), out_dtype)], [o_spec], (tm, tn), epilogue or _store(out_dtype),
               extras, [o_spec] * len(extras), deps)[0]


def _mm_tn(name, a, b, out_dtype=BF16, tm=1024, tn=512, tk=None, deps=()):
    T, M = a.shape
    N = b.shape[1]
    tm, tn, tk = _tile(M, tm), _tile(N, tn), _tile(T, tk or T)
    out_shape = jax.ShapeDtypeStruct((M, N), out_dtype)
    o_spec = pl.BlockSpec((tm, tn), lambda i, j, k: (i, j))
    grid = (M // tm, N // tn, T // tk)
    return _mm(name, a, b, _TN, grid, pl.BlockSpec((tk, tm), lambda i, j, k: (k, i)),
               pl.BlockSpec((tk, tn), lambda i, j, k: (k, j)), [out_shape], [o_spec], (tm, tn), _store(out_dtype),
               deps=deps)[0]


def _mean_last(v):
    return jnp.mean(v, axis=-1, keepdims=True)


def _rows_to_sublanes(v):
    r, c = v.shape
    return jnp.sum(v.reshape(r // SUBLANES, SUBLANES, c), axis=0)


def _accumulate(ref, val, first):
    @pl.when(first)
    def _():
        ref[...] = val

    @pl.when(jnp.logical_not(first))
    def _():
        ref[...] += val


def _rms_fwd(name, x, g, tr=256, deps=()):
    T, D = x.shape
    tr = _tile(T, tr)

    def body(x_ref, g_ref, o_ref):
        xv = x_ref[...]
        r = lax.rsqrt(_mean_last(xv * xv) + EPS)
        o_ref[...] = (xv * r * g_ref[...]).astype(BF16)

    row = pl.BlockSpec((tr, D), lambda i: (i, 0))
    return _pcall(
        body, deps, name=name, grid=(T // tr,),
        in_specs=[row, pl.BlockSpec((1, D), lambda i: (0, 0))],
        out_specs=row, out_shape=jax.ShapeDtypeStruct((T, D), BF16),
        compiler_params=_params(("parallel",)),
    )(x, g)


def _rms_bwd(name, dyn, x, g, dres, tr=256, deps=()):
    T, D = x.shape
    tr = _tile(T, tr)

    def body(dy_ref, x_ref, g_ref, dr_ref, dx_ref, dxb_ref, dg_ref):
        xv = x_ref[...]
        r = lax.rsqrt(_mean_last(xv * xv) + EPS)
        xn = xv * r
        dy = dy_ref[...]
        dxn = dy * g_ref[...]
        dx = dr_ref[...] + r * (dxn - xn * _mean_last(dxn * xn))
        dx_ref[...] = dx
        dxb_ref[...] = dx.astype(BF16)
        _accumulate(dg_ref, _rows_to_sublanes(dy * xn), pl.program_id(0) == 0)

    row = pl.BlockSpec((tr, D), lambda i: (i, 0))
    return _pcall(
        body, deps, name=name, grid=(T // tr,),
        in_specs=[row, row, pl.BlockSpec((1, D), lambda i: (0, 0)), row],
        out_specs=[row, row, pl.BlockSpec((SUBLANES, D), lambda i: (0, 0))],
        out_shape=[jax.ShapeDtypeStruct((T, D), F32), jax.ShapeDtypeStruct((T, D), BF16),
                   jax.ShapeDtypeStruct((SUBLANES, D), F32)],
        compiler_params=_params(("arbitrary",)),
    )(dyn, x, g, dres)


def _mm_nn_rms(name, a, b, res, g, tm=512, deps=()):
    M, K = a.shape
    N = b.shape[1]
    tm = _tile(M, tm)

    def epilogue(acc, extra, outs):
        h = acc + extra[0][...]
        outs[0][...] = h
        outs[1][...] = (h * lax.rsqrt(_mean_last(h * h) + EPS) * extra[1][...]).astype(BF16)

    row = pl.BlockSpec((tm, N), lambda i, j, k: (i, 0))
    return _mm(name, a, b, _NN, (M // tm, 1, 1), pl.BlockSpec((tm, K), lambda i, j, k: (i, 0)),
               pl.BlockSpec((K, N), lambda i, j, k: (0, 0)),
               [jax.ShapeDtypeStruct((M, N), F32), jax.ShapeDtypeStruct((M, N), BF16)], [row, row], (tm, N), epilogue,
               (res, g), [row, pl.BlockSpec((1, N), lambda i, j, k: (0, 0))], deps)


def _mm_nt_rms_bwd(name, a, b, x, g, dres, tm=256, deps=()):
    M, C = a.shape
    N = b.shape[0]
    tm = _tile(M, tm)

    def epilogue(dy, extra, outs):
        x_ref, dr_ref, g_ref = extra
        xv = x_ref[...]
        r = lax.rsqrt(_mean_last(xv * xv) + EPS)
        xn = xv * r
        dxn = dy * g_ref[...]
        dx = dr_ref[...] + r * (dxn - xn * _mean_last(dxn * xn))
        outs[0][...] = dx
        outs[1][...] = dx.astype(BF16)
        _accumulate(outs[2], _rows_to_sublanes(dy * xn), pl.program_id(0) == 0)

    row = pl.BlockSpec((tm, N), lambda i, j, k: (i, 0))
    return _mm(name, a, b, _NT, (M // tm, 1, 1), pl.BlockSpec((tm, C), lambda i, j, k: (i, 0)),
               pl.BlockSpec((N, C), lambda i, j, k: (0, 0)),
               [jax.ShapeDtypeStruct((M, N), F32), jax.ShapeDtypeStruct((M, N), BF16),
                jax.ShapeDtypeStruct((SUBLANES, N), F32)],
               [row, row, pl.BlockSpec((SUBLANES, N), lambda i, j, k: (0, 0))], (tm, N), epilogue,
               (x, dres, g), [row, row, pl.BlockSpec((1, N), lambda i, j, k: (0, 0))], deps,
               semantics=("arbitrary", "arbitrary", "arbitrary"))


def _tail(h2, z, pe, target, g_ple, g_final, tr=256):
    T, D = h2.shape
    tr = _tile(T, tr)

    def body(h2_ref, z_ref, pe_ref, t_ref, gp_ref, gf_ref,
             dh3_ref, dz_ref, dpe_ref, dgf_ref, dgp_ref, loss_ref):
        first = pl.program_id(0) == 0
        pev = pe_ref[...]
        r3 = lax.rsqrt(_mean_last(pev * pev) + EPS)
        en = pev * r3
        e = en * gp_ref[...]
        gate = 1.0 / (1.0 + jnp.exp(-z_ref[...]))
        h3 = h2_ref[...] + gate * e
        r5 = lax.rsqrt(_mean_last(h3 * h3) + EPS)
        hn = h3 * r5
        diff = hn * gf_ref[...] - t_ref[...]
        loss_rows = 0.5 * _mean_last(diff * diff)
        row0 = lax.broadcasted_iota(jnp.int32, (SUBLANES, LANES), 0) == 0
        _accumulate(loss_ref, jnp.where(row0, jnp.sum(loss_rows), 0.0), first)
        dy = diff * (1.0 / D)
        _accumulate(dgf_ref, _rows_to_sublanes(dy * hn), first)
        dhn = dy * gf_ref[...]
        dh3 = r5 * (dhn - hn * _mean_last(dhn * hn))
        dh3_ref[...] = dh3
        dgate = dh3 * e
        de = dh3 * gate
        dz_ref[...] = (dgate * gate * (1.0 - gate)).astype(BF16)
        _accumulate(dgp_ref, _rows_to_sublanes(de * en), first)
        den = de * gp_ref[...]
        dpe_ref[...] = (r3 * (den - en * _mean_last(den * en))).astype(BF16)

    row = pl.BlockSpec((tr, D), lambda i: (i, 0))
    vec = pl.BlockSpec((1, D), lambda i: (0, 0))
    part = pl.BlockSpec((SUBLANES, D), lambda i: (0, 0))
    return pl.pallas_call(
        body, name="tail", grid=(T // tr,),
        in_specs=[row, row, row, row, vec, vec],
        out_specs=[row, row, row, part, part, pl.BlockSpec((SUBLANES, LANES), lambda i: (0, 0))],
        out_shape=[jax.ShapeDtypeStruct((T, D), F32), jax.ShapeDtypeStruct((T, D), BF16),
                   jax.ShapeDtypeStruct((T, D), BF16), jax.ShapeDtypeStruct((SUBLANES, D), F32),
                   jax.ShapeDtypeStruct((SUBLANES, D), F32), jax.ShapeDtypeStruct((SUBLANES, LANES), F32)],
        compiler_params=_params(("arbitrary",)),
    )(h2, z, pe, target, g_ple, g_final)


def _rope_tables(T):
    pos = np.arange(T)
    half = HEAD_DIM // 2
    inv = (ROPE_THETA ** (-np.arange(0, half, 2, dtype=np.float32) / half)).astype(np.float32)
    ang_r = (pos // GRID_W).astype(np.float32)[:, None] * inv
    ang_c = (pos % GRID_W).astype(np.float32)[:, None] * inv
    cos = np.concatenate([np.cos(ang_r), np.cos(ang_r), np.cos(ang_c), np.cos(ang_c)], axis=-1)
    sin = np.concatenate([-np.sin(ang_r), np.sin(ang_r), -np.sin(ang_c), np.sin(ang_c)], axis=-1)
    return jnp.asarray(cos, F32), jnp.asarray(sin, F32)


def _swap32(x):
    lane = lax.broadcasted_iota(jnp.int32, x.shape, 1)
    return jnp.where((lane % 64) < 32, pltpu.roll(x, 96, 1), pltpu.roll(x, 32, 1))


def _qk_prep(proj, cos, sin, g_q, g_k, n_norm, tr=256):
    T, W = proj.shape
    tr = _tile(T, tr)
    n_q = n_norm * GROUP // (GROUP + 1)

    def body(p_ref, c_ref, s_ref, gq_ref, gk_ref, o_ref):
        c, s = c_ref[...], s_ref[...]
        for h in range(n_norm):
            cols = slice(h * HEAD_DIM, (h + 1) * HEAD_DIM)
            xv = p_ref[:, cols]
            g = gq_ref[...] if h < n_q else gk_ref[...]
            xn = xv * lax.rsqrt(_mean_last(xv * xv) + EPS) * g
            o_ref[:, cols] = (xn * c + _swap32(xn) * s).astype(BF16)
        rest = slice(n_norm * HEAD_DIM, W)
        o_ref[:, rest] = p_ref[:, rest].astype(BF16)

    row = pl.BlockSpec((tr, W), lambda i: (i, 0))
    tab = pl.BlockSpec((tr, HEAD_DIM), lambda i: (i, 0))
    vec = pl.BlockSpec((1, HEAD_DIM), lambda i: (0, 0))
    return pl.pallas_call(
        body, name="qk_prep", grid=(T // tr,),
        in_specs=[row, tab, tab, vec, vec], out_specs=row,
        out_shape=jax.ShapeDtypeStruct((T, W), BF16),
        compiler_params=_params(("parallel",)),
    )(proj, cos, sin, g_q, g_k)


def _dproj(proj, dqa, dka, dva, dqb, dkb, dvb, cos, sin, g_q, g_k, tr=256):
    T, W = proj.shape
    tr = _tile(T, tr)
    n_q = dqa.shape[1] // HEAD_DIM
    n_kv = dka.shape[1] // HEAD_DIM
    wa = (n_q + n_kv) * HEAD_DIM

    def body(p_ref, dqa_ref, dka_ref, dva_ref, dqb_ref, dkb_ref, dvb_ref, c_ref, s_ref, gq_ref, gk_ref,
             o_ref, dgq_ref, dgk_ref):
        c, s = c_ref[...], s_ref[...]
        dgq = jnp.zeros((SUBLANES, HEAD_DIM), F32)
        dgk = jnp.zeros((SUBLANES, HEAD_DIM), F32)
        for h in range(n_q + n_kv):
            cols = slice(h * HEAD_DIM, (h + 1) * HEAD_DIM)
            xv = p_ref[:, cols]
            r = lax.rsqrt(_mean_last(xv * xv) + EPS)
            xn = xv * r
            if h < n_q:
                d = dqa_ref[:, cols]
                g = gq_ref[...]
            else:
                d = dka_ref[:, (h - n_q) * HEAD_DIM:(h - n_q + 1) * HEAD_DIM]
                g = gk_ref[...]
            dqn = d * c + _swap32(d * s)
            part = _rows_to_sublanes(dqn * xn)
            if h < n_q:
                dgq = dgq + part
            else:
                dgk = dgk + part
            dxn = dqn * g
            o_ref[:, cols] = (r * (dxn - xn * _mean_last(dxn * xn))).astype(BF16)
        off = wa
        for ref in (dva_ref, dqb_ref, dkb_ref, dvb_ref):
            w = ref.shape[1]
            o_ref[:, off:off + w] = ref[...].astype(BF16)
            off += w
        first = pl.program_id(0) == 0
        _accumulate(dgq_ref, dgq, first)
        _accumulate(dgk_ref, dgk, first)

    def row(w):
        return pl.BlockSpec((tr, w), lambda i: (i, 0))

    vec = pl.BlockSpec((1, HEAD_DIM), lambda i: (0, 0))
    part = pl.BlockSpec((SUBLANES, HEAD_DIM), lambda i: (0, 0))
    return pl.pallas_call(
        body, name="dproj", grid=(T // tr,),
        in_specs=[row(wa), row(dqa.shape[1]), row(dka.shape[1]), row(dva.shape[1]), row(dqb.shape[1]),
                  row(dkb.shape[1]), row(dvb.shape[1]), row(HEAD_DIM), row(HEAD_DIM), vec, vec],
        out_specs=[row(W), part, part],
        out_shape=[jax.ShapeDtypeStruct((T, W), BF16), jax.ShapeDtypeStruct((SUBLANES, HEAD_DIM), F32),
                   jax.ShapeDtypeStruct((SUBLANES, HEAD_DIM), F32)],
        compiler_params=_params(("arbitrary",)),
    )(proj, dqa, dka, dva, dqb, dkb, dvb, cos, sin, g_q, g_k)


def _attn_a_fwd(pb, n_q, n_kv, tq=1024, tc=1024):
    T = pb.shape[0]
    tq, tc = _tile(T, tq), _tile(T, tc)
    scale = HEAD_DIM ** -0.5
    c = scale * LOG2E

    def body(q_ref, k_ref, v_ref, o_ref, lse_ref):
        q = q_ref[...]
        m = l = acc = None
        for j in range(T // tc):
            keys = slice(j * tc, (j + 1) * tc)
            s = lax.dot_general(q, k_ref[keys, :], _NT, preferred_element_type=F32)
            mj = jnp.max(s, axis=-1, keepdims=True)
            m_new = mj if j == 0 else jnp.maximum(m, mj)
            p = jnp.exp2((s - m_new) * c)
            pv = lax.dot_general(p.astype(BF16), v_ref[keys, :], _NN, preferred_element_type=F32)
            if j == 0:
                l, acc = jnp.sum(p, axis=-1, keepdims=True), pv
            else:
                alpha = jnp.exp2((m - m_new) * c)
                l = alpha * l + jnp.sum(p, axis=-1, keepdims=True)
                acc = alpha * acc + pv
            m = m_new
        o_ref[...] = (acc / l).astype(BF16)
        lse_ref[...] = m * scale + jnp.log(l)

    return pl.pallas_call(
        body, name="attn_a_fwd", grid=(n_kv, GROUP, T // tq),
        in_specs=[pl.BlockSpec((tq, HEAD_DIM), lambda kv, g, i: (i, kv * GROUP + g)),
                  pl.BlockSpec((T, HEAD_DIM), lambda kv, g, i: (0, n_q + kv)),
                  pl.BlockSpec((T, HEAD_DIM), lambda kv, g, i: (0, n_q + n_kv + kv))],
        out_specs=[pl.BlockSpec((tq, HEAD_DIM), lambda kv, g, i: (i, kv * GROUP + g)),
                   pl.BlockSpec((None, tq, 1), lambda kv, g, i: (kv * GROUP + g, i, 0))],
        out_shape=[jax.ShapeDtypeStruct((T, n_q * HEAD_DIM), BF16), jax.ShapeDtypeStruct((n_q, T, 1), F32)],
        compiler_params=_params(("parallel", "parallel", "parallel")),
    )(pb, pb, pb)


def _attn_a_bwd(pb, o_cat, d_o, lse, n_q, n_kv, tq=512, tc=512):
    T = pb.shape[0]
    tq, tc = _tile(T, tq), _tile(T, tc)
    scale = HEAD_DIM ** -0.5
    c = scale * LOG2E

    def body(q_ref, k_ref, v_ref, o_ref, do_ref, lse_ref, dq_ref, dk_ref, dv_ref):
        q, do = q_ref[...], do_ref[...]
        delta = jnp.sum(do.astype(F32) * o_ref[...].astype(F32), axis=-1, keepdims=True)
        lse2 = lse_ref[...] * LOG2E

        @pl.when(jnp.logical_and(pl.program_id(1) == 0, pl.program_id(2) == 0))
        def _():
            dk_ref[...] = jnp.zeros(dk_ref.shape, F32)
            dv_ref[...] = jnp.zeros(dv_ref.shape, F32)

        dq = None
        for j in range(T // tc):
            keys = slice(j * tc, (j + 1) * tc)
            kc, vc = k_ref[keys, :], v_ref[keys, :]
            s = lax.dot_general(q, kc, _NT, preferred_element_type=F32)
            p = jnp.exp2(s * c - lse2)
            dp = lax.dot_general(do, vc, _NT, preferred_element_type=F32)
            ds = (p * (dp - delta) * scale).astype(BF16)
            dqj = lax.dot_general(ds, kc, _NN, preferred_element_type=F32)
            dq = dqj if dq is None else dq + dqj
            dv_ref[keys, :] += lax.dot_general(p.astype(BF16), do, _TN, preferred_element_type=F32)
            dk_ref[keys, :] += lax.dot_general(ds, q, _TN, preferred_element_type=F32)
        dq_ref[...] = dq

    qmap = lambda kv, g, i: (i, kv * GROUP + g)
    return pl.pallas_call(
        body, name="attn_a_bwd", grid=(n_kv, GROUP, T // tq),
        in_specs=[pl.BlockSpec((tq, HEAD_DIM), qmap),
                  pl.BlockSpec((T, HEAD_DIM), lambda kv, g, i: (0, n_q + kv)),
                  pl.BlockSpec((T, HEAD_DIM), lambda kv, g, i: (0, n_q + n_kv + kv)),
                  pl.BlockSpec((tq, HEAD_DIM), qmap),
                  pl.BlockSpec((tq, HEAD_DIM), qmap),
                  pl.BlockSpec((None, tq, 1), lambda kv, g, i: (kv * GROUP + g, i, 0))],
        out_specs=[pl.BlockSpec((tq, HEAD_DIM), qmap),
                   pl.BlockSpec((T, HEAD_DIM), lambda kv, g, i: (0, kv)),
                   pl.BlockSpec((T, HEAD_DIM), lambda kv, g, i: (0, kv))],
        out_shape=[jax.ShapeDtypeStruct((T, n_q * HEAD_DIM), F32),
                   jax.ShapeDtypeStruct((T, n_kv * HEAD_DIM), F32),
                   jax.ShapeDtypeStruct((T, n_kv * HEAD_DIM), F32)],
        compiler_params=_params(("parallel", "arbitrary", "arbitrary")),
    )(pb, pb, pb, o_cat, d_o, lse)


def _bucket_index():
    r = np.arange(BLOCK_Q)[:, None]
    j = np.arange(3 * BLOCK_Q)[None, :]
    rel = (j - BLOCK_Q) - r
    nb = N_BUCKETS // 2
    ret = np.where(rel > 0, nb, 0)
    n = np.abs(rel)
    max_exact = nb // 2
    nf = np.maximum(n, 1).astype(np.float32)
    large = max_exact + (np.log(nf / max_exact) / math.log(MAX_DISTANCE / max_exact) * (nb - max_exact)).astype(np.int32)
    large = np.minimum(large, nb - 1)
    return jnp.asarray(ret + np.where(n < max_exact, n, large), jnp.int32)


def _bias_build(idx, table_flat, n_heads, deps=()):
    def body(idx_ref, tab_ref, o_ref):
        h = pl.program_id(0)
        iv = idx_ref[...]
        acc = jnp.zeros(iv.shape, F32)
        for b in range(N_BUCKETS):
            acc = jnp.where(iv == b, tab_ref[b * n_heads + h], acc)
        r = lax.broadcasted_iota(jnp.int32, iv.shape, 0)
        j = lax.broadcasted_iota(jnp.int32, iv.shape, 1)
        o_ref[...] = jnp.where(jnp.abs(j - BLOCK_Q - r) <= WINDOW, acc, NEG_INF)

    return _pcall(
        body, deps, name="bias_build", grid=(n_heads,),
        in_specs=[pl.BlockSpec(idx.shape, lambda h: (0, 0)), pl.BlockSpec(memory_space=pltpu.SMEM)],
        out_specs=pl.BlockSpec((None,) + idx.shape, lambda h: (h, 0, 0)),
        out_shape=jax.ShapeDtypeStruct((n_heads,) + idx.shape, F32),
        compiler_params=_params(("parallel",)),
    )(idx, table_flat)


def _in_sequence(n, T):
    j = lax.broadcasted_iota(jnp.int32, (BLOCK_Q, 3 * BLOCK_Q), 1)
    kabs = n * BLOCK_Q + j - BLOCK_Q
    return (kabs >= 0) & (kabs < T)


def _band_specs(col, nblk, sb):
    return [pl.BlockSpec((BLOCK_Q, HEAD_DIM), lambda kv, i: (jnp.maximum(sb * i - 1, 0), col(kv))),
            pl.BlockSpec((sb * BLOCK_Q, HEAD_DIM), lambda kv, i: (i, col(kv))),
            pl.BlockSpec((BLOCK_Q, HEAD_DIM), lambda kv, i: (jnp.minimum(sb * i + sb, nblk - 1), col(kv)))]


def _head_specs(base, rows):
    return [pl.BlockSpec((rows, HEAD_DIM), functools.partial(lambda kv, i, g: (i, base + kv * GROUP + g), g=g))
            for g in range(GROUP)]


def _attn_b_fwd(pb, bias, sink, q_off, n_q, n_kv, deps=(), sb=8):
    T = pb.shape[0]
    nblk = T // BLOCK_Q
    sb = min(sb, nblk)
    tq = sb * BLOCK_Q
    scale = HEAD_DIM ** -0.5

    def body(*refs):
        q_refs = refs[0:GROUP]
        k_refs, v_refs = refs[GROUP:GROUP + 3], refs[GROUP + 3:GROUP + 6]
        bias_ref, sink_ref, o_ref, lse_ref = refs[GROUP + 6:]
        kv, i = pl.program_id(0), pl.program_id(1)
        kb = jnp.concatenate([r[...] for r in k_refs], axis=0)
        vb = jnp.concatenate([r[...] for r in v_refs], axis=0)
        for b in range(sb):
            at_end = b == 0 or b == sb - 1
            mask = _in_sequence(i * sb + b, T) if at_end else None
            rows = slice(b * BLOCK_Q, (b + 1) * BLOCK_Q)
            kw, vw = kb[b * BLOCK_Q:(b + 3) * BLOCK_Q], vb[b * BLOCK_Q:(b + 3) * BLOCK_Q]
            for g in range(GROUP):
                sk = sink_ref[kv * GROUP + g]
                s = lax.dot_general(q_refs[g][rows, :], kw, _NT, preferred_element_type=F32) * scale + bias_ref[g]
                if at_end:
                    s = jnp.where(mask, s, NEG_INF)
                m = jnp.maximum(jnp.max(s, axis=-1, keepdims=True), sk)
                p = jnp.exp(s - m)
                l = jnp.sum(p, axis=-1, keepdims=True) + jnp.exp(sk - m)
                o = lax.dot_general(p.astype(BF16), vw, _NN, preferred_element_type=F32)
                o_ref[rows, g * HEAD_DIM:(g + 1) * HEAD_DIM] = (o / l).astype(BF16)
                lse_ref[g, rows, :] = m + jnp.log(l)

    return _pcall(
        body, deps, name="attn_b_fwd", grid=(n_kv, nblk // sb),
        in_specs=[*_head_specs(q_off, tq),
                  *_band_specs(lambda kv: q_off + n_q + kv, nblk, sb),
                  *_band_specs(lambda kv: q_off + n_q + n_kv + kv, nblk, sb),
                  pl.BlockSpec((GROUP, BLOCK_Q, 3 * BLOCK_Q), lambda kv, i: (kv, 0, 0)),
                  pl.BlockSpec(memory_space=pltpu.SMEM)],
        out_specs=[pl.BlockSpec((tq, GROUP * HEAD_DIM), lambda kv, i: (i, kv)),
                   pl.BlockSpec((GROUP, tq, 1), lambda kv, i: (kv, i, 0))],
        out_shape=[jax.ShapeDtypeStruct((T, n_q * HEAD_DIM), BF16), jax.ShapeDtypeStruct((n_q, T, 1), F32)],
        compiler_params=_params(("parallel", "parallel")),
    )(*([pb] * (GROUP + 6)), bias, sink)


def _attn_b_bwd(pb, o_cat, d_o, lse, bias, sink, q_off, n_q, n_kv, o_off, deps=(), sb=8):
    T = pb.shape[0]
    nblk = T // BLOCK_Q
    sb = min(sb, nblk)
    tq = sb * BLOCK_Q
    scale = HEAD_DIM ** -0.5

    def body(*refs):
        q_refs = refs[0:GROUP]
        k_refs, v_refs = refs[GROUP:GROUP + 3], refs[GROUP + 3:GROUP + 6]
        o_refs, do_refs = refs[GROUP + 6:2 * GROUP + 6], refs[2 * GROUP + 6:3 * GROUP + 6]
        lse_ref, bias_ref, sink_ref, dq_ref, dk_ref, dv_ref, dbias_ref, dsink_ref, dkb_ref, dvb_ref = refs[3 * GROUP + 6:]
        kv, i = pl.program_id(0), pl.program_id(1)
        first = i == 0

        @pl.when(first)
        def _():
            dk_ref[...] = jnp.zeros(dk_ref.shape, F32)
            dv_ref[...] = jnp.zeros(dv_ref.shape, F32)
            dbias_ref[...] = jnp.zeros(dbias_ref.shape, F32)

        kb = jnp.concatenate([r[...] for r in k_refs], axis=0)
        vb = jnp.concatenate([r[...] for r in v_refs], axis=0)
        dkb_ref[...] = jnp.zeros(dkb_ref.shape, F32)
        dvb_ref[...] = jnp.zeros(dvb_ref.shape, F32)
        row = lax.broadcasted_iota(jnp.int32, (SUBLANES, LANES), 0)
        dsink = jnp.zeros((SUBLANES, LANES), F32)
        for b in range(sb):
            at_end = b == 0 or b == sb - 1
            mask = _in_sequence(i * sb + b, T) if at_end else None
            rows = slice(b * BLOCK_Q, (b + 1) * BLOCK_Q)
            win = slice(b * BLOCK_Q, (b + 3) * BLOCK_Q)
            kw, vw = kb[win], vb[win]
            dkw = jnp.zeros((3 * BLOCK_Q, HEAD_DIM), F32)
            dvw = jnp.zeros((3 * BLOCK_Q, HEAD_DIM), F32)
            for g in range(GROUP):
                sk = sink_ref[kv * GROUP + g]
                q, do = q_refs[g][rows, :], do_refs[g][rows, :]
                lse_g = lse_ref[g, rows, :]
                delta = jnp.sum(do.astype(F32) * o_refs[g][rows, :].astype(F32), axis=-1, keepdims=True)
                s = lax.dot_general(q, kw, _NT, preferred_element_type=F32) * scale + bias_ref[g]
                if at_end:
                    s = jnp.where(mask, s, NEG_INF)
                p = jnp.exp(s - lse_g)
                dp = lax.dot_general(do, vw, _NT, preferred_element_type=F32)
                ds = p * (dp - delta)
                dbias_ref[g] += ds
                dsink = dsink + jnp.where(row == g, -jnp.sum(jnp.exp(sk - lse_g) * delta), 0.0)
                dsb = (ds * scale).astype(BF16)
                dq_ref[rows, g * HEAD_DIM:(g + 1) * HEAD_DIM] = lax.dot_general(dsb, kw, _NN, preferred_element_type=F32)
                dkw = dkw + lax.dot_general(dsb, q, _TN, preferred_element_type=F32)
                dvw = dvw + lax.dot_general(p.astype(BF16), do, _TN, preferred_element_type=F32)
            dkb_ref[win, :] += dkw
            dvb_ref[win, :] += dvw
        _accumulate(dsink_ref, dsink, first)

        before = pl.ds(pl.multiple_of(jnp.maximum(sb * i - 1, 0) * BLOCK_Q, BLOCK_Q), BLOCK_Q)
        own = pl.ds(pl.multiple_of(i * tq, BLOCK_Q), tq)
        after = pl.ds(pl.multiple_of(jnp.minimum(sb * i + sb, nblk - 1) * BLOCK_Q, BLOCK_Q), BLOCK_Q)
        for acc_ref, band_ref in ((dk_ref, dkb_ref), (dv_ref, dvb_ref)):
            acc_ref[before, :] += band_ref[0:BLOCK_Q, :]
            acc_ref[own, :] += band_ref[BLOCK_Q:BLOCK_Q + tq, :]
            acc_ref[after, :] += band_ref[BLOCK_Q + tq:, :]

    return _pcall(
        body, deps, name="attn_b_bwd", grid=(n_kv, nblk // sb),
        in_specs=[*_head_specs(q_off, tq),
                  *_band_specs(lambda kv: q_off + n_q + kv, nblk, sb),
                  *_band_specs(lambda kv: q_off + n_q + n_kv + kv, nblk, sb),
                  *_head_specs(o_off, tq), *_head_specs(o_off, tq),
                  pl.BlockSpec((GROUP, tq, 1), lambda kv, i: (kv, i, 0)),
                  pl.BlockSpec((GROUP, BLOCK_Q, 3 * BLOCK_Q), lambda kv, i: (kv, 0, 0)),
                  pl.BlockSpec(memory_space=pltpu.SMEM)],
        out_specs=[pl.BlockSpec((tq, GROUP * HEAD_DIM), lambda kv, i: (i, kv)),
                   pl.BlockSpec((T, HEAD_DIM), lambda kv, i: (0, kv)),
                   pl.BlockSpec((T, HEAD_DIM), lambda kv, i: (0, kv)),
                   pl.BlockSpec((GROUP, BLOCK_Q, 3 * BLOCK_Q), lambda kv, i: (kv, 0, 0)),
                   pl.BlockSpec((None, SUBLANES, LANES), lambda kv, i: (kv, 0, 0))],
        out_shape=[jax.ShapeDtypeStruct((T, n_q * HEAD_DIM), F32),
                   jax.ShapeDtypeStruct((T, n_kv * HEAD_DIM), F32),
                   jax.ShapeDtypeStruct((T, n_kv * HEAD_DIM), F32),
                   jax.ShapeDtypeStruct((n_q, BLOCK_Q, 3 * BLOCK_Q), F32),
                   jax.ShapeDtypeStruct((n_kv, SUBLANES, LANES), F32)],
        scratch_shapes=[pltpu.VMEM((tq + 2 * BLOCK_Q, HEAD_DIM), F32), pltpu.VMEM((tq + 2 * BLOCK_Q, HEAD_DIM), F32)],
        compiler_params=_params(("parallel", "arbitrary")),
    )(*([pb] * (GROUP + 6)), *([o_cat] * GROUP), *([d_o] * GROUP), lse, bias, sink)


def _table_grads(dbias, dsink_raw, idx):
    n_heads = dbias.shape[0]
    n_kv = dsink_raw.shape[0]

    def body(db_ref, ds_ref, idx_ref, dt_ref, dsk_ref):
        iv = idx_ref[...]
        row = lax.broadcasted_iota(jnp.int32, (SUBLANES, LANES), 0)
        lane = lax.broadcasted_iota(jnp.int32, (SUBLANES, LANES), 1)
        dsk = jnp.zeros((SUBLANES, LANES), F32)
        for h in range(n_heads):
            d = db_ref[h]
            acc = jnp.zeros((SUBLANES, LANES), F32)
            for b in range(N_BUCKETS):
                acc = jnp.where((row == 0) & (lane == b), jnp.sum(jnp.where(iv == b, d, 0.0)), acc)
            dt_ref[:, h * LANES:(h + 1) * LANES] = acc
            raw = ds_ref[h // GROUP]
            val = jnp.sum(jnp.where((row == h % GROUP) & (lane == 0), raw, 0.0))
            dsk = jnp.where((row == 0) & (lane == h), val, dsk)
        dsk_ref[...] = dsk

    return pl.pallas_call(
        body, name="table_grads",
        in_specs=[pl.BlockSpec(memory_space=pltpu.VMEM)] * 3,
        out_specs=[pl.BlockSpec(memory_space=pltpu.VMEM)] * 2,
        out_shape=[jax.ShapeDtypeStruct((SUBLANES, n_heads * LANES), F32),
                   jax.ShapeDtypeStruct((SUBLANES, LANES), F32)],
        compiler_params=pltpu.CompilerParams(vmem_limit_bytes=56 * 1024 * 1024),
    )(dbias, dsink_raw, idx)


def _position():
    x, y, c = lax.axis_index("x"), lax.axis_index("y"), lax.axis_index("c")
    return x, y, c


def _hbm(a):
    return pltpu.with_memory_space_constraint(a, pltpu.HBM)


def _split_start(name, bufs, sem_shapes, issue):
    nb, ns = len(bufs), len(sem_shapes)

    def body(*refs):
        buf_refs = refs[:nb]
        sems = refs[nb:nb + ns]
        token = refs[nb + ns + nb]
        issue(buf_refs, sems)
        token[...] = jnp.zeros(token.shape, F32)

    outs = pl.pallas_call(
        body, name=name,
        in_specs=[_HBM] * nb,
        out_specs=[_SEM] * ns + [_HBM] * nb + [_VMEM],
        out_shape=[pltpu.SemaphoreType.DMA(s) for s in sem_shapes] + [pltpu.HBM(b.shape, b.dtype) for b in bufs]
        + [jax.ShapeDtypeStruct((SUBLANES, LANES), F32)],
        input_output_aliases={i: ns + i for i in range(nb)},
        compiler_params=pltpu.CompilerParams(has_side_effects=_EFFECT),
    )(*[_hbm(b) for b in bufs])
    return outs[:ns], outs[ns:ns + nb], outs[-1]


def _split_wait(name, bufs, send, recv, counts, size_of, after):
    nb = len(bufs)

    def body(*refs):
        buf_refs = refs[:nb]
        send_ref, recv_ref = refs[nb], refs[nb + 1]
        x, y, c = _position()
        for w, n in enumerate(counts):
            ref = size_of(buf_refs, w)
            for k in range(n):
                s = sum(counts[:w]) + k
                cp = pltpu.make_async_remote_copy(
                    src_ref=ref, dst_ref=ref, send_sem=send_ref.at[s], recv_sem=recv_ref.at[s],
                    device_id=(x, y, c), device_id_type=MESH)
                cp.wait_send()
                cp.wait_recv()

    return pl.pallas_call(
        body, name=name,
        in_specs=[_HBM] * nb + [_SEM, _SEM, _ANY],
        out_specs=[_HBM] * nb,
        out_shape=[pltpu.HBM(b.shape, b.dtype) for b in bufs],
        input_output_aliases={i: i for i in range(nb)},
        compiler_params=pltpu.CompilerParams(has_side_effects=_EFFECT),
    )(*bufs, send, recv, after)


def _block_of(pos):
    return 4 * pos[0] + 2 * pos[1] + pos[2]


def _shard_of(ref, blk, by_cols):
    aligned = (lambda v, a: v) if isinstance(blk, int) else pl.multiple_of
    if by_cols:
        n = ref.shape[1] // N_DEV
        return ref.at[:, pl.ds(aligned(blk * n, LANES), n)]
    r = ref.shape[0] // N_DEV
    return ref.at[pl.ds(aligned(blk * r, SUBLANES), r), :]


def _place_shards(shards, by_cols):
    mine = _block_of(_position()).astype(jnp.int32).reshape(1)

    def place(name, s, cols, tr=256):
        r, n = s.shape
        tr = _tile(r, tr)

        def body(m_ref, s_ref, o_ref):
            o_ref[...] = s_ref[...].astype(BF16)

        if cols:
            out = pl.BlockSpec((tr, n), lambda i, m_ref: (i, m_ref[0]))
        else:
            out = pl.BlockSpec((tr, n), lambda i, m_ref: (m_ref[0] * (r // tr) + i, 0))
        return pl.pallas_call(
            body, name=name,
            grid_spec=pltpu.PrefetchScalarGridSpec(
                num_scalar_prefetch=1, grid=(r // tr,),
                in_specs=[pl.BlockSpec((tr, n), lambda i, m_ref: (i, 0))], out_specs=out),
            out_shape=jax.ShapeDtypeStruct((r, n * N_DEV) if cols else (r * N_DEV, n), BF16),
            compiler_params=_params(("parallel",)),
        )(mine, s)

    return [place("place_shard_%d" % w, s, cols) for w, (s, cols) in enumerate(zip(shards, by_cols))]


def _gather_start(shards, by_cols, groups):
    lands = _place_shards(shards, by_cols)

    def issue(land, sems):
        x, y, c = _position()
        peers = [(x, y, 1 - c), (1 - x, y, c), (x, 1 - y, c), (1 - x, 1 - y, c)]
        for gi, grp in enumerate(groups):
            for wi, w in enumerate(grp):
                own = _shard_of(land[w], _block_of((x, y, c)), by_cols[w])
                for k, peer in enumerate(peers):
                    pltpu.make_async_remote_copy(
                        src_ref=own, dst_ref=own, send_sem=sems[2 * gi].at[4 * wi + k],
                        recv_sem=sems[2 * gi + 1].at[4 * wi + k], device_id=peer, device_id_type=MESH).start()

    sem_shapes = [(4 * len(g),) for g in groups for _ in range(2)]
    return _split_start("gather_start", lands, sem_shapes, issue)


def _gather_forward(name, lands, by_cols):
    nw = len(lands)

    def issue(land, sems):
        x, y, c = _position()
        for w in range(nw):
            for k, chip in enumerate([(1 - x, y), (x, 1 - y), (1 - x, 1 - y)]):
                blk = _shard_of(land[w], _block_of((*chip, c)), by_cols[w])
                pltpu.make_async_remote_copy(
                    src_ref=blk, dst_ref=blk, send_sem=sems[0].at[3 * w + k], recv_sem=sems[1].at[3 * w + k],
                    device_id=(x, y, 1 - c), device_id_type=MESH).start()

    return _split_start(name, lands, [(3 * nw,), (3 * nw,)], issue)


def _first_block(bufs, w, offset=0):
    return bufs[offset + w].at[0]


_PEER_FLIPS = ((0, 0, 1), (1, 0, 0), (1, 0, 1), (0, 1, 0), (0, 1, 1), (1, 1, 0), (1, 1, 1))


def _scatter_start(name, grads, by_cols):
    nw = len(grads)
    lands = []
    for g, cols in zip(grads, by_cols):
        shard = (g.shape[0], g.shape[1] // N_DEV) if cols else (g.shape[0] // N_DEV, g.shape[1])
        lands.append(lax.empty((N_DEV,) + shard, g.dtype))

    def issue(bufs, sems):
        x, y, c = _position()
        flip = lambda v, f: 1 - v if f else v
        for w in range(nw):
            for k, (fx, fy, fc) in enumerate(_PEER_FLIPS):
                peer = (flip(x, fx), flip(y, fy), flip(c, fc))
                pltpu.make_async_remote_copy(
                    src_ref=_shard_of(bufs[w], _block_of(peer), by_cols[w]), dst_ref=bufs[nw + w].at[_block_of((x, y, c))],
                    send_sem=sems[0].at[7 * w + k], recv_sem=sems[1].at[7 * w + k],
                    device_id=peer, device_id_type=MESH).start()

    return _split_start(name, list(grads) + lands, [(7 * nw,), (7 * nw,)], issue)


def _adam(w, g, m, v):
    m = ADAM_B1 * m + (1.0 - ADAM_B1) * g
    v = ADAM_B2 * v + (1.0 - ADAM_B2) * (g * g)
    m_hat = m / (1.0 - ADAM_B1 ** ADAM_STEP)
    v_hat = v / (1.0 - ADAM_B2 ** ADAM_STEP)
    delta = -ADAM_LR * (m_hat / (jnp.sqrt(v_hat) + ADAM_EPS) + ADAM_WD * w)
    return delta, m, v


def _sum_adam(name, landed, grad, by_cols, w, m, v, tr=256):
    R, C = w.shape
    tr = _tile(R, tr)
    mine = _block_of(_position()).astype(jnp.int32).reshape(1)

    def body(me_ref, l_ref, own_ref, w_ref, m_ref, v_ref, g_ref, d_ref, nm_ref, nv_ref):
        own = own_ref[...].astype(F32)
        g = None
        for d in range(N_DEV):
            part = jnp.where(me_ref[0] == d, own, l_ref[d].astype(F32))
            g = part if g is None else g + part
        g_ref[...] = g
        d_ref[...], nm_ref[...], nv_ref[...] = _adam(w_ref[...], g, m_ref[...], v_ref[...])

    tile = pl.BlockSpec((tr, C), lambda i, me_ref: (i, 0))
    if by_cols:
        own = pl.BlockSpec((tr, C), lambda i, me_ref: (i, me_ref[0]))
    else:
        own = pl.BlockSpec((tr, C), lambda i, me_ref: (me_ref[0] * (R // tr) + i, 0))
    return pl.pallas_call(
        body, name=name,
        grid_spec=pltpu.PrefetchScalarGridSpec(
            num_scalar_prefetch=1, grid=(R // tr,),
            in_specs=[pl.BlockSpec((N_DEV, tr, C), lambda i, me_ref: (0, i, 0)), own, tile, tile, tile],
            out_specs=[tile] * 4),
        out_shape=[jax.ShapeDtypeStruct((R, C), F32)] * 4,
        compiler_params=_params(("parallel",)),
    )(mine, landed, grad, w, m, v)


def _small_all_reduce(parts, deps=()):
    W = parts.shape[1]

    def body(p_ref, o_ref, slots, send_sems, recv_sems):
        x, y, c = _position()
        me = 4 * x + 2 * y + c
        slots[me] = jnp.sum(p_ref[...], axis=0, keepdims=True)
        peers = [(x, y, 1 - c), (1 - x, y, c), (1 - x, y, 1 - c), (x, 1 - y, c), (x, 1 - y, 1 - c),
                 (1 - x, 1 - y, c), (1 - x, 1 - y, 1 - c)]
        copies = []
        for k, peer in enumerate(peers):
            cp = pltpu.make_async_remote_copy(
                src_ref=slots.at[me], dst_ref=slots.at[me], send_sem=send_sems.at[k], recv_sem=recv_sems.at[k],
                device_id=peer, device_id_type=MESH)
            cp.start()
            copies.append(cp)
        for cp in copies:
            cp.wait()
        total = slots[0]
        for d in range(1, N_DEV):
            total = total + slots[d]
        o_ref[...] = total

    return _pcall(
        body, deps, name="small_all_reduce",
        in_specs=[pl.BlockSpec(memory_space=pltpu.VMEM)], out_specs=pl.BlockSpec(memory_space=pltpu.VMEM),
        out_shape=jax.ShapeDtypeStruct((1, W), F32),
        scratch_shapes=[pltpu.VMEM((N_DEV, 1, W), F32), pltpu.SemaphoreType.DMA((7,)), pltpu.SemaphoreType.DMA((7,))],
    )(parts)


def _adam_small(w, g, m, v):
    def body(w_ref, g_ref, m_ref, v_ref, d_ref, nm_ref, nv_ref):
        d_ref[...], nm_ref[...], nv_ref[...] = _adam(w_ref[...], g_ref[...], m_ref[...], v_ref[...])

    return pl.pallas_call(
        body, name="adam_small",
        in_specs=[pl.BlockSpec(memory_space=pltpu.VMEM)] * 4, out_specs=[pl.BlockSpec(memory_space=pltpu.VMEM)] * 3,
        out_shape=[jax.ShapeDtypeStruct(w.shape, F32)] * 3,
    )(w, g, m, v)


_GATHER_GROUPS = (("w_in",), ("w_out", "w_up", "ple_w"), ("w_down", "w_gate"))
_COL_SHARDED = ("w_in", "w_up", "ple_w")


class _MeshComm:
    def __init__(self, w, mom, var):
        self.w, self.mom, self.var = w, mom, var
        self.out = {}
        self._scatters = {}

    def gather_begin(self):
        names = [n for g in _GATHER_GROUPS for n in g]
        self._idx = {n: i for i, n in enumerate(names)}
        groups = [[self._idx[n] for n in g] for g in _GATHER_GROUPS]
        self._sems, self._lands, token = _gather_start(
            [self.w[n] for n in names], [n in _COL_SHARDED for n in names], groups)
        return token

    @staticmethod
    def _shard_size(names, offset):
        return lambda bufs, w: _shard_of(bufs[offset + w], 0, names[w] in _COL_SHARDED)

    def gather_arrive(self, gi, after):
        names = _GATHER_GROUPS[gi]
        ids = [self._idx[n] for n in names]
        self._arrived = _split_wait("gather_arrive%d" % gi, [self._lands[i] for i in ids], self._sems[2 * gi],
                                    self._sems[2 * gi + 1], [4] * len(ids), self._shard_size(names, 0), after)

    def gather_forward(self, gi):
        by_cols = [n in _COL_SHARDED for n in _GATHER_GROUPS[gi]]
        self._fsems, self._fthru, token = _gather_forward("gather_forward%d" % gi, self._arrived, by_cols)
        return token

    def gather_finish(self, gi, after):
        names = _GATHER_GROUPS[gi]
        out = _split_wait("gather_finish%d" % gi, self._fthru, self._fsems[0], self._fsems[1], [3] * len(names),
                          self._shard_size(names, 0), after)
        return dict(zip(names, out))

    def reduce_begin(self, key, grads):
        names = list(grads)
        sems, thru, token = _scatter_start("scatter_start_" + key, [grads[n] for n in names],
                                           [n in _COL_SHARDED for n in names])
        self._scatters[key] = (names, sems, thru)
        return token

    def reduce_finish(self, key, after):
        names, sems, thru = self._scatters[key]
        nw = len(names)
        out = _split_wait("scatter_wait_" + key, thru, sems[0], sems[1], [N_DEV - 1] * nw,
                          functools.partial(_first_block, offset=nw), after)
        for i, n in enumerate(names):
            self.out[n] = _sum_adam("adam_" + n, out[nw + i], out[i], n in _COL_SHARDED, self.w[n], self.mom[n],
                                    self.var[n])


def _step(x, p, target, gains, comm):
    T, D = x.shape
    n_q = D // (2 * HEAD_DIM)
    n_kv = n_q // GROUP
    cos, sin = _rope_tables(T)
    idx = _bucket_index()

    t = comm.gather_begin()
    u = _rms_fwd("norm_attn", x, gains["attn_norm_g"], deps=(t,))
    comm.gather_arrive(0, u)
    t = comm.gather_forward(0)
    bias = _bias_build(idx, gains["rel_bias_table"].reshape(-1), n_q, deps=(t,))
    full = comm.gather_finish(0, bias)
    proj = _mm_nn("in_proj", u, full["w_in"])
    pb = _qk_prep(proj, cos, sin, gains["q_norm_g"], gains["k_norm_g"], n_q + n_kv)
    o_a, lse_a = _attn_a_fwd(pb, n_q, n_kv)
    comm.gather_arrive(1, o_a)
    t = comm.gather_forward(1)
    sink = gains["sink_logits"].reshape(-1)
    b_off = n_q + 2 * n_kv
    o_b, lse_b = _attn_b_fwd(pb, bias, sink, b_off, n_q, n_kv, deps=(t,))
    full.update(comm.gather_finish(1, o_b))
    o_cat = jnp.concatenate([o_a, o_b], axis=1)
    h1, m_in = _mm_nn_rms("out_proj", o_cat, full["w_out"], x, gains["mlp_norm_g"])

    def up_epilogue(acc, extra, outs):
        outs[0][...] = acc.astype(BF16)
        r = jnp.maximum(acc, 0.0)
        outs[1][...] = (r * r).astype(BF16)

    a_act, f_act = _mm_nn("up_proj", m_in, full["w_up"], epilogue=up_epilogue, out_dtypes=[BF16, BF16])
    comm.gather_arrive(2, f_act)
    t = comm.gather_forward(2)
    p_b = p.astype(BF16)
    pe = _mm_nn("ple_proj", p_b, full["ple_w"], deps=(t,))
    full.update(comm.gather_finish(2, pe))
    h2 = _mm_nn("down_proj", f_act, full["w_down"], epilogue=_store_add, extras=(h1,), tn=256)
    gn = _rms_fwd("norm_gate", h2, gains["gate_norm_g"])
    z = _mm_nn("gate_proj", gn, full["w_gate"])

    dh3, dz, dpe, dg_final, dg_ple, loss_part = _tail(h2, z, pe, target, gains["ple_norm_g"], gains["final_norm_g"])
    gw_gate = _mm_tn("grad_w_gate", gn, dz)
    gw_ple = _mm_tn("grad_ple_w", p_b, dpe)
    t = comm.reduce_begin("a", dict(w_gate=gw_gate, ple_w=gw_ple))
    dh2, dh2_b, dg_gate = _mm_nt_rms_bwd("d_gate_in", dz, full["w_gate"], h2, gains["gate_norm_g"], dh3, deps=(t,))
    gw_down = _mm_tn("grad_w_down", f_act, dh2_b)
    t = comm.reduce_begin("b", dict(w_down=gw_down))

    def act_bwd(acc, extra, outs):
        outs[0][...] = (acc * (2.0 * jnp.maximum(extra[0][...].astype(F32), 0.0))).astype(BF16)

    da = _mm_nt("d_act", dh2_b, full["w_down"], out_dtype=BF16, epilogue=act_bwd, extras=(a_act,), deps=(t,))
    gw_up = _mm_tn("grad_w_up", m_in, da)
    t = comm.reduce_begin("c", dict(w_up=gw_up))
    dm = _mm_nt("d_mlp_in", da, full["w_up"], tn=256, deps=(t,))
    dh1, dh1_b, dg_mlp = _rms_bwd("norm_mlp_bwd", dm, h1, gains["mlp_norm_g"], dh2)
    gw_out = _mm_tn("grad_w_out", o_cat, dh1_b)
    t = comm.reduce_begin("d", dict(w_out=gw_out))
    d_o = _mm_nt("d_attn_out", dh1_b, full["w_out"], out_dtype=BF16, deps=(t,))
    dqa, dka, dva = _attn_a_bwd(pb, o_cat, d_o, lse_a, n_q, n_kv)
    dqb, dkb, dvb, dbias, dsink_raw = _attn_b_bwd(pb, o_cat, d_o, lse_b, bias, sink, b_off, n_q, n_kv, n_q)
    dtable, dsink = _table_grads(dbias, dsink_raw, idx)
    dproj, dg_q, dg_k = _dproj(proj, dqa, dka, dva, dqb, dkb, dvb, cos, sin, gains["q_norm_g"], gains["k_norm_g"])
    gw_in = _mm_tn("grad_w_in", u, dproj)
    t = comm.reduce_begin("e", dict(w_in=gw_in))
    dx, _, dg_attn = _mm_nt_rms_bwd("d_attn_in", dproj, full["w_in"], x, gains["attn_norm_g"], dh1, deps=(t,))
    for key in "abcd":
        comm.reduce_finish(key, dx)

    parts = jnp.concatenate([dg_attn, dg_mlp, dg_ple, dg_gate, dg_final, dg_q, dg_k, dtable, dsink, loss_part], axis=1)
    return dx, parts


_SHARDED = ("w_in", "w_out", "w_up", "w_down", "ple_w", "w_gate")
_VECTORS = ("attn_norm_g", "mlp_norm_g", "ple_norm_g", "gate_norm_g", "final_norm_g")
_ORDER = ("attn_norm_g", "w_in", "q_norm_g", "k_norm_g", "sink_logits", "w_out", "mlp_norm_g", "w_up", "w_down",
          "ple_w", "ple_norm_g", "gate_norm_g", "w_gate", "rel_bias_table", "final_norm_g")


def _pack_small(vals, n_heads):
    lane_pad = lambda v: jnp.pad(v, ((0, 0), (0, LANES - v.shape[1])))
    table = lane_pad(vals["rel_bias_table"].T).reshape(1, n_heads * LANES)
    return jnp.concatenate(
        [vals[n].reshape(1, -1) for n in _VECTORS] + [vals["q_norm_g"], vals["k_norm_g"], table,
                                                      lane_pad(vals["sink_logits"]), jnp.zeros((1, LANES), F32)], axis=1)


def _unpack_small(row, like, n_heads):
    out, off = {}, 0
    for n in _VECTORS:
        out[n] = row[:, off:off + like[n].size].reshape(like[n].shape)
        off += like[n].size
    for n in ("q_norm_g", "k_norm_g"):
        out[n] = row[:, off:off + LANES]
        off += LANES
    out["rel_bias_table"] = row[:, off:off + n_heads * LANES].reshape(n_heads, LANES)[:, :N_BUCKETS].T
    off += n_heads * LANES
    out["sink_logits"] = row[:, off:off + n_heads]
    off += LANES
    return out, row[0, off]


def kernel(x, p, attn_norm_g, w_in, q_norm_g, k_norm_g, sink_logits, w_out, mlp_norm_g, w_up, w_down, ple_w, ple_norm_g, gate_norm_g, w_gate, rel_bias_table, final_norm_g, loss_target, m_attn_norm_g, m_w_in, m_q_norm_g, m_k_norm_g, m_sink_logits, m_w_out, m_mlp_norm_g, m_w_up, m_w_down, m_ple_w, m_ple_norm_g, m_gate_norm_g, m_w_gate, m_rel_bias_table, m_final_norm_g, v_attn_norm_g, v_w_in, v_q_norm_g, v_k_norm_g, v_sink_logits, v_w_out, v_mlp_norm_g, v_w_up, v_w_down, v_ple_w, v_ple_norm_g, v_gate_norm_g, v_w_gate, v_rel_bias_table, v_final_norm_g):
    w = dict(attn_norm_g=attn_norm_g, w_in=w_in[0], q_norm_g=q_norm_g, k_norm_g=k_norm_g, sink_logits=sink_logits,
             w_out=w_out[0], mlp_norm_g=mlp_norm_g, w_up=w_up[0], w_down=w_down[0], ple_w=ple_w[0],
             ple_norm_g=ple_norm_g, gate_norm_g=gate_norm_g, w_gate=w_gate[0], rel_bias_table=rel_bias_table,
             final_norm_g=final_norm_g)
    mom = dict(attn_norm_g=m_attn_norm_g, w_in=m_w_in[0], q_norm_g=m_q_norm_g, k_norm_g=m_k_norm_g,
               sink_logits=m_sink_logits, w_out=m_w_out[0], mlp_norm_g=m_mlp_norm_g, w_up=m_w_up[0],
               w_down=m_w_down[0], ple_w=m_ple_w[0], ple_norm_g=m_ple_norm_g, gate_norm_g=m_gate_norm_g,
               w_gate=m_w_gate[0], rel_bias_table=m_rel_bias_table, final_norm_g=m_final_norm_g)
    var = dict(attn_norm_g=v_attn_norm_g, w_in=v_w_in[0], q_norm_g=v_q_norm_g, k_norm_g=v_k_norm_g,
               sink_logits=v_sink_logits, w_out=v_w_out[0], mlp_norm_g=v_mlp_norm_g, w_up=v_w_up[0],
               w_down=v_w_down[0], ple_w=v_ple_w[0], ple_norm_g=v_ple_norm_g, gate_norm_g=v_gate_norm_g,
               w_gate=v_w_gate[0], rel_bias_table=v_rel_bias_table, final_norm_g=v_final_norm_g)
    D = x.shape[-1]
    n_heads = D // (2 * HEAD_DIM)

    gains = {n: w[n] for n in w if n not in _SHARDED}
    gains["final_norm_g"] = final_norm_g.reshape(1, -1)

    comm = _MeshComm(w, mom, var)
    dx, parts = _step(x[0], p[0, 0], loss_target[0], gains, comm)

    small_g = _small_all_reduce(parts, deps=[comm.out[n][0] for n in comm.out])
    comm.reduce_finish("e", small_g)

    g_out, d_out, m_out, v_out = {}, {}, {}, {}
    for n in _SHARDED:
        g, d, nm, nv = comm.out[n]
        g_out[n], d_out[n], m_out[n], v_out[n] = g[None], d[None], nm[None], nv[None]

    small = {n: v for n, v in w.items() if n not in _SHARDED}
    pack = lambda vals: _pack_small({n: vals[n] for n in small}, n_heads)
    sd, sm, sv = _adam_small(pack(w), small_g, pack(mom), pack(var))
    sg, loss = _unpack_small(small_g, small, n_heads)
    g_out.update(sg)
    for dst, row in ((d_out, sd), (m_out, sm), (v_out, sv)):
        dst.update(_unpack_small(row, small, n_heads)[0])

    return (loss, dx[None], *[g_out[n] for n in _ORDER], *[d_out[n] for n in _ORDER],
            *[m_out[n] for n in _ORDER], *[v_out[n] for n in _ORDER])
```

```python
import functools
import math

import numpy as np
import jax
import jax.numpy as jnp
from jax import lax
from jax.experimental import pallas as pl
from jax.experimental.pallas import tpu as pltpu

F32 = jnp.float32
BF16 = jnp.bfloat16

N_DEV = 8
N_CHIP = 4
HEAD_DIM = 128
GROUP = 4
GRID_W = 64
WINDOW = 128
BLOCK_Q = 128
N_BUCKETS = 32
MAX_DISTANCE = 128
ROPE_THETA = 10000.0
EPS = 1e-6
NEG_INF = -1e30
ADAM_LR = 0.001
ADAM_B1 = 0.9
ADAM_B2 = 0.999
ADAM_EPS = 1e-08
ADAM_WD = 0.01
ADAM_STEP = 10
LOG2E = math.log2(math.e)
LANES = 128
SUBLANES = 8
MESH = pl.DeviceIdType.MESH

_NT = (((1,), (1,)), ((), ()))
_NN = (((1,), (0,)), ((), ()))
_TN = (((0,), (0,)), ((), ()))


def _tile(dim, pref):
    return pref if dim % pref == 0 else dim


def _params(sem):
    return pltpu.CompilerParams(dimension_semantics=sem, vmem_limit_bytes=56 * 1024 * 1024)


_HBM = pl.BlockSpec(memory_space=pltpu.HBM)
_SEM = pl.BlockSpec(memory_space=pltpu.SEMAPHORE)
_ANY = pl.BlockSpec(memory_space=pl.ANY)
_VMEM = pl.BlockSpec(memory_space=pltpu.VMEM)
_EFFECT = pltpu.SideEffectType.DATAFLOW_SIDE_EFFECTING


def _pcall(body, deps=(), *, in_specs, into=None, **kw):
    deps = [d for d in deps if d is not None]
    nd = len(deps)
    if into is not None:
        deps = [into[0]] + deps
        nd += 1
        kw["input_output_aliases"] = {0: into[1]}

    def wrapped(*refs):
        body(*refs[nd:])

    call = pl.pallas_call(wrapped, in_specs=[_ANY] * nd + list(in_specs), **kw)
    return lambda *args: call(*deps, *args)


def _mm(name, a, b, dims, grid, a_spec, b_spec, out_shape, out_specs, acc_shape, epilogue,
        extras=(), extra_specs=(), deps=(), semantics=("parallel", "parallel", "arbitrary")):
    nk = grid[2]
    n_extra = len(extras)

    def body(*refs):
        a_ref, b_ref = refs[0], refs[1]
        extra = refs[2:2 + n_extra]
        outs = refs[2 + n_extra:-1]
        acc = refs[-1]
        part = lax.dot_general(a_ref[...], b_ref[...], dims, preferred_element_type=F32)
        if nk == 1:
            epilogue(part, extra, outs)
        else:
            k = pl.program_id(2)

            @pl.when(k == 0)
            def _():
                acc[...] = part

            @pl.when(k > 0)
            def _():
                acc[...] += part

            @pl.when(k == nk - 1)
            def _():
                epilogue(acc[...], extra, outs)

    return _pcall(
        body, deps, name=name, grid=grid,
        in_specs=[a_spec, b_spec, *extra_specs],
        out_specs=out_specs, out_shape=out_shape,
        scratch_shapes=[pltpu.VMEM(acc_shape if nk > 1 else (SUBLANES, LANES), F32)],
        compiler_params=_params(semantics),
    )(a, b, *extras)


def _store(dtype):
    def ep(acc, extra, outs):
        outs[0][...] = acc.astype(dtype)
    return ep


def _store_add(acc, extra, outs):
    outs[0][...] = acc + extra[0][...]


def _mm_nn(name, a, b, out_dtype=F32, epilogue=None, extras=(), n_out=1, out_dtypes=None, tm=1024, tn=1024, tk=None,
           deps=()):
    M, K = a.shape
    N = b.shape[1]
    tm, tn, tk = _tile(M, tm), _tile(N, tn), _tile(K, tk or K)
    b_spec = pl.BlockSpec((tk, tn), lambda i, j, k: (k, j))
    grid = (M // tm, N // tn, K // tk)
    o_spec = pl.BlockSpec((tm, tn), lambda i, j, k: (i, j))
    out_dtypes = out_dtypes or [out_dtype] * n_out
    out_shape = [jax.ShapeDtypeStruct((M, N), d) for d in out_dtypes]
    res = _mm(name, a, b, _NN, grid, pl.BlockSpec((tm, tk), lambda i, j, k: (i, k)), b_spec,
              out_shape, [o_spec] * len(out_dtypes), (tm, tn), epilogue or _store(out_dtype),
              extras, [o_spec] * len(extras), deps)
    return res if len(out_dtypes) > 1 else res[0]


def _mm_nt(name, a, b, out_dtype=F32, epilogue=None, extras=(), tm=1024, tn=1024, tk=None, deps=()):
    M, C = a.shape
    N = b.shape[0]
    tm, tn, tk = _tile(M, tm), _tile(N, tn), _tile(C, tk or C)
    b_spec = pl.BlockSpec((tn, tk), lambda i, j, k: (j, k))
    grid = (M // tm, N // tn, C // tk)
    o_spec = pl.BlockSpec((tm, tn), lambda i, j, k: (i, j))
    return _mm(name, a, b, _NT, grid, pl.BlockSpec((tm, tk), lambda i, j, k: (i, k)), b_spec,
               [jax.ShapeDtypeStruct((M, N), out_dtype)], [o_spec], (tm, tn), epilogue or _store(out_dtype),
               extras, [o_spec] * len(extras), deps)[0]


def _mm_tn(name, a, b, out_dtype=BF16, tm=1024, tn=512, tk=None, deps=()):
    T, M = a.shape
    N = b.shape[1]
    tm, tn, tk = _tile(M, tm), _tile(N, tn), _tile(T, tk or T)
    out_shape = jax.ShapeDtypeStruct((M, N), out_dtype)
    o_spec = pl.BlockSpec((tm, tn), lambda i, j, k: (i, j))
    grid = (M // tm, N // tn, T // tk)
    return _mm(name, a, b, _TN, grid, pl.BlockSpec((tk, tm), lambda i, j, k: (k, i)),
               pl.BlockSpec((tk, tn), lambda i, j, k: (k, j)), [out_shape], [o_spec], (tm, tn), _store(out_dtype),
               deps=deps)[0]


def _mean_last(v):
    return jnp.mean(v, axis=-1, keepdims=True)


def _rows_to_sublanes(v):
    r, c = v.shape
    return jnp.sum(v.reshape(r // SUBLANES, SUBLANES, c), axis=0)


def _accumulate(ref, val, first):
    @pl.when(first)
    def _():
        ref[...] = val

    @pl.when(jnp.logical_not(first))
    def _():
        ref[...] += val


def _rms_fwd(name, x, g, tr=256, deps=()):
    T, D = x.shape
    tr = _tile(T, tr)

    def body(x_ref, g_ref, o_ref):
        xv = x_ref[...]
        r = lax.rsqrt(_mean_last(xv * xv) + EPS)
        o_ref[...] = (xv * r * g_ref[...]).astype(BF16)

    row = pl.BlockSpec((tr, D), lambda i: (i, 0))
    return _pcall(
        body, deps, name=name, grid=(T // tr,),
        in_specs=[row, pl.BlockSpec((1, D), lambda i: (0, 0))],
        out_specs=row, out_shape=jax.ShapeDtypeStruct((T, D), BF16),
        compiler_params=_params(("parallel",)),
    )(x, g)


def _rms_bwd(name, dyn, x, g, dres, tr=256, deps=()):
    T, D = x.shape
    tr = _tile(T, tr)

    def body(dy_ref, x_ref, g_ref, dr_ref, dx_ref, dxb_ref, dg_ref):
        xv = x_ref[...]
        r = lax.rsqrt(_mean_last(xv * xv) + EPS)
        xn = xv * r
        dy = dy_ref[...]
        dxn = dy * g_ref[...]
        dx = dr_ref[...] + r * (dxn - xn * _mean_last(dxn * xn))
        dx_ref[...] = dx
        dxb_ref[...] = dx.astype(BF16)
        _accumulate(dg_ref, _rows_to_sublanes(dy * xn), pl.program_id(0) == 0)

    row = pl.BlockSpec((tr, D), lambda i: (i, 0))
    return _pcall(
        body, deps, name=name, grid=(T // tr,),
        in_specs=[row, row, pl.BlockSpec((1, D), lambda i: (0, 0)), row],
        out_specs=[row, row, pl.BlockSpec((SUBLANES, D), lambda i: (0, 0))],
        out_shape=[jax.ShapeDtypeStruct((T, D), F32), jax.ShapeDtypeStruct((T, D), BF16),
                   jax.ShapeDtypeStruct((SUBLANES, D), F32)],
        compiler_params=_params(("arbitrary",)),
    )(dyn, x, g, dres)


def _mm_nn_rms(name, a, b, res, g, tm=512, deps=()):
    M, K = a.shape
    N = b.shape[1]
    tm = _tile(M, tm)

    def epilogue(acc, extra, outs):
        h = acc + extra[0][...]
        outs[0][...] = h
        outs[1][...] = (h * lax.rsqrt(_mean_last(h * h) + EPS) * extra[1][...]).astype(BF16)

    row = pl.BlockSpec((tm, N), lambda i, j, k: (i, 0))
    return _mm(name, a, b, _NN, (M // tm, 1, 1), pl.BlockSpec((tm, K), lambda i, j, k: (i, 0)),
               pl.BlockSpec((K, N), lambda i, j, k: (0, 0)),
               [jax.ShapeDtypeStruct((M, N), F32), jax.ShapeDtypeStruct((M, N), BF16)], [row, row], (tm, N), epilogue,
               (res, g), [row, pl.BlockSpec((1, N), lambda i, j, k: (0, 0))], deps)


def _mm_nt_rms_bwd(name, a, b, x, g, dres, tm=256, deps=()):
    M, C = a.shape
    N = b.shape[0]
    tm = _tile(M, tm)

    def epilogue(dy, extra, outs):
        x_ref, dr_ref, g_ref = extra
        xv = x_ref[...]
        r = lax.rsqrt(_mean_last(xv * xv) + EPS)
        xn = xv * r
        dxn = dy * g_ref[...]
        dx = dr_ref[...] + r * (dxn - xn * _mean_last(dxn * xn))
        outs[0][...] = dx
        outs[1][...] = dx.astype(BF16)
        _accumulate(outs[2], _rows_to_sublanes(dy * xn), pl.program_id(0) == 0)

    row = pl.BlockSpec((tm, N), lambda i, j, k: (i, 0))
    return _mm(name, a, b, _NT, (M // tm, 1, 1), pl.BlockSpec((tm, C), lambda i, j, k: (i, 0)),
               pl.BlockSpec((N, C), lambda i, j, k: (0, 0)),
               [jax.ShapeDtypeStruct((M, N), F32), jax.ShapeDtypeStruct((M, N), BF16),
                jax.ShapeDtypeStruct((SUBLANES, N), F32)],
               [row, row, pl.BlockSpec((SUBLANES, N), lambda i, j, k: (0, 0))], (tm, N), epilogue,
               (x, dres, g), [row, row, pl.BlockSpec((1, N), lambda i, j, k: (0, 0))], deps,
               semantics=("arbitrary", "arbitrary", "arbitrary"))


def _tail(h2, z, pe, target, g_ple, g_final, tr=256):
    T, D = h2.shape
    tr = _tile(T, tr)

    def body(h2_ref, z_ref, pe_ref, t_ref, gp_ref, gf_ref,
             dh3_ref, dz_ref, dpe_ref, dgf_ref, dgp_ref, loss_ref):
        first = pl.program_id(0) == 0
        pev = pe_ref[...]
        r3 = lax.rsqrt(_mean_last(pev * pev) + EPS)
        en = pev * r3
        e = en * gp_ref[...]
        gate = 1.0 / (1.0 + jnp.exp(-z_ref[...]))
        h3 = h2_ref[...] + gate * e
        r5 = lax.rsqrt(_mean_last(h3 * h3) + EPS)
        hn = h3 * r5
        diff = hn * gf_ref[...] - t_ref[...]
        loss_rows = 0.5 * _mean_last(diff * diff)
        row0 = lax.broadcasted_iota(jnp.int32, (SUBLANES, LANES), 0) == 0
        _accumulate(loss_ref, jnp.where(row0, jnp.sum(loss_rows), 0.0), first)
        dy = diff * (1.0 / D)
        _accumulate(dgf_ref, _rows_to_sublanes(dy * hn), first)
        dhn = dy * gf_ref[...]
        dh3 = r5 * (dhn - hn * _mean_last(dhn * hn))
        dh3_ref[...] = dh3
        dgate = dh3 * e
        de = dh3 * gate
        dz_ref[...] = (dgate * gate * (1.0 - gate)).astype(BF16)
        _accumulate(dgp_ref, _rows_to_sublanes(de * en), first)
        den = de * gp_ref[...]
        dpe_ref[...] = (r3 * (den - en * _mean_last(den * en))).astype(BF16)

    row = pl.BlockSpec((tr, D), lambda i: (i, 0))
    vec = pl.BlockSpec((1, D), lambda i: (0, 0))
    part = pl.BlockSpec((SUBLANES, D), lambda i: (0, 0))
    return pl.pallas_call(
        body, name="tail", grid=(T // tr,),
        in_specs=[row, row, row, row, vec, vec],
        out_specs=[row, row, row, part, part, pl.BlockSpec((SUBLANES, LANES), lambda i: (0, 0))],
        out_shape=[jax.ShapeDtypeStruct((T, D), F32), jax.ShapeDtypeStruct((T, D), BF16),
                   jax.ShapeDtypeStruct((T, D), BF16), jax.ShapeDtypeStruct((SUBLANES, D), F32),
                   jax.ShapeDtypeStruct((SUBLANES, D), F32), jax.ShapeDtypeStruct((SUBLANES, LANES), F32)],
        compiler_params=_params(("arbitrary",)),
    )(h2, z, pe, target, g_ple, g_final)


def _rope_tables(T):
    pos = np.arange(T)
    half = HEAD_DIM // 2
    inv = (ROPE_THETA ** (-np.arange(0, half, 2, dtype=np.float32) / half)).astype(np.float32)
    ang_r = (pos // GRID_W).astype(np.float32)[:, None] * inv
    ang_c = (pos % GRID_W).astype(np.float32)[:, None] * inv
    cos = np.concatenate([np.cos(ang_r), np.cos(ang_r), np.cos(ang_c), np.cos(ang_c)], axis=-1)
    sin = np.concatenate([-np.sin(ang_r), np.sin(ang_r), -np.sin(ang_c), np.sin(ang_c)], axis=-1)
    return jnp.asarray(cos, F32), jnp.asarray(sin, F32)


def _swap32(x):
    lane = lax.broadcasted_iota(jnp.int32, x.shape, 1)
    return jnp.where((lane % 64) < 32, pltpu.roll(x, 96, 1), pltpu.roll(x, 32, 1))


def _qk_prep(proj, cos, sin, g_q, g_k, n_norm, tr=256):
    T, W = proj.shape
    tr = _tile(T, tr)
    n_q = n_norm * GROUP // (GROUP + 1)

    def body(p_ref, c_ref, s_ref, gq_ref, gk_ref, o_ref):
        c, s = c_ref[...], s_ref[...]
        for h in range(n_norm):
            cols = slice(h * HEAD_DIM, (h + 1) * HEAD_DIM)
            xv = p_ref[:, cols]
            g = gq_ref[...] if h < n_q else gk_ref[...]
            xn = xv * lax.rsqrt(_mean_last(xv * xv) + EPS) * g
            o_ref[:, cols] = (xn * c + _swap32(xn) * s).astype(BF16)
        rest = slice(n_norm * HEAD_DIM, W)
        o_ref[:, rest] = p_ref[:, rest].astype(BF16)

    row = pl.BlockSpec((tr, W), lambda i: (i, 0))
    tab = pl.BlockSpec((tr, HEAD_DIM), lambda i: (i, 0))
    vec = pl.BlockSpec((1, HEAD_DIM), lambda i: (0, 0))
    return pl.pallas_call(
        body, name="qk_prep", grid=(T // tr,),
        in_specs=[row, tab, tab, vec, vec], out_specs=row,
        out_shape=jax.ShapeDtypeStruct((T, W), BF16),
        compiler_params=_params(("parallel",)),
    )(proj, cos, sin, g_q, g_k)


def _dproj(proj, dqa, dka, dva, dqb, dkb, dvb, cos, sin, g_q, g_k, tr=256):
    T, W = proj.shape
    tr = _tile(T, tr)
    n_q = dqa.shape[1] // HEAD_DIM
    n_kv = dka.shape[1] // HEAD_DIM
    wa = (n_q + n_kv) * HEAD_DIM

    def body(p_ref, dqa_ref, dka_ref, dva_ref, dqb_ref, dkb_ref, dvb_ref, c_ref, s_ref, gq_ref, gk_ref,
             o_ref, dgq_ref, dgk_ref):
        c, s = c_ref[...], s_ref[...]
        dgq = jnp.zeros((SUBLANES, HEAD_DIM), F32)
        dgk = jnp.zeros((SUBLANES, HEAD_DIM), F32)
        for h in range(n_q + n_kv):
            cols = slice(h * HEAD_DIM, (h + 1) * HEAD_DIM)
            xv = p_ref[:, cols]
            r = lax.rsqrt(_mean_last(xv * xv) + EPS)
            xn = xv * r
            if h < n_q:
                d = dqa_ref[:, cols]
                g = gq_ref[...]
            else:
                d = dka_ref[:, (h - n_q) * HEAD_DIM:(h - n_q + 1) * HEAD_DIM]
                g = gk_ref[...]
            dqn = d * c + _swap32(d * s)
            part = _rows_to_sublanes(dqn * xn)
            if h < n_q:
                dgq = dgq + part
            else:
                dgk = dgk + part
            dxn = dqn * g
            o_ref[:, cols] = (r * (dxn - xn * _mean_last(dxn * xn))).astype(BF16)
        off = wa
        for ref in (dva_ref, dqb_ref, dkb_ref, dvb_ref):
            w = ref.shape[1]
            o_ref[:, off:off + w] = ref[...].astype(BF16)
            off += w
        first = pl.program_id(0) == 0
        _accumulate(dgq_ref, dgq, first)
        _accumulate(dgk_ref, dgk, first)

    def row(w):
        return pl.BlockSpec((tr, w), lambda i: (i, 0))

    vec = pl.BlockSpec((1, HEAD_DIM), lambda i: (0, 0))
    part = pl.BlockSpec((SUBLANES, HEAD_DIM), lambda i: (0, 0))
    return pl.pallas_call(
        body, name="dproj", grid=(T // tr,),
        in_specs=[row(wa), row(dqa.shape[1]), row(dka.shape[1]), row(dva.shape[1]), row(dqb.shape[1]),
                  row(dkb.shape[1]), row(dvb.shape[1]), row(HEAD_DIM), row(HEAD_DIM), vec, vec],
        out_specs=[row(W), part, part],
        out_shape=[jax.ShapeDtypeStruct((T, W), BF16), jax.ShapeDtypeStruct((SUBLANES, HEAD_DIM), F32),
                   jax.ShapeDtypeStruct((SUBLANES, HEAD_DIM), F32)],
        compiler_params=_params(("arbitrary",)),
    )(proj, dqa, dka, dva, dqb, dkb, dvb, cos, sin, g_q, g_k)


def _attn_a_fwd(pb, n_q, n_kv, out_heads, tq=1024, tc=1024):
    T = pb.shape[0]
    tq, tc = _tile(T, tq), _tile(T, tc)
    scale = HEAD_DIM ** -0.5
    c = scale * LOG2E

    def body(q_ref, k_ref, v_ref, o_ref, lse_ref):
        q = q_ref[...]
        m = l = acc = None
        for j in range(T // tc):
            keys = slice(j * tc, (j + 1) * tc)
            s = lax.dot_general(q, k_ref[keys, :], _NT, preferred_element_type=F32)
            mj = jnp.max(s, axis=-1, keepdims=True)
            m_new = mj if j == 0 else jnp.maximum(m, mj)
            p = jnp.exp2((s - m_new) * c)
            pv = lax.dot_general(p.astype(BF16), v_ref[keys, :], _NN, preferred_element_type=F32)
            if j == 0:
                l, acc = jnp.sum(p, axis=-1, keepdims=True), pv
            else:
                alpha = jnp.exp2((m - m_new) * c)
                l = alpha * l + jnp.sum(p, axis=-1, keepdims=True)
                acc = alpha * acc + pv
            m = m_new
        o_ref[...] = (acc / l).astype(BF16)
        lse_ref[...] = m * scale + jnp.log(l)

    return pl.pallas_call(
        body, name="attn_a_fwd", grid=(n_kv, GROUP, T // tq),
        in_specs=[pl.BlockSpec((tq, HEAD_DIM), lambda kv, g, i: (i, kv * GROUP + g)),
                  pl.BlockSpec((T, HEAD_DIM), lambda kv, g, i: (0, n_q + kv)),
                  pl.BlockSpec((T, HEAD_DIM), lambda kv, g, i: (0, n_q + n_kv + kv))],
        out_specs=[pl.BlockSpec((tq, HEAD_DIM), lambda kv, g, i: (i, kv * GROUP + g)),
                   pl.BlockSpec((None, tq, 1), lambda kv, g, i: (kv * GROUP + g, i, 0))],
        out_shape=[jax.ShapeDtypeStruct((T, out_heads * HEAD_DIM), BF16), jax.ShapeDtypeStruct((n_q, T, 1), F32)],
        compiler_params=_params(("parallel", "parallel", "parallel")),
    )(pb, pb, pb)


def _attn_a_bwd(pb, o_cat, d_o, lse, n_q, n_kv, tq=512, tc=512):
    T = pb.shape[0]
    tq, tc = _tile(T, tq), _tile(T, tc)
    scale = HEAD_DIM ** -0.5
    c = scale * LOG2E

    def body(q_ref, k_ref, v_ref, o_ref, do_ref, lse_ref, dq_ref, dkt_ref, dvt_ref):
        q, do = q_ref[...], do_ref[...]
        qt, dot = q.T, do.T
        delta = jnp.sum(do.astype(F32) * o_ref[...].astype(F32), axis=-1, keepdims=True)
        lse2 = lse_ref[...] * LOG2E

        @pl.when(jnp.logical_and(pl.program_id(1) == 0, pl.program_id(2) == 0))
        def _():
            dkt_ref[...] = jnp.zeros(dkt_ref.shape, F32)
            dvt_ref[...] = jnp.zeros(dvt_ref.shape, F32)

        dq = None
        for j in range(T // tc):
            keys = slice(j * tc, (j + 1) * tc)
            kc, vc = k_ref[keys, :], v_ref[keys, :]
            s = lax.dot_general(q, kc, _NT, preferred_element_type=F32)
            p = jnp.exp2(s * c - lse2)
            dp = lax.dot_general(do, vc, _NT, preferred_element_type=F32)
            ds = (p * (dp - delta) * scale).astype(BF16)
            dqj = lax.dot_general(ds, kc, _NN, preferred_element_type=F32)
            dq = dqj if dq is None else dq + dqj
            dvt_ref[:, keys] += lax.dot_general(dot, p.astype(BF16), _NN, preferred_element_type=F32)
            dkt_ref[:, keys] += lax.dot_general(qt, ds, _NN, preferred_element_type=F32)
        dq_ref[...] = dq

    qmap = lambda kv, g, i: (i, kv * GROUP + g)
    return pl.pallas_call(
        body, name="attn_a_bwd", grid=(n_kv, GROUP, T // tq),
        in_specs=[pl.BlockSpec((tq, HEAD_DIM), qmap),
                  pl.BlockSpec((T, HEAD_DIM), lambda kv, g, i: (0, n_q + kv)),
                  pl.BlockSpec((T, HEAD_DIM), lambda kv, g, i: (0, n_q + n_kv + kv)),
                  pl.BlockSpec((tq, HEAD_DIM), qmap),
                  pl.BlockSpec((tq, HEAD_DIM), qmap),
                  pl.BlockSpec((None, tq, 1), lambda kv, g, i: (kv * GROUP + g, i, 0))],
        out_specs=[pl.BlockSpec((tq, HEAD_DIM), qmap),
                   pl.BlockSpec((HEAD_DIM, T), lambda kv, g, i: (kv, 0)),
                   pl.BlockSpec((HEAD_DIM, T), lambda kv, g, i: (kv, 0))],
        out_shape=[jax.ShapeDtypeStruct((T, n_q * HEAD_DIM), F32),
                   jax.ShapeDtypeStruct((n_kv * HEAD_DIM, T), F32),
                   jax.ShapeDtypeStruct((n_kv * HEAD_DIM, T), F32)],
        compiler_params=_params(("parallel", "arbitrary", "arbitrary")),
    )(pb, pb, pb, o_cat, d_o, lse)


def _bucket_index():
    r = np.arange(BLOCK_Q)[:, None]
    j = np.arange(3 * BLOCK_Q)[None, :]
    rel = (j - BLOCK_Q) - r
    nb = N_BUCKETS // 2
    ret = np.where(rel > 0, nb, 0)
    n = np.abs(rel)
    max_exact = nb // 2
    nf = np.maximum(n, 1).astype(np.float32)
    large = max_exact + (np.log(nf / max_exact) / math.log(MAX_DISTANCE / max_exact) * (nb - max_exact)).astype(np.int32)
    large = np.minimum(large, nb - 1)
    return jnp.asarray(ret + np.where(n < max_exact, n, large), jnp.int32)


def _bias_build(idx, table_flat, n_heads, deps=()):
    def body(idx_ref, tab_ref, o_ref):
        h = pl.program_id(0)
        iv = idx_ref[...]
        acc = jnp.zeros(iv.shape, F32)
        for b in range(N_BUCKETS):
            acc = jnp.where(iv == b, tab_ref[b * n_heads + h], acc)
        r = lax.broadcasted_iota(jnp.int32, iv.shape, 0)
        j = lax.broadcasted_iota(jnp.int32, iv.shape, 1)
        o_ref[...] = jnp.where(jnp.abs(j - BLOCK_Q - r) <= WINDOW, acc, NEG_INF)

    return _pcall(
        body, deps, name="bias_build", grid=(n_heads,),
        in_specs=[pl.BlockSpec(idx.shape, lambda h: (0, 0)), pl.BlockSpec(memory_space=pltpu.SMEM)],
        out_specs=pl.BlockSpec((None,) + idx.shape, lambda h: (h, 0, 0)),
        out_shape=jax.ShapeDtypeStruct((n_heads,) + idx.shape, F32),
        compiler_params=_params(("parallel",)),
    )(idx, table_flat)


def _in_sequence(n, T):
    j = lax.broadcasted_iota(jnp.int32, (BLOCK_Q, 3 * BLOCK_Q), 1)
    kabs = n * BLOCK_Q + j - BLOCK_Q
    return (kabs >= 0) & (kabs < T)


def _band_specs(col, nblk, sb):
    return [pl.BlockSpec((BLOCK_Q, HEAD_DIM), lambda kv, i: (jnp.maximum(sb * i - 1, 0), col(kv))),
            pl.BlockSpec((sb * BLOCK_Q, HEAD_DIM), lambda kv, i: (i, col(kv))),
            pl.BlockSpec((BLOCK_Q, HEAD_DIM), lambda kv, i: (jnp.minimum(sb * i + sb, nblk - 1), col(kv)))]


def _head_specs(base, rows):
    return [pl.BlockSpec((rows, HEAD_DIM), functools.partial(lambda kv, i, g: (i, base + kv * GROUP + g), g=g))
            for g in range(GROUP)]


def _attn_b_fwd(pb, bias, sink, o_all, q_off, n_q, n_kv, deps=(), sb=8):
    T = pb.shape[0]
    nblk = T // BLOCK_Q
    sb = min(sb, nblk)
    tq = sb * BLOCK_Q
    scale = HEAD_DIM ** -0.5

    def body(*refs):
        q_refs = refs[0:GROUP]
        k_refs, v_refs = refs[GROUP:GROUP + 3], refs[GROUP + 3:GROUP + 6]
        bias_ref, sink_ref, o_ref, lse_ref = refs[GROUP + 6:]
        kv, i = pl.program_id(0), pl.program_id(1)
        kb = jnp.concatenate([r[...] for r in k_refs], axis=0)
        vb = jnp.concatenate([r[...] for r in v_refs], axis=0)
        for b in range(sb):
            at_end = b == 0 or b == sb - 1
            mask = _in_sequence(i * sb + b, T) if at_end else None
            rows = slice(b * BLOCK_Q, (b + 1) * BLOCK_Q)
            kw, vw = kb[b * BLOCK_Q:(b + 3) * BLOCK_Q], vb[b * BLOCK_Q:(b + 3) * BLOCK_Q]
            for g in range(GROUP):
                sk = sink_ref[kv * GROUP + g]
                s = lax.dot_general(q_refs[g][rows, :], kw, _NT, preferred_element_type=F32) * scale + bias_ref[g]
                if at_end:
                    s = jnp.where(mask, s, NEG_INF)
                m = jnp.maximum(jnp.max(s, axis=-1, keepdims=True), sk)
                p = jnp.exp(s - m)
                l = jnp.sum(p, axis=-1, keepdims=True) + jnp.exp(sk - m)
                o = lax.dot_general(p.astype(BF16), vw, _NN, preferred_element_type=F32)
                o_ref[rows, g * HEAD_DIM:(g + 1) * HEAD_DIM] = (o / l).astype(BF16)
                lse_ref[g, rows, :] = m + jnp.log(l)

    first_group = o_all.shape[1] // (GROUP * HEAD_DIM) - n_kv
    return _pcall(
        body, deps, into=(o_all, 0), name="attn_b_fwd", grid=(n_kv, nblk // sb),
        in_specs=[*_head_specs(q_off, tq),
                  *_band_specs(lambda kv: q_off + n_q + kv, nblk, sb),
                  *_band_specs(lambda kv: q_off + n_q + n_kv + kv, nblk, sb),
                  pl.BlockSpec((GROUP, BLOCK_Q, 3 * BLOCK_Q), lambda kv, i: (kv, 0, 0)),
                  pl.BlockSpec(memory_space=pltpu.SMEM)],
        out_specs=[pl.BlockSpec((tq, GROUP * HEAD_DIM), lambda kv, i: (i, first_group + kv)),
                   pl.BlockSpec((GROUP, tq, 1), lambda kv, i: (kv, i, 0))],
        out_shape=[jax.ShapeDtypeStruct(o_all.shape, BF16), jax.ShapeDtypeStruct((n_q, T, 1), F32)],
        compiler_params=_params(("parallel", "parallel")),
    )(*([pb] * (GROUP + 6)), bias, sink)


def _attn_b_bwd(pb, o_cat, d_o, lse, bias, sink, q_off, n_q, n_kv, o_off, deps=(), sb=8):
    T = pb.shape[0]
    nblk = T // BLOCK_Q
    sb = min(sb, nblk)
    tq = sb * BLOCK_Q
    scale = HEAD_DIM ** -0.5

    def body(*refs):
        q_refs = refs[0:GROUP]
        k_refs, v_refs = refs[GROUP:GROUP + 3], refs[GROUP + 3:GROUP + 6]
        o_refs, do_refs = refs[GROUP + 6:2 * GROUP + 6], refs[2 * GROUP + 6:3 * GROUP + 6]
        lse_ref, bias_ref, sink_ref, dq_ref, dk_ref, dv_ref, dbias_ref, dsink_ref, dkb_ref, dvb_ref = refs[3 * GROUP + 6:]
        kv, i = pl.program_id(0), pl.program_id(1)
        first = i == 0

        @pl.when(first)
        def _():
            dk_ref[...] = jnp.zeros(dk_ref.shape, F32)
            dv_ref[...] = jnp.zeros(dv_ref.shape, F32)
            dbias_ref[...] = jnp.zeros(dbias_ref.shape, F32)

        kb = jnp.concatenate([r[...] for r in k_refs], axis=0)
        vb = jnp.concatenate([r[...] for r in v_refs], axis=0)
        dkb_ref[...] = jnp.zeros(dkb_ref.shape, F32)
        dvb_ref[...] = jnp.zeros(dvb_ref.shape, F32)
        row = lax.broadcasted_iota(jnp.int32, (SUBLANES, LANES), 0)
        dsink = jnp.zeros((SUBLANES, LANES), F32)
        for b in range(sb):
            at_end = b == 0 or b == sb - 1
            mask = _in_sequence(i * sb + b, T) if at_end else None
            rows = slice(b * BLOCK_Q, (b + 1) * BLOCK_Q)
            win = slice(b * BLOCK_Q, (b + 3) * BLOCK_Q)
            kw, vw = kb[win], vb[win]
            dkw = jnp.zeros((3 * BLOCK_Q, HEAD_DIM), F32)
            dvw = jnp.zeros((3 * BLOCK_Q, HEAD_DIM), F32)
            for g in range(GROUP):
                sk = sink_ref[kv * GROUP + g]
                q, do = q_refs[g][rows, :], do_refs[g][rows, :]
                lse_g = lse_ref[g, rows, :]
                delta = jnp.sum(do.astype(F32) * o_refs[g][rows, :].astype(F32), axis=-1, keepdims=True)
                s = lax.dot_general(q, kw, _NT, preferred_element_type=F32) * scale + bias_ref[g]
                if at_end:
                    s = jnp.where(mask, s, NEG_INF)
                p = jnp.exp(s - lse_g)
                dp = lax.dot_general(do, vw, _NT, preferred_element_type=F32)
                ds = p * (dp - delta)
                dbias_ref[g] += ds
                dsink = dsink + jnp.where(row == g, -jnp.sum(jnp.exp(sk - lse_g) * delta), 0.0)
                dsb = (ds * scale).astype(BF16)
                dq_ref[rows, g * HEAD_DIM:(g + 1) * HEAD_DIM] = lax.dot_general(dsb, kw, _NN, preferred_element_type=F32)
                dkw = dkw + lax.dot_general(dsb, q, _TN, preferred_element_type=F32)
                dvw = dvw + lax.dot_general(p.astype(BF16), do, _TN, preferred_element_type=F32)
            dkb_ref[win, :] += dkw
            dvb_ref[win, :] += dvw
        _accumulate(dsink_ref, dsink, first)

        before = pl.ds(pl.multiple_of(jnp.maximum(sb * i - 1, 0) * BLOCK_Q, BLOCK_Q), BLOCK_Q)
        own = pl.ds(pl.multiple_of(i * tq, BLOCK_Q), tq)
        after = pl.ds(pl.multiple_of(jnp.minimum(sb * i + sb, nblk - 1) * BLOCK_Q, BLOCK_Q), BLOCK_Q)
        for acc_ref, band_ref in ((dk_ref, dkb_ref), (dv_ref, dvb_ref)):
            acc_ref[before, :] += band_ref[0:BLOCK_Q, :]
            acc_ref[own, :] += band_ref[BLOCK_Q:BLOCK_Q + tq, :]
            acc_ref[after, :] += band_ref[BLOCK_Q + tq:, :]

    return _pcall(
        body, deps, name="attn_b_bwd", grid=(n_kv, nblk // sb),
        in_specs=[*_head_specs(q_off, tq),
                  *_band_specs(lambda kv: q_off + n_q + kv, nblk, sb),
                  *_band_specs(lambda kv: q_off + n_q + n_kv + kv, nblk, sb),
                  *_head_specs(o_off, tq), *_head_specs(o_off, tq),
                  pl.BlockSpec((GROUP, tq, 1), lambda kv, i: (kv, i, 0)),
                  pl.BlockSpec((GROUP, BLOCK_Q, 3 * BLOCK_Q), lambda kv, i: (kv, 0, 0)),
                  pl.BlockSpec(memory_space=pltpu.SMEM)],
        out_specs=[pl.BlockSpec((tq, GROUP * HEAD_DIM), lambda kv, i: (i, kv)),
                   pl.BlockSpec((T, HEAD_DIM), lambda kv, i: (0, kv)),
                   pl.BlockSpec((T, HEAD_DIM), lambda kv, i: (0, kv)),
                   pl.BlockSpec((GROUP, BLOCK_Q, 3 * BLOCK_Q), lambda kv, i: (kv, 0, 0)),
                   pl.BlockSpec((None, SUBLANES, LANES), lambda kv, i: (kv, 0, 0))],
        out_shape=[jax.ShapeDtypeStruct((T, n_q * HEAD_DIM), F32),
                   jax.ShapeDtypeStruct((T, n_kv * HEAD_DIM), F32),
                   jax.ShapeDtypeStruct((T, n_kv * HEAD_DIM), F32),
                   jax.ShapeDtypeStruct((n_q, BLOCK_Q, 3 * BLOCK_Q), F32),
                   jax.ShapeDtypeStruct((n_kv, SUBLANES, LANES), F32)],
        scratch_shapes=[pltpu.VMEM((tq + 2 * BLOCK_Q, HEAD_DIM), F32), pltpu.VMEM((tq + 2 * BLOCK_Q, HEAD_DIM), F32)],
        compiler_params=_params(("parallel", "arbitrary")),
    )(*([pb] * (GROUP + 6)), *([o_cat] * GROUP), *([d_o] * GROUP), lse, bias, sink)


def _table_grads(dbias, dsink_raw, idx):
    n_heads = dbias.shape[0]
    n_kv = dsink_raw.shape[0]

    def body(db_ref, ds_ref, idx_ref, dt_ref, dsk_ref):
        iv = idx_ref[...]
        row = lax.broadcasted_iota(jnp.int32, (SUBLANES, LANES), 0)
        lane = lax.broadcasted_iota(jnp.int32, (SUBLANES, LANES), 1)
        dsk = jnp.zeros((SUBLANES, LANES), F32)
        for h in range(n_heads):
            d = db_ref[h]
            acc = jnp.zeros((SUBLANES, LANES), F32)
            for b in range(N_BUCKETS):
                acc = jnp.where((row == 0) & (lane == b), jnp.sum(jnp.where(iv == b, d, 0.0)), acc)
            dt_ref[:, h * LANES:(h + 1) * LANES] = acc
            raw = ds_ref[h // GROUP]
            val = jnp.sum(jnp.where((row == h % GROUP) & (lane == 0), raw, 0.0))
            dsk = jnp.where((row == 0) & (lane == h), val, dsk)
        dsk_ref[...] = dsk

    return pl.pallas_call(
        body, name="table_grads",
        in_specs=[pl.BlockSpec(memory_space=pltpu.VMEM)] * 3,
        out_specs=[pl.BlockSpec(memory_space=pltpu.VMEM)] * 2,
        out_shape=[jax.ShapeDtypeStruct((SUBLANES, n_heads * LANES), F32),
                   jax.ShapeDtypeStruct((SUBLANES, LANES), F32)],
        compiler_params=pltpu.CompilerParams(vmem_limit_bytes=56 * 1024 * 1024),
    )(dbias, dsink_raw, idx)


def _position():
    x, y, c = lax.axis_index("x"), lax.axis_index("y"), lax.axis_index("c")
    return x, y, c


def _hbm(a):
    return pltpu.with_memory_space_constraint(a, pltpu.HBM)


def _split_start(name, bufs, sem_shapes, issue):
    nb, ns = len(bufs), len(sem_shapes)

    def body(*refs):
        buf_refs = refs[:nb]
        sems = refs[nb:nb + ns]
        token = refs[nb + ns + nb]
        issue(buf_refs, sems)
        token[...] = jnp.zeros(token.shape, F32)

    outs = pl.pallas_call(
        body, name=name,
        in_specs=[_HBM] * nb,
        out_specs=[_SEM] * ns + [_HBM] * nb + [_VMEM],
        out_shape=[pltpu.SemaphoreType.DMA(s) for s in sem_shapes] + [pltpu.HBM(b.shape, b.dtype) for b in bufs]
        + [jax.ShapeDtypeStruct((SUBLANES, LANES), F32)],
        input_output_aliases={i: ns + i for i in range(nb)},
        compiler_params=pltpu.CompilerParams(has_side_effects=_EFFECT),
    )(*[_hbm(b) for b in bufs])
    return outs[:ns], outs[ns:ns + nb], outs[-1]


def _split_wait(name, bufs, send, recv, counts, size_of, after):
    nb = len(bufs)

    def body(*refs):
        buf_refs = refs[:nb]
        send_ref, recv_ref = refs[nb], refs[nb + 1]
        x, y, c = _position()
        for w, n in enumerate(counts):
            ref = size_of(buf_refs, w)
            for k in range(n):
                s = sum(counts[:w]) + k
                cp = pltpu.make_async_remote_copy(
                    src_ref=ref, dst_ref=ref, send_sem=send_ref.at[s], recv_sem=recv_ref.at[s],
                    device_id=(x, y, c), device_id_type=MESH)
                cp.wait_send()
                cp.wait_recv()

    return pl.pallas_call(
        body, name=name,
        in_specs=[_HBM] * nb + [_SEM, _SEM, _ANY],
        out_specs=[_HBM] * nb,
        out_shape=[pltpu.HBM(b.shape, b.dtype) for b in bufs],
        input_output_aliases={i: i for i in range(nb)},
        compiler_params=pltpu.CompilerParams(has_side_effects=_EFFECT),
    )(*bufs, send, recv, after)


def _block_of(pos):
    return 4 * pos[0] + 2 * pos[1] + pos[2]


def _shard_of(ref, blk, by_cols):
    aligned = (lambda v, a: v) if isinstance(blk, int) else pl.multiple_of
    if by_cols:
        n = ref.shape[1] // N_DEV
        return ref.at[:, pl.ds(aligned(blk * n, LANES), n)]
    r = ref.shape[0] // N_DEV
    return ref.at[pl.ds(aligned(blk * r, SUBLANES), r), :]


def _place_shards(shards, by_cols):
    mine = _block_of(_position()).astype(jnp.int32).reshape(1)

    def place(name, s, cols, tr=256):
        r, n = s.shape
        tr = _tile(r, tr)

        def body(m_ref, s_ref, o_ref):
            o_ref[...] = s_ref[...].astype(BF16)

        if cols:
            out = pl.BlockSpec((tr, n), lambda i, m_ref: (i, m_ref[0]))
        else:
            out = pl.BlockSpec((tr, n), lambda i, m_ref: (m_ref[0] * (r // tr) + i, 0))
        return pl.pallas_call(
            body, name=name,
            grid_spec=pltpu.PrefetchScalarGridSpec(
                num_scalar_prefetch=1, grid=(r // tr,),
                in_specs=[pl.BlockSpec((tr, n), lambda i, m_ref: (i, 0))], out_specs=out),
            out_shape=jax.ShapeDtypeStruct((r, n * N_DEV) if cols else (r * N_DEV, n), BF16),
            compiler_params=_params(("parallel",)),
        )(mine, s)

    return [place("place_shard_%d" % w, s, cols) for w, (s, cols) in enumerate(zip(shards, by_cols))]


def _gather_start(shards, by_cols, groups):
    lands = _place_shards(shards, by_cols)

    def issue(land, sems):
        x, y, c = _position()
        peers = [(x, y, 1 - c), (1 - x, y, c), (x, 1 - y, c), (1 - x, 1 - y, c)]
        for gi, grp in enumerate(groups):
            for wi, w in enumerate(grp):
                own = _shard_of(land[w], _block_of((x, y, c)), by_cols[w])
                for k, peer in enumerate(peers):
                    pltpu.make_async_remote_copy(
                        src_ref=own, dst_ref=own, send_sem=sems[2 * gi].at[4 * wi + k],
                        recv_sem=sems[2 * gi + 1].at[4 * wi + k], device_id=peer, device_id_type=MESH).start()

    sem_shapes = [(4 * len(g),) for g in groups for _ in range(2)]
    return _split_start("gather_start", lands, sem_shapes, issue)


def _gather_forward(name, lands, by_cols):
    nw = len(lands)

    def issue(land, sems):
        x, y, c = _position()
        for w in range(nw):
            for k, chip in enumerate([(1 - x, y), (x, 1 - y), (1 - x, 1 - y)]):
                blk = _shard_of(land[w], _block_of((*chip, c)), by_cols[w])
                pltpu.make_async_remote_copy(
                    src_ref=blk, dst_ref=blk, send_sem=sems[0].at[3 * w + k], recv_sem=sems[1].at[3 * w + k],
                    device_id=(x, y, 1 - c), device_id_type=MESH).start()

    return _split_start(name, lands, [(3 * nw,), (3 * nw,)], issue)


def _first_block(bufs, w, offset=0):
    return bufs[offset + w].at[0]


_PEER_FLIPS = ((0, 0, 1), (1, 0, 0), (1, 0, 1), (0, 1, 0), (0, 1, 1), (1, 1, 0), (1, 1, 1))


def _scatter_start(name, grads, by_cols):
    nw = len(grads)
    lands = []
    for g, cols in zip(grads, by_cols):
        shard = (g.shape[0], g.shape[1] // N_DEV) if cols else (g.shape[0] // N_DEV, g.shape[1])
        lands.append(lax.empty((N_DEV,) + shard, g.dtype))

    def issue(bufs, sems):
        x, y, c = _position()
        flip = lambda v, f: 1 - v if f else v
        for w in range(nw):
            for k, (fx, fy, fc) in enumerate(_PEER_FLIPS):
                peer = (flip(x, fx), flip(y, fy), flip(c, fc))
                pltpu.make_async_remote_copy(
                    src_ref=_shard_of(bufs[w], _block_of(peer), by_cols[w]), dst_ref=bufs[nw + w].at[_block_of((x, y, c))],
                    send_sem=sems[0].at[7 * w + k], recv_sem=sems[1].at[7 * w + k],
                    device_id=peer, device_id_type=MESH).start()

    return _split_start(name, list(grads) + lands, [(7 * nw,), (7 * nw,)], issue)


def _adam(w, g, m, v):
    m = ADAM_B1 * m + (1.0 - ADAM_B1) * g
    v = ADAM_B2 * v + (1.0 - ADAM_B2) * (g * g)
    m_hat = m / (1.0 - ADAM_B1 ** ADAM_STEP)
    v_hat = v / (1.0 - ADAM_B2 ** ADAM_STEP)
    delta = -ADAM_LR * (m_hat / (jnp.sqrt(v_hat) + ADAM_EPS) + ADAM_WD * w)
    return delta, m, v


def _sum_adam(name, landed, grad, by_cols, w, m, v, tr=256):
    R, C = w.shape
    tr = _tile(R, tr)
    mine = _block_of(_position()).astype(jnp.int32).reshape(1)

    def body(me_ref, l_ref, own_ref, w_ref, m_ref, v_ref, g_ref, d_ref, nm_ref, nv_ref):
        own = own_ref[...].astype(F32)
        g = None
        for d in range(N_DEV):
            part = jnp.where(me_ref[0] == d, own, l_ref[d].astype(F32))
            g = part if g is None else g + part
        g_ref[...] = g
        d_ref[...], nm_ref[...], nv_ref[...] = _adam(w_ref[...], g, m_ref[...], v_ref[...])

    tile = pl.BlockSpec((tr, C), lambda i, me_ref: (i, 0))
    if by_cols:
        own = pl.BlockSpec((tr, C), lambda i, me_ref: (i, me_ref[0]))
    else:
        own = pl.BlockSpec((tr, C), lambda i, me_ref: (me_ref[0] * (R // tr) + i, 0))
    return pl.pallas_call(
        body, name=name,
        grid_spec=pltpu.PrefetchScalarGridSpec(
            num_scalar_prefetch=1, grid=(R // tr,),
            in_specs=[pl.BlockSpec((N_DEV, tr, C), lambda i, me_ref: (0, i, 0)), own, tile, tile, tile],
            out_specs=[tile] * 4),
        out_shape=[jax.ShapeDtypeStruct((R, C), F32)] * 4,
        compiler_params=_params(("parallel",)),
    )(mine, landed, grad, w, m, v)


def _small_all_reduce(parts, deps=()):
    W = parts.shape[1]

    def body(p_ref, o_ref, slots, send_sems, recv_sems):
        x, y, c = _position()
        me = 4 * x + 2 * y + c
        slots[me] = jnp.sum(p_ref[...], axis=0, keepdims=True)
        peers = [(x, y, 1 - c), (1 - x, y, c), (1 - x, y, 1 - c), (x, 1 - y, c), (x, 1 - y, 1 - c),
                 (1 - x, 1 - y, c), (1 - x, 1 - y, 1 - c)]
        copies = []
        for k, peer in enumerate(peers):
            cp = pltpu.make_async_remote_copy(
                src_ref=slots.at[me], dst_ref=slots.at[me], send_sem=send_sems.at[k], recv_sem=recv_sems.at[k],
                device_id=peer, device_id_type=MESH)
            cp.start()
            copies.append(cp)
        for cp in copies:
            cp.wait()
        total = slots[0]
        for d in range(1, N_DEV):
            total = total + slots[d]
        o_ref[...] = total

    return _pcall(
        body, deps, name="small_all_reduce",
        in_specs=[pl.BlockSpec(memory_space=pltpu.VMEM)], out_specs=pl.BlockSpec(memory_space=pltpu.VMEM),
        out_shape=jax.ShapeDtypeStruct((1, W), F32),
        scratch_shapes=[pltpu.VMEM((N_DEV, 1, W), F32), pltpu.SemaphoreType.DMA((7,)), pltpu.SemaphoreType.DMA((7,))],
    )(parts)


def _adam_small(w, g, m, v):
    def body(w_ref, g_ref, m_ref, v_ref, d_ref, nm_ref, nv_ref):
        d_ref[...], nm_ref[...], nv_ref[...] = _adam(w_ref[...], g_ref[...], m_ref[...], v_ref[...])

    return pl.pallas_call(
        body, name="adam_small",
        in_specs=[pl.BlockSpec(memory_space=pltpu.VMEM)] * 4, out_specs=[pl.BlockSpec(memory_space=pltpu.VMEM)] * 3,
        out_shape=[jax.ShapeDtypeStruct(w.shape, F32)] * 3,
    )(w, g, m, v)


_GATHER_GROUPS = (("w_in",), ("w_out", "w_up", "ple_w"), ("w_down", "w_gate"))
_COL_SHARDED = ("w_in", "w_up", "ple_w")


class _MeshComm:
    def __init__(self, w, mom, var):
        self.w, self.mom, self.var = w, mom, var
        self.out = {}
        self._scatters = {}

    def gather_begin(self):
        names = [n for g in _GATHER_GROUPS for n in g]
        self._idx = {n: i for i, n in enumerate(names)}
        groups = [[self._idx[n] for n in g] for g in _GATHER_GROUPS]
        self._sems, self._lands, token = _gather_start(
            [self.w[n] for n in names], [n in _COL_SHARDED for n in names], groups)
        return token

    @staticmethod
    def _shard_size(names, offset):
        return lambda bufs, w: _shard_of(bufs[offset + w], 0, names[w] in _COL_SHARDED)

    def gather_arrive(self, gi, after):
        names = _GATHER_GROUPS[gi]
        ids = [self._idx[n] for n in names]
        self._arrived = _split_wait("gather_arrive%d" % gi, [self._lands[i] for i in ids], self._sems[2 * gi],
                                    self._sems[2 * gi + 1], [4] * len(ids), self._shard_size(names, 0), after)

    def gather_forward(self, gi):
        by_cols = [n in _COL_SHARDED for n in _GATHER_GROUPS[gi]]
        self._fsems, self._fthru, token = _gather_forward("gather_forward%d" % gi, self._arrived, by_cols)
        return token

    def gather_finish(self, gi, after):
        names = _GATHER_GROUPS[gi]
        out = _split_wait("gather_finish%d" % gi, self._fthru, self._fsems[0], self._fsems[1], [3] * len(names),
                          self._shard_size(names, 0), after)
        return dict(zip(names, out))

    def reduce_begin(self, key, grads):
        names = list(grads)
        sems, thru, token = _scatter_start("scatter_start_" + key, [grads[n] for n in names],
                                           [n in _COL_SHARDED for n in names])
        self._scatters[key] = (names, sems, thru)
        return token

    def reduce_finish(self, key, after):
        names, sems, thru = self._scatters[key]
        nw = len(names)
        out = _split_wait("scatter_wait_" + key, thru, sems[0], sems[1], [N_DEV - 1] * nw,
                          functools.partial(_first_block, offset=nw), after)
        for i, n in enumerate(names):
            self.out[n] = _sum_adam("adam_" + n, out[nw + i], out[i], n in _COL_SHARDED, self.w[n], self.mom[n],
                                    self.var[n])


def _step(x, p, target, gains, comm):
    T, D = x.shape
    n_q = D // (2 * HEAD_DIM)
    n_kv = n_q // GROUP
    cos, sin = _rope_tables(T)
    idx = _bucket_index()

    t = comm.gather_begin()
    u = _rms_fwd("norm_attn", x, gains["attn_norm_g"], deps=(t,))
    comm.gather_arrive(0, u)
    t = comm.gather_forward(0)
    bias = _bias_build(idx, gains["rel_bias_table"].reshape(-1), n_q, deps=(t,))
    full = comm.gather_finish(0, bias)
    proj = _mm_nn("in_proj", u, full["w_in"])
    pb = _qk_prep(proj, cos, sin, gains["q_norm_g"], gains["k_norm_g"], n_q + n_kv)
    o_a, lse_a = _attn_a_fwd(pb, n_q, n_kv, 2 * n_q)
    comm.gather_arrive(1, lse_a)
    t = comm.gather_forward(1)
    sink = gains["sink_logits"].reshape(-1)
    b_off = n_q + 2 * n_kv
    o_cat, lse_b = _attn_b_fwd(pb, bias, sink, o_a, b_off, n_q, n_kv, deps=(t,))
    full.update(comm.gather_finish(1, lse_b))
    h1, m_in = _mm_nn_rms("out_proj", o_cat, full["w_out"], x, gains["mlp_norm_g"])

    def up_epilogue(acc, extra, outs):
        outs[0][...] = acc.astype(BF16)
        r = jnp.maximum(acc, 0.0)
        outs[1][...] = (r * r).astype(BF16)

    a_act, f_act = _mm_nn("up_proj", m_in, full["w_up"], epilogue=up_epilogue, out_dtypes=[BF16, BF16], tn=2048)
    comm.gather_arrive(2, f_act)
    t = comm.gather_forward(2)
    p_b = p.astype(BF16)
    pe = _mm_nn("ple_proj", p_b, full["ple_w"], deps=(t,))
    full.update(comm.gather_finish(2, pe))
    h2 = _mm_nn("down_proj", f_act, full["w_down"], epilogue=_store_add, extras=(h1,), tn=256)
    gn = _rms_fwd("norm_gate", h2, gains["gate_norm_g"])
    z = _mm_nn("gate_proj", gn, full["w_gate"])

    dh3, dz, dpe, dg_final, dg_ple, loss_part = _tail(h2, z, pe, target, gains["ple_norm_g"], gains["final_norm_g"])
    gw_gate = _mm_tn("grad_w_gate", gn, dz)
    gw_ple = _mm_tn("grad_ple_w", p_b, dpe)
    t = comm.reduce_begin("a", dict(w_gate=gw_gate, ple_w=gw_ple))
    dh2, dh2_b, dg_gate = _mm_nt_rms_bwd("d_gate_in", dz, full["w_gate"], h2, gains["gate_norm_g"], dh3, deps=(t,))
    gw_down = _mm_tn("grad_w_down", f_act, dh2_b)
    t = comm.reduce_begin("b", dict(w_down=gw_down))

    def act_bwd(acc, extra, outs):
        outs[0][...] = (acc * (2.0 * jnp.maximum(extra[0][...].astype(F32), 0.0))).astype(BF16)

    da = _mm_nt("d_act", dh2_b, full["w_down"], out_dtype=BF16, epilogue=act_bwd, extras=(a_act,), tn=2048, deps=(t,))
    gw_up = _mm_tn("grad_w_up", m_in, da)
    t = comm.reduce_begin("c", dict(w_up=gw_up))
    dm = _mm_nt("d_mlp_in", da, full["w_up"], tn=256, deps=(t,))
    dh1, dh1_b, dg_mlp = _rms_bwd("norm_mlp_bwd", dm, h1, gains["mlp_norm_g"], dh2)
    gw_out = _mm_tn("grad_w_out", o_cat, dh1_b)
    t = comm.reduce_begin("d", dict(w_out=gw_out))
    d_o = _mm_nt("d_attn_out", dh1_b, full["w_out"], out_dtype=BF16, deps=(t,))
    dqa, dka_t, dva_t = _attn_a_bwd(pb, o_cat, d_o, lse_a, n_q, n_kv)
    dka, dva = dka_t.T, dva_t.T
    dqb, dkb, dvb, dbias, dsink_raw = _attn_b_bwd(pb, o_cat, d_o, lse_b, bias, sink, b_off, n_q, n_kv, n_q)
    dtable, dsink = _table_grads(dbias, dsink_raw, idx)
    dproj, dg_q, dg_k = _dproj(proj, dqa, dka, dva, dqb, dkb, dvb, cos, sin, gains["q_norm_g"], gains["k_norm_g"])
    gw_in = _mm_tn("grad_w_in", u, dproj)
    t = comm.reduce_begin("e", dict(w_in=gw_in))
    dx, _, dg_attn = _mm_nt_rms_bwd("d_attn_in", dproj, full["w_in"], x, gains["attn_norm_g"], dh1, deps=(t,))
    for key in "abcd":
        comm.reduce_finish(key, dx)

    parts = jnp.concatenate([dg_attn, dg_mlp, dg_ple, dg_gate, dg_final, dg_q, dg_k, dtable, dsink, loss_part], axis=1)
    return dx, parts


_SHARDED = ("w_in", "w_out", "w_up", "w_down", "ple_w", "w_gate")
_VECTORS = ("attn_norm_g", "mlp_norm_g", "ple_norm_g", "gate_norm_g", "final_norm_g")
_ORDER = ("attn_norm_g", "w_in", "q_norm_g", "k_norm_g", "sink_logits", "w_out", "mlp_norm_g", "w_up", "w_down",
          "ple_w", "ple_norm_g", "gate_norm_g", "w_gate", "rel_bias_table", "final_norm_g")


def _pack_small(vals, n_heads):
    lane_pad = lambda v: jnp.pad(v, ((0, 0), (0, LANES - v.shape[1])))
    table = lane_pad(vals["rel_bias_table"].T).reshape(1, n_heads * LANES)
    return jnp.concatenate(
        [vals[n].reshape(1, -1) for n in _VECTORS] + [vals["q_norm_g"], vals["k_norm_g"], table,
                                                      lane_pad(vals["sink_logits"]), jnp.zeros((1, LANES), F32)], axis=1)


def _unpack_small(row, like, n_heads):
    out, off = {}, 0
    for n in _VECTORS:
        out[n] = row[:, off:off + like[n].size].reshape(like[n].shape)
        off += like[n].size
    for n in ("q_norm_g", "k_norm_g"):
        out[n] = row[:, off:off + LANES]
        off += LANES
    out["rel_bias_table"] = row[:, off:off + n_heads * LANES].reshape(n_heads, LANES)[:, :N_BUCKETS].T
    off += n_heads * LANES
    out["sink_logits"] = row[:, off:off + n_heads]
    off += LANES
    return out, row[0, off]


def kernel(x, p, attn_norm_g, w_in, q_norm_g, k_norm_g, sink_logits, w_out, mlp_norm_g, w_up, w_down, ple_w, ple_norm_g, gate_norm_g, w_gate, rel_bias_table, final_norm_g, loss_target, m_attn_norm_g, m_w_in, m_q_norm_g, m_k_norm_g, m_sink_logits, m_w_out, m_mlp_norm_g, m_w_up, m_w_down, m_ple_w, m_ple_norm_g, m_gate_norm_g, m_w_gate, m_rel_bias_table, m_final_norm_g, v_attn_norm_g, v_w_in, v_q_norm_g, v_k_norm_g, v_sink_logits, v_w_out, v_mlp_norm_g, v_w_up, v_w_down, v_ple_w, v_ple_norm_g, v_gate_norm_g, v_w_gate, v_rel_bias_table, v_final_norm_g):
    w = dict(attn_norm_g=attn_norm_g, w_in=w_in[0], q_norm_g=q_norm_g, k_norm_g=k_norm_g, sink_logits=sink_logits,
             w_out=w_out[0], mlp_norm_g=mlp_norm_g, w_up=w_up[0], w_down=w_down[0], ple_w=ple_w[0],
             ple_norm_g=ple_norm_g, gate_norm_g=gate_norm_g, w_gate=w_gate[0], rel_bias_table=rel_bias_table,
             final_norm_g=final_norm_g)
    mom = dict(attn_norm_g=m_attn_norm_g, w_in=m_w_in[0], q_norm_g=m_q_norm_g, k_norm_g=m_k_norm_g,
               sink_logits=m_sink_logits, w_out=m_w_out[0], mlp_norm_g=m_mlp_norm_g, w_up=m_w_up[0],
               w_down=m_w_down[0], ple_w=m_ple_w[0], ple_norm_g=m_ple_norm_g, gate_norm_g=m_gate_norm_g,
               w_gate=m_w_gate[0], rel_bias_table=m_rel_bias_table, final_norm_g=m_final_norm_g)
    var = dict(attn_norm_g=v_attn_norm_g, w_in=v_w_in[0], q_norm_g=v_q_norm_g, k_norm_g=v_k_norm_g,
               sink_logits=v_sink_logits, w_out=v_w_out[0], mlp_norm_g=v_mlp_norm_g, w_up=v_w_up[0],
               w_down=v_w_down[0], ple_w=v_ple_w[0], ple_norm_g=v_ple_norm_g, gate_norm_g=v_gate_norm_g,
               w_gate=v_w_gate[0], rel_bias_table=v_rel_bias_table, final_norm_g=v_final_norm_g)
    D = x.shape[-1]
    n_heads = D // (2 * HEAD_DIM)

    gains = {n: w[n] for n in w if n not in _SHARDED}
    gains["final_norm_g"] = final_norm_g.reshape(1, -1)

    comm = _MeshComm(w, mom, var)
    dx, parts = _step(x[0], p[0, 0], loss_target[0], gains, comm)

    small_g = _small_all_reduce(parts, deps=[comm.out[n][0] for n in comm.out])
    comm.reduce_finish("e", small_g)

    g_out, d_out, m_out, v_out = {}, {}, {}, {}
    for n in _SHARDED:
        g, d, nm, nv = comm.out[n]
        g_out[n], d_out[n], m_out[n], v_out[n] = g[None], d[None], nm[None], nv[None]

    small = {n: v for n, v in w.items() if n not in _SHARDED}
    pack = lambda vals: _pack_small({n: vals[n] for n in small}, n_heads)
    sd, sm, sv = _adam_small(pack(w), small_g, pack(mom), pack(var))
    sg, loss = _unpack_small(small_g, small, n_heads)
    g_out.update(sg)
    for dst, row in ((d_out, sd), (m_out, sm), (v_out, sv)):
        dst.update(_unpack_small(row, small, n_heads)[0])

    return (loss, dx[None], *[g_out[n] for n in _ORDER], *[d_out[n] for n in _ORDER],
            *[m_out[n] for n in _ORDER], *[v_out[n] for n in _ORDER])
```

```python
import functools
import math

import numpy as np
import jax
import jax.numpy as jnp
from jax import lax
from jax.experimental import pallas as pl
from jax.experimental.pallas import tpu as pltpu

F32 = jnp.float32
BF16 = jnp.bfloat16

N_DEV = 8
N_CHIP = 4
HEAD_DIM = 128
GROUP = 4
GRID_W = 64
WINDOW = 128
BLOCK_Q = 128
N_BUCKETS = 32
MAX_DISTANCE = 128
ROPE_THETA = 10000.0
EPS = 1e-6
NEG_INF = -1e30
ADAM_LR = 0.001
ADAM_B1 = 0.9
ADAM_B2 = 0.999
ADAM_EPS = 1e-08
ADAM_WD = 0.01
ADAM_STEP = 10
LOG2E = math.log2(math.e)
LANES = 128
SUBLANES = 8
MESH = pl.DeviceIdType.MESH

_NT = (((1,), (1,)), ((), ()))
_NN = (((1,), (0,)), ((), ()))
_TN = (((0,), (0,)), ((), ()))


def _tile(dim, pref):
    return pref if dim % pref == 0 else dim


def _params(sem):
    return pltpu.CompilerParams(dimension_semantics=sem, vmem_limit_bytes=56 * 1024 * 1024)


_HBM = pl.BlockSpec(memory_space=pltpu.HBM)
_SEM = pl.BlockSpec(memory_space=pltpu.SEMAPHORE)
_ANY = pl.BlockSpec(memory_space=pl.ANY)
_VMEM = pl.BlockSpec(memory_space=pltpu.VMEM)
_EFFECT = pltpu.SideEffectType.DATAFLOW_SIDE_EFFECTING


def _pcall(body, deps=(), *, in_specs, into=None, **kw):
    deps = [d for d in deps if d is not None]
    nd = len(deps)
    if into is not None:
        deps = [into[0]] + deps
        nd += 1
        kw["input_output_aliases"] = {0: into[1]}

    def wrapped(*refs):
        body(*refs[nd:])

    call = pl.pallas_call(wrapped, in_specs=[_ANY] * nd + list(in_specs), **kw)
    return lambda *args: call(*deps, *args)


def _mm(name, a, b, dims, grid, a_spec, b_spec, out_shape, out_specs, acc_shape, epilogue,
        extras=(), extra_specs=(), deps=(), semantics=("parallel", "parallel", "arbitrary")):
    nk = grid[2]
    n_extra = len(extras)

    def body(*refs):
        a_ref, b_ref = refs[0], refs[1]
        extra = refs[2:2 + n_extra]
        outs = refs[2 + n_extra:-1]
        acc = refs[-1]
        part = lax.dot_general(a_ref[...], b_ref[...], dims, preferred_element_type=F32)
        if nk == 1:
            epilogue(part, extra, outs)
        else:
            k = pl.program_id(2)

            @pl.when(k == 0)
            def _():
                acc[...] = part

            @pl.when(k > 0)
            def _():
                acc[...] += part

            @pl.when(k == nk - 1)
            def _():
                epilogue(acc[...], extra, outs)

    return _pcall(
        body, deps, name=name, grid=grid,
        in_specs=[a_spec, b_spec, *extra_specs],
        out_specs=out_specs, out_shape=out_shape,
        scratch_shapes=[pltpu.VMEM(acc_shape if nk > 1 else (SUBLANES, LANES), F32)],
        compiler_params=_params(semantics),
    )(a, b, *extras)


def _store(dtype):
    def ep(acc, extra, outs):
        outs[0][...] = acc.astype(dtype)
    return ep


def _store_add(acc, extra, outs):
    outs[0][...] = acc + extra[0][...]


def _mm_nn(name, a, b, out_dtype=F32, epilogue=None, extras=(), n_out=1, out_dtypes=None, tm=1024, tn=1024, tk=None,
           deps=()):
    M, K = a.shape
    N = b.shape[1]
    tm, tn, tk = _tile(M, tm), _tile(N, tn), _tile(K, tk or K)
    b_spec = pl.BlockSpec((tk, tn), lambda i, j, k: (k, j))
    grid = (M // tm, N // tn, K // tk)
    o_spec = pl.BlockSpec((tm, tn), lambda i, j, k: (i, j))
    out_dtypes = out_dtypes or [out_dtype] * n_out
    out_shape = [jax.ShapeDtypeStruct((M, N), d) for d in out_dtypes]
    res = _mm(name, a, b, _NN, grid, pl.BlockSpec((tm, tk), lambda i, j, k: (i, k)), b_spec,
              out_shape, [o_spec] * len(out_dtypes), (tm, tn), epilogue or _store(out_dtype),
              extras, [o_spec] * len(extras), deps)
    return res if len(out_dtypes) > 1 else res[0]


def _mm_nt(name, a, b, out_dtype=F32, epilogue=None, extras=(), tm=1024, tn=1024, tk=None, deps=()):
    M, C = a.shape
    N = b.shape[0]
    tm, tn, tk = _tile(M, tm), _tile(N, tn), _tile(C, tk or C)
    b_spec = pl.BlockSpec((tn, tk), lambda i, j, k: (j, k))
    grid = (M // tm, N // tn, C // tk)
    o_spec = pl.BlockSpec((tm, tn), lambda i, j, k: (i, j))
    return _mm(name, a, b, _NT, grid, pl.BlockSpec((tm, tk), lambda i, j, k: (i, k)), b_spec,
               [jax.ShapeDtypeStruct((M, N), out_dtype)], [o_spec], (tm, tn), epilogue or _store(out_dtype),
               extras, [o_spec] * len(extras), deps)[0]


def _mm_tn(name, a, b, out_dtype=BF16, tm=1024, tn=512, tk=None, deps=()):
    T, M = a.shape
    N = b.shape[1]
    tm, tn, tk = _tile(M, tm), _tile(N, tn), _tile(T, tk or T)
    out_shape = jax.ShapeDtypeStruct((M, N), out_dtype)
    o_spec = pl.BlockSpec((tm, tn), lambda i, j, k: (i, j))
    grid = (M // tm, N // tn, T // tk)
    return _mm(name, a, b, _TN, grid, pl.BlockSpec((tk, tm), lambda i, j, k: (k, i)),
               pl.BlockSpec((tk, tn), lambda i, j, k: (k, j)), [out_shape], [o_spec], (tm, tn), _store(out_dtype),
               deps=deps)[0]


def _mean_last(v):
    return jnp.mean(v, axis=-1, keepdims=True)


def _rows_to_sublanes(v):
    r, c = v.shape
    return jnp.sum(v.reshape(r // SUBLANES, SUBLANES, c), axis=0)


def _accumulate(ref, val, first):
    @pl.when(first)
    def _():
        ref[...] = val

    @pl.when(jnp.logical_not(first))
    def _():
        ref[...] += val


def _rms_fwd(name, x, g, tr=256, deps=()):
    T, D = x.shape
    tr = _tile(T, tr)

    def body(x_ref, g_ref, o_ref):
        xv = x_ref[...]
        r = lax.rsqrt(_mean_last(xv * xv) + EPS)
        o_ref[...] = (xv * r * g_ref[...]).astype(BF16)

    row = pl.BlockSpec((tr, D), lambda i: (i, 0))
    return _pcall(
        body, deps, name=name, grid=(T // tr,),
        in_specs=[row, pl.BlockSpec((1, D), lambda i: (0, 0))],
        out_specs=row, out_shape=jax.ShapeDtypeStruct((T, D), BF16),
        compiler_params=_params(("parallel",)),
    )(x, g)


def _rms_bwd(name, dyn, x, g, dres, tr=256, deps=()):
    T, D = x.shape
    tr = _tile(T, tr)

    def body(dy_ref, x_ref, g_ref, dr_ref, dx_ref, dxb_ref, dg_ref):
        xv = x_ref[...]
        r = lax.rsqrt(_mean_last(xv * xv) + EPS)
        xn = xv * r
        dy = dy_ref[...]
        dxn = dy * g_ref[...]
        dx = dr_ref[...] + r * (dxn - xn * _mean_last(dxn * xn))
        dx_ref[...] = dx
        dxb_ref[...] = dx.astype(BF16)
        _accumulate(dg_ref, _rows_to_sublanes(dy * xn), pl.program_id(0) == 0)

    row = pl.BlockSpec((tr, D), lambda i: (i, 0))
    return _pcall(
        body, deps, name=name, grid=(T // tr,),
        in_specs=[row, row, pl.BlockSpec((1, D), lambda i: (0, 0)), row],
        out_specs=[row, row, pl.BlockSpec((SUBLANES, D), lambda i: (0, 0))],
        out_shape=[jax.ShapeDtypeStruct((T, D), F32), jax.ShapeDtypeStruct((T, D), BF16),
                   jax.ShapeDtypeStruct((SUBLANES, D), F32)],
        compiler_params=_params(("arbitrary",)),
    )(dyn, x, g, dres)


def _mm_nn_rms(name, a, b, res, g, tm=512, deps=()):
    M, K = a.shape
    N = b.shape[1]
    tm = _tile(M, tm)

    def epilogue(acc, extra, outs):
        h = acc + extra[0][...]
        outs[0][...] = h
        outs[1][...] = (h * lax.rsqrt(_mean_last(h * h) + EPS) * extra[1][...]).astype(BF16)

    row = pl.BlockSpec((tm, N), lambda i, j, k: (i, 0))
    return _mm(name, a, b, _NN, (M // tm, 1, 1), pl.BlockSpec((tm, K), lambda i, j, k: (i, 0)),
               pl.BlockSpec((K, N), lambda i, j, k: (0, 0)),
               [jax.ShapeDtypeStruct((M, N), F32), jax.ShapeDtypeStruct((M, N), BF16)], [row, row], (tm, N), epilogue,
               (res, g), [row, pl.BlockSpec((1, N), lambda i, j, k: (0, 0))], deps)


def _mm_nt_rms_bwd(name, a, b, x, g, dres, tm=256, deps=()):
    M, C = a.shape
    N = b.shape[0]
    tm = _tile(M, tm)

    def epilogue(dy, extra, outs):
        x_ref, dr_ref, g_ref = extra
        xv = x_ref[...]
        r = lax.rsqrt(_mean_last(xv * xv) + EPS)
        xn = xv * r
        dxn = dy * g_ref[...]
        dx = dr_ref[...] + r * (dxn - xn * _mean_last(dxn * xn))
        outs[0][...] = dx
        outs[1][...] = dx.astype(BF16)
        _accumulate(outs[2], _rows_to_sublanes(dy * xn), pl.program_id(0) == 0)

    row = pl.BlockSpec((tm, N), lambda i, j, k: (i, 0))
    return _mm(name, a, b, _NT, (M // tm, 1, 1), pl.BlockSpec((tm, C), lambda i, j, k: (i, 0)),
               pl.BlockSpec((N, C), lambda i, j, k: (0, 0)),
               [jax.ShapeDtypeStruct((M, N), F32), jax.ShapeDtypeStruct((M, N), BF16),
                jax.ShapeDtypeStruct((SUBLANES, N), F32)],
               [row, row, pl.BlockSpec((SUBLANES, N), lambda i, j, k: (0, 0))], (tm, N), epilogue,
               (x, dres, g), [row, row, pl.BlockSpec((1, N), lambda i, j, k: (0, 0))], deps,
               semantics=("arbitrary", "arbitrary", "arbitrary"))


def _gate_tail(gn, w_gate, h2, pe, target, g_ple, g_final, tm=256):
    T, D = h2.shape
    tm = _tile(T, tm)

    def epilogue(z, extra, outs):
        h2_ref, pe_ref, t_ref, gp_ref, gf_ref = extra
        dh3_ref, dz_ref, dpe_ref, dgf_ref, dgp_ref, loss_ref = outs
        first = pl.program_id(0) == 0
        pev = pe_ref[...]
        r3 = lax.rsqrt(_mean_last(pev * pev) + EPS)
        en = pev * r3
        e = en * gp_ref[...]
        gate = 1.0 / (1.0 + jnp.exp(-z))
        h3 = h2_ref[...] + gate * e
        r5 = lax.rsqrt(_mean_last(h3 * h3) + EPS)
        hn = h3 * r5
        diff = hn * gf_ref[...] - t_ref[...]
        loss_rows = 0.5 * _mean_last(diff * diff)
        row0 = lax.broadcasted_iota(jnp.int32, (SUBLANES, LANES), 0) == 0
        _accumulate(loss_ref, jnp.where(row0, jnp.sum(loss_rows), 0.0), first)
        dy = diff * (1.0 / D)
        _accumulate(dgf_ref, _rows_to_sublanes(dy * hn), first)
        dhn = dy * gf_ref[...]
        dh3 = r5 * (dhn - hn * _mean_last(dhn * hn))
        dh3_ref[...] = dh3
        dgate = dh3 * e
        de = dh3 * gate
        dz_ref[...] = (dgate * gate * (1.0 - gate)).astype(BF16)
        _accumulate(dgp_ref, _rows_to_sublanes(de * en), first)
        den = de * gp_ref[...]
        dpe_ref[...] = (r3 * (den - en * _mean_last(den * en))).astype(BF16)

    row = pl.BlockSpec((tm, D), lambda i, j, k: (i, 0))
    vec = pl.BlockSpec((1, D), lambda i, j, k: (0, 0))
    part = pl.BlockSpec((SUBLANES, D), lambda i, j, k: (0, 0))
    return _mm("gate_tail", gn, w_gate, _NN, (T // tm, 1, 1), row, pl.BlockSpec(w_gate.shape, lambda i, j, k: (0, 0)),
               [jax.ShapeDtypeStruct((T, D), F32), jax.ShapeDtypeStruct((T, D), BF16),
                jax.ShapeDtypeStruct((T, D), BF16), jax.ShapeDtypeStruct((SUBLANES, D), F32),
                jax.ShapeDtypeStruct((SUBLANES, D), F32), jax.ShapeDtypeStruct((SUBLANES, LANES), F32)],
               [row, row, row, part, part, pl.BlockSpec((SUBLANES, LANES), lambda i, j, k: (0, 0))], (tm, D), epilogue,
               (h2, pe, target, g_ple, g_final), [row, row, row, vec, vec],
               semantics=("arbitrary", "arbitrary", "arbitrary"))


def _rope_tables(T):
    pos = np.arange(T)
    half = HEAD_DIM // 2
    inv = (ROPE_THETA ** (-np.arange(0, half, 2, dtype=np.float32) / half)).astype(np.float32)
    ang_r = (pos // GRID_W).astype(np.float32)[:, None] * inv
    ang_c = (pos % GRID_W).astype(np.float32)[:, None] * inv
    cos = np.concatenate([np.cos(ang_r), np.cos(ang_r), np.cos(ang_c), np.cos(ang_c)], axis=-1)
    sin = np.concatenate([-np.sin(ang_r), np.sin(ang_r), -np.sin(ang_c), np.sin(ang_c)], axis=-1)
    return jnp.asarray(cos, F32), jnp.asarray(sin, F32)


def _swap32(x):
    lane = lax.broadcasted_iota(jnp.int32, x.shape, 1)
    return jnp.where((lane % 64) < 32, pltpu.roll(x, 96, 1), pltpu.roll(x, 32, 1))


def _qk_prep(proj, cos, sin, g_q, g_k, n_norm, tr=256):
    T, W = proj.shape
    tr = _tile(T, tr)
    n_q = n_norm * GROUP // (GROUP + 1)

    def body(p_ref, c_ref, s_ref, gq_ref, gk_ref, o_ref):
        c, s = c_ref[...], s_ref[...]
        for h in range(n_norm):
            cols = slice(h * HEAD_DIM, (h + 1) * HEAD_DIM)
            xv = p_ref[:, cols]
            g = gq_ref[...] if h < n_q else gk_ref[...]
            xn = xv * lax.rsqrt(_mean_last(xv * xv) + EPS) * g
            o_ref[:, cols] = (xn * c + _swap32(xn) * s).astype(BF16)
        rest = slice(n_norm * HEAD_DIM, W)
        o_ref[:, rest] = p_ref[:, rest].astype(BF16)

    row = pl.BlockSpec((tr, W), lambda i: (i, 0))
    tab = pl.BlockSpec((tr, HEAD_DIM), lambda i: (i, 0))
    vec = pl.BlockSpec((1, HEAD_DIM), lambda i: (0, 0))
    return pl.pallas_call(
        body, name="qk_prep", grid=(T // tr,),
        in_specs=[row, tab, tab, vec, vec], out_specs=row,
        out_shape=jax.ShapeDtypeStruct((T, W), BF16),
        compiler_params=_params(("parallel",)),
    )(proj, cos, sin, g_q, g_k)


def _dproj(proj, dqa, dka, dva, dqb, dkb, dvb, cos, sin, g_q, g_k, tr=256):
    T, W = proj.shape
    tr = _tile(T, tr)
    n_q = dqa.shape[1] // HEAD_DIM
    n_kv = dka.shape[1] // HEAD_DIM
    wa = (n_q + n_kv) * HEAD_DIM

    def body(p_ref, dqa_ref, dka_ref, dva_ref, dqb_ref, dkb_ref, dvb_ref, c_ref, s_ref, gq_ref, gk_ref,
             o_ref, dgq_ref, dgk_ref):
        c, s = c_ref[...], s_ref[...]
        dgq = jnp.zeros((SUBLANES, HEAD_DIM), F32)
        dgk = jnp.zeros((SUBLANES, HEAD_DIM), F32)
        for h in range(n_q + n_kv):
            cols = slice(h * HEAD_DIM, (h + 1) * HEAD_DIM)
            xv = p_ref[:, cols]
            r = lax.rsqrt(_mean_last(xv * xv) + EPS)
            xn = xv * r
            if h < n_q:
                d = dqa_ref[:, cols]
                g = gq_ref[...]
            else:
                d = dka_ref[:, (h - n_q) * HEAD_DIM:(h - n_q + 1) * HEAD_DIM]
                g = gk_ref[...]
            dqn = d * c + _swap32(d * s)
            part = _rows_to_sublanes(dqn * xn)
            if h < n_q:
                dgq = dgq + part
            else:
                dgk = dgk + part
            dxn = dqn * g
            o_ref[:, cols] = (r * (dxn - xn * _mean_last(dxn * xn))).astype(BF16)
        off = wa
        for ref in (dva_ref, dqb_ref, dkb_ref, dvb_ref):
            w = ref.shape[1]
            o_ref[:, off:off + w] = ref[...].astype(BF16)
            off += w
        first = pl.program_id(0) == 0
        _accumulate(dgq_ref, dgq, first)
        _accumulate(dgk_ref, dgk, first)

    def row(w):
        return pl.BlockSpec((tr, w), lambda i: (i, 0))

    vec = pl.BlockSpec((1, HEAD_DIM), lambda i: (0, 0))
    part = pl.BlockSpec((SUBLANES, HEAD_DIM), lambda i: (0, 0))
    return pl.pallas_call(
        body, name="dproj", grid=(T // tr,),
        in_specs=[row(wa), row(dqa.shape[1]), row(dka.shape[1]), row(dva.shape[1]), row(dqb.shape[1]),
                  row(dkb.shape[1]), row(dvb.shape[1]), row(HEAD_DIM), row(HEAD_DIM), vec, vec],
        out_specs=[row(W), part, part],
        out_shape=[jax.ShapeDtypeStruct((T, W), BF16), jax.ShapeDtypeStruct((SUBLANES, HEAD_DIM), F32),
                   jax.ShapeDtypeStruct((SUBLANES, HEAD_DIM), F32)],
        compiler_params=_params(("arbitrary",)),
    )(proj, dqa, dka, dva, dqb, dkb, dvb, cos, sin, g_q, g_k)


def _attn_a_fwd(pb, n_q, n_kv, out_heads, tq=1024, tc=1024):
    T = pb.shape[0]
    tq, tc = _tile(T, tq), _tile(T, tc)
    scale = HEAD_DIM ** -0.5
    c = scale * LOG2E

    def body(q_ref, k_ref, v_ref, o_ref, lse_ref):
        q = q_ref[...]
        m = l = acc = None
        for j in range(T // tc):
            keys = slice(j * tc, (j + 1) * tc)
            s = lax.dot_general(q, k_ref[keys, :], _NT, preferred_element_type=F32)
            mj = jnp.max(s, axis=-1, keepdims=True)
            m_new = mj if j == 0 else jnp.maximum(m, mj)
            p = jnp.exp2((s - m_new) * c)
            pv = lax.dot_general(p.astype(BF16), v_ref[keys, :], _NN, preferred_element_type=F32)
            if j == 0:
                l, acc = jnp.sum(p, axis=-1, keepdims=True), pv
            else:
                alpha = jnp.exp2((m - m_new) * c)
                l = alpha * l + jnp.sum(p, axis=-1, keepdims=True)
                acc = alpha * acc + pv
            m = m_new
        o_ref[...] = (acc / l).astype(BF16)
        lse_ref[...] = m * scale + jnp.log(l)

    return pl.pallas_call(
        body, name="attn_a_fwd", grid=(n_kv, GROUP, T // tq),
        in_specs=[pl.BlockSpec((tq, HEAD_DIM), lambda kv, g, i: (i, kv * GROUP + g)),
                  pl.BlockSpec((T, HEAD_DIM), lambda kv, g, i: (0, n_q + kv)),
                  pl.BlockSpec((T, HEAD_DIM), lambda kv, g, i: (0, n_q + n_kv + kv))],
        out_specs=[pl.BlockSpec((tq, HEAD_DIM), lambda kv, g, i: (i, kv * GROUP + g)),
                   pl.BlockSpec((None, tq, 1), lambda kv, g, i: (kv * GROUP + g, i, 0))],
        out_shape=[jax.ShapeDtypeStruct((T, out_heads * HEAD_DIM), BF16), jax.ShapeDtypeStruct((n_q, T, 1), F32)],
        compiler_params=_params(("parallel", "parallel", "parallel")),
    )(pb, pb, pb)


def _attn_a_bwd(pb, o_cat, d_o, lse, n_q, n_kv, tq=512, tc=512):
    T = pb.shape[0]
    tq, tc = _tile(T, tq), _tile(T, tc)
    scale = HEAD_DIM ** -0.5
    c = scale * LOG2E

    def body(q_ref, k_ref, v_ref, o_ref, do_ref, lse_ref, dq_ref, dkt_ref, dvt_ref):
        q, do = q_ref[...], do_ref[...]
        qt, dot = q.T, do.T
        delta = jnp.sum(do.astype(F32) * o_ref[...].astype(F32), axis=-1, keepdims=True)
        lse2 = lse_ref[...] * LOG2E

        @pl.when(jnp.logical_and(pl.program_id(1) == 0, pl.program_id(2) == 0))
        def _():
            dkt_ref[...] = jnp.zeros(dkt_ref.shape, F32)
            dvt_ref[...] = jnp.zeros(dvt_ref.shape, F32)

        dq = None
        for j in range(T // tc):
            keys = slice(j * tc, (j + 1) * tc)
            kc, vc = k_ref[keys, :], v_ref[keys, :]
            s = lax.dot_general(q, kc, _NT, preferred_element_type=F32)
            p = jnp.exp2(s * c - lse2)
            dp = lax.dot_general(do, vc, _NT, preferred_element_type=F32)
            ds = (p * (dp - delta) * scale).astype(BF16)
            dqj = lax.dot_general(ds, kc, _NN, preferred_element_type=F32)
            dq = dqj if dq is None else dq + dqj
            dvt_ref[:, keys] += lax.dot_general(dot, p.astype(BF16), _NN, preferred_element_type=F32)
            dkt_ref[:, keys] += lax.dot_general(qt, ds, _NN, preferred_element_type=F32)
        dq_ref[...] = dq

    qmap = lambda kv, g, i: (i, kv * GROUP + g)
    return pl.pallas_call(
        body, name="attn_a_bwd", grid=(n_kv, GROUP, T // tq),
        in_specs=[pl.BlockSpec((tq, HEAD_DIM), qmap),
                  pl.BlockSpec((T, HEAD_DIM), lambda kv, g, i: (0, n_q + kv)),
                  pl.BlockSpec((T, HEAD_DIM), lambda kv, g, i: (0, n_q + n_kv + kv)),
                  pl.BlockSpec((tq, HEAD_DIM), qmap),
                  pl.BlockSpec((tq, HEAD_DIM), qmap),
                  pl.BlockSpec((None, tq, 1), lambda kv, g, i: (kv * GROUP + g, i, 0))],
        out_specs=[pl.BlockSpec((tq, HEAD_DIM), qmap),
                   pl.BlockSpec((HEAD_DIM, T), lambda kv, g, i: (kv, 0)),
                   pl.BlockSpec((HEAD_DIM, T), lambda kv, g, i: (kv, 0))],
        out_shape=[jax.ShapeDtypeStruct((T, n_q * HEAD_DIM), F32),
                   jax.ShapeDtypeStruct((n_kv * HEAD_DIM, T), F32),
                   jax.ShapeDtypeStruct((n_kv * HEAD_DIM, T), F32)],
        compiler_params=_params(("parallel", "arbitrary", "arbitrary")),
    )(pb, pb, pb, o_cat, d_o, lse)


def _bucket_index():
    r = np.arange(BLOCK_Q)[:, None]
    j = np.arange(3 * BLOCK_Q)[None, :]
    rel = (j - BLOCK_Q) - r
    nb = N_BUCKETS // 2
    ret = np.where(rel > 0, nb, 0)
    n = np.abs(rel)
    max_exact = nb // 2
    nf = np.maximum(n, 1).astype(np.float32)
    large = max_exact + (np.log(nf / max_exact) / math.log(MAX_DISTANCE / max_exact) * (nb - max_exact)).astype(np.int32)
    large = np.minimum(large, nb - 1)
    return jnp.asarray(ret + np.where(n < max_exact, n, large), jnp.int32)


def _bias_build(idx, table_flat, n_heads, deps=()):
    def body(idx_ref, tab_ref, o_ref):
        h = pl.program_id(0)
        iv = idx_ref[...]
        acc = jnp.zeros(iv.shape, F32)
        for b in range(N_BUCKETS):
            acc = jnp.where(iv == b, tab_ref[b * n_heads + h], acc)
        r = lax.broadcasted_iota(jnp.int32, iv.shape, 0)
        j = lax.broadcasted_iota(jnp.int32, iv.shape, 1)
        o_ref[...] = jnp.where(jnp.abs(j - BLOCK_Q - r) <= WINDOW, acc, NEG_INF)

    return _pcall(
        body, deps, name="bias_build", grid=(n_heads,),
        in_specs=[pl.BlockSpec(idx.shape, lambda h: (0, 0)), pl.BlockSpec(memory_space=pltpu.SMEM)],
        out_specs=pl.BlockSpec((None,) + idx.shape, lambda h: (h, 0, 0)),
        out_shape=jax.ShapeDtypeStruct((n_heads,) + idx.shape, F32),
        compiler_params=_params(("parallel",)),
    )(idx, table_flat)


def _in_sequence(n, T):
    j = lax.broadcasted_iota(jnp.int32, (GROUP * BLOCK_Q, 3 * BLOCK_Q), 1)
    kabs = n * BLOCK_Q + j - BLOCK_Q
    return (kabs >= 0) & (kabs < T)


def _per_head_rows(values):
    head = lax.broadcasted_iota(jnp.int32, (GROUP * BLOCK_Q, 1), 0) // BLOCK_Q
    col = jnp.zeros((GROUP * BLOCK_Q, 1), F32)
    for g, v in enumerate(values):
        col = jnp.where(head == g, v, col)
    return col


def _band_specs(col, nblk, sb):
    return [pl.BlockSpec((BLOCK_Q, HEAD_DIM), lambda kv, i: (jnp.maximum(sb * i - 1, 0), col(kv))),
            pl.BlockSpec((sb * BLOCK_Q, HEAD_DIM), lambda kv, i: (i, col(kv))),
            pl.BlockSpec((BLOCK_Q, HEAD_DIM), lambda kv, i: (jnp.minimum(sb * i + sb, nblk - 1), col(kv)))]


def _head_specs(base, rows):
    return [pl.BlockSpec((rows, HEAD_DIM), functools.partial(lambda kv, i, g: (i, base + kv * GROUP + g), g=g))
            for g in range(GROUP)]


def _attn_b_fwd(pb, bias, sink, o_all, q_off, n_q, n_kv, deps=(), sb=8):
    T = pb.shape[0]
    nblk = T // BLOCK_Q
    sb = min(sb, nblk)
    tq = sb * BLOCK_Q
    scale = HEAD_DIM ** -0.5

    def body(*refs):
        q_refs = refs[0:GROUP]
        k_refs, v_refs = refs[GROUP:GROUP + 3], refs[GROUP + 3:GROUP + 6]
        bias_ref, sink_ref, o_ref, lse_ref = refs[GROUP + 6:]
        kv, i = pl.program_id(0), pl.program_id(1)
        kb = jnp.concatenate([r[...] for r in k_refs], axis=0)
        vb = jnp.concatenate([r[...] for r in v_refs], axis=0)
        bias_all = bias_ref[...].reshape(GROUP * BLOCK_Q, 3 * BLOCK_Q)
        sk = _per_head_rows([sink_ref[kv * GROUP + g] for g in range(GROUP)])
        for b in range(sb):
            rows = slice(b * BLOCK_Q, (b + 1) * BLOCK_Q)
            kw, vw = kb[b * BLOCK_Q:(b + 3) * BLOCK_Q], vb[b * BLOCK_Q:(b + 3) * BLOCK_Q]
            q = jnp.concatenate([r[rows, :] for r in q_refs], axis=0)
            s = lax.dot_general(q, kw, _NT, preferred_element_type=F32) * scale + bias_all
            if b == 0 or b == sb - 1:
                s = jnp.where(_in_sequence(i * sb + b, T), s, NEG_INF)
            m = jnp.maximum(jnp.max(s, axis=-1, keepdims=True), sk)
            p = jnp.exp(s - m)
            l = jnp.sum(p, axis=-1, keepdims=True) + jnp.exp(sk - m)
            o = (lax.dot_general(p.astype(BF16), vw, _NN, preferred_element_type=F32) / l).astype(BF16)
            lse = m + jnp.log(l)
            for g in range(GROUP):
                head = slice(g * BLOCK_Q, (g + 1) * BLOCK_Q)
                o_ref[rows, g * HEAD_DIM:(g + 1) * HEAD_DIM] = o[head]
                lse_ref[g, rows, :] = lse[head]

    first_group = o_all.shape[1] // (GROUP * HEAD_DIM) - n_kv
    return _pcall(
        body, deps, into=(o_all, 0), name="attn_b_fwd", grid=(n_kv, nblk // sb),
        in_specs=[*_head_specs(q_off, tq),
                  *_band_specs(lambda kv: q_off + n_q + kv, nblk, sb),
                  *_band_specs(lambda kv: q_off + n_q + n_kv + kv, nblk, sb),
                  pl.BlockSpec((GROUP, BLOCK_Q, 3 * BLOCK_Q), lambda kv, i: (kv, 0, 0)),
                  pl.BlockSpec(memory_space=pltpu.SMEM)],
        out_specs=[pl.BlockSpec((tq, GROUP * HEAD_DIM), lambda kv, i: (i, first_group + kv)),
                   pl.BlockSpec((GROUP, tq, 1), lambda kv, i: (kv, i, 0))],
        out_shape=[jax.ShapeDtypeStruct(o_all.shape, BF16), jax.ShapeDtypeStruct((n_q, T, 1), F32)],
        compiler_params=_params(("parallel", "parallel")),
    )(*([pb] * (GROUP + 6)), bias, sink)


def _attn_b_bwd(pb, o_cat, d_o, lse, bias, sink, q_off, n_q, n_kv, o_off, deps=(), sb=8):
    T = pb.shape[0]
    nblk = T // BLOCK_Q
    sb = min(sb, nblk)
    tq = sb * BLOCK_Q
    scale = HEAD_DIM ** -0.5

    def body(*refs):
        q_refs = refs[0:GROUP]
        k_refs, v_refs = refs[GROUP:GROUP + 3], refs[GROUP + 3:GROUP + 6]
        o_refs, do_refs = refs[GROUP + 6:2 * GROUP + 6], refs[2 * GROUP + 6:3 * GROUP + 6]
        lse_ref, bias_ref, sink_ref, dq_ref, dk_ref, dv_ref, dbias_ref, dsink_ref, dkb_ref, dvb_ref = refs[3 * GROUP + 6:]
        kv, i = pl.program_id(0), pl.program_id(1)
        first = i == 0

        @pl.when(first)
        def _():
            dk_ref[...] = jnp.zeros(dk_ref.shape, F32)
            dv_ref[...] = jnp.zeros(dv_ref.shape, F32)
            dbias_ref[...] = jnp.zeros(dbias_ref.shape, F32)

        kb = jnp.concatenate([r[...] for r in k_refs], axis=0)
        vb = jnp.concatenate([r[...] for r in v_refs], axis=0)
        dkb_ref[...] = jnp.zeros(dkb_ref.shape, F32)
        dvb_ref[...] = jnp.zeros(dvb_ref.shape, F32)
        row = lax.broadcasted_iota(jnp.int32, (SUBLANES, LANES), 0)
        dsink = jnp.zeros((SUBLANES, LANES), F32)
        bias_all = bias_ref[...].reshape(GROUP * BLOCK_Q, 3 * BLOCK_Q)
        sk = _per_head_rows([sink_ref[kv * GROUP + g] for g in range(GROUP)])
        for b in range(sb):
            rows = slice(b * BLOCK_Q, (b + 1) * BLOCK_Q)
            win = slice(b * BLOCK_Q, (b + 3) * BLOCK_Q)
            kw, vw = kb[win], vb[win]
            q = jnp.concatenate([r[rows, :] for r in q_refs], axis=0)
            do = jnp.concatenate([r[rows, :] for r in do_refs], axis=0)
            o = jnp.concatenate([r[rows, :] for r in o_refs], axis=0)
            lse = jnp.concatenate([lse_ref[g, rows, :] for g in range(GROUP)], axis=0)
            delta = jnp.sum(do.astype(F32) * o.astype(F32), axis=-1, keepdims=True)
            s = lax.dot_general(q, kw, _NT, preferred_element_type=F32) * scale + bias_all
            if b == 0 or b == sb - 1:
                s = jnp.where(_in_sequence(i * sb + b, T), s, NEG_INF)
            p = jnp.exp(s - lse)
            dp = lax.dot_general(do, vw, _NT, preferred_element_type=F32)
            ds = p * (dp - delta)
            dbias_ref[...] += ds.reshape(GROUP, BLOCK_Q, 3 * BLOCK_Q)
            sunk = jnp.exp(sk - lse) * delta
            for g in range(GROUP):
                dsink = dsink + jnp.where(row == g, -jnp.sum(sunk[g * BLOCK_Q:(g + 1) * BLOCK_Q]), 0.0)
            dsb = (ds * scale).astype(BF16)
            dq = lax.dot_general(dsb, kw, _NN, preferred_element_type=F32)
            for g in range(GROUP):
                dq_ref[rows, g * HEAD_DIM:(g + 1) * HEAD_DIM] = dq[g * BLOCK_Q:(g + 1) * BLOCK_Q]
            dkb_ref[win, :] += lax.dot_general(dsb, q, _TN, preferred_element_type=F32)
            dvb_ref[win, :] += lax.dot_general(p.astype(BF16), do, _TN, preferred_element_type=F32)
        _accumulate(dsink_ref, dsink, first)

        before = pl.ds(pl.multiple_of(jnp.maximum(sb * i - 1, 0) * BLOCK_Q, BLOCK_Q), BLOCK_Q)
        own = pl.ds(pl.multiple_of(i * tq, BLOCK_Q), tq)
        after = pl.ds(pl.multiple_of(jnp.minimum(sb * i + sb, nblk - 1) * BLOCK_Q, BLOCK_Q), BLOCK_Q)
        for acc_ref, band_ref in ((dk_ref, dkb_ref), (dv_ref, dvb_ref)):
            acc_ref[before, :] += band_ref[0:BLOCK_Q, :]
            acc_ref[own, :] += band_ref[BLOCK_Q:BLOCK_Q + tq, :]
            acc_ref[after, :] += band_ref[BLOCK_Q + tq:, :]

    return _pcall(
        body, deps, name="attn_b_bwd", grid=(n_kv, nblk // sb),
        in_specs=[*_head_specs(q_off, tq),
                  *_band_specs(lambda kv: q_off + n_q + kv, nblk, sb),
                  *_band_specs(lambda kv: q_off + n_q + n_kv + kv, nblk, sb),
                  *_head_specs(o_off, tq), *_head_specs(o_off, tq),
                  pl.BlockSpec((GROUP, tq, 1), lambda kv, i: (kv, i, 0)),
                  pl.BlockSpec((GROUP, BLOCK_Q, 3 * BLOCK_Q), lambda kv, i: (kv, 0, 0)),
                  pl.BlockSpec(memory_space=pltpu.SMEM)],
        out_specs=[pl.BlockSpec((tq, GROUP * HEAD_DIM), lambda kv, i: (i, kv)),
                   pl.BlockSpec((T, HEAD_DIM), lambda kv, i: (0, kv)),
                   pl.BlockSpec((T, HEAD_DIM), lambda kv, i: (0, kv)),
                   pl.BlockSpec((GROUP, BLOCK_Q, 3 * BLOCK_Q), lambda kv, i: (kv, 0, 0)),
                   pl.BlockSpec((None, SUBLANES, LANES), lambda kv, i: (kv, 0, 0))],
        out_shape=[jax.ShapeDtypeStruct((T, n_q * HEAD_DIM), F32),
                   jax.ShapeDtypeStruct((T, n_kv * HEAD_DIM), F32),
                   jax.ShapeDtypeStruct((T, n_kv * HEAD_DIM), F32),
                   jax.ShapeDtypeStruct((n_q, BLOCK_Q, 3 * BLOCK_Q), F32),
                   jax.ShapeDtypeStruct((n_kv, SUBLANES, LANES), F32)],
        scratch_shapes=[pltpu.VMEM((tq + 2 * BLOCK_Q, HEAD_DIM), F32), pltpu.VMEM((tq + 2 * BLOCK_Q, HEAD_DIM), F32)],
        compiler_params=_params(("parallel", "arbitrary")),
    )(*([pb] * (GROUP + 6)), *([o_cat] * GROUP), *([d_o] * GROUP), lse, bias, sink)


def _table_grads(dbias, dsink_raw, idx):
    n_heads = dbias.shape[0]
    n_kv = dsink_raw.shape[0]

    def body(db_ref, ds_ref, idx_ref, dt_ref, dsk_ref):
        iv = idx_ref[...]
        row = lax.broadcasted_iota(jnp.int32, (SUBLANES, LANES), 0)
        lane = lax.broadcasted_iota(jnp.int32, (SUBLANES, LANES), 1)
        dsk = jnp.zeros((SUBLANES, LANES), F32)
        for h in range(n_heads):
            d = db_ref[h]
            acc = jnp.zeros((SUBLANES, LANES), F32)
            for b in range(N_BUCKETS):
                acc = jnp.where((row == 0) & (lane == b), jnp.sum(jnp.where(iv == b, d, 0.0)), acc)
            dt_ref[:, h * LANES:(h + 1) * LANES] = acc
            raw = ds_ref[h // GROUP]
            val = jnp.sum(jnp.where((row == h % GROUP) & (lane == 0), raw, 0.0))
            dsk = jnp.where((row == 0) & (lane == h), val, dsk)
        dsk_ref[...] = dsk

    return pl.pallas_call(
        body, name="table_grads",
        in_specs=[pl.BlockSpec(memory_space=pltpu.VMEM)] * 3,
        out_specs=[pl.BlockSpec(memory_space=pltpu.VMEM)] * 2,
        out_shape=[jax.ShapeDtypeStruct((SUBLANES, n_heads * LANES), F32),
                   jax.ShapeDtypeStruct((SUBLANES, LANES), F32)],
        compiler_params=pltpu.CompilerParams(vmem_limit_bytes=56 * 1024 * 1024),
    )(dbias, dsink_raw, idx)


def _position():
    x, y, c = lax.axis_index("x"), lax.axis_index("y"), lax.axis_index("c")
    return x, y, c


def _hbm(a):
    return pltpu.with_memory_space_constraint(a, pltpu.HBM)


def _split_start(name, bufs, sem_shapes, issue):
    nb, ns = len(bufs), len(sem_shapes)

    def body(*refs):
        buf_refs = refs[:nb]
        sems = refs[nb:nb + ns]
        token = refs[nb + ns + nb]
        issue(buf_refs, sems)
        token[...] = jnp.zeros(token.shape, F32)

    outs = pl.pallas_call(
        body, name=name,
        in_specs=[_HBM] * nb,
        out_specs=[_SEM] * ns + [_HBM] * nb + [_VMEM],
        out_shape=[pltpu.SemaphoreType.DMA(s) for s in sem_shapes] + [pltpu.HBM(b.shape, b.dtype) for b in bufs]
        + [jax.ShapeDtypeStruct((SUBLANES, LANES), F32)],
        input_output_aliases={i: ns + i for i in range(nb)},
        compiler_params=pltpu.CompilerParams(has_side_effects=_EFFECT),
    )(*[_hbm(b) for b in bufs])
    return outs[:ns], outs[ns:ns + nb], outs[-1]


def _split_wait(name, bufs, send, recv, counts, size_of, after):
    nb = len(bufs)

    def body(*refs):
        buf_refs = refs[:nb]
        send_ref, recv_ref = refs[nb], refs[nb + 1]
        x, y, c = _position()
        for w, n in enumerate(counts):
            ref = size_of(buf_refs, w)
            for k in range(n):
                s = sum(counts[:w]) + k
                cp = pltpu.make_async_remote_copy(
                    src_ref=ref, dst_ref=ref, send_sem=send_ref.at[s], recv_sem=recv_ref.at[s],
                    device_id=(x, y, c), device_id_type=MESH)
                cp.wait_send()
                cp.wait_recv()

    return pl.pallas_call(
        body, name=name,
        in_specs=[_HBM] * nb + [_SEM, _SEM, _ANY],
        out_specs=[_HBM] * nb,
        out_shape=[pltpu.HBM(b.shape, b.dtype) for b in bufs],
        input_output_aliases={i: i for i in range(nb)},
        compiler_params=pltpu.CompilerParams(has_side_effects=_EFFECT),
    )(*bufs, send, recv, after)


def _block_of(pos):
    return 4 * pos[0] + 2 * pos[1] + pos[2]


def _shard_of(ref, blk, by_cols):
    aligned = (lambda v, a: v) if isinstance(blk, int) else pl.multiple_of
    if by_cols:
        n = ref.shape[1] // N_DEV
        return ref.at[:, pl.ds(aligned(blk * n, LANES), n)]
    r = ref.shape[0] // N_DEV
    return ref.at[pl.ds(aligned(blk * r, SUBLANES), r), :]


def _place_shards(shards, by_cols):
    mine = _block_of(_position()).astype(jnp.int32).reshape(1)

    def place(name, s, cols, tr=256):
        r, n = s.shape
        tr = _tile(r, tr)

        def body(m_ref, s_ref, o_ref):
            o_ref[...] = s_ref[...].astype(BF16)

        if cols:
            out = pl.BlockSpec((tr, n), lambda i, m_ref: (i, m_ref[0]))
        else:
            out = pl.BlockSpec((tr, n), lambda i, m_ref: (m_ref[0] * (r // tr) + i, 0))
        return pl.pallas_call(
            body, name=name,
            grid_spec=pltpu.PrefetchScalarGridSpec(
                num_scalar_prefetch=1, grid=(r // tr,),
                in_specs=[pl.BlockSpec((tr, n), lambda i, m_ref: (i, 0))], out_specs=out),
            out_shape=jax.ShapeDtypeStruct((r, n * N_DEV) if cols else (r * N_DEV, n), BF16),
            compiler_params=_params(("parallel",)),
        )(mine, s)

    return [place("place_shard_%d" % w, s, cols) for w, (s, cols) in enumerate(zip(shards, by_cols))]


def _gather_start(shards, by_cols, groups):
    lands = _place_shards(shards, by_cols)

    def issue(land, sems):
        x, y, c = _position()
        peers = [(x, y, 1 - c), (1 - x, y, c), (x, 1 - y, c), (1 - x, 1 - y, c)]
        for gi, grp in enumerate(groups):
            for wi, w in enumerate(grp):
                own = _shard_of(land[w], _block_of((x, y, c)), by_cols[w])
                for k, peer in enumerate(peers):
                    pltpu.make_async_remote_copy(
                        src_ref=own, dst_ref=own, send_sem=sems[2 * gi].at[4 * wi + k],
                        recv_sem=sems[2 * gi + 1].at[4 * wi + k], device_id=peer, device_id_type=MESH).start()

    sem_shapes = [(4 * len(g),) for g in groups for _ in range(2)]
    return _split_start("gather_start", lands, sem_shapes, issue)


def _gather_forward(name, lands, by_cols):
    nw = len(lands)

    def issue(land, sems):
        x, y, c = _position()
        for w in range(nw):
            for k, chip in enumerate([(1 - x, y), (x, 1 - y), (1 - x, 1 - y)]):
                blk = _shard_of(land[w], _block_of((*chip, c)), by_cols[w])
                pltpu.make_async_remote_copy(
                    src_ref=blk, dst_ref=blk, send_sem=sems[0].at[3 * w + k], recv_sem=sems[1].at[3 * w + k],
                    device_id=(x, y, 1 - c), device_id_type=MESH).start()

    return _split_start(name, lands, [(3 * nw,), (3 * nw,)], issue)


def _first_block(bufs, w, offset=0):
    return bufs[offset + w].at[0]


_PEER_FLIPS = ((0, 0, 1), (1, 0, 0), (1, 0, 1), (0, 1, 0), (0, 1, 1), (1, 1, 0), (1, 1, 1))


def _scatter_start(name, grads, by_cols):
    nw = len(grads)
    lands = []
    for g, cols in zip(grads, by_cols):
        shard = (g.shape[0], g.shape[1] // N_DEV) if cols else (g.shape[0] // N_DEV, g.shape[1])
        lands.append(lax.empty((N_DEV,) + shard, g.dtype))

    def issue(bufs, sems):
        x, y, c = _position()
        flip = lambda v, f: 1 - v if f else v
        for w in range(nw):
            for k, (fx, fy, fc) in enumerate(_PEER_FLIPS):
                peer = (flip(x, fx), flip(y, fy), flip(c, fc))
                pltpu.make_async_remote_copy(
                    src_ref=_shard_of(bufs[w], _block_of(peer), by_cols[w]), dst_ref=bufs[nw + w].at[_block_of((x, y, c))],
                    send_sem=sems[0].at[7 * w + k], recv_sem=sems[1].at[7 * w + k],
                    device_id=peer, device_id_type=MESH).start()

    return _split_start(name, list(grads) + lands, [(7 * nw,), (7 * nw,)], issue)


def _adam(w, g, m, v):
    m = ADAM_B1 * m + (1.0 - ADAM_B1) * g
    v = ADAM_B2 * v + (1.0 - ADAM_B2) * (g * g)
    m_hat = m / (1.0 - ADAM_B1 ** ADAM_STEP)
    v_hat = v / (1.0 - ADAM_B2 ** ADAM_STEP)
    delta = -ADAM_LR * (m_hat / (jnp.sqrt(v_hat) + ADAM_EPS) + ADAM_WD * w)
    return delta, m, v


def _sum_adam(name, landed, grad, by_cols, w, m, v, tr=256):
    R, C = w.shape
    tr = _tile(R, tr)
    mine = _block_of(_position()).astype(jnp.int32).reshape(1)

    def body(me_ref, l_ref, own_ref, w_ref, m_ref, v_ref, g_ref, d_ref, nm_ref, nv_ref):
        own = own_ref[...].astype(F32)
        g = None
        for d in range(N_DEV):
            part = jnp.where(me_ref[0] == d, own, l_ref[d].astype(F32))
            g = part if g is None else g + part
        g_ref[...] = g
        d_ref[...], nm_ref[...], nv_ref[...] = _adam(w_ref[...], g, m_ref[...], v_ref[...])

    tile = pl.BlockSpec((tr, C), lambda i, me_ref: (i, 0))
    if by_cols:
        own = pl.BlockSpec((tr, C), lambda i, me_ref: (i, me_ref[0]))
    else:
        own = pl.BlockSpec((tr, C), lambda i, me_ref: (me_ref[0] * (R // tr) + i, 0))
    return pl.pallas_call(
        body, name=name,
        grid_spec=pltpu.PrefetchScalarGridSpec(
            num_scalar_prefetch=1, grid=(R // tr,),
            in_specs=[pl.BlockSpec((N_DEV, tr, C), lambda i, me_ref: (0, i, 0)), own, tile, tile, tile],
            out_specs=[tile] * 4),
        out_shape=[jax.ShapeDtypeStruct((R, C), F32)] * 4,
        compiler_params=_params(("parallel",)),
    )(mine, landed, grad, w, m, v)


def _small_all_reduce(parts, deps=()):
    W = parts.shape[1]

    def body(p_ref, o_ref, slots, send_sems, recv_sems):
        x, y, c = _position()
        me = 4 * x + 2 * y + c
        slots[me] = jnp.sum(p_ref[...], axis=0, keepdims=True)
        peers = [(x, y, 1 - c), (1 - x, y, c), (1 - x, y, 1 - c), (x, 1 - y, c), (x, 1 - y, 1 - c),
                 (1 - x, 1 - y, c), (1 - x, 1 - y, 1 - c)]
        copies = []
        for k, peer in enumerate(peers):
            cp = pltpu.make_async_remote_copy(
                src_ref=slots.at[me], dst_ref=slots.at[me], send_sem=send_sems.at[k], recv_sem=recv_sems.at[k],
                device_id=peer, device_id_type=MESH)
            cp.start()
            copies.append(cp)
        for cp in copies:
            cp.wait()
        total = slots[0]
        for d in range(1, N_DEV):
            total = total + slots[d]
        o_ref[...] = total

    return _pcall(
        body, deps, name="small_all_reduce",
        in_specs=[pl.BlockSpec(memory_space=pltpu.VMEM)], out_specs=pl.BlockSpec(memory_space=pltpu.VMEM),
        out_shape=jax.ShapeDtypeStruct((1, W), F32),
        scratch_shapes=[pltpu.VMEM((N_DEV, 1, W), F32), pltpu.SemaphoreType.DMA((7,)), pltpu.SemaphoreType.DMA((7,))],
    )(parts)


def _adam_small(w, g, m, v):
    def body(w_ref, g_ref, m_ref, v_ref, d_ref, nm_ref, nv_ref):
        d_ref[...], nm_ref[...], nv_ref[...] = _adam(w_ref[...], g_ref[...], m_ref[...], v_ref[...])

    return pl.pallas_call(
        body, name="adam_small",
        in_specs=[pl.BlockSpec(memory_space=pltpu.VMEM)] * 4, out_specs=[pl.BlockSpec(memory_space=pltpu.VMEM)] * 3,
        out_shape=[jax.ShapeDtypeStruct(w.shape, F32)] * 3,
    )(w, g, m, v)


_GATHER_GROUPS = (("w_in",), ("w_out", "w_up", "ple_w"), ("w_down", "w_gate"))
_COL_SHARDED = ("w_in", "w_up", "ple_w")


class _MeshComm:
    def __init__(self, w, mom, var):
        self.w, self.mom, self.var = w, mom, var
        self.out = {}
        self._scatters = {}

    def gather_begin(self):
        names = [n for g in _GATHER_GROUPS for n in g]
        self._idx = {n: i for i, n in enumerate(names)}
        groups = [[self._idx[n] for n in g] for g in _GATHER_GROUPS]
        self._sems, self._lands, token = _gather_start(
            [self.w[n] for n in names], [n in _COL_SHARDED for n in names], groups)
        return token

    @staticmethod
    def _shard_size(names, offset):
        return lambda bufs, w: _shard_of(bufs[offset + w], 0, names[w] in _COL_SHARDED)

    def gather_arrive(self, gi, after):
        names = _GATHER_GROUPS[gi]
        ids = [self._idx[n] for n in names]
        self._arrived = _split_wait("gather_arrive%d" % gi, [self._lands[i] for i in ids], self._sems[2 * gi],
                                    self._sems[2 * gi + 1], [4] * len(ids), self._shard_size(names, 0), after)

    def gather_forward(self, gi):
        by_cols = [n in _COL_SHARDED for n in _GATHER_GROUPS[gi]]
        self._fsems, self._fthru, token = _gather_forward("gather_forward%d" % gi, self._arrived, by_cols)
        return token

    def gather_finish(self, gi, after):
        names = _GATHER_GROUPS[gi]
        out = _split_wait("gather_finish%d" % gi, self._fthru, self._fsems[0], self._fsems[1], [3] * len(names),
                          self._shard_size(names, 0), after)
        return dict(zip(names, out))

    def reduce_begin(self, key, grads):
        names = list(grads)
        sems, thru, token = _scatter_start("scatter_start_" + key, [grads[n] for n in names],
                                           [n in _COL_SHARDED for n in names])
        self._scatters[key] = (names, sems, thru)
        return token

    def reduce_finish(self, key, after):
        names, sems, thru = self._scatters[key]
        nw = len(names)
        out = _split_wait("scatter_wait_" + key, thru, sems[0], sems[1], [N_DEV - 1] * nw,
                          functools.partial(_first_block, offset=nw), after)
        for i, n in enumerate(names):
            self.out[n] = _sum_adam("adam_" + n, out[nw + i], out[i], n in _COL_SHARDED, self.w[n], self.mom[n],
                                    self.var[n])


def _step(x, p, target, gains, comm):
    T, D = x.shape
    n_q = D // (2 * HEAD_DIM)
    n_kv = n_q // GROUP
    cos, sin = _rope_tables(T)
    idx = _bucket_index()

    t = comm.gather_begin()
    u = _rms_fwd("norm_attn", x, gains["attn_norm_g"], deps=(t,))
    comm.gather_arrive(0, u)
    t = comm.gather_forward(0)
    bias = _bias_build(idx, gains["rel_bias_table"].reshape(-1), n_q, deps=(t,))
    full = comm.gather_finish(0, bias)
    proj = _mm_nn("in_proj", u, full["w_in"])
    pb = _qk_prep(proj, cos, sin, gains["q_norm_g"], gains["k_norm_g"], n_q + n_kv)
    o_a, lse_a = _attn_a_fwd(pb, n_q, n_kv, 2 * n_q)
    comm.gather_arrive(1, lse_a)
    t = comm.gather_forward(1)
    sink = gains["sink_logits"].reshape(-1)
    b_off = n_q + 2 * n_kv
    o_cat, lse_b = _attn_b_fwd(pb, bias, sink, o_a, b_off, n_q, n_kv, deps=(t,))
    full.update(comm.gather_finish(1, lse_b))
    h1, m_in = _mm_nn_rms("out_proj", o_cat, full["w_out"], x, gains["mlp_norm_g"])

    def up_epilogue(acc, extra, outs):
        outs[0][...] = acc.astype(BF16)
        r = jnp.maximum(acc, 0.0)
        outs[1][...] = (r * r).astype(BF16)

    a_act, f_act = _mm_nn("up_proj", m_in, full["w_up"], epilogue=up_epilogue, out_dtypes=[BF16, BF16], tn=2048)
    comm.gather_arrive(2, f_act)
    t = comm.gather_forward(2)
    p_b = p.astype(BF16)
    pe = _mm_nn("ple_proj", p_b, full["ple_w"], deps=(t,))
    full.update(comm.gather_finish(2, pe))
    h2 = _mm_nn("down_proj", f_act, full["w_down"], epilogue=_store_add, extras=(h1,), tn=256)
    gn = _rms_fwd("norm_gate", h2, gains["gate_norm_g"])

    dh3, dz, dpe, dg_final, dg_ple, loss_part = _gate_tail(gn, full["w_gate"], h2, pe, target, gains["ple_norm_g"],
                                                           gains["final_norm_g"])
    gw_gate = _mm_tn("grad_w_gate", gn, dz)
    gw_ple = _mm_tn("grad_ple_w", p_b, dpe)
    t = comm.reduce_begin("a", dict(w_gate=gw_gate, ple_w=gw_ple))
    dh2, dh2_b, dg_gate = _mm_nt_rms_bwd("d_gate_in", dz, full["w_gate"], h2, gains["gate_norm_g"], dh3, deps=(t,))
    gw_down = _mm_tn("grad_w_down", f_act, dh2_b)
    t = comm.reduce_begin("b", dict(w_down=gw_down))

    def act_bwd(acc, extra, outs):
        outs[0][...] = (acc * (2.0 * jnp.maximum(extra[0][...].astype(F32), 0.0))).astype(BF16)

    da = _mm_nt("d_act", dh2_b, full["w_down"], out_dtype=BF16, epilogue=act_bwd, extras=(a_act,), tn=2048, deps=(t,))
    gw_up = _mm_tn("grad_w_up", m_in, da)
    t = comm.reduce_begin("c", dict(w_up=gw_up))
    dm = _mm_nt("d_mlp_in", da, full["w_up"], tn=256, deps=(t,))
    dh1, dh1_b, dg_mlp = _rms_bwd("norm_mlp_bwd", dm, h1, gains["mlp_norm_g"], dh2)
    gw_out = _mm_tn("grad_w_out", o_cat, dh1_b)
    t = comm.reduce_begin("d", dict(w_out=gw_out))
    d_o = _mm_nt("d_attn_out", dh1_b, full["w_out"], out_dtype=BF16, deps=(t,))
    dqa, dka_t, dva_t = _attn_a_bwd(pb, o_cat, d_o, lse_a, n_q, n_kv)
    dka, dva = dka_t.T, dva_t.T
    dqb, dkb, dvb, dbias, dsink_raw = _attn_b_bwd(pb, o_cat, d_o, lse_b, bias, sink, b_off, n_q, n_kv, n_q)
    dtable, dsink = _table_grads(dbias, dsink_raw, idx)
    dproj, dg_q, dg_k = _dproj(proj, dqa, dka, dva, dqb, dkb, dvb, cos, sin, gains["q_norm_g"], gains["k_norm_g"])
    gw_in = _mm_tn("grad_w_in", u, dproj)
    t = comm.reduce_begin("e", dict(w_in=gw_in))
    dx, _, dg_attn = _mm_nt_rms_bwd("d_attn_in", dproj, full["w_in"], x, gains["attn_norm_g"], dh1, deps=(t,))
    for key in "abcd":
        comm.reduce_finish(key, dx)

    parts = jnp.concatenate([dg_attn, dg_mlp, dg_ple, dg_gate, dg_final, dg_q, dg_k, dtable, dsink, loss_part], axis=1)
    return dx, parts


_SHARDED = ("w_in", "w_out", "w_up", "w_down", "ple_w", "w_gate")
_VECTORS = ("attn_norm_g", "mlp_norm_g", "ple_norm_g", "gate_norm_g", "final_norm_g")
_ORDER = ("attn_norm_g", "w_in", "q_norm_g", "k_norm_g", "sink_logits", "w_out", "mlp_norm_g", "w_up", "w_down",
          "ple_w", "ple_norm_g", "gate_norm_g", "w_gate", "rel_bias_table", "final_norm_g")


def _pack_small(vals, n_heads):
    lane_pad = lambda v: jnp.pad(v, ((0, 0), (0, LANES - v.shape[1])))
    table = lane_pad(vals["rel_bias_table"].T).reshape(1, n_heads * LANES)
    return jnp.concatenate(
        [vals[n].reshape(1, -1) for n in _VECTORS] + [vals["q_norm_g"], vals["k_norm_g"], table,
                                                      lane_pad(vals["sink_logits"]), jnp.zeros((1, LANES), F32)], axis=1)


def _unpack_small(row, like, n_heads):
    out, off = {}, 0
    for n in _VECTORS:
        out[n] = row[:, off:off + like[n].size].reshape(like[n].shape)
        off += like[n].size
    for n in ("q_norm_g", "k_norm_g"):
        out[n] = row[:, off:off + LANES]
        off += LANES
    out["rel_bias_table"] = row[:, off:off + n_heads * LANES].reshape(n_heads, LANES)[:, :N_BUCKETS].T
    off += n_heads * LANES
    out["sink_logits"] = row[:, off:off + n_heads]
    off += LANES
    return out, row[0, off]


def kernel(x, p, attn_norm_g, w_in, q_norm_g, k_norm_g, sink_logits, w_out, mlp_norm_g, w_up, w_down, ple_w, ple_norm_g, gate_norm_g, w_gate, rel_bias_table, final_norm_g, loss_target, m_attn_norm_g, m_w_in, m_q_norm_g, m_k_norm_g, m_sink_logits, m_w_out, m_mlp_norm_g, m_w_up, m_w_down, m_ple_w, m_ple_norm_g, m_gate_norm_g, m_w_gate, m_rel_bias_table, m_final_norm_g, v_attn_norm_g, v_w_in, v_q_norm_g, v_k_norm_g, v_sink_logits, v_w_out, v_mlp_norm_g, v_w_up, v_w_down, v_ple_w, v_ple_norm_g, v_gate_norm_g, v_w_gate, v_rel_bias_table, v_final_norm_g):
    w = dict(attn_norm_g=attn_norm_g, w_in=w_in[0], q_norm_g=q_norm_g, k_norm_g=k_norm_g, sink_logits=sink_logits,
             w_out=w_out[0], mlp_norm_g=mlp_norm_g, w_up=w_up[0], w_down=w_down[0], ple_w=ple_w[0],
             ple_norm_g=ple_norm_g, gate_norm_g=gate_norm_g, w_gate=w_gate[0], rel_bias_table=rel_bias_table,
             final_norm_g=final_norm_g)
    mom = dict(attn_norm_g=m_attn_norm_g, w_in=m_w_in[0], q_norm_g=m_q_norm_g, k_norm_g=m_k_norm_g,
               sink_logits=m_sink_logits, w_out=m_w_out[0], mlp_norm_g=m_mlp_norm_g, w_up=m_w_up[0],
               w_down=m_w_down[0], ple_w=m_ple_w[0], ple_norm_g=m_ple_norm_g, gate_norm_g=m_gate_norm_g,
               w_gate=m_w_gate[0], rel_bias_table=m_rel_bias_table, final_norm_g=m_final_norm_g)
    var = dict(attn_norm_g=v_attn_norm_g, w_in=v_w_in[0], q_norm_g=v_q_norm_g, k_norm_g=v_k_norm_g,
               sink_logits=v_sink_logits, w_out=v_w_out[0], mlp_norm_g=v_mlp_norm_g, w_up=v_w_up[0],
               w_down=v_w_down[0], ple_w=v_ple_w[0], ple_norm_g=v_ple_norm_g, gate_norm_g=v_gate_norm_g,
               w_gate=v_w_gate[0], rel_bias_table=v_rel_bias_table, final_norm_g=v_final_norm_g)
    D = x.shape[-1]
    n_heads = D // (2 * HEAD_DIM)

    gains = {n: w[n] for n in w if n not in _SHARDED}
    gains["final_norm_g"] = final_norm_g.reshape(1, -1)

    comm = _MeshComm(w, mom, var)
    dx, parts = _step(x[0], p[0, 0], loss_target[0], gains, comm)

    small_g = _small_all_reduce(parts, deps=[comm.out[n][0] for n in comm.out])
    comm.reduce_finish("e", small_g)

    g_out, d_out, m_out, v_out = {}, {}, {}, {}
    for n in _SHARDED:
        g, d, nm, nv = comm.out[n]
        g_out[n], d_out[n], m_out[n], v_out[n] = g[None], d[None], nm[None], nv[None]

    small = {n: v for n, v in w.items() if n not in _SHARDED}
    pack = lambda vals: _pack_small({n: vals[n] for n in small}, n_heads)
    sd, sm, sv = _adam_small(pack(w), small_g, pack(mom), pack(var))
    sg, loss = _unpack_small(small_g, small, n_heads)
    g_out.update(sg)
    for dst, row in ((d_out, sd), (m_out, sm), (v_out, sv)):
        dst.update(_unpack_small(row, small, n_heads)[0])

    return (loss, dx[None], *[g_out[n] for n in _ORDER], *[d_out[n] for n in _ORDER],
            *[m_out[n] for n in _ORDER], *[v_out[n] for n in _ORDER])
```

```python
import functools
import math

import numpy as np
import jax
import jax.numpy as jnp
from jax import lax
from jax.experimental import pallas as pl
from jax.experimental.pallas import tpu as pltpu

F32 = jnp.float32
BF16 = jnp.bfloat16

N_DEV = 8
N_CHIP = 4
HEAD_DIM = 128
GROUP = 4
GRID_W = 64
WINDOW = 128
BLOCK_Q = 128
N_BUCKETS = 32
MAX_DISTANCE = 128
ROPE_THETA = 10000.0
EPS = 1e-6
NEG_INF = -1e30
ADAM_LR = 0.001
ADAM_B1 = 0.9
ADAM_B2 = 0.999
ADAM_EPS = 1e-08
ADAM_WD = 0.01
ADAM_STEP = 10
LOG2E = math.log2(math.e)
LANES = 128
SUBLANES = 8
MESH = pl.DeviceIdType.MESH

_NT = (((1,), (1,)), ((), ()))
_NN = (((1,), (0,)), ((), ()))
_TN = (((0,), (0,)), ((), ()))


def _tile(dim, pref):
    return pref if dim % pref == 0 else dim


def _params(sem):
    return pltpu.CompilerParams(dimension_semantics=sem, vmem_limit_bytes=56 * 1024 * 1024)


_HBM = pl.BlockSpec(memory_space=pltpu.HBM)
_SEM = pl.BlockSpec(memory_space=pltpu.SEMAPHORE)
_ANY = pl.BlockSpec(memory_space=pl.ANY)
_VMEM = pl.BlockSpec(memory_space=pltpu.VMEM)
_EFFECT = pltpu.SideEffectType.DATAFLOW_SIDE_EFFECTING


def _pcall(body, deps=(), *, in_specs, into=None, **kw):
    deps = [d for d in deps if d is not None]
    nd = len(deps)
    if into is not None:
        deps = [into[0]] + deps
        nd += 1
        kw["input_output_aliases"] = {0: into[1]}

    def wrapped(*refs):
        body(*refs[nd:])

    call = pl.pallas_call(wrapped, in_specs=[_ANY] * nd + list(in_specs), **kw)
    return lambda *args: call(*deps, *args)


def _mm(name, a, b, dims, grid, a_spec, b_spec, out_shape, out_specs, acc_shape, epilogue,
        extras=(), extra_specs=(), deps=(), semantics=("parallel", "parallel", "arbitrary")):
    nk = grid[2]
    n_extra = len(extras)

    def body(*refs):
        a_ref, b_ref = refs[0], refs[1]
        extra = refs[2:2 + n_extra]
        outs = refs[2 + n_extra:-1]
        acc = refs[-1]
        part = lax.dot_general(a_ref[...], b_ref[...], dims, preferred_element_type=F32)
        if nk == 1:
            epilogue(part, extra, outs)
        else:
            k = pl.program_id(2)

            @pl.when(k == 0)
            def _():
                acc[...] = part

            @pl.when(k > 0)
            def _():
                acc[...] += part

            @pl.when(k == nk - 1)
            def _():
                epilogue(acc[...], extra, outs)

    return _pcall(
        body, deps, name=name, grid=grid,
        in_specs=[a_spec, b_spec, *extra_specs],
        out_specs=out_specs, out_shape=out_shape,
        scratch_shapes=[pltpu.VMEM(acc_shape if nk > 1 else (SUBLANES, LANES), F32)],
        compiler_params=_params(semantics),
    )(a, b, *extras)


def _store(dtype):
    def ep(acc, extra, outs):
        outs[0][...] = acc.astype(dtype)
    return ep


def _store_add(acc, extra, outs):
    outs[0][...] = acc + extra[0][...]


def _mm_nn(name, a, b, out_dtype=F32, epilogue=None, extras=(), n_out=1, out_dtypes=None, tm=1024, tn=1024, tk=None,
           deps=()):
    M, K = a.shape
    N = b.shape[1]
    tm, tn, tk = _tile(M, tm), _tile(N, tn), _tile(K, tk or K)
    b_spec = pl.BlockSpec((tk, tn), lambda i, j, k: (k, j))
    grid = (M // tm, N // tn, K // tk)
    o_spec = pl.BlockSpec((tm, tn), lambda i, j, k: (i, j))
    out_dtypes = out_dtypes or [out_dtype] * n_out
    out_shape = [jax.ShapeDtypeStruct((M, N), d) for d in out_dtypes]
    res = _mm(name, a, b, _NN, grid, pl.BlockSpec((tm, tk), lambda i, j, k: (i, k)), b_spec,
              out_shape, [o_spec] * len(out_dtypes), (tm, tn), epilogue or _store(out_dtype),
              extras, [o_spec] * len(extras), deps)
    return res if len(out_dtypes) > 1 else res[0]


def _mm_nt(name, a, b, out_dtype=F32, epilogue=None, extras=(), tm=1024, tn=1024, tk=None, deps=()):
    M, C = a.shape
    N = b.shape[0]
    tm, tn, tk = _tile(M, tm), _tile(N, tn), _tile(C, tk or C)
    b_spec = pl.BlockSpec((tn, tk), lambda i, j, k: (j, k))
    grid = (M // tm, N // tn, C // tk)
    o_spec = pl.BlockSpec((tm, tn), lambda i, j, k: (i, j))
    return _mm(name, a, b, _NT, grid, pl.BlockSpec((tm, tk), lambda i, j, k: (i, k)), b_spec,
               [jax.ShapeDtypeStruct((M, N), out_dtype)], [o_spec], (tm, tn), epilogue or _store(out_dtype),
               extras, [o_spec] * len(extras), deps)[0]


def _mm_tn(name, a, b, out_dtype=BF16, tm=1024, tn=512, tk=None, deps=()):
    T, M = a.shape
    N = b.shape[1]
    tm, tn, tk = _tile(M, tm), _tile(N, tn), _tile(T, tk or T)
    out_shape = jax.ShapeDtypeStruct((M, N), out_dtype)
    o_spec = pl.BlockSpec((tm, tn), lambda i, j, k: (i, j))
    grid = (M // tm, N // tn, T // tk)
    return _mm(name, a, b, _TN, grid, pl.BlockSpec((tk, tm), lambda i, j, k: (k, i)),
               pl.BlockSpec((tk, tn), lambda i, j, k: (k, j)), [out_shape], [o_spec], (tm, tn), _store(out_dtype),
               deps=deps)[0]


def _mean_last(v):
    return jnp.mean(v, axis=-1, keepdims=True)


def _rows_to_sublanes(v):
    r, c = v.shape
    return jnp.sum(v.reshape(r // SUBLANES, SUBLANES, c), axis=0)


def _accumulate(ref, val, first):
    @pl.when(first)
    def _():
        ref[...] = val

    @pl.when(jnp.logical_not(first))
    def _():
        ref[...] += val


def _rms_fwd(name, x, g, tr=256, deps=()):
    T, D = x.shape
    tr = _tile(T, tr)

    def body(x_ref, g_ref, o_ref):
        xv = x_ref[...]
        r = lax.rsqrt(_mean_last(xv * xv) + EPS)
        o_ref[...] = (xv * r * g_ref[...]).astype(BF16)

    row = pl.BlockSpec((tr, D), lambda i: (i, 0))
    return _pcall(
        body, deps, name=name, grid=(T // tr,),
        in_specs=[row, pl.BlockSpec((1, D), lambda i: (0, 0))],
        out_specs=row, out_shape=jax.ShapeDtypeStruct((T, D), BF16),
        compiler_params=_params(("parallel",)),
    )(x, g)


def _rms_bwd(name, dyn, x, g, dres, tr=256, deps=()):
    T, D = x.shape
    tr = _tile(T, tr)

    def body(dy_ref, x_ref, g_ref, dr_ref, dx_ref, dxb_ref, dg_ref):
        xv = x_ref[...]
        r = lax.rsqrt(_mean_last(xv * xv) + EPS)
        xn = xv * r
        dy = dy_ref[...]
        dxn = dy * g_ref[...]
        dx = dr_ref[...] + r * (dxn - xn * _mean_last(dxn * xn))
        dx_ref[...] = dx
        dxb_ref[...] = dx.astype(BF16)
        _accumulate(dg_ref, _rows_to_sublanes(dy * xn), pl.program_id(0) == 0)

    row = pl.BlockSpec((tr, D), lambda i: (i, 0))
    return _pcall(
        body, deps, name=name, grid=(T // tr,),
        in_specs=[row, row, pl.BlockSpec((1, D), lambda i: (0, 0)), row],
        out_specs=[row, row, pl.BlockSpec((SUBLANES, D), lambda i: (0, 0))],
        out_shape=[jax.ShapeDtypeStruct((T, D), F32), jax.ShapeDtypeStruct((T, D), BF16),
                   jax.ShapeDtypeStruct((SUBLANES, D), F32)],
        compiler_params=_params(("arbitrary",)),
    )(dyn, x, g, dres)


def _mm_nn_rms(name, a, b, res, g, tm=512, deps=()):
    M, K = a.shape
    N = b.shape[1]
    tm = _tile(M, tm)

    def epilogue(acc, extra, outs):
        h = acc + extra[0][...]
        outs[0][...] = h
        outs[1][...] = (h * lax.rsqrt(_mean_last(h * h) + EPS) * extra[1][...]).astype(BF16)

    row = pl.BlockSpec((tm, N), lambda i, j, k: (i, 0))
    return _mm(name, a, b, _NN, (M // tm, 1, 1), pl.BlockSpec((tm, K), lambda i, j, k: (i, 0)),
               pl.BlockSpec((K, N), lambda i, j, k: (0, 0)),
               [jax.ShapeDtypeStruct((M, N), F32), jax.ShapeDtypeStruct((M, N), BF16)], [row, row], (tm, N), epilogue,
               (res, g), [row, pl.BlockSpec((1, N), lambda i, j, k: (0, 0))], deps)


def _mm_nt_rms_bwd(name, a, b, x, g, dres, tm=256, deps=()):
    M, C = a.shape
    N = b.shape[0]
    tm = _tile(M, tm)

    def epilogue(dy, extra, outs):
        x_ref, dr_ref, g_ref = extra
        xv = x_ref[...]
        r = lax.rsqrt(_mean_last(xv * xv) + EPS)
        xn = xv * r
        dxn = dy * g_ref[...]
        dx = dr_ref[...] + r * (dxn - xn * _mean_last(dxn * xn))
        outs[0][...] = dx
        outs[1][...] = dx.astype(BF16)
        _accumulate(outs[2], _rows_to_sublanes(dy * xn), pl.program_id(0) == 0)

    row = pl.BlockSpec((tm, N), lambda i, j, k: (i, 0))
    return _mm(name, a, b, _NT, (M // tm, 1, 1), pl.BlockSpec((tm, C), lambda i, j, k: (i, 0)),
               pl.BlockSpec((N, C), lambda i, j, k: (0, 0)),
               [jax.ShapeDtypeStruct((M, N), F32), jax.ShapeDtypeStruct((M, N), BF16),
                jax.ShapeDtypeStruct((SUBLANES, N), F32)],
               [row, row, pl.BlockSpec((SUBLANES, N), lambda i, j, k: (0, 0))], (tm, N), epilogue,
               (x, dres, g), [row, row, pl.BlockSpec((1, N), lambda i, j, k: (0, 0))], deps,
               semantics=("arbitrary", "arbitrary", "arbitrary"))


def _gate_tail(gn, w_gate, h2, pe, target, g_ple, g_final, tm=256):
    T, D = h2.shape
    tm = _tile(T, tm)

    def epilogue(z, extra, outs):
        h2_ref, pe_ref, t_ref, gp_ref, gf_ref = extra
        dh3_ref, dz_ref, dpe_ref, dgf_ref, dgp_ref, loss_ref = outs
        first = pl.program_id(0) == 0
        pev = pe_ref[...]
        r3 = lax.rsqrt(_mean_last(pev * pev) + EPS)
        en = pev * r3
        e = en * gp_ref[...]
        gate = 1.0 / (1.0 + jnp.exp(-z))
        h3 = h2_ref[...] + gate * e
        r5 = lax.rsqrt(_mean_last(h3 * h3) + EPS)
        hn = h3 * r5
        diff = hn * gf_ref[...] - t_ref[...]
        loss_rows = 0.5 * _mean_last(diff * diff)
        row0 = lax.broadcasted_iota(jnp.int32, (SUBLANES, LANES), 0) == 0
        _accumulate(loss_ref, jnp.where(row0, jnp.sum(loss_rows), 0.0), first)
        dy = diff * (1.0 / D)
        _accumulate(dgf_ref, _rows_to_sublanes(dy * hn), first)
        dhn = dy * gf_ref[...]
        dh3 = r5 * (dhn - hn * _mean_last(dhn * hn))
        dh3_ref[...] = dh3
        dgate = dh3 * e
        de = dh3 * gate
        dz_ref[...] = (dgate * gate * (1.0 - gate)).astype(BF16)
        _accumulate(dgp_ref, _rows_to_sublanes(de * en), first)
        den = de * gp_ref[...]
        dpe_ref[...] = (r3 * (den - en * _mean_last(den * en))).astype(BF16)

    row = pl.BlockSpec((tm, D), lambda i, j, k: (i, 0))
    vec = pl.BlockSpec((1, D), lambda i, j, k: (0, 0))
    part = pl.BlockSpec((SUBLANES, D), lambda i, j, k: (0, 0))
    return _mm("gate_tail", gn, w_gate, _NN, (T // tm, 1, 1), row, pl.BlockSpec(w_gate.shape, lambda i, j, k: (0, 0)),
               [jax.ShapeDtypeStruct((T, D), F32), jax.ShapeDtypeStruct((T, D), BF16),
                jax.ShapeDtypeStruct((T, D), BF16), jax.ShapeDtypeStruct((SUBLANES, D), F32),
                jax.ShapeDtypeStruct((SUBLANES, D), F32), jax.ShapeDtypeStruct((SUBLANES, LANES), F32)],
               [row, row, row, part, part, pl.BlockSpec((SUBLANES, LANES), lambda i, j, k: (0, 0))], (tm, D), epilogue,
               (h2, pe, target, g_ple, g_final), [row, row, row, vec, vec],
               semantics=("arbitrary", "arbitrary", "arbitrary"))


def _rope_tables(T):
    pos = np.arange(T)
    half = HEAD_DIM // 2
    inv = (ROPE_THETA ** (-np.arange(0, half, 2, dtype=np.float32) / half)).astype(np.float32)
    ang_r = (pos // GRID_W).astype(np.float32)[:, None] * inv
    ang_c = (pos % GRID_W).astype(np.float32)[:, None] * inv
    cos = np.concatenate([np.cos(ang_r), np.cos(ang_r), np.cos(ang_c), np.cos(ang_c)], axis=-1)
    sin = np.concatenate([-np.sin(ang_r), np.sin(ang_r), -np.sin(ang_c), np.sin(ang_c)], axis=-1)
    return jnp.asarray(cos, F32), jnp.asarray(sin, F32)


def _swap32(x):
    lane = lax.broadcasted_iota(jnp.int32, x.shape, 1)
    return jnp.where((lane % 64) < 32, pltpu.roll(x, 96, 1), pltpu.roll(x, 32, 1))


def _in_proj(u, w_in, cos, sin, g_q, g_k, n_norm, tm=512):
    T, K = u.shape
    W = w_in.shape[1]
    tm = _tile(T, tm)
    n_q = n_norm * GROUP // (GROUP + 1)
    wa = n_norm * HEAD_DIM

    def epilogue(acc, extra, outs):
        c_ref, s_ref, gq_ref, gk_ref = extra
        raw_ref, o_ref = outs
        c, s = c_ref[...], s_ref[...]
        raw_ref[...] = acc[:, :wa]
        for h in range(n_norm):
            cols = slice(h * HEAD_DIM, (h + 1) * HEAD_DIM)
            xv = acc[:, cols]
            g = gq_ref[...] if h < n_q else gk_ref[...]
            xn = xv * lax.rsqrt(_mean_last(xv * xv) + EPS) * g
            o_ref[:, cols] = (xn * c + _swap32(xn) * s).astype(BF16)
        o_ref[:, wa:] = acc[:, wa:].astype(BF16)

    tab = pl.BlockSpec((tm, HEAD_DIM), lambda i, j, k: (i, 0))
    vec = pl.BlockSpec((1, HEAD_DIM), lambda i, j, k: (0, 0))
    return _mm("in_proj", u, w_in, _NN, (T // tm, 1, 1), pl.BlockSpec((tm, K), lambda i, j, k: (i, 0)),
               pl.BlockSpec((K, W), lambda i, j, k: (0, 0)),
               [jax.ShapeDtypeStruct((T, wa), F32), jax.ShapeDtypeStruct((T, W), BF16)],
               [pl.BlockSpec((tm, wa), lambda i, j, k: (i, 0)), pl.BlockSpec((tm, W), lambda i, j, k: (i, 0))],
               (tm, W), epilogue, (cos, sin, g_q, g_k), [tab, tab, vec, vec])


def _dproj(proj_a, dqa, dka, dva, dqb, dkb, dvb, cos, sin, g_q, g_k, tr=256):
    T, wa = proj_a.shape
    tr = _tile(T, tr)
    n_q = dqa.shape[1] // HEAD_DIM
    n_kv = dka.shape[1] // HEAD_DIM
    W = wa + dva.shape[1] + dqb.shape[1] + dkb.shape[1] + dvb.shape[1]

    def body(p_ref, dqa_ref, dka_ref, dva_ref, dqb_ref, dkb_ref, dvb_ref, c_ref, s_ref, gq_ref, gk_ref,
             o_ref, dgq_ref, dgk_ref):
        c, s = c_ref[...], s_ref[...]
        dgq = jnp.zeros((SUBLANES, HEAD_DIM), F32)
        dgk = jnp.zeros((SUBLANES, HEAD_DIM), F32)
        for h in range(n_q + n_kv):
            cols = slice(h * HEAD_DIM, (h + 1) * HEAD_DIM)
            xv = p_ref[:, cols]
            r = lax.rsqrt(_mean_last(xv * xv) + EPS)
            xn = xv * r
            if h < n_q:
                d = dqa_ref[:, cols]
                g = gq_ref[...]
            else:
                d = dka_ref[:, (h - n_q) * HEAD_DIM:(h - n_q + 1) * HEAD_DIM]
                g = gk_ref[...]
            dqn = d * c + _swap32(d * s)
            part = _rows_to_sublanes(dqn * xn)
            if h < n_q:
                dgq = dgq + part
            else:
                dgk = dgk + part
            dxn = dqn * g
            o_ref[:, cols] = (r * (dxn - xn * _mean_last(dxn * xn))).astype(BF16)
        off = wa
        for ref in (dva_ref, dqb_ref, dkb_ref, dvb_ref):
            w = ref.shape[1]
            o_ref[:, off:off + w] = ref[...].astype(BF16)
            off += w
        first = pl.program_id(0) == 0
        _accumulate(dgq_ref, dgq, first)
        _accumulate(dgk_ref, dgk, first)

    def row(w):
        return pl.BlockSpec((tr, w), lambda i: (i, 0))

    vec = pl.BlockSpec((1, HEAD_DIM), lambda i: (0, 0))
    part = pl.BlockSpec((SUBLANES, HEAD_DIM), lambda i: (0, 0))
    return pl.pallas_call(
        body, name="dproj", grid=(T // tr,),
        in_specs=[row(wa), row(dqa.shape[1]), row(dka.shape[1]), row(dva.shape[1]), row(dqb.shape[1]),
                  row(dkb.shape[1]), row(dvb.shape[1]), row(HEAD_DIM), row(HEAD_DIM), vec, vec],
        out_specs=[row(W), part, part],
        out_shape=[jax.ShapeDtypeStruct((T, W), BF16), jax.ShapeDtypeStruct((SUBLANES, HEAD_DIM), F32),
                   jax.ShapeDtypeStruct((SUBLANES, HEAD_DIM), F32)],
        compiler_params=_params(("arbitrary",)),
    )(proj_a, dqa, dka, dva, dqb, dkb, dvb, cos, sin, g_q, g_k)


def _attn_a_fwd(pb, n_q, n_kv, out_heads, tq=1024, tc=1024):
    T = pb.shape[0]
    tq, tc = _tile(T, tq), _tile(T, tc)
    scale = HEAD_DIM ** -0.5
    c = scale * LOG2E

    def body(q_ref, k_ref, v_ref, o_ref, lse_ref):
        q = q_ref[...]
        m = l = acc = None
        for j in range(T // tc):
            keys = slice(j * tc, (j + 1) * tc)
            s = lax.dot_general(q, k_ref[keys, :], _NT, preferred_element_type=F32)
            mj = jnp.max(s, axis=-1, keepdims=True)
            m_new = mj if j == 0 else jnp.maximum(m, mj)
            p = jnp.exp2((s - m_new) * c)
            pv = lax.dot_general(p.astype(BF16), v_ref[keys, :], _NN, preferred_element_type=F32)
            if j == 0:
                l, acc = jnp.sum(p, axis=-1, keepdims=True), pv
            else:
                alpha = jnp.exp2((m - m_new) * c)
                l = alpha * l + jnp.sum(p, axis=-1, keepdims=True)
                acc = alpha * acc + pv
            m = m_new
        o_ref[...] = (acc / l).astype(BF16)
        lse_ref[...] = m * scale + jnp.log(l)

    return pl.pallas_call(
        body, name="attn_a_fwd", grid=(n_kv, GROUP, T // tq),
        in_specs=[pl.BlockSpec((tq, HEAD_DIM), lambda kv, g, i: (i, kv * GROUP + g)),
                  pl.BlockSpec((T, HEAD_DIM), lambda kv, g, i: (0, n_q + kv)),
                  pl.BlockSpec((T, HEAD_DIM), lambda kv, g, i: (0, n_q + n_kv + kv))],
        out_specs=[pl.BlockSpec((tq, HEAD_DIM), lambda kv, g, i: (i, kv * GROUP + g)),
                   pl.BlockSpec((None, tq, 1), lambda kv, g, i: (kv * GROUP + g, i, 0))],
        out_shape=[jax.ShapeDtypeStruct((T, out_heads * HEAD_DIM), BF16), jax.ShapeDtypeStruct((n_q, T, 1), F32)],
        compiler_params=_params(("parallel", "parallel", "parallel")),
    )(pb, pb, pb)


def _attn_a_bwd(pb, o_cat, d_o, lse, n_q, n_kv, tq=512, tc=512):
    T = pb.shape[0]
    tq, tc = _tile(T, tq), _tile(T, tc)
    scale = HEAD_DIM ** -0.5
    c = scale * LOG2E

    def body(q_ref, k_ref, v_ref, o_ref, do_ref, lse_ref, dq_ref, dkt_ref, dvt_ref):
        q, do = q_ref[...], do_ref[...]
        qt, dot = q.T, do.T
        delta = jnp.sum(do.astype(F32) * o_ref[...].astype(F32), axis=-1, keepdims=True)
        lse2 = lse_ref[...] * LOG2E

        @pl.when(jnp.logical_and(pl.program_id(1) == 0, pl.program_id(2) == 0))
        def _():
            dkt_ref[...] = jnp.zeros(dkt_ref.shape, F32)
            dvt_ref[...] = jnp.zeros(dvt_ref.shape, F32)

        dq = None
        for j in range(T // tc):
            keys = slice(j * tc, (j + 1) * tc)
            kc, vc = k_ref[keys, :], v_ref[keys, :]
            s = lax.dot_general(q, kc, _NT, preferred_element_type=F32)
            p = jnp.exp2(s * c - lse2)
            dp = lax.dot_general(do, vc, _NT, preferred_element_type=F32)
            ds = (p * (dp - delta) * scale).astype(BF16)
            dqj = lax.dot_general(ds, kc, _NN, preferred_element_type=F32)
            dq = dqj if dq is None else dq + dqj
            dvt_ref[:, keys] += lax.dot_general(dot, p.astype(BF16), _NN, preferred_element_type=F32)
            dkt_ref[:, keys] += lax.dot_general(qt, ds, _NN, preferred_element_type=F32)
        dq_ref[...] = dq

    qmap = lambda kv, g, i: (i, kv * GROUP + g)
    return pl.pallas_call(
        body, name="attn_a_bwd", grid=(n_kv, GROUP, T // tq),
        in_specs=[pl.BlockSpec((tq, HEAD_DIM), qmap),
                  pl.BlockSpec((T, HEAD_DIM), lambda kv, g, i: (0, n_q + kv)),
                  pl.BlockSpec((T, HEAD_DIM), lambda kv, g, i: (0, n_q + n_kv + kv)),
                  pl.BlockSpec((tq, HEAD_DIM), qmap),
                  pl.BlockSpec((tq, HEAD_DIM), qmap),
                  pl.BlockSpec((None, tq, 1), lambda kv, g, i: (kv * GROUP + g, i, 0))],
        out_specs=[pl.BlockSpec((tq, HEAD_DIM), qmap),
                   pl.BlockSpec((HEAD_DIM, T), lambda kv, g, i: (kv, 0)),
                   pl.BlockSpec((HEAD_DIM, T), lambda kv, g, i: (kv, 0))],
        out_shape=[jax.ShapeDtypeStruct((T, n_q * HEAD_DIM), F32),
                   jax.ShapeDtypeStruct((n_kv * HEAD_DIM, T), F32),
                   jax.ShapeDtypeStruct((n_kv * HEAD_DIM, T), F32)],
        compiler_params=_params(("parallel", "arbitrary", "arbitrary")),
    )(pb, pb, pb, o_cat, d_o, lse)


def _bucket_index():
    r = np.arange(BLOCK_Q)[:, None]
    j = np.arange(3 * BLOCK_Q)[None, :]
    rel = (j - BLOCK_Q) - r
    nb = N_BUCKETS // 2
    ret = np.where(rel > 0, nb, 0)
    n = np.abs(rel)
    max_exact = nb // 2
    nf = np.maximum(n, 1).astype(np.float32)
    large = max_exact + (np.log(nf / max_exact) / math.log(MAX_DISTANCE / max_exact) * (nb - max_exact)).astype(np.int32)
    large = np.minimum(large, nb - 1)
    return jnp.asarray(ret + np.where(n < max_exact, n, large), jnp.int32)


def _bias_build(idx, table_flat, n_heads, deps=()):
    def body(idx_ref, tab_ref, o_ref):
        h = pl.program_id(0)
        iv = idx_ref[...]
        acc = jnp.zeros(iv.shape, F32)
        for b in range(N_BUCKETS):
            acc = jnp.where(iv == b, tab_ref[b * n_heads + h], acc)
        r = lax.broadcasted_iota(jnp.int32, iv.shape, 0)
        j = lax.broadcasted_iota(jnp.int32, iv.shape, 1)
        o_ref[...] = jnp.where(jnp.abs(j - BLOCK_Q - r) <= WINDOW, acc, NEG_INF)

    return _pcall(
        body, deps, name="bias_build", grid=(n_heads,),
        in_specs=[pl.BlockSpec(idx.shape, lambda h: (0, 0)), pl.BlockSpec(memory_space=pltpu.SMEM)],
        out_specs=pl.BlockSpec((None,) + idx.shape, lambda h: (h, 0, 0)),
        out_shape=jax.ShapeDtypeStruct((n_heads,) + idx.shape, F32),
        compiler_params=_params(("parallel",)),
    )(idx, table_flat)


def _in_sequence(n, T):
    j = lax.broadcasted_iota(jnp.int32, (GROUP * BLOCK_Q, 3 * BLOCK_Q), 1)
    kabs = n * BLOCK_Q + j - BLOCK_Q
    return (kabs >= 0) & (kabs < T)


def _per_head_rows(values):
    head = lax.broadcasted_iota(jnp.int32, (GROUP * BLOCK_Q, 1), 0) // BLOCK_Q
    col = jnp.zeros((GROUP * BLOCK_Q, 1), F32)
    for g, v in enumerate(values):
        col = jnp.where(head == g, v, col)
    return col


def _band_specs(col, nblk, sb):
    return [pl.BlockSpec((BLOCK_Q, HEAD_DIM), lambda kv, i: (jnp.maximum(sb * i - 1, 0), col(kv))),
            pl.BlockSpec((sb * BLOCK_Q, HEAD_DIM), lambda kv, i: (i, col(kv))),
            pl.BlockSpec((BLOCK_Q, HEAD_DIM), lambda kv, i: (jnp.minimum(sb * i + sb, nblk - 1), col(kv)))]


def _head_specs(base, rows):
    return [pl.BlockSpec((rows, HEAD_DIM), functools.partial(lambda kv, i, g: (i, base + kv * GROUP + g), g=g))
            for g in range(GROUP)]


def _attn_b_fwd(pb, bias, sink, o_all, q_off, n_q, n_kv, deps=(), sb=8):
    T = pb.shape[0]
    nblk = T // BLOCK_Q
    sb = min(sb, nblk)
    tq = sb * BLOCK_Q
    scale = HEAD_DIM ** -0.5

    def body(*refs):
        q_refs = refs[0:GROUP]
        k_refs, v_refs = refs[GROUP:GROUP + 3], refs[GROUP + 3:GROUP + 6]
        bias_ref, sink_ref, o_ref, lse_ref = refs[GROUP + 6:]
        kv, i = pl.program_id(0), pl.program_id(1)
        kb = jnp.concatenate([r[...] for r in k_refs], axis=0)
        vb = jnp.concatenate([r[...] for r in v_refs], axis=0)
        bias_all = bias_ref[...].reshape(GROUP * BLOCK_Q, 3 * BLOCK_Q)
        sk = _per_head_rows([sink_ref[kv * GROUP + g] for g in range(GROUP)])
        for b in range(sb):
            rows = slice(b * BLOCK_Q, (b + 1) * BLOCK_Q)
            kw, vw = kb[b * BLOCK_Q:(b + 3) * BLOCK_Q], vb[b * BLOCK_Q:(b + 3) * BLOCK_Q]
            q = jnp.concatenate([r[rows, :] for r in q_refs], axis=0)
            s = lax.dot_general(q, kw, _NT, preferred_element_type=F32) * scale + bias_all
            if b == 0 or b == sb - 1:
                s = jnp.where(_in_sequence(i * sb + b, T), s, NEG_INF)
            m = jnp.maximum(jnp.max(s, axis=-1, keepdims=True), sk)
            p = jnp.exp(s - m)
            l = jnp.sum(p, axis=-1, keepdims=True) + jnp.exp(sk - m)
            o = (lax.dot_general(p.astype(BF16), vw, _NN, preferred_element_type=F32) / l).astype(BF16)
            lse = m + jnp.log(l)
            for g in range(GROUP):
                head = slice(g * BLOCK_Q, (g + 1) * BLOCK_Q)
                o_ref[rows, g * HEAD_DIM:(g + 1) * HEAD_DIM] = o[head]
                lse_ref[g, rows, :] = lse[head]

    first_group = o_all.shape[1] // (GROUP * HEAD_DIM) - n_kv
    return _pcall(
        body, deps, into=(o_all, 0), name="attn_b_fwd", grid=(n_kv, nblk // sb),
        in_specs=[*_head_specs(q_off, tq),
                  *_band_specs(lambda kv: q_off + n_q + kv, nblk, sb),
                  *_band_specs(lambda kv: q_off + n_q + n_kv + kv, nblk, sb),
                  pl.BlockSpec((GROUP, BLOCK_Q, 3 * BLOCK_Q), lambda kv, i: (kv, 0, 0)),
                  pl.BlockSpec(memory_space=pltpu.SMEM)],
        out_specs=[pl.BlockSpec((tq, GROUP * HEAD_DIM), lambda kv, i: (i, first_group + kv)),
                   pl.BlockSpec((GROUP, tq, 1), lambda kv, i: (kv, i, 0))],
        out_shape=[jax.ShapeDtypeStruct(o_all.shape, BF16), jax.ShapeDtypeStruct((n_q, T, 1), F32)],
        compiler_params=_params(("parallel", "parallel")),
    )(*([pb] * (GROUP + 6)), bias, sink)


def _attn_b_bwd(pb, o_cat, d_o, lse, bias, sink, q_off, n_q, n_kv, o_off, deps=(), sb=8):
    T = pb.shape[0]
    nblk = T // BLOCK_Q
    sb = min(sb, nblk)
    tq = sb * BLOCK_Q
    scale = HEAD_DIM ** -0.5

    def body(*refs):
        q_refs = refs[0:GROUP]
        k_refs, v_refs = refs[GROUP:GROUP + 3], refs[GROUP + 3:GROUP + 6]
        o_refs, do_refs = refs[GROUP + 6:2 * GROUP + 6], refs[2 * GROUP + 6:3 * GROUP + 6]
        lse_ref, bias_ref, sink_ref, dq_ref, dk_ref, dv_ref, dbias_ref, dsink_ref, dkb_ref, dvb_ref = refs[3 * GROUP + 6:]
        kv, i = pl.program_id(0), pl.program_id(1)
        first = i == 0

        @pl.when(first)
        def _():
            dk_ref[...] = jnp.zeros(dk_ref.shape, F32)
            dv_ref[...] = jnp.zeros(dv_ref.shape, F32)
            dbias_ref[...] = jnp.zeros(dbias_ref.shape, F32)

        kb = jnp.concatenate([r[...] for r in k_refs], axis=0)
        vb = jnp.concatenate([r[...] for r in v_refs], axis=0)
        dkb_ref[...] = jnp.zeros(dkb_ref.shape, F32)
        dvb_ref[...] = jnp.zeros(dvb_ref.shape, F32)
        row = lax.broadcasted_iota(jnp.int32, (SUBLANES, LANES), 0)
        dsink = jnp.zeros((SUBLANES, LANES), F32)
        bias_all = bias_ref[...].reshape(GROUP * BLOCK_Q, 3 * BLOCK_Q)
        sk = _per_head_rows([sink_ref[kv * GROUP + g] for g in range(GROUP)])
        for b in range(sb):
            rows = slice(b * BLOCK_Q, (b + 1) * BLOCK_Q)
            win = slice(b * BLOCK_Q, (b + 3) * BLOCK_Q)
            kw, vw = kb[win], vb[win]
            q = jnp.concatenate([r[rows, :] for r in q_refs], axis=0)
            do = jnp.concatenate([r[rows, :] for r in do_refs], axis=0)
            o = jnp.concatenate([r[rows, :] for r in o_refs], axis=0)
            lse = jnp.concatenate([lse_ref[g, rows, :] for g in range(GROUP)], axis=0)
            delta = jnp.sum(do.astype(F32) * o.astype(F32), axis=-1, keepdims=True)
            s = lax.dot_general(q, kw, _NT, preferred_element_type=F32) * scale + bias_all
            if b == 0 or b == sb - 1:
                s = jnp.where(_in_sequence(i * sb + b, T), s, NEG_INF)
            p = jnp.exp(s - lse)
            dp = lax.dot_general(do, vw, _NT, preferred_element_type=F32)
            ds = p * (dp - delta)
            dbias_ref[...] += ds.reshape(GROUP, BLOCK_Q, 3 * BLOCK_Q)
            sunk = jnp.exp(sk - lse) * delta
            for g in range(GROUP):
                dsink = dsink + jnp.where(row == g, -jnp.sum(sunk[g * BLOCK_Q:(g + 1) * BLOCK_Q]), 0.0)
            dsb = (ds * scale).astype(BF16)
            dq = lax.dot_general(dsb, kw, _NN, preferred_element_type=F32)
            for g in range(GROUP):
                dq_ref[rows, g * HEAD_DIM:(g + 1) * HEAD_DIM] = dq[g * BLOCK_Q:(g + 1) * BLOCK_Q]
            dkb_ref[win, :] += lax.dot_general(dsb, q, _TN, preferred_element_type=F32)
            dvb_ref[win, :] += lax.dot_general(p.astype(BF16), do, _TN, preferred_element_type=F32)
        _accumulate(dsink_ref, dsink, first)

        before = pl.ds(pl.multiple_of(jnp.maximum(sb * i - 1, 0) * BLOCK_Q, BLOCK_Q), BLOCK_Q)
        own = pl.ds(pl.multiple_of(i * tq, BLOCK_Q), tq)
        after = pl.ds(pl.multiple_of(jnp.minimum(sb * i + sb, nblk - 1) * BLOCK_Q, BLOCK_Q), BLOCK_Q)
        for acc_ref, band_ref in ((dk_ref, dkb_ref), (dv_ref, dvb_ref)):
            acc_ref[before, :] += band_ref[0:BLOCK_Q, :]
            acc_ref[own, :] += band_ref[BLOCK_Q:BLOCK_Q + tq, :]
            acc_ref[after, :] += band_ref[BLOCK_Q + tq:, :]

    return _pcall(
        body, deps, name="attn_b_bwd", grid=(n_kv, nblk // sb),
        in_specs=[*_head_specs(q_off, tq),
                  *_band_specs(lambda kv: q_off + n_q + kv, nblk, sb),
                  *_band_specs(lambda kv: q_off + n_q + n_kv + kv, nblk, sb),
                  *_head_specs(o_off, tq), *_head_specs(o_off, tq),
                  pl.BlockSpec((GROUP, tq, 1), lambda kv, i: (kv, i, 0)),
                  pl.BlockSpec((GROUP, BLOCK_Q, 3 * BLOCK_Q), lambda kv, i: (kv, 0, 0)),
                  pl.BlockSpec(memory_space=pltpu.SMEM)],
        out_specs=[pl.BlockSpec((tq, GROUP * HEAD_DIM), lambda kv, i: (i, kv)),
                   pl.BlockSpec((T, HEAD_DIM), lambda kv, i: (0, kv)),
                   pl.BlockSpec((T, HEAD_DIM), lambda kv, i: (0, kv)),
                   pl.BlockSpec((GROUP, BLOCK_Q, 3 * BLOCK_Q), lambda kv, i: (kv, 0, 0)),
                   pl.BlockSpec((None, SUBLANES, LANES), lambda kv, i: (kv, 0, 0))],
        out_shape=[jax.ShapeDtypeStruct((T, n_q * HEAD_DIM), F32),
                   jax.ShapeDtypeStruct((T, n_kv * HEAD_DIM), F32),
                   jax.ShapeDtypeStruct((T, n_kv * HEAD_DIM), F32),
                   jax.ShapeDtypeStruct((n_q, BLOCK_Q, 3 * BLOCK_Q), F32),
                   jax.ShapeDtypeStruct((n_kv, SUBLANES, LANES), F32)],
        scratch_shapes=[pltpu.VMEM((tq + 2 * BLOCK_Q, HEAD_DIM), F32), pltpu.VMEM((tq + 2 * BLOCK_Q, HEAD_DIM), F32)],
        compiler_params=_params(("parallel", "arbitrary")),
    )(*([pb] * (GROUP + 6)), *([o_cat] * GROUP), *([d_o] * GROUP), lse, bias, sink)


def _table_grads(dbias, dsink_raw, idx):
    n_heads = dbias.shape[0]
    n_kv = dsink_raw.shape[0]

    def body(db_ref, ds_ref, idx_ref, dt_ref, dsk_ref):
        iv = idx_ref[...]
        row = lax.broadcasted_iota(jnp.int32, (SUBLANES, LANES), 0)
        lane = lax.broadcasted_iota(jnp.int32, (SUBLANES, LANES), 1)
        dsk = jnp.zeros((SUBLANES, LANES), F32)
        for h in range(n_heads):
            d = db_ref[h]
            acc = jnp.zeros((SUBLANES, LANES), F32)
            for b in range(N_BUCKETS):
                acc = jnp.where((row == 0) & (lane == b), jnp.sum(jnp.where(iv == b, d, 0.0)), acc)
            dt_ref[:, h * LANES:(h + 1) * LANES] = acc
            raw = ds_ref[h // GROUP]
            val = jnp.sum(jnp.where((row == h % GROUP) & (lane == 0), raw, 0.0))
            dsk = jnp.where((row == 0) & (lane == h), val, dsk)
        dsk_ref[...] = dsk

    return pl.pallas_call(
        body, name="table_grads",
        in_specs=[pl.BlockSpec(memory_space=pltpu.VMEM)] * 3,
        out_specs=[pl.BlockSpec(memory_space=pltpu.VMEM)] * 2,
        out_shape=[jax.ShapeDtypeStruct((SUBLANES, n_heads * LANES), F32),
                   jax.ShapeDtypeStruct((SUBLANES, LANES), F32)],
        compiler_params=pltpu.CompilerParams(vmem_limit_bytes=56 * 1024 * 1024),
    )(dbias, dsink_raw, idx)


def _position():
    x, y, c = lax.axis_index("x"), lax.axis_index("y"), lax.axis_index("c")
    return x, y, c


def _hbm(a):
    return pltpu.with_memory_space_constraint(a, pltpu.HBM)


def _split_start(name, bufs, sem_shapes, issue):
    nb, ns = len(bufs), len(sem_shapes)

    def body(*refs):
        buf_refs = refs[:nb]
        sems = refs[nb:nb + ns]
        token = refs[nb + ns + nb]
        issue(buf_refs, sems)
        token[...] = jnp.zeros(token.shape, F32)

    outs = pl.pallas_call(
        body, name=name,
        in_specs=[_HBM] * nb,
        out_specs=[_SEM] * ns + [_HBM] * nb + [_VMEM],
        out_shape=[pltpu.SemaphoreType.DMA(s) for s in sem_shapes] + [pltpu.HBM(b.shape, b.dtype) for b in bufs]
        + [jax.ShapeDtypeStruct((SUBLANES, LANES), F32)],
        input_output_aliases={i: ns + i for i in range(nb)},
        compiler_params=pltpu.CompilerParams(has_side_effects=_EFFECT),
    )(*[_hbm(b) for b in bufs])
    return outs[:ns], outs[ns:ns + nb], outs[-1]


def _split_wait(name, bufs, send, recv, counts, size_of, after):
    nb = len(bufs)

    def body(*refs):
        buf_refs = refs[:nb]
        send_ref, recv_ref = refs[nb], refs[nb + 1]
        x, y, c = _position()
        for w, n in enumerate(counts):
            ref = size_of(buf_refs, w)
            for k in range(n):
                s = sum(counts[:w]) + k
                cp = pltpu.make_async_remote_copy(
                    src_ref=ref, dst_ref=ref, send_sem=send_ref.at[s], recv_sem=recv_ref.at[s],
                    device_id=(x, y, c), device_id_type=MESH)
                cp.wait_send()
                cp.wait_recv()

    return pl.pallas_call(
        body, name=name,
        in_specs=[_HBM] * nb + [_SEM, _SEM, _ANY],
        out_specs=[_HBM] * nb,
        out_shape=[pltpu.HBM(b.shape, b.dtype) for b in bufs],
        input_output_aliases={i: i for i in range(nb)},
        compiler_params=pltpu.CompilerParams(has_side_effects=_EFFECT),
    )(*bufs, send, recv, after)


def _block_of(pos):
    return 4 * pos[0] + 2 * pos[1] + pos[2]


def _shard_of(ref, blk, by_cols):
    aligned = (lambda v, a: v) if isinstance(blk, int) else pl.multiple_of
    if by_cols:
        n = ref.shape[1] // N_DEV
        return ref.at[:, pl.ds(aligned(blk * n, LANES), n)]
    r = ref.shape[0] // N_DEV
    return ref.at[pl.ds(aligned(blk * r, SUBLANES), r), :]


def _place_shards(shards, by_cols):
    mine = _block_of(_position()).astype(jnp.int32).reshape(1)

    def place(name, s, cols, tr=256):
        r, n = s.shape
        tr = _tile(r, tr)

        def body(m_ref, s_ref, o_ref):
            o_ref[...] = s_ref[...].astype(BF16)

        if cols:
            out = pl.BlockSpec((tr, n), lambda i, m_ref: (i, m_ref[0]))
        else:
            out = pl.BlockSpec((tr, n), lambda i, m_ref: (m_ref[0] * (r // tr) + i, 0))
        return pl.pallas_call(
            body, name=name,
            grid_spec=pltpu.PrefetchScalarGridSpec(
                num_scalar_prefetch=1, grid=(r // tr,),
                in_specs=[pl.BlockSpec((tr, n), lambda i, m_ref: (i, 0))], out_specs=out),
            out_shape=jax.ShapeDtypeStruct((r, n * N_DEV) if cols else (r * N_DEV, n), BF16),
            compiler_params=_params(("parallel",)),
        )(mine, s)

    return [place("place_shard_%d" % w, s, cols) for w, (s, cols) in enumerate(zip(shards, by_cols))]


def _gather_start(shards, by_cols, groups):
    lands = _place_shards(shards, by_cols)

    def issue(land, sems):
        x, y, c = _position()
        peers = [(x, y, 1 - c), (1 - x, y, c), (x, 1 - y, c), (1 - x, 1 - y, c)]
        for gi, grp in enumerate(groups):
            for wi, w in enumerate(grp):
                own = _shard_of(land[w], _block_of((x, y, c)), by_cols[w])
                for k, peer in enumerate(peers):
                    pltpu.make_async_remote_copy(
                        src_ref=own, dst_ref=own, send_sem=sems[2 * gi].at[4 * wi + k],
                        recv_sem=sems[2 * gi + 1].at[4 * wi + k], device_id=peer, device_id_type=MESH).start()

    sem_shapes = [(4 * len(g),) for g in groups for _ in range(2)]
    return _split_start("gather_start", lands, sem_shapes, issue)


def _gather_forward(name, lands, by_cols):
    nw = len(lands)

    def issue(land, sems):
        x, y, c = _position()
        for w in range(nw):
            for k, chip in enumerate([(1 - x, y), (x, 1 - y), (1 - x, 1 - y)]):
                blk = _shard_of(land[w], _block_of((*chip, c)), by_cols[w])
                pltpu.make_async_remote_copy(
                    src_ref=blk, dst_ref=blk, send_sem=sems[0].at[3 * w + k], recv_sem=sems[1].at[3 * w + k],
                    device_id=(x, y, 1 - c), device_id_type=MESH).start()

    return _split_start(name, lands, [(3 * nw,), (3 * nw,)], issue)


def _first_block(bufs, w, offset=0):
    return bufs[offset + w].at[0]


_PEER_FLIPS = ((0, 0, 1), (1, 0, 0), (1, 0, 1), (0, 1, 0), (0, 1, 1), (1, 1, 0), (1, 1, 1))


def _scatter_start(name, grads, by_cols):
    nw = len(grads)
    lands = []
    for g, cols in zip(grads, by_cols):
        shard = (g.shape[0], g.shape[1] // N_DEV) if cols else (g.shape[0] // N_DEV, g.shape[1])
        lands.append(lax.empty((N_DEV,) + shard, g.dtype))

    def issue(bufs, sems):
        x, y, c = _position()
        flip = lambda v, f: 1 - v if f else v
        for w in range(nw):
            for k, (fx, fy, fc) in enumerate(_PEER_FLIPS):
                peer = (flip(x, fx), flip(y, fy), flip(c, fc))
                pltpu.make_async_remote_copy(
                    src_ref=_shard_of(bufs[w], _block_of(peer), by_cols[w]), dst_ref=bufs[nw + w].at[_block_of((x, y, c))],
                    send_sem=sems[0].at[7 * w + k], recv_sem=sems[1].at[7 * w + k],
                    device_id=peer, device_id_type=MESH).start()

    return _split_start(name, list(grads) + lands, [(7 * nw,), (7 * nw,)], issue)


def _adam(w, g, m, v):
    m = ADAM_B1 * m + (1.0 - ADAM_B1) * g
    v = ADAM_B2 * v + (1.0 - ADAM_B2) * (g * g)
    m_hat = m / (1.0 - ADAM_B1 ** ADAM_STEP)
    v_hat = v / (1.0 - ADAM_B2 ** ADAM_STEP)
    delta = -ADAM_LR * (m_hat / (jnp.sqrt(v_hat) + ADAM_EPS) + ADAM_WD * w)
    return delta, m, v


def _sum_adam(name, landed, grad, by_cols, w, m, v, tr=256):
    R, C = w.shape
    tr = _tile(R, tr)
    mine = _block_of(_position()).astype(jnp.int32).reshape(1)

    def body(me_ref, l_ref, own_ref, w_ref, m_ref, v_ref, g_ref, d_ref, nm_ref, nv_ref):
        own = own_ref[...].astype(F32)
        g = None
        for d in range(N_DEV):
            part = jnp.where(me_ref[0] == d, own, l_ref[d].astype(F32))
            g = part if g is None else g + part
        g_ref[...] = g
        d_ref[...], nm_ref[...], nv_ref[...] = _adam(w_ref[...], g, m_ref[...], v_ref[...])

    tile = pl.BlockSpec((tr, C), lambda i, me_ref: (i, 0))
    if by_cols:
        own = pl.BlockSpec((tr, C), lambda i, me_ref: (i, me_ref[0]))
    else:
        own = pl.BlockSpec((tr, C), lambda i, me_ref: (me_ref[0] * (R // tr) + i, 0))
    return pl.pallas_call(
        body, name=name,
        grid_spec=pltpu.PrefetchScalarGridSpec(
            num_scalar_prefetch=1, grid=(R // tr,),
            in_specs=[pl.BlockSpec((N_DEV, tr, C), lambda i, me_ref: (0, i, 0)), own, tile, tile, tile],
            out_specs=[tile] * 4),
        out_shape=[jax.ShapeDtypeStruct((R, C), F32)] * 4,
        compiler_params=_params(("parallel",)),
    )(mine, landed, grad, w, m, v)


def _small_all_reduce(parts, deps=()):
    W = parts.shape[1]

    def body(p_ref, o_ref, slots, send_sems, recv_sems):
        x, y, c = _position()
        me = 4 * x + 2 * y + c
        slots[me] = jnp.sum(p_ref[...], axis=0, keepdims=True)
        peers = [(x, y, 1 - c), (1 - x, y, c), (1 - x, y, 1 - c), (x, 1 - y, c), (x, 1 - y, 1 - c),
                 (1 - x, 1 - y, c), (1 - x, 1 - y, 1 - c)]
        copies = []
        for k, peer in enumerate(peers):
            cp = pltpu.make_async_remote_copy(
                src_ref=slots.at[me], dst_ref=slots.at[me], send_sem=send_sems.at[k], recv_sem=recv_sems.at[k],
                device_id=peer, device_id_type=MESH)
            cp.start()
            copies.append(cp)
        for cp in copies:
            cp.wait()
        total = slots[0]
        for d in range(1, N_DEV):
            total = total + slots[d]
        o_ref[...] = total

    return _pcall(
        body, deps, name="small_all_reduce",
        in_specs=[pl.BlockSpec(memory_space=pltpu.VMEM)], out_specs=pl.BlockSpec(memory_space=pltpu.VMEM),
        out_shape=jax.ShapeDtypeStruct((1, W), F32),
        scratch_shapes=[pltpu.VMEM((N_DEV, 1, W), F32), pltpu.SemaphoreType.DMA((7,)), pltpu.SemaphoreType.DMA((7,))],
    )(parts)


def _adam_small(w, g, m, v):
    def body(w_ref, g_ref, m_ref, v_ref, d_ref, nm_ref, nv_ref):
        d_ref[...], nm_ref[...], nv_ref[...] = _adam(w_ref[...], g_ref[...], m_ref[...], v_ref[...])

    return pl.pallas_call(
        body, name="adam_small",
        in_specs=[pl.BlockSpec(memory_space=pltpu.VMEM)] * 4, out_specs=[pl.BlockSpec(memory_space=pltpu.VMEM)] * 3,
        out_shape=[jax.ShapeDtypeStruct(w.shape, F32)] * 3,
    )(w, g, m, v)


_GATHER_GROUPS = (("w_in",), ("w_out", "w_up", "ple_w"), ("w_down", "w_gate"))
_COL_SHARDED = ("w_in", "w_up", "ple_w")


class _MeshComm:
    def __init__(self, w, mom, var):
        self.w, self.mom, self.var = w, mom, var
        self.out = {}
        self._scatters = {}

    def gather_begin(self):
        names = [n for g in _GATHER_GROUPS for n in g]
        self._idx = {n: i for i, n in enumerate(names)}
        groups = [[self._idx[n] for n in g] for g in _GATHER_GROUPS]
        self._sems, self._lands, token = _gather_start(
            [self.w[n] for n in names], [n in _COL_SHARDED for n in names], groups)
        return token

    @staticmethod
    def _shard_size(names, offset):
        return lambda bufs, w: _shard_of(bufs[offset + w], 0, names[w] in _COL_SHARDED)

    def gather_arrive(self, gi, after):
        names = _GATHER_GROUPS[gi]
        ids = [self._idx[n] for n in names]
        self._arrived = _split_wait("gather_arrive%d" % gi, [self._lands[i] for i in ids], self._sems[2 * gi],
                                    self._sems[2 * gi + 1], [4] * len(ids), self._shard_size(names, 0), after)

    def gather_forward(self, gi):
        by_cols = [n in _COL_SHARDED for n in _GATHER_GROUPS[gi]]
        self._fsems, self._fthru, token = _gather_forward("gather_forward%d" % gi, self._arrived, by_cols)
        return token

    def gather_finish(self, gi, after):
        names = _GATHER_GROUPS[gi]
        out = _split_wait("gather_finish%d" % gi, self._fthru, self._fsems[0], self._fsems[1], [3] * len(names),
                          self._shard_size(names, 0), after)
        return dict(zip(names, out))

    def reduce_begin(self, key, grads):
        names = list(grads)
        sems, thru, token = _scatter_start("scatter_start_" + key, [grads[n] for n in names],
                                           [n in _COL_SHARDED for n in names])
        self._scatters[key] = (names, sems, thru)
        return token

    def reduce_finish(self, key, after):
        names, sems, thru = self._scatters[key]
        nw = len(names)
        out = _split_wait("scatter_wait_" + key, thru, sems[0], sems[1], [N_DEV - 1] * nw,
                          functools.partial(_first_block, offset=nw), after)
        for i, n in enumerate(names):
            self.out[n] = _sum_adam("adam_" + n, out[nw + i], out[i], n in _COL_SHARDED, self.w[n], self.mom[n],
                                    self.var[n])


def _step(x, p, target, gains, comm):
    T, D = x.shape
    n_q = D // (2 * HEAD_DIM)
    n_kv = n_q // GROUP
    cos, sin = _rope_tables(T)
    idx = _bucket_index()

    t = comm.gather_begin()
    u = _rms_fwd("norm_attn", x, gains["attn_norm_g"], deps=(t,))
    comm.gather_arrive(0, u)
    t = comm.gather_forward(0)
    bias = _bias_build(idx, gains["rel_bias_table"].reshape(-1), n_q, deps=(t,))
    full = comm.gather_finish(0, bias)
    proj_a, pb = _in_proj(u, full["w_in"], cos, sin, gains["q_norm_g"], gains["k_norm_g"], n_q + n_kv)
    o_a, lse_a = _attn_a_fwd(pb, n_q, n_kv, 2 * n_q)
    comm.gather_arrive(1, lse_a)
    t = comm.gather_forward(1)
    sink = gains["sink_logits"].reshape(-1)
    b_off = n_q + 2 * n_kv
    o_cat, lse_b = _attn_b_fwd(pb, bias, sink, o_a, b_off, n_q, n_kv, deps=(t,))
    full.update(comm.gather_finish(1, lse_b))
    h1, m_in = _mm_nn_rms("out_proj", o_cat, full["w_out"], x, gains["mlp_norm_g"])

    def up_epilogue(acc, extra, outs):
        outs[0][...] = acc.astype(BF16)
        r = jnp.maximum(acc, 0.0)
        outs[1][...] = (r * r).astype(BF16)

    a_act, f_act = _mm_nn("up_proj", m_in, full["w_up"], epilogue=up_epilogue, out_dtypes=[BF16, BF16], tn=2048)
    comm.gather_arrive(2, f_act)
    t = comm.gather_forward(2)
    p_b = p.astype(BF16)
    pe = _mm_nn("ple_proj", p_b, full["ple_w"], deps=(t,))
    full.update(comm.gather_finish(2, pe))
    h2 = _mm_nn("down_proj", f_act, full["w_down"], epilogue=_store_add, extras=(h1,), tn=256)
    gn = _rms_fwd("norm_gate", h2, gains["gate_norm_g"])

    dh3, dz, dpe, dg_final, dg_ple, loss_part = _gate_tail(gn, full["w_gate"], h2, pe, target, gains["ple_norm_g"],
                                                           gains["final_norm_g"])
    gw_gate = _mm_tn("grad_w_gate", gn, dz)
    gw_ple = _mm_tn("grad_ple_w", p_b, dpe)
    t = comm.reduce_begin("a", dict(w_gate=gw_gate, ple_w=gw_ple))
    dh2, dh2_b, dg_gate = _mm_nt_rms_bwd("d_gate_in", dz, full["w_gate"], h2, gains["gate_norm_g"], dh3, deps=(t,))
    gw_down = _mm_tn("grad_w_down", f_act, dh2_b)
    t = comm.reduce_begin("b", dict(w_down=gw_down))

    def act_bwd(acc, extra, outs):
        outs[0][...] = (acc * (2.0 * jnp.maximum(extra[0][...].astype(F32), 0.0))).astype(BF16)

    da = _mm_nt("d_act", dh2_b, full["w_down"], out_dtype=BF16, epilogue=act_bwd, extras=(a_act,), tn=2048, deps=(t,))
    gw_up = _mm_tn("grad_w_up", m_in, da)
    t = comm.reduce_begin("c", dict(w_up=gw_up))
    dm = _mm_nt("d_mlp_in", da, full["w_up"], tn=256, deps=(t,))
    dh1, dh1_b, dg_mlp = _rms_bwd("norm_mlp_bwd", dm, h1, gains["mlp_norm_g"], dh2)
    gw_out = _mm_tn("grad_w_out", o_cat, dh1_b)
    t = comm.reduce_begin("d", dict(w_out=gw_out))
    d_o = _mm_nt("d_attn_out", dh1_b, full["w_out"], out_dtype=BF16, deps=(t,))
    dqa, dka_t, dva_t = _attn_a_bwd(pb, o_cat, d_o, lse_a, n_q, n_kv)
    dka, dva = dka_t.T, dva_t.T
    dqb, dkb, dvb, dbias, dsink_raw = _attn_b_bwd(pb, o_cat, d_o, lse_b, bias, sink, b_off, n_q, n_kv, n_q)
    dtable, dsink = _table_grads(dbias, dsink_raw, idx)
    dproj, dg_q, dg_k = _dproj(proj_a, dqa, dka, dva, dqb, dkb, dvb, cos, sin, gains["q_norm_g"], gains["k_norm_g"])
    gw_in = _mm_tn("grad_w_in", u, dproj)
    t = comm.reduce_begin("e", dict(w_in=gw_in))
    dx, _, dg_attn = _mm_nt_rms_bwd("d_attn_in", dproj, full["w_in"], x, gains["attn_norm_g"], dh1, deps=(t,))
    for key in "abcd":
        comm.reduce_finish(key, dx)

    parts = jnp.concatenate([dg_attn, dg_mlp, dg_ple, dg_gate, dg_final, dg_q, dg_k, dtable, dsink, loss_part], axis=1)
    return dx, parts


_SHARDED = ("w_in", "w_out", "w_up", "w_down", "ple_w", "w_gate")
_VECTORS = ("attn_norm_g", "mlp_norm_g", "ple_norm_g", "gate_norm_g", "final_norm_g")
_ORDER = ("attn_norm_g", "w_in", "q_norm_g", "k_norm_g", "sink_logits", "w_out", "mlp_norm_g", "w_up", "w_down",
          "ple_w", "ple_norm_g", "gate_norm_g", "w_gate", "rel_bias_table", "final_norm_g")


def _pack_small(vals, n_heads):
    lane_pad = lambda v: jnp.pad(v, ((0, 0), (0, LANES - v.shape[1])))
    table = lane_pad(vals["rel_bias_table"].T).reshape(1, n_heads * LANES)
    return jnp.concatenate(
        [vals[n].reshape(1, -1) for n in _VECTORS] + [vals["q_norm_g"], vals["k_norm_g"], table,
                                                      lane_pad(vals["sink_logits"]), jnp.zeros((1, LANES), F32)], axis=1)


def _unpack_small(row, like, n_heads):
    out, off = {}, 0
    for n in _VECTORS:
        out[n] = row[:, off:off + like[n].size].reshape(like[n].shape)
        off += like[n].size
    for n in ("q_norm_g", "k_norm_g"):
        out[n] = row[:, off:off + LANES]
        off += LANES
    out["rel_bias_table"] = row[:, off:off + n_heads * LANES].reshape(n_heads, LANES)[:, :N_BUCKETS].T
    off += n_heads * LANES
    out["sink_logits"] = row[:, off:off + n_heads]
    off += LANES
    return out, row[0, off]


def kernel(x, p, attn_norm_g, w_in, q_norm_g, k_norm_g, sink_logits, w_out, mlp_norm_g, w_up, w_down, ple_w, ple_norm_g, gate_norm_g, w_gate, rel_bias_table, final_norm_g, loss_target, m_attn_norm_g, m_w_in, m_q_norm_g, m_k_norm_g, m_sink_logits, m_w_out, m_mlp_norm_g, m_w_up, m_w_down, m_ple_w, m_ple_norm_g, m_gate_norm_g, m_w_gate, m_rel_bias_table, m_final_norm_g, v_attn_norm_g, v_w_in, v_q_norm_g, v_k_norm_g, v_sink_logits, v_w_out, v_mlp_norm_g, v_w_up, v_w_down, v_ple_w, v_ple_norm_g, v_gate_norm_g, v_w_gate, v_rel_bias_table, v_final_norm_g):
    w = dict(attn_norm_g=attn_norm_g, w_in=w_in[0], q_norm_g=q_norm_g, k_norm_g=k_norm_g, sink_logits=sink_logits,
             w_out=w_out[0], mlp_norm_g=mlp_norm_g, w_up=w_up[0], w_down=w_down[0], ple_w=ple_w[0],
             ple_norm_g=ple_norm_g, gate_norm_g=gate_norm_g, w_gate=w_gate[0], rel_bias_table=rel_bias_table,
             final_norm_g=final_norm_g)
    mom = dict(attn_norm_g=m_attn_norm_g, w_in=m_w_in[0], q_norm_g=m_q_norm_g, k_norm_g=m_k_norm_g,
               sink_logits=m_sink_logits, w_out=m_w_out[0], mlp_norm_g=m_mlp_norm_g, w_up=m_w_up[0],
               w_down=m_w_down[0], ple_w=m_ple_w[0], ple_norm_g=m_ple_norm_g, gate_norm_g=m_gate_norm_g,
               w_gate=m_w_gate[0], rel_bias_table=m_rel_bias_table, final_norm_g=m_final_norm_g)
    var = dict(attn_norm_g=v_attn_norm_g, w_in=v_w_in[0], q_norm_g=v_q_norm_g, k_norm_g=v_k_norm_g,
               sink_logits=v_sink_logits, w_out=v_w_out[0], mlp_norm_g=v_mlp_norm_g, w_up=v_w_up[0],
               w_down=v_w_down[0], ple_w=v_ple_w[0], ple_norm_g=v_ple_norm_g, gate_norm_g=v_gate_norm_g,
               w_gate=v_w_gate[0], rel_bias_table=v_rel_bias_table, final_norm_g=v_final_norm_g)
    D = x.shape[-1]
    n_heads = D // (2 * HEAD_DIM)

    gains = {n: w[n] for n in w if n not in _SHARDED}
    gains["final_norm_g"] = final_norm_g.reshape(1, -1)

    comm = _MeshComm(w, mom, var)
    dx, parts = _step(x[0], p[0, 0], loss_target[0], gains, comm)

    small_g = _small_all_reduce(parts, deps=[comm.out[n][0] for n in comm.out])
    comm.reduce_finish("e", small_g)

    g_out, d_out, m_out, v_out = {}, {}, {}, {}
    for n in _SHARDED:
        g, d, nm, nv = comm.out[n]
        g_out[n], d_out[n], m_out[n], v_out[n] = g[None], d[None], nm[None], nv[None]

    small = {n: v for n, v in w.items() if n not in _SHARDED}
    pack = lambda vals: _pack_small({n: vals[n] for n in small}, n_heads)
    sd, sm, sv = _adam_small(pack(w), small_g, pack(mom), pack(var))
    sg, loss = _unpack_small(small_g, small, n_heads)
    g_out.update(sg)
    for dst, row in ((d_out, sd), (m_out, sm), (v_out, sv)):
        dst.update(_unpack_small(row, small, n_heads)[0])

    return (loss, dx[None], *[g_out[n] for n in _ORDER], *[d_out[n] for n in _ORDER],
            *[m_out[n] for n in _ORDER], *[v_out[n] for n in _ORDER])
```

```python
import functools
import math

import numpy as np
import jax
import jax.numpy as jnp
from jax import lax
from jax.experimental import pallas as pl
from jax.experimental.pallas import tpu as pltpu

F32 = jnp.float32
BF16 = jnp.bfloat16

N_DEV = 8
N_CHIP = 4
HEAD_DIM = 128
GROUP = 4
GRID_W = 64
WINDOW = 128
BLOCK_Q = 128
N_BUCKETS = 32
MAX_DISTANCE = 128
ROPE_THETA = 10000.0
EPS = 1e-6
NEG_INF = -1e30
ADAM_LR = 0.001
ADAM_B1 = 0.9
ADAM_B2 = 0.999
ADAM_EPS = 1e-08
ADAM_WD = 0.01
ADAM_STEP = 10
LOG2E = math.log2(math.e)
LANES = 128
SUBLANES = 8
MESH = pl.DeviceIdType.MESH

_NT = (((1,), (1,)), ((), ()))
_NN = (((1,), (0,)), ((), ()))
_TN = (((0,), (0,)), ((), ()))


def _tile(dim, pref):
    return pref if dim % pref == 0 else dim


def _params(sem):
    return pltpu.CompilerParams(dimension_semantics=sem, vmem_limit_bytes=56 * 1024 * 1024)


_HBM = pl.BlockSpec(memory_space=pltpu.HBM)
_SEM = pl.BlockSpec(memory_space=pltpu.SEMAPHORE)
_ANY = pl.BlockSpec(memory_space=pl.ANY)
_VMEM = pl.BlockSpec(memory_space=pltpu.VMEM)
_EFFECT = pltpu.SideEffectType.DATAFLOW_SIDE_EFFECTING


def _pcall(body, deps=(), *, in_specs, into=None, **kw):
    deps = [d for d in deps if d is not None]
    nd = len(deps)
    if into is not None:
        deps = [into[0]] + deps
        nd += 1
        kw["input_output_aliases"] = {0: into[1]}

    def wrapped(*refs):
        body(*refs[nd:])

    call = pl.pallas_call(wrapped, in_specs=[_ANY] * nd + list(in_specs), **kw)
    return lambda *args: call(*deps, *args)


def _mm(name, a, b, dims, grid, a_spec, b_spec, out_shape, out_specs, acc_shape, epilogue,
        extras=(), extra_specs=(), deps=(), semantics=("parallel", "parallel", "arbitrary")):
    nk = grid[2]
    n_extra = len(extras)

    def body(*refs):
        a_ref, b_ref = refs[0], refs[1]
        extra = refs[2:2 + n_extra]
        outs = refs[2 + n_extra:-1]
        acc = refs[-1]
        part = lax.dot_general(a_ref[...], b_ref[...], dims, preferred_element_type=F32)
        if nk == 1:
            epilogue(part, extra, outs)
        else:
            k = pl.program_id(2)

            @pl.when(k == 0)
            def _():
                acc[...] = part

            @pl.when(k > 0)
            def _():
                acc[...] += part

            @pl.when(k == nk - 1)
            def _():
                epilogue(acc[...], extra, outs)

    return _pcall(
        body, deps, name=name, grid=grid,
        in_specs=[a_spec, b_spec, *extra_specs],
        out_specs=out_specs, out_shape=out_shape,
        scratch_shapes=[pltpu.VMEM(acc_shape if nk > 1 else (SUBLANES, LANES), F32)],
        compiler_params=_params(semantics),
    )(a, b, *extras)


def _store(dtype):
    def ep(acc, extra, outs):
        outs[0][...] = acc.astype(dtype)
    return ep


def _store_add(acc, extra, outs):
    outs[0][...] = acc + extra[0][...]


def _mm_nn(name, a, b, out_dtype=F32, epilogue=None, extras=(), n_out=1, out_dtypes=None, tm=1024, tn=1024, tk=None,
           deps=()):
    M, K = a.shape
    N = b.shape[1]
    tm, tn, tk = _tile(M, tm), _tile(N, tn), _tile(K, tk or K)
    b_spec = pl.BlockSpec((tk, tn), lambda i, j, k: (k, j))
    grid = (M // tm, N // tn, K // tk)
    o_spec = pl.BlockSpec((tm, tn), lambda i, j, k: (i, j))
    out_dtypes = out_dtypes or [out_dtype] * n_out
    out_shape = [jax.ShapeDtypeStruct((M, N), d) for d in out_dtypes]
    res = _mm(name, a, b, _NN, grid, pl.BlockSpec((tm, tk), lambda i, j, k: (i, k)), b_spec,
              out_shape, [o_spec] * len(out_dtypes), (tm, tn), epilogue or _store(out_dtype),
              extras, [o_spec] * len(extras), deps)
    return res if len(out_dtypes) > 1 else res[0]


def _mm_nt(name, a, b, out_dtype=F32, epilogue=None, extras=(), tm=1024, tn=1024, tk=None, deps=()):
    M, C = a.shape
    N = b.shape[0]
    tm, tn, tk = _tile(M, tm), _tile(N, tn), _tile(C, tk or C)
    b_spec = pl.BlockSpec((tn, tk), lambda i, j, k: (j, k))
    grid = (M // tm, N // tn, C // tk)
    o_spec = pl.BlockSpec((tm, tn), lambda i, j, k: (i, j))
    return _mm(name, a, b, _NT, grid, pl.BlockSpec((tm, tk), lambda i, j, k: (i, k)), b_spec,
               [jax.ShapeDtypeStruct((M, N), out_dtype)], [o_spec], (tm, tn), epilogue or _store(out_dtype),
               extras, [o_spec] * len(extras), deps)[0]


def _mm_tn(name, a, b, out_dtype=BF16, tm=1024, tn=512, tk=None, deps=()):
    T, M = a.shape
    N = b.shape[1]
    tm, tn, tk = _tile(M, tm), _tile(N, tn), _tile(T, tk or T)
    out_shape = jax.ShapeDtypeStruct((M, N), out_dtype)
    o_spec = pl.BlockSpec((tm, tn), lambda i, j, k: (i, j))
    grid = (M // tm, N // tn, T // tk)
    return _mm(name, a, b, _TN, grid, pl.BlockSpec((tk, tm), lambda i, j, k: (k, i)),
               pl.BlockSpec((tk, tn), lambda i, j, k: (k, j)), [out_shape], [o_spec], (tm, tn), _store(out_dtype),
               deps=deps)[0]


def _mean_last(v):
    return jnp.mean(v, axis=-1, keepdims=True)


def _rows_to_sublanes(v):
    r, c = v.shape
    return jnp.sum(v.reshape(r // SUBLANES, SUBLANES, c), axis=0)


def _accumulate(ref, val, first):
    @pl.when(first)
    def _():
        ref[...] = val

    @pl.when(jnp.logical_not(first))
    def _():
        ref[...] += val


def _rms_fwd(name, x, g, tr=256, deps=()):
    T, D = x.shape
    tr = _tile(T, tr)

    def body(x_ref, g_ref, o_ref):
        xv = x_ref[...]
        r = lax.rsqrt(_mean_last(xv * xv) + EPS)
        o_ref[...] = (xv * r * g_ref[...]).astype(BF16)

    row = pl.BlockSpec((tr, D), lambda i: (i, 0))
    return _pcall(
        body, deps, name=name, grid=(T // tr,),
        in_specs=[row, pl.BlockSpec((1, D), lambda i: (0, 0))],
        out_specs=row, out_shape=jax.ShapeDtypeStruct((T, D), BF16),
        compiler_params=_params(("parallel",)),
    )(x, g)


def _rms_bwd(name, dyn, x, g, dres, tr=256, deps=()):
    T, D = x.shape
    tr = _tile(T, tr)

    def body(dy_ref, x_ref, g_ref, dr_ref, dx_ref, dxb_ref, dg_ref):
        xv = x_ref[...]
        r = lax.rsqrt(_mean_last(xv * xv) + EPS)
        xn = xv * r
        dy = dy_ref[...]
        dxn = dy * g_ref[...]
        dx = dr_ref[...] + r * (dxn - xn * _mean_last(dxn * xn))
        dx_ref[...] = dx
        dxb_ref[...] = dx.astype(BF16)
        _accumulate(dg_ref, _rows_to_sublanes(dy * xn), pl.program_id(0) == 0)

    row = pl.BlockSpec((tr, D), lambda i: (i, 0))
    return _pcall(
        body, deps, name=name, grid=(T // tr,),
        in_specs=[row, row, pl.BlockSpec((1, D), lambda i: (0, 0)), row],
        out_specs=[row, row, pl.BlockSpec((SUBLANES, D), lambda i: (0, 0))],
        out_shape=[jax.ShapeDtypeStruct((T, D), F32), jax.ShapeDtypeStruct((T, D), BF16),
                   jax.ShapeDtypeStruct((SUBLANES, D), F32)],
        compiler_params=_params(("arbitrary",)),
    )(dyn, x, g, dres)


def _mm_nn_rms(name, a, b, res, g, tm=512, deps=()):
    M, K = a.shape
    N = b.shape[1]
    tm = _tile(M, tm)

    def epilogue(acc, extra, outs):
        h = acc + extra[0][...]
        outs[0][...] = h
        outs[1][...] = (h * lax.rsqrt(_mean_last(h * h) + EPS) * extra[1][...]).astype(BF16)

    row = pl.BlockSpec((tm, N), lambda i, j, k: (i, 0))
    return _mm(name, a, b, _NN, (M // tm, 1, 1), pl.BlockSpec((tm, K), lambda i, j, k: (i, 0)),
               pl.BlockSpec((K, N), lambda i, j, k: (0, 0)),
               [jax.ShapeDtypeStruct((M, N), F32), jax.ShapeDtypeStruct((M, N), BF16)], [row, row], (tm, N), epilogue,
               (res, g), [row, pl.BlockSpec((1, N), lambda i, j, k: (0, 0))], deps)


def _mm_nt_rms_bwd(name, a, b, x, g, dres, tm=256, deps=()):
    M, C = a.shape
    N = b.shape[0]
    tm = _tile(M, tm)

    def epilogue(dy, extra, outs):
        x_ref, dr_ref, g_ref = extra
        xv = x_ref[...]
        r = lax.rsqrt(_mean_last(xv * xv) + EPS)
        xn = xv * r
        dxn = dy * g_ref[...]
        dx = dr_ref[...] + r * (dxn - xn * _mean_last(dxn * xn))
        outs[0][...] = dx
        outs[1][...] = dx.astype(BF16)
        _accumulate(outs[2], _rows_to_sublanes(dy * xn), pl.program_id(0) == 0)

    row = pl.BlockSpec((tm, N), lambda i, j, k: (i, 0))
    return _mm(name, a, b, _NT, (M // tm, 1, 1), pl.BlockSpec((tm, C), lambda i, j, k: (i, 0)),
               pl.BlockSpec((N, C), lambda i, j, k: (0, 0)),
               [jax.ShapeDtypeStruct((M, N), F32), jax.ShapeDtypeStruct((M, N), BF16),
                jax.ShapeDtypeStruct((SUBLANES, N), F32)],
               [row, row, pl.BlockSpec((SUBLANES, N), lambda i, j, k: (0, 0))], (tm, N), epilogue,
               (x, dres, g), [row, row, pl.BlockSpec((1, N), lambda i, j, k: (0, 0))], deps,
               semantics=("arbitrary", "arbitrary", "arbitrary"))


def _gate_tail(gn, w_gate, h2, pe, target, g_ple, g_final, tm=256):
    T, D = h2.shape
    tm = _tile(T, tm)

    def epilogue(z, extra, outs):
        h2_ref, pe_ref, t_ref, gp_ref, gf_ref = extra
        dh3_ref, dz_ref, dpe_ref, dgf_ref, dgp_ref, loss_ref = outs
        first = pl.program_id(0) == 0
        pev = pe_ref[...]
        r3 = lax.rsqrt(_mean_last(pev * pev) + EPS)
        en = pev * r3
        e = en * gp_ref[...]
        gate = 1.0 / (1.0 + jnp.exp(-z))
        h3 = h2_ref[...] + gate * e
        r5 = lax.rsqrt(_mean_last(h3 * h3) + EPS)
        hn = h3 * r5
        diff = hn * gf_ref[...] - t_ref[...]
        loss_rows = 0.5 * _mean_last(diff * diff)
        row0 = lax.broadcasted_iota(jnp.int32, (SUBLANES, LANES), 0) == 0
        _accumulate(loss_ref, jnp.where(row0, jnp.sum(loss_rows), 0.0), first)
        dy = diff * (1.0 / D)
        _accumulate(dgf_ref, _rows_to_sublanes(dy * hn), first)
        dhn = dy * gf_ref[...]
        dh3 = r5 * (dhn - hn * _mean_last(dhn * hn))
        dh3_ref[...] = dh3
        dgate = dh3 * e
        de = dh3 * gate
        dz_ref[...] = (dgate * gate * (1.0 - gate)).astype(BF16)
        _accumulate(dgp_ref, _rows_to_sublanes(de * en), first)
        den = de * gp_ref[...]
        dpe_ref[...] = (r3 * (den - en * _mean_last(den * en))).astype(BF16)

    row = pl.BlockSpec((tm, D), lambda i, j, k: (i, 0))
    vec = pl.BlockSpec((1, D), lambda i, j, k: (0, 0))
    part = pl.BlockSpec((SUBLANES, D), lambda i, j, k: (0, 0))
    return _mm("gate_tail", gn, w_gate, _NN, (T // tm, 1, 1), row, pl.BlockSpec(w_gate.shape, lambda i, j, k: (0, 0)),
               [jax.ShapeDtypeStruct((T, D), F32), jax.ShapeDtypeStruct((T, D), BF16),
                jax.ShapeDtypeStruct((T, D), BF16), jax.ShapeDtypeStruct((SUBLANES, D), F32),
                jax.ShapeDtypeStruct((SUBLANES, D), F32), jax.ShapeDtypeStruct((SUBLANES, LANES), F32)],
               [row, row, row, part, part, pl.BlockSpec((SUBLANES, LANES), lambda i, j, k: (0, 0))], (tm, D), epilogue,
               (h2, pe, target, g_ple, g_final), [row, row, row, vec, vec],
               semantics=("arbitrary", "arbitrary", "arbitrary"))


def _rope_tables(T):
    pos = np.arange(T)
    half = HEAD_DIM // 2
    inv = (ROPE_THETA ** (-np.arange(0, half, 2, dtype=np.float32) / half)).astype(np.float32)
    ang_r = (pos // GRID_W).astype(np.float32)[:, None] * inv
    ang_c = (pos % GRID_W).astype(np.float32)[:, None] * inv
    cos = np.concatenate([np.cos(ang_r), np.cos(ang_r), np.cos(ang_c), np.cos(ang_c)], axis=-1)
    sin = np.concatenate([-np.sin(ang_r), np.sin(ang_r), -np.sin(ang_c), np.sin(ang_c)], axis=-1)
    return jnp.asarray(cos, F32), jnp.asarray(sin, F32)


def _swap32(x):
    lane = lax.broadcasted_iota(jnp.int32, x.shape, 1)
    return jnp.where((lane % 64) < 32, pltpu.roll(x, 96, 1), pltpu.roll(x, 32, 1))


def _in_proj(u, w_in, cos, sin, g_q, g_k, n_norm, tm=512):
    T, K = u.shape
    W = w_in.shape[1]
    tm = _tile(T, tm)
    n_q = n_norm * GROUP // (GROUP + 1)
    wa = n_norm * HEAD_DIM

    def epilogue(acc, extra, outs):
        c_ref, s_ref, gq_ref, gk_ref = extra
        raw_ref, o_ref = outs
        c, s = c_ref[...], s_ref[...]
        raw_ref[...] = acc[:, :wa]
        for h in range(n_norm):
            cols = slice(h * HEAD_DIM, (h + 1) * HEAD_DIM)
            xv = acc[:, cols]
            g = gq_ref[...] if h < n_q else gk_ref[...]
            xn = xv * lax.rsqrt(_mean_last(xv * xv) + EPS) * g
            o_ref[:, cols] = (xn * c + _swap32(xn) * s).astype(BF16)
        o_ref[:, wa:] = acc[:, wa:].astype(BF16)

    tab = pl.BlockSpec((tm, HEAD_DIM), lambda i, j, k: (i, 0))
    vec = pl.BlockSpec((1, HEAD_DIM), lambda i, j, k: (0, 0))
    return _mm("in_proj", u, w_in, _NN, (T // tm, 1, 1), pl.BlockSpec((tm, K), lambda i, j, k: (i, 0)),
               pl.BlockSpec((K, W), lambda i, j, k: (0, 0)),
               [jax.ShapeDtypeStruct((T, wa), F32), jax.ShapeDtypeStruct((T, W), BF16)],
               [pl.BlockSpec((tm, wa), lambda i, j, k: (i, 0)), pl.BlockSpec((tm, W), lambda i, j, k: (i, 0))],
               (tm, W), epilogue, (cos, sin, g_q, g_k), [tab, tab, vec, vec])


def _dproj(proj_a, dqa, dka, dva, dqb, dkb, dvb, cos, sin, g_q, g_k, tr=256):
    T, wa = proj_a.shape
    tr = _tile(T, tr)
    n_q = dqa.shape[1] // HEAD_DIM
    n_kv = dka.shape[1] // HEAD_DIM
    W = wa + dva.shape[1] + dqb.shape[1] + dkb.shape[1] + dvb.shape[1]

    def body(p_ref, dqa_ref, dka_ref, dva_ref, dqb_ref, dkb_ref, dvb_ref, c_ref, s_ref, gq_ref, gk_ref,
             o_ref, dgq_ref, dgk_ref):
        c, s = c_ref[...], s_ref[...]
        dgq = jnp.zeros((SUBLANES, HEAD_DIM), F32)
        dgk = jnp.zeros((SUBLANES, HEAD_DIM), F32)
        for h in range(n_q + n_kv):
            cols = slice(h * HEAD_DIM, (h + 1) * HEAD_DIM)
            xv = p_ref[:, cols]
            r = lax.rsqrt(_mean_last(xv * xv) + EPS)
            xn = xv * r
            if h < n_q:
                d = dqa_ref[:, cols]
                g = gq_ref[...]
            else:
                d = dka_ref[:, (h - n_q) * HEAD_DIM:(h - n_q + 1) * HEAD_DIM]
                g = gk_ref[...]
            dqn = d * c + _swap32(d * s)
            part = _rows_to_sublanes(dqn * xn)
            if h < n_q:
                dgq = dgq + part
            else:
                dgk = dgk + part
            dxn = dqn * g
            o_ref[:, cols] = (r * (dxn - xn * _mean_last(dxn * xn))).astype(BF16)
        off = wa
        for ref in (dva_ref, dqb_ref, dkb_ref, dvb_ref):
            w = ref.shape[1]
            o_ref[:, off:off + w] = ref[...].astype(BF16)
            off += w
        first = pl.program_id(0) == 0
        _accumulate(dgq_ref, dgq, first)
        _accumulate(dgk_ref, dgk, first)

    def row(w):
        return pl.BlockSpec((tr, w), lambda i: (i, 0))

    vec = pl.BlockSpec((1, HEAD_DIM), lambda i: (0, 0))
    part = pl.BlockSpec((SUBLANES, HEAD_DIM), lambda i: (0, 0))
    return pl.pallas_call(
        body, name="dproj", grid=(T // tr,),
        in_specs=[row(wa), row(dqa.shape[1]), row(dka.shape[1]), row(dva.shape[1]), row(dqb.shape[1]),
                  row(dkb.shape[1]), row(dvb.shape[1]), row(HEAD_DIM), row(HEAD_DIM), vec, vec],
        out_specs=[row(W), part, part],
        out_shape=[jax.ShapeDtypeStruct((T, W), BF16), jax.ShapeDtypeStruct((SUBLANES, HEAD_DIM), F32),
                   jax.ShapeDtypeStruct((SUBLANES, HEAD_DIM), F32)],
        compiler_params=_params(("arbitrary",)),
    )(proj_a, dqa, dka, dva, dqb, dkb, dvb, cos, sin, g_q, g_k)


def _attn_a_fwd(pb, n_q, n_kv, out_heads, tq=1024, tc=1024):
    T = pb.shape[0]
    tq, tc = _tile(T, tq), _tile(T, tc)
    scale = HEAD_DIM ** -0.5
    c = scale * LOG2E

    def body(q_ref, k_ref, v_ref, o_ref, lse_ref):
        q = q_ref[...]
        m = l = acc = None
        for j in range(T // tc):
            keys = slice(j * tc, (j + 1) * tc)
            s = lax.dot_general(q, k_ref[keys, :], _NT, preferred_element_type=F32)
            mj = jnp.max(s, axis=-1, keepdims=True)
            m_new = mj if j == 0 else jnp.maximum(m, mj)
            p = jnp.exp2((s - m_new) * c)
            pv = lax.dot_general(p.astype(BF16), v_ref[keys, :], _NN, preferred_element_type=F32)
            if j == 0:
                l, acc = jnp.sum(p, axis=-1, keepdims=True), pv
            else:
                alpha = jnp.exp2((m - m_new) * c)
                l = alpha * l + jnp.sum(p, axis=-1, keepdims=True)
                acc = alpha * acc + pv
            m = m_new
        o_ref[...] = (acc / l).astype(BF16)
        lse_ref[...] = m * scale + jnp.log(l)

    return pl.pallas_call(
        body, name="attn_a_fwd", grid=(n_kv, GROUP, T // tq),
        in_specs=[pl.BlockSpec((tq, HEAD_DIM), lambda kv, g, i: (i, kv * GROUP + g)),
                  pl.BlockSpec((T, HEAD_DIM), lambda kv, g, i: (0, n_q + kv)),
                  pl.BlockSpec((T, HEAD_DIM), lambda kv, g, i: (0, n_q + n_kv + kv))],
        out_specs=[pl.BlockSpec((tq, HEAD_DIM), lambda kv, g, i: (i, kv * GROUP + g)),
                   pl.BlockSpec((None, tq, 1), lambda kv, g, i: (kv * GROUP + g, i, 0))],
        out_shape=[jax.ShapeDtypeStruct((T, out_heads * HEAD_DIM), BF16), jax.ShapeDtypeStruct((n_q, T, 1), F32)],
        compiler_params=_params(("parallel", "parallel", "parallel")),
    )(pb, pb, pb)


def _attn_a_bwd(pb, o_cat, d_o, lse, n_q, n_kv, tq=512, tc=512):
    T = pb.shape[0]
    tq, tc = _tile(T, tq), _tile(T, tc)
    scale = HEAD_DIM ** -0.5
    c = scale * LOG2E

    def body(q_ref, k_ref, v_ref, o_ref, do_ref, lse_ref, dq_ref, dkt_ref, dvt_ref):
        q, do = q_ref[...], do_ref[...]
        qt, dot = q.T, do.T
        delta = jnp.sum(do.astype(F32) * o_ref[...].astype(F32), axis=-1, keepdims=True)
        lse2 = lse_ref[...] * LOG2E

        @pl.when(jnp.logical_and(pl.program_id(1) == 0, pl.program_id(2) == 0))
        def _():
            dkt_ref[...] = jnp.zeros(dkt_ref.shape, F32)
            dvt_ref[...] = jnp.zeros(dvt_ref.shape, F32)

        dq = None
        for j in range(T // tc):
            keys = slice(j * tc, (j + 1) * tc)
            kc, vc = k_ref[keys, :], v_ref[keys, :]
            s = lax.dot_general(q, kc, _NT, preferred_element_type=F32)
            p = jnp.exp2(s * c - lse2)
            dp = lax.dot_general(do, vc, _NT, preferred_element_type=F32)
            ds = (p * (dp - delta) * scale).astype(BF16)
            dqj = lax.dot_general(ds, kc, _NN, preferred_element_type=F32)
            dq = dqj if dq is None else dq + dqj
            dvt_ref[:, keys] += lax.dot_general(dot, p.astype(BF16), _NN, preferred_element_type=F32)
            dkt_ref[:, keys] += lax.dot_general(qt, ds, _NN, preferred_element_type=F32)
        dq_ref[...] = dq

    qmap = lambda kv, g, i: (i, kv * GROUP + g)
    return pl.pallas_call(
        body, name="attn_a_bwd", grid=(n_kv, GROUP, T // tq),
        in_specs=[pl.BlockSpec((tq, HEAD_DIM), qmap),
                  pl.BlockSpec((T, HEAD_DIM), lambda kv, g, i: (0, n_q + kv)),
                  pl.BlockSpec((T, HEAD_DIM), lambda kv, g, i: (0, n_q + n_kv + kv)),
                  pl.BlockSpec((tq, HEAD_DIM), qmap),
                  pl.BlockSpec((tq, HEAD_DIM), qmap),
                  pl.BlockSpec((None, tq, 1), lambda kv, g, i: (kv * GROUP + g, i, 0))],
        out_specs=[pl.BlockSpec((tq, HEAD_DIM), qmap),
                   pl.BlockSpec((HEAD_DIM, T), lambda kv, g, i: (kv, 0)),
                   pl.BlockSpec((HEAD_DIM, T), lambda kv, g, i: (kv, 0))],
        out_shape=[jax.ShapeDtypeStruct((T, n_q * HEAD_DIM), F32),
                   jax.ShapeDtypeStruct((n_kv * HEAD_DIM, T), F32),
                   jax.ShapeDtypeStruct((n_kv * HEAD_DIM, T), F32)],
        compiler_params=_params(("parallel", "arbitrary", "arbitrary")),
    )(pb, pb, pb, o_cat, d_o, lse)


def _bucket_index():
    r = np.arange(BLOCK_Q)[:, None]
    j = np.arange(3 * BLOCK_Q)[None, :]
    rel = (j - BLOCK_Q) - r
    nb = N_BUCKETS // 2
    ret = np.where(rel > 0, nb, 0)
    n = np.abs(rel)
    max_exact = nb // 2
    nf = np.maximum(n, 1).astype(np.float32)
    large = max_exact + (np.log(nf / max_exact) / math.log(MAX_DISTANCE / max_exact) * (nb - max_exact)).astype(np.int32)
    large = np.minimum(large, nb - 1)
    return jnp.asarray(ret + np.where(n < max_exact, n, large), jnp.int32)


def _bias_build(idx, table_flat, n_heads, deps=()):
    def body(idx_ref, tab_ref, o_ref):
        h = pl.program_id(0)
        iv = idx_ref[...]
        acc = jnp.zeros(iv.shape, F32)
        for b in range(N_BUCKETS):
            acc = jnp.where(iv == b, tab_ref[b * n_heads + h], acc)
        r = lax.broadcasted_iota(jnp.int32, iv.shape, 0)
        j = lax.broadcasted_iota(jnp.int32, iv.shape, 1)
        o_ref[...] = jnp.where(jnp.abs(j - BLOCK_Q - r) <= WINDOW, acc, NEG_INF)

    return _pcall(
        body, deps, name="bias_build", grid=(n_heads,),
        in_specs=[pl.BlockSpec(idx.shape, lambda h: (0, 0)), pl.BlockSpec(memory_space=pltpu.SMEM)],
        out_specs=pl.BlockSpec((None,) + idx.shape, lambda h: (h, 0, 0)),
        out_shape=jax.ShapeDtypeStruct((n_heads,) + idx.shape, F32),
        compiler_params=_params(("parallel",)),
    )(idx, table_flat)


def _in_sequence(n, T):
    j = lax.broadcasted_iota(jnp.int32, (GROUP * BLOCK_Q, 3 * BLOCK_Q), 1)
    kabs = n * BLOCK_Q + j - BLOCK_Q
    return (kabs >= 0) & (kabs < T)


def _per_head_rows(values):
    head = lax.broadcasted_iota(jnp.int32, (GROUP * BLOCK_Q, 1), 0) // BLOCK_Q
    col = jnp.zeros((GROUP * BLOCK_Q, 1), F32)
    for g, v in enumerate(values):
        col = jnp.where(head == g, v, col)
    return col


def _band_specs(col, nblk, sb):
    return [pl.BlockSpec((BLOCK_Q, HEAD_DIM), lambda kv, i: (jnp.maximum(sb * i - 1, 0), col(kv))),
            pl.BlockSpec((sb * BLOCK_Q, HEAD_DIM), lambda kv, i: (i, col(kv))),
            pl.BlockSpec((BLOCK_Q, HEAD_DIM), lambda kv, i: (jnp.minimum(sb * i + sb, nblk - 1), col(kv)))]


def _head_specs(base, rows):
    return [pl.BlockSpec((rows, HEAD_DIM), functools.partial(lambda kv, i, g: (i, base + kv * GROUP + g), g=g))
            for g in range(GROUP)]


def _attn_b_fwd(pb, bias, sink, o_all, q_off, n_q, n_kv, deps=(), sb=8):
    T = pb.shape[0]
    nblk = T // BLOCK_Q
    sb = min(sb, nblk)
    tq = sb * BLOCK_Q
    scale = HEAD_DIM ** -0.5

    def body(*refs):
        q_refs = refs[0:GROUP]
        k_refs, v_refs = refs[GROUP:GROUP + 3], refs[GROUP + 3:GROUP + 6]
        bias_ref, sink_ref, o_ref, lse_ref = refs[GROUP + 6:]
        kv, i = pl.program_id(0), pl.program_id(1)
        kb = jnp.concatenate([r[...] for r in k_refs], axis=0)
        vb = jnp.concatenate([r[...] for r in v_refs], axis=0)
        bias_all = bias_ref[...].reshape(GROUP * BLOCK_Q, 3 * BLOCK_Q)
        sk = _per_head_rows([sink_ref[kv * GROUP + g] for g in range(GROUP)])
        for b in range(sb):
            rows = slice(b * BLOCK_Q, (b + 1) * BLOCK_Q)
            kw, vw = kb[b * BLOCK_Q:(b + 3) * BLOCK_Q], vb[b * BLOCK_Q:(b + 3) * BLOCK_Q]
            q = jnp.concatenate([r[rows, :] for r in q_refs], axis=0)
            s = lax.dot_general(q, kw, _NT, preferred_element_type=F32) * scale + bias_all
            if b == 0 or b == sb - 1:
                s = jnp.where(_in_sequence(i * sb + b, T), s, NEG_INF)
            m = jnp.maximum(jnp.max(s, axis=-1, keepdims=True), sk)
            p = jnp.exp(s - m)
            l = jnp.sum(p, axis=-1, keepdims=True) + jnp.exp(sk - m)
            o = (lax.dot_general(p.astype(BF16), vw, _NN, preferred_element_type=F32) / l).astype(BF16)
            lse = m + jnp.log(l)
            for g in range(GROUP):
                head = slice(g * BLOCK_Q, (g + 1) * BLOCK_Q)
                o_ref[rows, g * HEAD_DIM:(g + 1) * HEAD_DIM] = o[head]
                lse_ref[g, rows, :] = lse[head]

    first_group = o_all.shape[1] // (GROUP * HEAD_DIM) - n_kv
    return _pcall(
        body, deps, into=(o_all, 0), name="attn_b_fwd", grid=(n_kv, nblk // sb),
        in_specs=[*_head_specs(q_off, tq),
                  *_band_specs(lambda kv: q_off + n_q + kv, nblk, sb),
                  *_band_specs(lambda kv: q_off + n_q + n_kv + kv, nblk, sb),
                  pl.BlockSpec((GROUP, BLOCK_Q, 3 * BLOCK_Q), lambda kv, i: (kv, 0, 0)),
                  pl.BlockSpec(memory_space=pltpu.SMEM)],
        out_specs=[pl.BlockSpec((tq, GROUP * HEAD_DIM), lambda kv, i: (i, first_group + kv)),
                   pl.BlockSpec((GROUP, tq, 1), lambda kv, i: (kv, i, 0))],
        out_shape=[jax.ShapeDtypeStruct(o_all.shape, BF16), jax.ShapeDtypeStruct((n_q, T, 1), F32)],
        compiler_params=_params(("parallel", "parallel")),
    )(*([pb] * (GROUP + 6)), bias, sink)


def _attn_b_bwd(pb, o_cat, d_o, lse, bias, sink, q_off, n_q, n_kv, o_off, deps=(), sb=8):
    T = pb.shape[0]
    nblk = T // BLOCK_Q
    sb = min(sb, nblk)
    tq = sb * BLOCK_Q
    scale = HEAD_DIM ** -0.5

    def body(*refs):
        q_refs = refs[0:GROUP]
        k_refs, v_refs = refs[GROUP:GROUP + 3], refs[GROUP + 3:GROUP + 6]
        o_refs, do_refs = refs[GROUP + 6:2 * GROUP + 6], refs[2 * GROUP + 6:3 * GROUP + 6]
        lse_ref, bias_ref, sink_ref, dq_ref, dk_ref, dv_ref, dbias_ref, dsink_ref, dkb_ref, dvb_ref = refs[3 * GROUP + 6:]
        kv, i = pl.program_id(0), pl.program_id(1)
        first = i == 0

        @pl.when(first)
        def _():
            dk_ref[...] = jnp.zeros(dk_ref.shape, F32)
            dv_ref[...] = jnp.zeros(dv_ref.shape, F32)
            dbias_ref[...] = jnp.zeros(dbias_ref.shape, F32)

        kb = jnp.concatenate([r[...] for r in k_refs], axis=0)
        vb = jnp.concatenate([r[...] for r in v_refs], axis=0)
        dkb_ref[...] = jnp.zeros(dkb_ref.shape, F32)
        dvb_ref[...] = jnp.zeros(dvb_ref.shape, F32)
        row = lax.broadcasted_iota(jnp.int32, (SUBLANES, LANES), 0)
        dsink = jnp.zeros((SUBLANES, LANES), F32)
        bias_all = bias_ref[...].reshape(GROUP * BLOCK_Q, 3 * BLOCK_Q)
        sk = _per_head_rows([sink_ref[kv * GROUP + g] for g in range(GROUP)])
        for b in range(sb):
            rows = slice(b * BLOCK_Q, (b + 1) * BLOCK_Q)
            win = slice(b * BLOCK_Q, (b + 3) * BLOCK_Q)
            kw, vw = kb[win], vb[win]
            q = jnp.concatenate([r[rows, :] for r in q_refs], axis=0)
            do = jnp.concatenate([r[rows, :] for r in do_refs], axis=0)
            o = jnp.concatenate([r[rows, :] for r in o_refs], axis=0)
            lse = jnp.concatenate([lse_ref[g, rows, :] for g in range(GROUP)], axis=0)
            delta = jnp.sum(do.astype(F32) * o.astype(F32), axis=-1, keepdims=True)
            s = lax.dot_general(q, kw, _NT, preferred_element_type=F32) * scale + bias_all
            if b == 0 or b == sb - 1:
                s = jnp.where(_in_sequence(i * sb + b, T), s, NEG_INF)
            p = jnp.exp(s - lse)
            dp = lax.dot_general(do, vw, _NT, preferred_element_type=F32)
            ds = p * (dp - delta)
            dbias_ref[...] += ds.reshape(GROUP, BLOCK_Q, 3 * BLOCK_Q)
            sunk = jnp.exp(sk - lse) * delta
            for g in range(GROUP):
                dsink = dsink + jnp.where(row == g, -jnp.sum(sunk[g * BLOCK_Q:(g + 1) * BLOCK_Q]), 0.0)
            dsb = (ds * scale).astype(BF16)
            dq = lax.dot_general(dsb, kw, _NN, preferred_element_type=F32)
            for g in range(GROUP):
                dq_ref[rows, g * HEAD_DIM:(g + 1) * HEAD_DIM] = dq[g * BLOCK_Q:(g + 1) * BLOCK_Q]
            dkb_ref[win, :] += lax.dot_general(dsb, q, _TN, preferred_element_type=F32)
            dvb_ref[win, :] += lax.dot_general(p.astype(BF16), do, _TN, preferred_element_type=F32)
        _accumulate(dsink_ref, dsink, first)

        before = pl.ds(pl.multiple_of(jnp.maximum(sb * i - 1, 0) * BLOCK_Q, BLOCK_Q), BLOCK_Q)
        own = pl.ds(pl.multiple_of(i * tq, BLOCK_Q), tq)
        after = pl.ds(pl.multiple_of(jnp.minimum(sb * i + sb, nblk - 1) * BLOCK_Q, BLOCK_Q), BLOCK_Q)
        for acc_ref, band_ref in ((dk_ref, dkb_ref), (dv_ref, dvb_ref)):
            acc_ref[before, :] += band_ref[0:BLOCK_Q, :]
            acc_ref[own, :] += band_ref[BLOCK_Q:BLOCK_Q + tq, :]
            acc_ref[after, :] += band_ref[BLOCK_Q + tq:, :]

    return _pcall(
        body, deps, name="attn_b_bwd", grid=(n_kv, nblk // sb),
        in_specs=[*_head_specs(q_off, tq),
                  *_band_specs(lambda kv: q_off + n_q + kv, nblk, sb),
                  *_band_specs(lambda kv: q_off + n_q + n_kv + kv, nblk, sb),
                  *_head_specs(o_off, tq), *_head_specs(o_off, tq),
                  pl.BlockSpec((GROUP, tq, 1), lambda kv, i: (kv, i, 0)),
                  pl.BlockSpec((GROUP, BLOCK_Q, 3 * BLOCK_Q), lambda kv, i: (kv, 0, 0)),
                  pl.BlockSpec(memory_space=pltpu.SMEM)],
        out_specs=[pl.BlockSpec((tq, GROUP * HEAD_DIM), lambda kv, i: (i, kv)),
                   pl.BlockSpec((T, HEAD_DIM), lambda kv, i: (0, kv)),
                   pl.BlockSpec((T, HEAD_DIM), lambda kv, i: (0, kv)),
                   pl.BlockSpec((GROUP, BLOCK_Q, 3 * BLOCK_Q), lambda kv, i: (kv, 0, 0)),
                   pl.BlockSpec((None, SUBLANES, LANES), lambda kv, i: (kv, 0, 0))],
        out_shape=[jax.ShapeDtypeStruct((T, n_q * HEAD_DIM), F32),
                   jax.ShapeDtypeStruct((T, n_kv * HEAD_DIM), F32),
                   jax.ShapeDtypeStruct((T, n_kv * HEAD_DIM), F32),
                   jax.ShapeDtypeStruct((n_q, BLOCK_Q, 3 * BLOCK_Q), F32),
                   jax.ShapeDtypeStruct((n_kv, SUBLANES, LANES), F32)],
        scratch_shapes=[pltpu.VMEM((tq + 2 * BLOCK_Q, HEAD_DIM), F32), pltpu.VMEM((tq + 2 * BLOCK_Q, HEAD_DIM), F32)],
        compiler_params=_params(("parallel", "arbitrary")),
    )(*([pb] * (GROUP + 6)), *([o_cat] * GROUP), *([d_o] * GROUP), lse, bias, sink)


def _table_grads(dbias, dsink_raw, idx):
    n_heads = dbias.shape[0]
    n_kv = dsink_raw.shape[0]

    def body(db_ref, ds_ref, idx_ref, dt_ref, dsk_ref):
        iv = idx_ref[...]
        row = lax.broadcasted_iota(jnp.int32, (SUBLANES, LANES), 0)
        lane = lax.broadcasted_iota(jnp.int32, (SUBLANES, LANES), 1)
        dsk = jnp.zeros((SUBLANES, LANES), F32)
        for h in range(n_heads):
            d = db_ref[h]
            acc = jnp.zeros((SUBLANES, LANES), F32)
            for b in range(N_BUCKETS):
                acc = jnp.where((row == 0) & (lane == b), jnp.sum(jnp.where(iv == b, d, 0.0)), acc)
            dt_ref[:, h * LANES:(h + 1) * LANES] = acc
            raw = ds_ref[h // GROUP]
            val = jnp.sum(jnp.where((row == h % GROUP) & (lane == 0), raw, 0.0))
            dsk = jnp.where((row == 0) & (lane == h), val, dsk)
        dsk_ref[...] = dsk

    return pl.pallas_call(
        body, name="table_grads",
        in_specs=[pl.BlockSpec(memory_space=pltpu.VMEM)] * 3,
        out_specs=[pl.BlockSpec(memory_space=pltpu.VMEM)] * 2,
        out_shape=[jax.ShapeDtypeStruct((SUBLANES, n_heads * LANES), F32),
                   jax.ShapeDtypeStruct((SUBLANES, LANES), F32)],
        compiler_params=pltpu.CompilerParams(vmem_limit_bytes=56 * 1024 * 1024),
    )(dbias, dsink_raw, idx)


def _position():
    x, y, c = lax.axis_index("x"), lax.axis_index("y"), lax.axis_index("c")
    return x, y, c


def _hbm(a):
    return pltpu.with_memory_space_constraint(a, pltpu.HBM)


def _split_start(name, bufs, sem_shapes, issue):
    nb, ns = len(bufs), len(sem_shapes)

    def body(*refs):
        buf_refs = refs[:nb]
        sems = refs[nb:nb + ns]
        token = refs[nb + ns + nb]
        issue(buf_refs, sems)
        token[...] = jnp.zeros(token.shape, F32)

    outs = pl.pallas_call(
        body, name=name,
        in_specs=[_HBM] * nb,
        out_specs=[_SEM] * ns + [_HBM] * nb + [_VMEM],
        out_shape=[pltpu.SemaphoreType.DMA(s) for s in sem_shapes] + [pltpu.HBM(b.shape, b.dtype) for b in bufs]
        + [jax.ShapeDtypeStruct((SUBLANES, LANES), F32)],
        input_output_aliases={i: ns + i for i in range(nb)},
        compiler_params=pltpu.CompilerParams(has_side_effects=_EFFECT),
    )(*[_hbm(b) for b in bufs])
    return outs[:ns], outs[ns:ns + nb], outs[-1]


def _split_wait(name, bufs, send, recv, counts, size_of, after):
    nb = len(bufs)

    def body(*refs):
        buf_refs = refs[:nb]
        send_ref, recv_ref = refs[nb], refs[nb + 1]
        x, y, c = _position()
        for w, n in enumerate(counts):
            ref = size_of(buf_refs, w)
            for k in range(n):
                s = sum(counts[:w]) + k
                cp = pltpu.make_async_remote_copy(
                    src_ref=ref, dst_ref=ref, send_sem=send_ref.at[s], recv_sem=recv_ref.at[s],
                    device_id=(x, y, c), device_id_type=MESH)
                cp.wait_send()
                cp.wait_recv()

    return pl.pallas_call(
        body, name=name,
        in_specs=[_HBM] * nb + [_SEM, _SEM, _ANY],
        out_specs=[_HBM] * nb,
        out_shape=[pltpu.HBM(b.shape, b.dtype) for b in bufs],
        input_output_aliases={i: i for i in range(nb)},
        compiler_params=pltpu.CompilerParams(has_side_effects=_EFFECT),
    )(*bufs, send, recv, after)


def _block_of(pos):
    return 4 * pos[0] + 2 * pos[1] + pos[2]


def _shard_of(ref, blk, by_cols):
    aligned = (lambda v, a: v) if isinstance(blk, int) else pl.multiple_of
    if by_cols:
        n = ref.shape[1] // N_DEV
        return ref.at[:, pl.ds(aligned(blk * n, LANES), n)]
    r = ref.shape[0] // N_DEV
    return ref.at[pl.ds(aligned(blk * r, SUBLANES), r), :]


def _place_own(name, land, shard, by_cols, tr=256):
    r, n = shard.shape
    tr = _tile(r, tr)
    mine = _block_of(_position()).astype(jnp.int32).reshape(1)

    def body(m_ref, land_ref, s_ref, o_ref):
        o_ref[...] = s_ref[...]

    if by_cols:
        out = pl.BlockSpec((tr, n), lambda i, m_ref: (i, m_ref[0]))
    else:
        out = pl.BlockSpec((tr, n), lambda i, m_ref: (m_ref[0] * (r // tr) + i, 0))
    return pl.pallas_call(
        body, name=name,
        grid_spec=pltpu.PrefetchScalarGridSpec(
            num_scalar_prefetch=1, grid=(r // tr,),
            in_specs=[_ANY, pl.BlockSpec((tr, n), lambda i, m_ref: (i, 0))], out_specs=out),
        out_shape=jax.ShapeDtypeStruct(land.shape, land.dtype),
        input_output_aliases={1: 0},
        compiler_params=_params(("parallel",)),
    )(mine, land, shard)


def _gather_start(shards, by_cols, groups):
    nw = len(shards)
    lands = [lax.empty((s.shape[0], s.shape[1] * N_DEV) if cols else (s.shape[0] * N_DEV, s.shape[1]), s.dtype)
             for s, cols in zip(shards, by_cols)]

    def issue(bufs, sems):
        x, y, c = _position()
        peers = [(x, y, 1 - c), (1 - x, y, c), (x, 1 - y, c), (1 - x, 1 - y, c)]
        for gi, grp in enumerate(groups):
            for wi, w in enumerate(grp):
                for k, peer in enumerate(peers):
                    pltpu.make_async_remote_copy(
                        src_ref=bufs[w], dst_ref=_shard_of(bufs[nw + w], _block_of((x, y, c)), by_cols[w]),
                        send_sem=sems[2 * gi].at[4 * wi + k], recv_sem=sems[2 * gi + 1].at[4 * wi + k],
                        device_id=peer, device_id_type=MESH).start()

    sem_shapes = [(4 * len(g),) for g in groups for _ in range(2)]
    sems, thru, token = _split_start("gather_start", list(shards) + lands, sem_shapes, issue)
    return sems, thru[:nw], thru[nw:], token


def _gather_forward(name, lands, by_cols):
    nw = len(lands)

    def issue(land, sems):
        x, y, c = _position()
        for w in range(nw):
            for k, chip in enumerate([(1 - x, y), (x, 1 - y), (1 - x, 1 - y)]):
                blk = _shard_of(land[w], _block_of((*chip, c)), by_cols[w])
                pltpu.make_async_remote_copy(
                    src_ref=blk, dst_ref=blk, send_sem=sems[0].at[3 * w + k], recv_sem=sems[1].at[3 * w + k],
                    device_id=(x, y, 1 - c), device_id_type=MESH).start()

    return _split_start(name, lands, [(3 * nw,), (3 * nw,)], issue)


def _first_block(bufs, w, offset=0):
    return bufs[offset + w].at[0]


_PEER_FLIPS = ((0, 0, 1), (1, 0, 0), (1, 0, 1), (0, 1, 0), (0, 1, 1), (1, 1, 0), (1, 1, 1))


def _scatter_start(name, grads, by_cols):
    nw = len(grads)
    lands = []
    for g, cols in zip(grads, by_cols):
        shard = (g.shape[0], g.shape[1] // N_DEV) if cols else (g.shape[0] // N_DEV, g.shape[1])
        lands.append(lax.empty((N_DEV,) + shard, g.dtype))

    def issue(bufs, sems):
        x, y, c = _position()
        flip = lambda v, f: 1 - v if f else v
        for w in range(nw):
            for k, (fx, fy, fc) in enumerate(_PEER_FLIPS):
                peer = (flip(x, fx), flip(y, fy), flip(c, fc))
                pltpu.make_async_remote_copy(
                    src_ref=_shard_of(bufs[w], _block_of(peer), by_cols[w]), dst_ref=bufs[nw + w].at[_block_of((x, y, c))],
                    send_sem=sems[0].at[7 * w + k], recv_sem=sems[1].at[7 * w + k],
                    device_id=peer, device_id_type=MESH).start()

    return _split_start(name, list(grads) + lands, [(7 * nw,), (7 * nw,)], issue)


def _adam(w, g, m, v):
    m = ADAM_B1 * m + (1.0 - ADAM_B1) * g
    v = ADAM_B2 * v + (1.0 - ADAM_B2) * (g * g)
    m_hat = m / (1.0 - ADAM_B1 ** ADAM_STEP)
    v_hat = v / (1.0 - ADAM_B2 ** ADAM_STEP)
    delta = -ADAM_LR * (m_hat / (jnp.sqrt(v_hat) + ADAM_EPS) + ADAM_WD * w)
    return delta, m, v


def _sum_adam(name, landed, grad, by_cols, w, m, v, tr=256):
    R, C = w.shape
    tr = _tile(R, tr)
    mine = _block_of(_position()).astype(jnp.int32).reshape(1)

    def body(me_ref, l_ref, own_ref, w_ref, m_ref, v_ref, g_ref, d_ref, nm_ref, nv_ref):
        own = own_ref[...].astype(F32)
        g = None
        for d in range(N_DEV):
            part = jnp.where(me_ref[0] == d, own, l_ref[d].astype(F32))
            g = part if g is None else g + part
        g_ref[...] = g
        d_ref[...], nm_ref[...], nv_ref[...] = _adam(w_ref[...], g, m_ref[...], v_ref[...])

    tile = pl.BlockSpec((tr, C), lambda i, me_ref: (i, 0))
    if by_cols:
        own = pl.BlockSpec((tr, C), lambda i, me_ref: (i, me_ref[0]))
    else:
        own = pl.BlockSpec((tr, C), lambda i, me_ref: (me_ref[0] * (R // tr) + i, 0))
    return pl.pallas_call(
        body, name=name,
        grid_spec=pltpu.PrefetchScalarGridSpec(
            num_scalar_prefetch=1, grid=(R // tr,),
            in_specs=[pl.BlockSpec((N_DEV, tr, C), lambda i, me_ref: (0, i, 0)), own, tile, tile, tile],
            out_specs=[tile] * 4),
        out_shape=[jax.ShapeDtypeStruct((R, C), F32)] * 4,
        compiler_params=_params(("parallel",)),
    )(mine, landed, grad, w, m, v)


def _small_all_reduce(parts, deps=()):
    W = parts.shape[1]

    def body(p_ref, o_ref, slots, send_sems, recv_sems):
        x, y, c = _position()
        me = 4 * x + 2 * y + c
        slots[me] = jnp.sum(p_ref[...], axis=0, keepdims=True)
        peers = [(x, y, 1 - c), (1 - x, y, c), (1 - x, y, 1 - c), (x, 1 - y, c), (x, 1 - y, 1 - c),
                 (1 - x, 1 - y, c), (1 - x, 1 - y, 1 - c)]
        copies = []
        for k, peer in enumerate(peers):
            cp = pltpu.make_async_remote_copy(
                src_ref=slots.at[me], dst_ref=slots.at[me], send_sem=send_sems.at[k], recv_sem=recv_sems.at[k],
                device_id=peer, device_id_type=MESH)
            cp.start()
            copies.append(cp)
        for cp in copies:
            cp.wait()
        total = slots[0]
        for d in range(1, N_DEV):
            total = total + slots[d]
        o_ref[...] = total

    return _pcall(
        body, deps, name="small_all_reduce",
        in_specs=[pl.BlockSpec(memory_space=pltpu.VMEM)], out_specs=pl.BlockSpec(memory_space=pltpu.VMEM),
        out_shape=jax.ShapeDtypeStruct((1, W), F32),
        scratch_shapes=[pltpu.VMEM((N_DEV, 1, W), F32), pltpu.SemaphoreType.DMA((7,)), pltpu.SemaphoreType.DMA((7,))],
    )(parts)


def _adam_small(w, g, m, v):
    def body(w_ref, g_ref, m_ref, v_ref, d_ref, nm_ref, nv_ref):
        d_ref[...], nm_ref[...], nv_ref[...] = _adam(w_ref[...], g_ref[...], m_ref[...], v_ref[...])

    return pl.pallas_call(
        body, name="adam_small",
        in_specs=[pl.BlockSpec(memory_space=pltpu.VMEM)] * 4, out_specs=[pl.BlockSpec(memory_space=pltpu.VMEM)] * 3,
        out_shape=[jax.ShapeDtypeStruct(w.shape, F32)] * 3,
    )(w, g, m, v)


_GATHER_GROUPS = (("w_in",), ("w_out", "w_up", "ple_w"), ("w_down", "w_gate"))
_COL_SHARDED = ("w_in", "w_up", "ple_w")


class _MeshComm:
    def __init__(self, w, mom, var):
        self.w, self.mom, self.var = w, mom, var
        self.out = {}
        self._scatters = {}

    def gather_begin(self):
        names = [n for g in _GATHER_GROUPS for n in g]
        self._idx = {n: i for i, n in enumerate(names)}
        groups = [[self._idx[n] for n in g] for g in _GATHER_GROUPS]
        by_cols = [n in _COL_SHARDED for n in names]
        self._sems, self._src, lands, token = _gather_start([self.w[n].astype(BF16) for n in names], by_cols, groups)
        self._lands = [_place_own("place_" + n, land, src, cols)
                       for n, land, src, cols in zip(names, lands, self._src, by_cols)]
        return token

    @staticmethod
    def _shard_size(names, offset):
        return lambda bufs, w: _shard_of(bufs[offset + w], 0, names[w] in _COL_SHARDED)

    def gather_arrive(self, gi, after):
        names = _GATHER_GROUPS[gi]
        ids = [self._idx[n] for n in names]
        bufs = [self._src[i] for i in ids] + [self._lands[i] for i in ids]
        out = _split_wait("gather_arrive%d" % gi, bufs, self._sems[2 * gi], self._sems[2 * gi + 1], [4] * len(ids),
                          self._shard_size(names, len(ids)), after)
        self._arrived = out[len(ids):]

    def gather_forward(self, gi):
        by_cols = [n in _COL_SHARDED for n in _GATHER_GROUPS[gi]]
        self._fsems, self._fthru, token = _gather_forward("gather_forward%d" % gi, self._arrived, by_cols)
        return token

    def gather_finish(self, gi, after):
        names = _GATHER_GROUPS[gi]
        out = _split_wait("gather_finish%d" % gi, self._fthru, self._fsems[0], self._fsems[1], [3] * len(names),
                          self._shard_size(names, 0), after)
        return dict(zip(names, out))

    def reduce_begin(self, key, grads):
        names = list(grads)
        sems, thru, token = _scatter_start("scatter_start_" + key, [grads[n] for n in names],
                                           [n in _COL_SHARDED for n in names])
        self._scatters[key] = (names, sems, thru)
        return token

    def reduce_finish(self, key, after):
        names, sems, thru = self._scatters[key]
        nw = len(names)
        out = _split_wait("scatter_wait_" + key, thru, sems[0], sems[1], [N_DEV - 1] * nw,
                          functools.partial(_first_block, offset=nw), after)
        for i, n in enumerate(names):
            self.out[n] = _sum_adam("adam_" + n, out[nw + i], out[i], n in _COL_SHARDED, self.w[n], self.mom[n],
                                    self.var[n])


def _step(x, p, target, gains, comm):
    T, D = x.shape
    n_q = D // (2 * HEAD_DIM)
    n_kv = n_q // GROUP
    cos, sin = _rope_tables(T)
    idx = _bucket_index()

    t = comm.gather_begin()
    u = _rms_fwd("norm_attn", x, gains["attn_norm_g"], deps=(t,))
    comm.gather_arrive(0, u)
    t = comm.gather_forward(0)
    bias = _bias_build(idx, gains["rel_bias_table"].reshape(-1), n_q, deps=(t,))
    full = comm.gather_finish(0, bias)
    proj_a, pb = _in_proj(u, full["w_in"], cos, sin, gains["q_norm_g"], gains["k_norm_g"], n_q + n_kv)
    o_a, lse_a = _attn_a_fwd(pb, n_q, n_kv, 2 * n_q)
    comm.gather_arrive(1, lse_a)
    t = comm.gather_forward(1)
    sink = gains["sink_logits"].reshape(-1)
    b_off = n_q + 2 * n_kv
    o_cat, lse_b = _attn_b_fwd(pb, bias, sink, o_a, b_off, n_q, n_kv, deps=(t,))
    full.update(comm.gather_finish(1, lse_b))
    h1, m_in = _mm_nn_rms("out_proj", o_cat, full["w_out"], x, gains["mlp_norm_g"])

    def up_epilogue(acc, extra, outs):
        outs[0][...] = acc.astype(BF16)
        r = jnp.maximum(acc, 0.0)
        outs[1][...] = (r * r).astype(BF16)

    a_act, f_act = _mm_nn("up_proj", m_in, full["w_up"], epilogue=up_epilogue, out_dtypes=[BF16, BF16], tn=2048)
    comm.gather_arrive(2, f_act)
    t = comm.gather_forward(2)
    p_b = p.astype(BF16)
    pe = _mm_nn("ple_proj", p_b, full["ple_w"], deps=(t,))
    full.update(comm.gather_finish(2, pe))
    h2 = _mm_nn("down_proj", f_act, full["w_down"], epilogue=_store_add, extras=(h1,), tn=256)
    gn = _rms_fwd("norm_gate", h2, gains["gate_norm_g"])

    dh3, dz, dpe, dg_final, dg_ple, loss_part = _gate_tail(gn, full["w_gate"], h2, pe, target, gains["ple_norm_g"],
                                                           gains["final_norm_g"])
    gw_gate = _mm_tn("grad_w_gate", gn, dz)
    gw_ple = _mm_tn("grad_ple_w", p_b, dpe)
    dh2, dh2_b, dg_gate = _mm_nt_rms_bwd("d_gate_in", dz, full["w_gate"], h2, gains["gate_norm_g"], dh3)
    gw_down = _mm_tn("grad_w_down", f_act, dh2_b)
    t = comm.reduce_begin("b", dict(w_gate=gw_gate, ple_w=gw_ple, w_down=gw_down))

    def act_bwd(acc, extra, outs):
        outs[0][...] = (acc * (2.0 * jnp.maximum(extra[0][...].astype(F32), 0.0))).astype(BF16)

    da = _mm_nt("d_act", dh2_b, full["w_down"], out_dtype=BF16, epilogue=act_bwd, extras=(a_act,), tn=2048, deps=(t,))
    gw_up = _mm_tn("grad_w_up", m_in, da)
    dm = _mm_nt("d_mlp_in", da, full["w_up"], tn=256)
    dh1, dh1_b, dg_mlp = _rms_bwd("norm_mlp_bwd", dm, h1, gains["mlp_norm_g"], dh2)
    gw_out = _mm_tn("grad_w_out", o_cat, dh1_b)
    t = comm.reduce_begin("d", dict(w_up=gw_up, w_out=gw_out))
    d_o = _mm_nt("d_attn_out", dh1_b, full["w_out"], out_dtype=BF16, deps=(t,))
    dqa, dka_t, dva_t = _attn_a_bwd(pb, o_cat, d_o, lse_a, n_q, n_kv)
    dka, dva = dka_t.T, dva_t.T
    dqb, dkb, dvb, dbias, dsink_raw = _attn_b_bwd(pb, o_cat, d_o, lse_b, bias, sink, b_off, n_q, n_kv, n_q)
    dtable, dsink = _table_grads(dbias, dsink_raw, idx)
    dproj, dg_q, dg_k = _dproj(proj_a, dqa, dka, dva, dqb, dkb, dvb, cos, sin, gains["q_norm_g"], gains["k_norm_g"])
    gw_in = _mm_tn("grad_w_in", u, dproj)
    t = comm.reduce_begin("e", dict(w_in=gw_in))
    dx, _, dg_attn = _mm_nt_rms_bwd("d_attn_in", dproj, full["w_in"], x, gains["attn_norm_g"], dh1, deps=(t,))
    for key in "bd":
        comm.reduce_finish(key, dx)

    parts = jnp.concatenate([dg_attn, dg_mlp, dg_ple, dg_gate, dg_final, dg_q, dg_k, dtable, dsink, loss_part], axis=1)
    return dx, parts


_SHARDED = ("w_in", "w_out", "w_up", "w_down", "ple_w", "w_gate")
_VECTORS = ("attn_norm_g", "mlp_norm_g", "ple_norm_g", "gate_norm_g", "final_norm_g")
_ORDER = ("attn_norm_g", "w_in", "q_norm_g", "k_norm_g", "sink_logits", "w_out", "mlp_norm_g", "w_up", "w_down",
          "ple_w", "ple_norm_g", "gate_norm_g", "w_gate", "rel_bias_table", "final_norm_g")


def _pack_small(vals, n_heads):
    lane_pad = lambda v: jnp.pad(v, ((0, 0), (0, LANES - v.shape[1])))
    table = lane_pad(vals["rel_bias_table"].T).reshape(1, n_heads * LANES)
    return jnp.concatenate(
        [vals[n].reshape(1, -1) for n in _VECTORS] + [vals["q_norm_g"], vals["k_norm_g"], table,
                                                      lane_pad(vals["sink_logits"]), jnp.zeros((1, LANES), F32)], axis=1)


def _unpack_small(row, like, n_heads):
    out, off = {}, 0
    for n in _VECTORS:
        out[n] = row[:, off:off + like[n].size].reshape(like[n].shape)
        off += like[n].size
    for n in ("q_norm_g", "k_norm_g"):
        out[n] = row[:, off:off + LANES]
        off += LANES
    out["rel_bias_table"] = row[:, off:off + n_heads * LANES].reshape(n_heads, LANES)[:, :N_BUCKETS].T
    off += n_heads * LANES
    out["sink_logits"] = row[:, off:off + n_heads]
    off += LANES
    return out, row[0, off]


def kernel(x, p, attn_norm_g, w_in, q_norm_g, k_norm_g, sink_logits, w_out, mlp_norm_g, w_up, w_down, ple_w, ple_norm_g, gate_norm_g, w_gate, rel_bias_table, final_norm_g, loss_target, m_attn_norm_g, m_w_in, m_q_norm_g, m_k_norm_g, m_sink_logits, m_w_out, m_mlp_norm_g, m_w_up, m_w_down, m_ple_w, m_ple_norm_g, m_gate_norm_g, m_w_gate, m_rel_bias_table, m_final_norm_g, v_attn_norm_g, v_w_in, v_q_norm_g, v_k_norm_g, v_sink_logits, v_w_out, v_mlp_norm_g, v_w_up, v_w_down, v_ple_w, v_ple_norm_g, v_gate_norm_g, v_w_gate, v_rel_bias_table, v_final_norm_g):
    w = dict(attn_norm_g=attn_norm_g, w_in=w_in[0], q_norm_g=q_norm_g, k_norm_g=k_norm_g, sink_logits=sink_logits,
             w_out=w_out[0], mlp_norm_g=mlp_norm_g, w_up=w_up[0], w_down=w_down[0], ple_w=ple_w[0],
             ple_norm_g=ple_norm_g, gate_norm_g=gate_norm_g, w_gate=w_gate[0], rel_bias_table=rel_bias_table,
             final_norm_g=final_norm_g)
    mom = dict(attn_norm_g=m_attn_norm_g, w_in=m_w_in[0], q_norm_g=m_q_norm_g, k_norm_g=m_k_norm_g,
               sink_logits=m_sink_logits, w_out=m_w_out[0], mlp_norm_g=m_mlp_norm_g, w_up=m_w_up[0],
               w_down=m_w_down[0], ple_w=m_ple_w[0], ple_norm_g=m_ple_norm_g, gate_norm_g=m_gate_norm_g,
               w_gate=m_w_gate[0], rel_bias_table=m_rel_bias_table, final_norm_g=m_final_norm_g)
    var = dict(attn_norm_g=v_attn_norm_g, w_in=v_w_in[0], q_norm_g=v_q_norm_g, k_norm_g=v_k_norm_g,
               sink_logits=v_sink_logits, w_out=v_w_out[0], mlp_norm_g=v_mlp_norm_g, w_up=v_w_up[0],
               w_down=v_w_down[0], ple_w=v_ple_w[0], ple_norm_g=v_ple_norm_g, gate_norm_g=v_gate_norm_g,
               w_gate=v_w_gate[0], rel_bias_table=v_rel_bias_table, final_norm_g=v_final_norm_g)
    D = x.shape[-1]
    n_heads = D // (2 * HEAD_DIM)

    gains = {n: w[n] for n in w if n not in _SHARDED}
    gains["final_norm_g"] = final_norm_g.reshape(1, -1)

    comm = _MeshComm(w, mom, var)
    dx, parts = _step(x[0], p[0, 0], loss_target[0], gains, comm)

    small_g = _small_all_reduce(parts, deps=[comm.out[n][0] for n in comm.out])
    comm.reduce_finish("e", small_g)

    g_out, d_out, m_out, v_out = {}, {}, {}, {}
    for n in _SHARDED:
        g, d, nm, nv = comm.out[n]
        g_out[n], d_out[n], m_out[n], v_out[n] = g[None], d[None], nm[None], nv[None]

    small = {n: v for n, v in w.items() if n not in _SHARDED}
    pack = lambda vals: _pack_small({n: vals[n] for n in small}, n_heads)
    sd, sm, sv = _adam_small(pack(w), small_g, pack(mom), pack(var))
    sg, loss = _unpack_small(small_g, small, n_heads)
    g_out.update(sg)
    for dst, row in ((d_out, sd), (m_out, sm), (v_out, sv)):
        dst.update(_unpack_small(row, small, n_heads)[0])

    return (loss, dx[None], *[g_out[n] for n in _ORDER], *[d_out[n] for n in _ORDER],
            *[m_out[n] for n in _ORDER], *[v_out[n] for n in _ORDER])
```

```python
import functools
import math

import numpy as np
import jax
import jax.numpy as jnp
from jax import lax
from jax.experimental import pallas as pl
from jax.experimental.pallas import tpu as pltpu

F32 = jnp.float32
BF16 = jnp.bfloat16

N_DEV = 8
N_CHIP = 4
HEAD_DIM = 128
GROUP = 4
GRID_W = 64
WINDOW = 128
BLOCK_Q = 128
N_BUCKETS = 32
MAX_DISTANCE = 128
ROPE_THETA = 10000.0
EPS = 1e-6
NEG_INF = -1e30
ADAM_LR = 0.001
ADAM_B1 = 0.9
ADAM_B2 = 0.999
ADAM_EPS = 1e-08
ADAM_WD = 0.01
ADAM_STEP = 10
LOG2E = math.log2(math.e)
LANES = 128
SUBLANES = 8
MESH = pl.DeviceIdType.MESH

_NT = (((1,), (1,)), ((), ()))
_NN = (((1,), (0,)), ((), ()))
_TN = (((0,), (0,)), ((), ()))


def _tile(dim, pref):
    return pref if dim % pref == 0 else dim


def _params(sem):
    return pltpu.CompilerParams(dimension_semantics=sem, vmem_limit_bytes=56 * 1024 * 1024)


_HBM = pl.BlockSpec(memory_space=pltpu.HBM)
_SEM = pl.BlockSpec(memory_space=pltpu.SEMAPHORE)
_ANY = pl.BlockSpec(memory_space=pl.ANY)
_VMEM = pl.BlockSpec(memory_space=pltpu.VMEM)
_EFFECT = pltpu.SideEffectType.DATAFLOW_SIDE_EFFECTING


def _pcall(body, deps=(), *, in_specs, into=None, **kw):
    deps = [d for d in deps if d is not None]
    nd = len(deps)
    if into is not None:
        deps = [into[0]] + deps
        nd += 1
        kw["input_output_aliases"] = {0: into[1]}

    def wrapped(*refs):
        body(*refs[nd:])

    call = pl.pallas_call(wrapped, in_specs=[_ANY] * nd + list(in_specs), **kw)
    return lambda *args: call(*deps, *args)


def _mm(name, a, b, dims, grid, a_spec, b_spec, out_shape, out_specs, acc_shape, epilogue,
        extras=(), extra_specs=(), deps=(), semantics=("parallel", "parallel", "arbitrary")):
    nk = grid[2]
    n_extra = len(extras)

    def body(*refs):
        a_ref, b_ref = refs[0], refs[1]
        extra = refs[2:2 + n_extra]
        outs = refs[2 + n_extra:-1]
        acc = refs[-1]
        part = lax.dot_general(a_ref[...], b_ref[...], dims, preferred_element_type=F32)
        if nk == 1:
            epilogue(part, extra, outs)
        else:
            k = pl.program_id(2)

            @pl.when(k == 0)
            def _():
                acc[...] = part

            @pl.when(k > 0)
            def _():
                acc[...] += part

            @pl.when(k == nk - 1)
            def _():
                epilogue(acc[...], extra, outs)

    return _pcall(
        body, deps, name=name, grid=grid,
        in_specs=[a_spec, b_spec, *extra_specs],
        out_specs=out_specs, out_shape=out_shape,
        scratch_shapes=[pltpu.VMEM(acc_shape if nk > 1 else (SUBLANES, LANES), F32)],
        compiler_params=_params(semantics),
    )(a, b, *extras)


def _store(dtype):
    def ep(acc, extra, outs):
        outs[0][...] = acc.astype(dtype)
    return ep


def _store_add(acc, extra, outs):
    outs[0][...] = acc + extra[0][...]


def _mm_nn(name, a, b, out_dtype=F32, epilogue=None, extras=(), n_out=1, out_dtypes=None, tm=1024, tn=1024, tk=None,
           deps=()):
    M, K = a.shape
    N = b.shape[1]
    tm, tn, tk = _tile(M, tm), _tile(N, tn), _tile(K, tk or K)
    b_spec = pl.BlockSpec((tk, tn), lambda i, j, k: (k, j))
    grid = (M // tm, N // tn, K // tk)
    o_spec = pl.BlockSpec((tm, tn), lambda i, j, k: (i, j))
    out_dtypes = out_dtypes or [out_dtype] * n_out
    out_shape = [jax.ShapeDtypeStruct((M, N), d) for d in out_dtypes]
    res = _mm(name, a, b, _NN, grid, pl.BlockSpec((tm, tk), lambda i, j, k: (i, k)), b_spec,
              out_shape, [o_spec] * len(out_dtypes), (tm, tn), epilogue or _store(out_dtype),
              extras, [o_spec] * len(extras), deps)
    return res if len(out_dtypes) > 1 else res[0]


def _mm_nt(name, a, b, out_dtype=F32, epilogue=None, extras=(), tm=1024, tn=1024, tk=None, deps=()):
    M, C = a.shape
    N = b.shape[0]
    tm, tn, tk = _tile(M, tm), _tile(N, tn), _tile(C, tk or C)
    b_spec = pl.BlockSpec((tn, tk), lambda i, j, k: (j, k))
    grid = (M // tm, N // tn, C // tk)
    o_spec = pl.BlockSpec((tm, tn), lambda i, j, k: (i, j))
    return _mm(name, a, b, _NT, grid, pl.BlockSpec((tm, tk), lambda i, j, k: (i, k)), b_spec,
               [jax.ShapeDtypeStruct((M, N), out_dtype)], [o_spec], (tm, tn), epilogue or _store(out_dtype),
               extras, [o_spec] * len(extras), deps)[0]


def _mm_tn(name, a, b, out_dtype=BF16, tm=1024, tn=512, tk=None, deps=()):
    T, M = a.shape
    N = b.shape[1]
    tm, tn, tk = _tile(M, tm), _tile(N, tn), _tile(T, tk or T)
    out_shape = jax.ShapeDtypeStruct((M, N), out_dtype)
    o_spec = pl.BlockSpec((tm, tn), lambda i, j, k: (i, j))
    grid = (M // tm, N // tn, T // tk)
    return _mm(name, a, b, _TN, grid, pl.BlockSpec((tk, tm), lambda i, j, k: (k, i)),
               pl.BlockSpec((tk, tn), lambda i, j, k: (k, j)), [out_shape], [o_spec], (tm, tn), _store(out_dtype),
               deps=deps)[0]


def _mean_last(v):
    return jnp.mean(v, axis=-1, keepdims=True)


def _rows_to_sublanes(v):
    r, c = v.shape
    return jnp.sum(v.reshape(r // SUBLANES, SUBLANES, c), axis=0)


def _accumulate(ref, val, first):
    @pl.when(first)
    def _():
        ref[...] = val

    @pl.when(jnp.logical_not(first))
    def _():
        ref[...] += val


def _rms_fwd(name, x, g, tr=256, deps=()):
    T, D = x.shape
    tr = _tile(T, tr)

    def body(x_ref, g_ref, o_ref):
        xv = x_ref[...]
        r = lax.rsqrt(_mean_last(xv * xv) + EPS)
        o_ref[...] = (xv * r * g_ref[...]).astype(BF16)

    row = pl.BlockSpec((tr, D), lambda i: (i, 0))
    return _pcall(
        body, deps, name=name, grid=(T // tr,),
        in_specs=[row, pl.BlockSpec((1, D), lambda i: (0, 0))],
        out_specs=row, out_shape=jax.ShapeDtypeStruct((T, D), BF16),
        compiler_params=_params(("parallel",)),
    )(x, g)


def _rms_bwd(name, dyn, x, g, dres, tr=256, deps=()):
    T, D = x.shape
    tr = _tile(T, tr)

    def body(dy_ref, x_ref, g_ref, dr_ref, dx_ref, dxb_ref, dg_ref):
        xv = x_ref[...]
        r = lax.rsqrt(_mean_last(xv * xv) + EPS)
        xn = xv * r
        dy = dy_ref[...]
        dxn = dy * g_ref[...]
        dx = dr_ref[...] + r * (dxn - xn * _mean_last(dxn * xn))
        dx_ref[...] = dx
        dxb_ref[...] = dx.astype(BF16)
        _accumulate(dg_ref, _rows_to_sublanes(dy * xn), pl.program_id(0) == 0)

    row = pl.BlockSpec((tr, D), lambda i: (i, 0))
    return _pcall(
        body, deps, name=name, grid=(T // tr,),
        in_specs=[row, row, pl.BlockSpec((1, D), lambda i: (0, 0)), row],
        out_specs=[row, row, pl.BlockSpec((SUBLANES, D), lambda i: (0, 0))],
        out_shape=[jax.ShapeDtypeStruct((T, D), F32), jax.ShapeDtypeStruct((T, D), BF16),
                   jax.ShapeDtypeStruct((SUBLANES, D), F32)],
        compiler_params=_params(("arbitrary",)),
    )(dyn, x, g, dres)


def _mm_nn_rms(name, a, b, res, g, tm=512, deps=()):
    M, K = a.shape
    N = b.shape[1]
    tm = _tile(M, tm)

    def epilogue(acc, extra, outs):
        h = acc + extra[0][...]
        outs[0][...] = h
        outs[1][...] = (h * lax.rsqrt(_mean_last(h * h) + EPS) * extra[1][...]).astype(BF16)

    row = pl.BlockSpec((tm, N), lambda i, j, k: (i, 0))
    return _mm(name, a, b, _NN, (M // tm, 1, 1), pl.BlockSpec((tm, K), lambda i, j, k: (i, 0)),
               pl.BlockSpec((K, N), lambda i, j, k: (0, 0)),
               [jax.ShapeDtypeStruct((M, N), F32), jax.ShapeDtypeStruct((M, N), BF16)], [row, row], (tm, N), epilogue,
               (res, g), [row, pl.BlockSpec((1, N), lambda i, j, k: (0, 0))], deps)


def _mm_nt_rms_bwd(name, a, b, x, g, dres, with_bf16=True, tm=256, deps=()):
    M, C = a.shape
    N = b.shape[0]
    tm = _tile(M, tm)

    def epilogue(dy, extra, outs):
        x_ref, dr_ref, g_ref = extra
        xv = x_ref[...]
        r = lax.rsqrt(_mean_last(xv * xv) + EPS)
        xn = xv * r
        dxn = dy * g_ref[...]
        dx = dr_ref[...] + r * (dxn - xn * _mean_last(dxn * xn))
        outs[0][...] = dx
        if with_bf16:
            outs[1][...] = dx.astype(BF16)
        _accumulate(outs[-1], _rows_to_sublanes(dy * xn), pl.program_id(0) == 0)

    row = pl.BlockSpec((tm, N), lambda i, j, k: (i, 0))
    copies = [jax.ShapeDtypeStruct((M, N), F32)] + ([jax.ShapeDtypeStruct((M, N), BF16)] if with_bf16 else [])
    return _mm(name, a, b, _NT, (M // tm, 1, 1), pl.BlockSpec((tm, C), lambda i, j, k: (i, 0)),
               pl.BlockSpec((N, C), lambda i, j, k: (0, 0)),
               copies + [jax.ShapeDtypeStruct((SUBLANES, N), F32)],
               [row] * len(copies) + [pl.BlockSpec((SUBLANES, N), lambda i, j, k: (0, 0))], (tm, N), epilogue,
               (x, dres, g), [row, row, pl.BlockSpec((1, N), lambda i, j, k: (0, 0))], deps,
               semantics=("arbitrary", "arbitrary", "arbitrary"))


def _gate_tail(gn, w_gate, h2, pe, target, g_ple, g_final, tm=256):
    T, D = h2.shape
    tm = _tile(T, tm)

    def epilogue(z, extra, outs):
        h2_ref, pe_ref, t_ref, gp_ref, gf_ref = extra
        dh3_ref, dz_ref, dpe_ref, dgf_ref, dgp_ref, loss_ref = outs
        first = pl.program_id(0) == 0
        pev = pe_ref[...]
        r3 = lax.rsqrt(_mean_last(pev * pev) + EPS)
        en = pev * r3
        e = en * gp_ref[...]
        gate = 1.0 / (1.0 + jnp.exp(-z))
        h3 = h2_ref[...] + gate * e
        r5 = lax.rsqrt(_mean_last(h3 * h3) + EPS)
        hn = h3 * r5
        diff = hn * gf_ref[...] - t_ref[...]
        loss_rows = 0.5 * _mean_last(diff * diff)
        row0 = lax.broadcasted_iota(jnp.int32, (SUBLANES, LANES), 0) == 0
        _accumulate(loss_ref, jnp.where(row0, jnp.sum(loss_rows), 0.0), first)
        dy = diff * (1.0 / D)
        _accumulate(dgf_ref, _rows_to_sublanes(dy * hn), first)
        dhn = dy * gf_ref[...]
        dh3 = r5 * (dhn - hn * _mean_last(dhn * hn))
        dh3_ref[...] = dh3
        dgate = dh3 * e
        de = dh3 * gate
        dz_ref[...] = (dgate * gate * (1.0 - gate)).astype(BF16)
        _accumulate(dgp_ref, _rows_to_sublanes(de * en), first)
        den = de * gp_ref[...]
        dpe_ref[...] = (r3 * (den - en * _mean_last(den * en))).astype(BF16)

    row = pl.BlockSpec((tm, D), lambda i, j, k: (i, 0))
    vec = pl.BlockSpec((1, D), lambda i, j, k: (0, 0))
    part = pl.BlockSpec((SUBLANES, D), lambda i, j, k: (0, 0))
    return _mm("gate_tail", gn, w_gate, _NN, (T // tm, 1, 1), row, pl.BlockSpec(w_gate.shape, lambda i, j, k: (0, 0)),
               [jax.ShapeDtypeStruct((T, D), F32), jax.ShapeDtypeStruct((T, D), BF16),
                jax.ShapeDtypeStruct((T, D), BF16), jax.ShapeDtypeStruct((SUBLANES, D), F32),
                jax.ShapeDtypeStruct((SUBLANES, D), F32), jax.ShapeDtypeStruct((SUBLANES, LANES), F32)],
               [row, row, row, part, part, pl.BlockSpec((SUBLANES, LANES), lambda i, j, k: (0, 0))], (tm, D), epilogue,
               (h2, pe, target, g_ple, g_final), [row, row, row, vec, vec],
               semantics=("arbitrary", "arbitrary", "arbitrary"))


def _rope_tables(T):
    pos = np.arange(T)
    half = HEAD_DIM // 2
    inv = (ROPE_THETA ** (-np.arange(0, half, 2, dtype=np.float32) / half)).astype(np.float32)
    ang_r = (pos // GRID_W).astype(np.float32)[:, None] * inv
    ang_c = (pos % GRID_W).astype(np.float32)[:, None] * inv
    cos = np.concatenate([np.cos(ang_r), np.cos(ang_r), np.cos(ang_c), np.cos(ang_c)], axis=-1)
    sin = np.concatenate([-np.sin(ang_r), np.sin(ang_r), -np.sin(ang_c), np.sin(ang_c)], axis=-1)
    return jnp.asarray(cos, F32), jnp.asarray(sin, F32)


def _swap32(x):
    lane = lax.broadcasted_iota(jnp.int32, x.shape, 1)
    return jnp.where((lane % 64) < 32, pltpu.roll(x, 96, 1), pltpu.roll(x, 32, 1))


def _in_proj(u, w_in, cos, sin, g_q, g_k, n_norm, tm=512):
    T, K = u.shape
    W = w_in.shape[1]
    tm = _tile(T, tm)
    n_q = n_norm * GROUP // (GROUP + 1)
    wa = n_norm * HEAD_DIM

    def epilogue(acc, extra, outs):
        c_ref, s_ref, gq_ref, gk_ref = extra
        raw_ref, o_ref = outs
        c, s = c_ref[...], s_ref[...]
        raw_ref[...] = acc[:, :wa]
        for h in range(n_norm):
            cols = slice(h * HEAD_DIM, (h + 1) * HEAD_DIM)
            xv = acc[:, cols]
            g = gq_ref[...] if h < n_q else gk_ref[...]
            xn = xv * lax.rsqrt(_mean_last(xv * xv) + EPS) * g
            o_ref[:, cols] = (xn * c + _swap32(xn) * s).astype(BF16)
        o_ref[:, wa:] = acc[:, wa:].astype(BF16)

    tab = pl.BlockSpec((tm, HEAD_DIM), lambda i, j, k: (i, 0))
    vec = pl.BlockSpec((1, HEAD_DIM), lambda i, j, k: (0, 0))
    return _mm("in_proj", u, w_in, _NN, (T // tm, 1, 1), pl.BlockSpec((tm, K), lambda i, j, k: (i, 0)),
               pl.BlockSpec((K, W), lambda i, j, k: (0, 0)),
               [jax.ShapeDtypeStruct((T, wa), F32), jax.ShapeDtypeStruct((T, W), BF16)],
               [pl.BlockSpec((tm, wa), lambda i, j, k: (i, 0)), pl.BlockSpec((tm, W), lambda i, j, k: (i, 0))],
               (tm, W), epilogue, (cos, sin, g_q, g_k), [tab, tab, vec, vec])


def _dproj(proj_a, dqa, dka, dva, dqb, dkb, dvb, cos, sin, g_q, g_k, tr=256):
    T, wa = proj_a.shape
    tr = _tile(T, tr)
    n_q = dqa.shape[1] // HEAD_DIM
    n_kv = dka.shape[1] // HEAD_DIM
    W = wa + dva.shape[1] + dqb.shape[1] + dkb.shape[1] + dvb.shape[1]

    def body(p_ref, dqa_ref, dka_ref, dva_ref, dqb_ref, dkb_ref, dvb_ref, c_ref, s_ref, gq_ref, gk_ref,
             o_ref, dgq_ref, dgk_ref):
        c, s = c_ref[...], s_ref[...]
        dgq = jnp.zeros((SUBLANES, HEAD_DIM), F32)
        dgk = jnp.zeros((SUBLANES, HEAD_DIM), F32)
        for h in range(n_q + n_kv):
            cols = slice(h * HEAD_DIM, (h + 1) * HEAD_DIM)
            xv = p_ref[:, cols]
            r = lax.rsqrt(_mean_last(xv * xv) + EPS)
            xn = xv * r
            if h < n_q:
                d = dqa_ref[:, cols]
                g = gq_ref[...]
            else:
                d = dka_ref[:, (h - n_q) * HEAD_DIM:(h - n_q + 1) * HEAD_DIM]
                g = gk_ref[...]
            dqn = d * c + _swap32(d * s)
            part = _rows_to_sublanes(dqn * xn)
            if h < n_q:
                dgq = dgq + part
            else:
                dgk = dgk + part
            dxn = dqn * g
            o_ref[:, cols] = (r * (dxn - xn * _mean_last(dxn * xn))).astype(BF16)
        off = wa
        for ref in (dva_ref, dqb_ref, dkb_ref, dvb_ref):
            w = ref.shape[1]
            o_ref[:, off:off + w] = ref[...].astype(BF16)
            off += w
        first = pl.program_id(0) == 0
        _accumulate(dgq_ref, dgq, first)
        _accumulate(dgk_ref, dgk, first)

    def row(w):
        return pl.BlockSpec((tr, w), lambda i: (i, 0))

    vec = pl.BlockSpec((1, HEAD_DIM), lambda i: (0, 0))
    part = pl.BlockSpec((SUBLANES, HEAD_DIM), lambda i: (0, 0))
    return pl.pallas_call(
        body, name="dproj", grid=(T // tr,),
        in_specs=[row(wa), row(dqa.shape[1]), row(dka.shape[1]), row(dva.shape[1]), row(dqb.shape[1]),
                  row(dkb.shape[1]), row(dvb.shape[1]), row(HEAD_DIM), row(HEAD_DIM), vec, vec],
        out_specs=[row(W), part, part],
        out_shape=[jax.ShapeDtypeStruct((T, W), BF16), jax.ShapeDtypeStruct((SUBLANES, HEAD_DIM), F32),
                   jax.ShapeDtypeStruct((SUBLANES, HEAD_DIM), F32)],
        compiler_params=_params(("arbitrary",)),
    )(proj_a, dqa, dka, dva, dqb, dkb, dvb, cos, sin, g_q, g_k)


def _attn_a_fwd(pb, n_q, n_kv, out_heads, tq=1024, tc=1024):
    T = pb.shape[0]
    tq, tc = _tile(T, tq), _tile(T, tc)
    scale = HEAD_DIM ** -0.5
    c = scale * LOG2E

    def body(q_ref, k_ref, v_ref, o_ref, lse_ref):
        q = q_ref[...]
        m = l = acc = None
        for j in range(T // tc):
            keys = slice(j * tc, (j + 1) * tc)
            s = lax.dot_general(q, k_ref[keys, :], _NT, preferred_element_type=F32)
            mj = jnp.max(s, axis=-1, keepdims=True)
            m_new = mj if j == 0 else jnp.maximum(m, mj)
            p = jnp.exp2((s - m_new) * c)
            pv = lax.dot_general(p.astype(BF16), v_ref[keys, :], _NN, preferred_element_type=F32)
            if j == 0:
                l, acc = jnp.sum(p, axis=-1, keepdims=True), pv
            else:
                alpha = jnp.exp2((m - m_new) * c)
                l = alpha * l + jnp.sum(p, axis=-1, keepdims=True)
                acc = alpha * acc + pv
            m = m_new
        o_ref[...] = (acc / l).astype(BF16)
        lse_ref[...] = m * scale + jnp.log(l)

    return pl.pallas_call(
        body, name="attn_a_fwd", grid=(n_kv, GROUP, T // tq),
        in_specs=[pl.BlockSpec((tq, HEAD_DIM), lambda kv, g, i: (i, kv * GROUP + g)),
                  pl.BlockSpec((T, HEAD_DIM), lambda kv, g, i: (0, n_q + kv)),
                  pl.BlockSpec((T, HEAD_DIM), lambda kv, g, i: (0, n_q + n_kv + kv))],
        out_specs=[pl.BlockSpec((tq, HEAD_DIM), lambda kv, g, i: (i, kv * GROUP + g)),
                   pl.BlockSpec((None, tq, 1), lambda kv, g, i: (kv * GROUP + g, i, 0))],
        out_shape=[jax.ShapeDtypeStruct((T, out_heads * HEAD_DIM), BF16), jax.ShapeDtypeStruct((n_q, T, 1), F32)],
        compiler_params=_params(("parallel", "parallel", "parallel")),
    )(pb, pb, pb)


def _attn_a_bwd(pb, o_cat, d_o, lse, n_q, n_kv, tq=512, tc=512):
    T = pb.shape[0]
    tq, tc = _tile(T, tq), _tile(T, tc)
    scale = HEAD_DIM ** -0.5
    c = scale * LOG2E

    def body(q_ref, k_ref, v_ref, o_ref, do_ref, lse_ref, dq_ref, dkt_ref, dvt_ref):
        q, do = q_ref[...], do_ref[...]
        qt, dot = q.T, do.T
        delta = jnp.sum(do.astype(F32) * o_ref[...].astype(F32), axis=-1, keepdims=True)
        lse2 = lse_ref[...] * LOG2E

        @pl.when(jnp.logical_and(pl.program_id(1) == 0, pl.program_id(2) == 0))
        def _():
            dkt_ref[...] = jnp.zeros(dkt_ref.shape, F32)
            dvt_ref[...] = jnp.zeros(dvt_ref.shape, F32)

        dq = None
        for j in range(T // tc):
            keys = slice(j * tc, (j + 1) * tc)
            kc, vc = k_ref[keys, :], v_ref[keys, :]
            s = lax.dot_general(q, kc, _NT, preferred_element_type=F32)
            p = jnp.exp2(s * c - lse2)
            dp = lax.dot_general(do, vc, _NT, preferred_element_type=F32)
            ds = (p * (dp - delta) * scale).astype(BF16)
            dqj = lax.dot_general(ds, kc, _NN, preferred_element_type=F32)
            dq = dqj if dq is None else dq + dqj
            dvt_ref[:, keys] += lax.dot_general(dot, p.astype(BF16), _NN, preferred_element_type=F32)
            dkt_ref[:, keys] += lax.dot_general(qt, ds, _NN, preferred_element_type=F32)
        dq_ref[...] = dq

    qmap = lambda kv, g, i: (i, kv * GROUP + g)
    return pl.pallas_call(
        body, name="attn_a_bwd", grid=(n_kv, GROUP, T // tq),
        in_specs=[pl.BlockSpec((tq, HEAD_DIM), qmap),
                  pl.BlockSpec((T, HEAD_DIM), lambda kv, g, i: (0, n_q + kv)),
                  pl.BlockSpec((T, HEAD_DIM), lambda kv, g, i: (0, n_q + n_kv + kv)),
                  pl.BlockSpec((tq, HEAD_DIM), qmap),
                  pl.BlockSpec((tq, HEAD_DIM), qmap),
                  pl.BlockSpec((None, tq, 1), lambda kv, g, i: (kv * GROUP + g, i, 0))],
        out_specs=[pl.BlockSpec((tq, HEAD_DIM), qmap),
                   pl.BlockSpec((HEAD_DIM, T), lambda kv, g, i: (kv, 0)),
                   pl.BlockSpec((HEAD_DIM, T), lambda kv, g, i: (kv, 0))],
        out_shape=[jax.ShapeDtypeStruct((T, n_q * HEAD_DIM), F32),
                   jax.ShapeDtypeStruct((n_kv * HEAD_DIM, T), F32),
                   jax.ShapeDtypeStruct((n_kv * HEAD_DIM, T), F32)],
        compiler_params=_params(("parallel", "arbitrary", "arbitrary")),
    )(pb, pb, pb, o_cat, d_o, lse)


def _bucket_index():
    r = np.arange(BLOCK_Q)[:, None]
    j = np.arange(3 * BLOCK_Q)[None, :]
    rel = (j - BLOCK_Q) - r
    nb = N_BUCKETS // 2
    ret = np.where(rel > 0, nb, 0)
    n = np.abs(rel)
    max_exact = nb // 2
    nf = np.maximum(n, 1).astype(np.float32)
    large = max_exact + (np.log(nf / max_exact) / math.log(MAX_DISTANCE / max_exact) * (nb - max_exact)).astype(np.int32)
    large = np.minimum(large, nb - 1)
    return jnp.asarray(ret + np.where(n < max_exact, n, large), jnp.int32)


def _bias_build(idx, table_flat, n_heads, deps=()):
    def body(idx_ref, tab_ref, o_ref):
        h = pl.program_id(0)
        iv = idx_ref[...]
        acc = jnp.zeros(iv.shape, F32)
        for b in range(N_BUCKETS):
            acc = jnp.where(iv == b, tab_ref[b * n_heads + h], acc)
        r = lax.broadcasted_iota(jnp.int32, iv.shape, 0)
        j = lax.broadcasted_iota(jnp.int32, iv.shape, 1)
        o_ref[...] = jnp.where(jnp.abs(j - BLOCK_Q - r) <= WINDOW, acc, NEG_INF)

    return _pcall(
        body, deps, name="bias_build", grid=(n_heads,),
        in_specs=[pl.BlockSpec(idx.shape, lambda h: (0, 0)), pl.BlockSpec(memory_space=pltpu.SMEM)],
        out_specs=pl.BlockSpec((None,) + idx.shape, lambda h: (h, 0, 0)),
        out_shape=jax.ShapeDtypeStruct((n_heads,) + idx.shape, F32),
        compiler_params=_params(("parallel",)),
    )(idx, table_flat)


def _in_sequence(n, T):
    j = lax.broadcasted_iota(jnp.int32, (GROUP * BLOCK_Q, 3 * BLOCK_Q), 1)
    kabs = n * BLOCK_Q + j - BLOCK_Q
    return (kabs >= 0) & (kabs < T)


def _per_head_rows(values):
    head = lax.broadcasted_iota(jnp.int32, (GROUP * BLOCK_Q, 1), 0) // BLOCK_Q
    col = jnp.zeros((GROUP * BLOCK_Q, 1), F32)
    for g, v in enumerate(values):
        col = jnp.where(head == g, v, col)
    return col


def _band_specs(col, nblk, sb):
    return [pl.BlockSpec((BLOCK_Q, HEAD_DIM), lambda kv, i: (jnp.maximum(sb * i - 1, 0), col(kv))),
            pl.BlockSpec((sb * BLOCK_Q, HEAD_DIM), lambda kv, i: (i, col(kv))),
            pl.BlockSpec((BLOCK_Q, HEAD_DIM), lambda kv, i: (jnp.minimum(sb * i + sb, nblk - 1), col(kv)))]


def _head_specs(base, rows):
    return [pl.BlockSpec((rows, HEAD_DIM), functools.partial(lambda kv, i, g: (i, base + kv * GROUP + g), g=g))
            for g in range(GROUP)]


def _attn_b_fwd(pb, bias, sink, o_all, q_off, n_q, n_kv, deps=(), sb=8):
    T = pb.shape[0]
    nblk = T // BLOCK_Q
    sb = min(sb, nblk)
    tq = sb * BLOCK_Q
    scale = HEAD_DIM ** -0.5

    def body(*refs):
        q_refs = refs[0:GROUP]
        k_refs, v_refs = refs[GROUP:GROUP + 3], refs[GROUP + 3:GROUP + 6]
        bias_ref, sink_ref, o_ref, lse_ref = refs[GROUP + 6:]
        kv, i = pl.program_id(0), pl.program_id(1)
        kb = jnp.concatenate([r[...] for r in k_refs], axis=0)
        vb = jnp.concatenate([r[...] for r in v_refs], axis=0)
        bias_all = bias_ref[...].reshape(GROUP * BLOCK_Q, 3 * BLOCK_Q)
        sk = _per_head_rows([sink_ref[kv * GROUP + g] for g in range(GROUP)])
        for b in range(sb):
            rows = slice(b * BLOCK_Q, (b + 1) * BLOCK_Q)
            kw, vw = kb[b * BLOCK_Q:(b + 3) * BLOCK_Q], vb[b * BLOCK_Q:(b + 3) * BLOCK_Q]
            q = jnp.concatenate([r[rows, :] for r in q_refs], axis=0)
            s = lax.dot_general(q, kw, _NT, preferred_element_type=F32) * scale + bias_all
            if b == 0 or b == sb - 1:
                s = jnp.where(_in_sequence(i * sb + b, T), s, NEG_INF)
            m = jnp.maximum(jnp.max(s, axis=-1, keepdims=True), sk)
            p = jnp.exp(s - m)
            l = jnp.sum(p, axis=-1, keepdims=True) + jnp.exp(sk - m)
            o = (lax.dot_general(p.astype(BF16), vw, _NN, preferred_element_type=F32) / l).astype(BF16)
            lse = m + jnp.log(l)
            for g in range(GROUP):
                head = slice(g * BLOCK_Q, (g + 1) * BLOCK_Q)
                o_ref[rows, g * HEAD_DIM:(g + 1) * HEAD_DIM] = o[head]
                lse_ref[g, rows, :] = lse[head]

    first_group = o_all.shape[1] // (GROUP * HEAD_DIM) - n_kv
    return _pcall(
        body, deps, into=(o_all, 0), name="attn_b_fwd", grid=(n_kv, nblk // sb),
        in_specs=[*_head_specs(q_off, tq),
                  *_band_specs(lambda kv: q_off + n_q + kv, nblk, sb),
                  *_band_specs(lambda kv: q_off + n_q + n_kv + kv, nblk, sb),
                  pl.BlockSpec((GROUP, BLOCK_Q, 3 * BLOCK_Q), lambda kv, i: (kv, 0, 0)),
                  pl.BlockSpec(memory_space=pltpu.SMEM)],
        out_specs=[pl.BlockSpec((tq, GROUP * HEAD_DIM), lambda kv, i: (i, first_group + kv)),
                   pl.BlockSpec((GROUP, tq, 1), lambda kv, i: (kv, i, 0))],
        out_shape=[jax.ShapeDtypeStruct(o_all.shape, BF16), jax.ShapeDtypeStruct((n_q, T, 1), F32)],
        compiler_params=_params(("parallel", "parallel")),
    )(*([pb] * (GROUP + 6)), bias, sink)


def _attn_b_bwd(pb, o_cat, d_o, lse, bias, sink, q_off, n_q, n_kv, o_off, deps=(), sb=8):
    T = pb.shape[0]
    nblk = T // BLOCK_Q
    sb = min(sb, nblk)
    tq = sb * BLOCK_Q
    scale = HEAD_DIM ** -0.5

    def body(*refs):
        q_refs = refs[0:GROUP]
        k_refs, v_refs = refs[GROUP:GROUP + 3], refs[GROUP + 3:GROUP + 6]
        o_refs, do_refs = refs[GROUP + 6:2 * GROUP + 6], refs[2 * GROUP + 6:3 * GROUP + 6]
        lse_ref, bias_ref, sink_ref, dq_ref, dk_ref, dv_ref, dbias_ref, dsink_ref, dkb_ref, dvb_ref = refs[3 * GROUP + 6:]
        kv, i = pl.program_id(0), pl.program_id(1)
        first = i == 0

        @pl.when(first)
        def _():
            dk_ref[...] = jnp.zeros(dk_ref.shape, F32)
            dv_ref[...] = jnp.zeros(dv_ref.shape, F32)
            dbias_ref[...] = jnp.zeros(dbias_ref.shape, F32)

        kb = jnp.concatenate([r[...] for r in k_refs], axis=0)
        vb = jnp.concatenate([r[...] for r in v_refs], axis=0)
        dkb_ref[...] = jnp.zeros(dkb_ref.shape, F32)
        dvb_ref[...] = jnp.zeros(dvb_ref.shape, F32)
        row = lax.broadcasted_iota(jnp.int32, (SUBLANES, LANES), 0)
        dsink = jnp.zeros((SUBLANES, LANES), F32)
        bias_all = bias_ref[...].reshape(GROUP * BLOCK_Q, 3 * BLOCK_Q)
        sk = _per_head_rows([sink_ref[kv * GROUP + g] for g in range(GROUP)])
        for b in range(sb):
            rows = slice(b * BLOCK_Q, (b + 1) * BLOCK_Q)
            win = slice(b * BLOCK_Q, (b + 3) * BLOCK_Q)
            kw, vw = kb[win], vb[win]
            q = jnp.concatenate([r[rows, :] for r in q_refs], axis=0)
            do = jnp.concatenate([r[rows, :] for r in do_refs], axis=0)
            o = jnp.concatenate([r[rows, :] for r in o_refs], axis=0)
            lse = jnp.concatenate([lse_ref[g, rows, :] for g in range(GROUP)], axis=0)
            delta = jnp.sum(do.astype(F32) * o.astype(F32), axis=-1, keepdims=True)
            s = lax.dot_general(q, kw, _NT, preferred_element_type=F32) * scale + bias_all
            if b == 0 or b == sb - 1:
                s = jnp.where(_in_sequence(i * sb + b, T), s, NEG_INF)
            p = jnp.exp(s - lse)
            dp = lax.dot_general(do, vw, _NT, preferred_element_type=F32)
            ds = p * (dp - delta)
            dbias_ref[...] += ds.reshape(GROUP, BLOCK_Q, 3 * BLOCK_Q)
            sunk = jnp.exp(sk - lse) * delta
            for g in range(GROUP):
                dsink = dsink + jnp.where(row == g, -jnp.sum(sunk[g * BLOCK_Q:(g + 1) * BLOCK_Q]), 0.0)
            dsb = (ds * scale).astype(BF16)
            dq = lax.dot_general(dsb, kw, _NN, preferred_element_type=F32)
            for g in range(GROUP):
                dq_ref[rows, g * HEAD_DIM:(g + 1) * HEAD_DIM] = dq[g * BLOCK_Q:(g + 1) * BLOCK_Q]
            dkb_ref[win, :] += lax.dot_general(dsb, q, _TN, preferred_element_type=F32)
            dvb_ref[win, :] += lax.dot_general(p.astype(BF16), do, _TN, preferred_element_type=F32)
        _accumulate(dsink_ref, dsink, first)

        before = pl.ds(pl.multiple_of(jnp.maximum(sb * i - 1, 0) * BLOCK_Q, BLOCK_Q), BLOCK_Q)
        own = pl.ds(pl.multiple_of(i * tq, BLOCK_Q), tq)
        after = pl.ds(pl.multiple_of(jnp.minimum(sb * i + sb, nblk - 1) * BLOCK_Q, BLOCK_Q), BLOCK_Q)
        for acc_ref, band_ref in ((dk_ref, dkb_ref), (dv_ref, dvb_ref)):
            acc_ref[before, :] += band_ref[0:BLOCK_Q, :]
            acc_ref[own, :] += band_ref[BLOCK_Q:BLOCK_Q + tq, :]
            acc_ref[after, :] += band_ref[BLOCK_Q + tq:, :]

    return _pcall(
        body, deps, name="attn_b_bwd", grid=(n_kv, nblk // sb),
        in_specs=[*_head_specs(q_off, tq),
                  *_band_specs(lambda kv: q_off + n_q + kv, nblk, sb),
                  *_band_specs(lambda kv: q_off + n_q + n_kv + kv, nblk, sb),
                  *_head_specs(o_off, tq), *_head_specs(o_off, tq),
                  pl.BlockSpec((GROUP, tq, 1), lambda kv, i: (kv, i, 0)),
                  pl.BlockSpec((GROUP, BLOCK_Q, 3 * BLOCK_Q), lambda kv, i: (kv, 0, 0)),
                  pl.BlockSpec(memory_space=pltpu.SMEM)],
        out_specs=[pl.BlockSpec((tq, GROUP * HEAD_DIM), lambda kv, i: (i, kv)),
                   pl.BlockSpec((T, HEAD_DIM), lambda kv, i: (0, kv)),
                   pl.BlockSpec((T, HEAD_DIM), lambda kv, i: (0, kv)),
                   pl.BlockSpec((GROUP, BLOCK_Q, 3 * BLOCK_Q), lambda kv, i: (kv, 0, 0)),
                   pl.BlockSpec((None, SUBLANES, LANES), lambda kv, i: (kv, 0, 0))],
        out_shape=[jax.ShapeDtypeStruct((T, n_q * HEAD_DIM), F32),
                   jax.ShapeDtypeStruct((T, n_kv * HEAD_DIM), F32),
                   jax.ShapeDtypeStruct((T, n_kv * HEAD_DIM), F32),
                   jax.ShapeDtypeStruct((n_q, BLOCK_Q, 3 * BLOCK_Q), F32),
                   jax.ShapeDtypeStruct((n_kv, SUBLANES, LANES), F32)],
        scratch_shapes=[pltpu.VMEM((tq + 2 * BLOCK_Q, HEAD_DIM), F32), pltpu.VMEM((tq + 2 * BLOCK_Q, HEAD_DIM), F32)],
        compiler_params=_params(("parallel", "arbitrary")),
    )(*([pb] * (GROUP + 6)), *([o_cat] * GROUP), *([d_o] * GROUP), lse, bias, sink)


def _table_grads(dbias, dsink_raw, idx):
    n_heads = dbias.shape[0]
    n_kv = dsink_raw.shape[0]

    def body(db_ref, ds_ref, idx_ref, dt_ref, dsk_ref):
        iv = idx_ref[...]
        row = lax.broadcasted_iota(jnp.int32, (SUBLANES, LANES), 0)
        lane = lax.broadcasted_iota(jnp.int32, (SUBLANES, LANES), 1)
        dsk = jnp.zeros((SUBLANES, LANES), F32)
        for h in range(n_heads):
            d = db_ref[h]
            acc = jnp.zeros((SUBLANES, LANES), F32)
            for b in range(N_BUCKETS):
                acc = jnp.where((row == 0) & (lane == b), jnp.sum(jnp.where(iv == b, d, 0.0)), acc)
            dt_ref[:, h * LANES:(h + 1) * LANES] = acc
            raw = ds_ref[h // GROUP]
            val = jnp.sum(jnp.where((row == h % GROUP) & (lane == 0), raw, 0.0))
            dsk = jnp.where((row == 0) & (lane == h), val, dsk)
        dsk_ref[...] = dsk

    return pl.pallas_call(
        body, name="table_grads",
        in_specs=[pl.BlockSpec(memory_space=pltpu.VMEM)] * 3,
        out_specs=[pl.BlockSpec(memory_space=pltpu.VMEM)] * 2,
        out_shape=[jax.ShapeDtypeStruct((SUBLANES, n_heads * LANES), F32),
                   jax.ShapeDtypeStruct((SUBLANES, LANES), F32)],
        compiler_params=pltpu.CompilerParams(vmem_limit_bytes=56 * 1024 * 1024),
    )(dbias, dsink_raw, idx)


def _position():
    x, y, c = lax.axis_index("x"), lax.axis_index("y"), lax.axis_index("c")
    return x, y, c


def _hbm(a):
    return pltpu.with_memory_space_constraint(a, pltpu.HBM)


def _split_start(name, bufs, sem_shapes, issue):
    nb, ns = len(bufs), len(sem_shapes)

    def body(*refs):
        buf_refs = refs[:nb]
        sems = refs[nb:nb + ns]
        token = refs[nb + ns + nb]
        issue(buf_refs, sems)
        token[...] = jnp.zeros(token.shape, F32)

    outs = pl.pallas_call(
        body, name=name,
        in_specs=[_HBM] * nb,
        out_specs=[_SEM] * ns + [_HBM] * nb + [_VMEM],
        out_shape=[pltpu.SemaphoreType.DMA(s) for s in sem_shapes] + [pltpu.HBM(b.shape, b.dtype) for b in bufs]
        + [jax.ShapeDtypeStruct((SUBLANES, LANES), F32)],
        input_output_aliases={i: ns + i for i in range(nb)},
        compiler_params=pltpu.CompilerParams(has_side_effects=_EFFECT),
    )(*[_hbm(b) for b in bufs])
    return outs[:ns], outs[ns:ns + nb], outs[-1]


def _split_wait(name, bufs, send, recv, counts, size_of, after):
    nb = len(bufs)

    def body(*refs):
        buf_refs = refs[:nb]
        send_ref, recv_ref = refs[nb], refs[nb + 1]
        x, y, c = _position()
        for w, n in enumerate(counts):
            ref = size_of(buf_refs, w)
            for k in range(n):
                s = sum(counts[:w]) + k
                cp = pltpu.make_async_remote_copy(
                    src_ref=ref, dst_ref=ref, send_sem=send_ref.at[s], recv_sem=recv_ref.at[s],
                    device_id=(x, y, c), device_id_type=MESH)
                cp.wait_send()
                cp.wait_recv()

    return pl.pallas_call(
        body, name=name,
        in_specs=[_HBM] * nb + [_SEM, _SEM, _ANY],
        out_specs=[_HBM] * nb,
        out_shape=[pltpu.HBM(b.shape, b.dtype) for b in bufs],
        input_output_aliases={i: i for i in range(nb)},
        compiler_params=pltpu.CompilerParams(has_side_effects=_EFFECT),
    )(*bufs, send, recv, after)


def _block_of(pos):
    return 4 * pos[0] + 2 * pos[1] + pos[2]


def _shard_of(ref, blk, by_cols):
    aligned = (lambda v, a: v) if isinstance(blk, int) else pl.multiple_of
    if by_cols:
        n = ref.shape[1] // N_DEV
        return ref.at[:, pl.ds(aligned(blk * n, LANES), n)]
    r = ref.shape[0] // N_DEV
    return ref.at[pl.ds(aligned(blk * r, SUBLANES), r), :]


def _place_own(name, land, shard, by_cols, tr=256):
    r, n = shard.shape
    tr = _tile(r, tr)
    mine = _block_of(_position()).astype(jnp.int32).reshape(1)

    def body(m_ref, land_ref, s_ref, o_ref):
        o_ref[...] = s_ref[...]

    if by_cols:
        out = pl.BlockSpec((tr, n), lambda i, m_ref: (i, m_ref[0]))
    else:
        out = pl.BlockSpec((tr, n), lambda i, m_ref: (m_ref[0] * (r // tr) + i, 0))
    return pl.pallas_call(
        body, name=name,
        grid_spec=pltpu.PrefetchScalarGridSpec(
            num_scalar_prefetch=1, grid=(r // tr,),
            in_specs=[_ANY, pl.BlockSpec((tr, n), lambda i, m_ref: (i, 0))], out_specs=out),
        out_shape=jax.ShapeDtypeStruct(land.shape, land.dtype),
        input_output_aliases={1: 0},
        compiler_params=_params(("parallel",)),
    )(mine, land, shard)


def _gather_start(name, shards, by_cols, groups, after=None):
    nw = len(shards)
    lands = [lax.empty((s.shape[0], s.shape[1] * N_DEV) if cols else (s.shape[0] * N_DEV, s.shape[1]), s.dtype)
             for s, cols in zip(shards, by_cols)]
    order = [] if after is None else [after]

    def issue(bufs, sems):
        x, y, c = _position()
        peers = [(x, y, 1 - c), (1 - x, y, c), (x, 1 - y, c), (1 - x, 1 - y, c)]
        for gi, grp in enumerate(groups):
            for wi, w in enumerate(grp):
                for k, peer in enumerate(peers):
                    pltpu.make_async_remote_copy(
                        src_ref=bufs[w], dst_ref=_shard_of(bufs[nw + w], _block_of((x, y, c)), by_cols[w]),
                        send_sem=sems[2 * gi].at[4 * wi + k], recv_sem=sems[2 * gi + 1].at[4 * wi + k],
                        device_id=peer, device_id_type=MESH).start()

    sem_shapes = [(4 * len(g),) for g in groups for _ in range(2)]
    sems, thru, token = _split_start(name, list(shards) + lands + order, sem_shapes, issue)
    return sems, thru[:nw], thru[nw:2 * nw], token


def _gather_forward(name, lands, by_cols):
    nw = len(lands)

    def issue(land, sems):
        x, y, c = _position()
        for w in range(nw):
            for k, chip in enumerate([(1 - x, y), (x, 1 - y), (1 - x, 1 - y)]):
                blk = _shard_of(land[w], _block_of((*chip, c)), by_cols[w])
                pltpu.make_async_remote_copy(
                    src_ref=blk, dst_ref=blk, send_sem=sems[0].at[3 * w + k], recv_sem=sems[1].at[3 * w + k],
                    device_id=(x, y, 1 - c), device_id_type=MESH).start()

    return _split_start(name, lands, [(3 * nw,), (3 * nw,)], issue)


def _first_block(bufs, w, offset=0):
    return bufs[offset + w].at[0]


_PEER_FLIPS = ((0, 0, 1), (1, 0, 0), (1, 0, 1), (0, 1, 0), (0, 1, 1), (1, 1, 0), (1, 1, 1))


def _scatter_start(name, grads, by_cols):
    nw = len(grads)
    lands = []
    for g, cols in zip(grads, by_cols):
        shard = (g.shape[0], g.shape[1] // N_DEV) if cols else (g.shape[0] // N_DEV, g.shape[1])
        lands.append(lax.empty((N_DEV,) + shard, g.dtype))

    def issue(bufs, sems):
        x, y, c = _position()
        flip = lambda v, f: 1 - v if f else v
        for w in range(nw):
            for k, (fx, fy, fc) in enumerate(_PEER_FLIPS):
                peer = (flip(x, fx), flip(y, fy), flip(c, fc))
                pltpu.make_async_remote_copy(
                    src_ref=_shard_of(bufs[w], _block_of(peer), by_cols[w]), dst_ref=bufs[nw + w].at[_block_of((x, y, c))],
                    send_sem=sems[0].at[7 * w + k], recv_sem=sems[1].at[7 * w + k],
                    device_id=peer, device_id_type=MESH).start()

    return _split_start(name, list(grads) + lands, [(7 * nw,), (7 * nw,)], issue)


def _adam(w, g, m, v):
    m = ADAM_B1 * m + (1.0 - ADAM_B1) * g
    v = ADAM_B2 * v + (1.0 - ADAM_B2) * (g * g)
    m_hat = m / (1.0 - ADAM_B1 ** ADAM_STEP)
    v_hat = v / (1.0 - ADAM_B2 ** ADAM_STEP)
    delta = -ADAM_LR * (m_hat / (jnp.sqrt(v_hat) + ADAM_EPS) + ADAM_WD * w)
    return delta, m, v


def _sum_adam(name, landed, grad, by_cols, w, m, v, tr=256):
    R, C = w.shape
    tr = _tile(R, tr)
    mine = _block_of(_position()).astype(jnp.int32).reshape(1)

    def body(me_ref, l_ref, own_ref, w_ref, m_ref, v_ref, g_ref, d_ref, nm_ref, nv_ref):
        own = own_ref[...].astype(F32)
        g = None
        for d in range(N_DEV):
            part = jnp.where(me_ref[0] == d, own, l_ref[d].astype(F32))
            g = part if g is None else g + part
        g_ref[...] = g
        d_ref[...], nm_ref[...], nv_ref[...] = _adam(w_ref[...], g, m_ref[...], v_ref[...])

    tile = pl.BlockSpec((tr, C), lambda i, me_ref: (i, 0))
    if by_cols:
        own = pl.BlockSpec((tr, C), lambda i, me_ref: (i, me_ref[0]))
    else:
        own = pl.BlockSpec((tr, C), lambda i, me_ref: (me_ref[0] * (R // tr) + i, 0))
    return pl.pallas_call(
        body, name=name,
        grid_spec=pltpu.PrefetchScalarGridSpec(
            num_scalar_prefetch=1, grid=(R // tr,),
            in_specs=[pl.BlockSpec((N_DEV, tr, C), lambda i, me_ref: (0, i, 0)), own, tile, tile, tile],
            out_specs=[tile] * 4),
        out_shape=[jax.ShapeDtypeStruct((R, C), F32)] * 4,
        compiler_params=_params(("parallel",)),
    )(mine, landed, grad, w, m, v)


def _small_all_reduce(parts, deps=()):
    W = parts.shape[1]

    def body(p_ref, o_ref, slots, send_sems, recv_sems):
        x, y, c = _position()
        me = 4 * x + 2 * y + c
        slots[me] = jnp.sum(p_ref[...], axis=0, keepdims=True)
        peers = [(x, y, 1 - c), (1 - x, y, c), (1 - x, y, 1 - c), (x, 1 - y, c), (x, 1 - y, 1 - c),
                 (1 - x, 1 - y, c), (1 - x, 1 - y, 1 - c)]
        copies = []
        for k, peer in enumerate(peers):
            cp = pltpu.make_async_remote_copy(
                src_ref=slots.at[me], dst_ref=slots.at[me], send_sem=send_sems.at[k], recv_sem=recv_sems.at[k],
                device_id=peer, device_id_type=MESH)
            cp.start()
            copies.append(cp)
        for cp in copies:
            cp.wait()
        total = slots[0]
        for d in range(1, N_DEV):
            total = total + slots[d]
        o_ref[...] = total

    return _pcall(
        body, deps, name="small_all_reduce",
        in_specs=[pl.BlockSpec(memory_space=pltpu.VMEM)], out_specs=pl.BlockSpec(memory_space=pltpu.VMEM),
        out_shape=jax.ShapeDtypeStruct((1, W), F32),
        scratch_shapes=[pltpu.VMEM((N_DEV, 1, W), F32), pltpu.SemaphoreType.DMA((7,)), pltpu.SemaphoreType.DMA((7,))],
    )(parts)


def _adam_small(w, g, m, v):
    def body(w_ref, g_ref, m_ref, v_ref, d_ref, nm_ref, nv_ref):
        d_ref[...], nm_ref[...], nv_ref[...] = _adam(w_ref[...], g_ref[...], m_ref[...], v_ref[...])

    return pl.pallas_call(
        body, name="adam_small",
        in_specs=[pl.BlockSpec(memory_space=pltpu.VMEM)] * 4, out_specs=[pl.BlockSpec(memory_space=pltpu.VMEM)] * 3,
        out_shape=[jax.ShapeDtypeStruct(w.shape, F32)] * 3,
    )(w, g, m, v)


_GATHER_GROUPS = (("w_in",), ("w_out", "w_up", "ple_w"), ("w_down", "w_gate"))
_COL_SHARDED = ("w_in", "w_up", "ple_w")


class _MeshComm:
    def __init__(self, w, mom, var):
        self.w, self.mom, self.var = w, mom, var
        self.out = {}
        self._scatters = {}

    def gather_begin(self):
        self._groups = {}
        token = None
        for tag, first, group_list in (("gather_start0", 0, _GATHER_GROUPS[:1]), ("gather_start1", 1, _GATHER_GROUPS[1:])):
            names = [n for g in group_list for n in g]
            idx = {n: i for i, n in enumerate(names)}
            by_cols = [n in _COL_SHARDED for n in names]
            sems, src, lands, token = _gather_start(tag, [self.w[n].astype(BF16) for n in names], by_cols,
                                                    [[idx[n] for n in g] for g in group_list], token)
            lands = [_place_own("place_" + n, land, s, cols) for n, land, s, cols in zip(names, lands, src, by_cols)]
            for k, g in enumerate(group_list):
                self._groups[first + k] = (sems[2 * k], sems[2 * k + 1], [src[idx[n]] for n in g],
                                           [lands[idx[n]] for n in g])
        return token

    @staticmethod
    def _shard_size(names, offset):
        return lambda bufs, w: _shard_of(bufs[offset + w], 0, names[w] in _COL_SHARDED)

    def gather_arrive(self, gi, after):
        names = _GATHER_GROUPS[gi]
        send, recv, src, lands = self._groups[gi]
        out = _split_wait("gather_arrive%d" % gi, src + lands, send, recv, [4] * len(names),
                          self._shard_size(names, len(names)), after)
        self._arrived = out[len(names):]

    def gather_forward(self, gi):
        by_cols = [n in _COL_SHARDED for n in _GATHER_GROUPS[gi]]
        self._fsems, self._fthru, token = _gather_forward("gather_forward%d" % gi, self._arrived, by_cols)
        return token

    def gather_finish(self, gi, after):
        names = _GATHER_GROUPS[gi]
        out = _split_wait("gather_finish%d" % gi, self._fthru, self._fsems[0], self._fsems[1], [3] * len(names),
                          self._shard_size(names, 0), after)
        return dict(zip(names, out))

    def reduce_begin(self, key, grads):
        names = list(grads)
        sems, thru, token = _scatter_start("scatter_start_" + key, [grads[n] for n in names],
                                           [n in _COL_SHARDED for n in names])
        self._scatters[key] = (names, sems, thru)
        return token

    def reduce_finish(self, key, after):
        names, sems, thru = self._scatters[key]
        nw = len(names)
        out = _split_wait("scatter_wait_" + key, thru, sems[0], sems[1], [N_DEV - 1] * nw,
                          functools.partial(_first_block, offset=nw), after)
        for i, n in enumerate(names):
            self.out[n] = _sum_adam("adam_" + n, out[nw + i], out[i], n in _COL_SHARDED, self.w[n], self.mom[n],
                                    self.var[n])


def _step(x, p, target, gains, comm):
    T, D = x.shape
    n_q = D // (2 * HEAD_DIM)
    n_kv = n_q // GROUP
    cos, sin = _rope_tables(T)
    idx = _bucket_index()

    t = comm.gather_begin()
    u = _rms_fwd("norm_attn", x, gains["attn_norm_g"], deps=(t,))
    comm.gather_arrive(0, u)
    t = comm.gather_forward(0)
    bias = _bias_build(idx, gains["rel_bias_table"].reshape(-1), n_q, deps=(t,))
    full = comm.gather_finish(0, bias)
    proj_a, pb = _in_proj(u, full["w_in"], cos, sin, gains["q_norm_g"], gains["k_norm_g"], n_q + n_kv)
    o_a, lse_a = _attn_a_fwd(pb, n_q, n_kv, 2 * n_q)
    comm.gather_arrive(1, lse_a)
    t = comm.gather_forward(1)
    sink = gains["sink_logits"].reshape(-1)
    b_off = n_q + 2 * n_kv
    o_cat, lse_b = _attn_b_fwd(pb, bias, sink, o_a, b_off, n_q, n_kv, deps=(t,))
    full.update(comm.gather_finish(1, lse_b))
    h1, m_in = _mm_nn_rms("out_proj", o_cat, full["w_out"], x, gains["mlp_norm_g"])

    def up_epilogue(acc, extra, outs):
        outs[0][...] = acc.astype(BF16)
        r = jnp.maximum(acc, 0.0)
        outs[1][...] = (r * r).astype(BF16)

    a_act, f_act = _mm_nn("up_proj", m_in, full["w_up"], epilogue=up_epilogue, out_dtypes=[BF16, BF16], tn=2048)
    comm.gather_arrive(2, f_act)
    t = comm.gather_forward(2)
    p_b = p.astype(BF16)
    pe = _mm_nn("ple_proj", p_b, full["ple_w"], deps=(t,))
    full.update(comm.gather_finish(2, pe))
    h2 = _mm_nn("down_proj", f_act, full["w_down"], epilogue=_store_add, extras=(h1,), tn=256)
    gn = _rms_fwd("norm_gate", h2, gains["gate_norm_g"])

    dh3, dz, dpe, dg_final, dg_ple, loss_part = _gate_tail(gn, full["w_gate"], h2, pe, target, gains["ple_norm_g"],
                                                           gains["final_norm_g"])
    gw_gate = _mm_tn("grad_w_gate", gn, dz)
    gw_ple = _mm_tn("grad_ple_w", p_b, dpe)
    dh2, dh2_b, dg_gate = _mm_nt_rms_bwd("d_gate_in", dz, full["w_gate"], h2, gains["gate_norm_g"], dh3)
    gw_down = _mm_tn("grad_w_down", f_act, dh2_b)
    t = comm.reduce_begin("b", dict(w_gate=gw_gate, ple_w=gw_ple, w_down=gw_down))

    def act_bwd(acc, extra, outs):
        outs[0][...] = (acc * (2.0 * jnp.maximum(extra[0][...].astype(F32), 0.0))).astype(BF16)

    da = _mm_nt("d_act", dh2_b, full["w_down"], out_dtype=BF16, epilogue=act_bwd, extras=(a_act,), tn=2048, deps=(t,))
    gw_up = _mm_tn("grad_w_up", m_in, da)
    dm = _mm_nt("d_mlp_in", da, full["w_up"], tn=256)
    dh1, dh1_b, dg_mlp = _rms_bwd("norm_mlp_bwd", dm, h1, gains["mlp_norm_g"], dh2)
    gw_out = _mm_tn("grad_w_out", o_cat, dh1_b)
    t = comm.reduce_begin("d", dict(w_up=gw_up, w_out=gw_out))
    d_o = _mm_nt("d_attn_out", dh1_b, full["w_out"], out_dtype=BF16, deps=(t,))
    dqa, dka_t, dva_t = _attn_a_bwd(pb, o_cat, d_o, lse_a, n_q, n_kv)
    dka, dva = dka_t.T, dva_t.T
    dqb, dkb, dvb, dbias, dsink_raw = _attn_b_bwd(pb, o_cat, d_o, lse_b, bias, sink, b_off, n_q, n_kv, n_q)
    dtable, dsink = _table_grads(dbias, dsink_raw, idx)
    dproj, dg_q, dg_k = _dproj(proj_a, dqa, dka, dva, dqb, dkb, dvb, cos, sin, gains["q_norm_g"], gains["k_norm_g"])
    gw_in = _mm_tn("grad_w_in", u, dproj)
    t = comm.reduce_begin("e", dict(w_in=gw_in))
    dx, dg_attn = _mm_nt_rms_bwd("d_attn_in", dproj, full["w_in"], x, gains["attn_norm_g"], dh1, with_bf16=False,
                                 deps=(t,))
    for key in "bd":
        comm.reduce_finish(key, dx)

    parts = jnp.concatenate([dg_attn, dg_mlp, dg_ple, dg_gate, dg_final, dg_q, dg_k, dtable, dsink, loss_part], axis=1)
    return dx, parts


_SHARDED = ("w_in", "w_out", "w_up", "w_down", "ple_w", "w_gate")
_VECTORS = ("attn_norm_g", "mlp_norm_g", "ple_norm_g", "gate_norm_g", "final_norm_g")
_ORDER = ("attn_norm_g", "w_in", "q_norm_g", "k_norm_g", "sink_logits", "w_out", "mlp_norm_g", "w_up", "w_down",
          "ple_w", "ple_norm_g", "gate_norm_g", "w_gate", "rel_bias_table", "final_norm_g")


def _pack_small(vals, n_heads):
    lane_pad = lambda v: jnp.pad(v, ((0, 0), (0, LANES - v.shape[1])))
    table = lane_pad(vals["rel_bias_table"].T).reshape(1, n_heads * LANES)
    return jnp.concatenate(
        [vals[n].reshape(1, -1) for n in _VECTORS] + [vals["q_norm_g"], vals["k_norm_g"], table,
                                                      lane_pad(vals["sink_logits"]), jnp.zeros((1, LANES), F32)], axis=1)


def _unpack_small(row, like, n_heads):
    out, off = {}, 0
    for n in _VECTORS:
        out[n] = row[:, off:off + like[n].size].reshape(like[n].shape)
        off += like[n].size
    for n in ("q_norm_g", "k_norm_g"):
        out[n] = row[:, off:off + LANES]
        off += LANES
    out["rel_bias_table"] = row[:, off:off + n_heads * LANES].reshape(n_heads, LANES)[:, :N_BUCKETS].T
    off += n_heads * LANES
    out["sink_logits"] = row[:, off:off + n_heads]
    off += LANES
    return out, row[0, off]


def kernel(x, p, attn_norm_g, w_in, q_norm_g, k_norm_g, sink_logits, w_out, mlp_norm_g, w_up, w_down, ple_w, ple_norm_g, gate_norm_g, w_gate, rel_bias_table, final_norm_g, loss_target, m_attn_norm_g, m_w_in, m_q_norm_g, m_k_norm_g, m_sink_logits, m_w_out, m_mlp_norm_g, m_w_up, m_w_down, m_ple_w, m_ple_norm_g, m_gate_norm_g, m_w_gate, m_rel_bias_table, m_final_norm_g, v_attn_norm_g, v_w_in, v_q_norm_g, v_k_norm_g, v_sink_logits, v_w_out, v_mlp_norm_g, v_w_up, v_w_down, v_ple_w, v_ple_norm_g, v_gate_norm_g, v_w_gate, v_rel_bias_table, v_final_norm_g):
    w = dict(attn_norm_g=attn_norm_g, w_in=w_in[0], q_norm_g=q_norm_g, k_norm_g=k_norm_g, sink_logits=sink_logits,
             w_out=w_out[0], mlp_norm_g=mlp_norm_g, w_up=w_up[0], w_down=w_down[0], ple_w=ple_w[0],
             ple_norm_g=ple_norm_g, gate_norm_g=gate_norm_g, w_gate=w_gate[0], rel_bias_table=rel_bias_table,
             final_norm_g=final_norm_g)
    mom = dict(attn_norm_g=m_attn_norm_g, w_in=m_w_in[0], q_norm_g=m_q_norm_g, k_norm_g=m_k_norm_g,
               sink_logits=m_sink_logits, w_out=m_w_out[0], mlp_norm_g=m_mlp_norm_g, w_up=m_w_up[0],
               w_down=m_w_down[0], ple_w=m_ple_w[0], ple_norm_g=m_ple_norm_g, gate_norm_g=m_gate_norm_g,
               w_gate=m_w_gate[0], rel_bias_table=m_rel_bias_table, final_norm_g=m_final_norm_g)
    var = dict(attn_norm_g=v_attn_norm_g, w_in=v_w_in[0], q_norm_g=v_q_norm_g, k_norm_g=v_k_norm_g,
               sink_logits=v_sink_logits, w_out=v_w_out[0], mlp_norm_g=v_mlp_norm_g, w_up=v_w_up[0],
               w_down=v_w_down[0], ple_w=v_ple_w[0], ple_norm_g=v_ple_norm_g, gate_norm_g=v_gate_norm_g,
               w_gate=v_w_gate[0], rel_bias_table=v_rel_bias_table, final_norm_g=v_final_norm_g)
    D = x.shape[-1]
    n_heads = D // (2 * HEAD_DIM)

    gains = {n: w[n] for n in w if n not in _SHARDED}
    gains["final_norm_g"] = final_norm_g.reshape(1, -1)

    comm = _MeshComm(w, mom, var)
    dx, parts = _step(x[0], p[0, 0], loss_target[0], gains, comm)

    small_g = _small_all_reduce(parts, deps=[comm.out[n][0] for n in comm.out])
    comm.reduce_finish("e", small_g)

    g_out, d_out, m_out, v_out = {}, {}, {}, {}
    for n in _SHARDED:
        g, d, nm, nv = comm.out[n]
        g_out[n], d_out[n], m_out[n], v_out[n] = g[None], d[None], nm[None], nv[None]

    small = {n: v for n, v in w.items() if n not in _SHARDED}
    pack = lambda vals: _pack_small({n: vals[n] for n in small}, n_heads)
    sd, sm, sv = _adam_small(pack(w), small_g, pack(mom), pack(var))
    sg, loss = _unpack_small(small_g, small, n_heads)
    g_out.update(sg)
    for dst, row in ((d_out, sd), (m_out, sm), (v_out, sv)):
        dst.update(_unpack_small(row, small, n_heads)[0])

    return (loss, dx[None], *[g_out[n] for n in _ORDER], *[d_out[n] for n in _ORDER],
            *[m_out[n] for n in _ORDER], *[v_out[n] for n in _ORDER])
```

```python
import functools
import math

import numpy as np
import jax
import jax.numpy as jnp
from jax import lax
from jax.experimental import pallas as pl
from jax.experimental.pallas import tpu as pltpu

F32 = jnp.float32
BF16 = jnp.bfloat16

N_DEV = 8
N_CHIP = 4
HEAD_DIM = 128
GROUP = 4
GRID_W = 64
WINDOW = 128
BLOCK_Q = 128
N_BUCKETS = 32
MAX_DISTANCE = 128
ROPE_THETA = 10000.0
EPS = 1e-6
NEG_INF = -1e30
ADAM_LR = 0.001
ADAM_B1 = 0.9
ADAM_B2 = 0.999
ADAM_EPS = 1e-08
ADAM_WD = 0.01
ADAM_STEP = 10
LOG2E = math.log2(math.e)
LANES = 128
SUBLANES = 8
MESH = pl.DeviceIdType.MESH

_NT = (((1,), (1,)), ((), ()))
_NN = (((1,), (0,)), ((), ()))
_TN = (((0,), (0,)), ((), ()))


def _tile(dim, pref):
    return pref if dim % pref == 0 else dim


def _params(sem):
    return pltpu.CompilerParams(dimension_semantics=sem, vmem_limit_bytes=56 * 1024 * 1024)


_HBM = pl.BlockSpec(memory_space=pltpu.HBM)
_SEM = pl.BlockSpec(memory_space=pltpu.SEMAPHORE)
_ANY = pl.BlockSpec(memory_space=pl.ANY)
_VMEM = pl.BlockSpec(memory_space=pltpu.VMEM)
_EFFECT = pltpu.SideEffectType.DATAFLOW_SIDE_EFFECTING


def _pcall(body, deps=(), *, in_specs, into=None, **kw):
    deps = [d for d in deps if d is not None]
    nd = len(deps)
    if into is not None:
        deps = [into[0]] + deps
        nd += 1
        kw["input_output_aliases"] = {0: into[1]}

    def wrapped(*refs):
        body(*refs[nd:])

    call = pl.pallas_call(wrapped, in_specs=[_ANY] * nd + list(in_specs), **kw)
    return lambda *args: call(*deps, *args)


def _mm(name, a, b, dims, grid, a_spec, b_spec, out_shape, out_specs, acc_shape, epilogue,
        extras=(), extra_specs=(), deps=(), semantics=("parallel", "parallel", "arbitrary")):
    nk = grid[2]
    n_extra = len(extras)

    def body(*refs):
        a_ref, b_ref = refs[0], refs[1]
        extra = refs[2:2 + n_extra]
        outs = refs[2 + n_extra:-1]
        acc = refs[-1]
        part = lax.dot_general(a_ref[...], b_ref[...], dims, preferred_element_type=F32)
        if nk == 1:
            epilogue(part, extra, outs)
        else:
            k = pl.program_id(2)

            @pl.when(k == 0)
            def _():
                acc[...] = part

            @pl.when(k > 0)
            def _():
                acc[...] += part

            @pl.when(k == nk - 1)
            def _():
                epilogue(acc[...], extra, outs)

    return _pcall(
        body, deps, name=name, grid=grid,
        in_specs=[a_spec, b_spec, *extra_specs],
        out_specs=out_specs, out_shape=out_shape,
        scratch_shapes=[pltpu.VMEM(acc_shape if nk > 1 else (SUBLANES, LANES), F32)],
        compiler_params=_params(semantics),
    )(a, b, *extras)


def _store(dtype):
    def ep(acc, extra, outs):
        outs[0][...] = acc.astype(dtype)
    return ep


def _store_add(acc, extra, outs):
    outs[0][...] = acc + extra[0][...]


def _mm_nn(name, a, b, out_dtype=F32, epilogue=None, extras=(), n_out=1, out_dtypes=None, tm=1024, tn=1024, tk=None,
           deps=()):
    M, K = a.shape
    N = b.shape[1]
    tm, tn, tk = _tile(M, tm), _tile(N, tn), _tile(K, tk or K)
    b_spec = pl.BlockSpec((tk, tn), lambda i, j, k: (k, j))
    grid = (M // tm, N // tn, K // tk)
    o_spec = pl.BlockSpec((tm, tn), lambda i, j, k: (i, j))
    out_dtypes = out_dtypes or [out_dtype] * n_out
    out_shape = [jax.ShapeDtypeStruct((M, N), d) for d in out_dtypes]
    res = _mm(name, a, b, _NN, grid, pl.BlockSpec((tm, tk), lambda i, j, k: (i, k)), b_spec,
              out_shape, [o_spec] * len(out_dtypes), (tm, tn), epilogue or _store(out_dtype),
              extras, [o_spec] * len(extras), deps)
    return res if len(out_dtypes) > 1 else res[0]


def _mm_nt(name, a, b, out_dtype=F32, epilogue=None, extras=(), tm=1024, tn=1024, tk=None, deps=()):
    M, C = a.shape
    N = b.shape[0]
    tm, tn, tk = _tile(M, tm), _tile(N, tn), _tile(C, tk or C)
    b_spec = pl.BlockSpec((tn, tk), lambda i, j, k: (j, k))
    grid = (M // tm, N // tn, C // tk)
    o_spec = pl.BlockSpec((tm, tn), lambda i, j, k: (i, j))
    return _mm(name, a, b, _NT, grid, pl.BlockSpec((tm, tk), lambda i, j, k: (i, k)), b_spec,
               [jax.ShapeDtypeStruct((M, N), out_dtype)], [o_spec], (tm, tn), epilogue or _store(out_dtype),
               extras, [o_spec] * len(extras), deps)[0]


def _mm_tn(name, a, b, out_dtype=BF16, tm=1024, tn=512, tk=None, deps=()):
    T, M = a.shape
    N = b.shape[1]
    tm, tn, tk = _tile(M, tm), _tile(N, tn), _tile(T, tk or T)
    out_shape = jax.ShapeDtypeStruct((M, N), out_dtype)
    o_spec = pl.BlockSpec((tm, tn), lambda i, j, k: (i, j))
    grid = (M // tm, N // tn, T // tk)
    return _mm(name, a, b, _TN, grid, pl.BlockSpec((tk, tm), lambda i, j, k: (k, i)),
               pl.BlockSpec((tk, tn), lambda i, j, k: (k, j)), [out_shape], [o_spec], (tm, tn), _store(out_dtype),
               deps=deps)[0]


def _mean_last(v):
    return jnp.mean(v, axis=-1, keepdims=True)


def _rows_to_sublanes(v):
    r, c = v.shape
    return jnp.sum(v.reshape(r // SUBLANES, SUBLANES, c), axis=0)


def _accumulate(ref, val, first):
    @pl.when(first)
    def _():
        ref[...] = val

    @pl.when(jnp.logical_not(first))
    def _():
        ref[...] += val


def _rms_fwd(name, x, g, tr=256, deps=()):
    T, D = x.shape
    tr = _tile(T, tr)

    def body(x_ref, g_ref, o_ref):
        xv = x_ref[...]
        r = lax.rsqrt(_mean_last(xv * xv) + EPS)
        o_ref[...] = (xv * r * g_ref[...]).astype(BF16)

    row = pl.BlockSpec((tr, D), lambda i: (i, 0))
    return _pcall(
        body, deps, name=name, grid=(T // tr,),
        in_specs=[row, pl.BlockSpec((1, D), lambda i: (0, 0))],
        out_specs=row, out_shape=jax.ShapeDtypeStruct((T, D), BF16),
        compiler_params=_params(("parallel",)),
    )(x, g)


def _rms_bwd(name, dyn, x, g, dres, tr=256, deps=()):
    T, D = x.shape
    tr = _tile(T, tr)

    def body(dy_ref, x_ref, g_ref, dr_ref, dx_ref, dxb_ref, dg_ref):
        xv = x_ref[...]
        r = lax.rsqrt(_mean_last(xv * xv) + EPS)
        xn = xv * r
        dy = dy_ref[...]
        dxn = dy * g_ref[...]
        dx = dr_ref[...] + r * (dxn - xn * _mean_last(dxn * xn))
        dx_ref[...] = dx
        dxb_ref[...] = dx.astype(BF16)
        _accumulate(dg_ref, _rows_to_sublanes(dy * xn), pl.program_id(0) == 0)

    row = pl.BlockSpec((tr, D), lambda i: (i, 0))
    return _pcall(
        body, deps, name=name, grid=(T // tr,),
        in_specs=[row, row, pl.BlockSpec((1, D), lambda i: (0, 0)), row],
        out_specs=[row, row, pl.BlockSpec((SUBLANES, D), lambda i: (0, 0))],
        out_shape=[jax.ShapeDtypeStruct((T, D), F32), jax.ShapeDtypeStruct((T, D), BF16),
                   jax.ShapeDtypeStruct((SUBLANES, D), F32)],
        compiler_params=_params(("arbitrary",)),
    )(dyn, x, g, dres)


def _mm_nn_rms(name, a, b, res, g, tm=512, deps=()):
    M, K = a.shape
    N = b.shape[1]
    tm = _tile(M, tm)

    def epilogue(acc, extra, outs):
        h = acc + extra[0][...]
        outs[0][...] = h
        outs[1][...] = (h * lax.rsqrt(_mean_last(h * h) + EPS) * extra[1][...]).astype(BF16)

    row = pl.BlockSpec((tm, N), lambda i, j, k: (i, 0))
    return _mm(name, a, b, _NN, (M // tm, 1, 1), pl.BlockSpec((tm, K), lambda i, j, k: (i, 0)),
               pl.BlockSpec((K, N), lambda i, j, k: (0, 0)),
               [jax.ShapeDtypeStruct((M, N), F32), jax.ShapeDtypeStruct((M, N), BF16)], [row, row], (tm, N), epilogue,
               (res, g), [row, pl.BlockSpec((1, N), lambda i, j, k: (0, 0))], deps)


def _mm_nt_rms_bwd(name, a, b, x, g, dres, with_bf16=True, tm=256, deps=()):
    M, C = a.shape
    N = b.shape[0]
    tm = _tile(M, tm)

    def epilogue(dy, extra, outs):
        x_ref, dr_ref, g_ref = extra
        xv = x_ref[...]
        r = lax.rsqrt(_mean_last(xv * xv) + EPS)
        xn = xv * r
        dxn = dy * g_ref[...]
        dx = dr_ref[...] + r * (dxn - xn * _mean_last(dxn * xn))
        outs[0][...] = dx
        if with_bf16:
            outs[1][...] = dx.astype(BF16)
        _accumulate(outs[-1], _rows_to_sublanes(dy * xn), pl.program_id(0) == 0)

    row = pl.BlockSpec((tm, N), lambda i, j, k: (i, 0))
    copies = [jax.ShapeDtypeStruct((M, N), F32)] + ([jax.ShapeDtypeStruct((M, N), BF16)] if with_bf16 else [])
    return _mm(name, a, b, _NT, (M // tm, 1, 1), pl.BlockSpec((tm, C), lambda i, j, k: (i, 0)),
               pl.BlockSpec((N, C), lambda i, j, k: (0, 0)),
               copies + [jax.ShapeDtypeStruct((SUBLANES, N), F32)],
               [row] * len(copies) + [pl.BlockSpec((SUBLANES, N), lambda i, j, k: (0, 0))], (tm, N), epilogue,
               (x, dres, g), [row, row, pl.BlockSpec((1, N), lambda i, j, k: (0, 0))], deps,
               semantics=("arbitrary", "arbitrary", "arbitrary"))


def _gate_tail(gn, w_gate, h2, pe, target, g_ple, g_final, tm=256):
    T, D = h2.shape
    tm = _tile(T, tm)

    def epilogue(z, extra, outs):
        h2_ref, pe_ref, t_ref, gp_ref, gf_ref = extra
        dh3_ref, dz_ref, dpe_ref, dgf_ref, dgp_ref, loss_ref = outs
        first = pl.program_id(0) == 0
        pev = pe_ref[...]
        r3 = lax.rsqrt(_mean_last(pev * pev) + EPS)
        en = pev * r3
        e = en * gp_ref[...]
        gate = 1.0 / (1.0 + jnp.exp(-z))
        h3 = h2_ref[...] + gate * e
        r5 = lax.rsqrt(_mean_last(h3 * h3) + EPS)
        hn = h3 * r5
        diff = hn * gf_ref[...] - t_ref[...]
        loss_rows = 0.5 * _mean_last(diff * diff)
        row0 = lax.broadcasted_iota(jnp.int32, (SUBLANES, LANES), 0) == 0
        _accumulate(loss_ref, jnp.where(row0, jnp.sum(loss_rows), 0.0), first)
        dy = diff * (1.0 / D)
        _accumulate(dgf_ref, _rows_to_sublanes(dy * hn), first)
        dhn = dy * gf_ref[...]
        dh3 = r5 * (dhn - hn * _mean_last(dhn * hn))
        dh3_ref[...] = dh3
        dgate = dh3 * e
        de = dh3 * gate
        dz_ref[...] = (dgate * gate * (1.0 - gate)).astype(BF16)
        _accumulate(dgp_ref, _rows_to_sublanes(de * en), first)
        den = de * gp_ref[...]
        dpe_ref[...] = (r3 * (den - en * _mean_last(den * en))).astype(BF16)

    row = pl.BlockSpec((tm, D), lambda i, j, k: (i, 0))
    vec = pl.BlockSpec((1, D), lambda i, j, k: (0, 0))
    part = pl.BlockSpec((SUBLANES, D), lambda i, j, k: (0, 0))
    return _mm("gate_tail", gn, w_gate, _NN, (T // tm, 1, 1), row, pl.BlockSpec(w_gate.shape, lambda i, j, k: (0, 0)),
               [jax.ShapeDtypeStruct((T, D), F32), jax.ShapeDtypeStruct((T, D), BF16),
                jax.ShapeDtypeStruct((T, D), BF16), jax.ShapeDtypeStruct((SUBLANES, D), F32),
                jax.ShapeDtypeStruct((SUBLANES, D), F32), jax.ShapeDtypeStruct((SUBLANES, LANES), F32)],
               [row, row, row, part, part, pl.BlockSpec((SUBLANES, LANES), lambda i, j, k: (0, 0))], (tm, D), epilogue,
               (h2, pe, target, g_ple, g_final), [row, row, row, vec, vec],
               semantics=("arbitrary", "arbitrary", "arbitrary"))


def _rope_tables(T):
    pos = np.arange(T)
    half = HEAD_DIM // 2
    inv = (ROPE_THETA ** (-np.arange(0, half, 2, dtype=np.float32) / half)).astype(np.float32)
    ang_r = (pos // GRID_W).astype(np.float32)[:, None] * inv
    ang_c = (pos % GRID_W).astype(np.float32)[:, None] * inv
    cos = np.concatenate([np.cos(ang_r), np.cos(ang_r), np.cos(ang_c), np.cos(ang_c)], axis=-1)
    sin = np.concatenate([-np.sin(ang_r), np.sin(ang_r), -np.sin(ang_c), np.sin(ang_c)], axis=-1)
    return jnp.asarray(cos, F32), jnp.asarray(sin, F32)


def _swap32(x):
    lane = lax.broadcasted_iota(jnp.int32, x.shape, 1)
    return jnp.where((lane % 64) < 32, pltpu.roll(x, 96, 1), pltpu.roll(x, 32, 1))


def _in_proj(u, w_in, cos, sin, g_q, g_k, n_norm, tm=512):
    T, K = u.shape
    W = w_in.shape[1]
    tm = _tile(T, tm)
    n_q = n_norm * GROUP // (GROUP + 1)
    wa = n_norm * HEAD_DIM

    def epilogue(acc, extra, outs):
        c_ref, s_ref, gq_ref, gk_ref = extra
        raw_ref, o_ref = outs
        c, s = c_ref[...], s_ref[...]
        raw_ref[...] = acc[:, :wa]
        for h in range(n_norm):
            cols = slice(h * HEAD_DIM, (h + 1) * HEAD_DIM)
            xv = acc[:, cols]
            g = gq_ref[...] if h < n_q else gk_ref[...]
            xn = xv * lax.rsqrt(_mean_last(xv * xv) + EPS) * g
            o_ref[:, cols] = (xn * c + _swap32(xn) * s).astype(BF16)
        o_ref[:, wa:] = acc[:, wa:].astype(BF16)

    tab = pl.BlockSpec((tm, HEAD_DIM), lambda i, j, k: (i, 0))
    vec = pl.BlockSpec((1, HEAD_DIM), lambda i, j, k: (0, 0))
    return _mm("in_proj", u, w_in, _NN, (T // tm, 1, 1), pl.BlockSpec((tm, K), lambda i, j, k: (i, 0)),
               pl.BlockSpec((K, W), lambda i, j, k: (0, 0)),
               [jax.ShapeDtypeStruct((T, wa), F32), jax.ShapeDtypeStruct((T, W), BF16)],
               [pl.BlockSpec((tm, wa), lambda i, j, k: (i, 0)), pl.BlockSpec((tm, W), lambda i, j, k: (i, 0))],
               (tm, W), epilogue, (cos, sin, g_q, g_k), [tab, tab, vec, vec])


def _dproj(proj_a, dqa, dka, dva, dqb, dkb, dvb, cos, sin, g_q, g_k, tr=256):
    T, wa = proj_a.shape
    tr = _tile(T, tr)
    n_q = dqa.shape[1] // HEAD_DIM
    n_kv = dka.shape[1] // HEAD_DIM
    W = wa + dva.shape[1] + dqb.shape[1] + dkb.shape[1] + dvb.shape[1]

    def body(p_ref, dqa_ref, dka_ref, dva_ref, dqb_ref, dkb_ref, dvb_ref, c_ref, s_ref, gq_ref, gk_ref,
             o_ref, dgq_ref, dgk_ref):
        c, s = c_ref[...], s_ref[...]
        dgq = jnp.zeros((SUBLANES, HEAD_DIM), F32)
        dgk = jnp.zeros((SUBLANES, HEAD_DIM), F32)
        for h in range(n_q + n_kv):
            cols = slice(h * HEAD_DIM, (h + 1) * HEAD_DIM)
            xv = p_ref[:, cols]
            r = lax.rsqrt(_mean_last(xv * xv) + EPS)
            xn = xv * r
            if h < n_q:
                d = dqa_ref[:, cols]
                g = gq_ref[...]
            else:
                d = dka_ref[:, (h - n_q) * HEAD_DIM:(h - n_q + 1) * HEAD_DIM]
                g = gk_ref[...]
            dqn = d * c + _swap32(d * s)
            part = _rows_to_sublanes(dqn * xn)
            if h < n_q:
                dgq = dgq + part
            else:
                dgk = dgk + part
            dxn = dqn * g
            o_ref[:, cols] = (r * (dxn - xn * _mean_last(dxn * xn))).astype(BF16)
        off = wa
        for ref in (dva_ref, dqb_ref, dkb_ref, dvb_ref):
            w = ref.shape[1]
            o_ref[:, off:off + w] = ref[...].astype(BF16)
            off += w
        first = pl.program_id(0) == 0
        _accumulate(dgq_ref, dgq, first)
        _accumulate(dgk_ref, dgk, first)

    def row(w):
        return pl.BlockSpec((tr, w), lambda i: (i, 0))

    vec = pl.BlockSpec((1, HEAD_DIM), lambda i: (0, 0))
    part = pl.BlockSpec((SUBLANES, HEAD_DIM), lambda i: (0, 0))
    return pl.pallas_call(
        body, name="dproj", grid=(T // tr,),
        in_specs=[row(wa), row(dqa.shape[1]), row(dka.shape[1]), row(dva.shape[1]), row(dqb.shape[1]),
                  row(dkb.shape[1]), row(dvb.shape[1]), row(HEAD_DIM), row(HEAD_DIM), vec, vec],
        out_specs=[row(W), part, part],
        out_shape=[jax.ShapeDtypeStruct((T, W), BF16), jax.ShapeDtypeStruct((SUBLANES, HEAD_DIM), F32),
                   jax.ShapeDtypeStruct((SUBLANES, HEAD_DIM), F32)],
        compiler_params=_params(("arbitrary",)),
    )(proj_a, dqa, dka, dva, dqb, dkb, dvb, cos, sin, g_q, g_k)


def _attn_a_fwd(pb, n_q, n_kv, out_heads, tq=1024, tc=1024):
    T = pb.shape[0]
    tq, tc = _tile(T, tq), _tile(T, tc)
    scale = HEAD_DIM ** -0.5
    c = scale * LOG2E

    def body(q_ref, k_ref, v_ref, o_ref, lse_ref):
        q = q_ref[...]
        m = l = acc = None
        for j in range(T // tc):
            keys = slice(j * tc, (j + 1) * tc)
            s = lax.dot_general(q, k_ref[keys, :], _NT, preferred_element_type=F32)
            mj = jnp.max(s, axis=-1, keepdims=True)
            m_new = mj if j == 0 else jnp.maximum(m, mj)
            p = jnp.exp2((s - m_new) * c)
            pv = lax.dot_general(p.astype(BF16), v_ref[keys, :], _NN, preferred_element_type=F32)
            if j == 0:
                l, acc = jnp.sum(p, axis=-1, keepdims=True), pv
            else:
                alpha = jnp.exp2((m - m_new) * c)
                l = alpha * l + jnp.sum(p, axis=-1, keepdims=True)
                acc = alpha * acc + pv
            m = m_new
        o_ref[...] = (acc / l).astype(BF16)
        lse_ref[...] = m * scale + jnp.log(l)

    return pl.pallas_call(
        body, name="attn_a_fwd", grid=(n_kv, GROUP, T // tq),
        in_specs=[pl.BlockSpec((tq, HEAD_DIM), lambda kv, g, i: (i, kv * GROUP + g)),
                  pl.BlockSpec((T, HEAD_DIM), lambda kv, g, i: (0, n_q + kv)),
                  pl.BlockSpec((T, HEAD_DIM), lambda kv, g, i: (0, n_q + n_kv + kv))],
        out_specs=[pl.BlockSpec((tq, HEAD_DIM), lambda kv, g, i: (i, kv * GROUP + g)),
                   pl.BlockSpec((None, tq, 1), lambda kv, g, i: (kv * GROUP + g, i, 0))],
        out_shape=[jax.ShapeDtypeStruct((T, out_heads * HEAD_DIM), BF16), jax.ShapeDtypeStruct((n_q, T, 1), F32)],
        compiler_params=_params(("parallel", "parallel", "parallel")),
    )(pb, pb, pb)


def _attn_a_bwd(pb, o_cat, d_o, lse, n_q, n_kv, tq=1024, tc=256):
    T = pb.shape[0]
    tq, tc = _tile(T, tq), _tile(T, tc)
    scale = HEAD_DIM ** -0.5
    c = scale * LOG2E

    def body(q_ref, k_ref, v_ref, o_ref, do_ref, lse_ref, dq_ref, dkt_ref, dvt_ref):
        q, do = q_ref[...], do_ref[...]
        qt, dot = q.T, do.T
        delta = jnp.sum(do.astype(F32) * o_ref[...].astype(F32), axis=-1, keepdims=True)
        lse2 = lse_ref[...] * LOG2E

        @pl.when(jnp.logical_and(pl.program_id(1) == 0, pl.program_id(2) == 0))
        def _():
            dkt_ref[...] = jnp.zeros(dkt_ref.shape, F32)
            dvt_ref[...] = jnp.zeros(dvt_ref.shape, F32)

        dq = None
        for j in range(T // tc):
            keys = slice(j * tc, (j + 1) * tc)
            kc, vc = k_ref[keys, :], v_ref[keys, :]
            s = lax.dot_general(q, kc, _NT, preferred_element_type=F32)
            p = jnp.exp2(s * c - lse2)
            dp = lax.dot_general(do, vc, _NT, preferred_element_type=F32)
            ds = (p * (dp - delta) * scale).astype(BF16)
            dqj = lax.dot_general(ds, kc, _NN, preferred_element_type=F32)
            dq = dqj if dq is None else dq + dqj
            dvt_ref[:, keys] += lax.dot_general(dot, p.astype(BF16), _NN, preferred_element_type=F32)
            dkt_ref[:, keys] += lax.dot_general(qt, ds, _NN, preferred_element_type=F32)
        dq_ref[...] = dq

    qmap = lambda kv, g, i: (i, kv * GROUP + g)
    return pl.pallas_call(
        body, name="attn_a_bwd", grid=(n_kv, GROUP, T // tq),
        in_specs=[pl.BlockSpec((tq, HEAD_DIM), qmap),
                  pl.BlockSpec((T, HEAD_DIM), lambda kv, g, i: (0, n_q + kv)),
                  pl.BlockSpec((T, HEAD_DIM), lambda kv, g, i: (0, n_q + n_kv + kv)),
                  pl.BlockSpec((tq, HEAD_DIM), qmap),
                  pl.BlockSpec((tq, HEAD_DIM), qmap),
                  pl.BlockSpec((None, tq, 1), lambda kv, g, i: (kv * GROUP + g, i, 0))],
        out_specs=[pl.BlockSpec((tq, HEAD_DIM), qmap),
                   pl.BlockSpec((HEAD_DIM, T), lambda kv, g, i: (kv, 0)),
                   pl.BlockSpec((HEAD_DIM, T), lambda kv, g, i: (kv, 0))],
        out_shape=[jax.ShapeDtypeStruct((T, n_q * HEAD_DIM), F32),
                   jax.ShapeDtypeStruct((n_kv * HEAD_DIM, T), F32),
                   jax.ShapeDtypeStruct((n_kv * HEAD_DIM, T), F32)],
        compiler_params=_params(("parallel", "arbitrary", "arbitrary")),
    )(pb, pb, pb, o_cat, d_o, lse)


def _bucket_index():
    r = np.arange(BLOCK_Q)[:, None]
    j = np.arange(3 * BLOCK_Q)[None, :]
    rel = (j - BLOCK_Q) - r
    nb = N_BUCKETS // 2
    ret = np.where(rel > 0, nb, 0)
    n = np.abs(rel)
    max_exact = nb // 2
    nf = np.maximum(n, 1).astype(np.float32)
    large = max_exact + (np.log(nf / max_exact) / math.log(MAX_DISTANCE / max_exact) * (nb - max_exact)).astype(np.int32)
    large = np.minimum(large, nb - 1)
    return jnp.asarray(ret + np.where(n < max_exact, n, large), jnp.int32)


def _bias_build(idx, table_flat, n_heads, deps=()):
    def body(idx_ref, tab_ref, o_ref):
        h = pl.program_id(0)
        iv = idx_ref[...]
        acc = jnp.zeros(iv.shape, F32)
        for b in range(N_BUCKETS):
            acc = jnp.where(iv == b, tab_ref[b * n_heads + h], acc)
        r = lax.broadcasted_iota(jnp.int32, iv.shape, 0)
        j = lax.broadcasted_iota(jnp.int32, iv.shape, 1)
        o_ref[...] = jnp.where(jnp.abs(j - BLOCK_Q - r) <= WINDOW, acc, NEG_INF)

    return _pcall(
        body, deps, name="bias_build", grid=(n_heads,),
        in_specs=[pl.BlockSpec(idx.shape, lambda h: (0, 0)), pl.BlockSpec(memory_space=pltpu.SMEM)],
        out_specs=pl.BlockSpec((None,) + idx.shape, lambda h: (h, 0, 0)),
        out_shape=jax.ShapeDtypeStruct((n_heads,) + idx.shape, F32),
        compiler_params=_params(("parallel",)),
    )(idx, table_flat)


def _in_sequence(n, T):
    j = lax.broadcasted_iota(jnp.int32, (GROUP * BLOCK_Q, 3 * BLOCK_Q), 1)
    kabs = n * BLOCK_Q + j - BLOCK_Q
    return (kabs >= 0) & (kabs < T)


def _per_head_rows(values):
    head = lax.broadcasted_iota(jnp.int32, (GROUP * BLOCK_Q, 1), 0) // BLOCK_Q
    col = jnp.zeros((GROUP * BLOCK_Q, 1), F32)
    for g, v in enumerate(values):
        col = jnp.where(head == g, v, col)
    return col


def _band_specs(col, nblk, sb):
    return [pl.BlockSpec((BLOCK_Q, HEAD_DIM), lambda kv, i: (jnp.maximum(sb * i - 1, 0), col(kv))),
            pl.BlockSpec((sb * BLOCK_Q, HEAD_DIM), lambda kv, i: (i, col(kv))),
            pl.BlockSpec((BLOCK_Q, HEAD_DIM), lambda kv, i: (jnp.minimum(sb * i + sb, nblk - 1), col(kv)))]


def _head_specs(base, rows):
    return [pl.BlockSpec((rows, HEAD_DIM), functools.partial(lambda kv, i, g: (i, base + kv * GROUP + g), g=g))
            for g in range(GROUP)]


def _attn_b_fwd(pb, bias, sink, o_all, q_off, n_q, n_kv, deps=(), sb=16):
    T = pb.shape[0]
    nblk = T // BLOCK_Q
    sb = min(sb, nblk)
    tq = sb * BLOCK_Q
    scale = HEAD_DIM ** -0.5

    def body(*refs):
        q_refs = refs[0:GROUP]
        k_refs, v_refs = refs[GROUP:GROUP + 3], refs[GROUP + 3:GROUP + 6]
        bias_ref, sink_ref, o_ref, lse_ref = refs[GROUP + 6:]
        kv, i = pl.program_id(0), pl.program_id(1)
        kb = jnp.concatenate([r[...] for r in k_refs], axis=0)
        vb = jnp.concatenate([r[...] for r in v_refs], axis=0)
        bias_all = bias_ref[...].reshape(GROUP * BLOCK_Q, 3 * BLOCK_Q)
        sk = _per_head_rows([sink_ref[kv * GROUP + g] for g in range(GROUP)])
        for b in range(sb):
            rows = slice(b * BLOCK_Q, (b + 1) * BLOCK_Q)
            kw, vw = kb[b * BLOCK_Q:(b + 3) * BLOCK_Q], vb[b * BLOCK_Q:(b + 3) * BLOCK_Q]
            q = jnp.concatenate([r[rows, :] for r in q_refs], axis=0)
            s = lax.dot_general(q, kw, _NT, preferred_element_type=F32) * scale + bias_all
            if b == 0 or b == sb - 1:
                s = jnp.where(_in_sequence(i * sb + b, T), s, NEG_INF)
            m = jnp.maximum(jnp.max(s, axis=-1, keepdims=True), sk)
            p = jnp.exp(s - m)
            l = jnp.sum(p, axis=-1, keepdims=True) + jnp.exp(sk - m)
            o = (lax.dot_general(p.astype(BF16), vw, _NN, preferred_element_type=F32) / l).astype(BF16)
            lse = m + jnp.log(l)
            for g in range(GROUP):
                head = slice(g * BLOCK_Q, (g + 1) * BLOCK_Q)
                o_ref[rows, g * HEAD_DIM:(g + 1) * HEAD_DIM] = o[head]
                lse_ref[g, rows, :] = lse[head]

    first_group = o_all.shape[1] // (GROUP * HEAD_DIM) - n_kv
    return _pcall(
        body, deps, into=(o_all, 0), name="attn_b_fwd", grid=(n_kv, nblk // sb),
        in_specs=[*_head_specs(q_off, tq),
                  *_band_specs(lambda kv: q_off + n_q + kv, nblk, sb),
                  *_band_specs(lambda kv: q_off + n_q + n_kv + kv, nblk, sb),
                  pl.BlockSpec((GROUP, BLOCK_Q, 3 * BLOCK_Q), lambda kv, i: (kv, 0, 0)),
                  pl.BlockSpec(memory_space=pltpu.SMEM)],
        out_specs=[pl.BlockSpec((tq, GROUP * HEAD_DIM), lambda kv, i: (i, first_group + kv)),
                   pl.BlockSpec((GROUP, tq, 1), lambda kv, i: (kv, i, 0))],
        out_shape=[jax.ShapeDtypeStruct(o_all.shape, BF16), jax.ShapeDtypeStruct((n_q, T, 1), F32)],
        compiler_params=_params(("parallel", "parallel")),
    )(*([pb] * (GROUP + 6)), bias, sink)


def _attn_b_bwd(pb, o_cat, d_o, lse, bias, sink, q_off, n_q, n_kv, o_off, deps=(), sb=16):
    T = pb.shape[0]
    nblk = T // BLOCK_Q
    sb = min(sb, nblk)
    tq = sb * BLOCK_Q
    scale = HEAD_DIM ** -0.5

    def body(*refs):
        q_refs = refs[0:GROUP]
        k_refs, v_refs = refs[GROUP:GROUP + 3], refs[GROUP + 3:GROUP + 6]
        o_refs, do_refs = refs[GROUP + 6:2 * GROUP + 6], refs[2 * GROUP + 6:3 * GROUP + 6]
        lse_ref, bias_ref, sink_ref, dq_ref, dk_ref, dv_ref, dbias_ref, dsink_ref, dkb_ref, dvb_ref = refs[3 * GROUP + 6:]
        kv, i = pl.program_id(0), pl.program_id(1)
        first = i == 0

        @pl.when(first)
        def _():
            dk_ref[...] = jnp.zeros(dk_ref.shape, F32)
            dv_ref[...] = jnp.zeros(dv_ref.shape, F32)
            dbias_ref[...] = jnp.zeros(dbias_ref.shape, F32)

        kb = jnp.concatenate([r[...] for r in k_refs], axis=0)
        vb = jnp.concatenate([r[...] for r in v_refs], axis=0)
        dkb_ref[...] = jnp.zeros(dkb_ref.shape, F32)
        dvb_ref[...] = jnp.zeros(dvb_ref.shape, F32)
        row = lax.broadcasted_iota(jnp.int32, (SUBLANES, LANES), 0)
        dsink = jnp.zeros((SUBLANES, LANES), F32)
        bias_all = bias_ref[...].reshape(GROUP * BLOCK_Q, 3 * BLOCK_Q)
        sk = _per_head_rows([sink_ref[kv * GROUP + g] for g in range(GROUP)])
        for b in range(sb):
            rows = slice(b * BLOCK_Q, (b + 1) * BLOCK_Q)
            win = slice(b * BLOCK_Q, (b + 3) * BLOCK_Q)
            kw, vw = kb[win], vb[win]
            q = jnp.concatenate([r[rows, :] for r in q_refs], axis=0)
            do = jnp.concatenate([r[rows, :] for r in do_refs], axis=0)
            o = jnp.concatenate([r[rows, :] for r in o_refs], axis=0)
            lse = jnp.concatenate([lse_ref[g, rows, :] for g in range(GROUP)], axis=0)
            delta = jnp.sum(do.astype(F32) * o.astype(F32), axis=-1, keepdims=True)
            s = lax.dot_general(q, kw, _NT, preferred_element_type=F32) * scale + bias_all
            if b == 0 or b == sb - 1:
                s = jnp.where(_in_sequence(i * sb + b, T), s, NEG_INF)
            p = jnp.exp(s - lse)
            dp = lax.dot_general(do, vw, _NT, preferred_element_type=F32)
            ds = p * (dp - delta)
            dbias_ref[...] += ds.reshape(GROUP, BLOCK_Q, 3 * BLOCK_Q)
            sunk = jnp.exp(sk - lse) * delta
            for g in range(GROUP):
                dsink = dsink + jnp.where(row == g, -jnp.sum(sunk[g * BLOCK_Q:(g + 1) * BLOCK_Q]), 0.0)
            dsb = (ds * scale).astype(BF16)
            dq = lax.dot_general(dsb, kw, _NN, preferred_element_type=F32).astype(BF16)
            for g in range(GROUP):
                dq_ref[rows, g * HEAD_DIM:(g + 1) * HEAD_DIM] = dq[g * BLOCK_Q:(g + 1) * BLOCK_Q]
            dkb_ref[win, :] += lax.dot_general(dsb, q, _TN, preferred_element_type=F32)
            dvb_ref[win, :] += lax.dot_general(p.astype(BF16), do, _TN, preferred_element_type=F32)
        _accumulate(dsink_ref, dsink, first)

        before = pl.ds(pl.multiple_of(jnp.maximum(sb * i - 1, 0) * BLOCK_Q, BLOCK_Q), BLOCK_Q)
        own = pl.ds(pl.multiple_of(i * tq, BLOCK_Q), tq)
        after = pl.ds(pl.multiple_of(jnp.minimum(sb * i + sb, nblk - 1) * BLOCK_Q, BLOCK_Q), BLOCK_Q)
        for acc_ref, band_ref in ((dk_ref, dkb_ref), (dv_ref, dvb_ref)):
            acc_ref[before, :] += band_ref[0:BLOCK_Q, :]
            acc_ref[own, :] += band_ref[BLOCK_Q:BLOCK_Q + tq, :]
            acc_ref[after, :] += band_ref[BLOCK_Q + tq:, :]

    return _pcall(
        body, deps, name="attn_b_bwd", grid=(n_kv, nblk // sb),
        in_specs=[*_head_specs(q_off, tq),
                  *_band_specs(lambda kv: q_off + n_q + kv, nblk, sb),
                  *_band_specs(lambda kv: q_off + n_q + n_kv + kv, nblk, sb),
                  *_head_specs(o_off, tq), *_head_specs(o_off, tq),
                  pl.BlockSpec((GROUP, tq, 1), lambda kv, i: (kv, i, 0)),
                  pl.BlockSpec((GROUP, BLOCK_Q, 3 * BLOCK_Q), lambda kv, i: (kv, 0, 0)),
                  pl.BlockSpec(memory_space=pltpu.SMEM)],
        out_specs=[pl.BlockSpec((tq, GROUP * HEAD_DIM), lambda kv, i: (i, kv)),
                   pl.BlockSpec((T, HEAD_DIM), lambda kv, i: (0, kv)),
                   pl.BlockSpec((T, HEAD_DIM), lambda kv, i: (0, kv)),
                   pl.BlockSpec((GROUP, BLOCK_Q, 3 * BLOCK_Q), lambda kv, i: (kv, 0, 0)),
                   pl.BlockSpec((None, SUBLANES, LANES), lambda kv, i: (kv, 0, 0))],
        out_shape=[jax.ShapeDtypeStruct((T, n_q * HEAD_DIM), BF16),
                   jax.ShapeDtypeStruct((T, n_kv * HEAD_DIM), F32),
                   jax.ShapeDtypeStruct((T, n_kv * HEAD_DIM), F32),
                   jax.ShapeDtypeStruct((n_q, BLOCK_Q, 3 * BLOCK_Q), F32),
                   jax.ShapeDtypeStruct((n_kv, SUBLANES, LANES), F32)],
        scratch_shapes=[pltpu.VMEM((tq + 2 * BLOCK_Q, HEAD_DIM), F32), pltpu.VMEM((tq + 2 * BLOCK_Q, HEAD_DIM), F32)],
        compiler_params=_params(("parallel", "arbitrary")),
    )(*([pb] * (GROUP + 6)), *([o_cat] * GROUP), *([d_o] * GROUP), lse, bias, sink)


def _table_grads(dbias, dsink_raw, idx):
    n_heads = dbias.shape[0]
    n_kv = dsink_raw.shape[0]

    def body(db_ref, ds_ref, idx_ref, dt_ref, dsk_ref):
        iv = idx_ref[...]
        row = lax.broadcasted_iota(jnp.int32, (SUBLANES, LANES), 0)
        lane = lax.broadcasted_iota(jnp.int32, (SUBLANES, LANES), 1)
        dsk = jnp.zeros((SUBLANES, LANES), F32)
        for h in range(n_heads):
            d = db_ref[h]
            acc = jnp.zeros((SUBLANES, LANES), F32)
            for b in range(N_BUCKETS):
                acc = jnp.where((row == 0) & (lane == b), jnp.sum(jnp.where(iv == b, d, 0.0)), acc)
            dt_ref[:, h * LANES:(h + 1) * LANES] = acc
            raw = ds_ref[h // GROUP]
            val = jnp.sum(jnp.where((row == h % GROUP) & (lane == 0), raw, 0.0))
            dsk = jnp.where((row == 0) & (lane == h), val, dsk)
        dsk_ref[...] = dsk

    return pl.pallas_call(
        body, name="table_grads",
        in_specs=[pl.BlockSpec(memory_space=pltpu.VMEM)] * 3,
        out_specs=[pl.BlockSpec(memory_space=pltpu.VMEM)] * 2,
        out_shape=[jax.ShapeDtypeStruct((SUBLANES, n_heads * LANES), F32),
                   jax.ShapeDtypeStruct((SUBLANES, LANES), F32)],
        compiler_params=pltpu.CompilerParams(vmem_limit_bytes=56 * 1024 * 1024),
    )(dbias, dsink_raw, idx)


def _position():
    x, y, c = lax.axis_index("x"), lax.axis_index("y"), lax.axis_index("c")
    return x, y, c


def _hbm(a):
    return pltpu.with_memory_space_constraint(a, pltpu.HBM)


def _split_start(name, bufs, sem_shapes, issue):
    nb, ns = len(bufs), len(sem_shapes)

    def body(*refs):
        buf_refs = refs[:nb]
        sems = refs[nb:nb + ns]
        token = refs[nb + ns + nb]
        issue(buf_refs, sems)
        token[...] = jnp.zeros(token.shape, F32)

    outs = pl.pallas_call(
        body, name=name,
        in_specs=[_HBM] * nb,
        out_specs=[_SEM] * ns + [_HBM] * nb + [_VMEM],
        out_shape=[pltpu.SemaphoreType.DMA(s) for s in sem_shapes] + [pltpu.HBM(b.shape, b.dtype) for b in bufs]
        + [jax.ShapeDtypeStruct((SUBLANES, LANES), F32)],
        input_output_aliases={i: ns + i for i in range(nb)},
        compiler_params=pltpu.CompilerParams(has_side_effects=_EFFECT),
    )(*[_hbm(b) for b in bufs])
    return outs[:ns], outs[ns:ns + nb], outs[-1]


def _split_wait(name, bufs, send, recv, counts, size_of, after):
    nb = len(bufs)

    def body(*refs):
        buf_refs = refs[:nb]
        send_ref, recv_ref = refs[nb], refs[nb + 1]
        x, y, c = _position()
        for w, n in enumerate(counts):
            ref = size_of(buf_refs, w)
            for k in range(n):
                s = sum(counts[:w]) + k
                cp = pltpu.make_async_remote_copy(
                    src_ref=ref, dst_ref=ref, send_sem=send_ref.at[s], recv_sem=recv_ref.at[s],
                    device_id=(x, y, c), device_id_type=MESH)
                cp.wait_send()
                cp.wait_recv()

    return pl.pallas_call(
        body, name=name,
        in_specs=[_HBM] * nb + [_SEM, _SEM, _ANY],
        out_specs=[_HBM] * nb,
        out_shape=[pltpu.HBM(b.shape, b.dtype) for b in bufs],
        input_output_aliases={i: i for i in range(nb)},
        compiler_params=pltpu.CompilerParams(has_side_effects=_EFFECT),
    )(*bufs, send, recv, after)


def _block_of(pos):
    return 4 * pos[0] + 2 * pos[1] + pos[2]


def _shard_of(ref, blk, by_cols):
    aligned = (lambda v, a: v) if isinstance(blk, int) else pl.multiple_of
    if by_cols:
        n = ref.shape[1] // N_DEV
        return ref.at[:, pl.ds(aligned(blk * n, LANES), n)]
    r = ref.shape[0] // N_DEV
    return ref.at[pl.ds(aligned(blk * r, SUBLANES), r), :]


def _place_own(name, land, shard, by_cols, tr=256):
    r, n = shard.shape
    tr = _tile(r, tr)
    mine = _block_of(_position()).astype(jnp.int32).reshape(1)

    def body(m_ref, land_ref, s_ref, o_ref):
        o_ref[...] = s_ref[...]

    if by_cols:
        out = pl.BlockSpec((tr, n), lambda i, m_ref: (i, m_ref[0]))
    else:
        out = pl.BlockSpec((tr, n), lambda i, m_ref: (m_ref[0] * (r // tr) + i, 0))
    return pl.pallas_call(
        body, name=name,
        grid_spec=pltpu.PrefetchScalarGridSpec(
            num_scalar_prefetch=1, grid=(r // tr,),
            in_specs=[_ANY, pl.BlockSpec((tr, n), lambda i, m_ref: (i, 0))], out_specs=out),
        out_shape=jax.ShapeDtypeStruct(land.shape, land.dtype),
        input_output_aliases={1: 0},
        compiler_params=_params(("parallel",)),
    )(mine, land, shard)


def _gather_start(name, shards, by_cols, groups, after=None):
    nw = len(shards)
    lands = [lax.empty((s.shape[0], s.shape[1] * N_DEV) if cols else (s.shape[0] * N_DEV, s.shape[1]), s.dtype)
             for s, cols in zip(shards, by_cols)]
    order = [] if after is None else [after]

    def issue(bufs, sems):
        x, y, c = _position()
        peers = [(x, y, 1 - c), (1 - x, y, c), (x, 1 - y, c), (1 - x, 1 - y, c)]
        for gi, grp in enumerate(groups):
            for wi, w in enumerate(grp):
                for k, peer in enumerate(peers):
                    pltpu.make_async_remote_copy(
                        src_ref=bufs[w], dst_ref=_shard_of(bufs[nw + w], _block_of((x, y, c)), by_cols[w]),
                        send_sem=sems[2 * gi].at[4 * wi + k], recv_sem=sems[2 * gi + 1].at[4 * wi + k],
                        device_id=peer, device_id_type=MESH).start()

    sem_shapes = [(4 * len(g),) for g in groups for _ in range(2)]
    sems, thru, token = _split_start(name, list(shards) + lands + order, sem_shapes, issue)
    return sems, thru[:nw], thru[nw:2 * nw], token


def _gather_forward(name, lands, by_cols):
    nw = len(lands)

    def issue(land, sems):
        x, y, c = _position()
        for w in range(nw):
            for k, chip in enumerate([(1 - x, y), (x, 1 - y), (1 - x, 1 - y)]):
                blk = _shard_of(land[w], _block_of((*chip, c)), by_cols[w])
                pltpu.make_async_remote_copy(
                    src_ref=blk, dst_ref=blk, send_sem=sems[0].at[3 * w + k], recv_sem=sems[1].at[3 * w + k],
                    device_id=(x, y, 1 - c), device_id_type=MESH).start()

    return _split_start(name, lands, [(3 * nw,), (3 * nw,)], issue)


def _first_block(bufs, w, offset=0):
    return bufs[offset + w].at[0]


_PEER_FLIPS = ((0, 0, 1), (1, 0, 0), (1, 0, 1), (0, 1, 0), (0, 1, 1), (1, 1, 0), (1, 1, 1))


def _scatter_start(name, grads, by_cols):
    nw = len(grads)
    lands = []
    for g, cols in zip(grads, by_cols):
        shard = (g.shape[0], g.shape[1] // N_DEV) if cols else (g.shape[0] // N_DEV, g.shape[1])
        lands.append(lax.empty((N_DEV,) + shard, g.dtype))

    def issue(bufs, sems):
        x, y, c = _position()
        flip = lambda v, f: 1 - v if f else v
        for w in range(nw):
            for k, (fx, fy, fc) in enumerate(_PEER_FLIPS):
                peer = (flip(x, fx), flip(y, fy), flip(c, fc))
                pltpu.make_async_remote_copy(
                    src_ref=_shard_of(bufs[w], _block_of(peer), by_cols[w]), dst_ref=bufs[nw + w].at[_block_of((x, y, c))],
                    send_sem=sems[0].at[7 * w + k], recv_sem=sems[1].at[7 * w + k],
                    device_id=peer, device_id_type=MESH).start()

    return _split_start(name, list(grads) + lands, [(7 * nw,), (7 * nw,)], issue)


def _adam(w, g, m, v):
    m = ADAM_B1 * m + (1.0 - ADAM_B1) * g
    v = ADAM_B2 * v + (1.0 - ADAM_B2) * (g * g)
    m_hat = m / (1.0 - ADAM_B1 ** ADAM_STEP)
    v_hat = v / (1.0 - ADAM_B2 ** ADAM_STEP)
    delta = -ADAM_LR * (m_hat / (jnp.sqrt(v_hat) + ADAM_EPS) + ADAM_WD * w)
    return delta, m, v


def _sum_adam(name, landed, grad, by_cols, w, m, v, tr=256):
    R, C = w.shape
    tr = _tile(R, tr)
    mine = _block_of(_position()).astype(jnp.int32).reshape(1)

    def body(me_ref, l_ref, own_ref, w_ref, m_ref, v_ref, g_ref, d_ref, nm_ref, nv_ref):
        own = own_ref[...].astype(F32)
        g = None
        for d in range(N_DEV):
            part = jnp.where(me_ref[0] == d, own, l_ref[d].astype(F32))
            g = part if g is None else g + part
        g_ref[...] = g
        d_ref[...], nm_ref[...], nv_ref[...] = _adam(w_ref[...], g, m_ref[...], v_ref[...])

    tile = pl.BlockSpec((tr, C), lambda i, me_ref: (i, 0))
    if by_cols:
        own = pl.BlockSpec((tr, C), lambda i, me_ref: (i, me_ref[0]))
    else:
        own = pl.BlockSpec((tr, C), lambda i, me_ref: (me_ref[0] * (R // tr) + i, 0))
    return pl.pallas_call(
        body, name=name,
        grid_spec=pltpu.PrefetchScalarGridSpec(
            num_scalar_prefetch=1, grid=(R // tr,),
            in_specs=[pl.BlockSpec((N_DEV, tr, C), lambda i, me_ref: (0, i, 0)), own, tile, tile, tile],
            out_specs=[tile] * 4),
        out_shape=[jax.ShapeDtypeStruct((R, C), F32)] * 4,
        compiler_params=_params(("parallel",)),
    )(mine, landed, grad, w, m, v)


def _small_all_reduce(parts, deps=()):
    W = parts.shape[1]

    def body(p_ref, o_ref, slots, send_sems, recv_sems):
        x, y, c = _position()
        me = 4 * x + 2 * y + c
        slots[me] = jnp.sum(p_ref[...], axis=0, keepdims=True)
        peers = [(x, y, 1 - c), (1 - x, y, c), (1 - x, y, 1 - c), (x, 1 - y, c), (x, 1 - y, 1 - c),
                 (1 - x, 1 - y, c), (1 - x, 1 - y, 1 - c)]
        copies = []
        for k, peer in enumerate(peers):
            cp = pltpu.make_async_remote_copy(
                src_ref=slots.at[me], dst_ref=slots.at[me], send_sem=send_sems.at[k], recv_sem=recv_sems.at[k],
                device_id=peer, device_id_type=MESH)
            cp.start()
            copies.append(cp)
        for cp in copies:
            cp.wait()
        total = slots[0]
        for d in range(1, N_DEV):
            total = total + slots[d]
        o_ref[...] = total

    return _pcall(
        body, deps, name="small_all_reduce",
        in_specs=[pl.BlockSpec(memory_space=pltpu.VMEM)], out_specs=pl.BlockSpec(memory_space=pltpu.VMEM),
        out_shape=jax.ShapeDtypeStruct((1, W), F32),
        scratch_shapes=[pltpu.VMEM((N_DEV, 1, W), F32), pltpu.SemaphoreType.DMA((7,)), pltpu.SemaphoreType.DMA((7,))],
    )(parts)


def _adam_small(w, g, m, v):
    def body(w_ref, g_ref, m_ref, v_ref, d_ref, nm_ref, nv_ref):
        d_ref[...], nm_ref[...], nv_ref[...] = _adam(w_ref[...], g_ref[...], m_ref[...], v_ref[...])

    return pl.pallas_call(
        body, name="adam_small",
        in_specs=[pl.BlockSpec(memory_space=pltpu.VMEM)] * 4, out_specs=[pl.BlockSpec(memory_space=pltpu.VMEM)] * 3,
        out_shape=[jax.ShapeDtypeStruct(w.shape, F32)] * 3,
    )(w, g, m, v)


_GATHER_GROUPS = (("w_in",), ("w_out", "w_up", "ple_w"), ("w_down", "w_gate"))
_COL_SHARDED = ("w_in", "w_up", "ple_w")


class _MeshComm:
    def __init__(self, w, mom, var):
        self.w, self.mom, self.var = w, mom, var
        self.out = {}
        self._scatters = {}

    def gather_begin(self):
        self._groups = {}
        token = None
        for tag, first, group_list in (("gather_start0", 0, _GATHER_GROUPS[:1]), ("gather_start1", 1, _GATHER_GROUPS[1:])):
            names = [n for g in group_list for n in g]
            idx = {n: i for i, n in enumerate(names)}
            by_cols = [n in _COL_SHARDED for n in names]
            sems, src, lands, token = _gather_start(tag, [self.w[n].astype(BF16) for n in names], by_cols,
                                                    [[idx[n] for n in g] for g in group_list], token)
            lands = [_place_own("place_" + n, land, s, cols) for n, land, s, cols in zip(names, lands, src, by_cols)]
            for k, g in enumerate(group_list):
                self._groups[first + k] = (sems[2 * k], sems[2 * k + 1], [src[idx[n]] for n in g],
                                           [lands[idx[n]] for n in g])
        return token

    @staticmethod
    def _shard_size(names, offset):
        return lambda bufs, w: _shard_of(bufs[offset + w], 0, names[w] in _COL_SHARDED)

    def gather_arrive(self, gi, after):
        names = _GATHER_GROUPS[gi]
        send, recv, src, lands = self._groups[gi]
        out = _split_wait("gather_arrive%d" % gi, src + lands, send, recv, [4] * len(names),
                          self._shard_size(names, len(names)), after)
        self._arrived = out[len(names):]

    def gather_forward(self, gi):
        by_cols = [n in _COL_SHARDED for n in _GATHER_GROUPS[gi]]
        self._fsems, self._fthru, token = _gather_forward("gather_forward%d" % gi, self._arrived, by_cols)
        return token

    def gather_finish(self, gi, after):
        names = _GATHER_GROUPS[gi]
        out = _split_wait("gather_finish%d" % gi, self._fthru, self._fsems[0], self._fsems[1], [3] * len(names),
                          self._shard_size(names, 0), after)
        return dict(zip(names, out))

    def reduce_begin(self, key, grads):
        names = list(grads)
        sems, thru, token = _scatter_start("scatter_start_" + key, [grads[n] for n in names],
                                           [n in _COL_SHARDED for n in names])
        self._scatters[key] = (names, sems, thru)
        return token

    def reduce_finish(self, key, after):
        names, sems, thru = self._scatters[key]
        nw = len(names)
        out = _split_wait("scatter_wait_" + key, thru, sems[0], sems[1], [N_DEV - 1] * nw,
                          functools.partial(_first_block, offset=nw), after)
        for i, n in enumerate(names):
            self.out[n] = _sum_adam("adam_" + n, out[nw + i], out[i], n in _COL_SHARDED, self.w[n], self.mom[n],
                                    self.var[n])


def _step(x, p, target, gains, comm):
    T, D = x.shape
    n_q = D // (2 * HEAD_DIM)
    n_kv = n_q // GROUP
    cos, sin = _rope_tables(T)
    idx = _bucket_index()

    t = comm.gather_begin()
    u = _rms_fwd("norm_attn", x, gains["attn_norm_g"], deps=(t,))
    comm.gather_arrive(0, u)
    t = comm.gather_forward(0)
    bias = _bias_build(idx, gains["rel_bias_table"].reshape(-1), n_q, deps=(t,))
    full = comm.gather_finish(0, bias)
    proj_a, pb = _in_proj(u, full["w_in"], cos, sin, gains["q_norm_g"], gains["k_norm_g"], n_q + n_kv)
    o_a, lse_a = _attn_a_fwd(pb, n_q, n_kv, 2 * n_q)
    comm.gather_arrive(1, lse_a)
    t = comm.gather_forward(1)
    sink = gains["sink_logits"].reshape(-1)
    b_off = n_q + 2 * n_kv
    o_cat, lse_b = _attn_b_fwd(pb, bias, sink, o_a, b_off, n_q, n_kv, deps=(t,))
    full.update(comm.gather_finish(1, lse_b))
    h1, m_in = _mm_nn_rms("out_proj", o_cat, full["w_out"], x, gains["mlp_norm_g"])

    def up_epilogue(acc, extra, outs):
        outs[0][...] = acc.astype(BF16)
        r = jnp.maximum(acc, 0.0)
        outs[1][...] = (r * r).astype(BF16)

    a_act, f_act = _mm_nn("up_proj", m_in, full["w_up"], epilogue=up_epilogue, out_dtypes=[BF16, BF16], tn=2048)
    comm.gather_arrive(2, f_act)
    t = comm.gather_forward(2)
    p_b = p.astype(BF16)
    pe = _mm_nn("ple_proj", p_b, full["ple_w"], deps=(t,))
    full.update(comm.gather_finish(2, pe))
    h2 = _mm_nn("down_proj", f_act, full["w_down"], epilogue=_store_add, extras=(h1,), tn=256)
    gn = _rms_fwd("norm_gate", h2, gains["gate_norm_g"])

    dh3, dz, dpe, dg_final, dg_ple, loss_part = _gate_tail(gn, full["w_gate"], h2, pe, target, gains["ple_norm_g"],
                                                           gains["final_norm_g"])
    gw_gate = _mm_tn("grad_w_gate", gn, dz)
    gw_ple = _mm_tn("grad_ple_w", p_b, dpe)
    dh2, dh2_b, dg_gate = _mm_nt_rms_bwd("d_gate_in", dz, full["w_gate"], h2, gains["gate_norm_g"], dh3)
    gw_down = _mm_tn("grad_w_down", f_act, dh2_b)
    t = comm.reduce_begin("b", dict(w_gate=gw_gate, ple_w=gw_ple, w_down=gw_down))

    def act_bwd(acc, extra, outs):
        outs[0][...] = (acc * (2.0 * jnp.maximum(extra[0][...].astype(F32), 0.0))).astype(BF16)

    da = _mm_nt("d_act", dh2_b, full["w_down"], out_dtype=BF16, epilogue=act_bwd, extras=(a_act,), tn=2048, deps=(t,))
    gw_up = _mm_tn("grad_w_up", m_in, da)
    dm = _mm_nt("d_mlp_in", da, full["w_up"], tn=256)
    dh1, dh1_b, dg_mlp = _rms_bwd("norm_mlp_bwd", dm, h1, gains["mlp_norm_g"], dh2)
    gw_out = _mm_tn("grad_w_out", o_cat, dh1_b)
    t = comm.reduce_begin("d", dict(w_up=gw_up, w_out=gw_out))
    d_o = _mm_nt("d_attn_out", dh1_b, full["w_out"], out_dtype=BF16, deps=(t,))
    dqa, dka_t, dva_t = _attn_a_bwd(pb, o_cat, d_o, lse_a, n_q, n_kv)
    dka, dva = dka_t.T, dva_t.T
    dqb, dkb, dvb, dbias, dsink_raw = _attn_b_bwd(pb, o_cat, d_o, lse_b, bias, sink, b_off, n_q, n_kv, n_q)
    dtable, dsink = _table_grads(dbias, dsink_raw, idx)
    dproj, dg_q, dg_k = _dproj(proj_a, dqa, dka, dva, dqb, dkb, dvb, cos, sin, gains["q_norm_g"], gains["k_norm_g"])
    gw_in = _mm_tn("grad_w_in", u, dproj)
    t = comm.reduce_begin("e", dict(w_in=gw_in))
    dx, dg_attn = _mm_nt_rms_bwd("d_attn_in", dproj, full["w_in"], x, gains["attn_norm_g"], dh1, with_bf16=False,
                                 deps=(t,))
    for key in "bd":
        comm.reduce_finish(key, dx)

    parts = jnp.concatenate([dg_attn, dg_mlp, dg_ple, dg_gate, dg_final, dg_q, dg_k, dtable, dsink, loss_part], axis=1)
    return dx, parts


_SHARDED = ("w_in", "w_out", "w_up", "w_down", "ple_w", "w_gate")
_VECTORS = ("attn_norm_g", "mlp_norm_g", "ple_norm_g", "gate_norm_g", "final_norm_g")
_ORDER = ("attn_norm_g", "w_in", "q_norm_g", "k_norm_g", "sink_logits", "w_out", "mlp_norm_g", "w_up", "w_down",
          "ple_w", "ple_norm_g", "gate_norm_g", "w_gate", "rel_bias_table", "final_norm_g")


def _pack_small(vals, n_heads):
    lane_pad = lambda v: jnp.pad(v, ((0, 0), (0, LANES - v.shape[1])))
    table = lane_pad(vals["rel_bias_table"].T).reshape(1, n_heads * LANES)
    return jnp.concatenate(
        [vals[n].reshape(1, -1) for n in _VECTORS] + [vals["q_norm_g"], vals["k_norm_g"], table,
                                                      lane_pad(vals["sink_logits"]), jnp.zeros((1, LANES), F32)], axis=1)


def _unpack_small(row, like, n_heads):
    out, off = {}, 0
    for n in _VECTORS:
        out[n] = row[:, off:off + like[n].size].reshape(like[n].shape)
        off += like[n].size
    for n in ("q_norm_g", "k_norm_g"):
        out[n] = row[:, off:off + LANES]
        off += LANES
    out["rel_bias_table"] = row[:, off:off + n_heads * LANES].reshape(n_heads, LANES)[:, :N_BUCKETS].T
    off += n_heads * LANES
    out["sink_logits"] = row[:, off:off + n_heads]
    off += LANES
    return out, row[0, off]


def kernel(x, p, attn_norm_g, w_in, q_norm_g, k_norm_g, sink_logits, w_out, mlp_norm_g, w_up, w_down, ple_w, ple_norm_g, gate_norm_g, w_gate, rel_bias_table, final_norm_g, loss_target, m_attn_norm_g, m_w_in, m_q_norm_g, m_k_norm_g, m_sink_logits, m_w_out, m_mlp_norm_g, m_w_up, m_w_down, m_ple_w, m_ple_norm_g, m_gate_norm_g, m_w_gate, m_rel_bias_table, m_final_norm_g, v_attn_norm_g, v_w_in, v_q_norm_g, v_k_norm_g, v_sink_logits, v_w_out, v_mlp_norm_g, v_w_up, v_w_down, v_ple_w, v_ple_norm_g, v_gate_norm_g, v_w_gate, v_rel_bias_table, v_final_norm_g):
    w = dict(attn_norm_g=attn_norm_g, w_in=w_in[0], q_norm_g=q_norm_g, k_norm_g=k_norm_g, sink_logits=sink_logits,
             w_out=w_out[0], mlp_norm_g=mlp_norm_g, w_up=w_up[0], w_down=w_down[0], ple_w=ple_w[0],
             ple_norm_g=ple_norm_g, gate_norm_g=gate_norm_g, w_gate=w_gate[0], rel_bias_table=rel_bias_table,
             final_norm_g=final_norm_g)
    mom = dict(attn_norm_g=m_attn_norm_g, w_in=m_w_in[0], q_norm_g=m_q_norm_g, k_norm_g=m_k_norm_g,
               sink_logits=m_sink_logits, w_out=m_w_out[0], mlp_norm_g=m_mlp_norm_g, w_up=m_w_up[0],
               w_down=m_w_down[0], ple_w=m_ple_w[0], ple_norm_g=m_ple_norm_g, gate_norm_g=m_gate_norm_g,
               w_gate=m_w_gate[0], rel_bias_table=m_rel_bias_table, final_norm_g=m_final_norm_g)
    var = dict(attn_norm_g=v_attn_norm_g, w_in=v_w_in[0], q_norm_g=v_q_norm_g, k_norm_g=v_k_norm_g,
               sink_logits=v_sink_logits, w_out=v_w_out[0], mlp_norm_g=v_mlp_norm_g, w_up=v_w_up[0],
               w_down=v_w_down[0], ple_w=v_ple_w[0], ple_norm_g=v_ple_norm_g, gate_norm_g=v_gate_norm_g,
               w_gate=v_w_gate[0], rel_bias_table=v_rel_bias_table, final_norm_g=v_final_norm_g)
    D = x.shape[-1]
    n_heads = D // (2 * HEAD_DIM)

    gains = {n: w[n] for n in w if n not in _SHARDED}
    gains["final_norm_g"] = final_norm_g.reshape(1, -1)

    comm = _MeshComm(w, mom, var)
    dx, parts = _step(x[0], p[0, 0], loss_target[0], gains, comm)

    small_g = _small_all_reduce(parts, deps=[comm.out[n][0] for n in comm.out])
    comm.reduce_finish("e", small_g)

    g_out, d_out, m_out, v_out = {}, {}, {}, {}
    for n in _SHARDED:
        g, d, nm, nv = comm.out[n]
        g_out[n], d_out[n], m_out[n], v_out[n] = g[None], d[None], nm[None], nv[None]

    small = {n: v for n, v in w.items() if n not in _SHARDED}
    pack = lambda vals: _pack_small({n: vals[n] for n in small}, n_heads)
    sd, sm, sv = _adam_small(pack(w), small_g, pack(mom), pack(var))
    sg, loss = _unpack_small(small_g, small, n_heads)
    g_out.update(sg)
    for dst, row in ((d_out, sd), (m_out, sm), (v_out, sv)):
        dst.update(_unpack_small(row, small, n_heads)[0])

    return (loss, dx[None], *[g_out[n] for n in _ORDER], *[d_out[n] for n in _ORDER],
            *[m_out[n] for n in _ORDER], *[v_out[n] for n in _ORDER])
```

```python
import functools
import math

import numpy as np
import jax
import jax.numpy as jnp
from jax import lax
from jax.experimental import pallas as pl
from jax.experimental.pallas import tpu as pltpu

F32 = jnp.float32
BF16 = jnp.bfloat16

N_DEV = 8
N_CHIP = 4
HEAD_DIM = 128
GROUP = 4
GRID_W = 64
WINDOW = 128
BLOCK_Q = 128
N_BUCKETS = 32
MAX_DISTANCE = 128
ROPE_THETA = 10000.0
EPS = 1e-6
NEG_INF = -1e30
ADAM_LR = 0.001
ADAM_B1 = 0.9
ADAM_B2 = 0.999
ADAM_EPS = 1e-08
ADAM_WD = 0.01
ADAM_STEP = 10
LOG2E = math.log2(math.e)
LANES = 128
SUBLANES = 8
MESH = pl.DeviceIdType.MESH

_NT = (((1,), (1,)), ((), ()))
_NN = (((1,), (0,)), ((), ()))
_TN = (((0,), (0,)), ((), ()))


def _tile(dim, pref):
    return pref if dim % pref == 0 else dim


def _params(sem):
    return pltpu.CompilerParams(dimension_semantics=sem, vmem_limit_bytes=56 * 1024 * 1024)


_HBM = pl.BlockSpec(memory_space=pltpu.HBM)
_SEM = pl.BlockSpec(memory_space=pltpu.SEMAPHORE)
_ANY = pl.BlockSpec(memory_space=pl.ANY)
_VMEM = pl.BlockSpec(memory_space=pltpu.VMEM)
_EFFECT = pltpu.SideEffectType.DATAFLOW_SIDE_EFFECTING


def _pcall(body, deps=(), *, in_specs, into=None, **kw):
    deps = [d for d in deps if d is not None]
    nd = len(deps)
    if into is not None:
        deps = [into[0]] + deps
        nd += 1
        kw["input_output_aliases"] = {0: into[1]}

    def wrapped(*refs):
        body(*refs[nd:])

    call = pl.pallas_call(wrapped, in_specs=[_ANY] * nd + list(in_specs), **kw)
    return lambda *args: call(*deps, *args)


def _mm(name, a, b, dims, grid, a_spec, b_spec, out_shape, out_specs, acc_shape, epilogue,
        extras=(), extra_specs=(), deps=(), semantics=("parallel", "parallel", "arbitrary")):
    nk = grid[2]
    n_extra = len(extras)

    def body(*refs):
        a_ref, b_ref = refs[0], refs[1]
        extra = refs[2:2 + n_extra]
        outs = refs[2 + n_extra:-1]
        acc = refs[-1]
        part = lax.dot_general(a_ref[...], b_ref[...], dims, preferred_element_type=F32)
        if nk == 1:
            epilogue(part, extra, outs)
        else:
            k = pl.program_id(2)

            @pl.when(k == 0)
            def _():
                acc[...] = part

            @pl.when(k > 0)
            def _():
                acc[...] += part

            @pl.when(k == nk - 1)
            def _():
                epilogue(acc[...], extra, outs)

    return _pcall(
        body, deps, name=name, grid=grid,
        in_specs=[a_spec, b_spec, *extra_specs],
        out_specs=out_specs, out_shape=out_shape,
        scratch_shapes=[pltpu.VMEM(acc_shape if nk > 1 else (SUBLANES, LANES), F32)],
        compiler_params=_params(semantics),
    )(a, b, *extras)


def _store(dtype):
    def ep(acc, extra, outs):
        outs[0][...] = acc.astype(dtype)
    return ep


def _store_add(acc, extra, outs):
    outs[0][...] = acc + extra[0][...]


def _mm_nn(name, a, b, out_dtype=F32, epilogue=None, extras=(), n_out=1, out_dtypes=None, tm=1024, tn=1024, tk=None,
           deps=()):
    M, K = a.shape
    N = b.shape[1]
    tm, tn, tk = _tile(M, tm), _tile(N, tn), _tile(K, tk or K)
    b_spec = pl.BlockSpec((tk, tn), lambda i, j, k: (k, j))
    grid = (M // tm, N // tn, K // tk)
    o_spec = pl.BlockSpec((tm, tn), lambda i, j, k: (i, j))
    out_dtypes = out_dtypes or [out_dtype] * n_out
    out_shape = [jax.ShapeDtypeStruct((M, N), d) for d in out_dtypes]
    res = _mm(name, a, b, _NN, grid, pl.BlockSpec((tm, tk), lambda i, j, k: (i, k)), b_spec,
              out_shape, [o_spec] * len(out_dtypes), (tm, tn), epilogue or _store(out_dtype),
              extras, [o_spec] * len(extras), deps)
    return res if len(out_dtypes) > 1 else res[0]


def _mm_nt(name, a, b, out_dtype=F32, epilogue=None, extras=(), tm=1024, tn=1024, tk=None, deps=()):
    M, C = a.shape
    N = b.shape[0]
    tm, tn, tk = _tile(M, tm), _tile(N, tn), _tile(C, tk or C)
    b_spec = pl.BlockSpec((tn, tk), lambda i, j, k: (j, k))
    grid = (M // tm, N // tn, C // tk)
    o_spec = pl.BlockSpec((tm, tn), lambda i, j, k: (i, j))
    return _mm(name, a, b, _NT, grid, pl.BlockSpec((tm, tk), lambda i, j, k: (i, k)), b_spec,
               [jax.ShapeDtypeStruct((M, N), out_dtype)], [o_spec], (tm, tn), epilogue or _store(out_dtype),
               extras, [o_spec] * len(extras), deps)[0]


def _mm_tn(name, a, b, out_dtype=BF16, tm=1024, tn=512, tk=None, deps=()):
    T, M = a.shape
    N = b.shape[1]
    tm, tn, tk = _tile(M, tm), _tile(N, tn), _tile(T, tk or T)
    out_shape = jax.ShapeDtypeStruct((M, N), out_dtype)
    o_spec = pl.BlockSpec((tm, tn), lambda i, j, k: (i, j))
    grid = (M // tm, N // tn, T // tk)
    return _mm(name, a, b, _TN, grid, pl.BlockSpec((tk, tm), lambda i, j, k: (k, i)),
               pl.BlockSpec((tk, tn), lambda i, j, k: (k, j)), [out_shape], [o_spec], (tm, tn), _store(out_dtype),
               deps=deps)[0]


def _mean_last(v):
    return jnp.mean(v, axis=-1, keepdims=True)


def _rows_to_sublanes(v):
    r, c = v.shape
    return jnp.sum(v.reshape(r // SUBLANES, SUBLANES, c), axis=0)


def _accumulate(ref, val, first):
    @pl.when(first)
    def _():
        ref[...] = val

    @pl.when(jnp.logical_not(first))
    def _():
        ref[...] += val


def _rms_fwd(name, x, g, tr=256, deps=()):
    T, D = x.shape
    tr = _tile(T, tr)

    def body(x_ref, g_ref, o_ref):
        xv = x_ref[...]
        r = lax.rsqrt(_mean_last(xv * xv) + EPS)
        o_ref[...] = (xv * r * g_ref[...]).astype(BF16)

    row = pl.BlockSpec((tr, D), lambda i: (i, 0))
    return _pcall(
        body, deps, name=name, grid=(T // tr,),
        in_specs=[row, pl.BlockSpec((1, D), lambda i: (0, 0))],
        out_specs=row, out_shape=jax.ShapeDtypeStruct((T, D), BF16),
        compiler_params=_params(("parallel",)),
    )(x, g)


def _rms_bwd(name, dyn, x, g, dres, tr=256, deps=()):
    T, D = x.shape
    tr = _tile(T, tr)

    def body(dy_ref, x_ref, g_ref, dr_ref, dx_ref, dxb_ref, dg_ref):
        xv = x_ref[...]
        r = lax.rsqrt(_mean_last(xv * xv) + EPS)
        xn = xv * r
        dy = dy_ref[...].astype(F32)
        dxn = dy * g_ref[...]
        dx = dr_ref[...] + r * (dxn - xn * _mean_last(dxn * xn))
        dx_ref[...] = dx
        dxb_ref[...] = dx.astype(BF16)
        _accumulate(dg_ref, _rows_to_sublanes(dy * xn), pl.program_id(0) == 0)

    row = pl.BlockSpec((tr, D), lambda i: (i, 0))
    return _pcall(
        body, deps, name=name, grid=(T // tr,),
        in_specs=[row, row, pl.BlockSpec((1, D), lambda i: (0, 0)), row],
        out_specs=[row, row, pl.BlockSpec((SUBLANES, D), lambda i: (0, 0))],
        out_shape=[jax.ShapeDtypeStruct((T, D), F32), jax.ShapeDtypeStruct((T, D), BF16),
                   jax.ShapeDtypeStruct((SUBLANES, D), F32)],
        compiler_params=_params(("arbitrary",)),
    )(dyn, x, g, dres)


def _mm_nn_rms(name, a, b, res, g, tm=512, deps=()):
    M, K = a.shape
    N = b.shape[1]
    tm = _tile(M, tm)

    def epilogue(acc, extra, outs):
        h = acc + extra[0][...]
        outs[0][...] = h
        outs[1][...] = (h * lax.rsqrt(_mean_last(h * h) + EPS) * extra[1][...]).astype(BF16)

    row = pl.BlockSpec((tm, N), lambda i, j, k: (i, 0))
    return _mm(name, a, b, _NN, (M // tm, 1, 1), pl.BlockSpec((tm, K), lambda i, j, k: (i, 0)),
               pl.BlockSpec((K, N), lambda i, j, k: (0, 0)),
               [jax.ShapeDtypeStruct((M, N), F32), jax.ShapeDtypeStruct((M, N), BF16)], [row, row], (tm, N), epilogue,
               (res, g), [row, pl.BlockSpec((1, N), lambda i, j, k: (0, 0))], deps)


def _mm_nt_rms_bwd(name, a, b, x, g, dres, with_bf16=True, tm=256, deps=()):
    M, C = a.shape
    N = b.shape[0]
    tm = _tile(M, tm)

    def epilogue(dy, extra, outs):
        x_ref, dr_ref, g_ref = extra
        xv = x_ref[...]
        r = lax.rsqrt(_mean_last(xv * xv) + EPS)
        xn = xv * r
        dxn = dy * g_ref[...]
        dx = dr_ref[...] + r * (dxn - xn * _mean_last(dxn * xn))
        outs[0][...] = dx
        if with_bf16:
            outs[1][...] = dx.astype(BF16)
        _accumulate(outs[-1], _rows_to_sublanes(dy * xn), pl.program_id(0) == 0)

    row = pl.BlockSpec((tm, N), lambda i, j, k: (i, 0))
    copies = [jax.ShapeDtypeStruct((M, N), F32)] + ([jax.ShapeDtypeStruct((M, N), BF16)] if with_bf16 else [])
    return _mm(name, a, b, _NT, (M // tm, 1, 1), pl.BlockSpec((tm, C), lambda i, j, k: (i, 0)),
               pl.BlockSpec((N, C), lambda i, j, k: (0, 0)),
               copies + [jax.ShapeDtypeStruct((SUBLANES, N), F32)],
               [row] * len(copies) + [pl.BlockSpec((SUBLANES, N), lambda i, j, k: (0, 0))], (tm, N), epilogue,
               (x, dres, g), [row, row, pl.BlockSpec((1, N), lambda i, j, k: (0, 0))], deps,
               semantics=("arbitrary", "arbitrary", "arbitrary"))


def _gate_tail(gn, w_gate, h2, pe, target, g_ple, g_final, tm=256):
    T, D = h2.shape
    tm = _tile(T, tm)

    def epilogue(z, extra, outs):
        h2_ref, pe_ref, t_ref, gp_ref, gf_ref = extra
        dh3_ref, dz_ref, dpe_ref, dgf_ref, dgp_ref, loss_ref = outs
        first = pl.program_id(0) == 0
        pev = pe_ref[...]
        r3 = lax.rsqrt(_mean_last(pev * pev) + EPS)
        en = pev * r3
        e = en * gp_ref[...]
        gate = 1.0 / (1.0 + jnp.exp(-z))
        h3 = h2_ref[...] + gate * e
        r5 = lax.rsqrt(_mean_last(h3 * h3) + EPS)
        hn = h3 * r5
        diff = hn * gf_ref[...] - t_ref[...]
        loss_rows = 0.5 * _mean_last(diff * diff)
        row0 = lax.broadcasted_iota(jnp.int32, (SUBLANES, LANES), 0) == 0
        _accumulate(loss_ref, jnp.where(row0, jnp.sum(loss_rows), 0.0), first)
        dy = diff * (1.0 / D)
        _accumulate(dgf_ref, _rows_to_sublanes(dy * hn), first)
        dhn = dy * gf_ref[...]
        dh3 = r5 * (dhn - hn * _mean_last(dhn * hn))
        dh3_ref[...] = dh3
        dgate = dh3 * e
        de = dh3 * gate
        dz_ref[...] = (dgate * gate * (1.0 - gate)).astype(BF16)
        _accumulate(dgp_ref, _rows_to_sublanes(de * en), first)
        den = de * gp_ref[...]
        dpe_ref[...] = (r3 * (den - en * _mean_last(den * en))).astype(BF16)

    row = pl.BlockSpec((tm, D), lambda i, j, k: (i, 0))
    vec = pl.BlockSpec((1, D), lambda i, j, k: (0, 0))
    part = pl.BlockSpec((SUBLANES, D), lambda i, j, k: (0, 0))
    return _mm("gate_tail", gn, w_gate, _NN, (T // tm, 1, 1), row, pl.BlockSpec(w_gate.shape, lambda i, j, k: (0, 0)),
               [jax.ShapeDtypeStruct((T, D), F32), jax.ShapeDtypeStruct((T, D), BF16),
                jax.ShapeDtypeStruct((T, D), BF16), jax.ShapeDtypeStruct((SUBLANES, D), F32),
                jax.ShapeDtypeStruct((SUBLANES, D), F32), jax.ShapeDtypeStruct((SUBLANES, LANES), F32)],
               [row, row, row, part, part, pl.BlockSpec((SUBLANES, LANES), lambda i, j, k: (0, 0))], (tm, D), epilogue,
               (h2, pe, target, g_ple, g_final), [row, row, row, vec, vec],
               semantics=("arbitrary", "arbitrary", "arbitrary"))


def _rope_tables(T):
    pos = np.arange(T)
    half = HEAD_DIM // 2
    inv = (ROPE_THETA ** (-np.arange(0, half, 2, dtype=np.float32) / half)).astype(np.float32)
    ang_r = (pos // GRID_W).astype(np.float32)[:, None] * inv
    ang_c = (pos % GRID_W).astype(np.float32)[:, None] * inv
    cos = np.concatenate([np.cos(ang_r), np.cos(ang_r), np.cos(ang_c), np.cos(ang_c)], axis=-1)
    sin = np.concatenate([-np.sin(ang_r), np.sin(ang_r), -np.sin(ang_c), np.sin(ang_c)], axis=-1)
    return jnp.asarray(cos, F32), jnp.asarray(sin, F32)


def _swap32(x):
    lane = lax.broadcasted_iota(jnp.int32, x.shape, 1)
    return jnp.where((lane % 64) < 32, pltpu.roll(x, 96, 1), pltpu.roll(x, 32, 1))


def _in_proj(u, w_in, cos, sin, g_q, g_k, n_norm, tm=512):
    T, K = u.shape
    W = w_in.shape[1]
    tm = _tile(T, tm)
    n_q = n_norm * GROUP // (GROUP + 1)
    wa = n_norm * HEAD_DIM

    def epilogue(acc, extra, outs):
        c_ref, s_ref, gq_ref, gk_ref = extra
        raw_ref, o_ref = outs
        c, s = c_ref[...], s_ref[...]
        raw_ref[...] = acc[:, :wa]
        for h in range(n_norm):
            cols = slice(h * HEAD_DIM, (h + 1) * HEAD_DIM)
            xv = acc[:, cols]
            g = gq_ref[...] if h < n_q else gk_ref[...]
            xn = xv * lax.rsqrt(_mean_last(xv * xv) + EPS) * g
            o_ref[:, cols] = (xn * c + _swap32(xn) * s).astype(BF16)
        o_ref[:, wa:] = acc[:, wa:].astype(BF16)

    tab = pl.BlockSpec((tm, HEAD_DIM), lambda i, j, k: (i, 0))
    vec = pl.BlockSpec((1, HEAD_DIM), lambda i, j, k: (0, 0))
    return _mm("in_proj", u, w_in, _NN, (T // tm, 1, 1), pl.BlockSpec((tm, K), lambda i, j, k: (i, 0)),
               pl.BlockSpec((K, W), lambda i, j, k: (0, 0)),
               [jax.ShapeDtypeStruct((T, wa), F32), jax.ShapeDtypeStruct((T, W), BF16)],
               [pl.BlockSpec((tm, wa), lambda i, j, k: (i, 0)), pl.BlockSpec((tm, W), lambda i, j, k: (i, 0))],
               (tm, W), epilogue, (cos, sin, g_q, g_k), [tab, tab, vec, vec])


def _dproj(proj_a, dqa, dka_t, dva_t, dqb, dkb, dvb, cos, sin, g_q, g_k, tr=256):
    T, wa = proj_a.shape
    tr = _tile(T, tr)
    n_q = dqa.shape[1] // HEAD_DIM
    wkv = dka_t.shape[0]
    W = wa + wkv + dqb.shape[1] + dkb.shape[1] + dvb.shape[1]

    def body(p_ref, dqa_ref, dkat_ref, dvat_ref, dqb_ref, dkb_ref, dvb_ref, c_ref, s_ref, gq_ref, gk_ref,
             o_ref, dgq_ref, dgk_ref):
        c, s = c_ref[...], s_ref[...]
        dka = dkat_ref[...].T
        dgq = jnp.zeros((SUBLANES, HEAD_DIM), F32)
        dgk = jnp.zeros((SUBLANES, HEAD_DIM), F32)
        for h in range(wa // HEAD_DIM):
            cols = slice(h * HEAD_DIM, (h + 1) * HEAD_DIM)
            xv = p_ref[:, cols]
            r = lax.rsqrt(_mean_last(xv * xv) + EPS)
            xn = xv * r
            if h < n_q:
                d = dqa_ref[:, cols]
                g = gq_ref[...]
            else:
                d = dka[:, (h - n_q) * HEAD_DIM:(h - n_q + 1) * HEAD_DIM]
                g = gk_ref[...]
            dqn = d * c + _swap32(d * s)
            part = _rows_to_sublanes(dqn * xn)
            if h < n_q:
                dgq = dgq + part
            else:
                dgk = dgk + part
            dxn = dqn * g
            o_ref[:, cols] = (r * (dxn - xn * _mean_last(dxn * xn))).astype(BF16)
        o_ref[:, wa:wa + wkv] = dvat_ref[...].T.astype(BF16)
        off = wa + wkv
        for ref in (dqb_ref, dkb_ref, dvb_ref):
            w = ref.shape[1]
            o_ref[:, off:off + w] = ref[...].astype(BF16)
            off += w
        first = pl.program_id(0) == 0
        _accumulate(dgq_ref, dgq, first)
        _accumulate(dgk_ref, dgk, first)

    def row(w):
        return pl.BlockSpec((tr, w), lambda i: (i, 0))

    col = pl.BlockSpec((wkv, tr), lambda i: (0, i))
    vec = pl.BlockSpec((1, HEAD_DIM), lambda i: (0, 0))
    part = pl.BlockSpec((SUBLANES, HEAD_DIM), lambda i: (0, 0))
    return pl.pallas_call(
        body, name="dproj", grid=(T // tr,),
        in_specs=[row(wa), row(dqa.shape[1]), col, col, row(dqb.shape[1]),
                  row(dkb.shape[1]), row(dvb.shape[1]), row(HEAD_DIM), row(HEAD_DIM), vec, vec],
        out_specs=[row(W), part, part],
        out_shape=[jax.ShapeDtypeStruct((T, W), BF16), jax.ShapeDtypeStruct((SUBLANES, HEAD_DIM), F32),
                   jax.ShapeDtypeStruct((SUBLANES, HEAD_DIM), F32)],
        compiler_params=_params(("arbitrary",)),
    )(proj_a, dqa, dka_t, dva_t, dqb, dkb, dvb, cos, sin, g_q, g_k)


def _attn_a_fwd(pb, n_q, n_kv, out_heads, tq=1024, tc=1024):
    T = pb.shape[0]
    tq, tc = _tile(T, tq), _tile(T, tc)
    scale = HEAD_DIM ** -0.5
    c = scale * LOG2E

    def body(q_ref, k_ref, v_ref, o_ref, lse_ref):
        q = q_ref[...]
        m = l = acc = None
        for j in range(T // tc):
            keys = slice(j * tc, (j + 1) * tc)
            s = lax.dot_general(q, k_ref[keys, :], _NT, preferred_element_type=F32)
            mj = jnp.max(s, axis=-1, keepdims=True)
            m_new = mj if j == 0 else jnp.maximum(m, mj)
            p = jnp.exp2((s - m_new) * c)
            pv = lax.dot_general(p.astype(BF16), v_ref[keys, :], _NN, preferred_element_type=F32)
            if j == 0:
                l, acc = jnp.sum(p, axis=-1, keepdims=True), pv
            else:
                alpha = jnp.exp2((m - m_new) * c)
                l = alpha * l + jnp.sum(p, axis=-1, keepdims=True)
                acc = alpha * acc + pv
            m = m_new
        o_ref[...] = (acc / l).astype(BF16)
        lse_ref[...] = m * scale + jnp.log(l)

    return pl.pallas_call(
        body, name="attn_a_fwd", grid=(n_kv, GROUP, T // tq),
        in_specs=[pl.BlockSpec((tq, HEAD_DIM), lambda kv, g, i: (i, kv * GROUP + g)),
                  pl.BlockSpec((T, HEAD_DIM), lambda kv, g, i: (0, n_q + kv)),
                  pl.BlockSpec((T, HEAD_DIM), lambda kv, g, i: (0, n_q + n_kv + kv))],
        out_specs=[pl.BlockSpec((tq, HEAD_DIM), lambda kv, g, i: (i, kv * GROUP + g)),
                   pl.BlockSpec((None, tq, 1), lambda kv, g, i: (kv * GROUP + g, i, 0))],
        out_shape=[jax.ShapeDtypeStruct((T, out_heads * HEAD_DIM), BF16), jax.ShapeDtypeStruct((n_q, T, 1), F32)],
        compiler_params=_params(("parallel", "parallel", "parallel")),
    )(pb, pb, pb)


def _attn_a_bwd(pb, o_cat, d_o, lse, n_q, n_kv, tq=1024, tc=256):
    T = pb.shape[0]
    tq, tc = _tile(T, tq), _tile(T, tc)
    scale = HEAD_DIM ** -0.5
    c = scale * LOG2E

    def body(q_ref, k_ref, v_ref, o_ref, do_ref, lse_ref, dq_ref, dkt_ref, dvt_ref):
        q, do = q_ref[...], do_ref[...]
        qt, dot = q.T, do.T
        delta = jnp.sum(do.astype(F32) * o_ref[...].astype(F32), axis=-1, keepdims=True)
        lse2 = lse_ref[...] * LOG2E

        @pl.when(jnp.logical_and(pl.program_id(1) == 0, pl.program_id(2) == 0))
        def _():
            dkt_ref[...] = jnp.zeros(dkt_ref.shape, F32)
            dvt_ref[...] = jnp.zeros(dvt_ref.shape, F32)

        dq = None
        for j in range(T // tc):
            keys = slice(j * tc, (j + 1) * tc)
            kc, vc = k_ref[keys, :], v_ref[keys, :]
            s = lax.dot_general(q, kc, _NT, preferred_element_type=F32)
            p = jnp.exp2(s * c - lse2)
            dp = lax.dot_general(do, vc, _NT, preferred_element_type=F32)
            ds = (p * (dp - delta) * scale).astype(BF16)
            dqj = lax.dot_general(ds, kc, _NN, preferred_element_type=F32)
            dq = dqj if dq is None else dq + dqj
            dvt_ref[:, keys] += lax.dot_general(dot, p.astype(BF16), _NN, preferred_element_type=F32)
            dkt_ref[:, keys] += lax.dot_general(qt, ds, _NN, preferred_element_type=F32)
        dq_ref[...] = dq

    qmap = lambda kv, g, i: (i, kv * GROUP + g)
    return pl.pallas_call(
        body, name="attn_a_bwd", grid=(n_kv, GROUP, T // tq),
        in_specs=[pl.BlockSpec((tq, HEAD_DIM), qmap),
                  pl.BlockSpec((T, HEAD_DIM), lambda kv, g, i: (0, n_q + kv)),
                  pl.BlockSpec((T, HEAD_DIM), lambda kv, g, i: (0, n_q + n_kv + kv)),
                  pl.BlockSpec((tq, HEAD_DIM), qmap),
                  pl.BlockSpec((tq, HEAD_DIM), qmap),
                  pl.BlockSpec((None, tq, 1), lambda kv, g, i: (kv * GROUP + g, i, 0))],
        out_specs=[pl.BlockSpec((tq, HEAD_DIM), qmap),
                   pl.BlockSpec((HEAD_DIM, T), lambda kv, g, i: (kv, 0)),
                   pl.BlockSpec((HEAD_DIM, T), lambda kv, g, i: (kv, 0))],
        out_shape=[jax.ShapeDtypeStruct((T, n_q * HEAD_DIM), F32),
                   jax.ShapeDtypeStruct((n_kv * HEAD_DIM, T), F32),
                   jax.ShapeDtypeStruct((n_kv * HEAD_DIM, T), F32)],
        compiler_params=_params(("parallel", "arbitrary", "arbitrary")),
    )(pb, pb, pb, o_cat, d_o, lse)


def _bucket_index():
    r = np.arange(BLOCK_Q)[:, None]
    j = np.arange(3 * BLOCK_Q)[None, :]
    rel = (j - BLOCK_Q) - r
    nb = N_BUCKETS // 2
    ret = np.where(rel > 0, nb, 0)
    n = np.abs(rel)
    max_exact = nb // 2
    nf = np.maximum(n, 1).astype(np.float32)
    large = max_exact + (np.log(nf / max_exact) / math.log(MAX_DISTANCE / max_exact) * (nb - max_exact)).astype(np.int32)
    large = np.minimum(large, nb - 1)
    return jnp.asarray(ret + np.where(n < max_exact, n, large), jnp.int32)


def _bias_build(idx, table_flat, n_heads, deps=()):
    def body(idx_ref, tab_ref, o_ref):
        h = pl.program_id(0)
        iv = idx_ref[...]
        acc = jnp.zeros(iv.shape, F32)
        for b in range(N_BUCKETS):
            acc = jnp.where(iv == b, tab_ref[b * n_heads + h], acc)
        r = lax.broadcasted_iota(jnp.int32, iv.shape, 0)
        j = lax.broadcasted_iota(jnp.int32, iv.shape, 1)
        o_ref[...] = jnp.where(jnp.abs(j - BLOCK_Q - r) <= WINDOW, acc, NEG_INF)

    return _pcall(
        body, deps, name="bias_build", grid=(n_heads,),
        in_specs=[pl.BlockSpec(idx.shape, lambda h: (0, 0)), pl.BlockSpec(memory_space=pltpu.SMEM)],
        out_specs=pl.BlockSpec((None,) + idx.shape, lambda h: (h, 0, 0)),
        out_shape=jax.ShapeDtypeStruct((n_heads,) + idx.shape, F32),
        compiler_params=_params(("parallel",)),
    )(idx, table_flat)


def _in_sequence(n, T):
    j = lax.broadcasted_iota(jnp.int32, (GROUP * BLOCK_Q, 3 * BLOCK_Q), 1)
    kabs = n * BLOCK_Q + j - BLOCK_Q
    return (kabs >= 0) & (kabs < T)


def _per_head_rows(values):
    head = lax.broadcasted_iota(jnp.int32, (GROUP * BLOCK_Q, 1), 0) // BLOCK_Q
    col = jnp.zeros((GROUP * BLOCK_Q, 1), F32)
    for g, v in enumerate(values):
        col = jnp.where(head == g, v, col)
    return col


def _band_specs(col, nblk, sb):
    return [pl.BlockSpec((BLOCK_Q, HEAD_DIM), lambda kv, i: (jnp.maximum(sb * i - 1, 0), col(kv))),
            pl.BlockSpec((sb * BLOCK_Q, HEAD_DIM), lambda kv, i: (i, col(kv))),
            pl.BlockSpec((BLOCK_Q, HEAD_DIM), lambda kv, i: (jnp.minimum(sb * i + sb, nblk - 1), col(kv)))]


def _head_specs(base, rows):
    return [pl.BlockSpec((rows, HEAD_DIM), functools.partial(lambda kv, i, g: (i, base + kv * GROUP + g), g=g))
            for g in range(GROUP)]


def _attn_b_fwd(pb, bias, sink, o_all, q_off, n_q, n_kv, deps=(), sb=16):
    T = pb.shape[0]
    nblk = T // BLOCK_Q
    sb = min(sb, nblk)
    tq = sb * BLOCK_Q
    scale = HEAD_DIM ** -0.5

    def body(*refs):
        q_refs = refs[0:GROUP]
        k_refs, v_refs = refs[GROUP:GROUP + 3], refs[GROUP + 3:GROUP + 6]
        bias_ref, sink_ref, o_ref, lse_ref = refs[GROUP + 6:]
        kv, i = pl.program_id(0), pl.program_id(1)
        kb = jnp.concatenate([r[...] for r in k_refs], axis=0)
        vb = jnp.concatenate([r[...] for r in v_refs], axis=0)
        bias_all = bias_ref[...].reshape(GROUP * BLOCK_Q, 3 * BLOCK_Q)
        sk = _per_head_rows([sink_ref[kv * GROUP + g] for g in range(GROUP)])
        for b in range(sb):
            rows = slice(b * BLOCK_Q, (b + 1) * BLOCK_Q)
            kw, vw = kb[b * BLOCK_Q:(b + 3) * BLOCK_Q], vb[b * BLOCK_Q:(b + 3) * BLOCK_Q]
            q = jnp.concatenate([r[rows, :] for r in q_refs], axis=0)
            s = lax.dot_general(q, kw, _NT, preferred_element_type=F32) * scale + bias_all
            if b == 0 or b == sb - 1:
                s = jnp.where(_in_sequence(i * sb + b, T), s, NEG_INF)
            m = jnp.maximum(jnp.max(s, axis=-1, keepdims=True), sk)
            p = jnp.exp(s - m)
            l = jnp.sum(p, axis=-1, keepdims=True) + jnp.exp(sk - m)
            o = (lax.dot_general(p.astype(BF16), vw, _NN, preferred_element_type=F32) / l).astype(BF16)
            lse = m + jnp.log(l)
            for g in range(GROUP):
                head = slice(g * BLOCK_Q, (g + 1) * BLOCK_Q)
                o_ref[rows, g * HEAD_DIM:(g + 1) * HEAD_DIM] = o[head]
                lse_ref[g, rows, :] = lse[head]

    first_group = o_all.shape[1] // (GROUP * HEAD_DIM) - n_kv
    return _pcall(
        body, deps, into=(o_all, 0), name="attn_b_fwd", grid=(n_kv, nblk // sb),
        in_specs=[*_head_specs(q_off, tq),
                  *_band_specs(lambda kv: q_off + n_q + kv, nblk, sb),
                  *_band_specs(lambda kv: q_off + n_q + n_kv + kv, nblk, sb),
                  pl.BlockSpec((GROUP, BLOCK_Q, 3 * BLOCK_Q), lambda kv, i: (kv, 0, 0)),
                  pl.BlockSpec(memory_space=pltpu.SMEM)],
        out_specs=[pl.BlockSpec((tq, GROUP * HEAD_DIM), lambda kv, i: (i, first_group + kv)),
                   pl.BlockSpec((GROUP, tq, 1), lambda kv, i: (kv, i, 0))],
        out_shape=[jax.ShapeDtypeStruct(o_all.shape, BF16), jax.ShapeDtypeStruct((n_q, T, 1), F32)],
        compiler_params=_params(("parallel", "parallel")),
    )(*([pb] * (GROUP + 6)), bias, sink)


def _attn_b_bwd(pb, o_cat, d_o, lse, bias, sink, q_off, n_q, n_kv, o_off, deps=(), sb=16):
    T = pb.shape[0]
    nblk = T // BLOCK_Q
    sb = min(sb, nblk)
    tq = sb * BLOCK_Q
    scale = HEAD_DIM ** -0.5

    def body(*refs):
        q_refs = refs[0:GROUP]
        k_refs, v_refs = refs[GROUP:GROUP + 3], refs[GROUP + 3:GROUP + 6]
        o_refs, do_refs = refs[GROUP + 6:2 * GROUP + 6], refs[2 * GROUP + 6:3 * GROUP + 6]
        lse_ref, bias_ref, sink_ref, dq_ref, dk_ref, dv_ref, dbias_ref, dsink_ref, dkb_ref, dvb_ref = refs[3 * GROUP + 6:]
        kv, i = pl.program_id(0), pl.program_id(1)
        first = i == 0

        @pl.when(first)
        def _():
            dk_ref[...] = jnp.zeros(dk_ref.shape, F32)
            dv_ref[...] = jnp.zeros(dv_ref.shape, F32)
            dbias_ref[...] = jnp.zeros(dbias_ref.shape, F32)

        kb = jnp.concatenate([r[...] for r in k_refs], axis=0)
        vb = jnp.concatenate([r[...] for r in v_refs], axis=0)
        dkb_ref[...] = jnp.zeros(dkb_ref.shape, F32)
        dvb_ref[...] = jnp.zeros(dvb_ref.shape, F32)
        row = lax.broadcasted_iota(jnp.int32, (SUBLANES, LANES), 0)
        dsink = jnp.zeros((SUBLANES, LANES), F32)
        bias_all = bias_ref[...].reshape(GROUP * BLOCK_Q, 3 * BLOCK_Q)
        sk = _per_head_rows([sink_ref[kv * GROUP + g] for g in range(GROUP)])
        for b in range(sb):
            rows = slice(b * BLOCK_Q, (b + 1) * BLOCK_Q)
            win = slice(b * BLOCK_Q, (b + 3) * BLOCK_Q)
            kw, vw = kb[win], vb[win]
            q = jnp.concatenate([r[rows, :] for r in q_refs], axis=0)
            do = jnp.concatenate([r[rows, :] for r in do_refs], axis=0)
            o = jnp.concatenate([r[rows, :] for r in o_refs], axis=0)
            lse = jnp.concatenate([lse_ref[g, rows, :] for g in range(GROUP)], axis=0)
            delta = jnp.sum(do.astype(F32) * o.astype(F32), axis=-1, keepdims=True)
            s = lax.dot_general(q, kw, _NT, preferred_element_type=F32) * scale + bias_all
            if b == 0 or b == sb - 1:
                s = jnp.where(_in_sequence(i * sb + b, T), s, NEG_INF)
            p = jnp.exp(s - lse)
            dp = lax.dot_general(do, vw, _NT, preferred_element_type=F32)
            ds = p * (dp - delta)
            dbias_ref[...] += ds.reshape(GROUP, BLOCK_Q, 3 * BLOCK_Q)
            sunk = jnp.exp(sk - lse) * delta
            for g in range(GROUP):
                dsink = dsink + jnp.where(row == g, -jnp.sum(sunk[g * BLOCK_Q:(g + 1) * BLOCK_Q]), 0.0)
            dsb = (ds * scale).astype(BF16)
            dq = lax.dot_general(dsb, kw, _NN, preferred_element_type=F32).astype(BF16)
            for g in range(GROUP):
                dq_ref[rows, g * HEAD_DIM:(g + 1) * HEAD_DIM] = dq[g * BLOCK_Q:(g + 1) * BLOCK_Q]
            dkb_ref[win, :] += lax.dot_general(dsb, q, _TN, preferred_element_type=F32)
            dvb_ref[win, :] += lax.dot_general(p.astype(BF16), do, _TN, preferred_element_type=F32)
        _accumulate(dsink_ref, dsink, first)

        before = pl.ds(pl.multiple_of(jnp.maximum(sb * i - 1, 0) * BLOCK_Q, BLOCK_Q), BLOCK_Q)
        own = pl.ds(pl.multiple_of(i * tq, BLOCK_Q), tq)
        after = pl.ds(pl.multiple_of(jnp.minimum(sb * i + sb, nblk - 1) * BLOCK_Q, BLOCK_Q), BLOCK_Q)
        for acc_ref, band_ref in ((dk_ref, dkb_ref), (dv_ref, dvb_ref)):
            acc_ref[before, :] += band_ref[0:BLOCK_Q, :]
            acc_ref[own, :] += band_ref[BLOCK_Q:BLOCK_Q + tq, :]
            acc_ref[after, :] += band_ref[BLOCK_Q + tq:, :]

    return _pcall(
        body, deps, name="attn_b_bwd", grid=(n_kv, nblk // sb),
        in_specs=[*_head_specs(q_off, tq),
                  *_band_specs(lambda kv: q_off + n_q + kv, nblk, sb),
                  *_band_specs(lambda kv: q_off + n_q + n_kv + kv, nblk, sb),
                  *_head_specs(o_off, tq), *_head_specs(o_off, tq),
                  pl.BlockSpec((GROUP, tq, 1), lambda kv, i: (kv, i, 0)),
                  pl.BlockSpec((GROUP, BLOCK_Q, 3 * BLOCK_Q), lambda kv, i: (kv, 0, 0)),
                  pl.BlockSpec(memory_space=pltpu.SMEM)],
        out_specs=[pl.BlockSpec((tq, GROUP * HEAD_DIM), lambda kv, i: (i, kv)),
                   pl.BlockSpec((T, HEAD_DIM), lambda kv, i: (0, kv)),
                   pl.BlockSpec((T, HEAD_DIM), lambda kv, i: (0, kv)),
                   pl.BlockSpec((GROUP, BLOCK_Q, 3 * BLOCK_Q), lambda kv, i: (kv, 0, 0)),
                   pl.BlockSpec((None, SUBLANES, LANES), lambda kv, i: (kv, 0, 0))],
        out_shape=[jax.ShapeDtypeStruct((T, n_q * HEAD_DIM), BF16),
                   jax.ShapeDtypeStruct((T, n_kv * HEAD_DIM), F32),
                   jax.ShapeDtypeStruct((T, n_kv * HEAD_DIM), F32),
                   jax.ShapeDtypeStruct((n_q, BLOCK_Q, 3 * BLOCK_Q), F32),
                   jax.ShapeDtypeStruct((n_kv, SUBLANES, LANES), F32)],
        scratch_shapes=[pltpu.VMEM((tq + 2 * BLOCK_Q, HEAD_DIM), F32), pltpu.VMEM((tq + 2 * BLOCK_Q, HEAD_DIM), F32)],
        compiler_params=_params(("parallel", "arbitrary")),
    )(*([pb] * (GROUP + 6)), *([o_cat] * GROUP), *([d_o] * GROUP), lse, bias, sink)


def _table_grads(dbias, dsink_raw, idx):
    n_heads = dbias.shape[0]
    n_kv = dsink_raw.shape[0]

    def body(db_ref, ds_ref, idx_ref, dt_ref, dsk_ref):
        iv = idx_ref[...]
        row = lax.broadcasted_iota(jnp.int32, (SUBLANES, LANES), 0)
        lane = lax.broadcasted_iota(jnp.int32, (SUBLANES, LANES), 1)
        dsk = jnp.zeros((SUBLANES, LANES), F32)
        for h in range(n_heads):
            d = db_ref[h]
            acc = jnp.zeros((SUBLANES, LANES), F32)
            for b in range(N_BUCKETS):
                acc = jnp.where((row == 0) & (lane == b), jnp.sum(jnp.where(iv == b, d, 0.0)), acc)
            dt_ref[:, h * LANES:(h + 1) * LANES] = acc
            raw = ds_ref[h // GROUP]
            val = jnp.sum(jnp.where((row == h % GROUP) & (lane == 0), raw, 0.0))
            dsk = jnp.where((row == 0) & (lane == h), val, dsk)
        dsk_ref[...] = dsk

    return pl.pallas_call(
        body, name="table_grads",
        in_specs=[pl.BlockSpec(memory_space=pltpu.VMEM)] * 3,
        out_specs=[pl.BlockSpec(memory_space=pltpu.VMEM)] * 2,
        out_shape=[jax.ShapeDtypeStruct((SUBLANES, n_heads * LANES), F32),
                   jax.ShapeDtypeStruct((SUBLANES, LANES), F32)],
        compiler_params=pltpu.CompilerParams(vmem_limit_bytes=56 * 1024 * 1024),
    )(dbias, dsink_raw, idx)


def _position():
    x, y, c = lax.axis_index("x"), lax.axis_index("y"), lax.axis_index("c")
    return x, y, c


def _hbm(a):
    return pltpu.with_memory_space_constraint(a, pltpu.HBM)


def _split_start(name, bufs, sem_shapes, issue):
    nb, ns = len(bufs), len(sem_shapes)

    def body(*refs):
        buf_refs = refs[:nb]
        sems = refs[nb:nb + ns]
        token = refs[nb + ns + nb]
        issue(buf_refs, sems)
        token[...] = jnp.zeros(token.shape, F32)

    outs = pl.pallas_call(
        body, name=name,
        in_specs=[_HBM] * nb,
        out_specs=[_SEM] * ns + [_HBM] * nb + [_VMEM],
        out_shape=[pltpu.SemaphoreType.DMA(s) for s in sem_shapes] + [pltpu.HBM(b.shape, b.dtype) for b in bufs]
        + [jax.ShapeDtypeStruct((SUBLANES, LANES), F32)],
        input_output_aliases={i: ns + i for i in range(nb)},
        compiler_params=pltpu.CompilerParams(has_side_effects=_EFFECT),
    )(*[_hbm(b) for b in bufs])
    return outs[:ns], outs[ns:ns + nb], outs[-1]


def _split_wait(name, bufs, send, recv, counts, size_of, after):
    nb = len(bufs)

    def body(*refs):
        buf_refs = refs[:nb]
        send_ref, recv_ref = refs[nb], refs[nb + 1]
        x, y, c = _position()
        for w, n in enumerate(counts):
            ref = size_of(buf_refs, w)
            for k in range(n):
                s = sum(counts[:w]) + k
                cp = pltpu.make_async_remote_copy(
                    src_ref=ref, dst_ref=ref, send_sem=send_ref.at[s], recv_sem=recv_ref.at[s],
                    device_id=(x, y, c), device_id_type=MESH)
                cp.wait_send()
                cp.wait_recv()

    return pl.pallas_call(
        body, name=name,
        in_specs=[_HBM] * nb + [_SEM, _SEM, _ANY],
        out_specs=[_HBM] * nb,
        out_shape=[pltpu.HBM(b.shape, b.dtype) for b in bufs],
        input_output_aliases={i: i for i in range(nb)},
        compiler_params=pltpu.CompilerParams(has_side_effects=_EFFECT),
    )(*bufs, send, recv, after)


def _block_of(pos):
    return 4 * pos[0] + 2 * pos[1] + pos[2]


def _shard_of(ref, blk, by_cols):
    aligned = (lambda v, a: v) if isinstance(blk, int) else pl.multiple_of
    if by_cols:
        n = ref.shape[1] // N_DEV
        return ref.at[:, pl.ds(aligned(blk * n, LANES), n)]
    r = ref.shape[0] // N_DEV
    return ref.at[pl.ds(aligned(blk * r, SUBLANES), r), :]


def _place_own(name, land, shard, by_cols, tr=256):
    r, n = shard.shape
    tr = _tile(r, tr)
    mine = _block_of(_position()).astype(jnp.int32).reshape(1)

    def body(m_ref, land_ref, s_ref, o_ref):
        o_ref[...] = s_ref[...]

    if by_cols:
        out = pl.BlockSpec((tr, n), lambda i, m_ref: (i, m_ref[0]))
    else:
        out = pl.BlockSpec((tr, n), lambda i, m_ref: (m_ref[0] * (r // tr) + i, 0))
    return pl.pallas_call(
        body, name=name,
        grid_spec=pltpu.PrefetchScalarGridSpec(
            num_scalar_prefetch=1, grid=(r // tr,),
            in_specs=[_ANY, pl.BlockSpec((tr, n), lambda i, m_ref: (i, 0))], out_specs=out),
        out_shape=jax.ShapeDtypeStruct(land.shape, land.dtype),
        input_output_aliases={1: 0},
        compiler_params=_params(("parallel",)),
    )(mine, land, shard)


def _gather_start(name, shards, by_cols, groups, after=None):
    nw = len(shards)
    lands = [lax.empty((s.shape[0], s.shape[1] * N_DEV) if cols else (s.shape[0] * N_DEV, s.shape[1]), s.dtype)
             for s, cols in zip(shards, by_cols)]
    order = [] if after is None else [after]

    def issue(bufs, sems):
        x, y, c = _position()
        peers = [(x, y, 1 - c), (1 - x, y, c), (x, 1 - y, c), (1 - x, 1 - y, c)]
        for gi, grp in enumerate(groups):
            for wi, w in enumerate(grp):
                for k, peer in enumerate(peers):
                    pltpu.make_async_remote_copy(
                        src_ref=bufs[w], dst_ref=_shard_of(bufs[nw + w], _block_of((x, y, c)), by_cols[w]),
                        send_sem=sems[2 * gi].at[4 * wi + k], recv_sem=sems[2 * gi + 1].at[4 * wi + k],
                        device_id=peer, device_id_type=MESH).start()

    sem_shapes = [(4 * len(g),) for g in groups for _ in range(2)]
    sems, thru, token = _split_start(name, list(shards) + lands + order, sem_shapes, issue)
    return sems, thru[:nw], thru[nw:2 * nw], token


def _gather_forward(name, lands, by_cols):
    nw = len(lands)

    def issue(land, sems):
        x, y, c = _position()
        for w in range(nw):
            for k, chip in enumerate([(1 - x, y), (x, 1 - y), (1 - x, 1 - y)]):
                blk = _shard_of(land[w], _block_of((*chip, c)), by_cols[w])
                pltpu.make_async_remote_copy(
                    src_ref=blk, dst_ref=blk, send_sem=sems[0].at[3 * w + k], recv_sem=sems[1].at[3 * w + k],
                    device_id=(x, y, 1 - c), device_id_type=MESH).start()

    return _split_start(name, lands, [(3 * nw,), (3 * nw,)], issue)


def _first_block(bufs, w, offset=0):
    return bufs[offset + w].at[0]


_PEER_FLIPS = ((0, 0, 1), (1, 0, 0), (1, 0, 1), (0, 1, 0), (0, 1, 1), (1, 1, 0), (1, 1, 1))


def _scatter_start(name, grads, by_cols):
    nw = len(grads)
    lands = []
    for g, cols in zip(grads, by_cols):
        shard = (g.shape[0], g.shape[1] // N_DEV) if cols else (g.shape[0] // N_DEV, g.shape[1])
        lands.append(lax.empty((N_DEV,) + shard, g.dtype))

    def issue(bufs, sems):
        x, y, c = _position()
        flip = lambda v, f: 1 - v if f else v
        for w in range(nw):
            for k, (fx, fy, fc) in enumerate(_PEER_FLIPS):
                peer = (flip(x, fx), flip(y, fy), flip(c, fc))
                pltpu.make_async_remote_copy(
                    src_ref=_shard_of(bufs[w], _block_of(peer), by_cols[w]), dst_ref=bufs[nw + w].at[_block_of((x, y, c))],
                    send_sem=sems[0].at[7 * w + k], recv_sem=sems[1].at[7 * w + k],
                    device_id=peer, device_id_type=MESH).start()

    return _split_start(name, list(grads) + lands, [(7 * nw,), (7 * nw,)], issue)


def _adam(w, g, m, v):
    m = ADAM_B1 * m + (1.0 - ADAM_B1) * g
    v = ADAM_B2 * v + (1.0 - ADAM_B2) * (g * g)
    m_hat = m / (1.0 - ADAM_B1 ** ADAM_STEP)
    v_hat = v / (1.0 - ADAM_B2 ** ADAM_STEP)
    delta = -ADAM_LR * (m_hat / (jnp.sqrt(v_hat) + ADAM_EPS) + ADAM_WD * w)
    return delta, m, v


def _sum_adam(name, landed, grad, by_cols, w, m, v, tr=256):
    R, C = w.shape
    tr = _tile(R, tr)
    mine = _block_of(_position()).astype(jnp.int32).reshape(1)

    def body(me_ref, l_ref, own_ref, w_ref, m_ref, v_ref, g_ref, d_ref, nm_ref, nv_ref):
        own = own_ref[...].astype(F32)
        g = None
        for d in range(N_DEV):
            part = jnp.where(me_ref[0] == d, own, l_ref[d].astype(F32))
            g = part if g is None else g + part
        g_ref[...] = g
        d_ref[...], nm_ref[...], nv_ref[...] = _adam(w_ref[...], g, m_ref[...], v_ref[...])

    tile = pl.BlockSpec((tr, C), lambda i, me_ref: (i, 0))
    if by_cols:
        own = pl.BlockSpec((tr, C), lambda i, me_ref: (i, me_ref[0]))
    else:
        own = pl.BlockSpec((tr, C), lambda i, me_ref: (me_ref[0] * (R // tr) + i, 0))
    return pl.pallas_call(
        body, name=name,
        grid_spec=pltpu.PrefetchScalarGridSpec(
            num_scalar_prefetch=1, grid=(R // tr,),
            in_specs=[pl.BlockSpec((N_DEV, tr, C), lambda i, me_ref: (0, i, 0)), own, tile, tile, tile],
            out_specs=[tile] * 4),
        out_shape=[jax.ShapeDtypeStruct((R, C), F32)] * 4,
        compiler_params=_params(("parallel",)),
    )(mine, landed, grad, w, m, v)


def _small_all_reduce(parts, deps=()):
    W = parts.shape[1]

    def body(p_ref, o_ref, slots, send_sems, recv_sems):
        x, y, c = _position()
        me = 4 * x + 2 * y + c
        slots[me] = jnp.sum(p_ref[...], axis=0, keepdims=True)
        peers = [(x, y, 1 - c), (1 - x, y, c), (1 - x, y, 1 - c), (x, 1 - y, c), (x, 1 - y, 1 - c),
                 (1 - x, 1 - y, c), (1 - x, 1 - y, 1 - c)]
        copies = []
        for k, peer in enumerate(peers):
            cp = pltpu.make_async_remote_copy(
                src_ref=slots.at[me], dst_ref=slots.at[me], send_sem=send_sems.at[k], recv_sem=recv_sems.at[k],
                device_id=peer, device_id_type=MESH)
            cp.start()
            copies.append(cp)
        for cp in copies:
            cp.wait()
        total = slots[0]
        for d in range(1, N_DEV):
            total = total + slots[d]
        o_ref[...] = total

    return _pcall(
        body, deps, name="small_all_reduce",
        in_specs=[pl.BlockSpec(memory_space=pltpu.VMEM)], out_specs=pl.BlockSpec(memory_space=pltpu.VMEM),
        out_shape=jax.ShapeDtypeStruct((1, W), F32),
        scratch_shapes=[pltpu.VMEM((N_DEV, 1, W), F32), pltpu.SemaphoreType.DMA((7,)), pltpu.SemaphoreType.DMA((7,))],
    )(parts)


def _adam_small(w, g, m, v):
    def body(w_ref, g_ref, m_ref, v_ref, d_ref, nm_ref, nv_ref):
        d_ref[...], nm_ref[...], nv_ref[...] = _adam(w_ref[...], g_ref[...], m_ref[...], v_ref[...])

    return pl.pallas_call(
        body, name="adam_small",
        in_specs=[pl.BlockSpec(memory_space=pltpu.VMEM)] * 4, out_specs=[pl.BlockSpec(memory_space=pltpu.VMEM)] * 3,
        out_shape=[jax.ShapeDtypeStruct(w.shape, F32)] * 3,
    )(w, g, m, v)


_GATHER_GROUPS = (("w_in",), ("w_out", "w_up", "ple_w"), ("w_down", "w_gate"))
_COL_SHARDED = ("w_in", "w_up", "ple_w")


class _MeshComm:
    def __init__(self, w, mom, var):
        self.w, self.mom, self.var = w, mom, var
        self.out = {}
        self._scatters = {}

    def gather_begin(self):
        self._groups = {}
        token = None
        for tag, first, group_list in (("gather_start0", 0, _GATHER_GROUPS[:1]), ("gather_start1", 1, _GATHER_GROUPS[1:])):
            names = [n for g in group_list for n in g]
            idx = {n: i for i, n in enumerate(names)}
            by_cols = [n in _COL_SHARDED for n in names]
            sems, src, lands, token = _gather_start(tag, [self.w[n].astype(BF16) for n in names], by_cols,
                                                    [[idx[n] for n in g] for g in group_list], token)
            lands = [_place_own("place_" + n, land, s, cols) for n, land, s, cols in zip(names, lands, src, by_cols)]
            for k, g in enumerate(group_list):
                self._groups[first + k] = (sems[2 * k], sems[2 * k + 1], [src[idx[n]] for n in g],
                                           [lands[idx[n]] for n in g])
        return token

    @staticmethod
    def _shard_size(names, offset):
        return lambda bufs, w: _shard_of(bufs[offset + w], 0, names[w] in _COL_SHARDED)

    def gather_arrive(self, gi, after):
        names = _GATHER_GROUPS[gi]
        send, recv, src, lands = self._groups[gi]
        out = _split_wait("gather_arrive%d" % gi, src + lands, send, recv, [4] * len(names),
                          self._shard_size(names, len(names)), after)
        self._arrived = out[len(names):]

    def gather_forward(self, gi):
        by_cols = [n in _COL_SHARDED for n in _GATHER_GROUPS[gi]]
        self._fsems, self._fthru, token = _gather_forward("gather_forward%d" % gi, self._arrived, by_cols)
        return token

    def gather_finish(self, gi, after):
        names = _GATHER_GROUPS[gi]
        out = _split_wait("gather_finish%d" % gi, self._fthru, self._fsems[0], self._fsems[1], [3] * len(names),
                          self._shard_size(names, 0), after)
        return dict(zip(names, out))

    def reduce_begin(self, key, grads):
        names = list(grads)
        sems, thru, token = _scatter_start("scatter_start_" + key, [grads[n] for n in names],
                                           [n in _COL_SHARDED for n in names])
        self._scatters[key] = (names, sems, thru)
        return token

    def reduce_finish(self, key, after):
        names, sems, thru = self._scatters[key]
        nw = len(names)
        out = _split_wait("scatter_wait_" + key, thru, sems[0], sems[1], [N_DEV - 1] * nw,
                          functools.partial(_first_block, offset=nw), after)
        for i, n in enumerate(names):
            self.out[n] = _sum_adam("adam_" + n, out[nw + i], out[i], n in _COL_SHARDED, self.w[n], self.mom[n],
                                    self.var[n])


def _step(x, p, target, gains, comm):
    T, D = x.shape
    n_q = D // (2 * HEAD_DIM)
    n_kv = n_q // GROUP
    cos, sin = _rope_tables(T)
    idx = _bucket_index()

    t = comm.gather_begin()
    u = _rms_fwd("norm_attn", x, gains["attn_norm_g"], deps=(t,))
    comm.gather_arrive(0, u)
    t = comm.gather_forward(0)
    bias = _bias_build(idx, gains["rel_bias_table"].reshape(-1), n_q, deps=(t,))
    full = comm.gather_finish(0, bias)
    proj_a, pb = _in_proj(u, full["w_in"], cos, sin, gains["q_norm_g"], gains["k_norm_g"], n_q + n_kv)
    o_a, lse_a = _attn_a_fwd(pb, n_q, n_kv, 2 * n_q)
    comm.gather_arrive(1, lse_a)
    t = comm.gather_forward(1)
    sink = gains["sink_logits"].reshape(-1)
    b_off = n_q + 2 * n_kv
    o_cat, lse_b = _attn_b_fwd(pb, bias, sink, o_a, b_off, n_q, n_kv, deps=(t,))
    full.update(comm.gather_finish(1, lse_b))
    h1, m_in = _mm_nn_rms("out_proj", o_cat, full["w_out"], x, gains["mlp_norm_g"])

    def up_epilogue(acc, extra, outs):
        outs[0][...] = acc.astype(BF16)
        r = jnp.maximum(acc, 0.0)
        outs[1][...] = (r * r).astype(BF16)

    a_act, f_act = _mm_nn("up_proj", m_in, full["w_up"], epilogue=up_epilogue, out_dtypes=[BF16, BF16], tn=2048)
    comm.gather_arrive(2, f_act)
    t = comm.gather_forward(2)
    p_b = p.astype(BF16)
    pe = _mm_nn("ple_proj", p_b, full["ple_w"], deps=(t,))
    full.update(comm.gather_finish(2, pe))
    h2 = _mm_nn("down_proj", f_act, full["w_down"], epilogue=_store_add, extras=(h1,), tn=256)
    gn = _rms_fwd("norm_gate", h2, gains["gate_norm_g"])

    dh3, dz, dpe, dg_final, dg_ple, loss_part = _gate_tail(gn, full["w_gate"], h2, pe, target, gains["ple_norm_g"],
                                                           gains["final_norm_g"])
    gw_gate = _mm_tn("grad_w_gate", gn, dz)
    gw_ple = _mm_tn("grad_ple_w", p_b, dpe)
    dh2, dh2_b, dg_gate = _mm_nt_rms_bwd("d_gate_in", dz, full["w_gate"], h2, gains["gate_norm_g"], dh3)
    gw_down = _mm_tn("grad_w_down", f_act, dh2_b)
    t = comm.reduce_begin("b", dict(w_gate=gw_gate, ple_w=gw_ple, w_down=gw_down))

    def act_bwd(acc, extra, outs):
        outs[0][...] = (acc * (2.0 * jnp.maximum(extra[0][...].astype(F32), 0.0))).astype(BF16)

    da = _mm_nt("d_act", dh2_b, full["w_down"], out_dtype=BF16, epilogue=act_bwd, extras=(a_act,), tn=2048, deps=(t,))
    gw_up = _mm_tn("grad_w_up", m_in, da)
    dm = _mm_nt("d_mlp_in", da, full["w_up"], out_dtype=BF16, tn=256)
    dh1, dh1_b, dg_mlp = _rms_bwd("norm_mlp_bwd", dm, h1, gains["mlp_norm_g"], dh2)
    gw_out = _mm_tn("grad_w_out", o_cat, dh1_b)
    t = comm.reduce_begin("d", dict(w_up=gw_up, w_out=gw_out))
    d_o = _mm_nt("d_attn_out", dh1_b, full["w_out"], out_dtype=BF16, deps=(t,))
    dqa, dka_t, dva_t = _attn_a_bwd(pb, o_cat, d_o, lse_a, n_q, n_kv)
    dqb, dkb, dvb, dbias, dsink_raw = _attn_b_bwd(pb, o_cat, d_o, lse_b, bias, sink, b_off, n_q, n_kv, n_q)
    dtable, dsink = _table_grads(dbias, dsink_raw, idx)
    dproj, dg_q, dg_k = _dproj(proj_a, dqa, dka_t, dva_t, dqb, dkb, dvb, cos, sin, gains["q_norm_g"], gains["k_norm_g"])
    gw_in = _mm_tn("grad_w_in", u, dproj)
    t = comm.reduce_begin("e", dict(w_in=gw_in))
    dx, dg_attn = _mm_nt_rms_bwd("d_attn_in", dproj, full["w_in"], x, gains["attn_norm_g"], dh1, with_bf16=False,
                                 deps=(t,))
    for key in "bd":
        comm.reduce_finish(key, dx)

    parts = jnp.concatenate([dg_attn, dg_mlp, dg_ple, dg_gate, dg_final, dg_q, dg_k, dtable, dsink, loss_part], axis=1)
    return dx, parts


_SHARDED = ("w_in", "w_out", "w_up", "w_down", "ple_w", "w_gate")
_VECTORS = ("attn_norm_g", "mlp_norm_g", "ple_norm_g", "gate_norm_g", "final_norm_g")
_ORDER = ("attn_norm_g", "w_in", "q_norm_g", "k_norm_g", "sink_logits", "w_out", "mlp_norm_g", "w_up", "w_down",
          "ple_w", "ple_norm_g", "gate_norm_g", "w_gate", "rel_bias_table", "final_norm_g")


def _pack_small(vals, n_heads):
    lane_pad = lambda v: jnp.pad(v, ((0, 0), (0, LANES - v.shape[1])))
    table = lane_pad(vals["rel_bias_table"].T).reshape(1, n_heads * LANES)
    return jnp.concatenate(
        [vals[n].reshape(1, -1) for n in _VECTORS] + [vals["q_norm_g"], vals["k_norm_g"], table,
                                                      lane_pad(vals["sink_logits"]), jnp.zeros((1, LANES), F32)], axis=1)


def _unpack_small(row, like, n_heads):
    out, off = {}, 0
    for n in _VECTORS:
        out[n] = row[:, off:off + like[n].size].reshape(like[n].shape)
        off += like[n].size
    for n in ("q_norm_g", "k_norm_g"):
        out[n] = row[:, off:off + LANES]
        off += LANES
    out["rel_bias_table"] = row[:, off:off + n_heads * LANES].reshape(n_heads, LANES)[:, :N_BUCKETS].T
    off += n_heads * LANES
    out["sink_logits"] = row[:, off:off + n_heads]
    off += LANES
    return out, row[0, off]


def kernel(x, p, attn_norm_g, w_in, q_norm_g, k_norm_g, sink_logits, w_out, mlp_norm_g, w_up, w_down, ple_w, ple_norm_g, gate_norm_g, w_gate, rel_bias_table, final_norm_g, loss_target, m_attn_norm_g, m_w_in, m_q_norm_g, m_k_norm_g, m_sink_logits, m_w_out, m_mlp_norm_g, m_w_up, m_w_down, m_ple_w, m_ple_norm_g, m_gate_norm_g, m_w_gate, m_rel_bias_table, m_final_norm_g, v_attn_norm_g, v_w_in, v_q_norm_g, v_k_norm_g, v_sink_logits, v_w_out, v_mlp_norm_g, v_w_up, v_w_down, v_ple_w, v_ple_norm_g, v_gate_norm_g, v_w_gate, v_rel_bias_table, v_final_norm_g):
    w = dict(attn_norm_g=attn_norm_g, w_in=w_in[0], q_norm_g=q_norm_g, k_norm_g=k_norm_g, sink_logits=sink_logits,
             w_out=w_out[0], mlp_norm_g=mlp_norm_g, w_up=w_up[0], w_down=w_down[0], ple_w=ple_w[0],
             ple_norm_g=ple_norm_g, gate_norm_g=gate_norm_g, w_gate=w_gate[0], rel_bias_table=rel_bias_table,
             final_norm_g=final_norm_g)
    mom = dict(attn_norm_g=m_attn_norm_g, w_in=m_w_in[0], q_norm_g=m_q_norm_g, k_norm_g=m_k_norm_g,
               sink_logits=m_sink_logits, w_out=m_w_out[0], mlp_norm_g=m_mlp_norm_g, w_up=m_w_up[0],
               w_down=m_w_down[0], ple_w=m_ple_w[0], ple_norm_g=m_ple_norm_g, gate_norm_g=m_gate_norm_g,
               w_gate=m_w_gate[0], rel_bias_table=m_rel_bias_table, final_norm_g=m_final_norm_g)
    var = dict(attn_norm_g=v_attn_norm_g, w_in=v_w_in[0], q_norm_g=v_q_norm_g, k_norm_g=v_k_norm_g,
               sink_logits=v_sink_logits, w_out=v_w_out[0], mlp_norm_g=v_mlp_norm_g, w_up=v_w_up[0],
               w_down=v_w_down[0], ple_w=v_ple_w[0], ple_norm_g=v_ple_norm_g, gate_norm_g=v_gate_norm_g,
               w_gate=v_w_gate[0], rel_bias_table=v_rel_bias_table, final_norm_g=v_final_norm_g)
    D = x.shape[-1]
    n_heads = D // (2 * HEAD_DIM)

    gains = {n: w[n] for n in w if n not in _SHARDED}
    gains["final_norm_g"] = final_norm_g.reshape(1, -1)

    comm = _MeshComm(w, mom, var)
    dx, parts = _step(x[0], p[0, 0], loss_target[0], gains, comm)

    small_g = _small_all_reduce(parts, deps=[comm.out[n][0] for n in comm.out])
    comm.reduce_finish("e", small_g)

    g_out, d_out, m_out, v_out = {}, {}, {}, {}
    for n in _SHARDED:
        g, d, nm, nv = comm.out[n]
        g_out[n], d_out[n], m_out[n], v_out[n] = g[None], d[None], nm[None], nv[None]

    small = {n: v for n, v in w.items() if n not in _SHARDED}
    pack = lambda vals: _pack_small({n: vals[n] for n in small}, n_heads)
    sd, sm, sv = _adam_small(pack(w), small_g, pack(mom), pack(var))
    sg, loss = _unpack_small(small_g, small, n_heads)
    g_out.update(sg)
    for dst, row in ((d_out, sd), (m_out, sm), (v_out, sv)):
        dst.update(_unpack_small(row, small, n_heads)[0])

    return (loss, dx[None], *[g_out[n] for n in _ORDER], *[d_out[n] for n in _ORDER],
            *[m_out[n] for n in _ORDER], *[v_out[n] for n in _ORDER])
```

```python
import functools
import math

import numpy as np
import jax
import jax.numpy as jnp
from jax import lax
from jax.experimental import pallas as pl
from jax.experimental.pallas import tpu as pltpu

F32 = jnp.float32
BF16 = jnp.bfloat16

N_DEV = 8
N_CHIP = 4
HEAD_DIM = 128
GROUP = 4
GRID_W = 64
WINDOW = 128
BLOCK_Q = 128
N_BUCKETS = 32
MAX_DISTANCE = 128
ROPE_THETA = 10000.0
EPS = 1e-6
NEG_INF = -1e30
ADAM_LR = 0.001
ADAM_B1 = 0.9
ADAM_B2 = 0.999
ADAM_EPS = 1e-08
ADAM_WD = 0.01
ADAM_STEP = 10
LOG2E = math.log2(math.e)
LANES = 128
SUBLANES = 8
VMEM_LIMIT_BYTES = 56 * 1024 * 1024
MESH = pl.DeviceIdType.MESH

_NT = (((1,), (1,)), ((), ()))
_NN = (((1,), (0,)), ((), ()))
_TN = (((0,), (0,)), ((), ()))


def _tile(dim, pref):
    return pref if dim % pref == 0 else dim


def _params(sem=None):
    return pltpu.CompilerParams(dimension_semantics=sem, vmem_limit_bytes=VMEM_LIMIT_BYTES)


_HBM = pl.BlockSpec(memory_space=pltpu.HBM)
_SEM = pl.BlockSpec(memory_space=pltpu.SEMAPHORE)
_ANY = pl.BlockSpec(memory_space=pl.ANY)
_VMEM = pl.BlockSpec(memory_space=pltpu.VMEM)
_EFFECT = pltpu.SideEffectType.DATAFLOW_SIDE_EFFECTING


def _pcall(body, deps=(), *, in_specs, into=None, **kw):
    deps = [d for d in deps if d is not None]
    nd = len(deps)
    if into is not None:
        deps = [into[0]] + deps
        nd += 1
        kw["input_output_aliases"] = {0: into[1]}

    def wrapped(*refs):
        body(*refs[nd:])

    call = pl.pallas_call(wrapped, in_specs=[_ANY] * nd + list(in_specs), **kw)
    return lambda *args: call(*deps, *args)


def _mm(name, a, b, dims, grid, a_spec, b_spec, out_shape, out_specs, acc_shape, epilogue,
        extras=(), extra_specs=(), deps=(), semantics=("parallel", "parallel", "arbitrary")):
    nk = grid[2]
    n_extra = len(extras)

    def body(*refs):
        a_ref, b_ref = refs[0], refs[1]
        extra = refs[2:2 + n_extra]
        outs = refs[2 + n_extra:-1]
        acc = refs[-1]
        part = lax.dot_general(a_ref[...], b_ref[...], dims, preferred_element_type=F32)
        if nk == 1:
            epilogue(part, extra, outs)
        else:
            k = pl.program_id(2)

            @pl.when(k == 0)
            def _():
                acc[...] = part

            @pl.when(k > 0)
            def _():
                acc[...] += part

            @pl.when(k == nk - 1)
            def _():
                epilogue(acc[...], extra, outs)

    return _pcall(
        body, deps, name=name, grid=grid,
        in_specs=[a_spec, b_spec, *extra_specs],
        out_specs=out_specs, out_shape=out_shape,
        scratch_shapes=[pltpu.VMEM(acc_shape if nk > 1 else (SUBLANES, LANES), F32)],
        compiler_params=_params(semantics),
    )(a, b, *extras)


def _store(dtype):
    def ep(acc, extra, outs):
        outs[0][...] = acc.astype(dtype)
    return ep


def _store_add(acc, extra, outs):
    outs[0][...] = acc + extra[0][...]


def _mm_nn(name, a, b, out_dtype=F32, epilogue=None, extras=(), n_out=1, out_dtypes=None, tm=1024, tn=1024, tk=None,
           deps=()):
    M, K = a.shape
    N = b.shape[1]
    tm, tn, tk = _tile(M, tm), _tile(N, tn), _tile(K, tk or K)
    b_spec = pl.BlockSpec((tk, tn), lambda i, j, k: (k, j))
    grid = (M // tm, N // tn, K // tk)
    o_spec = pl.BlockSpec((tm, tn), lambda i, j, k: (i, j))
    out_dtypes = out_dtypes or [out_dtype] * n_out
    out_shape = [jax.ShapeDtypeStruct((M, N), d) for d in out_dtypes]
    res = _mm(name, a, b, _NN, grid, pl.BlockSpec((tm, tk), lambda i, j, k: (i, k)), b_spec,
              out_shape, [o_spec] * len(out_dtypes), (tm, tn), epilogue or _store(out_dtype),
              extras, [o_spec] * len(extras), deps)
    return res if len(out_dtypes) > 1 else res[0]


def _mm_nt(name, a, b, out_dtype=F32, epilogue=None, extras=(), tm=1024, tn=1024, tk=None, deps=()):
    M, C = a.shape
    N = b.shape[0]
    tm, tn, tk = _tile(M, tm), _tile(N, tn), _tile(C, tk or C)
    b_spec = pl.BlockSpec((tn, tk), lambda i, j, k: (j, k))
    grid = (M // tm, N // tn, C // tk)
    o_spec = pl.BlockSpec((tm, tn), lambda i, j, k: (i, j))
    return _mm(name, a, b, _NT, grid, pl.BlockSpec((tm, tk), lambda i, j, k: (i, k)), b_spec,
               [jax.ShapeDtypeStruct((M, N), out_dtype)], [o_spec], (tm, tn), epilogue or _store(out_dtype),
               extras, [o_spec] * len(extras), deps)[0]


def _mm_tn(name, a, b, out_dtype=BF16, tm=1024, tn=512, tk=None, deps=()):
    T, M = a.shape
    N = b.shape[1]
    tm, tn, tk = _tile(M, tm), _tile(N, tn), _tile(T, tk or T)
    out_shape = jax.ShapeDtypeStruct((M, N), out_dtype)
    o_spec = pl.BlockSpec((tm, tn), lambda i, j, k: (i, j))
    grid = (M // tm, N // tn, T // tk)
    return _mm(name, a, b, _TN, grid, pl.BlockSpec((tk, tm), lambda i, j, k: (k, i)),
               pl.BlockSpec((tk, tn), lambda i, j, k: (k, j)), [out_shape], [o_spec], (tm, tn), _store(out_dtype),
               deps=deps)[0]


def _mean_last(v):
    return jnp.mean(v, axis=-1, keepdims=True)


def _rows_to_sublanes(v):
    r, c = v.shape
    return jnp.sum(v.reshape(r // SUBLANES, SUBLANES, c), axis=0)


def _accumulate(ref, val, first):
    @pl.when(first)
    def _():
        ref[...] = val

    @pl.when(jnp.logical_not(first))
    def _():
        ref[...] += val


def _rms_fwd(name, x, g, tr=256, deps=()):
    T, D = x.shape
    tr = _tile(T, tr)

    def body(x_ref, g_ref, o_ref):
        xv = x_ref[...]
        r = lax.rsqrt(_mean_last(xv * xv) + EPS)
        o_ref[...] = (xv * r * g_ref[...]).astype(BF16)

    row = pl.BlockSpec((tr, D), lambda i: (i, 0))
    return _pcall(
        body, deps, name=name, grid=(T // tr,),
        in_specs=[row, pl.BlockSpec((1, D), lambda i: (0, 0))],
        out_specs=row, out_shape=jax.ShapeDtypeStruct((T, D), BF16),
        compiler_params=_params(("parallel",)),
    )(x, g)


def _rms_bwd(name, dyn, x, g, dres, tr=256, deps=()):
    T, D = x.shape
    tr = _tile(T, tr)

    def body(dy_ref, x_ref, g_ref, dr_ref, dx_ref, dxb_ref, dg_ref):
        xv = x_ref[...]
        r = lax.rsqrt(_mean_last(xv * xv) + EPS)
        xn = xv * r
        dy = dy_ref[...].astype(F32)
        dxn = dy * g_ref[...]
        dx = dr_ref[...] + r * (dxn - xn * _mean_last(dxn * xn))
        dx_ref[...] = dx
        dxb_ref[...] = dx.astype(BF16)
        _accumulate(dg_ref, _rows_to_sublanes(dy * xn), pl.program_id(0) == 0)

    row = pl.BlockSpec((tr, D), lambda i: (i, 0))
    return _pcall(
        body, deps, name=name, grid=(T // tr,),
        in_specs=[row, row, pl.BlockSpec((1, D), lambda i: (0, 0)), row],
        out_specs=[row, row, pl.BlockSpec((SUBLANES, D), lambda i: (0, 0))],
        out_shape=[jax.ShapeDtypeStruct((T, D), F32), jax.ShapeDtypeStruct((T, D), BF16),
                   jax.ShapeDtypeStruct((SUBLANES, D), F32)],
        compiler_params=_params(("arbitrary",)),
    )(dyn, x, g, dres)


def _mm_nn_rms(name, a, b, res, g, tm=512, deps=()):
    M, K = a.shape
    N = b.shape[1]
    tm = _tile(M, tm)

    def epilogue(acc, extra, outs):
        h = acc + extra[0][...]
        outs[0][...] = h
        outs[1][...] = (h * lax.rsqrt(_mean_last(h * h) + EPS) * extra[1][...]).astype(BF16)

    row = pl.BlockSpec((tm, N), lambda i, j, k: (i, 0))
    return _mm(name, a, b, _NN, (M // tm, 1, 1), pl.BlockSpec((tm, K), lambda i, j, k: (i, 0)),
               pl.BlockSpec((K, N), lambda i, j, k: (0, 0)),
               [jax.ShapeDtypeStruct((M, N), F32), jax.ShapeDtypeStruct((M, N), BF16)], [row, row], (tm, N), epilogue,
               (res, g), [row, pl.BlockSpec((1, N), lambda i, j, k: (0, 0))], deps)


def _mm_nt_rms_bwd(name, a, b, x, g, dres, with_bf16=True, tm=256, deps=()):
    M, C = a.shape
    N = b.shape[0]
    tm = _tile(M, tm)

    def epilogue(dy, extra, outs):
        x_ref, dr_ref, g_ref = extra
        xv = x_ref[...]
        r = lax.rsqrt(_mean_last(xv * xv) + EPS)
        xn = xv * r
        dxn = dy * g_ref[...]
        dx = dr_ref[...] + r * (dxn - xn * _mean_last(dxn * xn))
        outs[0][...] = dx
        if with_bf16:
            outs[1][...] = dx.astype(BF16)
        _accumulate(outs[-1], _rows_to_sublanes(dy * xn), pl.program_id(0) == 0)

    row = pl.BlockSpec((tm, N), lambda i, j, k: (i, 0))
    copies = [jax.ShapeDtypeStruct((M, N), F32)] + ([jax.ShapeDtypeStruct((M, N), BF16)] if with_bf16 else [])
    return _mm(name, a, b, _NT, (M // tm, 1, 1), pl.BlockSpec((tm, C), lambda i, j, k: (i, 0)),
               pl.BlockSpec((N, C), lambda i, j, k: (0, 0)),
               copies + [jax.ShapeDtypeStruct((SUBLANES, N), F32)],
               [row] * len(copies) + [pl.BlockSpec((SUBLANES, N), lambda i, j, k: (0, 0))], (tm, N), epilogue,
               (x, dres, g), [row, row, pl.BlockSpec((1, N), lambda i, j, k: (0, 0))], deps,
               semantics=("arbitrary", "arbitrary", "arbitrary"))


def _gate_tail(gn, w_gate, h2, pe, target, g_ple, g_final, tm=256):
    T, D = h2.shape
    tm = _tile(T, tm)

    def epilogue(z, extra, outs):
        h2_ref, pe_ref, t_ref, gp_ref, gf_ref = extra
        dh3_ref, dz_ref, dpe_ref, dgf_ref, dgp_ref, loss_ref = outs
        first = pl.program_id(0) == 0
        pev = pe_ref[...]
        r3 = lax.rsqrt(_mean_last(pev * pev) + EPS)
        en = pev * r3
        e = en * gp_ref[...]
        gate = 1.0 / (1.0 + jnp.exp(-z))
        h3 = h2_ref[...] + gate * e
        r5 = lax.rsqrt(_mean_last(h3 * h3) + EPS)
        hn = h3 * r5
        diff = hn * gf_ref[...] - t_ref[...]
        loss_rows = 0.5 * _mean_last(diff * diff)
        row0 = lax.broadcasted_iota(jnp.int32, (SUBLANES, LANES), 0) == 0
        _accumulate(loss_ref, jnp.where(row0, jnp.sum(loss_rows), 0.0), first)
        dy = diff * (1.0 / D)
        _accumulate(dgf_ref, _rows_to_sublanes(dy * hn), first)
        dhn = dy * gf_ref[...]
        dh3 = r5 * (dhn - hn * _mean_last(dhn * hn))
        dh3_ref[...] = dh3
        dgate = dh3 * e
        de = dh3 * gate
        dz_ref[...] = (dgate * gate * (1.0 - gate)).astype(BF16)
        _accumulate(dgp_ref, _rows_to_sublanes(de * en), first)
        den = de * gp_ref[...]
        dpe_ref[...] = (r3 * (den - en * _mean_last(den * en))).astype(BF16)

    row = pl.BlockSpec((tm, D), lambda i, j, k: (i, 0))
    vec = pl.BlockSpec((1, D), lambda i, j, k: (0, 0))
    part = pl.BlockSpec((SUBLANES, D), lambda i, j, k: (0, 0))
    return _mm("gate_tail", gn, w_gate, _NN, (T // tm, 1, 1), row, pl.BlockSpec(w_gate.shape, lambda i, j, k: (0, 0)),
               [jax.ShapeDtypeStruct((T, D), F32), jax.ShapeDtypeStruct((T, D), BF16),
                jax.ShapeDtypeStruct((T, D), BF16), jax.ShapeDtypeStruct((SUBLANES, D), F32),
                jax.ShapeDtypeStruct((SUBLANES, D), F32), jax.ShapeDtypeStruct((SUBLANES, LANES), F32)],
               [row, row, row, part, part, pl.BlockSpec((SUBLANES, LANES), lambda i, j, k: (0, 0))], (tm, D), epilogue,
               (h2, pe, target, g_ple, g_final), [row, row, row, vec, vec],
               semantics=("arbitrary", "arbitrary", "arbitrary"))


def _rope_tables(T):
    pos = np.arange(T)
    half = HEAD_DIM // 2
    inv = (ROPE_THETA ** (-np.arange(0, half, 2, dtype=np.float32) / half)).astype(np.float32)
    ang_r = (pos // GRID_W).astype(np.float32)[:, None] * inv
    ang_c = (pos % GRID_W).astype(np.float32)[:, None] * inv
    cos = np.concatenate([np.cos(ang_r), np.cos(ang_r), np.cos(ang_c), np.cos(ang_c)], axis=-1)
    sin = np.concatenate([-np.sin(ang_r), np.sin(ang_r), -np.sin(ang_c), np.sin(ang_c)], axis=-1)
    return jnp.asarray(cos, F32), jnp.asarray(sin, F32)


def _swap32(x):
    lane = lax.broadcasted_iota(jnp.int32, x.shape, 1)
    return jnp.where((lane % 64) < 32, pltpu.roll(x, 96, 1), pltpu.roll(x, 32, 1))


def _in_proj(u, w_in, cos, sin, g_q, g_k, n_norm, tm=512):
    T, K = u.shape
    W = w_in.shape[1]
    tm = _tile(T, tm)
    n_q = n_norm * GROUP // (GROUP + 1)
    wa = n_norm * HEAD_DIM

    def epilogue(acc, extra, outs):
        c_ref, s_ref, gq_ref, gk_ref = extra
        raw_ref, o_ref = outs
        c, s = c_ref[...], s_ref[...]
        raw_ref[...] = acc[:, :wa]
        for h in range(n_norm):
            cols = slice(h * HEAD_DIM, (h + 1) * HEAD_DIM)
            xv = acc[:, cols]
            g = gq_ref[...] if h < n_q else gk_ref[...]
            xn = xv * lax.rsqrt(_mean_last(xv * xv) + EPS) * g
            o_ref[:, cols] = (xn * c + _swap32(xn) * s).astype(BF16)
        o_ref[:, wa:] = acc[:, wa:].astype(BF16)

    tab = pl.BlockSpec((tm, HEAD_DIM), lambda i, j, k: (i, 0))
    vec = pl.BlockSpec((1, HEAD_DIM), lambda i, j, k: (0, 0))
    return _mm("in_proj", u, w_in, _NN, (T // tm, 1, 1), pl.BlockSpec((tm, K), lambda i, j, k: (i, 0)),
               pl.BlockSpec((K, W), lambda i, j, k: (0, 0)),
               [jax.ShapeDtypeStruct((T, wa), F32), jax.ShapeDtypeStruct((T, W), BF16)],
               [pl.BlockSpec((tm, wa), lambda i, j, k: (i, 0)), pl.BlockSpec((tm, W), lambda i, j, k: (i, 0))],
               (tm, W), epilogue, (cos, sin, g_q, g_k), [tab, tab, vec, vec])


def _dproj(proj_a, dqa, dka_t, dva_t, dqb, dkb, dvb, cos, sin, g_q, g_k, tr=256):
    T, wa = proj_a.shape
    tr = _tile(T, tr)
    n_q = dqa.shape[1] // HEAD_DIM
    wkv = dka_t.shape[0]
    W = wa + wkv + dqb.shape[1] + dkb.shape[1] + dvb.shape[1]

    def body(p_ref, dqa_ref, dkat_ref, dvat_ref, dqb_ref, dkb_ref, dvb_ref, c_ref, s_ref, gq_ref, gk_ref,
             o_ref, dgq_ref, dgk_ref):
        c, s = c_ref[...], s_ref[...]
        dka = dkat_ref[...].T
        dgq = jnp.zeros((SUBLANES, HEAD_DIM), F32)
        dgk = jnp.zeros((SUBLANES, HEAD_DIM), F32)
        for h in range(wa // HEAD_DIM):
            cols = slice(h * HEAD_DIM, (h + 1) * HEAD_DIM)
            xv = p_ref[:, cols]
            r = lax.rsqrt(_mean_last(xv * xv) + EPS)
            xn = xv * r
            if h < n_q:
                d = dqa_ref[:, cols]
                g = gq_ref[...]
            else:
                d = dka[:, (h - n_q) * HEAD_DIM:(h - n_q + 1) * HEAD_DIM]
                g = gk_ref[...]
            dqn = d * c + _swap32(d * s)
            part = _rows_to_sublanes(dqn * xn)
            if h < n_q:
                dgq = dgq + part
            else:
                dgk = dgk + part
            dxn = dqn * g
            o_ref[:, cols] = (r * (dxn - xn * _mean_last(dxn * xn))).astype(BF16)
        o_ref[:, wa:wa + wkv] = dvat_ref[...].T.astype(BF16)
        off = wa + wkv
        for ref in (dqb_ref, dkb_ref, dvb_ref):
            w = ref.shape[1]
            o_ref[:, off:off + w] = ref[...].astype(BF16)
            off += w
        first = pl.program_id(0) == 0
        _accumulate(dgq_ref, dgq, first)
        _accumulate(dgk_ref, dgk, first)

    def row(w):
        return pl.BlockSpec((tr, w), lambda i: (i, 0))

    col = pl.BlockSpec((wkv, tr), lambda i: (0, i))
    vec = pl.BlockSpec((1, HEAD_DIM), lambda i: (0, 0))
    part = pl.BlockSpec((SUBLANES, HEAD_DIM), lambda i: (0, 0))
    return pl.pallas_call(
        body, name="dproj", grid=(T // tr,),
        in_specs=[row(wa), row(dqa.shape[1]), col, col, row(dqb.shape[1]),
                  row(dkb.shape[1]), row(dvb.shape[1]), row(HEAD_DIM), row(HEAD_DIM), vec, vec],
        out_specs=[row(W), part, part],
        out_shape=[jax.ShapeDtypeStruct((T, W), BF16), jax.ShapeDtypeStruct((SUBLANES, HEAD_DIM), F32),
                   jax.ShapeDtypeStruct((SUBLANES, HEAD_DIM), F32)],
        compiler_params=_params(("arbitrary",)),
    )(proj_a, dqa, dka_t, dva_t, dqb, dkb, dvb, cos, sin, g_q, g_k)


def _attn_a_fwd(pb, n_q, n_kv, out_heads, tq=1024, tc=1024):
    T = pb.shape[0]
    tq, tc = _tile(T, tq), _tile(T, tc)
    scale = HEAD_DIM ** -0.5
    c = scale * LOG2E

    def body(q_ref, k_ref, v_ref, o_ref, lse_ref):
        q = q_ref[...]
        m = l = acc = None
        for j in range(T // tc):
            keys = slice(j * tc, (j + 1) * tc)
            s = lax.dot_general(q, k_ref[keys, :], _NT, preferred_element_type=F32)
            mj = jnp.max(s, axis=-1, keepdims=True)
            m_new = mj if j == 0 else jnp.maximum(m, mj)
            p = jnp.exp2((s - m_new) * c)
            pv = lax.dot_general(p.astype(BF16), v_ref[keys, :], _NN, preferred_element_type=F32)
            if j == 0:
                l, acc = jnp.sum(p, axis=-1, keepdims=True), pv
            else:
                alpha = jnp.exp2((m - m_new) * c)
                l = alpha * l + jnp.sum(p, axis=-1, keepdims=True)
                acc = alpha * acc + pv
            m = m_new
        o_ref[...] = (acc / l).astype(BF16)
        lse_ref[...] = m * scale + jnp.log(l)

    return pl.pallas_call(
        body, name="attn_a_fwd", grid=(n_kv, GROUP, T // tq),
        in_specs=[pl.BlockSpec((tq, HEAD_DIM), lambda kv, g, i: (i, kv * GROUP + g)),
                  pl.BlockSpec((T, HEAD_DIM), lambda kv, g, i: (0, n_q + kv)),
                  pl.BlockSpec((T, HEAD_DIM), lambda kv, g, i: (0, n_q + n_kv + kv))],
        out_specs=[pl.BlockSpec((tq, HEAD_DIM), lambda kv, g, i: (i, kv * GROUP + g)),
                   pl.BlockSpec((None, tq, 1), lambda kv, g, i: (kv * GROUP + g, i, 0))],
        out_shape=[jax.ShapeDtypeStruct((T, out_heads * HEAD_DIM), BF16), jax.ShapeDtypeStruct((n_q, T, 1), F32)],
        compiler_params=_params(("parallel", "parallel", "parallel")),
    )(pb, pb, pb)


def _attn_a_bwd(pb, o_cat, d_o, lse, n_q, n_kv, tq=1024, tc=256):
    T = pb.shape[0]
    tq, tc = _tile(T, tq), _tile(T, tc)
    scale = HEAD_DIM ** -0.5
    c = scale * LOG2E

    def body(q_ref, k_ref, v_ref, o_ref, do_ref, lse_ref, dq_ref, dkt_ref, dvt_ref):
        q, do = q_ref[...], do_ref[...]
        qt, dot = q.T, do.T
        delta = jnp.sum(do.astype(F32) * o_ref[...].astype(F32), axis=-1, keepdims=True)
        lse2 = lse_ref[...] * LOG2E

        @pl.when(jnp.logical_and(pl.program_id(1) == 0, pl.program_id(2) == 0))
        def _():
            dkt_ref[...] = jnp.zeros(dkt_ref.shape, F32)
            dvt_ref[...] = jnp.zeros(dvt_ref.shape, F32)

        dq = None
        for j in range(T // tc):
            keys = slice(j * tc, (j + 1) * tc)
            kc, vc = k_ref[keys, :], v_ref[keys, :]
            s = lax.dot_general(q, kc, _NT, preferred_element_type=F32)
            p = jnp.exp2(s * c - lse2)
            dp = lax.dot_general(do, vc, _NT, preferred_element_type=F32)
            ds = (p * (dp - delta) * scale).astype(BF16)
            dqj = lax.dot_general(ds, kc, _NN, preferred_element_type=F32)
            dq = dqj if dq is None else dq + dqj
            dvt_ref[:, keys] += lax.dot_general(dot, p.astype(BF16), _NN, preferred_element_type=F32)
            dkt_ref[:, keys] += lax.dot_general(qt, ds, _NN, preferred_element_type=F32)
        dq_ref[...] = dq

    qmap = lambda kv, g, i: (i, kv * GROUP + g)
    return pl.pallas_call(
        body, name="attn_a_bwd", grid=(n_kv, GROUP, T // tq),
        in_specs=[pl.BlockSpec((tq, HEAD_DIM), qmap),
                  pl.BlockSpec((T, HEAD_DIM), lambda kv, g, i: (0, n_q + kv)),
                  pl.BlockSpec((T, HEAD_DIM), lambda kv, g, i: (0, n_q + n_kv + kv)),
                  pl.BlockSpec((tq, HEAD_DIM), qmap),
                  pl.BlockSpec((tq, HEAD_DIM), qmap),
                  pl.BlockSpec((None, tq, 1), lambda kv, g, i: (kv * GROUP + g, i, 0))],
        out_specs=[pl.BlockSpec((tq, HEAD_DIM), qmap),
                   pl.BlockSpec((HEAD_DIM, T), lambda kv, g, i: (kv, 0)),
                   pl.BlockSpec((HEAD_DIM, T), lambda kv, g, i: (kv, 0))],
        out_shape=[jax.ShapeDtypeStruct((T, n_q * HEAD_DIM), F32),
                   jax.ShapeDtypeStruct((n_kv * HEAD_DIM, T), F32),
                   jax.ShapeDtypeStruct((n_kv * HEAD_DIM, T), F32)],
        compiler_params=_params(("parallel", "arbitrary", "arbitrary")),
    )(pb, pb, pb, o_cat, d_o, lse)


def _bucket_index():
    r = np.arange(BLOCK_Q)[:, None]
    j = np.arange(3 * BLOCK_Q)[None, :]
    rel = (j - BLOCK_Q) - r
    nb = N_BUCKETS // 2
    ret = np.where(rel > 0, nb, 0)
    n = np.abs(rel)
    max_exact = nb // 2
    nf = np.maximum(n, 1).astype(np.float32)
    large = max_exact + (np.log(nf / max_exact) / math.log(MAX_DISTANCE / max_exact) * (nb - max_exact)).astype(np.int32)
    large = np.minimum(large, nb - 1)
    return jnp.asarray(ret + np.where(n < max_exact, n, large), jnp.int32)


def _bias_build(idx, table_flat, n_heads, deps=()):
    def body(idx_ref, tab_ref, o_ref):
        h = pl.program_id(0)
        iv = idx_ref[...]
        acc = jnp.zeros(iv.shape, F32)
        for b in range(N_BUCKETS):
            acc = jnp.where(iv == b, tab_ref[b * n_heads + h], acc)
        r = lax.broadcasted_iota(jnp.int32, iv.shape, 0)
        j = lax.broadcasted_iota(jnp.int32, iv.shape, 1)
        o_ref[...] = jnp.where(jnp.abs(j - BLOCK_Q - r) <= WINDOW, acc, NEG_INF)

    return _pcall(
        body, deps, name="bias_build", grid=(n_heads,),
        in_specs=[pl.BlockSpec(idx.shape, lambda h: (0, 0)), pl.BlockSpec(memory_space=pltpu.SMEM)],
        out_specs=pl.BlockSpec((None,) + idx.shape, lambda h: (h, 0, 0)),
        out_shape=jax.ShapeDtypeStruct((n_heads,) + idx.shape, F32),
        compiler_params=_params(("parallel",)),
    )(idx, table_flat)


def _in_sequence(n, T):
    j = lax.broadcasted_iota(jnp.int32, (GROUP * BLOCK_Q, 3 * BLOCK_Q), 1)
    kabs = n * BLOCK_Q + j - BLOCK_Q
    return (kabs >= 0) & (kabs < T)


def _per_head_rows(values):
    head = lax.broadcasted_iota(jnp.int32, (GROUP * BLOCK_Q, 1), 0) // BLOCK_Q
    col = jnp.zeros((GROUP * BLOCK_Q, 1), F32)
    for g, v in enumerate(values):
        col = jnp.where(head == g, v, col)
    return col


def _band_specs(col, nblk, sb):
    return [pl.BlockSpec((BLOCK_Q, HEAD_DIM), lambda kv, i: (jnp.maximum(sb * i - 1, 0), col(kv))),
            pl.BlockSpec((sb * BLOCK_Q, HEAD_DIM), lambda kv, i: (i, col(kv))),
            pl.BlockSpec((BLOCK_Q, HEAD_DIM), lambda kv, i: (jnp.minimum(sb * i + sb, nblk - 1), col(kv)))]


def _head_specs(base, rows):
    return [pl.BlockSpec((rows, HEAD_DIM), functools.partial(lambda kv, i, g: (i, base + kv * GROUP + g), g=g))
            for g in range(GROUP)]


def _attn_b_fwd(pb, bias, sink, o_all, q_off, n_q, n_kv, deps=(), sb=16):
    T = pb.shape[0]
    nblk = T // BLOCK_Q
    sb = min(sb, nblk)
    tq = sb * BLOCK_Q
    scale = HEAD_DIM ** -0.5

    def body(*refs):
        q_refs = refs[0:GROUP]
        k_refs, v_refs = refs[GROUP:GROUP + 3], refs[GROUP + 3:GROUP + 6]
        bias_ref, sink_ref, o_ref, lse_ref = refs[GROUP + 6:]
        kv, i = pl.program_id(0), pl.program_id(1)
        kb = jnp.concatenate([r[...] for r in k_refs], axis=0)
        vb = jnp.concatenate([r[...] for r in v_refs], axis=0)
        bias_all = bias_ref[...].reshape(GROUP * BLOCK_Q, 3 * BLOCK_Q)
        sk = _per_head_rows([sink_ref[kv * GROUP + g] for g in range(GROUP)])
        for b in range(sb):
            rows = slice(b * BLOCK_Q, (b + 1) * BLOCK_Q)
            kw, vw = kb[b * BLOCK_Q:(b + 3) * BLOCK_Q], vb[b * BLOCK_Q:(b + 3) * BLOCK_Q]
            q = jnp.concatenate([r[rows, :] for r in q_refs], axis=0)
            s = lax.dot_general(q, kw, _NT, preferred_element_type=F32) * scale + bias_all
            if b == 0 or b == sb - 1:
                s = jnp.where(_in_sequence(i * sb + b, T), s, NEG_INF)
            m = jnp.maximum(jnp.max(s, axis=-1, keepdims=True), sk)
            p = jnp.exp(s - m)
            l = jnp.sum(p, axis=-1, keepdims=True) + jnp.exp(sk - m)
            o = (lax.dot_general(p.astype(BF16), vw, _NN, preferred_element_type=F32) / l).astype(BF16)
            lse = m + jnp.log(l)
            for g in range(GROUP):
                head = slice(g * BLOCK_Q, (g + 1) * BLOCK_Q)
                o_ref[rows, g * HEAD_DIM:(g + 1) * HEAD_DIM] = o[head]
                lse_ref[g, rows, :] = lse[head]

    first_group = o_all.shape[1] // (GROUP * HEAD_DIM) - n_kv
    return _pcall(
        body, deps, into=(o_all, 0), name="attn_b_fwd", grid=(n_kv, nblk // sb),
        in_specs=[*_head_specs(q_off, tq),
                  *_band_specs(lambda kv: q_off + n_q + kv, nblk, sb),
                  *_band_specs(lambda kv: q_off + n_q + n_kv + kv, nblk, sb),
                  pl.BlockSpec((GROUP, BLOCK_Q, 3 * BLOCK_Q), lambda kv, i: (kv, 0, 0)),
                  pl.BlockSpec(memory_space=pltpu.SMEM)],
        out_specs=[pl.BlockSpec((tq, GROUP * HEAD_DIM), lambda kv, i: (i, first_group + kv)),
                   pl.BlockSpec((GROUP, tq, 1), lambda kv, i: (kv, i, 0))],
        out_shape=[jax.ShapeDtypeStruct(o_all.shape, BF16), jax.ShapeDtypeStruct((n_q, T, 1), F32)],
        compiler_params=_params(("parallel", "parallel")),
    )(*([pb] * (GROUP + 6)), bias, sink)


def _attn_b_bwd(pb, o_cat, d_o, lse, bias, sink, q_off, n_q, n_kv, o_off, deps=(), sb=16):
    T = pb.shape[0]
    nblk = T // BLOCK_Q
    sb = min(sb, nblk)
    tq = sb * BLOCK_Q
    scale = HEAD_DIM ** -0.5

    def body(*refs):
        q_refs = refs[0:GROUP]
        k_refs, v_refs = refs[GROUP:GROUP + 3], refs[GROUP + 3:GROUP + 6]
        o_refs, do_refs = refs[GROUP + 6:2 * GROUP + 6], refs[2 * GROUP + 6:3 * GROUP + 6]
        lse_ref, bias_ref, sink_ref, dq_ref, dk_ref, dv_ref, dbias_ref, dsink_ref, dkb_ref, dvb_ref = refs[3 * GROUP + 6:]
        kv, i = pl.program_id(0), pl.program_id(1)
        first = i == 0

        @pl.when(first)
        def _():
            dk_ref[...] = jnp.zeros(dk_ref.shape, F32)
            dv_ref[...] = jnp.zeros(dv_ref.shape, F32)
            dbias_ref[...] = jnp.zeros(dbias_ref.shape, F32)

        kb = jnp.concatenate([r[...] for r in k_refs], axis=0)
        vb = jnp.concatenate([r[...] for r in v_refs], axis=0)
        dkb_ref[...] = jnp.zeros(dkb_ref.shape, F32)
        dvb_ref[...] = jnp.zeros(dvb_ref.shape, F32)
        row = lax.broadcasted_iota(jnp.int32, (SUBLANES, LANES), 0)
        dsink = jnp.zeros((SUBLANES, LANES), F32)
        bias_all = bias_ref[...].reshape(GROUP * BLOCK_Q, 3 * BLOCK_Q)
        sk = _per_head_rows([sink_ref[kv * GROUP + g] for g in range(GROUP)])
        for b in range(sb):
            rows = slice(b * BLOCK_Q, (b + 1) * BLOCK_Q)
            win = slice(b * BLOCK_Q, (b + 3) * BLOCK_Q)
            kw, vw = kb[win], vb[win]
            q = jnp.concatenate([r[rows, :] for r in q_refs], axis=0)
            do = jnp.concatenate([r[rows, :] for r in do_refs], axis=0)
            o = jnp.concatenate([r[rows, :] for r in o_refs], axis=0)
            lse = jnp.concatenate([lse_ref[g, rows, :] for g in range(GROUP)], axis=0)
            delta = jnp.sum(do.astype(F32) * o.astype(F32), axis=-1, keepdims=True)
            s = lax.dot_general(q, kw, _NT, preferred_element_type=F32) * scale + bias_all
            if b == 0 or b == sb - 1:
                s = jnp.where(_in_sequence(i * sb + b, T), s, NEG_INF)
            p = jnp.exp(s - lse)
            dp = lax.dot_general(do, vw, _NT, preferred_element_type=F32)
            ds = p * (dp - delta)
            dbias_ref[...] += ds.reshape(GROUP, BLOCK_Q, 3 * BLOCK_Q)
            sunk = jnp.exp(sk - lse) * delta
            for g in range(GROUP):
                dsink = dsink + jnp.where(row == g, -jnp.sum(sunk[g * BLOCK_Q:(g + 1) * BLOCK_Q]), 0.0)
            dsb = (ds * scale).astype(BF16)
            dq = lax.dot_general(dsb, kw, _NN, preferred_element_type=F32).astype(BF16)
            for g in range(GROUP):
                dq_ref[rows, g * HEAD_DIM:(g + 1) * HEAD_DIM] = dq[g * BLOCK_Q:(g + 1) * BLOCK_Q]
            dkb_ref[win, :] += lax.dot_general(dsb, q, _TN, preferred_element_type=F32)
            dvb_ref[win, :] += lax.dot_general(p.astype(BF16), do, _TN, preferred_element_type=F32)
        _accumulate(dsink_ref, dsink, first)

        before = pl.ds(pl.multiple_of(jnp.maximum(sb * i - 1, 0) * BLOCK_Q, BLOCK_Q), BLOCK_Q)
        own = pl.ds(pl.multiple_of(i * tq, BLOCK_Q), tq)
        after = pl.ds(pl.multiple_of(jnp.minimum(sb * i + sb, nblk - 1) * BLOCK_Q, BLOCK_Q), BLOCK_Q)
        for acc_ref, band_ref in ((dk_ref, dkb_ref), (dv_ref, dvb_ref)):
            acc_ref[before, :] += band_ref[0:BLOCK_Q, :]
            acc_ref[own, :] += band_ref[BLOCK_Q:BLOCK_Q + tq, :]
            acc_ref[after, :] += band_ref[BLOCK_Q + tq:, :]

    return _pcall(
        body, deps, name="attn_b_bwd", grid=(n_kv, nblk // sb),
        in_specs=[*_head_specs(q_off, tq),
                  *_band_specs(lambda kv: q_off + n_q + kv, nblk, sb),
                  *_band_specs(lambda kv: q_off + n_q + n_kv + kv, nblk, sb),
                  *_head_specs(o_off, tq), *_head_specs(o_off, tq),
                  pl.BlockSpec((GROUP, tq, 1), lambda kv, i: (kv, i, 0)),
                  pl.BlockSpec((GROUP, BLOCK_Q, 3 * BLOCK_Q), lambda kv, i: (kv, 0, 0)),
                  pl.BlockSpec(memory_space=pltpu.SMEM)],
        out_specs=[pl.BlockSpec((tq, GROUP * HEAD_DIM), lambda kv, i: (i, kv)),
                   pl.BlockSpec((T, HEAD_DIM), lambda kv, i: (0, kv)),
                   pl.BlockSpec((T, HEAD_DIM), lambda kv, i: (0, kv)),
                   pl.BlockSpec((GROUP, BLOCK_Q, 3 * BLOCK_Q), lambda kv, i: (kv, 0, 0)),
                   pl.BlockSpec((None, SUBLANES, LANES), lambda kv, i: (kv, 0, 0))],
        out_shape=[jax.ShapeDtypeStruct((T, n_q * HEAD_DIM), BF16),
                   jax.ShapeDtypeStruct((T, n_kv * HEAD_DIM), F32),
                   jax.ShapeDtypeStruct((T, n_kv * HEAD_DIM), F32),
                   jax.ShapeDtypeStruct((n_q, BLOCK_Q, 3 * BLOCK_Q), F32),
                   jax.ShapeDtypeStruct((n_kv, SUBLANES, LANES), F32)],
        scratch_shapes=[pltpu.VMEM((tq + 2 * BLOCK_Q, HEAD_DIM), F32), pltpu.VMEM((tq + 2 * BLOCK_Q, HEAD_DIM), F32)],
        compiler_params=_params(("parallel", "arbitrary")),
    )(*([pb] * (GROUP + 6)), *([o_cat] * GROUP), *([d_o] * GROUP), lse, bias, sink)


def _table_grads(dbias, dsink_raw, idx):
    n_heads = dbias.shape[0]
    n_kv = dsink_raw.shape[0]

    def body(db_ref, ds_ref, idx_ref, dt_ref, dsk_ref):
        iv = idx_ref[...]
        row = lax.broadcasted_iota(jnp.int32, (SUBLANES, LANES), 0)
        lane = lax.broadcasted_iota(jnp.int32, (SUBLANES, LANES), 1)
        dsk = jnp.zeros((SUBLANES, LANES), F32)
        for h in range(n_heads):
            d = db_ref[h]
            acc = jnp.zeros((SUBLANES, LANES), F32)
            for b in range(N_BUCKETS):
                acc = jnp.where((row == 0) & (lane == b), jnp.sum(jnp.where(iv == b, d, 0.0)), acc)
            dt_ref[:, h * LANES:(h + 1) * LANES] = acc
            raw = ds_ref[h // GROUP]
            val = jnp.sum(jnp.where((row == h % GROUP) & (lane == 0), raw, 0.0))
            dsk = jnp.where((row == 0) & (lane == h), val, dsk)
        dsk_ref[...] = dsk

    return pl.pallas_call(
        body, name="table_grads",
        in_specs=[pl.BlockSpec(memory_space=pltpu.VMEM)] * 3,
        out_specs=[pl.BlockSpec(memory_space=pltpu.VMEM)] * 2,
        out_shape=[jax.ShapeDtypeStruct((SUBLANES, n_heads * LANES), F32),
                   jax.ShapeDtypeStruct((SUBLANES, LANES), F32)],
        compiler_params=_params(),
    )(dbias, dsink_raw, idx)


def _position():
    x, y, c = lax.axis_index("x"), lax.axis_index("y"), lax.axis_index("c")
    return x, y, c


def _hbm(a):
    return pltpu.with_memory_space_constraint(a, pltpu.HBM)


def _split_start(name, bufs, sem_shapes, issue):
    nb, ns = len(bufs), len(sem_shapes)

    def body(*refs):
        buf_refs = refs[:nb]
        sems = refs[nb:nb + ns]
        token = refs[nb + ns + nb]
        issue(buf_refs, sems)
        token[...] = jnp.zeros(token.shape, F32)

    outs = pl.pallas_call(
        body, name=name,
        in_specs=[_HBM] * nb,
        out_specs=[_SEM] * ns + [_HBM] * nb + [_VMEM],
        out_shape=[pltpu.SemaphoreType.DMA(s) for s in sem_shapes] + [pltpu.HBM(b.shape, b.dtype) for b in bufs]
        + [jax.ShapeDtypeStruct((SUBLANES, LANES), F32)],
        input_output_aliases={i: ns + i for i in range(nb)},
        compiler_params=pltpu.CompilerParams(has_side_effects=_EFFECT),
    )(*[_hbm(b) for b in bufs])
    return outs[:ns], outs[ns:ns + nb], outs[-1]


def _split_wait(name, bufs, send, recv, counts, size_of, after):
    nb = len(bufs)

    def body(*refs):
        buf_refs = refs[:nb]
        send_ref, recv_ref = refs[nb], refs[nb + 1]
        x, y, c = _position()
        for w, n in enumerate(counts):
            ref = size_of(buf_refs, w)
            for k in range(n):
                s = sum(counts[:w]) + k
                cp = pltpu.make_async_remote_copy(
                    src_ref=ref, dst_ref=ref, send_sem=send_ref.at[s], recv_sem=recv_ref.at[s],
                    device_id=(x, y, c), device_id_type=MESH)
                cp.wait_send()
                cp.wait_recv()

    return pl.pallas_call(
        body, name=name,
        in_specs=[_HBM] * nb + [_SEM, _SEM, _ANY],
        out_specs=[_HBM] * nb,
        out_shape=[pltpu.HBM(b.shape, b.dtype) for b in bufs],
        input_output_aliases={i: i for i in range(nb)},
        compiler_params=pltpu.CompilerParams(has_side_effects=_EFFECT),
    )(*bufs, send, recv, after)


def _block_of(pos):
    return 4 * pos[0] + 2 * pos[1] + pos[2]


def _shard_of(ref, blk, by_cols):
    aligned = (lambda v, a: v) if isinstance(blk, int) else pl.multiple_of
    if by_cols:
        n = ref.shape[1] // N_DEV
        return ref.at[:, pl.ds(aligned(blk * n, LANES), n)]
    r = ref.shape[0] // N_DEV
    return ref.at[pl.ds(aligned(blk * r, SUBLANES), r), :]


def _place_own(name, land, shard, by_cols, tr=256):
    r, n = shard.shape
    tr = _tile(r, tr)
    mine = _block_of(_position()).astype(jnp.int32).reshape(1)

    def body(m_ref, land_ref, s_ref, o_ref):
        o_ref[...] = s_ref[...]

    if by_cols:
        out = pl.BlockSpec((tr, n), lambda i, m_ref: (i, m_ref[0]))
    else:
        out = pl.BlockSpec((tr, n), lambda i, m_ref: (m_ref[0] * (r // tr) + i, 0))
    return pl.pallas_call(
        body, name=name,
        grid_spec=pltpu.PrefetchScalarGridSpec(
            num_scalar_prefetch=1, grid=(r // tr,),
            in_specs=[_ANY, pl.BlockSpec((tr, n), lambda i, m_ref: (i, 0))], out_specs=out),
        out_shape=jax.ShapeDtypeStruct(land.shape, land.dtype),
        input_output_aliases={1: 0},
        compiler_params=_params(("parallel",)),
    )(mine, land, shard)


def _gather_start(name, shards, by_cols, groups, after=None):
    nw = len(shards)
    lands = [lax.empty((s.shape[0], s.shape[1] * N_DEV) if cols else (s.shape[0] * N_DEV, s.shape[1]), s.dtype)
             for s, cols in zip(shards, by_cols)]
    order = [] if after is None else [after]

    def issue(bufs, sems):
        x, y, c = _position()
        peers = [(x, y, 1 - c), (1 - x, y, c), (x, 1 - y, c), (1 - x, 1 - y, c)]
        for gi, grp in enumerate(groups):
            for wi, w in enumerate(grp):
                for k, peer in enumerate(peers):
                    pltpu.make_async_remote_copy(
                        src_ref=bufs[w], dst_ref=_shard_of(bufs[nw + w], _block_of((x, y, c)), by_cols[w]),
                        send_sem=sems[2 * gi].at[4 * wi + k], recv_sem=sems[2 * gi + 1].at[4 * wi + k],
                        device_id=peer, device_id_type=MESH).start()

    sem_shapes = [(4 * len(g),) for g in groups for _ in range(2)]
    sems, thru, token = _split_start(name, list(shards) + lands + order, sem_shapes, issue)
    return sems, thru[:nw], thru[nw:2 * nw], token


def _gather_forward(name, lands, by_cols):
    nw = len(lands)

    def issue(land, sems):
        x, y, c = _position()
        for w in range(nw):
            for k, chip in enumerate([(1 - x, y), (x, 1 - y), (1 - x, 1 - y)]):
                blk = _shard_of(land[w], _block_of((*chip, c)), by_cols[w])
                pltpu.make_async_remote_copy(
                    src_ref=blk, dst_ref=blk, send_sem=sems[0].at[3 * w + k], recv_sem=sems[1].at[3 * w + k],
                    device_id=(x, y, 1 - c), device_id_type=MESH).start()

    return _split_start(name, lands, [(3 * nw,), (3 * nw,)], issue)


def _first_block(bufs, w, offset=0):
    return bufs[offset + w].at[0]


_PEER_FLIPS = ((0, 0, 1), (1, 0, 0), (1, 0, 1), (0, 1, 0), (0, 1, 1), (1, 1, 0), (1, 1, 1))


def _scatter_start(name, grads, by_cols):
    nw = len(grads)
    lands = []
    for g, cols in zip(grads, by_cols):
        shard = (g.shape[0], g.shape[1] // N_DEV) if cols else (g.shape[0] // N_DEV, g.shape[1])
        lands.append(lax.empty((N_DEV,) + shard, g.dtype))

    def issue(bufs, sems):
        x, y, c = _position()
        flip = lambda v, f: 1 - v if f else v
        for w in range(nw):
            for k, (fx, fy, fc) in enumerate(_PEER_FLIPS):
                peer = (flip(x, fx), flip(y, fy), flip(c, fc))
                pltpu.make_async_remote_copy(
                    src_ref=_shard_of(bufs[w], _block_of(peer), by_cols[w]), dst_ref=bufs[nw + w].at[_block_of((x, y, c))],
                    send_sem=sems[0].at[7 * w + k], recv_sem=sems[1].at[7 * w + k],
                    device_id=peer, device_id_type=MESH).start()

    return _split_start(name, list(grads) + lands, [(7 * nw,), (7 * nw,)], issue)


def _adam(w, g, m, v):
    m = ADAM_B1 * m + (1.0 - ADAM_B1) * g
    v = ADAM_B2 * v + (1.0 - ADAM_B2) * (g * g)
    m_hat = m / (1.0 - ADAM_B1 ** ADAM_STEP)
    v_hat = v / (1.0 - ADAM_B2 ** ADAM_STEP)
    delta = -ADAM_LR * (m_hat / (jnp.sqrt(v_hat) + ADAM_EPS) + ADAM_WD * w)
    return delta, m, v


def _sum_adam(name, landed, grad, by_cols, w, m, v, tr=256):
    R, C = w.shape
    tr = _tile(R, tr)
    mine = _block_of(_position()).astype(jnp.int32).reshape(1)

    def body(me_ref, l_ref, own_ref, w_ref, m_ref, v_ref, g_ref, d_ref, nm_ref, nv_ref):
        own = own_ref[...].astype(F32)
        g = None
        for d in range(N_DEV):
            part = jnp.where(me_ref[0] == d, own, l_ref[d].astype(F32))
            g = part if g is None else g + part
        g_ref[...] = g
        d_ref[...], nm_ref[...], nv_ref[...] = _adam(w_ref[...], g, m_ref[...], v_ref[...])

    tile = pl.BlockSpec((tr, C), lambda i, me_ref: (i, 0))
    if by_cols:
        own = pl.BlockSpec((tr, C), lambda i, me_ref: (i, me_ref[0]))
    else:
        own = pl.BlockSpec((tr, C), lambda i, me_ref: (me_ref[0] * (R // tr) + i, 0))
    return pl.pallas_call(
        body, name=name,
        grid_spec=pltpu.PrefetchScalarGridSpec(
            num_scalar_prefetch=1, grid=(R // tr,),
            in_specs=[pl.BlockSpec((N_DEV, tr, C), lambda i, me_ref: (0, i, 0)), own, tile, tile, tile],
            out_specs=[tile] * 4),
        out_shape=[jax.ShapeDtypeStruct((R, C), F32)] * 4,
        compiler_params=_params(("parallel",)),
    )(mine, landed, grad, w, m, v)


def _small_all_reduce(parts, deps=()):
    W = parts.shape[1]

    def body(p_ref, o_ref, slots, send_sems, recv_sems):
        x, y, c = _position()
        me = 4 * x + 2 * y + c
        slots[me] = jnp.sum(p_ref[...], axis=0, keepdims=True)
        peers = [(x, y, 1 - c), (1 - x, y, c), (1 - x, y, 1 - c), (x, 1 - y, c), (x, 1 - y, 1 - c),
                 (1 - x, 1 - y, c), (1 - x, 1 - y, 1 - c)]
        copies = []
        for k, peer in enumerate(peers):
            cp = pltpu.make_async_remote_copy(
                src_ref=slots.at[me], dst_ref=slots.at[me], send_sem=send_sems.at[k], recv_sem=recv_sems.at[k],
                device_id=peer, device_id_type=MESH)
            cp.start()
            copies.append(cp)
        for cp in copies:
            cp.wait()
        total = slots[0]
        for d in range(1, N_DEV):
            total = total + slots[d]
        o_ref[...] = total

    return _pcall(
        body, deps, name="small_all_reduce",
        in_specs=[pl.BlockSpec(memory_space=pltpu.VMEM)], out_specs=pl.BlockSpec(memory_space=pltpu.VMEM),
        out_shape=jax.ShapeDtypeStruct((1, W), F32),
        scratch_shapes=[pltpu.VMEM((N_DEV, 1, W), F32), pltpu.SemaphoreType.DMA((7,)), pltpu.SemaphoreType.DMA((7,))],
    )(parts)


def _adam_small(w, g, m, v):
    def body(w_ref, g_ref, m_ref, v_ref, d_ref, nm_ref, nv_ref):
        d_ref[...], nm_ref[...], nv_ref[...] = _adam(w_ref[...], g_ref[...], m_ref[...], v_ref[...])

    return pl.pallas_call(
        body, name="adam_small",
        in_specs=[pl.BlockSpec(memory_space=pltpu.VMEM)] * 4, out_specs=[pl.BlockSpec(memory_space=pltpu.VMEM)] * 3,
        out_shape=[jax.ShapeDtypeStruct(w.shape, F32)] * 3,
    )(w, g, m, v)


_GATHER_GROUPS = (("w_in",), ("w_out", "w_up", "ple_w"), ("w_down", "w_gate"))
_COL_SHARDED = ("w_in", "w_up", "ple_w")


class _MeshComm:
    def __init__(self, w, mom, var):
        self.w, self.mom, self.var = w, mom, var
        self.out = {}
        self._scatters = {}

    def gather_begin(self):
        self._groups = {}
        token = None
        for tag, first, group_list in (("gather_start0", 0, _GATHER_GROUPS[:1]), ("gather_start1", 1, _GATHER_GROUPS[1:])):
            names = [n for g in group_list for n in g]
            idx = {n: i for i, n in enumerate(names)}
            by_cols = [n in _COL_SHARDED for n in names]
            sems, src, lands, token = _gather_start(tag, [self.w[n].astype(BF16) for n in names], by_cols,
                                                    [[idx[n] for n in g] for g in group_list], token)
            lands = [_place_own("place_" + n, land, s, cols) for n, land, s, cols in zip(names, lands, src, by_cols)]
            for k, g in enumerate(group_list):
                self._groups[first + k] = (sems[2 * k], sems[2 * k + 1], [src[idx[n]] for n in g],
                                           [lands[idx[n]] for n in g])
        return token

    @staticmethod
    def _shard_size(names, offset):
        return lambda bufs, w: _shard_of(bufs[offset + w], 0, names[w] in _COL_SHARDED)

    def gather_arrive(self, gi, after):
        names = _GATHER_GROUPS[gi]
        send, recv, src, lands = self._groups[gi]
        out = _split_wait("gather_arrive%d" % gi, src + lands, send, recv, [4] * len(names),
                          self._shard_size(names, len(names)), after)
        self._arrived = out[len(names):]

    def gather_forward(self, gi):
        by_cols = [n in _COL_SHARDED for n in _GATHER_GROUPS[gi]]
        self._fsems, self._fthru, token = _gather_forward("gather_forward%d" % gi, self._arrived, by_cols)
        return token

    def gather_finish(self, gi, after):
        names = _GATHER_GROUPS[gi]
        out = _split_wait("gather_finish%d" % gi, self._fthru, self._fsems[0], self._fsems[1], [3] * len(names),
                          self._shard_size(names, 0), after)
        return dict(zip(names, out))

    def reduce_begin(self, key, grads):
        names = list(grads)
        sems, thru, token = _scatter_start("scatter_start_" + key, [grads[n] for n in names],
                                           [n in _COL_SHARDED for n in names])
        self._scatters[key] = (names, sems, thru)
        return token

    def reduce_finish(self, key, after):
        names, sems, thru = self._scatters[key]
        nw = len(names)
        out = _split_wait("scatter_wait_" + key, thru, sems[0], sems[1], [N_DEV - 1] * nw,
                          functools.partial(_first_block, offset=nw), after)
        for i, n in enumerate(names):
            self.out[n] = _sum_adam("adam_" + n, out[nw + i], out[i], n in _COL_SHARDED, self.w[n], self.mom[n],
                                    self.var[n])


def _step(x, p, target, gains, comm):
    T, D = x.shape
    n_q = D // (2 * HEAD_DIM)
    n_kv = n_q // GROUP
    cos, sin = _rope_tables(T)
    idx = _bucket_index()

    t = comm.gather_begin()
    u = _rms_fwd("norm_attn", x, gains["attn_norm_g"], deps=(t,))
    comm.gather_arrive(0, u)
    t = comm.gather_forward(0)
    bias = _bias_build(idx, gains["rel_bias_table"].reshape(-1), n_q, deps=(t,))
    full = comm.gather_finish(0, bias)
    proj_a, pb = _in_proj(u, full["w_in"], cos, sin, gains["q_norm_g"], gains["k_norm_g"], n_q + n_kv)
    o_a, lse_a = _attn_a_fwd(pb, n_q, n_kv, 2 * n_q)
    comm.gather_arrive(1, lse_a)
    t = comm.gather_forward(1)
    sink = gains["sink_logits"].reshape(-1)
    b_off = n_q + 2 * n_kv
    o_cat, lse_b = _attn_b_fwd(pb, bias, sink, o_a, b_off, n_q, n_kv, deps=(t,))
    full.update(comm.gather_finish(1, lse_b))
    h1, m_in = _mm_nn_rms("out_proj", o_cat, full["w_out"], x, gains["mlp_norm_g"])

    def up_epilogue(acc, extra, outs):
        outs[0][...] = acc.astype(BF16)
        r = jnp.maximum(acc, 0.0)
        outs[1][...] = (r * r).astype(BF16)

    a_act, f_act = _mm_nn("up_proj", m_in, full["w_up"], epilogue=up_epilogue, out_dtypes=[BF16, BF16], tn=2048)
    comm.gather_arrive(2, f_act)
    t = comm.gather_forward(2)
    p_b = p.astype(BF16)
    pe = _mm_nn("ple_proj", p_b, full["ple_w"], deps=(t,))
    full.update(comm.gather_finish(2, pe))
    h2 = _mm_nn("down_proj", f_act, full["w_down"], epilogue=_store_add, extras=(h1,), tn=256)
    gn = _rms_fwd("norm_gate", h2, gains["gate_norm_g"])

    dh3, dz, dpe, dg_final, dg_ple, loss_part = _gate_tail(gn, full["w_gate"], h2, pe, target, gains["ple_norm_g"],
                                                           gains["final_norm_g"])
    gw_gate = _mm_tn("grad_w_gate", gn, dz)
    gw_ple = _mm_tn("grad_ple_w", p_b, dpe)
    dh2, dh2_b, dg_gate = _mm_nt_rms_bwd("d_gate_in", dz, full["w_gate"], h2, gains["gate_norm_g"], dh3)
    gw_down = _mm_tn("grad_w_down", f_act, dh2_b, tn=1024)
    t = comm.reduce_begin("b", dict(w_gate=gw_gate, ple_w=gw_ple, w_down=gw_down))

    def act_bwd(acc, extra, outs):
        outs[0][...] = (acc * (2.0 * jnp.maximum(extra[0][...].astype(F32), 0.0))).astype(BF16)

    da = _mm_nt("d_act", dh2_b, full["w_down"], out_dtype=BF16, epilogue=act_bwd, extras=(a_act,), tn=2048, deps=(t,))
    gw_up = _mm_tn("grad_w_up", m_in, da, tn=1024)
    dm = _mm_nt("d_mlp_in", da, full["w_up"], out_dtype=BF16, tn=512)
    dh1, dh1_b, dg_mlp = _rms_bwd("norm_mlp_bwd", dm, h1, gains["mlp_norm_g"], dh2)
    gw_out = _mm_tn("grad_w_out", o_cat, dh1_b)
    t = comm.reduce_begin("d", dict(w_up=gw_up, w_out=gw_out))
    d_o = _mm_nt("d_attn_out", dh1_b, full["w_out"], out_dtype=BF16, deps=(t,))
    dqa, dka_t, dva_t = _attn_a_bwd(pb, o_cat, d_o, lse_a, n_q, n_kv)
    dqb, dkb, dvb, dbias, dsink_raw = _attn_b_bwd(pb, o_cat, d_o, lse_b, bias, sink, b_off, n_q, n_kv, n_q)
    dtable, dsink = _table_grads(dbias, dsink_raw, idx)
    dproj, dg_q, dg_k = _dproj(proj_a, dqa, dka_t, dva_t, dqb, dkb, dvb, cos, sin, gains["q_norm_g"], gains["k_norm_g"])
    gw_in = _mm_tn("grad_w_in", u, dproj, tn=1024)
    t = comm.reduce_begin("e", dict(w_in=gw_in))
    dx, dg_attn = _mm_nt_rms_bwd("d_attn_in", dproj, full["w_in"], x, gains["attn_norm_g"], dh1, with_bf16=False,
                                 deps=(t,))
    for key in "bd":
        comm.reduce_finish(key, dx)

    parts = jnp.concatenate([dg_attn, dg_mlp, dg_ple, dg_gate, dg_final, dg_q, dg_k, dtable, dsink, loss_part], axis=1)
    return dx, parts


_SHARDED = ("w_in", "w_out", "w_up", "w_down", "ple_w", "w_gate")
_VECTORS = ("attn_norm_g", "mlp_norm_g", "ple_norm_g", "gate_norm_g", "final_norm_g")
_ORDER = ("attn_norm_g", "w_in", "q_norm_g", "k_norm_g", "sink_logits", "w_out", "mlp_norm_g", "w_up", "w_down",
          "ple_w", "ple_norm_g", "gate_norm_g", "w_gate", "rel_bias_table", "final_norm_g")


def _pack_small(vals, n_heads):
    lane_pad = lambda v: jnp.pad(v, ((0, 0), (0, LANES - v.shape[1])))
    table = lane_pad(vals["rel_bias_table"].T).reshape(1, n_heads * LANES)
    return jnp.concatenate(
        [vals[n].reshape(1, -1) for n in _VECTORS] + [vals["q_norm_g"], vals["k_norm_g"], table,
                                                      lane_pad(vals["sink_logits"]), jnp.zeros((1, LANES), F32)], axis=1)


def _unpack_small(row, like, n_heads):
    out, off = {}, 0
    for n in _VECTORS:
        out[n] = row[:, off:off + like[n].size].reshape(like[n].shape)
        off += like[n].size
    for n in ("q_norm_g", "k_norm_g"):
        out[n] = row[:, off:off + LANES]
        off += LANES
    out["rel_bias_table"] = row[:, off:off + n_heads * LANES].reshape(n_heads, LANES)[:, :N_BUCKETS].T
    off += n_heads * LANES
    out["sink_logits"] = row[:, off:off + n_heads]
    off += LANES
    return out, row[0, off]


def kernel(x, p, attn_norm_g, w_in, q_norm_g, k_norm_g, sink_logits, w_out, mlp_norm_g, w_up, w_down, ple_w, ple_norm_g, gate_norm_g, w_gate, rel_bias_table, final_norm_g, loss_target, m_attn_norm_g, m_w_in, m_q_norm_g, m_k_norm_g, m_sink_logits, m_w_out, m_mlp_norm_g, m_w_up, m_w_down, m_ple_w, m_ple_norm_g, m_gate_norm_g, m_w_gate, m_rel_bias_table, m_final_norm_g, v_attn_norm_g, v_w_in, v_q_norm_g, v_k_norm_g, v_sink_logits, v_w_out, v_mlp_norm_g, v_w_up, v_w_down, v_ple_w, v_ple_norm_g, v_gate_norm_g, v_w_gate, v_rel_bias_table, v_final_norm_g):
    w = dict(attn_norm_g=attn_norm_g, w_in=w_in[0], q_norm_g=q_norm_g, k_norm_g=k_norm_g, sink_logits=sink_logits,
             w_out=w_out[0], mlp_norm_g=mlp_norm_g, w_up=w_up[0], w_down=w_down[0], ple_w=ple_w[0],
             ple_norm_g=ple_norm_g, gate_norm_g=gate_norm_g, w_gate=w_gate[0], rel_bias_table=rel_bias_table,
             final_norm_g=final_norm_g)
    mom = dict(attn_norm_g=m_attn_norm_g, w_in=m_w_in[0], q_norm_g=m_q_norm_g, k_norm_g=m_k_norm_g,
               sink_logits=m_sink_logits, w_out=m_w_out[0], mlp_norm_g=m_mlp_norm_g, w_up=m_w_up[0],
               w_down=m_w_down[0], ple_w=m_ple_w[0], ple_norm_g=m_ple_norm_g, gate_norm_g=m_gate_norm_g,
               w_gate=m_w_gate[0], rel_bias_table=m_rel_bias_table, final_norm_g=m_final_norm_g)
    var = dict(attn_norm_g=v_attn_norm_g, w_in=v_w_in[0], q_norm_g=v_q_norm_g, k_norm_g=v_k_norm_g,
               sink_logits=v_sink_logits, w_out=v_w_out[0], mlp_norm_g=v_mlp_norm_g, w_up=v_w_up[0],
               w_down=v_w_down[0], ple_w=v_ple_w[0], ple_norm_g=v_ple_norm_g, gate_norm_g=v_gate_norm_g,
               w_gate=v_w_gate[0], rel_bias_table=v_rel_bias_table, final_norm_g=v_final_norm_g)
    D = x.shape[-1]
    n_heads = D // (2 * HEAD_DIM)

    gains = {n: w[n] for n in w if n not in _SHARDED}
    gains["final_norm_g"] = final_norm_g.reshape(1, -1)

    comm = _MeshComm(w, mom, var)
    dx, parts = _step(x[0], p[0, 0], loss_target[0], gains, comm)

    small_g = _small_all_reduce(parts, deps=[comm.out[n][0] for n in comm.out])
    comm.reduce_finish("e", small_g)

    g_out, d_out, m_out, v_out = {}, {}, {}, {}
    for n in _SHARDED:
        g, d, nm, nv = comm.out[n]
        g_out[n], d_out[n], m_out[n], v_out[n] = g[None], d[None], nm[None], nv[None]

    small = {n: v for n, v in w.items() if n not in _SHARDED}
    pack = lambda vals: _pack_small({n: vals[n] for n in small}, n_heads)
    sd, sm, sv = _adam_small(pack(w), small_g, pack(mom), pack(var))
    sg, loss = _unpack_small(small_g, small, n_heads)
    g_out.update(sg)
    for dst, row in ((d_out, sd), (m_out, sm), (v_out, sv)):
        dst.update(_unpack_small(row, small, n_heads)[0])

    return (loss, dx[None], *[g_out[n] for n in _ORDER], *[d_out[n] for n in _ORDER],
            *[m_out[n] for n in _ORDER], *[v_out[n] for n in _ORDER])
```

```python
import functools
import math

import numpy as np
import jax
import jax.numpy as jnp
from jax import lax
from jax.experimental import pallas as pl
from jax.experimental.pallas import tpu as pltpu

F32 = jnp.float32
BF16 = jnp.bfloat16

N_DEV = 8
N_CHIP = 4
HEAD_DIM = 128
GROUP = 4
GRID_W = 64
WINDOW = 128
BLOCK_Q = 128
N_BUCKETS = 32
MAX_DISTANCE = 128
ROPE_THETA = 10000.0
EPS = 1e-6
NEG_INF = -1e30
ADAM_LR = 0.001
ADAM_B1 = 0.9
ADAM_B2 = 0.999
ADAM_EPS = 1e-08
ADAM_WD = 0.01
ADAM_STEP = 10
LOG2E = math.log2(math.e)
LANES = 128
SUBLANES = 8
VMEM_LIMIT_BYTES = 56 * 1024 * 1024
MESH = pl.DeviceIdType.MESH

_NT = (((1,), (1,)), ((), ()))
_NN = (((1,), (0,)), ((), ()))
_TN = (((0,), (0,)), ((), ()))


def _tile(dim, pref):
    return pref if dim % pref == 0 else dim


def _params(sem=None):
    return pltpu.CompilerParams(dimension_semantics=sem, vmem_limit_bytes=VMEM_LIMIT_BYTES)


_HBM = pl.BlockSpec(memory_space=pltpu.HBM)
_SEM = pl.BlockSpec(memory_space=pltpu.SEMAPHORE)
_ANY = pl.BlockSpec(memory_space=pl.ANY)
_VMEM = pl.BlockSpec(memory_space=pltpu.VMEM)
_EFFECT = pltpu.SideEffectType.DATAFLOW_SIDE_EFFECTING


def _pcall(body, deps=(), *, in_specs, into=None, **kw):
    deps = [d for d in deps if d is not None]
    nd = len(deps)
    if into is not None:
        deps = [into[0]] + deps
        nd += 1
        kw["input_output_aliases"] = {0: into[1]}

    def wrapped(*refs):
        body(*refs[nd:])

    call = pl.pallas_call(wrapped, in_specs=[_ANY] * nd + list(in_specs), **kw)
    return lambda *args: call(*deps, *args)


def _mm(name, a, b, dims, grid, a_spec, b_spec, out_shape, out_specs, acc_shape, epilogue,
        extras=(), extra_specs=(), deps=(), semantics=("parallel", "parallel", "arbitrary")):
    nk = grid[2]
    n_extra = len(extras)

    def body(*refs):
        a_ref, b_ref = refs[0], refs[1]
        extra = refs[2:2 + n_extra]
        outs = refs[2 + n_extra:-1]
        acc = refs[-1]
        part = lax.dot_general(a_ref[...], b_ref[...], dims, preferred_element_type=F32)
        if nk == 1:
            epilogue(part, extra, outs)
        else:
            k = pl.program_id(2)

            @pl.when(k == 0)
            def _():
                acc[...] = part

            @pl.when(k > 0)
            def _():
                acc[...] += part

            @pl.when(k == nk - 1)
            def _():
                epilogue(acc[...], extra, outs)

    return _pcall(
        body, deps, name=name, grid=grid,
        in_specs=[a_spec, b_spec, *extra_specs],
        out_specs=out_specs, out_shape=out_shape,
        scratch_shapes=[pltpu.VMEM(acc_shape if nk > 1 else (SUBLANES, LANES), F32)],
        compiler_params=_params(semantics),
    )(a, b, *extras)


def _store(dtype):
    def ep(acc, extra, outs):
        outs[0][...] = acc.astype(dtype)
    return ep


def _store_add(acc, extra, outs):
    outs[0][...] = acc + extra[0][...]


def _mm_nn(name, a, b, out_dtype=F32, epilogue=None, extras=(), n_out=1, out_dtypes=None, tm=1024, tn=1024, tk=None,
           deps=()):
    M, K = a.shape
    N = b.shape[1]
    tm, tn, tk = _tile(M, tm), _tile(N, tn), _tile(K, tk or K)
    b_spec = pl.BlockSpec((tk, tn), lambda i, j, k: (k, j))
    grid = (M // tm, N // tn, K // tk)
    o_spec = pl.BlockSpec((tm, tn), lambda i, j, k: (i, j))
    out_dtypes = out_dtypes or [out_dtype] * n_out
    out_shape = [jax.ShapeDtypeStruct((M, N), d) for d in out_dtypes]
    res = _mm(name, a, b, _NN, grid, pl.BlockSpec((tm, tk), lambda i, j, k: (i, k)), b_spec,
              out_shape, [o_spec] * len(out_dtypes), (tm, tn), epilogue or _store(out_dtype),
              extras, [o_spec] * len(extras), deps)
    return res if len(out_dtypes) > 1 else res[0]


def _mm_nt(name, a, b, out_dtype=F32, epilogue=None, extras=(), tm=1024, tn=1024, tk=None, deps=()):
    M, C = a.shape
    N = b.shape[0]
    tm, tn, tk = _tile(M, tm), _tile(N, tn), _tile(C, tk or C)
    b_spec = pl.BlockSpec((tn, tk), lambda i, j, k: (j, k))
    grid = (M // tm, N // tn, C // tk)
    o_spec = pl.BlockSpec((tm, tn), lambda i, j, k: (i, j))
    return _mm(name, a, b, _NT, grid, pl.BlockSpec((tm, tk), lambda i, j, k: (i, k)), b_spec,
               [jax.ShapeDtypeStruct((M, N), out_dtype)], [o_spec], (tm, tn), epilogue or _store(out_dtype),
               extras, [o_spec] * len(extras), deps)[0]


def _mm_tn(name, a, b, out_dtype=BF16, tm=1024, tn=512, tk=None, deps=()):
    T, M = a.shape
    N = b.shape[1]
    tm, tn, tk = _tile(M, tm), _tile(N, tn), _tile(T, tk or T)
    out_shape = jax.ShapeDtypeStruct((M, N), out_dtype)
    o_spec = pl.BlockSpec((tm, tn), lambda i, j, k: (i, j))
    grid = (M // tm, N // tn, T // tk)
    return _mm(name, a, b, _TN, grid, pl.BlockSpec((tk, tm), lambda i, j, k: (k, i)),
               pl.BlockSpec((tk, tn), lambda i, j, k: (k, j)), [out_shape], [o_spec], (tm, tn), _store(out_dtype),
               deps=deps)[0]


def _mean_last(v):
    return jnp.mean(v, axis=-1, keepdims=True)


def _rows_to_sublanes(v):
    r, c = v.shape
    return jnp.sum(v.reshape(r // SUBLANES, SUBLANES, c), axis=0)


def _accumulate(ref, val, first):
    @pl.when(first)
    def _():
        ref[...] = val

    @pl.when(jnp.logical_not(first))
    def _():
        ref[...] += val


def _rms_fwd(name, x, g, tr=512, deps=()):
    T, D = x.shape
    tr = _tile(T, tr)

    def body(x_ref, g_ref, o_ref):
        xv = x_ref[...]
        r = lax.rsqrt(_mean_last(xv * xv) + EPS)
        o_ref[...] = (xv * r * g_ref[...]).astype(BF16)

    row = pl.BlockSpec((tr, D), lambda i: (i, 0))
    return _pcall(
        body, deps, name=name, grid=(T // tr,),
        in_specs=[row, pl.BlockSpec((1, D), lambda i: (0, 0))],
        out_specs=row, out_shape=jax.ShapeDtypeStruct((T, D), BF16),
        compiler_params=_params(("parallel",)),
    )(x, g)


def _rms_bwd(name, dyn, x, g, dres, tr=512, deps=()):
    T, D = x.shape
    tr = _tile(T, tr)

    def body(dy_ref, x_ref, g_ref, dr_ref, dx_ref, dxb_ref, dg_ref):
        xv = x_ref[...]
        r = lax.rsqrt(_mean_last(xv * xv) + EPS)
        xn = xv * r
        dy = dy_ref[...].astype(F32)
        dxn = dy * g_ref[...]
        dx = dr_ref[...] + r * (dxn - xn * _mean_last(dxn * xn))
        dx_ref[...] = dx
        dxb_ref[...] = dx.astype(BF16)
        _accumulate(dg_ref, _rows_to_sublanes(dy * xn), pl.program_id(0) == 0)

    row = pl.BlockSpec((tr, D), lambda i: (i, 0))
    return _pcall(
        body, deps, name=name, grid=(T // tr,),
        in_specs=[row, row, pl.BlockSpec((1, D), lambda i: (0, 0)), row],
        out_specs=[row, row, pl.BlockSpec((SUBLANES, D), lambda i: (0, 0))],
        out_shape=[jax.ShapeDtypeStruct((T, D), F32), jax.ShapeDtypeStruct((T, D), BF16),
                   jax.ShapeDtypeStruct((SUBLANES, D), F32)],
        compiler_params=_params(("arbitrary",)),
    )(dyn, x, g, dres)


def _mm_nn_rms(name, a, b, res, g, tm=512, deps=()):
    M, K = a.shape
    N = b.shape[1]
    tm = _tile(M, tm)

    def epilogue(acc, extra, outs):
        h = acc + extra[0][...]
        outs[0][...] = h
        outs[1][...] = (h * lax.rsqrt(_mean_last(h * h) + EPS) * extra[1][...]).astype(BF16)

    row = pl.BlockSpec((tm, N), lambda i, j, k: (i, 0))
    return _mm(name, a, b, _NN, (M // tm, 1, 1), pl.BlockSpec((tm, K), lambda i, j, k: (i, 0)),
               pl.BlockSpec((K, N), lambda i, j, k: (0, 0)),
               [jax.ShapeDtypeStruct((M, N), F32), jax.ShapeDtypeStruct((M, N), BF16)], [row, row], (tm, N), epilogue,
               (res, g), [row, pl.BlockSpec((1, N), lambda i, j, k: (0, 0))], deps)


def _mm_nt_rms_bwd(name, a, b, x, g, dres, with_bf16=True, tm=256, deps=()):
    M, C = a.shape
    N = b.shape[0]
    tm = _tile(M, tm)

    def epilogue(dy, extra, outs):
        x_ref, dr_ref, g_ref = extra
        xv = x_ref[...]
        r = lax.rsqrt(_mean_last(xv * xv) + EPS)
        xn = xv * r
        dxn = dy * g_ref[...]
        dx = dr_ref[...] + r * (dxn - xn * _mean_last(dxn * xn))
        outs[0][...] = dx
        if with_bf16:
            outs[1][...] = dx.astype(BF16)
        _accumulate(outs[-1], _rows_to_sublanes(dy * xn), pl.program_id(0) == 0)

    row = pl.BlockSpec((tm, N), lambda i, j, k: (i, 0))
    copies = [jax.ShapeDtypeStruct((M, N), F32)] + ([jax.ShapeDtypeStruct((M, N), BF16)] if with_bf16 else [])
    return _mm(name, a, b, _NT, (M // tm, 1, 1), pl.BlockSpec((tm, C), lambda i, j, k: (i, 0)),
               pl.BlockSpec((N, C), lambda i, j, k: (0, 0)),
               copies + [jax.ShapeDtypeStruct((SUBLANES, N), F32)],
               [row] * len(copies) + [pl.BlockSpec((SUBLANES, N), lambda i, j, k: (0, 0))], (tm, N), epilogue,
               (x, dres, g), [row, row, pl.BlockSpec((1, N), lambda i, j, k: (0, 0))], deps,
               semantics=("arbitrary", "arbitrary", "arbitrary"))


def _gate_tail(gn, w_gate, h2, pe, target, g_ple, g_final, tm=256):
    T, D = h2.shape
    tm = _tile(T, tm)

    def epilogue(z, extra, outs):
        h2_ref, pe_ref, t_ref, gp_ref, gf_ref = extra
        dh3_ref, dz_ref, dpe_ref, dgf_ref, dgp_ref, loss_ref = outs
        first = pl.program_id(0) == 0
        pev = pe_ref[...]
        r3 = lax.rsqrt(_mean_last(pev * pev) + EPS)
        en = pev * r3
        e = en * gp_ref[...]
        gate = 1.0 / (1.0 + jnp.exp(-z))
        h3 = h2_ref[...] + gate * e
        r5 = lax.rsqrt(_mean_last(h3 * h3) + EPS)
        hn = h3 * r5
        diff = hn * gf_ref[...] - t_ref[...]
        loss_rows = 0.5 * _mean_last(diff * diff)
        row0 = lax.broadcasted_iota(jnp.int32, (SUBLANES, LANES), 0) == 0
        _accumulate(loss_ref, jnp.where(row0, jnp.sum(loss_rows), 0.0), first)
        dy = diff * (1.0 / D)
        _accumulate(dgf_ref, _rows_to_sublanes(dy * hn), first)
        dhn = dy * gf_ref[...]
        dh3 = r5 * (dhn - hn * _mean_last(dhn * hn))
        dh3_ref[...] = dh3
        dgate = dh3 * e
        de = dh3 * gate
        dz_ref[...] = (dgate * gate * (1.0 - gate)).astype(BF16)
        _accumulate(dgp_ref, _rows_to_sublanes(de * en), first)
        den = de * gp_ref[...]
        dpe_ref[...] = (r3 * (den - en * _mean_last(den * en))).astype(BF16)

    row = pl.BlockSpec((tm, D), lambda i, j, k: (i, 0))
    vec = pl.BlockSpec((1, D), lambda i, j, k: (0, 0))
    part = pl.BlockSpec((SUBLANES, D), lambda i, j, k: (0, 0))
    return _mm("gate_tail", gn, w_gate, _NN, (T // tm, 1, 1), row, pl.BlockSpec(w_gate.shape, lambda i, j, k: (0, 0)),
               [jax.ShapeDtypeStruct((T, D), F32), jax.ShapeDtypeStruct((T, D), BF16),
                jax.ShapeDtypeStruct((T, D), BF16), jax.ShapeDtypeStruct((SUBLANES, D), F32),
                jax.ShapeDtypeStruct((SUBLANES, D), F32), jax.ShapeDtypeStruct((SUBLANES, LANES), F32)],
               [row, row, row, part, part, pl.BlockSpec((SUBLANES, LANES), lambda i, j, k: (0, 0))], (tm, D), epilogue,
               (h2, pe, target, g_ple, g_final), [row, row, row, vec, vec],
               semantics=("arbitrary", "arbitrary", "arbitrary"))


def _rope_tables(T):
    pos = np.arange(T)
    half = HEAD_DIM // 2
    inv = (ROPE_THETA ** (-np.arange(0, half, 2, dtype=np.float32) / half)).astype(np.float32)
    ang_r = (pos // GRID_W).astype(np.float32)[:, None] * inv
    ang_c = (pos % GRID_W).astype(np.float32)[:, None] * inv
    cos = np.concatenate([np.cos(ang_r), np.cos(ang_r), np.cos(ang_c), np.cos(ang_c)], axis=-1)
    sin = np.concatenate([-np.sin(ang_r), np.sin(ang_r), -np.sin(ang_c), np.sin(ang_c)], axis=-1)
    return jnp.asarray(cos, F32), jnp.asarray(sin, F32)


def _swap32(x):
    lane = lax.broadcasted_iota(jnp.int32, x.shape, 1)
    return jnp.where((lane % 64) < 32, pltpu.roll(x, 96, 1), pltpu.roll(x, 32, 1))


def _in_proj(u, w_in, cos, sin, g_q, g_k, n_norm, tm=512):
    T, K = u.shape
    W = w_in.shape[1]
    tm = _tile(T, tm)
    n_q = n_norm * GROUP // (GROUP + 1)
    wa = n_norm * HEAD_DIM

    def epilogue(acc, extra, outs):
        c_ref, s_ref, gq_ref, gk_ref = extra
        raw_ref, o_ref = outs
        c, s = c_ref[...], s_ref[...]
        raw_ref[...] = acc[:, :wa]
        for h in range(n_norm):
            cols = slice(h * HEAD_DIM, (h + 1) * HEAD_DIM)
            xv = acc[:, cols]
            g = gq_ref[...] if h < n_q else gk_ref[...]
            xn = xv * lax.rsqrt(_mean_last(xv * xv) + EPS) * g
            o_ref[:, cols] = (xn * c + _swap32(xn) * s).astype(BF16)
        o_ref[:, wa:] = acc[:, wa:].astype(BF16)

    tab = pl.BlockSpec((tm, HEAD_DIM), lambda i, j, k: (i, 0))
    vec = pl.BlockSpec((1, HEAD_DIM), lambda i, j, k: (0, 0))
    return _mm("in_proj", u, w_in, _NN, (T // tm, 1, 1), pl.BlockSpec((tm, K), lambda i, j, k: (i, 0)),
               pl.BlockSpec((K, W), lambda i, j, k: (0, 0)),
               [jax.ShapeDtypeStruct((T, wa), F32), jax.ShapeDtypeStruct((T, W), BF16)],
               [pl.BlockSpec((tm, wa), lambda i, j, k: (i, 0)), pl.BlockSpec((tm, W), lambda i, j, k: (i, 0))],
               (tm, W), epilogue, (cos, sin, g_q, g_k), [tab, tab, vec, vec])


def _dproj(proj_a, dqa, dka_t, dva_t, dqb, dkb, dvb, cos, sin, g_q, g_k, tr=512):
    T, wa = proj_a.shape
    tr = _tile(T, tr)
    n_q = dqa.shape[1] // HEAD_DIM
    wkv = dka_t.shape[0]
    W = wa + wkv + dqb.shape[1] + dkb.shape[1] + dvb.shape[1]

    def body(p_ref, dqa_ref, dkat_ref, dvat_ref, dqb_ref, dkb_ref, dvb_ref, c_ref, s_ref, gq_ref, gk_ref,
             o_ref, dgq_ref, dgk_ref):
        c, s = c_ref[...], s_ref[...]
        dka = dkat_ref[...].T
        dgq = jnp.zeros((SUBLANES, HEAD_DIM), F32)
        dgk = jnp.zeros((SUBLANES, HEAD_DIM), F32)
        for h in range(wa // HEAD_DIM):
            cols = slice(h * HEAD_DIM, (h + 1) * HEAD_DIM)
            xv = p_ref[:, cols]
            r = lax.rsqrt(_mean_last(xv * xv) + EPS)
            xn = xv * r
            if h < n_q:
                d = dqa_ref[:, cols]
                g = gq_ref[...]
            else:
                d = dka[:, (h - n_q) * HEAD_DIM:(h - n_q + 1) * HEAD_DIM]
                g = gk_ref[...]
            dqn = d * c + _swap32(d * s)
            part = _rows_to_sublanes(dqn * xn)
            if h < n_q:
                dgq = dgq + part
            else:
                dgk = dgk + part
            dxn = dqn * g
            o_ref[:, cols] = (r * (dxn - xn * _mean_last(dxn * xn))).astype(BF16)
        o_ref[:, wa:wa + wkv] = dvat_ref[...].T.astype(BF16)
        off = wa + wkv
        for ref in (dqb_ref, dkb_ref, dvb_ref):
            w = ref.shape[1]
            o_ref[:, off:off + w] = ref[...].astype(BF16)
            off += w
        first = pl.program_id(0) == 0
        _accumulate(dgq_ref, dgq, first)
        _accumulate(dgk_ref, dgk, first)

    def row(w):
        return pl.BlockSpec((tr, w), lambda i: (i, 0))

    col = pl.BlockSpec((wkv, tr), lambda i: (0, i))
    vec = pl.BlockSpec((1, HEAD_DIM), lambda i: (0, 0))
    part = pl.BlockSpec((SUBLANES, HEAD_DIM), lambda i: (0, 0))
    return pl.pallas_call(
        body, name="dproj", grid=(T // tr,),
        in_specs=[row(wa), row(dqa.shape[1]), col, col, row(dqb.shape[1]),
                  row(dkb.shape[1]), row(dvb.shape[1]), row(HEAD_DIM), row(HEAD_DIM), vec, vec],
        out_specs=[row(W), part, part],
        out_shape=[jax.ShapeDtypeStruct((T, W), BF16), jax.ShapeDtypeStruct((SUBLANES, HEAD_DIM), F32),
                   jax.ShapeDtypeStruct((SUBLANES, HEAD_DIM), F32)],
        compiler_params=_params(("arbitrary",)),
    )(proj_a, dqa, dka_t, dva_t, dqb, dkb, dvb, cos, sin, g_q, g_k)


def _attn_a_fwd(pb, n_q, n_kv, out_heads, tq=1024, tc=1024):
    T = pb.shape[0]
    tq, tc = _tile(T, tq), _tile(T, tc)
    scale = HEAD_DIM ** -0.5
    c = scale * LOG2E

    def body(q_ref, k_ref, v_ref, o_ref, lse_ref):
        q = q_ref[...]
        m = l = acc = None
        for j in range(T // tc):
            keys = slice(j * tc, (j + 1) * tc)
            s = lax.dot_general(q, k_ref[keys, :], _NT, preferred_element_type=F32)
            mj = jnp.max(s, axis=-1, keepdims=True)
            m_new = mj if j == 0 else jnp.maximum(m, mj)
            p = jnp.exp2((s - m_new) * c)
            pv = lax.dot_general(p.astype(BF16), v_ref[keys, :], _NN, preferred_element_type=F32)
            if j == 0:
                l, acc = jnp.sum(p, axis=-1, keepdims=True), pv
            else:
                alpha = jnp.exp2((m - m_new) * c)
                l = alpha * l + jnp.sum(p, axis=-1, keepdims=True)
                acc = alpha * acc + pv
            m = m_new
        o_ref[...] = (acc / l).astype(BF16)
        lse_ref[...] = m * scale + jnp.log(l)

    return pl.pallas_call(
        body, name="attn_a_fwd", grid=(n_kv, GROUP, T // tq),
        in_specs=[pl.BlockSpec((tq, HEAD_DIM), lambda kv, g, i: (i, kv * GROUP + g)),
                  pl.BlockSpec((T, HEAD_DIM), lambda kv, g, i: (0, n_q + kv)),
                  pl.BlockSpec((T, HEAD_DIM), lambda kv, g, i: (0, n_q + n_kv + kv))],
        out_specs=[pl.BlockSpec((tq, HEAD_DIM), lambda kv, g, i: (i, kv * GROUP + g)),
                   pl.BlockSpec((None, tq, 1), lambda kv, g, i: (kv * GROUP + g, i, 0))],
        out_shape=[jax.ShapeDtypeStruct((T, out_heads * HEAD_DIM), BF16), jax.ShapeDtypeStruct((n_q, T, 1), F32)],
        compiler_params=_params(("parallel", "parallel", "parallel")),
    )(pb, pb, pb)


def _attn_a_bwd(pb, o_cat, d_o, lse, n_q, n_kv, tq=1024, tc=256):
    T = pb.shape[0]
    tq, tc = _tile(T, tq), _tile(T, tc)
    scale = HEAD_DIM ** -0.5
    c = scale * LOG2E

    def body(q_ref, k_ref, v_ref, o_ref, do_ref, lse_ref, dq_ref, dkt_ref, dvt_ref):
        q, do = q_ref[...], do_ref[...]
        qt, dot = q.T, do.T
        delta = jnp.sum(do.astype(F32) * o_ref[...].astype(F32), axis=-1, keepdims=True)
        lse2 = lse_ref[...] * LOG2E

        @pl.when(jnp.logical_and(pl.program_id(1) == 0, pl.program_id(2) == 0))
        def _():
            dkt_ref[...] = jnp.zeros(dkt_ref.shape, F32)
            dvt_ref[...] = jnp.zeros(dvt_ref.shape, F32)

        dq = None
        for j in range(T // tc):
            keys = slice(j * tc, (j + 1) * tc)
            kc, vc = k_ref[keys, :], v_ref[keys, :]
            s = lax.dot_general(q, kc, _NT, preferred_element_type=F32)
            p = jnp.exp2(s * c - lse2)
            dp = lax.dot_general(do, vc, _NT, preferred_element_type=F32)
            ds = (p * (dp - delta) * scale).astype(BF16)
            dqj = lax.dot_general(ds, kc, _NN, preferred_element_type=F32)
            dq = dqj if dq is None else dq + dqj
            dvt_ref[:, keys] += lax.dot_general(dot, p.astype(BF16), _NN, preferred_element_type=F32)
            dkt_ref[:, keys] += lax.dot_general(qt, ds, _NN, preferred_element_type=F32)
        dq_ref[...] = dq

    qmap = lambda kv, g, i: (i, kv * GROUP + g)
    return pl.pallas_call(
        body, name="attn_a_bwd", grid=(n_kv, GROUP, T // tq),
        in_specs=[pl.BlockSpec((tq, HEAD_DIM), qmap),
                  pl.BlockSpec((T, HEAD_DIM), lambda kv, g, i: (0, n_q + kv)),
                  pl.BlockSpec((T, HEAD_DIM), lambda kv, g, i: (0, n_q + n_kv + kv)),
                  pl.BlockSpec((tq, HEAD_DIM), qmap),
                  pl.BlockSpec((tq, HEAD_DIM), qmap),
                  pl.BlockSpec((None, tq, 1), lambda kv, g, i: (kv * GROUP + g, i, 0))],
        out_specs=[pl.BlockSpec((tq, HEAD_DIM), qmap),
                   pl.BlockSpec((HEAD_DIM, T), lambda kv, g, i: (kv, 0)),
                   pl.BlockSpec((HEAD_DIM, T), lambda kv, g, i: (kv, 0))],
        out_shape=[jax.ShapeDtypeStruct((T, n_q * HEAD_DIM), F32),
                   jax.ShapeDtypeStruct((n_kv * HEAD_DIM, T), F32),
                   jax.ShapeDtypeStruct((n_kv * HEAD_DIM, T), F32)],
        compiler_params=_params(("parallel", "arbitrary", "arbitrary")),
    )(pb, pb, pb, o_cat, d_o, lse)


def _bucket_index():
    r = np.arange(BLOCK_Q)[:, None]
    j = np.arange(3 * BLOCK_Q)[None, :]
    rel = (j - BLOCK_Q) - r
    nb = N_BUCKETS // 2
    ret = np.where(rel > 0, nb, 0)
    n = np.abs(rel)
    max_exact = nb // 2
    nf = np.maximum(n, 1).astype(np.float32)
    large = max_exact + (np.log(nf / max_exact) / math.log(MAX_DISTANCE / max_exact) * (nb - max_exact)).astype(np.int32)
    large = np.minimum(large, nb - 1)
    return jnp.asarray(ret + np.where(n < max_exact, n, large), jnp.int32)


def _bias_build(idx, table_flat, n_heads, deps=()):
    def body(idx_ref, tab_ref, o_ref):
        h = pl.program_id(0)
        iv = idx_ref[...]
        acc = jnp.zeros(iv.shape, F32)
        for b in range(N_BUCKETS):
            acc = jnp.where(iv == b, tab_ref[b * n_heads + h], acc)
        r = lax.broadcasted_iota(jnp.int32, iv.shape, 0)
        j = lax.broadcasted_iota(jnp.int32, iv.shape, 1)
        o_ref[...] = jnp.where(jnp.abs(j - BLOCK_Q - r) <= WINDOW, acc, NEG_INF)

    return _pcall(
        body, deps, name="bias_build", grid=(n_heads,),
        in_specs=[pl.BlockSpec(idx.shape, lambda h: (0, 0)), pl.BlockSpec(memory_space=pltpu.SMEM)],
        out_specs=pl.BlockSpec((None,) + idx.shape, lambda h: (h, 0, 0)),
        out_shape=jax.ShapeDtypeStruct((n_heads,) + idx.shape, F32),
        compiler_params=_params(("parallel",)),
    )(idx, table_flat)


def _in_sequence(n, T):
    j = lax.broadcasted_iota(jnp.int32, (GROUP * BLOCK_Q, 3 * BLOCK_Q), 1)
    kabs = n * BLOCK_Q + j - BLOCK_Q
    return (kabs >= 0) & (kabs < T)


def _per_head_rows(values):
    head = lax.broadcasted_iota(jnp.int32, (GROUP * BLOCK_Q, 1), 0) // BLOCK_Q
    col = jnp.zeros((GROUP * BLOCK_Q, 1), F32)
    for g, v in enumerate(values):
        col = jnp.where(head == g, v, col)
    return col


def _band_specs(col, nblk, sb):
    return [pl.BlockSpec((BLOCK_Q, HEAD_DIM), lambda kv, i: (jnp.maximum(sb * i - 1, 0), col(kv))),
            pl.BlockSpec((sb * BLOCK_Q, HEAD_DIM), lambda kv, i: (i, col(kv))),
            pl.BlockSpec((BLOCK_Q, HEAD_DIM), lambda kv, i: (jnp.minimum(sb * i + sb, nblk - 1), col(kv)))]


def _head_specs(base, rows):
    return [pl.BlockSpec((rows, HEAD_DIM), functools.partial(lambda kv, i, g: (i, base + kv * GROUP + g), g=g))
            for g in range(GROUP)]


def _attn_b_fwd(pb, bias, sink, o_all, q_off, n_q, n_kv, deps=(), sb=16):
    T = pb.shape[0]
    nblk = T // BLOCK_Q
    sb = min(sb, nblk)
    tq = sb * BLOCK_Q
    scale = HEAD_DIM ** -0.5

    def body(*refs):
        q_refs = refs[0:GROUP]
        k_refs, v_refs = refs[GROUP:GROUP + 3], refs[GROUP + 3:GROUP + 6]
        bias_ref, sink_ref, o_ref, lse_ref = refs[GROUP + 6:]
        kv, i = pl.program_id(0), pl.program_id(1)
        kb = jnp.concatenate([r[...] for r in k_refs], axis=0)
        vb = jnp.concatenate([r[...] for r in v_refs], axis=0)
        bias_all = bias_ref[...].reshape(GROUP * BLOCK_Q, 3 * BLOCK_Q)
        sk = _per_head_rows([sink_ref[kv * GROUP + g] for g in range(GROUP)])
        for b in range(sb):
            rows = slice(b * BLOCK_Q, (b + 1) * BLOCK_Q)
            kw, vw = kb[b * BLOCK_Q:(b + 3) * BLOCK_Q], vb[b * BLOCK_Q:(b + 3) * BLOCK_Q]
            q = jnp.concatenate([r[rows, :] for r in q_refs], axis=0)
            s = lax.dot_general(q, kw, _NT, preferred_element_type=F32) * scale + bias_all
            if b == 0 or b == sb - 1:
                s = jnp.where(_in_sequence(i * sb + b, T), s, NEG_INF)
            m = jnp.maximum(jnp.max(s, axis=-1, keepdims=True), sk)
            p = jnp.exp(s - m)
            l = jnp.sum(p, axis=-1, keepdims=True) + jnp.exp(sk - m)
            o = (lax.dot_general(p.astype(BF16), vw, _NN, preferred_element_type=F32) / l).astype(BF16)
            lse = m + jnp.log(l)
            for g in range(GROUP):
                head = slice(g * BLOCK_Q, (g + 1) * BLOCK_Q)
                o_ref[rows, g * HEAD_DIM:(g + 1) * HEAD_DIM] = o[head]
                lse_ref[g, rows, :] = lse[head]

    first_group = o_all.shape[1] // (GROUP * HEAD_DIM) - n_kv
    return _pcall(
        body, deps, into=(o_all, 0), name="attn_b_fwd", grid=(n_kv, nblk // sb),
        in_specs=[*_head_specs(q_off, tq),
                  *_band_specs(lambda kv: q_off + n_q + kv, nblk, sb),
                  *_band_specs(lambda kv: q_off + n_q + n_kv + kv, nblk, sb),
                  pl.BlockSpec((GROUP, BLOCK_Q, 3 * BLOCK_Q), lambda kv, i: (kv, 0, 0)),
                  pl.BlockSpec(memory_space=pltpu.SMEM)],
        out_specs=[pl.BlockSpec((tq, GROUP * HEAD_DIM), lambda kv, i: (i, first_group + kv)),
                   pl.BlockSpec((GROUP, tq, 1), lambda kv, i: (kv, i, 0))],
        out_shape=[jax.ShapeDtypeStruct(o_all.shape, BF16), jax.ShapeDtypeStruct((n_q, T, 1), F32)],
        compiler_params=_params(("parallel", "parallel")),
    )(*([pb] * (GROUP + 6)), bias, sink)


def _attn_b_bwd(pb, o_cat, d_o, lse, bias, sink, q_off, n_q, n_kv, o_off, deps=(), sb=16):
    T = pb.shape[0]
    nblk = T // BLOCK_Q
    sb = min(sb, nblk)
    tq = sb * BLOCK_Q
    scale = HEAD_DIM ** -0.5

    def body(*refs):
        q_refs = refs[0:GROUP]
        k_refs, v_refs = refs[GROUP:GROUP + 3], refs[GROUP + 3:GROUP + 6]
        o_refs, do_refs = refs[GROUP + 6:2 * GROUP + 6], refs[2 * GROUP + 6:3 * GROUP + 6]
        lse_ref, bias_ref, sink_ref, dq_ref, dk_ref, dv_ref, dbias_ref, dsink_ref, dkb_ref, dvb_ref = refs[3 * GROUP + 6:]
        kv, i = pl.program_id(0), pl.program_id(1)
        first = i == 0

        @pl.when(first)
        def _():
            dk_ref[...] = jnp.zeros(dk_ref.shape, F32)
            dv_ref[...] = jnp.zeros(dv_ref.shape, F32)
            dbias_ref[...] = jnp.zeros(dbias_ref.shape, F32)

        kb = jnp.concatenate([r[...] for r in k_refs], axis=0)
        vb = jnp.concatenate([r[...] for r in v_refs], axis=0)
        dkb_ref[...] = jnp.zeros(dkb_ref.shape, F32)
        dvb_ref[...] = jnp.zeros(dvb_ref.shape, F32)
        row = lax.broadcasted_iota(jnp.int32, (SUBLANES, LANES), 0)
        dsink = jnp.zeros((SUBLANES, LANES), F32)
        bias_all = bias_ref[...].reshape(GROUP * BLOCK_Q, 3 * BLOCK_Q)
        sk = _per_head_rows([sink_ref[kv * GROUP + g] for g in range(GROUP)])
        for b in range(sb):
            rows = slice(b * BLOCK_Q, (b + 1) * BLOCK_Q)
            win = slice(b * BLOCK_Q, (b + 3) * BLOCK_Q)
            kw, vw = kb[win], vb[win]
            q = jnp.concatenate([r[rows, :] for r in q_refs], axis=0)
            do = jnp.concatenate([r[rows, :] for r in do_refs], axis=0)
            o = jnp.concatenate([r[rows, :] for r in o_refs], axis=0)
            lse = jnp.concatenate([lse_ref[g, rows, :] for g in range(GROUP)], axis=0)
            delta = jnp.sum(do.astype(F32) * o.astype(F32), axis=-1, keepdims=True)
            s = lax.dot_general(q, kw, _NT, preferred_element_type=F32) * scale + bias_all
            if b == 0 or b == sb - 1:
                s = jnp.where(_in_sequence(i * sb + b, T), s, NEG_INF)
            p = jnp.exp(s - lse)
            dp = lax.dot_general(do, vw, _NT, preferred_element_type=F32)
            ds = p * (dp - delta)
            dbias_ref[...] += ds.reshape(GROUP, BLOCK_Q, 3 * BLOCK_Q)
            sunk = jnp.exp(sk - lse) * delta
            for g in range(GROUP):
                dsink = dsink + jnp.where(row == g, -jnp.sum(sunk[g * BLOCK_Q:(g + 1) * BLOCK_Q]), 0.0)
            dsb = (ds * scale).astype(BF16)
            dq = lax.dot_general(dsb, kw, _NN, preferred_element_type=F32).astype(BF16)
            for g in range(GROUP):
                dq_ref[rows, g * HEAD_DIM:(g + 1) * HEAD_DIM] = dq[g * BLOCK_Q:(g + 1) * BLOCK_Q]
            dkb_ref[win, :] += lax.dot_general(dsb, q, _TN, preferred_element_type=F32)
            dvb_ref[win, :] += lax.dot_general(p.astype(BF16), do, _TN, preferred_element_type=F32)
        _accumulate(dsink_ref, dsink, first)

        before = pl.ds(pl.multiple_of(jnp.maximum(sb * i - 1, 0) * BLOCK_Q, BLOCK_Q), BLOCK_Q)
        own = pl.ds(pl.multiple_of(i * tq, BLOCK_Q), tq)
        after = pl.ds(pl.multiple_of(jnp.minimum(sb * i + sb, nblk - 1) * BLOCK_Q, BLOCK_Q), BLOCK_Q)
        for acc_ref, band_ref in ((dk_ref, dkb_ref), (dv_ref, dvb_ref)):
            acc_ref[before, :] += band_ref[0:BLOCK_Q, :]
            acc_ref[own, :] += band_ref[BLOCK_Q:BLOCK_Q + tq, :]
            acc_ref[after, :] += band_ref[BLOCK_Q + tq:, :]

    return _pcall(
        body, deps, name="attn_b_bwd", grid=(n_kv, nblk // sb),
        in_specs=[*_head_specs(q_off, tq),
                  *_band_specs(lambda kv: q_off + n_q + kv, nblk, sb),
                  *_band_specs(lambda kv: q_off + n_q + n_kv + kv, nblk, sb),
                  *_head_specs(o_off, tq), *_head_specs(o_off, tq),
                  pl.BlockSpec((GROUP, tq, 1), lambda kv, i: (kv, i, 0)),
                  pl.BlockSpec((GROUP, BLOCK_Q, 3 * BLOCK_Q), lambda kv, i: (kv, 0, 0)),
                  pl.BlockSpec(memory_space=pltpu.SMEM)],
        out_specs=[pl.BlockSpec((tq, GROUP * HEAD_DIM), lambda kv, i: (i, kv)),
                   pl.BlockSpec((T, HEAD_DIM), lambda kv, i: (0, kv)),
                   pl.BlockSpec((T, HEAD_DIM), lambda kv, i: (0, kv)),
                   pl.BlockSpec((GROUP, BLOCK_Q, 3 * BLOCK_Q), lambda kv, i: (kv, 0, 0)),
                   pl.BlockSpec((None, SUBLANES, LANES), lambda kv, i: (kv, 0, 0))],
        out_shape=[jax.ShapeDtypeStruct((T, n_q * HEAD_DIM), BF16),
                   jax.ShapeDtypeStruct((T, n_kv * HEAD_DIM), F32),
                   jax.ShapeDtypeStruct((T, n_kv * HEAD_DIM), F32),
                   jax.ShapeDtypeStruct((n_q, BLOCK_Q, 3 * BLOCK_Q), F32),
                   jax.ShapeDtypeStruct((n_kv, SUBLANES, LANES), F32)],
        scratch_shapes=[pltpu.VMEM((tq + 2 * BLOCK_Q, HEAD_DIM), F32), pltpu.VMEM((tq + 2 * BLOCK_Q, HEAD_DIM), F32)],
        compiler_params=_params(("parallel", "arbitrary")),
    )(*([pb] * (GROUP + 6)), *([o_cat] * GROUP), *([d_o] * GROUP), lse, bias, sink)


def _table_grads(dbias, dsink_raw, idx):
    n_heads = dbias.shape[0]
    n_kv = dsink_raw.shape[0]

    def body(db_ref, ds_ref, idx_ref, dt_ref, dsk_ref):
        iv = idx_ref[...]
        row = lax.broadcasted_iota(jnp.int32, (SUBLANES, LANES), 0)
        lane = lax.broadcasted_iota(jnp.int32, (SUBLANES, LANES), 1)
        dsk = jnp.zeros((SUBLANES, LANES), F32)
        for h in range(n_heads):
            d = db_ref[h]
            acc = jnp.zeros((SUBLANES, LANES), F32)
            for b in range(N_BUCKETS):
                acc = jnp.where((row == 0) & (lane == b), jnp.sum(jnp.where(iv == b, d, 0.0)), acc)
            dt_ref[:, h * LANES:(h + 1) * LANES] = acc
            raw = ds_ref[h // GROUP]
            val = jnp.sum(jnp.where((row == h % GROUP) & (lane == 0), raw, 0.0))
            dsk = jnp.where((row == 0) & (lane == h), val, dsk)
        dsk_ref[...] = dsk

    return pl.pallas_call(
        body, name="table_grads",
        in_specs=[pl.BlockSpec(memory_space=pltpu.VMEM)] * 3,
        out_specs=[pl.BlockSpec(memory_space=pltpu.VMEM)] * 2,
        out_shape=[jax.ShapeDtypeStruct((SUBLANES, n_heads * LANES), F32),
                   jax.ShapeDtypeStruct((SUBLANES, LANES), F32)],
        compiler_params=_params(),
    )(dbias, dsink_raw, idx)


def _position():
    x, y, c = lax.axis_index("x"), lax.axis_index("y"), lax.axis_index("c")
    return x, y, c


def _hbm(a):
    return pltpu.with_memory_space_constraint(a, pltpu.HBM)


def _split_start(name, bufs, sem_shapes, issue):
    nb, ns = len(bufs), len(sem_shapes)

    def body(*refs):
        buf_refs = refs[:nb]
        sems = refs[nb:nb + ns]
        token = refs[nb + ns + nb]
        issue(buf_refs, sems)
        token[...] = jnp.zeros(token.shape, F32)

    outs = pl.pallas_call(
        body, name=name,
        in_specs=[_HBM] * nb,
        out_specs=[_SEM] * ns + [_HBM] * nb + [_VMEM],
        out_shape=[pltpu.SemaphoreType.DMA(s) for s in sem_shapes] + [pltpu.HBM(b.shape, b.dtype) for b in bufs]
        + [jax.ShapeDtypeStruct((SUBLANES, LANES), F32)],
        input_output_aliases={i: ns + i for i in range(nb)},
        compiler_params=pltpu.CompilerParams(has_side_effects=_EFFECT),
    )(*[_hbm(b) for b in bufs])
    return outs[:ns], outs[ns:ns + nb], outs[-1]


def _split_wait(name, bufs, send, recv, counts, size_of, after):
    nb = len(bufs)

    def body(*refs):
        buf_refs = refs[:nb]
        send_ref, recv_ref = refs[nb], refs[nb + 1]
        x, y, c = _position()
        for w, n in enumerate(counts):
            ref = size_of(buf_refs, w)
            for k in range(n):
                s = sum(counts[:w]) + k
                cp = pltpu.make_async_remote_copy(
                    src_ref=ref, dst_ref=ref, send_sem=send_ref.at[s], recv_sem=recv_ref.at[s],
                    device_id=(x, y, c), device_id_type=MESH)
                cp.wait_send()
                cp.wait_recv()

    return pl.pallas_call(
        body, name=name,
        in_specs=[_HBM] * nb + [_SEM, _SEM, _ANY],
        out_specs=[_HBM] * nb,
        out_shape=[pltpu.HBM(b.shape, b.dtype) for b in bufs],
        input_output_aliases={i: i for i in range(nb)},
        compiler_params=pltpu.CompilerParams(has_side_effects=_EFFECT),
    )(*bufs, send, recv, after)


def _block_of(pos):
    return 4 * pos[0] + 2 * pos[1] + pos[2]


def _shard_of(ref, blk, by_cols):
    aligned = (lambda v, a: v) if isinstance(blk, int) else pl.multiple_of
    if by_cols:
        n = ref.shape[1] // N_DEV
        return ref.at[:, pl.ds(aligned(blk * n, LANES), n)]
    r = ref.shape[0] // N_DEV
    return ref.at[pl.ds(aligned(blk * r, SUBLANES), r), :]


def _place_own(name, land, shard, by_cols, tr=256):
    r, n = shard.shape
    tr = _tile(r, tr)
    mine = _block_of(_position()).astype(jnp.int32).reshape(1)

    def body(m_ref, land_ref, s_ref, o_ref):
        o_ref[...] = s_ref[...]

    if by_cols:
        out = pl.BlockSpec((tr, n), lambda i, m_ref: (i, m_ref[0]))
    else:
        out = pl.BlockSpec((tr, n), lambda i, m_ref: (m_ref[0] * (r // tr) + i, 0))
    return pl.pallas_call(
        body, name=name,
        grid_spec=pltpu.PrefetchScalarGridSpec(
            num_scalar_prefetch=1, grid=(r // tr,),
            in_specs=[_ANY, pl.BlockSpec((tr, n), lambda i, m_ref: (i, 0))], out_specs=out),
        out_shape=jax.ShapeDtypeStruct(land.shape, land.dtype),
        input_output_aliases={1: 0},
        compiler_params=_params(("parallel",)),
    )(mine, land, shard)


def _gather_start(name, shards, by_cols, groups, after=None):
    nw = len(shards)
    lands = [lax.empty((s.shape[0], s.shape[1] * N_DEV) if cols else (s.shape[0] * N_DEV, s.shape[1]), s.dtype)
             for s, cols in zip(shards, by_cols)]
    order = [] if after is None else [after]

    def issue(bufs, sems):
        x, y, c = _position()
        peers = [(x, y, 1 - c), (1 - x, y, c), (x, 1 - y, c), (1 - x, 1 - y, c)]
        for gi, grp in enumerate(groups):
            for wi, w in enumerate(grp):
                for k, peer in enumerate(peers):
                    pltpu.make_async_remote_copy(
                        src_ref=bufs[w], dst_ref=_shard_of(bufs[nw + w], _block_of((x, y, c)), by_cols[w]),
                        send_sem=sems[2 * gi].at[4 * wi + k], recv_sem=sems[2 * gi + 1].at[4 * wi + k],
                        device_id=peer, device_id_type=MESH).start()

    sem_shapes = [(4 * len(g),) for g in groups for _ in range(2)]
    sems, thru, token = _split_start(name, list(shards) + lands + order, sem_shapes, issue)
    return sems, thru[:nw], thru[nw:2 * nw], token


def _gather_forward(name, lands, by_cols):
    nw = len(lands)

    def issue(land, sems):
        x, y, c = _position()
        for w in range(nw):
            for k, chip in enumerate([(1 - x, y), (x, 1 - y), (1 - x, 1 - y)]):
                blk = _shard_of(land[w], _block_of((*chip, c)), by_cols[w])
                pltpu.make_async_remote_copy(
                    src_ref=blk, dst_ref=blk, send_sem=sems[0].at[3 * w + k], recv_sem=sems[1].at[3 * w + k],
                    device_id=(x, y, 1 - c), device_id_type=MESH).start()

    return _split_start(name, lands, [(3 * nw,), (3 * nw,)], issue)


def _first_block(bufs, w, offset=0):
    return bufs[offset + w].at[0]


_PEER_FLIPS = ((0, 0, 1), (1, 0, 0), (1, 0, 1), (0, 1, 0), (0, 1, 1), (1, 1, 0), (1, 1, 1))


def _scatter_start(name, grads, by_cols):
    nw = len(grads)
    lands = []
    for g, cols in zip(grads, by_cols):
        shard = (g.shape[0], g.shape[1] // N_DEV) if cols else (g.shape[0] // N_DEV, g.shape[1])
        lands.append(lax.empty((N_DEV,) + shard, g.dtype))

    def issue(bufs, sems):
        x, y, c = _position()
        flip = lambda v, f: 1 - v if f else v
        for w in range(nw):
            for k, (fx, fy, fc) in enumerate(_PEER_FLIPS):
                peer = (flip(x, fx), flip(y, fy), flip(c, fc))
                pltpu.make_async_remote_copy(
                    src_ref=_shard_of(bufs[w], _block_of(peer), by_cols[w]), dst_ref=bufs[nw + w].at[_block_of((x, y, c))],
                    send_sem=sems[0].at[7 * w + k], recv_sem=sems[1].at[7 * w + k],
                    device_id=peer, device_id_type=MESH).start()

    return _split_start(name, list(grads) + lands, [(7 * nw,), (7 * nw,)], issue)


def _adam(w, g, m, v):
    m = ADAM_B1 * m + (1.0 - ADAM_B1) * g
    v = ADAM_B2 * v + (1.0 - ADAM_B2) * (g * g)
    m_hat = m / (1.0 - ADAM_B1 ** ADAM_STEP)
    v_hat = v / (1.0 - ADAM_B2 ** ADAM_STEP)
    delta = -ADAM_LR * (m_hat / (jnp.sqrt(v_hat) + ADAM_EPS) + ADAM_WD * w)
    return delta, m, v


def _sum_adam(name, landed, grad, by_cols, w, m, v, tr=256):
    R, C = w.shape
    tr = _tile(R, tr if C > 1024 else 2 * tr)
    mine = _block_of(_position()).astype(jnp.int32).reshape(1)

    def body(me_ref, l_ref, own_ref, w_ref, m_ref, v_ref, g_ref, d_ref, nm_ref, nv_ref):
        own = own_ref[...].astype(F32)
        g = None
        for d in range(N_DEV):
            part = jnp.where(me_ref[0] == d, own, l_ref[d].astype(F32))
            g = part if g is None else g + part
        g_ref[...] = g
        d_ref[...], nm_ref[...], nv_ref[...] = _adam(w_ref[...], g, m_ref[...], v_ref[...])

    tile = pl.BlockSpec((tr, C), lambda i, me_ref: (i, 0))
    if by_cols:
        own = pl.BlockSpec((tr, C), lambda i, me_ref: (i, me_ref[0]))
    else:
        own = pl.BlockSpec((tr, C), lambda i, me_ref: (me_ref[0] * (R // tr) + i, 0))
    return pl.pallas_call(
        body, name=name,
        grid_spec=pltpu.PrefetchScalarGridSpec(
            num_scalar_prefetch=1, grid=(R // tr,),
            in_specs=[pl.BlockSpec((N_DEV, tr, C), lambda i, me_ref: (0, i, 0)), own, tile, tile, tile],
            out_specs=[tile] * 4),
        out_shape=[jax.ShapeDtypeStruct((R, C), F32)] * 4,
        compiler_params=_params(("parallel",)),
    )(mine, landed, grad, w, m, v)


def _small_all_reduce(parts, deps=()):
    W = parts.shape[1]

    def body(p_ref, o_ref, slots, send_sems, recv_sems):
        x, y, c = _position()
        me = 4 * x + 2 * y + c
        slots[me] = jnp.sum(p_ref[...], axis=0, keepdims=True)
        peers = [(x, y, 1 - c), (1 - x, y, c), (1 - x, y, 1 - c), (x, 1 - y, c), (x, 1 - y, 1 - c),
                 (1 - x, 1 - y, c), (1 - x, 1 - y, 1 - c)]
        copies = []
        for k, peer in enumerate(peers):
            cp = pltpu.make_async_remote_copy(
                src_ref=slots.at[me], dst_ref=slots.at[me], send_sem=send_sems.at[k], recv_sem=recv_sems.at[k],
                device_id=peer, device_id_type=MESH)
            cp.start()
            copies.append(cp)
        for cp in copies:
            cp.wait()
        total = slots[0]
        for d in range(1, N_DEV):
            total = total + slots[d]
        o_ref[...] = total

    return _pcall(
        body, deps, name="small_all_reduce",
        in_specs=[pl.BlockSpec(memory_space=pltpu.VMEM)], out_specs=pl.BlockSpec(memory_space=pltpu.VMEM),
        out_shape=jax.ShapeDtypeStruct((1, W), F32),
        scratch_shapes=[pltpu.VMEM((N_DEV, 1, W), F32), pltpu.SemaphoreType.DMA((7,)), pltpu.SemaphoreType.DMA((7,))],
    )(parts)


def _adam_small(w, g, m, v):
    def body(w_ref, g_ref, m_ref, v_ref, d_ref, nm_ref, nv_ref):
        d_ref[...], nm_ref[...], nv_ref[...] = _adam(w_ref[...], g_ref[...], m_ref[...], v_ref[...])

    return pl.pallas_call(
        body, name="adam_small",
        in_specs=[pl.BlockSpec(memory_space=pltpu.VMEM)] * 4, out_specs=[pl.BlockSpec(memory_space=pltpu.VMEM)] * 3,
        out_shape=[jax.ShapeDtypeStruct(w.shape, F32)] * 3,
    )(w, g, m, v)


_GATHER_GROUPS = (("w_in",), ("w_out", "w_up", "ple_w"), ("w_down", "w_gate"))
_COL_SHARDED = ("w_in", "w_up", "ple_w")


class _MeshComm:
    def __init__(self, w, mom, var):
        self.w, self.mom, self.var = w, mom, var
        self.out = {}
        self._scatters = {}

    def gather_begin(self):
        self._groups = {}
        token = None
        for tag, first, group_list in (("gather_start0", 0, _GATHER_GROUPS[:1]), ("gather_start1", 1, _GATHER_GROUPS[1:])):
            names = [n for g in group_list for n in g]
            idx = {n: i for i, n in enumerate(names)}
            by_cols = [n in _COL_SHARDED for n in names]
            sems, src, lands, token = _gather_start(tag, [self.w[n].astype(BF16) for n in names], by_cols,
                                                    [[idx[n] for n in g] for g in group_list], token)
            lands = [_place_own("place_" + n, land, s, cols) for n, land, s, cols in zip(names, lands, src, by_cols)]
            for k, g in enumerate(group_list):
                self._groups[first + k] = (sems[2 * k], sems[2 * k + 1], [src[idx[n]] for n in g],
                                           [lands[idx[n]] for n in g])
        return token

    @staticmethod
    def _shard_size(names, offset):
        return lambda bufs, w: _shard_of(bufs[offset + w], 0, names[w] in _COL_SHARDED)

    def gather_arrive(self, gi, after):
        names = _GATHER_GROUPS[gi]
        send, recv, src, lands = self._groups[gi]
        out = _split_wait("gather_arrive%d" % gi, src + lands, send, recv, [4] * len(names),
                          self._shard_size(names, len(names)), after)
        self._arrived = out[len(names):]

    def gather_forward(self, gi):
        by_cols = [n in _COL_SHARDED for n in _GATHER_GROUPS[gi]]
        self._fsems, self._fthru, token = _gather_forward("gather_forward%d" % gi, self._arrived, by_cols)
        return token

    def gather_finish(self, gi, after):
        names = _GATHER_GROUPS[gi]
        out = _split_wait("gather_finish%d" % gi, self._fthru, self._fsems[0], self._fsems[1], [3] * len(names),
                          self._shard_size(names, 0), after)
        return dict(zip(names, out))

    def reduce_begin(self, key, grads):
        names = list(grads)
        sems, thru, token = _scatter_start("scatter_start_" + key, [grads[n] for n in names],
                                           [n in _COL_SHARDED for n in names])
        self._scatters[key] = (names, sems, thru)
        return token

    def reduce_finish(self, key, after):
        names, sems, thru = self._scatters[key]
        nw = len(names)
        out = _split_wait("scatter_wait_" + key, thru, sems[0], sems[1], [N_DEV - 1] * nw,
                          functools.partial(_first_block, offset=nw), after)
        for i, n in enumerate(names):
            self.out[n] = _sum_adam("adam_" + n, out[nw + i], out[i], n in _COL_SHARDED, self.w[n], self.mom[n],
                                    self.var[n])


def _step(x, p, target, gains, comm):
    T, D = x.shape
    n_q = D // (2 * HEAD_DIM)
    n_kv = n_q // GROUP
    cos, sin = _rope_tables(T)
    idx = _bucket_index()

    t = comm.gather_begin()
    u = _rms_fwd("norm_attn", x, gains["attn_norm_g"], deps=(t,))
    comm.gather_arrive(0, u)
    t = comm.gather_forward(0)
    bias = _bias_build(idx, gains["rel_bias_table"].reshape(-1), n_q, deps=(t,))
    full = comm.gather_finish(0, bias)
    proj_a, pb = _in_proj(u, full["w_in"], cos, sin, gains["q_norm_g"], gains["k_norm_g"], n_q + n_kv)
    o_a, lse_a = _attn_a_fwd(pb, n_q, n_kv, 2 * n_q)
    comm.gather_arrive(1, lse_a)
    t = comm.gather_forward(1)
    sink = gains["sink_logits"].reshape(-1)
    b_off = n_q + 2 * n_kv
    o_cat, lse_b = _attn_b_fwd(pb, bias, sink, o_a, b_off, n_q, n_kv, deps=(t,))
    full.update(comm.gather_finish(1, lse_b))
    h1, m_in = _mm_nn_rms("out_proj", o_cat, full["w_out"], x, gains["mlp_norm_g"])

    def up_epilogue(acc, extra, outs):
        outs[0][...] = acc.astype(BF16)
        r = jnp.maximum(acc, 0.0)
        outs[1][...] = (r * r).astype(BF16)

    a_act, f_act = _mm_nn("up_proj", m_in, full["w_up"], epilogue=up_epilogue, out_dtypes=[BF16, BF16], tn=2048)
    comm.gather_arrive(2, f_act)
    t = comm.gather_forward(2)
    p_b = p.astype(BF16)
    pe = _mm_nn("ple_proj", p_b, full["ple_w"], deps=(t,))
    full.update(comm.gather_finish(2, pe))
    h2 = _mm_nn("down_proj", f_act, full["w_down"], epilogue=_store_add, extras=(h1,), tn=256)
    gn = _rms_fwd("norm_gate", h2, gains["gate_norm_g"])

    dh3, dz, dpe, dg_final, dg_ple, loss_part = _gate_tail(gn, full["w_gate"], h2, pe, target, gains["ple_norm_g"],
                                                           gains["final_norm_g"])
    gw_gate = _mm_tn("grad_w_gate", gn, dz)
    gw_ple = _mm_tn("grad_ple_w", p_b, dpe)
    dh2, dh2_b, dg_gate = _mm_nt_rms_bwd("d_gate_in", dz, full["w_gate"], h2, gains["gate_norm_g"], dh3)
    gw_down = _mm_tn("grad_w_down", f_act, dh2_b, tn=1024)
    t = comm.reduce_begin("b", dict(w_gate=gw_gate, ple_w=gw_ple, w_down=gw_down))

    def act_bwd(acc, extra, outs):
        outs[0][...] = (acc * (2.0 * jnp.maximum(extra[0][...].astype(F32), 0.0))).astype(BF16)

    da = _mm_nt("d_act", dh2_b, full["w_down"], out_dtype=BF16, epilogue=act_bwd, extras=(a_act,), tn=2048, deps=(t,))
    gw_up = _mm_tn("grad_w_up", m_in, da, tn=1024)
    dm = _mm_nt("d_mlp_in", da, full["w_up"], out_dtype=BF16, tn=512)
    dh1, dh1_b, dg_mlp = _rms_bwd("norm_mlp_bwd", dm, h1, gains["mlp_norm_g"], dh2)
    gw_out = _mm_tn("grad_w_out", o_cat, dh1_b)
    t = comm.reduce_begin("d", dict(w_up=gw_up, w_out=gw_out))
    d_o = _mm_nt("d_attn_out", dh1_b, full["w_out"], out_dtype=BF16, deps=(t,))
    dqa, dka_t, dva_t = _attn_a_bwd(pb, o_cat, d_o, lse_a, n_q, n_kv)
    dqb, dkb, dvb, dbias, dsink_raw = _attn_b_bwd(pb, o_cat, d_o, lse_b, bias, sink, b_off, n_q, n_kv, n_q)
    dtable, dsink = _table_grads(dbias, dsink_raw, idx)
    dproj, dg_q, dg_k = _dproj(proj_a, dqa, dka_t, dva_t, dqb, dkb, dvb, cos, sin, gains["q_norm_g"], gains["k_norm_g"])
    gw_in = _mm_tn("grad_w_in", u, dproj, tn=1024)
    t = comm.reduce_begin("e", dict(w_in=gw_in))
    dx, dg_attn = _mm_nt_rms_bwd("d_attn_in", dproj, full["w_in"], x, gains["attn_norm_g"], dh1, with_bf16=False,
                                 deps=(t,))
    for key in "bd":
        comm.reduce_finish(key, dx)

    parts = jnp.concatenate([dg_attn, dg_mlp, dg_ple, dg_gate, dg_final, dg_q, dg_k, dtable, dsink, loss_part], axis=1)
    return dx, parts


_SHARDED = ("w_in", "w_out", "w_up", "w_down", "ple_w", "w_gate")
_VECTORS = ("attn_norm_g", "mlp_norm_g", "ple_norm_g", "gate_norm_g", "final_norm_g")
_ORDER = ("attn_norm_g", "w_in", "q_norm_g", "k_norm_g", "sink_logits", "w_out", "mlp_norm_g", "w_up", "w_down",
          "ple_w", "ple_norm_g", "gate_norm_g", "w_gate", "rel_bias_table", "final_norm_g")


def _pack_small(vals, n_heads):
    lane_pad = lambda v: jnp.pad(v, ((0, 0), (0, LANES - v.shape[1])))
    table = lane_pad(vals["rel_bias_table"].T).reshape(1, n_heads * LANES)
    return jnp.concatenate(
        [vals[n].reshape(1, -1) for n in _VECTORS] + [vals["q_norm_g"], vals["k_norm_g"], table,
                                                      lane_pad(vals["sink_logits"]), jnp.zeros((1, LANES), F32)], axis=1)


def _unpack_small(row, like, n_heads):
    out, off = {}, 0
    for n in _VECTORS:
        out[n] = row[:, off:off + like[n].size].reshape(like[n].shape)
        off += like[n].size
    for n in ("q_norm_g", "k_norm_g"):
        out[n] = row[:, off:off + LANES]
        off += LANES
    out["rel_bias_table"] = row[:, off:off + n_heads * LANES].reshape(n_heads, LANES)[:, :N_BUCKETS].T
    off += n_heads * LANES
    out["sink_logits"] = row[:, off:off + n_heads]
    off += LANES
    return out, row[0, off]


def kernel(x, p, attn_norm_g, w_in, q_norm_g, k_norm_g, sink_logits, w_out, mlp_norm_g, w_up, w_down, ple_w, ple_norm_g, gate_norm_g, w_gate, rel_bias_table, final_norm_g, loss_target, m_attn_norm_g, m_w_in, m_q_norm_g, m_k_norm_g, m_sink_logits, m_w_out, m_mlp_norm_g, m_w_up, m_w_down, m_ple_w, m_ple_norm_g, m_gate_norm_g, m_w_gate, m_rel_bias_table, m_final_norm_g, v_attn_norm_g, v_w_in, v_q_norm_g, v_k_norm_g, v_sink_logits, v_w_out, v_mlp_norm_g, v_w_up, v_w_down, v_ple_w, v_ple_norm_g, v_gate_norm_g, v_w_gate, v_rel_bias_table, v_final_norm_g):
    w = dict(attn_norm_g=attn_norm_g, w_in=w_in[0], q_norm_g=q_norm_g, k_norm_g=k_norm_g, sink_logits=sink_logits,
             w_out=w_out[0], mlp_norm_g=mlp_norm_g, w_up=w_up[0], w_down=w_down[0], ple_w=ple_w[0],
             ple_norm_g=ple_norm_g, gate_norm_g=gate_norm_g, w_gate=w_gate[0], rel_bias_table=rel_bias_table,
             final_norm_g=final_norm_g)
    mom = dict(attn_norm_g=m_attn_norm_g, w_in=m_w_in[0], q_norm_g=m_q_norm_g, k_norm_g=m_k_norm_g,
               sink_logits=m_sink_logits, w_out=m_w_out[0], mlp_norm_g=m_mlp_norm_g, w_up=m_w_up[0],
               w_down=m_w_down[0], ple_w=m_ple_w[0], ple_norm_g=m_ple_norm_g, gate_norm_g=m_gate_norm_g,
               w_gate=m_w_gate[0], rel_bias_table=m_rel_bias_table, final_norm_g=m_final_norm_g)
    var = dict(attn_norm_g=v_attn_norm_g, w_in=v_w_in[0], q_norm_g=v_q_norm_g, k_norm_g=v_k_norm_g,
               sink_logits=v_sink_logits, w_out=v_w_out[0], mlp_norm_g=v_mlp_norm_g, w_up=v_w_up[0],
               w_down=v_w_down[0], ple_w=v_ple_w[0], ple_norm_g=v_ple_norm_g, gate_norm_g=v_gate_norm_g,
               w_gate=v_w_gate[0], rel_bias_table=v_rel_bias_table, final_norm_g=v_final_norm_g)
    D = x.shape[-1]
    n_heads = D // (2 * HEAD_DIM)

    gains = {n: w[n] for n in w if n not in _SHARDED}
    gains["final_norm_g"] = final_norm_g.reshape(1, -1)

    comm = _MeshComm(w, mom, var)
    dx, parts = _step(x[0], p[0, 0], loss_target[0], gains, comm)

    small_g = _small_all_reduce(parts, deps=[comm.out[n][0] for n in comm.out])
    comm.reduce_finish("e", small_g)

    g_out, d_out, m_out, v_out = {}, {}, {}, {}
    for n in _SHARDED:
        g, d, nm, nv = comm.out[n]
        g_out[n], d_out[n], m_out[n], v_out[n] = g[None], d[None], nm[None], nv[None]

    small = {n: v for n, v in w.items() if n not in _SHARDED}
    pack = lambda vals: _pack_small({n: vals[n] for n in small}, n_heads)
    sd, sm, sv = _adam_small(pack(w), small_g, pack(mom), pack(var))
    sg, loss = _unpack_small(small_g, small, n_heads)
    g_out.update(sg)
    for dst, row in ((d_out, sd), (m_out, sm), (v_out, sv)):
        dst.update(_unpack_small(row, small, n_heads)[0])

    return (loss, dx[None], *[g_out[n] for n in _ORDER], *[d_out[n] for n in _ORDER],
            *[m_out[n] for n in _ORDER], *[v_out[n] for n in _ORDER])
```

```python
import functools
import math

import numpy as np
import jax
import jax.numpy as jnp
from jax import lax
from jax.experimental import pallas as pl
from jax.experimental.pallas import tpu as pltpu

F32 = jnp.float32
BF16 = jnp.bfloat16

N_DEV = 8
N_CHIP = 4
HEAD_DIM = 128
GROUP = 4
GRID_W = 64
WINDOW = 128
BLOCK_Q = 128
N_BUCKETS = 32
MAX_DISTANCE = 128
ROPE_THETA = 10000.0
EPS = 1e-6
NEG_INF = -1e30
ADAM_LR = 0.001
ADAM_B1 = 0.9
ADAM_B2 = 0.999
ADAM_EPS = 1e-08
ADAM_WD = 0.01
ADAM_STEP = 10
LOG2E = math.log2(math.e)
LANES = 128
SUBLANES = 8
VMEM_LIMIT_BYTES = 60 * 1024 * 1024
MESH = pl.DeviceIdType.MESH

_NT = (((1,), (1,)), ((), ()))
_NN = (((1,), (0,)), ((), ()))
_TN = (((0,), (0,)), ((), ()))


def _tile(dim, pref):
    return pref if dim % pref == 0 else dim


def _params(sem=None):
    return pltpu.CompilerParams(dimension_semantics=sem, vmem_limit_bytes=VMEM_LIMIT_BYTES)


_HBM = pl.BlockSpec(memory_space=pltpu.HBM)
_SEM = pl.BlockSpec(memory_space=pltpu.SEMAPHORE)
_ANY = pl.BlockSpec(memory_space=pl.ANY)
_VMEM = pl.BlockSpec(memory_space=pltpu.VMEM)
_EFFECT = pltpu.SideEffectType.DATAFLOW_SIDE_EFFECTING


def _pcall(body, deps=(), *, in_specs, into=None, **kw):
    deps = [d for d in deps if d is not None]
    nd = len(deps)
    if into is not None:
        deps = [into[0]] + deps
        nd += 1
        kw["input_output_aliases"] = {0: into[1]}

    def wrapped(*refs):
        body(*refs[nd:])

    call = pl.pallas_call(wrapped, in_specs=[_ANY] * nd + list(in_specs), **kw)
    return lambda *args: call(*deps, *args)


def _mm(name, a, b, dims, grid, a_spec, b_spec, out_shape, out_specs, acc_shape, epilogue,
        extras=(), extra_specs=(), deps=(), semantics=("parallel", "parallel", "arbitrary")):
    nk = grid[2]
    n_extra = len(extras)

    def body(*refs):
        a_ref, b_ref = refs[0], refs[1]
        extra = refs[2:2 + n_extra]
        outs = refs[2 + n_extra:-1]
        acc = refs[-1]
        part = lax.dot_general(a_ref[...], b_ref[...], dims, preferred_element_type=F32)
        if nk == 1:
            epilogue(part, extra, outs)
        else:
            k = pl.program_id(2)

            @pl.when(k == 0)
            def _():
                acc[...] = part

            @pl.when(k > 0)
            def _():
                acc[...] += part

            @pl.when(k == nk - 1)
            def _():
                epilogue(acc[...], extra, outs)

    return _pcall(
        body, deps, name=name, grid=grid,
        in_specs=[a_spec, b_spec, *extra_specs],
        out_specs=out_specs, out_shape=out_shape,
        scratch_shapes=[pltpu.VMEM(acc_shape if nk > 1 else (SUBLANES, LANES), F32)],
        compiler_params=_params(semantics),
    )(a, b, *extras)


def _store(dtype):
    def ep(acc, extra, outs):
        outs[0][...] = acc.astype(dtype)
    return ep


def _store_add(acc, extra, outs):
    outs[0][...] = acc + extra[0][...]


def _mm_nn(name, a, b, out_dtype=F32, epilogue=None, extras=(), n_out=1, out_dtypes=None, tm=1024, tn=1024, tk=None,
           deps=()):
    M, K = a.shape
    N = b.shape[1]
    tm, tn, tk = _tile(M, tm), _tile(N, tn), _tile(K, tk or K)
    b_spec = pl.BlockSpec((tk, tn), lambda i, j, k: (k, j))
    grid = (M // tm, N // tn, K // tk)
    o_spec = pl.BlockSpec((tm, tn), lambda i, j, k: (i, j))
    out_dtypes = out_dtypes or [out_dtype] * n_out
    out_shape = [jax.ShapeDtypeStruct((M, N), d) for d in out_dtypes]
    res = _mm(name, a, b, _NN, grid, pl.BlockSpec((tm, tk), lambda i, j, k: (i, k)), b_spec,
              out_shape, [o_spec] * len(out_dtypes), (tm, tn), epilogue or _store(out_dtype),
              extras, [o_spec] * len(extras), deps)
    return res if len(out_dtypes) > 1 else res[0]


def _mm_nt(name, a, b, out_dtype=F32, epilogue=None, extras=(), tm=1024, tn=1024, tk=None, deps=()):
    M, C = a.shape
    N = b.shape[0]
    tm, tn, tk = _tile(M, tm), _tile(N, tn), _tile(C, tk or C)
    b_spec = pl.BlockSpec((tn, tk), lambda i, j, k: (j, k))
    grid = (M // tm, N // tn, C // tk)
    o_spec = pl.BlockSpec((tm, tn), lambda i, j, k: (i, j))
    return _mm(name, a, b, _NT, grid, pl.BlockSpec((tm, tk), lambda i, j, k: (i, k)), b_spec,
               [jax.ShapeDtypeStruct((M, N), out_dtype)], [o_spec], (tm, tn), epilogue or _store(out_dtype),
               extras, [o_spec] * len(extras), deps)[0]


def _mm_tn(name, a, b, out_dtype=BF16, tm=1024, tn=512, tk=None, deps=()):
    T, M = a.shape
    N = b.shape[1]
    tm, tn, tk = _tile(M, tm), _tile(N, tn), _tile(T, tk or T)
    out_shape = jax.ShapeDtypeStruct((M, N), out_dtype)
    o_spec = pl.BlockSpec((tm, tn), lambda i, j, k: (i, j))
    grid = (M // tm, N // tn, T // tk)
    return _mm(name, a, b, _TN, grid, pl.BlockSpec((tk, tm), lambda i, j, k: (k, i)),
               pl.BlockSpec((tk, tn), lambda i, j, k: (k, j)), [out_shape], [o_spec], (tm, tn), _store(out_dtype),
               deps=deps)[0]


def _mean_last(v):
    return jnp.mean(v, axis=-1, keepdims=True)


def _rows_to_sublanes(v):
    r, c = v.shape
    return jnp.sum(v.reshape(r // SUBLANES, SUBLANES, c), axis=0)


def _accumulate(ref, val, first):
    @pl.when(first)
    def _():
        ref[...] = val

    @pl.when(jnp.logical_not(first))
    def _():
        ref[...] += val


def _rms_fwd(name, x, g, tr=512, deps=()):
    T, D = x.shape
    tr = _tile(T, tr)

    def body(x_ref, g_ref, o_ref):
        xv = x_ref[...]
        r = lax.rsqrt(_mean_last(xv * xv) + EPS)
        o_ref[...] = (xv * r * g_ref[...]).astype(BF16)

    row = pl.BlockSpec((tr, D), lambda i: (i, 0))
    return _pcall(
        body, deps, name=name, grid=(T // tr,),
        in_specs=[row, pl.BlockSpec((1, D), lambda i: (0, 0))],
        out_specs=row, out_shape=jax.ShapeDtypeStruct((T, D), BF16),
        compiler_params=_params(("parallel",)),
    )(x, g)


def _rms_bwd(name, dyn, x, g, dres, tr=512, deps=()):
    T, D = x.shape
    tr = _tile(T, tr)

    def body(dy_ref, x_ref, g_ref, dr_ref, dx_ref, dxb_ref, dg_ref):
        xv = x_ref[...]
        r = lax.rsqrt(_mean_last(xv * xv) + EPS)
        xn = xv * r
        dy = dy_ref[...].astype(F32)
        dxn = dy * g_ref[...]
        dx = dr_ref[...] + r * (dxn - xn * _mean_last(dxn * xn))
        dx_ref[...] = dx
        dxb_ref[...] = dx.astype(BF16)
        _accumulate(dg_ref, _rows_to_sublanes(dy * xn), pl.program_id(0) == 0)

    row = pl.BlockSpec((tr, D), lambda i: (i, 0))
    return _pcall(
        body, deps, name=name, grid=(T // tr,),
        in_specs=[row, row, pl.BlockSpec((1, D), lambda i: (0, 0)), row],
        out_specs=[row, row, pl.BlockSpec((SUBLANES, D), lambda i: (0, 0))],
        out_shape=[jax.ShapeDtypeStruct((T, D), F32), jax.ShapeDtypeStruct((T, D), BF16),
                   jax.ShapeDtypeStruct((SUBLANES, D), F32)],
        compiler_params=_params(("arbitrary",)),
    )(dyn, x, g, dres)


def _mm_nn_rms(name, a, b, res, g, tm=512, deps=()):
    M, K = a.shape
    N = b.shape[1]
    tm = _tile(M, tm)

    def epilogue(acc, extra, outs):
        h = acc + extra[0][...]
        outs[0][...] = h
        outs[1][...] = (h * lax.rsqrt(_mean_last(h * h) + EPS) * extra[1][...]).astype(BF16)

    row = pl.BlockSpec((tm, N), lambda i, j, k: (i, 0))
    return _mm(name, a, b, _NN, (M // tm, 1, 1), pl.BlockSpec((tm, K), lambda i, j, k: (i, 0)),
               pl.BlockSpec((K, N), lambda i, j, k: (0, 0)),
               [jax.ShapeDtypeStruct((M, N), F32), jax.ShapeDtypeStruct((M, N), BF16)], [row, row], (tm, N), epilogue,
               (res, g), [row, pl.BlockSpec((1, N), lambda i, j, k: (0, 0))], deps)


def _mm_nt_rms_bwd(name, a, b, x, g, dres, with_bf16=True, tm=256, deps=()):
    M, C = a.shape
    N = b.shape[0]
    tm = _tile(M, tm)

    def epilogue(dy, extra, outs):
        x_ref, dr_ref, g_ref = extra
        xv = x_ref[...]
        r = lax.rsqrt(_mean_last(xv * xv) + EPS)
        xn = xv * r
        dxn = dy * g_ref[...]
        dx = dr_ref[...] + r * (dxn - xn * _mean_last(dxn * xn))
        outs[0][...] = dx
        if with_bf16:
            outs[1][...] = dx.astype(BF16)
        _accumulate(outs[-1], _rows_to_sublanes(dy * xn), pl.program_id(0) == 0)

    row = pl.BlockSpec((tm, N), lambda i, j, k: (i, 0))
    copies = [jax.ShapeDtypeStruct((M, N), F32)] + ([jax.ShapeDtypeStruct((M, N), BF16)] if with_bf16 else [])
    return _mm(name, a, b, _NT, (M // tm, 1, 1), pl.BlockSpec((tm, C), lambda i, j, k: (i, 0)),
               pl.BlockSpec((N, C), lambda i, j, k: (0, 0)),
               copies + [jax.ShapeDtypeStruct((SUBLANES, N), F32)],
               [row] * len(copies) + [pl.BlockSpec((SUBLANES, N), lambda i, j, k: (0, 0))], (tm, N), epilogue,
               (x, dres, g), [row, row, pl.BlockSpec((1, N), lambda i, j, k: (0, 0))], deps,
               semantics=("arbitrary", "arbitrary", "arbitrary"))


def _gate_tail(gn, w_gate, h2, pe, target, g_ple, g_final, tm=256):
    T, D = h2.shape
    tm = _tile(T, tm)

    def epilogue(z, extra, outs):
        h2_ref, pe_ref, t_ref, gp_ref, gf_ref = extra
        dh3_ref, dz_ref, dpe_ref, dgf_ref, dgp_ref, loss_ref = outs
        first = pl.program_id(0) == 0
        pev = pe_ref[...]
        r3 = lax.rsqrt(_mean_last(pev * pev) + EPS)
        en = pev * r3
        e = en * gp_ref[...]
        gate = 1.0 / (1.0 + jnp.exp(-z))
        h3 = h2_ref[...] + gate * e
        r5 = lax.rsqrt(_mean_last(h3 * h3) + EPS)
        hn = h3 * r5
        diff = hn * gf_ref[...] - t_ref[...]
        loss_rows = 0.5 * _mean_last(diff * diff)
        row0 = lax.broadcasted_iota(jnp.int32, (SUBLANES, LANES), 0) == 0
        _accumulate(loss_ref, jnp.where(row0, jnp.sum(loss_rows), 0.0), first)
        dy = diff * (1.0 / D)
        _accumulate(dgf_ref, _rows_to_sublanes(dy * hn), first)
        dhn = dy * gf_ref[...]
        dh3 = r5 * (dhn - hn * _mean_last(dhn * hn))
        dh3_ref[...] = dh3
        dgate = dh3 * e
        de = dh3 * gate
        dz_ref[...] = (dgate * gate * (1.0 - gate)).astype(BF16)
        _accumulate(dgp_ref, _rows_to_sublanes(de * en), first)
        den = de * gp_ref[...]
        dpe_ref[...] = (r3 * (den - en * _mean_last(den * en))).astype(BF16)

    row = pl.BlockSpec((tm, D), lambda i, j, k: (i, 0))
    vec = pl.BlockSpec((1, D), lambda i, j, k: (0, 0))
    part = pl.BlockSpec((SUBLANES, D), lambda i, j, k: (0, 0))
    return _mm("gate_tail", gn, w_gate, _NN, (T // tm, 1, 1), row, pl.BlockSpec(w_gate.shape, lambda i, j, k: (0, 0)),
               [jax.ShapeDtypeStruct((T, D), F32), jax.ShapeDtypeStruct((T, D), BF16),
                jax.ShapeDtypeStruct((T, D), BF16), jax.ShapeDtypeStruct((SUBLANES, D), F32),
                jax.ShapeDtypeStruct((SUBLANES, D), F32), jax.ShapeDtypeStruct((SUBLANES, LANES), F32)],
               [row, row, row, part, part, pl.BlockSpec((SUBLANES, LANES), lambda i, j, k: (0, 0))], (tm, D), epilogue,
               (h2, pe, target, g_ple, g_final), [row, row, row, vec, vec],
               semantics=("arbitrary", "arbitrary", "arbitrary"))


def _rope_tables(T):
    pos = np.arange(T)
    half = HEAD_DIM // 2
    inv = (ROPE_THETA ** (-np.arange(0, half, 2, dtype=np.float32) / half)).astype(np.float32)
    ang_r = (pos // GRID_W).astype(np.float32)[:, None] * inv
    ang_c = (pos % GRID_W).astype(np.float32)[:, None] * inv
    cos = np.concatenate([np.cos(ang_r), np.cos(ang_r), np.cos(ang_c), np.cos(ang_c)], axis=-1)
    sin = np.concatenate([-np.sin(ang_r), np.sin(ang_r), -np.sin(ang_c), np.sin(ang_c)], axis=-1)
    return jnp.asarray(cos, F32), jnp.asarray(sin, F32)


def _swap32(x):
    lane = lax.broadcasted_iota(jnp.int32, x.shape, 1)
    return jnp.where((lane % 64) < 32, pltpu.roll(x, 96, 1), pltpu.roll(x, 32, 1))


def _in_proj(u, w_in, cos, sin, g_q, g_k, n_norm, tm=512):
    T, K = u.shape
    W = w_in.shape[1]
    tm = _tile(T, tm)
    n_q = n_norm * GROUP // (GROUP + 1)
    wa = n_norm * HEAD_DIM

    def epilogue(acc, extra, outs):
        c_ref, s_ref, gq_ref, gk_ref = extra
        raw_ref, o_ref = outs
        c, s = c_ref[...], s_ref[...]
        raw_ref[...] = acc[:, :wa]
        for h in range(n_norm):
            cols = slice(h * HEAD_DIM, (h + 1) * HEAD_DIM)
            xv = acc[:, cols]
            g = gq_ref[...] if h < n_q else gk_ref[...]
            xn = xv * lax.rsqrt(_mean_last(xv * xv) + EPS) * g
            o_ref[:, cols] = (xn * c + _swap32(xn) * s).astype(BF16)
        o_ref[:, wa:] = acc[:, wa:].astype(BF16)

    tab = pl.BlockSpec((tm, HEAD_DIM), lambda i, j, k: (i, 0))
    vec = pl.BlockSpec((1, HEAD_DIM), lambda i, j, k: (0, 0))
    return _mm("in_proj", u, w_in, _NN, (T // tm, 1, 1), pl.BlockSpec((tm, K), lambda i, j, k: (i, 0)),
               pl.BlockSpec((K, W), lambda i, j, k: (0, 0)),
               [jax.ShapeDtypeStruct((T, wa), F32), jax.ShapeDtypeStruct((T, W), BF16)],
               [pl.BlockSpec((tm, wa), lambda i, j, k: (i, 0)), pl.BlockSpec((tm, W), lambda i, j, k: (i, 0))],
               (tm, W), epilogue, (cos, sin, g_q, g_k), [tab, tab, vec, vec])


def _dproj(proj_a, dqa, dka_t, dva_t, dqb, dkb, dvb, cos, sin, g_q, g_k, tr=512):
    T, wa = proj_a.shape
    tr = _tile(T, tr)
    n_q = dqa.shape[1] // HEAD_DIM
    wkv = dka_t.shape[0]
    W = wa + wkv + dqb.shape[1] + dkb.shape[1] + dvb.shape[1]

    def body(p_ref, dqa_ref, dkat_ref, dvat_ref, dqb_ref, dkb_ref, dvb_ref, c_ref, s_ref, gq_ref, gk_ref,
             o_ref, dgq_ref, dgk_ref):
        c, s = c_ref[...], s_ref[...]
        dka = dkat_ref[...].T
        dgq = jnp.zeros((SUBLANES, HEAD_DIM), F32)
        dgk = jnp.zeros((SUBLANES, HEAD_DIM), F32)
        for h in range(wa // HEAD_DIM):
            cols = slice(h * HEAD_DIM, (h + 1) * HEAD_DIM)
            xv = p_ref[:, cols]
            r = lax.rsqrt(_mean_last(xv * xv) + EPS)
            xn = xv * r
            if h < n_q:
                d = dqa_ref[:, cols]
                g = gq_ref[...]
            else:
                d = dka[:, (h - n_q) * HEAD_DIM:(h - n_q + 1) * HEAD_DIM]
                g = gk_ref[...]
            dqn = d * c + _swap32(d * s)
            part = _rows_to_sublanes(dqn * xn)
            if h < n_q:
                dgq = dgq + part
            else:
                dgk = dgk + part
            dxn = dqn * g
            o_ref[:, cols] = (r * (dxn - xn * _mean_last(dxn * xn))).astype(BF16)
        o_ref[:, wa:wa + wkv] = dvat_ref[...].T.astype(BF16)
        off = wa + wkv
        for ref in (dqb_ref, dkb_ref, dvb_ref):
            w = ref.shape[1]
            o_ref[:, off:off + w] = ref[...].astype(BF16)
            off += w
        first = pl.program_id(0) == 0
        _accumulate(dgq_ref, dgq, first)
        _accumulate(dgk_ref, dgk, first)

    def row(w):
        return pl.BlockSpec((tr, w), lambda i: (i, 0))

    col = pl.BlockSpec((wkv, tr), lambda i: (0, i))
    vec = pl.BlockSpec((1, HEAD_DIM), lambda i: (0, 0))
    part = pl.BlockSpec((SUBLANES, HEAD_DIM), lambda i: (0, 0))
    return pl.pallas_call(
        body, name="dproj", grid=(T // tr,),
        in_specs=[row(wa), row(dqa.shape[1]), col, col, row(dqb.shape[1]),
                  row(dkb.shape[1]), row(dvb.shape[1]), row(HEAD_DIM), row(HEAD_DIM), vec, vec],
        out_specs=[row(W), part, part],
        out_shape=[jax.ShapeDtypeStruct((T, W), BF16), jax.ShapeDtypeStruct((SUBLANES, HEAD_DIM), F32),
                   jax.ShapeDtypeStruct((SUBLANES, HEAD_DIM), F32)],
        compiler_params=_params(("arbitrary",)),
    )(proj_a, dqa, dka_t, dva_t, dqb, dkb, dvb, cos, sin, g_q, g_k)


def _attn_a_fwd(pb, n_q, n_kv, out_heads, tq=1024, tc=1024):
    T = pb.shape[0]
    tq, tc = _tile(T, tq), _tile(T, tc)
    scale = HEAD_DIM ** -0.5
    c = scale * LOG2E

    def body(q_ref, k_ref, v_ref, o_ref, lse_ref):
        q = q_ref[...]
        m = l = acc = None
        for j in range(T // tc):
            keys = slice(j * tc, (j + 1) * tc)
            s = lax.dot_general(q, k_ref[keys, :], _NT, preferred_element_type=F32)
            mj = jnp.max(s, axis=-1, keepdims=True)
            m_new = mj if j == 0 else jnp.maximum(m, mj)
            p = jnp.exp2((s - m_new) * c)
            pv = lax.dot_general(p.astype(BF16), v_ref[keys, :], _NN, preferred_element_type=F32)
            if j == 0:
                l, acc = jnp.sum(p, axis=-1, keepdims=True), pv
            else:
                alpha = jnp.exp2((m - m_new) * c)
                l = alpha * l + jnp.sum(p, axis=-1, keepdims=True)
                acc = alpha * acc + pv
            m = m_new
        o_ref[...] = (acc / l).astype(BF16)
        lse_ref[...] = m * scale + jnp.log(l)

    return pl.pallas_call(
        body, name="attn_a_fwd", grid=(n_kv, GROUP, T // tq),
        in_specs=[pl.BlockSpec((tq, HEAD_DIM), lambda kv, g, i: (i, kv * GROUP + g)),
                  pl.BlockSpec((T, HEAD_DIM), lambda kv, g, i: (0, n_q + kv)),
                  pl.BlockSpec((T, HEAD_DIM), lambda kv, g, i: (0, n_q + n_kv + kv))],
        out_specs=[pl.BlockSpec((tq, HEAD_DIM), lambda kv, g, i: (i, kv * GROUP + g)),
                   pl.BlockSpec((None, tq, 1), lambda kv, g, i: (kv * GROUP + g, i, 0))],
        out_shape=[jax.ShapeDtypeStruct((T, out_heads * HEAD_DIM), BF16), jax.ShapeDtypeStruct((n_q, T, 1), F32)],
        compiler_params=_params(("parallel", "parallel", "parallel")),
    )(pb, pb, pb)


def _attn_a_bwd(pb, o_cat, d_o, lse, n_q, n_kv, tq=1024, tc=256):
    T = pb.shape[0]
    tq, tc = _tile(T, tq), _tile(T, tc)
    scale = HEAD_DIM ** -0.5
    c = scale * LOG2E

    def body(q_ref, k_ref, v_ref, o_ref, do_ref, lse_ref, dq_ref, dkt_ref, dvt_ref):
        q, do = q_ref[...], do_ref[...]
        qt, dot = q.T, do.T
        delta = jnp.sum(do.astype(F32) * o_ref[...].astype(F32), axis=-1, keepdims=True)
        lse2 = lse_ref[...] * LOG2E

        @pl.when(jnp.logical_and(pl.program_id(1) == 0, pl.program_id(2) == 0))
        def _():
            dkt_ref[...] = jnp.zeros(dkt_ref.shape, F32)
            dvt_ref[...] = jnp.zeros(dvt_ref.shape, F32)

        dq = None
        for j in range(T // tc):
            keys = slice(j * tc, (j + 1) * tc)
            kc, vc = k_ref[keys, :], v_ref[keys, :]
            s = lax.dot_general(q, kc, _NT, preferred_element_type=F32)
            p = jnp.exp2(s * c - lse2)
            dp = lax.dot_general(do, vc, _NT, preferred_element_type=F32)
            ds = (p * (dp - delta) * scale).astype(BF16)
            dqj = lax.dot_general(ds, kc, _NN, preferred_element_type=F32)
            dq = dqj if dq is None else dq + dqj
            dvt_ref[:, keys] += lax.dot_general(dot, p.astype(BF16), _NN, preferred_element_type=F32)
            dkt_ref[:, keys] += lax.dot_general(qt, ds, _NN, preferred_element_type=F32)
        dq_ref[...] = dq

    qmap = lambda kv, g, i: (i, kv * GROUP + g)
    return pl.pallas_call(
        body, name="attn_a_bwd", grid=(n_kv, GROUP, T // tq),
        in_specs=[pl.BlockSpec((tq, HEAD_DIM), qmap),
                  pl.BlockSpec((T, HEAD_DIM), lambda kv, g, i: (0, n_q + kv)),
                  pl.BlockSpec((T, HEAD_DIM), lambda kv, g, i: (0, n_q + n_kv + kv)),
                  pl.BlockSpec((tq, HEAD_DIM), qmap),
                  pl.BlockSpec((tq, HEAD_DIM), qmap),
                  pl.BlockSpec((None, tq, 1), lambda kv, g, i: (kv * GROUP + g, i, 0))],
        out_specs=[pl.BlockSpec((tq, HEAD_DIM), qmap),
                   pl.BlockSpec((HEAD_DIM, T), lambda kv, g, i: (kv, 0)),
                   pl.BlockSpec((HEAD_DIM, T), lambda kv, g, i: (kv, 0))],
        out_shape=[jax.ShapeDtypeStruct((T, n_q * HEAD_DIM), F32),
                   jax.ShapeDtypeStruct((n_kv * HEAD_DIM, T), F32),
                   jax.ShapeDtypeStruct((n_kv * HEAD_DIM, T), F32)],
        compiler_params=_params(("parallel", "arbitrary", "arbitrary")),
    )(pb, pb, pb, o_cat, d_o, lse)


def _bucket_index():
    r = np.arange(BLOCK_Q)[:, None]
    j = np.arange(3 * BLOCK_Q)[None, :]
    rel = (j - BLOCK_Q) - r
    nb = N_BUCKETS // 2
    ret = np.where(rel > 0, nb, 0)
    n = np.abs(rel)
    max_exact = nb // 2
    nf = np.maximum(n, 1).astype(np.float32)
    large = max_exact + (np.log(nf / max_exact) / math.log(MAX_DISTANCE / max_exact) * (nb - max_exact)).astype(np.int32)
    large = np.minimum(large, nb - 1)
    return jnp.asarray(ret + np.where(n < max_exact, n, large), jnp.int32)


def _bias_build(idx, table_flat, n_heads, deps=()):
    def body(idx_ref, tab_ref, o_ref):
        h = pl.program_id(0)
        iv = idx_ref[...]
        acc = jnp.zeros(iv.shape, F32)
        for b in range(N_BUCKETS):
            acc = jnp.where(iv == b, tab_ref[b * n_heads + h], acc)
        r = lax.broadcasted_iota(jnp.int32, iv.shape, 0)
        j = lax.broadcasted_iota(jnp.int32, iv.shape, 1)
        o_ref[...] = jnp.where(jnp.abs(j - BLOCK_Q - r) <= WINDOW, acc, NEG_INF)

    return _pcall(
        body, deps, name="bias_build", grid=(n_heads,),
        in_specs=[pl.BlockSpec(idx.shape, lambda h: (0, 0)), pl.BlockSpec(memory_space=pltpu.SMEM)],
        out_specs=pl.BlockSpec((None,) + idx.shape, lambda h: (h, 0, 0)),
        out_shape=jax.ShapeDtypeStruct((n_heads,) + idx.shape, F32),
        compiler_params=_params(("parallel",)),
    )(idx, table_flat)


def _in_sequence(n, T):
    j = lax.broadcasted_iota(jnp.int32, (GROUP * BLOCK_Q, 3 * BLOCK_Q), 1)
    kabs = n * BLOCK_Q + j - BLOCK_Q
    return (kabs >= 0) & (kabs < T)


def _per_head_rows(values):
    head = lax.broadcasted_iota(jnp.int32, (GROUP * BLOCK_Q, 1), 0) // BLOCK_Q
    col = jnp.zeros((GROUP * BLOCK_Q, 1), F32)
    for g, v in enumerate(values):
        col = jnp.where(head == g, v, col)
    return col


def _band_specs(col, nblk, sb):
    return [pl.BlockSpec((BLOCK_Q, HEAD_DIM), lambda kv, i: (jnp.maximum(sb * i - 1, 0), col(kv))),
            pl.BlockSpec((sb * BLOCK_Q, HEAD_DIM), lambda kv, i: (i, col(kv))),
            pl.BlockSpec((BLOCK_Q, HEAD_DIM), lambda kv, i: (jnp.minimum(sb * i + sb, nblk - 1), col(kv)))]


def _head_specs(base, rows):
    return [pl.BlockSpec((rows, HEAD_DIM), functools.partial(lambda kv, i, g: (i, base + kv * GROUP + g), g=g))
            for g in range(GROUP)]


def _attn_b_fwd(pb, bias, sink, o_all, q_off, n_q, n_kv, deps=(), sb=16):
    T = pb.shape[0]
    nblk = T // BLOCK_Q
    sb = min(sb, nblk)
    tq = sb * BLOCK_Q
    scale = HEAD_DIM ** -0.5

    def body(*refs):
        q_refs = refs[0:GROUP]
        k_refs, v_refs = refs[GROUP:GROUP + 3], refs[GROUP + 3:GROUP + 6]
        bias_ref, sink_ref, o_ref, lse_ref = refs[GROUP + 6:]
        kv, i = pl.program_id(0), pl.program_id(1)
        kb = jnp.concatenate([r[...] for r in k_refs], axis=0)
        vb = jnp.concatenate([r[...] for r in v_refs], axis=0)
        bias_all = bias_ref[...].reshape(GROUP * BLOCK_Q, 3 * BLOCK_Q)
        sk = _per_head_rows([sink_ref[kv * GROUP + g] for g in range(GROUP)])
        for b in range(sb):
            rows = slice(b * BLOCK_Q, (b + 1) * BLOCK_Q)
            kw, vw = kb[b * BLOCK_Q:(b + 3) * BLOCK_Q], vb[b * BLOCK_Q:(b + 3) * BLOCK_Q]
            q = jnp.concatenate([r[rows, :] for r in q_refs], axis=0)
            s = lax.dot_general(q, kw, _NT, preferred_element_type=F32) * scale + bias_all
            if b == 0 or b == sb - 1:
                s = jnp.where(_in_sequence(i * sb + b, T), s, NEG_INF)
            m = jnp.maximum(jnp.max(s, axis=-1, keepdims=True), sk)
            p = jnp.exp(s - m)
            l = jnp.sum(p, axis=-1, keepdims=True) + jnp.exp(sk - m)
            o = (lax.dot_general(p.astype(BF16), vw, _NN, preferred_element_type=F32) / l).astype(BF16)
            lse = m + jnp.log(l)
            for g in range(GROUP):
                head = slice(g * BLOCK_Q, (g + 1) * BLOCK_Q)
                o_ref[rows, g * HEAD_DIM:(g + 1) * HEAD_DIM] = o[head]
                lse_ref[g, rows, :] = lse[head]

    first_group = o_all.shape[1] // (GROUP * HEAD_DIM) - n_kv
    return _pcall(
        body, deps, into=(o_all, 0), name="attn_b_fwd", grid=(n_kv, nblk // sb),
        in_specs=[*_head_specs(q_off, tq),
                  *_band_specs(lambda kv: q_off + n_q + kv, nblk, sb),
                  *_band_specs(lambda kv: q_off + n_q + n_kv + kv, nblk, sb),
                  pl.BlockSpec((GROUP, BLOCK_Q, 3 * BLOCK_Q), lambda kv, i: (kv, 0, 0)),
                  pl.BlockSpec(memory_space=pltpu.SMEM)],
        out_specs=[pl.BlockSpec((tq, GROUP * HEAD_DIM), lambda kv, i: (i, first_group + kv)),
                   pl.BlockSpec((GROUP, tq, 1), lambda kv, i: (kv, i, 0))],
        out_shape=[jax.ShapeDtypeStruct(o_all.shape, BF16), jax.ShapeDtypeStruct((n_q, T, 1), F32)],
        compiler_params=_params(("parallel", "parallel")),
    )(*([pb] * (GROUP + 6)), bias, sink)


def _attn_b_bwd(pb, o_cat, d_o, lse, bias, sink, q_off, n_q, n_kv, o_off, deps=(), sb=16):
    T = pb.shape[0]
    nblk = T // BLOCK_Q
    sb = min(sb, nblk)
    tq = sb * BLOCK_Q
    scale = HEAD_DIM ** -0.5

    def body(*refs):
        q_refs = refs[0:GROUP]
        k_refs, v_refs = refs[GROUP:GROUP + 3], refs[GROUP + 3:GROUP + 6]
        o_refs, do_refs = refs[GROUP + 6:2 * GROUP + 6], refs[2 * GROUP + 6:3 * GROUP + 6]
        lse_ref, bias_ref, sink_ref, dq_ref, dk_ref, dv_ref, dbias_ref, dsink_ref, dkb_ref, dvb_ref = refs[3 * GROUP + 6:]
        kv, i = pl.program_id(0), pl.program_id(1)
        first = i == 0

        @pl.when(first)
        def _():
            dk_ref[...] = jnp.zeros(dk_ref.shape, F32)
            dv_ref[...] = jnp.zeros(dv_ref.shape, F32)
            dbias_ref[...] = jnp.zeros(dbias_ref.shape, F32)

        kb = jnp.concatenate([r[...] for r in k_refs], axis=0)
        vb = jnp.concatenate([r[...] for r in v_refs], axis=0)
        dkb_ref[...] = jnp.zeros(dkb_ref.shape, F32)
        dvb_ref[...] = jnp.zeros(dvb_ref.shape, F32)
        row = lax.broadcasted_iota(jnp.int32, (SUBLANES, LANES), 0)
        dsink = jnp.zeros((SUBLANES, LANES), F32)
        bias_all = bias_ref[...].reshape(GROUP * BLOCK_Q, 3 * BLOCK_Q)
        sk = _per_head_rows([sink_ref[kv * GROUP + g] for g in range(GROUP)])
        for b in range(sb):
            rows = slice(b * BLOCK_Q, (b + 1) * BLOCK_Q)
            win = slice(b * BLOCK_Q, (b + 3) * BLOCK_Q)
            kw, vw = kb[win], vb[win]
            q = jnp.concatenate([r[rows, :] for r in q_refs], axis=0)
            do = jnp.concatenate([r[rows, :] for r in do_refs], axis=0)
            o = jnp.concatenate([r[rows, :] for r in o_refs], axis=0)
            lse = jnp.concatenate([lse_ref[g, rows, :] for g in range(GROUP)], axis=0)
            delta = jnp.sum(do.astype(F32) * o.astype(F32), axis=-1, keepdims=True)
            s = lax.dot_general(q, kw, _NT, preferred_element_type=F32) * scale + bias_all
            if b == 0 or b == sb - 1:
                s = jnp.where(_in_sequence(i * sb + b, T), s, NEG_INF)
            p = jnp.exp(s - lse)
            dp = lax.dot_general(do, vw, _NT, preferred_element_type=F32)
            ds = p * (dp - delta)
            dbias_ref[...] += ds.reshape(GROUP, BLOCK_Q, 3 * BLOCK_Q)
            sunk = jnp.exp(sk - lse) * delta
            for g in range(GROUP):
                dsink = dsink + jnp.where(row == g, -jnp.sum(sunk[g * BLOCK_Q:(g + 1) * BLOCK_Q]), 0.0)
            dsb = (ds * scale).astype(BF16)
            dq = lax.dot_general(dsb, kw, _NN, preferred_element_type=F32).astype(BF16)
            for g in range(GROUP):
                dq_ref[rows, g * HEAD_DIM:(g + 1) * HEAD_DIM] = dq[g * BLOCK_Q:(g + 1) * BLOCK_Q]
            dkb_ref[win, :] += lax.dot_general(dsb, q, _TN, preferred_element_type=F32)
            dvb_ref[win, :] += lax.dot_general(p.astype(BF16), do, _TN, preferred_element_type=F32)
        _accumulate(dsink_ref, dsink, first)

        before = pl.ds(pl.multiple_of(jnp.maximum(sb * i - 1, 0) * BLOCK_Q, BLOCK_Q), BLOCK_Q)
        own = pl.ds(pl.multiple_of(i * tq, BLOCK_Q), tq)
        after = pl.ds(pl.multiple_of(jnp.minimum(sb * i + sb, nblk - 1) * BLOCK_Q, BLOCK_Q), BLOCK_Q)
        for acc_ref, band_ref in ((dk_ref, dkb_ref), (dv_ref, dvb_ref)):
            acc_ref[before, :] += band_ref[0:BLOCK_Q, :]
            acc_ref[own, :] += band_ref[BLOCK_Q:BLOCK_Q + tq, :]
            acc_ref[after, :] += band_ref[BLOCK_Q + tq:, :]

    return _pcall(
        body, deps, name="attn_b_bwd", grid=(n_kv, nblk // sb),
        in_specs=[*_head_specs(q_off, tq),
                  *_band_specs(lambda kv: q_off + n_q + kv, nblk, sb),
                  *_band_specs(lambda kv: q_off + n_q + n_kv + kv, nblk, sb),
                  *_head_specs(o_off, tq), *_head_specs(o_off, tq),
                  pl.BlockSpec((GROUP, tq, 1), lambda kv, i: (kv, i, 0)),
                  pl.BlockSpec((GROUP, BLOCK_Q, 3 * BLOCK_Q), lambda kv, i: (kv, 0, 0)),
                  pl.BlockSpec(memory_space=pltpu.SMEM)],
        out_specs=[pl.BlockSpec((tq, GROUP * HEAD_DIM), lambda kv, i: (i, kv)),
                   pl.BlockSpec((T, HEAD_DIM), lambda kv, i: (0, kv)),
                   pl.BlockSpec((T, HEAD_DIM), lambda kv, i: (0, kv)),
                   pl.BlockSpec((GROUP, BLOCK_Q, 3 * BLOCK_Q), lambda kv, i: (kv, 0, 0)),
                   pl.BlockSpec((None, SUBLANES, LANES), lambda kv, i: (kv, 0, 0))],
        out_shape=[jax.ShapeDtypeStruct((T, n_q * HEAD_DIM), BF16),
                   jax.ShapeDtypeStruct((T, n_kv * HEAD_DIM), F32),
                   jax.ShapeDtypeStruct((T, n_kv * HEAD_DIM), F32),
                   jax.ShapeDtypeStruct((n_q, BLOCK_Q, 3 * BLOCK_Q), F32),
                   jax.ShapeDtypeStruct((n_kv, SUBLANES, LANES), F32)],
        scratch_shapes=[pltpu.VMEM((tq + 2 * BLOCK_Q, HEAD_DIM), F32), pltpu.VMEM((tq + 2 * BLOCK_Q, HEAD_DIM), F32)],
        compiler_params=_params(("parallel", "arbitrary")),
    )(*([pb] * (GROUP + 6)), *([o_cat] * GROUP), *([d_o] * GROUP), lse, bias, sink)


def _table_grads(dbias, dsink_raw, idx):
    n_heads = dbias.shape[0]
    n_kv = dsink_raw.shape[0]

    def body(db_ref, ds_ref, idx_ref, dt_ref, dsk_ref):
        iv = idx_ref[...]
        row = lax.broadcasted_iota(jnp.int32, (SUBLANES, LANES), 0)
        lane = lax.broadcasted_iota(jnp.int32, (SUBLANES, LANES), 1)
        dsk = jnp.zeros((SUBLANES, LANES), F32)
        for h in range(n_heads):
            d = db_ref[h]
            acc = jnp.zeros((SUBLANES, LANES), F32)
            for b in range(N_BUCKETS):
                acc = jnp.where((row == 0) & (lane == b), jnp.sum(jnp.where(iv == b, d, 0.0)), acc)
            dt_ref[:, h * LANES:(h + 1) * LANES] = acc
            raw = ds_ref[h // GROUP]
            val = jnp.sum(jnp.where((row == h % GROUP) & (lane == 0), raw, 0.0))
            dsk = jnp.where((row == 0) & (lane == h), val, dsk)
        dsk_ref[...] = dsk

    return pl.pallas_call(
        body, name="table_grads",
        in_specs=[pl.BlockSpec(memory_space=pltpu.VMEM)] * 3,
        out_specs=[pl.BlockSpec(memory_space=pltpu.VMEM)] * 2,
        out_shape=[jax.ShapeDtypeStruct((SUBLANES, n_heads * LANES), F32),
                   jax.ShapeDtypeStruct((SUBLANES, LANES), F32)],
        compiler_params=_params(),
    )(dbias, dsink_raw, idx)


def _position():
    x, y, c = lax.axis_index("x"), lax.axis_index("y"), lax.axis_index("c")
    return x, y, c


def _hbm(a):
    return pltpu.with_memory_space_constraint(a, pltpu.HBM)


def _split_start(name, bufs, sem_shapes, issue):
    nb, ns = len(bufs), len(sem_shapes)

    def body(*refs):
        buf_refs = refs[:nb]
        sems = refs[nb:nb + ns]
        token = refs[nb + ns + nb]
        issue(buf_refs, sems)
        token[...] = jnp.zeros(token.shape, F32)

    outs = pl.pallas_call(
        body, name=name,
        in_specs=[_HBM] * nb,
        out_specs=[_SEM] * ns + [_HBM] * nb + [_VMEM],
        out_shape=[pltpu.SemaphoreType.DMA(s) for s in sem_shapes] + [pltpu.HBM(b.shape, b.dtype) for b in bufs]
        + [jax.ShapeDtypeStruct((SUBLANES, LANES), F32)],
        input_output_aliases={i: ns + i for i in range(nb)},
        compiler_params=pltpu.CompilerParams(has_side_effects=_EFFECT),
    )(*[_hbm(b) for b in bufs])
    return outs[:ns], outs[ns:ns + nb], outs[-1]


def _split_wait(name, bufs, send, recv, counts, size_of, after):
    nb = len(bufs)

    def body(*refs):
        buf_refs = refs[:nb]
        send_ref, recv_ref = refs[nb], refs[nb + 1]
        x, y, c = _position()
        for w, n in enumerate(counts):
            ref = size_of(buf_refs, w)
            for k in range(n):
                s = sum(counts[:w]) + k
                cp = pltpu.make_async_remote_copy(
                    src_ref=ref, dst_ref=ref, send_sem=send_ref.at[s], recv_sem=recv_ref.at[s],
                    device_id=(x, y, c), device_id_type=MESH)
                cp.wait_send()
                cp.wait_recv()

    return pl.pallas_call(
        body, name=name,
        in_specs=[_HBM] * nb + [_SEM, _SEM, _ANY],
        out_specs=[_HBM] * nb,
        out_shape=[pltpu.HBM(b.shape, b.dtype) for b in bufs],
        input_output_aliases={i: i for i in range(nb)},
        compiler_params=pltpu.CompilerParams(has_side_effects=_EFFECT),
    )(*bufs, send, recv, after)


def _block_of(pos):
    return 4 * pos[0] + 2 * pos[1] + pos[2]


def _shard_of(ref, blk, by_cols):
    aligned = (lambda v, a: v) if isinstance(blk, int) else pl.multiple_of
    if by_cols:
        n = ref.shape[1] // N_DEV
        return ref.at[:, pl.ds(aligned(blk * n, LANES), n)]
    r = ref.shape[0] // N_DEV
    return ref.at[pl.ds(aligned(blk * r, SUBLANES), r), :]


def _place_own(name, land, shard, by_cols, tr=256):
    r, n = shard.shape
    tr = _tile(r, tr)
    mine = _block_of(_position()).astype(jnp.int32).reshape(1)

    def body(m_ref, land_ref, s_ref, o_ref):
        o_ref[...] = s_ref[...]

    if by_cols:
        out = pl.BlockSpec((tr, n), lambda i, m_ref: (i, m_ref[0]))
    else:
        out = pl.BlockSpec((tr, n), lambda i, m_ref: (m_ref[0] * (r // tr) + i, 0))
    return pl.pallas_call(
        body, name=name,
        grid_spec=pltpu.PrefetchScalarGridSpec(
            num_scalar_prefetch=1, grid=(r // tr,),
            in_specs=[_ANY, pl.BlockSpec((tr, n), lambda i, m_ref: (i, 0))], out_specs=out),
        out_shape=jax.ShapeDtypeStruct(land.shape, land.dtype),
        input_output_aliases={1: 0},
        compiler_params=_params(("parallel",)),
    )(mine, land, shard)


def _gather_start(name, shards, by_cols, groups, after=None):
    nw = len(shards)
    lands = [lax.empty((s.shape[0], s.shape[1] * N_DEV) if cols else (s.shape[0] * N_DEV, s.shape[1]), s.dtype)
             for s, cols in zip(shards, by_cols)]
    order = [] if after is None else [after]

    def issue(bufs, sems):
        x, y, c = _position()
        peers = [(x, y, 1 - c), (1 - x, y, c), (x, 1 - y, c), (1 - x, 1 - y, c)]
        for gi, grp in enumerate(groups):
            for wi, w in enumerate(grp):
                for k, peer in enumerate(peers):
                    pltpu.make_async_remote_copy(
                        src_ref=bufs[w], dst_ref=_shard_of(bufs[nw + w], _block_of((x, y, c)), by_cols[w]),
                        send_sem=sems[2 * gi].at[4 * wi + k], recv_sem=sems[2 * gi + 1].at[4 * wi + k],
                        device_id=peer, device_id_type=MESH).start()

    sem_shapes = [(4 * len(g),) for g in groups for _ in range(2)]
    sems, thru, token = _split_start(name, list(shards) + lands + order, sem_shapes, issue)
    return sems, thru[:nw], thru[nw:2 * nw], token


def _gather_forward(name, lands, by_cols):
    nw = len(lands)

    def issue(land, sems):
        x, y, c = _position()
        for w in range(nw):
            for k, chip in enumerate([(1 - x, y), (x, 1 - y), (1 - x, 1 - y)]):
                blk = _shard_of(land[w], _block_of((*chip, c)), by_cols[w])
                pltpu.make_async_remote_copy(
                    src_ref=blk, dst_ref=blk, send_sem=sems[0].at[3 * w + k], recv_sem=sems[1].at[3 * w + k],
                    device_id=(x, y, 1 - c), device_id_type=MESH).start()

    return _split_start(name, lands, [(3 * nw,), (3 * nw,)], issue)


def _first_block(bufs, w, offset=0):
    return bufs[offset + w].at[0]


_PEER_FLIPS = ((0, 0, 1), (1, 0, 0), (1, 0, 1), (0, 1, 0), (0, 1, 1), (1, 1, 0), (1, 1, 1))


def _scatter_start(name, grads, by_cols):
    nw = len(grads)
    lands = []
    for g, cols in zip(grads, by_cols):
        shard = (g.shape[0], g.shape[1] // N_DEV) if cols else (g.shape[0] // N_DEV, g.shape[1])
        lands.append(lax.empty((N_DEV,) + shard, g.dtype))

    def issue(bufs, sems):
        x, y, c = _position()
        flip = lambda v, f: 1 - v if f else v
        for w in range(nw):
            for k, (fx, fy, fc) in enumerate(_PEER_FLIPS):
                peer = (flip(x, fx), flip(y, fy), flip(c, fc))
                pltpu.make_async_remote_copy(
                    src_ref=_shard_of(bufs[w], _block_of(peer), by_cols[w]), dst_ref=bufs[nw + w].at[_block_of((x, y, c))],
                    send_sem=sems[0].at[7 * w + k], recv_sem=sems[1].at[7 * w + k],
                    device_id=peer, device_id_type=MESH).start()

    return _split_start(name, list(grads) + lands, [(7 * nw,), (7 * nw,)], issue)


def _adam(w, g, m, v):
    m = ADAM_B1 * m + (1.0 - ADAM_B1) * g
    v = ADAM_B2 * v + (1.0 - ADAM_B2) * (g * g)
    m_hat = m / (1.0 - ADAM_B1 ** ADAM_STEP)
    v_hat = v / (1.0 - ADAM_B2 ** ADAM_STEP)
    delta = -ADAM_LR * (m_hat / (jnp.sqrt(v_hat) + ADAM_EPS) + ADAM_WD * w)
    return delta, m, v


def _sum_adam(name, landed, grad, by_cols, w, m, v, tr=256):
    R, C = w.shape
    tr = _tile(R, tr if C > 1024 else 2 * tr)
    mine = _block_of(_position()).astype(jnp.int32).reshape(1)

    def body(me_ref, l_ref, own_ref, w_ref, m_ref, v_ref, g_ref, d_ref, nm_ref, nv_ref):
        own = own_ref[...].astype(F32)
        g = None
        for d in range(N_DEV):
            part = jnp.where(me_ref[0] == d, own, l_ref[d].astype(F32))
            g = part if g is None else g + part
        g_ref[...] = g
        d_ref[...], nm_ref[...], nv_ref[...] = _adam(w_ref[...], g, m_ref[...], v_ref[...])

    tile = pl.BlockSpec((tr, C), lambda i, me_ref: (i, 0))
    if by_cols:
        own = pl.BlockSpec((tr, C), lambda i, me_ref: (i, me_ref[0]))
    else:
        own = pl.BlockSpec((tr, C), lambda i, me_ref: (me_ref[0] * (R // tr) + i, 0))
    return pl.pallas_call(
        body, name=name,
        grid_spec=pltpu.PrefetchScalarGridSpec(
            num_scalar_prefetch=1, grid=(R // tr,),
            in_specs=[pl.BlockSpec((N_DEV, tr, C), lambda i, me_ref: (0, i, 0)), own, tile, tile, tile],
            out_specs=[tile] * 4),
        out_shape=[jax.ShapeDtypeStruct((R, C), F32)] * 4,
        compiler_params=_params(("parallel",)),
    )(mine, landed, grad, w, m, v)


def _small_all_reduce(parts, deps=()):
    W = parts.shape[1]

    def body(p_ref, o_ref, slots, send_sems, recv_sems):
        x, y, c = _position()
        me = 4 * x + 2 * y + c
        slots[me] = jnp.sum(p_ref[...], axis=0, keepdims=True)
        peers = [(x, y, 1 - c), (1 - x, y, c), (1 - x, y, 1 - c), (x, 1 - y, c), (x, 1 - y, 1 - c),
                 (1 - x, 1 - y, c), (1 - x, 1 - y, 1 - c)]
        copies = []
        for k, peer in enumerate(peers):
            cp = pltpu.make_async_remote_copy(
                src_ref=slots.at[me], dst_ref=slots.at[me], send_sem=send_sems.at[k], recv_sem=recv_sems.at[k],
                device_id=peer, device_id_type=MESH)
            cp.start()
            copies.append(cp)
        for cp in copies:
            cp.wait()
        total = slots[0]
        for d in range(1, N_DEV):
            total = total + slots[d]
        o_ref[...] = total

    return _pcall(
        body, deps, name="small_all_reduce",
        in_specs=[pl.BlockSpec(memory_space=pltpu.VMEM)], out_specs=pl.BlockSpec(memory_space=pltpu.VMEM),
        out_shape=jax.ShapeDtypeStruct((1, W), F32),
        scratch_shapes=[pltpu.VMEM((N_DEV, 1, W), F32), pltpu.SemaphoreType.DMA((7,)), pltpu.SemaphoreType.DMA((7,))],
    )(parts)


def _adam_small(w, g, m, v):
    def body(w_ref, g_ref, m_ref, v_ref, d_ref, nm_ref, nv_ref):
        d_ref[...], nm_ref[...], nv_ref[...] = _adam(w_ref[...], g_ref[...], m_ref[...], v_ref[...])

    return pl.pallas_call(
        body, name="adam_small",
        in_specs=[pl.BlockSpec(memory_space=pltpu.VMEM)] * 4, out_specs=[pl.BlockSpec(memory_space=pltpu.VMEM)] * 3,
        out_shape=[jax.ShapeDtypeStruct(w.shape, F32)] * 3,
    )(w, g, m, v)


_GATHER_GROUPS = (("w_in",), ("w_out", "w_up", "ple_w"), ("w_down", "w_gate"))
_COL_SHARDED = ("w_in", "w_up", "ple_w")


class _MeshComm:
    def __init__(self, w, mom, var):
        self.w, self.mom, self.var = w, mom, var
        self.out = {}
        self._scatters = {}

    def gather_begin(self):
        self._groups = {}
        token = None
        for tag, first, group_list in (("gather_start0", 0, _GATHER_GROUPS[:1]), ("gather_start1", 1, _GATHER_GROUPS[1:])):
            names = [n for g in group_list for n in g]
            idx = {n: i for i, n in enumerate(names)}
            by_cols = [n in _COL_SHARDED for n in names]
            sems, src, lands, token = _gather_start(tag, [self.w[n].astype(BF16) for n in names], by_cols,
                                                    [[idx[n] for n in g] for g in group_list], token)
            lands = [_place_own("place_" + n, land, s, cols) for n, land, s, cols in zip(names, lands, src, by_cols)]
            for k, g in enumerate(group_list):
                self._groups[first + k] = (sems[2 * k], sems[2 * k + 1], [src[idx[n]] for n in g],
                                           [lands[idx[n]] for n in g])
        return token

    @staticmethod
    def _shard_size(names, offset):
        return lambda bufs, w: _shard_of(bufs[offset + w], 0, names[w] in _COL_SHARDED)

    def gather_arrive(self, gi, after):
        names = _GATHER_GROUPS[gi]
        send, recv, src, lands = self._groups[gi]
        out = _split_wait("gather_arrive%d" % gi, src + lands, send, recv, [4] * len(names),
                          self._shard_size(names, len(names)), after)
        self._arrived = out[len(names):]

    def gather_forward(self, gi):
        by_cols = [n in _COL_SHARDED for n in _GATHER_GROUPS[gi]]
        self._fsems, self._fthru, token = _gather_forward("gather_forward%d" % gi, self._arrived, by_cols)
        return token

    def gather_finish(self, gi, after):
        names = _GATHER_GROUPS[gi]
        out = _split_wait("gather_finish%d" % gi, self._fthru, self._fsems[0], self._fsems[1], [3] * len(names),
                          self._shard_size(names, 0), after)
        return dict(zip(names, out))

    def reduce_begin(self, key, grads):
        names = list(grads)
        sems, thru, token = _scatter_start("scatter_start_" + key, [grads[n] for n in names],
                                           [n in _COL_SHARDED for n in names])
        self._scatters[key] = (names, sems, thru)
        return token

    def reduce_finish(self, key, after):
        names, sems, thru = self._scatters[key]
        nw = len(names)
        out = _split_wait("scatter_wait_" + key, thru, sems[0], sems[1], [N_DEV - 1] * nw,
                          functools.partial(_first_block, offset=nw), after)
        for i, n in enumerate(names):
            self.out[n] = _sum_adam("adam_" + n, out[nw + i], out[i], n in _COL_SHARDED, self.w[n], self.mom[n],
                                    self.var[n])


def _step(x, p, target, gains, comm):
    T, D = x.shape
    n_q = D // (2 * HEAD_DIM)
    n_kv = n_q // GROUP
    cos, sin = _rope_tables(T)
    idx = _bucket_index()

    t = comm.gather_begin()
    u = _rms_fwd("norm_attn", x, gains["attn_norm_g"], deps=(t,))
    comm.gather_arrive(0, u)
    t = comm.gather_forward(0)
    bias = _bias_build(idx, gains["rel_bias_table"].reshape(-1), n_q, deps=(t,))
    full = comm.gather_finish(0, bias)
    proj_a, pb = _in_proj(u, full["w_in"], cos, sin, gains["q_norm_g"], gains["k_norm_g"], n_q + n_kv)
    o_a, lse_a = _attn_a_fwd(pb, n_q, n_kv, 2 * n_q)
    comm.gather_arrive(1, lse_a)
    t = comm.gather_forward(1)
    sink = gains["sink_logits"].reshape(-1)
    b_off = n_q + 2 * n_kv
    o_cat, lse_b = _attn_b_fwd(pb, bias, sink, o_a, b_off, n_q, n_kv, deps=(t,))
    full.update(comm.gather_finish(1, lse_b))
    h1, m_in = _mm_nn_rms("out_proj", o_cat, full["w_out"], x, gains["mlp_norm_g"])

    def up_epilogue(acc, extra, outs):
        outs[0][...] = acc.astype(BF16)
        r = jnp.maximum(acc, 0.0)
        outs[1][...] = (r * r).astype(BF16)

    a_act, f_act = _mm_nn("up_proj", m_in, full["w_up"], epilogue=up_epilogue, out_dtypes=[BF16, BF16], tn=2048)
    comm.gather_arrive(2, f_act)
    t = comm.gather_forward(2)
    p_b = p.astype(BF16)
    pe = _mm_nn("ple_proj", p_b, full["ple_w"], deps=(t,))
    full.update(comm.gather_finish(2, pe))
    h2 = _mm_nn("down_proj", f_act, full["w_down"], epilogue=_store_add, extras=(h1,), tn=512)
    gn = _rms_fwd("norm_gate", h2, gains["gate_norm_g"])

    dh3, dz, dpe, dg_final, dg_ple, loss_part = _gate_tail(gn, full["w_gate"], h2, pe, target, gains["ple_norm_g"],
                                                           gains["final_norm_g"])
    gw_gate = _mm_tn("grad_w_gate", gn, dz)
    gw_ple = _mm_tn("grad_ple_w", p_b, dpe)
    dh2, dh2_b, dg_gate = _mm_nt_rms_bwd("d_gate_in", dz, full["w_gate"], h2, gains["gate_norm_g"], dh3)
    gw_down = _mm_tn("grad_w_down", f_act, dh2_b, tn=1024)
    t = comm.reduce_begin("b", dict(w_gate=gw_gate, ple_w=gw_ple, w_down=gw_down))

    def act_bwd(acc, extra, outs):
        outs[0][...] = (acc * (2.0 * jnp.maximum(extra[0][...].astype(F32), 0.0))).astype(BF16)

    da = _mm_nt("d_act", dh2_b, full["w_down"], out_dtype=BF16, epilogue=act_bwd, extras=(a_act,), tn=2048, deps=(t,))
    gw_up = _mm_tn("grad_w_up", m_in, da, tn=1024)
    dm = _mm_nt("d_mlp_in", da, full["w_up"], out_dtype=BF16, tn=512)
    dh1, dh1_b, dg_mlp = _rms_bwd("norm_mlp_bwd", dm, h1, gains["mlp_norm_g"], dh2)
    gw_out = _mm_tn("grad_w_out", o_cat, dh1_b)
    t = comm.reduce_begin("d", dict(w_up=gw_up, w_out=gw_out))
    d_o = _mm_nt("d_attn_out", dh1_b, full["w_out"], out_dtype=BF16, deps=(t,))
    dqa, dka_t, dva_t = _attn_a_bwd(pb, o_cat, d_o, lse_a, n_q, n_kv)
    dqb, dkb, dvb, dbias, dsink_raw = _attn_b_bwd(pb, o_cat, d_o, lse_b, bias, sink, b_off, n_q, n_kv, n_q)
    dtable, dsink = _table_grads(dbias, dsink_raw, idx)
    dproj, dg_q, dg_k = _dproj(proj_a, dqa, dka_t, dva_t, dqb, dkb, dvb, cos, sin, gains["q_norm_g"], gains["k_norm_g"])
    gw_in = _mm_tn("grad_w_in", u, dproj, tn=1024)
    t = comm.reduce_begin("e", dict(w_in=gw_in))
    dx, dg_attn = _mm_nt_rms_bwd("d_attn_in", dproj, full["w_in"], x, gains["attn_norm_g"], dh1, with_bf16=False,
                                 deps=(t,))
    for key in "bd":
        comm.reduce_finish(key, dx)

    parts = jnp.concatenate([dg_attn, dg_mlp, dg_ple, dg_gate, dg_final, dg_q, dg_k, dtable, dsink, loss_part], axis=1)
    return dx, parts


_SHARDED = ("w_in", "w_out", "w_up", "w_down", "ple_w", "w_gate")
_VECTORS = ("attn_norm_g", "mlp_norm_g", "ple_norm_g", "gate_norm_g", "final_norm_g")
_ORDER = ("attn_norm_g", "w_in", "q_norm_g", "k_norm_g", "sink_logits", "w_out", "mlp_norm_g", "w_up", "w_down",
          "ple_w", "ple_norm_g", "gate_norm_g", "w_gate", "rel_bias_table", "final_norm_g")


def _pack_small(vals, n_heads):
    lane_pad = lambda v: jnp.pad(v, ((0, 0), (0, LANES - v.shape[1])))
    table = lane_pad(vals["rel_bias_table"].T).reshape(1, n_heads * LANES)
    return jnp.concatenate(
        [vals[n].reshape(1, -1) for n in _VECTORS] + [vals["q_norm_g"], vals["k_norm_g"], table,
                                                      lane_pad(vals["sink_logits"]), jnp.zeros((1, LANES), F32)], axis=1)


def _unpack_small(row, like, n_heads):
    out, off = {}, 0
    for n in _VECTORS:
        out[n] = row[:, off:off + like[n].size].reshape(like[n].shape)
        off += like[n].size
    for n in ("q_norm_g", "k_norm_g"):
        out[n] = row[:, off:off + LANES]
        off += LANES
    out["rel_bias_table"] = row[:, off:off + n_heads * LANES].reshape(n_heads, LANES)[:, :N_BUCKETS].T
    off += n_heads * LANES
    out["sink_logits"] = row[:, off:off + n_heads]
    off += LANES
    return out, row[0, off]


def kernel(x, p, attn_norm_g, w_in, q_norm_g, k_norm_g, sink_logits, w_out, mlp_norm_g, w_up, w_down, ple_w, ple_norm_g, gate_norm_g, w_gate, rel_bias_table, final_norm_g, loss_target, m_attn_norm_g, m_w_in, m_q_norm_g, m_k_norm_g, m_sink_logits, m_w_out, m_mlp_norm_g, m_w_up, m_w_down, m_ple_w, m_ple_norm_g, m_gate_norm_g, m_w_gate, m_rel_bias_table, m_final_norm_g, v_attn_norm_g, v_w_in, v_q_norm_g, v_k_norm_g, v_sink_logits, v_w_out, v_mlp_norm_g, v_w_up, v_w_down, v_ple_w, v_ple_norm_g, v_gate_norm_g, v_w_gate, v_rel_bias_table, v_final_norm_g):
    w = dict(attn_norm_g=attn_norm_g, w_in=w_in[0], q_norm_g=q_norm_g, k_norm_g=k_norm_g, sink_logits=sink_logits,
             w_out=w_out[0], mlp_norm_g=mlp_norm_g, w_up=w_up[0], w_down=w_down[0], ple_w=ple_w[0],
             ple_norm_g=ple_norm_g, gate_norm_g=gate_norm_g, w_gate=w_gate[0], rel_bias_table=rel_bias_table,
             final_norm_g=final_norm_g)
    mom = dict(attn_norm_g=m_attn_norm_g, w_in=m_w_in[0], q_norm_g=m_q_norm_g, k_norm_g=m_k_norm_g,
               sink_logits=m_sink_logits, w_out=m_w_out[0], mlp_norm_g=m_mlp_norm_g, w_up=m_w_up[0],
               w_down=m_w_down[0], ple_w=m_ple_w[0], ple_norm_g=m_ple_norm_g, gate_norm_g=m_gate_norm_g,
               w_gate=m_w_gate[0], rel_bias_table=m_rel_bias_table, final_norm_g=m_final_norm_g)
    var = dict(attn_norm_g=v_attn_norm_g, w_in=v_w_in[0], q_norm_g=v_q_norm_g, k_norm_g=v_k_norm_g,
               sink_logits=v_sink_logits, w_out=v_w_out[0], mlp_norm_g=v_mlp_norm_g, w_up=v_w_up[0],
               w_down=v_w_down[0], ple_w=v_ple_w[0], ple_norm_g=v_ple_norm_g, gate_norm_g=v_gate_norm_g,
               w_gate=v_w_gate[0], rel_bias_table=v_rel_bias_table, final_norm_g=v_final_norm_g)
    D = x.shape[-1]
    n_heads = D // (2 * HEAD_DIM)

    gains = {n: w[n] for n in w if n not in _SHARDED}
    gains["final_norm_g"] = final_norm_g.reshape(1, -1)

    comm = _MeshComm(w, mom, var)
    dx, parts = _step(x[0], p[0, 0], loss_target[0], gains, comm)

    small_g = _small_all_reduce(parts, deps=[comm.out[n][0] for n in comm.out])
    comm.reduce_finish("e", small_g)

    g_out, d_out, m_out, v_out = {}, {}, {}, {}
    for n in _SHARDED:
        g, d, nm, nv = comm.out[n]
        g_out[n], d_out[n], m_out[n], v_out[n] = g[None], d[None], nm[None], nv[None]

    small = {n: v for n, v in w.items() if n not in _SHARDED}
    pack = lambda vals: _pack_small({n: vals[n] for n in small}, n_heads)
    sd, sm, sv = _adam_small(pack(w), small_g, pack(mom), pack(var))
    sg, loss = _unpack_small(small_g, small, n_heads)
    g_out.update(sg)
    for dst, row in ((d_out, sd), (m_out, sm), (v_out, sv)):
        dst.update(_unpack_small(row, small, n_heads)[0])

    return (loss, dx[None], *[g_out[n] for n in _ORDER], *[d_out[n] for n in _ORDER],
            *[m_out[n] for n in _ORDER], *[v_out[n] for n in _ORDER])
```

```python
import functools
import math

import numpy as np
import jax
import jax.numpy as jnp
from jax import lax
from jax.experimental import pallas as pl
from jax.experimental.pallas import tpu as pltpu

F32 = jnp.float32
BF16 = jnp.bfloat16

N_DEV = 8
N_CHIP = 4
HEAD_DIM = 128
GROUP = 4
GRID_W = 64
WINDOW = 128
BLOCK_Q = 128
N_BUCKETS = 32
MAX_DISTANCE = 128
ROPE_THETA = 10000.0
EPS = 1e-6
NEG_INF = -1e30
ADAM_LR = 0.001
ADAM_B1 = 0.9
ADAM_B2 = 0.999
ADAM_EPS = 1e-08
ADAM_WD = 0.01
ADAM_STEP = 10
LOG2E = math.log2(math.e)
LANES = 128
SUBLANES = 8
VMEM_LIMIT_BYTES = 60 * 1024 * 1024
MESH = pl.DeviceIdType.MESH

_NT = (((1,), (1,)), ((), ()))
_NN = (((1,), (0,)), ((), ()))
_TN = (((0,), (0,)), ((), ()))


def _tile(dim, pref):
    return pref if dim % pref == 0 else dim


def _params(sem=None):
    return pltpu.CompilerParams(dimension_semantics=sem, vmem_limit_bytes=VMEM_LIMIT_BYTES)


_HBM = pl.BlockSpec(memory_space=pltpu.HBM)
_SEM = pl.BlockSpec(memory_space=pltpu.SEMAPHORE)
_ANY = pl.BlockSpec(memory_space=pl.ANY)
_VMEM = pl.BlockSpec(memory_space=pltpu.VMEM)
_EFFECT = pltpu.SideEffectType.DATAFLOW_SIDE_EFFECTING


def _pcall(body, deps=(), *, in_specs, into=None, **kw):
    deps = [d for d in deps if d is not None]
    nd = len(deps)
    if into is not None:
        deps = [into[0]] + deps
        nd += 1
        kw["input_output_aliases"] = {0: into[1]}

    def wrapped(*refs):
        body(*refs[nd:])

    call = pl.pallas_call(wrapped, in_specs=[_ANY] * nd + list(in_specs), **kw)
    return lambda *args: call(*deps, *args)


def _mm(name, a, b, dims, grid, a_spec, b_spec, out_shape, out_specs, acc_shape, epilogue,
        extras=(), extra_specs=(), deps=(), semantics=("parallel", "parallel", "arbitrary")):
    nk = grid[2]
    n_extra = len(extras)

    def body(*refs):
        a_ref, b_ref = refs[0], refs[1]
        extra = refs[2:2 + n_extra]
        outs = refs[2 + n_extra:-1]
        acc = refs[-1]
        part = lax.dot_general(a_ref[...], b_ref[...], dims, preferred_element_type=F32)
        if nk == 1:
            epilogue(part, extra, outs)
        else:
            k = pl.program_id(2)

            @pl.when(k == 0)
            def _():
                acc[...] = part

            @pl.when(k > 0)
            def _():
                acc[...] += part

            @pl.when(k == nk - 1)
            def _():
                epilogue(acc[...], extra, outs)

    return _pcall(
        body, deps, name=name, grid=grid,
        in_specs=[a_spec, b_spec, *extra_specs],
        out_specs=out_specs, out_shape=out_shape,
        scratch_shapes=[pltpu.VMEM(acc_shape if nk > 1 else (SUBLANES, LANES), F32)],
        compiler_params=_params(semantics),
    )(a, b, *extras)


def _store(dtype):
    def ep(acc, extra, outs):
        outs[0][...] = acc.astype(dtype)
    return ep


def _store_add(acc, extra, outs):
    outs[0][...] = acc + extra[0][...]


def _mm_nn(name, a, b, out_dtype=F32, epilogue=None, extras=(), n_out=1, out_dtypes=None, tm=1024, tn=1024, tk=None,
           deps=()):
    M, K = a.shape
    N = b.shape[1]
    tm, tn, tk = _tile(M, tm), _tile(N, tn), _tile(K, tk or K)
    b_spec = pl.BlockSpec((tk, tn), lambda i, j, k: (k, j))
    grid = (M // tm, N // tn, K // tk)
    o_spec = pl.BlockSpec((tm, tn), lambda i, j, k: (i, j))
    out_dtypes = out_dtypes or [out_dtype] * n_out
    out_shape = [jax.ShapeDtypeStruct((M, N), d) for d in out_dtypes]
    res = _mm(name, a, b, _NN, grid, pl.BlockSpec((tm, tk), lambda i, j, k: (i, k)), b_spec,
              out_shape, [o_spec] * len(out_dtypes), (tm, tn), epilogue or _store(out_dtype),
              extras, [o_spec] * len(extras), deps)
    return res if len(out_dtypes) > 1 else res[0]


def _mm_nt(name, a, b, out_dtype=F32, epilogue=None, extras=(), tm=1024, tn=1024, tk=None, deps=()):
    M, C = a.shape
    N = b.shape[0]
    tm, tn, tk = _tile(M, tm), _tile(N, tn), _tile(C, tk or C)
    b_spec = pl.BlockSpec((tn, tk), lambda i, j, k: (j, k))
    grid = (M // tm, N // tn, C // tk)
    o_spec = pl.BlockSpec((tm, tn), lambda i, j, k: (i, j))
    return _mm(name, a, b, _NT, grid, pl.BlockSpec((tm, tk), lambda i, j, k: (i, k)), b_spec,
               [jax.ShapeDtypeStruct((M, N), out_dtype)], [o_spec], (tm, tn), epilogue or _store(out_dtype),
               extras, [o_spec] * len(extras), deps)[0]


def _mm_tn(name, a, b, out_dtype=BF16, tm=1024, tn=512, tk=None, deps=()):
    T, M = a.shape
    N = b.shape[1]
    tm, tn, tk = _tile(M, tm), _tile(N, tn), _tile(T, tk or T)
    out_shape = jax.ShapeDtypeStruct((M, N), out_dtype)
    o_spec = pl.BlockSpec((tm, tn), lambda i, j, k: (i, j))
    grid = (M // tm, N // tn, T // tk)
    return _mm(name, a, b, _TN, grid, pl.BlockSpec((tk, tm), lambda i, j, k: (k, i)),
               pl.BlockSpec((tk, tn), lambda i, j, k: (k, j)), [out_shape], [o_spec], (tm, tn), _store(out_dtype),
               deps=deps)[0]


def _mean_last(v):
    return jnp.mean(v, axis=-1, keepdims=True)


def _rows_to_sublanes(v):
    r, c = v.shape
    return jnp.sum(v.reshape(r // SUBLANES, SUBLANES, c), axis=0)


def _accumulate(ref, val, first):
    @pl.when(first)
    def _():
        ref[...] = val

    @pl.when(jnp.logical_not(first))
    def _():
        ref[...] += val


def _rms_fwd(name, x, g, tr=512, deps=()):
    T, D = x.shape
    tr = _tile(T, tr)

    def body(x_ref, g_ref, o_ref):
        xv = x_ref[...]
        r = lax.rsqrt(_mean_last(xv * xv) + EPS)
        o_ref[...] = (xv * r * g_ref[...]).astype(BF16)

    row = pl.BlockSpec((tr, D), lambda i: (i, 0))
    return _pcall(
        body, deps, name=name, grid=(T // tr,),
        in_specs=[row, pl.BlockSpec((1, D), lambda i: (0, 0))],
        out_specs=row, out_shape=jax.ShapeDtypeStruct((T, D), BF16),
        compiler_params=_params(("parallel",)),
    )(x, g)


def _rms_bwd(name, dyn, x, g, dres, tr=512, deps=()):
    T, D = x.shape
    tr = _tile(T, tr)

    def body(dy_ref, x_ref, g_ref, dr_ref, dx_ref, dxb_ref, dg_ref):
        xv = x_ref[...]
        r = lax.rsqrt(_mean_last(xv * xv) + EPS)
        xn = xv * r
        dy = dy_ref[...].astype(F32)
        dxn = dy * g_ref[...]
        dx = dr_ref[...] + r * (dxn - xn * _mean_last(dxn * xn))
        dx_ref[...] = dx
        dxb_ref[...] = dx.astype(BF16)
        _accumulate(dg_ref, _rows_to_sublanes(dy * xn), pl.program_id(0) == 0)

    row = pl.BlockSpec((tr, D), lambda i: (i, 0))
    return _pcall(
        body, deps, name=name, grid=(T // tr,),
        in_specs=[row, row, pl.BlockSpec((1, D), lambda i: (0, 0)), row],
        out_specs=[row, row, pl.BlockSpec((SUBLANES, D), lambda i: (0, 0))],
        out_shape=[jax.ShapeDtypeStruct((T, D), F32), jax.ShapeDtypeStruct((T, D), BF16),
                   jax.ShapeDtypeStruct((SUBLANES, D), F32)],
        compiler_params=_params(("arbitrary",)),
    )(dyn, x, g, dres)


def _mm_nn_rms(name, a, b, res, g, tm=512, deps=()):
    M, K = a.shape
    N = b.shape[1]
    tm = _tile(M, tm)

    def epilogue(acc, extra, outs):
        h = acc + extra[0][...]
        outs[0][...] = h
        outs[1][...] = (h * lax.rsqrt(_mean_last(h * h) + EPS) * extra[1][...]).astype(BF16)

    row = pl.BlockSpec((tm, N), lambda i, j, k: (i, 0))
    return _mm(name, a, b, _NN, (M // tm, 1, 1), pl.BlockSpec((tm, K), lambda i, j, k: (i, 0)),
               pl.BlockSpec((K, N), lambda i, j, k: (0, 0)),
               [jax.ShapeDtypeStruct((M, N), F32), jax.ShapeDtypeStruct((M, N), BF16)], [row, row], (tm, N), epilogue,
               (res, g), [row, pl.BlockSpec((1, N), lambda i, j, k: (0, 0))], deps)


def _mm_nt_rms_bwd(name, a, b, x, g, dres, with_bf16=True, tm=256, deps=()):
    M, C = a.shape
    N = b.shape[0]
    tm = _tile(M, tm)

    def epilogue(dy, extra, outs):
        x_ref, dr_ref, g_ref = extra
        xv = x_ref[...]
        r = lax.rsqrt(_mean_last(xv * xv) + EPS)
        xn = xv * r
        dxn = dy * g_ref[...]
        dx = dr_ref[...] + r * (dxn - xn * _mean_last(dxn * xn))
        outs[0][...] = dx
        if with_bf16:
            outs[1][...] = dx.astype(BF16)
        _accumulate(outs[-1], _rows_to_sublanes(dy * xn), pl.program_id(0) == 0)

    row = pl.BlockSpec((tm, N), lambda i, j, k: (i, 0))
    copies = [jax.ShapeDtypeStruct((M, N), F32)] + ([jax.ShapeDtypeStruct((M, N), BF16)] if with_bf16 else [])
    return _mm(name, a, b, _NT, (M // tm, 1, 1), pl.BlockSpec((tm, C), lambda i, j, k: (i, 0)),
               pl.BlockSpec((N, C), lambda i, j, k: (0, 0)),
               copies + [jax.ShapeDtypeStruct((SUBLANES, N), F32)],
               [row] * len(copies) + [pl.BlockSpec((SUBLANES, N), lambda i, j, k: (0, 0))], (tm, N), epilogue,
               (x, dres, g), [row, row, pl.BlockSpec((1, N), lambda i, j, k: (0, 0))], deps,
               semantics=("arbitrary", "arbitrary", "arbitrary"))


def _gate_tail(gn, w_gate, h2, pe, target, g_ple, g_final, tm=256):
    T, D = h2.shape
    tm = _tile(T, tm)

    def epilogue(z, extra, outs):
        h2_ref, pe_ref, t_ref, gp_ref, gf_ref = extra
        dh3_ref, dz_ref, dpe_ref, dgf_ref, dgp_ref, loss_ref = outs
        first = pl.program_id(0) == 0
        pev = pe_ref[...]
        r3 = lax.rsqrt(_mean_last(pev * pev) + EPS)
        en = pev * r3
        e = en * gp_ref[...]
        gate = 1.0 / (1.0 + jnp.exp(-z))
        h3 = h2_ref[...] + gate * e
        r5 = lax.rsqrt(_mean_last(h3 * h3) + EPS)
        hn = h3 * r5
        diff = hn * gf_ref[...] - t_ref[...]
        loss_rows = 0.5 * _mean_last(diff * diff)
        row0 = lax.broadcasted_iota(jnp.int32, (SUBLANES, LANES), 0) == 0
        _accumulate(loss_ref, jnp.where(row0, jnp.sum(loss_rows), 0.0), first)
        dy = diff * (1.0 / D)
        _accumulate(dgf_ref, _rows_to_sublanes(dy * hn), first)
        dhn = dy * gf_ref[...]
        dh3 = r5 * (dhn - hn * _mean_last(dhn * hn))
        dh3_ref[...] = dh3
        dgate = dh3 * e
        de = dh3 * gate
        dz_ref[...] = (dgate * gate * (1.0 - gate)).astype(BF16)
        _accumulate(dgp_ref, _rows_to_sublanes(de * en), first)
        den = de * gp_ref[...]
        dpe_ref[...] = (r3 * (den - en * _mean_last(den * en))).astype(BF16)

    row = pl.BlockSpec((tm, D), lambda i, j, k: (i, 0))
    vec = pl.BlockSpec((1, D), lambda i, j, k: (0, 0))
    part = pl.BlockSpec((SUBLANES, D), lambda i, j, k: (0, 0))
    return _mm("gate_tail", gn, w_gate, _NN, (T // tm, 1, 1), row, pl.BlockSpec(w_gate.shape, lambda i, j, k: (0, 0)),
               [jax.ShapeDtypeStruct((T, D), F32), jax.ShapeDtypeStruct((T, D), BF16),
                jax.ShapeDtypeStruct((T, D), BF16), jax.ShapeDtypeStruct((SUBLANES, D), F32),
                jax.ShapeDtypeStruct((SUBLANES, D), F32), jax.ShapeDtypeStruct((SUBLANES, LANES), F32)],
               [row, row, row, part, part, pl.BlockSpec((SUBLANES, LANES), lambda i, j, k: (0, 0))], (tm, D), epilogue,
               (h2, pe, target, g_ple, g_final), [row, row, row, vec, vec],
               semantics=("arbitrary", "arbitrary", "arbitrary"))


def _rope_tables(T):
    pos = np.arange(T)
    half = HEAD_DIM // 2
    inv = (ROPE_THETA ** (-np.arange(0, half, 2, dtype=np.float32) / half)).astype(np.float32)
    ang_r = (pos // GRID_W).astype(np.float32)[:, None] * inv
    ang_c = (pos % GRID_W).astype(np.float32)[:, None] * inv
    cos = np.concatenate([np.cos(ang_r), np.cos(ang_r), np.cos(ang_c), np.cos(ang_c)], axis=-1)
    sin = np.concatenate([-np.sin(ang_r), np.sin(ang_r), -np.sin(ang_c), np.sin(ang_c)], axis=-1)
    return jnp.asarray(cos, F32), jnp.asarray(sin, F32)


def _swap32(x):
    lane = lax.broadcasted_iota(jnp.int32, x.shape, 1)
    return jnp.where((lane % 64) < 32, pltpu.roll(x, 96, 1), pltpu.roll(x, 32, 1))


def _in_proj(u, w_in, cos, sin, g_q, g_k, n_norm, tm=512):
    T, K = u.shape
    W = w_in.shape[1]
    tm = _tile(T, tm)
    n_q = n_norm * GROUP // (GROUP + 1)
    wa = n_norm * HEAD_DIM

    def epilogue(acc, extra, outs):
        c_ref, s_ref, gq_ref, gk_ref = extra
        raw_ref, o_ref = outs
        c, s = c_ref[...], s_ref[...]
        raw_ref[...] = acc[:, :wa]
        for h in range(n_norm):
            cols = slice(h * HEAD_DIM, (h + 1) * HEAD_DIM)
            xv = acc[:, cols]
            g = gq_ref[...] if h < n_q else gk_ref[...]
            xn = xv * lax.rsqrt(_mean_last(xv * xv) + EPS) * g
            o_ref[:, cols] = (xn * c + _swap32(xn) * s).astype(BF16)
        o_ref[:, wa:] = acc[:, wa:].astype(BF16)

    tab = pl.BlockSpec((tm, HEAD_DIM), lambda i, j, k: (i, 0))
    vec = pl.BlockSpec((1, HEAD_DIM), lambda i, j, k: (0, 0))
    return _mm("in_proj", u, w_in, _NN, (T // tm, 1, 1), pl.BlockSpec((tm, K), lambda i, j, k: (i, 0)),
               pl.BlockSpec((K, W), lambda i, j, k: (0, 0)),
               [jax.ShapeDtypeStruct((T, wa), F32), jax.ShapeDtypeStruct((T, W), BF16)],
               [pl.BlockSpec((tm, wa), lambda i, j, k: (i, 0)), pl.BlockSpec((tm, W), lambda i, j, k: (i, 0))],
               (tm, W), epilogue, (cos, sin, g_q, g_k), [tab, tab, vec, vec])


def _dproj(proj_a, dqa, dka_t, dva_t, dqb, dkb, dvb, cos, sin, g_q, g_k, tr=512):
    T, wa = proj_a.shape
    tr = _tile(T, tr)
    n_q = dqa.shape[1] // HEAD_DIM
    wkv = dka_t.shape[0]
    W = wa + wkv + dqb.shape[1] + dkb.shape[1] + dvb.shape[1]

    def body(p_ref, dqa_ref, dkat_ref, dvat_ref, dqb_ref, dkb_ref, dvb_ref, c_ref, s_ref, gq_ref, gk_ref,
             o_ref, dgq_ref, dgk_ref):
        c, s = c_ref[...], s_ref[...]
        dka = dkat_ref[...].T
        dgq = jnp.zeros((SUBLANES, HEAD_DIM), F32)
        dgk = jnp.zeros((SUBLANES, HEAD_DIM), F32)
        for h in range(wa // HEAD_DIM):
            cols = slice(h * HEAD_DIM, (h + 1) * HEAD_DIM)
            xv = p_ref[:, cols]
            r = lax.rsqrt(_mean_last(xv * xv) + EPS)
            xn = xv * r
            if h < n_q:
                d = dqa_ref[:, cols]
                g = gq_ref[...]
            else:
                d = dka[:, (h - n_q) * HEAD_DIM:(h - n_q + 1) * HEAD_DIM]
                g = gk_ref[...]
            dqn = d * c + _swap32(d * s)
            part = _rows_to_sublanes(dqn * xn)
            if h < n_q:
                dgq = dgq + part
            else:
                dgk = dgk + part
            dxn = dqn * g
            o_ref[:, cols] = (r * (dxn - xn * _mean_last(dxn * xn))).astype(BF16)
        o_ref[:, wa:wa + wkv] = dvat_ref[...].T.astype(BF16)
        off = wa + wkv
        for ref in (dqb_ref, dkb_ref, dvb_ref):
            w = ref.shape[1]
            o_ref[:, off:off + w] = ref[...].astype(BF16)
            off += w
        first = pl.program_id(0) == 0
        _accumulate(dgq_ref, dgq, first)
        _accumulate(dgk_ref, dgk, first)

    def row(w):
        return pl.BlockSpec((tr, w), lambda i: (i, 0))

    col = pl.BlockSpec((wkv, tr), lambda i: (0, i))
    vec = pl.BlockSpec((1, HEAD_DIM), lambda i: (0, 0))
    part = pl.BlockSpec((SUBLANES, HEAD_DIM), lambda i: (0, 0))
    return pl.pallas_call(
        body, name="dproj", grid=(T // tr,),
        in_specs=[row(wa), row(dqa.shape[1]), col, col, row(dqb.shape[1]),
                  row(dkb.shape[1]), row(dvb.shape[1]), row(HEAD_DIM), row(HEAD_DIM), vec, vec],
        out_specs=[row(W), part, part],
        out_shape=[jax.ShapeDtypeStruct((T, W), BF16), jax.ShapeDtypeStruct((SUBLANES, HEAD_DIM), F32),
                   jax.ShapeDtypeStruct((SUBLANES, HEAD_DIM), F32)],
        compiler_params=_params(("arbitrary",)),
    )(proj_a, dqa, dka_t, dva_t, dqb, dkb, dvb, cos, sin, g_q, g_k)


def _attn_a_fwd(pb, n_q, n_kv, out_heads, tq=1024, tc=1024):
    T = pb.shape[0]
    tq, tc = _tile(T, tq), _tile(T, tc)
    scale = HEAD_DIM ** -0.5
    c = scale * LOG2E

    def body(q_ref, k_ref, v_ref, o_ref, lse_ref):
        q = q_ref[...]
        m = l = acc = None
        for j in range(T // tc):
            keys = slice(j * tc, (j + 1) * tc)
            s = lax.dot_general(q, k_ref[keys, :], _NT, preferred_element_type=F32)
            mj = jnp.max(s, axis=-1, keepdims=True)
            m_new = mj if j == 0 else jnp.maximum(m, mj)
            p = jnp.exp2((s - m_new) * c)
            pv = lax.dot_general(p.astype(BF16), v_ref[keys, :], _NN, preferred_element_type=F32)
            if j == 0:
                l, acc = jnp.sum(p, axis=-1, keepdims=True), pv
            else:
                alpha = jnp.exp2((m - m_new) * c)
                l = alpha * l + jnp.sum(p, axis=-1, keepdims=True)
                acc = alpha * acc + pv
            m = m_new
        o_ref[...] = (acc / l).astype(BF16)
        lse_ref[...] = m * scale + jnp.log(l)

    return pl.pallas_call(
        body, name="attn_a_fwd", grid=(n_kv, GROUP, T // tq),
        in_specs=[pl.BlockSpec((tq, HEAD_DIM), lambda kv, g, i: (i, kv * GROUP + g)),
                  pl.BlockSpec((T, HEAD_DIM), lambda kv, g, i: (0, n_q + kv)),
                  pl.BlockSpec((T, HEAD_DIM), lambda kv, g, i: (0, n_q + n_kv + kv))],
        out_specs=[pl.BlockSpec((tq, HEAD_DIM), lambda kv, g, i: (i, kv * GROUP + g)),
                   pl.BlockSpec((None, tq, 1), lambda kv, g, i: (kv * GROUP + g, i, 0))],
        out_shape=[jax.ShapeDtypeStruct((T, out_heads * HEAD_DIM), BF16), jax.ShapeDtypeStruct((n_q, T, 1), F32)],
        compiler_params=_params(("parallel", "parallel", "parallel")),
    )(pb, pb, pb)


def _attn_a_bwd(pb, o_cat, d_o, lse, n_q, n_kv, tq=1024, tc=256):
    T = pb.shape[0]
    tq, tc = _tile(T, tq), _tile(T, tc)
    scale = HEAD_DIM ** -0.5
    c = scale * LOG2E

    def body(q_ref, k_ref, v_ref, o_ref, do_ref, lse_ref, dq_ref, dkt_ref, dvt_ref):
        q, do = q_ref[...], do_ref[...]
        qt, dot = q.T, do.T
        delta = jnp.sum(do.astype(F32) * o_ref[...].astype(F32), axis=-1, keepdims=True)
        lse2 = lse_ref[...] * LOG2E

        @pl.when(jnp.logical_and(pl.program_id(1) == 0, pl.program_id(2) == 0))
        def _():
            dkt_ref[...] = jnp.zeros(dkt_ref.shape, F32)
            dvt_ref[...] = jnp.zeros(dvt_ref.shape, F32)

        dq = None
        for j in range(T // tc):
            keys = slice(j * tc, (j + 1) * tc)
            kc, vc = k_ref[keys, :], v_ref[keys, :]
            s = lax.dot_general(q, kc, _NT, preferred_element_type=F32)
            p = jnp.exp2(s * c - lse2)
            dp = lax.dot_general(do, vc, _NT, preferred_element_type=F32)
            ds = (p * (dp - delta) * scale).astype(BF16)
            dqj = lax.dot_general(ds, kc, _NN, preferred_element_type=F32)
            dq = dqj if dq is None else dq + dqj
            dvt_ref[:, keys] += lax.dot_general(dot, p.astype(BF16), _NN, preferred_element_type=F32)
            dkt_ref[:, keys] += lax.dot_general(qt, ds, _NN, preferred_element_type=F32)
        dq_ref[...] = dq

    qmap = lambda kv, g, i: (i, kv * GROUP + g)
    return pl.pallas_call(
        body, name="attn_a_bwd", grid=(n_kv, GROUP, T // tq),
        in_specs=[pl.BlockSpec((tq, HEAD_DIM), qmap),
                  pl.BlockSpec((T, HEAD_DIM), lambda kv, g, i: (0, n_q + kv)),
                  pl.BlockSpec((T, HEAD_DIM), lambda kv, g, i: (0, n_q + n_kv + kv)),
                  pl.BlockSpec((tq, HEAD_DIM), qmap),
                  pl.BlockSpec((tq, HEAD_DIM), qmap),
                  pl.BlockSpec((None, tq, 1), lambda kv, g, i: (kv * GROUP + g, i, 0))],
        out_specs=[pl.BlockSpec((tq, HEAD_DIM), qmap),
                   pl.BlockSpec((HEAD_DIM, T), lambda kv, g, i: (kv, 0)),
                   pl.BlockSpec((HEAD_DIM, T), lambda kv, g, i: (kv, 0))],
        out_shape=[jax.ShapeDtypeStruct((T, n_q * HEAD_DIM), F32),
                   jax.ShapeDtypeStruct((n_kv * HEAD_DIM, T), F32),
                   jax.ShapeDtypeStruct((n_kv * HEAD_DIM, T), F32)],
        compiler_params=_params(("parallel", "arbitrary", "arbitrary")),
    )(pb, pb, pb, o_cat, d_o, lse)


def _bucket_index():
    r = np.arange(BLOCK_Q)[:, None]
    j = np.arange(3 * BLOCK_Q)[None, :]
    rel = (j - BLOCK_Q) - r
    nb = N_BUCKETS // 2
    ret = np.where(rel > 0, nb, 0)
    n = np.abs(rel)
    max_exact = nb // 2
    nf = np.maximum(n, 1).astype(np.float32)
    large = max_exact + (np.log(nf / max_exact) / math.log(MAX_DISTANCE / max_exact) * (nb - max_exact)).astype(np.int32)
    large = np.minimum(large, nb - 1)
    return jnp.asarray(ret + np.where(n < max_exact, n, large), jnp.int32)


def _bias_build(idx, table_flat, n_heads, deps=()):
    def body(idx_ref, tab_ref, o_ref):
        h = pl.program_id(0)
        iv = idx_ref[...]
        acc = jnp.zeros(iv.shape, F32)
        for b in range(N_BUCKETS):
            acc = jnp.where(iv == b, tab_ref[b * n_heads + h], acc)
        r = lax.broadcasted_iota(jnp.int32, iv.shape, 0)
        j = lax.broadcasted_iota(jnp.int32, iv.shape, 1)
        o_ref[...] = jnp.where(jnp.abs(j - BLOCK_Q - r) <= WINDOW, acc, NEG_INF)

    return _pcall(
        body, deps, name="bias_build", grid=(n_heads,),
        in_specs=[pl.BlockSpec(idx.shape, lambda h: (0, 0)), pl.BlockSpec(memory_space=pltpu.SMEM)],
        out_specs=pl.BlockSpec((None,) + idx.shape, lambda h: (h, 0, 0)),
        out_shape=jax.ShapeDtypeStruct((n_heads,) + idx.shape, F32),
        compiler_params=_params(("parallel",)),
    )(idx, table_flat)


def _in_sequence(n, T):
    j = lax.broadcasted_iota(jnp.int32, (GROUP * BLOCK_Q, 3 * BLOCK_Q), 1)
    kabs = n * BLOCK_Q + j - BLOCK_Q
    return (kabs >= 0) & (kabs < T)


def _per_head_rows(values):
    head = lax.broadcasted_iota(jnp.int32, (GROUP * BLOCK_Q, 1), 0) // BLOCK_Q
    col = jnp.zeros((GROUP * BLOCK_Q, 1), F32)
    for g, v in enumerate(values):
        col = jnp.where(head == g, v, col)
    return col


def _band_specs(col, nblk, sb):
    return [pl.BlockSpec((BLOCK_Q, HEAD_DIM), lambda kv, i: (jnp.maximum(sb * i - 1, 0), col(kv))),
            pl.BlockSpec((sb * BLOCK_Q, HEAD_DIM), lambda kv, i: (i, col(kv))),
            pl.BlockSpec((BLOCK_Q, HEAD_DIM), lambda kv, i: (jnp.minimum(sb * i + sb, nblk - 1), col(kv)))]


def _head_specs(base, rows):
    return [pl.BlockSpec((rows, HEAD_DIM), functools.partial(lambda kv, i, g: (i, base + kv * GROUP + g), g=g))
            for g in range(GROUP)]


def _attn_b_fwd(pb, bias, sink, o_all, q_off, n_q, n_kv, deps=(), sb=16):
    T = pb.shape[0]
    nblk = T // BLOCK_Q
    sb = min(sb, nblk)
    tq = sb * BLOCK_Q
    scale = HEAD_DIM ** -0.5

    def body(*refs):
        q_refs = refs[0:GROUP]
        k_refs, v_refs = refs[GROUP:GROUP + 3], refs[GROUP + 3:GROUP + 6]
        bias_ref, sink_ref, o_ref, lse_ref = refs[GROUP + 6:]
        kv, i = pl.program_id(0), pl.program_id(1)
        kb = jnp.concatenate([r[...] for r in k_refs], axis=0)
        vb = jnp.concatenate([r[...] for r in v_refs], axis=0)
        bias_all = bias_ref[...].reshape(GROUP * BLOCK_Q, 3 * BLOCK_Q)
        sk = _per_head_rows([sink_ref[kv * GROUP + g] for g in range(GROUP)])
        for b in range(sb):
            rows = slice(b * BLOCK_Q, (b + 1) * BLOCK_Q)
            kw, vw = kb[b * BLOCK_Q:(b + 3) * BLOCK_Q], vb[b * BLOCK_Q:(b + 3) * BLOCK_Q]
            q = jnp.concatenate([r[rows, :] for r in q_refs], axis=0)
            s = lax.dot_general(q, kw, _NT, preferred_element_type=F32) * scale + bias_all
            if b == 0 or b == sb - 1:
                s = jnp.where(_in_sequence(i * sb + b, T), s, NEG_INF)
            m = jnp.maximum(jnp.max(s, axis=-1, keepdims=True), sk)
            p = jnp.exp(s - m)
            l = jnp.sum(p, axis=-1, keepdims=True) + jnp.exp(sk - m)
            o = (lax.dot_general(p.astype(BF16), vw, _NN, preferred_element_type=F32) / l).astype(BF16)
            lse = m + jnp.log(l)
            for g in range(GROUP):
                head = slice(g * BLOCK_Q, (g + 1) * BLOCK_Q)
                o_ref[rows, g * HEAD_DIM:(g + 1) * HEAD_DIM] = o[head]
                lse_ref[g, rows, :] = lse[head]

    first_group = o_all.shape[1] // (GROUP * HEAD_DIM) - n_kv
    return _pcall(
        body, deps, into=(o_all, 0), name="attn_b_fwd", grid=(n_kv, nblk // sb),
        in_specs=[*_head_specs(q_off, tq),
                  *_band_specs(lambda kv: q_off + n_q + kv, nblk, sb),
                  *_band_specs(lambda kv: q_off + n_q + n_kv + kv, nblk, sb),
                  pl.BlockSpec((GROUP, BLOCK_Q, 3 * BLOCK_Q), lambda kv, i: (kv, 0, 0)),
                  pl.BlockSpec(memory_space=pltpu.SMEM)],
        out_specs=[pl.BlockSpec((tq, GROUP * HEAD_DIM), lambda kv, i: (i, first_group + kv)),
                   pl.BlockSpec((GROUP, tq, 1), lambda kv, i: (kv, i, 0))],
        out_shape=[jax.ShapeDtypeStruct(o_all.shape, BF16), jax.ShapeDtypeStruct((n_q, T, 1), F32)],
        compiler_params=_params(("parallel", "parallel")),
    )(*([pb] * (GROUP + 6)), bias, sink)


def _attn_b_bwd(pb, o_cat, d_o, lse, bias, sink, q_off, n_q, n_kv, o_off, deps=(), sb=16):
    T = pb.shape[0]
    nblk = T // BLOCK_Q
    sb = min(sb, nblk)
    tq = sb * BLOCK_Q
    scale = HEAD_DIM ** -0.5

    def body(*refs):
        q_refs = refs[0:GROUP]
        k_refs, v_refs = refs[GROUP:GROUP + 3], refs[GROUP + 3:GROUP + 6]
        o_refs, do_refs = refs[GROUP + 6:2 * GROUP + 6], refs[2 * GROUP + 6:3 * GROUP + 6]
        lse_ref, bias_ref, sink_ref, dq_ref, dk_ref, dv_ref, dbias_ref, dsink_ref, dkb_ref, dvb_ref = refs[3 * GROUP + 6:]
        kv, i = pl.program_id(0), pl.program_id(1)
        first = i == 0

        @pl.when(first)
        def _():
            dk_ref[...] = jnp.zeros(dk_ref.shape, F32)
            dv_ref[...] = jnp.zeros(dv_ref.shape, F32)
            dbias_ref[...] = jnp.zeros(dbias_ref.shape, F32)

        kb = jnp.concatenate([r[...] for r in k_refs], axis=0)
        vb = jnp.concatenate([r[...] for r in v_refs], axis=0)
        dkb_ref[...] = jnp.zeros(dkb_ref.shape, F32)
        dvb_ref[...] = jnp.zeros(dvb_ref.shape, F32)
        row = lax.broadcasted_iota(jnp.int32, (SUBLANES, LANES), 0)
        dsink = jnp.zeros((SUBLANES, LANES), F32)
        bias_all = bias_ref[...].reshape(GROUP * BLOCK_Q, 3 * BLOCK_Q)
        sk = _per_head_rows([sink_ref[kv * GROUP + g] for g in range(GROUP)])
        for b in range(sb):
            rows = slice(b * BLOCK_Q, (b + 1) * BLOCK_Q)
            win = slice(b * BLOCK_Q, (b + 3) * BLOCK_Q)
            kw, vw = kb[win], vb[win]
            q = jnp.concatenate([r[rows, :] for r in q_refs], axis=0)
            do = jnp.concatenate([r[rows, :] for r in do_refs], axis=0)
            o = jnp.concatenate([r[rows, :] for r in o_refs], axis=0)
            lse = jnp.concatenate([lse_ref[g, rows, :] for g in range(GROUP)], axis=0)
            delta = jnp.sum(do.astype(F32) * o.astype(F32), axis=-1, keepdims=True)
            s = lax.dot_general(q, kw, _NT, preferred_element_type=F32) * scale + bias_all
            if b == 0 or b == sb - 1:
                s = jnp.where(_in_sequence(i * sb + b, T), s, NEG_INF)
            p = jnp.exp(s - lse)
            dp = lax.dot_general(do, vw, _NT, preferred_element_type=F32)
            ds = p * (dp - delta)
            dbias_ref[...] += ds.reshape(GROUP, BLOCK_Q, 3 * BLOCK_Q)
            sunk = jnp.exp(sk - lse) * delta
            for g in range(GROUP):
                dsink = dsink + jnp.where(row == g, -jnp.sum(sunk[g * BLOCK_Q:(g + 1) * BLOCK_Q]), 0.0)
            dsb = (ds * scale).astype(BF16)
            dq = lax.dot_general(dsb, kw, _NN, preferred_element_type=F32).astype(BF16)
            for g in range(GROUP):
                dq_ref[rows, g * HEAD_DIM:(g + 1) * HEAD_DIM] = dq[g * BLOCK_Q:(g + 1) * BLOCK_Q]
            dkb_ref[win, :] += lax.dot_general(dsb, q, _TN, preferred_element_type=F32)
            dvb_ref[win, :] += lax.dot_general(p.astype(BF16), do, _TN, preferred_element_type=F32)
        _accumulate(dsink_ref, dsink, first)

        before = pl.ds(pl.multiple_of(jnp.maximum(sb * i - 1, 0) * BLOCK_Q, BLOCK_Q), BLOCK_Q)
        own = pl.ds(pl.multiple_of(i * tq, BLOCK_Q), tq)
        after = pl.ds(pl.multiple_of(jnp.minimum(sb * i + sb, nblk - 1) * BLOCK_Q, BLOCK_Q), BLOCK_Q)
        for acc_ref, band_ref in ((dk_ref, dkb_ref), (dv_ref, dvb_ref)):
            acc_ref[before, :] += band_ref[0:BLOCK_Q, :]
            acc_ref[own, :] += band_ref[BLOCK_Q:BLOCK_Q + tq, :]
            acc_ref[after, :] += band_ref[BLOCK_Q + tq:, :]

    return _pcall(
        body, deps, name="attn_b_bwd", grid=(n_kv, nblk // sb),
        in_specs=[*_head_specs(q_off, tq),
                  *_band_specs(lambda kv: q_off + n_q + kv, nblk, sb),
                  *_band_specs(lambda kv: q_off + n_q + n_kv + kv, nblk, sb),
                  *_head_specs(o_off, tq), *_head_specs(o_off, tq),
                  pl.BlockSpec((GROUP, tq, 1), lambda kv, i: (kv, i, 0)),
                  pl.BlockSpec((GROUP, BLOCK_Q, 3 * BLOCK_Q), lambda kv, i: (kv, 0, 0)),
                  pl.BlockSpec(memory_space=pltpu.SMEM)],
        out_specs=[pl.BlockSpec((tq, GROUP * HEAD_DIM), lambda kv, i: (i, kv)),
                   pl.BlockSpec((T, HEAD_DIM), lambda kv, i: (0, kv)),
                   pl.BlockSpec((T, HEAD_DIM), lambda kv, i: (0, kv)),
                   pl.BlockSpec((GROUP, BLOCK_Q, 3 * BLOCK_Q), lambda kv, i: (kv, 0, 0)),
                   pl.BlockSpec((None, SUBLANES, LANES), lambda kv, i: (kv, 0, 0))],
        out_shape=[jax.ShapeDtypeStruct((T, n_q * HEAD_DIM), BF16),
                   jax.ShapeDtypeStruct((T, n_kv * HEAD_DIM), F32),
                   jax.ShapeDtypeStruct((T, n_kv * HEAD_DIM), F32),
                   jax.ShapeDtypeStruct((n_q, BLOCK_Q, 3 * BLOCK_Q), F32),
                   jax.ShapeDtypeStruct((n_kv, SUBLANES, LANES), F32)],
        scratch_shapes=[pltpu.VMEM((tq + 2 * BLOCK_Q, HEAD_DIM), F32), pltpu.VMEM((tq + 2 * BLOCK_Q, HEAD_DIM), F32)],
        compiler_params=_params(("parallel", "arbitrary")),
    )(*([pb] * (GROUP + 6)), *([o_cat] * GROUP), *([d_o] * GROUP), lse, bias, sink)


def _table_grads(dbias, dsink_raw, idx):
    n_heads = dbias.shape[0]
    n_kv = dsink_raw.shape[0]

    def body(db_ref, ds_ref, idx_ref, dt_ref, dsk_ref):
        iv = idx_ref[...]
        row = lax.broadcasted_iota(jnp.int32, (SUBLANES, LANES), 0)
        lane = lax.broadcasted_iota(jnp.int32, (SUBLANES, LANES), 1)
        dsk = jnp.zeros((SUBLANES, LANES), F32)
        for h in range(n_heads):
            d = db_ref[h]
            acc = jnp.zeros((SUBLANES, LANES), F32)
            for b in range(N_BUCKETS):
                acc = jnp.where((row == 0) & (lane == b), jnp.sum(jnp.where(iv == b, d, 0.0)), acc)
            dt_ref[:, h * LANES:(h + 1) * LANES] = acc
            raw = ds_ref[h // GROUP]
            val = jnp.sum(jnp.where((row == h % GROUP) & (lane == 0), raw, 0.0))
            dsk = jnp.where((row == 0) & (lane == h), val, dsk)
        dsk_ref[...] = dsk

    return pl.pallas_call(
        body, name="table_grads",
        in_specs=[pl.BlockSpec(memory_space=pltpu.VMEM)] * 3,
        out_specs=[pl.BlockSpec(memory_space=pltpu.VMEM)] * 2,
        out_shape=[jax.ShapeDtypeStruct((SUBLANES, n_heads * LANES), F32),
                   jax.ShapeDtypeStruct((SUBLANES, LANES), F32)],
        compiler_params=_params(),
    )(dbias, dsink_raw, idx)


def _position():
    x, y, c = lax.axis_index("x"), lax.axis_index("y"), lax.axis_index("c")
    return x, y, c


def _hbm(a):
    return pltpu.with_memory_space_constraint(a, pltpu.HBM)


def _split_start(name, bufs, sem_shapes, issue):
    nb, ns = len(bufs), len(sem_shapes)

    def body(*refs):
        buf_refs = refs[:nb]
        sems = refs[nb:nb + ns]
        token = refs[nb + ns + nb]
        issue(buf_refs, sems)
        token[...] = jnp.zeros(token.shape, F32)

    outs = pl.pallas_call(
        body, name=name,
        in_specs=[_HBM] * nb,
        out_specs=[_SEM] * ns + [_HBM] * nb + [_VMEM],
        out_shape=[pltpu.SemaphoreType.DMA(s) for s in sem_shapes] + [pltpu.HBM(b.shape, b.dtype) for b in bufs]
        + [jax.ShapeDtypeStruct((SUBLANES, LANES), F32)],
        input_output_aliases={i: ns + i for i in range(nb)},
        compiler_params=pltpu.CompilerParams(has_side_effects=_EFFECT),
    )(*[_hbm(b) for b in bufs])
    return outs[:ns], outs[ns:ns + nb], outs[-1]


def _split_wait(name, bufs, send, recv, counts, size_of, after):
    nb = len(bufs)

    def body(*refs):
        buf_refs = refs[:nb]
        send_ref, recv_ref = refs[nb], refs[nb + 1]
        x, y, c = _position()
        for w, n in enumerate(counts):
            ref = size_of(buf_refs, w)
            for k in range(n):
                s = sum(counts[:w]) + k
                cp = pltpu.make_async_remote_copy(
                    src_ref=ref, dst_ref=ref, send_sem=send_ref.at[s], recv_sem=recv_ref.at[s],
                    device_id=(x, y, c), device_id_type=MESH)
                cp.wait_send()
                cp.wait_recv()

    return pl.pallas_call(
        body, name=name,
        in_specs=[_HBM] * nb + [_SEM, _SEM, _ANY],
        out_specs=[_HBM] * nb,
        out_shape=[pltpu.HBM(b.shape, b.dtype) for b in bufs],
        input_output_aliases={i: i for i in range(nb)},
        compiler_params=pltpu.CompilerParams(has_side_effects=_EFFECT),
    )(*bufs, send, recv, after)


def _block_of(pos):
    return 4 * pos[0] + 2 * pos[1] + pos[2]


def _shard_of(ref, blk, by_cols):
    aligned = (lambda v, a: v) if isinstance(blk, int) else pl.multiple_of
    if by_cols:
        n = ref.shape[1] // N_DEV
        return ref.at[:, pl.ds(aligned(blk * n, LANES), n)]
    r = ref.shape[0] // N_DEV
    return ref.at[pl.ds(aligned(blk * r, SUBLANES), r), :]


def _place_own(name, land, shard, by_cols, tr=256):
    r, n = shard.shape
    tr = _tile(r, tr)
    mine = _block_of(_position()).astype(jnp.int32).reshape(1)

    def body(m_ref, land_ref, s_ref, o_ref):
        o_ref[...] = s_ref[...]

    if by_cols:
        out = pl.BlockSpec((tr, n), lambda i, m_ref: (i, m_ref[0]))
    else:
        out = pl.BlockSpec((tr, n), lambda i, m_ref: (m_ref[0] * (r // tr) + i, 0))
    return pl.pallas_call(
        body, name=name,
        grid_spec=pltpu.PrefetchScalarGridSpec(
            num_scalar_prefetch=1, grid=(r // tr,),
            in_specs=[_ANY, pl.BlockSpec((tr, n), lambda i, m_ref: (i, 0))], out_specs=out),
        out_shape=jax.ShapeDtypeStruct(land.shape, land.dtype),
        input_output_aliases={1: 0},
        compiler_params=_params(("parallel",)),
    )(mine, land, shard)


def _gather_start(name, shards, by_cols, groups, after=None):
    nw = len(shards)
    lands = [lax.empty((s.shape[0], s.shape[1] * N_DEV) if cols else (s.shape[0] * N_DEV, s.shape[1]), s.dtype)
             for s, cols in zip(shards, by_cols)]
    order = [] if after is None else [after]

    def issue(bufs, sems):
        x, y, c = _position()
        peers = [(x, y, 1 - c), (1 - x, y, c), (x, 1 - y, c), (1 - x, 1 - y, c)]
        for gi, grp in enumerate(groups):
            for wi, w in enumerate(grp):
                for k, peer in enumerate(peers):
                    pltpu.make_async_remote_copy(
                        src_ref=bufs[w], dst_ref=_shard_of(bufs[nw + w], _block_of((x, y, c)), by_cols[w]),
                        send_sem=sems[2 * gi].at[4 * wi + k], recv_sem=sems[2 * gi + 1].at[4 * wi + k],
                        device_id=peer, device_id_type=MESH).start()

    sem_shapes = [(4 * len(g),) for g in groups for _ in range(2)]
    sems, thru, token = _split_start(name, list(shards) + lands + order, sem_shapes, issue)
    return sems, thru[:nw], thru[nw:2 * nw], token


def _gather_forward(name, lands, by_cols):
    nw = len(lands)

    def issue(land, sems):
        x, y, c = _position()
        for w in range(nw):
            for k, chip in enumerate([(1 - x, y), (x, 1 - y), (1 - x, 1 - y)]):
                blk = _shard_of(land[w], _block_of((*chip, c)), by_cols[w])
                pltpu.make_async_remote_copy(
                    src_ref=blk, dst_ref=blk, send_sem=sems[0].at[3 * w + k], recv_sem=sems[1].at[3 * w + k],
                    device_id=(x, y, 1 - c), device_id_type=MESH).start()

    return _split_start(name, lands, [(3 * nw,), (3 * nw,)], issue)


def _first_block(bufs, w, offset=0):
    return bufs[offset + w].at[0]


_PEER_FLIPS = ((0, 0, 1), (1, 0, 0), (1, 0, 1), (0, 1, 0), (0, 1, 1), (1, 1, 0), (1, 1, 1))


def _scatter_start(name, grads, by_cols):
    nw = len(grads)
    lands = []
    for g, cols in zip(grads, by_cols):
        shard = (g.shape[0], g.shape[1] // N_DEV) if cols else (g.shape[0] // N_DEV, g.shape[1])
        lands.append(lax.empty((N_DEV,) + shard, g.dtype))

    def issue(bufs, sems):
        x, y, c = _position()
        flip = lambda v, f: 1 - v if f else v
        for w in range(nw):
            for k, (fx, fy, fc) in enumerate(_PEER_FLIPS):
                peer = (flip(x, fx), flip(y, fy), flip(c, fc))
                pltpu.make_async_remote_copy(
                    src_ref=_shard_of(bufs[w], _block_of(peer), by_cols[w]), dst_ref=bufs[nw + w].at[_block_of((x, y, c))],
                    send_sem=sems[0].at[7 * w + k], recv_sem=sems[1].at[7 * w + k],
                    device_id=peer, device_id_type=MESH).start()

    return _split_start(name, list(grads) + lands, [(7 * nw,), (7 * nw,)], issue)


def _adam(w, g, m, v):
    m = ADAM_B1 * m + (1.0 - ADAM_B1) * g
    v = ADAM_B2 * v + (1.0 - ADAM_B2) * (g * g)
    m_hat = m / (1.0 - ADAM_B1 ** ADAM_STEP)
    v_hat = v / (1.0 - ADAM_B2 ** ADAM_STEP)
    delta = -ADAM_LR * (m_hat / (jnp.sqrt(v_hat) + ADAM_EPS) + ADAM_WD * w)
    return delta, m, v


def _sum_adam(name, landed, grad, by_cols, w, m, v, tr=256):
    R, C = w.shape
    tr = _tile(R, tr if C > 1024 else 2 * tr)
    mine = _block_of(_position()).astype(jnp.int32).reshape(1)

    def body(me_ref, l_ref, own_ref, w_ref, m_ref, v_ref, g_ref, d_ref, nm_ref, nv_ref):
        own = own_ref[...].astype(F32)
        g = None
        for d in range(N_DEV):
            part = jnp.where(me_ref[0] == d, own, l_ref[d].astype(F32))
            g = part if g is None else g + part
        g_ref[...] = g
        d_ref[...], nm_ref[...], nv_ref[...] = _adam(w_ref[...], g, m_ref[...], v_ref[...])

    tile = pl.BlockSpec((tr, C), lambda i, me_ref: (i, 0))
    if by_cols:
        own = pl.BlockSpec((tr, C), lambda i, me_ref: (i, me_ref[0]))
    else:
        own = pl.BlockSpec((tr, C), lambda i, me_ref: (me_ref[0] * (R // tr) + i, 0))
    return pl.pallas_call(
        body, name=name,
        grid_spec=pltpu.PrefetchScalarGridSpec(
            num_scalar_prefetch=1, grid=(R // tr,),
            in_specs=[pl.BlockSpec((N_DEV, tr, C), lambda i, me_ref: (0, i, 0)), own, tile, tile, tile],
            out_specs=[tile] * 4),
        out_shape=[jax.ShapeDtypeStruct((R, C), F32)] * 4,
        compiler_params=_params(("parallel",)),
    )(mine, landed, grad, w, m, v)


def _small_all_reduce(parts, deps=()):
    W = parts.shape[1]

    def body(p_ref, o_ref, slots, send_sems, recv_sems):
        x, y, c = _position()
        me = 4 * x + 2 * y + c
        slots[me] = jnp.sum(p_ref[...], axis=0, keepdims=True)
        peers = [(x, y, 1 - c), (1 - x, y, c), (1 - x, y, 1 - c), (x, 1 - y, c), (x, 1 - y, 1 - c),
                 (1 - x, 1 - y, c), (1 - x, 1 - y, 1 - c)]
        copies = []
        for k, peer in enumerate(peers):
            cp = pltpu.make_async_remote_copy(
                src_ref=slots.at[me], dst_ref=slots.at[me], send_sem=send_sems.at[k], recv_sem=recv_sems.at[k],
                device_id=peer, device_id_type=MESH)
            cp.start()
            copies.append(cp)
        for cp in copies:
            cp.wait()
        total = slots[0]
        for d in range(1, N_DEV):
            total = total + slots[d]
        o_ref[...] = total

    return _pcall(
        body, deps, name="small_all_reduce",
        in_specs=[pl.BlockSpec(memory_space=pltpu.VMEM)], out_specs=pl.BlockSpec(memory_space=pltpu.VMEM),
        out_shape=jax.ShapeDtypeStruct((1, W), F32),
        scratch_shapes=[pltpu.VMEM((N_DEV, 1, W), F32), pltpu.SemaphoreType.DMA((7,)), pltpu.SemaphoreType.DMA((7,))],
    )(parts)


def _adam_small(w, g, m, v):
    def body(w_ref, g_ref, m_ref, v_ref, d_ref, nm_ref, nv_ref):
        d_ref[...], nm_ref[...], nv_ref[...] = _adam(w_ref[...], g_ref[...], m_ref[...], v_ref[...])

    return pl.pallas_call(
        body, name="adam_small",
        in_specs=[pl.BlockSpec(memory_space=pltpu.VMEM)] * 4, out_specs=[pl.BlockSpec(memory_space=pltpu.VMEM)] * 3,
        out_shape=[jax.ShapeDtypeStruct(w.shape, F32)] * 3,
    )(w, g, m, v)


_GATHER_GROUPS = (("w_in",), ("w_out", "w_up", "ple_w"), ("w_down", "w_gate"))
_COL_SHARDED = ("w_in", "w_up", "ple_w")


class _MeshComm:
    def __init__(self, w, mom, var):
        self.w, self.mom, self.var = w, mom, var
        self.out = {}
        self._scatters = {}

    def gather_begin(self):
        self._groups = {}
        token = None
        for tag, first, group_list in (("gather_start0", 0, _GATHER_GROUPS[:1]), ("gather_start1", 1, _GATHER_GROUPS[1:])):
            names = [n for g in group_list for n in g]
            idx = {n: i for i, n in enumerate(names)}
            by_cols = [n in _COL_SHARDED for n in names]
            sems, src, lands, token = _gather_start(tag, [self.w[n].astype(BF16) for n in names], by_cols,
                                                    [[idx[n] for n in g] for g in group_list], token)
            lands = [_place_own("place_" + n, land, s, cols) for n, land, s, cols in zip(names, lands, src, by_cols)]
            for k, g in enumerate(group_list):
                self._groups[first + k] = (sems[2 * k], sems[2 * k + 1], [src[idx[n]] for n in g],
                                           [lands[idx[n]] for n in g])
        return token

    @staticmethod
    def _shard_size(names, offset):
        return lambda bufs, w: _shard_of(bufs[offset + w], 0, names[w] in _COL_SHARDED)

    def gather_arrive(self, gi, after):
        names = _GATHER_GROUPS[gi]
        send, recv, src, lands = self._groups[gi]
        out = _split_wait("gather_arrive%d" % gi, src + lands, send, recv, [4] * len(names),
                          self._shard_size(names, len(names)), after)
        self._arrived = out[len(names):]

    def gather_forward(self, gi):
        by_cols = [n in _COL_SHARDED for n in _GATHER_GROUPS[gi]]
        self._fsems, self._fthru, token = _gather_forward("gather_forward%d" % gi, self._arrived, by_cols)
        return token

    def gather_finish(self, gi, after):
        names = _GATHER_GROUPS[gi]
        out = _split_wait("gather_finish%d" % gi, self._fthru, self._fsems[0], self._fsems[1], [3] * len(names),
                          self._shard_size(names, 0), after)
        return dict(zip(names, out))

    def reduce_begin(self, key, grads):
        names = list(grads)
        sems, thru, token = _scatter_start("scatter_start_" + key, [grads[n] for n in names],
                                           [n in _COL_SHARDED for n in names])
        self._scatters[key] = (names, sems, thru)
        return token

    def reduce_finish(self, key, after):
        names, sems, thru = self._scatters[key]
        nw = len(names)
        out = _split_wait("scatter_wait_" + key, thru, sems[0], sems[1], [N_DEV - 1] * nw,
                          functools.partial(_first_block, offset=nw), after)
        for i, n in enumerate(names):
            self.out[n] = _sum_adam("adam_" + n, out[nw + i], out[i], n in _COL_SHARDED, self.w[n], self.mom[n],
                                    self.var[n])


def _step(x, p, target, gains, comm):
    T, D = x.shape
    n_q = D // (2 * HEAD_DIM)
    n_kv = n_q // GROUP
    cos, sin = _rope_tables(T)
    idx = _bucket_index()

    t = comm.gather_begin()
    u = _rms_fwd("norm_attn", x, gains["attn_norm_g"], deps=(t,))
    comm.gather_arrive(0, u)
    t = comm.gather_forward(0)
    bias = _bias_build(idx, gains["rel_bias_table"].reshape(-1), n_q, deps=(t,))
    full = comm.gather_finish(0, bias)
    proj_a, pb = _in_proj(u, full["w_in"], cos, sin, gains["q_norm_g"], gains["k_norm_g"], n_q + n_kv)
    o_a, lse_a = _attn_a_fwd(pb, n_q, n_kv, 2 * n_q)
    comm.gather_arrive(1, lse_a)
    t = comm.gather_forward(1)
    sink = gains["sink_logits"].reshape(-1)
    b_off = n_q + 2 * n_kv
    o_cat, lse_b = _attn_b_fwd(pb, bias, sink, o_a, b_off, n_q, n_kv, deps=(t,))
    full.update(comm.gather_finish(1, lse_b))
    h1, m_in = _mm_nn_rms("out_proj", o_cat, full["w_out"], x, gains["mlp_norm_g"])

    def up_epilogue(acc, extra, outs):
        outs[0][...] = acc.astype(BF16)
        r = jnp.maximum(acc, 0.0)
        outs[1][...] = (r * r).astype(BF16)

    a_act, f_act = _mm_nn("up_proj", m_in, full["w_up"], epilogue=up_epilogue, out_dtypes=[BF16, BF16], tn=2048)
    comm.gather_arrive(2, f_act)
    t = comm.gather_forward(2)
    p_b = p.astype(BF16)
    pe = _mm_nn("ple_proj", p_b, full["ple_w"], deps=(t,))
    full.update(comm.gather_finish(2, pe))
    h2 = _mm_nn("down_proj", f_act, full["w_down"], epilogue=_store_add, extras=(h1,), tn=512)
    gn = _rms_fwd("norm_gate", h2, gains["gate_norm_g"])

    dh3, dz, dpe, dg_final, dg_ple, loss_part = _gate_tail(gn, full["w_gate"], h2, pe, target, gains["ple_norm_g"],
                                                           gains["final_norm_g"])
    gw_gate = _mm_tn("grad_w_gate", gn, dz, tn=1024)
    gw_ple = _mm_tn("grad_ple_w", p_b, dpe)
    dh2, dh2_b, dg_gate = _mm_nt_rms_bwd("d_gate_in", dz, full["w_gate"], h2, gains["gate_norm_g"], dh3, tm=512)
    gw_down = _mm_tn("grad_w_down", f_act, dh2_b, tn=1024)
    t = comm.reduce_begin("b", dict(w_gate=gw_gate, ple_w=gw_ple, w_down=gw_down))

    def act_bwd(acc, extra, outs):
        outs[0][...] = (acc * (2.0 * jnp.maximum(extra[0][...].astype(F32), 0.0))).astype(BF16)

    da = _mm_nt("d_act", dh2_b, full["w_down"], out_dtype=BF16, epilogue=act_bwd, extras=(a_act,), tn=2048, deps=(t,))
    gw_up = _mm_tn("grad_w_up", m_in, da, tn=1024)
    dm = _mm_nt("d_mlp_in", da, full["w_up"], out_dtype=BF16, tn=512)
    dh1, dh1_b, dg_mlp = _rms_bwd("norm_mlp_bwd", dm, h1, gains["mlp_norm_g"], dh2)
    gw_out = _mm_tn("grad_w_out", o_cat, dh1_b, tn=1024)
    t = comm.reduce_begin("d", dict(w_up=gw_up, w_out=gw_out))
    d_o = _mm_nt("d_attn_out", dh1_b, full["w_out"], out_dtype=BF16, deps=(t,))
    dqa, dka_t, dva_t = _attn_a_bwd(pb, o_cat, d_o, lse_a, n_q, n_kv)
    dqb, dkb, dvb, dbias, dsink_raw = _attn_b_bwd(pb, o_cat, d_o, lse_b, bias, sink, b_off, n_q, n_kv, n_q)
    dtable, dsink = _table_grads(dbias, dsink_raw, idx)
    dproj, dg_q, dg_k = _dproj(proj_a, dqa, dka_t, dva_t, dqb, dkb, dvb, cos, sin, gains["q_norm_g"], gains["k_norm_g"])
    gw_in = _mm_tn("grad_w_in", u, dproj, tn=1024)
    t = comm.reduce_begin("e", dict(w_in=gw_in))
    dx, dg_attn = _mm_nt_rms_bwd("d_attn_in", dproj, full["w_in"], x, gains["attn_norm_g"], dh1, with_bf16=False,
                                 tm=512, deps=(t,))
    for key in "bd":
        comm.reduce_finish(key, dx)

    parts = jnp.concatenate([dg_attn, dg_mlp, dg_ple, dg_gate, dg_final, dg_q, dg_k, dtable, dsink, loss_part], axis=1)
    return dx, parts


_SHARDED = ("w_in", "w_out", "w_up", "w_down", "ple_w", "w_gate")
_VECTORS = ("attn_norm_g", "mlp_norm_g", "ple_norm_g", "gate_norm_g", "final_norm_g")
_ORDER = ("attn_norm_g", "w_in", "q_norm_g", "k_norm_g", "sink_logits", "w_out", "mlp_norm_g", "w_up", "w_down",
          "ple_w", "ple_norm_g", "gate_norm_g", "w_gate", "rel_bias_table", "final_norm_g")


def _pack_small(vals, n_heads):
    lane_pad = lambda v: jnp.pad(v, ((0, 0), (0, LANES - v.shape[1])))
    table = lane_pad(vals["rel_bias_table"].T).reshape(1, n_heads * LANES)
    return jnp.concatenate(
        [vals[n].reshape(1, -1) for n in _VECTORS] + [vals["q_norm_g"], vals["k_norm_g"], table,
                                                      lane_pad(vals["sink_logits"]), jnp.zeros((1, LANES), F32)], axis=1)


def _unpack_small(row, like, n_heads):
    out, off = {}, 0
    for n in _VECTORS:
        out[n] = row[:, off:off + like[n].size].reshape(like[n].shape)
        off += like[n].size
    for n in ("q_norm_g", "k_norm_g"):
        out[n] = row[:, off:off + LANES]
        off += LANES
    out["rel_bias_table"] = row[:, off:off + n_heads * LANES].reshape(n_heads, LANES)[:, :N_BUCKETS].T
    off += n_heads * LANES
    out["sink_logits"] = row[:, off:off + n_heads]
    off += LANES
    return out, row[0, off]


def kernel(x, p, attn_norm_g, w_in, q_norm_g, k_norm_g, sink_logits, w_out, mlp_norm_g, w_up, w_down, ple_w, ple_norm_g, gate_norm_g, w_gate, rel_bias_table, final_norm_g, loss_target, m_attn_norm_g, m_w_in, m_q_norm_g, m_k_norm_g, m_sink_logits, m_w_out, m_mlp_norm_g, m_w_up, m_w_down, m_ple_w, m_ple_norm_g, m_gate_norm_g, m_w_gate, m_rel_bias_table, m_final_norm_g, v_attn_norm_g, v_w_in, v_q_norm_g, v_k_norm_g, v_sink_logits, v_w_out, v_mlp_norm_g, v_w_up, v_w_down, v_ple_w, v_ple_norm_g, v_gate_norm_g, v_w_gate, v_rel_bias_table, v_final_norm_g):
    w = dict(attn_norm_g=attn_norm_g, w_in=w_in[0], q_norm_g=q_norm_g, k_norm_g=k_norm_g, sink_logits=sink_logits,
             w_out=w_out[0], mlp_norm_g=mlp_norm_g, w_up=w_up[0], w_down=w_down[0], ple_w=ple_w[0],
             ple_norm_g=ple_norm_g, gate_norm_g=gate_norm_g, w_gate=w_gate[0], rel_bias_table=rel_bias_table,
             final_norm_g=final_norm_g)
    mom = dict(attn_norm_g=m_attn_norm_g, w_in=m_w_in[0], q_norm_g=m_q_norm_g, k_norm_g=m_k_norm_g,
               sink_logits=m_sink_logits, w_out=m_w_out[0], mlp_norm_g=m_mlp_norm_g, w_up=m_w_up[0],
               w_down=m_w_down[0], ple_w=m_ple_w[0], ple_norm_g=m_ple_norm_g, gate_norm_g=m_gate_norm_g,
               w_gate=m_w_gate[0], rel_bias_table=m_rel_bias_table, final_norm_g=m_final_norm_g)
    var = dict(attn_norm_g=v_attn_norm_g, w_in=v_w_in[0], q_norm_g=v_q_norm_g, k_norm_g=v_k_norm_g,
               sink_logits=v_sink_logits, w_out=v_w_out[0], mlp_norm_g=v_mlp_norm_g, w_up=v_w_up[0],
               w_down=v_w_down[0], ple_w=v_ple_w[0], ple_norm_g=v_ple_norm_g, gate_norm_g=v_gate_norm_g,
               w_gate=v_w_gate[0], rel_bias_table=v_rel_bias_table, final_norm_g=v_final_norm_g)
    D = x.shape[-1]
    n_heads = D // (2 * HEAD_DIM)

    gains = {n: w[n] for n in w if n not in _SHARDED}
    gains["final_norm_g"] = final_norm_g.reshape(1, -1)

    comm = _MeshComm(w, mom, var)
    dx, parts = _step(x[0], p[0, 0], loss_target[0], gains, comm)

    small_g = _small_all_reduce(parts, deps=[comm.out[n][0] for n in comm.out])
    comm.reduce_finish("e", small_g)

    g_out, d_out, m_out, v_out = {}, {}, {}, {}
    for n in _SHARDED:
        g, d, nm, nv = comm.out[n]
        g_out[n], d_out[n], m_out[n], v_out[n] = g[None], d[None], nm[None], nv[None]

    small = {n: v for n, v in w.items() if n not in _SHARDED}
    pack = lambda vals: _pack_small({n: vals[n] for n in small}, n_heads)
    sd, sm, sv = _adam_small(pack(w), small_g, pack(mom), pack(var))
    sg, loss = _unpack_small(small_g, small, n_heads)
    g_out.update(sg)
    for dst, row in ((d_out, sd), (m_out, sm), (v_out, sv)):
        dst.update(_unpack_small(row, small, n_heads)[0])

    return (loss, dx[None], *[g_out[n] for n in _ORDER], *[d_out[n] for n in _ORDER],
            *[m_out[n] for n in _ORDER], *[v_out[n] for n in _ORDER])
```

```python
import functools
import math

import numpy as np
import jax
import jax.numpy as jnp
from jax import lax
from jax.experimental import pallas as pl
from jax.experimental.pallas import tpu as pltpu

F32 = jnp.float32
BF16 = jnp.bfloat16

N_DEV = 8
N_CHIP = 4
HEAD_DIM = 128
GROUP = 4
GRID_W = 64
WINDOW = 128
BLOCK_Q = 128
N_BUCKETS = 32
MAX_DISTANCE = 128
ROPE_THETA = 10000.0
EPS = 1e-6
NEG_INF = -1e30
ADAM_LR = 0.001
ADAM_B1 = 0.9
ADAM_B2 = 0.999
ADAM_EPS = 1e-08
ADAM_WD = 0.01
ADAM_STEP = 10
LOG2E = math.log2(math.e)
LANES = 128
SUBLANES = 8
VMEM_LIMIT_BYTES = 60 * 1024 * 1024
MESH = pl.DeviceIdType.MESH

_NT = (((1,), (1,)), ((), ()))
_NN = (((1,), (0,)), ((), ()))
_TN = (((0,), (0,)), ((), ()))


def _tile(dim, pref):
    return pref if dim % pref == 0 else dim


def _params(sem=None):
    return pltpu.CompilerParams(dimension_semantics=sem, vmem_limit_bytes=VMEM_LIMIT_BYTES)


_HBM = pl.BlockSpec(memory_space=pltpu.HBM)
_SEM = pl.BlockSpec(memory_space=pltpu.SEMAPHORE)
_ANY = pl.BlockSpec(memory_space=pl.ANY)
_VMEM = pl.BlockSpec(memory_space=pltpu.VMEM)
_EFFECT = pltpu.SideEffectType.DATAFLOW_SIDE_EFFECTING


def _pcall(body, deps=(), *, in_specs, into=None, **kw):
    deps = [d for d in deps if d is not None]
    nd = len(deps)
    if into is not None:
        deps = [into[0]] + deps
        nd += 1
        kw["input_output_aliases"] = {0: into[1]}

    def wrapped(*refs):
        body(*refs[nd:])

    call = pl.pallas_call(wrapped, in_specs=[_ANY] * nd + list(in_specs), **kw)
    return lambda *args: call(*deps, *args)


def _mm(name, a, b, dims, grid, a_spec, b_spec, out_shape, out_specs, acc_shape, epilogue,
        extras=(), extra_specs=(), deps=(), semantics=("parallel", "parallel", "arbitrary")):
    nk = grid[2]
    n_extra = len(extras)

    def body(*refs):
        a_ref, b_ref = refs[0], refs[1]
        extra = refs[2:2 + n_extra]
        outs = refs[2 + n_extra:-1]
        acc = refs[-1]
        part = lax.dot_general(a_ref[...], b_ref[...], dims, preferred_element_type=F32)
        if nk == 1:
            epilogue(part, extra, outs)
        else:
            k = pl.program_id(2)

            @pl.when(k == 0)
            def _():
                acc[...] = part

            @pl.when(k > 0)
            def _():
                acc[...] += part

            @pl.when(k == nk - 1)
            def _():
                epilogue(acc[...], extra, outs)

    return _pcall(
        body, deps, name=name, grid=grid,
        in_specs=[a_spec, b_spec, *extra_specs],
        out_specs=out_specs, out_shape=out_shape,
        scratch_shapes=[pltpu.VMEM(acc_shape if nk > 1 else (SUBLANES, LANES), F32)],
        compiler_params=_params(semantics),
    )(a, b, *extras)


def _store(dtype):
    def ep(acc, extra, outs):
        outs[0][...] = acc.astype(dtype)
    return ep


def _store_add(acc, extra, outs):
    outs[0][...] = acc + extra[0][...]


def _mm_nn(name, a, b, out_dtype=F32, epilogue=None, extras=(), n_out=1, out_dtypes=None, tm=1024, tn=1024, tk=None,
           deps=()):
    M, K = a.shape
    N = b.shape[1]
    tm, tn, tk = _tile(M, tm), _tile(N, tn), _tile(K, tk or K)
    b_spec = pl.BlockSpec((tk, tn), lambda i, j, k: (k, j))
    grid = (M // tm, N // tn, K // tk)
    o_spec = pl.BlockSpec((tm, tn), lambda i, j, k: (i, j))
    out_dtypes = out_dtypes or [out_dtype] * n_out
    out_shape = [jax.ShapeDtypeStruct((M, N), d) for d in out_dtypes]
    res = _mm(name, a, b, _NN, grid, pl.BlockSpec((tm, tk), lambda i, j, k: (i, k)), b_spec,
              out_shape, [o_spec] * len(out_dtypes), (tm, tn), epilogue or _store(out_dtype),
              extras, [o_spec] * len(extras), deps)
    return res if len(out_dtypes) > 1 else res[0]


def _mm_nt(name, a, b, out_dtype=F32, epilogue=None, extras=(), tm=1024, tn=1024, tk=None, deps=()):
    M, C = a.shape
    N = b.shape[0]
    tm, tn, tk = _tile(M, tm), _tile(N, tn), _tile(C, tk or C)
    b_spec = pl.BlockSpec((tn, tk), lambda i, j, k: (j, k))
    grid = (M // tm, N // tn, C // tk)
    o_spec = pl.BlockSpec((tm, tn), lambda i, j, k: (i, j))
    return _mm(name, a, b, _NT, grid, pl.BlockSpec((tm, tk), lambda i, j, k: (i, k)), b_spec,
               [jax.ShapeDtypeStruct((M, N), out_dtype)], [o_spec], (tm, tn), epilogue or _store(out_dtype),
               extras, [o_spec] * len(extras), deps)[0]


def _mm_tn(name, a, b, out_dtype=BF16, tm=1024, tn=512, tk=None, deps=()):
    T, M = a.shape
    N = b.shape[1]
    tm, tn, tk = _tile(M, tm), _tile(N, tn), _tile(T, tk or T)
    out_shape = jax.ShapeDtypeStruct((M, N), out_dtype)
    o_spec = pl.BlockSpec((tm, tn), lambda i, j, k: (i, j))
    grid = (M // tm, N // tn, T // tk)
    return _mm(name, a, b, _TN, grid, pl.BlockSpec((tk, tm), lambda i, j, k: (k, i)),
               pl.BlockSpec((tk, tn), lambda i, j, k: (k, j)), [out_shape], [o_spec], (tm, tn), _store(out_dtype),
               deps=deps)[0]


def _mean_last(v):
    return jnp.mean(v, axis=-1, keepdims=True)


def _rows_to_sublanes(v):
    r, c = v.shape
    return jnp.sum(v.reshape(r // SUBLANES, SUBLANES, c), axis=0)


def _accumulate(ref, val, first):
    @pl.when(first)
    def _():
        ref[...] = val

    @pl.when(jnp.logical_not(first))
    def _():
        ref[...] += val


def _rms_fwd(name, x, g, tr=512, deps=()):
    T, D = x.shape
    tr = _tile(T, tr)

    def body(x_ref, g_ref, o_ref):
        xv = x_ref[...]
        r = lax.rsqrt(_mean_last(xv * xv) + EPS)
        o_ref[...] = (xv * r * g_ref[...]).astype(BF16)

    row = pl.BlockSpec((tr, D), lambda i: (i, 0))
    return _pcall(
        body, deps, name=name, grid=(T // tr,),
        in_specs=[row, pl.BlockSpec((1, D), lambda i: (0, 0))],
        out_specs=row, out_shape=jax.ShapeDtypeStruct((T, D), BF16),
        compiler_params=_params(("parallel",)),
    )(x, g)


def _rms_bwd(name, dyn, x, g, dres, tr=512, deps=()):
    T, D = x.shape
    tr = _tile(T, tr)

    def body(dy_ref, x_ref, g_ref, dr_ref, dx_ref, dxb_ref, dg_ref):
        xv = x_ref[...]
        r = lax.rsqrt(_mean_last(xv * xv) + EPS)
        xn = xv * r
        dy = dy_ref[...].astype(F32)
        dxn = dy * g_ref[...]
        dx = dr_ref[...] + r * (dxn - xn * _mean_last(dxn * xn))
        dx_ref[...] = dx
        dxb_ref[...] = dx.astype(BF16)
        _accumulate(dg_ref, _rows_to_sublanes(dy * xn), pl.program_id(0) == 0)

    row = pl.BlockSpec((tr, D), lambda i: (i, 0))
    return _pcall(
        body, deps, name=name, grid=(T // tr,),
        in_specs=[row, row, pl.BlockSpec((1, D), lambda i: (0, 0)), row],
        out_specs=[row, row, pl.BlockSpec((SUBLANES, D), lambda i: (0, 0))],
        out_shape=[jax.ShapeDtypeStruct((T, D), F32), jax.ShapeDtypeStruct((T, D), BF16),
                   jax.ShapeDtypeStruct((SUBLANES, D), F32)],
        compiler_params=_params(("arbitrary",)),
    )(dyn, x, g, dres)


def _mm_nn_rms(name, a, b, res, g, tm=512, deps=()):
    M, K = a.shape
    N = b.shape[1]
    tm = _tile(M, tm)

    def epilogue(acc, extra, outs):
        h = acc + extra[0][...]
        outs[0][...] = h
        outs[1][...] = (h * lax.rsqrt(_mean_last(h * h) + EPS) * extra[1][...]).astype(BF16)

    row = pl.BlockSpec((tm, N), lambda i, j, k: (i, 0))
    return _mm(name, a, b, _NN, (M // tm, 1, 1), pl.BlockSpec((tm, K), lambda i, j, k: (i, 0)),
               pl.BlockSpec((K, N), lambda i, j, k: (0, 0)),
               [jax.ShapeDtypeStruct((M, N), F32), jax.ShapeDtypeStruct((M, N), BF16)], [row, row], (tm, N), epilogue,
               (res, g), [row, pl.BlockSpec((1, N), lambda i, j, k: (0, 0))], deps)


def _mm_nt_rms_bwd(name, a, b, x, g, dres, with_bf16=True, tm=256, deps=()):
    M, C = a.shape
    N = b.shape[0]
    tm = _tile(M, tm)

    def epilogue(dy, extra, outs):
        x_ref, dr_ref, g_ref = extra
        xv = x_ref[...]
        r = lax.rsqrt(_mean_last(xv * xv) + EPS)
        xn = xv * r
        dxn = dy * g_ref[...]
        dx = dr_ref[...] + r * (dxn - xn * _mean_last(dxn * xn))
        outs[0][...] = dx
        if with_bf16:
            outs[1][...] = dx.astype(BF16)
        _accumulate(outs[-1], _rows_to_sublanes(dy * xn), pl.program_id(0) == 0)

    row = pl.BlockSpec((tm, N), lambda i, j, k: (i, 0))
    copies = [jax.ShapeDtypeStruct((M, N), F32)] + ([jax.ShapeDtypeStruct((M, N), BF16)] if with_bf16 else [])
    return _mm(name, a, b, _NT, (M // tm, 1, 1), pl.BlockSpec((tm, C), lambda i, j, k: (i, 0)),
               pl.BlockSpec((N, C), lambda i, j, k: (0, 0)),
               copies + [jax.ShapeDtypeStruct((SUBLANES, N), F32)],
               [row] * len(copies) + [pl.BlockSpec((SUBLANES, N), lambda i, j, k: (0, 0))], (tm, N), epilogue,
               (x, dres, g), [row, row, pl.BlockSpec((1, N), lambda i, j, k: (0, 0))], deps,
               semantics=("arbitrary", "arbitrary", "arbitrary"))


def _gate_tail(gn, w_gate, h2, pe, target, g_ple, g_final, tm=256):
    T, D = h2.shape
    tm = _tile(T, tm)

    def epilogue(z, extra, outs):
        h2_ref, pe_ref, t_ref, gp_ref, gf_ref = extra
        dh3_ref, dz_ref, dpe_ref, dgf_ref, dgp_ref, loss_ref = outs
        first = pl.program_id(0) == 0
        pev = pe_ref[...]
        r3 = lax.rsqrt(_mean_last(pev * pev) + EPS)
        en = pev * r3
        e = en * gp_ref[...]
        gate = 0.5 * jnp.tanh(0.5 * z) + 0.5
        h3 = h2_ref[...] + gate * e
        r5 = lax.rsqrt(_mean_last(h3 * h3) + EPS)
        hn = h3 * r5
        diff = hn * gf_ref[...] - t_ref[...]
        loss_rows = 0.5 * _mean_last(diff * diff)
        row0 = lax.broadcasted_iota(jnp.int32, (SUBLANES, LANES), 0) == 0
        _accumulate(loss_ref, jnp.where(row0, jnp.sum(loss_rows), 0.0), first)
        dy = diff * (1.0 / D)
        _accumulate(dgf_ref, _rows_to_sublanes(dy * hn), first)
        dhn = dy * gf_ref[...]
        dh3 = r5 * (dhn - hn * _mean_last(dhn * hn))
        dh3_ref[...] = dh3
        dgate = dh3 * e
        de = dh3 * gate
        dz_ref[...] = (dgate * gate * (1.0 - gate)).astype(BF16)
        _accumulate(dgp_ref, _rows_to_sublanes(de * en), first)
        den = de * gp_ref[...]
        dpe_ref[...] = (r3 * (den - en * _mean_last(den * en))).astype(BF16)

    row = pl.BlockSpec((tm, D), lambda i, j, k: (i, 0))
    vec = pl.BlockSpec((1, D), lambda i, j, k: (0, 0))
    part = pl.BlockSpec((SUBLANES, D), lambda i, j, k: (0, 0))
    return _mm("gate_tail", gn, w_gate, _NN, (T // tm, 1, 1), row, pl.BlockSpec(w_gate.shape, lambda i, j, k: (0, 0)),
               [jax.ShapeDtypeStruct((T, D), F32), jax.ShapeDtypeStruct((T, D), BF16),
                jax.ShapeDtypeStruct((T, D), BF16), jax.ShapeDtypeStruct((SUBLANES, D), F32),
                jax.ShapeDtypeStruct((SUBLANES, D), F32), jax.ShapeDtypeStruct((SUBLANES, LANES), F32)],
               [row, row, row, part, part, pl.BlockSpec((SUBLANES, LANES), lambda i, j, k: (0, 0))], (tm, D), epilogue,
               (h2, pe, target, g_ple, g_final), [row, row, row, vec, vec],
               semantics=("arbitrary", "arbitrary", "arbitrary"))


def _rope_tables(T):
    pos = np.arange(T)
    half = HEAD_DIM // 2
    inv = (ROPE_THETA ** (-np.arange(0, half, 2, dtype=np.float32) / half)).astype(np.float32)
    ang_r = (pos // GRID_W).astype(np.float32)[:, None] * inv
    ang_c = (pos % GRID_W).astype(np.float32)[:, None] * inv
    cos = np.concatenate([np.cos(ang_r), np.cos(ang_r), np.cos(ang_c), np.cos(ang_c)], axis=-1)
    sin = np.concatenate([-np.sin(ang_r), np.sin(ang_r), -np.sin(ang_c), np.sin(ang_c)], axis=-1)
    return jnp.asarray(cos, F32), jnp.asarray(sin, F32)


def _swap32(x):
    lane = lax.broadcasted_iota(jnp.int32, x.shape, 1)
    return jnp.where((lane % 64) < 32, pltpu.roll(x, 96, 1), pltpu.roll(x, 32, 1))


def _in_proj(u, w_in, cos, sin, g_q, g_k, n_norm, tm=1024):
    T, K = u.shape
    W = w_in.shape[1]
    tm = _tile(T, tm)
    n_q = n_norm * GROUP // (GROUP + 1)
    wa = n_norm * HEAD_DIM

    def epilogue(acc, extra, outs):
        c_ref, s_ref, gq_ref, gk_ref = extra
        raw_ref, o_ref = outs
        c, s = c_ref[...], s_ref[...]
        raw_ref[...] = acc[:, :wa]
        for h in range(n_norm):
            cols = slice(h * HEAD_DIM, (h + 1) * HEAD_DIM)
            xv = acc[:, cols]
            g = gq_ref[...] if h < n_q else gk_ref[...]
            xn = xv * lax.rsqrt(_mean_last(xv * xv) + EPS) * g
            o_ref[:, cols] = (xn * c + _swap32(xn) * s).astype(BF16)
        o_ref[:, wa:] = acc[:, wa:].astype(BF16)

    tab = pl.BlockSpec((tm, HEAD_DIM), lambda i, j, k: (i, 0))
    vec = pl.BlockSpec((1, HEAD_DIM), lambda i, j, k: (0, 0))
    return _mm("in_proj", u, w_in, _NN, (T // tm, 1, 1), pl.BlockSpec((tm, K), lambda i, j, k: (i, 0)),
               pl.BlockSpec((K, W), lambda i, j, k: (0, 0)),
               [jax.ShapeDtypeStruct((T, wa), F32), jax.ShapeDtypeStruct((T, W), BF16)],
               [pl.BlockSpec((tm, wa), lambda i, j, k: (i, 0)), pl.BlockSpec((tm, W), lambda i, j, k: (i, 0))],
               (tm, W), epilogue, (cos, sin, g_q, g_k), [tab, tab, vec, vec])


def _dproj(proj_a, dqa, dka_t, dva_t, dqb, dkb, dvb, cos, sin, g_q, g_k, tr=512):
    T, wa = proj_a.shape
    tr = _tile(T, tr)
    n_q = dqa.shape[1] // HEAD_DIM
    wkv = dka_t.shape[0]
    W = wa + wkv + dqb.shape[1] + dkb.shape[1] + dvb.shape[1]

    def body(p_ref, dqa_ref, dkat_ref, dvat_ref, dqb_ref, dkb_ref, dvb_ref, c_ref, s_ref, gq_ref, gk_ref,
             o_ref, dgq_ref, dgk_ref):
        c, s = c_ref[...], s_ref[...]
        dka = dkat_ref[...].T
        dgq = jnp.zeros((SUBLANES, HEAD_DIM), F32)
        dgk = jnp.zeros((SUBLANES, HEAD_DIM), F32)
        for h in range(wa // HEAD_DIM):
            cols = slice(h * HEAD_DIM, (h + 1) * HEAD_DIM)
            xv = p_ref[:, cols]
            r = lax.rsqrt(_mean_last(xv * xv) + EPS)
            xn = xv * r
            if h < n_q:
                d = dqa_ref[:, cols]
                g = gq_ref[...]
            else:
                d = dka[:, (h - n_q) * HEAD_DIM:(h - n_q + 1) * HEAD_DIM]
                g = gk_ref[...]
            dqn = d * c + _swap32(d * s)
            part = _rows_to_sublanes(dqn * xn)
            if h < n_q:
                dgq = dgq + part
            else:
                dgk = dgk + part
            dxn = dqn * g
            o_ref[:, cols] = (r * (dxn - xn * _mean_last(dxn * xn))).astype(BF16)
        o_ref[:, wa:wa + wkv] = dvat_ref[...].T.astype(BF16)
        off = wa + wkv
        for ref in (dqb_ref, dkb_ref, dvb_ref):
            w = ref.shape[1]
            o_ref[:, off:off + w] = ref[...].astype(BF16)
            off += w
        first = pl.program_id(0) == 0
        _accumulate(dgq_ref, dgq, first)
        _accumulate(dgk_ref, dgk, first)

    def row(w):
        return pl.BlockSpec((tr, w), lambda i: (i, 0))

    col = pl.BlockSpec((wkv, tr), lambda i: (0, i))
    vec = pl.BlockSpec((1, HEAD_DIM), lambda i: (0, 0))
    part = pl.BlockSpec((SUBLANES, HEAD_DIM), lambda i: (0, 0))
    return pl.pallas_call(
        body, name="dproj", grid=(T // tr,),
        in_specs=[row(wa), row(dqa.shape[1]), col, col, row(dqb.shape[1]),
                  row(dkb.shape[1]), row(dvb.shape[1]), row(HEAD_DIM), row(HEAD_DIM), vec, vec],
        out_specs=[row(W), part, part],
        out_shape=[jax.ShapeDtypeStruct((T, W), BF16), jax.ShapeDtypeStruct((SUBLANES, HEAD_DIM), F32),
                   jax.ShapeDtypeStruct((SUBLANES, HEAD_DIM), F32)],
        compiler_params=_params(("arbitrary",)),
    )(proj_a, dqa, dka_t, dva_t, dqb, dkb, dvb, cos, sin, g_q, g_k)


def _attn_a_fwd(pb, n_q, n_kv, out_heads, tq=1024, tc=1024):
    T = pb.shape[0]
    tq, tc = _tile(T, tq), _tile(T, tc)
    scale = HEAD_DIM ** -0.5
    c = scale * LOG2E

    def body(q_ref, k_ref, v_ref, o_ref, lse_ref):
        q = q_ref[...]
        m = l = acc = None
        for j in range(T // tc):
            keys = slice(j * tc, (j + 1) * tc)
            s = lax.dot_general(q, k_ref[keys, :], _NT, preferred_element_type=F32)
            mj = jnp.max(s, axis=-1, keepdims=True)
            m_new = mj if j == 0 else jnp.maximum(m, mj)
            p = jnp.exp2((s - m_new) * c)
            pv = lax.dot_general(p.astype(BF16), v_ref[keys, :], _NN, preferred_element_type=F32)
            if j == 0:
                l, acc = jnp.sum(p, axis=-1, keepdims=True), pv
            else:
                alpha = jnp.exp2((m - m_new) * c)
                l = alpha * l + jnp.sum(p, axis=-1, keepdims=True)
                acc = alpha * acc + pv
            m = m_new
        o_ref[...] = (acc / l).astype(BF16)
        lse_ref[...] = m * scale + jnp.log(l)

    return pl.pallas_call(
        body, name="attn_a_fwd", grid=(n_kv, GROUP, T // tq),
        in_specs=[pl.BlockSpec((tq, HEAD_DIM), lambda kv, g, i: (i, kv * GROUP + g)),
                  pl.BlockSpec((T, HEAD_DIM), lambda kv, g, i: (0, n_q + kv)),
                  pl.BlockSpec((T, HEAD_DIM), lambda kv, g, i: (0, n_q + n_kv + kv))],
        out_specs=[pl.BlockSpec((tq, HEAD_DIM), lambda kv, g, i: (i, kv * GROUP + g)),
                   pl.BlockSpec((None, tq, 1), lambda kv, g, i: (kv * GROUP + g, i, 0))],
        out_shape=[jax.ShapeDtypeStruct((T, out_heads * HEAD_DIM), BF16), jax.ShapeDtypeStruct((n_q, T, 1), F32)],
        compiler_params=_params(("parallel", "parallel", "parallel")),
    )(pb, pb, pb)


def _attn_a_bwd(pb, o_cat, d_o, lse, n_q, n_kv, tq=1024, tc=256):
    T = pb.shape[0]
    tq, tc = _tile(T, tq), _tile(T, tc)
    scale = HEAD_DIM ** -0.5
    c = scale * LOG2E

    def body(q_ref, k_ref, v_ref, o_ref, do_ref, lse_ref, dq_ref, dkt_ref, dvt_ref):
        q, do = q_ref[...], do_ref[...]
        qt, dot = q.T, do.T
        delta = jnp.sum(do.astype(F32) * o_ref[...].astype(F32), axis=-1, keepdims=True)
        lse2 = lse_ref[...] * LOG2E

        @pl.when(jnp.logical_and(pl.program_id(1) == 0, pl.program_id(2) == 0))
        def _():
            dkt_ref[...] = jnp.zeros(dkt_ref.shape, F32)
            dvt_ref[...] = jnp.zeros(dvt_ref.shape, F32)

        dq = None
        for j in range(T // tc):
            keys = slice(j * tc, (j + 1) * tc)
            kc, vc = k_ref[keys, :], v_ref[keys, :]
            s = lax.dot_general(q, kc, _NT, preferred_element_type=F32)
            p = jnp.exp2(s * c - lse2)
            dp = lax.dot_general(do, vc, _NT, preferred_element_type=F32)
            ds = (p * (dp - delta) * scale).astype(BF16)
            dqj = lax.dot_general(ds, kc, _NN, preferred_element_type=F32)
            dq = dqj if dq is None else dq + dqj
            dvt_ref[:, keys] += lax.dot_general(dot, p.astype(BF16), _NN, preferred_element_type=F32)
            dkt_ref[:, keys] += lax.dot_general(qt, ds, _NN, preferred_element_type=F32)
        dq_ref[...] = dq

    qmap = lambda kv, g, i: (i, kv * GROUP + g)
    return pl.pallas_call(
        body, name="attn_a_bwd", grid=(n_kv, GROUP, T // tq),
        in_specs=[pl.BlockSpec((tq, HEAD_DIM), qmap),
                  pl.BlockSpec((T, HEAD_DIM), lambda kv, g, i: (0, n_q + kv)),
                  pl.BlockSpec((T, HEAD_DIM), lambda kv, g, i: (0, n_q + n_kv + kv)),
                  pl.BlockSpec((tq, HEAD_DIM), qmap),
                  pl.BlockSpec((tq, HEAD_DIM), qmap),
                  pl.BlockSpec((None, tq, 1), lambda kv, g, i: (kv * GROUP + g, i, 0))],
        out_specs=[pl.BlockSpec((tq, HEAD_DIM), qmap),
                   pl.BlockSpec((HEAD_DIM, T), lambda kv, g, i: (kv, 0)),
                   pl.BlockSpec((HEAD_DIM, T), lambda kv, g, i: (kv, 0))],
        out_shape=[jax.ShapeDtypeStruct((T, n_q * HEAD_DIM), F32),
                   jax.ShapeDtypeStruct((n_kv * HEAD_DIM, T), F32),
                   jax.ShapeDtypeStruct((n_kv * HEAD_DIM, T), F32)],
        compiler_params=_params(("parallel", "arbitrary", "arbitrary")),
    )(pb, pb, pb, o_cat, d_o, lse)


def _bucket_index():
    r = np.arange(BLOCK_Q)[:, None]
    j = np.arange(3 * BLOCK_Q)[None, :]
    rel = (j - BLOCK_Q) - r
    nb = N_BUCKETS // 2
    ret = np.where(rel > 0, nb, 0)
    n = np.abs(rel)
    max_exact = nb // 2
    nf = np.maximum(n, 1).astype(np.float32)
    large = max_exact + (np.log(nf / max_exact) / math.log(MAX_DISTANCE / max_exact) * (nb - max_exact)).astype(np.int32)
    large = np.minimum(large, nb - 1)
    return jnp.asarray(ret + np.where(n < max_exact, n, large), jnp.int32)


def _bias_build(idx, table_flat, n_heads, deps=()):
    def body(idx_ref, tab_ref, o_ref):
        h = pl.program_id(0)
        iv = idx_ref[...]
        acc = jnp.zeros(iv.shape, F32)
        for b in range(N_BUCKETS):
            acc = jnp.where(iv == b, tab_ref[b * n_heads + h], acc)
        r = lax.broadcasted_iota(jnp.int32, iv.shape, 0)
        j = lax.broadcasted_iota(jnp.int32, iv.shape, 1)
        o_ref[...] = jnp.where(jnp.abs(j - BLOCK_Q - r) <= WINDOW, acc, NEG_INF)

    return _pcall(
        body, deps, name="bias_build", grid=(n_heads,),
        in_specs=[pl.BlockSpec(idx.shape, lambda h: (0, 0)), pl.BlockSpec(memory_space=pltpu.SMEM)],
        out_specs=pl.BlockSpec((None,) + idx.shape, lambda h: (h, 0, 0)),
        out_shape=jax.ShapeDtypeStruct((n_heads,) + idx.shape, F32),
        compiler_params=_params(("parallel",)),
    )(idx, table_flat)


def _in_sequence(n, T):
    j = lax.broadcasted_iota(jnp.int32, (GROUP * BLOCK_Q, 3 * BLOCK_Q), 1)
    kabs = n * BLOCK_Q + j - BLOCK_Q
    return (kabs >= 0) & (kabs < T)


def _per_head_rows(values):
    head = lax.broadcasted_iota(jnp.int32, (GROUP * BLOCK_Q, 1), 0) // BLOCK_Q
    col = jnp.zeros((GROUP * BLOCK_Q, 1), F32)
    for g, v in enumerate(values):
        col = jnp.where(head == g, v, col)
    return col


def _band_specs(col, nblk, sb):
    return [pl.BlockSpec((BLOCK_Q, HEAD_DIM), lambda kv, i: (jnp.maximum(sb * i - 1, 0), col(kv))),
            pl.BlockSpec((sb * BLOCK_Q, HEAD_DIM), lambda kv, i: (i, col(kv))),
            pl.BlockSpec((BLOCK_Q, HEAD_DIM), lambda kv, i: (jnp.minimum(sb * i + sb, nblk - 1), col(kv)))]


def _head_specs(base, rows):
    return [pl.BlockSpec((rows, HEAD_DIM), functools.partial(lambda kv, i, g: (i, base + kv * GROUP + g), g=g))
            for g in range(GROUP)]


def _attn_b_fwd(pb, bias, sink, o_all, q_off, n_q, n_kv, deps=(), sb=16):
    T = pb.shape[0]
    nblk = T // BLOCK_Q
    sb = min(sb, nblk)
    tq = sb * BLOCK_Q
    scale = HEAD_DIM ** -0.5

    def body(*refs):
        q_refs = refs[0:GROUP]
        k_refs, v_refs = refs[GROUP:GROUP + 3], refs[GROUP + 3:GROUP + 6]
        bias_ref, sink_ref, o_ref, lse_ref = refs[GROUP + 6:]
        kv, i = pl.program_id(0), pl.program_id(1)
        kb = jnp.concatenate([r[...] for r in k_refs], axis=0)
        vb = jnp.concatenate([r[...] for r in v_refs], axis=0)
        bias_all = bias_ref[...].reshape(GROUP * BLOCK_Q, 3 * BLOCK_Q)
        sk = _per_head_rows([sink_ref[kv * GROUP + g] for g in range(GROUP)])
        for b in range(sb):
            rows = slice(b * BLOCK_Q, (b + 1) * BLOCK_Q)
            kw, vw = kb[b * BLOCK_Q:(b + 3) * BLOCK_Q], vb[b * BLOCK_Q:(b + 3) * BLOCK_Q]
            q = jnp.concatenate([r[rows, :] for r in q_refs], axis=0)
            s = lax.dot_general(q, kw, _NT, preferred_element_type=F32) * scale + bias_all
            if b == 0 or b == sb - 1:
                s = jnp.where(_in_sequence(i * sb + b, T), s, NEG_INF)
            m = jnp.maximum(jnp.max(s, axis=-1, keepdims=True), sk)
            p = jnp.exp(s - m)
            l = jnp.sum(p, axis=-1, keepdims=True) + jnp.exp(sk - m)
            o = (lax.dot_general(p.astype(BF16), vw, _NN, preferred_element_type=F32) / l).astype(BF16)
            lse = m + jnp.log(l)
            for g in range(GROUP):
                head = slice(g * BLOCK_Q, (g + 1) * BLOCK_Q)
                o_ref[rows, g * HEAD_DIM:(g + 1) * HEAD_DIM] = o[head]
                lse_ref[g, rows, :] = lse[head]

    first_group = o_all.shape[1] // (GROUP * HEAD_DIM) - n_kv
    return _pcall(
        body, deps, into=(o_all, 0), name="attn_b_fwd", grid=(n_kv, nblk // sb),
        in_specs=[*_head_specs(q_off, tq),
                  *_band_specs(lambda kv: q_off + n_q + kv, nblk, sb),
                  *_band_specs(lambda kv: q_off + n_q + n_kv + kv, nblk, sb),
                  pl.BlockSpec((GROUP, BLOCK_Q, 3 * BLOCK_Q), lambda kv, i: (kv, 0, 0)),
                  pl.BlockSpec(memory_space=pltpu.SMEM)],
        out_specs=[pl.BlockSpec((tq, GROUP * HEAD_DIM), lambda kv, i: (i, first_group + kv)),
                   pl.BlockSpec((GROUP, tq, 1), lambda kv, i: (kv, i, 0))],
        out_shape=[jax.ShapeDtypeStruct(o_all.shape, BF16), jax.ShapeDtypeStruct((n_q, T, 1), F32)],
        compiler_params=_params(("parallel", "parallel")),
    )(*([pb] * (GROUP + 6)), bias, sink)


def _attn_b_bwd(pb, o_cat, d_o, lse, bias, sink, q_off, n_q, n_kv, o_off, deps=(), sb=16):
    T = pb.shape[0]
    nblk = T // BLOCK_Q
    sb = min(sb, nblk)
    tq = sb * BLOCK_Q
    scale = HEAD_DIM ** -0.5

    def body(*refs):
        q_refs = refs[0:GROUP]
        k_refs, v_refs = refs[GROUP:GROUP + 3], refs[GROUP + 3:GROUP + 6]
        o_refs, do_refs = refs[GROUP + 6:2 * GROUP + 6], refs[2 * GROUP + 6:3 * GROUP + 6]
        lse_ref, bias_ref, sink_ref, dq_ref, dk_ref, dv_ref, dbias_ref, dsink_ref, dkb_ref, dvb_ref = refs[3 * GROUP + 6:]
        kv, i = pl.program_id(0), pl.program_id(1)
        first = i == 0

        @pl.when(first)
        def _():
            dk_ref[...] = jnp.zeros(dk_ref.shape, F32)
            dv_ref[...] = jnp.zeros(dv_ref.shape, F32)
            dbias_ref[...] = jnp.zeros(dbias_ref.shape, F32)

        kb = jnp.concatenate([r[...] for r in k_refs], axis=0)
        vb = jnp.concatenate([r[...] for r in v_refs], axis=0)
        dkb_ref[...] = jnp.zeros(dkb_ref.shape, F32)
        dvb_ref[...] = jnp.zeros(dvb_ref.shape, F32)
        row = lax.broadcasted_iota(jnp.int32, (SUBLANES, LANES), 0)
        dsink = jnp.zeros((SUBLANES, LANES), F32)
        bias_all = bias_ref[...].reshape(GROUP * BLOCK_Q, 3 * BLOCK_Q)
        sk = _per_head_rows([sink_ref[kv * GROUP + g] for g in range(GROUP)])
        for b in range(sb):
            rows = slice(b * BLOCK_Q, (b + 1) * BLOCK_Q)
            win = slice(b * BLOCK_Q, (b + 3) * BLOCK_Q)
            kw, vw = kb[win], vb[win]
            q = jnp.concatenate([r[rows, :] for r in q_refs], axis=0)
            do = jnp.concatenate([r[rows, :] for r in do_refs], axis=0)
            o = jnp.concatenate([r[rows, :] for r in o_refs], axis=0)
            lse = jnp.concatenate([lse_ref[g, rows, :] for g in range(GROUP)], axis=0)
            delta = jnp.sum(do.astype(F32) * o.astype(F32), axis=-1, keepdims=True)
            s = lax.dot_general(q, kw, _NT, preferred_element_type=F32) * scale + bias_all
            if b == 0 or b == sb - 1:
                s = jnp.where(_in_sequence(i * sb + b, T), s, NEG_INF)
            p = jnp.exp(s - lse)
            dp = lax.dot_general(do, vw, _NT, preferred_element_type=F32)
            ds = p * (dp - delta)
            dbias_ref[...] += ds.reshape(GROUP, BLOCK_Q, 3 * BLOCK_Q)
            sunk = jnp.exp(sk - lse) * delta
            for g in range(GROUP):
                dsink = dsink + jnp.where(row == g, -jnp.sum(sunk[g * BLOCK_Q:(g + 1) * BLOCK_Q]), 0.0)
            dsb = (ds * scale).astype(BF16)
            dq = lax.dot_general(dsb, kw, _NN, preferred_element_type=F32).astype(BF16)
            for g in range(GROUP):
                dq_ref[rows, g * HEAD_DIM:(g + 1) * HEAD_DIM] = dq[g * BLOCK_Q:(g + 1) * BLOCK_Q]
            dkb_ref[win, :] += lax.dot_general(dsb, q, _TN, preferred_element_type=F32)
            dvb_ref[win, :] += lax.dot_general(p.astype(BF16), do, _TN, preferred_element_type=F32)
        _accumulate(dsink_ref, dsink, first)

        before = pl.ds(pl.multiple_of(jnp.maximum(sb * i - 1, 0) * BLOCK_Q, BLOCK_Q), BLOCK_Q)
        own = pl.ds(pl.multiple_of(i * tq, BLOCK_Q), tq)
        after = pl.ds(pl.multiple_of(jnp.minimum(sb * i + sb, nblk - 1) * BLOCK_Q, BLOCK_Q), BLOCK_Q)
        for acc_ref, band_ref in ((dk_ref, dkb_ref), (dv_ref, dvb_ref)):
            acc_ref[before, :] += band_ref[0:BLOCK_Q, :]
            acc_ref[own, :] += band_ref[BLOCK_Q:BLOCK_Q + tq, :]
            acc_ref[after, :] += band_ref[BLOCK_Q + tq:, :]

    return _pcall(
        body, deps, name="attn_b_bwd", grid=(n_kv, nblk // sb),
        in_specs=[*_head_specs(q_off, tq),
                  *_band_specs(lambda kv: q_off + n_q + kv, nblk, sb),
                  *_band_specs(lambda kv: q_off + n_q + n_kv + kv, nblk, sb),
                  *_head_specs(o_off, tq), *_head_specs(o_off, tq),
                  pl.BlockSpec((GROUP, tq, 1), lambda kv, i: (kv, i, 0)),
                  pl.BlockSpec((GROUP, BLOCK_Q, 3 * BLOCK_Q), lambda kv, i: (kv, 0, 0)),
                  pl.BlockSpec(memory_space=pltpu.SMEM)],
        out_specs=[pl.BlockSpec((tq, GROUP * HEAD_DIM), lambda kv, i: (i, kv)),
                   pl.BlockSpec((T, HEAD_DIM), lambda kv, i: (0, kv)),
                   pl.BlockSpec((T, HEAD_DIM), lambda kv, i: (0, kv)),
                   pl.BlockSpec((GROUP, BLOCK_Q, 3 * BLOCK_Q), lambda kv, i: (kv, 0, 0)),
                   pl.BlockSpec((None, SUBLANES, LANES), lambda kv, i: (kv, 0, 0))],
        out_shape=[jax.ShapeDtypeStruct((T, n_q * HEAD_DIM), BF16),
                   jax.ShapeDtypeStruct((T, n_kv * HEAD_DIM), F32),
                   jax.ShapeDtypeStruct((T, n_kv * HEAD_DIM), F32),
                   jax.ShapeDtypeStruct((n_q, BLOCK_Q, 3 * BLOCK_Q), F32),
                   jax.ShapeDtypeStruct((n_kv, SUBLANES, LANES), F32)],
        scratch_shapes=[pltpu.VMEM((tq + 2 * BLOCK_Q, HEAD_DIM), F32), pltpu.VMEM((tq + 2 * BLOCK_Q, HEAD_DIM), F32)],
        compiler_params=_params(("parallel", "arbitrary")),
    )(*([pb] * (GROUP + 6)), *([o_cat] * GROUP), *([d_o] * GROUP), lse, bias, sink)


def _table_grads(dbias, dsink_raw, idx):
    n_heads = dbias.shape[0]
    n_kv = dsink_raw.shape[0]

    def body(db_ref, ds_ref, idx_ref, dt_ref, dsk_ref):
        iv = idx_ref[...]
        row = lax.broadcasted_iota(jnp.int32, (SUBLANES, LANES), 0)
        lane = lax.broadcasted_iota(jnp.int32, (SUBLANES, LANES), 1)
        dsk = jnp.zeros((SUBLANES, LANES), F32)
        for h in range(n_heads):
            d = db_ref[h]
            acc = jnp.zeros((SUBLANES, LANES), F32)
            for b in range(N_BUCKETS):
                acc = jnp.where((row == 0) & (lane == b), jnp.sum(jnp.where(iv == b, d, 0.0)), acc)
            dt_ref[:, h * LANES:(h + 1) * LANES] = acc
            raw = ds_ref[h // GROUP]
            val = jnp.sum(jnp.where((row == h % GROUP) & (lane == 0), raw, 0.0))
            dsk = jnp.where((row == 0) & (lane == h), val, dsk)
        dsk_ref[...] = dsk

    return pl.pallas_call(
        body, name="table_grads",
        in_specs=[pl.BlockSpec(memory_space=pltpu.VMEM)] * 3,
        out_specs=[pl.BlockSpec(memory_space=pltpu.VMEM)] * 2,
        out_shape=[jax.ShapeDtypeStruct((SUBLANES, n_heads * LANES), F32),
                   jax.ShapeDtypeStruct((SUBLANES, LANES), F32)],
        compiler_params=_params(),
    )(dbias, dsink_raw, idx)


def _position():
    x, y, c = lax.axis_index("x"), lax.axis_index("y"), lax.axis_index("c")
    return x, y, c


def _hbm(a):
    return pltpu.with_memory_space_constraint(a, pltpu.HBM)


def _split_start(name, bufs, sem_shapes, issue):
    nb, ns = len(bufs), len(sem_shapes)

    def body(*refs):
        buf_refs = refs[:nb]
        sems = refs[nb:nb + ns]
        token = refs[nb + ns + nb]
        issue(buf_refs, sems)
        token[...] = jnp.zeros(token.shape, F32)

    outs = pl.pallas_call(
        body, name=name,
        in_specs=[_HBM] * nb,
        out_specs=[_SEM] * ns + [_HBM] * nb + [_VMEM],
        out_shape=[pltpu.SemaphoreType.DMA(s) for s in sem_shapes] + [pltpu.HBM(b.shape, b.dtype) for b in bufs]
        + [jax.ShapeDtypeStruct((SUBLANES, LANES), F32)],
        input_output_aliases={i: ns + i for i in range(nb)},
        compiler_params=pltpu.CompilerParams(has_side_effects=_EFFECT),
    )(*[_hbm(b) for b in bufs])
    return outs[:ns], outs[ns:ns + nb], outs[-1]


def _split_wait(name, bufs, send, recv, counts, size_of, after):
    nb = len(bufs)

    def body(*refs):
        buf_refs = refs[:nb]
        send_ref, recv_ref = refs[nb], refs[nb + 1]
        x, y, c = _position()
        for w, n in enumerate(counts):
            ref = size_of(buf_refs, w)
            for k in range(n):
                s = sum(counts[:w]) + k
                cp = pltpu.make_async_remote_copy(
                    src_ref=ref, dst_ref=ref, send_sem=send_ref.at[s], recv_sem=recv_ref.at[s],
                    device_id=(x, y, c), device_id_type=MESH)
                cp.wait_send()
                cp.wait_recv()

    return pl.pallas_call(
        body, name=name,
        in_specs=[_HBM] * nb + [_SEM, _SEM, _ANY],
        out_specs=[_HBM] * nb,
        out_shape=[pltpu.HBM(b.shape, b.dtype) for b in bufs],
        input_output_aliases={i: i for i in range(nb)},
        compiler_params=pltpu.CompilerParams(has_side_effects=_EFFECT),
    )(*bufs, send, recv, after)


def _block_of(pos):
    return 4 * pos[0] + 2 * pos[1] + pos[2]


def _shard_of(ref, blk, by_cols):
    aligned = (lambda v, a: v) if isinstance(blk, int) else pl.multiple_of
    if by_cols:
        n = ref.shape[1] // N_DEV
        return ref.at[:, pl.ds(aligned(blk * n, LANES), n)]
    r = ref.shape[0] // N_DEV
    return ref.at[pl.ds(aligned(blk * r, SUBLANES), r), :]


def _place_own(name, land, shard, by_cols, tr=256):
    r, n = shard.shape
    tr = _tile(r, tr)
    mine = _block_of(_position()).astype(jnp.int32).reshape(1)

    def body(m_ref, land_ref, s_ref, o_ref):
        o_ref[...] = s_ref[...]

    if by_cols:
        out = pl.BlockSpec((tr, n), lambda i, m_ref: (i, m_ref[0]))
    else:
        out = pl.BlockSpec((tr, n), lambda i, m_ref: (m_ref[0] * (r // tr) + i, 0))
    return pl.pallas_call(
        body, name=name,
        grid_spec=pltpu.PrefetchScalarGridSpec(
            num_scalar_prefetch=1, grid=(r // tr,),
            in_specs=[_ANY, pl.BlockSpec((tr, n), lambda i, m_ref: (i, 0))], out_specs=out),
        out_shape=jax.ShapeDtypeStruct(land.shape, land.dtype),
        input_output_aliases={1: 0},
        compiler_params=_params(("parallel",)),
    )(mine, land, shard)


def _gather_start(name, shards, by_cols, groups, after=None):
    nw = len(shards)
    lands = [lax.empty((s.shape[0], s.shape[1] * N_DEV) if cols else (s.shape[0] * N_DEV, s.shape[1]), s.dtype)
             for s, cols in zip(shards, by_cols)]
    order = [] if after is None else [after]

    def issue(bufs, sems):
        x, y, c = _position()
        peers = [(x, y, 1 - c), (1 - x, y, c), (x, 1 - y, c), (1 - x, 1 - y, c)]
        for gi, grp in enumerate(groups):
            for wi, w in enumerate(grp):
                for k, peer in enumerate(peers):
                    pltpu.make_async_remote_copy(
                        src_ref=bufs[w], dst_ref=_shard_of(bufs[nw + w], _block_of((x, y, c)), by_cols[w]),
                        send_sem=sems[2 * gi].at[4 * wi + k], recv_sem=sems[2 * gi + 1].at[4 * wi + k],
                        device_id=peer, device_id_type=MESH).start()

    sem_shapes = [(4 * len(g),) for g in groups for _ in range(2)]
    sems, thru, token = _split_start(name, list(shards) + lands + order, sem_shapes, issue)
    return sems, thru[:nw], thru[nw:2 * nw], token


def _gather_forward(name, lands, by_cols):
    nw = len(lands)

    def issue(land, sems):
        x, y, c = _position()
        for w in range(nw):
            for k, chip in enumerate([(1 - x, y), (x, 1 - y), (1 - x, 1 - y)]):
                blk = _shard_of(land[w], _block_of((*chip, c)), by_cols[w])
                pltpu.make_async_remote_copy(
                    src_ref=blk, dst_ref=blk, send_sem=sems[0].at[3 * w + k], recv_sem=sems[1].at[3 * w + k],
                    device_id=(x, y, 1 - c), device_id_type=MESH).start()

    return _split_start(name, lands, [(3 * nw,), (3 * nw,)], issue)


def _first_block(bufs, w, offset=0):
    return bufs[offset + w].at[0]


_PEER_FLIPS = ((0, 0, 1), (1, 0, 0), (1, 0, 1), (0, 1, 0), (0, 1, 1), (1, 1, 0), (1, 1, 1))


def _scatter_start(name, grads, by_cols):
    nw = len(grads)
    lands = []
    for g, cols in zip(grads, by_cols):
        shard = (g.shape[0], g.shape[1] // N_DEV) if cols else (g.shape[0] // N_DEV, g.shape[1])
        lands.append(lax.empty((N_DEV,) + shard, g.dtype))

    def issue(bufs, sems):
        x, y, c = _position()
        flip = lambda v, f: 1 - v if f else v
        for w in range(nw):
            for k, (fx, fy, fc) in enumerate(_PEER_FLIPS):
                peer = (flip(x, fx), flip(y, fy), flip(c, fc))
                pltpu.make_async_remote_copy(
                    src_ref=_shard_of(bufs[w], _block_of(peer), by_cols[w]), dst_ref=bufs[nw + w].at[_block_of((x, y, c))],
                    send_sem=sems[0].at[7 * w + k], recv_sem=sems[1].at[7 * w + k],
                    device_id=peer, device_id_type=MESH).start()

    return _split_start(name, list(grads) + lands, [(7 * nw,), (7 * nw,)], issue)


def _adam(w, g, m, v):
    m = ADAM_B1 * m + (1.0 - ADAM_B1) * g
    v = ADAM_B2 * v + (1.0 - ADAM_B2) * (g * g)
    m_hat = m / (1.0 - ADAM_B1 ** ADAM_STEP)
    v_hat = v / (1.0 - ADAM_B2 ** ADAM_STEP)
    delta = -ADAM_LR * (m_hat / (jnp.sqrt(v_hat) + ADAM_EPS) + ADAM_WD * w)
    return delta, m, v


def _sum_adam(name, landed, grad, by_cols, w, m, v, tr=256):
    R, C = w.shape
    tr = _tile(R, tr if C > 1024 else 2 * tr)
    mine = _block_of(_position()).astype(jnp.int32).reshape(1)

    def body(me_ref, l_ref, own_ref, w_ref, m_ref, v_ref, g_ref, d_ref, nm_ref, nv_ref):
        own = own_ref[...].astype(F32)
        g = None
        for d in range(N_DEV):
            part = jnp.where(me_ref[0] == d, own, l_ref[d].astype(F32))
            g = part if g is None else g + part
        g_ref[...] = g
        d_ref[...], nm_ref[...], nv_ref[...] = _adam(w_ref[...], g, m_ref[...], v_ref[...])

    tile = pl.BlockSpec((tr, C), lambda i, me_ref: (i, 0))
    if by_cols:
        own = pl.BlockSpec((tr, C), lambda i, me_ref: (i, me_ref[0]))
    else:
        own = pl.BlockSpec((tr, C), lambda i, me_ref: (me_ref[0] * (R // tr) + i, 0))
    return pl.pallas_call(
        body, name=name,
        grid_spec=pltpu.PrefetchScalarGridSpec(
            num_scalar_prefetch=1, grid=(R // tr,),
            in_specs=[pl.BlockSpec((N_DEV, tr, C), lambda i, me_ref: (0, i, 0)), own, tile, tile, tile],
            out_specs=[tile] * 4),
        out_shape=[jax.ShapeDtypeStruct((R, C), F32)] * 4,
        compiler_params=_params(("parallel",)),
    )(mine, landed, grad, w, m, v)


def _small_all_reduce(parts, deps=()):
    W = parts.shape[1]

    def body(p_ref, o_ref, slots, send_sems, recv_sems):
        x, y, c = _position()
        me = 4 * x + 2 * y + c
        slots[me] = jnp.sum(p_ref[...], axis=0, keepdims=True)
        peers = [(x, y, 1 - c), (1 - x, y, c), (1 - x, y, 1 - c), (x, 1 - y, c), (x, 1 - y, 1 - c),
                 (1 - x, 1 - y, c), (1 - x, 1 - y, 1 - c)]
        copies = []
        for k, peer in enumerate(peers):
            cp = pltpu.make_async_remote_copy(
                src_ref=slots.at[me], dst_ref=slots.at[me], send_sem=send_sems.at[k], recv_sem=recv_sems.at[k],
                device_id=peer, device_id_type=MESH)
            cp.start()
            copies.append(cp)
        for cp in copies:
            cp.wait()
        total = slots[0]
        for d in range(1, N_DEV):
            total = total + slots[d]
        o_ref[...] = total

    return _pcall(
        body, deps, name="small_all_reduce",
        in_specs=[pl.BlockSpec(memory_space=pltpu.VMEM)], out_specs=pl.BlockSpec(memory_space=pltpu.VMEM),
        out_shape=jax.ShapeDtypeStruct((1, W), F32),
        scratch_shapes=[pltpu.VMEM((N_DEV, 1, W), F32), pltpu.SemaphoreType.DMA((7,)), pltpu.SemaphoreType.DMA((7,))],
    )(parts)


def _adam_small(w, g, m, v):
    def body(w_ref, g_ref, m_ref, v_ref, d_ref, nm_ref, nv_ref):
        d_ref[...], nm_ref[...], nv_ref[...] = _adam(w_ref[...], g_ref[...], m_ref[...], v_ref[...])

    return pl.pallas_call(
        body, name="adam_small",
        in_specs=[pl.BlockSpec(memory_space=pltpu.VMEM)] * 4, out_specs=[pl.BlockSpec(memory_space=pltpu.VMEM)] * 3,
        out_shape=[jax.ShapeDtypeStruct(w.shape, F32)] * 3,
    )(w, g, m, v)


_GATHER_GROUPS = (("w_in",), ("w_out", "w_up", "ple_w"), ("w_down", "w_gate"))
_COL_SHARDED = ("w_in", "w_up", "ple_w")


class _MeshComm:
    def __init__(self, w, mom, var):
        self.w, self.mom, self.var = w, mom, var
        self.out = {}
        self._scatters = {}

    def gather_begin(self):
        self._groups = {}
        token = None
        for tag, first, group_list in (("gather_start0", 0, _GATHER_GROUPS[:1]), ("gather_start1", 1, _GATHER_GROUPS[1:])):
            names = [n for g in group_list for n in g]
            idx = {n: i for i, n in enumerate(names)}
            by_cols = [n in _COL_SHARDED for n in names]
            sems, src, lands, token = _gather_start(tag, [self.w[n].astype(BF16) for n in names], by_cols,
                                                    [[idx[n] for n in g] for g in group_list], token)
            lands = [_place_own("place_" + n, land, s, cols) for n, land, s, cols in zip(names, lands, src, by_cols)]
            for k, g in enumerate(group_list):
                self._groups[first + k] = (sems[2 * k], sems[2 * k + 1], [src[idx[n]] for n in g],
                                           [lands[idx[n]] for n in g])
        return token

    @staticmethod
    def _shard_size(names, offset):
        return lambda bufs, w: _shard_of(bufs[offset + w], 0, names[w] in _COL_SHARDED)

    def gather_arrive(self, gi, after):
        names = _GATHER_GROUPS[gi]
        send, recv, src, lands = self._groups[gi]
        out = _split_wait("gather_arrive%d" % gi, src + lands, send, recv, [4] * len(names),
                          self._shard_size(names, len(names)), after)
        self._arrived = out[len(names):]

    def gather_forward(self, gi):
        by_cols = [n in _COL_SHARDED for n in _GATHER_GROUPS[gi]]
        self._fsems, self._fthru, token = _gather_forward("gather_forward%d" % gi, self._arrived, by_cols)
        return token

    def gather_finish(self, gi, after):
        names = _GATHER_GROUPS[gi]
        out = _split_wait("gather_finish%d" % gi, self._fthru, self._fsems[0], self._fsems[1], [3] * len(names),
                          self._shard_size(names, 0), after)
        return dict(zip(names, out))

    def reduce_begin(self, key, grads):
        names = list(grads)
        sems, thru, token = _scatter_start("scatter_start_" + key, [grads[n] for n in names],
                                           [n in _COL_SHARDED for n in names])
        self._scatters[key] = (names, sems, thru)
        return token

    def reduce_finish(self, key, after):
        names, sems, thru = self._scatters[key]
        nw = len(names)
        out = _split_wait("scatter_wait_" + key, thru, sems[0], sems[1], [N_DEV - 1] * nw,
                          functools.partial(_first_block, offset=nw), after)
        for i, n in enumerate(names):
            self.out[n] = _sum_adam("adam_" + n, out[nw + i], out[i], n in _COL_SHARDED, self.w[n], self.mom[n],
                                    self.var[n])


def _step(x, p, target, gains, comm):
    T, D = x.shape
    n_q = D // (2 * HEAD_DIM)
    n_kv = n_q // GROUP
    cos, sin = _rope_tables(T)
    idx = _bucket_index()

    t = comm.gather_begin()
    u = _rms_fwd("norm_attn", x, gains["attn_norm_g"], deps=(t,))
    comm.gather_arrive(0, u)
    t = comm.gather_forward(0)
    bias = _bias_build(idx, gains["rel_bias_table"].reshape(-1), n_q, deps=(t,))
    full = comm.gather_finish(0, bias)
    proj_a, pb = _in_proj(u, full["w_in"], cos, sin, gains["q_norm_g"], gains["k_norm_g"], n_q + n_kv)
    o_a, lse_a = _attn_a_fwd(pb, n_q, n_kv, 2 * n_q)
    comm.gather_arrive(1, lse_a)
    t = comm.gather_forward(1)
    sink = gains["sink_logits"].reshape(-1)
    b_off = n_q + 2 * n_kv
    o_cat, lse_b = _attn_b_fwd(pb, bias, sink, o_a, b_off, n_q, n_kv, deps=(t,))
    full.update(comm.gather_finish(1, lse_b))
    h1, m_in = _mm_nn_rms("out_proj", o_cat, full["w_out"], x, gains["mlp_norm_g"])

    def up_epilogue(acc, extra, outs):
        outs[0][...] = acc.astype(BF16)
        r = jnp.maximum(acc, 0.0)
        outs[1][...] = (r * r).astype(BF16)

    a_act, f_act = _mm_nn("up_proj", m_in, full["w_up"], epilogue=up_epilogue, out_dtypes=[BF16, BF16], tn=2048)
    comm.gather_arrive(2, f_act)
    t = comm.gather_forward(2)
    p_b = p.astype(BF16)
    pe = _mm_nn("ple_proj", p_b, full["ple_w"], deps=(t,))
    full.update(comm.gather_finish(2, pe))
    h2 = _mm_nn("down_proj", f_act, full["w_down"], epilogue=_store_add, extras=(h1,), tn=512)
    gn = _rms_fwd("norm_gate", h2, gains["gate_norm_g"])

    dh3, dz, dpe, dg_final, dg_ple, loss_part = _gate_tail(gn, full["w_gate"], h2, pe, target, gains["ple_norm_g"],
                                                           gains["final_norm_g"])
    gw_gate = _mm_tn("grad_w_gate", gn, dz, tn=1024)
    gw_ple = _mm_tn("grad_ple_w", p_b, dpe)
    dh2, dh2_b, dg_gate = _mm_nt_rms_bwd("d_gate_in", dz, full["w_gate"], h2, gains["gate_norm_g"], dh3, tm=512)
    gw_down = _mm_tn("grad_w_down", f_act, dh2_b, tn=1024)
    t = comm.reduce_begin("b", dict(w_gate=gw_gate, ple_w=gw_ple, w_down=gw_down))

    def act_bwd(acc, extra, outs):
        outs[0][...] = (acc * (2.0 * jnp.maximum(extra[0][...].astype(F32), 0.0))).astype(BF16)

    da = _mm_nt("d_act", dh2_b, full["w_down"], out_dtype=BF16, epilogue=act_bwd, extras=(a_act,), tn=2048, deps=(t,))
    gw_up = _mm_tn("grad_w_up", m_in, da, tn=1024)
    dm = _mm_nt("d_mlp_in", da, full["w_up"], out_dtype=BF16, tn=512)
    dh1, dh1_b, dg_mlp = _rms_bwd("norm_mlp_bwd", dm, h1, gains["mlp_norm_g"], dh2)
    gw_out = _mm_tn("grad_w_out", o_cat, dh1_b, tn=1024)
    t = comm.reduce_begin("d", dict(w_up=gw_up, w_out=gw_out))
    d_o = _mm_nt("d_attn_out", dh1_b, full["w_out"], out_dtype=BF16, tn=2048, deps=(t,))
    dqa, dka_t, dva_t = _attn_a_bwd(pb, o_cat, d_o, lse_a, n_q, n_kv)
    dqb, dkb, dvb, dbias, dsink_raw = _attn_b_bwd(pb, o_cat, d_o, lse_b, bias, sink, b_off, n_q, n_kv, n_q)
    dtable, dsink = _table_grads(dbias, dsink_raw, idx)
    dproj, dg_q, dg_k = _dproj(proj_a, dqa, dka_t, dva_t, dqb, dkb, dvb, cos, sin, gains["q_norm_g"], gains["k_norm_g"])
    gw_in = _mm_tn("grad_w_in", u, dproj, tn=1024)
    t = comm.reduce_begin("e", dict(w_in=gw_in))
    dx, dg_attn = _mm_nt_rms_bwd("d_attn_in", dproj, full["w_in"], x, gains["attn_norm_g"], dh1, with_bf16=False,
                                 tm=512, deps=(t,))
    for key in "bd":
        comm.reduce_finish(key, dx)

    parts = jnp.concatenate([dg_attn, dg_mlp, dg_ple, dg_gate, dg_final, dg_q, dg_k, dtable, dsink, loss_part], axis=1)
    return dx, parts


_SHARDED = ("w_in", "w_out", "w_up", "w_down", "ple_w", "w_gate")
_VECTORS = ("attn_norm_g", "mlp_norm_g", "ple_norm_g", "gate_norm_g", "final_norm_g")
_ORDER = ("attn_norm_g", "w_in", "q_norm_g", "k_norm_g", "sink_logits", "w_out", "mlp_norm_g", "w_up", "w_down",
          "ple_w", "ple_norm_g", "gate_norm_g", "w_gate", "rel_bias_table", "final_norm_g")


def _pack_small(vals, n_heads):
    lane_pad = lambda v: jnp.pad(v, ((0, 0), (0, LANES - v.shape[1])))
    table = lane_pad(vals["rel_bias_table"].T).reshape(1, n_heads * LANES)
    return jnp.concatenate(
        [vals[n].reshape(1, -1) for n in _VECTORS] + [vals["q_norm_g"], vals["k_norm_g"], table,
                                                      lane_pad(vals["sink_logits"]), jnp.zeros((1, LANES), F32)], axis=1)


def _unpack_small(row, like, n_heads):
    out, off = {}, 0
    for n in _VECTORS:
        out[n] = row[:, off:off + like[n].size].reshape(like[n].shape)
        off += like[n].size
    for n in ("q_norm_g", "k_norm_g"):
        out[n] = row[:, off:off + LANES]
        off += LANES
    out["rel_bias_table"] = row[:, off:off + n_heads * LANES].reshape(n_heads, LANES)[:, :N_BUCKETS].T
    off += n_heads * LANES
    out["sink_logits"] = row[:, off:off + n_heads]
    off += LANES
    return out, row[0, off]


def kernel(x, p, attn_norm_g, w_in, q_norm_g, k_norm_g, sink_logits, w_out, mlp_norm_g, w_up, w_down, ple_w, ple_norm_g, gate_norm_g, w_gate, rel_bias_table, final_norm_g, loss_target, m_attn_norm_g, m_w_in, m_q_norm_g, m_k_norm_g, m_sink_logits, m_w_out, m_mlp_norm_g, m_w_up, m_w_down, m_ple_w, m_ple_norm_g, m_gate_norm_g, m_w_gate, m_rel_bias_table, m_final_norm_g, v_attn_norm_g, v_w_in, v_q_norm_g, v_k_norm_g, v_sink_logits, v_w_out, v_mlp_norm_g, v_w_up, v_w_down, v_ple_w, v_ple_norm_g, v_gate_norm_g, v_w_gate, v_rel_bias_table, v_final_norm_g):
    w = dict(attn_norm_g=attn_norm_g, w_in=w_in[0], q_norm_g=q_norm_g, k_norm_g=k_norm_g, sink_logits=sink_logits,
             w_out=w_out[0], mlp_norm_g=mlp_norm_g, w_up=w_up[0], w_down=w_down[0], ple_w=ple_w[0],
             ple_norm_g=ple_norm_g, gate_norm_g=gate_norm_g, w_gate=w_gate[0], rel_bias_table=rel_bias_table,
             final_norm_g=final_norm_g)
    mom = dict(attn_norm_g=m_attn_norm_g, w_in=m_w_in[0], q_norm_g=m_q_norm_g, k_norm_g=m_k_norm_g,
               sink_logits=m_sink_logits, w_out=m_w_out[0], mlp_norm_g=m_mlp_norm_g, w_up=m_w_up[0],
               w_down=m_w_down[0], ple_w=m_ple_w[0], ple_norm_g=m_ple_norm_g, gate_norm_g=m_gate_norm_g,
               w_gate=m_w_gate[0], rel_bias_table=m_rel_bias_table, final_norm_g=m_final_norm_g)
    var = dict(attn_norm_g=v_attn_norm_g, w_in=v_w_in[0], q_norm_g=v_q_norm_g, k_norm_g=v_k_norm_g,
               sink_logits=v_sink_logits, w_out=v_w_out[0], mlp_norm_g=v_mlp_norm_g, w_up=v_w_up[0],
               w_down=v_w_down[0], ple_w=v_ple_w[0], ple_norm_g=v_ple_norm_g, gate_norm_g=v_gate_norm_g,
               w_gate=v_w_gate[0], rel_bias_table=v_rel_bias_table, final_norm_g=v_final_norm_g)
    D = x.shape[-1]
    n_heads = D // (2 * HEAD_DIM)

    gains = {n: w[n] for n in w if n not in _SHARDED}
    gains["final_norm_g"] = final_norm_g.reshape(1, -1)

    comm = _MeshComm(w, mom, var)
    dx, parts = _step(x[0], p[0, 0], loss_target[0], gains, comm)

    small_g = _small_all_reduce(parts, deps=[comm.out[n][0] for n in comm.out])
    comm.reduce_finish("e", small_g)

    g_out, d_out, m_out, v_out = {}, {}, {}, {}
    for n in _SHARDED:
        g, d, nm, nv = comm.out[n]
        g_out[n], d_out[n], m_out[n], v_out[n] = g[None], d[None], nm[None], nv[None]

    small = {n: v for n, v in w.items() if n not in _SHARDED}
    pack = lambda vals: _pack_small({n: vals[n] for n in small}, n_heads)
    sd, sm, sv = _adam_small(pack(w), small_g, pack(mom), pack(var))
    sg, loss = _unpack_small(small_g, small, n_heads)
    g_out.update(sg)
    for dst, row in ((d_out, sd), (m_out, sm), (v_out, sv)):
        dst.update(_unpack_small(row, small, n_heads)[0])

    return (loss, dx[None], *[g_out[n] for n in _ORDER], *[d_out[n] for n in _ORDER],
            *[m_out[n] for n in _ORDER], *[v_out[n] for n in _ORDER])
```

```python
import functools
import math

import numpy as np
import jax
import jax.numpy as jnp
from jax import lax
from jax.experimental import pallas as pl
from jax.experimental.pallas import tpu as pltpu

F32 = jnp.float32
BF16 = jnp.bfloat16

N_DEV = 8
N_CHIP = 4
HEAD_DIM = 128
GROUP = 4
GRID_W = 64
WINDOW = 128
BLOCK_Q = 128
N_BUCKETS = 32
MAX_DISTANCE = 128
ROPE_THETA = 10000.0
EPS = 1e-6
NEG_INF = -1e30
ADAM_LR = 0.001
ADAM_B1 = 0.9
ADAM_B2 = 0.999
ADAM_EPS = 1e-08
ADAM_WD = 0.01
ADAM_STEP = 10
LOG2E = math.log2(math.e)
LANES = 128
SUBLANES = 8
VMEM_LIMIT_BYTES = 60 * 1024 * 1024
MESH = pl.DeviceIdType.MESH

_NT = (((1,), (1,)), ((), ()))
_NN = (((1,), (0,)), ((), ()))
_TN = (((0,), (0,)), ((), ()))


def _tile(dim, pref):
    return pref if dim % pref == 0 else dim


def _params(sem=None):
    return pltpu.CompilerParams(dimension_semantics=sem, vmem_limit_bytes=VMEM_LIMIT_BYTES)


_HBM = pl.BlockSpec(memory_space=pltpu.HBM)
_SEM = pl.BlockSpec(memory_space=pltpu.SEMAPHORE)
_ANY = pl.BlockSpec(memory_space=pl.ANY)
_VMEM = pl.BlockSpec(memory_space=pltpu.VMEM)
_EFFECT = pltpu.SideEffectType.DATAFLOW_SIDE_EFFECTING


def _pcall(body, deps=(), *, in_specs, into=None, **kw):
    deps = [d for d in deps if d is not None]
    nd = len(deps)
    if into is not None:
        deps = [into[0]] + deps
        nd += 1
        kw["input_output_aliases"] = {0: into[1]}

    def wrapped(*refs):
        body(*refs[nd:])

    call = pl.pallas_call(wrapped, in_specs=[_ANY] * nd + list(in_specs), **kw)
    return lambda *args: call(*deps, *args)


def _mm(name, a, b, dims, grid, a_spec, b_spec, out_shape, out_specs, acc_shape, epilogue,
        extras=(), extra_specs=(), deps=(), semantics=("parallel", "parallel", "arbitrary")):
    nk = grid[2]
    n_extra = len(extras)

    def body(*refs):
        a_ref, b_ref = refs[0], refs[1]
        extra = refs[2:2 + n_extra]
        outs = refs[2 + n_extra:-1]
        acc = refs[-1]
        part = lax.dot_general(a_ref[...], b_ref[...], dims, preferred_element_type=F32)
        if nk == 1:
            epilogue(part, extra, outs)
        else:
            k = pl.program_id(2)

            @pl.when(k == 0)
            def _():
                acc[...] = part

            @pl.when(k > 0)
            def _():
                acc[...] += part

            @pl.when(k == nk - 1)
            def _():
                epilogue(acc[...], extra, outs)

    return _pcall(
        body, deps, name=name, grid=grid,
        in_specs=[a_spec, b_spec, *extra_specs],
        out_specs=out_specs, out_shape=out_shape,
        scratch_shapes=[pltpu.VMEM(acc_shape if nk > 1 else (SUBLANES, LANES), F32)],
        compiler_params=_params(semantics),
    )(a, b, *extras)


def _store(dtype):
    def ep(acc, extra, outs):
        outs[0][...] = acc.astype(dtype)
    return ep


def _store_add(acc, extra, outs):
    outs[0][...] = acc + extra[0][...]


def _mm_nn(name, a, b, out_dtype=F32, epilogue=None, extras=(), n_out=1, out_dtypes=None, tm=1024, tn=1024, tk=None,
           deps=()):
    M, K = a.shape
    N = b.shape[1]
    tm, tn, tk = _tile(M, tm), _tile(N, tn), _tile(K, tk or K)
    b_spec = pl.BlockSpec((tk, tn), lambda i, j, k: (k, j))
    grid = (M // tm, N // tn, K // tk)
    o_spec = pl.BlockSpec((tm, tn), lambda i, j, k: (i, j))
    out_dtypes = out_dtypes or [out_dtype] * n_out
    out_shape = [jax.ShapeDtypeStruct((M, N), d) for d in out_dtypes]
    res = _mm(name, a, b, _NN, grid, pl.BlockSpec((tm, tk), lambda i, j, k: (i, k)), b_spec,
              out_shape, [o_spec] * len(out_dtypes), (tm, tn), epilogue or _store(out_dtype),
              extras, [o_spec] * len(extras), deps)
    return res if len(out_dtypes) > 1 else res[0]


def _mm_nt(name, a, b, out_dtype=F32, epilogue=None, extras=(), tm=1024, tn=1024, tk=None, deps=()):
    M, C = a.shape
    N = b.shape[0]
    tm, tn, tk = _tile(M, tm), _tile(N, tn), _tile(C, tk or C)
    b_spec = pl.BlockSpec((tn, tk), lambda i, j, k: (j, k))
    grid = (M // tm, N // tn, C // tk)
    o_spec = pl.BlockSpec((tm, tn), lambda i, j, k: (i, j))
    return _mm(name, a, b, _NT, grid, pl.BlockSpec((tm, tk), lambda i, j, k: (i, k)), b_spec,
               [jax.ShapeDtypeStruct((M, N), out_dtype)], [o_spec], (tm, tn), epilogue or _store(out_dtype),
               extras, [o_spec] * len(extras), deps)[0]


def _mm_tn(name, a, b, out_dtype=BF16, tm=1024, tn=512, tk=None, deps=()):
    T, M = a.shape
    N = b.shape[1]
    tm, tn, tk = _tile(M, tm), _tile(N, tn), _tile(T, tk or T)
    out_shape = jax.ShapeDtypeStruct((M, N), out_dtype)
    o_spec = pl.BlockSpec((tm, tn), lambda i, j, k: (i, j))
    grid = (M // tm, N // tn, T // tk)
    return _mm(name, a, b, _TN, grid, pl.BlockSpec((tk, tm), lambda i, j, k: (k, i)),
               pl.BlockSpec((tk, tn), lambda i, j, k: (k, j)), [out_shape], [o_spec], (tm, tn), _store(out_dtype),
               deps=deps)[0]


def _mean_last(v):
    return jnp.mean(v, axis=-1, keepdims=True)


def _rows_to_sublanes(v):
    r, c = v.shape
    return jnp.sum(v.reshape(r // SUBLANES, SUBLANES, c), axis=0)


def _accumulate(ref, val, first):
    @pl.when(first)
    def _():
        ref[...] = val

    @pl.when(jnp.logical_not(first))
    def _():
        ref[...] += val


def _rms_fwd(name, x, g, tr=512, deps=()):
    T, D = x.shape
    tr = _tile(T, tr)

    def body(x_ref, g_ref, o_ref):
        xv = x_ref[...]
        r = lax.rsqrt(_mean_last(xv * xv) + EPS)
        o_ref[...] = (xv * r * g_ref[...]).astype(BF16)

    row = pl.BlockSpec((tr, D), lambda i: (i, 0))
    return _pcall(
        body, deps, name=name, grid=(T // tr,),
        in_specs=[row, pl.BlockSpec((1, D), lambda i: (0, 0))],
        out_specs=row, out_shape=jax.ShapeDtypeStruct((T, D), BF16),
        compiler_params=_params(("parallel",)),
    )(x, g)


def _rms_bwd(name, dyn, x, g, dres, tr=512, deps=()):
    T, D = x.shape
    tr = _tile(T, tr)

    def body(dy_ref, x_ref, g_ref, dr_ref, dx_ref, dxb_ref, dg_ref):
        xv = x_ref[...]
        r = lax.rsqrt(_mean_last(xv * xv) + EPS)
        xn = xv * r
        dy = dy_ref[...].astype(F32)
        dxn = dy * g_ref[...]
        dx = dr_ref[...] + r * (dxn - xn * _mean_last(dxn * xn))
        dx_ref[...] = dx
        dxb_ref[...] = dx.astype(BF16)
        _accumulate(dg_ref, _rows_to_sublanes(dy * xn), pl.program_id(0) == 0)

    row = pl.BlockSpec((tr, D), lambda i: (i, 0))
    return _pcall(
        body, deps, name=name, grid=(T // tr,),
        in_specs=[row, row, pl.BlockSpec((1, D), lambda i: (0, 0)), row],
        out_specs=[row, row, pl.BlockSpec((SUBLANES, D), lambda i: (0, 0))],
        out_shape=[jax.ShapeDtypeStruct((T, D), F32), jax.ShapeDtypeStruct((T, D), BF16),
                   jax.ShapeDtypeStruct((SUBLANES, D), F32)],
        compiler_params=_params(("arbitrary",)),
    )(dyn, x, g, dres)


def _mm_nn_rms(name, a, b, res, g, tm=512, deps=()):
    M, K = a.shape
    N = b.shape[1]
    tm = _tile(M, tm)

    def epilogue(acc, extra, outs):
        h = acc + extra[0][...]
        outs[0][...] = h
        outs[1][...] = (h * lax.rsqrt(_mean_last(h * h) + EPS) * extra[1][...]).astype(BF16)

    row = pl.BlockSpec((tm, N), lambda i, j, k: (i, 0))
    return _mm(name, a, b, _NN, (M // tm, 1, 1), pl.BlockSpec((tm, K), lambda i, j, k: (i, 0)),
               pl.BlockSpec((K, N), lambda i, j, k: (0, 0)),
               [jax.ShapeDtypeStruct((M, N), F32), jax.ShapeDtypeStruct((M, N), BF16)], [row, row], (tm, N), epilogue,
               (res, g), [row, pl.BlockSpec((1, N), lambda i, j, k: (0, 0))], deps)


def _mm_nt_rms_bwd(name, a, b, x, g, dres, with_bf16=True, tm=256, deps=()):
    M, C = a.shape
    N = b.shape[0]
    tm = _tile(M, tm)

    def epilogue(dy, extra, outs):
        x_ref, dr_ref, g_ref = extra
        xv = x_ref[...]
        r = lax.rsqrt(_mean_last(xv * xv) + EPS)
        xn = xv * r
        dxn = dy * g_ref[...]
        dx = dr_ref[...] + r * (dxn - xn * _mean_last(dxn * xn))
        outs[0][...] = dx
        if with_bf16:
            outs[1][...] = dx.astype(BF16)
        _accumulate(outs[-1], _rows_to_sublanes(dy * xn), pl.program_id(0) == 0)

    row = pl.BlockSpec((tm, N), lambda i, j, k: (i, 0))
    copies = [jax.ShapeDtypeStruct((M, N), F32)] + ([jax.ShapeDtypeStruct((M, N), BF16)] if with_bf16 else [])
    return _mm(name, a, b, _NT, (M // tm, 1, 1), pl.BlockSpec((tm, C), lambda i, j, k: (i, 0)),
               pl.BlockSpec((N, C), lambda i, j, k: (0, 0)),
               copies + [jax.ShapeDtypeStruct((SUBLANES, N), F32)],
               [row] * len(copies) + [pl.BlockSpec((SUBLANES, N), lambda i, j, k: (0, 0))], (tm, N), epilogue,
               (x, dres, g), [row, row, pl.BlockSpec((1, N), lambda i, j, k: (0, 0))], deps,
               semantics=("arbitrary", "arbitrary", "arbitrary"))


def _gate_tail(gn, w_gate, h2, pe, target, g_ple, g_final, tm=256):
    T, D = h2.shape
    tm = _tile(T, tm)

    def epilogue(z, extra, outs):
        h2_ref, pe_ref, t_ref, gp_ref, gf_ref = extra
        dh3_ref, dz_ref, dpe_ref, dgf_ref, dgp_ref, loss_ref = outs
        first = pl.program_id(0) == 0
        pev = pe_ref[...]
        r3 = lax.rsqrt(_mean_last(pev * pev) + EPS)
        en = pev * r3
        e = en * gp_ref[...]
        gate = 1.0 / (1.0 + jnp.exp(-z))
        h3 = h2_ref[...] + gate * e
        r5 = lax.rsqrt(_mean_last(h3 * h3) + EPS)
        hn = h3 * r5
        diff = hn * gf_ref[...] - t_ref[...]
        loss_rows = 0.5 * _mean_last(diff * diff)
        row0 = lax.broadcasted_iota(jnp.int32, (SUBLANES, LANES), 0) == 0
        _accumulate(loss_ref, jnp.where(row0, jnp.sum(loss_rows), 0.0), first)
        dy = diff * (1.0 / D)
        _accumulate(dgf_ref, _rows_to_sublanes(dy * hn), first)
        dhn = dy * gf_ref[...]
        dh3 = r5 * (dhn - hn * _mean_last(dhn * hn))
        dh3_ref[...] = dh3
        dgate = dh3 * e
        de = dh3 * gate
        dz_ref[...] = (dgate * gate * (1.0 - gate)).astype(BF16)
        _accumulate(dgp_ref, _rows_to_sublanes(de * en), first)
        den = de * gp_ref[...]
        dpe_ref[...] = (r3 * (den - en * _mean_last(den * en))).astype(BF16)

    row = pl.BlockSpec((tm, D), lambda i, j, k: (i, 0))
    vec = pl.BlockSpec((1, D), lambda i, j, k: (0, 0))
    part = pl.BlockSpec((SUBLANES, D), lambda i, j, k: (0, 0))
    return _mm("gate_tail", gn, w_gate, _NN, (T // tm, 1, 1), row, pl.BlockSpec(w_gate.shape, lambda i, j, k: (0, 0)),
               [jax.ShapeDtypeStruct((T, D), F32), jax.ShapeDtypeStruct((T, D), BF16),
                jax.ShapeDtypeStruct((T, D), BF16), jax.ShapeDtypeStruct((SUBLANES, D), F32),
                jax.ShapeDtypeStruct((SUBLANES, D), F32), jax.ShapeDtypeStruct((SUBLANES, LANES), F32)],
               [row, row, row, part, part, pl.BlockSpec((SUBLANES, LANES), lambda i, j, k: (0, 0))], (tm, D), epilogue,
               (h2, pe, target, g_ple, g_final), [row, row, row, vec, vec],
               semantics=("arbitrary", "arbitrary", "arbitrary"))


def _rope_tables(T):
    pos = np.arange(T)
    half = HEAD_DIM // 2
    inv = (ROPE_THETA ** (-np.arange(0, half, 2, dtype=np.float32) / half)).astype(np.float32)
    ang_r = (pos // GRID_W).astype(np.float32)[:, None] * inv
    ang_c = (pos % GRID_W).astype(np.float32)[:, None] * inv
    cos = np.concatenate([np.cos(ang_r), np.cos(ang_r), np.cos(ang_c), np.cos(ang_c)], axis=-1)
    sin = np.concatenate([-np.sin(ang_r), np.sin(ang_r), -np.sin(ang_c), np.sin(ang_c)], axis=-1)
    return jnp.asarray(cos, F32), jnp.asarray(sin, F32)


def _swap32(x):
    lane = lax.broadcasted_iota(jnp.int32, x.shape, 1)
    return jnp.where((lane % 64) < 32, pltpu.roll(x, 96, 1), pltpu.roll(x, 32, 1))


def _in_proj(u, w_in, cos, sin, g_q, g_k, n_norm, tm=512):
    T, K = u.shape
    W = w_in.shape[1]
    tm = _tile(T, tm)
    n_q = n_norm * GROUP // (GROUP + 1)
    wa = n_norm * HEAD_DIM

    def epilogue(acc, extra, outs):
        c_ref, s_ref, gq_ref, gk_ref = extra
        raw_ref, o_ref = outs
        c, s = c_ref[...], s_ref[...]
        raw_ref[...] = acc[:, :wa]
        for h in range(n_norm):
            cols = slice(h * HEAD_DIM, (h + 1) * HEAD_DIM)
            xv = acc[:, cols]
            g = gq_ref[...] if h < n_q else gk_ref[...]
            xn = xv * lax.rsqrt(_mean_last(xv * xv) + EPS) * g
            o_ref[:, cols] = (xn * c + _swap32(xn) * s).astype(BF16)
        o_ref[:, wa:] = acc[:, wa:].astype(BF16)

    tab = pl.BlockSpec((tm, HEAD_DIM), lambda i, j, k: (i, 0))
    vec = pl.BlockSpec((1, HEAD_DIM), lambda i, j, k: (0, 0))
    return _mm("in_proj", u, w_in, _NN, (T // tm, 1, 1), pl.BlockSpec((tm, K), lambda i, j, k: (i, 0)),
               pl.BlockSpec((K, W), lambda i, j, k: (0, 0)),
               [jax.ShapeDtypeStruct((T, wa), F32), jax.ShapeDtypeStruct((T, W), BF16)],
               [pl.BlockSpec((tm, wa), lambda i, j, k: (i, 0)), pl.BlockSpec((tm, W), lambda i, j, k: (i, 0))],
               (tm, W), epilogue, (cos, sin, g_q, g_k), [tab, tab, vec, vec])


def _dproj(proj_a, dqa, dka_t, dva_t, dqb, dkb, dvb, cos, sin, g_q, g_k, tr=512):
    T, wa = proj_a.shape
    tr = _tile(T, tr)
    n_q = dqa.shape[1] // HEAD_DIM
    wkv = dka_t.shape[0]
    W = wa + wkv + dqb.shape[1] + dkb.shape[1] + dvb.shape[1]

    def body(p_ref, dqa_ref, dkat_ref, dvat_ref, dqb_ref, dkb_ref, dvb_ref, c_ref, s_ref, gq_ref, gk_ref,
             o_ref, dgq_ref, dgk_ref):
        c, s = c_ref[...], s_ref[...]
        dka = dkat_ref[...].T
        dgq = jnp.zeros((SUBLANES, HEAD_DIM), F32)
        dgk = jnp.zeros((SUBLANES, HEAD_DIM), F32)
        for h in range(wa // HEAD_DIM):
            cols = slice(h * HEAD_DIM, (h + 1) * HEAD_DIM)
            xv = p_ref[:, cols]
            r = lax.rsqrt(_mean_last(xv * xv) + EPS)
            xn = xv * r
            if h < n_q:
                d = dqa_ref[:, cols]
                g = gq_ref[...]
            else:
                d = dka[:, (h - n_q) * HEAD_DIM:(h - n_q + 1) * HEAD_DIM]
                g = gk_ref[...]
            dqn = d * c + _swap32(d * s)
            part = _rows_to_sublanes(dqn * xn)
            if h < n_q:
                dgq = dgq + part
            else:
                dgk = dgk + part
            dxn = dqn * g
            o_ref[:, cols] = (r * (dxn - xn * _mean_last(dxn * xn))).astype(BF16)
        o_ref[:, wa:wa + wkv] = dvat_ref[...].T.astype(BF16)
        off = wa + wkv
        for ref in (dqb_ref, dkb_ref, dvb_ref):
            w = ref.shape[1]
            o_ref[:, off:off + w] = ref[...].astype(BF16)
            off += w
        first = pl.program_id(0) == 0
        _accumulate(dgq_ref, dgq, first)
        _accumulate(dgk_ref, dgk, first)

    def row(w):
        return pl.BlockSpec((tr, w), lambda i: (i, 0))

    col = pl.BlockSpec((wkv, tr), lambda i: (0, i))
    vec = pl.BlockSpec((1, HEAD_DIM), lambda i: (0, 0))
    part = pl.BlockSpec((SUBLANES, HEAD_DIM), lambda i: (0, 0))
    return pl.pallas_call(
        body, name="dproj", grid=(T // tr,),
        in_specs=[row(wa), row(dqa.shape[1]), col, col, row(dqb.shape[1]),
                  row(dkb.shape[1]), row(dvb.shape[1]), row(HEAD_DIM), row(HEAD_DIM), vec, vec],
        out_specs=[row(W), part, part],
        out_shape=[jax.ShapeDtypeStruct((T, W), BF16), jax.ShapeDtypeStruct((SUBLANES, HEAD_DIM), F32),
                   jax.ShapeDtypeStruct((SUBLANES, HEAD_DIM), F32)],
        compiler_params=_params(("arbitrary",)),
    )(proj_a, dqa, dka_t, dva_t, dqb, dkb, dvb, cos, sin, g_q, g_k)


def _attn_a_fwd(pb, n_q, n_kv, out_heads, tq=1024, tc=2048, halves=2):
    T = pb.shape[0]
    tq, tc = _tile(T, tq), _tile(T, tc)
    th = tq // halves
    scale = HEAD_DIM ** -0.5
    c = scale * LOG2E

    def body(q_ref, k_ref, v_ref, o_ref, lse_ref):
        qs = [q_ref[h * th:(h + 1) * th, :] for h in range(halves)]
        m, l, acc = [None] * halves, [None] * halves, [None] * halves
        for j in range(T // tc):
            keys = slice(j * tc, (j + 1) * tc)
            kc, vc = k_ref[keys, :], v_ref[keys, :]
            for h in range(halves):
                s = lax.dot_general(qs[h], kc, _NT, preferred_element_type=F32)
                mj = jnp.max(s, axis=-1, keepdims=True)
                m_new = mj if j == 0 else jnp.maximum(m[h], mj)
                p = jnp.exp2((s - m_new) * c)
                pv = lax.dot_general(p.astype(BF16), vc, _NN, preferred_element_type=F32)
                if j == 0:
                    l[h], acc[h] = jnp.sum(p, axis=-1, keepdims=True), pv
                else:
                    alpha = jnp.exp2((m[h] - m_new) * c)
                    l[h] = alpha * l[h] + jnp.sum(p, axis=-1, keepdims=True)
                    acc[h] = alpha * acc[h] + pv
                m[h] = m_new
        for h in range(halves):
            rows = slice(h * th, (h + 1) * th)
            o_ref[rows, :] = (acc[h] / l[h]).astype(BF16)
            lse_ref[rows, :] = m[h] * scale + jnp.log(l[h])

    return pl.pallas_call(
        body, name="attn_a_fwd", grid=(n_kv, GROUP, T // tq),
        in_specs=[pl.BlockSpec((tq, HEAD_DIM), lambda kv, g, i: (i, kv * GROUP + g)),
                  pl.BlockSpec((T, HEAD_DIM), lambda kv, g, i: (0, n_q + kv)),
                  pl.BlockSpec((T, HEAD_DIM), lambda kv, g, i: (0, n_q + n_kv + kv))],
        out_specs=[pl.BlockSpec((tq, HEAD_DIM), lambda kv, g, i: (i, kv * GROUP + g)),
                   pl.BlockSpec((None, tq, 1), lambda kv, g, i: (kv * GROUP + g, i, 0))],
        out_shape=[jax.ShapeDtypeStruct((T, out_heads * HEAD_DIM), BF16), jax.ShapeDtypeStruct((n_q, T, 1), F32)],
        compiler_params=_params(("parallel", "parallel", "parallel")),
    )(pb, pb, pb)


def _attn_a_bwd(pb, o_cat, d_o, lse, n_q, n_kv, tq=1024, tc=256):
    T = pb.shape[0]
    tq, tc = _tile(T, tq), _tile(T, tc)
    scale = HEAD_DIM ** -0.5
    c = scale * LOG2E

    def body(q_ref, k_ref, v_ref, o_ref, do_ref, lse_ref, dq_ref, dkt_ref, dvt_ref):
        q, do = q_ref[...], do_ref[...]
        qt, dot = q.T, do.T
        delta = jnp.sum(do.astype(F32) * o_ref[...].astype(F32), axis=-1, keepdims=True)
        lse2 = lse_ref[...] * LOG2E

        @pl.when(jnp.logical_and(pl.program_id(1) == 0, pl.program_id(2) == 0))
        def _():
            dkt_ref[...] = jnp.zeros(dkt_ref.shape, F32)
            dvt_ref[...] = jnp.zeros(dvt_ref.shape, F32)

        dq = None
        for j in range(T // tc):
            keys = slice(j * tc, (j + 1) * tc)
            kc, vc = k_ref[keys, :], v_ref[keys, :]
            s = lax.dot_general(q, kc, _NT, preferred_element_type=F32)
            p = jnp.exp2(s * c - lse2)
            dp = lax.dot_general(do, vc, _NT, preferred_element_type=F32)
            ds = (p * (dp - delta) * scale).astype(BF16)
            dqj = lax.dot_general(ds, kc, _NN, preferred_element_type=F32)
            dq = dqj if dq is None else dq + dqj
            dvt_ref[:, keys] += lax.dot_general(dot, p.astype(BF16), _NN, preferred_element_type=F32)
            dkt_ref[:, keys] += lax.dot_general(qt, ds, _NN, preferred_element_type=F32)
        dq_ref[...] = dq

    qmap = lambda kv, g, i: (i, kv * GROUP + g)
    return pl.pallas_call(
        body, name="attn_a_bwd", grid=(n_kv, GROUP, T // tq),
        in_specs=[pl.BlockSpec((tq, HEAD_DIM), qmap),
                  pl.BlockSpec((T, HEAD_DIM), lambda kv, g, i: (0, n_q + kv)),
                  pl.BlockSpec((T, HEAD_DIM), lambda kv, g, i: (0, n_q + n_kv + kv)),
                  pl.BlockSpec((tq, HEAD_DIM), qmap),
                  pl.BlockSpec((tq, HEAD_DIM), qmap),
                  pl.BlockSpec((None, tq, 1), lambda kv, g, i: (kv * GROUP + g, i, 0))],
        out_specs=[pl.BlockSpec((tq, HEAD_DIM), qmap),
                   pl.BlockSpec((HEAD_DIM, T), lambda kv, g, i: (kv, 0)),
                   pl.BlockSpec((HEAD_DIM, T), lambda kv, g, i: (kv, 0))],
        out_shape=[jax.ShapeDtypeStruct((T, n_q * HEAD_DIM), F32),
                   jax.ShapeDtypeStruct((n_kv * HEAD_DIM, T), F32),
                   jax.ShapeDtypeStruct((n_kv * HEAD_DIM, T), F32)],
        compiler_params=_params(("parallel", "arbitrary", "arbitrary")),
    )(pb, pb, pb, o_cat, d_o, lse)


def _bucket_index():
    r = np.arange(BLOCK_Q)[:, None]
    j = np.arange(3 * BLOCK_Q)[None, :]
    rel = (j - BLOCK_Q) - r
    nb = N_BUCKETS // 2
    ret = np.where(rel > 0, nb, 0)
    n = np.abs(rel)
    max_exact = nb // 2
    nf = np.maximum(n, 1).astype(np.float32)
    large = max_exact + (np.log(nf / max_exact) / math.log(MAX_DISTANCE / max_exact) * (nb - max_exact)).astype(np.int32)
    large = np.minimum(large, nb - 1)
    return jnp.asarray(ret + np.where(n < max_exact, n, large), jnp.int32)


def _bias_build(idx, table_flat, n_heads, deps=()):
    def body(idx_ref, tab_ref, o_ref):
        h = pl.program_id(0)
        iv = idx_ref[...]
        acc = jnp.zeros(iv.shape, F32)
        for b in range(N_BUCKETS):
            acc = jnp.where(iv == b, tab_ref[b * n_heads + h], acc)
        r = lax.broadcasted_iota(jnp.int32, iv.shape, 0)
        j = lax.broadcasted_iota(jnp.int32, iv.shape, 1)
        o_ref[...] = jnp.where(jnp.abs(j - BLOCK_Q - r) <= WINDOW, acc, NEG_INF)

    return _pcall(
        body, deps, name="bias_build", grid=(n_heads,),
        in_specs=[pl.BlockSpec(idx.shape, lambda h: (0, 0)), pl.BlockSpec(memory_space=pltpu.SMEM)],
        out_specs=pl.BlockSpec((None,) + idx.shape, lambda h: (h, 0, 0)),
        out_shape=jax.ShapeDtypeStruct((n_heads,) + idx.shape, F32),
        compiler_params=_params(("parallel",)),
    )(idx, table_flat)


def _in_sequence(n, T):
    j = lax.broadcasted_iota(jnp.int32, (GROUP * BLOCK_Q, 3 * BLOCK_Q), 1)
    kabs = n * BLOCK_Q + j - BLOCK_Q
    return (kabs >= 0) & (kabs < T)


def _per_head_rows(values):
    head = lax.broadcasted_iota(jnp.int32, (GROUP * BLOCK_Q, 1), 0) // BLOCK_Q
    col = jnp.zeros((GROUP * BLOCK_Q, 1), F32)
    for g, v in enumerate(values):
        col = jnp.where(head == g, v, col)
    return col


def _band_specs(col, nblk, sb):
    return [pl.BlockSpec((BLOCK_Q, HEAD_DIM), lambda kv, i: (jnp.maximum(sb * i - 1, 0), col(kv))),
            pl.BlockSpec((sb * BLOCK_Q, HEAD_DIM), lambda kv, i: (i, col(kv))),
            pl.BlockSpec((BLOCK_Q, HEAD_DIM), lambda kv, i: (jnp.minimum(sb * i + sb, nblk - 1), col(kv)))]


def _head_specs(base, rows):
    return [pl.BlockSpec((rows, HEAD_DIM), functools.partial(lambda kv, i, g: (i, base + kv * GROUP + g), g=g))
            for g in range(GROUP)]


def _attn_b_fwd(pb, bias, sink, o_all, q_off, n_q, n_kv, deps=(), sb=16):
    T = pb.shape[0]
    nblk = T // BLOCK_Q
    sb = min(sb, nblk)
    tq = sb * BLOCK_Q
    scale = HEAD_DIM ** -0.5

    def body(*refs):
        q_refs = refs[0:GROUP]
        k_refs, v_refs = refs[GROUP:GROUP + 3], refs[GROUP + 3:GROUP + 6]
        bias_ref, sink_ref, o_ref, lse_ref = refs[GROUP + 6:]
        kv, i = pl.program_id(0), pl.program_id(1)
        kb = jnp.concatenate([r[...] for r in k_refs], axis=0)
        vb = jnp.concatenate([r[...] for r in v_refs], axis=0)
        bias_all = bias_ref[...].reshape(GROUP * BLOCK_Q, 3 * BLOCK_Q)
        sk = _per_head_rows([sink_ref[kv * GROUP + g] for g in range(GROUP)])
        for b in range(sb):
            rows = slice(b * BLOCK_Q, (b + 1) * BLOCK_Q)
            kw, vw = kb[b * BLOCK_Q:(b + 3) * BLOCK_Q], vb[b * BLOCK_Q:(b + 3) * BLOCK_Q]
            q = jnp.concatenate([r[rows, :] for r in q_refs], axis=0)
            s = lax.dot_general(q, kw, _NT, preferred_element_type=F32) * scale + bias_all
            if b == 0 or b == sb - 1:
                s = jnp.where(_in_sequence(i * sb + b, T), s, NEG_INF)
            m = jnp.maximum(jnp.max(s, axis=-1, keepdims=True), sk)
            p = jnp.exp(s - m)
            l = jnp.sum(p, axis=-1, keepdims=True) + jnp.exp(sk - m)
            o = (lax.dot_general(p.astype(BF16), vw, _NN, preferred_element_type=F32) / l).astype(BF16)
            lse = m + jnp.log(l)
            for g in range(GROUP):
                head = slice(g * BLOCK_Q, (g + 1) * BLOCK_Q)
                o_ref[rows, g * HEAD_DIM:(g + 1) * HEAD_DIM] = o[head]
                lse_ref[g, rows, :] = lse[head]

    first_group = o_all.shape[1] // (GROUP * HEAD_DIM) - n_kv
    return _pcall(
        body, deps, into=(o_all, 0), name="attn_b_fwd", grid=(n_kv, nblk // sb),
        in_specs=[*_head_specs(q_off, tq),
                  *_band_specs(lambda kv: q_off + n_q + kv, nblk, sb),
                  *_band_specs(lambda kv: q_off + n_q + n_kv + kv, nblk, sb),
                  pl.BlockSpec((GROUP, BLOCK_Q, 3 * BLOCK_Q), lambda kv, i: (kv, 0, 0)),
                  pl.BlockSpec(memory_space=pltpu.SMEM)],
        out_specs=[pl.BlockSpec((tq, GROUP * HEAD_DIM), lambda kv, i: (i, first_group + kv)),
                   pl.BlockSpec((GROUP, tq, 1), lambda kv, i: (kv, i, 0))],
        out_shape=[jax.ShapeDtypeStruct(o_all.shape, BF16), jax.ShapeDtypeStruct((n_q, T, 1), F32)],
        compiler_params=_params(("parallel", "parallel")),
    )(*([pb] * (GROUP + 6)), bias, sink)


def _attn_b_bwd(pb, o_cat, d_o, lse, bias, sink, q_off, n_q, n_kv, o_off, deps=(), sb=16):
    T = pb.shape[0]
    nblk = T // BLOCK_Q
    sb = min(sb, nblk)
    tq = sb * BLOCK_Q
    scale = HEAD_DIM ** -0.5

    def body(*refs):
        q_refs = refs[0:GROUP]
        k_refs, v_refs = refs[GROUP:GROUP + 3], refs[GROUP + 3:GROUP + 6]
        o_refs, do_refs = refs[GROUP + 6:2 * GROUP + 6], refs[2 * GROUP + 6:3 * GROUP + 6]
        lse_ref, bias_ref, sink_ref, dq_ref, dk_ref, dv_ref, dbias_ref, dsink_ref, dkb_ref, dvb_ref = refs[3 * GROUP + 6:]
        kv, i = pl.program_id(0), pl.program_id(1)
        first = i == 0

        @pl.when(first)
        def _():
            dk_ref[...] = jnp.zeros(dk_ref.shape, F32)
            dv_ref[...] = jnp.zeros(dv_ref.shape, F32)
            dbias_ref[...] = jnp.zeros(dbias_ref.shape, F32)

        kb = jnp.concatenate([r[...] for r in k_refs], axis=0)
        vb = jnp.concatenate([r[...] for r in v_refs], axis=0)
        dkb_ref[...] = jnp.zeros(dkb_ref.shape, F32)
        dvb_ref[...] = jnp.zeros(dvb_ref.shape, F32)
        row = lax.broadcasted_iota(jnp.int32, (SUBLANES, LANES), 0)
        dsink = jnp.zeros((SUBLANES, LANES), F32)
        bias_all = bias_ref[...].reshape(GROUP * BLOCK_Q, 3 * BLOCK_Q)
        sk = _per_head_rows([sink_ref[kv * GROUP + g] for g in range(GROUP)])
        for b in range(sb):
            rows = slice(b * BLOCK_Q, (b + 1) * BLOCK_Q)
            win = slice(b * BLOCK_Q, (b + 3) * BLOCK_Q)
            kw, vw = kb[win], vb[win]
            q = jnp.concatenate([r[rows, :] for r in q_refs], axis=0)
            do = jnp.concatenate([r[rows, :] for r in do_refs], axis=0)
            o = jnp.concatenate([r[rows, :] for r in o_refs], axis=0)
            lse = jnp.concatenate([lse_ref[g, rows, :] for g in range(GROUP)], axis=0)
            delta = jnp.sum(do.astype(F32) * o.astype(F32), axis=-1, keepdims=True)
            s = lax.dot_general(q, kw, _NT, preferred_element_type=F32) * scale + bias_all
            if b == 0 or b == sb - 1:
                s = jnp.where(_in_sequence(i * sb + b, T), s, NEG_INF)
            p = jnp.exp(s - lse)
            dp = lax.dot_general(do, vw, _NT, preferred_element_type=F32)
            ds = p * (dp - delta)
            dbias_ref[...] += ds.reshape(GROUP, BLOCK_Q, 3 * BLOCK_Q)
            sunk = jnp.exp(sk - lse) * delta
            for g in range(GROUP):
                dsink = dsink + jnp.where(row == g, -jnp.sum(sunk[g * BLOCK_Q:(g + 1) * BLOCK_Q]), 0.0)
            dsb = (ds * scale).astype(BF16)
            dq = lax.dot_general(dsb, kw, _NN, preferred_element_type=F32).astype(BF16)
            for g in range(GROUP):
                dq_ref[rows, g * HEAD_DIM:(g + 1) * HEAD_DIM] = dq[g * BLOCK_Q:(g + 1) * BLOCK_Q]
            dkb_ref[win, :] += lax.dot_general(dsb, q, _TN, preferred_element_type=F32)
            dvb_ref[win, :] += lax.dot_general(p.astype(BF16), do, _TN, preferred_element_type=F32)
        _accumulate(dsink_ref, dsink, first)

        before = pl.ds(pl.multiple_of(jnp.maximum(sb * i - 1, 0) * BLOCK_Q, BLOCK_Q), BLOCK_Q)
        own = pl.ds(pl.multiple_of(i * tq, BLOCK_Q), tq)
        after = pl.ds(pl.multiple_of(jnp.minimum(sb * i + sb, nblk - 1) * BLOCK_Q, BLOCK_Q), BLOCK_Q)
        for acc_ref, band_ref in ((dk_ref, dkb_ref), (dv_ref, dvb_ref)):
            acc_ref[before, :] += band_ref[0:BLOCK_Q, :]
            acc_ref[own, :] += band_ref[BLOCK_Q:BLOCK_Q + tq, :]
            acc_ref[after, :] += band_ref[BLOCK_Q + tq:, :]

    return _pcall(
        body, deps, name="attn_b_bwd", grid=(n_kv, nblk // sb),
        in_specs=[*_head_specs(q_off, tq),
                  *_band_specs(lambda kv: q_off + n_q + kv, nblk, sb),
                  *_band_specs(lambda kv: q_off + n_q + n_kv + kv, nblk, sb),
                  *_head_specs(o_off, tq), *_head_specs(o_off, tq),
                  pl.BlockSpec((GROUP, tq, 1), lambda kv, i: (kv, i, 0)),
                  pl.BlockSpec((GROUP, BLOCK_Q, 3 * BLOCK_Q), lambda kv, i: (kv, 0, 0)),
                  pl.BlockSpec(memory_space=pltpu.SMEM)],
        out_specs=[pl.BlockSpec((tq, GROUP * HEAD_DIM), lambda kv, i: (i, kv)),
                   pl.BlockSpec((T, HEAD_DIM), lambda kv, i: (0, kv)),
                   pl.BlockSpec((T, HEAD_DIM), lambda kv, i: (0, kv)),
                   pl.BlockSpec((GROUP, BLOCK_Q, 3 * BLOCK_Q), lambda kv, i: (kv, 0, 0)),
                   pl.BlockSpec((None, SUBLANES, LANES), lambda kv, i: (kv, 0, 0))],
        out_shape=[jax.ShapeDtypeStruct((T, n_q * HEAD_DIM), BF16),
                   jax.ShapeDtypeStruct((T, n_kv * HEAD_DIM), F32),
                   jax.ShapeDtypeStruct((T, n_kv * HEAD_DIM), F32),
                   jax.ShapeDtypeStruct((n_q, BLOCK_Q, 3 * BLOCK_Q), F32),
                   jax.ShapeDtypeStruct((n_kv, SUBLANES, LANES), F32)],
        scratch_shapes=[pltpu.VMEM((tq + 2 * BLOCK_Q, HEAD_DIM), F32), pltpu.VMEM((tq + 2 * BLOCK_Q, HEAD_DIM), F32)],
        compiler_params=_params(("parallel", "arbitrary")),
    )(*([pb] * (GROUP + 6)), *([o_cat] * GROUP), *([d_o] * GROUP), lse, bias, sink)


def _table_grads(dbias, dsink_raw, idx):
    n_heads = dbias.shape[0]
    n_kv = dsink_raw.shape[0]

    def body(db_ref, ds_ref, idx_ref, dt_ref, dsk_ref):
        iv = idx_ref[...]
        row = lax.broadcasted_iota(jnp.int32, (SUBLANES, LANES), 0)
        lane = lax.broadcasted_iota(jnp.int32, (SUBLANES, LANES), 1)
        dsk = jnp.zeros((SUBLANES, LANES), F32)
        for h in range(n_heads):
            d = db_ref[h]
            acc = jnp.zeros((SUBLANES, LANES), F32)
            for b in range(N_BUCKETS):
                acc = jnp.where((row == 0) & (lane == b), jnp.sum(jnp.where(iv == b, d, 0.0)), acc)
            dt_ref[:, h * LANES:(h + 1) * LANES] = acc
            raw = ds_ref[h // GROUP]
            val = jnp.sum(jnp.where((row == h % GROUP) & (lane == 0), raw, 0.0))
            dsk = jnp.where((row == 0) & (lane == h), val, dsk)
        dsk_ref[...] = dsk

    return pl.pallas_call(
        body, name="table_grads",
        in_specs=[pl.BlockSpec(memory_space=pltpu.VMEM)] * 3,
        out_specs=[pl.BlockSpec(memory_space=pltpu.VMEM)] * 2,
        out_shape=[jax.ShapeDtypeStruct((SUBLANES, n_heads * LANES), F32),
                   jax.ShapeDtypeStruct((SUBLANES, LANES), F32)],
        compiler_params=_params(),
    )(dbias, dsink_raw, idx)


def _position():
    x, y, c = lax.axis_index("x"), lax.axis_index("y"), lax.axis_index("c")
    return x, y, c


def _hbm(a):
    return pltpu.with_memory_space_constraint(a, pltpu.HBM)


def _split_start(name, bufs, sem_shapes, issue):
    nb, ns = len(bufs), len(sem_shapes)

    def body(*refs):
        buf_refs = refs[:nb]
        sems = refs[nb:nb + ns]
        token = refs[nb + ns + nb]
        issue(buf_refs, sems)
        token[...] = jnp.zeros(token.shape, F32)

    outs = pl.pallas_call(
        body, name=name,
        in_specs=[_HBM] * nb,
        out_specs=[_SEM] * ns + [_HBM] * nb + [_VMEM],
        out_shape=[pltpu.SemaphoreType.DMA(s) for s in sem_shapes] + [pltpu.HBM(b.shape, b.dtype) for b in bufs]
        + [jax.ShapeDtypeStruct((SUBLANES, LANES), F32)],
        input_output_aliases={i: ns + i for i in range(nb)},
        compiler_params=pltpu.CompilerParams(has_side_effects=_EFFECT),
    )(*[_hbm(b) for b in bufs])
    return outs[:ns], outs[ns:ns + nb], outs[-1]


def _split_wait(name, bufs, send, recv, counts, size_of, after):
    nb = len(bufs)

    def body(*refs):
        buf_refs = refs[:nb]
        send_ref, recv_ref = refs[nb], refs[nb + 1]
        x, y, c = _position()
        for w, n in enumerate(counts):
            ref = size_of(buf_refs, w)
            for k in range(n):
                s = sum(counts[:w]) + k
                cp = pltpu.make_async_remote_copy(
                    src_ref=ref, dst_ref=ref, send_sem=send_ref.at[s], recv_sem=recv_ref.at[s],
                    device_id=(x, y, c), device_id_type=MESH)
                cp.wait_send()
                cp.wait_recv()

    return pl.pallas_call(
        body, name=name,
        in_specs=[_HBM] * nb + [_SEM, _SEM, _ANY],
        out_specs=[_HBM] * nb,
        out_shape=[pltpu.HBM(b.shape, b.dtype) for b in bufs],
        input_output_aliases={i: i for i in range(nb)},
        compiler_params=pltpu.CompilerParams(has_side_effects=_EFFECT),
    )(*bufs, send, recv, after)


def _block_of(pos):
    return 4 * pos[0] + 2 * pos[1] + pos[2]


def _shard_of(ref, blk, by_cols):
    aligned = (lambda v, a: v) if isinstance(blk, int) else pl.multiple_of
    if by_cols:
        n = ref.shape[1] // N_DEV
        return ref.at[:, pl.ds(aligned(blk * n, LANES), n)]
    r = ref.shape[0] // N_DEV
    return ref.at[pl.ds(aligned(blk * r, SUBLANES), r), :]


def _place_own(name, land, shard, by_cols, tr=256):
    r, n = shard.shape
    tr = _tile(r, tr)
    mine = _block_of(_position()).astype(jnp.int32).reshape(1)

    def body(m_ref, land_ref, s_ref, o_ref):
        o_ref[...] = s_ref[...]

    if by_cols:
        out = pl.BlockSpec((tr, n), lambda i, m_ref: (i, m_ref[0]))
    else:
        out = pl.BlockSpec((tr, n), lambda i, m_ref: (m_ref[0] * (r // tr) + i, 0))
    return pl.pallas_call(
        body, name=name,
        grid_spec=pltpu.PrefetchScalarGridSpec(
            num_scalar_prefetch=1, grid=(r // tr,),
            in_specs=[_ANY, pl.BlockSpec((tr, n), lambda i, m_ref: (i, 0))], out_specs=out),
        out_shape=jax.ShapeDtypeStruct(land.shape, land.dtype),
        input_output_aliases={1: 0},
        compiler_params=_params(("parallel",)),
    )(mine, land, shard)


def _gather_start(name, shards, by_cols, groups, after=None):
    nw = len(shards)
    lands = [lax.empty((s.shape[0], s.shape[1] * N_DEV) if cols else (s.shape[0] * N_DEV, s.shape[1]), s.dtype)
             for s, cols in zip(shards, by_cols)]
    order = [] if after is None else [after]

    def issue(bufs, sems):
        x, y, c = _position()
        peers = [(x, y, 1 - c), (1 - x, y, c), (x, 1 - y, c), (1 - x, 1 - y, c)]
        for gi, grp in enumerate(groups):
            for wi, w in enumerate(grp):
                for k, peer in enumerate(peers):
                    pltpu.make_async_remote_copy(
                        src_ref=bufs[w], dst_ref=_shard_of(bufs[nw + w], _block_of((x, y, c)), by_cols[w]),
                        send_sem=sems[2 * gi].at[4 * wi + k], recv_sem=sems[2 * gi + 1].at[4 * wi + k],
                        device_id=peer, device_id_type=MESH).start()

    sem_shapes = [(4 * len(g),) for g in groups for _ in range(2)]
    sems, thru, token = _split_start(name, list(shards) + lands + order, sem_shapes, issue)
    return sems, thru[:nw], thru[nw:2 * nw], token


def _gather_forward(name, lands, by_cols):
    nw = len(lands)

    def issue(land, sems):
        x, y, c = _position()
        for w in range(nw):
            for k, chip in enumerate([(1 - x, y), (x, 1 - y), (1 - x, 1 - y)]):
                blk = _shard_of(land[w], _block_of((*chip, c)), by_cols[w])
                pltpu.make_async_remote_copy(
                    src_ref=blk, dst_ref=blk, send_sem=sems[0].at[3 * w + k], recv_sem=sems[1].at[3 * w + k],
                    device_id=(x, y, 1 - c), device_id_type=MESH).start()

    return _split_start(name, lands, [(3 * nw,), (3 * nw,)], issue)


def _first_block(bufs, w, offset=0):
    return bufs[offset + w].at[0]


_PEER_FLIPS = ((0, 0, 1), (1, 0, 0), (1, 0, 1), (0, 1, 0), (0, 1, 1), (1, 1, 0), (1, 1, 1))


def _scatter_start(name, grads, by_cols):
    nw = len(grads)
    lands = []
    for g, cols in zip(grads, by_cols):
        shard = (g.shape[0], g.shape[1] // N_DEV) if cols else (g.shape[0] // N_DEV, g.shape[1])
        lands.append(lax.empty((N_DEV,) + shard, g.dtype))

    def issue(bufs, sems):
        x, y, c = _position()
        flip = lambda v, f: 1 - v if f else v
        for w in range(nw):
            for k, (fx, fy, fc) in enumerate(_PEER_FLIPS):
                peer = (flip(x, fx), flip(y, fy), flip(c, fc))
                pltpu.make_async_remote_copy(
                    src_ref=_shard_of(bufs[w], _block_of(peer), by_cols[w]), dst_ref=bufs[nw + w].at[_block_of((x, y, c))],
                    send_sem=sems[0].at[7 * w + k], recv_sem=sems[1].at[7 * w + k],
                    device_id=peer, device_id_type=MESH).start()

    return _split_start(name, list(grads) + lands, [(7 * nw,), (7 * nw,)], issue)


def _adam(w, g, m, v):
    m = ADAM_B1 * m + (1.0 - ADAM_B1) * g
    v = ADAM_B2 * v + (1.0 - ADAM_B2) * (g * g)
    m_hat = m / (1.0 - ADAM_B1 ** ADAM_STEP)
    v_hat = v / (1.0 - ADAM_B2 ** ADAM_STEP)
    delta = -ADAM_LR * (m_hat / (jnp.sqrt(v_hat) + ADAM_EPS) + ADAM_WD * w)
    return delta, m, v


def _sum_adam(name, landed, grad, by_cols, w, m, v, tr=256):
    R, C = w.shape
    tr = _tile(R, tr if C > 1024 else 2 * tr)
    mine = _block_of(_position()).astype(jnp.int32).reshape(1)

    def body(me_ref, l_ref, own_ref, w_ref, m_ref, v_ref, g_ref, d_ref, nm_ref, nv_ref):
        own = own_ref[...].astype(F32)
        g = None
        for d in range(N_DEV):
            part = jnp.where(me_ref[0] == d, own, l_ref[d].astype(F32))
            g = part if g is None else g + part
        g_ref[...] = g
        d_ref[...], nm_ref[...], nv_ref[...] = _adam(w_ref[...], g, m_ref[...], v_ref[...])

    tile = pl.BlockSpec((tr, C), lambda i, me_ref: (i, 0))
    if by_cols:
        own = pl.BlockSpec((tr, C), lambda i, me_ref: (i, me_ref[0]))
    else:
        own = pl.BlockSpec((tr, C), lambda i, me_ref: (me_ref[0] * (R // tr) + i, 0))
    return pl.pallas_call(
        body, name=name,
        grid_spec=pltpu.PrefetchScalarGridSpec(
            num_scalar_prefetch=1, grid=(R // tr,),
            in_specs=[pl.BlockSpec((N_DEV, tr, C), lambda i, me_ref: (0, i, 0)), own, tile, tile, tile],
            out_specs=[tile] * 4),
        out_shape=[jax.ShapeDtypeStruct((R, C), F32)] * 4,
        compiler_params=_params(("parallel",)),
    )(mine, landed, grad, w, m, v)


def _small_all_reduce(parts, deps=()):
    W = parts.shape[1]

    def body(p_ref, o_ref, slots, send_sems, recv_sems):
        x, y, c = _position()
        me = 4 * x + 2 * y + c
        slots[me] = jnp.sum(p_ref[...], axis=0, keepdims=True)
        peers = [(x, y, 1 - c), (1 - x, y, c), (1 - x, y, 1 - c), (x, 1 - y, c), (x, 1 - y, 1 - c),
                 (1 - x, 1 - y, c), (1 - x, 1 - y, 1 - c)]
        copies = []
        for k, peer in enumerate(peers):
            cp = pltpu.make_async_remote_copy(
                src_ref=slots.at[me], dst_ref=slots.at[me], send_sem=send_sems.at[k], recv_sem=recv_sems.at[k],
                device_id=peer, device_id_type=MESH)
            cp.start()
            copies.append(cp)
        for cp in copies:
            cp.wait()
        total = slots[0]
        for d in range(1, N_DEV):
            total = total + slots[d]
        o_ref[...] = total

    return _pcall(
        body, deps, name="small_all_reduce",
        in_specs=[pl.BlockSpec(memory_space=pltpu.VMEM)], out_specs=pl.BlockSpec(memory_space=pltpu.VMEM),
        out_shape=jax.ShapeDtypeStruct((1, W), F32),
        scratch_shapes=[pltpu.VMEM((N_DEV, 1, W), F32), pltpu.SemaphoreType.DMA((7,)), pltpu.SemaphoreType.DMA((7,))],
    )(parts)


def _adam_small(w, g, m, v):
    def body(w_ref, g_ref, m_ref, v_ref, d_ref, nm_ref, nv_ref):
        d_ref[...], nm_ref[...], nv_ref[...] = _adam(w_ref[...], g_ref[...], m_ref[...], v_ref[...])

    return pl.pallas_call(
        body, name="adam_small",
        in_specs=[pl.BlockSpec(memory_space=pltpu.VMEM)] * 4, out_specs=[pl.BlockSpec(memory_space=pltpu.VMEM)] * 3,
        out_shape=[jax.ShapeDtypeStruct(w.shape, F32)] * 3,
    )(w, g, m, v)


_GATHER_GROUPS = (("w_in",), ("w_out", "w_up", "ple_w"), ("w_down", "w_gate"))
_COL_SHARDED = ("w_in", "w_up", "ple_w")


class _MeshComm:
    def __init__(self, w, mom, var):
        self.w, self.mom, self.var = w, mom, var
        self.out = {}
        self._scatters = {}

    def gather_begin(self):
        self._groups = {}
        token = None
        for tag, first, group_list in (("gather_start0", 0, _GATHER_GROUPS[:1]), ("gather_start1", 1, _GATHER_GROUPS[1:])):
            names = [n for g in group_list for n in g]
            idx = {n: i for i, n in enumerate(names)}
            by_cols = [n in _COL_SHARDED for n in names]
            sems, src, lands, token = _gather_start(tag, [self.w[n].astype(BF16) for n in names], by_cols,
                                                    [[idx[n] for n in g] for g in group_list], token)
            lands = [_place_own("place_" + n, land, s, cols) for n, land, s, cols in zip(names, lands, src, by_cols)]
            for k, g in enumerate(group_list):
                self._groups[first + k] = (sems[2 * k], sems[2 * k + 1], [src[idx[n]] for n in g],
                                           [lands[idx[n]] for n in g])
        return token

    @staticmethod
    def _shard_size(names, offset):
        return lambda bufs, w: _shard_of(bufs[offset + w], 0, names[w] in _COL_SHARDED)

    def gather_arrive(self, gi, after):
        names = _GATHER_GROUPS[gi]
        send, recv, src, lands = self._groups[gi]
        out = _split_wait("gather_arrive%d" % gi, src + lands, send, recv, [4] * len(names),
                          self._shard_size(names, len(names)), after)
        self._arrived = out[len(names):]

    def gather_forward(self, gi):
        by_cols = [n in _COL_SHARDED for n in _GATHER_GROUPS[gi]]
        self._fsems, self._fthru, token = _gather_forward("gather_forward%d" % gi, self._arrived, by_cols)
        return token

    def gather_finish(self, gi, after):
        names = _GATHER_GROUPS[gi]
        out = _split_wait("gather_finish%d" % gi, self._fthru, self._fsems[0], self._fsems[1], [3] * len(names),
                          self._shard_size(names, 0), after)
        return dict(zip(names, out))

    def reduce_begin(self, key, grads):
        names = list(grads)
        sems, thru, token = _scatter_start("scatter_start_" + key, [grads[n] for n in names],
                                           [n in _COL_SHARDED for n in names])
        self._scatters[key] = (names, sems, thru)
        return token

    def reduce_finish(self, key, after):
        names, sems, thru = self._scatters[key]
        nw = len(names)
        out = _split_wait("scatter_wait_" + key, thru, sems[0], sems[1], [N_DEV - 1] * nw,
                          functools.partial(_first_block, offset=nw), after)
        for i, n in enumerate(names):
            self.out[n] = _sum_adam("adam_" + n, out[nw + i], out[i], n in _COL_SHARDED, self.w[n], self.mom[n],
                                    self.var[n])


def _step(x, p, target, gains, comm):
    T, D = x.shape
    n_q = D // (2 * HEAD_DIM)
    n_kv = n_q // GROUP
    cos, sin = _rope_tables(T)
    idx = _bucket_index()

    t = comm.gather_begin()
    u = _rms_fwd("norm_attn", x, gains["attn_norm_g"], deps=(t,))
    comm.gather_arrive(0, u)
    t = comm.gather_forward(0)
    bias = _bias_build(idx, gains["rel_bias_table"].reshape(-1), n_q, deps=(t,))
    full = comm.gather_finish(0, bias)
    proj_a, pb = _in_proj(u, full["w_in"], cos, sin, gains["q_norm_g"], gains["k_norm_g"], n_q + n_kv)
    o_a, lse_a = _attn_a_fwd(pb, n_q, n_kv, 2 * n_q)
    comm.gather_arrive(1, lse_a)
    t = comm.gather_forward(1)
    sink = gains["sink_logits"].reshape(-1)
    b_off = n_q + 2 * n_kv
    o_cat, lse_b = _attn_b_fwd(pb, bias, sink, o_a, b_off, n_q, n_kv, deps=(t,))
    full.update(comm.gather_finish(1, lse_b))
    h1, m_in = _mm_nn_rms("out_proj", o_cat, full["w_out"], x, gains["mlp_norm_g"])

    def up_epilogue(acc, extra, outs):
        outs[0][...] = acc.astype(BF16)
        r = jnp.maximum(acc, 0.0)
        outs[1][...] = (r * r).astype(BF16)

    a_act, f_act = _mm_nn("up_proj", m_in, full["w_up"], epilogue=up_epilogue, out_dtypes=[BF16, BF16], tn=2048)
    comm.gather_arrive(2, f_act)
    t = comm.gather_forward(2)
    p_b = p.astype(BF16)
    pe = _mm_nn("ple_proj", p_b, full["ple_w"], deps=(t,))
    full.update(comm.gather_finish(2, pe))
    h2 = _mm_nn("down_proj", f_act, full["w_down"], epilogue=_store_add, extras=(h1,), tn=512)
    gn = _rms_fwd("norm_gate", h2, gains["gate_norm_g"])

    dh3, dz, dpe, dg_final, dg_ple, loss_part = _gate_tail(gn, full["w_gate"], h2, pe, target, gains["ple_norm_g"],
                                                           gains["final_norm_g"])
    gw_gate = _mm_tn("grad_w_gate", gn, dz, tn=1024)
    gw_ple = _mm_tn("grad_ple_w", p_b, dpe)
    dh2, dh2_b, dg_gate = _mm_nt_rms_bwd("d_gate_in", dz, full["w_gate"], h2, gains["gate_norm_g"], dh3, tm=512)
    gw_down = _mm_tn("grad_w_down", f_act, dh2_b, tn=1024)
    t = comm.reduce_begin("b", dict(w_gate=gw_gate, ple_w=gw_ple, w_down=gw_down))

    def act_bwd(acc, extra, outs):
        outs[0][...] = (acc * (2.0 * jnp.maximum(extra[0][...].astype(F32), 0.0))).astype(BF16)

    da = _mm_nt("d_act", dh2_b, full["w_down"], out_dtype=BF16, epilogue=act_bwd, extras=(a_act,), tn=2048, deps=(t,))
    gw_up = _mm_tn("grad_w_up", m_in, da, tn=1024)
    dm = _mm_nt("d_mlp_in", da, full["w_up"], out_dtype=BF16, tn=512)
    dh1, dh1_b, dg_mlp = _rms_bwd("norm_mlp_bwd", dm, h1, gains["mlp_norm_g"], dh2)
    gw_out = _mm_tn("grad_w_out", o_cat, dh1_b, tn=1024)
    t = comm.reduce_begin("d", dict(w_up=gw_up, w_out=gw_out))
    d_o = _mm_nt("d_attn_out", dh1_b, full["w_out"], out_dtype=BF16, deps=(t,))
    dqa, dka_t, dva_t = _attn_a_bwd(pb, o_cat, d_o, lse_a, n_q, n_kv)
    dqb, dkb, dvb, dbias, dsink_raw = _attn_b_bwd(pb, o_cat, d_o, lse_b, bias, sink, b_off, n_q, n_kv, n_q)
    dtable, dsink = _table_grads(dbias, dsink_raw, idx)
    dproj, dg_q, dg_k = _dproj(proj_a, dqa, dka_t, dva_t, dqb, dkb, dvb, cos, sin, gains["q_norm_g"], gains["k_norm_g"])
    gw_in = _mm_tn("grad_w_in", u, dproj, tn=1024)
    t = comm.reduce_begin("e", dict(w_in=gw_in))
    dx, dg_attn = _mm_nt_rms_bwd("d_attn_in", dproj, full["w_in"], x, gains["attn_norm_g"], dh1, with_bf16=False,
                                 tm=512, deps=(t,))
    for key in "bd":
        comm.reduce_finish(key, dx)

    parts = jnp.concatenate([dg_attn, dg_mlp, dg_ple, dg_gate, dg_final, dg_q, dg_k, dtable, dsink, loss_part], axis=1)
    return dx, parts


_SHARDED = ("w_in", "w_out", "w_up", "w_down", "ple_w", "w_gate")
_VECTORS = ("attn_norm_g", "mlp_norm_g", "ple_norm_g", "gate_norm_g", "final_norm_g")
_ORDER = ("attn_norm_g", "w_in", "q_norm_g", "k_norm_g", "sink_logits", "w_out", "mlp_norm_g", "w_up", "w_down",
          "ple_w", "ple_norm_g", "gate_norm_g", "w_gate", "rel_bias_table", "final_norm_g")


def _pack_small(vals, n_heads):
    lane_pad = lambda v: jnp.pad(v, ((0, 0), (0, LANES - v.shape[1])))
    table = lane_pad(vals["rel_bias_table"].T).reshape(1, n_heads * LANES)
    return jnp.concatenate(
        [vals[n].reshape(1, -1) for n in _VECTORS] + [vals["q_norm_g"], vals["k_norm_g"], table,
                                                      lane_pad(vals["sink_logits"]), jnp.zeros((1, LANES), F32)], axis=1)


def _unpack_small(row, like, n_heads):
    out, off = {}, 0
    for n in _VECTORS:
        out[n] = row[:, off:off + like[n].size].reshape(like[n].shape)
        off += like[n].size
    for n in ("q_norm_g", "k_norm_g"):
        out[n] = row[:, off:off + LANES]
        off += LANES
    out["rel_bias_table"] = row[:, off:off + n_heads * LANES].reshape(n_heads, LANES)[:, :N_BUCKETS].T
    off += n_heads * LANES
    out["sink_logits"] = row[:, off:off + n_heads]
    off += LANES
    return out, row[0, off]


def kernel(x, p, attn_norm_g, w_in, q_norm_g, k_norm_g, sink_logits, w_out, mlp_norm_g, w_up, w_down, ple_w, ple_norm_g, gate_norm_g, w_gate, rel_bias_table, final_norm_g, loss_target, m_attn_norm_g, m_w_in, m_q_norm_g, m_k_norm_g, m_sink_logits, m_w_out, m_mlp_norm_g, m_w_up, m_w_down, m_ple_w, m_ple_norm_g, m_gate_norm_g, m_w_gate, m_rel_bias_table, m_final_norm_g, v_attn_norm_g, v_w_in, v_q_norm_g, v_k_norm_g, v_sink_logits, v_w_out, v_mlp_norm_g, v_w_up, v_w_down, v_ple_w, v_ple_norm_g, v_gate_norm_g, v_w_gate, v_rel_bias_table, v_final_norm_g):
    w = dict(attn_norm_g=attn_norm_g, w_in=w_in[0], q_norm_g=q_norm_g, k_norm_g=k_norm_g, sink_logits=sink_logits,
             w_out=w_out[0], mlp_norm_g=mlp_norm_g, w_up=w_up[0], w_down=w_down[0], ple_w=ple_w[0],
             ple_norm_g=ple_norm_g, gate_norm_g=gate_norm_g, w_gate=w_gate[0], rel_bias_table=rel_bias_table,
             final_norm_g=final_norm_g)
    mom = dict(attn_norm_g=m_attn_norm_g, w_in=m_w_in[0], q_norm_g=m_q_norm_g, k_norm_g=m_k_norm_g,
               sink_logits=m_sink_logits, w_out=m_w_out[0], mlp_norm_g=m_mlp_norm_g, w_up=m_w_up[0],
               w_down=m_w_down[0], ple_w=m_ple_w[0], ple_norm_g=m_ple_norm_g, gate_norm_g=m_gate_norm_g,
               w_gate=m_w_gate[0], rel_bias_table=m_rel_bias_table, final_norm_g=m_final_norm_g)
    var = dict(attn_norm_g=v_attn_norm_g, w_in=v_w_in[0], q_norm_g=v_q_norm_g, k_norm_g=v_k_norm_g,
               sink_logits=v_sink_logits, w_out=v_w_out[0], mlp_norm_g=v_mlp_norm_g, w_up=v_w_up[0],
               w_down=v_w_down[0], ple_w=v_ple_w[0], ple_norm_g=v_ple_norm_g, gate_norm_g=v_gate_norm_g,
               w_gate=v_w_gate[0], rel_bias_table=v_rel_bias_table, final_norm_g=v_final_norm_g)
    D = x.shape[-1]
    n_heads = D // (2 * HEAD_DIM)

    gains = {n: w[n] for n in w if n not in _SHARDED}
    gains["final_norm_g"] = final_norm_g.reshape(1, -1)

    comm = _MeshComm(w, mom, var)
    dx, parts = _step(x[0], p[0, 0], loss_target[0], gains, comm)

    small_g = _small_all_reduce(parts, deps=[comm.out[n][0] for n in comm.out])
    comm.reduce_finish("e", small_g)

    g_out, d_out, m_out, v_out = {}, {}, {}, {}
    for n in _SHARDED:
        g, d, nm, nv = comm.out[n]
        g_out[n], d_out[n], m_out[n], v_out[n] = g[None], d[None], nm[None], nv[None]

    small = {n: v for n, v in w.items() if n not in _SHARDED}
    pack = lambda vals: _pack_small({n: vals[n] for n in small}, n_heads)
    sd, sm, sv = _adam_small(pack(w), small_g, pack(mom), pack(var))
    sg, loss = _unpack_small(small_g, small, n_heads)
    g_out.update(sg)
    for dst, row in ((d_out, sd), (m_out, sm), (v_out, sv)):
        dst.update(_unpack_small(row, small, n_heads)[0])

    return (loss, dx[None], *[g_out[n] for n in _ORDER], *[d_out[n] for n in _ORDER],
            *[m_out[n] for n in _ORDER], *[v_out[n] for n in _ORDER])
```

```python
import functools
import math

import numpy as np
import jax
import jax.numpy as jnp
from jax import lax
from jax.experimental import pallas as pl
from jax.experimental.pallas import tpu as pltpu

F32 = jnp.float32
BF16 = jnp.bfloat16

N_DEV = 8
N_CHIP = 4
HEAD_DIM = 128
GROUP = 4
GRID_W = 64
WINDOW = 128
BLOCK_Q = 128
N_BUCKETS = 32
MAX_DISTANCE = 128
ROPE_THETA = 10000.0
EPS = 1e-6
NEG_INF = -1e30
ADAM_LR = 0.001
ADAM_B1 = 0.9
ADAM_B2 = 0.999
ADAM_EPS = 1e-08
ADAM_WD = 0.01
ADAM_STEP = 10
LOG2E = math.log2(math.e)
LANES = 128
SUBLANES = 8
VMEM_LIMIT_BYTES = 60 * 1024 * 1024
MESH = pl.DeviceIdType.MESH

_NT = (((1,), (1,)), ((), ()))
_NN = (((1,), (0,)), ((), ()))
_TN = (((0,), (0,)), ((), ()))


def _tile(dim, pref):
    return pref if dim % pref == 0 else dim


def _params(sem=None):
    return pltpu.CompilerParams(dimension_semantics=sem, vmem_limit_bytes=VMEM_LIMIT_BYTES)


_HBM = pl.BlockSpec(memory_space=pltpu.HBM)
_SEM = pl.BlockSpec(memory_space=pltpu.SEMAPHORE)
_ANY = pl.BlockSpec(memory_space=pl.ANY)
_VMEM = pl.BlockSpec(memory_space=pltpu.VMEM)
_EFFECT = pltpu.SideEffectType.DATAFLOW_SIDE_EFFECTING


def _pcall(body, deps=(), *, in_specs, into=None, **kw):
    deps = [d for d in deps if d is not None]
    nd = len(deps)
    if into is not None:
        deps = [into[0]] + deps
        nd += 1
        kw["input_output_aliases"] = {0: into[1]}

    def wrapped(*refs):
        body(*refs[nd:])

    call = pl.pallas_call(wrapped, in_specs=[_ANY] * nd + list(in_specs), **kw)
    return lambda *args: call(*deps, *args)


def _mm(name, a, b, dims, grid, a_spec, b_spec, out_shape, out_specs, acc_shape, epilogue,
        extras=(), extra_specs=(), deps=(), semantics=("parallel", "parallel", "arbitrary")):
    nk = grid[2]
    n_extra = len(extras)

    def body(*refs):
        a_ref, b_ref = refs[0], refs[1]
        extra = refs[2:2 + n_extra]
        outs = refs[2 + n_extra:-1]
        acc = refs[-1]
        part = lax.dot_general(a_ref[...], b_ref[...], dims, preferred_element_type=F32)
        if nk == 1:
            epilogue(part, extra, outs)
        else:
            k = pl.program_id(2)

            @pl.when(k == 0)
            def _():
                acc[...] = part

            @pl.when(k > 0)
            def _():
                acc[...] += part

            @pl.when(k == nk - 1)
            def _():
                epilogue(acc[...], extra, outs)

    return _pcall(
        body, deps, name=name, grid=grid,
        in_specs=[a_spec, b_spec, *extra_specs],
        out_specs=out_specs, out_shape=out_shape,
        scratch_shapes=[pltpu.VMEM(acc_shape if nk > 1 else (SUBLANES, LANES), F32)],
        compiler_params=_params(semantics),
    )(a, b, *extras)


def _store(dtype):
    def ep(acc, extra, outs):
        outs[0][...] = acc.astype(dtype)
    return ep


def _store_add(acc, extra, outs):
    outs[0][...] = acc + extra[0][...]


def _mm_nn(name, a, b, out_dtype=F32, epilogue=None, extras=(), n_out=1, out_dtypes=None, tm=1024, tn=1024, tk=None,
           deps=()):
    M, K = a.shape
    N = b.shape[1]
    tm, tn, tk = _tile(M, tm), _tile(N, tn), _tile(K, tk or K)
    b_spec = pl.BlockSpec((tk, tn), lambda i, j, k: (k, j))
    grid = (M // tm, N // tn, K // tk)
    o_spec = pl.BlockSpec((tm, tn), lambda i, j, k: (i, j))
    out_dtypes = out_dtypes or [out_dtype] * n_out
    out_shape = [jax.ShapeDtypeStruct((M, N), d) for d in out_dtypes]
    res = _mm(name, a, b, _NN, grid, pl.BlockSpec((tm, tk), lambda i, j, k: (i, k)), b_spec,
              out_shape, [o_spec] * len(out_dtypes), (tm, tn), epilogue or _store(out_dtype),
              extras, [o_spec] * len(extras), deps)
    return res if len(out_dtypes) > 1 else res[0]


def _mm_nt(name, a, b, out_dtype=F32, epilogue=None, extras=(), tm=1024, tn=1024, tk=None, deps=()):
    M, C = a.shape
    N = b.shape[0]
    tm, tn, tk = _tile(M, tm), _tile(N, tn), _tile(C, tk or C)
    b_spec = pl.BlockSpec((tn, tk), lambda i, j, k: (j, k))
    grid = (M // tm, N // tn, C // tk)
    o_spec = pl.BlockSpec((tm, tn), lambda i, j, k: (i, j))
    return _mm(name, a, b, _NT, grid, pl.BlockSpec((tm, tk), lambda i, j, k: (i, k)), b_spec,
               [jax.ShapeDtypeStruct((M, N), out_dtype)], [o_spec], (tm, tn), epilogue or _store(out_dtype),
               extras, [o_spec] * len(extras), deps)[0]


def _mm_tn(name, a, b, out_dtype=BF16, tm=1024, tn=512, tk=None, deps=()):
    T, M = a.shape
    N = b.shape[1]
    tm, tn, tk = _tile(M, tm), _tile(N, tn), _tile(T, tk or T)
    out_shape = jax.ShapeDtypeStruct((M, N), out_dtype)
    o_spec = pl.BlockSpec((tm, tn), lambda i, j, k: (i, j))
    grid = (M // tm, N // tn, T // tk)
    return _mm(name, a, b, _TN, grid, pl.BlockSpec((tk, tm), lambda i, j, k: (k, i)),
               pl.BlockSpec((tk, tn), lambda i, j, k: (k, j)), [out_shape], [o_spec], (tm, tn), _store(out_dtype),
               deps=deps)[0]


def _mean_last(v):
    return jnp.mean(v, axis=-1, keepdims=True)


def _rows_to_sublanes(v):
    r, c = v.shape
    return jnp.sum(v.reshape(r // SUBLANES, SUBLANES, c), axis=0)


def _accumulate(ref, val, first):
    @pl.when(first)
    def _():
        ref[...] = val

    @pl.when(jnp.logical_not(first))
    def _():
        ref[...] += val


def _rms_fwd(name, x, g, tr=512, deps=()):
    T, D = x.shape
    tr = _tile(T, tr)

    def body(x_ref, g_ref, o_ref):
        xv = x_ref[...]
        r = lax.rsqrt(_mean_last(xv * xv) + EPS)
        o_ref[...] = (xv * r * g_ref[...]).astype(BF16)

    row = pl.BlockSpec((tr, D), lambda i: (i, 0))
    return _pcall(
        body, deps, name=name, grid=(T // tr,),
        in_specs=[row, pl.BlockSpec((1, D), lambda i: (0, 0))],
        out_specs=row, out_shape=jax.ShapeDtypeStruct((T, D), BF16),
        compiler_params=_params(("parallel",)),
    )(x, g)


def _rms_bwd(name, dyn, x, g, dres, tr=512, deps=()):
    T, D = x.shape
    tr = _tile(T, tr)

    def body(dy_ref, x_ref, g_ref, dr_ref, dx_ref, dxb_ref, dg_ref):
        xv = x_ref[...]
        r = lax.rsqrt(_mean_last(xv * xv) + EPS)
        xn = xv * r
        dy = dy_ref[...].astype(F32)
        dxn = dy * g_ref[...]
        dx = dr_ref[...] + r * (dxn - xn * _mean_last(dxn * xn))
        dx_ref[...] = dx
        dxb_ref[...] = dx.astype(BF16)
        _accumulate(dg_ref, _rows_to_sublanes(dy * xn), pl.program_id(0) == 0)

    row = pl.BlockSpec((tr, D), lambda i: (i, 0))
    return _pcall(
        body, deps, name=name, grid=(T // tr,),
        in_specs=[row, row, pl.BlockSpec((1, D), lambda i: (0, 0)), row],
        out_specs=[row, row, pl.BlockSpec((SUBLANES, D), lambda i: (0, 0))],
        out_shape=[jax.ShapeDtypeStruct((T, D), F32), jax.ShapeDtypeStruct((T, D), BF16),
                   jax.ShapeDtypeStruct((SUBLANES, D), F32)],
        compiler_params=_params(("arbitrary",)),
    )(dyn, x, g, dres)


def _mm_nn_rms(name, a, b, res, g, tm=512, deps=()):
    M, K = a.shape
    N = b.shape[1]
    tm = _tile(M, tm)

    def epilogue(acc, extra, outs):
        h = acc + extra[0][...]
        outs[0][...] = h
        outs[1][...] = (h * lax.rsqrt(_mean_last(h * h) + EPS) * extra[1][...]).astype(BF16)

    row = pl.BlockSpec((tm, N), lambda i, j, k: (i, 0))
    return _mm(name, a, b, _NN, (M // tm, 1, 1), pl.BlockSpec((tm, K), lambda i, j, k: (i, 0)),
               pl.BlockSpec((K, N), lambda i, j, k: (0, 0)),
               [jax.ShapeDtypeStruct((M, N), F32), jax.ShapeDtypeStruct((M, N), BF16)], [row, row], (tm, N), epilogue,
               (res, g), [row, pl.BlockSpec((1, N), lambda i, j, k: (0, 0))], deps)


def _mm_nt_rms_bwd(name, a, b, x, g, dres, with_bf16=True, tm=256, deps=()):
    M, C = a.shape
    N = b.shape[0]
    tm = _tile(M, tm)

    def epilogue(dy, extra, outs):
        x_ref, dr_ref, g_ref = extra
        xv = x_ref[...]
        r = lax.rsqrt(_mean_last(xv * xv) + EPS)
        xn = xv * r
        dxn = dy * g_ref[...]
        dx = dr_ref[...] + r * (dxn - xn * _mean_last(dxn * xn))
        outs[0][...] = dx
        if with_bf16:
            outs[1][...] = dx.astype(BF16)
        _accumulate(outs[-1], _rows_to_sublanes(dy * xn), pl.program_id(0) == 0)

    row = pl.BlockSpec((tm, N), lambda i, j, k: (i, 0))
    copies = [jax.ShapeDtypeStruct((M, N), F32)] + ([jax.ShapeDtypeStruct((M, N), BF16)] if with_bf16 else [])
    return _mm(name, a, b, _NT, (M // tm, 1, 1), pl.BlockSpec((tm, C), lambda i, j, k: (i, 0)),
               pl.BlockSpec((N, C), lambda i, j, k: (0, 0)),
               copies + [jax.ShapeDtypeStruct((SUBLANES, N), F32)],
               [row] * len(copies) + [pl.BlockSpec((SUBLANES, N), lambda i, j, k: (0, 0))], (tm, N), epilogue,
               (x, dres, g), [row, row, pl.BlockSpec((1, N), lambda i, j, k: (0, 0))], deps,
               semantics=("arbitrary", "arbitrary", "arbitrary"))


def _gate_tail(gn, w_gate, h2, pe, target, g_ple, g_final, tm=256):
    T, D = h2.shape
    tm = _tile(T, tm)

    def epilogue(z, extra, outs):
        h2_ref, pe_ref, t_ref, gp_ref, gf_ref = extra
        dh3_ref, dz_ref, dpe_ref, dgf_ref, dgp_ref, loss_ref = outs
        first = pl.program_id(0) == 0
        pev = pe_ref[...]
        r3 = lax.rsqrt(_mean_last(pev * pev) + EPS)
        en = pev * r3
        e = en * gp_ref[...]
        gate = 1.0 / (1.0 + jnp.exp(-z))
        h3 = h2_ref[...] + gate * e
        r5 = lax.rsqrt(_mean_last(h3 * h3) + EPS)
        hn = h3 * r5
        diff = hn * gf_ref[...] - t_ref[...]
        loss_rows = 0.5 * _mean_last(diff * diff)
        row0 = lax.broadcasted_iota(jnp.int32, (SUBLANES, LANES), 0) == 0
        _accumulate(loss_ref, jnp.where(row0, jnp.sum(loss_rows), 0.0), first)
        dy = diff * (1.0 / D)
        _accumulate(dgf_ref, _rows_to_sublanes(dy * hn), first)
        dhn = dy * gf_ref[...]
        dh3 = r5 * (dhn - hn * _mean_last(dhn * hn))
        dh3_ref[...] = dh3
        dgate = dh3 * e
        de = dh3 * gate
        dz_ref[...] = (dgate * gate * (1.0 - gate)).astype(BF16)
        _accumulate(dgp_ref, _rows_to_sublanes(de * en), first)
        den = de * gp_ref[...]
        dpe_ref[...] = (r3 * (den - en * _mean_last(den * en))).astype(BF16)

    row = pl.BlockSpec((tm, D), lambda i, j, k: (i, 0))
    vec = pl.BlockSpec((1, D), lambda i, j, k: (0, 0))
    part = pl.BlockSpec((SUBLANES, D), lambda i, j, k: (0, 0))
    return _mm("gate_tail", gn, w_gate, _NN, (T // tm, 1, 1), row, pl.BlockSpec(w_gate.shape, lambda i, j, k: (0, 0)),
               [jax.ShapeDtypeStruct((T, D), F32), jax.ShapeDtypeStruct((T, D), BF16),
                jax.ShapeDtypeStruct((T, D), BF16), jax.ShapeDtypeStruct((SUBLANES, D), F32),
                jax.ShapeDtypeStruct((SUBLANES, D), F32), jax.ShapeDtypeStruct((SUBLANES, LANES), F32)],
               [row, row, row, part, part, pl.BlockSpec((SUBLANES, LANES), lambda i, j, k: (0, 0))], (tm, D), epilogue,
               (h2, pe, target, g_ple, g_final), [row, row, row, vec, vec],
               semantics=("arbitrary", "arbitrary", "arbitrary"))


def _rope_tables(T):
    pos = np.arange(T)
    half = HEAD_DIM // 2
    inv = (ROPE_THETA ** (-np.arange(0, half, 2, dtype=np.float32) / half)).astype(np.float32)
    ang_r = (pos // GRID_W).astype(np.float32)[:, None] * inv
    ang_c = (pos % GRID_W).astype(np.float32)[:, None] * inv
    cos = np.concatenate([np.cos(ang_r), np.cos(ang_r), np.cos(ang_c), np.cos(ang_c)], axis=-1)
    sin = np.concatenate([-np.sin(ang_r), np.sin(ang_r), -np.sin(ang_c), np.sin(ang_c)], axis=-1)
    return jnp.asarray(cos, F32), jnp.asarray(sin, F32)


def _swap32(x):
    lane = lax.broadcasted_iota(jnp.int32, x.shape, 1)
    return jnp.where((lane % 64) < 32, pltpu.roll(x, 96, 1), pltpu.roll(x, 32, 1))


def _in_proj(u, w_in, cos, sin, g_q, g_k, n_norm, tm=512):
    T, K = u.shape
    W = w_in.shape[1]
    tm = _tile(T, tm)
    n_q = n_norm * GROUP // (GROUP + 1)
    wa = n_norm * HEAD_DIM

    def epilogue(acc, extra, outs):
        c_ref, s_ref, gq_ref, gk_ref = extra
        raw_ref, o_ref = outs
        c, s = c_ref[...], s_ref[...]
        raw_ref[...] = acc[:, :wa]
        for h in range(n_norm):
            cols = slice(h * HEAD_DIM, (h + 1) * HEAD_DIM)
            xv = acc[:, cols]
            g = gq_ref[...] if h < n_q else gk_ref[...]
            xn = xv * lax.rsqrt(_mean_last(xv * xv) + EPS) * g
            o_ref[:, cols] = (xn * c + _swap32(xn) * s).astype(BF16)
        o_ref[:, wa:] = acc[:, wa:].astype(BF16)

    tab = pl.BlockSpec((tm, HEAD_DIM), lambda i, j, k: (i, 0))
    vec = pl.BlockSpec((1, HEAD_DIM), lambda i, j, k: (0, 0))
    return _mm("in_proj", u, w_in, _NN, (T // tm, 1, 1), pl.BlockSpec((tm, K), lambda i, j, k: (i, 0)),
               pl.BlockSpec((K, W), lambda i, j, k: (0, 0)),
               [jax.ShapeDtypeStruct((T, wa), F32), jax.ShapeDtypeStruct((T, W), BF16)],
               [pl.BlockSpec((tm, wa), lambda i, j, k: (i, 0)), pl.BlockSpec((tm, W), lambda i, j, k: (i, 0))],
               (tm, W), epilogue, (cos, sin, g_q, g_k), [tab, tab, vec, vec])


def _dproj(proj_a, dqa, dka_t, dva_t, dqb, dkb, dvb, cos, sin, g_q, g_k, tr=512):
    T, wa = proj_a.shape
    tr = _tile(T, tr)
    n_q = dqa.shape[1] // HEAD_DIM
    wkv = dka_t.shape[0]
    W = wa + wkv + dqb.shape[1] + dkb.shape[1] + dvb.shape[1]

    def body(p_ref, dqa_ref, dkat_ref, dvat_ref, dqb_ref, dkb_ref, dvb_ref, c_ref, s_ref, gq_ref, gk_ref,
             o_ref, dgq_ref, dgk_ref):
        c, s = c_ref[...], s_ref[...]
        dka = dkat_ref[...].T
        dgq = jnp.zeros((SUBLANES, HEAD_DIM), F32)
        dgk = jnp.zeros((SUBLANES, HEAD_DIM), F32)
        for h in range(wa // HEAD_DIM):
            cols = slice(h * HEAD_DIM, (h + 1) * HEAD_DIM)
            xv = p_ref[:, cols]
            r = lax.rsqrt(_mean_last(xv * xv) + EPS)
            xn = xv * r
            if h < n_q:
                d = dqa_ref[:, cols]
                g = gq_ref[...]
            else:
                d = dka[:, (h - n_q) * HEAD_DIM:(h - n_q + 1) * HEAD_DIM]
                g = gk_ref[...]
            dqn = d * c + _swap32(d * s)
            part = _rows_to_sublanes(dqn * xn)
            if h < n_q:
                dgq = dgq + part
            else:
                dgk = dgk + part
            dxn = dqn * g
            o_ref[:, cols] = (r * (dxn - xn * _mean_last(dxn * xn))).astype(BF16)
        o_ref[:, wa:wa + wkv] = dvat_ref[...].T.astype(BF16)
        off = wa + wkv
        for ref in (dqb_ref, dkb_ref, dvb_ref):
            w = ref.shape[1]
            o_ref[:, off:off + w] = ref[...].astype(BF16)
            off += w
        first = pl.program_id(0) == 0
        _accumulate(dgq_ref, dgq, first)
        _accumulate(dgk_ref, dgk, first)

    def row(w):
        return pl.BlockSpec((tr, w), lambda i: (i, 0))

    col = pl.BlockSpec((wkv, tr), lambda i: (0, i))
    vec = pl.BlockSpec((1, HEAD_DIM), lambda i: (0, 0))
    part = pl.BlockSpec((SUBLANES, HEAD_DIM), lambda i: (0, 0))
    return pl.pallas_call(
        body, name="dproj", grid=(T // tr,),
        in_specs=[row(wa), row(dqa.shape[1]), col, col, row(dqb.shape[1]),
                  row(dkb.shape[1]), row(dvb.shape[1]), row(HEAD_DIM), row(HEAD_DIM), vec, vec],
        out_specs=[row(W), part, part],
        out_shape=[jax.ShapeDtypeStruct((T, W), BF16), jax.ShapeDtypeStruct((SUBLANES, HEAD_DIM), F32),
                   jax.ShapeDtypeStruct((SUBLANES, HEAD_DIM), F32)],
        compiler_params=_params(("arbitrary",)),
    )(proj_a, dqa, dka_t, dva_t, dqb, dkb, dvb, cos, sin, g_q, g_k)


def _attn_a_fwd(pb, n_q, n_kv, out_heads, tq=1024, tc=2048, halves=2):
    T = pb.shape[0]
    tq, tc = _tile(T, tq), _tile(T, tc)
    th = tq // halves
    scale = HEAD_DIM ** -0.5
    c = scale * LOG2E

    def body(q_ref, k_ref, v_ref, o_ref, lse_ref):
        qs = [q_ref[h * th:(h + 1) * th, :] for h in range(halves)]
        m, l, acc = [None] * halves, [None] * halves, [None] * halves
        for j in range(T // tc):
            keys = slice(j * tc, (j + 1) * tc)
            kc, vc = k_ref[keys, :], v_ref[keys, :]
            for h in range(halves):
                s = lax.dot_general(qs[h], kc, _NT, preferred_element_type=F32)
                mj = jnp.max(s, axis=-1, keepdims=True)
                m_new = mj if j == 0 else jnp.maximum(m[h], mj)
                p = jnp.exp2((s - m_new) * c)
                pv = lax.dot_general(p.astype(BF16), vc, _NN, preferred_element_type=F32)
                if j == 0:
                    l[h], acc[h] = jnp.sum(p, axis=-1, keepdims=True), pv
                else:
                    alpha = jnp.exp2((m[h] - m_new) * c)
                    l[h] = alpha * l[h] + jnp.sum(p, axis=-1, keepdims=True)
                    acc[h] = alpha * acc[h] + pv
                m[h] = m_new
        for h in range(halves):
            rows = slice(h * th, (h + 1) * th)
            o_ref[rows, :] = (acc[h] / l[h]).astype(BF16)
            lse_ref[rows, :] = m[h] * scale + jnp.log(l[h])

    return pl.pallas_call(
        body, name="attn_a_fwd", grid=(n_kv, GROUP, T // tq),
        in_specs=[pl.BlockSpec((tq, HEAD_DIM), lambda kv, g, i: (i, kv * GROUP + g)),
                  pl.BlockSpec((T, HEAD_DIM), lambda kv, g, i: (0, n_q + kv)),
                  pl.BlockSpec((T, HEAD_DIM), lambda kv, g, i: (0, n_q + n_kv + kv))],
        out_specs=[pl.BlockSpec((tq, HEAD_DIM), lambda kv, g, i: (i, kv * GROUP + g)),
                   pl.BlockSpec((None, tq, 1), lambda kv, g, i: (kv * GROUP + g, i, 0))],
        out_shape=[jax.ShapeDtypeStruct((T, out_heads * HEAD_DIM), BF16), jax.ShapeDtypeStruct((n_q, T, 1), F32)],
        compiler_params=_params(("parallel", "parallel", "parallel")),
    )(pb, pb, pb)


def _attn_a_bwd(pb, o_cat, d_o, lse, n_q, n_kv, tq=2048, tc=256, halves=2):
    T = pb.shape[0]
    tq, tc = _tile(T, tq), _tile(T, tc)
    th = tq // halves
    scale = HEAD_DIM ** -0.5
    c = scale * LOG2E

    def body(q_ref, k_ref, v_ref, o_ref, do_ref, lse_ref, dq_ref, dkt_ref, dvt_ref):
        @pl.when(jnp.logical_and(pl.program_id(1) == 0, pl.program_id(2) == 0))
        def _():
            dkt_ref[...] = jnp.zeros(dkt_ref.shape, F32)
            dvt_ref[...] = jnp.zeros(dvt_ref.shape, F32)

        groups = []
        for h in range(halves):
            rows = slice(h * th, (h + 1) * th)
            q, do = q_ref[rows, :], do_ref[rows, :]
            delta = jnp.sum(do.astype(F32) * o_ref[rows, :].astype(F32), axis=-1, keepdims=True)
            groups.append((q, do, q.T, do.T, delta, lse_ref[rows, :] * LOG2E))
        dq = [None] * halves
        for j in range(T // tc):
            keys = slice(j * tc, (j + 1) * tc)
            kc, vc = k_ref[keys, :], v_ref[keys, :]
            for h, (q, do, qt, dot, delta, lse2) in enumerate(groups):
                s = lax.dot_general(q, kc, _NT, preferred_element_type=F32)
                p = jnp.exp2(s * c - lse2)
                dp = lax.dot_general(do, vc, _NT, preferred_element_type=F32)
                ds = (p * (dp - delta) * scale).astype(BF16)
                dqj = lax.dot_general(ds, kc, _NN, preferred_element_type=F32)
                dq[h] = dqj if dq[h] is None else dq[h] + dqj
                dvt_ref[:, keys] += lax.dot_general(dot, p.astype(BF16), _NN, preferred_element_type=F32)
                dkt_ref[:, keys] += lax.dot_general(qt, ds, _NN, preferred_element_type=F32)
        for h in range(halves):
            dq_ref[h * th:(h + 1) * th, :] = dq[h]

    qmap = lambda kv, g, i: (i, kv * GROUP + g)
    return pl.pallas_call(
        body, name="attn_a_bwd", grid=(n_kv, GROUP, T // tq),
        in_specs=[pl.BlockSpec((tq, HEAD_DIM), qmap),
                  pl.BlockSpec((T, HEAD_DIM), lambda kv, g, i: (0, n_q + kv)),
                  pl.BlockSpec((T, HEAD_DIM), lambda kv, g, i: (0, n_q + n_kv + kv)),
                  pl.BlockSpec((tq, HEAD_DIM), qmap),
                  pl.BlockSpec((tq, HEAD_DIM), qmap),
                  pl.BlockSpec((None, tq, 1), lambda kv, g, i: (kv * GROUP + g, i, 0))],
        out_specs=[pl.BlockSpec((tq, HEAD_DIM), qmap),
                   pl.BlockSpec((HEAD_DIM, T), lambda kv, g, i: (kv, 0)),
                   pl.BlockSpec((HEAD_DIM, T), lambda kv, g, i: (kv, 0))],
        out_shape=[jax.ShapeDtypeStruct((T, n_q * HEAD_DIM), F32),
                   jax.ShapeDtypeStruct((n_kv * HEAD_DIM, T), F32),
                   jax.ShapeDtypeStruct((n_kv * HEAD_DIM, T), F32)],
        compiler_params=_params(("parallel", "arbitrary", "arbitrary")),
    )(pb, pb, pb, o_cat, d_o, lse)


def _bucket_index():
    r = np.arange(BLOCK_Q)[:, None]
    j = np.arange(3 * BLOCK_Q)[None, :]
    rel = (j - BLOCK_Q) - r
    nb = N_BUCKETS // 2
    ret = np.where(rel > 0, nb, 0)
    n = np.abs(rel)
    max_exact = nb // 2
    nf = np.maximum(n, 1).astype(np.float32)
    large = max_exact + (np.log(nf / max_exact) / math.log(MAX_DISTANCE / max_exact) * (nb - max_exact)).astype(np.int32)
    large = np.minimum(large, nb - 1)
    return jnp.asarray(ret + np.where(n < max_exact, n, large), jnp.int32)


def _bias_build(idx, table_flat, n_heads, deps=()):
    def body(idx_ref, tab_ref, o_ref):
        h = pl.program_id(0)
        iv = idx_ref[...]
        acc = jnp.zeros(iv.shape, F32)
        for b in range(N_BUCKETS):
            acc = jnp.where(iv == b, tab_ref[b * n_heads + h], acc)
        r = lax.broadcasted_iota(jnp.int32, iv.shape, 0)
        j = lax.broadcasted_iota(jnp.int32, iv.shape, 1)
        o_ref[...] = jnp.where(jnp.abs(j - BLOCK_Q - r) <= WINDOW, acc, NEG_INF)

    return _pcall(
        body, deps, name="bias_build", grid=(n_heads,),
        in_specs=[pl.BlockSpec(idx.shape, lambda h: (0, 0)), pl.BlockSpec(memory_space=pltpu.SMEM)],
        out_specs=pl.BlockSpec((None,) + idx.shape, lambda h: (h, 0, 0)),
        out_shape=jax.ShapeDtypeStruct((n_heads,) + idx.shape, F32),
        compiler_params=_params(("parallel",)),
    )(idx, table_flat)


def _in_sequence(n, T):
    j = lax.broadcasted_iota(jnp.int32, (GROUP * BLOCK_Q, 3 * BLOCK_Q), 1)
    kabs = n * BLOCK_Q + j - BLOCK_Q
    return (kabs >= 0) & (kabs < T)


def _per_head_rows(values):
    head = lax.broadcasted_iota(jnp.int32, (GROUP * BLOCK_Q, 1), 0) // BLOCK_Q
    col = jnp.zeros((GROUP * BLOCK_Q, 1), F32)
    for g, v in enumerate(values):
        col = jnp.where(head == g, v, col)
    return col


def _band_specs(col, nblk, sb):
    return [pl.BlockSpec((BLOCK_Q, HEAD_DIM), lambda kv, i: (jnp.maximum(sb * i - 1, 0), col(kv))),
            pl.BlockSpec((sb * BLOCK_Q, HEAD_DIM), lambda kv, i: (i, col(kv))),
            pl.BlockSpec((BLOCK_Q, HEAD_DIM), lambda kv, i: (jnp.minimum(sb * i + sb, nblk - 1), col(kv)))]


def _head_specs(base, rows):
    return [pl.BlockSpec((rows, HEAD_DIM), functools.partial(lambda kv, i, g: (i, base + kv * GROUP + g), g=g))
            for g in range(GROUP)]


def _attn_b_fwd(pb, bias, sink, o_all, q_off, n_q, n_kv, deps=(), sb=16):
    T = pb.shape[0]
    nblk = T // BLOCK_Q
    sb = min(sb, nblk)
    tq = sb * BLOCK_Q
    scale = HEAD_DIM ** -0.5

    def body(*refs):
        q_refs = refs[0:GROUP]
        k_refs, v_refs = refs[GROUP:GROUP + 3], refs[GROUP + 3:GROUP + 6]
        bias_ref, sink_ref, o_ref, lse_ref = refs[GROUP + 6:]
        kv, i = pl.program_id(0), pl.program_id(1)
        kb = jnp.concatenate([r[...] for r in k_refs], axis=0)
        vb = jnp.concatenate([r[...] for r in v_refs], axis=0)
        bias_all = bias_ref[...].reshape(GROUP * BLOCK_Q, 3 * BLOCK_Q)
        sk = _per_head_rows([sink_ref[kv * GROUP + g] for g in range(GROUP)])
        for b in range(sb):
            rows = slice(b * BLOCK_Q, (b + 1) * BLOCK_Q)
            kw, vw = kb[b * BLOCK_Q:(b + 3) * BLOCK_Q], vb[b * BLOCK_Q:(b + 3) * BLOCK_Q]
            q = jnp.concatenate([r[rows, :] for r in q_refs], axis=0)
            s = lax.dot_general(q, kw, _NT, preferred_element_type=F32) * scale + bias_all
            if b == 0 or b == sb - 1:
                s = jnp.where(_in_sequence(i * sb + b, T), s, NEG_INF)
            m = jnp.maximum(jnp.max(s, axis=-1, keepdims=True), sk)
            p = jnp.exp(s - m)
            l = jnp.sum(p, axis=-1, keepdims=True) + jnp.exp(sk - m)
            o = (lax.dot_general(p.astype(BF16), vw, _NN, preferred_element_type=F32) / l).astype(BF16)
            lse = m + jnp.log(l)
            for g in range(GROUP):
                head = slice(g * BLOCK_Q, (g + 1) * BLOCK_Q)
                o_ref[rows, g * HEAD_DIM:(g + 1) * HEAD_DIM] = o[head]
                lse_ref[g, rows, :] = lse[head]

    first_group = o_all.shape[1] // (GROUP * HEAD_DIM) - n_kv
    return _pcall(
        body, deps, into=(o_all, 0), name="attn_b_fwd", grid=(n_kv, nblk // sb),
        in_specs=[*_head_specs(q_off, tq),
                  *_band_specs(lambda kv: q_off + n_q + kv, nblk, sb),
                  *_band_specs(lambda kv: q_off + n_q + n_kv + kv, nblk, sb),
                  pl.BlockSpec((GROUP, BLOCK_Q, 3 * BLOCK_Q), lambda kv, i: (kv, 0, 0)),
                  pl.BlockSpec(memory_space=pltpu.SMEM)],
        out_specs=[pl.BlockSpec((tq, GROUP * HEAD_DIM), lambda kv, i: (i, first_group + kv)),
                   pl.BlockSpec((GROUP, tq, 1), lambda kv, i: (kv, i, 0))],
        out_shape=[jax.ShapeDtypeStruct(o_all.shape, BF16), jax.ShapeDtypeStruct((n_q, T, 1), F32)],
        compiler_params=_params(("parallel", "parallel")),
    )(*([pb] * (GROUP + 6)), bias, sink)


def _attn_b_bwd(pb, o_cat, d_o, lse, bias, sink, q_off, n_q, n_kv, o_off, deps=(), sb=16):
    T = pb.shape[0]
    nblk = T // BLOCK_Q
    sb = min(sb, nblk)
    tq = sb * BLOCK_Q
    scale = HEAD_DIM ** -0.5

    def body(*refs):
        q_refs = refs[0:GROUP]
        k_refs, v_refs = refs[GROUP:GROUP + 3], refs[GROUP + 3:GROUP + 6]
        o_refs, do_refs = refs[GROUP + 6:2 * GROUP + 6], refs[2 * GROUP + 6:3 * GROUP + 6]
        lse_ref, bias_ref, sink_ref, dq_ref, dk_ref, dv_ref, dbias_ref, dsink_ref, dkb_ref, dvb_ref = refs[3 * GROUP + 6:]
        kv, i = pl.program_id(0), pl.program_id(1)
        first = i == 0

        @pl.when(first)
        def _():
            dk_ref[...] = jnp.zeros(dk_ref.shape, F32)
            dv_ref[...] = jnp.zeros(dv_ref.shape, F32)
            dbias_ref[...] = jnp.zeros(dbias_ref.shape, F32)

        kb = jnp.concatenate([r[...] for r in k_refs], axis=0)
        vb = jnp.concatenate([r[...] for r in v_refs], axis=0)
        dkb_ref[...] = jnp.zeros(dkb_ref.shape, F32)
        dvb_ref[...] = jnp.zeros(dvb_ref.shape, F32)
        row = lax.broadcasted_iota(jnp.int32, (SUBLANES, LANES), 0)
        dsink = jnp.zeros((SUBLANES, LANES), F32)
        bias_all = bias_ref[...].reshape(GROUP * BLOCK_Q, 3 * BLOCK_Q)
        sk = _per_head_rows([sink_ref[kv * GROUP + g] for g in range(GROUP)])
        for b in range(sb):
            rows = slice(b * BLOCK_Q, (b + 1) * BLOCK_Q)
            win = slice(b * BLOCK_Q, (b + 3) * BLOCK_Q)
            kw, vw = kb[win], vb[win]
            q = jnp.concatenate([r[rows, :] for r in q_refs], axis=0)
            do = jnp.concatenate([r[rows, :] for r in do_refs], axis=0)
            o = jnp.concatenate([r[rows, :] for r in o_refs], axis=0)
            lse = jnp.concatenate([lse_ref[g, rows, :] for g in range(GROUP)], axis=0)
            delta = jnp.sum(do.astype(F32) * o.astype(F32), axis=-1, keepdims=True)
            s = lax.dot_general(q, kw, _NT, preferred_element_type=F32) * scale + bias_all
            if b == 0 or b == sb - 1:
                s = jnp.where(_in_sequence(i * sb + b, T), s, NEG_INF)
            p = jnp.exp(s - lse)
            dp = lax.dot_general(do, vw, _NT, preferred_element_type=F32)
            ds = p * (dp - delta)
            dbias_ref[...] += ds.reshape(GROUP, BLOCK_Q, 3 * BLOCK_Q)
            sunk = jnp.exp(sk - lse) * delta
            for g in range(GROUP):
                dsink = dsink + jnp.where(row == g, -jnp.sum(sunk[g * BLOCK_Q:(g + 1) * BLOCK_Q]), 0.0)
            dsb = (ds * scale).astype(BF16)
            dq = lax.dot_general(dsb, kw, _NN, preferred_element_type=F32).astype(BF16)
            for g in range(GROUP):
                dq_ref[rows, g * HEAD_DIM:(g + 1) * HEAD_DIM] = dq[g * BLOCK_Q:(g + 1) * BLOCK_Q]
            dkb_ref[win, :] += lax.dot_general(dsb, q, _TN, preferred_element_type=F32)
            dvb_ref[win, :] += lax.dot_general(p.astype(BF16), do, _TN, preferred_element_type=F32)
        _accumulate(dsink_ref, dsink, first)

        before = pl.ds(pl.multiple_of(jnp.maximum(sb * i - 1, 0) * BLOCK_Q, BLOCK_Q), BLOCK_Q)
        own = pl.ds(pl.multiple_of(i * tq, BLOCK_Q), tq)
        after = pl.ds(pl.multiple_of(jnp.minimum(sb * i + sb, nblk - 1) * BLOCK_Q, BLOCK_Q), BLOCK_Q)
        for acc_ref, band_ref in ((dk_ref, dkb_ref), (dv_ref, dvb_ref)):
            acc_ref[before, :] += band_ref[0:BLOCK_Q, :]
            acc_ref[own, :] += band_ref[BLOCK_Q:BLOCK_Q + tq, :]
            acc_ref[after, :] += band_ref[BLOCK_Q + tq:, :]

    return _pcall(
        body, deps, name="attn_b_bwd", grid=(n_kv, nblk // sb),
        in_specs=[*_head_specs(q_off, tq),
                  *_band_specs(lambda kv: q_off + n_q + kv, nblk, sb),
                  *_band_specs(lambda kv: q_off + n_q + n_kv + kv, nblk, sb),
                  *_head_specs(o_off, tq), *_head_specs(o_off, tq),
                  pl.BlockSpec((GROUP, tq, 1), lambda kv, i: (kv, i, 0)),
                  pl.BlockSpec((GROUP, BLOCK_Q, 3 * BLOCK_Q), lambda kv, i: (kv, 0, 0)),
                  pl.BlockSpec(memory_space=pltpu.SMEM)],
        out_specs=[pl.BlockSpec((tq, GROUP * HEAD_DIM), lambda kv, i: (i, kv)),
                   pl.BlockSpec((T, HEAD_DIM), lambda kv, i: (0, kv)),
                   pl.BlockSpec((T, HEAD_DIM), lambda kv, i: (0, kv)),
                   pl.BlockSpec((GROUP, BLOCK_Q, 3 * BLOCK_Q), lambda kv, i: (kv, 0, 0)),
                   pl.BlockSpec((None, SUBLANES, LANES), lambda kv, i: (kv, 0, 0))],
        out_shape=[jax.ShapeDtypeStruct((T, n_q * HEAD_DIM), BF16),
                   jax.ShapeDtypeStruct((T, n_kv * HEAD_DIM), F32),
                   jax.ShapeDtypeStruct((T, n_kv * HEAD_DIM), F32),
                   jax.ShapeDtypeStruct((n_q, BLOCK_Q, 3 * BLOCK_Q), F32),
                   jax.ShapeDtypeStruct((n_kv, SUBLANES, LANES), F32)],
        scratch_shapes=[pltpu.VMEM((tq + 2 * BLOCK_Q, HEAD_DIM), F32), pltpu.VMEM((tq + 2 * BLOCK_Q, HEAD_DIM), F32)],
        compiler_params=_params(("parallel", "arbitrary")),
    )(*([pb] * (GROUP + 6)), *([o_cat] * GROUP), *([d_o] * GROUP), lse, bias, sink)


def _table_grads(dbias, dsink_raw, idx):
    n_heads = dbias.shape[0]
    n_kv = dsink_raw.shape[0]

    def body(db_ref, ds_ref, idx_ref, dt_ref, dsk_ref):
        iv = idx_ref[...]
        row = lax.broadcasted_iota(jnp.int32, (SUBLANES, LANES), 0)
        lane = lax.broadcasted_iota(jnp.int32, (SUBLANES, LANES), 1)
        dsk = jnp.zeros((SUBLANES, LANES), F32)
        for h in range(n_heads):
            d = db_ref[h]
            acc = jnp.zeros((SUBLANES, LANES), F32)
            for b in range(N_BUCKETS):
                acc = jnp.where((row == 0) & (lane == b), jnp.sum(jnp.where(iv == b, d, 0.0)), acc)
            dt_ref[:, h * LANES:(h + 1) * LANES] = acc
            raw = ds_ref[h // GROUP]
            val = jnp.sum(jnp.where((row == h % GROUP) & (lane == 0), raw, 0.0))
            dsk = jnp.where((row == 0) & (lane == h), val, dsk)
        dsk_ref[...] = dsk

    return pl.pallas_call(
        body, name="table_grads",
        in_specs=[pl.BlockSpec(memory_space=pltpu.VMEM)] * 3,
        out_specs=[pl.BlockSpec(memory_space=pltpu.VMEM)] * 2,
        out_shape=[jax.ShapeDtypeStruct((SUBLANES, n_heads * LANES), F32),
                   jax.ShapeDtypeStruct((SUBLANES, LANES), F32)],
        compiler_params=_params(),
    )(dbias, dsink_raw, idx)


def _position():
    x, y, c = lax.axis_index("x"), lax.axis_index("y"), lax.axis_index("c")
    return x, y, c


def _hbm(a):
    return pltpu.with_memory_space_constraint(a, pltpu.HBM)


def _split_start(name, bufs, sem_shapes, issue):
    nb, ns = len(bufs), len(sem_shapes)

    def body(*refs):
        buf_refs = refs[:nb]
        sems = refs[nb:nb + ns]
        token = refs[nb + ns + nb]
        issue(buf_refs, sems)
        token[...] = jnp.zeros(token.shape, F32)

    outs = pl.pallas_call(
        body, name=name,
        in_specs=[_HBM] * nb,
        out_specs=[_SEM] * ns + [_HBM] * nb + [_VMEM],
        out_shape=[pltpu.SemaphoreType.DMA(s) for s in sem_shapes] + [pltpu.HBM(b.shape, b.dtype) for b in bufs]
        + [jax.ShapeDtypeStruct((SUBLANES, LANES), F32)],
        input_output_aliases={i: ns + i for i in range(nb)},
        compiler_params=pltpu.CompilerParams(has_side_effects=_EFFECT),
    )(*[_hbm(b) for b in bufs])
    return outs[:ns], outs[ns:ns + nb], outs[-1]


def _split_wait(name, bufs, send, recv, counts, size_of, after):
    nb = len(bufs)

    def body(*refs):
        buf_refs = refs[:nb]
        send_ref, recv_ref = refs[nb], refs[nb + 1]
        x, y, c = _position()
        for w, n in enumerate(counts):
            ref = size_of(buf_refs, w)
            for k in range(n):
                s = sum(counts[:w]) + k
                cp = pltpu.make_async_remote_copy(
                    src_ref=ref, dst_ref=ref, send_sem=send_ref.at[s], recv_sem=recv_ref.at[s],
                    device_id=(x, y, c), device_id_type=MESH)
                cp.wait_send()
                cp.wait_recv()

    return pl.pallas_call(
        body, name=name,
        in_specs=[_HBM] * nb + [_SEM, _SEM, _ANY],
        out_specs=[_HBM] * nb,
        out_shape=[pltpu.HBM(b.shape, b.dtype) for b in bufs],
        input_output_aliases={i: i for i in range(nb)},
        compiler_params=pltpu.CompilerParams(has_side_effects=_EFFECT),
    )(*bufs, send, recv, after)


def _block_of(pos):
    return 4 * pos[0] + 2 * pos[1] + pos[2]


def _shard_of(ref, blk, by_cols):
    aligned = (lambda v, a: v) if isinstance(blk, int) else pl.multiple_of
    if by_cols:
        n = ref.shape[1] // N_DEV
        return ref.at[:, pl.ds(aligned(blk * n, LANES), n)]
    r = ref.shape[0] // N_DEV
    return ref.at[pl.ds(aligned(blk * r, SUBLANES), r), :]


def _place_own(name, land, shard, by_cols, tr=256):
    r, n = shard.shape
    tr = _tile(r, tr)
    mine = _block_of(_position()).astype(jnp.int32).reshape(1)

    def body(m_ref, land_ref, s_ref, o_ref):
        o_ref[...] = s_ref[...]

    if by_cols:
        out = pl.BlockSpec((tr, n), lambda i, m_ref: (i, m_ref[0]))
    else:
        out = pl.BlockSpec((tr, n), lambda i, m_ref: (m_ref[0] * (r // tr) + i, 0))
    return pl.pallas_call(
        body, name=name,
        grid_spec=pltpu.PrefetchScalarGridSpec(
            num_scalar_prefetch=1, grid=(r // tr,),
            in_specs=[_ANY, pl.BlockSpec((tr, n), lambda i, m_ref: (i, 0))], out_specs=out),
        out_shape=jax.ShapeDtypeStruct(land.shape, land.dtype),
        input_output_aliases={1: 0},
        compiler_params=_params(("parallel",)),
    )(mine, land, shard)


def _gather_start(name, shards, by_cols, groups, after=None):
    nw = len(shards)
    lands = [lax.empty((s.shape[0], s.shape[1] * N_DEV) if cols else (s.shape[0] * N_DEV, s.shape[1]), s.dtype)
             for s, cols in zip(shards, by_cols)]
    order = [] if after is None else [after]

    def issue(bufs, sems):
        x, y, c = _position()
        peers = [(x, y, 1 - c), (1 - x, y, c), (x, 1 - y, c), (1 - x, 1 - y, c)]
        for gi, grp in enumerate(groups):
            for wi, w in enumerate(grp):
                for k, peer in enumerate(peers):
                    pltpu.make_async_remote_copy(
                        src_ref=bufs[w], dst_ref=_shard_of(bufs[nw + w], _block_of((x, y, c)), by_cols[w]),
                        send_sem=sems[2 * gi].at[4 * wi + k], recv_sem=sems[2 * gi + 1].at[4 * wi + k],
                        device_id=peer, device_id_type=MESH).start()

    sem_shapes = [(4 * len(g),) for g in groups for _ in range(2)]
    sems, thru, token = _split_start(name, list(shards) + lands + order, sem_shapes, issue)
    return sems, thru[:nw], thru[nw:2 * nw], token


def _gather_forward(name, lands, by_cols):
    nw = len(lands)

    def issue(land, sems):
        x, y, c = _position()
        for w in range(nw):
            for k, chip in enumerate([(1 - x, y), (x, 1 - y), (1 - x, 1 - y)]):
                blk = _shard_of(land[w], _block_of((*chip, c)), by_cols[w])
                pltpu.make_async_remote_copy(
                    src_ref=blk, dst_ref=blk, send_sem=sems[0].at[3 * w + k], recv_sem=sems[1].at[3 * w + k],
                    device_id=(x, y, 1 - c), device_id_type=MESH).start()

    return _split_start(name, lands, [(3 * nw,), (3 * nw,)], issue)


def _first_block(bufs, w, offset=0):
    return bufs[offset + w].at[0]


_PEER_FLIPS = ((0, 0, 1), (1, 0, 0), (1, 0, 1), (0, 1, 0), (0, 1, 1), (1, 1, 0), (1, 1, 1))


def _scatter_start(name, grads, by_cols):
    nw = len(grads)
    lands = []
    for g, cols in zip(grads, by_cols):
        shard = (g.shape[0], g.shape[1] // N_DEV) if cols else (g.shape[0] // N_DEV, g.shape[1])
        lands.append(lax.empty((N_DEV,) + shard, g.dtype))

    def issue(bufs, sems):
        x, y, c = _position()
        flip = lambda v, f: 1 - v if f else v
        for w in range(nw):
            for k, (fx, fy, fc) in enumerate(_PEER_FLIPS):
                peer = (flip(x, fx), flip(y, fy), flip(c, fc))
                pltpu.make_async_remote_copy(
                    src_ref=_shard_of(bufs[w], _block_of(peer), by_cols[w]), dst_ref=bufs[nw + w].at[_block_of((x, y, c))],
                    send_sem=sems[0].at[7 * w + k], recv_sem=sems[1].at[7 * w + k],
                    device_id=peer, device_id_type=MESH).start()

    return _split_start(name, list(grads) + lands, [(7 * nw,), (7 * nw,)], issue)


def _adam(w, g, m, v):
    m = ADAM_B1 * m + (1.0 - ADAM_B1) * g
    v = ADAM_B2 * v + (1.0 - ADAM_B2) * (g * g)
    m_hat = m / (1.0 - ADAM_B1 ** ADAM_STEP)
    v_hat = v / (1.0 - ADAM_B2 ** ADAM_STEP)
    delta = -ADAM_LR * (m_hat / (jnp.sqrt(v_hat) + ADAM_EPS) + ADAM_WD * w)
    return delta, m, v


def _sum_adam(name, landed, grad, by_cols, w, m, v, tr=256):
    R, C = w.shape
    tr = _tile(R, tr if C > 1024 else 2 * tr)
    mine = _block_of(_position()).astype(jnp.int32).reshape(1)

    def body(me_ref, l_ref, own_ref, w_ref, m_ref, v_ref, g_ref, d_ref, nm_ref, nv_ref):
        own = own_ref[...].astype(F32)
        g = None
        for d in range(N_DEV):
            part = jnp.where(me_ref[0] == d, own, l_ref[d].astype(F32))
            g = part if g is None else g + part
        g_ref[...] = g
        d_ref[...], nm_ref[...], nv_ref[...] = _adam(w_ref[...], g, m_ref[...], v_ref[...])

    tile = pl.BlockSpec((tr, C), lambda i, me_ref: (i, 0))
    if by_cols:
        own = pl.BlockSpec((tr, C), lambda i, me_ref: (i, me_ref[0]))
    else:
        own = pl.BlockSpec((tr, C), lambda i, me_ref: (me_ref[0] * (R // tr) + i, 0))
    return pl.pallas_call(
        body, name=name,
        grid_spec=pltpu.PrefetchScalarGridSpec(
            num_scalar_prefetch=1, grid=(R // tr,),
            in_specs=[pl.BlockSpec((N_DEV, tr, C), lambda i, me_ref: (0, i, 0)), own, tile, tile, tile],
            out_specs=[tile] * 4),
        out_shape=[jax.ShapeDtypeStruct((R, C), F32)] * 4,
        compiler_params=_params(("parallel",)),
    )(mine, landed, grad, w, m, v)


def _small_all_reduce(parts, deps=()):
    W = parts.shape[1]

    def body(p_ref, o_ref, slots, send_sems, recv_sems):
        x, y, c = _position()
        me = 4 * x + 2 * y + c
        slots[me] = jnp.sum(p_ref[...], axis=0, keepdims=True)
        peers = [(x, y, 1 - c), (1 - x, y, c), (1 - x, y, 1 - c), (x, 1 - y, c), (x, 1 - y, 1 - c),
                 (1 - x, 1 - y, c), (1 - x, 1 - y, 1 - c)]
        copies = []
        for k, peer in enumerate(peers):
            cp = pltpu.make_async_remote_copy(
                src_ref=slots.at[me], dst_ref=slots.at[me], send_sem=send_sems.at[k], recv_sem=recv_sems.at[k],
                device_id=peer, device_id_type=MESH)
            cp.start()
            copies.append(cp)
        for cp in copies:
            cp.wait()
        total = slots[0]
        for d in range(1, N_DEV):
            total = total + slots[d]
        o_ref[...] = total

    return _pcall(
        body, deps, name="small_all_reduce",
        in_specs=[pl.BlockSpec(memory_space=pltpu.VMEM)], out_specs=pl.BlockSpec(memory_space=pltpu.VMEM),
        out_shape=jax.ShapeDtypeStruct((1, W), F32),
        scratch_shapes=[pltpu.VMEM((N_DEV, 1, W), F32), pltpu.SemaphoreType.DMA((7,)), pltpu.SemaphoreType.DMA((7,))],
    )(parts)


def _adam_small(w, g, m, v):
    def body(w_ref, g_ref, m_ref, v_ref, d_ref, nm_ref, nv_ref):
        d_ref[...], nm_ref[...], nv_ref[...] = _adam(w_ref[...], g_ref[...], m_ref[...], v_ref[...])

    return pl.pallas_call(
        body, name="adam_small",
        in_specs=[pl.BlockSpec(memory_space=pltpu.VMEM)] * 4, out_specs=[pl.BlockSpec(memory_space=pltpu.VMEM)] * 3,
        out_shape=[jax.ShapeDtypeStruct(w.shape, F32)] * 3,
    )(w, g, m, v)


_GATHER_GROUPS = (("w_in",), ("w_out", "w_up", "ple_w"), ("w_down", "w_gate"))
_COL_SHARDED = ("w_in", "w_up", "ple_w")


class _MeshComm:
    def __init__(self, w, mom, var):
        self.w, self.mom, self.var = w, mom, var
        self.out = {}
        self._scatters = {}

    def gather_begin(self):
        self._groups = {}
        token = None
        for tag, first, group_list in (("gather_start0", 0, _GATHER_GROUPS[:1]), ("gather_start1", 1, _GATHER_GROUPS[1:])):
            names = [n for g in group_list for n in g]
            idx = {n: i for i, n in enumerate(names)}
            by_cols = [n in _COL_SHARDED for n in names]
            sems, src, lands, token = _gather_start(tag, [self.w[n].astype(BF16) for n in names], by_cols,
                                                    [[idx[n] for n in g] for g in group_list], token)
            lands = [_place_own("place_" + n, land, s, cols) for n, land, s, cols in zip(names, lands, src, by_cols)]
            for k, g in enumerate(group_list):
                self._groups[first + k] = (sems[2 * k], sems[2 * k + 1], [src[idx[n]] for n in g],
                                           [lands[idx[n]] for n in g])
        return token

    @staticmethod
    def _shard_size(names, offset):
        return lambda bufs, w: _shard_of(bufs[offset + w], 0, names[w] in _COL_SHARDED)

    def gather_arrive(self, gi, after):
        names = _GATHER_GROUPS[gi]
        send, recv, src, lands = self._groups[gi]
        out = _split_wait("gather_arrive%d" % gi, src + lands, send, recv, [4] * len(names),
                          self._shard_size(names, len(names)), after)
        self._arrived = out[len(names):]

    def gather_forward(self, gi):
        by_cols = [n in _COL_SHARDED for n in _GATHER_GROUPS[gi]]
        self._fsems, self._fthru, token = _gather_forward("gather_forward%d" % gi, self._arrived, by_cols)
        return token

    def gather_finish(self, gi, after):
        names = _GATHER_GROUPS[gi]
        out = _split_wait("gather_finish%d" % gi, self._fthru, self._fsems[0], self._fsems[1], [3] * len(names),
                          self._shard_size(names, 0), after)
        return dict(zip(names, out))

    def reduce_begin(self, key, grads):
        names = list(grads)
        sems, thru, token = _scatter_start("scatter_start_" + key, [grads[n] for n in names],
                                           [n in _COL_SHARDED for n in names])
        self._scatters[key] = (names, sems, thru)
        return token

    def reduce_finish(self, key, after):
        names, sems, thru = self._scatters[key]
        nw = len(names)
        out = _split_wait("scatter_wait_" + key, thru, sems[0], sems[1], [N_DEV - 1] * nw,
                          functools.partial(_first_block, offset=nw), after)
        for i, n in enumerate(names):
            self.out[n] = _sum_adam("adam_" + n, out[nw + i], out[i], n in _COL_SHARDED, self.w[n], self.mom[n],
                                    self.var[n])


def _step(x, p, target, gains, comm):
    T, D = x.shape
    n_q = D // (2 * HEAD_DIM)
    n_kv = n_q // GROUP
    cos, sin = _rope_tables(T)
    idx = _bucket_index()

    t = comm.gather_begin()
    u = _rms_fwd("norm_attn", x, gains["attn_norm_g"], deps=(t,))
    comm.gather_arrive(0, u)
    t = comm.gather_forward(0)
    bias = _bias_build(idx, gains["rel_bias_table"].reshape(-1), n_q, deps=(t,))
    full = comm.gather_finish(0, bias)
    proj_a, pb = _in_proj(u, full["w_in"], cos, sin, gains["q_norm_g"], gains["k_norm_g"], n_q + n_kv)
    o_a, lse_a = _attn_a_fwd(pb, n_q, n_kv, 2 * n_q)
    comm.gather_arrive(1, lse_a)
    t = comm.gather_forward(1)
    sink = gains["sink_logits"].reshape(-1)
    b_off = n_q + 2 * n_kv
    o_cat, lse_b = _attn_b_fwd(pb, bias, sink, o_a, b_off, n_q, n_kv, deps=(t,))
    full.update(comm.gather_finish(1, lse_b))
    h1, m_in = _mm_nn_rms("out_proj", o_cat, full["w_out"], x, gains["mlp_norm_g"])

    def up_epilogue(acc, extra, outs):
        outs[0][...] = acc.astype(BF16)
        r = jnp.maximum(acc, 0.0)
        outs[1][...] = (r * r).astype(BF16)

    a_act, f_act = _mm_nn("up_proj", m_in, full["w_up"], epilogue=up_epilogue, out_dtypes=[BF16, BF16], tn=2048)
    comm.gather_arrive(2, f_act)
    t = comm.gather_forward(2)
    p_b = p.astype(BF16)
    pe = _mm_nn("ple_proj", p_b, full["ple_w"], deps=(t,))
    full.update(comm.gather_finish(2, pe))
    h2 = _mm_nn("down_proj", f_act, full["w_down"], epilogue=_store_add, extras=(h1,), tn=512)
    gn = _rms_fwd("norm_gate", h2, gains["gate_norm_g"])

    dh3, dz, dpe, dg_final, dg_ple, loss_part = _gate_tail(gn, full["w_gate"], h2, pe, target, gains["ple_norm_g"],
                                                           gains["final_norm_g"])
    gw_gate = _mm_tn("grad_w_gate", gn, dz, tn=1024)
    gw_ple = _mm_tn("grad_ple_w", p_b, dpe)
    dh2, dh2_b, dg_gate = _mm_nt_rms_bwd("d_gate_in", dz, full["w_gate"], h2, gains["gate_norm_g"], dh3, tm=512)
    gw_down = _mm_tn("grad_w_down", f_act, dh2_b, tn=1024)
    t = comm.reduce_begin("b", dict(w_gate=gw_gate, ple_w=gw_ple, w_down=gw_down))

    def act_bwd(acc, extra, outs):
        outs[0][...] = (acc * (2.0 * jnp.maximum(extra[0][...].astype(F32), 0.0))).astype(BF16)

    da = _mm_nt("d_act", dh2_b, full["w_down"], out_dtype=BF16, epilogue=act_bwd, extras=(a_act,), tn=2048, deps=(t,))
    gw_up = _mm_tn("grad_w_up", m_in, da, tn=1024)
    dm = _mm_nt("d_mlp_in", da, full["w_up"], out_dtype=BF16, tn=512)
    dh1, dh1_b, dg_mlp = _rms_bwd("norm_mlp_bwd", dm, h1, gains["mlp_norm_g"], dh2)
    gw_out = _mm_tn("grad_w_out", o_cat, dh1_b, tn=1024)
    t = comm.reduce_begin("d", dict(w_up=gw_up, w_out=gw_out))
    d_o = _mm_nt("d_attn_out", dh1_b, full["w_out"], out_dtype=BF16, deps=(t,))
    dqa, dka_t, dva_t = _attn_a_bwd(pb, o_cat, d_o, lse_a, n_q, n_kv)
    dqb, dkb, dvb, dbias, dsink_raw = _attn_b_bwd(pb, o_cat, d_o, lse_b, bias, sink, b_off, n_q, n_kv, n_q)
    dtable, dsink = _table_grads(dbias, dsink_raw, idx)
    dproj, dg_q, dg_k = _dproj(proj_a, dqa, dka_t, dva_t, dqb, dkb, dvb, cos, sin, gains["q_norm_g"], gains["k_norm_g"])
    gw_in = _mm_tn("grad_w_in", u, dproj, tn=1024)
    t = comm.reduce_begin("e", dict(w_in=gw_in))
    dx, dg_attn = _mm_nt_rms_bwd("d_attn_in", dproj, full["w_in"], x, gains["attn_norm_g"], dh1, with_bf16=False,
                                 tm=512, deps=(t,))
    for key in "bd":
        comm.reduce_finish(key, dx)

    parts = jnp.concatenate([dg_attn, dg_mlp, dg_ple, dg_gate, dg_final, dg_q, dg_k, dtable, dsink, loss_part], axis=1)
    return dx, parts


_SHARDED = ("w_in", "w_out", "w_up", "w_down", "ple_w", "w_gate")
_VECTORS = ("attn_norm_g", "mlp_norm_g", "ple_norm_g", "gate_norm_g", "final_norm_g")
_ORDER = ("attn_norm_g", "w_in", "q_norm_g", "k_norm_g", "sink_logits", "w_out", "mlp_norm_g", "w_up", "w_down",
          "ple_w", "ple_norm_g", "gate_norm_g", "w_gate", "rel_bias_table", "final_norm_g")


def _pack_small(vals, n_heads):
    lane_pad = lambda v: jnp.pad(v, ((0, 0), (0, LANES - v.shape[1])))
    table = lane_pad(vals["rel_bias_table"].T).reshape(1, n_heads * LANES)
    return jnp.concatenate(
        [vals[n].reshape(1, -1) for n in _VECTORS] + [vals["q_norm_g"], vals["k_norm_g"], table,
                                                      lane_pad(vals["sink_logits"]), jnp.zeros((1, LANES), F32)], axis=1)


def _unpack_small(row, like, n_heads):
    out, off = {}, 0
    for n in _VECTORS:
        out[n] = row[:, off:off + like[n].size].reshape(like[n].shape)
        off += like[n].size
    for n in ("q_norm_g", "k_norm_g"):
        out[n] = row[:, off:off + LANES]
        off += LANES
    out["rel_bias_table"] = row[:, off:off + n_heads * LANES].reshape(n_heads, LANES)[:, :N_BUCKETS].T
    off += n_heads * LANES
    out["sink_logits"] = row[:, off:off + n_heads]
    off += LANES
    return out, row[0, off]


def kernel(x, p, attn_norm_g, w_in, q_norm_g, k_norm_g, sink_logits, w_out, mlp_norm_g, w_up, w_down, ple_w, ple_norm_g, gate_norm_g, w_gate, rel_bias_table, final_norm_g, loss_target, m_attn_norm_g, m_w_in, m_q_norm_g, m_k_norm_g, m_sink_logits, m_w_out, m_mlp_norm_g, m_w_up, m_w_down, m_ple_w, m_ple_norm_g, m_gate_norm_g, m_w_gate, m_rel_bias_table, m_final_norm_g, v_attn_norm_g, v_w_in, v_q_norm_g, v_k_norm_g, v_sink_logits, v_w_out, v_mlp_norm_g, v_w_up, v_w_down, v_ple_w, v_ple_norm_g, v_gate_norm_g, v_w_gate, v_rel_bias_table, v_final_norm_g):
    w = dict(attn_norm_g=attn_norm_g, w_in=w_in[0], q_norm_g=q_norm_g, k_norm_g=k_norm_g, sink_logits=sink_logits,
             w_out=w_out[0], mlp_norm_g=mlp_norm_g, w_up=w_up[0], w_down=w_down[0], ple_w=ple_w[0],
             ple_norm_g=ple_norm_g, gate_norm_g=gate_norm_g, w_gate=w_gate[0], rel_bias_table=rel_bias_table,
             final_norm_g=final_norm_g)
    mom = dict(attn_norm_g=m_attn_norm_g, w_in=m_w_in[0], q_norm_g=m_q_norm_g, k_norm_g=m_k_norm_g,
               sink_logits=m_sink_logits, w_out=m_w_out[0], mlp_norm_g=m_mlp_norm_g, w_up=m_w_up[0],
               w_down=m_w_down[0], ple_w=m_ple_w[0], ple_norm_g=m_ple_norm_g, gate_norm_g=m_gate_norm_g,
               w_gate=m_w_gate[0], rel_bias_table=m_rel_bias_table, final_norm_g=m_final_norm_g)
    var = dict(attn_norm_g=v_attn_norm_g, w_in=v_w_in[0], q_norm_g=v_q_norm_g, k_norm_g=v_k_norm_g,
               sink_logits=v_sink_logits, w_out=v_w_out[0], mlp_norm_g=v_mlp_norm_g, w_up=v_w_up[0],
               w_down=v_w_down[0], ple_w=v_ple_w[0], ple_norm_g=v_ple_norm_g, gate_norm_g=v_gate_norm_g,
               w_gate=v_w_gate[0], rel_bias_table=v_rel_bias_table, final_norm_g=v_final_norm_g)
    D = x.shape[-1]
    n_heads = D // (2 * HEAD_DIM)

    gains = {n: w[n] for n in w if n not in _SHARDED}
    gains["final_norm_g"] = final_norm_g.reshape(1, -1)

    comm = _MeshComm(w, mom, var)
    dx, parts = _step(x[0], p[0, 0], loss_target[0], gains, comm)

    small_g = _small_all_reduce(parts, deps=[comm.out[n][0] for n in comm.out])
    comm.reduce_finish("e", small_g)

    g_out, d_out, m_out, v_out = {}, {}, {}, {}
    for n in _SHARDED:
        g, d, nm, nv = comm.out[n]
        g_out[n], d_out[n], m_out[n], v_out[n] = g[None], d[None], nm[None], nv[None]

    small = {n: v for n, v in w.items() if n not in _SHARDED}
    pack = lambda vals: _pack_small({n: vals[n] for n in small}, n_heads)
    sd, sm, sv = _adam_small(pack(w), small_g, pack(mom), pack(var))
    sg, loss = _unpack_small(small_g, small, n_heads)
    g_out.update(sg)
    for dst, row in ((d_out, sd), (m_out, sm), (v_out, sv)):
        dst.update(_unpack_small(row, small, n_heads)[0])

    return (loss, dx[None], *[g_out[n] for n in _ORDER], *[d_out[n] for n in _ORDER],
            *[m_out[n] for n in _ORDER], *[v_out[n] for n in _ORDER])
```

```python
import functools
import math

import numpy as np
import jax
import jax.numpy as jnp
from jax import lax
from jax.experimental import pallas as pl
from jax.experimental.pallas import tpu as pltpu

F32 = jnp.float32
BF16 = jnp.bfloat16

N_DEV = 8
N_CHIP = 4
HEAD_DIM = 128
GROUP = 4
GRID_W = 64
WINDOW = 128
BLOCK_Q = 128
N_BUCKETS = 32
MAX_DISTANCE = 128
ROPE_THETA = 10000.0
EPS = 1e-6
NEG_INF = -1e30
ADAM_LR = 0.001
ADAM_B1 = 0.9
ADAM_B2 = 0.999
ADAM_EPS = 1e-08
ADAM_WD = 0.01
ADAM_STEP = 10
LOG2E = math.log2(math.e)
LANES = 128
SUBLANES = 8
VMEM_LIMIT_BYTES = 60 * 1024 * 1024
MESH = pl.DeviceIdType.MESH

_NT = (((1,), (1,)), ((), ()))
_NN = (((1,), (0,)), ((), ()))
_TN = (((0,), (0,)), ((), ()))


def _tile(dim, pref):
    return pref if dim % pref == 0 else dim


def _params(sem=None):
    return pltpu.CompilerParams(dimension_semantics=sem, vmem_limit_bytes=VMEM_LIMIT_BYTES)


_HBM = pl.BlockSpec(memory_space=pltpu.HBM)
_SEM = pl.BlockSpec(memory_space=pltpu.SEMAPHORE)
_ANY = pl.BlockSpec(memory_space=pl.ANY)
_VMEM = pl.BlockSpec(memory_space=pltpu.VMEM)
_EFFECT = pltpu.SideEffectType.DATAFLOW_SIDE_EFFECTING


def _pcall(body, deps=(), *, in_specs, into=None, **kw):
    deps = [d for d in deps if d is not None]
    nd = len(deps)
    if into is not None:
        deps = [into[0]] + deps
        nd += 1
        kw["input_output_aliases"] = {0: into[1]}

    def wrapped(*refs):
        body(*refs[nd:])

    call = pl.pallas_call(wrapped, in_specs=[_ANY] * nd + list(in_specs), **kw)
    return lambda *args: call(*deps, *args)


def _mm(name, a, b, dims, grid, a_spec, b_spec, out_shape, out_specs, acc_shape, epilogue,
        extras=(), extra_specs=(), deps=(), semantics=("parallel", "parallel", "arbitrary")):
    nk = grid[2]
    n_extra = len(extras)

    def body(*refs):
        a_ref, b_ref = refs[0], refs[1]
        extra = refs[2:2 + n_extra]
        outs = refs[2 + n_extra:-1]
        acc = refs[-1]
        part = lax.dot_general(a_ref[...], b_ref[...], dims, preferred_element_type=F32)
        if nk == 1:
            epilogue(part, extra, outs)
        else:
            k = pl.program_id(2)

            @pl.when(k == 0)
            def _():
                acc[...] = part

            @pl.when(k > 0)
            def _():
                acc[...] += part

            @pl.when(k == nk - 1)
            def _():
                epilogue(acc[...], extra, outs)

    return _pcall(
        body, deps, name=name, grid=grid,
        in_specs=[a_spec, b_spec, *extra_specs],
        out_specs=out_specs, out_shape=out_shape,
        scratch_shapes=[pltpu.VMEM(acc_shape if nk > 1 else (SUBLANES, LANES), F32)],
        compiler_params=_params(semantics),
    )(a, b, *extras)


def _store(dtype):
    def ep(acc, extra, outs):
        outs[0][...] = acc.astype(dtype)
    return ep


def _store_add(acc, extra, outs):
    outs[0][...] = acc + extra[0][...]


def _mm_nn(name, a, b, out_dtype=F32, epilogue=None, extras=(), n_out=1, out_dtypes=None, tm=1024, tn=1024, tk=None,
           deps=()):
    M, K = a.shape
    N = b.shape[1]
    tm, tn, tk = _tile(M, tm), _tile(N, tn), _tile(K, tk or K)
    b_spec = pl.BlockSpec((tk, tn), lambda i, j, k: (k, j))
    grid = (M // tm, N // tn, K // tk)
    o_spec = pl.BlockSpec((tm, tn), lambda i, j, k: (i, j))
    out_dtypes = out_dtypes or [out_dtype] * n_out
    out_shape = [jax.ShapeDtypeStruct((M, N), d) for d in out_dtypes]
    res = _mm(name, a, b, _NN, grid, pl.BlockSpec((tm, tk), lambda i, j, k: (i, k)), b_spec,
              out_shape, [o_spec] * len(out_dtypes), (tm, tn), epilogue or _store(out_dtype),
              extras, [o_spec] * len(extras), deps)
    return res if len(out_dtypes) > 1 else res[0]


def _mm_nt(name, a, b, out_dtype=F32, epilogue=None, extras=(), tm=1024, tn=1024, tk=None, deps=()):
    M, C = a.shape
    N = b.shape[0]
    tm, tn, tk = _tile(M, tm), _tile(N, tn), _tile(C, tk or C)
    b_spec = pl.BlockSpec((tn, tk), lambda i, j, k: (j, k))
    grid = (M // tm, N // tn, C // tk)
    o_spec = pl.BlockSpec((tm, tn), lambda i, j, k: (i, j))
    return _mm(name, a, b, _NT, grid, pl.BlockSpec((tm, tk), lambda i, j, k: (i, k)), b_spec,
               [jax.ShapeDtypeStruct((M, N), out_dtype)], [o_spec], (tm, tn), epilogue or _store(out_dtype),
               extras, [o_spec] * len(extras), deps)[0]


def _mm_tn(name, a, b, out_dtype=BF16, tm=1024, tn=512, tk=None, deps=()):
    T, M = a.shape
    N = b.shape[1]
    tm, tn, tk = _tile(M, tm), _tile(N, tn), _tile(T, tk or T)
    out_shape = jax.ShapeDtypeStruct((M, N), out_dtype)
    o_spec = pl.BlockSpec((tm, tn), lambda i, j, k: (i, j))
    grid = (M // tm, N // tn, T // tk)
    return _mm(name, a, b, _TN, grid, pl.BlockSpec((tk, tm), lambda i, j, k: (k, i)),
               pl.BlockSpec((tk, tn), lambda i, j, k: (k, j)), [out_shape], [o_spec], (tm, tn), _store(out_dtype),
               deps=deps)[0]


def _mean_last(v):
    return jnp.mean(v, axis=-1, keepdims=True)


def _rows_to_sublanes(v):
    r, c = v.shape
    return jnp.sum(v.reshape(r // SUBLANES, SUBLANES, c), axis=0)


def _accumulate(ref, val, first):
    @pl.when(first)
    def _():
        ref[...] = val

    @pl.when(jnp.logical_not(first))
    def _():
        ref[...] += val


def _rms_fwd(name, x, g, tr=512, deps=()):
    T, D = x.shape
    tr = _tile(T, tr)

    def body(x_ref, g_ref, o_ref):
        xv = x_ref[...]
        r = lax.rsqrt(_mean_last(xv * xv) + EPS)
        o_ref[...] = (xv * r * g_ref[...]).astype(BF16)

    row = pl.BlockSpec((tr, D), lambda i: (i, 0))
    return _pcall(
        body, deps, name=name, grid=(T // tr,),
        in_specs=[row, pl.BlockSpec((1, D), lambda i: (0, 0))],
        out_specs=row, out_shape=jax.ShapeDtypeStruct((T, D), BF16),
        compiler_params=_params(("parallel",)),
    )(x, g)


def _rms_bwd(name, dyn, x, g, dres, tr=512, deps=()):
    T, D = x.shape
    tr = _tile(T, tr)

    def body(dy_ref, x_ref, g_ref, dr_ref, dx_ref, dxb_ref, dg_ref):
        xv = x_ref[...]
        r = lax.rsqrt(_mean_last(xv * xv) + EPS)
        xn = xv * r
        dy = dy_ref[...].astype(F32)
        dxn = dy * g_ref[...]
        dx = dr_ref[...] + r * (dxn - xn * _mean_last(dxn * xn))
        dx_ref[...] = dx
        dxb_ref[...] = dx.astype(BF16)
        _accumulate(dg_ref, _rows_to_sublanes(dy * xn), pl.program_id(0) == 0)

    row = pl.BlockSpec((tr, D), lambda i: (i, 0))
    return _pcall(
        body, deps, name=name, grid=(T // tr,),
        in_specs=[row, row, pl.BlockSpec((1, D), lambda i: (0, 0)), row],
        out_specs=[row, row, pl.BlockSpec((SUBLANES, D), lambda i: (0, 0))],
        out_shape=[jax.ShapeDtypeStruct((T, D), F32), jax.ShapeDtypeStruct((T, D), BF16),
                   jax.ShapeDtypeStruct((SUBLANES, D), F32)],
        compiler_params=_params(("arbitrary",)),
    )(dyn, x, g, dres)


def _mm_nn_rms(name, a, b, res, g, tm=512, deps=()):
    M, K = a.shape
    N = b.shape[1]
    tm = _tile(M, tm)

    def epilogue(acc, extra, outs):
        h = acc + extra[0][...]
        outs[0][...] = h
        outs[1][...] = (h * lax.rsqrt(_mean_last(h * h) + EPS) * extra[1][...]).astype(BF16)

    row = pl.BlockSpec((tm, N), lambda i, j, k: (i, 0))
    return _mm(name, a, b, _NN, (M // tm, 1, 1), pl.BlockSpec((tm, K), lambda i, j, k: (i, 0)),
               pl.BlockSpec((K, N), lambda i, j, k: (0, 0)),
               [jax.ShapeDtypeStruct((M, N), F32), jax.ShapeDtypeStruct((M, N), BF16)], [row, row], (tm, N), epilogue,
               (res, g), [row, pl.BlockSpec((1, N), lambda i, j, k: (0, 0))], deps)


def _mm_nt_rms_bwd(name, a, b, x, g, dres, with_bf16=True, tm=256, deps=()):
    M, C = a.shape
    N = b.shape[0]
    tm = _tile(M, tm)

    def epilogue(dy, extra, outs):
        x_ref, dr_ref, g_ref = extra
        xv = x_ref[...]
        r = lax.rsqrt(_mean_last(xv * xv) + EPS)
        xn = xv * r
        dxn = dy * g_ref[...]
        dx = dr_ref[...] + r * (dxn - xn * _mean_last(dxn * xn))
        outs[0][...] = dx
        if with_bf16:
            outs[1][...] = dx.astype(BF16)
        _accumulate(outs[-1], _rows_to_sublanes(dy * xn), pl.program_id(0) == 0)

    row = pl.BlockSpec((tm, N), lambda i, j, k: (i, 0))
    copies = [jax.ShapeDtypeStruct((M, N), F32)] + ([jax.ShapeDtypeStruct((M, N), BF16)] if with_bf16 else [])
    return _mm(name, a, b, _NT, (M // tm, 1, 1), pl.BlockSpec((tm, C), lambda i, j, k: (i, 0)),
               pl.BlockSpec((N, C), lambda i, j, k: (0, 0)),
               copies + [jax.ShapeDtypeStruct((SUBLANES, N), F32)],
               [row] * len(copies) + [pl.BlockSpec((SUBLANES, N), lambda i, j, k: (0, 0))], (tm, N), epilogue,
               (x, dres, g), [row, row, pl.BlockSpec((1, N), lambda i, j, k: (0, 0))], deps,
               semantics=("arbitrary", "arbitrary", "arbitrary"))


def _gate_tail(gn, w_gate, h2, pe, target, g_ple, g_final, tm=256):
    T, D = h2.shape
    tm = _tile(T, tm)

    def epilogue(z, extra, outs):
        h2_ref, pe_ref, t_ref, gp_ref, gf_ref = extra
        dh3_ref, dz_ref, dpe_ref, dgf_ref, dgp_ref, loss_ref = outs
        first = pl.program_id(0) == 0
        pev = pe_ref[...]
        r3 = lax.rsqrt(_mean_last(pev * pev) + EPS)
        en = pev * r3
        e = en * gp_ref[...]
        gate = 1.0 / (1.0 + jnp.exp(-z))
        h3 = h2_ref[...] + gate * e
        r5 = lax.rsqrt(_mean_last(h3 * h3) + EPS)
        hn = h3 * r5
        diff = hn * gf_ref[...] - t_ref[...]
        loss_rows = 0.5 * _mean_last(diff * diff)
        row0 = lax.broadcasted_iota(jnp.int32, (SUBLANES, LANES), 0) == 0
        _accumulate(loss_ref, jnp.where(row0, jnp.sum(loss_rows), 0.0), first)
        dy = diff * (1.0 / D)
        _accumulate(dgf_ref, _rows_to_sublanes(dy * hn), first)
        dhn = dy * gf_ref[...]
        dh3 = r5 * (dhn - hn * _mean_last(dhn * hn))
        dh3_ref[...] = dh3
        dgate = dh3 * e
        de = dh3 * gate
        dz_ref[...] = (dgate * gate * (1.0 - gate)).astype(BF16)
        _accumulate(dgp_ref, _rows_to_sublanes(de * en), first)
        den = de * gp_ref[...]
        dpe_ref[...] = (r3 * (den - en * _mean_last(den * en))).astype(BF16)

    row = pl.BlockSpec((tm, D), lambda i, j, k: (i, 0))
    vec = pl.BlockSpec((1, D), lambda i, j, k: (0, 0))
    part = pl.BlockSpec((SUBLANES, D), lambda i, j, k: (0, 0))
    return _mm("gate_tail", gn, w_gate, _NN, (T // tm, 1, 1), row, pl.BlockSpec(w_gate.shape, lambda i, j, k: (0, 0)),
               [jax.ShapeDtypeStruct((T, D), F32), jax.ShapeDtypeStruct((T, D), BF16),
                jax.ShapeDtypeStruct((T, D), BF16), jax.ShapeDtypeStruct((SUBLANES, D), F32),
                jax.ShapeDtypeStruct((SUBLANES, D), F32), jax.ShapeDtypeStruct((SUBLANES, LANES), F32)],
               [row, row, row, part, part, pl.BlockSpec((SUBLANES, LANES), lambda i, j, k: (0, 0))], (tm, D), epilogue,
               (h2, pe, target, g_ple, g_final), [row, row, row, vec, vec],
               semantics=("arbitrary", "arbitrary", "arbitrary"))


def _rope_tables(T):
    pos = np.arange(T)
    half = HEAD_DIM // 2
    inv = (ROPE_THETA ** (-np.arange(0, half, 2, dtype=np.float32) / half)).astype(np.float32)
    ang_r = (pos // GRID_W).astype(np.float32)[:, None] * inv
    ang_c = (pos % GRID_W).astype(np.float32)[:, None] * inv
    cos = np.concatenate([np.cos(ang_r), np.cos(ang_r), np.cos(ang_c), np.cos(ang_c)], axis=-1)
    sin = np.concatenate([-np.sin(ang_r), np.sin(ang_r), -np.sin(ang_c), np.sin(ang_c)], axis=-1)
    return jnp.asarray(cos, F32), jnp.asarray(sin, F32)


def _swap32(x):
    lane = lax.broadcasted_iota(jnp.int32, x.shape, 1)
    return jnp.where((lane % 64) < 32, pltpu.roll(x, 96, 1), pltpu.roll(x, 32, 1))


def _in_proj(u, w_in, cos, sin, g_q, g_k, n_norm, tm=512):
    T, K = u.shape
    W = w_in.shape[1]
    tm = _tile(T, tm)
    n_q = n_norm * GROUP // (GROUP + 1)
    wa = n_norm * HEAD_DIM

    def epilogue(acc, extra, outs):
        c_ref, s_ref, gq_ref, gk_ref = extra
        raw_ref, o_ref = outs
        c, s = c_ref[...], s_ref[...]
        raw_ref[...] = acc[:, :wa]
        for h in range(n_norm):
            cols = slice(h * HEAD_DIM, (h + 1) * HEAD_DIM)
            xv = acc[:, cols]
            g = gq_ref[...] if h < n_q else gk_ref[...]
            xn = xv * lax.rsqrt(_mean_last(xv * xv) + EPS) * g
            o_ref[:, cols] = (xn * c + _swap32(xn) * s).astype(BF16)
        o_ref[:, wa:] = acc[:, wa:].astype(BF16)

    tab = pl.BlockSpec((tm, HEAD_DIM), lambda i, j, k: (i, 0))
    vec = pl.BlockSpec((1, HEAD_DIM), lambda i, j, k: (0, 0))
    return _mm("in_proj", u, w_in, _NN, (T // tm, 1, 1), pl.BlockSpec((tm, K), lambda i, j, k: (i, 0)),
               pl.BlockSpec((K, W), lambda i, j, k: (0, 0)),
               [jax.ShapeDtypeStruct((T, wa), F32), jax.ShapeDtypeStruct((T, W), BF16)],
               [pl.BlockSpec((tm, wa), lambda i, j, k: (i, 0)), pl.BlockSpec((tm, W), lambda i, j, k: (i, 0))],
               (tm, W), epilogue, (cos, sin, g_q, g_k), [tab, tab, vec, vec])


def _dproj(proj_a, dqa, dka_t, dva_t, dqb, dkb, dvb, cos, sin, g_q, g_k, tr=512):
    T, wa = proj_a.shape
    tr = _tile(T, tr)
    n_q = dqa.shape[1] // HEAD_DIM
    wkv = dka_t.shape[0]
    W = wa + wkv + dqb.shape[1] + dkb.shape[1] + dvb.shape[1]

    def body(p_ref, dqa_ref, dkat_ref, dvat_ref, dqb_ref, dkb_ref, dvb_ref, c_ref, s_ref, gq_ref, gk_ref,
             o_ref, dgq_ref, dgk_ref):
        c, s = c_ref[...], s_ref[...]
        dka = dkat_ref[...].T
        dgq = jnp.zeros((SUBLANES, HEAD_DIM), F32)
        dgk = jnp.zeros((SUBLANES, HEAD_DIM), F32)
        for h in range(wa // HEAD_DIM):
            cols = slice(h * HEAD_DIM, (h + 1) * HEAD_DIM)
            xv = p_ref[:, cols]
            r = lax.rsqrt(_mean_last(xv * xv) + EPS)
            xn = xv * r
            if h < n_q:
                d = dqa_ref[:, cols]
                g = gq_ref[...]
            else:
                d = dka[:, (h - n_q) * HEAD_DIM:(h - n_q + 1) * HEAD_DIM]
                g = gk_ref[...]
            dqn = d * c + _swap32(d * s)
            part = _rows_to_sublanes(dqn * xn)
            if h < n_q:
                dgq = dgq + part
            else:
                dgk = dgk + part
            dxn = dqn * g
            o_ref[:, cols] = (r * (dxn - xn * _mean_last(dxn * xn))).astype(BF16)
        o_ref[:, wa:wa + wkv] = dvat_ref[...].T.astype(BF16)
        off = wa + wkv
        for ref in (dqb_ref, dkb_ref, dvb_ref):
            w = ref.shape[1]
            o_ref[:, off:off + w] = ref[...].astype(BF16)
            off += w
        first = pl.program_id(0) == 0
        _accumulate(dgq_ref, dgq, first)
        _accumulate(dgk_ref, dgk, first)

    def row(w):
        return pl.BlockSpec((tr, w), lambda i: (i, 0))

    col = pl.BlockSpec((wkv, tr), lambda i: (0, i))
    vec = pl.BlockSpec((1, HEAD_DIM), lambda i: (0, 0))
    part = pl.BlockSpec((SUBLANES, HEAD_DIM), lambda i: (0, 0))
    return pl.pallas_call(
        body, name="dproj", grid=(T // tr,),
        in_specs=[row(wa), row(dqa.shape[1]), col, col, row(dqb.shape[1]),
                  row(dkb.shape[1]), row(dvb.shape[1]), row(HEAD_DIM), row(HEAD_DIM), vec, vec],
        out_specs=[row(W), part, part],
        out_shape=[jax.ShapeDtypeStruct((T, W), BF16), jax.ShapeDtypeStruct((SUBLANES, HEAD_DIM), F32),
                   jax.ShapeDtypeStruct((SUBLANES, HEAD_DIM), F32)],
        compiler_params=_params(("arbitrary",)),
    )(proj_a, dqa, dka_t, dva_t, dqb, dkb, dvb, cos, sin, g_q, g_k)


def _attn_a_fwd(pb, n_q, n_kv, out_heads, tq=1024, tc=2048, halves=2):
    T = pb.shape[0]
    tq, tc = _tile(T, tq), _tile(T, tc)
    th = tq // halves
    scale = HEAD_DIM ** -0.5
    c = scale * LOG2E

    def body(q_ref, k_ref, v_ref, o_ref, lse_ref):
        qs = [q_ref[h * th:(h + 1) * th, :] for h in range(halves)]
        m, l, acc = [None] * halves, [None] * halves, [None] * halves
        for j in range(T // tc):
            keys = slice(j * tc, (j + 1) * tc)
            kc, vc = k_ref[keys, :], v_ref[keys, :]
            for h in range(halves):
                s = lax.dot_general(qs[h], kc, _NT, preferred_element_type=F32)
                mj = jnp.max(s, axis=-1, keepdims=True)
                m_new = mj if j == 0 else jnp.maximum(m[h], mj)
                p = jnp.exp2((s - m_new) * c)
                pv = lax.dot_general(p.astype(BF16), vc, _NN, preferred_element_type=F32)
                if j == 0:
                    l[h], acc[h] = jnp.sum(p, axis=-1, keepdims=True), pv
                else:
                    alpha = jnp.exp2((m[h] - m_new) * c)
                    l[h] = alpha * l[h] + jnp.sum(p, axis=-1, keepdims=True)
                    acc[h] = alpha * acc[h] + pv
                m[h] = m_new
        for h in range(halves):
            rows = slice(h * th, (h + 1) * th)
            o_ref[rows, :] = (acc[h] / l[h]).astype(BF16)
            lse_ref[rows, :] = m[h] * scale + jnp.log(l[h])

    return pl.pallas_call(
        body, name="attn_a_fwd", grid=(n_kv, GROUP, T // tq),
        in_specs=[pl.BlockSpec((tq, HEAD_DIM), lambda kv, g, i: (i, kv * GROUP + g)),
                  pl.BlockSpec((T, HEAD_DIM), lambda kv, g, i: (0, n_q + kv)),
                  pl.BlockSpec((T, HEAD_DIM), lambda kv, g, i: (0, n_q + n_kv + kv))],
        out_specs=[pl.BlockSpec((tq, HEAD_DIM), lambda kv, g, i: (i, kv * GROUP + g)),
                   pl.BlockSpec((None, tq, 1), lambda kv, g, i: (kv * GROUP + g, i, 0))],
        out_shape=[jax.ShapeDtypeStruct((T, out_heads * HEAD_DIM), BF16), jax.ShapeDtypeStruct((n_q, T, 1), F32)],
        compiler_params=_params(("parallel", "parallel", "parallel")),
    )(pb, pb, pb)


def _attn_a_bwd(pb, o_cat, d_o, lse, n_q, n_kv, tq=4096, tc=256, halves=4):
    T = pb.shape[0]
    tq, tc = _tile(T, tq), _tile(T, tc)
    th = tq // halves
    scale = HEAD_DIM ** -0.5
    c = scale * LOG2E

    def body(q_ref, k_ref, v_ref, o_ref, do_ref, lse_ref, dq_ref, dkt_ref, dvt_ref):
        @pl.when(jnp.logical_and(pl.program_id(1) == 0, pl.program_id(2) == 0))
        def _():
            dkt_ref[...] = jnp.zeros(dkt_ref.shape, F32)
            dvt_ref[...] = jnp.zeros(dvt_ref.shape, F32)

        groups = []
        for h in range(halves):
            rows = slice(h * th, (h + 1) * th)
            q, do = q_ref[rows, :], do_ref[rows, :]
            delta = jnp.sum(do.astype(F32) * o_ref[rows, :].astype(F32), axis=-1, keepdims=True)
            groups.append((q, do, q.T, do.T, delta, lse_ref[rows, :] * LOG2E))
        dq = [None] * halves
        for j in range(T // tc):
            keys = slice(j * tc, (j + 1) * tc)
            kc, vc = k_ref[keys, :], v_ref[keys, :]
            for h, (q, do, qt, dot, delta, lse2) in enumerate(groups):
                s = lax.dot_general(q, kc, _NT, preferred_element_type=F32)
                p = jnp.exp2(s * c - lse2)
                dp = lax.dot_general(do, vc, _NT, preferred_element_type=F32)
                ds = (p * (dp - delta) * scale).astype(BF16)
                dqj = lax.dot_general(ds, kc, _NN, preferred_element_type=F32)
                dq[h] = dqj if dq[h] is None else dq[h] + dqj
                dvt_ref[:, keys] += lax.dot_general(dot, p.astype(BF16), _NN, preferred_element_type=F32)
                dkt_ref[:, keys] += lax.dot_general(qt, ds, _NN, preferred_element_type=F32)
        for h in range(halves):
            dq_ref[h * th:(h + 1) * th, :] = dq[h]

    qmap = lambda kv, g, i: (i, kv * GROUP + g)
    return pl.pallas_call(
        body, name="attn_a_bwd", grid=(n_kv, GROUP, T // tq),
        in_specs=[pl.BlockSpec((tq, HEAD_DIM), qmap),
                  pl.BlockSpec((T, HEAD_DIM), lambda kv, g, i: (0, n_q + kv)),
                  pl.BlockSpec((T, HEAD_DIM), lambda kv, g, i: (0, n_q + n_kv + kv)),
                  pl.BlockSpec((tq, HEAD_DIM), qmap),
                  pl.BlockSpec((tq, HEAD_DIM), qmap),
                  pl.BlockSpec((None, tq, 1), lambda kv, g, i: (kv * GROUP + g, i, 0))],
        out_specs=[pl.BlockSpec((tq, HEAD_DIM), qmap),
                   pl.BlockSpec((HEAD_DIM, T), lambda kv, g, i: (kv, 0)),
                   pl.BlockSpec((HEAD_DIM, T), lambda kv, g, i: (kv, 0))],
        out_shape=[jax.ShapeDtypeStruct((T, n_q * HEAD_DIM), F32),
                   jax.ShapeDtypeStruct((n_kv * HEAD_DIM, T), F32),
                   jax.ShapeDtypeStruct((n_kv * HEAD_DIM, T), F32)],
        compiler_params=_params(("parallel", "arbitrary", "arbitrary")),
    )(pb, pb, pb, o_cat, d_o, lse)


def _bucket_index():
    r = np.arange(BLOCK_Q)[:, None]
    j = np.arange(3 * BLOCK_Q)[None, :]
    rel = (j - BLOCK_Q) - r
    nb = N_BUCKETS // 2
    ret = np.where(rel > 0, nb, 0)
    n = np.abs(rel)
    max_exact = nb // 2
    nf = np.maximum(n, 1).astype(np.float32)
    large = max_exact + (np.log(nf / max_exact) / math.log(MAX_DISTANCE / max_exact) * (nb - max_exact)).astype(np.int32)
    large = np.minimum(large, nb - 1)
    return jnp.asarray(ret + np.where(n < max_exact, n, large), jnp.int32)


def _bias_build(idx, table_flat, n_heads, deps=()):
    def body(idx_ref, tab_ref, o_ref):
        h = pl.program_id(0)
        iv = idx_ref[...]
        acc = jnp.zeros(iv.shape, F32)
        for b in range(N_BUCKETS):
            acc = jnp.where(iv == b, tab_ref[b * n_heads + h], acc)
        r = lax.broadcasted_iota(jnp.int32, iv.shape, 0)
        j = lax.broadcasted_iota(jnp.int32, iv.shape, 1)
        o_ref[...] = jnp.where(jnp.abs(j - BLOCK_Q - r) <= WINDOW, acc, NEG_INF)

    return _pcall(
        body, deps, name="bias_build", grid=(n_heads,),
        in_specs=[pl.BlockSpec(idx.shape, lambda h: (0, 0)), pl.BlockSpec(memory_space=pltpu.SMEM)],
        out_specs=pl.BlockSpec((None,) + idx.shape, lambda h: (h, 0, 0)),
        out_shape=jax.ShapeDtypeStruct((n_heads,) + idx.shape, F32),
        compiler_params=_params(("parallel",)),
    )(idx, table_flat)


def _in_sequence(n, T):
    j = lax.broadcasted_iota(jnp.int32, (GROUP * BLOCK_Q, 3 * BLOCK_Q), 1)
    kabs = n * BLOCK_Q + j - BLOCK_Q
    return (kabs >= 0) & (kabs < T)


def _per_head_rows(values):
    head = lax.broadcasted_iota(jnp.int32, (GROUP * BLOCK_Q, 1), 0) // BLOCK_Q
    col = jnp.zeros((GROUP * BLOCK_Q, 1), F32)
    for g, v in enumerate(values):
        col = jnp.where(head == g, v, col)
    return col


def _band_specs(col, nblk, sb):
    return [pl.BlockSpec((BLOCK_Q, HEAD_DIM), lambda kv, i: (jnp.maximum(sb * i - 1, 0), col(kv))),
            pl.BlockSpec((sb * BLOCK_Q, HEAD_DIM), lambda kv, i: (i, col(kv))),
            pl.BlockSpec((BLOCK_Q, HEAD_DIM), lambda kv, i: (jnp.minimum(sb * i + sb, nblk - 1), col(kv)))]


def _head_specs(base, rows):
    return [pl.BlockSpec((rows, HEAD_DIM), functools.partial(lambda kv, i, g: (i, base + kv * GROUP + g), g=g))
            for g in range(GROUP)]


def _attn_b_fwd(pb, bias, sink, o_all, q_off, n_q, n_kv, deps=(), sb=16):
    T = pb.shape[0]
    nblk = T // BLOCK_Q
    sb = min(sb, nblk)
    tq = sb * BLOCK_Q
    scale = HEAD_DIM ** -0.5

    def body(*refs):
        q_refs = refs[0:GROUP]
        k_refs, v_refs = refs[GROUP:GROUP + 3], refs[GROUP + 3:GROUP + 6]
        bias_ref, sink_ref, o_ref, lse_ref = refs[GROUP + 6:]
        kv, i = pl.program_id(0), pl.program_id(1)
        kb = jnp.concatenate([r[...] for r in k_refs], axis=0)
        vb = jnp.concatenate([r[...] for r in v_refs], axis=0)
        bias_all = bias_ref[...].reshape(GROUP * BLOCK_Q, 3 * BLOCK_Q)
        sk = _per_head_rows([sink_ref[kv * GROUP + g] for g in range(GROUP)])
        for b in range(sb):
            rows = slice(b * BLOCK_Q, (b + 1) * BLOCK_Q)
            kw, vw = kb[b * BLOCK_Q:(b + 3) * BLOCK_Q], vb[b * BLOCK_Q:(b + 3) * BLOCK_Q]
            q = jnp.concatenate([r[rows, :] for r in q_refs], axis=0)
            s = lax.dot_general(q, kw, _NT, preferred_element_type=F32) * scale + bias_all
            if b == 0 or b == sb - 1:
                s = jnp.where(_in_sequence(i * sb + b, T), s, NEG_INF)
            m = jnp.maximum(jnp.max(s, axis=-1, keepdims=True), sk)
            p = jnp.exp(s - m)
            l = jnp.sum(p, axis=-1, keepdims=True) + jnp.exp(sk - m)
            o = (lax.dot_general(p.astype(BF16), vw, _NN, preferred_element_type=F32) / l).astype(BF16)
            lse = m + jnp.log(l)
            for g in range(GROUP):
                head = slice(g * BLOCK_Q, (g + 1) * BLOCK_Q)
                o_ref[rows, g * HEAD_DIM:(g + 1) * HEAD_DIM] = o[head]
                lse_ref[g, rows, :] = lse[head]

    first_group = o_all.shape[1] // (GROUP * HEAD_DIM) - n_kv
    return _pcall(
        body, deps, into=(o_all, 0), name="attn_b_fwd", grid=(n_kv, nblk // sb),
        in_specs=[*_head_specs(q_off, tq),
                  *_band_specs(lambda kv: q_off + n_q + kv, nblk, sb),
                  *_band_specs(lambda kv: q_off + n_q + n_kv + kv, nblk, sb),
                  pl.BlockSpec((GROUP, BLOCK_Q, 3 * BLOCK_Q), lambda kv, i: (kv, 0, 0)),
                  pl.BlockSpec(memory_space=pltpu.SMEM)],
        out_specs=[pl.BlockSpec((tq, GROUP * HEAD_DIM), lambda kv, i: (i, first_group + kv)),
                   pl.BlockSpec((GROUP, tq, 1), lambda kv, i: (kv, i, 0))],
        out_shape=[jax.ShapeDtypeStruct(o_all.shape, BF16), jax.ShapeDtypeStruct((n_q, T, 1), F32)],
        compiler_params=_params(("parallel", "parallel")),
    )(*([pb] * (GROUP + 6)), bias, sink)


def _attn_b_bwd(pb, o_cat, d_o, lse, bias, sink, q_off, n_q, n_kv, o_off, deps=(), sb=16):
    T = pb.shape[0]
    nblk = T // BLOCK_Q
    sb = min(sb, nblk)
    tq = sb * BLOCK_Q
    scale = HEAD_DIM ** -0.5

    def body(*refs):
        q_refs = refs[0:GROUP]
        k_refs, v_refs = refs[GROUP:GROUP + 3], refs[GROUP + 3:GROUP + 6]
        o_refs, do_refs = refs[GROUP + 6:2 * GROUP + 6], refs[2 * GROUP + 6:3 * GROUP + 6]
        lse_ref, bias_ref, sink_ref, dq_ref, dk_ref, dv_ref, dbias_ref, dsink_ref, dkb_ref, dvb_ref = refs[3 * GROUP + 6:]
        kv, i = pl.program_id(0), pl.program_id(1)
        first = i == 0

        @pl.when(first)
        def _():
            dk_ref[...] = jnp.zeros(dk_ref.shape, F32)
            dv_ref[...] = jnp.zeros(dv_ref.shape, F32)
            dbias_ref[...] = jnp.zeros(dbias_ref.shape, F32)

        kb = jnp.concatenate([r[...] for r in k_refs], axis=0)
        vb = jnp.concatenate([r[...] for r in v_refs], axis=0)
        dkb_ref[...] = jnp.zeros(dkb_ref.shape, F32)
        dvb_ref[...] = jnp.zeros(dvb_ref.shape, F32)
        row = lax.broadcasted_iota(jnp.int32, (SUBLANES, LANES), 0)
        dsink = jnp.zeros((SUBLANES, LANES), F32)
        bias_all = bias_ref[...].reshape(GROUP * BLOCK_Q, 3 * BLOCK_Q)
        sk = _per_head_rows([sink_ref[kv * GROUP + g] for g in range(GROUP)])
        for b in range(sb):
            rows = slice(b * BLOCK_Q, (b + 1) * BLOCK_Q)
            win = slice(b * BLOCK_Q, (b + 3) * BLOCK_Q)
            kw, vw = kb[win], vb[win]
            q = jnp.concatenate([r[rows, :] for r in q_refs], axis=0)
            do = jnp.concatenate([r[rows, :] for r in do_refs], axis=0)
            o = jnp.concatenate([r[rows, :] for r in o_refs], axis=0)
            lse = jnp.concatenate([lse_ref[g, rows, :] for g in range(GROUP)], axis=0)
            delta = jnp.sum(do.astype(F32) * o.astype(F32), axis=-1, keepdims=True)
            s = lax.dot_general(q, kw, _NT, preferred_element_type=F32) * scale + bias_all
            if b == 0 or b == sb - 1:
                s = jnp.where(_in_sequence(i * sb + b, T), s, NEG_INF)
            p = jnp.exp(s - lse)
            dp = lax.dot_general(do, vw, _NT, preferred_element_type=F32)
            ds = p * (dp - delta)
            dbias_ref[...] += ds.reshape(GROUP, BLOCK_Q, 3 * BLOCK_Q)
            sunk = jnp.exp(sk - lse) * delta
            for g in range(GROUP):
                dsink = dsink + jnp.where(row == g, -jnp.sum(sunk[g * BLOCK_Q:(g + 1) * BLOCK_Q]), 0.0)
            dsb = (ds * scale).astype(BF16)
            dq = lax.dot_general(dsb, kw, _NN, preferred_element_type=F32).astype(BF16)
            for g in range(GROUP):
                dq_ref[rows, g * HEAD_DIM:(g + 1) * HEAD_DIM] = dq[g * BLOCK_Q:(g + 1) * BLOCK_Q]
            dkb_ref[win, :] += lax.dot_general(dsb, q, _TN, preferred_element_type=F32)
            dvb_ref[win, :] += lax.dot_general(p.astype(BF16), do, _TN, preferred_element_type=F32)
        _accumulate(dsink_ref, dsink, first)

        before = pl.ds(pl.multiple_of(jnp.maximum(sb * i - 1, 0) * BLOCK_Q, BLOCK_Q), BLOCK_Q)
        own = pl.ds(pl.multiple_of(i * tq, BLOCK_Q), tq)
        after = pl.ds(pl.multiple_of(jnp.minimum(sb * i + sb, nblk - 1) * BLOCK_Q, BLOCK_Q), BLOCK_Q)
        for acc_ref, band_ref in ((dk_ref, dkb_ref), (dv_ref, dvb_ref)):
            acc_ref[before, :] += band_ref[0:BLOCK_Q, :]
            acc_ref[own, :] += band_ref[BLOCK_Q:BLOCK_Q + tq, :]
            acc_ref[after, :] += band_ref[BLOCK_Q + tq:, :]

    return _pcall(
        body, deps, name="attn_b_bwd", grid=(n_kv, nblk // sb),
        in_specs=[*_head_specs(q_off, tq),
                  *_band_specs(lambda kv: q_off + n_q + kv, nblk, sb),
                  *_band_specs(lambda kv: q_off + n_q + n_kv + kv, nblk, sb),
                  *_head_specs(o_off, tq), *_head_specs(o_off, tq),
                  pl.BlockSpec((GROUP, tq, 1), lambda kv, i: (kv, i, 0)),
                  pl.BlockSpec((GROUP, BLOCK_Q, 3 * BLOCK_Q), lambda kv, i: (kv, 0, 0)),
                  pl.BlockSpec(memory_space=pltpu.SMEM)],
        out_specs=[pl.BlockSpec((tq, GROUP * HEAD_DIM), lambda kv, i: (i, kv)),
                   pl.BlockSpec((T, HEAD_DIM), lambda kv, i: (0, kv)),
                   pl.BlockSpec((T, HEAD_DIM), lambda kv, i: (0, kv)),
                   pl.BlockSpec((GROUP, BLOCK_Q, 3 * BLOCK_Q), lambda kv, i: (kv, 0, 0)),
                   pl.BlockSpec((None, SUBLANES, LANES), lambda kv, i: (kv, 0, 0))],
        out_shape=[jax.ShapeDtypeStruct((T, n_q * HEAD_DIM), BF16),
                   jax.ShapeDtypeStruct((T, n_kv * HEAD_DIM), F32),
                   jax.ShapeDtypeStruct((T, n_kv * HEAD_DIM), F32),
                   jax.ShapeDtypeStruct((n_q, BLOCK_Q, 3 * BLOCK_Q), F32),
                   jax.ShapeDtypeStruct((n_kv, SUBLANES, LANES), F32)],
        scratch_shapes=[pltpu.VMEM((tq + 2 * BLOCK_Q, HEAD_DIM), F32), pltpu.VMEM((tq + 2 * BLOCK_Q, HEAD_DIM), F32)],
        compiler_params=_params(("parallel", "arbitrary")),
    )(*([pb] * (GROUP + 6)), *([o_cat] * GROUP), *([d_o] * GROUP), lse, bias, sink)


def _table_grads(dbias, dsink_raw, idx):
    n_heads = dbias.shape[0]
    n_kv = dsink_raw.shape[0]

    def body(db_ref, ds_ref, idx_ref, dt_ref, dsk_ref):
        iv = idx_ref[...]
        row = lax.broadcasted_iota(jnp.int32, (SUBLANES, LANES), 0)
        lane = lax.broadcasted_iota(jnp.int32, (SUBLANES, LANES), 1)
        dsk = jnp.zeros((SUBLANES, LANES), F32)
        for h in range(n_heads):
            d = db_ref[h]
            acc = jnp.zeros((SUBLANES, LANES), F32)
            for b in range(N_BUCKETS):
                acc = jnp.where((row == 0) & (lane == b), jnp.sum(jnp.where(iv == b, d, 0.0)), acc)
            dt_ref[:, h * LANES:(h + 1) * LANES] = acc
            raw = ds_ref[h // GROUP]
            val = jnp.sum(jnp.where((row == h % GROUP) & (lane == 0), raw, 0.0))
            dsk = jnp.where((row == 0) & (lane == h), val, dsk)
        dsk_ref[...] = dsk

    return pl.pallas_call(
        body, name="table_grads",
        in_specs=[pl.BlockSpec(memory_space=pltpu.VMEM)] * 3,
        out_specs=[pl.BlockSpec(memory_space=pltpu.VMEM)] * 2,
        out_shape=[jax.ShapeDtypeStruct((SUBLANES, n_heads * LANES), F32),
                   jax.ShapeDtypeStruct((SUBLANES, LANES), F32)],
        compiler_params=_params(),
    )(dbias, dsink_raw, idx)


def _position():
    x, y, c = lax.axis_index("x"), lax.axis_index("y"), lax.axis_index("c")
    return x, y, c


def _hbm(a):
    return pltpu.with_memory_space_constraint(a, pltpu.HBM)


def _split_start(name, bufs, sem_shapes, issue):
    nb, ns = len(bufs), len(sem_shapes)

    def body(*refs):
        buf_refs = refs[:nb]
        sems = refs[nb:nb + ns]
        token = refs[nb + ns + nb]
        issue(buf_refs, sems)
        token[...] = jnp.zeros(token.shape, F32)

    outs = pl.pallas_call(
        body, name=name,
        in_specs=[_HBM] * nb,
        out_specs=[_SEM] * ns + [_HBM] * nb + [_VMEM],
        out_shape=[pltpu.SemaphoreType.DMA(s) for s in sem_shapes] + [pltpu.HBM(b.shape, b.dtype) for b in bufs]
        + [jax.ShapeDtypeStruct((SUBLANES, LANES), F32)],
        input_output_aliases={i: ns + i for i in range(nb)},
        compiler_params=pltpu.CompilerParams(has_side_effects=_EFFECT),
    )(*[_hbm(b) for b in bufs])
    return outs[:ns], outs[ns:ns + nb], outs[-1]


def _split_wait(name, bufs, send, recv, counts, size_of, after):
    nb = len(bufs)

    def body(*refs):
        buf_refs = refs[:nb]
        send_ref, recv_ref = refs[nb], refs[nb + 1]
        x, y, c = _position()
        for w, n in enumerate(counts):
            ref = size_of(buf_refs, w)
            for k in range(n):
                s = sum(counts[:w]) + k
                cp = pltpu.make_async_remote_copy(
                    src_ref=ref, dst_ref=ref, send_sem=send_ref.at[s], recv_sem=recv_ref.at[s],
                    device_id=(x, y, c), device_id_type=MESH)
                cp.wait_send()
                cp.wait_recv()

    return pl.pallas_call(
        body, name=name,
        in_specs=[_HBM] * nb + [_SEM, _SEM, _ANY],
        out_specs=[_HBM] * nb,
        out_shape=[pltpu.HBM(b.shape, b.dtype) for b in bufs],
        input_output_aliases={i: i for i in range(nb)},
        compiler_params=pltpu.CompilerParams(has_side_effects=_EFFECT),
    )(*bufs, send, recv, after)


def _block_of(pos):
    return 4 * pos[0] + 2 * pos[1] + pos[2]


def _shard_of(ref, blk, by_cols):
    aligned = (lambda v, a: v) if isinstance(blk, int) else pl.multiple_of
    if by_cols:
        n = ref.shape[1] // N_DEV
        return ref.at[:, pl.ds(aligned(blk * n, LANES), n)]
    r = ref.shape[0] // N_DEV
    return ref.at[pl.ds(aligned(blk * r, SUBLANES), r), :]


def _place_own(name, land, shard, by_cols, tr=256):
    r, n = shard.shape
    tr = _tile(r, tr)
    mine = _block_of(_position()).astype(jnp.int32).reshape(1)

    def body(m_ref, land_ref, s_ref, o_ref):
        o_ref[...] = s_ref[...]

    if by_cols:
        out = pl.BlockSpec((tr, n), lambda i, m_ref: (i, m_ref[0]))
    else:
        out = pl.BlockSpec((tr, n), lambda i, m_ref: (m_ref[0] * (r // tr) + i, 0))
    return pl.pallas_call(
        body, name=name,
        grid_spec=pltpu.PrefetchScalarGridSpec(
            num_scalar_prefetch=1, grid=(r // tr,),
            in_specs=[_ANY, pl.BlockSpec((tr, n), lambda i, m_ref: (i, 0))], out_specs=out),
        out_shape=jax.ShapeDtypeStruct(land.shape, land.dtype),
        input_output_aliases={1: 0},
        compiler_params=_params(("parallel",)),
    )(mine, land, shard)


def _gather_start(name, shards, by_cols, groups, after=None):
    nw = len(shards)
    lands = [lax.empty((s.shape[0], s.shape[1] * N_DEV) if cols else (s.shape[0] * N_DEV, s.shape[1]), s.dtype)
             for s, cols in zip(shards, by_cols)]
    order = [] if after is None else [after]

    def issue(bufs, sems):
        x, y, c = _position()
        peers = [(x, y, 1 - c), (1 - x, y, c), (x, 1 - y, c), (1 - x, 1 - y, c)]
        for gi, grp in enumerate(groups):
            for wi, w in enumerate(grp):
                for k, peer in enumerate(peers):
                    pltpu.make_async_remote_copy(
                        src_ref=bufs[w], dst_ref=_shard_of(bufs[nw + w], _block_of((x, y, c)), by_cols[w]),
                        send_sem=sems[2 * gi].at[4 * wi + k], recv_sem=sems[2 * gi + 1].at[4 * wi + k],
                        device_id=peer, device_id_type=MESH).start()

    sem_shapes = [(4 * len(g),) for g in groups for _ in range(2)]
    sems, thru, token = _split_start(name, list(shards) + lands + order, sem_shapes, issue)
    return sems, thru[:nw], thru[nw:2 * nw], token


def _gather_forward(name, lands, by_cols):
    nw = len(lands)

    def issue(land, sems):
        x, y, c = _position()
        for w in range(nw):
            for k, chip in enumerate([(1 - x, y), (x, 1 - y), (1 - x, 1 - y)]):
                blk = _shard_of(land[w], _block_of((*chip, c)), by_cols[w])
                pltpu.make_async_remote_copy(
                    src_ref=blk, dst_ref=blk, send_sem=sems[0].at[3 * w + k], recv_sem=sems[1].at[3 * w + k],
                    device_id=(x, y, 1 - c), device_id_type=MESH).start()

    return _split_start(name, lands, [(3 * nw,), (3 * nw,)], issue)


def _first_block(bufs, w, offset=0):
    return bufs[offset + w].at[0]


_PEER_FLIPS = ((0, 0, 1), (1, 0, 0), (1, 0, 1), (0, 1, 0), (0, 1, 1), (1, 1, 0), (1, 1, 1))


def _scatter_start(name, grads, by_cols):
    nw = len(grads)
    lands = []
    for g, cols in zip(grads, by_cols):
        shard = (g.shape[0], g.shape[1] // N_DEV) if cols else (g.shape[0] // N_DEV, g.shape[1])
        lands.append(lax.empty((N_DEV,) + shard, g.dtype))

    def issue(bufs, sems):
        x, y, c = _position()
        flip = lambda v, f: 1 - v if f else v
        for w in range(nw):
            for k, (fx, fy, fc) in enumerate(_PEER_FLIPS):
                peer = (flip(x, fx), flip(y, fy), flip(c, fc))
                pltpu.make_async_remote_copy(
                    src_ref=_shard_of(bufs[w], _block_of(peer), by_cols[w]), dst_ref=bufs[nw + w].at[_block_of((x, y, c))],
                    send_sem=sems[0].at[7 * w + k], recv_sem=sems[1].at[7 * w + k],
                    device_id=peer, device_id_type=MESH).start()

    return _split_start(name, list(grads) + lands, [(7 * nw,), (7 * nw,)], issue)


def _adam(w, g, m, v):
    m = ADAM_B1 * m + (1.0 - ADAM_B1) * g
    v = ADAM_B2 * v + (1.0 - ADAM_B2) * (g * g)
    m_hat = m / (1.0 - ADAM_B1 ** ADAM_STEP)
    v_hat = v / (1.0 - ADAM_B2 ** ADAM_STEP)
    delta = -ADAM_LR * (m_hat / (jnp.sqrt(v_hat) + ADAM_EPS) + ADAM_WD * w)
    return delta, m, v


def _sum_adam(name, landed, grad, by_cols, w, m, v, tr=256):
    R, C = w.shape
    tr = _tile(R, tr if C > 1024 else 2 * tr)
    mine = _block_of(_position()).astype(jnp.int32).reshape(1)

    def body(me_ref, l_ref, own_ref, w_ref, m_ref, v_ref, g_ref, d_ref, nm_ref, nv_ref):
        own = own_ref[...].astype(F32)
        g = None
        for d in range(N_DEV):
            part = jnp.where(me_ref[0] == d, own, l_ref[d].astype(F32))
            g = part if g is None else g + part
        g_ref[...] = g
        d_ref[...], nm_ref[...], nv_ref[...] = _adam(w_ref[...], g, m_ref[...], v_ref[...])

    tile = pl.BlockSpec((tr, C), lambda i, me_ref: (i, 0))
    if by_cols:
        own = pl.BlockSpec((tr, C), lambda i, me_ref: (i, me_ref[0]))
    else:
        own = pl.BlockSpec((tr, C), lambda i, me_ref: (me_ref[0] * (R // tr) + i, 0))
    return pl.pallas_call(
        body, name=name,
        grid_spec=pltpu.PrefetchScalarGridSpec(
            num_scalar_prefetch=1, grid=(R // tr,),
            in_specs=[pl.BlockSpec((N_DEV, tr, C), lambda i, me_ref: (0, i, 0)), own, tile, tile, tile],
            out_specs=[tile] * 4),
        out_shape=[jax.ShapeDtypeStruct((R, C), F32)] * 4,
        compiler_params=_params(("parallel",)),
    )(mine, landed, grad, w, m, v)


def _small_all_reduce(parts, deps=()):
    W = parts.shape[1]

    def body(p_ref, o_ref, slots, send_sems, recv_sems):
        x, y, c = _position()
        me = 4 * x + 2 * y + c
        slots[me] = jnp.sum(p_ref[...], axis=0, keepdims=True)
        peers = [(x, y, 1 - c), (1 - x, y, c), (1 - x, y, 1 - c), (x, 1 - y, c), (x, 1 - y, 1 - c),
                 (1 - x, 1 - y, c), (1 - x, 1 - y, 1 - c)]
        copies = []
        for k, peer in enumerate(peers):
            cp = pltpu.make_async_remote_copy(
                src_ref=slots.at[me], dst_ref=slots.at[me], send_sem=send_sems.at[k], recv_sem=recv_sems.at[k],
                device_id=peer, device_id_type=MESH)
            cp.start()
            copies.append(cp)
        for cp in copies:
            cp.wait()
        total = slots[0]
        for d in range(1, N_DEV):
            total = total + slots[d]
        o_ref[...] = total

    return _pcall(
        body, deps, name="small_all_reduce",
        in_specs=[pl.BlockSpec(memory_space=pltpu.VMEM)], out_specs=pl.BlockSpec(memory_space=pltpu.VMEM),
        out_shape=jax.ShapeDtypeStruct((1, W), F32),
        scratch_shapes=[pltpu.VMEM((N_DEV, 1, W), F32), pltpu.SemaphoreType.DMA((7,)), pltpu.SemaphoreType.DMA((7,))],
    )(parts)


def _adam_small(w, g, m, v):
    def body(w_ref, g_ref, m_ref, v_ref, d_ref, nm_ref, nv_ref):
        d_ref[...], nm_ref[...], nv_ref[...] = _adam(w_ref[...], g_ref[...], m_ref[...], v_ref[...])

    return pl.pallas_call(
        body, name="adam_small",
        in_specs=[pl.BlockSpec(memory_space=pltpu.VMEM)] * 4, out_specs=[pl.BlockSpec(memory_space=pltpu.VMEM)] * 3,
        out_shape=[jax.ShapeDtypeStruct(w.shape, F32)] * 3,
    )(w, g, m, v)


_GATHER_GROUPS = (("w_in",), ("w_out", "w_up", "ple_w"), ("w_down", "w_gate"))
_COL_SHARDED = ("w_in", "w_up", "ple_w")


class _MeshComm:
    def __init__(self, w, mom, var):
        self.w, self.mom, self.var = w, mom, var
        self.out = {}
        self._scatters = {}

    def gather_begin(self):
        self._groups = {}
        token = None
        for tag, first, group_list in (("gather_start0", 0, _GATHER_GROUPS[:1]), ("gather_start1", 1, _GATHER_GROUPS[1:])):
            names = [n for g in group_list for n in g]
            idx = {n: i for i, n in enumerate(names)}
            by_cols = [n in _COL_SHARDED for n in names]
            sems, src, lands, token = _gather_start(tag, [self.w[n].astype(BF16) for n in names], by_cols,
                                                    [[idx[n] for n in g] for g in group_list], token)
            lands = [_place_own("place_" + n, land, s, cols) for n, land, s, cols in zip(names, lands, src, by_cols)]
            for k, g in enumerate(group_list):
                self._groups[first + k] = (sems[2 * k], sems[2 * k + 1], [src[idx[n]] for n in g],
                                           [lands[idx[n]] for n in g])
        return token

    @staticmethod
    def _shard_size(names, offset):
        return lambda bufs, w: _shard_of(bufs[offset + w], 0, names[w] in _COL_SHARDED)

    def gather_arrive(self, gi, after):
        names = _GATHER_GROUPS[gi]
        send, recv, src, lands = self._groups[gi]
        out = _split_wait("gather_arrive%d" % gi, src + lands, send, recv, [4] * len(names),
                          self._shard_size(names, len(names)), after)
        self._arrived = out[len(names):]

    def gather_forward(self, gi):
        by_cols = [n in _COL_SHARDED for n in _GATHER_GROUPS[gi]]
        self._fsems, self._fthru, token = _gather_forward("gather_forward%d" % gi, self._arrived, by_cols)
        return token

    def gather_finish(self, gi, after):
        names = _GATHER_GROUPS[gi]
        out = _split_wait("gather_finish%d" % gi, self._fthru, self._fsems[0], self._fsems[1], [3] * len(names),
                          self._shard_size(names, 0), after)
        return dict(zip(names, out))

    def reduce_begin(self, key, grads):
        names = list(grads)
        sems, thru, token = _scatter_start("scatter_start_" + key, [grads[n] for n in names],
                                           [n in _COL_SHARDED for n in names])
        self._scatters[key] = (names, sems, thru)
        return token

    def reduce_finish(self, key, after):
        names, sems, thru = self._scatters[key]
        nw = len(names)
        out = _split_wait("scatter_wait_" + key, thru, sems[0], sems[1], [N_DEV - 1] * nw,
                          functools.partial(_first_block, offset=nw), after)
        for i, n in enumerate(names):
            self.out[n] = _sum_adam("adam_" + n, out[nw + i], out[i], n in _COL_SHARDED, self.w[n], self.mom[n],
                                    self.var[n])


def _step(x, p, target, gains, comm):
    T, D = x.shape
    n_q = D // (2 * HEAD_DIM)
    n_kv = n_q // GROUP
    cos, sin = _rope_tables(T)
    idx = _bucket_index()

    t = comm.gather_begin()
    u = _rms_fwd("norm_attn", x, gains["attn_norm_g"], deps=(t,))
    comm.gather_arrive(0, u)
    t = comm.gather_forward(0)
    bias = _bias_build(idx, gains["rel_bias_table"].reshape(-1), n_q, deps=(t,))
    full = comm.gather_finish(0, bias)
    proj_a, pb = _in_proj(u, full["w_in"], cos, sin, gains["q_norm_g"], gains["k_norm_g"], n_q + n_kv)
    o_a, lse_a = _attn_a_fwd(pb, n_q, n_kv, 2 * n_q)
    comm.gather_arrive(1, lse_a)
    t = comm.gather_forward(1)
    sink = gains["sink_logits"].reshape(-1)
    b_off = n_q + 2 * n_kv
    o_cat, lse_b = _attn_b_fwd(pb, bias, sink, o_a, b_off, n_q, n_kv, deps=(t,))
    full.update(comm.gather_finish(1, lse_b))
    h1, m_in = _mm_nn_rms("out_proj", o_cat, full["w_out"], x, gains["mlp_norm_g"])

    def up_epilogue(acc, extra, outs):
        outs[0][...] = acc.astype(BF16)
        r = jnp.maximum(acc, 0.0)
        outs[1][...] = (r * r).astype(BF16)

    a_act, f_act = _mm_nn("up_proj", m_in, full["w_up"], epilogue=up_epilogue, out_dtypes=[BF16, BF16], tn=2048)
    comm.gather_arrive(2, f_act)
    t = comm.gather_forward(2)
    p_b = p.astype(BF16)
    pe = _mm_nn("ple_proj", p_b, full["ple_w"], deps=(t,))
    full.update(comm.gather_finish(2, pe))
    h2 = _mm_nn("down_proj", f_act, full["w_down"], epilogue=_store_add, extras=(h1,), tn=512)
    gn = _rms_fwd("norm_gate", h2, gains["gate_norm_g"])

    dh3, dz, dpe, dg_final, dg_ple, loss_part = _gate_tail(gn, full["w_gate"], h2, pe, target, gains["ple_norm_g"],
                                                           gains["final_norm_g"])
    gw_gate = _mm_tn("grad_w_gate", gn, dz, tn=1024)
    gw_ple = _mm_tn("grad_ple_w", p_b, dpe)
    dh2, dh2_b, dg_gate = _mm_nt_rms_bwd("d_gate_in", dz, full["w_gate"], h2, gains["gate_norm_g"], dh3, tm=512)
    gw_down = _mm_tn("grad_w_down", f_act, dh2_b, tn=1024)
    t = comm.reduce_begin("b", dict(w_gate=gw_gate, ple_w=gw_ple, w_down=gw_down))

    def act_bwd(acc, extra, outs):
        outs[0][...] = (acc * (2.0 * jnp.maximum(extra[0][...].astype(F32), 0.0))).astype(BF16)

    da = _mm_nt("d_act", dh2_b, full["w_down"], out_dtype=BF16, epilogue=act_bwd, extras=(a_act,), tn=2048, deps=(t,))
    gw_up = _mm_tn("grad_w_up", m_in, da, tn=1024)
    dm = _mm_nt("d_mlp_in", da, full["w_up"], out_dtype=BF16, tn=512)
    dh1, dh1_b, dg_mlp = _rms_bwd("norm_mlp_bwd", dm, h1, gains["mlp_norm_g"], dh2)
    gw_out = _mm_tn("grad_w_out", o_cat, dh1_b, tn=1024)
    t = comm.reduce_begin("d", dict(w_up=gw_up, w_out=gw_out))
    d_o = _mm_nt("d_attn_out", dh1_b, full["w_out"], out_dtype=BF16, deps=(t,))
    dqa, dka_t, dva_t = _attn_a_bwd(pb, o_cat, d_o, lse_a, n_q, n_kv)
    dqb, dkb, dvb, dbias, dsink_raw = _attn_b_bwd(pb, o_cat, d_o, lse_b, bias, sink, b_off, n_q, n_kv, n_q)
    dtable, dsink = _table_grads(dbias, dsink_raw, idx)
    dproj, dg_q, dg_k = _dproj(proj_a, dqa, dka_t, dva_t, dqb, dkb, dvb, cos, sin, gains["q_norm_g"], gains["k_norm_g"])
    gw_in = _mm_tn("grad_w_in", u, dproj, tn=1024)
    t = comm.reduce_begin("e", dict(w_in=gw_in))
    dx, dg_attn = _mm_nt_rms_bwd("d_attn_in", dproj, full["w_in"], x, gains["attn_norm_g"], dh1, with_bf16=False,
                                 tm=512, deps=(t,))
    for key in "bd":
        comm.reduce_finish(key, dx)

    parts = jnp.concatenate([dg_attn, dg_mlp, dg_ple, dg_gate, dg_final, dg_q, dg_k, dtable, dsink, loss_part], axis=1)
    return dx, parts


_SHARDED = ("w_in", "w_out", "w_up", "w_down", "ple_w", "w_gate")
_VECTORS = ("attn_norm_g", "mlp_norm_g", "ple_norm_g", "gate_norm_g", "final_norm_g")
_ORDER = ("attn_norm_g", "w_in", "q_norm_g", "k_norm_g", "sink_logits", "w_out", "mlp_norm_g", "w_up", "w_down",
          "ple_w", "ple_norm_g", "gate_norm_g", "w_gate", "rel_bias_table", "final_norm_g")


def _pack_small(vals, n_heads):
    lane_pad = lambda v: jnp.pad(v, ((0, 0), (0, LANES - v.shape[1])))
    table = lane_pad(vals["rel_bias_table"].T).reshape(1, n_heads * LANES)
    return jnp.concatenate(
        [vals[n].reshape(1, -1) for n in _VECTORS] + [vals["q_norm_g"], vals["k_norm_g"], table,
                                                      lane_pad(vals["sink_logits"]), jnp.zeros((1, LANES), F32)], axis=1)


def _unpack_small(row, like, n_heads):
    out, off = {}, 0
    for n in _VECTORS:
        out[n] = row[:, off:off + like[n].size].reshape(like[n].shape)
        off += like[n].size
    for n in ("q_norm_g", "k_norm_g"):
        out[n] = row[:, off:off + LANES]
        off += LANES
    out["rel_bias_table"] = row[:, off:off + n_heads * LANES].reshape(n_heads, LANES)[:, :N_BUCKETS].T
    off += n_heads * LANES
    out["sink_logits"] = row[:, off:off + n_heads]
    off += LANES
    return out, row[0, off]


def kernel(x, p, attn_norm_g, w_in, q_norm_g, k_norm_g, sink_logits, w_out, mlp_norm_g, w_up, w_down, ple_w, ple_norm_g, gate_norm_g, w_gate, rel_bias_table, final_norm_g, loss_target, m_attn_norm_g, m_w_in, m_q_norm_g, m_k_norm_g, m_sink_logits, m_w_out, m_mlp_norm_g, m_w_up, m_w_down, m_ple_w, m_ple_norm_g, m_gate_norm_g, m_w_gate, m_rel_bias_table, m_final_norm_g, v_attn_norm_g, v_w_in, v_q_norm_g, v_k_norm_g, v_sink_logits, v_w_out, v_mlp_norm_g, v_w_up, v_w_down, v_ple_w, v_ple_norm_g, v_gate_norm_g, v_w_gate, v_rel_bias_table, v_final_norm_g):
    w = dict(attn_norm_g=attn_norm_g, w_in=w_in[0], q_norm_g=q_norm_g, k_norm_g=k_norm_g, sink_logits=sink_logits,
             w_out=w_out[0], mlp_norm_g=mlp_norm_g, w_up=w_up[0], w_down=w_down[0], ple_w=ple_w[0],
             ple_norm_g=ple_norm_g, gate_norm_g=gate_norm_g, w_gate=w_gate[0], rel_bias_table=rel_bias_table,
             final_norm_g=final_norm_g)
    mom = dict(attn_norm_g=m_attn_norm_g, w_in=m_w_in[0], q_norm_g=m_q_norm_g, k_norm_g=m_k_norm_g,
               sink_logits=m_sink_logits, w_out=m_w_out[0], mlp_norm_g=m_mlp_norm_g, w_up=m_w_up[0],
               w_down=m_w_down[0], ple_w=m_ple_w[0], ple_norm_g=m_ple_norm_g, gate_norm_g=m_gate_norm_g,
               w_gate=m_w_gate[0], rel_bias_table=m_rel_bias_table, final_norm_g=m_final_norm_g)
    var = dict(attn_norm_g=v_attn_norm_g, w_in=v_w_in[0], q_norm_g=v_q_norm_g, k_norm_g=v_k_norm_g,
               sink_logits=v_sink_logits, w_out=v_w_out[0], mlp_norm_g=v_mlp_norm_g, w_up=v_w_up[0],
               w_down=v_w_down[0], ple_w=v_ple_w[0], ple_norm_g=v_ple_norm_g, gate_norm_g=v_gate_norm_g,
               w_gate=v_w_gate[0], rel_bias_table=v_rel_bias_table, final_norm_g=v_final_norm_g)
    D = x.shape[-1]
    n_heads = D // (2 * HEAD_DIM)

    gains = {n: w[n] for n in w if n not in _SHARDED}
    gains["final_norm_g"] = final_norm_g.reshape(1, -1)

    comm = _MeshComm(w, mom, var)
    dx, parts = _step(x[0], p[0, 0], loss_target[0], gains, comm)

    small_g = _small_all_reduce(parts, deps=[comm.out[n][0] for n in comm.out])
    comm.reduce_finish("e", small_g)

    g_out, d_out, m_out, v_out = {}, {}, {}, {}
    for n in _SHARDED:
        g, d, nm, nv = comm.out[n]
        g_out[n], d_out[n], m_out[n], v_out[n] = g[None], d[None], nm[None], nv[None]

    small = {n: v for n, v in w.items() if n not in _SHARDED}
    pack = lambda vals: _pack_small({n: vals[n] for n in small}, n_heads)
    sd, sm, sv = _adam_small(pack(w), small_g, pack(mom), pack(var))
    sg, loss = _unpack_small(small_g, small, n_heads)
    g_out.update(sg)
    for dst, row in ((d_out, sd), (m_out, sm), (v_out, sv)):
        dst.update(_unpack_small(row, small, n_heads)[0])

    return (loss, dx[None], *[g_out[n] for n in _ORDER], *[d_out[n] for n in _ORDER],
            *[m_out[n] for n in _ORDER], *[v_out[n] for n in _ORDER])
```

```python
import functools
import math

import numpy as np
import jax
import jax.numpy as jnp
from jax import lax
from jax.experimental import pallas as pl
from jax.experimental.pallas import tpu as pltpu

F32 = jnp.float32
BF16 = jnp.bfloat16

N_DEV = 8
N_CHIP = 4
HEAD_DIM = 128
GROUP = 4
GRID_W = 64
WINDOW = 128
BLOCK_Q = 128
N_BUCKETS = 32
MAX_DISTANCE = 128
ROPE_THETA = 10000.0
EPS = 1e-6
NEG_INF = -1e30
ADAM_LR = 0.001
ADAM_B1 = 0.9
ADAM_B2 = 0.999
ADAM_EPS = 1e-08
ADAM_WD = 0.01
ADAM_STEP = 10
LOG2E = math.log2(math.e)
LANES = 128
SUBLANES = 8
VMEM_LIMIT_BYTES = 60 * 1024 * 1024
MESH = pl.DeviceIdType.MESH

_NT = (((1,), (1,)), ((), ()))
_NN = (((1,), (0,)), ((), ()))
_TN = (((0,), (0,)), ((), ()))


def _tile(dim, pref):
    return pref if dim % pref == 0 else dim


def _params(sem=None):
    return pltpu.CompilerParams(dimension_semantics=sem, vmem_limit_bytes=VMEM_LIMIT_BYTES)


_HBM = pl.BlockSpec(memory_space=pltpu.HBM)
_SEM = pl.BlockSpec(memory_space=pltpu.SEMAPHORE)
_ANY = pl.BlockSpec(memory_space=pl.ANY)
_VMEM = pl.BlockSpec(memory_space=pltpu.VMEM)
_EFFECT = pltpu.SideEffectType.DATAFLOW_SIDE_EFFECTING


def _pcall(body, deps=(), *, in_specs, into=None, **kw):
    deps = [d for d in deps if d is not None]
    nd = len(deps)
    if into is not None:
        deps = [into[0]] + deps
        nd += 1
        kw["input_output_aliases"] = {0: into[1]}

    def wrapped(*refs):
        body(*refs[nd:])

    call = pl.pallas_call(wrapped, in_specs=[_ANY] * nd + list(in_specs), **kw)
    return lambda *args: call(*deps, *args)


def _mm(name, a, b, dims, grid, a_spec, b_spec, out_shape, out_specs, acc_shape, epilogue,
        extras=(), extra_specs=(), deps=(), semantics=("parallel", "parallel", "arbitrary")):
    nk = grid[2]
    n_extra = len(extras)

    def body(*refs):
        a_ref, b_ref = refs[0], refs[1]
        extra = refs[2:2 + n_extra]
        outs = refs[2 + n_extra:-1]
        acc = refs[-1]
        part = lax.dot_general(a_ref[...], b_ref[...], dims, preferred_element_type=F32)
        if nk == 1:
            epilogue(part, extra, outs)
        else:
            k = pl.program_id(2)

            @pl.when(k == 0)
            def _():
                acc[...] = part

            @pl.when(k > 0)
            def _():
                acc[...] += part

            @pl.when(k == nk - 1)
            def _():
                epilogue(acc[...], extra, outs)

    return _pcall(
        body, deps, name=name, grid=grid,
        in_specs=[a_spec, b_spec, *extra_specs],
        out_specs=out_specs, out_shape=out_shape,
        scratch_shapes=[pltpu.VMEM(acc_shape if nk > 1 else (SUBLANES, LANES), F32)],
        compiler_params=_params(semantics),
    )(a, b, *extras)


def _store(dtype):
    def ep(acc, extra, outs):
        outs[0][...] = acc.astype(dtype)
    return ep


def _store_add(acc, extra, outs):
    outs[0][...] = acc + extra[0][...]


def _mm_nn(name, a, b, out_dtype=F32, epilogue=None, extras=(), n_out=1, out_dtypes=None, tm=1024, tn=1024, tk=None,
           deps=()):
    M, K = a.shape
    N = b.shape[1]
    tm, tn, tk = _tile(M, tm), _tile(N, tn), _tile(K, tk or K)
    b_spec = pl.BlockSpec((tk, tn), lambda i, j, k: (k, j))
    grid = (M // tm, N // tn, K // tk)
    o_spec = pl.BlockSpec((tm, tn), lambda i, j, k: (i, j))
    out_dtypes = out_dtypes or [out_dtype] * n_out
    out_shape = [jax.ShapeDtypeStruct((M, N), d) for d in out_dtypes]
    res = _mm(name, a, b, _NN, grid, pl.BlockSpec((tm, tk), lambda i, j, k: (i, k)), b_spec,
              out_shape, [o_spec] * len(out_dtypes), (tm, tn), epilogue or _store(out_dtype),
              extras, [o_spec] * len(extras), deps)
    return res if len(out_dtypes) > 1 else res[0]


def _mm_nt(name, a, b, out_dtype=F32, epilogue=None, extras=(), tm=1024, tn=1024, tk=None, deps=()):
    M, C = a.shape
    N = b.shape[0]
    tm, tn, tk = _tile(M, tm), _tile(N, tn), _tile(C, tk or C)
    b_spec = pl.BlockSpec((tn, tk), lambda i, j, k: (j, k))
    grid = (M // tm, N // tn, C // tk)
    o_spec = pl.BlockSpec((tm, tn), lambda i, j, k: (i, j))
    return _mm(name, a, b, _NT, grid, pl.BlockSpec((tm, tk), lambda i, j, k: (i, k)), b_spec,
               [jax.ShapeDtypeStruct((M, N), out_dtype)], [o_spec], (tm, tn), epilogue or _store(out_dtype),
               extras, [o_spec] * len(extras), deps)[0]


def _mm_tn(name, a, b, out_dtype=BF16, tm=1024, tn=512, tk=None, deps=()):
    T, M = a.shape
    N = b.shape[1]
    tm, tn, tk = _tile(M, tm), _tile(N, tn), _tile(T, tk or T)
    out_shape = jax.ShapeDtypeStruct((M, N), out_dtype)
    o_spec = pl.BlockSpec((tm, tn), lambda i, j, k: (i, j))
    grid = (M // tm, N // tn, T // tk)
    return _mm(name, a, b, _TN, grid, pl.BlockSpec((tk, tm), lambda i, j, k: (k, i)),
               pl.BlockSpec((tk, tn), lambda i, j, k: (k, j)), [out_shape], [o_spec], (tm, tn), _store(out_dtype),
               deps=deps)[0]


def _mean_last(v):
    return jnp.mean(v, axis=-1, keepdims=True)


def _rows_to_sublanes(v):
    r, c = v.shape
    return jnp.sum(v.reshape(r // SUBLANES, SUBLANES, c), axis=0)


def _accumulate(ref, val, first):
    @pl.when(first)
    def _():
        ref[...] = val

    @pl.when(jnp.logical_not(first))
    def _():
        ref[...] += val


def _rms_fwd(name, x, g, tr=512, deps=()):
    T, D = x.shape
    tr = _tile(T, tr)

    def body(x_ref, g_ref, o_ref):
        xv = x_ref[...]
        r = lax.rsqrt(_mean_last(xv * xv) + EPS)
        o_ref[...] = (xv * r * g_ref[...]).astype(BF16)

    row = pl.BlockSpec((tr, D), lambda i: (i, 0))
    return _pcall(
        body, deps, name=name, grid=(T // tr,),
        in_specs=[row, pl.BlockSpec((1, D), lambda i: (0, 0))],
        out_specs=row, out_shape=jax.ShapeDtypeStruct((T, D), BF16),
        compiler_params=_params(("parallel",)),
    )(x, g)


def _rms_bwd(name, dyn, x, g, dres, tr=512, deps=()):
    T, D = x.shape
    tr = _tile(T, tr)

    def body(dy_ref, x_ref, g_ref, dr_ref, dx_ref, dxb_ref, dg_ref):
        xv = x_ref[...]
        r = lax.rsqrt(_mean_last(xv * xv) + EPS)
        xn = xv * r
        dy = dy_ref[...].astype(F32)
        dxn = dy * g_ref[...]
        dx = dr_ref[...] + r * (dxn - xn * _mean_last(dxn * xn))
        dx_ref[...] = dx
        dxb_ref[...] = dx.astype(BF16)
        _accumulate(dg_ref, _rows_to_sublanes(dy * xn), pl.program_id(0) == 0)

    row = pl.BlockSpec((tr, D), lambda i: (i, 0))
    return _pcall(
        body, deps, name=name, grid=(T // tr,),
        in_specs=[row, row, pl.BlockSpec((1, D), lambda i: (0, 0)), row],
        out_specs=[row, row, pl.BlockSpec((SUBLANES, D), lambda i: (0, 0))],
        out_shape=[jax.ShapeDtypeStruct((T, D), F32), jax.ShapeDtypeStruct((T, D), BF16),
                   jax.ShapeDtypeStruct((SUBLANES, D), F32)],
        compiler_params=_params(("arbitrary",)),
    )(dyn, x, g, dres)


def _mm_nn_rms(name, a, b, res, g, tm=512, deps=()):
    M, K = a.shape
    N = b.shape[1]
    tm = _tile(M, tm)

    def epilogue(acc, extra, outs):
        h = acc + extra[0][...]
        outs[0][...] = h
        outs[1][...] = (h * lax.rsqrt(_mean_last(h * h) + EPS) * extra[1][...]).astype(BF16)

    row = pl.BlockSpec((tm, N), lambda i, j, k: (i, 0))
    return _mm(name, a, b, _NN, (M // tm, 1, 1), pl.BlockSpec((tm, K), lambda i, j, k: (i, 0)),
               pl.BlockSpec((K, N), lambda i, j, k: (0, 0)),
               [jax.ShapeDtypeStruct((M, N), F32), jax.ShapeDtypeStruct((M, N), BF16)], [row, row], (tm, N), epilogue,
               (res, g), [row, pl.BlockSpec((1, N), lambda i, j, k: (0, 0))], deps)


def _mm_nt_rms_bwd(name, a, b, x, g, dres, with_bf16=True, tm=256, deps=()):
    M, C = a.shape
    N = b.shape[0]
    tm = _tile(M, tm)

    def epilogue(dy, extra, outs):
        x_ref, dr_ref, g_ref = extra
        xv = x_ref[...]
        r = lax.rsqrt(_mean_last(xv * xv) + EPS)
        xn = xv * r
        dxn = dy * g_ref[...]
        dx = dr_ref[...] + r * (dxn - xn * _mean_last(dxn * xn))
        outs[0][...] = dx
        if with_bf16:
            outs[1][...] = dx.astype(BF16)
        _accumulate(outs[-1], _rows_to_sublanes(dy * xn), pl.program_id(0) == 0)

    row = pl.BlockSpec((tm, N), lambda i, j, k: (i, 0))
    copies = [jax.ShapeDtypeStruct((M, N), F32)] + ([jax.ShapeDtypeStruct((M, N), BF16)] if with_bf16 else [])
    return _mm(name, a, b, _NT, (M // tm, 1, 1), pl.BlockSpec((tm, C), lambda i, j, k: (i, 0)),
               pl.BlockSpec((N, C), lambda i, j, k: (0, 0)),
               copies + [jax.ShapeDtypeStruct((SUBLANES, N), F32)],
               [row] * len(copies) + [pl.BlockSpec((SUBLANES, N), lambda i, j, k: (0, 0))], (tm, N), epilogue,
               (x, dres, g), [row, row, pl.BlockSpec((1, N), lambda i, j, k: (0, 0))], deps,
               semantics=("arbitrary", "arbitrary", "arbitrary"))


def _gate_tail(gn, w_gate, h2, pe, target, g_ple, g_final, tm=256):
    T, D = h2.shape
    tm = _tile(T, tm)

    def epilogue(z, extra, outs):
        h2_ref, pe_ref, t_ref, gp_ref, gf_ref = extra
        dh3_ref, dz_ref, dpe_ref, dgf_ref, dgp_ref, loss_ref = outs
        first = pl.program_id(0) == 0
        pev = pe_ref[...]
        r3 = lax.rsqrt(_mean_last(pev * pev) + EPS)
        en = pev * r3
        e = en * gp_ref[...]
        gate = 1.0 / (1.0 + jnp.exp(-z))
        h3 = h2_ref[...] + gate * e
        r5 = lax.rsqrt(_mean_last(h3 * h3) + EPS)
        hn = h3 * r5
        diff = hn * gf_ref[...] - t_ref[...]
        loss_rows = 0.5 * _mean_last(diff * diff)
        row0 = lax.broadcasted_iota(jnp.int32, (SUBLANES, LANES), 0) == 0
        _accumulate(loss_ref, jnp.where(row0, jnp.sum(loss_rows), 0.0), first)
        dy = diff * (1.0 / D)
        _accumulate(dgf_ref, _rows_to_sublanes(dy * hn), first)
        dhn = dy * gf_ref[...]
        dh3 = r5 * (dhn - hn * _mean_last(dhn * hn))
        dh3_ref[...] = dh3
        dgate = dh3 * e
        de = dh3 * gate
        dz_ref[...] = (dgate * gate * (1.0 - gate)).astype(BF16)
        _accumulate(dgp_ref, _rows_to_sublanes(de * en), first)
        den = de * gp_ref[...]
        dpe_ref[...] = (r3 * (den - en * _mean_last(den * en))).astype(BF16)

    row = pl.BlockSpec((tm, D), lambda i, j, k: (i, 0))
    vec = pl.BlockSpec((1, D), lambda i, j, k: (0, 0))
    part = pl.BlockSpec((SUBLANES, D), lambda i, j, k: (0, 0))
    return _mm("gate_tail", gn, w_gate, _NN, (T // tm, 1, 1), row, pl.BlockSpec(w_gate.shape, lambda i, j, k: (0, 0)),
               [jax.ShapeDtypeStruct((T, D), F32), jax.ShapeDtypeStruct((T, D), BF16),
                jax.ShapeDtypeStruct((T, D), BF16), jax.ShapeDtypeStruct((SUBLANES, D), F32),
                jax.ShapeDtypeStruct((SUBLANES, D), F32), jax.ShapeDtypeStruct((SUBLANES, LANES), F32)],
               [row, row, row, part, part, pl.BlockSpec((SUBLANES, LANES), lambda i, j, k: (0, 0))], (tm, D), epilogue,
               (h2, pe, target, g_ple, g_final), [row, row, row, vec, vec],
               semantics=("arbitrary", "arbitrary", "arbitrary"))


def _rope_tables(T):
    pos = np.arange(T)
    half = HEAD_DIM // 2
    inv = (ROPE_THETA ** (-np.arange(0, half, 2, dtype=np.float32) / half)).astype(np.float32)
    ang_r = (pos // GRID_W).astype(np.float32)[:, None] * inv
    ang_c = (pos % GRID_W).astype(np.float32)[:, None] * inv
    cos = np.concatenate([np.cos(ang_r), np.cos(ang_r), np.cos(ang_c), np.cos(ang_c)], axis=-1)
    sin = np.concatenate([-np.sin(ang_r), np.sin(ang_r), -np.sin(ang_c), np.sin(ang_c)], axis=-1)
    return jnp.asarray(cos, F32), jnp.asarray(sin, F32)


def _swap32(x):
    lane = lax.broadcasted_iota(jnp.int32, x.shape, 1)
    return jnp.where((lane % 64) < 32, pltpu.roll(x, 96, 1), pltpu.roll(x, 32, 1))


def _in_proj(u, w_in, cos, sin, g_q, g_k, n_norm, tm=512):
    T, K = u.shape
    W = w_in.shape[1]
    tm = _tile(T, tm)
    n_q = n_norm * GROUP // (GROUP + 1)
    wa = n_norm * HEAD_DIM

    def epilogue(acc, extra, outs):
        c_ref, s_ref, gq_ref, gk_ref = extra
        raw_ref, o_ref = outs
        c, s = c_ref[...], s_ref[...]
        raw_ref[...] = acc[:, :wa]
        for h in range(n_norm):
            cols = slice(h * HEAD_DIM, (h + 1) * HEAD_DIM)
            xv = acc[:, cols]
            g = gq_ref[...] if h < n_q else gk_ref[...]
            xn = xv * lax.rsqrt(_mean_last(xv * xv) + EPS) * g
            o_ref[:, cols] = (xn * c + _swap32(xn) * s).astype(BF16)
        o_ref[:, wa:] = acc[:, wa:].astype(BF16)

    tab = pl.BlockSpec((tm, HEAD_DIM), lambda i, j, k: (i, 0))
    vec = pl.BlockSpec((1, HEAD_DIM), lambda i, j, k: (0, 0))
    return _mm("in_proj", u, w_in, _NN, (T // tm, 1, 1), pl.BlockSpec((tm, K), lambda i, j, k: (i, 0)),
               pl.BlockSpec((K, W), lambda i, j, k: (0, 0)),
               [jax.ShapeDtypeStruct((T, wa), F32), jax.ShapeDtypeStruct((T, W), BF16)],
               [pl.BlockSpec((tm, wa), lambda i, j, k: (i, 0)), pl.BlockSpec((tm, W), lambda i, j, k: (i, 0))],
               (tm, W), epilogue, (cos, sin, g_q, g_k), [tab, tab, vec, vec])


def _dproj(proj_a, dqa, dka_t, dva_t, dqb, dkb, dvb, cos, sin, g_q, g_k, tr=512):
    T, wa = proj_a.shape
    tr = _tile(T, tr)
    n_q = dqa.shape[1] // HEAD_DIM
    wkv = dka_t.shape[0]
    W = wa + wkv + dqb.shape[1] + dkb.shape[1] + dvb.shape[1]

    def body(p_ref, dqa_ref, dkat_ref, dvat_ref, dqb_ref, dkb_ref, dvb_ref, c_ref, s_ref, gq_ref, gk_ref,
             o_ref, dgq_ref, dgk_ref):
        c, s = c_ref[...], s_ref[...]
        dka = dkat_ref[...].T
        dgq = jnp.zeros((SUBLANES, HEAD_DIM), F32)
        dgk = jnp.zeros((SUBLANES, HEAD_DIM), F32)
        for h in range(wa // HEAD_DIM):
            cols = slice(h * HEAD_DIM, (h + 1) * HEAD_DIM)
            xv = p_ref[:, cols]
            r = lax.rsqrt(_mean_last(xv * xv) + EPS)
            xn = xv * r
            if h < n_q:
                d = dqa_ref[:, cols]
                g = gq_ref[...]
            else:
                d = dka[:, (h - n_q) * HEAD_DIM:(h - n_q + 1) * HEAD_DIM]
                g = gk_ref[...]
            dqn = d * c + _swap32(d * s)
            part = _rows_to_sublanes(dqn * xn)
            if h < n_q:
                dgq = dgq + part
            else:
                dgk = dgk + part
            dxn = dqn * g
            o_ref[:, cols] = (r * (dxn - xn * _mean_last(dxn * xn))).astype(BF16)
        o_ref[:, wa:wa + wkv] = dvat_ref[...].T.astype(BF16)
        off = wa + wkv
        for ref in (dqb_ref, dkb_ref, dvb_ref):
            w = ref.shape[1]
            o_ref[:, off:off + w] = ref[...].astype(BF16)
            off += w
        first = pl.program_id(0) == 0
        _accumulate(dgq_ref, dgq, first)
        _accumulate(dgk_ref, dgk, first)

    def row(w):
        return pl.BlockSpec((tr, w), lambda i: (i, 0))

    col = pl.BlockSpec((wkv, tr), lambda i: (0, i))
    vec = pl.BlockSpec((1, HEAD_DIM), lambda i: (0, 0))
    part = pl.BlockSpec((SUBLANES, HEAD_DIM), lambda i: (0, 0))
    return pl.pallas_call(
        body, name="dproj", grid=(T // tr,),
        in_specs=[row(wa), row(dqa.shape[1]), col, col, row(dqb.shape[1]),
                  row(dkb.shape[1]), row(dvb.shape[1]), row(HEAD_DIM), row(HEAD_DIM), vec, vec],
        out_specs=[row(W), part, part],
        out_shape=[jax.ShapeDtypeStruct((T, W), BF16), jax.ShapeDtypeStruct((SUBLANES, HEAD_DIM), F32),
                   jax.ShapeDtypeStruct((SUBLANES, HEAD_DIM), F32)],
        compiler_params=_params(("arbitrary",)),
    )(proj_a, dqa, dka_t, dva_t, dqb, dkb, dvb, cos, sin, g_q, g_k)


def _attn_a_fwd(pb, n_q, n_kv, out_heads, tq=1024, tc=2048, halves=2):
    T = pb.shape[0]
    tq, tc = _tile(T, tq), _tile(T, tc)
    th = tq // halves
    scale = HEAD_DIM ** -0.5
    c = scale * LOG2E

    def body(q_ref, k_ref, v_ref, o_ref, lse_ref):
        qs = [q_ref[h * th:(h + 1) * th, :] for h in range(halves)]
        m, l, acc = [None] * halves, [None] * halves, [None] * halves
        for j in range(T // tc):
            keys = slice(j * tc, (j + 1) * tc)
            kc, vc = k_ref[keys, :], v_ref[keys, :]
            for h in range(halves):
                s = lax.dot_general(qs[h], kc, _NT, preferred_element_type=F32)
                mj = jnp.max(s, axis=-1, keepdims=True)
                m_new = mj if j == 0 else jnp.maximum(m[h], mj)
                p = jnp.exp2((s - m_new) * c)
                pv = lax.dot_general(p.astype(BF16), vc, _NN, preferred_element_type=F32)
                if j == 0:
                    l[h], acc[h] = jnp.sum(p, axis=-1, keepdims=True), pv
                else:
                    alpha = jnp.exp2((m[h] - m_new) * c)
                    l[h] = alpha * l[h] + jnp.sum(p, axis=-1, keepdims=True)
                    acc[h] = alpha * acc[h] + pv
                m[h] = m_new
        for h in range(halves):
            rows = slice(h * th, (h + 1) * th)
            o_ref[rows, :] = (acc[h] / l[h]).astype(BF16)
            lse_ref[rows, :] = m[h] * scale + jnp.log(l[h])

    return pl.pallas_call(
        body, name="attn_a_fwd", grid=(n_kv, GROUP, T // tq),
        in_specs=[pl.BlockSpec((tq, HEAD_DIM), lambda kv, g, i: (i, kv * GROUP + g)),
                  pl.BlockSpec((T, HEAD_DIM), lambda kv, g, i: (0, n_q + kv)),
                  pl.BlockSpec((T, HEAD_DIM), lambda kv, g, i: (0, n_q + n_kv + kv))],
        out_specs=[pl.BlockSpec((tq, HEAD_DIM), lambda kv, g, i: (i, kv * GROUP + g)),
                   pl.BlockSpec((None, tq, 1), lambda kv, g, i: (kv * GROUP + g, i, 0))],
        out_shape=[jax.ShapeDtypeStruct((T, out_heads * HEAD_DIM), BF16), jax.ShapeDtypeStruct((n_q, T, 1), F32)],
        compiler_params=_params(("parallel", "parallel", "parallel")),
    )(pb, pb, pb)


def _attn_a_bwd(pb, o_cat, d_o, lse, n_q, n_kv, tq=2048, tc=512, halves=4):
    T = pb.shape[0]
    tq, tc = _tile(T, tq), _tile(T, tc)
    th = tq // halves
    scale = HEAD_DIM ** -0.5
    c = scale * LOG2E

    def body(q_ref, k_ref, v_ref, o_ref, do_ref, lse_ref, dq_ref, dkt_ref, dvt_ref):
        @pl.when(jnp.logical_and(pl.program_id(1) == 0, pl.program_id(2) == 0))
        def _():
            dkt_ref[...] = jnp.zeros(dkt_ref.shape, F32)
            dvt_ref[...] = jnp.zeros(dvt_ref.shape, F32)

        groups = []
        for h in range(halves):
            rows = slice(h * th, (h + 1) * th)
            q, do = q_ref[rows, :], do_ref[rows, :]
            delta = jnp.sum(do.astype(F32) * o_ref[rows, :].astype(F32), axis=-1, keepdims=True)
            groups.append((q, do, q.T, do.T, delta, lse_ref[rows, :] * LOG2E))
        dq = [None] * halves
        for j in range(T // tc):
            keys = slice(j * tc, (j + 1) * tc)
            kc, vc = k_ref[keys, :], v_ref[keys, :]
            for h, (q, do, qt, dot, delta, lse2) in enumerate(groups):
                s = lax.dot_general(q, kc, _NT, preferred_element_type=F32)
                p = jnp.exp2(s * c - lse2)
                dp = lax.dot_general(do, vc, _NT, preferred_element_type=F32)
                ds = (p * (dp - delta) * scale).astype(BF16)
                dqj = lax.dot_general(ds, kc, _NN, preferred_element_type=F32)
                dq[h] = dqj if dq[h] is None else dq[h] + dqj
                dvt_ref[:, keys] += lax.dot_general(dot, p.astype(BF16), _NN, preferred_element_type=F32)
                dkt_ref[:, keys] += lax.dot_general(qt, ds, _NN, preferred_element_type=F32)
        for h in range(halves):
            dq_ref[h * th:(h + 1) * th, :] = dq[h]

    qmap = lambda kv, g, i: (i, kv * GROUP + g)
    return pl.pallas_call(
        body, name="attn_a_bwd", grid=(n_kv, GROUP, T // tq),
        in_specs=[pl.BlockSpec((tq, HEAD_DIM), qmap),
                  pl.BlockSpec((T, HEAD_DIM), lambda kv, g, i: (0, n_q + kv)),
                  pl.BlockSpec((T, HEAD_DIM), lambda kv, g, i: (0, n_q + n_kv + kv)),
                  pl.BlockSpec((tq, HEAD_DIM), qmap),
                  pl.BlockSpec((tq, HEAD_DIM), qmap),
                  pl.BlockSpec((None, tq, 1), lambda kv, g, i: (kv * GROUP + g, i, 0))],
        out_specs=[pl.BlockSpec((tq, HEAD_DIM), qmap),
                   pl.BlockSpec((HEAD_DIM, T), lambda kv, g, i: (kv, 0)),
                   pl.BlockSpec((HEAD_DIM, T), lambda kv, g, i: (kv, 0))],
        out_shape=[jax.ShapeDtypeStruct((T, n_q * HEAD_DIM), F32),
                   jax.ShapeDtypeStruct((n_kv * HEAD_DIM, T), F32),
                   jax.ShapeDtypeStruct((n_kv * HEAD_DIM, T), F32)],
        compiler_params=_params(("parallel", "arbitrary", "arbitrary")),
    )(pb, pb, pb, o_cat, d_o, lse)


def _bucket_index():
    r = np.arange(BLOCK_Q)[:, None]
    j = np.arange(3 * BLOCK_Q)[None, :]
    rel = (j - BLOCK_Q) - r
    nb = N_BUCKETS // 2
    ret = np.where(rel > 0, nb, 0)
    n = np.abs(rel)
    max_exact = nb // 2
    nf = np.maximum(n, 1).astype(np.float32)
    large = max_exact + (np.log(nf / max_exact) / math.log(MAX_DISTANCE / max_exact) * (nb - max_exact)).astype(np.int32)
    large = np.minimum(large, nb - 1)
    return jnp.asarray(ret + np.where(n < max_exact, n, large), jnp.int32)


def _bias_build(idx, table_flat, n_heads, deps=()):
    def body(idx_ref, tab_ref, o_ref):
        h = pl.program_id(0)
        iv = idx_ref[...]
        acc = jnp.zeros(iv.shape, F32)
        for b in range(N_BUCKETS):
            acc = jnp.where(iv == b, tab_ref[b * n_heads + h], acc)
        r = lax.broadcasted_iota(jnp.int32, iv.shape, 0)
        j = lax.broadcasted_iota(jnp.int32, iv.shape, 1)
        o_ref[...] = jnp.where(jnp.abs(j - BLOCK_Q - r) <= WINDOW, acc, NEG_INF)

    return _pcall(
        body, deps, name="bias_build", grid=(n_heads,),
        in_specs=[pl.BlockSpec(idx.shape, lambda h: (0, 0)), pl.BlockSpec(memory_space=pltpu.SMEM)],
        out_specs=pl.BlockSpec((None,) + idx.shape, lambda h: (h, 0, 0)),
        out_shape=jax.ShapeDtypeStruct((n_heads,) + idx.shape, F32),
        compiler_params=_params(("parallel",)),
    )(idx, table_flat)


def _in_sequence(n, T):
    j = lax.broadcasted_iota(jnp.int32, (GROUP * BLOCK_Q, 3 * BLOCK_Q), 1)
    kabs = n * BLOCK_Q + j - BLOCK_Q
    return (kabs >= 0) & (kabs < T)


def _per_head_rows(values):
    head = lax.broadcasted_iota(jnp.int32, (GROUP * BLOCK_Q, 1), 0) // BLOCK_Q
    col = jnp.zeros((GROUP * BLOCK_Q, 1), F32)
    for g, v in enumerate(values):
        col = jnp.where(head == g, v, col)
    return col


def _band_specs(col, nblk, sb):
    return [pl.BlockSpec((BLOCK_Q, HEAD_DIM), lambda kv, i: (jnp.maximum(sb * i - 1, 0), col(kv))),
            pl.BlockSpec((sb * BLOCK_Q, HEAD_DIM), lambda kv, i: (i, col(kv))),
            pl.BlockSpec((BLOCK_Q, HEAD_DIM), lambda kv, i: (jnp.minimum(sb * i + sb, nblk - 1), col(kv)))]


def _head_specs(base, rows):
    return [pl.BlockSpec((rows, HEAD_DIM), functools.partial(lambda kv, i, g: (i, base + kv * GROUP + g), g=g))
            for g in range(GROUP)]


def _attn_b_fwd(pb, bias, sink, o_all, q_off, n_q, n_kv, deps=(), sb=16):
    T = pb.shape[0]
    nblk = T // BLOCK_Q
    sb = min(sb, nblk)
    tq = sb * BLOCK_Q
    scale = HEAD_DIM ** -0.5

    def body(*refs):
        q_refs = refs[0:GROUP]
        k_refs, v_refs = refs[GROUP:GROUP + 3], refs[GROUP + 3:GROUP + 6]
        bias_ref, sink_ref, o_ref, lse_ref = refs[GROUP + 6:]
        kv, i = pl.program_id(0), pl.program_id(1)
        kb = jnp.concatenate([r[...] for r in k_refs], axis=0)
        vb = jnp.concatenate([r[...] for r in v_refs], axis=0)
        bias_all = bias_ref[...].reshape(GROUP * BLOCK_Q, 3 * BLOCK_Q)
        sk = _per_head_rows([sink_ref[kv * GROUP + g] for g in range(GROUP)])
        for b in range(sb):
            rows = slice(b * BLOCK_Q, (b + 1) * BLOCK_Q)
            kw, vw = kb[b * BLOCK_Q:(b + 3) * BLOCK_Q], vb[b * BLOCK_Q:(b + 3) * BLOCK_Q]
            q = jnp.concatenate([r[rows, :] for r in q_refs], axis=0)
            s = lax.dot_general(q, kw, _NT, preferred_element_type=F32) * scale + bias_all
            if b == 0 or b == sb - 1:
                s = jnp.where(_in_sequence(i * sb + b, T), s, NEG_INF)
            m = jnp.maximum(jnp.max(s, axis=-1, keepdims=True), sk)
            p = jnp.exp(s - m)
            l = jnp.sum(p, axis=-1, keepdims=True) + jnp.exp(sk - m)
            o = (lax.dot_general(p.astype(BF16), vw, _NN, preferred_element_type=F32) / l).astype(BF16)
            lse = m + jnp.log(l)
            for g in range(GROUP):
                head = slice(g * BLOCK_Q, (g + 1) * BLOCK_Q)
                o_ref[rows, g * HEAD_DIM:(g + 1) * HEAD_DIM] = o[head]
                lse_ref[g, rows, :] = lse[head]

    first_group = o_all.shape[1] // (GROUP * HEAD_DIM) - n_kv
    return _pcall(
        body, deps, into=(o_all, 0), name="attn_b_fwd", grid=(n_kv, nblk // sb),
        in_specs=[*_head_specs(q_off, tq),
                  *_band_specs(lambda kv: q_off + n_q + kv, nblk, sb),
                  *_band_specs(lambda kv: q_off + n_q + n_kv + kv, nblk, sb),
                  pl.BlockSpec((GROUP, BLOCK_Q, 3 * BLOCK_Q), lambda kv, i: (kv, 0, 0)),
                  pl.BlockSpec(memory_space=pltpu.SMEM)],
        out_specs=[pl.BlockSpec((tq, GROUP * HEAD_DIM), lambda kv, i: (i, first_group + kv)),
                   pl.BlockSpec((GROUP, tq, 1), lambda kv, i: (kv, i, 0))],
        out_shape=[jax.ShapeDtypeStruct(o_all.shape, BF16), jax.ShapeDtypeStruct((n_q, T, 1), F32)],
        compiler_params=_params(("parallel", "parallel")),
    )(*([pb] * (GROUP + 6)), bias, sink)


def _attn_b_bwd(pb, o_cat, d_o, lse, bias, sink, q_off, n_q, n_kv, o_off, deps=(), sb=16):
    T = pb.shape[0]
    nblk = T // BLOCK_Q
    sb = min(sb, nblk)
    tq = sb * BLOCK_Q
    scale = HEAD_DIM ** -0.5

    def body(*refs):
        q_refs = refs[0:GROUP]
        k_refs, v_refs = refs[GROUP:GROUP + 3], refs[GROUP + 3:GROUP + 6]
        o_refs, do_refs = refs[GROUP + 6:2 * GROUP + 6], refs[2 * GROUP + 6:3 * GROUP + 6]
        lse_ref, bias_ref, sink_ref, dq_ref, dk_ref, dv_ref, dbias_ref, dsink_ref, dkb_ref, dvb_ref = refs[3 * GROUP + 6:]
        kv, i = pl.program_id(0), pl.program_id(1)
        first = i == 0

        @pl.when(first)
        def _():
            dk_ref[...] = jnp.zeros(dk_ref.shape, F32)
            dv_ref[...] = jnp.zeros(dv_ref.shape, F32)
            dbias_ref[...] = jnp.zeros(dbias_ref.shape, F32)

        kb = jnp.concatenate([r[...] for r in k_refs], axis=0)
        vb = jnp.concatenate([r[...] for r in v_refs], axis=0)
        dkb_ref[...] = jnp.zeros(dkb_ref.shape, F32)
        dvb_ref[...] = jnp.zeros(dvb_ref.shape, F32)
        row = lax.broadcasted_iota(jnp.int32, (SUBLANES, LANES), 0)
        dsink = jnp.zeros((SUBLANES, LANES), F32)
        bias_all = bias_ref[...].reshape(GROUP * BLOCK_Q, 3 * BLOCK_Q)
        sk = _per_head_rows([sink_ref[kv * GROUP + g] for g in range(GROUP)])
        for b in range(sb):
            rows = slice(b * BLOCK_Q, (b + 1) * BLOCK_Q)
            win = slice(b * BLOCK_Q, (b + 3) * BLOCK_Q)
            kw, vw = kb[win], vb[win]
            q = jnp.concatenate([r[rows, :] for r in q_refs], axis=0)
            do = jnp.concatenate([r[rows, :] for r in do_refs], axis=0)
            o = jnp.concatenate([r[rows, :] for r in o_refs], axis=0)
            lse = jnp.concatenate([lse_ref[g, rows, :] for g in range(GROUP)], axis=0)
            delta = jnp.sum(do.astype(F32) * o.astype(F32), axis=-1, keepdims=True)
            s = lax.dot_general(q, kw, _NT, preferred_element_type=F32) * scale + bias_all
            if b == 0 or b == sb - 1:
                s = jnp.where(_in_sequence(i * sb + b, T), s, NEG_INF)
            p = jnp.exp(s - lse)
            dp = lax.dot_general(do, vw, _NT, preferred_element_type=F32)
            ds = p * (dp - delta)
            dbias_ref[...] += ds.reshape(GROUP, BLOCK_Q, 3 * BLOCK_Q)
            sunk = jnp.exp(sk - lse) * delta
            for g in range(GROUP):
                dsink = dsink + jnp.where(row == g, -jnp.sum(sunk[g * BLOCK_Q:(g + 1) * BLOCK_Q]), 0.0)
            dsb = (ds * scale).astype(BF16)
            dq = lax.dot_general(dsb, kw, _NN, preferred_element_type=F32).astype(BF16)
            for g in range(GROUP):
                dq_ref[rows, g * HEAD_DIM:(g + 1) * HEAD_DIM] = dq[g * BLOCK_Q:(g + 1) * BLOCK_Q]
            dkb_ref[win, :] += lax.dot_general(dsb, q, _TN, preferred_element_type=F32)
            dvb_ref[win, :] += lax.dot_general(p.astype(BF16), do, _TN, preferred_element_type=F32)
        _accumulate(dsink_ref, dsink, first)

        before = pl.ds(pl.multiple_of(jnp.maximum(sb * i - 1, 0) * BLOCK_Q, BLOCK_Q), BLOCK_Q)
        own = pl.ds(pl.multiple_of(i * tq, BLOCK_Q), tq)
        after = pl.ds(pl.multiple_of(jnp.minimum(sb * i + sb, nblk - 1) * BLOCK_Q, BLOCK_Q), BLOCK_Q)
        for acc_ref, band_ref in ((dk_ref, dkb_ref), (dv_ref, dvb_ref)):
            acc_ref[before, :] += band_ref[0:BLOCK_Q, :]
            acc_ref[own, :] += band_ref[BLOCK_Q:BLOCK_Q + tq, :]
            acc_ref[after, :] += band_ref[BLOCK_Q + tq:, :]

    return _pcall(
        body, deps, name="attn_b_bwd", grid=(n_kv, nblk // sb),
        in_specs=[*_head_specs(q_off, tq),
                  *_band_specs(lambda kv: q_off + n_q + kv, nblk, sb),
                  *_band_specs(lambda kv: q_off + n_q + n_kv + kv, nblk, sb),
                  *_head_specs(o_off, tq), *_head_specs(o_off, tq),
                  pl.BlockSpec((GROUP, tq, 1), lambda kv, i: (kv, i, 0)),
                  pl.BlockSpec((GROUP, BLOCK_Q, 3 * BLOCK_Q), lambda kv, i: (kv, 0, 0)),
                  pl.BlockSpec(memory_space=pltpu.SMEM)],
        out_specs=[pl.BlockSpec((tq, GROUP * HEAD_DIM), lambda kv, i: (i, kv)),
                   pl.BlockSpec((T, HEAD_DIM), lambda kv, i: (0, kv)),
                   pl.BlockSpec((T, HEAD_DIM), lambda kv, i: (0, kv)),
                   pl.BlockSpec((GROUP, BLOCK_Q, 3 * BLOCK_Q), lambda kv, i: (kv, 0, 0)),
                   pl.BlockSpec((None, SUBLANES, LANES), lambda kv, i: (kv, 0, 0))],
        out_shape=[jax.ShapeDtypeStruct((T, n_q * HEAD_DIM), BF16),
                   jax.ShapeDtypeStruct((T, n_kv * HEAD_DIM), F32),
                   jax.ShapeDtypeStruct((T, n_kv * HEAD_DIM), F32),
                   jax.ShapeDtypeStruct((n_q, BLOCK_Q, 3 * BLOCK_Q), F32),
                   jax.ShapeDtypeStruct((n_kv, SUBLANES, LANES), F32)],
        scratch_shapes=[pltpu.VMEM((tq + 2 * BLOCK_Q, HEAD_DIM), F32), pltpu.VMEM((tq + 2 * BLOCK_Q, HEAD_DIM), F32)],
        compiler_params=_params(("parallel", "arbitrary")),
    )(*([pb] * (GROUP + 6)), *([o_cat] * GROUP), *([d_o] * GROUP), lse, bias, sink)


def _table_grads(dbias, dsink_raw, idx):
    n_heads = dbias.shape[0]
    n_kv = dsink_raw.shape[0]

    def body(db_ref, ds_ref, idx_ref, dt_ref, dsk_ref):
        iv = idx_ref[...]
        row = lax.broadcasted_iota(jnp.int32, (SUBLANES, LANES), 0)
        lane = lax.broadcasted_iota(jnp.int32, (SUBLANES, LANES), 1)
        dsk = jnp.zeros((SUBLANES, LANES), F32)
        for h in range(n_heads):
            d = db_ref[h]
            acc = jnp.zeros((SUBLANES, LANES), F32)
            for b in range(N_BUCKETS):
                acc = jnp.where((row == 0) & (lane == b), jnp.sum(jnp.where(iv == b, d, 0.0)), acc)
            dt_ref[:, h * LANES:(h + 1) * LANES] = acc
            raw = ds_ref[h // GROUP]
            val = jnp.sum(jnp.where((row == h % GROUP) & (lane == 0), raw, 0.0))
            dsk = jnp.where((row == 0) & (lane == h), val, dsk)
        dsk_ref[...] = dsk

    return pl.pallas_call(
        body, name="table_grads",
        in_specs=[pl.BlockSpec(memory_space=pltpu.VMEM)] * 3,
        out_specs=[pl.BlockSpec(memory_space=pltpu.VMEM)] * 2,
        out_shape=[jax.ShapeDtypeStruct((SUBLANES, n_heads * LANES), F32),
                   jax.ShapeDtypeStruct((SUBLANES, LANES), F32)],
        compiler_params=_params(),
    )(dbias, dsink_raw, idx)


def _position():
    x, y, c = lax.axis_index("x"), lax.axis_index("y"), lax.axis_index("c")
    return x, y, c


def _hbm(a):
    return pltpu.with_memory_space_constraint(a, pltpu.HBM)


def _split_start(name, bufs, sem_shapes, issue):
    nb, ns = len(bufs), len(sem_shapes)

    def body(*refs):
        buf_refs = refs[:nb]
        sems = refs[nb:nb + ns]
        token = refs[nb + ns + nb]
        issue(buf_refs, sems)
        token[...] = jnp.zeros(token.shape, F32)

    outs = pl.pallas_call(
        body, name=name,
        in_specs=[_HBM] * nb,
        out_specs=[_SEM] * ns + [_HBM] * nb + [_VMEM],
        out_shape=[pltpu.SemaphoreType.DMA(s) for s in sem_shapes] + [pltpu.HBM(b.shape, b.dtype) for b in bufs]
        + [jax.ShapeDtypeStruct((SUBLANES, LANES), F32)],
        input_output_aliases={i: ns + i for i in range(nb)},
        compiler_params=pltpu.CompilerParams(has_side_effects=_EFFECT),
    )(*[_hbm(b) for b in bufs])
    return outs[:ns], outs[ns:ns + nb], outs[-1]


def _split_wait(name, bufs, send, recv, counts, size_of, after):
    nb = len(bufs)

    def body(*refs):
        buf_refs = refs[:nb]
        send_ref, recv_ref = refs[nb], refs[nb + 1]
        x, y, c = _position()
        for w, n in enumerate(counts):
            ref = size_of(buf_refs, w)
            for k in range(n):
                s = sum(counts[:w]) + k
                cp = pltpu.make_async_remote_copy(
                    src_ref=ref, dst_ref=ref, send_sem=send_ref.at[s], recv_sem=recv_ref.at[s],
                    device_id=(x, y, c), device_id_type=MESH)
                cp.wait_send()
                cp.wait_recv()

    return pl.pallas_call(
        body, name=name,
        in_specs=[_HBM] * nb + [_SEM, _SEM, _ANY],
        out_specs=[_HBM] * nb,
        out_shape=[pltpu.HBM(b.shape, b.dtype) for b in bufs],
        input_output_aliases={i: i for i in range(nb)},
        compiler_params=pltpu.CompilerParams(has_side_effects=_EFFECT),
    )(*bufs, send, recv, after)


def _block_of(pos):
    return 4 * pos[0] + 2 * pos[1] + pos[2]


def _shard_of(ref, blk, by_cols):
    aligned = (lambda v, a: v) if isinstance(blk, int) else pl.multiple_of
    if by_cols:
        n = ref.shape[1] // N_DEV
        return ref.at[:, pl.ds(aligned(blk * n, LANES), n)]
    r = ref.shape[0] // N_DEV
    return ref.at[pl.ds(aligned(blk * r, SUBLANES), r), :]


def _place_own(name, land, shard, by_cols, tr=256):
    r, n = shard.shape
    tr = _tile(r, tr)
    mine = _block_of(_position()).astype(jnp.int32).reshape(1)

    def body(m_ref, land_ref, s_ref, o_ref):
        o_ref[...] = s_ref[...]

    if by_cols:
        out = pl.BlockSpec((tr, n), lambda i, m_ref: (i, m_ref[0]))
    else:
        out = pl.BlockSpec((tr, n), lambda i, m_ref: (m_ref[0] * (r // tr) + i, 0))
    return pl.pallas_call(
        body, name=name,
        grid_spec=pltpu.PrefetchScalarGridSpec(
            num_scalar_prefetch=1, grid=(r // tr,),
            in_specs=[_ANY, pl.BlockSpec((tr, n), lambda i, m_ref: (i, 0))], out_specs=out),
        out_shape=jax.ShapeDtypeStruct(land.shape, land.dtype),
        input_output_aliases={1: 0},
        compiler_params=_params(("parallel",)),
    )(mine, land, shard)


def _gather_start(name, shards, by_cols, groups, after=None):
    nw = len(shards)
    lands = [lax.empty((s.shape[0], s.shape[1] * N_DEV) if cols else (s.shape[0] * N_DEV, s.shape[1]), s.dtype)
             for s, cols in zip(shards, by_cols)]
    order = [] if after is None else [after]

    def issue(bufs, sems):
        x, y, c = _position()
        peers = [(x, y, 1 - c), (1 - x, y, c), (x, 1 - y, c), (1 - x, 1 - y, c)]
        for gi, grp in enumerate(groups):
            for wi, w in enumerate(grp):
                for k, peer in enumerate(peers):
                    pltpu.make_async_remote_copy(
                        src_ref=bufs[w], dst_ref=_shard_of(bufs[nw + w], _block_of((x, y, c)), by_cols[w]),
                        send_sem=sems[2 * gi].at[4 * wi + k], recv_sem=sems[2 * gi + 1].at[4 * wi + k],
                        device_id=peer, device_id_type=MESH).start()

    sem_shapes = [(4 * len(g),) for g in groups for _ in range(2)]
    sems, thru, token = _split_start(name, list(shards) + lands + order, sem_shapes, issue)
    return sems, thru[:nw], thru[nw:2 * nw], token


def _gather_forward(name, lands, by_cols):
    nw = len(lands)

    def issue(land, sems):
        x, y, c = _position()
        for w in range(nw):
            for k, chip in enumerate([(1 - x, y), (x, 1 - y), (1 - x, 1 - y)]):
                blk = _shard_of(land[w], _block_of((*chip, c)), by_cols[w])
                pltpu.make_async_remote_copy(
                    src_ref=blk, dst_ref=blk, send_sem=sems[0].at[3 * w + k], recv_sem=sems[1].at[3 * w + k],
                    device_id=(x, y, 1 - c), device_id_type=MESH).start()

    return _split_start(name, lands, [(3 * nw,), (3 * nw,)], issue)


def _first_block(bufs, w, offset=0):
    return bufs[offset + w].at[0]


_PEER_FLIPS = ((0, 0, 1), (1, 0, 0), (1, 0, 1), (0, 1, 0), (0, 1, 1), (1, 1, 0), (1, 1, 1))


def _scatter_start(name, grads, by_cols):
    nw = len(grads)
    lands = []
    for g, cols in zip(grads, by_cols):
        shard = (g.shape[0], g.shape[1] // N_DEV) if cols else (g.shape[0] // N_DEV, g.shape[1])
        lands.append(lax.empty((N_DEV,) + shard, g.dtype))

    def issue(bufs, sems):
        x, y, c = _position()
        flip = lambda v, f: 1 - v if f else v
        for w in range(nw):
            for k, (fx, fy, fc) in enumerate(_PEER_FLIPS):
                peer = (flip(x, fx), flip(y, fy), flip(c, fc))
                pltpu.make_async_remote_copy(
                    src_ref=_shard_of(bufs[w], _block_of(peer), by_cols[w]), dst_ref=bufs[nw + w].at[_block_of((x, y, c))],
                    send_sem=sems[0].at[7 * w + k], recv_sem=sems[1].at[7 * w + k],
                    device_id=peer, device_id_type=MESH).start()

    return _split_start(name, list(grads) + lands, [(7 * nw,), (7 * nw,)], issue)


def _adam(w, g, m, v):
    m = ADAM_B1 * m + (1.0 - ADAM_B1) * g
    v = ADAM_B2 * v + (1.0 - ADAM_B2) * (g * g)
    m_hat = m / (1.0 - ADAM_B1 ** ADAM_STEP)
    v_hat = v / (1.0 - ADAM_B2 ** ADAM_STEP)
    delta = -ADAM_LR * (m_hat / (jnp.sqrt(v_hat) + ADAM_EPS) + ADAM_WD * w)
    return delta, m, v


def _sum_adam(name, landed, grad, by_cols, w, m, v, tr=256):
    R, C = w.shape
    tr = _tile(R, tr if C > 1024 else 2 * tr)
    mine = _block_of(_position()).astype(jnp.int32).reshape(1)

    def body(me_ref, l_ref, own_ref, w_ref, m_ref, v_ref, g_ref, d_ref, nm_ref, nv_ref):
        own = own_ref[...].astype(F32)
        g = None
        for d in range(N_DEV):
            part = jnp.where(me_ref[0] == d, own, l_ref[d].astype(F32))
            g = part if g is None else g + part
        g_ref[...] = g
        d_ref[...], nm_ref[...], nv_ref[...] = _adam(w_ref[...], g, m_ref[...], v_ref[...])

    tile = pl.BlockSpec((tr, C), lambda i, me_ref: (i, 0))
    if by_cols:
        own = pl.BlockSpec((tr, C), lambda i, me_ref: (i, me_ref[0]))
    else:
        own = pl.BlockSpec((tr, C), lambda i, me_ref: (me_ref[0] * (R // tr) + i, 0))
    return pl.pallas_call(
        body, name=name,
        grid_spec=pltpu.PrefetchScalarGridSpec(
            num_scalar_prefetch=1, grid=(R // tr,),
            in_specs=[pl.BlockSpec((N_DEV, tr, C), lambda i, me_ref: (0, i, 0)), own, tile, tile, tile],
            out_specs=[tile] * 4),
        out_shape=[jax.ShapeDtypeStruct((R, C), F32)] * 4,
        compiler_params=_params(("parallel",)),
    )(mine, landed, grad, w, m, v)


def _small_all_reduce(parts, deps=()):
    W = parts.shape[1]

    def body(p_ref, o_ref, slots, send_sems, recv_sems):
        x, y, c = _position()
        me = 4 * x + 2 * y + c
        slots[me] = jnp.sum(p_ref[...], axis=0, keepdims=True)
        peers = [(x, y, 1 - c), (1 - x, y, c), (1 - x, y, 1 - c), (x, 1 - y, c), (x, 1 - y, 1 - c),
                 (1 - x, 1 - y, c), (1 - x, 1 - y, 1 - c)]
        copies = []
        for k, peer in enumerate(peers):
            cp = pltpu.make_async_remote_copy(
                src_ref=slots.at[me], dst_ref=slots.at[me], send_sem=send_sems.at[k], recv_sem=recv_sems.at[k],
                device_id=peer, device_id_type=MESH)
            cp.start()
            copies.append(cp)
        for cp in copies:
            cp.wait()
        total = slots[0]
        for d in range(1, N_DEV):
            total = total + slots[d]
        o_ref[...] = total

    return _pcall(
        body, deps, name="small_all_reduce",
        in_specs=[pl.BlockSpec(memory_space=pltpu.VMEM)], out_specs=pl.BlockSpec(memory_space=pltpu.VMEM),
        out_shape=jax.ShapeDtypeStruct((1, W), F32),
        scratch_shapes=[pltpu.VMEM((N_DEV, 1, W), F32), pltpu.SemaphoreType.DMA((7,)), pltpu.SemaphoreType.DMA((7,))],
    )(parts)


def _adam_small(w, g, m, v):
    def body(w_ref, g_ref, m_ref, v_ref, d_ref, nm_ref, nv_ref):
        d_ref[...], nm_ref[...], nv_ref[...] = _adam(w_ref[...], g_ref[...], m_ref[...], v_ref[...])

    return pl.pallas_call(
        body, name="adam_small",
        in_specs=[pl.BlockSpec(memory_space=pltpu.VMEM)] * 4, out_specs=[pl.BlockSpec(memory_space=pltpu.VMEM)] * 3,
        out_shape=[jax.ShapeDtypeStruct(w.shape, F32)] * 3,
    )(w, g, m, v)


_GATHER_GROUPS = (("w_in",), ("w_out", "w_up", "ple_w"), ("w_down", "w_gate"))
_COL_SHARDED = ("w_in", "w_up", "ple_w")


class _MeshComm:
    def __init__(self, w, mom, var):
        self.w, self.mom, self.var = w, mom, var
        self.out = {}
        self._scatters = {}

    def gather_begin(self):
        self._groups = {}
        token = None
        for tag, first, group_list in (("gather_start0", 0, _GATHER_GROUPS[:1]), ("gather_start1", 1, _GATHER_GROUPS[1:])):
            names = [n for g in group_list for n in g]
            idx = {n: i for i, n in enumerate(names)}
            by_cols = [n in _COL_SHARDED for n in names]
            sems, src, lands, token = _gather_start(tag, [self.w[n].astype(BF16) for n in names], by_cols,
                                                    [[idx[n] for n in g] for g in group_list], token)
            lands = [_place_own("place_" + n, land, s, cols) for n, land, s, cols in zip(names, lands, src, by_cols)]
            for k, g in enumerate(group_list):
                self._groups[first + k] = (sems[2 * k], sems[2 * k + 1], [src[idx[n]] for n in g],
                                           [lands[idx[n]] for n in g])
        return token

    @staticmethod
    def _shard_size(names, offset):
        return lambda bufs, w: _shard_of(bufs[offset + w], 0, names[w] in _COL_SHARDED)

    def gather_arrive(self, gi, after):
        names = _GATHER_GROUPS[gi]
        send, recv, src, lands = self._groups[gi]
        out = _split_wait("gather_arrive%d" % gi, src + lands, send, recv, [4] * len(names),
                          self._shard_size(names, len(names)), after)
        self._arrived = out[len(names):]

    def gather_forward(self, gi):
        by_cols = [n in _COL_SHARDED for n in _GATHER_GROUPS[gi]]
        self._fsems, self._fthru, token = _gather_forward("gather_forward%d" % gi, self._arrived, by_cols)
        return token

    def gather_finish(self, gi, after):
        names = _GATHER_GROUPS[gi]
        out = _split_wait("gather_finish%d" % gi, self._fthru, self._fsems[0], self._fsems[1], [3] * len(names),
                          self._shard_size(names, 0), after)
        return dict(zip(names, out))

    def reduce_begin(self, key, grads):
        names = list(grads)
        sems, thru, token = _scatter_start("scatter_start_" + key, [grads[n] for n in names],
                                           [n in _COL_SHARDED for n in names])
        self._scatters[key] = (names, sems, thru)
        return token

    def reduce_finish(self, key, after):
        names, sems, thru = self._scatters[key]
        nw = len(names)
        out = _split_wait("scatter_wait_" + key, thru, sems[0], sems[1], [N_DEV - 1] * nw,
                          functools.partial(_first_block, offset=nw), after)
        for i, n in enumerate(names):
            self.out[n] = _sum_adam("adam_" + n, out[nw + i], out[i], n in _COL_SHARDED, self.w[n], self.mom[n],
                                    self.var[n])


def _step(x, p, target, gains, comm):
    T, D = x.shape
    n_q = D // (2 * HEAD_DIM)
    n_kv = n_q // GROUP
    cos, sin = _rope_tables(T)
    idx = _bucket_index()

    t = comm.gather_begin()
    u = _rms_fwd("norm_attn", x, gains["attn_norm_g"], deps=(t,))
    comm.gather_arrive(0, u)
    t = comm.gather_forward(0)
    bias = _bias_build(idx, gains["rel_bias_table"].reshape(-1), n_q, deps=(t,))
    full = comm.gather_finish(0, bias)
    proj_a, pb = _in_proj(u, full["w_in"], cos, sin, gains["q_norm_g"], gains["k_norm_g"], n_q + n_kv)
    o_a, lse_a = _attn_a_fwd(pb, n_q, n_kv, 2 * n_q)
    comm.gather_arrive(1, lse_a)
    t = comm.gather_forward(1)
    sink = gains["sink_logits"].reshape(-1)
    b_off = n_q + 2 * n_kv
    o_cat, lse_b = _attn_b_fwd(pb, bias, sink, o_a, b_off, n_q, n_kv, deps=(t,))
    full.update(comm.gather_finish(1, lse_b))
    h1, m_in = _mm_nn_rms("out_proj", o_cat, full["w_out"], x, gains["mlp_norm_g"])

    def up_epilogue(acc, extra, outs):
        outs[0][...] = acc.astype(BF16)
        r = jnp.maximum(acc, 0.0)
        outs[1][...] = (r * r).astype(BF16)

    a_act, f_act = _mm_nn("up_proj", m_in, full["w_up"], epilogue=up_epilogue, out_dtypes=[BF16, BF16], tn=2048)
    comm.gather_arrive(2, f_act)
    t = comm.gather_forward(2)
    p_b = p.astype(BF16)
    pe = _mm_nn("ple_proj", p_b, full["ple_w"], deps=(t,))
    full.update(comm.gather_finish(2, pe))
    h2 = _mm_nn("down_proj", f_act, full["w_down"], epilogue=_store_add, extras=(h1,), tn=512)
    gn = _rms_fwd("norm_gate", h2, gains["gate_norm_g"])

    dh3, dz, dpe, dg_final, dg_ple, loss_part = _gate_tail(gn, full["w_gate"], h2, pe, target, gains["ple_norm_g"],
                                                           gains["final_norm_g"])
    gw_gate = _mm_tn("grad_w_gate", gn, dz, tn=1024)
    gw_ple = _mm_tn("grad_ple_w", p_b, dpe)
    dh2, dh2_b, dg_gate = _mm_nt_rms_bwd("d_gate_in", dz, full["w_gate"], h2, gains["gate_norm_g"], dh3, tm=512)
    gw_down = _mm_tn("grad_w_down", f_act, dh2_b, tn=1024)
    t = comm.reduce_begin("b", dict(w_gate=gw_gate, ple_w=gw_ple, w_down=gw_down))

    def act_bwd(acc, extra, outs):
        outs[0][...] = (acc * (2.0 * jnp.maximum(extra[0][...].astype(F32), 0.0))).astype(BF16)

    da = _mm_nt("d_act", dh2_b, full["w_down"], out_dtype=BF16, epilogue=act_bwd, extras=(a_act,), tn=2048, deps=(t,))
    gw_up = _mm_tn("grad_w_up", m_in, da, tn=1024)
    dm = _mm_nt("d_mlp_in", da, full["w_up"], out_dtype=BF16, tn=512)
    dh1, dh1_b, dg_mlp = _rms_bwd("norm_mlp_bwd", dm, h1, gains["mlp_norm_g"], dh2)
    gw_out = _mm_tn("grad_w_out", o_cat, dh1_b, tn=1024)
    t = comm.reduce_begin("d", dict(w_up=gw_up, w_out=gw_out))
    d_o = _mm_nt("d_attn_out", dh1_b, full["w_out"], out_dtype=BF16, deps=(t,))
    dqa, dka_t, dva_t = _attn_a_bwd(pb, o_cat, d_o, lse_a, n_q, n_kv)
    dqb, dkb, dvb, dbias, dsink_raw = _attn_b_bwd(pb, o_cat, d_o, lse_b, bias, sink, b_off, n_q, n_kv, n_q)
    dtable, dsink = _table_grads(dbias, dsink_raw, idx)
    dproj, dg_q, dg_k = _dproj(proj_a, dqa, dka_t, dva_t, dqb, dkb, dvb, cos, sin, gains["q_norm_g"], gains["k_norm_g"])
    gw_in = _mm_tn("grad_w_in", u, dproj, tn=1024)
    t = comm.reduce_begin("e", dict(w_in=gw_in))
    dx, dg_attn = _mm_nt_rms_bwd("d_attn_in", dproj, full["w_in"], x, gains["attn_norm_g"], dh1, with_bf16=False,
                                 tm=512, deps=(t,))
    for key in "bd":
        comm.reduce_finish(key, dx)

    parts = jnp.concatenate([dg_attn, dg_mlp, dg_ple, dg_gate, dg_final, dg_q, dg_k, dtable, dsink, loss_part], axis=1)
    return dx, parts


_SHARDED = ("w_in", "w_out", "w_up", "w_down", "ple_w", "w_gate")
_VECTORS = ("attn_norm_g", "mlp_norm_g", "ple_norm_g", "gate_norm_g", "final_norm_g")
_ORDER = ("attn_norm_g", "w_in", "q_norm_g", "k_norm_g", "sink_logits", "w_out", "mlp_norm_g", "w_up", "w_down",
          "ple_w", "ple_norm_g", "gate_norm_g", "w_gate", "rel_bias_table", "final_norm_g")


def _pack_small(vals, n_heads):
    lane_pad = lambda v: jnp.pad(v, ((0, 0), (0, LANES - v.shape[1])))
    table = lane_pad(vals["rel_bias_table"].T).reshape(1, n_heads * LANES)
    return jnp.concatenate(
        [vals[n].reshape(1, -1) for n in _VECTORS] + [vals["q_norm_g"], vals["k_norm_g"], table,
                                                      lane_pad(vals["sink_logits"]), jnp.zeros((1, LANES), F32)], axis=1)


def _unpack_small(row, like, n_heads):
    out, off = {}, 0
    for n in _VECTORS:
        out[n] = row[:, off:off + like[n].size].reshape(like[n].shape)
        off += like[n].size
    for n in ("q_norm_g", "k_norm_g"):
        out[n] = row[:, off:off + LANES]
        off += LANES
    out["rel_bias_table"] = row[:, off:off + n_heads * LANES].reshape(n_heads, LANES)[:, :N_BUCKETS].T
    off += n_heads * LANES
    out["sink_logits"] = row[:, off:off + n_heads]
    off += LANES
    return out, row[0, off]


def kernel(x, p, attn_norm_g, w_in, q_norm_g, k_norm_g, sink_logits, w_out, mlp_norm_g, w_up, w_down, ple_w, ple_norm_g, gate_norm_g, w_gate, rel_bias_table, final_norm_g, loss_target, m_attn_norm_g, m_w_in, m_q_norm_g, m_k_norm_g, m_sink_logits, m_w_out, m_mlp_norm_g, m_w_up, m_w_down, m_ple_w, m_ple_norm_g, m_gate_norm_g, m_w_gate, m_rel_bias_table, m_final_norm_g, v_attn_norm_g, v_w_in, v_q_norm_g, v_k_norm_g, v_sink_logits, v_w_out, v_mlp_norm_g, v_w_up, v_w_down, v_ple_w, v_ple_norm_g, v_gate_norm_g, v_w_gate, v_rel_bias_table, v_final_norm_g):
    w = dict(attn_norm_g=attn_norm_g, w_in=w_in[0], q_norm_g=q_norm_g, k_norm_g=k_norm_g, sink_logits=sink_logits,
             w_out=w_out[0], mlp_norm_g=mlp_norm_g, w_up=w_up[0], w_down=w_down[0], ple_w=ple_w[0],
             ple_norm_g=ple_norm_g, gate_norm_g=gate_norm_g, w_gate=w_gate[0], rel_bias_table=rel_bias_table,
             final_norm_g=final_norm_g)
    mom = dict(attn_norm_g=m_attn_norm_g, w_in=m_w_in[0], q_norm_g=m_q_norm_g, k_norm_g=m_k_norm_g,
               sink_logits=m_sink_logits, w_out=m_w_out[0], mlp_norm_g=m_mlp_norm_g, w_up=m_w_up[0],
               w_down=m_w_down[0], ple_w=m_ple_w[0], ple_norm_g=m_ple_norm_g, gate_norm_g=m_gate_norm_g,
               w_gate=m_w_gate[0], rel_bias_table=m_rel_bias_table, final_norm_g=m_final_norm_g)
    var = dict(attn_norm_g=v_attn_norm_g, w_in=v_w_in[0], q_norm_g=v_q_norm_g, k_norm_g=v_k_norm_g,
               sink_logits=v_sink_logits, w_out=v_w_out[0], mlp_norm_g=v_mlp_norm_g, w_up=v_w_up[0],
               w_down=v_w_down[0], ple_w=v_ple_w[0], ple_norm_g=v_ple_norm_g, gate_norm_g=v_gate_norm_g,
               w_gate=v_w_gate[0], rel_bias_table=v_rel_bias_table, final_norm_g=v_final_norm_g)
    D = x.shape[-1]
    n_heads = D // (2 * HEAD_DIM)

    gains = {n: w[n] for n in w if n not in _SHARDED}
    gains["final_norm_g"] = final_norm_g.reshape(1, -1)

    comm = _MeshComm(w, mom, var)
    dx, parts = _step(x[0], p[0, 0], loss_target[0], gains, comm)

    small_g = _small_all_reduce(parts, deps=[comm.out[n][0] for n in comm.out])
    comm.reduce_finish("e", small_g)

    g_out, d_out, m_out, v_out = {}, {}, {}, {}
    for n in _SHARDED:
        g, d, nm, nv = comm.out[n]
        g_out[n], d_out[n], m_out[n], v_out[n] = g[None], d[None], nm[None], nv[None]

    small = {n: v for n, v in w.items() if n not in _SHARDED}
    pack = lambda vals: _pack_small({n: vals[n] for n in small}, n_heads)
    sd, sm, sv = _adam_small(pack(w), small_g, pack(mom), pack(var))
    sg, loss = _unpack_small(small_g, small, n_heads)
    g_out.update(sg)
    for dst, row in ((d_out, sd), (m_out, sm), (v_out, sv)):
        dst.update(_unpack_small(row, small, n_heads)[0])

    return (loss, dx[None], *[g_out[n] for n in _ORDER], *[d_out[n] for n in _ORDER],
            *[m_out[n] for n in _ORDER], *[v_out[n] for n in _ORDER])
```

```python
import functools
import math

import numpy as np
import jax
import jax.numpy as jnp
from jax import lax
from jax.experimental import pallas as pl
from jax.experimental.pallas import tpu as pltpu

F32 = jnp.float32
BF16 = jnp.bfloat16

N_DEV = 8
N_CHIP = 4
HEAD_DIM = 128
GROUP = 4
GRID_W = 64
WINDOW = 128
BLOCK_Q = 128
N_BUCKETS = 32
MAX_DISTANCE = 128
ROPE_THETA = 10000.0
EPS = 1e-6
NEG_INF = -1e30
ADAM_LR = 0.001
ADAM_B1 = 0.9
ADAM_B2 = 0.999
ADAM_EPS = 1e-08
ADAM_WD = 0.01
ADAM_STEP = 10
LOG2E = math.log2(math.e)
LANES = 128
SUBLANES = 8
VMEM_LIMIT_BYTES = 60 * 1024 * 1024
MESH = pl.DeviceIdType.MESH

_NT = (((1,), (1,)), ((), ()))
_NN = (((1,), (0,)), ((), ()))
_TN = (((0,), (0,)), ((), ()))


def _tile(dim, pref):
    return pref if dim % pref == 0 else dim


def _params(sem=None):
    return pltpu.CompilerParams(dimension_semantics=sem, vmem_limit_bytes=VMEM_LIMIT_BYTES)


_HBM = pl.BlockSpec(memory_space=pltpu.HBM)
_SEM = pl.BlockSpec(memory_space=pltpu.SEMAPHORE)
_ANY = pl.BlockSpec(memory_space=pl.ANY)
_VMEM = pl.BlockSpec(memory_space=pltpu.VMEM)
_EFFECT = pltpu.SideEffectType.DATAFLOW_SIDE_EFFECTING


def _pcall(body, deps=(), *, in_specs, into=None, **kw):
    deps = [d for d in deps if d is not None]
    nd = len(deps)
    if into is not None:
        deps = [into[0]] + deps
        nd += 1
        kw["input_output_aliases"] = {0: into[1]}

    def wrapped(*refs):
        body(*refs[nd:])

    call = pl.pallas_call(wrapped, in_specs=[_ANY] * nd + list(in_specs), **kw)
    return lambda *args: call(*deps, *args)


def _mm(name, a, b, dims, grid, a_spec, b_spec, out_shape, out_specs, acc_shape, epilogue,
        extras=(), extra_specs=(), deps=(), semantics=("parallel", "parallel", "arbitrary")):
    nk = grid[2]
    n_extra = len(extras)

    def body(*refs):
        a_ref, b_ref = refs[0], refs[1]
        extra = refs[2:2 + n_extra]
        outs = refs[2 + n_extra:-1]
        acc = refs[-1]
        part = lax.dot_general(a_ref[...], b_ref[...], dims, preferred_element_type=F32)
        if nk == 1:
            epilogue(part, extra, outs)
        else:
            k = pl.program_id(2)

            @pl.when(k == 0)
            def _():
                acc[...] = part

            @pl.when(k > 0)
            def _():
                acc[...] += part

            @pl.when(k == nk - 1)
            def _():
                epilogue(acc[...], extra, outs)

    return _pcall(
        body, deps, name=name, grid=grid,
        in_specs=[a_spec, b_spec, *extra_specs],
        out_specs=out_specs, out_shape=out_shape,
        scratch_shapes=[pltpu.VMEM(acc_shape if nk > 1 else (SUBLANES, LANES), F32)],
        compiler_params=_params(semantics),
    )(a, b, *extras)


def _store(dtype):
    def ep(acc, extra, outs):
        outs[0][...] = acc.astype(dtype)
    return ep


def _store_add(acc, extra, outs):
    outs[0][...] = acc + extra[0][...]


def _mm_nn(name, a, b, out_dtype=F32, epilogue=None, extras=(), n_out=1, out_dtypes=None, tm=1024, tn=1024, tk=None,
           deps=()):
    M, K = a.shape
    N = b.shape[1]
    tm, tn, tk = _tile(M, tm), _tile(N, tn), _tile(K, tk or K)
    b_spec = pl.BlockSpec((tk, tn), lambda i, j, k: (k, j))
    grid = (M // tm, N // tn, K // tk)
    o_spec = pl.BlockSpec((tm, tn), lambda i, j, k: (i, j))
    out_dtypes = out_dtypes or [out_dtype] * n_out
    out_shape = [jax.ShapeDtypeStruct((M, N), d) for d in out_dtypes]
    res = _mm(name, a, b, _NN, grid, pl.BlockSpec((tm, tk), lambda i, j, k: (i, k)), b_spec,
              out_shape, [o_spec] * len(out_dtypes), (tm, tn), epilogue or _store(out_dtype),
              extras, [o_spec] * len(extras), deps)
    return res if len(out_dtypes) > 1 else res[0]


def _mm_nt(name, a, b, out_dtype=F32, epilogue=None, extras=(), tm=1024, tn=1024, tk=None, deps=()):
    M, C = a.shape
    N = b.shape[0]
    tm, tn, tk = _tile(M, tm), _tile(N, tn), _tile(C, tk or C)
    b_spec = pl.BlockSpec((tn, tk), lambda i, j, k: (j, k))
    grid = (M // tm, N // tn, C // tk)
    o_spec = pl.BlockSpec((tm, tn), lambda i, j, k: (i, j))
    return _mm(name, a, b, _NT, grid, pl.BlockSpec((tm, tk), lambda i, j, k: (i, k)), b_spec,
               [jax.ShapeDtypeStruct((M, N), out_dtype)], [o_spec], (tm, tn), epilogue or _store(out_dtype),
               extras, [o_spec] * len(extras), deps)[0]


def _mm_tn(name, a, b, out_dtype=BF16, tm=1024, tn=512, tk=None, deps=()):
    T, M = a.shape
    N = b.shape[1]
    tm, tn, tk = _tile(M, tm), _tile(N, tn), _tile(T, tk or T)
    out_shape = jax.ShapeDtypeStruct((M, N), out_dtype)
    o_spec = pl.BlockSpec((tm, tn), lambda i, j, k: (i, j))
    grid = (M // tm, N // tn, T // tk)
    return _mm(name, a, b, _TN, grid, pl.BlockSpec((tk, tm), lambda i, j, k: (k, i)),
               pl.BlockSpec((tk, tn), lambda i, j, k: (k, j)), [out_shape], [o_spec], (tm, tn), _store(out_dtype),
               deps=deps)[0]


def _mean_last(v):
    return jnp.mean(v, axis=-1, keepdims=True)


def _rows_to_sublanes(v):
    r, c = v.shape
    return jnp.sum(v.reshape(r // SUBLANES, SUBLANES, c), axis=0)


def _accumulate(ref, val, first):
    @pl.when(first)
    def _():
        ref[...] = val

    @pl.when(jnp.logical_not(first))
    def _():
        ref[...] += val


def _rms_fwd(name, x, g, tr=512, deps=()):
    T, D = x.shape
    tr = _tile(T, tr)

    def body(x_ref, g_ref, o_ref):
        xv = x_ref[...]
        r = lax.rsqrt(_mean_last(xv * xv) + EPS)
        o_ref[...] = (xv * r * g_ref[...]).astype(BF16)

    row = pl.BlockSpec((tr, D), lambda i: (i, 0))
    return _pcall(
        body, deps, name=name, grid=(T // tr,),
        in_specs=[row, pl.BlockSpec((1, D), lambda i: (0, 0))],
        out_specs=row, out_shape=jax.ShapeDtypeStruct((T, D), BF16),
        compiler_params=_params(("parallel",)),
    )(x, g)


def _rms_bwd(name, dyn, x, g, dres, tr=512, deps=()):
    T, D = x.shape
    tr = _tile(T, tr)

    def body(dy_ref, x_ref, g_ref, dr_ref, dx_ref, dxb_ref, dg_ref):
        xv = x_ref[...]
        r = lax.rsqrt(_mean_last(xv * xv) + EPS)
        xn = xv * r
        dy = dy_ref[...].astype(F32)
        dxn = dy * g_ref[...]
        dx = dr_ref[...] + r * (dxn - xn * _mean_last(dxn * xn))
        dx_ref[...] = dx
        dxb_ref[...] = dx.astype(BF16)
        _accumulate(dg_ref, _rows_to_sublanes(dy * xn), pl.program_id(0) == 0)

    row = pl.BlockSpec((tr, D), lambda i: (i, 0))
    return _pcall(
        body, deps, name=name, grid=(T // tr,),
        in_specs=[row, row, pl.BlockSpec((1, D), lambda i: (0, 0)), row],
        out_specs=[row, row, pl.BlockSpec((SUBLANES, D), lambda i: (0, 0))],
        out_shape=[jax.ShapeDtypeStruct((T, D), F32), jax.ShapeDtypeStruct((T, D), BF16),
                   jax.ShapeDtypeStruct((SUBLANES, D), F32)],
        compiler_params=_params(("arbitrary",)),
    )(dyn, x, g, dres)


def _mm_nn_rms(name, a, b, res, g, tm=512, deps=()):
    M, K = a.shape
    N = b.shape[1]
    tm = _tile(M, tm)

    def epilogue(acc, extra, outs):
        h = acc + extra[0][...]
        outs[0][...] = h
        outs[1][...] = (h * lax.rsqrt(_mean_last(h * h) + EPS) * extra[1][...]).astype(BF16)

    row = pl.BlockSpec((tm, N), lambda i, j, k: (i, 0))
    return _mm(name, a, b, _NN, (M // tm, 1, 1), pl.BlockSpec((tm, K), lambda i, j, k: (i, 0)),
               pl.BlockSpec((K, N), lambda i, j, k: (0, 0)),
               [jax.ShapeDtypeStruct((M, N), F32), jax.ShapeDtypeStruct((M, N), BF16)], [row, row], (tm, N), epilogue,
               (res, g), [row, pl.BlockSpec((1, N), lambda i, j, k: (0, 0))], deps)


def _mm_nt_rms_bwd(name, a, b, x, g, dres, with_bf16=True, tm=256, deps=()):
    M, C = a.shape
    N = b.shape[0]
    tm = _tile(M, tm)

    def epilogue(dy, extra, outs):
        x_ref, dr_ref, g_ref = extra
        xv = x_ref[...]
        r = lax.rsqrt(_mean_last(xv * xv) + EPS)
        xn = xv * r
        dxn = dy * g_ref[...]
        dx = dr_ref[...] + r * (dxn - xn * _mean_last(dxn * xn))
        outs[0][...] = dx
        if with_bf16:
            outs[1][...] = dx.astype(BF16)
        _accumulate(outs[-1], _rows_to_sublanes(dy * xn), pl.program_id(0) == 0)

    row = pl.BlockSpec((tm, N), lambda i, j, k: (i, 0))
    copies = [jax.ShapeDtypeStruct((M, N), F32)] + ([jax.ShapeDtypeStruct((M, N), BF16)] if with_bf16 else [])
    return _mm(name, a, b, _NT, (M // tm, 1, 1), pl.BlockSpec((tm, C), lambda i, j, k: (i, 0)),
               pl.BlockSpec((N, C), lambda i, j, k: (0, 0)),
               copies + [jax.ShapeDtypeStruct((SUBLANES, N), F32)],
               [row] * len(copies) + [pl.BlockSpec((SUBLANES, N), lambda i, j, k: (0, 0))], (tm, N), epilogue,
               (x, dres, g), [row, row, pl.BlockSpec((1, N), lambda i, j, k: (0, 0))], deps,
               semantics=("arbitrary", "arbitrary", "arbitrary"))


def _gate_tail(gn, w_gate, h2, pe, target, g_ple, g_final, tm=256):
    T, D = h2.shape
    tm = _tile(T, tm)

    def epilogue(z, extra, outs):
        h2_ref, pe_ref, t_ref, gp_ref, gf_ref = extra
        dh3_ref, dz_ref, dpe_ref, dgf_ref, dgp_ref, loss_ref = outs
        first = pl.program_id(0) == 0
        pev = pe_ref[...]
        r3 = lax.rsqrt(_mean_last(pev * pev) + EPS)
        en = pev * r3
        e = en * gp_ref[...]
        gate = 1.0 / (1.0 + jnp.exp(-z))
        h3 = h2_ref[...] + gate * e
        r5 = lax.rsqrt(_mean_last(h3 * h3) + EPS)
        hn = h3 * r5
        diff = hn * gf_ref[...] - t_ref[...]
        loss_rows = 0.5 * _mean_last(diff * diff)
        row0 = lax.broadcasted_iota(jnp.int32, (SUBLANES, LANES), 0) == 0
        _accumulate(loss_ref, jnp.where(row0, jnp.sum(loss_rows), 0.0), first)
        dy = diff * (1.0 / D)
        _accumulate(dgf_ref, _rows_to_sublanes(dy * hn), first)
        dhn = dy * gf_ref[...]
        dh3 = r5 * (dhn - hn * _mean_last(dhn * hn))
        dh3_ref[...] = dh3
        dgate = dh3 * e
        de = dh3 * gate
        dz_ref[...] = (dgate * gate * (1.0 - gate)).astype(BF16)
        _accumulate(dgp_ref, _rows_to_sublanes(de * en), first)
        den = de * gp_ref[...]
        dpe_ref[...] = (r3 * (den - en * _mean_last(den * en))).astype(BF16)

    row = pl.BlockSpec((tm, D), lambda i, j, k: (i, 0))
    vec = pl.BlockSpec((1, D), lambda i, j, k: (0, 0))
    part = pl.BlockSpec((SUBLANES, D), lambda i, j, k: (0, 0))
    return _mm("gate_tail", gn, w_gate, _NN, (T // tm, 1, 1), row, pl.BlockSpec(w_gate.shape, lambda i, j, k: (0, 0)),
               [jax.ShapeDtypeStruct((T, D), F32), jax.ShapeDtypeStruct((T, D), BF16),
                jax.ShapeDtypeStruct((T, D), BF16), jax.ShapeDtypeStruct((SUBLANES, D), F32),
                jax.ShapeDtypeStruct((SUBLANES, D), F32), jax.ShapeDtypeStruct((SUBLANES, LANES), F32)],
               [row, row, row, part, part, pl.BlockSpec((SUBLANES, LANES), lambda i, j, k: (0, 0))], (tm, D), epilogue,
               (h2, pe, target, g_ple, g_final), [row, row, row, vec, vec],
               semantics=("arbitrary", "arbitrary", "arbitrary"))


def _rope_tables(T):
    pos = np.arange(T)
    half = HEAD_DIM // 2
    inv = (ROPE_THETA ** (-np.arange(0, half, 2, dtype=np.float32) / half)).astype(np.float32)
    ang_r = (pos // GRID_W).astype(np.float32)[:, None] * inv
    ang_c = (pos % GRID_W).astype(np.float32)[:, None] * inv
    cos = np.concatenate([np.cos(ang_r), np.cos(ang_r), np.cos(ang_c), np.cos(ang_c)], axis=-1)
    sin = np.concatenate([-np.sin(ang_r), np.sin(ang_r), -np.sin(ang_c), np.sin(ang_c)], axis=-1)
    return jnp.asarray(cos, F32), jnp.asarray(sin, F32)


def _swap32(x):
    lane = lax.broadcasted_iota(jnp.int32, x.shape, 1)
    return jnp.where((lane % 64) < 32, pltpu.roll(x, 96, 1), pltpu.roll(x, 32, 1))


def _in_proj(u, w_in, cos, sin, g_q, g_k, n_norm, tm=512):
    T, K = u.shape
    W = w_in.shape[1]
    tm = _tile(T, tm)
    n_q = n_norm * GROUP // (GROUP + 1)
    wa = n_norm * HEAD_DIM

    def epilogue(acc, extra, outs):
        c_ref, s_ref, gq_ref, gk_ref = extra
        raw_ref, o_ref = outs
        c, s = c_ref[...], s_ref[...]
        raw_ref[...] = acc[:, :wa]
        for h in range(n_norm):
            cols = slice(h * HEAD_DIM, (h + 1) * HEAD_DIM)
            xv = acc[:, cols]
            g = gq_ref[...] if h < n_q else gk_ref[...]
            xn = xv * lax.rsqrt(_mean_last(xv * xv) + EPS) * g
            o_ref[:, cols] = (xn * c + _swap32(xn) * s).astype(BF16)
        o_ref[:, wa:] = acc[:, wa:].astype(BF16)

    tab = pl.BlockSpec((tm, HEAD_DIM), lambda i, j, k: (i, 0))
    vec = pl.BlockSpec((1, HEAD_DIM), lambda i, j, k: (0, 0))
    return _mm("in_proj", u, w_in, _NN, (T // tm, 1, 1), pl.BlockSpec((tm, K), lambda i, j, k: (i, 0)),
               pl.BlockSpec((K, W), lambda i, j, k: (0, 0)),
               [jax.ShapeDtypeStruct((T, wa), F32), jax.ShapeDtypeStruct((T, W), BF16)],
               [pl.BlockSpec((tm, wa), lambda i, j, k: (i, 0)), pl.BlockSpec((tm, W), lambda i, j, k: (i, 0))],
               (tm, W), epilogue, (cos, sin, g_q, g_k), [tab, tab, vec, vec])


def _dproj(proj_a, dqa, dka_t, dva_t, dqb, dkb, dvb, cos, sin, g_q, g_k, tr=512):
    T, wa = proj_a.shape
    tr = _tile(T, tr)
    n_q = dqa.shape[1] // HEAD_DIM
    wkv = dka_t.shape[0]
    W = wa + wkv + dqb.shape[1] + dkb.shape[1] + dvb.shape[1]

    def body(p_ref, dqa_ref, dkat_ref, dvat_ref, dqb_ref, dkb_ref, dvb_ref, c_ref, s_ref, gq_ref, gk_ref,
             o_ref, dgq_ref, dgk_ref):
        c, s = c_ref[...], s_ref[...]
        dka = dkat_ref[...].T
        dgq = jnp.zeros((SUBLANES, HEAD_DIM), F32)
        dgk = jnp.zeros((SUBLANES, HEAD_DIM), F32)
        for h in range(wa // HEAD_DIM):
            cols = slice(h * HEAD_DIM, (h + 1) * HEAD_DIM)
            xv = p_ref[:, cols]
            r = lax.rsqrt(_mean_last(xv * xv) + EPS)
            xn = xv * r
            if h < n_q:
                d = dqa_ref[:, cols]
                g = gq_ref[...]
            else:
                d = dka[:, (h - n_q) * HEAD_DIM:(h - n_q + 1) * HEAD_DIM]
                g = gk_ref[...]
            dqn = d * c + _swap32(d * s)
            part = _rows_to_sublanes(dqn * xn)
            if h < n_q:
                dgq = dgq + part
            else:
                dgk = dgk + part
            dxn = dqn * g
            o_ref[:, cols] = (r * (dxn - xn * _mean_last(dxn * xn))).astype(BF16)
        o_ref[:, wa:wa + wkv] = dvat_ref[...].T.astype(BF16)
        off = wa + wkv
        for ref in (dqb_ref, dkb_ref, dvb_ref):
            w = ref.shape[1]
            o_ref[:, off:off + w] = ref[...].astype(BF16)
            off += w
        first = pl.program_id(0) == 0
        _accumulate(dgq_ref, dgq, first)
        _accumulate(dgk_ref, dgk, first)

    def row(w):
        return pl.BlockSpec((tr, w), lambda i: (i, 0))

    col = pl.BlockSpec((wkv, tr), lambda i: (0, i))
    vec = pl.BlockSpec((1, HEAD_DIM), lambda i: (0, 0))
    part = pl.BlockSpec((SUBLANES, HEAD_DIM), lambda i: (0, 0))
    return pl.pallas_call(
        body, name="dproj", grid=(T // tr,),
        in_specs=[row(wa), row(dqa.shape[1]), col, col, row(dqb.shape[1]),
                  row(dkb.shape[1]), row(dvb.shape[1]), row(HEAD_DIM), row(HEAD_DIM), vec, vec],
        out_specs=[row(W), part, part],
        out_shape=[jax.ShapeDtypeStruct((T, W), BF16), jax.ShapeDtypeStruct((SUBLANES, HEAD_DIM), F32),
                   jax.ShapeDtypeStruct((SUBLANES, HEAD_DIM), F32)],
        compiler_params=_params(("arbitrary",)),
    )(proj_a, dqa, dka_t, dva_t, dqb, dkb, dvb, cos, sin, g_q, g_k)


def _attn_a_fwd(pb, n_q, n_kv, out_heads, tq=2048, tc=2048, halves=4):
    T = pb.shape[0]
    tq, tc = _tile(T, tq), _tile(T, tc)
    th = tq // halves
    scale = HEAD_DIM ** -0.5
    c = scale * LOG2E

    def body(q_ref, k_ref, v_ref, o_ref, lse_ref):
        qs = [q_ref[h * th:(h + 1) * th, :] for h in range(halves)]
        m, l, acc = [None] * halves, [None] * halves, [None] * halves
        for j in range(T // tc):
            keys = slice(j * tc, (j + 1) * tc)
            kc, vc = k_ref[keys, :], v_ref[keys, :]
            for h in range(halves):
                s = lax.dot_general(qs[h], kc, _NT, preferred_element_type=F32)
                mj = jnp.max(s, axis=-1, keepdims=True)
                m_new = mj if j == 0 else jnp.maximum(m[h], mj)
                p = jnp.exp2((s - m_new) * c)
                pv = lax.dot_general(p.astype(BF16), vc, _NN, preferred_element_type=F32)
                if j == 0:
                    l[h], acc[h] = jnp.sum(p, axis=-1, keepdims=True), pv
                else:
                    alpha = jnp.exp2((m[h] - m_new) * c)
                    l[h] = alpha * l[h] + jnp.sum(p, axis=-1, keepdims=True)
                    acc[h] = alpha * acc[h] + pv
                m[h] = m_new
        for h in range(halves):
            rows = slice(h * th, (h + 1) * th)
            o_ref[rows, :] = (acc[h] / l[h]).astype(BF16)
            lse_ref[rows, :] = m[h] * scale + jnp.log(l[h])

    return pl.pallas_call(
        body, name="attn_a_fwd", grid=(n_kv, GROUP, T // tq),
        in_specs=[pl.BlockSpec((tq, HEAD_DIM), lambda kv, g, i: (i, kv * GROUP + g)),
                  pl.BlockSpec((T, HEAD_DIM), lambda kv, g, i: (0, n_q + kv)),
                  pl.BlockSpec((T, HEAD_DIM), lambda kv, g, i: (0, n_q + n_kv + kv))],
        out_specs=[pl.BlockSpec((tq, HEAD_DIM), lambda kv, g, i: (i, kv * GROUP + g)),
                   pl.BlockSpec((None, tq, 1), lambda kv, g, i: (kv * GROUP + g, i, 0))],
        out_shape=[jax.ShapeDtypeStruct((T, out_heads * HEAD_DIM), BF16), jax.ShapeDtypeStruct((n_q, T, 1), F32)],
        compiler_params=_params(("parallel", "parallel", "parallel")),
    )(pb, pb, pb)


def _attn_a_bwd(pb, o_cat, d_o, lse, n_q, n_kv, tq=2048, tc=512, halves=4):
    T = pb.shape[0]
    tq, tc = _tile(T, tq), _tile(T, tc)
    th = tq // halves
    scale = HEAD_DIM ** -0.5
    c = scale * LOG2E

    def body(q_ref, k_ref, v_ref, o_ref, do_ref, lse_ref, dq_ref, dkt_ref, dvt_ref):
        @pl.when(jnp.logical_and(pl.program_id(1) == 0, pl.program_id(2) == 0))
        def _():
            dkt_ref[...] = jnp.zeros(dkt_ref.shape, F32)
            dvt_ref[...] = jnp.zeros(dvt_ref.shape, F32)

        groups = []
        for h in range(halves):
            rows = slice(h * th, (h + 1) * th)
            q, do = q_ref[rows, :], do_ref[rows, :]
            delta = jnp.sum(do.astype(F32) * o_ref[rows, :].astype(F32), axis=-1, keepdims=True)
            groups.append((q, do, q.T, do.T, delta, lse_ref[rows, :] * LOG2E))
        dq = [None] * halves
        for j in range(T // tc):
            keys = slice(j * tc, (j + 1) * tc)
            kc, vc = k_ref[keys, :], v_ref[keys, :]
            for h, (q, do, qt, dot, delta, lse2) in enumerate(groups):
                s = lax.dot_general(q, kc, _NT, preferred_element_type=F32)
                p = jnp.exp2(s * c - lse2)
                dp = lax.dot_general(do, vc, _NT, preferred_element_type=F32)
                ds = (p * (dp - delta) * scale).astype(BF16)
                dqj = lax.dot_general(ds, kc, _NN, preferred_element_type=F32)
                dq[h] = dqj if dq[h] is None else dq[h] + dqj
                dvt_ref[:, keys] += lax.dot_general(dot, p.astype(BF16), _NN, preferred_element_type=F32)
                dkt_ref[:, keys] += lax.dot_general(qt, ds, _NN, preferred_element_type=F32)
        for h in range(halves):
            dq_ref[h * th:(h + 1) * th, :] = dq[h]

    qmap = lambda kv, g, i: (i, kv * GROUP + g)
    return pl.pallas_call(
        body, name="attn_a_bwd", grid=(n_kv, GROUP, T // tq),
        in_specs=[pl.BlockSpec((tq, HEAD_DIM), qmap),
                  pl.BlockSpec((T, HEAD_DIM), lambda kv, g, i: (0, n_q + kv)),
                  pl.BlockSpec((T, HEAD_DIM), lambda kv, g, i: (0, n_q + n_kv + kv)),
                  pl.BlockSpec((tq, HEAD_DIM), qmap),
                  pl.BlockSpec((tq, HEAD_DIM), qmap),
                  pl.BlockSpec((None, tq, 1), lambda kv, g, i: (kv * GROUP + g, i, 0))],
        out_specs=[pl.BlockSpec((tq, HEAD_DIM), qmap),
                   pl.BlockSpec((HEAD_DIM, T), lambda kv, g, i: (kv, 0)),
                   pl.BlockSpec((HEAD_DIM, T), lambda kv, g, i: (kv, 0))],
        out_shape=[jax.ShapeDtypeStruct((T, n_q * HEAD_DIM), F32),
                   jax.ShapeDtypeStruct((n_kv * HEAD_DIM, T), F32),
                   jax.ShapeDtypeStruct((n_kv * HEAD_DIM, T), F32)],
        compiler_params=_params(("parallel", "arbitrary", "arbitrary")),
    )(pb, pb, pb, o_cat, d_o, lse)


def _bucket_index():
    r = np.arange(BLOCK_Q)[:, None]
    j = np.arange(3 * BLOCK_Q)[None, :]
    rel = (j - BLOCK_Q) - r
    nb = N_BUCKETS // 2
    ret = np.where(rel > 0, nb, 0)
    n = np.abs(rel)
    max_exact = nb // 2
    nf = np.maximum(n, 1).astype(np.float32)
    large = max_exact + (np.log(nf / max_exact) / math.log(MAX_DISTANCE / max_exact) * (nb - max_exact)).astype(np.int32)
    large = np.minimum(large, nb - 1)
    return jnp.asarray(ret + np.where(n < max_exact, n, large), jnp.int32)


def _bias_build(idx, table_flat, n_heads, deps=()):
    def body(idx_ref, tab_ref, o_ref):
        h = pl.program_id(0)
        iv = idx_ref[...]
        acc = jnp.zeros(iv.shape, F32)
        for b in range(N_BUCKETS):
            acc = jnp.where(iv == b, tab_ref[b * n_heads + h], acc)
        r = lax.broadcasted_iota(jnp.int32, iv.shape, 0)
        j = lax.broadcasted_iota(jnp.int32, iv.shape, 1)
        o_ref[...] = jnp.where(jnp.abs(j - BLOCK_Q - r) <= WINDOW, acc, NEG_INF)

    return _pcall(
        body, deps, name="bias_build", grid=(n_heads,),
        in_specs=[pl.BlockSpec(idx.shape, lambda h: (0, 0)), pl.BlockSpec(memory_space=pltpu.SMEM)],
        out_specs=pl.BlockSpec((None,) + idx.shape, lambda h: (h, 0, 0)),
        out_shape=jax.ShapeDtypeStruct((n_heads,) + idx.shape, F32),
        compiler_params=_params(("parallel",)),
    )(idx, table_flat)


def _in_sequence(n, T):
    j = lax.broadcasted_iota(jnp.int32, (GROUP * BLOCK_Q, 3 * BLOCK_Q), 1)
    kabs = n * BLOCK_Q + j - BLOCK_Q
    return (kabs >= 0) & (kabs < T)


def _per_head_rows(values):
    head = lax.broadcasted_iota(jnp.int32, (GROUP * BLOCK_Q, 1), 0) // BLOCK_Q
    col = jnp.zeros((GROUP * BLOCK_Q, 1), F32)
    for g, v in enumerate(values):
        col = jnp.where(head == g, v, col)
    return col


def _band_specs(col, nblk, sb):
    return [pl.BlockSpec((BLOCK_Q, HEAD_DIM), lambda kv, i: (jnp.maximum(sb * i - 1, 0), col(kv))),
            pl.BlockSpec((sb * BLOCK_Q, HEAD_DIM), lambda kv, i: (i, col(kv))),
            pl.BlockSpec((BLOCK_Q, HEAD_DIM), lambda kv, i: (jnp.minimum(sb * i + sb, nblk - 1), col(kv)))]


def _head_specs(base, rows):
    return [pl.BlockSpec((rows, HEAD_DIM), functools.partial(lambda kv, i, g: (i, base + kv * GROUP + g), g=g))
            for g in range(GROUP)]


def _attn_b_fwd(pb, bias, sink, o_all, q_off, n_q, n_kv, deps=(), sb=16):
    T = pb.shape[0]
    nblk = T // BLOCK_Q
    sb = min(sb, nblk)
    tq = sb * BLOCK_Q
    scale = HEAD_DIM ** -0.5

    def body(*refs):
        q_refs = refs[0:GROUP]
        k_refs, v_refs = refs[GROUP:GROUP + 3], refs[GROUP + 3:GROUP + 6]
        bias_ref, sink_ref, o_ref, lse_ref = refs[GROUP + 6:]
        kv, i = pl.program_id(0), pl.program_id(1)
        kb = jnp.concatenate([r[...] for r in k_refs], axis=0)
        vb = jnp.concatenate([r[...] for r in v_refs], axis=0)
        bias_all = bias_ref[...].reshape(GROUP * BLOCK_Q, 3 * BLOCK_Q)
        sk = _per_head_rows([sink_ref[kv * GROUP + g] for g in range(GROUP)])
        for b in range(sb):
            rows = slice(b * BLOCK_Q, (b + 1) * BLOCK_Q)
            kw, vw = kb[b * BLOCK_Q:(b + 3) * BLOCK_Q], vb[b * BLOCK_Q:(b + 3) * BLOCK_Q]
            q = jnp.concatenate([r[rows, :] for r in q_refs], axis=0)
            s = lax.dot_general(q, kw, _NT, preferred_element_type=F32) * scale + bias_all
            if b == 0 or b == sb - 1:
                s = jnp.where(_in_sequence(i * sb + b, T), s, NEG_INF)
            m = jnp.maximum(jnp.max(s, axis=-1, keepdims=True), sk)
            p = jnp.exp(s - m)
            l = jnp.sum(p, axis=-1, keepdims=True) + jnp.exp(sk - m)
            o = (lax.dot_general(p.astype(BF16), vw, _NN, preferred_element_type=F32) / l).astype(BF16)
            lse = m + jnp.log(l)
            for g in range(GROUP):
                head = slice(g * BLOCK_Q, (g + 1) * BLOCK_Q)
                o_ref[rows, g * HEAD_DIM:(g + 1) * HEAD_DIM] = o[head]
                lse_ref[g, rows, :] = lse[head]

    first_group = o_all.shape[1] // (GROUP * HEAD_DIM) - n_kv
    return _pcall(
        body, deps, into=(o_all, 0), name="attn_b_fwd", grid=(n_kv, nblk // sb),
        in_specs=[*_head_specs(q_off, tq),
                  *_band_specs(lambda kv: q_off + n_q + kv, nblk, sb),
                  *_band_specs(lambda kv: q_off + n_q + n_kv + kv, nblk, sb),
                  pl.BlockSpec((GROUP, BLOCK_Q, 3 * BLOCK_Q), lambda kv, i: (kv, 0, 0)),
                  pl.BlockSpec(memory_space=pltpu.SMEM)],
        out_specs=[pl.BlockSpec((tq, GROUP * HEAD_DIM), lambda kv, i: (i, first_group + kv)),
                   pl.BlockSpec((GROUP, tq, 1), lambda kv, i: (kv, i, 0))],
        out_shape=[jax.ShapeDtypeStruct(o_all.shape, BF16), jax.ShapeDtypeStruct((n_q, T, 1), F32)],
        compiler_params=_params(("parallel", "parallel")),
    )(*([pb] * (GROUP + 6)), bias, sink)


def _attn_b_bwd(pb, o_cat, d_o, lse, bias, sink, q_off, n_q, n_kv, o_off, deps=(), sb=16):
    T = pb.shape[0]
    nblk = T // BLOCK_Q
    sb = min(sb, nblk)
    tq = sb * BLOCK_Q
    scale = HEAD_DIM ** -0.5

    def body(*refs):
        q_refs = refs[0:GROUP]
        k_refs, v_refs = refs[GROUP:GROUP + 3], refs[GROUP + 3:GROUP + 6]
        o_refs, do_refs = refs[GROUP + 6:2 * GROUP + 6], refs[2 * GROUP + 6:3 * GROUP + 6]
        lse_ref, bias_ref, sink_ref, dq_ref, dk_ref, dv_ref, dbias_ref, dsink_ref, dkb_ref, dvb_ref = refs[3 * GROUP + 6:]
        kv, i = pl.program_id(0), pl.program_id(1)
        first = i == 0

        @pl.when(first)
        def _():
            dk_ref[...] = jnp.zeros(dk_ref.shape, F32)
            dv_ref[...] = jnp.zeros(dv_ref.shape, F32)
            dbias_ref[...] = jnp.zeros(dbias_ref.shape, F32)

        kb = jnp.concatenate([r[...] for r in k_refs], axis=0)
        vb = jnp.concatenate([r[...] for r in v_refs], axis=0)
        dkb_ref[...] = jnp.zeros(dkb_ref.shape, F32)
        dvb_ref[...] = jnp.zeros(dvb_ref.shape, F32)
        row = lax.broadcasted_iota(jnp.int32, (SUBLANES, LANES), 0)
        dsink = jnp.zeros((SUBLANES, LANES), F32)
        bias_all = bias_ref[...].reshape(GROUP * BLOCK_Q, 3 * BLOCK_Q)
        sk = _per_head_rows([sink_ref[kv * GROUP + g] for g in range(GROUP)])
        for b in range(sb):
            rows = slice(b * BLOCK_Q, (b + 1) * BLOCK_Q)
            win = slice(b * BLOCK_Q, (b + 3) * BLOCK_Q)
            kw, vw = kb[win], vb[win]
            q = jnp.concatenate([r[rows, :] for r in q_refs], axis=0)
            do = jnp.concatenate([r[rows, :] for r in do_refs], axis=0)
            o = jnp.concatenate([r[rows, :] for r in o_refs], axis=0)
            lse = jnp.concatenate([lse_ref[g, rows, :] for g in range(GROUP)], axis=0)
            delta = jnp.sum(do.astype(F32) * o.astype(F32), axis=-1, keepdims=True)
            s = lax.dot_general(q, kw, _NT, preferred_element_type=F32) * scale + bias_all
            if b == 0 or b == sb - 1:
                s = jnp.where(_in_sequence(i * sb + b, T), s, NEG_INF)
            p = jnp.exp(s - lse)
            dp = lax.dot_general(do, vw, _NT, preferred_element_type=F32)
            ds = p * (dp - delta)
            dbias_ref[...] += ds.reshape(GROUP, BLOCK_Q, 3 * BLOCK_Q)
            sunk = jnp.exp(sk - lse) * delta
            for g in range(GROUP):
                dsink = dsink + jnp.where(row == g, -jnp.sum(sunk[g * BLOCK_Q:(g + 1) * BLOCK_Q]), 0.0)
            dsb = (ds * scale).astype(BF16)
            dq = lax.dot_general(dsb, kw, _NN, preferred_element_type=F32).astype(BF16)
            for g in range(GROUP):
                dq_ref[rows, g * HEAD_DIM:(g + 1) * HEAD_DIM] = dq[g * BLOCK_Q:(g + 1) * BLOCK_Q]
            dkb_ref[win, :] += lax.dot_general(dsb, q, _TN, preferred_element_type=F32)
            dvb_ref[win, :] += lax.dot_general(p.astype(BF16), do, _TN, preferred_element_type=F32)
        _accumulate(dsink_ref, dsink, first)

        before = pl.ds(pl.multiple_of(jnp.maximum(sb * i - 1, 0) * BLOCK_Q, BLOCK_Q), BLOCK_Q)
        own = pl.ds(pl.multiple_of(i * tq, BLOCK_Q), tq)
        after = pl.ds(pl.multiple_of(jnp.minimum(sb * i + sb, nblk - 1) * BLOCK_Q, BLOCK_Q), BLOCK_Q)
        for acc_ref, band_ref in ((dk_ref, dkb_ref), (dv_ref, dvb_ref)):
            acc_ref[before, :] += band_ref[0:BLOCK_Q, :]
            acc_ref[own, :] += band_ref[BLOCK_Q:BLOCK_Q + tq, :]
            acc_ref[after, :] += band_ref[BLOCK_Q + tq:, :]

    return _pcall(
        body, deps, name="attn_b_bwd", grid=(n_kv, nblk // sb),
        in_specs=[*_head_specs(q_off, tq),
                  *_band_specs(lambda kv: q_off + n_q + kv, nblk, sb),
                  *_band_specs(lambda kv: q_off + n_q + n_kv + kv, nblk, sb),
                  *_head_specs(o_off, tq), *_head_specs(o_off, tq),
                  pl.BlockSpec((GROUP, tq, 1), lambda kv, i: (kv, i, 0)),
                  pl.BlockSpec((GROUP, BLOCK_Q, 3 * BLOCK_Q), lambda kv, i: (kv, 0, 0)),
                  pl.BlockSpec(memory_space=pltpu.SMEM)],
        out_specs=[pl.BlockSpec((tq, GROUP * HEAD_DIM), lambda kv, i: (i, kv)),
                   pl.BlockSpec((T, HEAD_DIM), lambda kv, i: (0, kv)),
                   pl.BlockSpec((T, HEAD_DIM), lambda kv, i: (0, kv)),
                   pl.BlockSpec((GROUP, BLOCK_Q, 3 * BLOCK_Q), lambda kv, i: (kv, 0, 0)),
                   pl.BlockSpec((None, SUBLANES, LANES), lambda kv, i: (kv, 0, 0))],
        out_shape=[jax.ShapeDtypeStruct((T, n_q * HEAD_DIM), BF16),
                   jax.ShapeDtypeStruct((T, n_kv * HEAD_DIM), F32),
                   jax.ShapeDtypeStruct((T, n_kv * HEAD_DIM), F32),
                   jax.ShapeDtypeStruct((n_q, BLOCK_Q, 3 * BLOCK_Q), F32),
                   jax.ShapeDtypeStruct((n_kv, SUBLANES, LANES), F32)],
        scratch_shapes=[pltpu.VMEM((tq + 2 * BLOCK_Q, HEAD_DIM), F32), pltpu.VMEM((tq + 2 * BLOCK_Q, HEAD_DIM), F32)],
        compiler_params=_params(("parallel", "arbitrary")),
    )(*([pb] * (GROUP + 6)), *([o_cat] * GROUP), *([d_o] * GROUP), lse, bias, sink)


def _table_grads(dbias, dsink_raw, idx):
    n_heads = dbias.shape[0]
    n_kv = dsink_raw.shape[0]

    def body(db_ref, ds_ref, idx_ref, dt_ref, dsk_ref):
        iv = idx_ref[...]
        row = lax.broadcasted_iota(jnp.int32, (SUBLANES, LANES), 0)
        lane = lax.broadcasted_iota(jnp.int32, (SUBLANES, LANES), 1)
        dsk = jnp.zeros((SUBLANES, LANES), F32)
        for h in range(n_heads):
            d = db_ref[h]
            acc = jnp.zeros((SUBLANES, LANES), F32)
            for b in range(N_BUCKETS):
                acc = jnp.where((row == 0) & (lane == b), jnp.sum(jnp.where(iv == b, d, 0.0)), acc)
            dt_ref[:, h * LANES:(h + 1) * LANES] = acc
            raw = ds_ref[h // GROUP]
            val = jnp.sum(jnp.where((row == h % GROUP) & (lane == 0), raw, 0.0))
            dsk = jnp.where((row == 0) & (lane == h), val, dsk)
        dsk_ref[...] = dsk

    return pl.pallas_call(
        body, name="table_grads",
        in_specs=[pl.BlockSpec(memory_space=pltpu.VMEM)] * 3,
        out_specs=[pl.BlockSpec(memory_space=pltpu.VMEM)] * 2,
        out_shape=[jax.ShapeDtypeStruct((SUBLANES, n_heads * LANES), F32),
                   jax.ShapeDtypeStruct((SUBLANES, LANES), F32)],
        compiler_params=_params(),
    )(dbias, dsink_raw, idx)


def _position():
    x, y, c = lax.axis_index("x"), lax.axis_index("y"), lax.axis_index("c")
    return x, y, c


def _hbm(a):
    return pltpu.with_memory_space_constraint(a, pltpu.HBM)


def _split_start(name, bufs, sem_shapes, issue):
    nb, ns = len(bufs), len(sem_shapes)

    def body(*refs):
        buf_refs = refs[:nb]
        sems = refs[nb:nb + ns]
        token = refs[nb + ns + nb]
        issue(buf_refs, sems)
        token[...] = jnp.zeros(token.shape, F32)

    outs = pl.pallas_call(
        body, name=name,
        in_specs=[_HBM] * nb,
        out_specs=[_SEM] * ns + [_HBM] * nb + [_VMEM],
        out_shape=[pltpu.SemaphoreType.DMA(s) for s in sem_shapes] + [pltpu.HBM(b.shape, b.dtype) for b in bufs]
        + [jax.ShapeDtypeStruct((SUBLANES, LANES), F32)],
        input_output_aliases={i: ns + i for i in range(nb)},
        compiler_params=pltpu.CompilerParams(has_side_effects=_EFFECT),
    )(*[_hbm(b) for b in bufs])
    return outs[:ns], outs[ns:ns + nb], outs[-1]


def _split_wait(name, bufs, send, recv, counts, size_of, after):
    nb = len(bufs)

    def body(*refs):
        buf_refs = refs[:nb]
        send_ref, recv_ref = refs[nb], refs[nb + 1]
        x, y, c = _position()
        for w, n in enumerate(counts):
            ref = size_of(buf_refs, w)
            for k in range(n):
                s = sum(counts[:w]) + k
                cp = pltpu.make_async_remote_copy(
                    src_ref=ref, dst_ref=ref, send_sem=send_ref.at[s], recv_sem=recv_ref.at[s],
                    device_id=(x, y, c), device_id_type=MESH)
                cp.wait_send()
                cp.wait_recv()

    return pl.pallas_call(
        body, name=name,
        in_specs=[_HBM] * nb + [_SEM, _SEM, _ANY],
        out_specs=[_HBM] * nb,
        out_shape=[pltpu.HBM(b.shape, b.dtype) for b in bufs],
        input_output_aliases={i: i for i in range(nb)},
        compiler_params=pltpu.CompilerParams(has_side_effects=_EFFECT),
    )(*bufs, send, recv, after)


def _block_of(pos):
    return 4 * pos[0] + 2 * pos[1] + pos[2]


def _shard_of(ref, blk, by_cols):
    aligned = (lambda v, a: v) if isinstance(blk, int) else pl.multiple_of
    if by_cols:
        n = ref.shape[1] // N_DEV
        return ref.at[:, pl.ds(aligned(blk * n, LANES), n)]
    r = ref.shape[0] // N_DEV
    return ref.at[pl.ds(aligned(blk * r, SUBLANES), r), :]


def _place_own(name, land, shard, by_cols, tr=256):
    r, n = shard.shape
    tr = _tile(r, tr)
    mine = _block_of(_position()).astype(jnp.int32).reshape(1)

    def body(m_ref, land_ref, s_ref, o_ref):
        o_ref[...] = s_ref[...]

    if by_cols:
        out = pl.BlockSpec((tr, n), lambda i, m_ref: (i, m_ref[0]))
    else:
        out = pl.BlockSpec((tr, n), lambda i, m_ref: (m_ref[0] * (r // tr) + i, 0))
    return pl.pallas_call(
        body, name=name,
        grid_spec=pltpu.PrefetchScalarGridSpec(
            num_scalar_prefetch=1, grid=(r // tr,),
            in_specs=[_ANY, pl.BlockSpec((tr, n), lambda i, m_ref: (i, 0))], out_specs=out),
        out_shape=jax.ShapeDtypeStruct(land.shape, land.dtype),
        input_output_aliases={1: 0},
        compiler_params=_params(("parallel",)),
    )(mine, land, shard)


def _gather_start(name, shards, by_cols, groups, after=None):
    nw = len(shards)
    lands = [lax.empty((s.shape[0], s.shape[1] * N_DEV) if cols else (s.shape[0] * N_DEV, s.shape[1]), s.dtype)
             for s, cols in zip(shards, by_cols)]
    order = [] if after is None else [after]

    def issue(bufs, sems):
        x, y, c = _position()
        peers = [(x, y, 1 - c), (1 - x, y, c), (x, 1 - y, c), (1 - x, 1 - y, c)]
        for gi, grp in enumerate(groups):
            for wi, w in enumerate(grp):
                for k, peer in enumerate(peers):
                    pltpu.make_async_remote_copy(
                        src_ref=bufs[w], dst_ref=_shard_of(bufs[nw + w], _block_of((x, y, c)), by_cols[w]),
                        send_sem=sems[2 * gi].at[4 * wi + k], recv_sem=sems[2 * gi + 1].at[4 * wi + k],
                        device_id=peer, device_id_type=MESH).start()

    sem_shapes = [(4 * len(g),) for g in groups for _ in range(2)]
    sems, thru, token = _split_start(name, list(shards) + lands + order, sem_shapes, issue)
    return sems, thru[:nw], thru[nw:2 * nw], token


def _gather_forward(name, lands, by_cols):
    nw = len(lands)

    def issue(land, sems):
        x, y, c = _position()
        for w in range(nw):
            for k, chip in enumerate([(1 - x, y), (x, 1 - y), (1 - x, 1 - y)]):
                blk = _shard_of(land[w], _block_of((*chip, c)), by_cols[w])
                pltpu.make_async_remote_copy(
                    src_ref=blk, dst_ref=blk, send_sem=sems[0].at[3 * w + k], recv_sem=sems[1].at[3 * w + k],
                    device_id=(x, y, 1 - c), device_id_type=MESH).start()

    return _split_start(name, lands, [(3 * nw,), (3 * nw,)], issue)


def _first_block(bufs, w, offset=0):
    return bufs[offset + w].at[0]


_PEER_FLIPS = ((0, 0, 1), (1, 0, 0), (1, 0, 1), (0, 1, 0), (0, 1, 1), (1, 1, 0), (1, 1, 1))


def _scatter_start(name, grads, by_cols):
    nw = len(grads)
    lands = []
    for g, cols in zip(grads, by_cols):
        shard = (g.shape[0], g.shape[1] // N_DEV) if cols else (g.shape[0] // N_DEV, g.shape[1])
        lands.append(lax.empty((N_DEV,) + shard, g.dtype))

    def issue(bufs, sems):
        x, y, c = _position()
        flip = lambda v, f: 1 - v if f else v
        for w in range(nw):
            for k, (fx, fy, fc) in enumerate(_PEER_FLIPS):
                peer = (flip(x, fx), flip(y, fy), flip(c, fc))
                pltpu.make_async_remote_copy(
                    src_ref=_shard_of(bufs[w], _block_of(peer), by_cols[w]), dst_ref=bufs[nw + w].at[_block_of((x, y, c))],
                    send_sem=sems[0].at[7 * w + k], recv_sem=sems[1].at[7 * w + k],
                    device_id=peer, device_id_type=MESH).start()

    return _split_start(name, list(grads) + lands, [(7 * nw,), (7 * nw,)], issue)


def _adam(w, g, m, v):
    m = ADAM_B1 * m + (1.0 - ADAM_B1) * g
    v = ADAM_B2 * v + (1.0 - ADAM_B2) * (g * g)
    m_hat = m / (1.0 - ADAM_B1 ** ADAM_STEP)
    v_hat = v / (1.0 - ADAM_B2 ** ADAM_STEP)
    delta = -ADAM_LR * (m_hat / (jnp.sqrt(v_hat) + ADAM_EPS) + ADAM_WD * w)
    return delta, m, v


def _sum_adam(name, landed, grad, by_cols, w, m, v, tr=256):
    R, C = w.shape
    tr = _tile(R, tr if C > 1024 else 2 * tr)
    mine = _block_of(_position()).astype(jnp.int32).reshape(1)

    def body(me_ref, l_ref, own_ref, w_ref, m_ref, v_ref, g_ref, d_ref, nm_ref, nv_ref):
        own = own_ref[...].astype(F32)
        g = None
        for d in range(N_DEV):
            part = jnp.where(me_ref[0] == d, own, l_ref[d].astype(F32))
            g = part if g is None else g + part
        g_ref[...] = g
        d_ref[...], nm_ref[...], nv_ref[...] = _adam(w_ref[...], g, m_ref[...], v_ref[...])

    tile = pl.BlockSpec((tr, C), lambda i, me_ref: (i, 0))
    if by_cols:
        own = pl.BlockSpec((tr, C), lambda i, me_ref: (i, me_ref[0]))
    else:
        own = pl.BlockSpec((tr, C), lambda i, me_ref: (me_ref[0] * (R // tr) + i, 0))
    return pl.pallas_call(
        body, name=name,
        grid_spec=pltpu.PrefetchScalarGridSpec(
            num_scalar_prefetch=1, grid=(R // tr,),
            in_specs=[pl.BlockSpec((N_DEV, tr, C), lambda i, me_ref: (0, i, 0)), own, tile, tile, tile],
            out_specs=[tile] * 4),
        out_shape=[jax.ShapeDtypeStruct((R, C), F32)] * 4,
        compiler_params=_params(("parallel",)),
    )(mine, landed, grad, w, m, v)


def _small_all_reduce(parts, deps=()):
    W = parts.shape[1]

    def body(p_ref, o_ref, slots, send_sems, recv_sems):
        x, y, c = _position()
        me = 4 * x + 2 * y + c
        slots[me] = jnp.sum(p_ref[...], axis=0, keepdims=True)
        peers = [(x, y, 1 - c), (1 - x, y, c), (1 - x, y, 1 - c), (x, 1 - y, c), (x, 1 - y, 1 - c),
                 (1 - x, 1 - y, c), (1 - x, 1 - y, 1 - c)]
        copies = []
        for k, peer in enumerate(peers):
            cp = pltpu.make_async_remote_copy(
                src_ref=slots.at[me], dst_ref=slots.at[me], send_sem=send_sems.at[k], recv_sem=recv_sems.at[k],
                device_id=peer, device_id_type=MESH)
            cp.start()
            copies.append(cp)
        for cp in copies:
            cp.wait()
        total = slots[0]
        for d in range(1, N_DEV):
            total = total + slots[d]
        o_ref[...] = total

    return _pcall(
        body, deps, name="small_all_reduce",
        in_specs=[pl.BlockSpec(memory_space=pltpu.VMEM)], out_specs=pl.BlockSpec(memory_space=pltpu.VMEM),
        out_shape=jax.ShapeDtypeStruct((1, W), F32),
        scratch_shapes=[pltpu.VMEM((N_DEV, 1, W), F32), pltpu.SemaphoreType.DMA((7,)), pltpu.SemaphoreType.DMA((7,))],
    )(parts)


def _adam_small(w, g, m, v):
    def body(w_ref, g_ref, m_ref, v_ref, d_ref, nm_ref, nv_ref):
        d_ref[...], nm_ref[...], nv_ref[...] = _adam(w_ref[...], g_ref[...], m_ref[...], v_ref[...])

    return pl.pallas_call(
        body, name="adam_small",
        in_specs=[pl.BlockSpec(memory_space=pltpu.VMEM)] * 4, out_specs=[pl.BlockSpec(memory_space=pltpu.VMEM)] * 3,
        out_shape=[jax.ShapeDtypeStruct(w.shape, F32)] * 3,
    )(w, g, m, v)


_GATHER_GROUPS = (("w_in",), ("w_out", "w_up", "ple_w"), ("w_down", "w_gate"))
_COL_SHARDED = ("w_in", "w_up", "ple_w")


class _MeshComm:
    def __init__(self, w, mom, var):
        self.w, self.mom, self.var = w, mom, var
        self.out = {}
        self._scatters = {}

    def gather_begin(self):
        self._groups = {}
        token = None
        for tag, first, group_list in (("gather_start0", 0, _GATHER_GROUPS[:1]), ("gather_start1", 1, _GATHER_GROUPS[1:])):
            names = [n for g in group_list for n in g]
            idx = {n: i for i, n in enumerate(names)}
            by_cols = [n in _COL_SHARDED for n in names]
            sems, src, lands, token = _gather_start(tag, [self.w[n].astype(BF16) for n in names], by_cols,
                                                    [[idx[n] for n in g] for g in group_list], token)
            lands = [_place_own("place_" + n, land, s, cols) for n, land, s, cols in zip(names, lands, src, by_cols)]
            for k, g in enumerate(group_list):
                self._groups[first + k] = (sems[2 * k], sems[2 * k + 1], [src[idx[n]] for n in g],
                                           [lands[idx[n]] for n in g])
        return token

    @staticmethod
    def _shard_size(names, offset):
        return lambda bufs, w: _shard_of(bufs[offset + w], 0, names[w] in _COL_SHARDED)

    def gather_arrive(self, gi, after):
        names = _GATHER_GROUPS[gi]
        send, recv, src, lands = self._groups[gi]
        out = _split_wait("gather_arrive%d" % gi, src + lands, send, recv, [4] * len(names),
                          self._shard_size(names, len(names)), after)
        self._arrived = out[len(names):]

    def gather_forward(self, gi):
        by_cols = [n in _COL_SHARDED for n in _GATHER_GROUPS[gi]]
        self._fsems, self._fthru, token = _gather_forward("gather_forward%d" % gi, self._arrived, by_cols)
        return token

    def gather_finish(self, gi, after):
        names = _GATHER_GROUPS[gi]
        out = _split_wait("gather_finish%d" % gi, self._fthru, self._fsems[0], self._fsems[1], [3] * len(names),
                          self._shard_size(names, 0), after)
        return dict(zip(names, out))

    def reduce_begin(self, key, grads):
        names = list(grads)
        sems, thru, token = _scatter_start("scatter_start_" + key, [grads[n] for n in names],
                                           [n in _COL_SHARDED for n in names])
        self._scatters[key] = (names, sems, thru)
        return token

    def reduce_finish(self, key, after):
        names, sems, thru = self._scatters[key]
        nw = len(names)
        out = _split_wait("scatter_wait_" + key, thru, sems[0], sems[1], [N_DEV - 1] * nw,
                          functools.partial(_first_block, offset=nw), after)
        for i, n in enumerate(names):
            self.out[n] = _sum_adam("adam_" + n, out[nw + i], out[i], n in _COL_SHARDED, self.w[n], self.mom[n],
                                    self.var[n])


def _step(x, p, target, gains, comm):
    T, D = x.shape
    n_q = D // (2 * HEAD_DIM)
    n_kv = n_q // GROUP
    cos, sin = _rope_tables(T)
    idx = _bucket_index()

    t = comm.gather_begin()
    u = _rms_fwd("norm_attn", x, gains["attn_norm_g"], deps=(t,))
    comm.gather_arrive(0, u)
    t = comm.gather_forward(0)
    bias = _bias_build(idx, gains["rel_bias_table"].reshape(-1), n_q, deps=(t,))
    full = comm.gather_finish(0, bias)
    proj_a, pb = _in_proj(u, full["w_in"], cos, sin, gains["q_norm_g"], gains["k_norm_g"], n_q + n_kv)
    o_a, lse_a = _attn_a_fwd(pb, n_q, n_kv, 2 * n_q)
    comm.gather_arrive(1, lse_a)
    t = comm.gather_forward(1)
    sink = gains["sink_logits"].reshape(-1)
    b_off = n_q + 2 * n_kv
    o_cat, lse_b = _attn_b_fwd(pb, bias, sink, o_a, b_off, n_q, n_kv, deps=(t,))
    full.update(comm.gather_finish(1, lse_b))
    h1, m_in = _mm_nn_rms("out_proj", o_cat, full["w_out"], x, gains["mlp_norm_g"])

    def up_epilogue(acc, extra, outs):
        outs[0][...] = acc.astype(BF16)
        r = jnp.maximum(acc, 0.0)
        outs[1][...] = (r * r).astype(BF16)

    a_act, f_act = _mm_nn("up_proj", m_in, full["w_up"], epilogue=up_epilogue, out_dtypes=[BF16, BF16], tn=2048)
    comm.gather_arrive(2, f_act)
    t = comm.gather_forward(2)
    p_b = p.astype(BF16)
    pe = _mm_nn("ple_proj", p_b, full["ple_w"], deps=(t,))
    full.update(comm.gather_finish(2, pe))
    h2 = _mm_nn("down_proj", f_act, full["w_down"], epilogue=_store_add, extras=(h1,), tn=512)
    gn = _rms_fwd("norm_gate", h2, gains["gate_norm_g"])

    dh3, dz, dpe, dg_final, dg_ple, loss_part = _gate_tail(gn, full["w_gate"], h2, pe, target, gains["ple_norm_g"],
                                                           gains["final_norm_g"])
    gw_gate = _mm_tn("grad_w_gate", gn, dz, tn=1024)
    gw_ple = _mm_tn("grad_ple_w", p_b, dpe)
    dh2, dh2_b, dg_gate = _mm_nt_rms_bwd("d_gate_in", dz, full["w_gate"], h2, gains["gate_norm_g"], dh3, tm=512)
    gw_down = _mm_tn("grad_w_down", f_act, dh2_b, tn=1024)
    t = comm.reduce_begin("b", dict(w_gate=gw_gate, ple_w=gw_ple, w_down=gw_down))

    def act_bwd(acc, extra, outs):
        outs[0][...] = (acc * (2.0 * jnp.maximum(extra[0][...].astype(F32), 0.0))).astype(BF16)

    da = _mm_nt("d_act", dh2_b, full["w_down"], out_dtype=BF16, epilogue=act_bwd, extras=(a_act,), tn=2048, deps=(t,))
    gw_up = _mm_tn("grad_w_up", m_in, da, tn=1024)
    dm = _mm_nt("d_mlp_in", da, full["w_up"], out_dtype=BF16, tn=512)
    dh1, dh1_b, dg_mlp = _rms_bwd("norm_mlp_bwd", dm, h1, gains["mlp_norm_g"], dh2)
    gw_out = _mm_tn("grad_w_out", o_cat, dh1_b, tn=1024)
    t = comm.reduce_begin("d", dict(w_up=gw_up, w_out=gw_out))
    d_o = _mm_nt("d_attn_out", dh1_b, full["w_out"], out_dtype=BF16, deps=(t,))
    dqa, dka_t, dva_t = _attn_a_bwd(pb, o_cat, d_o, lse_a, n_q, n_kv)
    dqb, dkb, dvb, dbias, dsink_raw = _attn_b_bwd(pb, o_cat, d_o, lse_b, bias, sink, b_off, n_q, n_kv, n_q)
    dtable, dsink = _table_grads(dbias, dsink_raw, idx)
    dproj, dg_q, dg_k = _dproj(proj_a, dqa, dka_t, dva_t, dqb, dkb, dvb, cos, sin, gains["q_norm_g"], gains["k_norm_g"])
    gw_in = _mm_tn("grad_w_in", u, dproj, tn=1024)
    t = comm.reduce_begin("e", dict(w_in=gw_in))
    dx, dg_attn = _mm_nt_rms_bwd("d_attn_in", dproj, full["w_in"], x, gains["attn_norm_g"], dh1, with_bf16=False,
                                 tm=512, deps=(t,))
    for key in "bd":
        comm.reduce_finish(key, dx)

    parts = jnp.concatenate([dg_attn, dg_mlp, dg_ple, dg_gate, dg_final, dg_q, dg_k, dtable, dsink, loss_part], axis=1)
    return dx, parts


_SHARDED = ("w_in", "w_out", "w_up", "w_down", "ple_w", "w_gate")
_VECTORS = ("attn_norm_g", "mlp_norm_g", "ple_norm_g", "gate_norm_g", "final_norm_g")
_ORDER = ("attn_norm_g", "w_in", "q_norm_g", "k_norm_g", "sink_logits", "w_out", "mlp_norm_g", "w_up", "w_down",
          "ple_w", "ple_norm_g", "gate_norm_g", "w_gate", "rel_bias_table", "final_norm_g")


def _pack_small(vals, n_heads):
    lane_pad = lambda v: jnp.pad(v, ((0, 0), (0, LANES - v.shape[1])))
    table = lane_pad(vals["rel_bias_table"].T).reshape(1, n_heads * LANES)
    return jnp.concatenate(
        [vals[n].reshape(1, -1) for n in _VECTORS] + [vals["q_norm_g"], vals["k_norm_g"], table,
                                                      lane_pad(vals["sink_logits"]), jnp.zeros((1, LANES), F32)], axis=1)


def _unpack_small(row, like, n_heads):
    out, off = {}, 0
    for n in _VECTORS:
        out[n] = row[:, off:off + like[n].size].reshape(like[n].shape)
        off += like[n].size
    for n in ("q_norm_g", "k_norm_g"):
        out[n] = row[:, off:off + LANES]
        off += LANES
    out["rel_bias_table"] = row[:, off:off + n_heads * LANES].reshape(n_heads, LANES)[:, :N_BUCKETS].T
    off += n_heads * LANES
    out["sink_logits"] = row[:, off:off + n_heads]
    off += LANES
    return out, row[0, off]


def kernel(x, p, attn_norm_g, w_in, q_norm_g, k_norm_g, sink_logits, w_out, mlp_norm_g, w_up, w_down, ple_w, ple_norm_g, gate_norm_g, w_gate, rel_bias_table, final_norm_g, loss_target, m_attn_norm_g, m_w_in, m_q_norm_g, m_k_norm_g, m_sink_logits, m_w_out, m_mlp_norm_g, m_w_up, m_w_down, m_ple_w, m_ple_norm_g, m_gate_norm_g, m_w_gate, m_rel_bias_table, m_final_norm_g, v_attn_norm_g, v_w_in, v_q_norm_g, v_k_norm_g, v_sink_logits, v_w_out, v_mlp_norm_g, v_w_up, v_w_down, v_ple_w, v_ple_norm_g, v_gate_norm_g, v_w_gate, v_rel_bias_table, v_final_norm_g):
    w = dict(attn_norm_g=attn_norm_g, w_in=w_in[0], q_norm_g=q_norm_g, k_norm_g=k_norm_g, sink_logits=sink_logits,
             w_out=w_out[0], mlp_norm_g=mlp_norm_g, w_up=w_up[0], w_down=w_down[0], ple_w=ple_w[0],
             ple_norm_g=ple_norm_g, gate_norm_g=gate_norm_g, w_gate=w_gate[0], rel_bias_table=rel_bias_table,
             final_norm_g=final_norm_g)
    mom = dict(attn_norm_g=m_attn_norm_g, w_in=m_w_in[0], q_norm_g=m_q_norm_g, k_norm_g=m_k_norm_g,
               sink_logits=m_sink_logits, w_out=m_w_out[0], mlp_norm_g=m_mlp_norm_g, w_up=m_w_up[0],
               w_down=m_w_down[0], ple_w=m_ple_w[0], ple_norm_g=m_ple_norm_g, gate_norm_g=m_gate_norm_g,
               w_gate=m_w_gate[0], rel_bias_table=m_rel_bias_table, final_norm_g=m_final_norm_g)
    var = dict(attn_norm_g=v_attn_norm_g, w_in=v_w_in[0], q_norm_g=v_q_norm_g, k_norm_g=v_k_norm_g,
               sink_logits=v_sink_logits, w_out=v_w_out[0], mlp_norm_g=v_mlp_norm_g, w_up=v_w_up[0],
               w_down=v_w_down[0], ple_w=v_ple_w[0], ple_norm_g=v_ple_norm_g, gate_norm_g=v_gate_norm_g,
               w_gate=v_w_gate[0], rel_bias_table=v_rel_bias_table, final_norm_g=v_final_norm_g)
    D = x.shape[-1]
    n_heads = D // (2 * HEAD_DIM)

    gains = {n: w[n] for n in w if n not in _SHARDED}
    gains["final_norm_g"] = final_norm_g.reshape(1, -1)

    comm = _MeshComm(w, mom, var)
    dx, parts = _step(x[0], p[0, 0], loss_target[0], gains, comm)

    small_g = _small_all_reduce(parts, deps=[comm.out[n][0] for n in comm.out])
    comm.reduce_finish("e", small_g)

    g_out, d_out, m_out, v_out = {}, {}, {}, {}
    for n in _SHARDED:
        g, d, nm, nv = comm.out[n]
        g_out[n], d_out[n], m_out[n], v_out[n] = g[None], d[None], nm[None], nv[None]

    small = {n: v for n, v in w.items() if n not in _SHARDED}
    pack = lambda vals: _pack_small({n: vals[n] for n in small}, n_heads)
    sd, sm, sv = _adam_small(pack(w), small_g, pack(mom), pack(var))
    sg, loss = _unpack_small(small_g, small, n_heads)
    g_out.update(sg)
    for dst, row in ((d_out, sd), (m_out, sm), (v_out, sv)):
        dst.update(_unpack_small(row, small, n_heads)[0])

    return (loss, dx[None], *[g_out[n] for n in _ORDER], *[d_out[n] for n in _ORDER],
            *[m_out[n] for n in _ORDER], *[v_out[n] for n in _ORDER])
```
